```python
import math
import jax, jax.numpy as jnp
from jax import lax
import numpy as np

D_MODEL = 1024
BATCH = 8
SEQ = 2048
DEPTH = 4

N_Q_HEADS = 8
N_KV_HEADS = 2
HEAD_DIM = 64
Q_GROUP = N_Q_HEADS // N_KV_HEADS
WINDOW = 128
BLOCK = 128
ROPE_THETA = 500000.0
ROT_DIM = HEAD_DIM // 4
ATTN_WIDTH = N_Q_HEADS * HEAD_DIM
KV_WIDTH = N_KV_HEADS * HEAD_DIM
NEG_INF = -1e30
CONV_WIDTH = D_MODEL // 2
CONV_K = 3
SSM_WIDTH = D_MODEL // 2
SSM_GROUP = 16
SSM_GROUPS = SSM_WIDTH // SSM_GROUP
SSM_STATE = 64
DT_MIN = 1e-3
DT_MAX = 1e-1
N_BRANCH = 3
GATE_WIDTH = N_BRANCH * D_MODEL
FFN_HIDDEN = -(-8 * D_MODEL // (3 * 256)) * 256
NORM_EPS = 1e-6

IN_SIZES = (ATTN_WIDTH, KV_WIDTH, KV_WIDTH, CONV_WIDTH, CONV_WIDTH, CONV_WIDTH, SSM_WIDTH, GATE_WIDTH)
IN_COLS = sum(IN_SIZES)
IN_SPLITS = tuple(int(v) for v in np.cumsum(IN_SIZES)[:-1])

kernel_name = "hybrid_gated_swa_conv_s5_block"


def rmsnorm(x, g):
    xf = x.astype(jnp.float32)
    y = xf * lax.rsqrt(jnp.mean(xf * xf, axis=-1, keepdims=True) + NORM_EPS)
    return (y * g.astype(jnp.float32)).astype(x.dtype)


def rope_tables(seq_len):
    pos = jnp.arange(seq_len, dtype=jnp.float32)
    inv_freq = ROPE_THETA ** (-jnp.arange(0, ROT_DIM, 2, dtype=jnp.float32) / ROT_DIM)
    ang = pos[:, None] * inv_freq[None, :]
    return jnp.cos(ang), jnp.sin(ang)


def partial_rope(t, cos, sin):
    half = ROT_DIM // 2
    tf = t.astype(jnp.float32)
    t1, t2, rest = tf[..., :half], tf[..., half:ROT_DIM], tf[..., ROT_DIM:]
    c = cos[None, :, None, :]
    s = sin[None, :, None, :]
    out = jnp.concatenate([t1 * c - t2 * s, t2 * c + t1 * s, rest], axis=-1)
    return out.astype(t.dtype)


def sliding_window_attention(q, k, v, sinks):
    b, l = q.shape[0], q.shape[1]
    nb = l // BLOCK
    qb = q.reshape(b, nb, BLOCK, N_KV_HEADS, Q_GROUP, HEAD_DIM).astype(jnp.float32)

    def band(t):
        tp = jnp.pad(t, ((0, 0), (BLOCK, 0), (0, 0), (0, 0)))
        tp = tp.reshape(b, nb + 1, BLOCK, N_KV_HEADS, HEAD_DIM)
        return jnp.concatenate([tp[:, :-1], tp[:, 1:]], axis=2).astype(jnp.float32)

    kb, vb = band(k), band(v)
    s = jnp.einsum("bnqkgd,bnskd->bnkgqs", qb, kb) * (HEAD_DIM ** -0.5)
    qi = jnp.arange(BLOCK)[:, None]
    kj = jnp.arange(2 * BLOCK)[None, :]
    delta = qi + BLOCK - kj
    band_ok = (delta >= 0) & (delta < WINDOW)
    kpos = jnp.arange(nb)[:, None] * BLOCK - BLOCK + kj
    mask = band_ok[None, :, :] & (kpos >= 0)[:, None, :]
    s = jnp.where(mask[None, :, None, None, :, :], s, NEG_INF)
    sink = sinks.astype(jnp.float32).reshape(N_KV_HEADS, Q_GROUP)[None, None, :, :, None, None]
    m = jnp.maximum(jnp.max(s, axis=-1, keepdims=True), sink)
    p = jnp.exp(s - m)
    denom = jnp.sum(p, axis=-1, keepdims=True) + jnp.exp(sink - m)
    o = jnp.einsum("bnkgqs,bnskd->bnqkgd", p / denom, vb)
    return o.reshape(b, l, ATTN_WIDTH).astype(q.dtype)


def short_conv(z, w):
    l = z.shape[1]
    zp = jnp.pad(z, ((0, 0), (CONV_K - 1, 0), (0, 0)))
    y = w[0] * zp[:, 0:l]
    for j in range(1, CONV_K):
        y = y + w[j] * zp[:, j:j + l]
    return y


def s5_ssm(u, a_re, a_im, b_re, b_im, c_re, c_im, d, log_dt):
    bsz, l = u.shape[0], u.shape[1]
    uf = u.astype(jnp.float32).reshape(bsz, l, SSM_GROUPS, SSM_GROUP)
    lam = lax.complex(a_re.astype(jnp.float32), a_im.astype(jnp.float32))
    dt = jnp.exp(log_dt.astype(jnp.float32))[:, None]
    lam_bar = jnp.exp(lam * dt)
    b_c = lax.complex(b_re.astype(jnp.float32), b_im.astype(jnp.float32))
    b_bar = ((lam_bar - 1.0) / lam)[..., None] * b_c
    bu = jnp.einsum("blgh,gph->blgp", uf.astype(jnp.complex64), b_bar)
    a_elems = jnp.broadcast_to(lam_bar, bu.shape)

    def combine(e1, e2):
        a1, x1 = e1
        a2, x2 = e2
        return a1 * a2, a2 * x1 + x2

    _, states = lax.associative_scan(combine, (a_elems, bu), axis=1)
    c_c = lax.complex(c_re.astype(jnp.float32), c_im.astype(jnp.float32))
    y = jnp.einsum("blgp,ghp->blgh", states, c_c).real
    y = y + d.astype(jnp.float32).reshape(SSM_GROUPS, SSM_GROUP) * uf
    return y.reshape(bsz, l, SSM_WIDTH).astype(u.dtype)


def _fwd_setup_inputs(seed: int = 0) -> dict:
    key = jax.random.key(seed)
    ks = jax.random.split(key, 24)
    L = DEPTH

    def nrm(k, shape, fan_in):
        return jax.random.normal(k, shape, jnp.float32) * (fan_in ** -0.5)

    x = jax.random.normal(ks[0], (BATCH, SEQ, D_MODEL), jnp.float32)
    norm_mix = 1.0 + 0.02 * jax.random.normal(ks[1], (L, D_MODEL), jnp.float32)
    w_in = nrm(ks[2], (L, D_MODEL, IN_COLS), D_MODEL)
    b_gate = 0.02 * jax.random.normal(ks[3], (L, GATE_WIDTH), jnp.float32)
    attn_sinks = 0.5 * jax.random.normal(ks[4], (L, N_Q_HEADS), jnp.float32)
    w_attn_o = nrm(ks[5], (L, ATTN_WIDTH, D_MODEL), ATTN_WIDTH)
    conv_w = nrm(ks[6], (L, CONV_K, CONV_WIDTH), CONV_K)
    w_conv_o = nrm(ks[7], (L, CONV_WIDTH, D_MODEL), CONV_WIDTH)
    ssm_a_re = -0.5 + 0.01 * jax.random.normal(ks[8], (L, SSM_GROUPS, SSM_STATE), jnp.float32)
    ssm_a_im = (math.pi * jnp.arange(SSM_STATE, dtype=jnp.float32))[None, None, :] \
        + 0.01 * jax.random.normal(ks[9], (L, SSM_GROUPS, SSM_STATE), jnp.float32)
    ssm_b_re = nrm(ks[10], (L, SSM_GROUPS, SSM_STATE, SSM_GROUP), 2 * SSM_GROUP)
    ssm_b_im = nrm(ks[11], (L, SSM_GROUPS, SSM_STATE, SSM_GROUP), 2 * SSM_GROUP)
    ssm_c_re = nrm(ks[12], (L, SSM_GROUPS, SSM_GROUP, SSM_STATE), 2 * SSM_STATE)
    ssm_c_im = nrm(ks[13], (L, SSM_GROUPS, SSM_GROUP, SSM_STATE), 2 * SSM_STATE)
    ssm_d = jax.random.normal(ks[14], (L, SSM_WIDTH), jnp.float32)
    ssm_log_dt = jax.random.uniform(ks[15], (L, SSM_GROUPS), jnp.float32,
                                    minval=math.log(DT_MIN), maxval=math.log(DT_MAX))
    w_ssm_glu = nrm(ks[16], (L, SSM_WIDTH, SSM_WIDTH), SSM_WIDTH)
    w_ssm_o = nrm(ks[17], (L, SSM_WIDTH, D_MODEL), SSM_WIDTH)
    w_mix_o = nrm(ks[18], (L, D_MODEL, D_MODEL), D_MODEL)
    norm_ffn = 1.0 + 0.02 * jax.random.normal(ks[19], (L, D_MODEL), jnp.float32)
    w_ffn_in = nrm(ks[20], (L, D_MODEL, 2 * FFN_HIDDEN), D_MODEL)
    w_ffn_out = nrm(ks[21], (L, FFN_HIDDEN, D_MODEL), FFN_HIDDEN)
    norm_final = 1.0 + 0.02 * jax.random.normal(ks[22], (D_MODEL,), jnp.float32)
    return {"x": x, "norm_mix": norm_mix, "w_in": w_in, "b_gate": b_gate,
            "attn_sinks": attn_sinks, "w_attn_o": w_attn_o, "conv_w": conv_w, "w_conv_o": w_conv_o,
            "ssm_a_re": ssm_a_re, "ssm_a_im": ssm_a_im, "ssm_b_re": ssm_b_re, "ssm_b_im": ssm_b_im,
            "ssm_c_re": ssm_c_re, "ssm_c_im": ssm_c_im, "ssm_d": ssm_d, "ssm_log_dt": ssm_log_dt,
            "w_ssm_glu": w_ssm_glu, "w_ssm_o": w_ssm_o, "w_mix_o": w_mix_o, "norm_ffn": norm_ffn,
            "w_ffn_in": w_ffn_in, "w_ffn_out": w_ffn_out, "norm_final": norm_final}


def _fwd_reference(x, norm_mix, w_in, b_gate, attn_sinks, w_attn_o, conv_w, w_conv_o,
              ssm_a_re, ssm_a_im, ssm_b_re, ssm_b_im, ssm_c_re, ssm_c_im, ssm_d, ssm_log_dt,
              w_ssm_glu, w_ssm_o, w_mix_o, norm_ffn, w_ffn_in, w_ffn_out, norm_final):
    b, l = x.shape[0], x.shape[1]
    cos, sin = rope_tables(l)
    for i in range(DEPTH):
        h = rmsnorm(x, norm_mix[i])
        proj = h @ w_in[i]
        q, k, v, cb, cc, cx, u, g = jnp.split(proj, IN_SPLITS, axis=-1)
        q = partial_rope(q.reshape(b, l, N_Q_HEADS, HEAD_DIM), cos, sin)
        k = partial_rope(k.reshape(b, l, N_KV_HEADS, HEAD_DIM), cos, sin)
        v = v.reshape(b, l, N_KV_HEADS, HEAD_DIM)
        y_attn = sliding_window_attention(q, k, v, attn_sinks[i]) @ w_attn_o[i]
        y_conv = (cb * short_conv(cc * cx, conv_w[i])) @ w_conv_o[i]
        ys = jax.nn.gelu(s5_ssm(u, ssm_a_re[i], ssm_a_im[i], ssm_b_re[i], ssm_b_im[i],
                                ssm_c_re[i], ssm_c_im[i], ssm_d[i], ssm_log_dt[i]))
        y_ssm = (ys * jax.nn.sigmoid(ys @ w_ssm_glu[i])) @ w_ssm_o[i]
        gates = jax.nn.sigmoid(g + b_gate[i]).reshape(b, l, N_BRANCH, D_MODEL)
        merged = gates[:, :, 0] * y_attn + gates[:, :, 1] * y_conv + gates[:, :, 2] * y_ssm
        x = x + merged @ w_mix_o[i]
        h = rmsnorm(x, norm_ffn[i])
        gt, up = jnp.split(h @ w_ffn_in[i], 2, axis=-1)
        x = x + (jax.nn.silu(gt) * up) @ w_ffn_out[i]
    return rmsnorm(x, norm_final)


import jax as _jax
import jax.numpy as _jnp

TWIN_FORMAT = 'train_step'
FWD_PARAMS = ['x', 'norm_mix', 'w_in', 'b_gate', 'attn_sinks', 'w_attn_o', 'conv_w', 'w_conv_o', 'ssm_a_re', 'ssm_a_im', 'ssm_b_re', 'ssm_b_im', 'ssm_c_re', 'ssm_c_im', 'ssm_d', 'ssm_log_dt', 'w_ssm_glu', 'w_ssm_o', 'w_mix_o', 'norm_ffn', 'w_ffn_in', 'w_ffn_out', 'norm_final']
TWIN_WEIGHTS = ['norm_mix', 'w_in', 'b_gate', 'attn_sinks', 'w_attn_o', 'conv_w', 'w_conv_o', 'ssm_a_re', 'ssm_a_im', 'ssm_b_re', 'ssm_b_im', 'ssm_c_re', 'ssm_c_im', 'ssm_d', 'ssm_log_dt', 'w_ssm_glu', 'w_ssm_o', 'w_mix_o', 'norm_ffn', 'w_ffn_in', 'w_ffn_out', 'norm_final']
TWIN_DIFF_INPUT = 'x'
TWIN_INPUTS = ['x', 'norm_mix', 'w_in', 'b_gate', 'attn_sinks', 'w_attn_o', 'conv_w', 'w_conv_o', 'ssm_a_re', 'ssm_a_im', 'ssm_b_re', 'ssm_b_im', 'ssm_c_re', 'ssm_c_im', 'ssm_d', 'ssm_log_dt', 'w_ssm_glu', 'w_ssm_o', 'w_mix_o', 'norm_ffn', 'w_ffn_in', 'w_ffn_out', 'norm_final', 'loss_target', 'm_norm_mix', 'm_w_in', 'm_b_gate', 'm_attn_sinks', 'm_w_attn_o', 'm_conv_w', 'm_w_conv_o', 'm_ssm_a_re', 'm_ssm_a_im', 'm_ssm_b_re', 'm_ssm_b_im', 'm_ssm_c_re', 'm_ssm_c_im', 'm_ssm_d', 'm_ssm_log_dt', 'm_w_ssm_glu', 'm_w_ssm_o', 'm_w_mix_o', 'm_norm_ffn', 'm_w_ffn_in', 'm_w_ffn_out', 'm_norm_final', 'v_norm_mix', 'v_w_in', 'v_b_gate', 'v_attn_sinks', 'v_w_attn_o', 'v_conv_w', 'v_w_conv_o', 'v_ssm_a_re', 'v_ssm_a_im', 'v_ssm_b_re', 'v_ssm_b_im', 'v_ssm_c_re', 'v_ssm_c_im', 'v_ssm_d', 'v_ssm_log_dt', 'v_w_ssm_glu', 'v_w_ssm_o', 'v_w_mix_o', 'v_norm_ffn', 'v_w_ffn_in', 'v_w_ffn_out', 'v_norm_final']
TWIN_OUTPUTS = ['loss', 'grad_x', 'grad_norm_mix', 'grad_w_in', 'grad_b_gate', 'grad_attn_sinks', 'grad_w_attn_o', 'grad_conv_w', 'grad_w_conv_o', 'grad_ssm_a_re', 'grad_ssm_a_im', 'grad_ssm_b_re', 'grad_ssm_b_im', 'grad_ssm_c_re', 'grad_ssm_c_im', 'grad_ssm_d', 'grad_ssm_log_dt', 'grad_w_ssm_glu', 'grad_w_ssm_o', 'grad_w_mix_o', 'grad_norm_ffn', 'grad_w_ffn_in', 'grad_w_ffn_out', 'grad_norm_final', 'delta_norm_mix', 'delta_w_in', 'delta_b_gate', 'delta_attn_sinks', 'delta_w_attn_o', 'delta_conv_w', 'delta_w_conv_o', 'delta_ssm_a_re', 'delta_ssm_a_im', 'delta_ssm_b_re', 'delta_ssm_b_im', 'delta_ssm_c_re', 'delta_ssm_c_im', 'delta_ssm_d', 'delta_ssm_log_dt', 'delta_w_ssm_glu', 'delta_w_ssm_o', 'delta_w_mix_o', 'delta_norm_ffn', 'delta_w_ffn_in', 'delta_w_ffn_out', 'delta_norm_final', 'new_m_norm_mix', 'new_m_w_in', 'new_m_b_gate', 'new_m_attn_sinks', 'new_m_w_attn_o', 'new_m_conv_w', 'new_m_w_conv_o', 'new_m_ssm_a_re', 'new_m_ssm_a_im', 'new_m_ssm_b_re', 'new_m_ssm_b_im', 'new_m_ssm_c_re', 'new_m_ssm_c_im', 'new_m_ssm_d', 'new_m_ssm_log_dt', 'new_m_w_ssm_glu', 'new_m_w_ssm_o', 'new_m_w_mix_o', 'new_m_norm_ffn', 'new_m_w_ffn_in', 'new_m_w_ffn_out', 'new_m_norm_final', 'new_v_norm_mix', 'new_v_w_in', 'new_v_b_gate', 'new_v_attn_sinks', 'new_v_w_attn_o', 'new_v_conv_w', 'new_v_w_conv_o', 'new_v_ssm_a_re', 'new_v_ssm_a_im', 'new_v_ssm_b_re', 'new_v_ssm_b_im', 'new_v_ssm_c_re', 'new_v_ssm_c_im', 'new_v_ssm_d', 'new_v_ssm_log_dt', 'new_v_w_ssm_glu', 'new_v_w_ssm_o', 'new_v_w_mix_o', 'new_v_norm_ffn', 'new_v_w_ffn_in', 'new_v_w_ffn_out', 'new_v_norm_final']
TWIN_LEAF_KINDS = {'loss': 'loss', 'grad_x': 'grad_x', 'grad_norm_mix': 'grad_w', 'grad_w_in': 'grad_w', 'grad_b_gate': 'grad_w', 'grad_attn_sinks': 'grad_w', 'grad_w_attn_o': 'grad_w', 'grad_conv_w': 'grad_w', 'grad_w_conv_o': 'grad_w', 'grad_ssm_a_re': 'grad_w', 'grad_ssm_a_im': 'grad_w', 'grad_ssm_b_re': 'grad_w', 'grad_ssm_b_im': 'grad_w', 'grad_ssm_c_re': 'grad_w', 'grad_ssm_c_im': 'grad_w', 'grad_ssm_d': 'grad_w', 'grad_ssm_log_dt': 'grad_w', 'grad_w_ssm_glu': 'grad_w', 'grad_w_ssm_o': 'grad_w', 'grad_w_mix_o': 'grad_w', 'grad_norm_ffn': 'grad_w', 'grad_w_ffn_in': 'grad_w', 'grad_w_ffn_out': 'grad_w', 'grad_norm_final': 'grad_w', 'delta_norm_mix': 'delta_w', 'delta_w_in': 'delta_w', 'delta_b_gate': 'delta_w', 'delta_attn_sinks': 'delta_w', 'delta_w_attn_o': 'delta_w', 'delta_conv_w': 'delta_w', 'delta_w_conv_o': 'delta_w', 'delta_ssm_a_re': 'delta_w', 'delta_ssm_a_im': 'delta_w', 'delta_ssm_b_re': 'delta_w', 'delta_ssm_b_im': 'delta_w', 'delta_ssm_c_re': 'delta_w', 'delta_ssm_c_im': 'delta_w', 'delta_ssm_d': 'delta_w', 'delta_ssm_log_dt': 'delta_w', 'delta_w_ssm_glu': 'delta_w', 'delta_w_ssm_o': 'delta_w', 'delta_w_mix_o': 'delta_w', 'delta_norm_ffn': 'delta_w', 'delta_w_ffn_in': 'delta_w', 'delta_w_ffn_out': 'delta_w', 'delta_norm_final': 'delta_w', 'new_m_norm_mix': 'new_m', 'new_m_w_in': 'new_m', 'new_m_b_gate': 'new_m', 'new_m_attn_sinks': 'new_m', 'new_m_w_attn_o': 'new_m', 'new_m_conv_w': 'new_m', 'new_m_w_conv_o': 'new_m', 'new_m_ssm_a_re': 'new_m', 'new_m_ssm_a_im': 'new_m', 'new_m_ssm_b_re': 'new_m', 'new_m_ssm_b_im': 'new_m', 'new_m_ssm_c_re': 'new_m', 'new_m_ssm_c_im': 'new_m', 'new_m_ssm_d': 'new_m', 'new_m_ssm_log_dt': 'new_m', 'new_m_w_ssm_glu': 'new_m', 'new_m_w_ssm_o': 'new_m', 'new_m_w_mix_o': 'new_m', 'new_m_norm_ffn': 'new_m', 'new_m_w_ffn_in': 'new_m', 'new_m_w_ffn_out': 'new_m', 'new_m_norm_final': 'new_m', 'new_v_norm_mix': 'new_v', 'new_v_w_in': 'new_v', 'new_v_b_gate': 'new_v', 'new_v_attn_sinks': 'new_v', 'new_v_w_attn_o': 'new_v', 'new_v_conv_w': 'new_v', 'new_v_w_conv_o': 'new_v', 'new_v_ssm_a_re': 'new_v', 'new_v_ssm_a_im': 'new_v', 'new_v_ssm_b_re': 'new_v', 'new_v_ssm_b_im': 'new_v', 'new_v_ssm_c_re': 'new_v', 'new_v_ssm_c_im': 'new_v', 'new_v_ssm_d': 'new_v', 'new_v_ssm_log_dt': 'new_v', 'new_v_w_ssm_glu': 'new_v', 'new_v_w_ssm_o': 'new_v', 'new_v_w_mix_o': 'new_v', 'new_v_norm_ffn': 'new_v', 'new_v_w_ffn_in': 'new_v', 'new_v_w_ffn_out': 'new_v', 'new_v_norm_final': 'new_v'}


def _forward(args):
    return _fwd_reference(*[args[k] for k in FWD_PARAMS])


def _output_shape():
    out = _jax.eval_shape(lambda: _forward(_fwd_setup_inputs(0)))
    return out.shape, out.dtype

N_MICROBATCH = 1
ADAM_LR = 0.001
ADAM_B1 = 0.9
ADAM_B2 = 0.999
ADAM_EPS = 1e-08
ADAM_WD = 0.01
ADAM_STEP = 10
PER_EXAMPLE_BATCH_AXIS = {'x': 0, 'loss_target': 0}
SHARED_INPUTS = []
_WEIGHT_DTYPES = {'norm_mix': _jnp.float32, 'w_in': _jnp.float32, 'b_gate': _jnp.float32, 'attn_sinks': _jnp.float32, 'w_attn_o': _jnp.float32, 'conv_w': _jnp.float32, 'w_conv_o': _jnp.float32, 'ssm_a_re': _jnp.float32, 'ssm_a_im': _jnp.float32, 'ssm_b_re': _jnp.float32, 'ssm_b_im': _jnp.float32, 'ssm_c_re': _jnp.float32, 'ssm_c_im': _jnp.float32, 'ssm_d': _jnp.float32, 'ssm_log_dt': _jnp.float32, 'w_ssm_glu': _jnp.float32, 'w_ssm_o': _jnp.float32, 'w_mix_o': _jnp.float32, 'norm_ffn': _jnp.float32, 'w_ffn_in': _jnp.float32, 'w_ffn_out': _jnp.float32, 'norm_final': _jnp.float32}
MOMENT_SCALE = {'norm_mix': 1.382723e-01, 'w_in': 5.577338e-02, 'b_gate': 1.750552e-02, 'attn_sinks': 1.853290e-02, 'w_attn_o': 1.587494e-02, 'conv_w': 1.019890e-01, 'w_conv_o': 7.221719e-02, 'ssm_a_re': 1.982498e-03, 'ssm_a_im': 1.952140e-03, 'ssm_b_re': 1.238571e-03, 'ssm_b_im': 1.265259e-03, 'ssm_c_re': 2.504470e-03, 'ssm_c_im': 2.488421e-03, 'ssm_d': 3.858084e-02, 'ssm_log_dt': 1.812337e+00, 'w_ssm_glu': 1.030928e-02, 'w_ssm_o': 2.509513e-02, 'w_mix_o': 7.777395e-02, 'norm_ffn': 9.815728e-02, 'w_ffn_in': 4.059529e-02, 'w_ffn_out': 6.622689e-02, 'norm_final': 1.600407e+01}


def _to_microbatches(a, axis):
    t = _jnp.moveaxis(a, axis, 0)
    t = t.reshape((N_MICROBATCH, t.shape[0] // N_MICROBATCH) + t.shape[1:])
    return _jnp.moveaxis(t, 1, axis + 1)


def setup_inputs(seed: int = 0) -> dict:
    inp = _fwd_setup_inputs(seed)
    key = _jax.random.fold_in(_jax.random.key(seed), 7919)
    shape, _ = _output_shape()
    out = dict(inp)
    out["loss_target"] = _jax.random.normal(_jax.random.fold_in(key, 0), shape, _jnp.float32)
    for i, name in enumerate(TWIN_WEIGHTS):
        w = inp[name].astype(_jnp.float32)
        if MOMENT_SCALE is None:
            s = _jnp.sqrt(_jnp.mean(_jnp.square(w)) + 1e-30)
        else:
            s = MOMENT_SCALE[name]
        km, kv = _jax.random.split(_jax.random.fold_in(key, i + 1))
        out[name] = w
        out["m_" + name] = s * _jax.random.normal(km, w.shape, _jnp.float32)
        out["v_" + name] = (s * s) * _jax.random.uniform(kv, w.shape, _jnp.float32, 0.5, 1.5)
    if N_MICROBATCH > 1:
        for name, axis in PER_EXAMPLE_BATCH_AXIS.items():
            out[name] = _to_microbatches(out[name], axis)
    return {'x': out['x'], 'norm_mix': out['norm_mix'], 'w_in': out['w_in'], 'b_gate': out['b_gate'], 'attn_sinks': out['attn_sinks'], 'w_attn_o': out['w_attn_o'], 'conv_w': out['conv_w'], 'w_conv_o': out['w_conv_o'], 'ssm_a_re': out['ssm_a_re'], 'ssm_a_im': out['ssm_a_im'], 'ssm_b_re': out['ssm_b_re'], 'ssm_b_im': out['ssm_b_im'], 'ssm_c_re': out['ssm_c_re'], 'ssm_c_im': out['ssm_c_im'], 'ssm_d': out['ssm_d'], 'ssm_log_dt': out['ssm_log_dt'], 'w_ssm_glu': out['w_ssm_glu'], 'w_ssm_o': out['w_ssm_o'], 'w_mix_o': out['w_mix_o'], 'norm_ffn': out['norm_ffn'], 'w_ffn_in': out['w_ffn_in'], 'w_ffn_out': out['w_ffn_out'], 'norm_final': out['norm_final'], 'loss_target': out['loss_target'], 'm_norm_mix': out['m_norm_mix'], 'm_w_in': out['m_w_in'], 'm_b_gate': out['m_b_gate'], 'm_attn_sinks': out['m_attn_sinks'], 'm_w_attn_o': out['m_w_attn_o'], 'm_conv_w': out['m_conv_w'], 'm_w_conv_o': out['m_w_conv_o'], 'm_ssm_a_re': out['m_ssm_a_re'], 'm_ssm_a_im': out['m_ssm_a_im'], 'm_ssm_b_re': out['m_ssm_b_re'], 'm_ssm_b_im': out['m_ssm_b_im'], 'm_ssm_c_re': out['m_ssm_c_re'], 'm_ssm_c_im': out['m_ssm_c_im'], 'm_ssm_d': out['m_ssm_d'], 'm_ssm_log_dt': out['m_ssm_log_dt'], 'm_w_ssm_glu': out['m_w_ssm_glu'], 'm_w_ssm_o': out['m_w_ssm_o'], 'm_w_mix_o': out['m_w_mix_o'], 'm_norm_ffn': out['m_norm_ffn'], 'm_w_ffn_in': out['m_w_ffn_in'], 'm_w_ffn_out': out['m_w_ffn_out'], 'm_norm_final': out['m_norm_final'], 'v_norm_mix': out['v_norm_mix'], 'v_w_in': out['v_w_in'], 'v_b_gate': out['v_b_gate'], 'v_attn_sinks': out['v_attn_sinks'], 'v_w_attn_o': out['v_w_attn_o'], 'v_conv_w': out['v_conv_w'], 'v_w_conv_o': out['v_w_conv_o'], 'v_ssm_a_re': out['v_ssm_a_re'], 'v_ssm_a_im': out['v_ssm_a_im'], 'v_ssm_b_re': out['v_ssm_b_re'], 'v_ssm_b_im': out['v_ssm_b_im'], 'v_ssm_c_re': out['v_ssm_c_re'], 'v_ssm_c_im': out['v_ssm_c_im'], 'v_ssm_d': out['v_ssm_d'], 'v_ssm_log_dt': out['v_ssm_log_dt'], 'v_w_ssm_glu': out['v_w_ssm_glu'], 'v_w_ssm_o': out['v_w_ssm_o'], 'v_w_mix_o': out['v_w_mix_o'], 'v_norm_ffn': out['v_norm_ffn'], 'v_w_ffn_in': out['v_w_ffn_in'], 'v_w_ffn_out': out['v_w_ffn_out'], 'v_norm_final': out['v_norm_final']}


def _loss(weights, diff, rest, loss_target):
    with _jax.named_scope("forward"):
        args = {**rest, TWIN_DIFF_INPUT: diff, **{k: w.astype(_WEIGHT_DTYPES[k]) for k, w in weights.items()}}
        y = _forward(args)
    with _jax.named_scope("loss_head"):
        err = _jnp.square(y.astype(_jnp.float32) - loss_target)
        return 0.5 * _jnp.sum(_jnp.mean(err, axis=-1)) if err.ndim else 0.5 * err


def _adamw(w, g, m, v):
    m = ADAM_B1 * m + (1.0 - ADAM_B1) * g
    v = ADAM_B2 * v + (1.0 - ADAM_B2) * _jnp.square(g)
    m_hat = m / (1.0 - ADAM_B1 ** ADAM_STEP)
    v_hat = v / (1.0 - ADAM_B2 ** ADAM_STEP)
    delta = -ADAM_LR * (m_hat / (_jnp.sqrt(v_hat) + ADAM_EPS) + ADAM_WD * w)
    return delta, m, v


def reference(x, norm_mix, w_in, b_gate, attn_sinks, w_attn_o, conv_w, w_conv_o, ssm_a_re, ssm_a_im, ssm_b_re, ssm_b_im, ssm_c_re, ssm_c_im, ssm_d, ssm_log_dt, w_ssm_glu, w_ssm_o, w_mix_o, norm_ffn, w_ffn_in, w_ffn_out, norm_final, loss_target, m_norm_mix, m_w_in, m_b_gate, m_attn_sinks, m_w_attn_o, m_conv_w, m_w_conv_o, m_ssm_a_re, m_ssm_a_im, m_ssm_b_re, m_ssm_b_im, m_ssm_c_re, m_ssm_c_im, m_ssm_d, m_ssm_log_dt, m_w_ssm_glu, m_w_ssm_o, m_w_mix_o, m_norm_ffn, m_w_ffn_in, m_w_ffn_out, m_norm_final, v_norm_mix, v_w_in, v_b_gate, v_attn_sinks, v_w_attn_o, v_conv_w, v_w_conv_o, v_ssm_a_re, v_ssm_a_im, v_ssm_b_re, v_ssm_b_im, v_ssm_c_re, v_ssm_c_im, v_ssm_d, v_ssm_log_dt, v_w_ssm_glu, v_w_ssm_o, v_w_mix_o, v_norm_ffn, v_w_ffn_in, v_w_ffn_out, v_norm_final):
    given = dict(x=x, norm_mix=norm_mix, w_in=w_in, b_gate=b_gate, attn_sinks=attn_sinks, w_attn_o=w_attn_o, conv_w=conv_w, w_conv_o=w_conv_o, ssm_a_re=ssm_a_re, ssm_a_im=ssm_a_im, ssm_b_re=ssm_b_re, ssm_b_im=ssm_b_im, ssm_c_re=ssm_c_re, ssm_c_im=ssm_c_im, ssm_d=ssm_d, ssm_log_dt=ssm_log_dt, w_ssm_glu=w_ssm_glu, w_ssm_o=w_ssm_o, w_mix_o=w_mix_o, norm_ffn=norm_ffn, w_ffn_in=w_ffn_in, w_ffn_out=w_ffn_out, norm_final=norm_final, loss_target=loss_target, m_norm_mix=m_norm_mix, m_w_in=m_w_in, m_b_gate=m_b_gate, m_attn_sinks=m_attn_sinks, m_w_attn_o=m_w_attn_o, m_conv_w=m_conv_w, m_w_conv_o=m_w_conv_o, m_ssm_a_re=m_ssm_a_re, m_ssm_a_im=m_ssm_a_im, m_ssm_b_re=m_ssm_b_re, m_ssm_b_im=m_ssm_b_im, m_ssm_c_re=m_ssm_c_re, m_ssm_c_im=m_ssm_c_im, m_ssm_d=m_ssm_d, m_ssm_log_dt=m_ssm_log_dt, m_w_ssm_glu=m_w_ssm_glu, m_w_ssm_o=m_w_ssm_o, m_w_mix_o=m_w_mix_o, m_norm_ffn=m_norm_ffn, m_w_ffn_in=m_w_ffn_in, m_w_ffn_out=m_w_ffn_out, m_norm_final=m_norm_final, v_norm_mix=v_norm_mix, v_w_in=v_w_in, v_b_gate=v_b_gate, v_attn_sinks=v_attn_sinks, v_w_attn_o=v_w_attn_o, v_conv_w=v_conv_w, v_w_conv_o=v_w_conv_o, v_ssm_a_re=v_ssm_a_re, v_ssm_a_im=v_ssm_a_im, v_ssm_b_re=v_ssm_b_re, v_ssm_b_im=v_ssm_b_im, v_ssm_c_re=v_ssm_c_re, v_ssm_c_im=v_ssm_c_im, v_ssm_d=v_ssm_d, v_ssm_log_dt=v_ssm_log_dt, v_w_ssm_glu=v_w_ssm_glu, v_w_ssm_o=v_w_ssm_o, v_w_mix_o=v_w_mix_o, v_norm_ffn=v_norm_ffn, v_w_ffn_in=v_w_ffn_in, v_w_ffn_out=v_w_ffn_out, v_norm_final=v_norm_final)
    weights = {n: given[n] for n in TWIN_WEIGHTS}
    shared = {n: given[n] for n in SHARED_INPUTS}
    per_example = {n: given[n] for n in ['x']}
    grad_fn = _jax.value_and_grad(_loss, argnums=(0, 1))

    def one_microbatch(ex, loss_target):
        ex = dict(ex)
        diff = ex.pop(TWIN_DIFF_INPUT)
        return grad_fn(weights, diff, {**shared, **ex}, loss_target)

    if N_MICROBATCH == 1:
        loss, (grad_w, grad_x) = one_microbatch(per_example, given["loss_target"])
    else:
        def body(carry, xs):
            loss_sum, grad_sum = carry
            l_k, (gw_k, gx_k) = one_microbatch(xs[0], xs[1])
            with _jax.named_scope("update"):
                return (loss_sum + l_k, _jax.tree.map(_jnp.add, grad_sum, gw_k)), gx_k

        init = (_jnp.zeros((), _jnp.float32), _jax.tree.map(_jnp.zeros_like, weights))
        (loss, grad_w), grad_x = _jax.lax.scan(body, init, (per_example, given["loss_target"]))
    with _jax.named_scope("update"):
        delta_w, new_m, new_v = {}, {}, {}
        for n in TWIN_WEIGHTS:
            delta_w[n], new_m[n], new_v[n] = _adamw(weights[n], grad_w[n], given["m_" + n], given["v_" + n])
    return (loss, grad_x, *[grad_w[n] for n in TWIN_WEIGHTS], *[delta_w[n] for n in TWIN_WEIGHTS],
            *[new_m[n] for n in TWIN_WEIGHTS], *[new_v[n] for n in TWIN_WEIGHTS])
```

```python
import functools
import math

import jax
import jax.numpy as jnp
from jax import lax
from jax.experimental import pallas as pl
from jax.experimental.pallas import tpu as pltpu

F32 = jnp.float32
BF16 = jnp.bfloat16

N_DEV = 8
DEPTH = 4
SEQ = 2048
D_MODEL = 1024
N_Q_HEADS = 8
HEAD_DIM = 64
ATTN_W = 512
KV_W = 128
BLOCK = 128
N_BLOCKS = SEQ // BLOCK
ROPE_THETA = 500000.0
ROT_DIM = 16
NEG_INF = -1e30
WIDTH = 512
SSM_GROUPS = 32
SSM_GROUP = 16
SSM_STATE = 64
SLABS = 16
CHUNK = 256
N_CHUNKS = SEQ // CHUNK
GATE_W = 3 * D_MODEL
IN_COLS = 5888
FFN_H = 2816
NORM_EPS = 1e-6
LR, B1, B2, ADAM_EPS, WD, STEP = 0.001, 0.9, 0.999, 1e-08, 0.01, 10

COL_G, COL_Q, COL_CB, COL_CC, COL_CX, COL_U, COL_KV = 0, 3072, 3584, 4096, 4608, 5120, 5632

PACK = (("w_in", 736), ("w_attn_o", 64), ("w_conv_o", 64), ("w_ssm_glu", 32), ("w_ssm_o", 64),
        ("w_mix_o", 128), ("w_ffn_in", 704), ("w_ffn_out", 352), ("conv_w", 16))
PACK_ROWS = sum(r for _, r in PACK)
PACK_OFF = {}
_o = 0
for _n, _r in PACK:
    PACK_OFF[_n] = (_o, _r)
    _o += _r

SMALL = (("norm_mix", 1024), ("b_gate", 3072), ("attn_sinks", 8), ("ssm_a_re", 2048), ("ssm_a_im", 2048),
         ("ssm_b_re", 32768), ("ssm_b_im", 32768), ("ssm_c_re", 32768), ("ssm_c_im", 32768),
         ("ssm_d", 512), ("ssm_log_dt", 32), ("norm_ffn", 1024))
SMALL_ROWS = 4416

VMEM_LIMIT = 56 * 1024 * 1024


def _cp(**kw):
    return pltpu.CompilerParams(vmem_limit_bytes=VMEM_LIMIT, **kw)


def _full(shape):
    return pl.BlockSpec(shape, lambda *_: (0,) * len(shape))


def _mm(a, b, *, ta=False, tb=False, tm, tn, tk, out_dtype=F32, name):
    m = a.shape[1] if ta else a.shape[0]
    k = a.shape[0] if ta else a.shape[1]
    n = b.shape[0] if tb else b.shape[1]
    nk = k // tk
    dims = (((0 if ta else 1,), (1 if tb else 0,)), ((), ()))

    def body(a_ref, b_ref, o_ref, acc_ref):
        kk = pl.program_id(2)

        @pl.when(kk == 0)
        def _():
            acc_ref[...] = jnp.zeros_like(acc_ref)

        acc_ref[...] += lax.dot_general(a_ref[...].astype(BF16), b_ref[...].astype(BF16), dims,
                                        preferred_element_type=F32)

        @pl.when(kk == nk - 1)
        def _():
            o_ref[...] = acc_ref[...].astype(out_dtype)

    a_spec = pl.BlockSpec((tk, tm), lambda i, j, kk: (kk, i)) if ta else pl.BlockSpec((tm, tk), lambda i, j, kk: (i, kk))
    b_spec = pl.BlockSpec((tn, tk), lambda i, j, kk: (j, kk)) if tb else pl.BlockSpec((tk, tn), lambda i, j, kk: (kk, j))
    return pl.pallas_call(
        body, grid=(m // tm, n // tn, nk), in_specs=[a_spec, b_spec],
        out_specs=pl.BlockSpec((tm, tn), lambda i, j, kk: (i, j)),
        out_shape=jax.ShapeDtypeStruct((m, n), out_dtype),
        scratch_shapes=[pltpu.VMEM((tm, tn), F32)], compiler_params=_cp(), name=name)(a, b)


def _rms_mm(x, g, w, name):
    l, d = x.shape
    n = w.shape[1]
    tt = 256

    def body(x_ref, g_ref, w_ref, o_ref, h_ref):
        xv = x_ref[...]
        r = lax.rsqrt(jnp.mean(xv * xv, axis=-1, keepdims=True) + NORM_EPS)
        h = ((xv * r) * g_ref[...]).astype(BF16)
        h_ref[...] = h
        o_ref[...] = jnp.dot(h, w_ref[...], preferred_element_type=F32)

    return pl.pallas_call(
        body, grid=(l // tt,),
        in_specs=[pl.BlockSpec((tt, d), lambda i: (i, 0)), _full((1, d)), _full((d, n))],
        out_specs=[pl.BlockSpec((tt, n), lambda i: (i, 0)), pl.BlockSpec((tt, d), lambda i: (i, 0))],
        out_shape=[jax.ShapeDtypeStruct((l, n), F32), jax.ShapeDtypeStruct((l, d), BF16)],
        compiler_params=_cp(), name=name)(x, g, w)


def _mm_rmsbwd(a, w, x, g, dres, name):
    l, k = a.shape
    d = w.shape[0]
    tt = 256

    def body(a_ref, w_ref, x_ref, g_ref, r_ref, dx_ref, dg_ref):
        @pl.when(pl.program_id(0) == 0)
        def _():
            dg_ref[...] = jnp.zeros_like(dg_ref)

        dh = lax.dot_general(a_ref[...], w_ref[...], (((1,), (1,)), ((), ())), preferred_element_type=F32)
        xv = x_ref[...]
        r = lax.rsqrt(jnp.mean(xv * xv, axis=-1, keepdims=True) + NORM_EPS)
        xh = xv * r
        gy = dh * g_ref[...]
        dx_ref[...] = r_ref[...] + r * (gy - xh * jnp.mean(gy * xh, axis=-1, keepdims=True))
        dg_ref[...] += jnp.sum(dh * xh, axis=0, keepdims=True)

    row = pl.BlockSpec((tt, d), lambda i: (i, 0))
    return pl.pallas_call(
        body, grid=(l // tt,),
        in_specs=[pl.BlockSpec((tt, k), lambda i: (i, 0)), _full((d, k)), row, _full((1, d)), row],
        out_specs=[row, _full((1, d))],
        out_shape=[jax.ShapeDtypeStruct((l, d), F32), jax.ShapeDtypeStruct((1, d), F32)],
        compiler_params=_cp(), name=name)(a, w, x, g, dres)


def _rope(t, c, a, b):
    return t * c + pltpu.roll(t, 120, axis=1) * a + pltpu.roll(t, 8, axis=1) * b


def _rope_t(d, c, a, b):
    return d * c + pltpu.roll(d * a, 8, axis=1) + pltpu.roll(d * b, 120, axis=1)


def _band_sides(band):
    left = lax.broadcasted_iota(jnp.int32, band.shape, 1) < HEAD_DIM
    h0 = jnp.where(left, band, 0.0)
    h1 = jnp.where(left, 0.0, band)
    r0 = pltpu.roll(h0, HEAD_DIM, axis=1)
    r1 = pltpu.roll(h1, HEAD_DIM, axis=1)
    return ((h0.astype(BF16), r0.astype(BF16)), (r1.astype(BF16), h1.astype(BF16)))


def _attn_mask(i):
    qi = lax.broadcasted_iota(jnp.int32, (BLOCK, 2 * BLOCK), 0)
    kj = lax.broadcasted_iota(jnp.int32, (BLOCK, 2 * BLOCK), 1)
    delta = qi + BLOCK - kj
    return (delta >= 0) & (delta < BLOCK) & ((kj >= BLOCK) | (i > 0))


def _attn_probs(qc, kside, ok, sink):
    s = lax.dot_general(qc, kside, (((1,), (1,)), ((), ())), preferred_element_type=F32) * (HEAD_DIM ** -0.5)
    s = jnp.where(ok, s, NEG_INF)
    m = jnp.maximum(jnp.max(s, axis=-1, keepdims=True), sink)
    p = jnp.exp(s - m)
    es = jnp.exp(sink - m)
    inv = 1.0 / (jnp.sum(p, axis=-1, keepdims=True) + es)
    return p * inv, es * inv


def _attn_load(i, q_ref, kvc_ref, kvp_ref, tc_ref, ta_ref, tb_ref, pc_ref, pa_ref, pb_ref):
    c, a, b = tc_ref[...], ta_ref[...], tb_ref[...]
    kc = _rope(kvc_ref[:, :KV_W], c, a, b)
    kp = _rope(kvp_ref[:, :KV_W], pc_ref[...], pa_ref[...], pb_ref[...])
    kband = jnp.concatenate([kp, kc], axis=0)
    vband = jnp.concatenate([kvp_ref[:, KV_W:], kvc_ref[:, KV_W:]], axis=0)
    qs = [_rope(q_ref[:, 128 * j:128 * (j + 1)], c, a, b).astype(BF16) for j in range(4)]
    return qs, _band_sides(kband), _band_sides(vband), (c, a, b)


def _attn_specs(clamp):
    cur = lambda i: (clamp(i), 0)
    prev = lambda i: (jnp.maximum(clamp(i) - 1, 0), 0)
    return [
        pl.BlockSpec((BLOCK, ATTN_W), lambda i: (clamp(i), COL_Q // ATTN_W)),
        pl.BlockSpec((BLOCK, 2 * KV_W), lambda i: (clamp(i), COL_KV // (2 * KV_W))),
        pl.BlockSpec((BLOCK, 2 * KV_W), lambda i: (jnp.maximum(clamp(i) - 1, 0), COL_KV // (2 * KV_W))),
        pl.BlockSpec((BLOCK, 128), cur), pl.BlockSpec((BLOCK, 128), cur), pl.BlockSpec((BLOCK, 128), cur),
        pl.BlockSpec((BLOCK, 128), prev), pl.BlockSpec((BLOCK, 128), prev), pl.BlockSpec((BLOCK, 128), prev),
        pl.BlockSpec(memory_space=pltpu.SMEM),
    ]


def _attn_fwd(proj, tabs, sinks):
    tc, ta, tb = tabs

    def body(q_ref, kvc_ref, kvp_ref, tc_ref, ta_ref, tb_ref, pc_ref, pa_ref, pb_ref, sink_ref, o_ref):
        i = pl.program_id(0)
        qs, ks, vs, _ = _attn_load(i, q_ref, kvc_ref, kvp_ref, tc_ref, ta_ref, tb_ref, pc_ref, pa_ref, pb_ref)
        ok = _attn_mask(i)
        for j in range(4):
            kh = j // 2
            acc = jnp.zeros((BLOCK, 128), F32)
            for side in range(2):
                pn, _ = _attn_probs(qs[j], ks[kh][side], ok, sink_ref[0, 2 * j + side])
                acc += jnp.dot(pn.astype(BF16), vs[kh][side], preferred_element_type=F32)
            o_ref[:, 128 * j:128 * (j + 1)] = acc.astype(BF16)

    return pl.pallas_call(
        body, grid=(N_BLOCKS,), in_specs=_attn_specs(lambda i: i),
        out_specs=pl.BlockSpec((BLOCK, ATTN_W), lambda i: (i, 0)),
        out_shape=jax.ShapeDtypeStruct((SEQ, ATTN_W), BF16), compiler_params=_cp(), name="attn_fwd",
    )(proj, proj, proj, tc, ta, tb, tc, ta, tb, sinks)


def _attn_bwd(proj, tabs, sinks, do):
    tc, ta, tb = tabs
    last = N_BLOCKS - 1
    clamp = lambda i: jnp.minimum(i, last)

    def place(full, side, kh):
        left = lax.broadcasted_iota(jnp.int32, full.shape, 1) < HEAD_DIM
        valid = jnp.where(left, full, 0.0) if side == 0 else jnp.where(left, 0.0, full)
        return valid if side == kh else pltpu.roll(valid, HEAD_DIM, axis=1)

    def body(q_ref, kvc_ref, kvp_ref, tc_ref, ta_ref, tb_ref, pc_ref, pa_ref, pb_ref, sink_ref, do_ref,
             dq_ref, dkv_ref, ds_ref, carry_ref):
        i = pl.program_id(0)

        @pl.when(i == 0)
        def _():
            ds_ref[...] = jnp.zeros_like(ds_ref)
            carry_ref[...] = jnp.zeros_like(carry_ref)

        @pl.when(i > last)
        def _():
            dkv_ref[...] = carry_ref[...].astype(BF16)

        @pl.when(i <= last)
        def _():
            qs, ks, vs, (c, a, b) = _attn_load(i, q_ref, kvc_ref, kvp_ref, tc_ref, ta_ref, tb_ref,
                                               pc_ref, pa_ref, pb_ref)
            ok = _attn_mask(i)
            dk = jnp.zeros((2 * BLOCK, 128), F32)
            dv = jnp.zeros((2 * BLOCK, 128), F32)
            dsink = jnp.zeros((1, 128), F32)
            lane = lax.broadcasted_iota(jnp.int32, (1, 128), 1)
            for j in range(4):
                kh = j // 2
                doc = do_ref[:, 128 * j:128 * (j + 1)].astype(BF16)
                dq = jnp.zeros((BLOCK, 128), F32)
                for side in range(2):
                    pn, ps = _attn_probs(qs[j], ks[kh][side], ok, sink_ref[0, 2 * j + side])
                    dp = lax.dot_general(doc, vs[kh][side], (((1,), (1,)), ((), ())), preferred_element_type=F32)
                    dr = jnp.sum(pn * dp, axis=-1, keepdims=True)
                    dsb = (pn * (dp - dr) * (HEAD_DIM ** -0.5)).astype(BF16)
                    dsink += jnp.where(lane == 2 * j + side, -jnp.sum(ps * dr), 0.0)
                    dq += jnp.dot(dsb, ks[kh][side], preferred_element_type=F32)
                    tn = (((0,), (0,)), ((), ()))
                    dk += place(lax.dot_general(dsb, qs[j], tn, preferred_element_type=F32), side, kh)
                    dv += place(lax.dot_general(pn.astype(BF16), doc, tn, preferred_element_type=F32), side, kh)
                dq_ref[:, 128 * j:128 * (j + 1)] = _rope_t(dq, c, a, b).astype(BF16)
            ds_ref[...] += dsink
            dk_prev = _rope_t(dk[:BLOCK], pc_ref[...], pa_ref[...], pb_ref[...])
            dk_cur = _rope_t(dk[BLOCK:], c, a, b)
            prev = jnp.concatenate([dk_prev, dv[:BLOCK]], axis=1)
            dkv_ref[...] = (carry_ref[...] + prev).astype(BF16)
            carry_ref[...] = jnp.concatenate([dk_cur, dv[BLOCK:]], axis=1)

    return pl.pallas_call(
        body, grid=(N_BLOCKS + 1,),
        in_specs=_attn_specs(clamp) + [pl.BlockSpec((BLOCK, ATTN_W), lambda i: (clamp(i), 0))],
        out_specs=[pl.BlockSpec((BLOCK, ATTN_W), lambda i: (clamp(i), 0)),
                   pl.BlockSpec((BLOCK, 2 * KV_W), lambda i: (jnp.maximum(i - 1, 0), 0)),
                   pl.BlockSpec((1, 128), lambda i: (0, 0))],
        out_shape=[jax.ShapeDtypeStruct((SEQ, ATTN_W), BF16), jax.ShapeDtypeStruct((SEQ, 2 * KV_W), BF16),
                   jax.ShapeDtypeStruct((1, 128), F32)],
        scratch_shapes=[pltpu.VMEM((BLOCK, 2 * KV_W), F32)], compiler_params=_cp(), name="attn_bwd",
    )(proj, proj, proj, tc, ta, tb, tc, ta, tb, sinks, do)


def _shift_down(z, k):
    row = lax.broadcasted_iota(jnp.int32, z.shape, 0)
    return jnp.where(row < k, 0.0, pltpu.roll(z, k, axis=0))


def _shift_up(z, k):
    n = z.shape[0]
    row = lax.broadcasted_iota(jnp.int32, z.shape, 0)
    return jnp.where(row >= n - k, 0.0, pltpu.roll(z, n - k, axis=0))


def _conv_specs():
    return [pl.BlockSpec((SEQ, 128), lambda j: (0, COL_CB // 128 + j)),
            pl.BlockSpec((SEQ, 128), lambda j: (0, COL_CC // 128 + j)),
            pl.BlockSpec((SEQ, 128), lambda j: (0, COL_CX // 128 + j)),
            pl.BlockSpec((8, 128), lambda j: (0, j))]


def _conv_fwd(proj, cw):
    def body(cb_ref, cc_ref, cx_ref, w_ref, o_ref):
        z = cc_ref[...] * cx_ref[...]
        s = w_ref[0:1, :] * _shift_down(z, 2) + w_ref[1:2, :] * _shift_down(z, 1) + w_ref[2:3, :] * z
        o_ref[...] = (cb_ref[...] * s).astype(BF16)

    return pl.pallas_call(
        body, grid=(WIDTH // 128,), in_specs=_conv_specs(),
        out_specs=pl.BlockSpec((SEQ, 128), lambda j: (0, j)),
        out_shape=jax.ShapeDtypeStruct((SEQ, WIDTH), BF16), compiler_params=_cp(), name="conv_fwd",
    )(proj, proj, proj, cw)


def _conv_bwd(proj, cw, dout):
    def body(cb_ref, cc_ref, cx_ref, w_ref, do_ref, dcb_ref, dcc_ref, dcx_ref, dw_ref):
        cc, cx = cc_ref[...], cx_ref[...]
        z = cc * cx
        z1, z2 = _shift_down(z, 1), _shift_down(z, 2)
        w0, w1, w2 = w_ref[0:1, :], w_ref[1:2, :], w_ref[2:3, :]
        dout = do_ref[...]
        ds = dout * cb_ref[...]
        dcb_ref[...] = (dout * (w0 * z2 + w1 * z1 + w2 * z)).astype(BF16)
        dz = w2 * ds + w1 * _shift_up(ds, 1) + w0 * _shift_up(ds, 2)
        dcc_ref[...] = (dz * cx).astype(BF16)
        dcx_ref[...] = (dz * cc).astype(BF16)
        rows = [jnp.sum(ds * zz, axis=0, keepdims=True) for zz in (z2, z1, z)]
        dw_ref[...] = jnp.concatenate(rows + [jnp.zeros((5, 128), F32)], axis=0)

    col = lambda j: (0, j)
    return pl.pallas_call(
        body, grid=(WIDTH // 128,), in_specs=_conv_specs() + [pl.BlockSpec((SEQ, 128), col)],
        out_specs=[pl.BlockSpec((SEQ, 128), col), pl.BlockSpec((SEQ, 128), col), pl.BlockSpec((SEQ, 128), col),
                   pl.BlockSpec((8, 128), col)],
        out_shape=[jax.ShapeDtypeStruct((SEQ, WIDTH), BF16)] * 3 + [jax.ShapeDtypeStruct((8, WIDTH), F32)],
        compiler_params=_cp(), name="conv_bwd",
    )(proj, proj, proj, cw, dout)


def _ssm_prep_math(a_re, a_im, log_dt, bt_re, bt_im):
    dt = jnp.exp(log_dt)
    er = jnp.exp(a_re * dt)
    lr = er * jnp.cos(a_im * dt)
    li = er * jnp.sin(a_im * dt)
    n2 = a_re * a_re + a_im * a_im
    cr = ((lr - 1.0) * a_re + li * a_im) / n2
    ci = (li * a_re - (lr - 1.0) * a_im) / n2
    cr3, ci3 = cr[:, None, :], ci[:, None, :]
    return lr, li, cr3 * bt_re - ci3 * bt_im, cr3 * bt_im + ci3 * bt_re


_PREP_SHAPES = [(SSM_GROUPS, SSM_STATE), (SSM_GROUPS, SSM_STATE), (SSM_GROUPS, SSM_GROUP, SSM_STATE),
                (SSM_GROUPS, SSM_GROUP, SSM_STATE)]


def _ssm_prep(a_re, a_im, log_dt, bt_re, bt_im):
    def body(ar, ai, ld, br, bi, o0, o1, o2, o3):
        outs = _ssm_prep_math(ar[...], ai[...], ld[...], br[...], bi[...])
        for o, v in zip((o0, o1, o2, o3), outs):
            o[...] = v

    return pl.pallas_call(body, out_shape=[jax.ShapeDtypeStruct(s, F32) for s in _PREP_SHAPES],
                          name="ssm_prep")(a_re, a_im, log_dt, bt_re, bt_im)


def _ssm_prep_bwd(a_re, a_im, log_dt, bt_re, bt_im, cots):
    def body(ar, ai, ld, br, bi, c0, c1, c2, c3, o0, o1, o2, o3, o4):
        _, vjp = jax.vjp(_ssm_prep_math, ar[...], ai[...], ld[...], br[...], bi[...])
        for o, v in zip((o0, o1, o2, o3, o4), vjp((c0[...], c1[...], c2[...], c3[...]))):
            o[...] = v

    shapes = [_PREP_SHAPES[0], _PREP_SHAPES[0], (SSM_GROUPS, 1), _PREP_SHAPES[2], _PREP_SHAPES[2]]
    return pl.pallas_call(body, out_shape=[jax.ShapeDtypeStruct(s, F32) for s in shapes],
                          name="ssm_prep_bwd")(a_re, a_im, log_dt, bt_re, bt_im, *cots)


def _slab_mm(u, w_re, w_im, name):
    def body(u_ref, wr_ref, wi_ref, or_ref, oi_ref):
        uv = u_ref[...]
        or_ref[0] = jnp.dot(uv, wr_ref[...], preferred_element_type=F32)
        oi_ref[0] = jnp.dot(uv, wi_ref[...], preferred_element_type=F32)

    slab = pl.BlockSpec((1, SEQ, 128), lambda k: (k, 0, 0))
    return pl.pallas_call(
        body, grid=(SLABS,),
        in_specs=[_full((SEQ, WIDTH)), pl.BlockSpec((WIDTH, 128), lambda k: (0, k)),
                  pl.BlockSpec((WIDTH, 128), lambda k: (0, k))],
        out_specs=[slab, slab], out_shape=[jax.ShapeDtypeStruct((SLABS, SEQ, 128), F32)] * 2,
        compiler_params=_cp(), name=name)(u, w_re, w_im)


def _scan(b_re, b_im, a_re, a_im, reverse, name):
    def body(br_ref, bi_ref, ar_ref, ai_ref, xr_ref, xi_ref, pr_ref, pi_ref):
        ar = jnp.broadcast_to(ar_ref[...], (N_CHUNKS, 128))
        ai = jnp.broadcast_to(ai_ref[...], (N_CHUNKS, 128))

        def rows(tau):
            t = (CHUNK - 1 - tau) if reverse else tau
            return pl.ds(t, N_CHUNKS, stride=CHUNK)

        def first(tau, carry):
            sr, si, pr, pi = carry
            sr, si = ar * sr - ai * si + br_ref[rows(tau), :], ar * si + ai * sr + bi_ref[rows(tau), :]
            pr, pi = ar * pr - ai * pi, ar * pi + ai * pr
            xr_ref[rows(tau), :] = sr
            xi_ref[rows(tau), :] = si
            at = pl.ds(pl.multiple_of(tau * N_CHUNKS, N_CHUNKS), N_CHUNKS)
            pr_ref[at, :] = pr
            pi_ref[at, :] = pi
            return sr, si, pr, pi

        zero = jnp.zeros((N_CHUNKS, 128), F32)
        er, ei, qr, qi = lax.fori_loop(0, CHUNK, first, (zero, zero, zero + 1.0, zero))

        shift = _shift_up if reverse else _shift_down
        for k in (1, 2, 4):
            sr, si = shift(er, k), shift(ei, k)
            er, ei = er + qr * sr - qi * si, ei + qr * si + qi * sr
            qr, qi = qr * qr - qi * qi, 2.0 * qr * qi
        cr, ci = shift(er, 1), shift(ei, 1)

        def second(tau, _):
            at = pl.ds(pl.multiple_of(tau * N_CHUNKS, N_CHUNKS), N_CHUNKS)
            pr, pi = pr_ref[at, :], pi_ref[at, :]
            xr_ref[rows(tau), :] += pr * cr - pi * ci
            xi_ref[rows(tau), :] += pr * ci + pi * cr
            return 0

        lax.fori_loop(0, CHUNK, second, 0)

    slab = pl.BlockSpec((None, SEQ, 128), lambda k: (k, 0, 0))
    vec = pl.BlockSpec((None, 1, 128), lambda k: (k, 0, 0))
    return pl.pallas_call(
        body, grid=(SLABS,), in_specs=[slab, slab, vec, vec], out_specs=[slab, slab],
        out_shape=[jax.ShapeDtypeStruct((SLABS, SEQ, 128), F32)] * 2,
        scratch_shapes=[pltpu.VMEM((CHUNK * N_CHUNKS, 128), F32)] * 2, compiler_params=_cp(), name=name,
    )(b_re, b_im, a_re, a_im)


def _slab_contract(x_re, x_im, m_re, m_im, e1, e2, name):
    tt = 256

    def body(xr_ref, xi_ref, mr_ref, mi_ref, e1_ref, e2_ref, o_ref):
        acc = e1_ref[...].astype(F32) * e2_ref[...]
        for k in range(SLABS):
            acc += jnp.dot(xr_ref[k].astype(BF16), mr_ref[k], preferred_element_type=F32)
            acc += jnp.dot(xi_ref[k].astype(BF16), mi_ref[k], preferred_element_type=F32)
        o_ref[...] = acc

    xs = pl.BlockSpec((SLABS, tt, 128), lambda i: (0, i, 0))
    ecol = e1.shape[1] // WIDTH
    return pl.pallas_call(
        body, grid=(SEQ // tt,),
        in_specs=[xs, xs, _full((SLABS, 128, WIDTH)), _full((SLABS, 128, WIDTH)),
                  pl.BlockSpec((tt, WIDTH), (lambda i: (i, COL_U // WIDTH)) if ecol > 1 else (lambda i: (i, 0))),
                  _full((1, WIDTH))],
        out_specs=pl.BlockSpec((tt, WIDTH), lambda i: (i, 0)),
        out_shape=jax.ShapeDtypeStruct((SEQ, WIDTH), F32), compiler_params=_cp(), name=name,
    )(x_re, x_im, m_re, m_im, e1, e2)


def _slab_tn(s_re, s_im, v, name):
    def body(sr_ref, si_ref, v_ref, or_ref, oi_ref):
        tn = (((0,), (0,)), ((), ()))
        vv = v_ref[...]
        or_ref[0] = lax.dot_general(sr_ref[0].astype(BF16), vv, tn, preferred_element_type=F32)
        oi_ref[0] = lax.dot_general(si_ref[0].astype(BF16), vv, tn, preferred_element_type=F32)

    slab = pl.BlockSpec((1, SEQ, 128), lambda k: (k, 0, 0))
    out = pl.BlockSpec((1, 128, WIDTH), lambda k: (k, 0, 0))
    return pl.pallas_call(
        body, grid=(SLABS,), in_specs=[slab, slab, _full((SEQ, WIDTH))], out_specs=[out, out],
        out_shape=[jax.ShapeDtypeStruct((SLABS, 128, WIDTH), F32)] * 2, compiler_params=_cp(), name=name,
    )(s_re, s_im, v)


def _state_grad(x_re, x_im, l_re, l_im):
    def body(xr_ref, xi_ref, lr_ref, li_ref, or_ref, oi_ref):
        xr, xi = _shift_down(xr_ref[0], 1), _shift_down(xi_ref[0], 1)
        lr, li = lr_ref[0], li_ref[0]
        or_ref[0] = jnp.sum(xr * lr + xi * li, axis=0, keepdims=True)
        oi_ref[0] = jnp.sum(xr * li - xi * lr, axis=0, keepdims=True)

    slab = pl.BlockSpec((1, SEQ, 128), lambda k: (k, 0, 0))
    vec = pl.BlockSpec((1, 1, 128), lambda k: (k, 0, 0))
    return pl.pallas_call(
        body, grid=(SLABS,), in_specs=[slab] * 4, out_specs=[vec, vec],
        out_shape=[jax.ShapeDtypeStruct((SLABS, 1, 128), F32)] * 2, compiler_params=_cp(), name="state_grad",
    )(x_re, x_im, l_re, l_im)


_GELU_C = math.sqrt(2.0 / math.pi)


def _gelu(y):
    return 0.5 * y * (1.0 + jnp.tanh(_GELU_C * (y + 0.044715 * (y * y * y))))


def _glu_fwd(y, wglu):
    tt = 512

    def body(y_ref, w_ref, z_ref):
        ys = _gelu(y_ref[...])
        a = jnp.dot(ys.astype(BF16), w_ref[...], preferred_element_type=F32)
        z_ref[...] = (ys * jax.nn.sigmoid(a)).astype(BF16)

    blk = pl.BlockSpec((tt, WIDTH), lambda i: (i, 0))
    return pl.pallas_call(body, grid=(SEQ // tt,), in_specs=[blk, _full((WIDTH, WIDTH))], out_specs=blk,
                          out_shape=jax.ShapeDtypeStruct((SEQ, WIDTH), BF16), compiler_params=_cp(),
                          name="glu_fwd")(y, wglu)


def _glu_bwd(y, wglu, dz, proj):
    tt = 512

    def body(y_ref, w_ref, dz_ref, u_ref, dy_ref, ys_ref, da_ref, dd_ref):
        @pl.when(pl.program_id(0) == 0)
        def _():
            dd_ref[...] = jnp.zeros_like(dd_ref)

        yv = y_ref[...]
        t = jnp.tanh(_GELU_C * (yv + 0.044715 * (yv * yv * yv)))
        ys = 0.5 * yv * (1.0 + t)
        ysb = ys.astype(BF16)
        sg = jax.nn.sigmoid(jnp.dot(ysb, w_ref[...], preferred_element_type=F32))
        dz = dz_ref[...].astype(F32)
        da = (dz * ys * sg * (1.0 - sg)).astype(BF16)
        dys = dz * sg + lax.dot_general(da, w_ref[...], (((1,), (1,)), ((), ())), preferred_element_type=F32)
        dy = dys * (0.5 * (1.0 + t) + 0.5 * yv * (1.0 - t * t) * _GELU_C * (1.0 + 3 * 0.044715 * (yv * yv)))
        dy_ref[...] = dy.astype(BF16)
        ys_ref[...] = ysb
        da_ref[...] = da
        dd_ref[...] += jnp.sum(dy * u_ref[...], axis=0, keepdims=True)

    blk = pl.BlockSpec((tt, WIDTH), lambda i: (i, 0))
    return pl.pallas_call(
        body, grid=(SEQ // tt,),
        in_specs=[blk, _full((WIDTH, WIDTH)), blk, pl.BlockSpec((tt, WIDTH), lambda i: (i, COL_U // WIDTH))],
        out_specs=[blk, blk, blk, _full((1, WIDTH))],
        out_shape=[jax.ShapeDtypeStruct((SEQ, WIDTH), BF16)] * 3 + [jax.ShapeDtypeStruct((1, WIDTH), F32)],
        compiler_params=_cp(), name="glu_bwd")(y, wglu, dz, proj)


def _mix_specs(tt):
    row = lambda w: pl.BlockSpec((tt, w), lambda i: (i, 0))
    gate = lambda j: pl.BlockSpec((tt, D_MODEL), lambda i: (i, j))
    wo = _full((WIDTH, D_MODEL))
    return [row(D_MODEL), row(WIDTH), row(WIDTH), row(WIDTH), gate(0), gate(1), gate(2), _full((1, GATE_W)),
            wo, wo, wo, _full((D_MODEL, D_MODEL))]


def _mix_branches(o_ref, c_ref, z_ref, g_refs, b_ref, wa_ref, wc_ref, ws_ref):
    ys = [jnp.dot(r[...], w[...], preferred_element_type=F32)
          for r, w in ((o_ref, wa_ref), (c_ref, wc_ref), (z_ref, ws_ref))]
    gates = [jax.nn.sigmoid(g_refs[j][...] + b_ref[:, D_MODEL * j:D_MODEL * (j + 1)]) for j in range(3)]
    return ys, gates


def _mix_fwd(x, o, cv, z, proj, b_gate, wa, wc, ws, wmix):
    tt = 256

    def body(x_ref, o_ref, c_ref, z_ref, g0, g1, g2, b_ref, wa_ref, wc_ref, ws_ref, wm_ref, x1_ref):
        ys, gates = _mix_branches(o_ref, c_ref, z_ref, (g0, g1, g2), b_ref, wa_ref, wc_ref, ws_ref)
        merged = gates[0] * ys[0] + gates[1] * ys[1] + gates[2] * ys[2]
        x1_ref[...] = x_ref[...] + jnp.dot(merged.astype(BF16), wm_ref[...], preferred_element_type=F32)

    return pl.pallas_call(
        body, grid=(SEQ // tt,), in_specs=_mix_specs(tt),
        out_specs=pl.BlockSpec((tt, D_MODEL), lambda i: (i, 0)),
        out_shape=jax.ShapeDtypeStruct((SEQ, D_MODEL), F32), compiler_params=_cp(), name="mix_fwd",
    )(x, o, cv, z, proj, proj, proj, b_gate, wa, wc, ws, wmix)


def _mix_bwd(dx1, o, cv, z, proj, b_gate, wa, wc, ws, wmix):
    tt = 256

    def body(dx_ref, o_ref, c_ref, z_ref, g0, g1, g2, b_ref, wa_ref, wc_ref, ws_ref, wm_ref,
             mg_ref, dya_ref, dyc_ref, dys_ref, do_ref, dc_ref, dz_ref, dgl_ref, db_ref):
        @pl.when(pl.program_id(0) == 0)
        def _():
            db_ref[...] = jnp.zeros_like(db_ref)

        nt = (((1,), (1,)), ((), ()))
        ys, gates = _mix_branches(o_ref, c_ref, z_ref, (g0, g1, g2), b_ref, wa_ref, wc_ref, ws_ref)
        mg_ref[...] = (gates[0] * ys[0] + gates[1] * ys[1] + gates[2] * ys[2]).astype(BF16)
        dm = lax.dot_general(dx_ref[...].astype(BF16), wm_ref[...], nt, preferred_element_type=F32)
        for j, (dy_ref, w_ref, d_ref) in enumerate(((dya_ref, wa_ref, do_ref), (dyc_ref, wc_ref, dc_ref),
                                                    (dys_ref, ws_ref, dz_ref))):
            dy = (dm * gates[j]).astype(BF16)
            dy_ref[...] = dy
            d_ref[...] = lax.dot_general(dy, w_ref[...], nt, preferred_element_type=F32)
            dgl = dm * ys[j] * gates[j] * (1.0 - gates[j])
            dgl_ref[:, D_MODEL * j:D_MODEL * (j + 1)] = dgl.astype(BF16)
            db_ref[:, D_MODEL * j:D_MODEL * (j + 1)] += jnp.sum(dgl, axis=0, keepdims=True)

    row = lambda w: pl.BlockSpec((tt, w), lambda i: (i, 0))
    sds = jax.ShapeDtypeStruct
    return pl.pallas_call(
        body, grid=(SEQ // tt,), in_specs=_mix_specs(tt),
        out_specs=[row(D_MODEL)] * 4 + [row(WIDTH)] * 3 + [row(GATE_W), _full((1, GATE_W))],
        out_shape=[sds((SEQ, D_MODEL), BF16)] * 4 + [sds((SEQ, WIDTH), F32)] * 3
        + [sds((SEQ, GATE_W), BF16), sds((1, GATE_W), F32)],
        compiler_params=_cp(), name="mix_bwd",
    )(dx1, o, cv, z, proj, proj, proj, b_gate, wa, wc, ws, wmix)


def _ffn_out_fwd(x1, gu, wout):
    tt = 256

    def body(x_ref, gt_ref, up_ref, w_ref, o_ref):
        gt = gt_ref[...]
        act = (gt * jax.nn.sigmoid(gt) * up_ref[...]).astype(BF16)
        o_ref[...] = x_ref[...] + jnp.dot(act, w_ref[...], preferred_element_type=F32)

    return pl.pallas_call(
        body, grid=(SEQ // tt,),
        in_specs=[pl.BlockSpec((tt, D_MODEL), lambda i: (i, 0)), pl.BlockSpec((tt, FFN_H), lambda i: (i, 0)),
                  pl.BlockSpec((tt, FFN_H), lambda i: (i, 1)), _full((FFN_H, D_MODEL))],
        out_specs=pl.BlockSpec((tt, D_MODEL), lambda i: (i, 0)),
        out_shape=jax.ShapeDtypeStruct((SEQ, D_MODEL), F32), compiler_params=_cp(), name="ffn_out_fwd",
    )(x1, gu, gu, wout)


def _ffn_out_bwd(dx2, gu, wout):
    tt = 256

    def body(dx_ref, gt_ref, up_ref, w_ref, dgu_ref, act_ref):
        gt, up = gt_ref[...], up_ref[...]
        sg = jax.nn.sigmoid(gt)
        silu = gt * sg
        act_ref[...] = (silu * up).astype(BF16)
        dact = lax.dot_general(dx_ref[...].astype(BF16), w_ref[...], (((1,), (1,)), ((), ())),
                               preferred_element_type=F32)
        dgu_ref[:, :FFN_H] = (dact * up * (sg * (1.0 + gt * (1.0 - sg)))).astype(BF16)
        dgu_ref[:, FFN_H:] = (dact * silu).astype(BF16)

    return pl.pallas_call(
        body, grid=(SEQ // tt,),
        in_specs=[pl.BlockSpec((tt, D_MODEL), lambda i: (i, 0)), pl.BlockSpec((tt, FFN_H), lambda i: (i, 0)),
                  pl.BlockSpec((tt, FFN_H), lambda i: (i, 1)), _full((FFN_H, D_MODEL))],
        out_specs=[pl.BlockSpec((tt, 2 * FFN_H), lambda i: (i, 0)), pl.BlockSpec((tt, FFN_H), lambda i: (i, 0))],
        out_shape=[jax.ShapeDtypeStruct((SEQ, 2 * FFN_H), BF16), jax.ShapeDtypeStruct((SEQ, FFN_H), BF16)],
        compiler_params=_cp(), name="ffn_out_bwd",
    )(dx2, gu, gu, wout)


def _loss_head(x, g, target):
    tt = 256

    def body(x_ref, g_ref, t_ref, loss_ref, dx_ref, dg_ref):
        @pl.when(pl.program_id(0) == 0)
        def _():
            loss_ref[...] = jnp.zeros_like(loss_ref)
            dg_ref[...] = jnp.zeros_like(dg_ref)

        xv = x_ref[...]
        r = lax.rsqrt(jnp.mean(xv * xv, axis=-1, keepdims=True) + NORM_EPS)
        xh = xv * r
        err = xh * g_ref[...] - t_ref[...]
        loss_ref[...] += 0.5 * jnp.sum(jnp.mean(err * err, axis=-1, keepdims=True))
        dy = err * (1.0 / D_MODEL)
        gy = dy * g_ref[...]
        dx_ref[...] = r * (gy - xh * jnp.mean(gy * xh, axis=-1, keepdims=True))
        dg_ref[...] += jnp.sum(dy * xh, axis=0, keepdims=True)

    row = pl.BlockSpec((tt, D_MODEL), lambda i: (i, 0))
    return pl.pallas_call(
        body, grid=(SEQ // tt,), in_specs=[row, _full((1, D_MODEL)), row],
        out_specs=[_full((1, 128)), row, _full((1, D_MODEL))],
        out_shape=[jax.ShapeDtypeStruct((1, 128), F32), jax.ShapeDtypeStruct((SEQ, D_MODEL), F32),
                   jax.ShapeDtypeStruct((1, D_MODEL), F32)],
        compiler_params=_cp(), name="loss_head")(x, g, target)


def _adamw(parts, w, m, v, tr, name):
    rows, cols = w.shape

    def body(p_ref, w_ref, m_ref, v_ref, g_ref, d_ref, nm_ref, nv_ref):
        g = p_ref[0].astype(F32)
        for k in range(1, N_DEV):
            g = g + p_ref[k].astype(F32)
        nm = B1 * m_ref[...] + (1.0 - B1) * g
        nv = B2 * v_ref[...] + (1.0 - B2) * (g * g)
        m_hat = nm / (1.0 - B1 ** STEP)
        v_hat = nv / (1.0 - B2 ** STEP)
        g_ref[...] = g
        d_ref[...] = -LR * (m_hat / (jnp.sqrt(v_hat) + ADAM_EPS) + WD * w_ref[...])
        nm_ref[...] = nm
        nv_ref[...] = nv

    blk = pl.BlockSpec((tr, cols), lambda i: (i, 0))
    return pl.pallas_call(
        body, grid=(rows // tr,), in_specs=[pl.BlockSpec((N_DEV, tr, cols), lambda i: (0, i, 0)), blk, blk, blk],
        out_specs=[blk] * 4, out_shape=[jax.ShapeDtypeStruct((rows, cols), F32)] * 4,
        compiler_params=_cp(), name=name)(parts, w, m, v)


MESH_ID = pl.DeviceIdType.MESH
ANY = pl.BlockSpec(memory_space=pl.ANY)


def _all_gather(shard, name):
    r, c_ = shard.shape

    def body(x_ref, out_ref, send_sems, recv_sems, local_sem):
        x, y, c = lax.axis_index("x"), lax.axis_index("y"), lax.axis_index("c")
        me, sibling = (x, y, c), (x, y, 1 - c)
        chips = [(1 - x, y), (x, 1 - y), (1 - x, 1 - y)]

        def slot(px, py, pc):
            return out_ref.at[4 * px + 2 * py + pc]

        def copy(k, block, to, src=None):
            return pltpu.make_async_remote_copy(
                src_ref=slot(*block) if src is None else src, dst_ref=slot(*block),
                send_sem=send_sems.at[k], recv_sem=recv_sems.at[k], device_id=to, device_id_type=MESH_ID)

        mine = pltpu.make_async_copy(x_ref, slot(*me), local_sem)
        mine.start()
        first = [copy(0, me, sibling, src=x_ref)]
        first += [copy(1 + j, me, (*chip, c), src=x_ref) for j, chip in enumerate(chips)]
        for cp in first:
            cp.start()
        passed = [copy(4 + j, (*chip, c), sibling) for j, chip in enumerate(chips)]
        for j, chip in enumerate(chips):
            copy(1 + j, (*chip, c), me).wait_recv()
            passed[j].start()
        copy(0, sibling, me).wait_recv()
        for j, chip in enumerate(chips):
            copy(4 + j, (*chip, 1 - c), me).wait_recv()
        for cp in first + passed:
            cp.wait_send()
        mine.wait()

    return pl.pallas_call(
        body, in_specs=[ANY], out_specs=ANY, out_shape=jax.ShapeDtypeStruct((N_DEV, r, c_), shard.dtype),
        scratch_shapes=[pltpu.SemaphoreType.DMA((7,)), pltpu.SemaphoreType.DMA((7,)), pltpu.SemaphoreType.DMA],
        name=name)(shard)


def _scatter_parts(parts, name):
    def body(p_ref, out_ref, send_sems, recv_sems, local_sem):
        x, y, c = lax.axis_index("x"), lax.axis_index("y"), lax.axis_index("c")
        mine_idx = 4 * x + 2 * y + c
        mine = pltpu.make_async_copy(p_ref.at[mine_idx], out_ref.at[mine_idx], local_sem)
        mine.start()
        copies = []
        for rel in range(1, N_DEV):
            px = 1 - x if rel & 4 else x
            py = 1 - y if rel & 2 else y
            pc = 1 - c if rel & 1 else c
            copies.append(pltpu.make_async_remote_copy(
                src_ref=p_ref.at[4 * px + 2 * py + pc], dst_ref=out_ref.at[mine_idx],
                send_sem=send_sems.at[rel - 1], recv_sem=recv_sems.at[rel - 1],
                device_id=(px, py, pc), device_id_type=MESH_ID))
        for cp in copies:
            cp.start()
        for cp in copies:
            cp.wait_recv()
        for cp in copies:
            cp.wait_send()
        mine.wait()

    return pl.pallas_call(
        body, in_specs=[ANY], out_specs=ANY, out_shape=jax.ShapeDtypeStruct(parts.shape, parts.dtype),
        scratch_shapes=[pltpu.SemaphoreType.DMA((7,)), pltpu.SemaphoreType.DMA((7,)), pltpu.SemaphoreType.DMA],
        name=name)(parts)


def _pack_shards(t):
    rows = []
    for name, r in PACK:
        a = t[name].reshape(-1)
        if name == "conv_w":
            a = jnp.pad(a, (0, r * 1024 - a.shape[0]))
        rows.append(a.reshape(r, 1024))
    return jnp.concatenate(rows, axis=0)


def _unpack_shards(p):
    shapes = {"w_in": (1024, 736), "w_attn_o": (512, 128), "w_conv_o": (512, 128), "w_ssm_glu": (64, 512),
              "w_ssm_o": (512, 128), "w_mix_o": (128, 1024), "w_ffn_in": (1024, 704), "w_ffn_out": (352, 1024)}
    out = {}
    for name, (o, r) in PACK_OFF.items():
        a = p[o:o + r]
        out[name] = a.reshape(-1)[:192].reshape(3, 64) if name == "conv_w" else a.reshape(shapes[name])
    return out


def _cols_from_blocks(g, name, rows, cols):
    o, r = PACK_OFF[name]
    return g[:, o:o + r].reshape(N_DEV, rows, cols).transpose(1, 0, 2).reshape(rows, N_DEV * cols)


def _rows_from_blocks(g, name, rows, cols):
    o, r = PACK_OFF[name]
    return g[:, o:o + r].reshape(N_DEV * rows, cols)


def _full_weights(g):
    w_in = _cols_from_blocks(g, "w_in", 1024, 736)
    w_in = jnp.concatenate([w_in[:, 2816:], w_in[:, :512], w_in[:, 768:2816], w_in[:, 512:768]], axis=1)
    o, _ = PACK_OFF["conv_w"]
    conv = g[:, o].reshape(N_DEV, 1024)[:, :192].reshape(N_DEV, 3, 64).transpose(1, 0, 2).reshape(3, WIDTH)
    return {
        "w_in": w_in,
        "w_attn_o": _cols_from_blocks(g, "w_attn_o", 512, 128),
        "w_conv_o": _cols_from_blocks(g, "w_conv_o", 512, 128),
        "w_ssm_glu": _rows_from_blocks(g, "w_ssm_glu", 64, 512),
        "w_ssm_o": _cols_from_blocks(g, "w_ssm_o", 512, 128),
        "w_mix_o": _rows_from_blocks(g, "w_mix_o", 128, 1024),
        "w_ffn_in": _cols_from_blocks(g, "w_ffn_in", 1024, 704),
        "w_ffn_out": _rows_from_blocks(g, "w_ffn_out", 352, 1024),
        "conv_w": jnp.pad(conv.astype(F32), ((0, 5), (0, 0))),
    }


def _col_blocks(dw, cols):
    rows = dw.shape[0]
    return dw.reshape(rows, N_DEV, cols).transpose(1, 0, 2).reshape(N_DEV, rows * cols // 1024, 1024)


def _grad_parts(gr):
    d = gr["w_in"]
    d = jnp.concatenate([d[:, COL_Q:COL_Q + 512], d[:, COL_KV:], d[:, COL_CB:COL_KV], d[:, :COL_Q]], axis=1)
    conv = gr["conv_w"][:3].astype(BF16).reshape(3, N_DEV, 64).transpose(1, 0, 2).reshape(N_DEV, 192)
    conv = jnp.pad(conv, ((0, 0), (0, 16 * 1024 - 192))).reshape(N_DEV, 16, 1024)
    return jnp.concatenate([
        _col_blocks(d, 736), _col_blocks(gr["w_attn_o"], 128), _col_blocks(gr["w_conv_o"], 128),
        gr["w_ssm_glu"].reshape(N_DEV, 32, 1024), _col_blocks(gr["w_ssm_o"], 128),
        gr["w_mix_o"].reshape(N_DEV, 128, 1024), _col_blocks(gr["w_ffn_in"], 704),
        gr["w_ffn_out"].reshape(N_DEV, 352, 1024), conv], axis=1)


def _embed(t):
    eye = jnp.eye(SSM_GROUPS, dtype=t.dtype)
    return (t[:, :, None, :] * eye[:, None, :, None]).reshape(WIDTH, SSM_GROUPS * SSM_STATE)


def _diag_blocks(t):
    t = t.reshape(SSM_GROUPS, SSM_STATE, SSM_GROUPS, SSM_GROUP)
    return jnp.einsum("gpgh->ghp", t)


def _rope_tabs():
    pos = jnp.arange(SEQ, dtype=F32)
    inv_freq = ROPE_THETA ** (-jnp.arange(0, ROT_DIM, 2, dtype=F32) / ROT_DIM)
    ang = pos[:, None] * inv_freq[None, :]
    cos, sin = jnp.cos(ang), jnp.sin(ang)
    one, zero = jnp.ones((SEQ, HEAD_DIM - ROT_DIM), F32), jnp.zeros((SEQ, HEAD_DIM - ROT_DIM), F32)
    z8 = jnp.zeros((SEQ, 8), F32)
    head = lambda *p: jnp.tile(jnp.concatenate(p, axis=1), (1, 2))
    return head(cos, cos, one), head(-sin, z8, zero), head(z8, sin, zero)


def _ssm_mats(sp):
    lr, li, bbr, bbi = _ssm_prep(sp["a_re"], sp["a_im"], sp["log_dt"], sp["bt_re"], sp["bt_im"])
    b_re, b_im = _embed(bbr), _embed(bbi)
    c_re, c_im = _embed(sp["c_re"]), _embed(sp["c_im"])
    slab3 = lambda t: t.T.reshape(SLABS, 128, WIDTH).astype(BF16)
    return {
        "a_re": lr.reshape(SLABS, 1, 128), "a_im": li.reshape(SLABS, 1, 128),
        "b_re": b_re.astype(BF16), "b_im": b_im.astype(BF16),
        "c_re": c_re.astype(BF16), "c_im_neg": (-c_im).astype(BF16),
        "bt_re": slab3(b_re), "bt_im": slab3(b_im), "ct_re": slab3(c_re), "ct_im_neg": slab3(-c_im),
    }


def _layer_fwd(x, w, sm, sp, tabs):
    proj, h = _rms_mm(x, sm["norm_mix"], w["w_in"], "rms_mm_in")
    o = _attn_fwd(proj, tabs, sm["attn_sinks"])
    cv = _conv_fwd(proj, w["conv_w"])
    mats = _ssm_mats(sp)
    u16 = proj[:, COL_U:COL_U + WIDTH].astype(BF16)
    bu_re, bu_im = _slab_mm(u16, mats["b_re"], mats["b_im"], "slab_mm")
    x_re, x_im = _scan(bu_re, bu_im, mats["a_re"], mats["a_im"], False, "scan_fwd")
    y = _slab_contract(x_re, x_im, mats["ct_re"], mats["ct_im_neg"], proj, sp["d"], "slab_contract_y")
    z = _glu_fwd(y, w["w_ssm_glu"])
    x1 = _mix_fwd(x, o, cv, z, proj, sm["b_gate"], w["w_attn_o"], w["w_conv_o"], w["w_ssm_o"], w["w_mix_o"])
    gu, h2 = _rms_mm(x1, sm["norm_ffn"], w["w_ffn_in"], "rms_mm_ffn")
    x2 = _ffn_out_fwd(x1, gu, w["w_ffn_out"])
    kept = dict(x=x, proj=proj, h=h, o=o, cv=cv, z=z, y=y, u16=u16, x_re=x_re, x_im=x_im, mats=mats,
                x1=x1, gu=gu, h2=h2)
    return x2, kept


def _layer_bwd(dx2, k, w, sm, sp, tabs):
    tn = dict(ta=True, out_dtype=BF16)
    dgu, act = _ffn_out_bwd(dx2, k["gu"], w["w_ffn_out"])
    dx16 = dx2.astype(BF16)
    g = {"w_ffn_out": _mm(act, dx16, tm=FFN_H // 2, tn=1024, tk=512, name="mm_tn_ffn_out", **tn),
         "w_ffn_in": _mm(k["h2"], dgu, tm=512, tn=FFN_H, tk=512, name="mm_tn_ffn_in", **tn)}
    dx1, d_norm_ffn = _mm_rmsbwd(dgu, w["w_ffn_in"], k["x1"], sm["norm_ffn"], dx2, "mm_rmsbwd_ffn")

    mg, dya, dyc, dys, do, dcv, dz, dgl, db_gate = _mix_bwd(
        dx1, k["o"], k["cv"], k["z"], k["proj"], sm["b_gate"], w["w_attn_o"], w["w_conv_o"], w["w_ssm_o"],
        w["w_mix_o"])
    dx116 = dx1.astype(BF16)
    g["w_mix_o"] = _mm(mg, dx116, tm=1024, tn=1024, tk=512, name="mm_tn_mix", **tn)
    for name, a, dy in (("w_attn_o", k["o"], dya), ("w_conv_o", k["cv"], dyc), ("w_ssm_o", k["z"], dys)):
        g[name] = _mm(a, dy, tm=512, tn=1024, tk=512, name="mm_tn_branch", **tn)

    mats = k["mats"]
    dy16, ys16, da16, dd = _glu_bwd(k["y"], w["w_ssm_glu"], dz, k["proj"])
    g["w_ssm_glu"] = _mm(ys16, da16, tm=512, tn=512, tk=512, name="mm_tn_glu", **tn)
    gx_re, gx_im = _slab_mm(dy16, mats["c_re"], mats["c_im_neg"], "slab_mm")
    l_re, l_im = _scan(gx_re, gx_im, mats["a_re"], -mats["a_im"], True, "scan_bwd")
    du = _slab_contract(l_re, l_im, mats["bt_re"], mats["bt_im"], dy16, sp["d"], "slab_contract_du")
    da_re, da_im = _state_grad(k["x_re"], k["x_im"], l_re, l_im)
    db_re, db_im = _slab_tn(l_re, l_im, k["u16"], "slab_tn")
    dc_re, dc_im = _slab_tn(k["x_re"], k["x_im"], dy16, "slab_tn")
    cots = (da_re.reshape(SSM_GROUPS, SSM_STATE), da_im.reshape(SSM_GROUPS, SSM_STATE),
            _diag_blocks(db_re), _diag_blocks(db_im))
    d_a_re, d_a_im, d_log_dt, d_bt_re, d_bt_im = _ssm_prep_bwd(
        sp["a_re"], sp["a_im"], sp["log_dt"], sp["bt_re"], sp["bt_im"], cots)

    dcb, dcc, dcx, d_conv_w = _conv_bwd(k["proj"], w["conv_w"], dcv)
    dq, dkv, d_sinks = _attn_bwd(k["proj"], tabs, sm["attn_sinks"], do)
    g["conv_w"] = d_conv_w

    dproj = jnp.concatenate([dgl, dq, dcb, dcc, dcx, du.astype(BF16), dkv], axis=1)
    g["w_in"] = _mm(k["h"], dproj, tm=512, tn=IN_COLS // 2, tk=512, name="mm_tn_in", **tn)
    dx, d_norm_mix = _mm_rmsbwd(dproj, w["w_in"], k["x"], sm["norm_mix"], dx1, "mm_rmsbwd_in")

    small = {
        "norm_mix": d_norm_mix, "b_gate": db_gate, "attn_sinks": d_sinks[:, :N_Q_HEADS],
        "ssm_a_re": d_a_re, "ssm_a_im": d_a_im, "ssm_b_re": d_bt_re.transpose(0, 2, 1),
        "ssm_b_im": d_bt_im.transpose(0, 2, 1), "ssm_c_re": _diag_blocks(dc_re), "ssm_c_im": -_diag_blocks(dc_im),
        "ssm_d": dd, "ssm_log_dt": d_log_dt, "norm_ffn": d_norm_ffn,
    }
    return dx, g, small


def _pack_small(layers, final):
    flat = [layers[i][name].reshape(-1) for i in range(DEPTH) for name, _ in SMALL] + [final.reshape(-1)]
    flat = jnp.concatenate(flat)
    return jnp.pad(flat, (0, SMALL_ROWS * 128 - flat.shape[0])).reshape(SMALL_ROWS, 128)


def _unpack_small(p, shapes):
    flat = p.reshape(-1)
    per = sum(n for _, n in SMALL)
    out = {}
    off = 0
    for name, n in SMALL:
        out[name] = jnp.stack([flat[i * per + off:i * per + off + n] for i in range(DEPTH)]).reshape(shapes[name])
        off += n
    return out, flat[DEPTH * per:DEPTH * per + D_MODEL]


def kernel(x, norm_mix, w_in, b_gate, attn_sinks, w_attn_o, conv_w, w_conv_o, ssm_a_re, ssm_a_im, ssm_b_re, ssm_b_im, ssm_c_re, ssm_c_im, ssm_d, ssm_log_dt, w_ssm_glu, w_ssm_o, w_mix_o, norm_ffn, w_ffn_in, w_ffn_out, norm_final, loss_target, m_norm_mix, m_w_in, m_b_gate, m_attn_sinks, m_w_attn_o, m_conv_w, m_w_conv_o, m_ssm_a_re, m_ssm_a_im, m_ssm_b_re, m_ssm_b_im, m_ssm_c_re, m_ssm_c_im, m_ssm_d, m_ssm_log_dt, m_w_ssm_glu, m_w_ssm_o, m_w_mix_o, m_norm_ffn, m_w_ffn_in, m_w_ffn_out, m_norm_final, v_norm_mix, v_w_in, v_b_gate, v_attn_sinks, v_w_attn_o, v_conv_w, v_w_conv_o, v_ssm_a_re, v_ssm_a_im, v_ssm_b_re, v_ssm_b_im, v_ssm_c_re, v_ssm_c_im, v_ssm_d, v_ssm_log_dt, v_w_ssm_glu, v_w_ssm_o, v_w_mix_o, v_norm_ffn, v_w_ffn_in, v_w_ffn_out, v_norm_final):
    big = {"w": dict(w_in=w_in, w_attn_o=w_attn_o, w_conv_o=w_conv_o, w_ssm_glu=w_ssm_glu, w_ssm_o=w_ssm_o,
                     w_mix_o=w_mix_o, w_ffn_in=w_ffn_in, w_ffn_out=w_ffn_out, conv_w=conv_w),
           "m": dict(w_in=m_w_in, w_attn_o=m_w_attn_o, w_conv_o=m_w_conv_o, w_ssm_glu=m_w_ssm_glu,
                     w_ssm_o=m_w_ssm_o, w_mix_o=m_w_mix_o, w_ffn_in=m_w_ffn_in, w_ffn_out=m_w_ffn_out,
                     conv_w=m_conv_w),
           "v": dict(w_in=v_w_in, w_attn_o=v_w_attn_o, w_conv_o=v_w_conv_o, w_ssm_glu=v_w_ssm_glu,
                     w_ssm_o=v_w_ssm_o, w_mix_o=v_w_mix_o, w_ffn_in=v_w_ffn_in, w_ffn_out=v_w_ffn_out,
                     conv_w=v_conv_w)}
    small = {"w": dict(norm_mix=norm_mix, b_gate=b_gate, attn_sinks=attn_sinks, ssm_a_re=ssm_a_re,
                       ssm_a_im=ssm_a_im, ssm_b_re=ssm_b_re, ssm_b_im=ssm_b_im, ssm_c_re=ssm_c_re,
                       ssm_c_im=ssm_c_im, ssm_d=ssm_d, ssm_log_dt=ssm_log_dt, norm_ffn=norm_ffn),
             "m": dict(norm_mix=m_norm_mix, b_gate=m_b_gate, attn_sinks=m_attn_sinks, ssm_a_re=m_ssm_a_re,
                       ssm_a_im=m_ssm_a_im, ssm_b_re=m_ssm_b_re, ssm_b_im=m_ssm_b_im, ssm_c_re=m_ssm_c_re,
                       ssm_c_im=m_ssm_c_im, ssm_d=m_ssm_d, ssm_log_dt=m_ssm_log_dt, norm_ffn=m_norm_ffn),
             "v": dict(norm_mix=v_norm_mix, b_gate=v_b_gate, attn_sinks=v_attn_sinks, ssm_a_re=v_ssm_a_re,
                       ssm_a_im=v_ssm_a_im, ssm_b_re=v_ssm_b_re, ssm_b_im=v_ssm_b_im, ssm_c_re=v_ssm_c_re,
                       ssm_c_im=v_ssm_c_im, ssm_d=v_ssm_d, ssm_log_dt=v_ssm_log_dt, norm_ffn=v_norm_ffn)}
    finals = {"w": norm_final, "m": m_norm_final, "v": v_norm_final}

    packed = {s: [_pack_shards({n: a[i] for n, a in big[s].items()}) for i in range(DEPTH)] for s in "wmv"}
    weights = [_full_weights(_all_gather(packed["w"][i].astype(BF16), "gather_weights")) for i in range(DEPTH)]

    tabs = _rope_tabs()
    sm, sp = [], []
    for i in range(DEPTH):
        sm.append({"norm_mix": norm_mix[i][None], "b_gate": b_gate[i][None], "attn_sinks": attn_sinks[i][None],
                   "norm_ffn": norm_ffn[i][None]})
        sp.append({"a_re": ssm_a_re[i], "a_im": ssm_a_im[i], "log_dt": ssm_log_dt[i][:, None],
                   "bt_re": ssm_b_re[i].transpose(0, 2, 1), "bt_im": ssm_b_im[i].transpose(0, 2, 1),
                   "c_re": ssm_c_re[i], "c_im": ssm_c_im[i], "d": ssm_d[i][None]})

    act = x[0]
    kept = []
    for i in range(DEPTH):
        act, k = _layer_fwd(act, weights[i], sm[i], sp[i], tabs)
        kept.append(k)
    loss_row, dx, d_norm_final = _loss_head(act, norm_final[None], loss_target[0])
    loss = lax.psum(loss_row[0, 0], ("x", "y", "c"))

    small_grads = [None] * DEPTH
    big_out = [None] * DEPTH
    for i in reversed(range(DEPTH)):
        dx, g, small_grads[i] = _layer_bwd(dx, kept[i], weights[i], sm[i], sp[i], tabs)
        parts = _scatter_parts(_grad_parts(g), "scatter_grads")
        big_out[i] = _adamw(parts, packed["w"][i], packed["m"][i], packed["v"][i], 240, "adamw_sharded")

    shapes = {n: a.shape for n, a in small["w"].items()}
    sparts = _all_gather(_pack_small(small_grads, d_norm_final), "gather_small_grads")
    sw, sm_, sv = (_pack_small([{n: small[s][n][i] for n, _ in SMALL} for i in range(DEPTH)], finals[s])
                   for s in "wmv")
    small_out = [_unpack_small(p, shapes) for p in _adamw(sparts, sw, sm_, sv, SMALL_ROWS // 8, "adamw_replicated")]

    order = ["norm_mix", "w_in", "b_gate", "attn_sinks", "w_attn_o", "conv_w", "w_conv_o", "ssm_a_re", "ssm_a_im",
             "ssm_b_re", "ssm_b_im", "ssm_c_re", "ssm_c_im", "ssm_d", "ssm_log_dt", "w_ssm_glu", "w_ssm_o",
             "w_mix_o", "norm_ffn", "w_ffn_in", "w_ffn_out", "norm_final"]
    outs = [loss, dx[None]]
    for kind in range(4):
        unpacked = [_unpack_shards(big_out[i][kind]) for i in range(DEPTH)]
        for name in order:
            if name == "norm_final":
                outs.append(small_out[kind][1])
            elif name in small_out[kind][0]:
                outs.append(small_out[kind][0][name])
            else:
                outs.append(jnp.stack([unpacked[i][name] for i in range(DEPTH)]))
    return tuple(outs)
```

```python
import functools
import math

import jax
import jax.numpy as jnp
from jax import lax
from jax.experimental import pallas as pl
from jax.experimental.pallas import tpu as pltpu

F32 = jnp.float32
BF16 = jnp.bfloat16

N_DEV = 8
DEPTH = 4
SEQ = 2048
D_MODEL = 1024
N_Q_HEADS = 8
HEAD_DIM = 64
ATTN_W = 512
KV_W = 128
BLOCK = 128
N_BLOCKS = SEQ // BLOCK
ROPE_THETA = 500000.0
ROT_DIM = 16
NEG_INF = -1e30
WIDTH = 512
SSM_GROUPS = 32
SSM_GROUP = 16
SSM_STATE = 64
SLABS = 16
CHUNK = 256
N_CHUNKS = SEQ // CHUNK
GATE_W = 3 * D_MODEL
IN_COLS = 5888
FFN_H = 2816
NORM_EPS = 1e-6
LR, B1, B2, ADAM_EPS, WD, STEP = 0.001, 0.9, 0.999, 1e-08, 0.01, 10

COL_Q, COL_KV, COL_CBX, COL_U, COL_G = 0, 512, 768, 2304, 2816
PIECE_W = (512, 256, 512, 512, 512, 512, 3072)
PIECE_OFF = tuple(sum(PIECE_W[:i]) for i in range(len(PIECE_W)))

KINDS = (("win_t", 736, 1024), ("wffn_t", 704, 1024), ("wout", 352, 1024), ("wmix", 128, 1024),
         ("branch_t", 128, 1536), ("wglu", 64, 512))

SMALL = (("norm_mix", 1024), ("b_gate", 3072), ("attn_sinks", 8), ("ssm_a_re", 2048), ("ssm_a_im", 2048),
         ("ssm_b_re", 32768), ("ssm_b_im", 32768), ("ssm_c_re", 32768), ("ssm_c_im", 32768),
         ("ssm_d", 512), ("ssm_log_dt", 32), ("norm_ffn", 1024))
SMALL_PER_LAYER = sum(n for _, n in SMALL)
CONV_N = DEPTH * 3 * WIDTH
SMALL_ROWS = 4480

VMEM_LIMIT = 56 * 1024 * 1024
NT = (((1,), (1,)), ((), ()))
TN = (((0,), (0,)), ((), ()))


def _cp(**kw):
    return pltpu.CompilerParams(vmem_limit_bytes=VMEM_LIMIT, **kw)


def _full(shape):
    return pl.BlockSpec(shape, lambda *_: (0,) * len(shape))


def _mm(a, b, *, ta=False, tb=False, tm, tn, tk, out_dtype=F32, name):
    m = a.shape[1] if ta else a.shape[0]
    k = a.shape[0] if ta else a.shape[1]
    n = b.shape[0] if tb else b.shape[1]
    nk = k // tk
    dims = (((0 if ta else 1,), (1 if tb else 0,)), ((), ()))

    def body(a_ref, b_ref, o_ref, acc_ref):
        kk = pl.program_id(2)

        @pl.when(kk == 0)
        def _():
            acc_ref[...] = jnp.zeros_like(acc_ref)

        acc_ref[...] += lax.dot_general(a_ref[...].astype(BF16), b_ref[...].astype(BF16), dims,
                                        preferred_element_type=F32)

        @pl.when(kk == nk - 1)
        def _():
            o_ref[...] = acc_ref[...].astype(out_dtype)

    a_spec = pl.BlockSpec((tk, tm), lambda i, j, kk: (kk, i)) if ta else pl.BlockSpec((tm, tk), lambda i, j, kk: (i, kk))
    b_spec = pl.BlockSpec((tn, tk), lambda i, j, kk: (j, kk)) if tb else pl.BlockSpec((tk, tn), lambda i, j, kk: (kk, j))
    return pl.pallas_call(
        body, grid=(m // tm, n // tn, nk), in_specs=[a_spec, b_spec],
        out_specs=pl.BlockSpec((tm, tn), lambda i, j, kk: (i, j)),
        out_shape=jax.ShapeDtypeStruct((m, n), out_dtype),
        scratch_shapes=[pltpu.VMEM((tm, tn), F32)], compiler_params=_cp(), name=name)(a, b)


def _rms_rows(xv, g):
    r = lax.rsqrt(jnp.mean(xv * xv, axis=-1, keepdims=True) + NORM_EPS)
    return ((xv * r) * g).astype(BF16)


def _rms_mm_in(x, g, wt):
    tt = 256
    widths = (ATTN_W, 2 * KV_W, 3 * WIDTH, WIDTH, GATE_W)
    offs = (COL_Q, COL_KV, COL_CBX, COL_U, COL_G)

    def body(x_ref, g_ref, w_ref, q_ref, kv_ref, cbx_ref, u_ref, u16_ref, gl_ref, h_ref):
        h = _rms_rows(x_ref[...], g_ref[...])
        h_ref[...] = h
        prod = lax.dot_general(h, w_ref[...], NT, preferred_element_type=F32)
        for ref, o, w in zip((q_ref, kv_ref, cbx_ref, u_ref, gl_ref), offs, widths):
            ref[...] = prod[:, o:o + w]
        u16_ref[...] = prod[:, COL_U:COL_U + WIDTH].astype(BF16)

    row = lambda w: pl.BlockSpec((tt, w), lambda i: (i, 0))
    sds = jax.ShapeDtypeStruct
    return pl.pallas_call(
        body, grid=(SEQ // tt,), in_specs=[row(D_MODEL), _full((1, D_MODEL)), _full((IN_COLS, D_MODEL))],
        out_specs=[row(ATTN_W), row(2 * KV_W), row(3 * WIDTH), row(WIDTH), row(WIDTH), row(GATE_W), row(D_MODEL)],
        out_shape=[sds((SEQ, ATTN_W), F32), sds((SEQ, 2 * KV_W), F32), sds((SEQ, 3 * WIDTH), F32),
                   sds((SEQ, WIDTH), F32), sds((SEQ, WIDTH), BF16), sds((SEQ, GATE_W), F32),
                   sds((SEQ, D_MODEL), BF16)],
        compiler_params=_cp(), name="rms_mm_in")(x, g, wt)


def _rms_mm_ffn(x, g, wt):
    tt = 256

    def body(x_ref, g_ref, w_ref, o_ref, h_ref):
        h = _rms_rows(x_ref[...], g_ref[...])
        h_ref[...] = h
        o_ref[...] = lax.dot_general(h, w_ref[...], NT, preferred_element_type=F32)

    row = lambda w: pl.BlockSpec((tt, w), lambda i: (i, 0))
    return pl.pallas_call(
        body, grid=(SEQ // tt,), in_specs=[row(D_MODEL), _full((1, D_MODEL)), _full((2 * FFN_H, D_MODEL))],
        out_specs=[row(2 * FFN_H), row(D_MODEL)],
        out_shape=[jax.ShapeDtypeStruct((SEQ, 2 * FFN_H), F32), jax.ShapeDtypeStruct((SEQ, D_MODEL), BF16)],
        compiler_params=_cp(), name="rms_mm_ffn")(x, g, wt)


def _mm_rmsbwd(pieces, wt, x, g, dres, name):
    tt = 256
    widths = [p.shape[1] for p in pieces]
    offs = [sum(widths[:i]) for i in range(len(widths))]
    n = len(pieces)

    def body(*refs):
        p_refs, (w_ref, x_ref, g_ref, r_ref, dx_ref, dg_ref) = refs[:n], refs[n:]

        @pl.when(pl.program_id(0) == 0)
        def _():
            dg_ref[...] = jnp.zeros_like(dg_ref)

        dh = jnp.zeros((tt, D_MODEL), F32)
        for p_ref, o, w in zip(p_refs, offs, widths):
            dh += jnp.dot(p_ref[...], w_ref[o:o + w, :], preferred_element_type=F32)
        xv = x_ref[...]
        r = lax.rsqrt(jnp.mean(xv * xv, axis=-1, keepdims=True) + NORM_EPS)
        xh = xv * r
        gy = dh * g_ref[...]
        dx_ref[...] = r_ref[...] + r * (gy - xh * jnp.mean(gy * xh, axis=-1, keepdims=True))
        dg_ref[...] += jnp.sum(dh * xh, axis=0, keepdims=True)

    row = lambda w: pl.BlockSpec((tt, w), lambda i: (i, 0))
    return pl.pallas_call(
        body, grid=(SEQ // tt,),
        in_specs=[row(w) for w in widths] + [_full(wt.shape), row(D_MODEL), _full((1, D_MODEL)), row(D_MODEL)],
        out_specs=[row(D_MODEL), _full((1, D_MODEL))],
        out_shape=[jax.ShapeDtypeStruct((SEQ, D_MODEL), F32), jax.ShapeDtypeStruct((1, D_MODEL), F32)],
        compiler_params=_cp(), name=name)(*pieces, wt, x, g, dres)


def _tn_pieces(pieces, h):
    tk, tn = 512, 512
    nk = SEQ // tk
    n = len(pieces)

    def body(*refs):
        p_refs, (h_ref, o_ref, acc_ref) = refs[:n], refs[n:]
        kk = pl.program_id(1)

        @pl.when(kk == 0)
        def _():
            acc_ref[...] = jnp.zeros_like(acc_ref)

        hv = h_ref[...]
        for p_ref, o, w in zip(p_refs, PIECE_OFF, PIECE_W):
            acc_ref[o:o + w, :] += lax.dot_general(p_ref[...], hv, TN, preferred_element_type=F32)

        @pl.when(kk == nk - 1)
        def _():
            o_ref[...] = acc_ref[...].astype(BF16)

    return pl.pallas_call(
        body, grid=(D_MODEL // tn, nk),
        in_specs=[pl.BlockSpec((tk, w), lambda j, kk: (kk, 0)) for w in PIECE_W]
        + [pl.BlockSpec((tk, tn), lambda j, kk: (kk, j))],
        out_specs=pl.BlockSpec((IN_COLS, tn), lambda j, kk: (0, j)),
        out_shape=jax.ShapeDtypeStruct((IN_COLS, D_MODEL), BF16),
        scratch_shapes=[pltpu.VMEM((IN_COLS, tn), F32)], compiler_params=_cp(), name="tn_pieces")(*pieces, h)


def _tn_branches(dys, acts):
    tk = 512
    nk = SEQ // tk

    def body(d0, d1, d2, a0, a1, a2, o_ref, acc_ref):
        kk = pl.program_id(0)

        @pl.when(kk == 0)
        def _():
            acc_ref[...] = jnp.zeros_like(acc_ref)

        for j, (d, a) in enumerate(((d0, a0), (d1, a1), (d2, a2))):
            acc_ref[:, WIDTH * j:WIDTH * (j + 1)] += lax.dot_general(d[...], a[...], TN, preferred_element_type=F32)

        @pl.when(kk == nk - 1)
        def _():
            o_ref[...] = acc_ref[...].astype(BF16)

    row = lambda w: pl.BlockSpec((tk, w), lambda kk: (kk, 0))
    return pl.pallas_call(
        body, grid=(nk,), in_specs=[row(D_MODEL)] * 3 + [row(WIDTH)] * 3,
        out_specs=_full((D_MODEL, 3 * WIDTH)), out_shape=jax.ShapeDtypeStruct((D_MODEL, 3 * WIDTH), BF16),
        scratch_shapes=[pltpu.VMEM((D_MODEL, 3 * WIDTH), F32)], compiler_params=_cp(), name="tn_branches",
    )(*dys, *acts)


def _rope(t, c, a, b):
    return t * c + pltpu.roll(t, 120, axis=1) * a + pltpu.roll(t, 8, axis=1) * b


def _rope_t(d, c, a, b):
    return d * c + pltpu.roll(d * a, 8, axis=1) + pltpu.roll(d * b, 120, axis=1)


def _band_sides(band):
    left = lax.broadcasted_iota(jnp.int32, band.shape, 1) < HEAD_DIM
    h0 = jnp.where(left, band, 0.0)
    h1 = jnp.where(left, 0.0, band)
    r0 = pltpu.roll(h0, HEAD_DIM, axis=1)
    r1 = pltpu.roll(h1, HEAD_DIM, axis=1)
    return ((h0.astype(BF16), r0.astype(BF16)), (r1.astype(BF16), h1.astype(BF16)))


def _attn_mask(i):
    qi = lax.broadcasted_iota(jnp.int32, (BLOCK, 2 * BLOCK), 0)
    kj = lax.broadcasted_iota(jnp.int32, (BLOCK, 2 * BLOCK), 1)
    delta = qi + BLOCK - kj
    return (delta >= 0) & (delta < BLOCK) & ((kj >= BLOCK) | (i > 0))


def _attn_probs(qc, kside, ok, sink):
    s = lax.dot_general(qc, kside, NT, preferred_element_type=F32) * (HEAD_DIM ** -0.5)
    s = jnp.where(ok, s, NEG_INF)
    m = jnp.maximum(jnp.max(s, axis=-1, keepdims=True), sink)
    p = jnp.exp(s - m)
    es = jnp.exp(sink - m)
    inv = 1.0 / (jnp.sum(p, axis=-1, keepdims=True) + es)
    return p * inv, es * inv


def _attn_load(q_ref, kvc_ref, kvp_ref, tc_ref, ta_ref, tb_ref, pc_ref, pa_ref, pb_ref):
    c, a, b = tc_ref[...], ta_ref[...], tb_ref[...]
    kc = _rope(kvc_ref[:, :KV_W], c, a, b)
    kp = _rope(kvp_ref[:, :KV_W], pc_ref[...], pa_ref[...], pb_ref[...])
    kband = jnp.concatenate([kp, kc], axis=0)
    vband = jnp.concatenate([kvp_ref[:, KV_W:], kvc_ref[:, KV_W:]], axis=0)
    qs = [_rope(q_ref[:, 128 * j:128 * (j + 1)], c, a, b).astype(BF16) for j in range(4)]
    return qs, _band_sides(kband), _band_sides(vband), (c, a, b)


def _attn_specs(clamp):
    cur = lambda i: (clamp(i), 0)
    prev = lambda i: (jnp.maximum(clamp(i) - 1, 0), 0)
    return [
        pl.BlockSpec((BLOCK, ATTN_W), cur), pl.BlockSpec((BLOCK, 2 * KV_W), cur),
        pl.BlockSpec((BLOCK, 2 * KV_W), prev),
        pl.BlockSpec((BLOCK, 128), cur), pl.BlockSpec((BLOCK, 128), cur), pl.BlockSpec((BLOCK, 128), cur),
        pl.BlockSpec((BLOCK, 128), prev), pl.BlockSpec((BLOCK, 128), prev), pl.BlockSpec((BLOCK, 128), prev),
        pl.BlockSpec(memory_space=pltpu.SMEM),
    ]


def _attn_fwd(q, kv, tabs, sinks):
    tc, ta, tb = tabs

    def body(q_ref, kvc_ref, kvp_ref, tc_ref, ta_ref, tb_ref, pc_ref, pa_ref, pb_ref, sink_ref, o_ref):
        i = pl.program_id(0)
        qs, ks, vs, _ = _attn_load(q_ref, kvc_ref, kvp_ref, tc_ref, ta_ref, tb_ref, pc_ref, pa_ref, pb_ref)
        ok = _attn_mask(i)
        for j in range(4):
            kh = j // 2
            acc = jnp.zeros((BLOCK, 128), F32)
            for side in range(2):
                pn, _ = _attn_probs(qs[j], ks[kh][side], ok, sink_ref[0, 2 * j + side])
                acc += jnp.dot(pn.astype(BF16), vs[kh][side], preferred_element_type=F32)
            o_ref[:, 128 * j:128 * (j + 1)] = acc.astype(BF16)

    return pl.pallas_call(
        body, grid=(N_BLOCKS,), in_specs=_attn_specs(lambda i: i),
        out_specs=pl.BlockSpec((BLOCK, ATTN_W), lambda i: (i, 0)),
        out_shape=jax.ShapeDtypeStruct((SEQ, ATTN_W), BF16), compiler_params=_cp(), name="attn_fwd",
    )(q, kv, kv, tc, ta, tb, tc, ta, tb, sinks)


def _attn_bwd(q, kv, tabs, sinks, do):
    tc, ta, tb = tabs
    last = N_BLOCKS - 1
    clamp = lambda i: jnp.minimum(i, last)

    def place(full, side, kh):
        left = lax.broadcasted_iota(jnp.int32, full.shape, 1) < HEAD_DIM
        valid = jnp.where(left, full, 0.0) if side == 0 else jnp.where(left, 0.0, full)
        return valid if side == kh else pltpu.roll(valid, HEAD_DIM, axis=1)

    def body(q_ref, kvc_ref, kvp_ref, tc_ref, ta_ref, tb_ref, pc_ref, pa_ref, pb_ref, sink_ref, do_ref,
             dq_ref, dkv_ref, ds_ref, carry_ref):
        i = pl.program_id(0)

        @pl.when(i == 0)
        def _():
            ds_ref[...] = jnp.zeros_like(ds_ref)
            carry_ref[...] = jnp.zeros_like(carry_ref)

        @pl.when(i > last)
        def _():
            dkv_ref[...] = carry_ref[...].astype(BF16)

        @pl.when(i <= last)
        def _():
            qs, ks, vs, (c, a, b) = _attn_load(q_ref, kvc_ref, kvp_ref, tc_ref, ta_ref, tb_ref,
                                               pc_ref, pa_ref, pb_ref)
            ok = _attn_mask(i)
            dk = jnp.zeros((2 * BLOCK, 128), F32)
            dv = jnp.zeros((2 * BLOCK, 128), F32)
            dsink = jnp.zeros((1, 128), F32)
            lane = lax.broadcasted_iota(jnp.int32, (1, 128), 1)
            for j in range(4):
                kh = j // 2
                doc = do_ref[:, 128 * j:128 * (j + 1)].astype(BF16)
                dq = jnp.zeros((BLOCK, 128), F32)
                for side in range(2):
                    pn, ps = _attn_probs(qs[j], ks[kh][side], ok, sink_ref[0, 2 * j + side])
                    dp = lax.dot_general(doc, vs[kh][side], NT, preferred_element_type=F32)
                    dr = jnp.sum(pn * dp, axis=-1, keepdims=True)
                    dsb = (pn * (dp - dr) * (HEAD_DIM ** -0.5)).astype(BF16)
                    dsink += jnp.where(lane == 2 * j + side, -jnp.sum(ps * dr), 0.0)
                    dq += jnp.dot(dsb, ks[kh][side], preferred_element_type=F32)
                    dk += place(lax.dot_general(dsb, qs[j], TN, preferred_element_type=F32), side, kh)
                    dv += place(lax.dot_general(pn.astype(BF16), doc, TN, preferred_element_type=F32), side, kh)
                dq_ref[:, 128 * j:128 * (j + 1)] = _rope_t(dq, c, a, b).astype(BF16)
            ds_ref[...] += dsink
            dk_prev = _rope_t(dk[:BLOCK], pc_ref[...], pa_ref[...], pb_ref[...])
            dk_cur = _rope_t(dk[BLOCK:], c, a, b)
            prev = jnp.concatenate([dk_prev, dv[:BLOCK]], axis=1)
            dkv_ref[...] = (carry_ref[...] + prev).astype(BF16)
            carry_ref[...] = jnp.concatenate([dk_cur, dv[BLOCK:]], axis=1)

    return pl.pallas_call(
        body, grid=(N_BLOCKS + 1,),
        in_specs=_attn_specs(clamp) + [pl.BlockSpec((BLOCK, ATTN_W), lambda i: (clamp(i), 0))],
        out_specs=[pl.BlockSpec((BLOCK, ATTN_W), lambda i: (clamp(i), 0)),
                   pl.BlockSpec((BLOCK, 2 * KV_W), lambda i: (jnp.maximum(i - 1, 0), 0)),
                   pl.BlockSpec((1, 128), lambda i: (0, 0))],
        out_shape=[jax.ShapeDtypeStruct((SEQ, ATTN_W), BF16), jax.ShapeDtypeStruct((SEQ, 2 * KV_W), BF16),
                   jax.ShapeDtypeStruct((1, 128), F32)],
        scratch_shapes=[pltpu.VMEM((BLOCK, 2 * KV_W), F32)], compiler_params=_cp(), name="attn_bwd",
    )(q, kv, kv, tc, ta, tb, tc, ta, tb, sinks, do)


def _shift_down(z, k):
    row = lax.broadcasted_iota(jnp.int32, z.shape, 0)
    return jnp.where(row < k, 0.0, pltpu.roll(z, k, axis=0))


def _shift_up(z, k):
    n = z.shape[0]
    row = lax.broadcasted_iota(jnp.int32, z.shape, 0)
    return jnp.where(row >= n - k, 0.0, pltpu.roll(z, n - k, axis=0))


def _conv_specs():
    nb = WIDTH // 128
    return [pl.BlockSpec((SEQ, 128), lambda j: (0, j)), pl.BlockSpec((SEQ, 128), lambda j: (0, nb + j)),
            pl.BlockSpec((SEQ, 128), lambda j: (0, 2 * nb + j)), pl.BlockSpec((None, 8, 128), lambda j: (0, 0, j))]


def _conv_fwd(cbx, cw, layer):
    def body(cb_ref, cc_ref, cx_ref, w_ref, o_ref):
        z = cc_ref[...] * cx_ref[...]
        s = w_ref[0:1, :] * _shift_down(z, 2) + w_ref[1:2, :] * _shift_down(z, 1) + w_ref[2:3, :] * z
        o_ref[...] = (cb_ref[...] * s).astype(BF16)

    specs = _conv_specs()
    specs[3] = pl.BlockSpec((None, 8, 128), lambda j: (layer, 0, j))
    return pl.pallas_call(
        body, grid=(WIDTH // 128,), in_specs=specs,
        out_specs=pl.BlockSpec((SEQ, 128), lambda j: (0, j)),
        out_shape=jax.ShapeDtypeStruct((SEQ, WIDTH), BF16), compiler_params=_cp(), name="conv_fwd",
    )(cbx, cbx, cbx, cw)


def _conv_bwd(cbx, cw, layer, dout):
    def body(cb_ref, cc_ref, cx_ref, w_ref, do_ref, dcb_ref, dcc_ref, dcx_ref, dw_ref):
        cc, cx = cc_ref[...], cx_ref[...]
        z = cc * cx
        z1, z2 = _shift_down(z, 1), _shift_down(z, 2)
        w0, w1, w2 = w_ref[0:1, :], w_ref[1:2, :], w_ref[2:3, :]
        dout = do_ref[...]
        ds = dout * cb_ref[...]
        dcb_ref[...] = (dout * (w0 * z2 + w1 * z1 + w2 * z)).astype(BF16)
        dz = w2 * ds + w1 * _shift_up(ds, 1) + w0 * _shift_up(ds, 2)
        dcc_ref[...] = (dz * cx).astype(BF16)
        dcx_ref[...] = (dz * cc).astype(BF16)
        rows = [jnp.sum(ds * zz, axis=0, keepdims=True) for zz in (z2, z1, z)]
        dw_ref[...] = jnp.concatenate(rows + [jnp.zeros((5, 128), F32)], axis=0)

    col = lambda j: (0, j)
    specs = _conv_specs()
    specs[3] = pl.BlockSpec((None, 8, 128), lambda j: (layer, 0, j))
    return pl.pallas_call(
        body, grid=(WIDTH // 128,), in_specs=specs + [pl.BlockSpec((SEQ, 128), col)],
        out_specs=[pl.BlockSpec((SEQ, 128), col), pl.BlockSpec((SEQ, 128), col), pl.BlockSpec((SEQ, 128), col),
                   pl.BlockSpec((8, 128), col)],
        out_shape=[jax.ShapeDtypeStruct((SEQ, WIDTH), BF16)] * 3 + [jax.ShapeDtypeStruct((8, WIDTH), F32)],
        compiler_params=_cp(), name="conv_bwd",
    )(cbx, cbx, cbx, cw, dout)


def _ssm_prep_math(a_re, a_im, log_dt, bt_re, bt_im):
    dt = jnp.exp(log_dt)
    er = jnp.exp(a_re * dt)
    lr = er * jnp.cos(a_im * dt)
    li = er * jnp.sin(a_im * dt)
    n2 = a_re * a_re + a_im * a_im
    cr = ((lr - 1.0) * a_re + li * a_im) / n2
    ci = (li * a_re - (lr - 1.0) * a_im) / n2
    cr3, ci3 = cr[:, None, :], ci[:, None, :]
    return lr, li, cr3 * bt_re - ci3 * bt_im, cr3 * bt_im + ci3 * bt_re


_GS = (SSM_GROUPS, SSM_STATE)
_GHS = (SSM_GROUPS, SSM_GROUP, SSM_STATE)


def _layered(shape):
    return pl.BlockSpec((None,) + shape, lambda l: (l,) + (0,) * len(shape))


def _ssm_prep(a_re, a_im, log_dt, bt_re, bt_im):
    def body(ar, ai, ld, br, bi, o0, o1, o2, o3):
        outs = _ssm_prep_math(ar[...], ai[...], ld[...], br[...], bi[...])
        for o, v in zip((o0, o1, o2, o3), outs):
            o[...] = v

    shapes = [_GS, _GS, _GHS, _GHS]
    return pl.pallas_call(
        body, grid=(DEPTH,), in_specs=[_layered(s) for s in (_GS, _GS, (SSM_GROUPS, 1), _GHS, _GHS)],
        out_specs=[_layered(s) for s in shapes],
        out_shape=[jax.ShapeDtypeStruct((DEPTH,) + s, F32) for s in shapes],
        name="ssm_prep")(a_re, a_im, log_dt, bt_re, bt_im)


def _ssm_prep_bwd(a_re, a_im, log_dt, bt_re, bt_im, cots):
    def body(ar, ai, ld, br, bi, c0, c1, c2, c3, o0, o1, o2, o3, o4):
        _, vjp = jax.vjp(_ssm_prep_math, ar[...], ai[...], ld[...], br[...], bi[...])
        for o, v in zip((o0, o1, o2, o3, o4), vjp((c0[...], c1[...], c2[...], c3[...]))):
            o[...] = v

    ins = (_GS, _GS, (SSM_GROUPS, 1), _GHS, _GHS)
    return pl.pallas_call(
        body, grid=(DEPTH,), in_specs=[_layered(s) for s in ins + (_GS, _GS, _GHS, _GHS)],
        out_specs=[_layered(s) for s in ins],
        out_shape=[jax.ShapeDtypeStruct((DEPTH,) + s, F32) for s in ins],
        name="ssm_prep_bwd")(a_re, a_im, log_dt, bt_re, bt_im, *cots)


def _slab_mm(u, w_re, w_im, layer, name):
    def body(u_ref, wr_ref, wi_ref, or_ref, oi_ref):
        uv = u_ref[...]
        or_ref[...] = jnp.dot(uv, wr_ref[...], preferred_element_type=F32)
        oi_ref[...] = jnp.dot(uv, wi_ref[...], preferred_element_type=F32)

    slab = pl.BlockSpec((None, SEQ, 128), lambda k: (k, 0, 0))
    wcol = pl.BlockSpec((None, WIDTH, 128), lambda k: (layer, 0, k))
    return pl.pallas_call(
        body, grid=(SLABS,), in_specs=[_full((SEQ, WIDTH)), wcol, wcol],
        out_specs=[slab, slab], out_shape=[jax.ShapeDtypeStruct((SLABS, SEQ, 128), F32)] * 2,
        compiler_params=_cp(), name=name)(u, w_re, w_im)


def _scan(b_re, b_im, a_re, a_im, layer, reverse, name):
    def body(br_ref, bi_ref, ar_ref, ai_ref, xr_ref, xi_ref, pr_ref, pi_ref):
        ar = jnp.broadcast_to(ar_ref[...], (N_CHUNKS, 128))
        ai = jnp.broadcast_to(ai_ref[...], (N_CHUNKS, 128))
        if reverse:
            ai = -ai

        def rows(tau):
            t = (CHUNK - 1 - tau) if reverse else tau
            return pl.ds(t, N_CHUNKS, stride=CHUNK)

        def first(tau, carry):
            sr, si, pr, pi = carry
            sr, si = ar * sr - ai * si + br_ref[rows(tau), :], ar * si + ai * sr + bi_ref[rows(tau), :]
            pr, pi = ar * pr - ai * pi, ar * pi + ai * pr
            xr_ref[rows(tau), :] = sr
            xi_ref[rows(tau), :] = si
            at = pl.ds(pl.multiple_of(tau * N_CHUNKS, N_CHUNKS), N_CHUNKS)
            pr_ref[at, :] = pr
            pi_ref[at, :] = pi
            return sr, si, pr, pi

        zero = jnp.zeros((N_CHUNKS, 128), F32)
        er, ei, qr, qi = lax.fori_loop(0, CHUNK, first, (zero, zero, zero + 1.0, zero), unroll=8)

        shift = _shift_up if reverse else _shift_down
        for k in (1, 2, 4):
            sr, si = shift(er, k), shift(ei, k)
            er, ei = er + qr * sr - qi * si, ei + qr * si + qi * sr
            qr, qi = qr * qr - qi * qi, 2.0 * qr * qi
        cr, ci = shift(er, 1), shift(ei, 1)

        def second(tau, _):
            at = pl.ds(pl.multiple_of(tau * N_CHUNKS, N_CHUNKS), N_CHUNKS)
            pr, pi = pr_ref[at, :], pi_ref[at, :]
            xr_ref[rows(tau), :] += pr * cr - pi * ci
            xi_ref[rows(tau), :] += pr * ci + pi * cr
            return 0

        lax.fori_loop(0, CHUNK, second, 0, unroll=8)

    slab = pl.BlockSpec((None, SEQ, 128), lambda k: (k, 0, 0))
    vec = pl.BlockSpec((None, None, 1, 128), lambda k: (layer, k, 0, 0))
    return pl.pallas_call(
        body, grid=(SLABS,), in_specs=[slab, slab, vec, vec], out_specs=[slab, slab],
        out_shape=[jax.ShapeDtypeStruct((SLABS, SEQ, 128), F32)] * 2,
        scratch_shapes=[pltpu.VMEM((CHUNK * N_CHUNKS, 128), F32)] * 2, compiler_params=_cp(), name=name,
    )(b_re, b_im, a_re, a_im)


def _slab_contract(x_re, x_im, m_re, m_im, layer, e1, e2, name):
    tt = 256

    def body(xr_ref, xi_ref, mr_ref, mi_ref, e1_ref, e2_ref, o_ref):
        acc = e1_ref[...].astype(F32) * e2_ref[...]
        for k in range(SLABS):
            acc += jnp.dot(xr_ref[k].astype(BF16), mr_ref[k], preferred_element_type=F32)
            acc += jnp.dot(xi_ref[k].astype(BF16), mi_ref[k], preferred_element_type=F32)
        o_ref[...] = acc

    xs = pl.BlockSpec((SLABS, tt, 128), lambda i: (0, i, 0))
    ms = pl.BlockSpec((None, SLABS, 128, WIDTH), lambda i: (layer, 0, 0, 0))
    row = pl.BlockSpec((tt, WIDTH), lambda i: (i, 0))
    return pl.pallas_call(
        body, grid=(SEQ // tt,),
        in_specs=[xs, xs, ms, ms, row, pl.BlockSpec((None, 1, WIDTH), lambda i: (layer, 0, 0))],
        out_specs=row, out_shape=jax.ShapeDtypeStruct((SEQ, WIDTH), F32), compiler_params=_cp(), name=name,
    )(x_re, x_im, m_re, m_im, e1, e2)


def _slab_tn(s_re, s_im, v, name):
    def body(sr_ref, si_ref, v_ref, or_ref, oi_ref):
        vv = v_ref[...]
        or_ref[...] = lax.dot_general(sr_ref[...].astype(BF16), vv, TN, preferred_element_type=F32)
        oi_ref[...] = lax.dot_general(si_ref[...].astype(BF16), vv, TN, preferred_element_type=F32)

    slab = pl.BlockSpec((None, SEQ, 128), lambda k: (k, 0, 0))
    out = pl.BlockSpec((None, 128, WIDTH), lambda k: (k, 0, 0))
    return pl.pallas_call(
        body, grid=(SLABS,), in_specs=[slab, slab, _full((SEQ, WIDTH))], out_specs=[out, out],
        out_shape=[jax.ShapeDtypeStruct((SLABS, 128, WIDTH), F32)] * 2, compiler_params=_cp(), name=name,
    )(s_re, s_im, v)


def _state_grad(x_re, x_im, l_re, l_im):
    def body(xr_ref, xi_ref, lr_ref, li_ref, or_ref, oi_ref):
        xr, xi = _shift_down(xr_ref[...], 1), _shift_down(xi_ref[...], 1)
        lr, li = lr_ref[...], li_ref[...]
        or_ref[...] = jnp.sum(xr * lr + xi * li, axis=0, keepdims=True)
        oi_ref[...] = jnp.sum(xr * li - xi * lr, axis=0, keepdims=True)

    slab = pl.BlockSpec((None, SEQ, 128), lambda k: (k, 0, 0))
    vec = pl.BlockSpec((None, 1, 128), lambda k: (k, 0, 0))
    return pl.pallas_call(
        body, grid=(SLABS,), in_specs=[slab] * 4, out_specs=[vec, vec],
        out_shape=[jax.ShapeDtypeStruct((SLABS, 1, 128), F32)] * 2, compiler_params=_cp(), name="state_grad",
    )(x_re, x_im, l_re, l_im)


_GELU_C = math.sqrt(2.0 / math.pi)


def _gelu(y):
    return 0.5 * y * (1.0 + jnp.tanh(_GELU_C * (y + 0.044715 * (y * y * y))))


def _glu_fwd(y, wglu):
    tt = 512

    def body(y_ref, w_ref, z_ref):
        ys = _gelu(y_ref[...])
        a = jnp.dot(ys.astype(BF16), w_ref[...], preferred_element_type=F32)
        z_ref[...] = (ys * jax.nn.sigmoid(a)).astype(BF16)

    blk = pl.BlockSpec((tt, WIDTH), lambda i: (i, 0))
    return pl.pallas_call(body, grid=(SEQ // tt,), in_specs=[blk, _full((WIDTH, WIDTH))], out_specs=blk,
                          out_shape=jax.ShapeDtypeStruct((SEQ, WIDTH), BF16), compiler_params=_cp(),
                          name="glu_fwd")(y, wglu)


def _glu_bwd(y, wglu, dz, u):
    tt = 512

    def body(y_ref, w_ref, dz_ref, u_ref, dy_ref, ys_ref, da_ref, dd_ref):
        @pl.when(pl.program_id(0) == 0)
        def _():
            dd_ref[...] = jnp.zeros_like(dd_ref)

        yv = y_ref[...]
        t = jnp.tanh(_GELU_C * (yv + 0.044715 * (yv * yv * yv)))
        ys = 0.5 * yv * (1.0 + t)
        ysb = ys.astype(BF16)
        sg = jax.nn.sigmoid(jnp.dot(ysb, w_ref[...], preferred_element_type=F32))
        dz = dz_ref[...].astype(F32)
        da = (dz * ys * sg * (1.0 - sg)).astype(BF16)
        dys = dz * sg + lax.dot_general(da, w_ref[...], NT, preferred_element_type=F32)
        dy = dys * (0.5 * (1.0 + t) + 0.5 * yv * (1.0 - t * t) * _GELU_C * (1.0 + 3 * 0.044715 * (yv * yv)))
        dy_ref[...] = dy.astype(BF16)
        ys_ref[...] = ysb
        da_ref[...] = da
        dd_ref[...] += jnp.sum(dy * u_ref[...], axis=0, keepdims=True)

    blk = pl.BlockSpec((tt, WIDTH), lambda i: (i, 0))
    return pl.pallas_call(
        body, grid=(SEQ // tt,), in_specs=[blk, _full((WIDTH, WIDTH)), blk, blk],
        out_specs=[blk, blk, blk, _full((1, WIDTH))],
        out_shape=[jax.ShapeDtypeStruct((SEQ, WIDTH), BF16)] * 3 + [jax.ShapeDtypeStruct((1, WIDTH), F32)],
        compiler_params=_cp(), name="glu_bwd")(y, wglu, dz, u)


def _mix_specs(tt, layer):
    row = lambda w: pl.BlockSpec((tt, w), lambda i: (i, 0))
    gate = lambda j: pl.BlockSpec((tt, D_MODEL), lambda i: (i, j))
    wo = lambda j: pl.BlockSpec((D_MODEL, WIDTH), lambda i: (0, j))
    return [row(D_MODEL), row(WIDTH), row(WIDTH), row(WIDTH), gate(0), gate(1), gate(2),
            pl.BlockSpec((None, 1, GATE_W), lambda i: (layer, 0, 0)), wo(0), wo(1), wo(2),
            _full((D_MODEL, D_MODEL))]


def _mix_branches(o_ref, c_ref, z_ref, g_refs, b_ref, wa_ref, wc_ref, ws_ref):
    ys = [lax.dot_general(r[...], w[...], NT, preferred_element_type=F32)
          for r, w in ((o_ref, wa_ref), (c_ref, wc_ref), (z_ref, ws_ref))]
    gates = [jax.nn.sigmoid(g_refs[j][...] + b_ref[:, D_MODEL * j:D_MODEL * (j + 1)]) for j in range(3)]
    return ys, gates


def _mix_fwd(x, o, cv, z, glog, b_gate, layer, wbt, wmix):
    tt = 256

    def body(x_ref, o_ref, c_ref, z_ref, g0, g1, g2, b_ref, wa_ref, wc_ref, ws_ref, wm_ref, x1_ref):
        ys, gates = _mix_branches(o_ref, c_ref, z_ref, (g0, g1, g2), b_ref, wa_ref, wc_ref, ws_ref)
        merged = gates[0] * ys[0] + gates[1] * ys[1] + gates[2] * ys[2]
        x1_ref[...] = x_ref[...] + jnp.dot(merged.astype(BF16), wm_ref[...], preferred_element_type=F32)

    return pl.pallas_call(
        body, grid=(SEQ // tt,), in_specs=_mix_specs(tt, layer),
        out_specs=pl.BlockSpec((tt, D_MODEL), lambda i: (i, 0)),
        out_shape=jax.ShapeDtypeStruct((SEQ, D_MODEL), F32), compiler_params=_cp(), name="mix_fwd",
    )(x, o, cv, z, glog, glog, glog, b_gate, wbt, wbt, wbt, wmix)


def _mix_bwd(dx1, o, cv, z, glog, b_gate, layer, wbt, wmix):
    tt = 256

    def body(dx_ref, o_ref, c_ref, z_ref, g0, g1, g2, b_ref, wa_ref, wc_ref, ws_ref, wm_ref,
             mg_ref, dya_ref, dyc_ref, dys_ref, do_ref, dc_ref, dz_ref, dgl_ref, db_ref):
        @pl.when(pl.program_id(0) == 0)
        def _():
            db_ref[...] = jnp.zeros_like(db_ref)

        ys, gates = _mix_branches(o_ref, c_ref, z_ref, (g0, g1, g2), b_ref, wa_ref, wc_ref, ws_ref)
        mg_ref[...] = (gates[0] * ys[0] + gates[1] * ys[1] + gates[2] * ys[2]).astype(BF16)
        dm = lax.dot_general(dx_ref[...].astype(BF16), wm_ref[...], NT, preferred_element_type=F32)
        for j, (dy_ref, w_ref, d_ref) in enumerate(((dya_ref, wa_ref, do_ref), (dyc_ref, wc_ref, dc_ref),
                                                    (dys_ref, ws_ref, dz_ref))):
            dy = (dm * gates[j]).astype(BF16)
            dy_ref[...] = dy
            d_ref[...] = jnp.dot(dy, w_ref[...], preferred_element_type=F32)
            dgl = dm * ys[j] * gates[j] * (1.0 - gates[j])
            dgl_ref[:, D_MODEL * j:D_MODEL * (j + 1)] = dgl.astype(BF16)
            db_ref[:, D_MODEL * j:D_MODEL * (j + 1)] += jnp.sum(dgl, axis=0, keepdims=True)

    row = lambda w: pl.BlockSpec((tt, w), lambda i: (i, 0))
    sds = jax.ShapeDtypeStruct
    return pl.pallas_call(
        body, grid=(SEQ // tt,), in_specs=_mix_specs(tt, layer),
        out_specs=[row(D_MODEL)] * 4 + [row(WIDTH)] * 3 + [row(GATE_W), _full((1, GATE_W))],
        out_shape=[sds((SEQ, D_MODEL), BF16)] * 4 + [sds((SEQ, WIDTH), F32)] * 3
        + [sds((SEQ, GATE_W), BF16), sds((1, GATE_W), F32)],
        compiler_params=_cp(), name="mix_bwd",
    )(dx1, o, cv, z, glog, glog, glog, b_gate, wbt, wbt, wbt, wmix)


def _ffn_out_fwd(x1, gu, wout):
    tt = 256

    def body(x_ref, gt_ref, up_ref, w_ref, o_ref):
        gt = gt_ref[...]
        act = (gt * jax.nn.sigmoid(gt) * up_ref[...]).astype(BF16)
        o_ref[...] = x_ref[...] + jnp.dot(act, w_ref[...], preferred_element_type=F32)

    return pl.pallas_call(
        body, grid=(SEQ // tt,),
        in_specs=[pl.BlockSpec((tt, D_MODEL), lambda i: (i, 0)), pl.BlockSpec((tt, FFN_H), lambda i: (i, 0)),
                  pl.BlockSpec((tt, FFN_H), lambda i: (i, 1)), _full((FFN_H, D_MODEL))],
        out_specs=pl.BlockSpec((tt, D_MODEL), lambda i: (i, 0)),
        out_shape=jax.ShapeDtypeStruct((SEQ, D_MODEL), F32), compiler_params=_cp(), name="ffn_out_fwd",
    )(x1, gu, gu, wout)


def _ffn_out_bwd(dx2, gu, wout):
    tt = 256

    def body(dx_ref, gt_ref, up_ref, w_ref, dgu_ref, act_ref):
        gt, up = gt_ref[...], up_ref[...]
        sg = jax.nn.sigmoid(gt)
        silu = gt * sg
        act_ref[...] = (silu * up).astype(BF16)
        dact = lax.dot_general(dx_ref[...].astype(BF16), w_ref[...], NT, preferred_element_type=F32)
        dgu_ref[:, :FFN_H] = (dact * up * (sg * (1.0 + gt * (1.0 - sg)))).astype(BF16)
        dgu_ref[:, FFN_H:] = (dact * silu).astype(BF16)

    return pl.pallas_call(
        body, grid=(SEQ // tt,),
        in_specs=[pl.BlockSpec((tt, D_MODEL), lambda i: (i, 0)), pl.BlockSpec((tt, FFN_H), lambda i: (i, 0)),
                  pl.BlockSpec((tt, FFN_H), lambda i: (i, 1)), _full((FFN_H, D_MODEL))],
        out_specs=[pl.BlockSpec((tt, 2 * FFN_H), lambda i: (i, 0)), pl.BlockSpec((tt, FFN_H), lambda i: (i, 0))],
        out_shape=[jax.ShapeDtypeStruct((SEQ, 2 * FFN_H), BF16), jax.ShapeDtypeStruct((SEQ, FFN_H), BF16)],
        compiler_params=_cp(), name="ffn_out_bwd",
    )(dx2, gu, gu, wout)


def _loss_head(x, g, target):
    tt = 256

    def body(x_ref, g_ref, t_ref, loss_ref, dx_ref, dg_ref):
        @pl.when(pl.program_id(0) == 0)
        def _():
            loss_ref[...] = jnp.zeros_like(loss_ref)
            dg_ref[...] = jnp.zeros_like(dg_ref)

        xv = x_ref[...]
        r = lax.rsqrt(jnp.mean(xv * xv, axis=-1, keepdims=True) + NORM_EPS)
        xh = xv * r
        err = xh * g_ref[...] - t_ref[...]
        loss_ref[...] += 0.5 * jnp.sum(jnp.mean(err * err, axis=-1, keepdims=True))
        dy = err * (1.0 / D_MODEL)
        gy = dy * g_ref[...]
        dx_ref[...] = r * (gy - xh * jnp.mean(gy * xh, axis=-1, keepdims=True))
        dg_ref[...] += jnp.sum(dy * xh, axis=0, keepdims=True)

    row = pl.BlockSpec((tt, D_MODEL), lambda i: (i, 0))
    return pl.pallas_call(
        body, grid=(SEQ // tt,), in_specs=[row, _full((1, D_MODEL)), row],
        out_specs=[_full((1, 128)), row, _full((1, D_MODEL))],
        out_shape=[jax.ShapeDtypeStruct((1, 128), F32), jax.ShapeDtypeStruct((SEQ, D_MODEL), F32),
                   jax.ShapeDtypeStruct((1, D_MODEL), F32)],
        compiler_params=_cp(), name="loss_head")(x, g, target)


def _adamw(parts, w, m, v, tr, name):
    groups, rows, cols = w.shape

    def body(p_ref, w_ref, m_ref, v_ref, g_ref, d_ref, nm_ref, nv_ref):
        g = p_ref[0].astype(F32)
        for k in range(1, N_DEV):
            g = g + p_ref[k].astype(F32)
        nm = B1 * m_ref[...] + (1.0 - B1) * g
        nv = B2 * v_ref[...] + (1.0 - B2) * (g * g)
        m_hat = nm / (1.0 - B1 ** STEP)
        v_hat = nv / (1.0 - B2 ** STEP)
        g_ref[...] = g
        d_ref[...] = -LR * (m_hat / (jnp.sqrt(v_hat) + ADAM_EPS) + WD * w_ref[...])
        nm_ref[...] = nm
        nv_ref[...] = nv

    blk = pl.BlockSpec((None, tr, cols), lambda l, i: (l, i, 0))
    return pl.pallas_call(
        body, grid=(groups, rows // tr),
        in_specs=[pl.BlockSpec((None, N_DEV, tr, cols), lambda l, i: (l, 0, i, 0)), blk, blk, blk],
        out_specs=[blk] * 4, out_shape=[jax.ShapeDtypeStruct((groups, rows, cols), F32)] * 4,
        compiler_params=_cp(), name=name)(parts, w, m, v)


MESH_ID = pl.DeviceIdType.MESH
ANY = pl.BlockSpec(memory_space=pl.ANY)


def _gather_body(n, pick):
    def body(*refs):
        srcs, outs, (send_sems, recv_sems, local_sems) = refs[:n], refs[n:2 * n], refs[2 * n:]
        x, y, c = lax.axis_index("x"), lax.axis_index("y"), lax.axis_index("c")
        me, sibling = (x, y, c), (x, y, 1 - c)
        chips = [(1 - x, y), (x, 1 - y), (1 - x, 1 - y)]

        def copy(i, k, block, to, src=None):
            slot = outs[i].at[4 * block[0] + 2 * block[1] + block[2]]
            return pltpu.make_async_remote_copy(
                src_ref=slot if src is None else src, dst_ref=slot,
                send_sem=send_sems.at[7 * i + k], recv_sem=recv_sems.at[7 * i + k],
                device_id=to, device_id_type=MESH_ID)

        mine = [pltpu.make_async_copy(pick(srcs[i]), outs[i].at[4 * x + 2 * y + c], local_sems.at[i])
                for i in range(n)]
        for cp in mine:
            cp.start()
        first = []
        for i in range(n):
            first.append(copy(i, 0, me, sibling, src=pick(srcs[i])))
            first += [copy(i, 1 + j, me, (*chip, c), src=pick(srcs[i])) for j, chip in enumerate(chips)]
        for cp in first:
            cp.start()
        passed = []
        for j, chip in enumerate(chips):
            for i in range(n):
                copy(i, 1 + j, (*chip, c), me).wait_recv()
                passed.append(copy(i, 4 + j, (*chip, c), sibling))
                passed[-1].start()
        for i in range(n):
            copy(i, 0, sibling, me).wait_recv()
            for j, chip in enumerate(chips):
                copy(i, 4 + j, (*chip, 1 - c), me).wait_recv()
        for cp in first + passed:
            cp.wait_send()
        for cp in mine:
            cp.wait()

    return body


def _gather_scratch(n):
    return [pltpu.SemaphoreType.DMA((7 * n,)), pltpu.SemaphoreType.DMA((7 * n,)), pltpu.SemaphoreType.DMA((n,))]


def _gather_layer(stacked, layer):
    n = len(stacked)
    return pl.pallas_call(
        _gather_body(n, lambda ref: ref.at[layer]), in_specs=[ANY] * n, out_specs=[ANY] * n,
        out_shape=[jax.ShapeDtypeStruct((N_DEV,) + s.shape[1:], s.dtype) for s in stacked],
        scratch_shapes=_gather_scratch(n), name=f"gather_weights_{layer}")(*stacked)


def _all_gather(shard, name):
    return pl.pallas_call(
        _gather_body(1, lambda ref: ref), in_specs=[ANY], out_specs=[ANY],
        out_shape=[jax.ShapeDtypeStruct((N_DEV,) + shard.shape, shard.dtype)],
        scratch_shapes=_gather_scratch(1), name=name)(shard)[0]


def _scatter_layer(parts, landings, layer):
    n = len(parts)

    def body(*refs):
        srcs, outs, (send_sems, recv_sems, local_sems) = refs[:n], refs[2 * n:3 * n], refs[3 * n:]
        x, y, c = lax.axis_index("x"), lax.axis_index("y"), lax.axis_index("c")
        me = 4 * x + 2 * y + c
        mine = [pltpu.make_async_copy(srcs[i].at[me], outs[i].at[layer, me], local_sems.at[i]) for i in range(n)]
        for cp in mine:
            cp.start()
        copies = []
        for rel in range(1, N_DEV):
            px = 1 - x if rel & 4 else x
            py = 1 - y if rel & 2 else y
            pc = 1 - c if rel & 1 else c
            for i in range(n):
                copies.append(pltpu.make_async_remote_copy(
                    src_ref=srcs[i].at[4 * px + 2 * py + pc], dst_ref=outs[i].at[layer, me],
                    send_sem=send_sems.at[7 * i + rel - 1], recv_sem=recv_sems.at[7 * i + rel - 1],
                    device_id=(px, py, pc), device_id_type=MESH_ID))
        for cp in copies:
            cp.start()
        for cp in copies:
            cp.wait_recv()
        for cp in copies:
            cp.wait_send()
        for cp in mine:
            cp.wait()

    return pl.pallas_call(
        body, in_specs=[ANY] * (2 * n), out_specs=[ANY] * n,
        out_shape=[jax.ShapeDtypeStruct(a.shape, a.dtype) for a in landings],
        input_output_aliases={n + i: i for i in range(n)},
        scratch_shapes=_gather_scratch(n), name=f"scatter_grads_{layer}")(*parts, *landings)


def _travel_layout(t):
    tr = lambda a: jnp.swapaxes(a, 1, 2)
    branch = jnp.concatenate([tr(t["w_attn_o"]), tr(t["w_conv_o"]), tr(t["w_ssm_o"])], axis=2)
    return [tr(t["w_in"]), tr(t["w_ffn_in"]), t["w_ffn_out"], t["w_mix_o"], branch, t["w_ssm_glu"]]


def _native_layout(a):
    tr = lambda x: jnp.swapaxes(x, 1, 2)
    b = a[4]
    return {"w_in": tr(a[0]), "w_ffn_in": tr(a[1]), "w_ffn_out": a[2], "w_mix_o": a[3],
            "w_attn_o": tr(b[:, :, :WIDTH]), "w_conv_o": tr(b[:, :, WIDTH:2 * WIDTH]),
            "w_ssm_o": tr(b[:, :, 2 * WIDTH:]), "w_ssm_glu": a[5]}


def _embed(t):
    eye = jnp.eye(SSM_GROUPS, dtype=t.dtype)
    return (t[:, :, :, None, :] * eye[None, :, None, :, None]).reshape(DEPTH, WIDTH, SSM_GROUPS * SSM_STATE)


def _diag_blocks(t):
    t = t.reshape(DEPTH, SSM_GROUPS, SSM_STATE, SSM_GROUPS, SSM_GROUP)
    return jnp.einsum("lgpgh->lghp", t)


def _rope_tabs():
    pos = jnp.arange(SEQ, dtype=F32)
    inv_freq = ROPE_THETA ** (-jnp.arange(0, ROT_DIM, 2, dtype=F32) / ROT_DIM)
    ang = pos[:, None] * inv_freq[None, :]
    cos, sin = jnp.cos(ang), jnp.sin(ang)
    one, zero = jnp.ones((SEQ, HEAD_DIM - ROT_DIM), F32), jnp.zeros((SEQ, HEAD_DIM - ROT_DIM), F32)
    z8 = jnp.zeros((SEQ, 8), F32)
    head = lambda *p: jnp.tile(jnp.concatenate(p, axis=1), (1, 2))
    return head(cos, cos, one), head(-sin, z8, zero), head(z8, sin, zero)


def _ssm_mats(sp):
    lr, li, bbr, bbi = _ssm_prep(sp["a_re"], sp["a_im"], sp["log_dt"], sp["bt_re"], sp["bt_im"])
    b_re, b_im = _embed(bbr), _embed(bbi)
    c_re, c_im = _embed(sp["c_re"]), _embed(sp["c_im"])
    slab3 = lambda t: jnp.swapaxes(t, 1, 2).reshape(DEPTH, SLABS, 128, WIDTH).astype(BF16)
    return {
        "a_re": lr.reshape(DEPTH, SLABS, 1, 128), "a_im": li.reshape(DEPTH, SLABS, 1, 128),
        "b_re": b_re.astype(BF16), "b_im": b_im.astype(BF16),
        "c_re": c_re.astype(BF16), "c_im_neg": (-c_im).astype(BF16),
        "bt_re": slab3(b_re), "bt_im": slab3(b_im), "ct_re": slab3(c_re), "ct_im_neg": slab3(-c_im),
    }


def _layer_fwd(x, i, w, rp, mats, tabs):
    q, kv, cbx, u, u16, glog, h = _rms_mm_in(x, rp["norm_mix"][i], w["win_t"])
    o = _attn_fwd(q, kv, tabs, rp["attn_sinks"][i])
    cv = _conv_fwd(cbx, rp["conv_w"], i)
    bu_re, bu_im = _slab_mm(u16, mats["b_re"], mats["b_im"], i, "slab_mm")
    x_re, x_im = _scan(bu_re, bu_im, mats["a_re"], mats["a_im"], i, False, "scan_fwd")
    y = _slab_contract(x_re, x_im, mats["ct_re"], mats["ct_im_neg"], i, u, rp["ssm_d"], "slab_contract_y")
    z = _glu_fwd(y, w["wglu"])
    x1 = _mix_fwd(x, o, cv, z, glog, rp["b_gate"], i, w["branch_t"], w["wmix"])
    gu, h2 = _rms_mm_ffn(x1, rp["norm_ffn"][i], w["wffn_t"])
    x2 = _ffn_out_fwd(x1, gu, w["wout"])
    kept = dict(x=x, q=q, kv=kv, cbx=cbx, u=u, u16=u16, glog=glog, h=h, o=o, cv=cv, z=z, y=y,
                x_re=x_re, x_im=x_im, x1=x1, gu=gu, h2=h2)
    return x2, kept


def _layer_bwd(dx2, k, i, w, rp, mats, tabs):
    tn = dict(ta=True, out_dtype=BF16)
    dgu, act = _ffn_out_bwd(dx2, k["gu"], w["wout"])
    g_wout = _mm(act, dx2, tm=FFN_H // 2, tn=1024, tk=512, name="mm_tn_ffn_out", **tn)
    g_wffn_t = _mm(dgu, k["h2"], tm=FFN_H // 2, tn=1024, tk=512, name="mm_tn_ffn_in", **tn)
    dx1, d_norm_ffn = _mm_rmsbwd([dgu], w["wffn_t"], k["x1"], rp["norm_ffn"][i], dx2, "mm_rmsbwd_ffn")

    mg, dya, dyc, dys, do, dcv, dz, dgl, db_gate = _mix_bwd(
        dx1, k["o"], k["cv"], k["z"], k["glog"], rp["b_gate"], i, w["branch_t"], w["wmix"])
    g_wmix = _mm(mg, dx1, tm=1024, tn=1024, tk=512, name="mm_tn_mix", **tn)
    g_branch_t = _tn_branches((dya, dyc, dys), (k["o"], k["cv"], k["z"]))

    dy16, ys16, da16, dd = _glu_bwd(k["y"], w["wglu"], dz, k["u"])
    g_wglu = _mm(ys16, da16, tm=512, tn=512, tk=512, name="mm_tn_glu", **tn)
    gx_re, gx_im = _slab_mm(dy16, mats["c_re"], mats["c_im_neg"], i, "slab_mm")
    l_re, l_im = _scan(gx_re, gx_im, mats["a_re"], mats["a_im"], i, True, "scan_bwd")
    du = _slab_contract(l_re, l_im, mats["bt_re"], mats["bt_im"], i, dy16, rp["ssm_d"], "slab_contract_du")
    da_re, da_im = _state_grad(k["x_re"], k["x_im"], l_re, l_im)
    db_re, db_im = _slab_tn(l_re, l_im, k["u16"], "slab_tn")
    dc_re, dc_im = _slab_tn(k["x_re"], k["x_im"], dy16, "slab_tn")

    dcb, dcc, dcx, d_conv_w = _conv_bwd(k["cbx"], rp["conv_w"], i, dcv)
    dq, dkv, d_sinks = _attn_bwd(k["q"], k["kv"], tabs, rp["attn_sinks"][i], do)

    pieces = [dq, dkv, dcb, dcc, dcx, du.astype(BF16), dgl]
    g_win_t = _tn_pieces(pieces, k["h"])
    dx, d_norm_mix = _mm_rmsbwd(pieces, w["win_t"], k["x"], rp["norm_mix"][i], dx1, "mm_rmsbwd_in")

    grads = [g_win_t, g_wffn_t, g_wout, g_wmix, g_branch_t, g_wglu]
    small = dict(norm_mix=d_norm_mix, b_gate=db_gate, attn_sinks=d_sinks, ssm_d=dd, norm_ffn=d_norm_ffn,
                 conv_w=d_conv_w, da_re=da_re, da_im=da_im, db_re=db_re, db_im=db_im, dc_re=dc_re, dc_im=dc_im)
    return dx, grads, small


def _replicated_grads(sg, sp):
    stack = lambda name: jnp.stack([sg[i][name] for i in range(DEPTH)])
    cots = (stack("da_re").reshape(DEPTH, *_GS), stack("da_im").reshape(DEPTH, *_GS),
            _diag_blocks(stack("db_re")), _diag_blocks(stack("db_im")))
    d_a_re, d_a_im, d_log_dt, d_bt_re, d_bt_im = _ssm_prep_bwd(
        sp["a_re"], sp["a_im"], sp["log_dt"], sp["bt_re"], sp["bt_im"], cots)
    sgrads = {"norm_mix": stack("norm_mix"), "b_gate": stack("b_gate"),
              "attn_sinks": stack("attn_sinks")[:, :, :N_Q_HEADS], "ssm_a_re": d_a_re, "ssm_a_im": d_a_im,
              "ssm_b_re": jnp.swapaxes(d_bt_re, 2, 3), "ssm_b_im": jnp.swapaxes(d_bt_im, 2, 3),
              "ssm_c_re": _diag_blocks(stack("dc_re")), "ssm_c_im": -_diag_blocks(stack("dc_im")),
              "ssm_d": stack("ssm_d"), "ssm_log_dt": d_log_dt, "norm_ffn": stack("norm_ffn")}
    return sgrads, stack("conv_w")[:, :3]


def kernel(x, norm_mix, w_in, b_gate, attn_sinks, w_attn_o, conv_w, w_conv_o, ssm_a_re, ssm_a_im, ssm_b_re, ssm_b_im, ssm_c_re, ssm_c_im, ssm_d, ssm_log_dt, w_ssm_glu, w_ssm_o, w_mix_o, norm_ffn, w_ffn_in, w_ffn_out, norm_final, loss_target, m_norm_mix, m_w_in, m_b_gate, m_attn_sinks, m_w_attn_o, m_conv_w, m_w_conv_o, m_ssm_a_re, m_ssm_a_im, m_ssm_b_re, m_ssm_b_im, m_ssm_c_re, m_ssm_c_im, m_ssm_d, m_ssm_log_dt, m_w_ssm_glu, m_w_ssm_o, m_w_mix_o, m_norm_ffn, m_w_ffn_in, m_w_ffn_out, m_norm_final, v_norm_mix, v_w_in, v_b_gate, v_attn_sinks, v_w_attn_o, v_conv_w, v_w_conv_o, v_ssm_a_re, v_ssm_a_im, v_ssm_b_re, v_ssm_b_im, v_ssm_c_re, v_ssm_c_im, v_ssm_d, v_ssm_log_dt, v_w_ssm_glu, v_w_ssm_o, v_w_mix_o, v_norm_ffn, v_w_ffn_in, v_w_ffn_out, v_norm_final):
    big = {"w": dict(w_in=w_in, w_attn_o=w_attn_o, w_conv_o=w_conv_o, w_ssm_glu=w_ssm_glu, w_ssm_o=w_ssm_o,
                     w_mix_o=w_mix_o, w_ffn_in=w_ffn_in, w_ffn_out=w_ffn_out),
           "m": dict(w_in=m_w_in, w_attn_o=m_w_attn_o, w_conv_o=m_w_conv_o, w_ssm_glu=m_w_ssm_glu,
                     w_ssm_o=m_w_ssm_o, w_mix_o=m_w_mix_o, w_ffn_in=m_w_ffn_in, w_ffn_out=m_w_ffn_out),
           "v": dict(w_in=v_w_in, w_attn_o=v_w_attn_o, w_conv_o=v_w_conv_o, w_ssm_glu=v_w_ssm_glu,
                     w_ssm_o=v_w_ssm_o, w_mix_o=v_w_mix_o, w_ffn_in=v_w_ffn_in, w_ffn_out=v_w_ffn_out)}
    small = {"w": dict(norm_mix=norm_mix, b_gate=b_gate, attn_sinks=attn_sinks, ssm_a_re=ssm_a_re,
                       ssm_a_im=ssm_a_im, ssm_b_re=ssm_b_re, ssm_b_im=ssm_b_im, ssm_c_re=ssm_c_re,
                       ssm_c_im=ssm_c_im, ssm_d=ssm_d, ssm_log_dt=ssm_log_dt, norm_ffn=norm_ffn),
             "m": dict(norm_mix=m_norm_mix, b_gate=m_b_gate, attn_sinks=m_attn_sinks, ssm_a_re=m_ssm_a_re,
                       ssm_a_im=m_ssm_a_im, ssm_b_re=m_ssm_b_re, ssm_b_im=m_ssm_b_im, ssm_c_re=m_ssm_c_re,
                       ssm_c_im=m_ssm_c_im, ssm_d=m_ssm_d, ssm_log_dt=m_ssm_log_dt, norm_ffn=m_norm_ffn),
             "v": dict(norm_mix=v_norm_mix, b_gate=v_b_gate, attn_sinks=v_attn_sinks, ssm_a_re=v_ssm_a_re,
                       ssm_a_im=v_ssm_a_im, ssm_b_re=v_ssm_b_re, ssm_b_im=v_ssm_b_im, ssm_c_re=v_ssm_c_re,
                       ssm_c_im=v_ssm_c_im, ssm_d=v_ssm_d, ssm_log_dt=v_ssm_log_dt, norm_ffn=v_norm_ffn)}
    finals = {"w": norm_final, "m": m_norm_final, "v": v_norm_final}
    convs = {"w": conv_w, "m": m_conv_w, "v": v_conv_w}
    mine = 4 * lax.axis_index("x") + 2 * lax.axis_index("y") + lax.axis_index("c")

    travel = {s: _travel_layout(big[s]) for s in "wmv"}
    stacked16 = [a.astype(BF16) for a in travel["w"]]
    conv_all = _all_gather(jnp.pad(conv_w.reshape(6, 128), ((0, 2), (0, 0))), "gather_conv_w")
    conv_full = conv_all[:, :6].reshape(N_DEV, DEPTH, 3, 64).transpose(1, 2, 0, 3).reshape(DEPTH, 3, WIDTH)
    rp = {"norm_mix": norm_mix[:, None], "norm_ffn": norm_ffn[:, None], "attn_sinks": attn_sinks[:, None],
          "b_gate": b_gate[:, None], "ssm_d": ssm_d[:, None], "conv_w": jnp.pad(conv_full, ((0, 0), (0, 5), (0, 0)))}
    sp = {"a_re": ssm_a_re, "a_im": ssm_a_im, "log_dt": ssm_log_dt[:, :, None],
          "bt_re": jnp.swapaxes(ssm_b_re, 2, 3), "bt_im": jnp.swapaxes(ssm_b_im, 2, 3),
          "c_re": ssm_c_re, "c_im": ssm_c_im}
    mats = _ssm_mats(sp)
    tabs = _rope_tabs()

    weights = []
    for i in range(DEPTH):
        g = _gather_layer(stacked16, i)
        weights.append({name: a.reshape(N_DEV * r, c) for a, (name, r, c) in zip(g, KINDS)})

    act = x[0]
    kept = []
    for i in range(DEPTH):
        act, k = _layer_fwd(act, i, weights[i], rp, mats, tabs)
        kept.append(k)
    loss_row, dx, d_norm_final = _loss_head(act, norm_final[None], loss_target[0])
    loss = lax.psum(loss_row[0, 0], ("x", "y", "c"))

    landings = [lax.empty((DEPTH, N_DEV, r, c), BF16) for _, r, c in KINDS]
    sg = [None] * DEPTH
    for i in reversed(range(DEPTH)):
        dx, grads, sg[i] = _layer_bwd(dx, kept[i], i, weights[i], rp, mats, tabs)
        parts = [g.reshape(N_DEV, r, c) for g, (_, r, c) in zip(grads, KINDS)]
        landings = _scatter_layer(parts, landings, i)

    rows_tile = {"win_t": 368, "wffn_t": 352, "wout": 176, "wmix": 128, "branch_t": 128, "wglu": 64}
    big_out = [_adamw(landings[j], travel["w"][j], travel["m"][j], travel["v"][j], rows_tile[name], "adamw_" + name)
               for j, (name, _, _) in enumerate(KINDS)]
    big_res = [_native_layout([big_out[j][kind] for j in range(len(KINDS))]) for kind in range(4)]

    sgrads, conv_grad = _replicated_grads(sg, sp)

    def pack_small(t, final, conv):
        flat = [t[name].reshape(DEPTH, n) for name, n in SMALL]
        flat = jnp.concatenate([jnp.concatenate(flat, axis=1).reshape(-1), final.reshape(-1), conv.reshape(-1)])
        return jnp.pad(flat, (0, SMALL_ROWS * 128 - flat.shape[0])).reshape(SMALL_ROWS, 128)

    zeros_conv = jnp.zeros((CONV_N,), F32)
    sparts = _all_gather(pack_small(sgrads, d_norm_final, conv_grad), "gather_small_grads")
    sw, sm_, sv = (pack_small(small[s], finals[s], zeros_conv) for s in "wmv")
    small_out = _adamw(sparts[None], sw[None], sm_[None], sv[None], SMALL_ROWS // 8, "adamw_replicated")

    def unpack_small(p):
        flat = p.reshape(-1)
        per = flat[:DEPTH * SMALL_PER_LAYER].reshape(DEPTH, SMALL_PER_LAYER)
        out, off = {}, 0
        for name, n in SMALL:
            out[name] = per[:, off:off + n].reshape(small["w"][name].shape)
            off += n
        out["norm_final"] = flat[DEPTH * SMALL_PER_LAYER:DEPTH * SMALL_PER_LAYER + D_MODEL]
        return out

    small_res = [unpack_small(p) for p in small_out]

    conv_off = DEPTH * SMALL_PER_LAYER + D_MODEL
    conv_parts = sparts.reshape(N_DEV, -1)[:, conv_off:conv_off + CONV_N].reshape(N_DEV, DEPTH * 3, WIDTH)
    conv_parts = lax.dynamic_slice_in_dim(conv_parts, mine * 64, 64, axis=2)
    conv_res = _adamw(conv_parts[None], *(convs[s].reshape(1, DEPTH * 3, 64) for s in "wmv"), DEPTH * 3, "adamw_conv_w")

    order = ["norm_mix", "w_in", "b_gate", "attn_sinks", "w_attn_o", "conv_w", "w_conv_o", "ssm_a_re", "ssm_a_im",
             "ssm_b_re", "ssm_b_im", "ssm_c_re", "ssm_c_im", "ssm_d", "ssm_log_dt", "w_ssm_glu", "w_ssm_o",
             "w_mix_o", "norm_ffn", "w_ffn_in", "w_ffn_out", "norm_final"]
    outs = [loss, dx[None]]
    for kind in range(4):
        for name in order:
            if name == "conv_w":
                outs.append(conv_res[kind].reshape(DEPTH, 3, 64))
            elif name in big_res[kind]:
                outs.append(big_res[kind][name])
            else:
                outs.append(small_res[kind][name])
    return tuple(outs)
```

```python
import functools
import math

import jax
import jax.numpy as jnp
from jax import lax
from jax.experimental import pallas as pl
from jax.experimental.pallas import tpu as pltpu

F32 = jnp.float32
BF16 = jnp.bfloat16

N_DEV = 8
DEPTH = 4
SEQ = 2048
D_MODEL = 1024
N_Q_HEADS = 8
HEAD_DIM = 64
ATTN_W = 512
KV_W = 128
BLOCK = 128
N_BLOCKS = SEQ // BLOCK
ROPE_THETA = 500000.0
ROT_DIM = 16
NEG_INF = -1e30
WIDTH = 512
SSM_GROUPS = 32
SSM_GROUP = 16
SSM_STATE = 64
SLABS = 16
CHUNK = 256
N_CHUNKS = SEQ // CHUNK
GATE_W = 3 * D_MODEL
IN_COLS = 5888
FFN_H = 2816
NORM_EPS = 1e-6
LR, B1, B2, ADAM_EPS, WD, STEP = 0.001, 0.9, 0.999, 1e-08, 0.01, 10

COL_Q, COL_KV, COL_CBX, COL_U, COL_G = 0, 512, 768, 2304, 2816
PIECE_W = (512, 256, 512, 512, 512, 512, 3072)
PIECE_OFF = tuple(sum(PIECE_W[:i]) for i in range(len(PIECE_W)))

KINDS = (("win_t", 736, 1024), ("wffn_t", 704, 1024), ("wout", 352, 1024), ("wmix", 128, 1024),
         ("branch_t", 128, 1536), ("wglu", 64, 512))

SMALL = (("norm_mix", 1024), ("b_gate", 3072), ("attn_sinks", 8), ("ssm_a_re", 2048), ("ssm_a_im", 2048),
         ("ssm_b_re", 32768), ("ssm_b_im", 32768), ("ssm_c_re", 32768), ("ssm_c_im", 32768),
         ("ssm_d", 512), ("ssm_log_dt", 32), ("norm_ffn", 1024))
SMALL_PER_LAYER = sum(n for _, n in SMALL)
CONV_N = DEPTH * 3 * WIDTH
SMALL_ROWS = 4480

VMEM_LIMIT = 56 * 1024 * 1024
NT = (((1,), (1,)), ((), ()))
TN = (((0,), (0,)), ((), ()))
MESH_ID = pl.DeviceIdType.MESH
ANY = pl.BlockSpec(memory_space=pl.ANY)
HBM = pl.BlockSpec(memory_space=pltpu.HBM)
SEM = pl.BlockSpec(memory_space=pltpu.SEMAPHORE)
EFFECT = pltpu.SideEffectType.DATAFLOW_SIDE_EFFECTING


def _cp(**kw):
    return pltpu.CompilerParams(vmem_limit_bytes=VMEM_LIMIT, **kw)


def _full(shape):
    return pl.BlockSpec(shape, lambda *_: (0,) * len(shape))


def _mm(a, b, *, ta=False, tb=False, tm, tn, tk, out_dtype=F32, name):
    m = a.shape[1] if ta else a.shape[0]
    k = a.shape[0] if ta else a.shape[1]
    n = b.shape[0] if tb else b.shape[1]
    nk = k // tk
    dims = (((0 if ta else 1,), (1 if tb else 0,)), ((), ()))

    def body(a_ref, b_ref, o_ref, acc_ref):
        kk = pl.program_id(2)

        @pl.when(kk == 0)
        def _():
            acc_ref[...] = jnp.zeros_like(acc_ref)

        acc_ref[...] += lax.dot_general(a_ref[...].astype(BF16), b_ref[...].astype(BF16), dims,
                                        preferred_element_type=F32)

        @pl.when(kk == nk - 1)
        def _():
            o_ref[...] = acc_ref[...].astype(out_dtype)

    a_spec = pl.BlockSpec((tk, tm), lambda i, j, kk: (kk, i)) if ta else pl.BlockSpec((tm, tk), lambda i, j, kk: (i, kk))
    b_spec = pl.BlockSpec((tn, tk), lambda i, j, kk: (j, kk)) if tb else pl.BlockSpec((tk, tn), lambda i, j, kk: (kk, j))
    return pl.pallas_call(
        body, grid=(m // tm, n // tn, nk), in_specs=[a_spec, b_spec],
        out_specs=pl.BlockSpec((tm, tn), lambda i, j, kk: (i, j)),
        out_shape=jax.ShapeDtypeStruct((m, n), out_dtype),
        scratch_shapes=[pltpu.VMEM((tm, tn), F32)], compiler_params=_cp(), name=name)(a, b)


def _rms_rows(xv, g):
    r = lax.rsqrt(jnp.mean(xv * xv, axis=-1, keepdims=True) + NORM_EPS)
    return ((xv * r) * g).astype(BF16)


def _rms_mm_in(x, g, wt, tie):
    tt = 256
    widths = (ATTN_W, 2 * KV_W, 3 * WIDTH, WIDTH, GATE_W)
    offs = (COL_Q, COL_KV, COL_CBX, COL_U, COL_G)

    def body(x_ref, g_ref, w_ref, tie_ref, q_ref, kv_ref, cbx_ref, u_ref, u16_ref, gl_ref, h_ref):
        h = _rms_rows(x_ref[...], g_ref[...])
        h_ref[...] = h
        prod = lax.dot_general(h, w_ref[...], NT, preferred_element_type=F32)
        for ref, o, w in zip((q_ref, kv_ref, cbx_ref, u_ref, gl_ref), offs, widths):
            ref[...] = prod[:, o:o + w]
        u16_ref[...] = prod[:, COL_U:COL_U + WIDTH].astype(BF16)

    row = lambda w: pl.BlockSpec((tt, w), lambda i: (i, 0))
    sds = jax.ShapeDtypeStruct
    return pl.pallas_call(
        body, grid=(SEQ // tt,), in_specs=[row(D_MODEL), _full((1, D_MODEL)), _full((IN_COLS, D_MODEL)), ANY],
        out_specs=[row(ATTN_W), row(2 * KV_W), row(3 * WIDTH), row(WIDTH), row(WIDTH), row(GATE_W), row(D_MODEL)],
        out_shape=[sds((SEQ, ATTN_W), F32), sds((SEQ, 2 * KV_W), F32), sds((SEQ, 3 * WIDTH), F32),
                   sds((SEQ, WIDTH), F32), sds((SEQ, WIDTH), BF16), sds((SEQ, GATE_W), F32),
                   sds((SEQ, D_MODEL), BF16)],
        compiler_params=_cp(), name="rms_mm_in")(x, g, wt, tie)


def _rms_mm_ffn(x, g, wt):
    tt = 256

    def body(x_ref, g_ref, w_ref, o_ref, h_ref):
        h = _rms_rows(x_ref[...], g_ref[...])
        h_ref[...] = h
        o_ref[...] = lax.dot_general(h, w_ref[...], NT, preferred_element_type=F32)

    row = lambda w: pl.BlockSpec((tt, w), lambda i: (i, 0))
    return pl.pallas_call(
        body, grid=(SEQ // tt,), in_specs=[row(D_MODEL), _full((1, D_MODEL)), _full((2 * FFN_H, D_MODEL))],
        out_specs=[row(2 * FFN_H), row(D_MODEL)],
        out_shape=[jax.ShapeDtypeStruct((SEQ, 2 * FFN_H), F32), jax.ShapeDtypeStruct((SEQ, D_MODEL), BF16)],
        compiler_params=_cp(), name="rms_mm_ffn")(x, g, wt)


def _mm_rmsbwd(pieces, wt, x, g, dres, name):
    tt = 256
    widths = [p.shape[1] for p in pieces]
    offs = [sum(widths[:i]) for i in range(len(widths))]
    n = len(pieces)

    def body(*refs):
        p_refs, (w_ref, x_ref, g_ref, r_ref, dx_ref, dg_ref) = refs[:n], refs[n:]

        @pl.when(pl.program_id(0) == 0)
        def _():
            dg_ref[...] = jnp.zeros_like(dg_ref)

        dh = jnp.zeros((tt, D_MODEL), F32)
        for p_ref, o, w in zip(p_refs, offs, widths):
            dh += jnp.dot(p_ref[...], w_ref[o:o + w, :], preferred_element_type=F32)
        xv = x_ref[...]
        r = lax.rsqrt(jnp.mean(xv * xv, axis=-1, keepdims=True) + NORM_EPS)
        xh = xv * r
        gy = dh * g_ref[...]
        dx_ref[...] = r_ref[...] + r * (gy - xh * jnp.mean(gy * xh, axis=-1, keepdims=True))
        dg_ref[...] += jnp.sum(dh * xh, axis=0, keepdims=True)

    row = lambda w: pl.BlockSpec((tt, w), lambda i: (i, 0))
    return pl.pallas_call(
        body, grid=(SEQ // tt,),
        in_specs=[row(w) for w in widths] + [_full(wt.shape), row(D_MODEL), _full((1, D_MODEL)), row(D_MODEL)],
        out_specs=[row(D_MODEL), _full((1, D_MODEL))],
        out_shape=[jax.ShapeDtypeStruct((SEQ, D_MODEL), F32), jax.ShapeDtypeStruct((1, D_MODEL), F32)],
        compiler_params=_cp(), name=name)(*pieces, wt, x, g, dres)


def _tn_pieces(pieces, h):
    tk, tn = 512, 512
    nk = SEQ // tk
    n = len(pieces)

    def body(*refs):
        p_refs, (h_ref, o_ref, acc_ref) = refs[:n], refs[n:]
        kk = pl.program_id(1)

        @pl.when(kk == 0)
        def _():
            acc_ref[...] = jnp.zeros_like(acc_ref)

        hv = h_ref[...]
        for p_ref, o, w in zip(p_refs, PIECE_OFF, PIECE_W):
            acc_ref[o:o + w, :] += lax.dot_general(p_ref[...], hv, TN, preferred_element_type=F32)

        @pl.when(kk == nk - 1)
        def _():
            o_ref[...] = acc_ref[...].astype(BF16)

    return pl.pallas_call(
        body, grid=(D_MODEL // tn, nk),
        in_specs=[pl.BlockSpec((tk, w), lambda j, kk: (kk, 0)) for w in PIECE_W]
        + [pl.BlockSpec((tk, tn), lambda j, kk: (kk, j))],
        out_specs=pl.BlockSpec((IN_COLS, tn), lambda j, kk: (0, j)),
        out_shape=jax.ShapeDtypeStruct((IN_COLS, D_MODEL), BF16),
        scratch_shapes=[pltpu.VMEM((IN_COLS, tn), F32)], compiler_params=_cp(), name="tn_pieces")(*pieces, h)


def _tn_branches(dys, acts):
    tk = 512
    nk = SEQ // tk

    def body(d0, d1, d2, a0, a1, a2, o_ref, acc_ref):
        kk = pl.program_id(0)

        @pl.when(kk == 0)
        def _():
            acc_ref[...] = jnp.zeros_like(acc_ref)

        for j, (d, a) in enumerate(((d0, a0), (d1, a1), (d2, a2))):
            acc_ref[:, WIDTH * j:WIDTH * (j + 1)] += lax.dot_general(d[...], a[...], TN, preferred_element_type=F32)

        @pl.when(kk == nk - 1)
        def _():
            o_ref[...] = acc_ref[...].astype(BF16)

    row = lambda w: pl.BlockSpec((tk, w), lambda kk: (kk, 0))
    return pl.pallas_call(
        body, grid=(nk,), in_specs=[row(D_MODEL)] * 3 + [row(WIDTH)] * 3,
        out_specs=_full((D_MODEL, 3 * WIDTH)), out_shape=jax.ShapeDtypeStruct((D_MODEL, 3 * WIDTH), BF16),
        scratch_shapes=[pltpu.VMEM((D_MODEL, 3 * WIDTH), F32)], compiler_params=_cp(), name="tn_branches",
    )(*dys, *acts)


def _rope(t, c, a, b):
    return t * c + pltpu.roll(t, 120, axis=1) * a + pltpu.roll(t, 8, axis=1) * b


def _rope_t(d, c, a, b):
    return d * c + pltpu.roll(d * a, 8, axis=1) + pltpu.roll(d * b, 120, axis=1)


def _band_sides(band):
    left = lax.broadcasted_iota(jnp.int32, band.shape, 1) < HEAD_DIM
    h0 = jnp.where(left, band, 0.0)
    h1 = jnp.where(left, 0.0, band)
    r0 = pltpu.roll(h0, HEAD_DIM, axis=1)
    r1 = pltpu.roll(h1, HEAD_DIM, axis=1)
    return ((h0.astype(BF16), r0.astype(BF16)), (r1.astype(BF16), h1.astype(BF16)))


def _attn_mask(i):
    qi = lax.broadcasted_iota(jnp.int32, (BLOCK, 2 * BLOCK), 0)
    kj = lax.broadcasted_iota(jnp.int32, (BLOCK, 2 * BLOCK), 1)
    delta = qi + BLOCK - kj
    return (delta >= 0) & (delta < BLOCK) & ((kj >= BLOCK) | (i > 0))


def _attn_probs(qc, kside, ok, sink):
    s = lax.dot_general(qc, kside, NT, preferred_element_type=F32) * (HEAD_DIM ** -0.5)
    s = jnp.where(ok, s, NEG_INF)
    m = jnp.maximum(jnp.max(s, axis=-1, keepdims=True), sink)
    p = jnp.exp(s - m)
    es = jnp.exp(sink - m)
    inv = 1.0 / (jnp.sum(p, axis=-1, keepdims=True) + es)
    return p * inv, es * inv


def _attn_load(q_ref, kvc_ref, kvp_ref, tc_ref, ta_ref, tb_ref, pc_ref, pa_ref, pb_ref):
    c, a, b = tc_ref[...], ta_ref[...], tb_ref[...]
    kc = _rope(kvc_ref[:, :KV_W], c, a, b)
    kp = _rope(kvp_ref[:, :KV_W], pc_ref[...], pa_ref[...], pb_ref[...])
    kband = jnp.concatenate([kp, kc], axis=0)
    vband = jnp.concatenate([kvp_ref[:, KV_W:], kvc_ref[:, KV_W:]], axis=0)
    qs = [_rope(q_ref[:, 128 * j:128 * (j + 1)], c, a, b).astype(BF16) for j in range(4)]
    return qs, _band_sides(kband), _band_sides(vband), (c, a, b)


def _attn_specs(clamp):
    cur = lambda i: (clamp(i), 0)
    prev = lambda i: (jnp.maximum(clamp(i) - 1, 0), 0)
    return [
        pl.BlockSpec((BLOCK, ATTN_W), cur), pl.BlockSpec((BLOCK, 2 * KV_W), cur),
        pl.BlockSpec((BLOCK, 2 * KV_W), prev),
        pl.BlockSpec((BLOCK, 128), cur), pl.BlockSpec((BLOCK, 128), cur), pl.BlockSpec((BLOCK, 128), cur),
        pl.BlockSpec((BLOCK, 128), prev), pl.BlockSpec((BLOCK, 128), prev), pl.BlockSpec((BLOCK, 128), prev),
        pl.BlockSpec(memory_space=pltpu.SMEM),
    ]


def _attn_fwd(q, kv, tabs, sinks):
    tc, ta, tb = tabs

    def body(q_ref, kvc_ref, kvp_ref, tc_ref, ta_ref, tb_ref, pc_ref, pa_ref, pb_ref, sink_ref, o_ref):
        i = pl.program_id(0)
        qs, ks, vs, _ = _attn_load(q_ref, kvc_ref, kvp_ref, tc_ref, ta_ref, tb_ref, pc_ref, pa_ref, pb_ref)
        ok = _attn_mask(i)
        for j in range(4):
            kh = j // 2
            acc = jnp.zeros((BLOCK, 128), F32)
            for side in range(2):
                pn, _ = _attn_probs(qs[j], ks[kh][side], ok, sink_ref[0, 2 * j + side])
                acc += jnp.dot(pn.astype(BF16), vs[kh][side], preferred_element_type=F32)
            o_ref[:, 128 * j:128 * (j + 1)] = acc.astype(BF16)

    return pl.pallas_call(
        body, grid=(N_BLOCKS,), in_specs=_attn_specs(lambda i: i),
        out_specs=pl.BlockSpec((BLOCK, ATTN_W), lambda i: (i, 0)),
        out_shape=jax.ShapeDtypeStruct((SEQ, ATTN_W), BF16), compiler_params=_cp(), name="attn_fwd",
    )(q, kv, kv, tc, ta, tb, tc, ta, tb, sinks)


def _attn_bwd(q, kv, tabs, sinks, do):
    tc, ta, tb = tabs
    last = N_BLOCKS - 1
    clamp = lambda i: jnp.minimum(i, last)

    def place(full, side, kh):
        left = lax.broadcasted_iota(jnp.int32, full.shape, 1) < HEAD_DIM
        valid = jnp.where(left, full, 0.0) if side == 0 else jnp.where(left, 0.0, full)
        return valid if side == kh else pltpu.roll(valid, HEAD_DIM, axis=1)

    def body(q_ref, kvc_ref, kvp_ref, tc_ref, ta_ref, tb_ref, pc_ref, pa_ref, pb_ref, sink_ref, do_ref,
             dq_ref, dkv_ref, ds_ref, carry_ref):
        i = pl.program_id(0)

        @pl.when(i == 0)
        def _():
            ds_ref[...] = jnp.zeros_like(ds_ref)
            carry_ref[...] = jnp.zeros_like(carry_ref)

        @pl.when(i > last)
        def _():
            dkv_ref[...] = carry_ref[...].astype(BF16)

        @pl.when(i <= last)
        def _():
            qs, ks, vs, (c, a, b) = _attn_load(q_ref, kvc_ref, kvp_ref, tc_ref, ta_ref, tb_ref,
                                               pc_ref, pa_ref, pb_ref)
            ok = _attn_mask(i)
            dk = jnp.zeros((2 * BLOCK, 128), F32)
            dv = jnp.zeros((2 * BLOCK, 128), F32)
            dsink = jnp.zeros((1, 128), F32)
            lane = lax.broadcasted_iota(jnp.int32, (1, 128), 1)
            for j in range(4):
                kh = j // 2
                doc = do_ref[:, 128 * j:128 * (j + 1)].astype(BF16)
                dq = jnp.zeros((BLOCK, 128), F32)
                for side in range(2):
                    pn, ps = _attn_probs(qs[j], ks[kh][side], ok, sink_ref[0, 2 * j + side])
                    dp = lax.dot_general(doc, vs[kh][side], NT, preferred_element_type=F32)
                    dr = jnp.sum(pn * dp, axis=-1, keepdims=True)
                    dsb = (pn * (dp - dr) * (HEAD_DIM ** -0.5)).astype(BF16)
                    dsink += jnp.where(lane == 2 * j + side, -jnp.sum(ps * dr), 0.0)
                    dq += jnp.dot(dsb, ks[kh][side], preferred_element_type=F32)
                    dk += place(lax.dot_general(dsb, qs[j], TN, preferred_element_type=F32), side, kh)
                    dv += place(lax.dot_general(pn.astype(BF16), doc, TN, preferred_element_type=F32), side, kh)
                dq_ref[:, 128 * j:128 * (j + 1)] = _rope_t(dq, c, a, b).astype(BF16)
            ds_ref[...] += dsink
            dk_prev = _rope_t(dk[:BLOCK], pc_ref[...], pa_ref[...], pb_ref[...])
            dk_cur = _rope_t(dk[BLOCK:], c, a, b)
            prev = jnp.concatenate([dk_prev, dv[:BLOCK]], axis=1)
            dkv_ref[...] = (carry_ref[...] + prev).astype(BF16)
            carry_ref[...] = jnp.concatenate([dk_cur, dv[BLOCK:]], axis=1)

    return pl.pallas_call(
        body, grid=(N_BLOCKS + 1,),
        in_specs=_attn_specs(clamp) + [pl.BlockSpec((BLOCK, ATTN_W), lambda i: (clamp(i), 0))],
        out_specs=[pl.BlockSpec((BLOCK, ATTN_W), lambda i: (clamp(i), 0)),
                   pl.BlockSpec((BLOCK, 2 * KV_W), lambda i: (jnp.maximum(i - 1, 0), 0)),
                   pl.BlockSpec((1, 128), lambda i: (0, 0))],
        out_shape=[jax.ShapeDtypeStruct((SEQ, ATTN_W), BF16), jax.ShapeDtypeStruct((SEQ, 2 * KV_W), BF16),
                   jax.ShapeDtypeStruct((1, 128), F32)],
        scratch_shapes=[pltpu.VMEM((BLOCK, 2 * KV_W), F32)], compiler_params=_cp(), name="attn_bwd",
    )(q, kv, kv, tc, ta, tb, tc, ta, tb, sinks, do)


def _shift_down(z, k):
    row = lax.broadcasted_iota(jnp.int32, z.shape, 0)
    return jnp.where(row < k, 0.0, pltpu.roll(z, k, axis=0))


def _shift_up(z, k):
    n = z.shape[0]
    row = lax.broadcasted_iota(jnp.int32, z.shape, 0)
    return jnp.where(row >= n - k, 0.0, pltpu.roll(z, n - k, axis=0))


def _conv_specs():
    nb = WIDTH // 128
    return [pl.BlockSpec((SEQ, 128), lambda j: (0, j)), pl.BlockSpec((SEQ, 128), lambda j: (0, nb + j)),
            pl.BlockSpec((SEQ, 128), lambda j: (0, 2 * nb + j)), pl.BlockSpec((None, 8, 128), lambda j: (0, 0, j))]


def _conv_fwd(cbx, cw, layer):
    def body(cb_ref, cc_ref, cx_ref, w_ref, o_ref):
        z = cc_ref[...] * cx_ref[...]
        s = w_ref[0:1, :] * _shift_down(z, 2) + w_ref[1:2, :] * _shift_down(z, 1) + w_ref[2:3, :] * z
        o_ref[...] = (cb_ref[...] * s).astype(BF16)

    specs = _conv_specs()
    specs[3] = pl.BlockSpec((None, 8, 128), lambda j: (layer, 0, j))
    return pl.pallas_call(
        body, grid=(WIDTH // 128,), in_specs=specs,
        out_specs=pl.BlockSpec((SEQ, 128), lambda j: (0, j)),
        out_shape=jax.ShapeDtypeStruct((SEQ, WIDTH), BF16), compiler_params=_cp(), name="conv_fwd",
    )(cbx, cbx, cbx, cw)


def _conv_bwd(cbx, cw, layer, dout):
    def body(cb_ref, cc_ref, cx_ref, w_ref, do_ref, dcb_ref, dcc_ref, dcx_ref, dw_ref):
        cc, cx = cc_ref[...], cx_ref[...]
        z = cc * cx
        z1, z2 = _shift_down(z, 1), _shift_down(z, 2)
        w0, w1, w2 = w_ref[0:1, :], w_ref[1:2, :], w_ref[2:3, :]
        dout = do_ref[...]
        ds = dout * cb_ref[...]
        dcb_ref[...] = (dout * (w0 * z2 + w1 * z1 + w2 * z)).astype(BF16)
        dz = w2 * ds + w1 * _shift_up(ds, 1) + w0 * _shift_up(ds, 2)
        dcc_ref[...] = (dz * cx).astype(BF16)
        dcx_ref[...] = (dz * cc).astype(BF16)
        rows = [jnp.sum(ds * zz, axis=0, keepdims=True) for zz in (z2, z1, z)]
        dw_ref[...] = jnp.concatenate(rows + [jnp.zeros((5, 128), F32)], axis=0)

    col = lambda j: (0, j)
    specs = _conv_specs()
    specs[3] = pl.BlockSpec((None, 8, 128), lambda j: (layer, 0, j))
    return pl.pallas_call(
        body, grid=(WIDTH // 128,), in_specs=specs + [pl.BlockSpec((SEQ, 128), col)],
        out_specs=[pl.BlockSpec((SEQ, 128), col), pl.BlockSpec((SEQ, 128), col), pl.BlockSpec((SEQ, 128), col),
                   pl.BlockSpec((8, 128), col)],
        out_shape=[jax.ShapeDtypeStruct((SEQ, WIDTH), BF16)] * 3 + [jax.ShapeDtypeStruct((8, WIDTH), F32)],
        compiler_params=_cp(), name="conv_bwd",
    )(cbx, cbx, cbx, cw, dout)


def _ssm_prep_math(a_re, a_im, log_dt, bt_re, bt_im):
    dt = jnp.exp(log_dt)
    er = jnp.exp(a_re * dt)
    lr = er * jnp.cos(a_im * dt)
    li = er * jnp.sin(a_im * dt)
    n2 = a_re * a_re + a_im * a_im
    cr = ((lr - 1.0) * a_re + li * a_im) / n2
    ci = (li * a_re - (lr - 1.0) * a_im) / n2
    cr3, ci3 = cr[:, None, :], ci[:, None, :]
    return lr, li, cr3 * bt_re - ci3 * bt_im, cr3 * bt_im + ci3 * bt_re


_GS = (SSM_GROUPS, SSM_STATE)
_GHS = (SSM_GROUPS, SSM_GROUP, SSM_STATE)


def _layered(shape):
    return pl.BlockSpec((None,) + shape, lambda l: (l,) + (0,) * len(shape))


def _ssm_prep(a_re, a_im, log_dt, bt_re, bt_im):
    def body(ar, ai, ld, br, bi, o0, o1, o2, o3):
        outs = _ssm_prep_math(ar[...], ai[...], ld[...], br[...], bi[...])
        for o, v in zip((o0, o1, o2, o3), outs):
            o[...] = v

    shapes = [_GS, _GS, _GHS, _GHS]
    return pl.pallas_call(
        body, grid=(DEPTH,), in_specs=[_layered(s) for s in (_GS, _GS, (SSM_GROUPS, 1), _GHS, _GHS)],
        out_specs=[_layered(s) for s in shapes],
        out_shape=[jax.ShapeDtypeStruct((DEPTH,) + s, F32) for s in shapes],
        name="ssm_prep")(a_re, a_im, log_dt, bt_re, bt_im)


def _ssm_prep_bwd(a_re, a_im, log_dt, bt_re, bt_im, cots):
    def body(ar, ai, ld, br, bi, c0, c1, c2, c3, o0, o1, o2, o3, o4):
        _, vjp = jax.vjp(_ssm_prep_math, ar[...], ai[...], ld[...], br[...], bi[...])
        for o, v in zip((o0, o1, o2, o3, o4), vjp((c0[...], c1[...], c2[...], c3[...]))):
            o[...] = v

    ins = (_GS, _GS, (SSM_GROUPS, 1), _GHS, _GHS)
    return pl.pallas_call(
        body, grid=(DEPTH,), in_specs=[_layered(s) for s in ins + (_GS, _GS, _GHS, _GHS)],
        out_specs=[_layered(s) for s in ins],
        out_shape=[jax.ShapeDtypeStruct((DEPTH,) + s, F32) for s in ins],
        name="ssm_prep_bwd")(a_re, a_im, log_dt, bt_re, bt_im, *cots)


def _slab_mm(u, w_re, w_im, layer, name):
    def body(u_ref, wr_ref, wi_ref, or_ref, oi_ref):
        uv = u_ref[...]
        or_ref[...] = jnp.dot(uv, wr_ref[...], preferred_element_type=F32)
        oi_ref[...] = jnp.dot(uv, wi_ref[...], preferred_element_type=F32)

    slab = pl.BlockSpec((None, SEQ, 128), lambda k: (k, 0, 0))
    wcol = pl.BlockSpec((None, WIDTH, 128), lambda k: (layer, 0, k))
    return pl.pallas_call(
        body, grid=(SLABS,), in_specs=[_full((SEQ, WIDTH)), wcol, wcol],
        out_specs=[slab, slab], out_shape=[jax.ShapeDtypeStruct((SLABS, SEQ, 128), F32)] * 2,
        compiler_params=_cp(), name=name)(u, w_re, w_im)


def _scan(b_re, b_im, a_re, a_im, layer, reverse, name):
    def body(br_ref, bi_ref, ar_ref, ai_ref, xr_ref, xi_ref, pr_ref, pi_ref):
        ar = jnp.broadcast_to(ar_ref[...], (N_CHUNKS, 128))
        ai = jnp.broadcast_to(ai_ref[...], (N_CHUNKS, 128))
        if reverse:
            ai = -ai

        def rows(tau):
            t = (CHUNK - 1 - tau) if reverse else tau
            return pl.ds(t, N_CHUNKS, stride=CHUNK)

        def first(tau, carry):
            sr, si, pr, pi = carry
            sr, si = ar * sr - ai * si + br_ref[rows(tau), :], ar * si + ai * sr + bi_ref[rows(tau), :]
            pr, pi = ar * pr - ai * pi, ar * pi + ai * pr
            xr_ref[rows(tau), :] = sr
            xi_ref[rows(tau), :] = si
            at = pl.ds(pl.multiple_of(tau * N_CHUNKS, N_CHUNKS), N_CHUNKS)
            pr_ref[at, :] = pr
            pi_ref[at, :] = pi
            return sr, si, pr, pi

        zero = jnp.zeros((N_CHUNKS, 128), F32)
        er, ei, qr, qi = lax.fori_loop(0, CHUNK, first, (zero, zero, zero + 1.0, zero), unroll=8)

        shift = _shift_up if reverse else _shift_down
        for k in (1, 2, 4):
            sr, si = shift(er, k), shift(ei, k)
            er, ei = er + qr * sr - qi * si, ei + qr * si + qi * sr
            qr, qi = qr * qr - qi * qi, 2.0 * qr * qi
        cr, ci = shift(er, 1), shift(ei, 1)

        def second(tau, _):
            at = pl.ds(pl.multiple_of(tau * N_CHUNKS, N_CHUNKS), N_CHUNKS)
            pr, pi = pr_ref[at, :], pi_ref[at, :]
            xr_ref[rows(tau), :] += pr * cr - pi * ci
            xi_ref[rows(tau), :] += pr * ci + pi * cr
            return 0

        lax.fori_loop(0, CHUNK, second, 0, unroll=8)

    slab = pl.BlockSpec((None, SEQ, 128), lambda k: (k, 0, 0))
    vec = pl.BlockSpec((None, None, 1, 128), lambda k: (layer, k, 0, 0))
    return pl.pallas_call(
        body, grid=(SLABS,), in_specs=[slab, slab, vec, vec], out_specs=[slab, slab],
        out_shape=[jax.ShapeDtypeStruct((SLABS, SEQ, 128), F32)] * 2,
        scratch_shapes=[pltpu.VMEM((CHUNK * N_CHUNKS, 128), F32)] * 2, compiler_params=_cp(), name=name,
    )(b_re, b_im, a_re, a_im)


def _slab_contract(x_re, x_im, m_re, m_im, layer, e1, e2, name):
    tt = 256

    def body(xr_ref, xi_ref, mr_ref, mi_ref, e1_ref, e2_ref, o_ref):
        acc = e1_ref[...].astype(F32) * e2_ref[...]
        for k in range(SLABS):
            acc += jnp.dot(xr_ref[k].astype(BF16), mr_ref[k], preferred_element_type=F32)
            acc += jnp.dot(xi_ref[k].astype(BF16), mi_ref[k], preferred_element_type=F32)
        o_ref[...] = acc

    xs = pl.BlockSpec((SLABS, tt, 128), lambda i: (0, i, 0))
    ms = pl.BlockSpec((None, SLABS, 128, WIDTH), lambda i: (layer, 0, 0, 0))
    row = pl.BlockSpec((tt, WIDTH), lambda i: (i, 0))
    return pl.pallas_call(
        body, grid=(SEQ // tt,),
        in_specs=[xs, xs, ms, ms, row, pl.BlockSpec((None, 1, WIDTH), lambda i: (layer, 0, 0))],
        out_specs=row, out_shape=jax.ShapeDtypeStruct((SEQ, WIDTH), F32), compiler_params=_cp(), name=name,
    )(x_re, x_im, m_re, m_im, e1, e2)


def _slab_tn(s_re, s_im, v, name):
    def body(sr_ref, si_ref, v_ref, or_ref, oi_ref):
        vv = v_ref[...]
        or_ref[...] = lax.dot_general(sr_ref[...].astype(BF16), vv, TN, preferred_element_type=F32)
        oi_ref[...] = lax.dot_general(si_ref[...].astype(BF16), vv, TN, preferred_element_type=F32)

    slab = pl.BlockSpec((None, SEQ, 128), lambda k: (k, 0, 0))
    out = pl.BlockSpec((None, 128, WIDTH), lambda k: (k, 0, 0))
    return pl.pallas_call(
        body, grid=(SLABS,), in_specs=[slab, slab, _full((SEQ, WIDTH))], out_specs=[out, out],
        out_shape=[jax.ShapeDtypeStruct((SLABS, 128, WIDTH), F32)] * 2, compiler_params=_cp(), name=name,
    )(s_re, s_im, v)


def _state_grad(x_re, x_im, l_re, l_im):
    def body(xr_ref, xi_ref, lr_ref, li_ref, or_ref, oi_ref):
        xr, xi = _shift_down(xr_ref[...], 1), _shift_down(xi_ref[...], 1)
        lr, li = lr_ref[...], li_ref[...]
        or_ref[...] = jnp.sum(xr * lr + xi * li, axis=0, keepdims=True)
        oi_ref[...] = jnp.sum(xr * li - xi * lr, axis=0, keepdims=True)

    slab = pl.BlockSpec((None, SEQ, 128), lambda k: (k, 0, 0))
    vec = pl.BlockSpec((None, 1, 128), lambda k: (k, 0, 0))
    return pl.pallas_call(
        body, grid=(SLABS,), in_specs=[slab] * 4, out_specs=[vec, vec],
        out_shape=[jax.ShapeDtypeStruct((SLABS, 1, 128), F32)] * 2, compiler_params=_cp(), name="state_grad",
    )(x_re, x_im, l_re, l_im)


_GELU_C = math.sqrt(2.0 / math.pi)


def _gelu(y):
    return 0.5 * y * (1.0 + jnp.tanh(_GELU_C * (y + 0.044715 * (y * y * y))))


def _glu_fwd(y, wglu):
    tt = 512

    def body(y_ref, w_ref, z_ref):
        ys = _gelu(y_ref[...])
        a = jnp.dot(ys.astype(BF16), w_ref[...], preferred_element_type=F32)
        z_ref[...] = (ys * jax.nn.sigmoid(a)).astype(BF16)

    blk = pl.BlockSpec((tt, WIDTH), lambda i: (i, 0))
    return pl.pallas_call(body, grid=(SEQ // tt,), in_specs=[blk, _full((WIDTH, WIDTH))], out_specs=blk,
                          out_shape=jax.ShapeDtypeStruct((SEQ, WIDTH), BF16), compiler_params=_cp(),
                          name="glu_fwd")(y, wglu)


def _glu_bwd(y, wglu, dz, u):
    tt = 512

    def body(y_ref, w_ref, dz_ref, u_ref, dy_ref, ys_ref, da_ref, dd_ref):
        @pl.when(pl.program_id(0) == 0)
        def _():
            dd_ref[...] = jnp.zeros_like(dd_ref)

        yv = y_ref[...]
        t = jnp.tanh(_GELU_C * (yv + 0.044715 * (yv * yv * yv)))
        ys = 0.5 * yv * (1.0 + t)
        ysb = ys.astype(BF16)
        sg = jax.nn.sigmoid(jnp.dot(ysb, w_ref[...], preferred_element_type=F32))
        dz = dz_ref[...].astype(F32)
        da = (dz * ys * sg * (1.0 - sg)).astype(BF16)
        dys = dz * sg + lax.dot_general(da, w_ref[...], NT, preferred_element_type=F32)
        dy = dys * (0.5 * (1.0 + t) + 0.5 * yv * (1.0 - t * t) * _GELU_C * (1.0 + 3 * 0.044715 * (yv * yv)))
        dy_ref[...] = dy.astype(BF16)
        ys_ref[...] = ysb
        da_ref[...] = da
        dd_ref[...] += jnp.sum(dy * u_ref[...], axis=0, keepdims=True)

    blk = pl.BlockSpec((tt, WIDTH), lambda i: (i, 0))
    return pl.pallas_call(
        body, grid=(SEQ // tt,), in_specs=[blk, _full((WIDTH, WIDTH)), blk, blk],
        out_specs=[blk, blk, blk, _full((1, WIDTH))],
        out_shape=[jax.ShapeDtypeStruct((SEQ, WIDTH), BF16)] * 3 + [jax.ShapeDtypeStruct((1, WIDTH), F32)],
        compiler_params=_cp(), name="glu_bwd")(y, wglu, dz, u)


def _mix_specs(tt, layer):
    row = lambda w: pl.BlockSpec((tt, w), lambda i: (i, 0))
    gate = lambda j: pl.BlockSpec((tt, D_MODEL), lambda i: (i, j))
    wo = lambda j: pl.BlockSpec((D_MODEL, WIDTH), lambda i: (0, j))
    return [row(D_MODEL), row(WIDTH), row(WIDTH), row(WIDTH), gate(0), gate(1), gate(2),
            pl.BlockSpec((None, 1, GATE_W), lambda i: (layer, 0, 0)), wo(0), wo(1), wo(2),
            _full((D_MODEL, D_MODEL))]


def _mix_branches(o_ref, c_ref, z_ref, g_refs, b_ref, wa_ref, wc_ref, ws_ref):
    ys = [lax.dot_general(r[...], w[...], NT, preferred_element_type=F32)
          for r, w in ((o_ref, wa_ref), (c_ref, wc_ref), (z_ref, ws_ref))]
    gates = [jax.nn.sigmoid(g_refs[j][...] + b_ref[:, D_MODEL * j:D_MODEL * (j + 1)]) for j in range(3)]
    return ys, gates


def _mix_fwd(x, o, cv, z, glog, b_gate, layer, wbt, wmix):
    tt = 256

    def body(x_ref, o_ref, c_ref, z_ref, g0, g1, g2, b_ref, wa_ref, wc_ref, ws_ref, wm_ref, x1_ref):
        ys, gates = _mix_branches(o_ref, c_ref, z_ref, (g0, g1, g2), b_ref, wa_ref, wc_ref, ws_ref)
        merged = gates[0] * ys[0] + gates[1] * ys[1] + gates[2] * ys[2]
        x1_ref[...] = x_ref[...] + jnp.dot(merged.astype(BF16), wm_ref[...], preferred_element_type=F32)

    return pl.pallas_call(
        body, grid=(SEQ // tt,), in_specs=_mix_specs(tt, layer),
        out_specs=pl.BlockSpec((tt, D_MODEL), lambda i: (i, 0)),
        out_shape=jax.ShapeDtypeStruct((SEQ, D_MODEL), F32), compiler_params=_cp(), name="mix_fwd",
    )(x, o, cv, z, glog, glog, glog, b_gate, wbt, wbt, wbt, wmix)


def _mix_bwd(dx1, o, cv, z, glog, b_gate, layer, wbt, wmix, tie):
    tt = 256

    def body(dx_ref, o_ref, c_ref, z_ref, g0, g1, g2, b_ref, wa_ref, wc_ref, ws_ref, wm_ref, tie_ref,
             mg_ref, dya_ref, dyc_ref, dys_ref, do_ref, dc_ref, dz_ref, dgl_ref, db_ref):
        @pl.when(pl.program_id(0) == 0)
        def _():
            db_ref[...] = jnp.zeros_like(db_ref)

        ys, gates = _mix_branches(o_ref, c_ref, z_ref, (g0, g1, g2), b_ref, wa_ref, wc_ref, ws_ref)
        mg_ref[...] = (gates[0] * ys[0] + gates[1] * ys[1] + gates[2] * ys[2]).astype(BF16)
        dm = lax.dot_general(dx_ref[...].astype(BF16), wm_ref[...], NT, preferred_element_type=F32)
        for j, (dy_ref, w_ref, d_ref) in enumerate(((dya_ref, wa_ref, do_ref), (dyc_ref, wc_ref, dc_ref),
                                                    (dys_ref, ws_ref, dz_ref))):
            dy = (dm * gates[j]).astype(BF16)
            dy_ref[...] = dy
            d_ref[...] = jnp.dot(dy, w_ref[...], preferred_element_type=F32)
            dgl = dm * ys[j] * gates[j] * (1.0 - gates[j])
            dgl_ref[:, D_MODEL * j:D_MODEL * (j + 1)] = dgl.astype(BF16)
            db_ref[:, D_MODEL * j:D_MODEL * (j + 1)] += jnp.sum(dgl, axis=0, keepdims=True)

    row = lambda w: pl.BlockSpec((tt, w), lambda i: (i, 0))
    sds = jax.ShapeDtypeStruct
    return pl.pallas_call(
        body, grid=(SEQ // tt,), in_specs=_mix_specs(tt, layer) + [ANY],
        out_specs=[row(D_MODEL)] * 4 + [row(WIDTH)] * 3 + [row(GATE_W), _full((1, GATE_W))],
        out_shape=[sds((SEQ, D_MODEL), BF16)] * 4 + [sds((SEQ, WIDTH), F32)] * 3
        + [sds((SEQ, GATE_W), BF16), sds((1, GATE_W), F32)],
        compiler_params=_cp(), name="mix_bwd",
    )(dx1, o, cv, z, glog, glog, glog, b_gate, wbt, wbt, wbt, wmix, tie)


def _ffn_out_fwd(x1, gu, wout, tie):
    tt = 256

    def body(x_ref, gt_ref, up_ref, w_ref, tie_ref, o_ref):
        gt = gt_ref[...]
        act = (gt * jax.nn.sigmoid(gt) * up_ref[...]).astype(BF16)
        o_ref[...] = x_ref[...] + jnp.dot(act, w_ref[...], preferred_element_type=F32)

    return pl.pallas_call(
        body, grid=(SEQ // tt,),
        in_specs=[pl.BlockSpec((tt, D_MODEL), lambda i: (i, 0)), pl.BlockSpec((tt, FFN_H), lambda i: (i, 0)),
                  pl.BlockSpec((tt, FFN_H), lambda i: (i, 1)), _full((FFN_H, D_MODEL)), ANY],
        out_specs=pl.BlockSpec((tt, D_MODEL), lambda i: (i, 0)),
        out_shape=jax.ShapeDtypeStruct((SEQ, D_MODEL), F32), compiler_params=_cp(), name="ffn_out_fwd",
    )(x1, gu, gu, wout, tie)


def _ffn_out_bwd(dx2, gu, wout, tie):
    tt = 256

    def body(dx_ref, gt_ref, up_ref, w_ref, tie_ref, dgu_ref, act_ref):
        gt, up = gt_ref[...], up_ref[...]
        sg = jax.nn.sigmoid(gt)
        silu = gt * sg
        act_ref[...] = (silu * up).astype(BF16)
        dact = lax.dot_general(dx_ref[...].astype(BF16), w_ref[...], NT, preferred_element_type=F32)
        dgu_ref[:, :FFN_H] = (dact * up * (sg * (1.0 + gt * (1.0 - sg)))).astype(BF16)
        dgu_ref[:, FFN_H:] = (dact * silu).astype(BF16)

    return pl.pallas_call(
        body, grid=(SEQ // tt,),
        in_specs=[pl.BlockSpec((tt, D_MODEL), lambda i: (i, 0)), pl.BlockSpec((tt, FFN_H), lambda i: (i, 0)),
                  pl.BlockSpec((tt, FFN_H), lambda i: (i, 1)), _full((FFN_H, D_MODEL)), ANY],
        out_specs=[pl.BlockSpec((tt, 2 * FFN_H), lambda i: (i, 0)), pl.BlockSpec((tt, FFN_H), lambda i: (i, 0))],
        out_shape=[jax.ShapeDtypeStruct((SEQ, 2 * FFN_H), BF16), jax.ShapeDtypeStruct((SEQ, FFN_H), BF16)],
        compiler_params=_cp(), name="ffn_out_bwd",
    )(dx2, gu, gu, wout, tie)


def _loss_head(x, g, target):
    tt = 256

    def body(x_ref, g_ref, t_ref, loss_ref, dx_ref, dg_ref):
        @pl.when(pl.program_id(0) == 0)
        def _():
            loss_ref[...] = jnp.zeros_like(loss_ref)
            dg_ref[...] = jnp.zeros_like(dg_ref)

        xv = x_ref[...]
        r = lax.rsqrt(jnp.mean(xv * xv, axis=-1, keepdims=True) + NORM_EPS)
        xh = xv * r
        err = xh * g_ref[...] - t_ref[...]
        loss_ref[...] += 0.5 * jnp.sum(jnp.mean(err * err, axis=-1, keepdims=True))
        dy = err * (1.0 / D_MODEL)
        gy = dy * g_ref[...]
        dx_ref[...] = r * (gy - xh * jnp.mean(gy * xh, axis=-1, keepdims=True))
        dg_ref[...] += jnp.sum(dy * xh, axis=0, keepdims=True)

    row = pl.BlockSpec((tt, D_MODEL), lambda i: (i, 0))
    return pl.pallas_call(
        body, grid=(SEQ // tt,), in_specs=[row, _full((1, D_MODEL)), row],
        out_specs=[_full((1, 128)), row, _full((1, D_MODEL))],
        out_shape=[jax.ShapeDtypeStruct((1, 128), F32), jax.ShapeDtypeStruct((SEQ, D_MODEL), F32),
                   jax.ShapeDtypeStruct((1, D_MODEL), F32)],
        compiler_params=_cp(), name="loss_head")(x, g, target)


def _adamw(parts, w, m, v, tr, name, groups=None, fill=None, tie=None):
    n_groups, rows, cols = w.shape
    n_parts = parts.shape[1]
    lo, hi = groups if groups is not None else (0, n_groups)

    def body(p_ref, w_ref, m_ref, v_ref, *rest):
        g_ref, d_ref, nm_ref, nv_ref = rest[-4:]
        g = p_ref[0].astype(F32)
        for k in range(1, n_parts):
            g = g + p_ref[k].astype(F32)
        nm = B1 * m_ref[...] + (1.0 - B1) * g
        nv = B2 * v_ref[...] + (1.0 - B2) * (g * g)
        m_hat = nm / (1.0 - B1 ** STEP)
        v_hat = nv / (1.0 - B2 ** STEP)
        g_ref[...] = g
        d_ref[...] = -LR * (m_hat / (jnp.sqrt(v_hat) + ADAM_EPS) + WD * w_ref[...])
        nm_ref[...] = nm
        nv_ref[...] = nv

    blk = pl.BlockSpec((None, tr, cols), lambda l, i: (l + lo, i, 0))
    p_lo = lo if parts.shape[0] == n_groups else 0
    extra = ([] if fill is None else list(fill)) + ([] if tie is None else [tie])
    return pl.pallas_call(
        body, grid=(hi - lo, rows // tr),
        in_specs=[pl.BlockSpec((None, n_parts, tr, cols), lambda l, i: (l + p_lo, 0, i, 0)), blk, blk, blk]
        + [ANY] * len(extra),
        out_specs=[blk] * 4, out_shape=[jax.ShapeDtypeStruct((n_groups, rows, cols), F32)] * 4,
        input_output_aliases={} if fill is None else {4 + j: j for j in range(4)},
        compiler_params=_cp(), name=name)(parts, w, m, v, *extra)


def _gather_body(n, pick):
    def body(*refs):
        srcs, outs, (send_sems, recv_sems, local_sems) = refs[:n], refs[n:2 * n], refs[2 * n:]
        x, y, c = lax.axis_index("x"), lax.axis_index("y"), lax.axis_index("c")
        me, sibling = (x, y, c), (x, y, 1 - c)
        chips = [(1 - x, y), (x, 1 - y), (1 - x, 1 - y)]

        def copy(i, k, block, to, src=None):
            slot = outs[i].at[4 * block[0] + 2 * block[1] + block[2]]
            return pltpu.make_async_remote_copy(
                src_ref=slot if src is None else src, dst_ref=slot,
                send_sem=send_sems.at[7 * i + k], recv_sem=recv_sems.at[7 * i + k],
                device_id=to, device_id_type=MESH_ID)

        mine = [pltpu.make_async_copy(pick(srcs[i]), outs[i].at[4 * x + 2 * y + c], local_sems.at[i])
                for i in range(n)]
        for cp in mine:
            cp.start()
        first = []
        for i in range(n):
            first.append(copy(i, 0, me, sibling, src=pick(srcs[i])))
            first += [copy(i, 1 + j, me, (*chip, c), src=pick(srcs[i])) for j, chip in enumerate(chips)]
        for cp in first:
            cp.start()
        passed = []
        for j, chip in enumerate(chips):
            for i in range(n):
                copy(i, 1 + j, (*chip, c), me).wait_recv()
                passed.append(copy(i, 4 + j, (*chip, c), sibling))
                passed[-1].start()
        for i in range(n):
            copy(i, 0, sibling, me).wait_recv()
            for j, chip in enumerate(chips):
                copy(i, 4 + j, (*chip, 1 - c), me).wait_recv()
        for cp in first + passed:
            cp.wait_send()
        for cp in mine:
            cp.wait()

    return body


def _gather_scratch(n):
    return [pltpu.SemaphoreType.DMA((7 * n,)), pltpu.SemaphoreType.DMA((7 * n,)), pltpu.SemaphoreType.DMA((n,))]


def _all_gather(shard, name):
    return pl.pallas_call(
        _gather_body(1, lambda ref: ref), in_specs=[ANY], out_specs=[ANY],
        out_shape=[jax.ShapeDtypeStruct((N_DEV,) + shard.shape, shard.dtype)],
        scratch_shapes=_gather_scratch(1), name=name)(shard)[0]


def _split_start(name, arrays, n_sems, plan, after=None):
    n = len(arrays)
    order = [] if after is None else [after]
    n_in = n + len(order)

    def body(*refs):
        ins, send_sems, recv_sems, token = refs[:n], refs[n_in], refs[n_in + 1], refs[-1]
        for src, dst, k, to in plan(ins)[0]:
            pltpu.make_async_remote_copy(src_ref=src, dst_ref=dst, send_sem=send_sems.at[k], recv_sem=recv_sems.at[k],
                                         device_id=to, device_id_type=MESH_ID).start()
        token[...] = jnp.zeros_like(token)

    outs = pl.pallas_call(
        body, name=name,
        out_shape=(pltpu.SemaphoreType.DMA((n_sems,)), pltpu.SemaphoreType.DMA((n_sems,)),
                   *[pltpu.HBM(a.shape, a.dtype) for a in arrays], jax.ShapeDtypeStruct((8, 128), F32)),
        in_specs=[HBM] * n + [ANY] * len(order),
        out_specs=(SEM, SEM, *[HBM] * n, pl.BlockSpec(memory_space=pltpu.VMEM)),
        input_output_aliases={i: 2 + i for i in range(n)},
        compiler_params=pltpu.CompilerParams(has_side_effects=EFFECT),
    )(*[pltpu.with_memory_space_constraint(a, pltpu.HBM) for a in arrays], *order)
    return outs[0], outs[1], list(outs[2:2 + n]), outs[-1]


def _split_wait(name, arrays, send_sems, recv_sems, after, plan):
    n = len(arrays)

    def body(*refs):
        ins, s_sems, r_sems = refs[:n], refs[n], refs[n + 1]
        sends, arrivals = plan(ins)
        x, y, c = lax.axis_index("x"), lax.axis_index("y"), lax.axis_index("c")
        for src, dst, k, to in sends:
            pltpu.make_async_remote_copy(src_ref=src, dst_ref=dst, send_sem=s_sems.at[k], recv_sem=r_sems.at[k],
                                         device_id=to, device_id_type=MESH_ID).wait_send()
        for dst, k in arrivals:
            pltpu.make_async_remote_copy(src_ref=dst, dst_ref=dst, send_sem=s_sems.at[k], recv_sem=r_sems.at[k],
                                         device_id=(x, y, c), device_id_type=MESH_ID).wait_recv()

    return pl.pallas_call(
        body, name=name, out_shape=[pltpu.HBM(a.shape, a.dtype) for a in arrays],
        in_specs=[HBM] * n + [SEM, SEM, ANY], out_specs=[HBM] * n,
        input_output_aliases={i: i for i in range(n)},
        compiler_params=pltpu.CompilerParams(has_side_effects=EFFECT),
    )(*arrays, send_sems, recv_sems, after)


def _chips():
    x, y, c = lax.axis_index("x"), lax.axis_index("y"), lax.axis_index("c")
    return x, y, c, [(1 - x, y), (x, 1 - y), (1 - x, 1 - y)]


N_KINDS = len(KINDS)


def _plan_gather_chips(refs):
    x, y, c, chips = _chips()
    me = 4 * x + 2 * y + c
    sends, arrivals = [], []
    for i in range(N_KINDS):
        src, land = refs[i], refs[N_KINDS + i]
        sends.append((src, land.at[me], 4 * i, (x, y, 1 - c)))
        arrivals.append((land.at[4 * x + 2 * y + 1 - c], 4 * i))
        for j, (px, py) in enumerate(chips):
            sends.append((src, land.at[me], 4 * i + 1 + j, (px, py, c)))
            arrivals.append((land.at[4 * px + 2 * py + c], 4 * i + 1 + j))
    return sends, arrivals


def _plan_gather_pass(refs):
    x, y, c, chips = _chips()
    sends, arrivals = [], []
    for i in range(N_KINDS):
        for j, (px, py) in enumerate(chips):
            slot = refs[i].at[4 * px + 2 * py + c]
            sends.append((slot, slot, 3 * i + j, (x, y, 1 - c)))
            arrivals.append((refs[i].at[4 * px + 2 * py + 1 - c], 3 * i + j))
    return sends, arrivals


def _plan_scatter_pair(refs):
    x, y, c = lax.axis_index("x"), lax.axis_index("y"), lax.axis_index("c")
    sends, arrivals = [], []
    for i in range(N_KINDS):
        for q in range(4):
            sends.append((refs[i].at[q, 1 - c], refs[N_KINDS + i].at[q], 4 * i + q, (x, y, 1 - c)))
            arrivals.append((refs[N_KINDS + i].at[q], 4 * i + q))
    return sends, arrivals


def _plan_scatter_chips(layer):
    def plan(refs):
        x, y, c, chips = _chips()
        sends, arrivals = [], []
        for i in range(N_KINDS):
            for j, (px, py) in enumerate(chips):
                sends.append((refs[i].at[2 * px + py], refs[N_KINDS + i].at[layer, 2 * x + y], 3 * i + j, (px, py, c)))
                arrivals.append((refs[N_KINDS + i].at[layer, 2 * px + py], 3 * i + j))
        return sends, arrivals

    return plan


def _pair_sum(parts4, from_pair, landing, layer, core, tr, name):
    _, _, rows, cols = parts4.shape

    def body(c_ref, p_ref, s_ref, l_ref, sum_ref, land_ref):
        v = (p_ref[...].astype(F32) + s_ref[...].astype(F32)).astype(BF16)
        sum_ref[...] = v
        land_ref[...] = v

    blk = pl.BlockSpec((None, tr, cols), lambda q, i, c_ref: (q, i, 0))
    return pl.pallas_call(
        body,
        grid_spec=pltpu.PrefetchScalarGridSpec(
            num_scalar_prefetch=1, grid=(4, rows // tr),
            in_specs=[pl.BlockSpec((None, None, tr, cols), lambda q, i, c_ref: (q, c_ref[0], i, 0)), blk, ANY],
            out_specs=[blk, pl.BlockSpec((None, None, tr, cols), lambda q, i, c_ref: (layer, q, i, 0))]),
        out_shape=[jax.ShapeDtypeStruct((4, rows, cols), BF16), jax.ShapeDtypeStruct(landing.shape, BF16)],
        input_output_aliases={3: 1}, compiler_params=_cp(), name=name,
    )(core, parts4, from_pair, landing)


def _travel_layout(t):
    tr = lambda a: jnp.swapaxes(a, 1, 2)
    branch = jnp.concatenate([tr(t["w_attn_o"]), tr(t["w_conv_o"]), tr(t["w_ssm_o"])], axis=2)
    return [tr(t["w_in"]), tr(t["w_ffn_in"]), t["w_ffn_out"], t["w_mix_o"], branch, t["w_ssm_glu"]]


def _native_layout(a):
    tr = lambda x: jnp.swapaxes(x, 1, 2)
    b = a[4]
    return {"w_in": tr(a[0]), "w_ffn_in": tr(a[1]), "w_ffn_out": a[2], "w_mix_o": a[3],
            "w_attn_o": tr(b[:, :, :WIDTH]), "w_conv_o": tr(b[:, :, WIDTH:2 * WIDTH]),
            "w_ssm_o": tr(b[:, :, 2 * WIDTH:]), "w_ssm_glu": a[5]}


def _embed(t):
    eye = jnp.eye(SSM_GROUPS, dtype=t.dtype)
    return (t[:, :, :, None, :] * eye[None, :, None, :, None]).reshape(DEPTH, WIDTH, SSM_GROUPS * SSM_STATE)


def _diag_blocks(t):
    t = t.reshape(DEPTH, SSM_GROUPS, SSM_STATE, SSM_GROUPS, SSM_GROUP)
    return jnp.einsum("lgpgh->lghp", t)


def _rope_tabs():
    pos = jnp.arange(SEQ, dtype=F32)
    inv_freq = ROPE_THETA ** (-jnp.arange(0, ROT_DIM, 2, dtype=F32) / ROT_DIM)
    ang = pos[:, None] * inv_freq[None, :]
    cos, sin = jnp.cos(ang), jnp.sin(ang)
    one, zero = jnp.ones((SEQ, HEAD_DIM - ROT_DIM), F32), jnp.zeros((SEQ, HEAD_DIM - ROT_DIM), F32)
    z8 = jnp.zeros((SEQ, 8), F32)
    head = lambda *p: jnp.tile(jnp.concatenate(p, axis=1), (1, 2))
    return head(cos, cos, one), head(-sin, z8, zero), head(z8, sin, zero)


def _ssm_mats(sp):
    lr, li, bbr, bbi = _ssm_prep(sp["a_re"], sp["a_im"], sp["log_dt"], sp["bt_re"], sp["bt_im"])
    b_re, b_im = _embed(bbr), _embed(bbi)
    c_re, c_im = _embed(sp["c_re"]), _embed(sp["c_im"])
    slab3 = lambda t: jnp.swapaxes(t, 1, 2).reshape(DEPTH, SLABS, 128, WIDTH).astype(BF16)
    return {
        "a_re": lr.reshape(DEPTH, SLABS, 1, 128), "a_im": li.reshape(DEPTH, SLABS, 1, 128),
        "b_re": b_re.astype(BF16), "b_im": b_im.astype(BF16),
        "c_re": c_re.astype(BF16), "c_im_neg": (-c_im).astype(BF16),
        "bt_re": slab3(b_re), "bt_im": slab3(b_im), "ct_re": slab3(c_re), "ct_im_neg": slab3(-c_im),
    }


def _layer_fwd(x, i, w, rp, mats, tabs, tie, mid):
    q, kv, cbx, u, u16, glog, h = _rms_mm_in(x, rp["norm_mix"][i], w["win_t"], tie)
    o = _attn_fwd(q, kv, tabs, rp["attn_sinks"][i])
    cv = _conv_fwd(cbx, rp["conv_w"], i)
    bu_re, bu_im = _slab_mm(u16, mats["b_re"], mats["b_im"], i, "slab_mm")
    x_re, x_im = _scan(bu_re, bu_im, mats["a_re"], mats["a_im"], i, False, "scan_fwd")
    y = _slab_contract(x_re, x_im, mats["ct_re"], mats["ct_im_neg"], i, u, rp["ssm_d"], "slab_contract_y")
    z = _glu_fwd(y, w["wglu"])
    x1 = _mix_fwd(x, o, cv, z, glog, rp["b_gate"], i, w["branch_t"], w["wmix"])
    gu, h2 = _rms_mm_ffn(x1, rp["norm_ffn"][i], w["wffn_t"])
    x2 = _ffn_out_fwd(x1, gu, w["wout"], mid(h2))
    kept = dict(x=x, q=q, kv=kv, cbx=cbx, u=u, u16=u16, glog=glog, h=h, o=o, cv=cv, z=z, y=y,
                x_re=x_re, x_im=x_im, x1=x1, gu=gu, h2=h2)
    return x2, kept


def _layer_bwd(dx2, k, i, w, rp, mats, tabs, tie, mid):
    tn = dict(ta=True, out_dtype=BF16)
    dgu, act = _ffn_out_bwd(dx2, k["gu"], w["wout"], tie)
    g_wout = _mm(act, dx2, tm=FFN_H // 2, tn=1024, tk=512, name="mm_tn_ffn_out", **tn)
    g_wffn_t = _mm(dgu, k["h2"], tm=FFN_H // 2, tn=1024, tk=512, name="mm_tn_ffn_in", **tn)
    dx1, d_norm_ffn = _mm_rmsbwd([dgu], w["wffn_t"], k["x1"], rp["norm_ffn"][i], dx2, "mm_rmsbwd_ffn")

    mg, dya, dyc, dys, do, dcv, dz, dgl, db_gate = _mix_bwd(
        dx1, k["o"], k["cv"], k["z"], k["glog"], rp["b_gate"], i, w["branch_t"], w["wmix"], mid(d_norm_ffn))
    g_wmix = _mm(mg, dx1, tm=1024, tn=1024, tk=512, name="mm_tn_mix", **tn)
    g_branch_t = _tn_branches((dya, dyc, dys), (k["o"], k["cv"], k["z"]))

    dy16, ys16, da16, dd = _glu_bwd(k["y"], w["wglu"], dz, k["u"])
    g_wglu = _mm(ys16, da16, tm=512, tn=512, tk=512, name="mm_tn_glu", **tn)
    gx_re, gx_im = _slab_mm(dy16, mats["c_re"], mats["c_im_neg"], i, "slab_mm")
    l_re, l_im = _scan(gx_re, gx_im, mats["a_re"], mats["a_im"], i, True, "scan_bwd")
    du = _slab_contract(l_re, l_im, mats["bt_re"], mats["bt_im"], i, dy16, rp["ssm_d"], "slab_contract_du")
    da_re, da_im = _state_grad(k["x_re"], k["x_im"], l_re, l_im)
    db_re, db_im = _slab_tn(l_re, l_im, k["u16"], "slab_tn")
    dc_re, dc_im = _slab_tn(k["x_re"], k["x_im"], dy16, "slab_tn")

    dcb, dcc, dcx, d_conv_w = _conv_bwd(k["cbx"], rp["conv_w"], i, dcv)
    dq, dkv, d_sinks = _attn_bwd(k["q"], k["kv"], tabs, rp["attn_sinks"][i], do)

    pieces = [dq, dkv, dcb, dcc, dcx, du.astype(BF16), dgl]
    g_win_t = _tn_pieces(pieces, k["h"])
    dx, d_norm_mix = _mm_rmsbwd(pieces, w["win_t"], k["x"], rp["norm_mix"][i], dx1, "mm_rmsbwd_in")

    grads = [g_win_t, g_wffn_t, g_wout, g_wmix, g_branch_t, g_wglu]
    small = dict(norm_mix=d_norm_mix, b_gate=db_gate, attn_sinks=d_sinks, ssm_d=dd, norm_ffn=d_norm_ffn,
                 conv_w=d_conv_w, da_re=da_re, da_im=da_im, db_re=db_re, db_im=db_im, dc_re=dc_re, dc_im=dc_im)
    return dx, grads, small


def _replicated_grads(sg, sp):
    stack = lambda name: jnp.stack([sg[i][name] for i in range(DEPTH)])
    cots = (stack("da_re").reshape(DEPTH, *_GS), stack("da_im").reshape(DEPTH, *_GS),
            _diag_blocks(stack("db_re")), _diag_blocks(stack("db_im")))
    d_a_re, d_a_im, d_log_dt, d_bt_re, d_bt_im = _ssm_prep_bwd(
        sp["a_re"], sp["a_im"], sp["log_dt"], sp["bt_re"], sp["bt_im"], cots)
    sgrads = {"norm_mix": stack("norm_mix"), "b_gate": stack("b_gate"),
              "attn_sinks": stack("attn_sinks")[:, :, :N_Q_HEADS], "ssm_a_re": d_a_re, "ssm_a_im": d_a_im,
              "ssm_b_re": jnp.swapaxes(d_bt_re, 2, 3), "ssm_b_im": jnp.swapaxes(d_bt_im, 2, 3),
              "ssm_c_re": _diag_blocks(stack("dc_re")), "ssm_c_im": -_diag_blocks(stack("dc_im")),
              "ssm_d": stack("ssm_d"), "ssm_log_dt": d_log_dt, "norm_ffn": stack("norm_ffn")}
    return sgrads, stack("conv_w")[:, :3]


def kernel(x, norm_mix, w_in, b_gate, attn_sinks, w_attn_o, conv_w, w_conv_o, ssm_a_re, ssm_a_im, ssm_b_re, ssm_b_im, ssm_c_re, ssm_c_im, ssm_d, ssm_log_dt, w_ssm_glu, w_ssm_o, w_mix_o, norm_ffn, w_ffn_in, w_ffn_out, norm_final, loss_target, m_norm_mix, m_w_in, m_b_gate, m_attn_sinks, m_w_attn_o, m_conv_w, m_w_conv_o, m_ssm_a_re, m_ssm_a_im, m_ssm_b_re, m_ssm_b_im, m_ssm_c_re, m_ssm_c_im, m_ssm_d, m_ssm_log_dt, m_w_ssm_glu, m_w_ssm_o, m_w_mix_o, m_norm_ffn, m_w_ffn_in, m_w_ffn_out, m_norm_final, v_norm_mix, v_w_in, v_b_gate, v_attn_sinks, v_w_attn_o, v_conv_w, v_w_conv_o, v_ssm_a_re, v_ssm_a_im, v_ssm_b_re, v_ssm_b_im, v_ssm_c_re, v_ssm_c_im, v_ssm_d, v_ssm_log_dt, v_w_ssm_glu, v_w_ssm_o, v_w_mix_o, v_norm_ffn, v_w_ffn_in, v_w_ffn_out, v_norm_final):
    big = {"w": dict(w_in=w_in, w_attn_o=w_attn_o, w_conv_o=w_conv_o, w_ssm_glu=w_ssm_glu, w_ssm_o=w_ssm_o,
                     w_mix_o=w_mix_o, w_ffn_in=w_ffn_in, w_ffn_out=w_ffn_out),
           "m": dict(w_in=m_w_in, w_attn_o=m_w_attn_o, w_conv_o=m_w_conv_o, w_ssm_glu=m_w_ssm_glu,
                     w_ssm_o=m_w_ssm_o, w_mix_o=m_w_mix_o, w_ffn_in=m_w_ffn_in, w_ffn_out=m_w_ffn_out),
           "v": dict(w_in=v_w_in, w_attn_o=v_w_attn_o, w_conv_o=v_w_conv_o, w_ssm_glu=v_w_ssm_glu,
                     w_ssm_o=v_w_ssm_o, w_mix_o=v_w_mix_o, w_ffn_in=v_w_ffn_in, w_ffn_out=v_w_ffn_out)}
    small = {"w": dict(norm_mix=norm_mix, b_gate=b_gate, attn_sinks=attn_sinks, ssm_a_re=ssm_a_re,
                       ssm_a_im=ssm_a_im, ssm_b_re=ssm_b_re, ssm_b_im=ssm_b_im, ssm_c_re=ssm_c_re,
                       ssm_c_im=ssm_c_im, ssm_d=ssm_d, ssm_log_dt=ssm_log_dt, norm_ffn=norm_ffn),
             "m": dict(norm_mix=m_norm_mix, b_gate=m_b_gate, attn_sinks=m_attn_sinks, ssm_a_re=m_ssm_a_re,
                       ssm_a_im=m_ssm_a_im, ssm_b_re=m_ssm_b_re, ssm_b_im=m_ssm_b_im, ssm_c_re=m_ssm_c_re,
                       ssm_c_im=m_ssm_c_im, ssm_d=m_ssm_d, ssm_log_dt=m_ssm_log_dt, norm_ffn=m_norm_ffn),
             "v": dict(norm_mix=v_norm_mix, b_gate=v_b_gate, attn_sinks=v_attn_sinks, ssm_a_re=v_ssm_a_re,
                       ssm_a_im=v_ssm_a_im, ssm_b_re=v_ssm_b_re, ssm_b_im=v_ssm_b_im, ssm_c_re=v_ssm_c_re,
                       ssm_c_im=v_ssm_c_im, ssm_d=v_ssm_d, ssm_log_dt=v_ssm_log_dt, norm_ffn=v_norm_ffn)}
    finals = {"w": norm_final, "m": m_norm_final, "v": v_norm_final}
    convs = {"w": conv_w, "m": m_conv_w, "v": v_conv_w}
    mine = 4 * lax.axis_index("x") + 2 * lax.axis_index("y") + lax.axis_index("c")

    travel = {s: _travel_layout(big[s]) for s in "wmv"}
    stacked16 = [a.astype(BF16) for a in travel["w"]]
    conv_all = _all_gather(jnp.pad(conv_w.reshape(6, 128), ((0, 2), (0, 0))), "gather_conv_w")
    conv_full = conv_all[:, :6].reshape(N_DEV, DEPTH, 3, 64).transpose(1, 2, 0, 3).reshape(DEPTH, 3, WIDTH)
    rp = {"norm_mix": norm_mix[:, None], "norm_ffn": norm_ffn[:, None], "attn_sinks": attn_sinks[:, None],
          "b_gate": b_gate[:, None], "ssm_d": ssm_d[:, None], "conv_w": jnp.pad(conv_full, ((0, 0), (0, 5), (0, 0)))}
    sp = {"a_re": ssm_a_re, "a_im": ssm_a_im, "log_dt": ssm_log_dt[:, :, None],
          "bt_re": jnp.swapaxes(ssm_b_re, 2, 3), "bt_im": jnp.swapaxes(ssm_b_im, 2, 3),
          "c_re": ssm_c_re, "c_im": ssm_c_im}
    rows_tile = {"win_t": 368, "wffn_t": 352, "wout": 176, "wmix": 128, "branch_t": 128, "wglu": 64}
    core = lax.axis_index("c").astype(jnp.int32).reshape(1)
    no_tie = jnp.zeros((8, 128), F32)

    def gather_chips(i, after):
        srcs = [a[i] for a in stacked16]
        lands = [lax.dynamic_update_slice(lax.empty((N_DEV, r, c), BF16), s[None], (mine, 0, 0))
                 for s, (_, r, c) in zip(srcs, KINDS)]
        s_sems, r_sems, arrays, token = _split_start(
            f"gather_chips_start_{i}", srcs + lands, 4 * N_KINDS, _plan_gather_chips, after)
        return (s_sems, r_sems, arrays), token

    def gather_pass(i, state, after):
        arrays = _split_wait(f"gather_chips_wait_{i}", state[2], state[0], state[1], after, _plan_gather_chips)
        s_sems, r_sems, lands, token = _split_start(
            f"gather_pass_start_{i}", arrays[N_KINDS:], 3 * N_KINDS, _plan_gather_pass)
        return (s_sems, r_sems, lands), token

    def gather_done(i, state, after):
        lands = _split_wait(f"gather_pass_wait_{i}", state[2], state[0], state[1], after, _plan_gather_pass)
        return {name: a.reshape(N_DEV * r, c) for a, (name, r, c) in zip(lands, KINDS)}

    state, _ = gather_chips(0, None)
    mats = _ssm_mats(sp)
    tabs = _rope_tabs()
    state, _ = gather_pass(0, state, mats["ct_re"])
    w_next = gather_done(0, state, tabs[2])

    act = x[0]
    weights, kept = [], []
    for i in range(DEPTH):
        w_i = w_next
        if i + 1 < DEPTH:
            state, tie = gather_chips(i + 1, w_i["win_t"])
            held = {}

            def mid(value, i=i, state=state, held=held):
                held["state"], token = gather_pass(i + 1, state, value)
                return token
        else:
            tie, mid = no_tie, (lambda value: no_tie)
        act, k = _layer_fwd(act, i, w_i, rp, mats, tabs, tie, mid)
        if i + 1 < DEPTH:
            w_next = gather_done(i + 1, held["state"], act)
        weights.append(w_i)
        kept.append(k)
    loss_row, dx, d_norm_final = _loss_head(act, norm_final[None], loss_target[0])
    loss = lax.psum(loss_row[0, 0], ("x", "y", "c"))

    landings = [lax.empty((DEPTH, 4, r, c), BF16) for _, r, c in KINDS]
    landings0 = [lax.empty((1, 4, r, c), BF16) for _, r, c in KINDS]

    def scatter_pair(i, grads, after):
        parts4 = [g.reshape(4, 2, r, c) for g, (_, r, c) in zip(grads, KINDS)]
        zones = [lax.empty((4, r, c), BF16) for _, r, c in KINDS]
        s_sems, r_sems, arrays, token = _split_start(
            f"scatter_pair_start_{i}", parts4 + zones, 4 * N_KINDS, _plan_scatter_pair, after)
        return (i, s_sems, r_sems, arrays), token

    def scatter_chips(state, lands, after):
        i, s_sems, r_sems, arrays = state
        arrays = _split_wait(f"scatter_pair_wait_{i}", arrays, s_sems, r_sems, after, _plan_scatter_pair)
        slot = i if i > 0 else 0
        sums, lands = [], list(lands)
        for j, (name, _, _) in enumerate(KINDS):
            chip_sum, lands[j] = _pair_sum(arrays[j], arrays[N_KINDS + j], lands[j], slot, core, rows_tile[name],
                                           f"pair_sum_{name}")
            sums.append(chip_sum)
        s_sems, r_sems, arrays, token = _split_start(
            f"scatter_chips_start_{i}", sums + lands, 3 * N_KINDS, _plan_scatter_chips(slot))
        return (i, s_sems, r_sems, arrays), token

    def scatter_done(state, after):
        i, s_sems, r_sems, arrays = state
        slot = i if i > 0 else 0
        arrays = _split_wait(f"scatter_chips_wait_{i}", arrays, s_sems, r_sems, after, _plan_scatter_chips(slot))
        return list(arrays[N_KINDS:])

    sg = [None] * DEPTH
    pending, tie = None, no_tie
    for i in reversed(range(DEPTH)):
        held = {}
        if pending is None:
            mid = lambda value: no_tie
        else:
            def mid(value, pending=pending, held=held):
                held["state"], token = scatter_chips(pending, landings, value)
                return token
        dx, grads, sg[i] = _layer_bwd(dx, kept[i], i, weights[i], rp, mats, tabs, tie, mid)
        if pending is not None:
            landings = scatter_done(held["state"], dx)
        pending, tie = scatter_pair(i, grads, dx)

    sgrads, conv_grad = _replicated_grads(sg, sp)
    last, tie = scatter_chips(pending, landings0, sgrads["ssm_a_re"])

    big_out = [_adamw(landings[j], travel["w"][j], travel["m"][j], travel["v"][j], rows_tile[name],
                      "adamw_late_" + name, groups=(1, DEPTH), tie=tie) for j, (name, _, _) in enumerate(KINDS)]
    landings0 = scatter_done(last, big_out[-1][0])
    big_out = [_adamw(landings0[j], travel["w"][j], travel["m"][j], travel["v"][j], rows_tile[name],
                      "adamw_first_" + name, groups=(0, 1), fill=big_out[j]) for j, (name, _, _) in enumerate(KINDS)]
    big_res = [_native_layout([big_out[j][kind] for j in range(len(KINDS))]) for kind in range(4)]

    def pack_small(t, final, conv):
        flat = [t[name].reshape(DEPTH, n) for name, n in SMALL]
        flat = jnp.concatenate([jnp.concatenate(flat, axis=1).reshape(-1), final.reshape(-1), conv.reshape(-1)])
        return jnp.pad(flat, (0, SMALL_ROWS * 128 - flat.shape[0])).reshape(SMALL_ROWS, 128)

    zeros_conv = jnp.zeros((CONV_N,), F32)
    sparts = _all_gather(pack_small(sgrads, d_norm_final, conv_grad), "gather_small_grads")
    sw, sm_, sv = (pack_small(small[s], finals[s], zeros_conv) for s in "wmv")
    small_out = _adamw(sparts[None], sw[None], sm_[None], sv[None], SMALL_ROWS // 8, "adamw_replicated")

    def unpack_small(p):
        flat = p.reshape(-1)
        per = flat[:DEPTH * SMALL_PER_LAYER].reshape(DEPTH, SMALL_PER_LAYER)
        out, off = {}, 0
        for name, n in SMALL:
            out[name] = per[:, off:off + n].reshape(small["w"][name].shape)
            off += n
        out["norm_final"] = flat[DEPTH * SMALL_PER_LAYER:DEPTH * SMALL_PER_LAYER + D_MODEL]
        return out

    small_res = [unpack_small(p) for p in small_out]

    conv_off = DEPTH * SMALL_PER_LAYER + D_MODEL
    conv_parts = sparts.reshape(N_DEV, -1)[:, conv_off:conv_off + CONV_N].reshape(N_DEV, DEPTH * 3, WIDTH)
    conv_parts = lax.dynamic_slice_in_dim(conv_parts, mine * 64, 64, axis=2)
    conv_res = _adamw(conv_parts[None], *(convs[s].reshape(1, DEPTH * 3, 64) for s in "wmv"), DEPTH * 3, "adamw_conv_w")

    order = ["norm_mix", "w_in", "b_gate", "attn_sinks", "w_attn_o", "conv_w", "w_conv_o", "ssm_a_re", "ssm_a_im",
             "ssm_b_re", "ssm_b_im", "ssm_c_re", "ssm_c_im", "ssm_d", "ssm_log_dt", "w_ssm_glu", "w_ssm_o",
             "w_mix_o", "norm_ffn", "w_ffn_in", "w_ffn_out", "norm_final"]
    outs = [loss, dx[None]]
    for kind in range(4):
        for name in order:
            if name == "conv_w":
                outs.append(conv_res[kind].reshape(DEPTH, 3, 64))
            elif name in big_res[kind]:
                outs.append(big_res[kind][name])
            else:
                outs.append(small_res[kind][name])
    return tuple(outs)
```

```python
import functools
import math

import jax
import jax.numpy as jnp
from jax import lax
from jax.experimental import pallas as pl
from jax.experimental.pallas import tpu as pltpu

F32 = jnp.float32
BF16 = jnp.bfloat16

N_DEV = 8
DEPTH = 4
SEQ = 2048
D_MODEL = 1024
N_Q_HEADS = 8
HEAD_DIM = 64
ATTN_W = 512
KV_W = 128
BLOCK = 128
N_BLOCKS = SEQ // BLOCK
ROPE_THETA = 500000.0
ROT_DIM = 16
NEG_INF = -1e30
WIDTH = 512
SSM_GROUPS = 32
SSM_GROUP = 16
SSM_STATE = 64
SLABS = 16
CHUNK = 256
N_CHUNKS = SEQ // CHUNK
GATE_W = 3 * D_MODEL
IN_COLS = 5888
FFN_H = 2816
NORM_EPS = 1e-6
LR, B1, B2, ADAM_EPS, WD, STEP = 0.001, 0.9, 0.999, 1e-08, 0.01, 10

COL_Q, COL_KV, COL_CBX, COL_U, COL_G = 0, 512, 768, 2304, 2816
PIECE_W = (512, 256, 512, 512, 512, 512, 3072)
PIECE_OFF = tuple(sum(PIECE_W[:i]) for i in range(len(PIECE_W)))

KINDS = (("win_t", 736, 1024), ("wffn_t", 704, 1024), ("wout", 352, 1024), ("wmix", 128, 1024),
         ("branch_t", 128, 1536), ("wglu", 64, 512))

SMALL = (("norm_mix", 1024), ("b_gate", 3072), ("attn_sinks", 8), ("ssm_a_re", 2048), ("ssm_a_im", 2048),
         ("ssm_b_re", 32768), ("ssm_b_im", 32768), ("ssm_c_re", 32768), ("ssm_c_im", 32768),
         ("ssm_d", 512), ("ssm_log_dt", 32), ("norm_ffn", 1024))
SMALL_PER_LAYER = sum(n for _, n in SMALL)
CONV_N = DEPTH * 3 * WIDTH
SMALL_ROWS = 4480

VMEM_LIMIT = 56 * 1024 * 1024
NT = (((1,), (1,)), ((), ()))
TN = (((0,), (0,)), ((), ()))
MESH_ID = pl.DeviceIdType.MESH
ANY = pl.BlockSpec(memory_space=pl.ANY)
HBM = pl.BlockSpec(memory_space=pltpu.HBM)
SEM = pl.BlockSpec(memory_space=pltpu.SEMAPHORE)
EFFECT = pltpu.SideEffectType.DATAFLOW_SIDE_EFFECTING


def _cp(**kw):
    return pltpu.CompilerParams(vmem_limit_bytes=VMEM_LIMIT, **kw)


def _full(shape):
    return pl.BlockSpec(shape, lambda *_: (0,) * len(shape))


def _mm(a, b, *, ta=False, tb=False, tm, tn, tk, out_dtype=F32, name):
    m = a.shape[1] if ta else a.shape[0]
    k = a.shape[0] if ta else a.shape[1]
    n = b.shape[0] if tb else b.shape[1]
    nk = k // tk
    dims = (((0 if ta else 1,), (1 if tb else 0,)), ((), ()))

    def body(a_ref, b_ref, o_ref, acc_ref):
        kk = pl.program_id(2)

        @pl.when(kk == 0)
        def _():
            acc_ref[...] = jnp.zeros_like(acc_ref)

        acc_ref[...] += lax.dot_general(a_ref[...].astype(BF16), b_ref[...].astype(BF16), dims,
                                        preferred_element_type=F32)

        @pl.when(kk == nk - 1)
        def _():
            o_ref[...] = acc_ref[...].astype(out_dtype)

    a_spec = pl.BlockSpec((tk, tm), lambda i, j, kk: (kk, i)) if ta else pl.BlockSpec((tm, tk), lambda i, j, kk: (i, kk))
    b_spec = pl.BlockSpec((tn, tk), lambda i, j, kk: (j, kk)) if tb else pl.BlockSpec((tk, tn), lambda i, j, kk: (kk, j))
    return pl.pallas_call(
        body, grid=(m // tm, n // tn, nk), in_specs=[a_spec, b_spec],
        out_specs=pl.BlockSpec((tm, tn), lambda i, j, kk: (i, j)),
        out_shape=jax.ShapeDtypeStruct((m, n), out_dtype),
        scratch_shapes=[pltpu.VMEM((tm, tn), F32)], compiler_params=_cp(), name=name)(a, b)


def _rms_rows(xv, g):
    r = lax.rsqrt(jnp.mean(xv * xv, axis=-1, keepdims=True) + NORM_EPS)
    return ((xv * r) * g).astype(BF16)


def _rms_mm_in(x, g, wt, tie):
    tt = 256
    widths = (ATTN_W, 2 * KV_W, 3 * WIDTH, WIDTH, GATE_W)
    offs = (COL_Q, COL_KV, COL_CBX, COL_U, COL_G)

    def body(x_ref, g_ref, w_ref, tie_ref, q_ref, kv_ref, cbx_ref, u_ref, u16_ref, gl_ref, h_ref):
        h = _rms_rows(x_ref[...], g_ref[...])
        h_ref[...] = h
        prod = lax.dot_general(h, w_ref[...], NT, preferred_element_type=F32)
        for ref, o, w in zip((q_ref, kv_ref, cbx_ref, u_ref, gl_ref), offs, widths):
            ref[...] = prod[:, o:o + w]
        u16_ref[...] = prod[:, COL_U:COL_U + WIDTH].astype(BF16)

    row = lambda w: pl.BlockSpec((tt, w), lambda i: (i, 0))
    sds = jax.ShapeDtypeStruct
    return pl.pallas_call(
        body, grid=(SEQ // tt,), in_specs=[row(D_MODEL), _full((1, D_MODEL)), _full((IN_COLS, D_MODEL)), ANY],
        out_specs=[row(ATTN_W), row(2 * KV_W), row(3 * WIDTH), row(WIDTH), row(WIDTH), row(GATE_W), row(D_MODEL)],
        out_shape=[sds((SEQ, ATTN_W), F32), sds((SEQ, 2 * KV_W), F32), sds((SEQ, 3 * WIDTH), F32),
                   sds((SEQ, WIDTH), F32), sds((SEQ, WIDTH), BF16), sds((SEQ, GATE_W), F32),
                   sds((SEQ, D_MODEL), BF16)],
        compiler_params=_cp(), name="rms_mm_in")(x, g, wt, tie)


def _rms_mm_ffn(x, g, wt):
    tt = 256

    def body(x_ref, g_ref, w_ref, o_ref, h_ref):
        h = _rms_rows(x_ref[...], g_ref[...])
        h_ref[...] = h
        o_ref[...] = lax.dot_general(h, w_ref[...], NT, preferred_element_type=F32)

    row = lambda w: pl.BlockSpec((tt, w), lambda i: (i, 0))
    return pl.pallas_call(
        body, grid=(SEQ // tt,), in_specs=[row(D_MODEL), _full((1, D_MODEL)), _full((2 * FFN_H, D_MODEL))],
        out_specs=[row(2 * FFN_H), row(D_MODEL)],
        out_shape=[jax.ShapeDtypeStruct((SEQ, 2 * FFN_H), F32), jax.ShapeDtypeStruct((SEQ, D_MODEL), BF16)],
        compiler_params=_cp(), name="rms_mm_ffn")(x, g, wt)


def _mm_rmsbwd(pieces, wt, x, g, dres, name):
    tt = 256
    widths = [p.shape[1] for p in pieces]
    offs = [sum(widths[:i]) for i in range(len(widths))]
    n = len(pieces)

    def body(*refs):
        p_refs, (w_ref, x_ref, g_ref, r_ref, dx_ref, dg_ref) = refs[:n], refs[n:]

        @pl.when(pl.program_id(0) == 0)
        def _():
            dg_ref[...] = jnp.zeros_like(dg_ref)

        dh = jnp.zeros((tt, D_MODEL), F32)
        for p_ref, o, w in zip(p_refs, offs, widths):
            dh += jnp.dot(p_ref[...], w_ref[o:o + w, :], preferred_element_type=F32)
        xv = x_ref[...]
        r = lax.rsqrt(jnp.mean(xv * xv, axis=-1, keepdims=True) + NORM_EPS)
        xh = xv * r
        gy = dh * g_ref[...]
        dx_ref[...] = r_ref[...] + r * (gy - xh * jnp.mean(gy * xh, axis=-1, keepdims=True))
        dg_ref[...] += jnp.sum(dh * xh, axis=0, keepdims=True)

    row = lambda w: pl.BlockSpec((tt, w), lambda i: (i, 0))
    return pl.pallas_call(
        body, grid=(SEQ // tt,),
        in_specs=[row(w) for w in widths] + [_full(wt.shape), row(D_MODEL), _full((1, D_MODEL)), row(D_MODEL)],
        out_specs=[row(D_MODEL), _full((1, D_MODEL))],
        out_shape=[jax.ShapeDtypeStruct((SEQ, D_MODEL), F32), jax.ShapeDtypeStruct((1, D_MODEL), F32)],
        compiler_params=_cp(), name=name)(*pieces, wt, x, g, dres)


def _tn_pieces(pieces, h):
    tk, tn = 512, 512
    nk = SEQ // tk
    n = len(pieces)

    def body(*refs):
        p_refs, (h_ref, o_ref, acc_ref) = refs[:n], refs[n:]
        kk = pl.program_id(1)

        @pl.when(kk == 0)
        def _():
            acc_ref[...] = jnp.zeros_like(acc_ref)

        hv = h_ref[...]
        for p_ref, o, w in zip(p_refs, PIECE_OFF, PIECE_W):
            acc_ref[o:o + w, :] += lax.dot_general(p_ref[...], hv, TN, preferred_element_type=F32)

        @pl.when(kk == nk - 1)
        def _():
            o_ref[...] = acc_ref[...].astype(BF16)

    return pl.pallas_call(
        body, grid=(D_MODEL // tn, nk),
        in_specs=[pl.BlockSpec((tk, w), lambda j, kk: (kk, 0)) for w in PIECE_W]
        + [pl.BlockSpec((tk, tn), lambda j, kk: (kk, j))],
        out_specs=pl.BlockSpec((IN_COLS, tn), lambda j, kk: (0, j)),
        out_shape=jax.ShapeDtypeStruct((IN_COLS, D_MODEL), BF16),
        scratch_shapes=[pltpu.VMEM((IN_COLS, tn), F32)], compiler_params=_cp(), name="tn_pieces")(*pieces, h)


def _tn_branches(dys, acts):
    tk = 512
    nk = SEQ // tk

    def body(d0, d1, d2, a0, a1, a2, o_ref, acc_ref):
        kk = pl.program_id(0)

        @pl.when(kk == 0)
        def _():
            acc_ref[...] = jnp.zeros_like(acc_ref)

        for j, (d, a) in enumerate(((d0, a0), (d1, a1), (d2, a2))):
            acc_ref[:, WIDTH * j:WIDTH * (j + 1)] += lax.dot_general(d[...], a[...], TN, preferred_element_type=F32)

        @pl.when(kk == nk - 1)
        def _():
            o_ref[...] = acc_ref[...].astype(BF16)

    row = lambda w: pl.BlockSpec((tk, w), lambda kk: (kk, 0))
    return pl.pallas_call(
        body, grid=(nk,), in_specs=[row(D_MODEL)] * 3 + [row(WIDTH)] * 3,
        out_specs=_full((D_MODEL, 3 * WIDTH)), out_shape=jax.ShapeDtypeStruct((D_MODEL, 3 * WIDTH), BF16),
        scratch_shapes=[pltpu.VMEM((D_MODEL, 3 * WIDTH), F32)], compiler_params=_cp(), name="tn_branches",
    )(*dys, *acts)


def _rope(t, c, a, b):
    return t * c + pltpu.roll(t, 120, axis=1) * a + pltpu.roll(t, 8, axis=1) * b


def _rope_t(d, c, a, b):
    return d * c + pltpu.roll(d * a, 8, axis=1) + pltpu.roll(d * b, 120, axis=1)


def _band_sides(band):
    left = lax.broadcasted_iota(jnp.int32, band.shape, 1) < HEAD_DIM
    h0 = jnp.where(left, band, 0.0)
    h1 = jnp.where(left, 0.0, band)
    r0 = pltpu.roll(h0, HEAD_DIM, axis=1)
    r1 = pltpu.roll(h1, HEAD_DIM, axis=1)
    return ((h0.astype(BF16), r0.astype(BF16)), (r1.astype(BF16), h1.astype(BF16)))


def _attn_mask(i):
    qi = lax.broadcasted_iota(jnp.int32, (BLOCK, 2 * BLOCK), 0)
    kj = lax.broadcasted_iota(jnp.int32, (BLOCK, 2 * BLOCK), 1)
    delta = qi + BLOCK - kj
    return (delta >= 0) & (delta < BLOCK) & ((kj >= BLOCK) | (i > 0))


def _attn_probs(qc, kside, ok, sink):
    s = lax.dot_general(qc, kside, NT, preferred_element_type=F32) * (HEAD_DIM ** -0.5)
    s = jnp.where(ok, s, NEG_INF)
    m = jnp.maximum(jnp.max(s, axis=-1, keepdims=True), sink)
    p = jnp.exp(s - m)
    es = jnp.exp(sink - m)
    inv = 1.0 / (jnp.sum(p, axis=-1, keepdims=True) + es)
    return p * inv, es * inv


def _attn_load(q_ref, kvc_ref, kvp_ref, tc_ref, ta_ref, tb_ref, pc_ref, pa_ref, pb_ref):
    c, a, b = tc_ref[...], ta_ref[...], tb_ref[...]
    kc = _rope(kvc_ref[:, :KV_W], c, a, b)
    kp = _rope(kvp_ref[:, :KV_W], pc_ref[...], pa_ref[...], pb_ref[...])
    kband = jnp.concatenate([kp, kc], axis=0)
    vband = jnp.concatenate([kvp_ref[:, KV_W:], kvc_ref[:, KV_W:]], axis=0)
    qs = [_rope(q_ref[:, 128 * j:128 * (j + 1)], c, a, b).astype(BF16) for j in range(4)]
    return qs, _band_sides(kband), _band_sides(vband), (c, a, b)


def _attn_specs(clamp):
    cur = lambda i: (clamp(i), 0)
    prev = lambda i: (jnp.maximum(clamp(i) - 1, 0), 0)
    return [
        pl.BlockSpec((BLOCK, ATTN_W), cur), pl.BlockSpec((BLOCK, 2 * KV_W), cur),
        pl.BlockSpec((BLOCK, 2 * KV_W), prev),
        pl.BlockSpec((BLOCK, 128), cur), pl.BlockSpec((BLOCK, 128), cur), pl.BlockSpec((BLOCK, 128), cur),
        pl.BlockSpec((BLOCK, 128), prev), pl.BlockSpec((BLOCK, 128), prev), pl.BlockSpec((BLOCK, 128), prev),
        pl.BlockSpec(memory_space=pltpu.SMEM),
    ]


def _attn_fwd(q, kv, tabs, sinks):
    tc, ta, tb = tabs

    def body(q_ref, kvc_ref, kvp_ref, tc_ref, ta_ref, tb_ref, pc_ref, pa_ref, pb_ref, sink_ref, o_ref):
        i = pl.program_id(0)
        qs, ks, vs, _ = _attn_load(q_ref, kvc_ref, kvp_ref, tc_ref, ta_ref, tb_ref, pc_ref, pa_ref, pb_ref)
        ok = _attn_mask(i)
        for j in range(4):
            kh = j // 2
            acc = jnp.zeros((BLOCK, 128), F32)
            for side in range(2):
                pn, _ = _attn_probs(qs[j], ks[kh][side], ok, sink_ref[0, 2 * j + side])
                acc += jnp.dot(pn.astype(BF16), vs[kh][side], preferred_element_type=F32)
            o_ref[:, 128 * j:128 * (j + 1)] = acc.astype(BF16)

    return pl.pallas_call(
        body, grid=(N_BLOCKS,), in_specs=_attn_specs(lambda i: i),
        out_specs=pl.BlockSpec((BLOCK, ATTN_W), lambda i: (i, 0)),
        out_shape=jax.ShapeDtypeStruct((SEQ, ATTN_W), BF16), compiler_params=_cp(), name="attn_fwd",
    )(q, kv, kv, tc, ta, tb, tc, ta, tb, sinks)


def _attn_bwd(q, kv, tabs, sinks, do):
    tc, ta, tb = tabs
    last = N_BLOCKS - 1
    clamp = lambda i: jnp.minimum(i, last)

    def place(full, side, kh):
        left = lax.broadcasted_iota(jnp.int32, full.shape, 1) < HEAD_DIM
        valid = jnp.where(left, full, 0.0) if side == 0 else jnp.where(left, 0.0, full)
        return valid if side == kh else pltpu.roll(valid, HEAD_DIM, axis=1)

    def body(q_ref, kvc_ref, kvp_ref, tc_ref, ta_ref, tb_ref, pc_ref, pa_ref, pb_ref, sink_ref, do_ref,
             dq_ref, dkv_ref, ds_ref, carry_ref):
        i = pl.program_id(0)

        @pl.when(i == 0)
        def _():
            ds_ref[...] = jnp.zeros_like(ds_ref)
            carry_ref[...] = jnp.zeros_like(carry_ref)

        @pl.when(i > last)
        def _():
            dkv_ref[...] = carry_ref[...].astype(BF16)

        @pl.when(i <= last)
        def _():
            qs, ks, vs, (c, a, b) = _attn_load(q_ref, kvc_ref, kvp_ref, tc_ref, ta_ref, tb_ref,
                                               pc_ref, pa_ref, pb_ref)
            ok = _attn_mask(i)
            dk = jnp.zeros((2 * BLOCK, 128), F32)
            dv = jnp.zeros((2 * BLOCK, 128), F32)
            dsink = jnp.zeros((1, 128), F32)
            lane = lax.broadcasted_iota(jnp.int32, (1, 128), 1)
            for j in range(4):
                kh = j // 2
                doc = do_ref[:, 128 * j:128 * (j + 1)].astype(BF16)
                dq = jnp.zeros((BLOCK, 128), F32)
                for side in range(2):
                    pn, ps = _attn_probs(qs[j], ks[kh][side], ok, sink_ref[0, 2 * j + side])
                    dp = lax.dot_general(doc, vs[kh][side], NT, preferred_element_type=F32)
                    dr = jnp.sum(pn * dp, axis=-1, keepdims=True)
                    dsb = (pn * (dp - dr) * (HEAD_DIM ** -0.5)).astype(BF16)
                    dsink += jnp.where(lane == 2 * j + side, -jnp.sum(ps * dr), 0.0)
                    dq += jnp.dot(dsb, ks[kh][side], preferred_element_type=F32)
                    dk += place(lax.dot_general(dsb, qs[j], TN, preferred_element_type=F32), side, kh)
                    dv += place(lax.dot_general(pn.astype(BF16), doc, TN, preferred_element_type=F32), side, kh)
                dq_ref[:, 128 * j:128 * (j + 1)] = _rope_t(dq, c, a, b).astype(BF16)
            ds_ref[...] += dsink
            dk_prev = _rope_t(dk[:BLOCK], pc_ref[...], pa_ref[...], pb_ref[...])
            dk_cur = _rope_t(dk[BLOCK:], c, a, b)
            prev = jnp.concatenate([dk_prev, dv[:BLOCK]], axis=1)
            dkv_ref[...] = (carry_ref[...] + prev).astype(BF16)
            carry_ref[...] = jnp.concatenate([dk_cur, dv[BLOCK:]], axis=1)

    return pl.pallas_call(
        body, grid=(N_BLOCKS + 1,),
        in_specs=_attn_specs(clamp) + [pl.BlockSpec((BLOCK, ATTN_W), lambda i: (clamp(i), 0))],
        out_specs=[pl.BlockSpec((BLOCK, ATTN_W), lambda i: (clamp(i), 0)),
                   pl.BlockSpec((BLOCK, 2 * KV_W), lambda i: (jnp.maximum(i - 1, 0), 0)),
                   pl.BlockSpec((1, 128), lambda i: (0, 0))],
        out_shape=[jax.ShapeDtypeStruct((SEQ, ATTN_W), BF16), jax.ShapeDtypeStruct((SEQ, 2 * KV_W), BF16),
                   jax.ShapeDtypeStruct((1, 128), F32)],
        scratch_shapes=[pltpu.VMEM((BLOCK, 2 * KV_W), F32)], compiler_params=_cp(), name="attn_bwd",
    )(q, kv, kv, tc, ta, tb, tc, ta, tb, sinks, do)


def _shift_down(z, k):
    row = lax.broadcasted_iota(jnp.int32, z.shape, 0)
    return jnp.where(row < k, 0.0, pltpu.roll(z, k, axis=0))


def _shift_up(z, k):
    n = z.shape[0]
    row = lax.broadcasted_iota(jnp.int32, z.shape, 0)
    return jnp.where(row >= n - k, 0.0, pltpu.roll(z, n - k, axis=0))


def _conv_specs():
    nb = WIDTH // 128
    return [pl.BlockSpec((SEQ, 128), lambda j: (0, j)), pl.BlockSpec((SEQ, 128), lambda j: (0, nb + j)),
            pl.BlockSpec((SEQ, 128), lambda j: (0, 2 * nb + j)), pl.BlockSpec((None, 8, 128), lambda j: (0, 0, j))]


def _conv_fwd(cbx, cw, layer):
    def body(cb_ref, cc_ref, cx_ref, w_ref, o_ref):
        z = cc_ref[...] * cx_ref[...]
        s = w_ref[0:1, :] * _shift_down(z, 2) + w_ref[1:2, :] * _shift_down(z, 1) + w_ref[2:3, :] * z
        o_ref[...] = (cb_ref[...] * s).astype(BF16)

    specs = _conv_specs()
    specs[3] = pl.BlockSpec((None, 8, 128), lambda j: (layer, 0, j))
    return pl.pallas_call(
        body, grid=(WIDTH // 128,), in_specs=specs,
        out_specs=pl.BlockSpec((SEQ, 128), lambda j: (0, j)),
        out_shape=jax.ShapeDtypeStruct((SEQ, WIDTH), BF16), compiler_params=_cp(), name="conv_fwd",
    )(cbx, cbx, cbx, cw)


def _conv_bwd(cbx, cw, layer, dout):
    def body(cb_ref, cc_ref, cx_ref, w_ref, do_ref, dcb_ref, dcc_ref, dcx_ref, dw_ref):
        cc, cx = cc_ref[...], cx_ref[...]
        z = cc * cx
        z1, z2 = _shift_down(z, 1), _shift_down(z, 2)
        w0, w1, w2 = w_ref[0:1, :], w_ref[1:2, :], w_ref[2:3, :]
        dout = do_ref[...]
        ds = dout * cb_ref[...]
        dcb_ref[...] = (dout * (w0 * z2 + w1 * z1 + w2 * z)).astype(BF16)
        dz = w2 * ds + w1 * _shift_up(ds, 1) + w0 * _shift_up(ds, 2)
        dcc_ref[...] = (dz * cx).astype(BF16)
        dcx_ref[...] = (dz * cc).astype(BF16)
        rows = [jnp.sum(ds * zz, axis=0, keepdims=True) for zz in (z2, z1, z)]
        dw_ref[...] = jnp.concatenate(rows + [jnp.zeros((5, 128), F32)], axis=0)

    col = lambda j: (0, j)
    specs = _conv_specs()
    specs[3] = pl.BlockSpec((None, 8, 128), lambda j: (layer, 0, j))
    return pl.pallas_call(
        body, grid=(WIDTH // 128,), in_specs=specs + [pl.BlockSpec((SEQ, 128), col)],
        out_specs=[pl.BlockSpec((SEQ, 128), col), pl.BlockSpec((SEQ, 128), col), pl.BlockSpec((SEQ, 128), col),
                   pl.BlockSpec((8, 128), col)],
        out_shape=[jax.ShapeDtypeStruct((SEQ, WIDTH), BF16)] * 3 + [jax.ShapeDtypeStruct((8, WIDTH), F32)],
        compiler_params=_cp(), name="conv_bwd",
    )(cbx, cbx, cbx, cw, dout)


def _ssm_prep_math(a_re, a_im, log_dt, bt_re, bt_im):
    dt = jnp.exp(log_dt)
    er = jnp.exp(a_re * dt)
    lr = er * jnp.cos(a_im * dt)
    li = er * jnp.sin(a_im * dt)
    n2 = a_re * a_re + a_im * a_im
    cr = ((lr - 1.0) * a_re + li * a_im) / n2
    ci = (li * a_re - (lr - 1.0) * a_im) / n2
    cr3, ci3 = cr[:, None, :], ci[:, None, :]
    return lr, li, cr3 * bt_re - ci3 * bt_im, cr3 * bt_im + ci3 * bt_re


_GS = (SSM_GROUPS, SSM_STATE)
_GHS = (SSM_GROUPS, SSM_GROUP, SSM_STATE)


def _layered(shape):
    return pl.BlockSpec((None,) + shape, lambda l: (l,) + (0,) * len(shape))


def _ssm_prep(a_re, a_im, log_dt, bt_re, bt_im):
    def body(ar, ai, ld, br, bi, o0, o1, o2, o3):
        outs = _ssm_prep_math(ar[...], ai[...], ld[...], br[...], bi[...])
        for o, v in zip((o0, o1, o2, o3), outs):
            o[...] = v

    shapes = [_GS, _GS, _GHS, _GHS]
    return pl.pallas_call(
        body, grid=(DEPTH,), in_specs=[_layered(s) for s in (_GS, _GS, (SSM_GROUPS, 1), _GHS, _GHS)],
        out_specs=[_layered(s) for s in shapes],
        out_shape=[jax.ShapeDtypeStruct((DEPTH,) + s, F32) for s in shapes],
        name="ssm_prep")(a_re, a_im, log_dt, bt_re, bt_im)


def _ssm_prep_bwd(a_re, a_im, log_dt, bt_re, bt_im, cots):
    def body(ar, ai, ld, br, bi, c0, c1, c2, c3, o0, o1, o2, o3, o4):
        _, vjp = jax.vjp(_ssm_prep_math, ar[...], ai[...], ld[...], br[...], bi[...])
        for o, v in zip((o0, o1, o2, o3, o4), vjp((c0[...], c1[...], c2[...], c3[...]))):
            o[...] = v

    ins = (_GS, _GS, (SSM_GROUPS, 1), _GHS, _GHS)
    return pl.pallas_call(
        body, grid=(DEPTH,), in_specs=[_layered(s) for s in ins + (_GS, _GS, _GHS, _GHS)],
        out_specs=[_layered(s) for s in ins],
        out_shape=[jax.ShapeDtypeStruct((DEPTH,) + s, F32) for s in ins],
        name="ssm_prep_bwd")(a_re, a_im, log_dt, bt_re, bt_im, *cots)


LANES_G = 512
N_LANE_GROUPS = SSM_GROUPS * SSM_STATE // LANES_G


def _scan_order(a):
    return a.reshape(N_CHUNKS, CHUNK, -1).transpose(1, 0, 2).reshape(a.shape)


def _time_order(a):
    return a.reshape(CHUNK, N_CHUNKS, -1).transpose(1, 0, 2).reshape(a.shape)


def _scan_in_place(xr_ref, xi_ref, ar, ai, reverse):
    shape = (N_CHUNKS, xr_ref.shape[1])
    ar, ai = jnp.broadcast_to(ar, shape), jnp.broadcast_to(ai, shape)

    def rows(tau):
        t = (CHUNK - 1 - tau) if reverse else tau
        return pl.ds(pl.multiple_of(t * N_CHUNKS, N_CHUNKS), N_CHUNKS)

    def step(tau, carry):
        sr, si = carry
        return ar * sr - ai * si + xr_ref[rows(tau), :], ar * si + ai * sr + xi_ref[rows(tau), :]

    zero = jnp.zeros(shape, F32)
    er, ei = lax.fori_loop(0, CHUNK, step, (zero, zero), unroll=8)
    qr, qi = ar, ai
    for _ in range(8):
        qr, qi = qr * qr - qi * qi, 2.0 * qr * qi
    shift = _shift_up if reverse else _shift_down
    for k in (1, 2, 4):
        sr, si = shift(er, k), shift(ei, k)
        er, ei = er + qr * sr - qi * si, ei + qr * si + qi * sr
        qr, qi = qr * qr - qi * qi, 2.0 * qr * qi
    start = (shift(er, 1), shift(ei, 1))

    def write(tau, carry):
        sr, si = step(tau, carry)
        xr_ref[rows(tau), :] = sr
        xi_ref[rows(tau), :] = si
        return sr, si

    return write, start


def _ssm_specs(layer):
    col = lambda w: pl.BlockSpec((SEQ, w), lambda g: (0, g))
    diag = pl.BlockSpec((None, None, 128, LANES_G), lambda g: (layer, g, 0, 0))
    vec = pl.BlockSpec((None, 1, LANES_G), lambda g: (layer, 0, g))
    return col, diag, vec


def _ssm_fwd(u16, u, mats, layer, d):
    def body(u16_ref, u_ref, d_ref, br_ref, bi_ref, cr_ref, ci_ref, ar_ref, ai_ref, xr_ref, xi_ref, y_ref):
        uv = u16_ref[...]
        xr_ref[...] = jnp.dot(uv, br_ref[...], preferred_element_type=F32)
        xi_ref[...] = jnp.dot(uv, bi_ref[...], preferred_element_type=F32)
        write, start = _scan_in_place(xr_ref, xi_ref, ar_ref[...], ai_ref[...], False)
        lax.fori_loop(0, CHUNK, write, start, unroll=8)
        y = lax.dot_general(xr_ref[...].astype(BF16), cr_ref[...], NT, preferred_element_type=F32)
        y += lax.dot_general(xi_ref[...].astype(BF16), ci_ref[...], NT, preferred_element_type=F32)
        y_ref[...] = y + d_ref[...] * u_ref[...]

    col, diag, vec = _ssm_specs(layer)
    return pl.pallas_call(
        body, grid=(N_LANE_GROUPS,),
        in_specs=[col(128), col(128), pl.BlockSpec((None, 1, 128), lambda g: (layer, 0, g)),
                  diag, diag, diag, diag, vec, vec],
        out_specs=[col(LANES_G), col(LANES_G), col(128)],
        out_shape=[jax.ShapeDtypeStruct((SEQ, SSM_GROUPS * SSM_STATE), F32)] * 2
        + [jax.ShapeDtypeStruct((SEQ, WIDTH), F32)],
        compiler_params=_cp(), name="ssm_fwd",
    )(u16, u, d, mats["b_re"], mats["b_im"], mats["c_re"], mats["c_im_neg"], mats["a_re"], mats["a_im"])


def _ssm_bwd(dy16, x_re, x_im, u16, mats, layer, d):
    def body(dy_ref, u_ref, d_ref, xr_ref, xi_ref, br_ref, bi_ref, cr_ref, ci_ref, ar_ref, ai_ref,
             du_ref, dar_ref, dai_ref, dbr_ref, dbi_ref, dcr_ref, dci_ref, lr_ref, li_ref):
        dy = dy_ref[...]
        lr_ref[...] = jnp.dot(dy, cr_ref[...], preferred_element_type=F32)
        li_ref[...] = jnp.dot(dy, ci_ref[...], preferred_element_type=F32)
        write, start = _scan_in_place(lr_ref, li_ref, ar_ref[...], -ai_ref[...], True)

        def rows(t):
            return pl.ds(pl.multiple_of(t * N_CHUNKS, N_CHUNKS), N_CHUNKS)

        def grad(acc, lam, xpr, xpi):
            return acc[0] + xpr * lam[0] + xpi * lam[1], acc[1] + xpr * lam[1] - xpi * lam[0]

        def down(tau, carry):
            lam = write(tau, carry[0])
            t = CHUNK - 2 - tau
            return lam, grad(carry[1], lam, xr_ref[rows(t), :], xi_ref[rows(t), :])

        zero = jnp.zeros((N_CHUNKS, LANES_G), F32)
        lam, acc = lax.fori_loop(0, CHUNK - 1, down, (start, (zero, zero)), unroll=5)
        lam = write(CHUNK - 1, lam)
        last = rows(CHUNK - 1)
        acc = grad(acc, lam, _shift_down(xr_ref[last, :], 1), _shift_down(xi_ref[last, :], 1))
        dar_ref[...] = jnp.sum(acc[0], axis=0, keepdims=True)
        dai_ref[...] = jnp.sum(acc[1], axis=0, keepdims=True)

        l_re, l_im = lr_ref[...].astype(BF16), li_ref[...].astype(BF16)
        du = lax.dot_general(l_re, br_ref[...], NT, preferred_element_type=F32)
        du += lax.dot_general(l_im, bi_ref[...], NT, preferred_element_type=F32)
        du_ref[...] = (du + dy.astype(F32) * d_ref[...]).astype(BF16)
        uv = u_ref[...]
        dbr_ref[...] = lax.dot_general(uv, l_re, TN, preferred_element_type=F32)
        dbi_ref[...] = lax.dot_general(uv, l_im, TN, preferred_element_type=F32)
        dcr_ref[...] = lax.dot_general(dy, xr_ref[...].astype(BF16), TN, preferred_element_type=F32)
        dci_ref[...] = lax.dot_general(dy, xi_ref[...].astype(BF16), TN, preferred_element_type=F32)

    col, diag, vec = _ssm_specs(layer)
    out_vec = pl.BlockSpec((1, LANES_G), lambda g: (0, g))
    out_blk = pl.BlockSpec((None, 128, LANES_G), lambda g: (g, 0, 0))
    sds = jax.ShapeDtypeStruct
    return pl.pallas_call(
        body, grid=(N_LANE_GROUPS,),
        in_specs=[col(128), col(128), pl.BlockSpec((None, 1, 128), lambda g: (layer, 0, g)),
                  col(LANES_G), col(LANES_G), diag, diag, diag, diag, vec, vec],
        out_specs=[col(128), out_vec, out_vec, out_blk, out_blk, out_blk, out_blk],
        out_shape=[sds((SEQ, WIDTH), BF16)] + [sds((1, SSM_GROUPS * SSM_STATE), F32)] * 2
        + [sds((N_LANE_GROUPS, 128, LANES_G), F32)] * 4,
        scratch_shapes=[pltpu.VMEM((SEQ, LANES_G), F32)] * 2, compiler_params=_cp(), name="ssm_bwd",
    )(dy16, u16, d, x_re, x_im, mats["b_re"], mats["b_im"], mats["c_re"], mats["c_im_neg"],
      mats["a_re"], mats["a_im"])


_GELU_C = math.sqrt(2.0 / math.pi)


def _gelu(y):
    return 0.5 * y * (1.0 + jnp.tanh(_GELU_C * (y + 0.044715 * (y * y * y))))


def _glu_fwd(y, wglu):
    tt = 512

    def body(y_ref, w_ref, z_ref):
        ys = _gelu(y_ref[...])
        a = jnp.dot(ys.astype(BF16), w_ref[...], preferred_element_type=F32)
        z_ref[...] = (ys * jax.nn.sigmoid(a)).astype(BF16)

    blk = pl.BlockSpec((tt, WIDTH), lambda i: (i, 0))
    return pl.pallas_call(body, grid=(SEQ // tt,), in_specs=[blk, _full((WIDTH, WIDTH))], out_specs=blk,
                          out_shape=jax.ShapeDtypeStruct((SEQ, WIDTH), BF16), compiler_params=_cp(),
                          name="glu_fwd")(y, wglu)


def _glu_bwd(y, wglu, dz, u):
    tt = 512

    def body(y_ref, w_ref, dz_ref, u_ref, dy_ref, ys_ref, da_ref, dd_ref):
        @pl.when(pl.program_id(0) == 0)
        def _():
            dd_ref[...] = jnp.zeros_like(dd_ref)

        yv = y_ref[...]
        t = jnp.tanh(_GELU_C * (yv + 0.044715 * (yv * yv * yv)))
        ys = 0.5 * yv * (1.0 + t)
        ysb = ys.astype(BF16)
        sg = jax.nn.sigmoid(jnp.dot(ysb, w_ref[...], preferred_element_type=F32))
        dz = dz_ref[...].astype(F32)
        da = (dz * ys * sg * (1.0 - sg)).astype(BF16)
        dys = dz * sg + lax.dot_general(da, w_ref[...], NT, preferred_element_type=F32)
        dy = dys * (0.5 * (1.0 + t) + 0.5 * yv * (1.0 - t * t) * _GELU_C * (1.0 + 3 * 0.044715 * (yv * yv)))
        dy_ref[...] = dy.astype(BF16)
        ys_ref[...] = ysb
        da_ref[...] = da
        dd_ref[...] += jnp.sum(dy * u_ref[...], axis=0, keepdims=True)

    blk = pl.BlockSpec((tt, WIDTH), lambda i: (i, 0))
    return pl.pallas_call(
        body, grid=(SEQ // tt,), in_specs=[blk, _full((WIDTH, WIDTH)), blk, blk],
        out_specs=[blk, blk, blk, _full((1, WIDTH))],
        out_shape=[jax.ShapeDtypeStruct((SEQ, WIDTH), BF16)] * 3 + [jax.ShapeDtypeStruct((1, WIDTH), F32)],
        compiler_params=_cp(), name="glu_bwd")(y, wglu, dz, u)


def _mix_specs(tt, layer):
    row = lambda w: pl.BlockSpec((tt, w), lambda i: (i, 0))
    gate = lambda j: pl.BlockSpec((tt, D_MODEL), lambda i: (i, j))
    wo = lambda j: pl.BlockSpec((D_MODEL, WIDTH), lambda i: (0, j))
    return [row(D_MODEL), row(WIDTH), row(WIDTH), row(WIDTH), gate(0), gate(1), gate(2),
            pl.BlockSpec((None, 1, GATE_W), lambda i: (layer, 0, 0)), wo(0), wo(1), wo(2),
            _full((D_MODEL, D_MODEL))]


def _mix_branches(o_ref, c_ref, z_ref, g_refs, b_ref, wa_ref, wc_ref, ws_ref):
    ys = [lax.dot_general(r[...], w[...], NT, preferred_element_type=F32)
          for r, w in ((o_ref, wa_ref), (c_ref, wc_ref), (z_ref, ws_ref))]
    gates = [jax.nn.sigmoid(g_refs[j][...] + b_ref[:, D_MODEL * j:D_MODEL * (j + 1)]) for j in range(3)]
    return ys, gates


def _mix_fwd(x, o, cv, z, glog, b_gate, layer, wbt, wmix):
    tt = 256

    def body(x_ref, o_ref, c_ref, z_ref, g0, g1, g2, b_ref, wa_ref, wc_ref, ws_ref, wm_ref, x1_ref):
        ys, gates = _mix_branches(o_ref, c_ref, z_ref, (g0, g1, g2), b_ref, wa_ref, wc_ref, ws_ref)
        merged = gates[0] * ys[0] + gates[1] * ys[1] + gates[2] * ys[2]
        x1_ref[...] = x_ref[...] + jnp.dot(merged.astype(BF16), wm_ref[...], preferred_element_type=F32)

    return pl.pallas_call(
        body, grid=(SEQ // tt,), in_specs=_mix_specs(tt, layer),
        out_specs=pl.BlockSpec((tt, D_MODEL), lambda i: (i, 0)),
        out_shape=jax.ShapeDtypeStruct((SEQ, D_MODEL), F32), compiler_params=_cp(), name="mix_fwd",
    )(x, o, cv, z, glog, glog, glog, b_gate, wbt, wbt, wbt, wmix)


def _mix_bwd(dx1, o, cv, z, glog, b_gate, layer, wbt, wmix, tie):
    tt = 256

    def body(dx_ref, o_ref, c_ref, z_ref, g0, g1, g2, b_ref, wa_ref, wc_ref, ws_ref, wm_ref, tie_ref,
             mg_ref, dya_ref, dyc_ref, dys_ref, do_ref, dc_ref, dz_ref, dgl_ref, db_ref):
        @pl.when(pl.program_id(0) == 0)
        def _():
            db_ref[...] = jnp.zeros_like(db_ref)

        ys, gates = _mix_branches(o_ref, c_ref, z_ref, (g0, g1, g2), b_ref, wa_ref, wc_ref, ws_ref)
        mg_ref[...] = (gates[0] * ys[0] + gates[1] * ys[1] + gates[2] * ys[2]).astype(BF16)
        dm = lax.dot_general(dx_ref[...].astype(BF16), wm_ref[...], NT, preferred_element_type=F32)
        for j, (dy_ref, w_ref, d_ref) in enumerate(((dya_ref, wa_ref, do_ref), (dyc_ref, wc_ref, dc_ref),
                                                    (dys_ref, ws_ref, dz_ref))):
            dy = (dm * gates[j]).astype(BF16)
            dy_ref[...] = dy
            d_ref[...] = jnp.dot(dy, w_ref[...], preferred_element_type=F32)
            dgl = dm * ys[j] * gates[j] * (1.0 - gates[j])
            dgl_ref[:, D_MODEL * j:D_MODEL * (j + 1)] = dgl.astype(BF16)
            db_ref[:, D_MODEL * j:D_MODEL * (j + 1)] += jnp.sum(dgl, axis=0, keepdims=True)

    row = lambda w: pl.BlockSpec((tt, w), lambda i: (i, 0))
    sds = jax.ShapeDtypeStruct
    return pl.pallas_call(
        body, grid=(SEQ // tt,), in_specs=_mix_specs(tt, layer) + [ANY],
        out_specs=[row(D_MODEL)] * 4 + [row(WIDTH)] * 3 + [row(GATE_W), _full((1, GATE_W))],
        out_shape=[sds((SEQ, D_MODEL), BF16)] * 4 + [sds((SEQ, WIDTH), F32)] * 3
        + [sds((SEQ, GATE_W), BF16), sds((1, GATE_W), F32)],
        compiler_params=_cp(), name="mix_bwd",
    )(dx1, o, cv, z, glog, glog, glog, b_gate, wbt, wbt, wbt, wmix, tie)


def _ffn_out_fwd(x1, gu, wout, tie):
    tt = 256

    def body(x_ref, gt_ref, up_ref, w_ref, tie_ref, o_ref):
        gt = gt_ref[...]
        act = (gt * jax.nn.sigmoid(gt) * up_ref[...]).astype(BF16)
        o_ref[...] = x_ref[...] + jnp.dot(act, w_ref[...], preferred_element_type=F32)

    return pl.pallas_call(
        body, grid=(SEQ // tt,),
        in_specs=[pl.BlockSpec((tt, D_MODEL), lambda i: (i, 0)), pl.BlockSpec((tt, FFN_H), lambda i: (i, 0)),
                  pl.BlockSpec((tt, FFN_H), lambda i: (i, 1)), _full((FFN_H, D_MODEL)), ANY],
        out_specs=pl.BlockSpec((tt, D_MODEL), lambda i: (i, 0)),
        out_shape=jax.ShapeDtypeStruct((SEQ, D_MODEL), F32), compiler_params=_cp(), name="ffn_out_fwd",
    )(x1, gu, gu, wout, tie)


def _ffn_out_bwd(dx2, gu, wout, tie):
    tt = 256

    def body(dx_ref, gt_ref, up_ref, w_ref, tie_ref, dgu_ref, act_ref):
        gt, up = gt_ref[...], up_ref[...]
        sg = jax.nn.sigmoid(gt)
        silu = gt * sg
        act_ref[...] = (silu * up).astype(BF16)
        dact = lax.dot_general(dx_ref[...].astype(BF16), w_ref[...], NT, preferred_element_type=F32)
        dgu_ref[:, :FFN_H] = (dact * up * (sg * (1.0 + gt * (1.0 - sg)))).astype(BF16)
        dgu_ref[:, FFN_H:] = (dact * silu).astype(BF16)

    return pl.pallas_call(
        body, grid=(SEQ // tt,),
        in_specs=[pl.BlockSpec((tt, D_MODEL), lambda i: (i, 0)), pl.BlockSpec((tt, FFN_H), lambda i: (i, 0)),
                  pl.BlockSpec((tt, FFN_H), lambda i: (i, 1)), _full((FFN_H, D_MODEL)), ANY],
        out_specs=[pl.BlockSpec((tt, 2 * FFN_H), lambda i: (i, 0)), pl.BlockSpec((tt, FFN_H), lambda i: (i, 0))],
        out_shape=[jax.ShapeDtypeStruct((SEQ, 2 * FFN_H), BF16), jax.ShapeDtypeStruct((SEQ, FFN_H), BF16)],
        compiler_params=_cp(), name="ffn_out_bwd",
    )(dx2, gu, gu, wout, tie)


def _loss_head(x, g, target):
    tt = 256

    def body(x_ref, g_ref, t_ref, loss_ref, dx_ref, dg_ref):
        @pl.when(pl.program_id(0) == 0)
        def _():
            loss_ref[...] = jnp.zeros_like(loss_ref)
            dg_ref[...] = jnp.zeros_like(dg_ref)

        xv = x_ref[...]
        r = lax.rsqrt(jnp.mean(xv * xv, axis=-1, keepdims=True) + NORM_EPS)
        xh = xv * r
        err = xh * g_ref[...] - t_ref[...]
        loss_ref[...] += 0.5 * jnp.sum(jnp.mean(err * err, axis=-1, keepdims=True))
        dy = err * (1.0 / D_MODEL)
        gy = dy * g_ref[...]
        dx_ref[...] = r * (gy - xh * jnp.mean(gy * xh, axis=-1, keepdims=True))
        dg_ref[...] += jnp.sum(dy * xh, axis=0, keepdims=True)

    row = pl.BlockSpec((tt, D_MODEL), lambda i: (i, 0))
    return pl.pallas_call(
        body, grid=(SEQ // tt,), in_specs=[row, _full((1, D_MODEL)), row],
        out_specs=[_full((1, 128)), row, _full((1, D_MODEL))],
        out_shape=[jax.ShapeDtypeStruct((1, 128), F32), jax.ShapeDtypeStruct((SEQ, D_MODEL), F32),
                   jax.ShapeDtypeStruct((1, D_MODEL), F32)],
        compiler_params=_cp(), name="loss_head")(x, g, target)


def _adamw(parts, w, m, v, tr, name, groups=None, fill=None, tie=None):
    n_groups, rows, cols = w.shape
    n_parts = parts.shape[1]
    lo, hi = groups if groups is not None else (0, n_groups)

    def body(p_ref, w_ref, m_ref, v_ref, *rest):
        g_ref, d_ref, nm_ref, nv_ref = rest[-4:]
        g = p_ref[0].astype(F32)
        for k in range(1, n_parts):
            g = g + p_ref[k].astype(F32)
        nm = B1 * m_ref[...] + (1.0 - B1) * g
        nv = B2 * v_ref[...] + (1.0 - B2) * (g * g)
        m_hat = nm / (1.0 - B1 ** STEP)
        v_hat = nv / (1.0 - B2 ** STEP)
        g_ref[...] = g
        d_ref[...] = -LR * (m_hat / (jnp.sqrt(v_hat) + ADAM_EPS) + WD * w_ref[...])
        nm_ref[...] = nm
        nv_ref[...] = nv

    blk = pl.BlockSpec((None, tr, cols), lambda l, i: (l + lo, i, 0))
    p_lo = lo if parts.shape[0] == n_groups else 0
    extra = ([] if fill is None else list(fill)) + ([] if tie is None else [tie])
    return pl.pallas_call(
        body, grid=(hi - lo, rows // tr),
        in_specs=[pl.BlockSpec((None, n_parts, tr, cols), lambda l, i: (l + p_lo, 0, i, 0)), blk, blk, blk]
        + [ANY] * len(extra),
        out_specs=[blk] * 4, out_shape=[jax.ShapeDtypeStruct((n_groups, rows, cols), F32)] * 4,
        input_output_aliases={} if fill is None else {4 + j: j for j in range(4)},
        compiler_params=_cp(), name=name)(parts, w, m, v, *extra)


def _gather_body(n, pick):
    def body(*refs):
        srcs, outs, (send_sems, recv_sems, local_sems) = refs[:n], refs[n:2 * n], refs[2 * n:]
        x, y, c = lax.axis_index("x"), lax.axis_index("y"), lax.axis_index("c")
        me, sibling = (x, y, c), (x, y, 1 - c)
        chips = [(1 - x, y), (x, 1 - y), (1 - x, 1 - y)]

        def copy(i, k, block, to, src=None):
            slot = outs[i].at[4 * block[0] + 2 * block[1] + block[2]]
            return pltpu.make_async_remote_copy(
                src_ref=slot if src is None else src, dst_ref=slot,
                send_sem=send_sems.at[7 * i + k], recv_sem=recv_sems.at[7 * i + k],
                device_id=to, device_id_type=MESH_ID)

        mine = [pltpu.make_async_copy(pick(srcs[i]), outs[i].at[4 * x + 2 * y + c], local_sems.at[i])
                for i in range(n)]
        for cp in mine:
            cp.start()
        first = []
        for i in range(n):
            first.append(copy(i, 0, me, sibling, src=pick(srcs[i])))
            first += [copy(i, 1 + j, me, (*chip, c), src=pick(srcs[i])) for j, chip in enumerate(chips)]
        for cp in first:
            cp.start()
        passed = []
        for j, chip in enumerate(chips):
            for i in range(n):
                copy(i, 1 + j, (*chip, c), me).wait_recv()
                passed.append(copy(i, 4 + j, (*chip, c), sibling))
                passed[-1].start()
        for i in range(n):
            copy(i, 0, sibling, me).wait_recv()
            for j, chip in enumerate(chips):
                copy(i, 4 + j, (*chip, 1 - c), me).wait_recv()
        for cp in first + passed:
            cp.wait_send()
        for cp in mine:
            cp.wait()

    return body


def _gather_scratch(n):
    return [pltpu.SemaphoreType.DMA((7 * n,)), pltpu.SemaphoreType.DMA((7 * n,)), pltpu.SemaphoreType.DMA((n,))]


def _all_gather(shard, name):
    return pl.pallas_call(
        _gather_body(1, lambda ref: ref), in_specs=[ANY], out_specs=[ANY],
        out_shape=[jax.ShapeDtypeStruct((N_DEV,) + shard.shape, shard.dtype)],
        scratch_shapes=_gather_scratch(1), name=name)(shard)[0]


def _split_start(name, arrays, n_sems, plan, after=None):
    n = len(arrays)
    order = [] if after is None else [after]
    n_in = n + len(order)

    def body(*refs):
        ins, send_sems, recv_sems, token = refs[:n], refs[n_in], refs[n_in + 1], refs[-1]
        for src, dst, k, to in plan(ins)[0]:
            pltpu.make_async_remote_copy(src_ref=src, dst_ref=dst, send_sem=send_sems.at[k], recv_sem=recv_sems.at[k],
                                         device_id=to, device_id_type=MESH_ID).start()
        token[...] = jnp.zeros_like(token)

    outs = pl.pallas_call(
        body, name=name,
        out_shape=(pltpu.SemaphoreType.DMA((n_sems,)), pltpu.SemaphoreType.DMA((n_sems,)),
                   *[pltpu.HBM(a.shape, a.dtype) for a in arrays], jax.ShapeDtypeStruct((8, 128), F32)),
        in_specs=[HBM] * n + [ANY] * len(order),
        out_specs=(SEM, SEM, *[HBM] * n, pl.BlockSpec(memory_space=pltpu.VMEM)),
        input_output_aliases={i: 2 + i for i in range(n)},
        compiler_params=pltpu.CompilerParams(has_side_effects=EFFECT),
    )(*[pltpu.with_memory_space_constraint(a, pltpu.HBM) for a in arrays], *order)
    return outs[0], outs[1], list(outs[2:2 + n]), outs[-1]


def _split_wait(name, arrays, send_sems, recv_sems, after, plan):
    n = len(arrays)

    def body(*refs):
        ins, s_sems, r_sems = refs[:n], refs[n], refs[n + 1]
        sends, arrivals = plan(ins)
        x, y, c = lax.axis_index("x"), lax.axis_index("y"), lax.axis_index("c")
        for src, dst, k, to in sends:
            pltpu.make_async_remote_copy(src_ref=src, dst_ref=dst, send_sem=s_sems.at[k], recv_sem=r_sems.at[k],
                                         device_id=to, device_id_type=MESH_ID).wait_send()
        for dst, k in arrivals:
            pltpu.make_async_remote_copy(src_ref=dst, dst_ref=dst, send_sem=s_sems.at[k], recv_sem=r_sems.at[k],
                                         device_id=(x, y, c), device_id_type=MESH_ID).wait_recv()

    return pl.pallas_call(
        body, name=name, out_shape=[pltpu.HBM(a.shape, a.dtype) for a in arrays],
        in_specs=[HBM] * n + [SEM, SEM, ANY], out_specs=[HBM] * n,
        input_output_aliases={i: i for i in range(n)},
        compiler_params=pltpu.CompilerParams(has_side_effects=EFFECT),
    )(*arrays, send_sems, recv_sems, after)


def _chips():
    x, y, c = lax.axis_index("x"), lax.axis_index("y"), lax.axis_index("c")
    return x, y, c, [(1 - x, y), (x, 1 - y), (1 - x, 1 - y)]


N_KINDS = len(KINDS)


def _plan_gather_chips(refs):
    x, y, c, chips = _chips()
    me = 4 * x + 2 * y + c
    sends, arrivals = [], []
    for i in range(N_KINDS):
        src, land = refs[i], refs[N_KINDS + i]
        sends.append((src, land.at[me], 4 * i, (x, y, 1 - c)))
        arrivals.append((land.at[4 * x + 2 * y + 1 - c], 4 * i))
        for j, (px, py) in enumerate(chips):
            sends.append((src, land.at[me], 4 * i + 1 + j, (px, py, c)))
            arrivals.append((land.at[4 * px + 2 * py + c], 4 * i + 1 + j))
    return sends, arrivals


def _plan_gather_pass(refs):
    x, y, c, chips = _chips()
    sends, arrivals = [], []
    for i in range(N_KINDS):
        for j, (px, py) in enumerate(chips):
            slot = refs[i].at[4 * px + 2 * py + c]
            sends.append((slot, slot, 3 * i + j, (x, y, 1 - c)))
            arrivals.append((refs[i].at[4 * px + 2 * py + 1 - c], 3 * i + j))
    return sends, arrivals


def _plan_scatter_pair(refs):
    x, y, c = lax.axis_index("x"), lax.axis_index("y"), lax.axis_index("c")
    sends, arrivals = [], []
    for i in range(N_KINDS):
        for q in range(4):
            sends.append((refs[i].at[q, 1 - c], refs[N_KINDS + i].at[q], 4 * i + q, (x, y, 1 - c)))
            arrivals.append((refs[N_KINDS + i].at[q], 4 * i + q))
    return sends, arrivals


def _plan_scatter_chips(layer):
    def plan(refs):
        x, y, c, chips = _chips()
        sends, arrivals = [], []
        for i in range(N_KINDS):
            for j, (px, py) in enumerate(chips):
                sends.append((refs[i].at[2 * px + py], refs[N_KINDS + i].at[layer, 2 * x + y], 3 * i + j, (px, py, c)))
                arrivals.append((refs[N_KINDS + i].at[layer, 2 * px + py], 3 * i + j))
        return sends, arrivals

    return plan


def _pair_sum(parts4, from_pair, landing, layer, core, tr, name):
    _, _, rows, cols = parts4.shape

    def body(c_ref, p_ref, s_ref, l_ref, sum_ref, land_ref):
        v = (p_ref[...].astype(F32) + s_ref[...].astype(F32)).astype(BF16)
        sum_ref[...] = v
        land_ref[...] = v

    blk = pl.BlockSpec((None, tr, cols), lambda q, i, c_ref: (q, i, 0))
    return pl.pallas_call(
        body,
        grid_spec=pltpu.PrefetchScalarGridSpec(
            num_scalar_prefetch=1, grid=(4, rows // tr),
            in_specs=[pl.BlockSpec((None, None, tr, cols), lambda q, i, c_ref: (q, c_ref[0], i, 0)), blk, ANY],
            out_specs=[blk, pl.BlockSpec((None, None, tr, cols), lambda q, i, c_ref: (layer, q, i, 0))]),
        out_shape=[jax.ShapeDtypeStruct((4, rows, cols), BF16), jax.ShapeDtypeStruct(landing.shape, BF16)],
        input_output_aliases={3: 1}, compiler_params=_cp(), name=name,
    )(core, parts4, from_pair, landing)


def _travel_layout(t):
    tr = lambda a: jnp.swapaxes(a, 1, 2)
    branch = jnp.concatenate([tr(t["w_attn_o"]), tr(t["w_conv_o"]), tr(t["w_ssm_o"])], axis=2)
    return [tr(t["w_in"]), tr(t["w_ffn_in"]), t["w_ffn_out"], t["w_mix_o"], branch, t["w_ssm_glu"]]


def _native_layout(a):
    tr = lambda x: jnp.swapaxes(x, 1, 2)
    b = a[4]
    return {"w_in": tr(a[0]), "w_ffn_in": tr(a[1]), "w_ffn_out": a[2], "w_mix_o": a[3],
            "w_attn_o": tr(b[:, :, :WIDTH]), "w_conv_o": tr(b[:, :, WIDTH:2 * WIDTH]),
            "w_ssm_o": tr(b[:, :, 2 * WIDTH:]), "w_ssm_glu": a[5]}


def _embed(t):
    eye = jnp.eye(8, dtype=t.dtype)
    t = t.reshape(DEPTH, N_LANE_GROUPS, 8, SSM_GROUP, SSM_STATE)
    return (t[:, :, :, :, None, :] * eye[None, None, :, None, :, None]).reshape(DEPTH, N_LANE_GROUPS, 128, LANES_G)


def _diag_blocks(t):
    t = t.reshape(DEPTH, N_LANE_GROUPS, 8, SSM_GROUP, 8, SSM_STATE)
    return jnp.einsum("lgahap->lgahp", t).reshape(DEPTH, SSM_GROUPS, SSM_GROUP, SSM_STATE)


def _rope_tabs():
    pos = jnp.arange(SEQ, dtype=F32)
    inv_freq = ROPE_THETA ** (-jnp.arange(0, ROT_DIM, 2, dtype=F32) / ROT_DIM)
    ang = pos[:, None] * inv_freq[None, :]
    cos, sin = jnp.cos(ang), jnp.sin(ang)
    one, zero = jnp.ones((SEQ, HEAD_DIM - ROT_DIM), F32), jnp.zeros((SEQ, HEAD_DIM - ROT_DIM), F32)
    z8 = jnp.zeros((SEQ, 8), F32)
    head = lambda *p: jnp.tile(jnp.concatenate(p, axis=1), (1, 2))
    return head(cos, cos, one), head(-sin, z8, zero), head(z8, sin, zero)


def _ssm_mats(sp):
    lr, li, bbr, bbi = _ssm_prep(sp["a_re"], sp["a_im"], sp["log_dt"], sp["bt_re"], sp["bt_im"])
    lanes = SSM_GROUPS * SSM_STATE
    return {
        "a_re": lr.reshape(DEPTH, 1, lanes), "a_im": li.reshape(DEPTH, 1, lanes),
        "b_re": _embed(bbr).astype(BF16), "b_im": _embed(bbi).astype(BF16),
        "c_re": _embed(sp["c_re"]).astype(BF16), "c_im_neg": _embed(-sp["c_im"]).astype(BF16),
    }


def _layer_fwd(x, i, w, rp, mats, tabs, tie, mid):
    q, kv, cbx, u, u16, glog, h = _rms_mm_in(x, rp["norm_mix"][i], w["win_t"], tie)
    o = _attn_fwd(q, kv, tabs, rp["attn_sinks"][i])
    cv = _conv_fwd(cbx, rp["conv_w"], i)
    u16, u = _scan_order(u16), _scan_order(u)
    x_re, x_im, y = _ssm_fwd(u16, u, mats, i, rp["ssm_d"])
    z = _time_order(_glu_fwd(y, w["wglu"]))
    x1 = _mix_fwd(x, o, cv, z, glog, rp["b_gate"], i, w["branch_t"], w["wmix"])
    gu, h2 = _rms_mm_ffn(x1, rp["norm_ffn"][i], w["wffn_t"])
    x2 = _ffn_out_fwd(x1, gu, w["wout"], mid(h2))
    kept = dict(x=x, q=q, kv=kv, cbx=cbx, u=u, u16=u16, glog=glog, h=h, o=o, cv=cv, z=z, y=y,
                x_re=x_re, x_im=x_im, x1=x1, gu=gu, h2=h2)
    return x2, kept


def _layer_bwd(dx2, k, i, w, rp, mats, tabs, tie, mid):
    tn = dict(ta=True, out_dtype=BF16)
    dgu, act = _ffn_out_bwd(dx2, k["gu"], w["wout"], tie)
    g_wout = _mm(act, dx2, tm=FFN_H // 2, tn=1024, tk=512, name="mm_tn_ffn_out", **tn)
    g_wffn_t = _mm(dgu, k["h2"], tm=FFN_H // 2, tn=1024, tk=512, name="mm_tn_ffn_in", **tn)
    dx1, d_norm_ffn = _mm_rmsbwd([dgu], w["wffn_t"], k["x1"], rp["norm_ffn"][i], dx2, "mm_rmsbwd_ffn")

    mg, dya, dyc, dys, do, dcv, dz, dgl, db_gate = _mix_bwd(
        dx1, k["o"], k["cv"], k["z"], k["glog"], rp["b_gate"], i, w["branch_t"], w["wmix"], mid(d_norm_ffn))
    g_wmix = _mm(mg, dx1, tm=1024, tn=1024, tk=512, name="mm_tn_mix", **tn)
    g_branch_t = _tn_branches((dya, dyc, dys), (k["o"], k["cv"], k["z"]))

    dy16, ys16, da16, dd = _glu_bwd(k["y"], w["wglu"], _scan_order(dz), k["u"])
    g_wglu = _mm(ys16, da16, tm=512, tn=512, tk=512, name="mm_tn_glu", **tn)
    du, da_re, da_im, db_re, db_im, dc_re, dc_im = _ssm_bwd(dy16, k["x_re"], k["x_im"], k["u16"], mats, i,
                                                             rp["ssm_d"])
    du = _time_order(du)

    dcb, dcc, dcx, d_conv_w = _conv_bwd(k["cbx"], rp["conv_w"], i, dcv)
    dq, dkv, d_sinks = _attn_bwd(k["q"], k["kv"], tabs, rp["attn_sinks"][i], do)

    pieces = [dq, dkv, dcb, dcc, dcx, du, dgl]
    g_win_t = _tn_pieces(pieces, k["h"])
    dx, d_norm_mix = _mm_rmsbwd(pieces, w["win_t"], k["x"], rp["norm_mix"][i], dx1, "mm_rmsbwd_in")

    grads = [g_win_t, g_wffn_t, g_wout, g_wmix, g_branch_t, g_wglu]
    small = dict(norm_mix=d_norm_mix, b_gate=db_gate, attn_sinks=d_sinks, ssm_d=dd, norm_ffn=d_norm_ffn,
                 conv_w=d_conv_w, da_re=da_re, da_im=da_im, db_re=db_re, db_im=db_im, dc_re=dc_re, dc_im=dc_im)
    return dx, grads, small


def _replicated_grads(sg, sp):
    stack = lambda name: jnp.stack([sg[i][name] for i in range(DEPTH)])
    cots = (stack("da_re").reshape(DEPTH, *_GS), stack("da_im").reshape(DEPTH, *_GS),
            _diag_blocks(stack("db_re")), _diag_blocks(stack("db_im")))
    d_a_re, d_a_im, d_log_dt, d_bt_re, d_bt_im = _ssm_prep_bwd(
        sp["a_re"], sp["a_im"], sp["log_dt"], sp["bt_re"], sp["bt_im"], cots)
    sgrads = {"norm_mix": stack("norm_mix"), "b_gate": stack("b_gate"),
              "attn_sinks": stack("attn_sinks")[:, :, :N_Q_HEADS], "ssm_a_re": d_a_re, "ssm_a_im": d_a_im,
              "ssm_b_re": jnp.swapaxes(d_bt_re, 2, 3), "ssm_b_im": jnp.swapaxes(d_bt_im, 2, 3),
              "ssm_c_re": _diag_blocks(stack("dc_re")), "ssm_c_im": -_diag_blocks(stack("dc_im")),
              "ssm_d": stack("ssm_d"), "ssm_log_dt": d_log_dt, "norm_ffn": stack("norm_ffn")}
    return sgrads, stack("conv_w")[:, :3]


def kernel(x, norm_mix, w_in, b_gate, attn_sinks, w_attn_o, conv_w, w_conv_o, ssm_a_re, ssm_a_im, ssm_b_re, ssm_b_im, ssm_c_re, ssm_c_im, ssm_d, ssm_log_dt, w_ssm_glu, w_ssm_o, w_mix_o, norm_ffn, w_ffn_in, w_ffn_out, norm_final, loss_target, m_norm_mix, m_w_in, m_b_gate, m_attn_sinks, m_w_attn_o, m_conv_w, m_w_conv_o, m_ssm_a_re, m_ssm_a_im, m_ssm_b_re, m_ssm_b_im, m_ssm_c_re, m_ssm_c_im, m_ssm_d, m_ssm_log_dt, m_w_ssm_glu, m_w_ssm_o, m_w_mix_o, m_norm_ffn, m_w_ffn_in, m_w_ffn_out, m_norm_final, v_norm_mix, v_w_in, v_b_gate, v_attn_sinks, v_w_attn_o, v_conv_w, v_w_conv_o, v_ssm_a_re, v_ssm_a_im, v_ssm_b_re, v_ssm_b_im, v_ssm_c_re, v_ssm_c_im, v_ssm_d, v_ssm_log_dt, v_w_ssm_glu, v_w_ssm_o, v_w_mix_o, v_norm_ffn, v_w_ffn_in, v_w_ffn_out, v_norm_final):
    big = {"w": dict(w_in=w_in, w_attn_o=w_attn_o, w_conv_o=w_conv_o, w_ssm_glu=w_ssm_glu, w_ssm_o=w_ssm_o,
                     w_mix_o=w_mix_o, w_ffn_in=w_ffn_in, w_ffn_out=w_ffn_out),
           "m": dict(w_in=m_w_in, w_attn_o=m_w_attn_o, w_conv_o=m_w_conv_o, w_ssm_glu=m_w_ssm_glu,
                     w_ssm_o=m_w_ssm_o, w_mix_o=m_w_mix_o, w_ffn_in=m_w_ffn_in, w_ffn_out=m_w_ffn_out),
           "v": dict(w_in=v_w_in, w_attn_o=v_w_attn_o, w_conv_o=v_w_conv_o, w_ssm_glu=v_w_ssm_glu,
                     w_ssm_o=v_w_ssm_o, w_mix_o=v_w_mix_o, w_ffn_in=v_w_ffn_in, w_ffn_out=v_w_ffn_out)}
    small = {"w": dict(norm_mix=norm_mix, b_gate=b_gate, attn_sinks=attn_sinks, ssm_a_re=ssm_a_re,
                       ssm_a_im=ssm_a_im, ssm_b_re=ssm_b_re, ssm_b_im=ssm_b_im, ssm_c_re=ssm_c_re,
                       ssm_c_im=ssm_c_im, ssm_d=ssm_d, ssm_log_dt=ssm_log_dt, norm_ffn=norm_ffn),
             "m": dict(norm_mix=m_norm_mix, b_gate=m_b_gate, attn_sinks=m_attn_sinks, ssm_a_re=m_ssm_a_re,
                       ssm_a_im=m_ssm_a_im, ssm_b_re=m_ssm_b_re, ssm_b_im=m_ssm_b_im, ssm_c_re=m_ssm_c_re,
                       ssm_c_im=m_ssm_c_im, ssm_d=m_ssm_d, ssm_log_dt=m_ssm_log_dt, norm_ffn=m_norm_ffn),
             "v": dict(norm_mix=v_norm_mix, b_gate=v_b_gate, attn_sinks=v_attn_sinks, ssm_a_re=v_ssm_a_re,
                       ssm_a_im=v_ssm_a_im, ssm_b_re=v_ssm_b_re, ssm_b_im=v_ssm_b_im, ssm_c_re=v_ssm_c_re,
                       ssm_c_im=v_ssm_c_im, ssm_d=v_ssm_d, ssm_log_dt=v_ssm_log_dt, norm_ffn=v_norm_ffn)}
    finals = {"w": norm_final, "m": m_norm_final, "v": v_norm_final}
    convs = {"w": conv_w, "m": m_conv_w, "v": v_conv_w}
    mine = 4 * lax.axis_index("x") + 2 * lax.axis_index("y") + lax.axis_index("c")

    travel = {s: _travel_layout(big[s]) for s in "wmv"}
    stacked16 = [a.astype(BF16) for a in travel["w"]]
    conv_all = _all_gather(jnp.pad(conv_w.reshape(6, 128), ((0, 2), (0, 0))), "gather_conv_w")
    conv_full = conv_all[:, :6].reshape(N_DEV, DEPTH, 3, 64).transpose(1, 2, 0, 3).reshape(DEPTH, 3, WIDTH)
    rp = {"norm_mix": norm_mix[:, None], "norm_ffn": norm_ffn[:, None], "attn_sinks": attn_sinks[:, None],
          "b_gate": b_gate[:, None], "ssm_d": ssm_d[:, None], "conv_w": jnp.pad(conv_full, ((0, 0), (0, 5), (0, 0)))}
    sp = {"a_re": ssm_a_re, "a_im": ssm_a_im, "log_dt": ssm_log_dt[:, :, None],
          "bt_re": jnp.swapaxes(ssm_b_re, 2, 3), "bt_im": jnp.swapaxes(ssm_b_im, 2, 3),
          "c_re": ssm_c_re, "c_im": ssm_c_im}
    rows_tile = {"win_t": 368, "wffn_t": 352, "wout": 176, "wmix": 128, "branch_t": 128, "wglu": 64}
    core = lax.axis_index("c").astype(jnp.int32).reshape(1)
    no_tie = jnp.zeros((8, 128), F32)

    def gather_chips(i, after):
        srcs = [a[i] for a in stacked16]
        lands = [lax.dynamic_update_slice(lax.empty((N_DEV, r, c), BF16), s[None], (mine, 0, 0))
                 for s, (_, r, c) in zip(srcs, KINDS)]
        s_sems, r_sems, arrays, token = _split_start(
            f"gather_chips_start_{i}", srcs + lands, 4 * N_KINDS, _plan_gather_chips, after)
        return (s_sems, r_sems, arrays), token

    def gather_pass(i, state, after):
        arrays = _split_wait(f"gather_chips_wait_{i}", state[2], state[0], state[1], after, _plan_gather_chips)
        s_sems, r_sems, lands, token = _split_start(
            f"gather_pass_start_{i}", arrays[N_KINDS:], 3 * N_KINDS, _plan_gather_pass)
        return (s_sems, r_sems, lands), token

    def gather_done(i, state, after):
        lands = _split_wait(f"gather_pass_wait_{i}", state[2], state[0], state[1], after, _plan_gather_pass)
        return {name: a.reshape(N_DEV * r, c) for a, (name, r, c) in zip(lands, KINDS)}

    state, _ = gather_chips(0, None)
    mats = _ssm_mats(sp)
    tabs = _rope_tabs()
    state, _ = gather_pass(0, state, mats["c_im_neg"])
    w_next = gather_done(0, state, tabs[2])

    act = x[0]
    weights, kept = [], []
    for i in range(DEPTH):
        w_i = w_next
        if i + 1 < DEPTH:
            state, tie = gather_chips(i + 1, w_i["win_t"])
            held = {}

            def mid(value, i=i, state=state, held=held):
                held["state"], token = gather_pass(i + 1, state, value)
                return token
        else:
            tie, mid = no_tie, (lambda value: no_tie)
        act, k = _layer_fwd(act, i, w_i, rp, mats, tabs, tie, mid)
        if i + 1 < DEPTH:
            w_next = gather_done(i + 1, held["state"], act)
        weights.append(w_i)
        kept.append(k)
    loss_row, dx, d_norm_final = _loss_head(act, norm_final[None], loss_target[0])
    loss = lax.psum(loss_row[0, 0], ("x", "y", "c"))

    landings = [lax.empty((DEPTH, 4, r, c), BF16) for _, r, c in KINDS]
    landings0 = [lax.empty((1, 4, r, c), BF16) for _, r, c in KINDS]

    def scatter_pair(i, grads, after):
        parts4 = [g.reshape(4, 2, r, c) for g, (_, r, c) in zip(grads, KINDS)]
        zones = [lax.empty((4, r, c), BF16) for _, r, c in KINDS]
        s_sems, r_sems, arrays, token = _split_start(
            f"scatter_pair_start_{i}", parts4 + zones, 4 * N_KINDS, _plan_scatter_pair, after)
        return (i, s_sems, r_sems, arrays), token

    def scatter_chips(state, lands, after):
        i, s_sems, r_sems, arrays = state
        arrays = _split_wait(f"scatter_pair_wait_{i}", arrays, s_sems, r_sems, after, _plan_scatter_pair)
        slot = i if i > 0 else 0
        sums, lands = [], list(lands)
        for j, (name, _, _) in enumerate(KINDS):
            chip_sum, lands[j] = _pair_sum(arrays[j], arrays[N_KINDS + j], lands[j], slot, core, rows_tile[name],
                                           f"pair_sum_{name}")
            sums.append(chip_sum)
        s_sems, r_sems, arrays, token = _split_start(
            f"scatter_chips_start_{i}", sums + lands, 3 * N_KINDS, _plan_scatter_chips(slot))
        return (i, s_sems, r_sems, arrays), token

    def scatter_done(state, after):
        i, s_sems, r_sems, arrays = state
        slot = i if i > 0 else 0
        arrays = _split_wait(f"scatter_chips_wait_{i}", arrays, s_sems, r_sems, after, _plan_scatter_chips(slot))
        return list(arrays[N_KINDS:])

    sg = [None] * DEPTH
    pending, tie = None, no_tie
    for i in reversed(range(DEPTH)):
        held = {}
        if pending is None:
            mid = lambda value: no_tie
        else:
            def mid(value, pending=pending, held=held):
                held["state"], token = scatter_chips(pending, landings, value)
                return token
        dx, grads, sg[i] = _layer_bwd(dx, kept[i], i, weights[i], rp, mats, tabs, tie, mid)
        if pending is not None:
            landings = scatter_done(held["state"], dx)
        pending, tie = scatter_pair(i, grads, dx)

    sgrads, conv_grad = _replicated_grads(sg, sp)
    last, tie = scatter_chips(pending, landings0, sgrads["ssm_a_re"])

    big_out = [_adamw(landings[j], travel["w"][j], travel["m"][j], travel["v"][j], rows_tile[name],
                      "adamw_late_" + name, groups=(1, DEPTH), tie=tie) for j, (name, _, _) in enumerate(KINDS)]
    landings0 = scatter_done(last, big_out[-1][0])
    big_out = [_adamw(landings0[j], travel["w"][j], travel["m"][j], travel["v"][j], rows_tile[name],
                      "adamw_first_" + name, groups=(0, 1), fill=big_out[j]) for j, (name, _, _) in enumerate(KINDS)]
    big_res = [_native_layout([big_out[j][kind] for j in range(len(KINDS))]) for kind in range(4)]

    def pack_small(t, final, conv):
        flat = [t[name].reshape(DEPTH, n) for name, n in SMALL]
        flat = jnp.concatenate([jnp.concatenate(flat, axis=1).reshape(-1), final.reshape(-1), conv.reshape(-1)])
        return jnp.pad(flat, (0, SMALL_ROWS * 128 - flat.shape[0])).reshape(SMALL_ROWS, 128)

    zeros_conv = jnp.zeros((CONV_N,), F32)
    sparts = _all_gather(pack_small(sgrads, d_norm_final, conv_grad), "gather_small_grads")
    sw, sm_, sv = (pack_small(small[s], finals[s], zeros_conv) for s in "wmv")
    small_out = _adamw(sparts[None], sw[None], sm_[None], sv[None], SMALL_ROWS // 8, "adamw_replicated")

    def unpack_small(p):
        flat = p.reshape(-1)
        per = flat[:DEPTH * SMALL_PER_LAYER].reshape(DEPTH, SMALL_PER_LAYER)
        out, off = {}, 0
        for name, n in SMALL:
            out[name] = per[:, off:off + n].reshape(small["w"][name].shape)
            off += n
        out["norm_final"] = flat[DEPTH * SMALL_PER_LAYER:DEPTH * SMALL_PER_LAYER + D_MODEL]
        return out

    small_res = [unpack_small(p) for p in small_out]

    conv_off = DEPTH * SMALL_PER_LAYER + D_MODEL
    conv_parts = sparts.reshape(N_DEV, -1)[:, conv_off:conv_off + CONV_N].reshape(N_DEV, DEPTH * 3, WIDTH)
    conv_parts = lax.dynamic_slice_in_dim(conv_parts, mine * 64, 64, axis=2)
    conv_res = _adamw(conv_parts[None], *(convs[s].reshape(1, DEPTH * 3, 64) for s in "wmv"), DEPTH * 3, "adamw_conv_w")

    order = ["norm_mix", "w_in", "b_gate", "attn_sinks", "w_attn_o", "conv_w", "w_conv_o", "ssm_a_re", "ssm_a_im",
             "ssm_b_re", "ssm_b_im", "ssm_c_re", "ssm_c_im", "ssm_d", "ssm_log_dt", "w_ssm_glu", "w_ssm_o",
             "w_mix_o", "norm_ffn", "w_ffn_in", "w_ffn_out", "norm_final"]
    outs = [loss, dx[None]]
    for kind in range(4):
        for name in order:
            if name == "conv_w":
                outs.append(conv_res[kind].reshape(DEPTH, 3, 64))
            elif name in big_res[kind]:
                outs.append(big_res[kind][name])
            else:
                outs.append(small_res[kind][name])
    return tuple(outs)
```

```python
import functools
import math

import jax
import jax.numpy as jnp
from jax import lax
from jax.experimental import pallas as pl
from jax.experimental.pallas import tpu as pltpu

F32 = jnp.float32
BF16 = jnp.bfloat16

N_DEV = 8
DEPTH = 4
SEQ = 2048
D_MODEL = 1024
N_Q_HEADS = 8
HEAD_DIM = 64
ATTN_W = 512
KV_W = 128
BLOCK = 128
N_BLOCKS = SEQ // BLOCK
ROPE_THETA = 500000.0
ROT_DIM = 16
NEG_INF = -1e30
WIDTH = 512
SSM_GROUPS = 32
SSM_GROUP = 16
SSM_STATE = 64
SLABS = 16
CHUNK = 256
N_CHUNKS = SEQ // CHUNK
GATE_W = 3 * D_MODEL
IN_COLS = 5888
FFN_H = 2816
NORM_EPS = 1e-6
LR, B1, B2, ADAM_EPS, WD, STEP = 0.001, 0.9, 0.999, 1e-08, 0.01, 10

COL_Q, COL_KV, COL_CBX, COL_U, COL_G = 0, 512, 768, 2304, 2816
PIECE_W = (512, 256, 512, 512, 512, 512, 3072)
PIECE_OFF = tuple(sum(PIECE_W[:i]) for i in range(len(PIECE_W)))

KINDS = (("win_t", 736, 1024), ("wffn_t", 704, 1024), ("wout", 352, 1024), ("wmix", 128, 1024),
         ("branch_t", 128, 1536), ("wglu", 64, 512))

SMALL = (("norm_mix", 1024), ("b_gate", 3072), ("attn_sinks", 8), ("ssm_a_re", 2048), ("ssm_a_im", 2048),
         ("ssm_b_re", 32768), ("ssm_b_im", 32768), ("ssm_c_re", 32768), ("ssm_c_im", 32768),
         ("ssm_d", 512), ("ssm_log_dt", 32), ("norm_ffn", 1024))
SMALL_PER_LAYER = sum(n for _, n in SMALL)
CONV_N = DEPTH * 3 * WIDTH
SMALL_ROWS = 4480

VMEM_LIMIT = 56 * 1024 * 1024
NT = (((1,), (1,)), ((), ()))
TN = (((0,), (0,)), ((), ()))
MESH_ID = pl.DeviceIdType.MESH
ANY = pl.BlockSpec(memory_space=pl.ANY)
HBM = pl.BlockSpec(memory_space=pltpu.HBM)
SEM = pl.BlockSpec(memory_space=pltpu.SEMAPHORE)
EFFECT = pltpu.SideEffectType.DATAFLOW_SIDE_EFFECTING


def _cp(**kw):
    return pltpu.CompilerParams(vmem_limit_bytes=VMEM_LIMIT, **kw)


def _full(shape):
    return pl.BlockSpec(shape, lambda *_: (0,) * len(shape))


def _mm(a, b, *, ta=False, tb=False, tm, tn, tk, out_dtype=F32, name):
    m = a.shape[1] if ta else a.shape[0]
    k = a.shape[0] if ta else a.shape[1]
    n = b.shape[0] if tb else b.shape[1]
    nk = k // tk
    dims = (((0 if ta else 1,), (1 if tb else 0,)), ((), ()))

    def body(a_ref, b_ref, o_ref, acc_ref):
        kk = pl.program_id(2)

        @pl.when(kk == 0)
        def _():
            acc_ref[...] = jnp.zeros_like(acc_ref)

        acc_ref[...] += lax.dot_general(a_ref[...].astype(BF16), b_ref[...].astype(BF16), dims,
                                        preferred_element_type=F32)

        @pl.when(kk == nk - 1)
        def _():
            o_ref[...] = acc_ref[...].astype(out_dtype)

    a_spec = pl.BlockSpec((tk, tm), lambda i, j, kk: (kk, i)) if ta else pl.BlockSpec((tm, tk), lambda i, j, kk: (i, kk))
    b_spec = pl.BlockSpec((tn, tk), lambda i, j, kk: (j, kk)) if tb else pl.BlockSpec((tk, tn), lambda i, j, kk: (kk, j))
    return pl.pallas_call(
        body, grid=(m // tm, n // tn, nk), in_specs=[a_spec, b_spec],
        out_specs=pl.BlockSpec((tm, tn), lambda i, j, kk: (i, j)),
        out_shape=jax.ShapeDtypeStruct((m, n), out_dtype),
        scratch_shapes=[pltpu.VMEM((tm, tn), F32)], compiler_params=_cp(), name=name)(a, b)


def _rms_rows(xv, g):
    r = lax.rsqrt(jnp.mean(xv * xv, axis=-1, keepdims=True) + NORM_EPS)
    return ((xv * r) * g).astype(BF16)


def _rms_mm_in(x, g, wt, tie):
    tt = 256
    widths = (ATTN_W, 2 * KV_W, 3 * WIDTH, WIDTH, GATE_W)
    offs = (COL_Q, COL_KV, COL_CBX, COL_U, COL_G)

    def body(x_ref, g_ref, w_ref, tie_ref, q_ref, kv_ref, cbx_ref, u_ref, u16_ref, gl_ref, h_ref):
        h = _rms_rows(x_ref[...], g_ref[...])
        h_ref[...] = h
        prod = lax.dot_general(h, w_ref[...], NT, preferred_element_type=F32)
        for ref, o, w in zip((q_ref, kv_ref, cbx_ref, u_ref, gl_ref), offs, widths):
            ref[...] = prod[:, o:o + w]
        u16_ref[...] = prod[:, COL_U:COL_U + WIDTH].astype(BF16)

    row = lambda w: pl.BlockSpec((tt, w), lambda i: (i, 0))
    sds = jax.ShapeDtypeStruct
    return pl.pallas_call(
        body, grid=(SEQ // tt,), in_specs=[row(D_MODEL), _full((1, D_MODEL)), _full((IN_COLS, D_MODEL)), ANY],
        out_specs=[row(ATTN_W), row(2 * KV_W), row(3 * WIDTH), row(WIDTH), row(WIDTH), row(GATE_W), row(D_MODEL)],
        out_shape=[sds((SEQ, ATTN_W), F32), sds((SEQ, 2 * KV_W), F32), sds((SEQ, 3 * WIDTH), F32),
                   sds((SEQ, WIDTH), F32), sds((SEQ, WIDTH), BF16), sds((SEQ, GATE_W), F32),
                   sds((SEQ, D_MODEL), BF16)],
        compiler_params=_cp(), name="rms_mm_in")(x, g, wt, tie)


def _rms_mm_ffn(x, g, wt):
    tt = 256

    def body(x_ref, g_ref, w_ref, o_ref, h_ref):
        h = _rms_rows(x_ref[...], g_ref[...])
        h_ref[...] = h
        o_ref[...] = lax.dot_general(h, w_ref[...], NT, preferred_element_type=F32)

    row = lambda w: pl.BlockSpec((tt, w), lambda i: (i, 0))
    return pl.pallas_call(
        body, grid=(SEQ // tt,), in_specs=[row(D_MODEL), _full((1, D_MODEL)), _full((2 * FFN_H, D_MODEL))],
        out_specs=[row(2 * FFN_H), row(D_MODEL)],
        out_shape=[jax.ShapeDtypeStruct((SEQ, 2 * FFN_H), F32), jax.ShapeDtypeStruct((SEQ, D_MODEL), BF16)],
        compiler_params=_cp(), name="rms_mm_ffn")(x, g, wt)


def _mm_rmsbwd(pieces, wt, x, g, dres, name):
    tt = 256
    widths = [p.shape[1] for p in pieces]
    offs = [sum(widths[:i]) for i in range(len(widths))]
    n = len(pieces)

    def body(*refs):
        p_refs, (w_ref, x_ref, g_ref, r_ref, dx_ref, dg_ref) = refs[:n], refs[n:]

        @pl.when(pl.program_id(0) == 0)
        def _():
            dg_ref[...] = jnp.zeros_like(dg_ref)

        dh = jnp.zeros((tt, D_MODEL), F32)
        for p_ref, o, w in zip(p_refs, offs, widths):
            dh += jnp.dot(p_ref[...], w_ref[o:o + w, :], preferred_element_type=F32)
        xv = x_ref[...]
        r = lax.rsqrt(jnp.mean(xv * xv, axis=-1, keepdims=True) + NORM_EPS)
        xh = xv * r
        gy = dh * g_ref[...]
        dx_ref[...] = r_ref[...] + r * (gy - xh * jnp.mean(gy * xh, axis=-1, keepdims=True))
        dg_ref[...] += jnp.sum(dh * xh, axis=0, keepdims=True)

    row = lambda w: pl.BlockSpec((tt, w), lambda i: (i, 0))
    return pl.pallas_call(
        body, grid=(SEQ // tt,),
        in_specs=[row(w) for w in widths] + [_full(wt.shape), row(D_MODEL), _full((1, D_MODEL)), row(D_MODEL)],
        out_specs=[row(D_MODEL), _full((1, D_MODEL))],
        out_shape=[jax.ShapeDtypeStruct((SEQ, D_MODEL), F32), jax.ShapeDtypeStruct((1, D_MODEL), F32)],
        compiler_params=_cp(), name=name)(*pieces, wt, x, g, dres)


def _tn_pieces(pieces, h):
    tk, tn = 512, 512
    nk = SEQ // tk
    n = len(pieces)

    def body(*refs):
        p_refs, (h_ref, o_ref, acc_ref) = refs[:n], refs[n:]
        kk = pl.program_id(1)

        @pl.when(kk == 0)
        def _():
            acc_ref[...] = jnp.zeros_like(acc_ref)

        hv = h_ref[...]
        for p_ref, o, w in zip(p_refs, PIECE_OFF, PIECE_W):
            acc_ref[o:o + w, :] += lax.dot_general(p_ref[...], hv, TN, preferred_element_type=F32)

        @pl.when(kk == nk - 1)
        def _():
            o_ref[...] = acc_ref[...].astype(BF16)

    return pl.pallas_call(
        body, grid=(D_MODEL // tn, nk),
        in_specs=[pl.BlockSpec((tk, w), lambda j, kk: (kk, 0)) for w in PIECE_W]
        + [pl.BlockSpec((tk, tn), lambda j, kk: (kk, j))],
        out_specs=pl.BlockSpec((IN_COLS, tn), lambda j, kk: (0, j)),
        out_shape=jax.ShapeDtypeStruct((IN_COLS, D_MODEL), BF16),
        scratch_shapes=[pltpu.VMEM((IN_COLS, tn), F32)], compiler_params=_cp(), name="tn_pieces")(*pieces, h)


def _tn_branches(dys, acts):
    tk = 512
    nk = SEQ // tk

    def body(d0, d1, d2, a0, a1, a2, o_ref, acc_ref):
        kk = pl.program_id(0)

        @pl.when(kk == 0)
        def _():
            acc_ref[...] = jnp.zeros_like(acc_ref)

        for j, (d, a) in enumerate(((d0, a0), (d1, a1), (d2, a2))):
            acc_ref[:, WIDTH * j:WIDTH * (j + 1)] += lax.dot_general(d[...], a[...], TN, preferred_element_type=F32)

        @pl.when(kk == nk - 1)
        def _():
            o_ref[...] = acc_ref[...].astype(BF16)

    row = lambda w: pl.BlockSpec((tk, w), lambda kk: (kk, 0))
    return pl.pallas_call(
        body, grid=(nk,), in_specs=[row(D_MODEL)] * 3 + [row(WIDTH)] * 3,
        out_specs=_full((D_MODEL, 3 * WIDTH)), out_shape=jax.ShapeDtypeStruct((D_MODEL, 3 * WIDTH), BF16),
        scratch_shapes=[pltpu.VMEM((D_MODEL, 3 * WIDTH), F32)], compiler_params=_cp(), name="tn_branches",
    )(*dys, *acts)


def _rope(t, c, a, b):
    return t * c + pltpu.roll(t, 120, axis=1) * a + pltpu.roll(t, 8, axis=1) * b


def _rope_t(d, c, a, b):
    return d * c + pltpu.roll(d * a, 8, axis=1) + pltpu.roll(d * b, 120, axis=1)


def _band_sides(band):
    left = lax.broadcasted_iota(jnp.int32, band.shape, 1) < HEAD_DIM
    h0 = jnp.where(left, band, 0.0)
    h1 = jnp.where(left, 0.0, band)
    r0 = pltpu.roll(h0, HEAD_DIM, axis=1)
    r1 = pltpu.roll(h1, HEAD_DIM, axis=1)
    return ((h0.astype(BF16), r0.astype(BF16)), (r1.astype(BF16), h1.astype(BF16)))


def _attn_mask(i):
    qi = lax.broadcasted_iota(jnp.int32, (BLOCK, 2 * BLOCK), 0)
    kj = lax.broadcasted_iota(jnp.int32, (BLOCK, 2 * BLOCK), 1)
    delta = qi + BLOCK - kj
    return (delta >= 0) & (delta < BLOCK) & ((kj >= BLOCK) | (i > 0))


def _attn_probs(qc, kside, ok, sink):
    s = lax.dot_general(qc, kside, NT, preferred_element_type=F32) * (HEAD_DIM ** -0.5)
    s = jnp.where(ok, s, NEG_INF)
    m = jnp.maximum(jnp.max(s, axis=-1, keepdims=True), sink)
    p = jnp.exp(s - m)
    es = jnp.exp(sink - m)
    inv = 1.0 / (jnp.sum(p, axis=-1, keepdims=True) + es)
    return p * inv, es * inv


def _attn_load(q_ref, kvc_ref, kvp_ref, tc_ref, ta_ref, tb_ref, pc_ref, pa_ref, pb_ref):
    c, a, b = tc_ref[...], ta_ref[...], tb_ref[...]
    kc = _rope(kvc_ref[:, :KV_W], c, a, b)
    kp = _rope(kvp_ref[:, :KV_W], pc_ref[...], pa_ref[...], pb_ref[...])
    kband = jnp.concatenate([kp, kc], axis=0)
    vband = jnp.concatenate([kvp_ref[:, KV_W:], kvc_ref[:, KV_W:]], axis=0)
    qs = [_rope(q_ref[:, 128 * j:128 * (j + 1)], c, a, b).astype(BF16) for j in range(4)]
    return qs, _band_sides(kband), _band_sides(vband), (c, a, b)


def _attn_specs(clamp):
    cur = lambda i: (clamp(i), 0)
    prev = lambda i: (jnp.maximum(clamp(i) - 1, 0), 0)
    return [
        pl.BlockSpec((BLOCK, ATTN_W), cur), pl.BlockSpec((BLOCK, 2 * KV_W), cur),
        pl.BlockSpec((BLOCK, 2 * KV_W), prev),
        pl.BlockSpec((BLOCK, 128), cur), pl.BlockSpec((BLOCK, 128), cur), pl.BlockSpec((BLOCK, 128), cur),
        pl.BlockSpec((BLOCK, 128), prev), pl.BlockSpec((BLOCK, 128), prev), pl.BlockSpec((BLOCK, 128), prev),
        pl.BlockSpec(memory_space=pltpu.SMEM),
    ]


def _attn_fwd(q, kv, tabs, sinks):
    tc, ta, tb = tabs

    def body(q_ref, kvc_ref, kvp_ref, tc_ref, ta_ref, tb_ref, pc_ref, pa_ref, pb_ref, sink_ref, o_ref):
        i = pl.program_id(0)
        qs, ks, vs, _ = _attn_load(q_ref, kvc_ref, kvp_ref, tc_ref, ta_ref, tb_ref, pc_ref, pa_ref, pb_ref)
        ok = _attn_mask(i)
        for j in range(4):
            kh = j // 2
            acc = jnp.zeros((BLOCK, 128), F32)
            for side in range(2):
                pn, _ = _attn_probs(qs[j], ks[kh][side], ok, sink_ref[0, 2 * j + side])
                acc += jnp.dot(pn.astype(BF16), vs[kh][side], preferred_element_type=F32)
            o_ref[:, 128 * j:128 * (j + 1)] = acc.astype(BF16)

    return pl.pallas_call(
        body, grid=(N_BLOCKS,), in_specs=_attn_specs(lambda i: i),
        out_specs=pl.BlockSpec((BLOCK, ATTN_W), lambda i: (i, 0)),
        out_shape=jax.ShapeDtypeStruct((SEQ, ATTN_W), BF16), compiler_params=_cp(), name="attn_fwd",
    )(q, kv, kv, tc, ta, tb, tc, ta, tb, sinks)


def _attn_bwd(q, kv, tabs, sinks, do):
    tc, ta, tb = tabs
    last = N_BLOCKS - 1
    clamp = lambda i: jnp.minimum(i, last)

    def place(full, side, kh):
        left = lax.broadcasted_iota(jnp.int32, full.shape, 1) < HEAD_DIM
        valid = jnp.where(left, full, 0.0) if side == 0 else jnp.where(left, 0.0, full)
        return valid if side == kh else pltpu.roll(valid, HEAD_DIM, axis=1)

    def body(q_ref, kvc_ref, kvp_ref, tc_ref, ta_ref, tb_ref, pc_ref, pa_ref, pb_ref, sink_ref, do_ref,
             dq_ref, dkv_ref, ds_ref, carry_ref):
        i = pl.program_id(0)

        @pl.when(i == 0)
        def _():
            ds_ref[...] = jnp.zeros_like(ds_ref)
            carry_ref[...] = jnp.zeros_like(carry_ref)

        @pl.when(i > last)
        def _():
            dkv_ref[...] = carry_ref[...].astype(BF16)

        @pl.when(i <= last)
        def _():
            qs, ks, vs, (c, a, b) = _attn_load(q_ref, kvc_ref, kvp_ref, tc_ref, ta_ref, tb_ref,
                                               pc_ref, pa_ref, pb_ref)
            ok = _attn_mask(i)
            dk = jnp.zeros((2 * BLOCK, 128), F32)
            dv = jnp.zeros((2 * BLOCK, 128), F32)
            dsink = jnp.zeros((1, 128), F32)
            lane = lax.broadcasted_iota(jnp.int32, (1, 128), 1)
            for j in range(4):
                kh = j // 2
                doc = do_ref[:, 128 * j:128 * (j + 1)].astype(BF16)
                dq = jnp.zeros((BLOCK, 128), F32)
                for side in range(2):
                    pn, ps = _attn_probs(qs[j], ks[kh][side], ok, sink_ref[0, 2 * j + side])
                    dp = lax.dot_general(doc, vs[kh][side], NT, preferred_element_type=F32)
                    dr = jnp.sum(pn * dp, axis=-1, keepdims=True)
                    dsb = (pn * (dp - dr) * (HEAD_DIM ** -0.5)).astype(BF16)
                    dsink += jnp.where(lane == 2 * j + side, -jnp.sum(ps * dr), 0.0)
                    dq += jnp.dot(dsb, ks[kh][side], preferred_element_type=F32)
                    dk += place(lax.dot_general(dsb, qs[j], TN, preferred_element_type=F32), side, kh)
                    dv += place(lax.dot_general(pn.astype(BF16), doc, TN, preferred_element_type=F32), side, kh)
                dq_ref[:, 128 * j:128 * (j + 1)] = _rope_t(dq, c, a, b).astype(BF16)
            ds_ref[...] += dsink
            dk_prev = _rope_t(dk[:BLOCK], pc_ref[...], pa_ref[...], pb_ref[...])
            dk_cur = _rope_t(dk[BLOCK:], c, a, b)
            prev = jnp.concatenate([dk_prev, dv[:BLOCK]], axis=1)
            dkv_ref[...] = (carry_ref[...] + prev).astype(BF16)
            carry_ref[...] = jnp.concatenate([dk_cur, dv[BLOCK:]], axis=1)

    return pl.pallas_call(
        body, grid=(N_BLOCKS + 1,),
        in_specs=_attn_specs(clamp) + [pl.BlockSpec((BLOCK, ATTN_W), lambda i: (clamp(i), 0))],
        out_specs=[pl.BlockSpec((BLOCK, ATTN_W), lambda i: (clamp(i), 0)),
                   pl.BlockSpec((BLOCK, 2 * KV_W), lambda i: (jnp.maximum(i - 1, 0), 0)),
                   pl.BlockSpec((1, 128), lambda i: (0, 0))],
        out_shape=[jax.ShapeDtypeStruct((SEQ, ATTN_W), BF16), jax.ShapeDtypeStruct((SEQ, 2 * KV_W), BF16),
                   jax.ShapeDtypeStruct((1, 128), F32)],
        scratch_shapes=[pltpu.VMEM((BLOCK, 2 * KV_W), F32)], compiler_params=_cp(), name="attn_bwd",
    )(q, kv, kv, tc, ta, tb, tc, ta, tb, sinks, do)


def _shift_down(z, k):
    row = lax.broadcasted_iota(jnp.int32, z.shape, 0)
    return jnp.where(row < k, 0.0, pltpu.roll(z, k, axis=0))


def _shift_up(z, k):
    n = z.shape[0]
    row = lax.broadcasted_iota(jnp.int32, z.shape, 0)
    return jnp.where(row >= n - k, 0.0, pltpu.roll(z, n - k, axis=0))


def _conv_specs():
    nb = WIDTH // 128
    return [pl.BlockSpec((SEQ, 128), lambda j: (0, j)), pl.BlockSpec((SEQ, 128), lambda j: (0, nb + j)),
            pl.BlockSpec((SEQ, 128), lambda j: (0, 2 * nb + j)), pl.BlockSpec((None, 8, 128), lambda j: (0, 0, j))]


def _conv_fwd(cbx, cw, layer):
    def body(cb_ref, cc_ref, cx_ref, w_ref, o_ref):
        z = cc_ref[...] * cx_ref[...]
        s = w_ref[0:1, :] * _shift_down(z, 2) + w_ref[1:2, :] * _shift_down(z, 1) + w_ref[2:3, :] * z
        o_ref[...] = (cb_ref[...] * s).astype(BF16)

    specs = _conv_specs()
    specs[3] = pl.BlockSpec((None, 8, 128), lambda j: (layer, 0, j))
    return pl.pallas_call(
        body, grid=(WIDTH // 128,), in_specs=specs,
        out_specs=pl.BlockSpec((SEQ, 128), lambda j: (0, j)),
        out_shape=jax.ShapeDtypeStruct((SEQ, WIDTH), BF16), compiler_params=_cp(), name="conv_fwd",
    )(cbx, cbx, cbx, cw)


def _conv_bwd(cbx, cw, layer, dout, tie):
    def body(cb_ref, cc_ref, cx_ref, w_ref, do_ref, tie_ref, dcb_ref, dcc_ref, dcx_ref, dw_ref):
        cc, cx = cc_ref[...], cx_ref[...]
        z = cc * cx
        z1, z2 = _shift_down(z, 1), _shift_down(z, 2)
        w0, w1, w2 = w_ref[0:1, :], w_ref[1:2, :], w_ref[2:3, :]
        dout = do_ref[...]
        ds = dout * cb_ref[...]
        dcb_ref[...] = (dout * (w0 * z2 + w1 * z1 + w2 * z)).astype(BF16)
        dz = w2 * ds + w1 * _shift_up(ds, 1) + w0 * _shift_up(ds, 2)
        dcc_ref[...] = (dz * cx).astype(BF16)
        dcx_ref[...] = (dz * cc).astype(BF16)
        rows = [jnp.sum(ds * zz, axis=0, keepdims=True) for zz in (z2, z1, z)]
        dw_ref[...] = jnp.concatenate(rows + [jnp.zeros((5, 128), F32)], axis=0)

    col = lambda j: (0, j)
    specs = _conv_specs()
    specs[3] = pl.BlockSpec((None, 8, 128), lambda j: (layer, 0, j))
    return pl.pallas_call(
        body, grid=(WIDTH // 128,), in_specs=specs + [pl.BlockSpec((SEQ, 128), col), ANY],
        out_specs=[pl.BlockSpec((SEQ, 128), col), pl.BlockSpec((SEQ, 128), col), pl.BlockSpec((SEQ, 128), col),
                   pl.BlockSpec((8, 128), col)],
        out_shape=[jax.ShapeDtypeStruct((SEQ, WIDTH), BF16)] * 3 + [jax.ShapeDtypeStruct((8, WIDTH), F32)],
        compiler_params=_cp(), name="conv_bwd",
    )(cbx, cbx, cbx, cw, dout, tie)


def _ssm_prep_math(a_re, a_im, log_dt, bt_re, bt_im):
    dt = jnp.exp(log_dt)
    er = jnp.exp(a_re * dt)
    lr = er * jnp.cos(a_im * dt)
    li = er * jnp.sin(a_im * dt)
    n2 = a_re * a_re + a_im * a_im
    cr = ((lr - 1.0) * a_re + li * a_im) / n2
    ci = (li * a_re - (lr - 1.0) * a_im) / n2
    cr3, ci3 = cr[:, None, :], ci[:, None, :]
    return lr, li, cr3 * bt_re - ci3 * bt_im, cr3 * bt_im + ci3 * bt_re


_GS = (SSM_GROUPS, SSM_STATE)
_GHS = (SSM_GROUPS, SSM_GROUP, SSM_STATE)


def _layered(shape):
    return pl.BlockSpec((None,) + shape, lambda l: (l,) + (0,) * len(shape))


def _ssm_prep(a_re, a_im, log_dt, bt_re, bt_im):
    def body(ar, ai, ld, br, bi, o0, o1, o2, o3):
        outs = _ssm_prep_math(ar[...], ai[...], ld[...], br[...], bi[...])
        for o, v in zip((o0, o1, o2, o3), outs):
            o[...] = v

    shapes = [_GS, _GS, _GHS, _GHS]
    return pl.pallas_call(
        body, grid=(DEPTH,), in_specs=[_layered(s) for s in (_GS, _GS, (SSM_GROUPS, 1), _GHS, _GHS)],
        out_specs=[_layered(s) for s in shapes],
        out_shape=[jax.ShapeDtypeStruct((DEPTH,) + s, F32) for s in shapes],
        name="ssm_prep")(a_re, a_im, log_dt, bt_re, bt_im)


def _ssm_prep_bwd(a_re, a_im, log_dt, bt_re, bt_im, cots):
    def body(ar, ai, ld, br, bi, c0, c1, c2, c3, o0, o1, o2, o3, o4):
        _, vjp = jax.vjp(_ssm_prep_math, ar[...], ai[...], ld[...], br[...], bi[...])
        for o, v in zip((o0, o1, o2, o3, o4), vjp((c0[...], c1[...], c2[...], c3[...]))):
            o[...] = v

    ins = (_GS, _GS, (SSM_GROUPS, 1), _GHS, _GHS)
    return pl.pallas_call(
        body, grid=(DEPTH,), in_specs=[_layered(s) for s in ins + (_GS, _GS, _GHS, _GHS)],
        out_specs=[_layered(s) for s in ins],
        out_shape=[jax.ShapeDtypeStruct((DEPTH,) + s, F32) for s in ins],
        name="ssm_prep_bwd")(a_re, a_im, log_dt, bt_re, bt_im, *cots)


LANES_G = 512
N_LANE_GROUPS = SSM_GROUPS * SSM_STATE // LANES_G


def _scan_order(a):
    return a.reshape(N_CHUNKS, CHUNK, -1).transpose(1, 0, 2).reshape(a.shape)


def _time_order(a):
    return a.reshape(CHUNK, N_CHUNKS, -1).transpose(1, 0, 2).reshape(a.shape)


def _scan_in_place(xr_ref, xi_ref, ar, ai, reverse):
    shape = (N_CHUNKS, xr_ref.shape[1])
    ar, ai = jnp.broadcast_to(ar, shape), jnp.broadcast_to(ai, shape)

    def rows(tau):
        t = (CHUNK - 1 - tau) if reverse else tau
        return pl.ds(pl.multiple_of(t * N_CHUNKS, N_CHUNKS), N_CHUNKS)

    def step(tau, carry):
        sr, si = carry
        return ar * sr - ai * si + xr_ref[rows(tau), :], ar * si + ai * sr + xi_ref[rows(tau), :]

    zero = jnp.zeros(shape, F32)
    er, ei = lax.fori_loop(0, CHUNK, step, (zero, zero), unroll=8)
    qr, qi = ar, ai
    for _ in range(8):
        qr, qi = qr * qr - qi * qi, 2.0 * qr * qi
    shift = _shift_up if reverse else _shift_down
    for k in (1, 2, 4):
        sr, si = shift(er, k), shift(ei, k)
        er, ei = er + qr * sr - qi * si, ei + qr * si + qi * sr
        qr, qi = qr * qr - qi * qi, 2.0 * qr * qi
    start = (shift(er, 1), shift(ei, 1))

    def write(tau, carry):
        sr, si = step(tau, carry)
        xr_ref[rows(tau), :] = sr
        xi_ref[rows(tau), :] = si
        return sr, si

    return write, start


def _ssm_specs(layer):
    col = lambda w: pl.BlockSpec((SEQ, w), lambda g: (0, g))
    diag = pl.BlockSpec((None, None, 128, LANES_G), lambda g: (layer, g, 0, 0))
    vec = pl.BlockSpec((None, 1, LANES_G), lambda g: (layer, 0, g))
    return col, diag, vec


def _ssm_fwd(u16, u, mats, layer, d):
    def body(u16_ref, u_ref, d_ref, br_ref, bi_ref, cr_ref, ci_ref, ar_ref, ai_ref, xr_ref, xi_ref, y_ref):
        uv = u16_ref[...]
        xr_ref[...] = jnp.dot(uv, br_ref[...], preferred_element_type=F32)
        xi_ref[...] = jnp.dot(uv, bi_ref[...], preferred_element_type=F32)
        write, start = _scan_in_place(xr_ref, xi_ref, ar_ref[...], ai_ref[...], False)
        lax.fori_loop(0, CHUNK, write, start, unroll=8)
        y = lax.dot_general(xr_ref[...].astype(BF16), cr_ref[...], NT, preferred_element_type=F32)
        y += lax.dot_general(xi_ref[...].astype(BF16), ci_ref[...], NT, preferred_element_type=F32)
        y_ref[...] = y + d_ref[...] * u_ref[...]

    col, diag, vec = _ssm_specs(layer)
    return pl.pallas_call(
        body, grid=(N_LANE_GROUPS,),
        in_specs=[col(128), col(128), pl.BlockSpec((None, 1, 128), lambda g: (layer, 0, g)),
                  diag, diag, diag, diag, vec, vec],
        out_specs=[col(LANES_G), col(LANES_G), col(128)],
        out_shape=[jax.ShapeDtypeStruct((SEQ, SSM_GROUPS * SSM_STATE), F32)] * 2
        + [jax.ShapeDtypeStruct((SEQ, WIDTH), F32)],
        compiler_params=_cp(), name="ssm_fwd",
    )(u16, u, d, mats["b_re"], mats["b_im"], mats["c_re"], mats["c_im_neg"], mats["a_re"], mats["a_im"])


def _ssm_bwd(dy16, x_re, x_im, u16, mats, layer, d):
    def body(dy_ref, u_ref, d_ref, xr_ref, xi_ref, br_ref, bi_ref, cr_ref, ci_ref, ar_ref, ai_ref,
             du_ref, dar_ref, dai_ref, dbr_ref, dbi_ref, dcr_ref, dci_ref, lr_ref, li_ref):
        dy = dy_ref[...]
        lr_ref[...] = jnp.dot(dy, cr_ref[...], preferred_element_type=F32)
        li_ref[...] = jnp.dot(dy, ci_ref[...], preferred_element_type=F32)
        write, start = _scan_in_place(lr_ref, li_ref, ar_ref[...], -ai_ref[...], True)

        def rows(t):
            return pl.ds(pl.multiple_of(t * N_CHUNKS, N_CHUNKS), N_CHUNKS)

        def grad(acc, lam, xpr, xpi):
            return acc[0] + xpr * lam[0] + xpi * lam[1], acc[1] + xpr * lam[1] - xpi * lam[0]

        def down(tau, carry):
            lam = write(tau, carry[0])
            t = CHUNK - 2 - tau
            return lam, grad(carry[1], lam, xr_ref[rows(t), :], xi_ref[rows(t), :])

        zero = jnp.zeros((N_CHUNKS, LANES_G), F32)
        lam, acc = lax.fori_loop(0, CHUNK - 1, down, (start, (zero, zero)), unroll=5)
        lam = write(CHUNK - 1, lam)
        last = rows(CHUNK - 1)
        acc = grad(acc, lam, _shift_down(xr_ref[last, :], 1), _shift_down(xi_ref[last, :], 1))
        dar_ref[...] = jnp.sum(acc[0], axis=0, keepdims=True)
        dai_ref[...] = jnp.sum(acc[1], axis=0, keepdims=True)

        l_re, l_im = lr_ref[...].astype(BF16), li_ref[...].astype(BF16)
        du = lax.dot_general(l_re, br_ref[...], NT, preferred_element_type=F32)
        du += lax.dot_general(l_im, bi_ref[...], NT, preferred_element_type=F32)
        du_ref[...] = (du + dy.astype(F32) * d_ref[...]).astype(BF16)
        uv = u_ref[...]
        dbr_ref[...] = lax.dot_general(uv, l_re, TN, preferred_element_type=F32)
        dbi_ref[...] = lax.dot_general(uv, l_im, TN, preferred_element_type=F32)
        dcr_ref[...] = lax.dot_general(dy, xr_ref[...].astype(BF16), TN, preferred_element_type=F32)
        dci_ref[...] = lax.dot_general(dy, xi_ref[...].astype(BF16), TN, preferred_element_type=F32)

    col, diag, vec = _ssm_specs(layer)
    out_vec = pl.BlockSpec((1, LANES_G), lambda g: (0, g))
    out_blk = pl.BlockSpec((None, 128, LANES_G), lambda g: (g, 0, 0))
    sds = jax.ShapeDtypeStruct
    return pl.pallas_call(
        body, grid=(N_LANE_GROUPS,),
        in_specs=[col(128), col(128), pl.BlockSpec((None, 1, 128), lambda g: (layer, 0, g)),
                  col(LANES_G), col(LANES_G), diag, diag, diag, diag, vec, vec],
        out_specs=[col(128), out_vec, out_vec, out_blk, out_blk, out_blk, out_blk],
        out_shape=[sds((SEQ, WIDTH), BF16)] + [sds((1, SSM_GROUPS * SSM_STATE), F32)] * 2
        + [sds((N_LANE_GROUPS, 128, LANES_G), F32)] * 4,
        scratch_shapes=[pltpu.VMEM((SEQ, LANES_G), F32)] * 2, compiler_params=_cp(), name="ssm_bwd",
    )(dy16, u16, d, x_re, x_im, mats["b_re"], mats["b_im"], mats["c_re"], mats["c_im_neg"],
      mats["a_re"], mats["a_im"])


_GELU_C = math.sqrt(2.0 / math.pi)


def _gelu(y):
    return 0.5 * y * (1.0 + jnp.tanh(_GELU_C * (y + 0.044715 * (y * y * y))))


def _glu_fwd(y, wglu):
    tt = 512

    def body(y_ref, w_ref, z_ref):
        ys = _gelu(y_ref[...])
        a = jnp.dot(ys.astype(BF16), w_ref[...], preferred_element_type=F32)
        z_ref[...] = (ys * jax.nn.sigmoid(a)).astype(BF16)

    blk = pl.BlockSpec((tt, WIDTH), lambda i: (i, 0))
    return pl.pallas_call(body, grid=(SEQ // tt,), in_specs=[blk, _full((WIDTH, WIDTH))], out_specs=blk,
                          out_shape=jax.ShapeDtypeStruct((SEQ, WIDTH), BF16), compiler_params=_cp(),
                          name="glu_fwd")(y, wglu)


def _glu_bwd(y, wglu, dz, u):
    tt = 512

    def body(y_ref, w_ref, dz_ref, u_ref, dy_ref, ys_ref, da_ref, dd_ref):
        @pl.when(pl.program_id(0) == 0)
        def _():
            dd_ref[...] = jnp.zeros_like(dd_ref)

        yv = y_ref[...]
        t = jnp.tanh(_GELU_C * (yv + 0.044715 * (yv * yv * yv)))
        ys = 0.5 * yv * (1.0 + t)
        ysb = ys.astype(BF16)
        sg = jax.nn.sigmoid(jnp.dot(ysb, w_ref[...], preferred_element_type=F32))
        dz = dz_ref[...].astype(F32)
        da = (dz * ys * sg * (1.0 - sg)).astype(BF16)
        dys = dz * sg + lax.dot_general(da, w_ref[...], NT, preferred_element_type=F32)
        dy = dys * (0.5 * (1.0 + t) + 0.5 * yv * (1.0 - t * t) * _GELU_C * (1.0 + 3 * 0.044715 * (yv * yv)))
        dy_ref[...] = dy.astype(BF16)
        ys_ref[...] = ysb
        da_ref[...] = da
        dd_ref[...] += jnp.sum(dy * u_ref[...], axis=0, keepdims=True)

    blk = pl.BlockSpec((tt, WIDTH), lambda i: (i, 0))
    return pl.pallas_call(
        body, grid=(SEQ // tt,), in_specs=[blk, _full((WIDTH, WIDTH)), blk, blk],
        out_specs=[blk, blk, blk, _full((1, WIDTH))],
        out_shape=[jax.ShapeDtypeStruct((SEQ, WIDTH), BF16)] * 3 + [jax.ShapeDtypeStruct((1, WIDTH), F32)],
        compiler_params=_cp(), name="glu_bwd")(y, wglu, dz, u)


def _mix_specs(tt, layer):
    row = lambda w: pl.BlockSpec((tt, w), lambda i: (i, 0))
    gate = lambda j: pl.BlockSpec((tt, D_MODEL), lambda i: (i, j))
    wo = lambda j: pl.BlockSpec((D_MODEL, WIDTH), lambda i: (0, j))
    return [row(D_MODEL), row(WIDTH), row(WIDTH), row(WIDTH), gate(0), gate(1), gate(2),
            pl.BlockSpec((None, 1, GATE_W), lambda i: (layer, 0, 0)), wo(0), wo(1), wo(2),
            _full((D_MODEL, D_MODEL))]


def _mix_branches(o_ref, c_ref, z_ref, g_refs, b_ref, wa_ref, wc_ref, ws_ref):
    ys = [lax.dot_general(r[...], w[...], NT, preferred_element_type=F32)
          for r, w in ((o_ref, wa_ref), (c_ref, wc_ref), (z_ref, ws_ref))]
    gates = [jax.nn.sigmoid(g_refs[j][...] + b_ref[:, D_MODEL * j:D_MODEL * (j + 1)]) for j in range(3)]
    return ys, gates


def _mix_fwd(x, o, cv, z, glog, b_gate, layer, wbt, wmix, tie):
    tt = 256

    def body(x_ref, o_ref, c_ref, z_ref, g0, g1, g2, b_ref, wa_ref, wc_ref, ws_ref, wm_ref, tie_ref, x1_ref):
        ys, gates = _mix_branches(o_ref, c_ref, z_ref, (g0, g1, g2), b_ref, wa_ref, wc_ref, ws_ref)
        merged = gates[0] * ys[0] + gates[1] * ys[1] + gates[2] * ys[2]
        x1_ref[...] = x_ref[...] + jnp.dot(merged.astype(BF16), wm_ref[...], preferred_element_type=F32)

    return pl.pallas_call(
        body, grid=(SEQ // tt,), in_specs=_mix_specs(tt, layer) + [ANY],
        out_specs=pl.BlockSpec((tt, D_MODEL), lambda i: (i, 0)),
        out_shape=jax.ShapeDtypeStruct((SEQ, D_MODEL), F32), compiler_params=_cp(), name="mix_fwd",
    )(x, o, cv, z, glog, glog, glog, b_gate, wbt, wbt, wbt, wmix, tie)


def _mix_bwd(dx1, o, cv, z, glog, b_gate, layer, wbt, wmix, tie):
    tt = 256

    def body(dx_ref, o_ref, c_ref, z_ref, g0, g1, g2, b_ref, wa_ref, wc_ref, ws_ref, wm_ref, tie_ref,
             mg_ref, dya_ref, dyc_ref, dys_ref, do_ref, dc_ref, dz_ref, dgl_ref, db_ref):
        @pl.when(pl.program_id(0) == 0)
        def _():
            db_ref[...] = jnp.zeros_like(db_ref)

        ys, gates = _mix_branches(o_ref, c_ref, z_ref, (g0, g1, g2), b_ref, wa_ref, wc_ref, ws_ref)
        mg_ref[...] = (gates[0] * ys[0] + gates[1] * ys[1] + gates[2] * ys[2]).astype(BF16)
        dm = lax.dot_general(dx_ref[...].astype(BF16), wm_ref[...], NT, preferred_element_type=F32)
        for j, (dy_ref, w_ref, d_ref) in enumerate(((dya_ref, wa_ref, do_ref), (dyc_ref, wc_ref, dc_ref),
                                                    (dys_ref, ws_ref, dz_ref))):
            dy = (dm * gates[j]).astype(BF16)
            dy_ref[...] = dy
            d_ref[...] = jnp.dot(dy, w_ref[...], preferred_element_type=F32)
            dgl = dm * ys[j] * gates[j] * (1.0 - gates[j])
            dgl_ref[:, D_MODEL * j:D_MODEL * (j + 1)] = dgl.astype(BF16)
            db_ref[:, D_MODEL * j:D_MODEL * (j + 1)] += jnp.sum(dgl, axis=0, keepdims=True)

    row = lambda w: pl.BlockSpec((tt, w), lambda i: (i, 0))
    sds = jax.ShapeDtypeStruct
    return pl.pallas_call(
        body, grid=(SEQ // tt,), in_specs=_mix_specs(tt, layer) + [ANY],
        out_specs=[row(D_MODEL)] * 4 + [row(WIDTH)] * 3 + [row(GATE_W), _full((1, GATE_W))],
        out_shape=[sds((SEQ, D_MODEL), BF16)] * 4 + [sds((SEQ, WIDTH), F32)] * 3
        + [sds((SEQ, GATE_W), BF16), sds((1, GATE_W), F32)],
        compiler_params=_cp(), name="mix_bwd",
    )(dx1, o, cv, z, glog, glog, glog, b_gate, wbt, wbt, wbt, wmix, tie)


def _ffn_out_fwd(x1, gu, wout, tie):
    tt = 256

    def body(x_ref, gt_ref, up_ref, w_ref, tie_ref, o_ref):
        gt = gt_ref[...]
        act = (gt * jax.nn.sigmoid(gt) * up_ref[...]).astype(BF16)
        o_ref[...] = x_ref[...] + jnp.dot(act, w_ref[...], preferred_element_type=F32)

    return pl.pallas_call(
        body, grid=(SEQ // tt,),
        in_specs=[pl.BlockSpec((tt, D_MODEL), lambda i: (i, 0)), pl.BlockSpec((tt, FFN_H), lambda i: (i, 0)),
                  pl.BlockSpec((tt, FFN_H), lambda i: (i, 1)), _full((FFN_H, D_MODEL)), ANY],
        out_specs=pl.BlockSpec((tt, D_MODEL), lambda i: (i, 0)),
        out_shape=jax.ShapeDtypeStruct((SEQ, D_MODEL), F32), compiler_params=_cp(), name="ffn_out_fwd",
    )(x1, gu, gu, wout, tie)


def _ffn_out_bwd(dx2, gu, wout, tie):
    tt = 256

    def body(dx_ref, gt_ref, up_ref, w_ref, tie_ref, dgu_ref, act_ref):
        gt, up = gt_ref[...], up_ref[...]
        sg = jax.nn.sigmoid(gt)
        silu = gt * sg
        act_ref[...] = (silu * up).astype(BF16)
        dact = lax.dot_general(dx_ref[...].astype(BF16), w_ref[...], NT, preferred_element_type=F32)
        dgu_ref[:, :FFN_H] = (dact * up * (sg * (1.0 + gt * (1.0 - sg)))).astype(BF16)
        dgu_ref[:, FFN_H:] = (dact * silu).astype(BF16)

    return pl.pallas_call(
        body, grid=(SEQ // tt,),
        in_specs=[pl.BlockSpec((tt, D_MODEL), lambda i: (i, 0)), pl.BlockSpec((tt, FFN_H), lambda i: (i, 0)),
                  pl.BlockSpec((tt, FFN_H), lambda i: (i, 1)), _full((FFN_H, D_MODEL)), ANY],
        out_specs=[pl.BlockSpec((tt, 2 * FFN_H), lambda i: (i, 0)), pl.BlockSpec((tt, FFN_H), lambda i: (i, 0))],
        out_shape=[jax.ShapeDtypeStruct((SEQ, 2 * FFN_H), BF16), jax.ShapeDtypeStruct((SEQ, FFN_H), BF16)],
        compiler_params=_cp(), name="ffn_out_bwd",
    )(dx2, gu, gu, wout, tie)


def _loss_head(x, g, target):
    tt = 256

    def body(x_ref, g_ref, t_ref, loss_ref, dx_ref, dg_ref):
        @pl.when(pl.program_id(0) == 0)
        def _():
            loss_ref[...] = jnp.zeros_like(loss_ref)
            dg_ref[...] = jnp.zeros_like(dg_ref)

        xv = x_ref[...]
        r = lax.rsqrt(jnp.mean(xv * xv, axis=-1, keepdims=True) + NORM_EPS)
        xh = xv * r
        err = xh * g_ref[...] - t_ref[...]
        loss_ref[...] += 0.5 * jnp.sum(jnp.mean(err * err, axis=-1, keepdims=True))
        dy = err * (1.0 / D_MODEL)
        gy = dy * g_ref[...]
        dx_ref[...] = r * (gy - xh * jnp.mean(gy * xh, axis=-1, keepdims=True))
        dg_ref[...] += jnp.sum(dy * xh, axis=0, keepdims=True)

    row = pl.BlockSpec((tt, D_MODEL), lambda i: (i, 0))
    return pl.pallas_call(
        body, grid=(SEQ // tt,), in_specs=[row, _full((1, D_MODEL)), row],
        out_specs=[_full((1, 128)), row, _full((1, D_MODEL))],
        out_shape=[jax.ShapeDtypeStruct((1, 128), F32), jax.ShapeDtypeStruct((SEQ, D_MODEL), F32),
                   jax.ShapeDtypeStruct((1, D_MODEL), F32)],
        compiler_params=_cp(), name="loss_head")(x, g, target)


def _adamw(parts, w, m, v, tr, name, groups=None, fill=None, tie=None):
    n_groups, rows, cols = w.shape
    n_parts = parts.shape[1]
    lo, hi = groups if groups is not None else (0, n_groups)

    def body(p_ref, w_ref, m_ref, v_ref, *rest):
        g_ref, d_ref, nm_ref, nv_ref = rest[-4:]
        g = p_ref[0].astype(F32)
        for k in range(1, n_parts):
            g = g + p_ref[k].astype(F32)
        nm = B1 * m_ref[...] + (1.0 - B1) * g
        nv = B2 * v_ref[...] + (1.0 - B2) * (g * g)
        m_hat = nm / (1.0 - B1 ** STEP)
        v_hat = nv / (1.0 - B2 ** STEP)
        g_ref[...] = g
        d_ref[...] = -LR * (m_hat / (jnp.sqrt(v_hat) + ADAM_EPS) + WD * w_ref[...])
        nm_ref[...] = nm
        nv_ref[...] = nv

    blk = pl.BlockSpec((None, tr, cols), lambda l, i: (l + lo, i, 0))
    p_lo = lo if parts.shape[0] == n_groups else 0
    extra = ([] if fill is None else list(fill)) + ([] if tie is None else [tie])
    return pl.pallas_call(
        body, grid=(hi - lo, rows // tr),
        in_specs=[pl.BlockSpec((None, n_parts, tr, cols), lambda l, i: (l + p_lo, 0, i, 0)), blk, blk, blk]
        + [ANY] * len(extra),
        out_specs=[blk] * 4, out_shape=[jax.ShapeDtypeStruct((n_groups, rows, cols), F32)] * 4,
        input_output_aliases={} if fill is None else {4 + j: j for j in range(4)},
        compiler_params=_cp(), name=name)(parts, w, m, v, *extra)


def _split_start(name, arrays, n_sems, plan, after=None):
    n = len(arrays)
    order = [] if after is None else [after]
    n_in = n + len(order)

    def body(*refs):
        ins, send_sems, recv_sems, token = refs[:n], refs[n_in], refs[n_in + 1], refs[-1]
        for src, dst, k, to in plan(ins)[0]:
            pltpu.make_async_remote_copy(src_ref=src, dst_ref=dst, send_sem=send_sems.at[k], recv_sem=recv_sems.at[k],
                                         device_id=to, device_id_type=MESH_ID).start()
        token[...] = jnp.zeros_like(token)

    outs = pl.pallas_call(
        body, name=name,
        out_shape=(pltpu.SemaphoreType.DMA((n_sems,)), pltpu.SemaphoreType.DMA((n_sems,)),
                   *[pltpu.HBM(a.shape, a.dtype) for a in arrays], jax.ShapeDtypeStruct((8, 128), F32)),
        in_specs=[HBM] * n + [ANY] * len(order),
        out_specs=(SEM, SEM, *[HBM] * n, pl.BlockSpec(memory_space=pltpu.VMEM)),
        input_output_aliases={i: 2 + i for i in range(n)},
        compiler_params=pltpu.CompilerParams(has_side_effects=EFFECT),
    )(*[pltpu.with_memory_space_constraint(a, pltpu.HBM) for a in arrays], *order)
    return outs[0], outs[1], list(outs[2:2 + n]), outs[-1]


def _split_wait(name, arrays, send_sems, recv_sems, after, plan):
    n = len(arrays)

    def body(*refs):
        ins, s_sems, r_sems = refs[:n], refs[n], refs[n + 1]
        sends, arrivals = plan(ins)
        x, y, c = lax.axis_index("x"), lax.axis_index("y"), lax.axis_index("c")
        for src, dst, k, to in sends:
            pltpu.make_async_remote_copy(src_ref=src, dst_ref=dst, send_sem=s_sems.at[k], recv_sem=r_sems.at[k],
                                         device_id=to, device_id_type=MESH_ID).wait_send()
        for dst, k in arrivals:
            pltpu.make_async_remote_copy(src_ref=dst, dst_ref=dst, send_sem=s_sems.at[k], recv_sem=r_sems.at[k],
                                         device_id=(x, y, c), device_id_type=MESH_ID).wait_recv()

    return pl.pallas_call(
        body, name=name, out_shape=[pltpu.HBM(a.shape, a.dtype) for a in arrays],
        in_specs=[HBM] * n + [SEM, SEM, ANY], out_specs=[HBM] * n,
        input_output_aliases={i: i for i in range(n)},
        compiler_params=pltpu.CompilerParams(has_side_effects=EFFECT),
    )(*arrays, send_sems, recv_sems, after)


def _chips():
    x, y, c = lax.axis_index("x"), lax.axis_index("y"), lax.axis_index("c")
    return x, y, c, [(1 - x, y), (x, 1 - y), (1 - x, 1 - y)]


def _plan_gather_chips(refs):
    x, y, c, chips = _chips()
    me = 4 * x + 2 * y + c
    n = len(refs) // 2
    sends, arrivals = [], []
    for i in range(n):
        src, land = refs[i], refs[n + i]
        sends.append((src, land.at[me], 4 * i, (x, y, 1 - c)))
        arrivals.append((land.at[4 * x + 2 * y + 1 - c], 4 * i))
        for j, (px, py) in enumerate(chips):
            sends.append((src, land.at[me], 4 * i + 1 + j, (px, py, c)))
            arrivals.append((land.at[4 * px + 2 * py + c], 4 * i + 1 + j))
    return sends, arrivals


def _plan_gather_pass(refs):
    x, y, c, chips = _chips()
    sends, arrivals = [], []
    for i in range(len(refs)):
        for j, (px, py) in enumerate(chips):
            slot = refs[i].at[4 * px + 2 * py + c]
            sends.append((slot, slot, 3 * i + j, (x, y, 1 - c)))
            arrivals.append((refs[i].at[4 * px + 2 * py + 1 - c], 3 * i + j))
    return sends, arrivals


def _plan_scatter_pair(refs):
    x, y, c = lax.axis_index("x"), lax.axis_index("y"), lax.axis_index("c")
    n = len(refs) // 2
    sends, arrivals = [], []
    for i in range(n):
        for q in range(4):
            sends.append((refs[i].at[q, 1 - c], refs[n + i].at[q], 4 * i + q, (x, y, 1 - c)))
            arrivals.append((refs[n + i].at[q], 4 * i + q))
    return sends, arrivals


def _plan_scatter_chips(layer):
    def plan(refs):
        x, y, c, chips = _chips()
        n = len(refs) // 2
        sends, arrivals = [], []
        for i in range(n):
            for j, (px, py) in enumerate(chips):
                sends.append((refs[i].at[2 * px + py], refs[n + i].at[layer, 2 * x + y], 3 * i + j, (px, py, c)))
                arrivals.append((refs[n + i].at[layer, 2 * px + py], 3 * i + j))
        return sends, arrivals

    return plan


def _place_own(srcs):
    n = len(srcs)

    def body(*refs):
        ins, outs, sems = refs[:n], refs[n:2 * n], refs[2 * n]
        me = 4 * lax.axis_index("x") + 2 * lax.axis_index("y") + lax.axis_index("c")
        copies = [pltpu.make_async_copy(ins[i], outs[i].at[me], sems.at[i]) for i in range(n)]
        for cp in copies:
            cp.start()
        for cp in copies:
            cp.wait()

    return list(pl.pallas_call(
        body, in_specs=[ANY] * n, out_specs=[ANY] * n,
        out_shape=[jax.ShapeDtypeStruct((N_DEV,) + s.shape, s.dtype) for s in srcs],
        scratch_shapes=[pltpu.SemaphoreType.DMA((n,))], name="place_own")(*srcs))


def _pair_sum(parts4, from_pair, landing, layer, core, tr, name):
    _, _, rows, cols = parts4.shape

    def body(c_ref, p_ref, s_ref, l_ref, sum_ref, land_ref):
        v = (p_ref[...].astype(F32) + s_ref[...].astype(F32)).astype(BF16)
        sum_ref[...] = v
        land_ref[...] = v

    blk = pl.BlockSpec((None, tr, cols), lambda q, i, c_ref: (q, i, 0))
    return pl.pallas_call(
        body,
        grid_spec=pltpu.PrefetchScalarGridSpec(
            num_scalar_prefetch=1, grid=(4, rows // tr),
            in_specs=[pl.BlockSpec((None, None, tr, cols), lambda q, i, c_ref: (q, c_ref[0], i, 0)), blk, ANY],
            out_specs=[blk, pl.BlockSpec((None, None, tr, cols), lambda q, i, c_ref: (layer, q, i, 0))]),
        out_shape=[jax.ShapeDtypeStruct((4, rows, cols), BF16), jax.ShapeDtypeStruct(landing.shape, BF16)],
        input_output_aliases={3: 1}, compiler_params=_cp(), name=name,
    )(core, parts4, from_pair, landing)


def _travel_layout(t):
    tr = lambda a: jnp.swapaxes(a, 1, 2)
    branch = jnp.concatenate([tr(t["w_attn_o"]), tr(t["w_conv_o"]), tr(t["w_ssm_o"])], axis=2)
    return [tr(t["w_in"]), tr(t["w_ffn_in"]), t["w_ffn_out"], t["w_mix_o"], branch, t["w_ssm_glu"]]


def _native_layout(a):
    tr = lambda x: jnp.swapaxes(x, 1, 2)
    b = a[4]
    return {"w_in": tr(a[0]), "w_ffn_in": tr(a[1]), "w_ffn_out": a[2], "w_mix_o": a[3],
            "w_attn_o": tr(b[:, :, :WIDTH]), "w_conv_o": tr(b[:, :, WIDTH:2 * WIDTH]),
            "w_ssm_o": tr(b[:, :, 2 * WIDTH:]), "w_ssm_glu": a[5]}


def _embed(t):
    eye = jnp.eye(8, dtype=t.dtype)
    t = t.reshape(DEPTH, N_LANE_GROUPS, 8, SSM_GROUP, SSM_STATE)
    return (t[:, :, :, :, None, :] * eye[None, None, :, None, :, None]).reshape(DEPTH, N_LANE_GROUPS, 128, LANES_G)


def _diag_blocks(t):
    t = t.reshape(DEPTH, N_LANE_GROUPS, 8, SSM_GROUP, 8, SSM_STATE)
    return jnp.einsum("lgahap->lgahp", t).reshape(DEPTH, SSM_GROUPS, SSM_GROUP, SSM_STATE)


def _rope_tabs():
    pos = jnp.arange(SEQ, dtype=F32)
    inv_freq = ROPE_THETA ** (-jnp.arange(0, ROT_DIM, 2, dtype=F32) / ROT_DIM)
    ang = pos[:, None] * inv_freq[None, :]
    cos, sin = jnp.cos(ang), jnp.sin(ang)
    one, zero = jnp.ones((SEQ, HEAD_DIM - ROT_DIM), F32), jnp.zeros((SEQ, HEAD_DIM - ROT_DIM), F32)
    z8 = jnp.zeros((SEQ, 8), F32)
    head = lambda *p: jnp.tile(jnp.concatenate(p, axis=1), (1, 2))
    return head(cos, cos, one), head(-sin, z8, zero), head(z8, sin, zero)


def _ssm_mats(sp):
    lr, li, bbr, bbi = _ssm_prep(sp["a_re"], sp["a_im"], sp["log_dt"], sp["bt_re"], sp["bt_im"])
    lanes = SSM_GROUPS * SSM_STATE
    return {
        "a_re": lr.reshape(DEPTH, 1, lanes), "a_im": li.reshape(DEPTH, 1, lanes),
        "b_re": _embed(bbr).astype(BF16), "b_im": _embed(bbi).astype(BF16),
        "c_re": _embed(sp["c_re"]).astype(BF16), "c_im_neg": _embed(-sp["c_im"]).astype(BF16),
    }


def _layer_fwd(x, i, w, rp, mats, tabs, tie, hooks):
    q, kv, cbx, u, u16, glog, h = _rms_mm_in(x, rp["norm_mix"][i], w["win_t"], tie)
    o = _attn_fwd(q, kv, tabs, rp["attn_sinks"][i])
    cv = _conv_fwd(cbx, rp["conv_w"], i)
    u16, u = _scan_order(u16), _scan_order(u)
    x_re, x_im, y = _ssm_fwd(u16, u, mats, i, rp["ssm_d"])
    z = _time_order(_glu_fwd(y, w["wglu"]))
    x1 = _mix_fwd(x, o, cv, z, glog, rp["b_gate"], i, w["branch_t"], w["wmix"], hooks["early"](z))
    hooks["pre_ffn"](x1)
    gu, h2 = _rms_mm_ffn(x1, rp["norm_ffn"][i], w["wffn_t"])
    x2 = _ffn_out_fwd(x1, gu, w["wout"], hooks["mid"](h2))
    kept = dict(x=x, q=q, kv=kv, cbx=cbx, u=u, u16=u16, glog=glog, h=h, o=o, cv=cv, z=z, y=y,
                x_re=x_re, x_im=x_im, x1=x1, gu=gu, h2=h2)
    return x2, kept


def _layer_bwd(dx2, k, i, w, rp, mats, tabs, tie, hooks):
    tn = dict(ta=True, out_dtype=BF16)
    dgu, act = _ffn_out_bwd(dx2, k["gu"], w["wout"], tie)
    g_wout = _mm(act, dx2, tm=FFN_H // 2, tn=1024, tk=512, name="mm_tn_ffn_out", **tn)
    g_wffn_t = _mm(dgu, k["h2"], tm=FFN_H // 2, tn=1024, tk=512, name="mm_tn_ffn_in", **tn)
    dx1, d_norm_ffn = _mm_rmsbwd([dgu], w["wffn_t"], k["x1"], rp["norm_ffn"][i], dx2, "mm_rmsbwd_ffn")

    mg, dya, dyc, dys, do, dcv, dz, dgl, db_gate = _mix_bwd(
        dx1, k["o"], k["cv"], k["z"], k["glog"], rp["b_gate"], i, w["branch_t"], w["wmix"],
        hooks["mid"]((g_wffn_t, g_wout, d_norm_ffn)))
    g_wmix = _mm(mg, dx1, tm=1024, tn=1024, tk=512, name="mm_tn_mix", **tn)
    g_branch_t = _tn_branches((dya, dyc, dys), (k["o"], k["cv"], k["z"]))

    dy16, ys16, da16, dd = _glu_bwd(k["y"], w["wglu"], _scan_order(dz), k["u"])
    g_wglu = _mm(ys16, da16, tm=512, tn=512, tk=512, name="mm_tn_glu", **tn)
    du, da_re, da_im, db_re, db_im, dc_re, dc_im = _ssm_bwd(dy16, k["x_re"], k["x_im"], k["u16"], mats, i,
                                                             rp["ssm_d"])
    du = _time_order(du)

    dcb, dcc, dcx, d_conv_w = _conv_bwd(k["cbx"], rp["conv_w"], i, dcv, hooks["late"](du))
    dq, dkv, d_sinks = _attn_bwd(k["q"], k["kv"], tabs, rp["attn_sinks"][i], do)

    pieces = [dq, dkv, dcb, dcc, dcx, du, dgl]
    g_win_t = _tn_pieces(pieces, k["h"])
    dx, d_norm_mix = _mm_rmsbwd(pieces, w["win_t"], k["x"], rp["norm_mix"][i], dx1, "mm_rmsbwd_in")

    grads = [g_win_t, g_wffn_t, g_wout, g_wmix, g_branch_t, g_wglu]
    small = dict(norm_mix=d_norm_mix, b_gate=db_gate, attn_sinks=d_sinks, ssm_d=dd, norm_ffn=d_norm_ffn,
                 conv_w=d_conv_w, da_re=da_re, da_im=da_im, db_re=db_re, db_im=db_im, dc_re=dc_re, dc_im=dc_im)
    return dx, grads, small


def _replicated_grads(sg, sp):
    stack = lambda name: jnp.stack([sg[i][name] for i in range(DEPTH)])
    cots = (stack("da_re").reshape(DEPTH, *_GS), stack("da_im").reshape(DEPTH, *_GS),
            _diag_blocks(stack("db_re")), _diag_blocks(stack("db_im")))
    d_a_re, d_a_im, d_log_dt, d_bt_re, d_bt_im = _ssm_prep_bwd(
        sp["a_re"], sp["a_im"], sp["log_dt"], sp["bt_re"], sp["bt_im"], cots)
    sgrads = {"norm_mix": stack("norm_mix"), "b_gate": stack("b_gate"),
              "attn_sinks": stack("attn_sinks")[:, :, :N_Q_HEADS], "ssm_a_re": d_a_re, "ssm_a_im": d_a_im,
              "ssm_b_re": jnp.swapaxes(d_bt_re, 2, 3), "ssm_b_im": jnp.swapaxes(d_bt_im, 2, 3),
              "ssm_c_re": _diag_blocks(stack("dc_re")), "ssm_c_im": -_diag_blocks(stack("dc_im")),
              "ssm_d": stack("ssm_d"), "ssm_log_dt": d_log_dt, "norm_ffn": stack("norm_ffn")}
    return sgrads, stack("conv_w")[:, :3]


def kernel(x, norm_mix, w_in, b_gate, attn_sinks, w_attn_o, conv_w, w_conv_o, ssm_a_re, ssm_a_im, ssm_b_re, ssm_b_im, ssm_c_re, ssm_c_im, ssm_d, ssm_log_dt, w_ssm_glu, w_ssm_o, w_mix_o, norm_ffn, w_ffn_in, w_ffn_out, norm_final, loss_target, m_norm_mix, m_w_in, m_b_gate, m_attn_sinks, m_w_attn_o, m_conv_w, m_w_conv_o, m_ssm_a_re, m_ssm_a_im, m_ssm_b_re, m_ssm_b_im, m_ssm_c_re, m_ssm_c_im, m_ssm_d, m_ssm_log_dt, m_w_ssm_glu, m_w_ssm_o, m_w_mix_o, m_norm_ffn, m_w_ffn_in, m_w_ffn_out, m_norm_final, v_norm_mix, v_w_in, v_b_gate, v_attn_sinks, v_w_attn_o, v_conv_w, v_w_conv_o, v_ssm_a_re, v_ssm_a_im, v_ssm_b_re, v_ssm_b_im, v_ssm_c_re, v_ssm_c_im, v_ssm_d, v_ssm_log_dt, v_w_ssm_glu, v_w_ssm_o, v_w_mix_o, v_norm_ffn, v_w_ffn_in, v_w_ffn_out, v_norm_final):
    big = {"w": dict(w_in=w_in, w_attn_o=w_attn_o, w_conv_o=w_conv_o, w_ssm_glu=w_ssm_glu, w_ssm_o=w_ssm_o,
                     w_mix_o=w_mix_o, w_ffn_in=w_ffn_in, w_ffn_out=w_ffn_out),
           "m": dict(w_in=m_w_in, w_attn_o=m_w_attn_o, w_conv_o=m_w_conv_o, w_ssm_glu=m_w_ssm_glu,
                     w_ssm_o=m_w_ssm_o, w_mix_o=m_w_mix_o, w_ffn_in=m_w_ffn_in, w_ffn_out=m_w_ffn_out),
           "v": dict(w_in=v_w_in, w_attn_o=v_w_attn_o, w_conv_o=v_w_conv_o, w_ssm_glu=v_w_ssm_glu,
                     w_ssm_o=v_w_ssm_o, w_mix_o=v_w_mix_o, w_ffn_in=v_w_ffn_in, w_ffn_out=v_w_ffn_out)}
    small = {"w": dict(norm_mix=norm_mix, b_gate=b_gate, attn_sinks=attn_sinks, ssm_a_re=ssm_a_re,
                       ssm_a_im=ssm_a_im, ssm_b_re=ssm_b_re, ssm_b_im=ssm_b_im, ssm_c_re=ssm_c_re,
                       ssm_c_im=ssm_c_im, ssm_d=ssm_d, ssm_log_dt=ssm_log_dt, norm_ffn=norm_ffn),
             "m": dict(norm_mix=m_norm_mix, b_gate=m_b_gate, attn_sinks=m_attn_sinks, ssm_a_re=m_ssm_a_re,
                       ssm_a_im=m_ssm_a_im, ssm_b_re=m_ssm_b_re, ssm_b_im=m_ssm_b_im, ssm_c_re=m_ssm_c_re,
                       ssm_c_im=m_ssm_c_im, ssm_d=m_ssm_d, ssm_log_dt=m_ssm_log_dt, norm_ffn=m_norm_ffn),
             "v": dict(norm_mix=v_norm_mix, b_gate=v_b_gate, attn_sinks=v_attn_sinks, ssm_a_re=v_ssm_a_re,
                       ssm_a_im=v_ssm_a_im, ssm_b_re=v_ssm_b_re, ssm_b_im=v_ssm_b_im, ssm_c_re=v_ssm_c_re,
                       ssm_c_im=v_ssm_c_im, ssm_d=v_ssm_d, ssm_log_dt=v_ssm_log_dt, norm_ffn=v_norm_ffn)}
    finals = {"w": norm_final, "m": m_norm_final, "v": v_norm_final}
    convs = {"w": conv_w, "m": m_conv_w, "v": v_conv_w}
    mine = 4 * lax.axis_index("x") + 2 * lax.axis_index("y") + lax.axis_index("c")

    travel = {s: _travel_layout(big[s]) for s in "wmv"}
    stacked16 = [a.astype(BF16) for a in travel["w"]]
    rp = {"norm_mix": norm_mix[:, None], "norm_ffn": norm_ffn[:, None], "attn_sinks": attn_sinks[:, None],
          "b_gate": b_gate[:, None], "ssm_d": ssm_d[:, None]}
    sp = {"a_re": ssm_a_re, "a_im": ssm_a_im, "log_dt": ssm_log_dt[:, :, None],
          "bt_re": jnp.swapaxes(ssm_b_re, 2, 3), "bt_im": jnp.swapaxes(ssm_b_im, 2, 3),
          "c_re": ssm_c_re, "c_im": ssm_c_im}
    rows_tile = {"win_t": 368, "wffn_t": 352, "wout": 176, "wmix": 128, "branch_t": 128, "wglu": 64}
    core = lax.axis_index("c").astype(jnp.int32).reshape(1)
    no_tie = jnp.zeros((8, 128), F32)

    def gather_chips(tag, i, kinds, after, extra=()):
        srcs = [stacked16[j][i] for j in kinds] + list(extra)
        s_sems, r_sems, arrays, token = _split_start(
            f"gather_chips_start_{tag}", srcs + _place_own(srcs), 4 * len(srcs), _plan_gather_chips, after)
        return (tag, s_sems, r_sems, arrays), token

    def gather_pass(state, after):
        tag, s_sems, r_sems, arrays = state
        arrays = _split_wait(f"gather_chips_wait_{tag}", arrays, s_sems, r_sems, after, _plan_gather_chips)
        n = len(arrays) // 2
        s_sems, r_sems, lands, token = _split_start(
            f"gather_pass_start_{tag}", list(arrays[n:]), 3 * n, _plan_gather_pass)
        return (tag, s_sems, r_sems, lands), token

    def gather_done(state, after, kinds):
        tag, s_sems, r_sems, lands = state
        lands = _split_wait(f"gather_pass_wait_{tag}", lands, s_sems, r_sems, after, _plan_gather_pass)
        named = {KINDS[j][0]: a.reshape(N_DEV * KINDS[j][1], KINDS[j][2]) for a, j in zip(lands, kinds)}
        return named, list(lands[len(kinds):])

    all_kinds, mixer_kinds, ffn_kinds = tuple(range(len(KINDS))), (0, 3, 4, 5), (1, 2)
    no_hooks = {name: (lambda value: no_tie) for name in ("early", "pre_ffn", "mid", "late")}
    state, _ = gather_chips("0m", 0, mixer_kinds, None, extra=[jnp.pad(conv_w.reshape(6, 128), ((0, 2), (0, 0)))])
    mats = _ssm_mats(sp)
    tabs = _rope_tabs()
    state, _ = gather_pass(state, mats["c_im_neg"])
    ffn_state, tie = gather_chips("0f", 0, ffn_kinds, state[3][0])
    w_next, (conv_all,) = gather_done(state, tabs[2], mixer_kinds)
    conv_full = conv_all[:, :6].reshape(N_DEV, DEPTH, 3, 64).transpose(1, 2, 0, 3).reshape(DEPTH, 3, WIDTH)
    rp["conv_w"] = jnp.pad(conv_full, ((0, 0), (0, 5), (0, 0)))

    act = x[0]
    weights, kept = [], []
    for i in range(DEPTH):
        w_i, hooks, held = w_next, dict(no_hooks), {}
        if i == 0:
            def early(value, held=held):
                held["ffn"], token = gather_pass(ffn_state, value)
                return token

            def pre_ffn(value, w_i=w_i, held=held):
                w_i.update(gather_done(held["ffn"], value, ffn_kinds)[0])

            hooks.update(early=early, pre_ffn=pre_ffn)
        if i + 1 < DEPTH:
            state, tie_next = gather_chips(str(i + 1), i + 1, all_kinds, w_i["win_t"])
            tie = tie_next if i > 0 else tie

            def mid(value, state=state, held=held):
                held["next"], token = gather_pass(state, value)
                return token

            hooks.update(mid=mid)
        elif i > 0:
            tie = no_tie
        act, k = _layer_fwd(act, i, w_i, rp, mats, tabs, tie, hooks)
        if i + 1 < DEPTH:
            w_next, _ = gather_done(held["next"], act, all_kinds)
        weights.append(w_i)
        kept.append(k)
    loss_row, dx, d_norm_final = _loss_head(act, norm_final[None], loss_target[0])
    loss = lax.psum(loss_row[0, 0], ("x", "y", "c"))

    landings = [lax.empty((DEPTH, 4, r, c), BF16) for _, r, c in KINDS]
    landings0 = [lax.empty((1, 4, r, c), BF16) for _, r, c in KINDS]

    def scatter_pair(tag, kinds, grads, after):
        parts4 = [g.reshape(4, 2, KINDS[j][1], KINDS[j][2]) for g, j in zip(grads, kinds)]
        zones = [lax.empty((4, KINDS[j][1], KINDS[j][2]), BF16) for j in kinds]
        s_sems, r_sems, arrays, token = _split_start(
            f"scatter_pair_start_{tag}", parts4 + zones, 4 * len(kinds), _plan_scatter_pair, after)
        return (tag, kinds, s_sems, r_sems, arrays), token

    def scatter_chips(state, lands, slot, after):
        tag, kinds, s_sems, r_sems, arrays = state
        arrays = _split_wait(f"scatter_pair_wait_{tag}", arrays, s_sems, r_sems, after, _plan_scatter_pair)
        n = len(kinds)
        sums, mine_lands = [], []
        for k, j in enumerate(kinds):
            name = KINDS[j][0]
            chip_sum, land = _pair_sum(arrays[k], arrays[n + k], lands[j], slot, core, rows_tile[name],
                                       f"pair_sum_{name}")
            sums.append(chip_sum)
            mine_lands.append(land)
        s_sems, r_sems, arrays, token = _split_start(
            f"scatter_chips_start_{tag}", sums + mine_lands, 3 * n, _plan_scatter_chips(slot))
        return (tag, kinds, slot, s_sems, r_sems, arrays), token

    def scatter_done(state, lands, after):
        tag, kinds, slot, s_sems, r_sems, arrays = state
        arrays = _split_wait(f"scatter_chips_wait_{tag}", arrays, s_sems, r_sems, after, _plan_scatter_chips(slot))
        lands = list(lands)
        for k, j in enumerate(kinds):
            lands[j] = arrays[len(kinds) + k]
        return lands

    sg = [None] * DEPTH
    pending, tie = None, no_tie
    for i in reversed(range(DEPTH)):
        hooks, held = dict(no_hooks), {}
        if pending is not None:
            def mid(value, i=i, pending=pending, held=held):
                held["chips"], token = scatter_chips(pending, landings, i + 1, value[2])
                if i == 0:
                    held["ffn_pair"], _ = scatter_pair("0f", ffn_kinds, value[:2], value[2])
                return token

            hooks.update(mid=mid)
        if i == 0:
            def late(value, held=held):
                held["ffn_chips"], token = scatter_chips(held["ffn_pair"], landings0, 0, value)
                return token

            hooks.update(late=late)
        dx, grads, sg[i] = _layer_bwd(dx, kept[i], i, weights[i], rp, mats, tabs, tie, hooks)
        if pending is not None:
            landings = scatter_done(held["chips"], landings, dx)
        if i > 0:
            pending, tie = scatter_pair(str(i), all_kinds, grads, dx)
        else:
            pending, _ = scatter_pair("0m", mixer_kinds, [grads[j] for j in mixer_kinds], dx)

    sgrads, conv_grad = _replicated_grads(sg, sp)

    def pack_small(t, final, conv):
        flat = [t[name].reshape(DEPTH, n) for name, n in SMALL]
        flat = jnp.concatenate([jnp.concatenate(flat, axis=1).reshape(-1), final.reshape(-1), conv.reshape(-1)])
        return jnp.pad(flat, (0, SMALL_ROWS * 128 - flat.shape[0])).reshape(SMALL_ROWS, 128)

    small_src = [pack_small(sgrads, d_norm_final, conv_grad).astype(BF16)]
    last, tie = scatter_chips(pending, landings0, 0, small_src[0])
    s_sems, r_sems, arrays, _ = _split_start(
        "gather_small_chips_start", small_src + _place_own(small_src), 4, _plan_gather_chips, last[5][0])
    small_state = ("small", s_sems, r_sems, arrays)

    big_out = [_adamw(landings[j], travel["w"][j], travel["m"][j], travel["v"][j], rows_tile[name],
                      "adamw_late_" + name, groups=(1, DEPTH), tie=tie) for j, (name, _, _) in enumerate(KINDS)]
    landings0 = scatter_done(held["ffn_chips"], landings0, big_out[-1][0])
    landings0 = scatter_done(last, landings0, big_out[-1][0])
    small_state, _ = gather_pass(small_state, landings0[0])
    big_out = [_adamw(landings0[j], travel["w"][j], travel["m"][j], travel["v"][j], rows_tile[name],
                      "adamw_first_" + name, groups=(0, 1), fill=big_out[j]) for j, (name, _, _) in enumerate(KINDS)]
    big_res = [_native_layout([big_out[j][kind] for j in range(len(KINDS))]) for kind in range(4)]

    zeros_conv = jnp.zeros((CONV_N,), F32)
    _, (sparts,) = gather_done(small_state, big_out[-1][0], ())
    sw, sm_, sv = (pack_small(small[s], finals[s], zeros_conv) for s in "wmv")
    small_out = _adamw(sparts[None], sw[None], sm_[None], sv[None], SMALL_ROWS // 8, "adamw_replicated")

    def unpack_small(p):
        flat = p.reshape(-1)
        per = flat[:DEPTH * SMALL_PER_LAYER].reshape(DEPTH, SMALL_PER_LAYER)
        out, off = {}, 0
        for name, n in SMALL:
            out[name] = per[:, off:off + n].reshape(small["w"][name].shape)
            off += n
        out["norm_final"] = flat[DEPTH * SMALL_PER_LAYER:DEPTH * SMALL_PER_LAYER + D_MODEL]
        return out

    small_res = [unpack_small(p) for p in small_out]

    conv_off = DEPTH * SMALL_PER_LAYER + D_MODEL
    conv_parts = sparts.reshape(N_DEV, -1)[:, conv_off:conv_off + CONV_N].reshape(N_DEV, DEPTH * 3, WIDTH)
    conv_parts = lax.dynamic_slice_in_dim(conv_parts, mine * 64, 64, axis=2)
    conv_res = _adamw(conv_parts[None], *(convs[s].reshape(1, DEPTH * 3, 64) for s in "wmv"), DEPTH * 3, "adamw_conv_w")

    order = ["norm_mix", "w_in", "b_gate", "attn_sinks", "w_attn_o", "conv_w", "w_conv_o", "ssm_a_re", "ssm_a_im",
             "ssm_b_re", "ssm_b_im", "ssm_c_re", "ssm_c_im", "ssm_d", "ssm_log_dt", "w_ssm_glu", "w_ssm_o",
             "w_mix_o", "norm_ffn", "w_ffn_in", "w_ffn_out", "norm_final"]
    outs = [loss, dx[None]]
    for kind in range(4):
        for name in order:
            if name == "conv_w":
                outs.append(conv_res[kind].reshape(DEPTH, 3, 64))
            elif name in big_res[kind]:
                outs.append(big_res[kind][name])
            else:
                outs.append(small_res[kind][name])
    return tuple(outs)
```

```python
import functools
import math

import jax
import jax.numpy as jnp
from jax import lax
from jax.experimental import pallas as pl
from jax.experimental.pallas import tpu as pltpu

F32 = jnp.float32
BF16 = jnp.bfloat16

N_DEV = 8
DEPTH = 4
SEQ = 2048
D_MODEL = 1024
N_Q_HEADS = 8
HEAD_DIM = 64
ATTN_W = 512
KV_W = 128
BLOCK = 128
N_BLOCKS = SEQ // BLOCK
ROPE_THETA = 500000.0
ROT_DIM = 16
NEG_INF = -1e30
WIDTH = 512
SSM_GROUPS = 32
SSM_GROUP = 16
SSM_STATE = 64
SLABS = 16
CHUNK = 256
N_CHUNKS = SEQ // CHUNK
GATE_W = 3 * D_MODEL
IN_COLS = 5888
FFN_H = 2816
NORM_EPS = 1e-6
LR, B1, B2, ADAM_EPS, WD, STEP = 0.001, 0.9, 0.999, 1e-08, 0.01, 10

COL_Q, COL_KV, COL_CBX, COL_U, COL_G = 0, 512, 768, 2304, 2816
PIECE_W = (512, 256, 512, 512, 512, 512, 3072)
PIECE_OFF = tuple(sum(PIECE_W[:i]) for i in range(len(PIECE_W)))

KINDS = (("win_t", 736, 1024), ("wffn_t", 704, 1024), ("wout", 352, 1024), ("wmix", 128, 1024),
         ("branch_t", 128, 1536), ("wglu", 64, 512))

SMALL = (("norm_mix", 1024), ("b_gate", 3072), ("attn_sinks", 8), ("ssm_a_re", 2048), ("ssm_a_im", 2048),
         ("ssm_b_re", 32768), ("ssm_b_im", 32768), ("ssm_c_re", 32768), ("ssm_c_im", 32768),
         ("ssm_d", 512), ("ssm_log_dt", 32), ("norm_ffn", 1024))
SMALL_PER_LAYER = sum(n for _, n in SMALL)
CONV_N = DEPTH * 3 * WIDTH
SMALL_ROWS = 4480

VMEM_LIMIT = 56 * 1024 * 1024
NT = (((1,), (1,)), ((), ()))
TN = (((0,), (0,)), ((), ()))
MESH_ID = pl.DeviceIdType.MESH
ANY = pl.BlockSpec(memory_space=pl.ANY)
HBM = pl.BlockSpec(memory_space=pltpu.HBM)
SEM = pl.BlockSpec(memory_space=pltpu.SEMAPHORE)
EFFECT = pltpu.SideEffectType.DATAFLOW_SIDE_EFFECTING


def _cp(**kw):
    return pltpu.CompilerParams(vmem_limit_bytes=VMEM_LIMIT, **kw)


def _full(shape):
    return pl.BlockSpec(shape, lambda *_: (0,) * len(shape))


def _mm(a, b, *, ta=False, tb=False, tm, tn, tk, out_dtype=F32, name):
    m = a.shape[1] if ta else a.shape[0]
    k = a.shape[0] if ta else a.shape[1]
    n = b.shape[0] if tb else b.shape[1]
    nk = k // tk
    dims = (((0 if ta else 1,), (1 if tb else 0,)), ((), ()))

    def body(a_ref, b_ref, o_ref, acc_ref):
        kk = pl.program_id(2)

        @pl.when(kk == 0)
        def _():
            acc_ref[...] = jnp.zeros_like(acc_ref)

        acc_ref[...] += lax.dot_general(a_ref[...].astype(BF16), b_ref[...].astype(BF16), dims,
                                        preferred_element_type=F32)

        @pl.when(kk == nk - 1)
        def _():
            o_ref[...] = acc_ref[...].astype(out_dtype)

    a_spec = pl.BlockSpec((tk, tm), lambda i, j, kk: (kk, i)) if ta else pl.BlockSpec((tm, tk), lambda i, j, kk: (i, kk))
    b_spec = pl.BlockSpec((tn, tk), lambda i, j, kk: (j, kk)) if tb else pl.BlockSpec((tk, tn), lambda i, j, kk: (kk, j))
    return pl.pallas_call(
        body, grid=(m // tm, n // tn, nk), in_specs=[a_spec, b_spec],
        out_specs=pl.BlockSpec((tm, tn), lambda i, j, kk: (i, j)),
        out_shape=jax.ShapeDtypeStruct((m, n), out_dtype),
        scratch_shapes=[pltpu.VMEM((tm, tn), F32)], compiler_params=_cp(), name=name)(a, b)


def _rms_rows(xv, g):
    r = lax.rsqrt(jnp.mean(xv * xv, axis=-1, keepdims=True) + NORM_EPS)
    return ((xv * r) * g).astype(BF16)


def _rms_mm_in(x, g, wt, tie):
    tt = 256
    widths = (ATTN_W, 2 * KV_W, 3 * WIDTH, WIDTH, GATE_W)
    offs = (COL_Q, COL_KV, COL_CBX, COL_U, COL_G)

    def body(x_ref, g_ref, w_ref, tie_ref, q_ref, kv_ref, cbx_ref, u_ref, u16_ref, gl_ref, h_ref):
        h = _rms_rows(x_ref[...], g_ref[...])
        h_ref[...] = h
        prod = lax.dot_general(h, w_ref[...], NT, preferred_element_type=F32)
        for ref, o, w in zip((q_ref, kv_ref, cbx_ref, u_ref, gl_ref), offs, widths):
            ref[...] = prod[:, o:o + w]
        u16_ref[...] = prod[:, COL_U:COL_U + WIDTH].astype(BF16)

    row = lambda w: pl.BlockSpec((tt, w), lambda i: (i, 0))
    sds = jax.ShapeDtypeStruct
    return pl.pallas_call(
        body, grid=(SEQ // tt,), in_specs=[row(D_MODEL), _full((1, D_MODEL)), _full((IN_COLS, D_MODEL)), ANY],
        out_specs=[row(ATTN_W), row(2 * KV_W), row(3 * WIDTH), row(WIDTH), row(WIDTH), row(GATE_W), row(D_MODEL)],
        out_shape=[sds((SEQ, ATTN_W), F32), sds((SEQ, 2 * KV_W), F32), sds((SEQ, 3 * WIDTH), F32),
                   sds((SEQ, WIDTH), F32), sds((SEQ, WIDTH), BF16), sds((SEQ, GATE_W), F32),
                   sds((SEQ, D_MODEL), BF16)],
        compiler_params=_cp(), name="rms_mm_in")(x, g, wt, tie)


def _rms_mm_ffn(x, g, wt):
    tt = 256

    def body(x_ref, g_ref, w_ref, o_ref, h_ref):
        h = _rms_rows(x_ref[...], g_ref[...])
        h_ref[...] = h
        o_ref[...] = lax.dot_general(h, w_ref[...], NT, preferred_element_type=F32)

    row = lambda w: pl.BlockSpec((tt, w), lambda i: (i, 0))
    return pl.pallas_call(
        body, grid=(SEQ // tt,), in_specs=[row(D_MODEL), _full((1, D_MODEL)), _full((2 * FFN_H, D_MODEL))],
        out_specs=[row(2 * FFN_H), row(D_MODEL)],
        out_shape=[jax.ShapeDtypeStruct((SEQ, 2 * FFN_H), F32), jax.ShapeDtypeStruct((SEQ, D_MODEL), BF16)],
        compiler_params=_cp(), name="rms_mm_ffn")(x, g, wt)


def _mm_rmsbwd(pieces, wt, x, g, dres, name):
    tt = 256
    widths = [p.shape[1] for p in pieces]
    offs = [sum(widths[:i]) for i in range(len(widths))]
    n = len(pieces)

    def body(*refs):
        p_refs, (w_ref, x_ref, g_ref, r_ref, dx_ref, dg_ref) = refs[:n], refs[n:]

        @pl.when(pl.program_id(0) == 0)
        def _():
            dg_ref[...] = jnp.zeros_like(dg_ref)

        dh = jnp.zeros((tt, D_MODEL), F32)
        for p_ref, o, w in zip(p_refs, offs, widths):
            dh += jnp.dot(p_ref[...], w_ref[o:o + w, :], preferred_element_type=F32)
        xv = x_ref[...]
        r = lax.rsqrt(jnp.mean(xv * xv, axis=-1, keepdims=True) + NORM_EPS)
        xh = xv * r
        gy = dh * g_ref[...]
        dx_ref[...] = r_ref[...] + r * (gy - xh * jnp.mean(gy * xh, axis=-1, keepdims=True))
        dg_ref[...] += jnp.sum(dh * xh, axis=0, keepdims=True)

    row = lambda w: pl.BlockSpec((tt, w), lambda i: (i, 0))
    return pl.pallas_call(
        body, grid=(SEQ // tt,),
        in_specs=[row(w) for w in widths] + [_full(wt.shape), row(D_MODEL), _full((1, D_MODEL)), row(D_MODEL)],
        out_specs=[row(D_MODEL), _full((1, D_MODEL))],
        out_shape=[jax.ShapeDtypeStruct((SEQ, D_MODEL), F32), jax.ShapeDtypeStruct((1, D_MODEL), F32)],
        compiler_params=_cp(), name=name)(*pieces, wt, x, g, dres)


def _tn_pieces(pieces, h):
    tk, tn = 512, 512
    nk = SEQ // tk
    n = len(pieces)

    def body(*refs):
        p_refs, (h_ref, o_ref, acc_ref) = refs[:n], refs[n:]
        kk = pl.program_id(1)

        @pl.when(kk == 0)
        def _():
            acc_ref[...] = jnp.zeros_like(acc_ref)

        hv = h_ref[...]
        for p_ref, o, w in zip(p_refs, PIECE_OFF, PIECE_W):
            acc_ref[o:o + w, :] += lax.dot_general(p_ref[...], hv, TN, preferred_element_type=F32)

        @pl.when(kk == nk - 1)
        def _():
            o_ref[...] = acc_ref[...].astype(BF16)

    return pl.pallas_call(
        body, grid=(D_MODEL // tn, nk),
        in_specs=[pl.BlockSpec((tk, w), lambda j, kk: (kk, 0)) for w in PIECE_W]
        + [pl.BlockSpec((tk, tn), lambda j, kk: (kk, j))],
        out_specs=pl.BlockSpec((IN_COLS, tn), lambda j, kk: (0, j)),
        out_shape=jax.ShapeDtypeStruct((IN_COLS, D_MODEL), BF16),
        scratch_shapes=[pltpu.VMEM((IN_COLS, tn), F32)], compiler_params=_cp(), name="tn_pieces")(*pieces, h)


def _tn_branches(dys, acts):
    tk = 512
    nk = SEQ // tk

    def body(d0, d1, d2, a0, a1, a2, o_ref, acc_ref):
        kk = pl.program_id(0)

        @pl.when(kk == 0)
        def _():
            acc_ref[...] = jnp.zeros_like(acc_ref)

        for j, (d, a) in enumerate(((d0, a0), (d1, a1), (d2, a2))):
            acc_ref[:, WIDTH * j:WIDTH * (j + 1)] += lax.dot_general(d[...], a[...], TN, preferred_element_type=F32)

        @pl.when(kk == nk - 1)
        def _():
            o_ref[...] = acc_ref[...].astype(BF16)

    row = lambda w: pl.BlockSpec((tk, w), lambda kk: (kk, 0))
    return pl.pallas_call(
        body, grid=(nk,), in_specs=[row(D_MODEL)] * 3 + [row(WIDTH)] * 3,
        out_specs=_full((D_MODEL, 3 * WIDTH)), out_shape=jax.ShapeDtypeStruct((D_MODEL, 3 * WIDTH), BF16),
        scratch_shapes=[pltpu.VMEM((D_MODEL, 3 * WIDTH), F32)], compiler_params=_cp(), name="tn_branches",
    )(*dys, *acts)


def _rope(t, c, a, b):
    return t * c + pltpu.roll(t, 120, axis=1) * a + pltpu.roll(t, 8, axis=1) * b


def _rope_t(d, c, a, b):
    return d * c + pltpu.roll(d * a, 8, axis=1) + pltpu.roll(d * b, 120, axis=1)


def _band_sides(band):
    left = lax.broadcasted_iota(jnp.int32, band.shape, 1) < HEAD_DIM
    h0 = jnp.where(left, band, 0.0)
    h1 = jnp.where(left, 0.0, band)
    r0 = pltpu.roll(h0, HEAD_DIM, axis=1)
    r1 = pltpu.roll(h1, HEAD_DIM, axis=1)
    return ((h0.astype(BF16), r0.astype(BF16)), (r1.astype(BF16), h1.astype(BF16)))


def _attn_mask(i):
    qi = lax.broadcasted_iota(jnp.int32, (BLOCK, 2 * BLOCK), 0)
    kj = lax.broadcasted_iota(jnp.int32, (BLOCK, 2 * BLOCK), 1)
    delta = qi + BLOCK - kj
    return (delta >= 0) & (delta < BLOCK) & ((kj >= BLOCK) | (i > 0))


def _attn_probs(qc, kside, ok, sink):
    s = lax.dot_general(qc, kside, NT, preferred_element_type=F32) * (HEAD_DIM ** -0.5)
    s = jnp.where(ok, s, NEG_INF)
    m = jnp.maximum(jnp.max(s, axis=-1, keepdims=True), sink)
    p = jnp.exp(s - m)
    es = jnp.exp(sink - m)
    inv = 1.0 / (jnp.sum(p, axis=-1, keepdims=True) + es)
    return p * inv, es * inv


def _attn_load(q_ref, kvc_ref, kvp_ref, tc_ref, ta_ref, tb_ref, pc_ref, pa_ref, pb_ref):
    c, a, b = tc_ref[...], ta_ref[...], tb_ref[...]
    kc = _rope(kvc_ref[:, :KV_W], c, a, b)
    kp = _rope(kvp_ref[:, :KV_W], pc_ref[...], pa_ref[...], pb_ref[...])
    kband = jnp.concatenate([kp, kc], axis=0)
    vband = jnp.concatenate([kvp_ref[:, KV_W:], kvc_ref[:, KV_W:]], axis=0)
    qs = [_rope(q_ref[:, 128 * j:128 * (j + 1)], c, a, b).astype(BF16) for j in range(4)]
    return qs, _band_sides(kband), _band_sides(vband), (c, a, b)


def _attn_specs(clamp):
    cur = lambda i: (clamp(i), 0)
    prev = lambda i: (jnp.maximum(clamp(i) - 1, 0), 0)
    return [
        pl.BlockSpec((BLOCK, ATTN_W), cur), pl.BlockSpec((BLOCK, 2 * KV_W), cur),
        pl.BlockSpec((BLOCK, 2 * KV_W), prev),
        pl.BlockSpec((BLOCK, 128), cur), pl.BlockSpec((BLOCK, 128), cur), pl.BlockSpec((BLOCK, 128), cur),
        pl.BlockSpec((BLOCK, 128), prev), pl.BlockSpec((BLOCK, 128), prev), pl.BlockSpec((BLOCK, 128), prev),
        pl.BlockSpec(memory_space=pltpu.SMEM),
    ]


def _attn_fwd(q, kv, tabs, sinks):
    tc, ta, tb = tabs

    def body(q_ref, kvc_ref, kvp_ref, tc_ref, ta_ref, tb_ref, pc_ref, pa_ref, pb_ref, sink_ref, o_ref):
        i = pl.program_id(0)
        qs, ks, vs, _ = _attn_load(q_ref, kvc_ref, kvp_ref, tc_ref, ta_ref, tb_ref, pc_ref, pa_ref, pb_ref)
        ok = _attn_mask(i)
        for j in range(4):
            kh = j // 2
            acc = jnp.zeros((BLOCK, 128), F32)
            for side in range(2):
                pn, _ = _attn_probs(qs[j], ks[kh][side], ok, sink_ref[0, 2 * j + side])
                acc += jnp.dot(pn.astype(BF16), vs[kh][side], preferred_element_type=F32)
            o_ref[:, 128 * j:128 * (j + 1)] = acc.astype(BF16)

    return pl.pallas_call(
        body, grid=(N_BLOCKS,), in_specs=_attn_specs(lambda i: i),
        out_specs=pl.BlockSpec((BLOCK, ATTN_W), lambda i: (i, 0)),
        out_shape=jax.ShapeDtypeStruct((SEQ, ATTN_W), BF16), compiler_params=_cp(), name="attn_fwd",
    )(q, kv, kv, tc, ta, tb, tc, ta, tb, sinks)


def _attn_bwd(q, kv, tabs, sinks, do):
    tc, ta, tb = tabs
    last = N_BLOCKS - 1
    clamp = lambda i: jnp.minimum(i, last)

    def place(full, side, kh):
        left = lax.broadcasted_iota(jnp.int32, full.shape, 1) < HEAD_DIM
        valid = jnp.where(left, full, 0.0) if side == 0 else jnp.where(left, 0.0, full)
        return valid if side == kh else pltpu.roll(valid, HEAD_DIM, axis=1)

    def body(q_ref, kvc_ref, kvp_ref, tc_ref, ta_ref, tb_ref, pc_ref, pa_ref, pb_ref, sink_ref, do_ref,
             dq_ref, dkv_ref, ds_ref, carry_ref):
        i = pl.program_id(0)

        @pl.when(i == 0)
        def _():
            ds_ref[...] = jnp.zeros_like(ds_ref)
            carry_ref[...] = jnp.zeros_like(carry_ref)

        @pl.when(i > last)
        def _():
            dkv_ref[...] = carry_ref[...].astype(BF16)

        @pl.when(i <= last)
        def _():
            qs, ks, vs, (c, a, b) = _attn_load(q_ref, kvc_ref, kvp_ref, tc_ref, ta_ref, tb_ref,
                                               pc_ref, pa_ref, pb_ref)
            ok = _attn_mask(i)
            dk = jnp.zeros((2 * BLOCK, 128), F32)
            dv = jnp.zeros((2 * BLOCK, 128), F32)
            dsink = jnp.zeros((1, 128), F32)
            lane = lax.broadcasted_iota(jnp.int32, (1, 128), 1)
            for j in range(4):
                kh = j // 2
                doc = do_ref[:, 128 * j:128 * (j + 1)].astype(BF16)
                dq = jnp.zeros((BLOCK, 128), F32)
                for side in range(2):
                    pn, ps = _attn_probs(qs[j], ks[kh][side], ok, sink_ref[0, 2 * j + side])
                    dp = lax.dot_general(doc, vs[kh][side], NT, preferred_element_type=F32)
                    dr = jnp.sum(pn * dp, axis=-1, keepdims=True)
                    dsb = (pn * (dp - dr) * (HEAD_DIM ** -0.5)).astype(BF16)
                    dsink += jnp.where(lane == 2 * j + side, -jnp.sum(ps * dr), 0.0)
                    dq += jnp.dot(dsb, ks[kh][side], preferred_element_type=F32)
                    dk += place(lax.dot_general(dsb, qs[j], TN, preferred_element_type=F32), side, kh)
                    dv += place(lax.dot_general(pn.astype(BF16), doc, TN, preferred_element_type=F32), side, kh)
                dq_ref[:, 128 * j:128 * (j + 1)] = _rope_t(dq, c, a, b).astype(BF16)
            ds_ref[...] += dsink
            dk_prev = _rope_t(dk[:BLOCK], pc_ref[...], pa_ref[...], pb_ref[...])
            dk_cur = _rope_t(dk[BLOCK:], c, a, b)
            prev = jnp.concatenate([dk_prev, dv[:BLOCK]], axis=1)
            dkv_ref[...] = (carry_ref[...] + prev).astype(BF16)
            carry_ref[...] = jnp.concatenate([dk_cur, dv[BLOCK:]], axis=1)

    return pl.pallas_call(
        body, grid=(N_BLOCKS + 1,),
        in_specs=_attn_specs(clamp) + [pl.BlockSpec((BLOCK, ATTN_W), lambda i: (clamp(i), 0))],
        out_specs=[pl.BlockSpec((BLOCK, ATTN_W), lambda i: (clamp(i), 0)),
                   pl.BlockSpec((BLOCK, 2 * KV_W), lambda i: (jnp.maximum(i - 1, 0), 0)),
                   pl.BlockSpec((1, 128), lambda i: (0, 0))],
        out_shape=[jax.ShapeDtypeStruct((SEQ, ATTN_W), BF16), jax.ShapeDtypeStruct((SEQ, 2 * KV_W), BF16),
                   jax.ShapeDtypeStruct((1, 128), F32)],
        scratch_shapes=[pltpu.VMEM((BLOCK, 2 * KV_W), F32)], compiler_params=_cp(), name="attn_bwd",
    )(q, kv, kv, tc, ta, tb, tc, ta, tb, sinks, do)


def _shift_down(z, k):
    row = lax.broadcasted_iota(jnp.int32, z.shape, 0)
    return jnp.where(row < k, 0.0, pltpu.roll(z, k, axis=0))


def _shift_up(z, k):
    n = z.shape[0]
    row = lax.broadcasted_iota(jnp.int32, z.shape, 0)
    return jnp.where(row >= n - k, 0.0, pltpu.roll(z, n - k, axis=0))


def _conv_specs():
    nb = WIDTH // 128
    return [pl.BlockSpec((SEQ, 128), lambda j: (0, j)), pl.BlockSpec((SEQ, 128), lambda j: (0, nb + j)),
            pl.BlockSpec((SEQ, 128), lambda j: (0, 2 * nb + j)), pl.BlockSpec((None, 8, 128), lambda j: (0, 0, j))]


def _conv_fwd(cbx, cw, layer):
    def body(cb_ref, cc_ref, cx_ref, w_ref, o_ref):
        z = cc_ref[...] * cx_ref[...]
        s = w_ref[0:1, :] * _shift_down(z, 2) + w_ref[1:2, :] * _shift_down(z, 1) + w_ref[2:3, :] * z
        o_ref[...] = (cb_ref[...] * s).astype(BF16)

    specs = _conv_specs()
    specs[3] = pl.BlockSpec((None, 8, 128), lambda j: (layer, 0, j))
    return pl.pallas_call(
        body, grid=(WIDTH // 128,), in_specs=specs,
        out_specs=pl.BlockSpec((SEQ, 128), lambda j: (0, j)),
        out_shape=jax.ShapeDtypeStruct((SEQ, WIDTH), BF16), compiler_params=_cp(), name="conv_fwd",
    )(cbx, cbx, cbx, cw)


def _conv_bwd(cbx, cw, layer, dout, tie):
    def body(cb_ref, cc_ref, cx_ref, w_ref, do_ref, tie_ref, dcb_ref, dcc_ref, dcx_ref, dw_ref):
        cc, cx = cc_ref[...], cx_ref[...]
        z = cc * cx
        z1, z2 = _shift_down(z, 1), _shift_down(z, 2)
        w0, w1, w2 = w_ref[0:1, :], w_ref[1:2, :], w_ref[2:3, :]
        dout = do_ref[...]
        ds = dout * cb_ref[...]
        dcb_ref[...] = (dout * (w0 * z2 + w1 * z1 + w2 * z)).astype(BF16)
        dz = w2 * ds + w1 * _shift_up(ds, 1) + w0 * _shift_up(ds, 2)
        dcc_ref[...] = (dz * cx).astype(BF16)
        dcx_ref[...] = (dz * cc).astype(BF16)
        rows = [jnp.sum(ds * zz, axis=0, keepdims=True) for zz in (z2, z1, z)]
        dw_ref[...] = jnp.concatenate(rows + [jnp.zeros((5, 128), F32)], axis=0)

    col = lambda j: (0, j)
    specs = _conv_specs()
    specs[3] = pl.BlockSpec((None, 8, 128), lambda j: (layer, 0, j))
    return pl.pallas_call(
        body, grid=(WIDTH // 128,), in_specs=specs + [pl.BlockSpec((SEQ, 128), col), ANY],
        out_specs=[pl.BlockSpec((SEQ, 128), col), pl.BlockSpec((SEQ, 128), col), pl.BlockSpec((SEQ, 128), col),
                   pl.BlockSpec((8, 128), col)],
        out_shape=[jax.ShapeDtypeStruct((SEQ, WIDTH), BF16)] * 3 + [jax.ShapeDtypeStruct((8, WIDTH), F32)],
        compiler_params=_cp(), name="conv_bwd",
    )(cbx, cbx, cbx, cw, dout, tie)


def _ssm_prep_math(a_re, a_im, log_dt, bt_re, bt_im):
    dt = jnp.exp(log_dt)
    er = jnp.exp(a_re * dt)
    lr = er * jnp.cos(a_im * dt)
    li = er * jnp.sin(a_im * dt)
    n2 = a_re * a_re + a_im * a_im
    cr = ((lr - 1.0) * a_re + li * a_im) / n2
    ci = (li * a_re - (lr - 1.0) * a_im) / n2
    cr3, ci3 = cr[:, None, :], ci[:, None, :]
    return lr, li, cr3 * bt_re - ci3 * bt_im, cr3 * bt_im + ci3 * bt_re


_GS = (SSM_GROUPS, SSM_STATE)
_GHS = (SSM_GROUPS, SSM_GROUP, SSM_STATE)


def _layered(shape):
    return pl.BlockSpec((None,) + shape, lambda l: (l,) + (0,) * len(shape))


def _ssm_prep(a_re, a_im, log_dt, bt_re, bt_im):
    def body(ar, ai, ld, br, bi, o0, o1, o2, o3):
        outs = _ssm_prep_math(ar[...], ai[...], ld[...], br[...], bi[...])
        for o, v in zip((o0, o1, o2, o3), outs):
            o[...] = v

    shapes = [_GS, _GS, _GHS, _GHS]
    return pl.pallas_call(
        body, grid=(DEPTH,), in_specs=[_layered(s) for s in (_GS, _GS, (SSM_GROUPS, 1), _GHS, _GHS)],
        out_specs=[_layered(s) for s in shapes],
        out_shape=[jax.ShapeDtypeStruct((DEPTH,) + s, F32) for s in shapes],
        name="ssm_prep")(a_re, a_im, log_dt, bt_re, bt_im)


def _ssm_prep_bwd(a_re, a_im, log_dt, bt_re, bt_im, cots):
    def body(ar, ai, ld, br, bi, c0, c1, c2, c3, o0, o1, o2, o3, o4):
        _, vjp = jax.vjp(_ssm_prep_math, ar[...], ai[...], ld[...], br[...], bi[...])
        for o, v in zip((o0, o1, o2, o3, o4), vjp((c0[...], c1[...], c2[...], c3[...]))):
            o[...] = v

    ins = (_GS, _GS, (SSM_GROUPS, 1), _GHS, _GHS)
    return pl.pallas_call(
        body, grid=(DEPTH,), in_specs=[_layered(s) for s in ins + (_GS, _GS, _GHS, _GHS)],
        out_specs=[_layered(s) for s in ins],
        out_shape=[jax.ShapeDtypeStruct((DEPTH,) + s, F32) for s in ins],
        name="ssm_prep_bwd")(a_re, a_im, log_dt, bt_re, bt_im, *cots)


LANES_G = 512
N_LANE_GROUPS = SSM_GROUPS * SSM_STATE // LANES_G


def _scan_order(a):
    return a.reshape(N_CHUNKS, CHUNK, -1).transpose(1, 0, 2).reshape(a.shape)


def _time_order(a):
    return a.reshape(CHUNK, N_CHUNKS, -1).transpose(1, 0, 2).reshape(a.shape)


def _scan_in_place(xr_ref, xi_ref, ar, ai, reverse):
    shape = (N_CHUNKS, xr_ref.shape[1])
    ar, ai = jnp.broadcast_to(ar, shape), jnp.broadcast_to(ai, shape)

    def rows(tau):
        t = (CHUNK - 1 - tau) if reverse else tau
        return pl.ds(pl.multiple_of(t * N_CHUNKS, N_CHUNKS), N_CHUNKS)

    def step(tau, carry):
        sr, si = carry
        return ar * sr - ai * si + xr_ref[rows(tau), :], ar * si + ai * sr + xi_ref[rows(tau), :]

    zero = jnp.zeros(shape, F32)
    er, ei = lax.fori_loop(0, CHUNK, step, (zero, zero), unroll=8)
    qr, qi = ar, ai
    for _ in range(8):
        qr, qi = qr * qr - qi * qi, 2.0 * qr * qi
    shift = _shift_up if reverse else _shift_down
    for k in (1, 2, 4):
        sr, si = shift(er, k), shift(ei, k)
        er, ei = er + qr * sr - qi * si, ei + qr * si + qi * sr
        qr, qi = qr * qr - qi * qi, 2.0 * qr * qi
    start = (shift(er, 1), shift(ei, 1))

    def write(tau, carry):
        sr, si = step(tau, carry)
        xr_ref[rows(tau), :] = sr
        xi_ref[rows(tau), :] = si
        return sr, si

    return write, start


def _ssm_specs(layer):
    col = lambda w: pl.BlockSpec((SEQ, w), lambda g: (0, g))
    diag = pl.BlockSpec((None, None, 128, LANES_G), lambda g: (layer, g, 0, 0))
    vec = pl.BlockSpec((None, 1, LANES_G), lambda g: (layer, 0, g))
    return col, diag, vec


def _ssm_fwd(u16, u, mats, layer, d):
    def body(u16_ref, u_ref, d_ref, br_ref, bi_ref, cr_ref, ci_ref, ar_ref, ai_ref, xr_ref, xi_ref, y_ref):
        uv = u16_ref[...]
        xr_ref[...] = jnp.dot(uv, br_ref[...], preferred_element_type=F32)
        xi_ref[...] = jnp.dot(uv, bi_ref[...], preferred_element_type=F32)
        write, start = _scan_in_place(xr_ref, xi_ref, ar_ref[...], ai_ref[...], False)
        lax.fori_loop(0, CHUNK, write, start, unroll=8)
        y = lax.dot_general(xr_ref[...].astype(BF16), cr_ref[...], NT, preferred_element_type=F32)
        y += lax.dot_general(xi_ref[...].astype(BF16), ci_ref[...], NT, preferred_element_type=F32)
        y_ref[...] = y + d_ref[...] * u_ref[...]

    col, diag, vec = _ssm_specs(layer)
    return pl.pallas_call(
        body, grid=(N_LANE_GROUPS,),
        in_specs=[col(128), col(128), pl.BlockSpec((None, 1, 128), lambda g: (layer, 0, g)),
                  diag, diag, diag, diag, vec, vec],
        out_specs=[col(LANES_G), col(LANES_G), col(128)],
        out_shape=[jax.ShapeDtypeStruct((SEQ, SSM_GROUPS * SSM_STATE), F32)] * 2
        + [jax.ShapeDtypeStruct((SEQ, WIDTH), F32)],
        compiler_params=_cp(), name="ssm_fwd",
    )(u16, u, d, mats["b_re"], mats["b_im"], mats["c_re"], mats["c_im_neg"], mats["a_re"], mats["a_im"])


def _ssm_bwd(dy16, x_re, x_im, u16, mats, layer, d):
    def body(dy_ref, u_ref, d_ref, xr_ref, xi_ref, br_ref, bi_ref, cr_ref, ci_ref, ar_ref, ai_ref,
             du_ref, dar_ref, dai_ref, dbr_ref, dbi_ref, dcr_ref, dci_ref, lr_ref, li_ref):
        dy = dy_ref[...]
        lr_ref[...] = jnp.dot(dy, cr_ref[...], preferred_element_type=F32)
        li_ref[...] = jnp.dot(dy, ci_ref[...], preferred_element_type=F32)
        write, start = _scan_in_place(lr_ref, li_ref, ar_ref[...], -ai_ref[...], True)

        def rows(t):
            return pl.ds(pl.multiple_of(t * N_CHUNKS, N_CHUNKS), N_CHUNKS)

        def grad(acc, lam, xpr, xpi):
            return acc[0] + xpr * lam[0] + xpi * lam[1], acc[1] + xpr * lam[1] - xpi * lam[0]

        def down(tau, carry):
            lam = write(tau, carry[0])
            t = CHUNK - 2 - tau
            return lam, grad(carry[1], lam, xr_ref[rows(t), :], xi_ref[rows(t), :])

        zero = jnp.zeros((N_CHUNKS, LANES_G), F32)
        lam, acc = lax.fori_loop(0, CHUNK - 1, down, (start, (zero, zero)), unroll=5)
        lam = write(CHUNK - 1, lam)
        last = rows(CHUNK - 1)
        acc = grad(acc, lam, _shift_down(xr_ref[last, :], 1), _shift_down(xi_ref[last, :], 1))
        dar_ref[...] = jnp.sum(acc[0], axis=0, keepdims=True)
        dai_ref[...] = jnp.sum(acc[1], axis=0, keepdims=True)

        l_re, l_im = lr_ref[...].astype(BF16), li_ref[...].astype(BF16)
        du = lax.dot_general(l_re, br_ref[...], NT, preferred_element_type=F32)
        du += lax.dot_general(l_im, bi_ref[...], NT, preferred_element_type=F32)
        du_ref[...] = (du + dy.astype(F32) * d_ref[...]).astype(BF16)
        uv = u_ref[...]
        dbr_ref[...] = lax.dot_general(uv, l_re, TN, preferred_element_type=F32)
        dbi_ref[...] = lax.dot_general(uv, l_im, TN, preferred_element_type=F32)
        dcr_ref[...] = lax.dot_general(dy, xr_ref[...].astype(BF16), TN, preferred_element_type=F32)
        dci_ref[...] = lax.dot_general(dy, xi_ref[...].astype(BF16), TN, preferred_element_type=F32)

    col, diag, vec = _ssm_specs(layer)
    out_vec = pl.BlockSpec((1, LANES_G), lambda g: (0, g))
    out_blk = pl.BlockSpec((None, 128, LANES_G), lambda g: (g, 0, 0))
    sds = jax.ShapeDtypeStruct
    return pl.pallas_call(
        body, grid=(N_LANE_GROUPS,),
        in_specs=[col(128), col(128), pl.BlockSpec((None, 1, 128), lambda g: (layer, 0, g)),
                  col(LANES_G), col(LANES_G), diag, diag, diag, diag, vec, vec],
        out_specs=[col(128), out_vec, out_vec, out_blk, out_blk, out_blk, out_blk],
        out_shape=[sds((SEQ, WIDTH), BF16)] + [sds((1, SSM_GROUPS * SSM_STATE), F32)] * 2
        + [sds((N_LANE_GROUPS, 128, LANES_G), F32)] * 4,
        scratch_shapes=[pltpu.VMEM((SEQ, LANES_G), F32)] * 2, compiler_params=_cp(), name="ssm_bwd",
    )(dy16, u16, d, x_re, x_im, mats["b_re"], mats["b_im"], mats["c_re"], mats["c_im_neg"],
      mats["a_re"], mats["a_im"])


_GELU_C = math.sqrt(2.0 / math.pi)


def _gelu(y):
    return 0.5 * y * (1.0 + jnp.tanh(_GELU_C * (y + 0.044715 * (y * y * y))))


def _glu_fwd(y, wglu):
    tt = 512

    def body(y_ref, w_ref, z_ref):
        ys = _gelu(y_ref[...])
        a = jnp.dot(ys.astype(BF16), w_ref[...], preferred_element_type=F32)
        z_ref[...] = (ys * jax.nn.sigmoid(a)).astype(BF16)

    blk = pl.BlockSpec((tt, WIDTH), lambda i: (i, 0))
    return pl.pallas_call(body, grid=(SEQ // tt,), in_specs=[blk, _full((WIDTH, WIDTH))], out_specs=blk,
                          out_shape=jax.ShapeDtypeStruct((SEQ, WIDTH), BF16), compiler_params=_cp(),
                          name="glu_fwd")(y, wglu)


def _glu_bwd(y, wglu, dz, u):
    tt = 512

    def body(y_ref, w_ref, dz_ref, u_ref, dy_ref, ys_ref, da_ref, dd_ref):
        @pl.when(pl.program_id(0) == 0)
        def _():
            dd_ref[...] = jnp.zeros_like(dd_ref)

        yv = y_ref[...]
        t = jnp.tanh(_GELU_C * (yv + 0.044715 * (yv * yv * yv)))
        ys = 0.5 * yv * (1.0 + t)
        ysb = ys.astype(BF16)
        sg = jax.nn.sigmoid(jnp.dot(ysb, w_ref[...], preferred_element_type=F32))
        dz = dz_ref[...].astype(F32)
        da = (dz * ys * sg * (1.0 - sg)).astype(BF16)
        dys = dz * sg + lax.dot_general(da, w_ref[...], NT, preferred_element_type=F32)
        dy = dys * (0.5 * (1.0 + t) + 0.5 * yv * (1.0 - t * t) * _GELU_C * (1.0 + 3 * 0.044715 * (yv * yv)))
        dy_ref[...] = dy.astype(BF16)
        ys_ref[...] = ysb
        da_ref[...] = da
        dd_ref[...] += jnp.sum(dy * u_ref[...], axis=0, keepdims=True)

    blk = pl.BlockSpec((tt, WIDTH), lambda i: (i, 0))
    return pl.pallas_call(
        body, grid=(SEQ // tt,), in_specs=[blk, _full((WIDTH, WIDTH)), blk, blk],
        out_specs=[blk, blk, blk, _full((1, WIDTH))],
        out_shape=[jax.ShapeDtypeStruct((SEQ, WIDTH), BF16)] * 3 + [jax.ShapeDtypeStruct((1, WIDTH), F32)],
        compiler_params=_cp(), name="glu_bwd")(y, wglu, dz, u)


def _mix_specs(tt, layer):
    row = lambda w: pl.BlockSpec((tt, w), lambda i: (i, 0))
    gate = lambda j: pl.BlockSpec((tt, D_MODEL), lambda i: (i, j))
    wo = lambda j: pl.BlockSpec((D_MODEL, WIDTH), lambda i: (0, j))
    return [row(D_MODEL), row(WIDTH), row(WIDTH), row(WIDTH), gate(0), gate(1), gate(2),
            pl.BlockSpec((None, 1, GATE_W), lambda i: (layer, 0, 0)), wo(0), wo(1), wo(2),
            _full((D_MODEL, D_MODEL))]


def _mix_branches(o_ref, c_ref, z_ref, g_refs, b_ref, wa_ref, wc_ref, ws_ref):
    ys = [lax.dot_general(r[...], w[...], NT, preferred_element_type=F32)
          for r, w in ((o_ref, wa_ref), (c_ref, wc_ref), (z_ref, ws_ref))]
    gates = [jax.nn.sigmoid(g_refs[j][...] + b_ref[:, D_MODEL * j:D_MODEL * (j + 1)]) for j in range(3)]
    return ys, gates


def _mix_fwd(x, o, cv, z, glog, b_gate, layer, wbt, wmix, tie):
    tt = 256

    def body(x_ref, o_ref, c_ref, z_ref, g0, g1, g2, b_ref, wa_ref, wc_ref, ws_ref, wm_ref, tie_ref, x1_ref):
        ys, gates = _mix_branches(o_ref, c_ref, z_ref, (g0, g1, g2), b_ref, wa_ref, wc_ref, ws_ref)
        merged = gates[0] * ys[0] + gates[1] * ys[1] + gates[2] * ys[2]
        x1_ref[...] = x_ref[...] + jnp.dot(merged.astype(BF16), wm_ref[...], preferred_element_type=F32)

    return pl.pallas_call(
        body, grid=(SEQ // tt,), in_specs=_mix_specs(tt, layer) + [ANY],
        out_specs=pl.BlockSpec((tt, D_MODEL), lambda i: (i, 0)),
        out_shape=jax.ShapeDtypeStruct((SEQ, D_MODEL), F32), compiler_params=_cp(), name="mix_fwd",
    )(x, o, cv, z, glog, glog, glog, b_gate, wbt, wbt, wbt, wmix, tie)


def _mix_bwd(dx1, o, cv, z, glog, b_gate, layer, wbt, wmix, tie):
    tt = 256

    def body(dx_ref, o_ref, c_ref, z_ref, g0, g1, g2, b_ref, wa_ref, wc_ref, ws_ref, wm_ref, tie_ref,
             mg_ref, dya_ref, dyc_ref, dys_ref, do_ref, dc_ref, dz_ref, dgl_ref, db_ref):
        @pl.when(pl.program_id(0) == 0)
        def _():
            db_ref[...] = jnp.zeros_like(db_ref)

        ys, gates = _mix_branches(o_ref, c_ref, z_ref, (g0, g1, g2), b_ref, wa_ref, wc_ref, ws_ref)
        mg_ref[...] = (gates[0] * ys[0] + gates[1] * ys[1] + gates[2] * ys[2]).astype(BF16)
        dm = lax.dot_general(dx_ref[...].astype(BF16), wm_ref[...], NT, preferred_element_type=F32)
        for j, (dy_ref, w_ref, d_ref) in enumerate(((dya_ref, wa_ref, do_ref), (dyc_ref, wc_ref, dc_ref),
                                                    (dys_ref, ws_ref, dz_ref))):
            dy = (dm * gates[j]).astype(BF16)
            dy_ref[...] = dy
            d_ref[...] = jnp.dot(dy, w_ref[...], preferred_element_type=F32)
            dgl = dm * ys[j] * gates[j] * (1.0 - gates[j])
            dgl_ref[:, D_MODEL * j:D_MODEL * (j + 1)] = dgl.astype(BF16)
            db_ref[:, D_MODEL * j:D_MODEL * (j + 1)] += jnp.sum(dgl, axis=0, keepdims=True)

    row = lambda w: pl.BlockSpec((tt, w), lambda i: (i, 0))
    sds = jax.ShapeDtypeStruct
    return pl.pallas_call(
        body, grid=(SEQ // tt,), in_specs=_mix_specs(tt, layer) + [ANY],
        out_specs=[row(D_MODEL)] * 4 + [row(WIDTH)] * 3 + [row(GATE_W), _full((1, GATE_W))],
        out_shape=[sds((SEQ, D_MODEL), BF16)] * 4 + [sds((SEQ, WIDTH), F32)] * 3
        + [sds((SEQ, GATE_W), BF16), sds((1, GATE_W), F32)],
        compiler_params=_cp(), name="mix_bwd",
    )(dx1, o, cv, z, glog, glog, glog, b_gate, wbt, wbt, wbt, wmix, tie)


def _ffn_out_fwd(x1, gu, wout, tie):
    tt = 256

    def body(x_ref, gt_ref, up_ref, w_ref, tie_ref, o_ref):
        gt = gt_ref[...]
        act = (gt * jax.nn.sigmoid(gt) * up_ref[...]).astype(BF16)
        o_ref[...] = x_ref[...] + jnp.dot(act, w_ref[...], preferred_element_type=F32)

    return pl.pallas_call(
        body, grid=(SEQ // tt,),
        in_specs=[pl.BlockSpec((tt, D_MODEL), lambda i: (i, 0)), pl.BlockSpec((tt, FFN_H), lambda i: (i, 0)),
                  pl.BlockSpec((tt, FFN_H), lambda i: (i, 1)), _full((FFN_H, D_MODEL)), ANY],
        out_specs=pl.BlockSpec((tt, D_MODEL), lambda i: (i, 0)),
        out_shape=jax.ShapeDtypeStruct((SEQ, D_MODEL), F32), compiler_params=_cp(), name="ffn_out_fwd",
    )(x1, gu, gu, wout, tie)


def _ffn_out_bwd(dx2, gu, wout, tie):
    tt = 256

    def body(dx_ref, gt_ref, up_ref, w_ref, tie_ref, dgu_ref, act_ref):
        gt, up = gt_ref[...], up_ref[...]
        sg = jax.nn.sigmoid(gt)
        silu = gt * sg
        act_ref[...] = (silu * up).astype(BF16)
        dact = lax.dot_general(dx_ref[...].astype(BF16), w_ref[...], NT, preferred_element_type=F32)
        dgu_ref[:, :FFN_H] = (dact * up * (sg * (1.0 + gt * (1.0 - sg)))).astype(BF16)
        dgu_ref[:, FFN_H:] = (dact * silu).astype(BF16)

    return pl.pallas_call(
        body, grid=(SEQ // tt,),
        in_specs=[pl.BlockSpec((tt, D_MODEL), lambda i: (i, 0)), pl.BlockSpec((tt, FFN_H), lambda i: (i, 0)),
                  pl.BlockSpec((tt, FFN_H), lambda i: (i, 1)), _full((FFN_H, D_MODEL)), ANY],
        out_specs=[pl.BlockSpec((tt, 2 * FFN_H), lambda i: (i, 0)), pl.BlockSpec((tt, FFN_H), lambda i: (i, 0))],
        out_shape=[jax.ShapeDtypeStruct((SEQ, 2 * FFN_H), BF16), jax.ShapeDtypeStruct((SEQ, FFN_H), BF16)],
        compiler_params=_cp(), name="ffn_out_bwd",
    )(dx2, gu, gu, wout, tie)


def _loss_head(x, g, target):
    tt = 256

    def body(x_ref, g_ref, t_ref, loss_ref, dx_ref, dg_ref):
        @pl.when(pl.program_id(0) == 0)
        def _():
            loss_ref[...] = jnp.zeros_like(loss_ref)
            dg_ref[...] = jnp.zeros_like(dg_ref)

        xv = x_ref[...]
        r = lax.rsqrt(jnp.mean(xv * xv, axis=-1, keepdims=True) + NORM_EPS)
        xh = xv * r
        err = xh * g_ref[...] - t_ref[...]
        loss_ref[...] += 0.5 * jnp.sum(jnp.mean(err * err, axis=-1, keepdims=True))
        dy = err * (1.0 / D_MODEL)
        gy = dy * g_ref[...]
        dx_ref[...] = r * (gy - xh * jnp.mean(gy * xh, axis=-1, keepdims=True))
        dg_ref[...] += jnp.sum(dy * xh, axis=0, keepdims=True)

    row = pl.BlockSpec((tt, D_MODEL), lambda i: (i, 0))
    return pl.pallas_call(
        body, grid=(SEQ // tt,), in_specs=[row, _full((1, D_MODEL)), row],
        out_specs=[_full((1, 128)), row, _full((1, D_MODEL))],
        out_shape=[jax.ShapeDtypeStruct((1, 128), F32), jax.ShapeDtypeStruct((SEQ, D_MODEL), F32),
                   jax.ShapeDtypeStruct((1, D_MODEL), F32)],
        compiler_params=_cp(), name="loss_head")(x, g, target)


def _adamw(parts, w, m, v, tr, name, groups=None, fill=None, tie=None):
    n_groups, rows, cols = w.shape
    n_parts = parts.shape[1]
    lo, hi = groups if groups is not None else (0, n_groups)

    def body(p_ref, w_ref, m_ref, v_ref, *rest):
        g_ref, d_ref, nm_ref, nv_ref = rest[-4:]
        g = p_ref[0].astype(F32)
        for k in range(1, n_parts):
            g = g + p_ref[k].astype(F32)
        nm = B1 * m_ref[...] + (1.0 - B1) * g
        nv = B2 * v_ref[...] + (1.0 - B2) * (g * g)
        m_hat = nm / (1.0 - B1 ** STEP)
        v_hat = nv / (1.0 - B2 ** STEP)
        g_ref[...] = g
        d_ref[...] = -LR * (m_hat / (jnp.sqrt(v_hat) + ADAM_EPS) + WD * w_ref[...])
        nm_ref[...] = nm
        nv_ref[...] = nv

    blk = pl.BlockSpec((None, tr, cols), lambda l, i: (l + lo, i, 0))
    p_lo = lo if parts.shape[0] == n_groups else 0
    extra = ([] if fill is None else list(fill)) + ([] if tie is None else [tie])
    return pl.pallas_call(
        body, grid=(hi - lo, rows // tr),
        in_specs=[pl.BlockSpec((None, n_parts, tr, cols), lambda l, i: (l + p_lo, 0, i, 0)), blk, blk, blk]
        + [ANY] * len(extra),
        out_specs=[blk] * 4, out_shape=[jax.ShapeDtypeStruct((n_groups, rows, cols), F32)] * 4,
        input_output_aliases={} if fill is None else {4 + j: j for j in range(4)},
        compiler_params=_cp(), name=name)(parts, w, m, v, *extra)


def _split_start(name, arrays, n_sems, plan, after=None):
    n = len(arrays)
    order = [] if after is None else [after]
    n_in = n + len(order)

    def body(*refs):
        ins, send_sems, recv_sems, token = refs[:n], refs[n_in], refs[n_in + 1], refs[-1]
        for src, dst, k, to in plan(ins)[0]:
            pltpu.make_async_remote_copy(src_ref=src, dst_ref=dst, send_sem=send_sems.at[k], recv_sem=recv_sems.at[k],
                                         device_id=to, device_id_type=MESH_ID).start()
        token[...] = jnp.zeros_like(token)

    outs = pl.pallas_call(
        body, name=name,
        out_shape=(pltpu.SemaphoreType.DMA((n_sems,)), pltpu.SemaphoreType.DMA((n_sems,)),
                   *[pltpu.HBM(a.shape, a.dtype) for a in arrays], jax.ShapeDtypeStruct((8, 128), F32)),
        in_specs=[HBM] * n + [ANY] * len(order),
        out_specs=(SEM, SEM, *[HBM] * n, pl.BlockSpec(memory_space=pltpu.VMEM)),
        input_output_aliases={i: 2 + i for i in range(n)},
        compiler_params=pltpu.CompilerParams(has_side_effects=EFFECT),
    )(*[pltpu.with_memory_space_constraint(a, pltpu.HBM) for a in arrays], *order)
    return outs[0], outs[1], list(outs[2:2 + n]), outs[-1]


def _split_wait(name, arrays, send_sems, recv_sems, after, plan):
    n = len(arrays)

    def body(*refs):
        ins, s_sems, r_sems = refs[:n], refs[n], refs[n + 1]
        sends, arrivals = plan(ins)
        x, y, c = lax.axis_index("x"), lax.axis_index("y"), lax.axis_index("c")
        for src, dst, k, to in sends:
            pltpu.make_async_remote_copy(src_ref=src, dst_ref=dst, send_sem=s_sems.at[k], recv_sem=r_sems.at[k],
                                         device_id=to, device_id_type=MESH_ID).wait_send()
        for dst, k in arrivals:
            pltpu.make_async_remote_copy(src_ref=dst, dst_ref=dst, send_sem=s_sems.at[k], recv_sem=r_sems.at[k],
                                         device_id=(x, y, c), device_id_type=MESH_ID).wait_recv()

    return pl.pallas_call(
        body, name=name, out_shape=[pltpu.HBM(a.shape, a.dtype) for a in arrays],
        in_specs=[HBM] * n + [SEM, SEM, ANY], out_specs=[HBM] * n,
        input_output_aliases={i: i for i in range(n)},
        compiler_params=pltpu.CompilerParams(has_side_effects=EFFECT),
    )(*arrays, send_sems, recv_sems, after)


def _chips():
    x, y, c = lax.axis_index("x"), lax.axis_index("y"), lax.axis_index("c")
    return x, y, c, [(1 - x, y), (x, 1 - y), (1 - x, 1 - y)]


def _plan_gather_chips(refs):
    x, y, c, chips = _chips()
    me = 4 * x + 2 * y + c
    n = len(refs) // 2
    sends, arrivals = [], []
    for i in range(n):
        src, land = refs[i], refs[n + i]
        sends.append((src, land.at[me], 4 * i, (x, y, 1 - c)))
        arrivals.append((land.at[4 * x + 2 * y + 1 - c], 4 * i))
        for j, (px, py) in enumerate(chips):
            sends.append((src, land.at[me], 4 * i + 1 + j, (px, py, c)))
            arrivals.append((land.at[4 * px + 2 * py + c], 4 * i + 1 + j))
    return sends, arrivals


def _plan_gather_pass(refs):
    x, y, c, chips = _chips()
    sends, arrivals = [], []
    for i in range(len(refs)):
        for j, (px, py) in enumerate(chips):
            slot = refs[i].at[4 * px + 2 * py + c]
            sends.append((slot, slot, 3 * i + j, (x, y, 1 - c)))
            arrivals.append((refs[i].at[4 * px + 2 * py + 1 - c], 3 * i + j))
    return sends, arrivals


def _plan_scatter_pair(refs):
    x, y, c = lax.axis_index("x"), lax.axis_index("y"), lax.axis_index("c")
    n = len(refs) // 2
    sends, arrivals = [], []
    for i in range(n):
        for q in range(4):
            sends.append((refs[i].at[q, 1 - c], refs[n + i].at[q], 4 * i + q, (x, y, 1 - c)))
            arrivals.append((refs[n + i].at[q], 4 * i + q))
    return sends, arrivals


def _plan_scatter_chips(layer):
    def plan(refs):
        x, y, c, chips = _chips()
        n = len(refs) // 2
        sends, arrivals = [], []
        for i in range(n):
            for j, (px, py) in enumerate(chips):
                sends.append((refs[i].at[2 * px + py], refs[n + i].at[layer, 2 * x + y], 3 * i + j, (px, py, c)))
                arrivals.append((refs[n + i].at[layer, 2 * px + py], 3 * i + j))
        return sends, arrivals

    return plan


def _pair_sum(parts4, from_pair, landing, layer, core, tr, name):
    _, _, rows, cols = parts4.shape

    def body(c_ref, p_ref, s_ref, l_ref, sum_ref, land_ref):
        v = (p_ref[...].astype(F32) + s_ref[...].astype(F32)).astype(BF16)
        sum_ref[...] = v
        land_ref[...] = v

    blk = pl.BlockSpec((None, tr, cols), lambda q, i, c_ref: (q, i, 0))
    return pl.pallas_call(
        body,
        grid_spec=pltpu.PrefetchScalarGridSpec(
            num_scalar_prefetch=1, grid=(4, rows // tr),
            in_specs=[pl.BlockSpec((None, None, tr, cols), lambda q, i, c_ref: (q, c_ref[0], i, 0)), blk, ANY],
            out_specs=[blk, pl.BlockSpec((None, None, tr, cols), lambda q, i, c_ref: (layer, q, i, 0))]),
        out_shape=[jax.ShapeDtypeStruct((4, rows, cols), BF16), jax.ShapeDtypeStruct(landing.shape, BF16)],
        input_output_aliases={3: 1}, compiler_params=_cp(), name=name,
    )(core, parts4, from_pair, landing)


def _travel_layout(t):
    tr = lambda a: jnp.swapaxes(a, 1, 2)
    branch = jnp.concatenate([tr(t["w_attn_o"]), tr(t["w_conv_o"]), tr(t["w_ssm_o"])], axis=2)
    return [tr(t["w_in"]), tr(t["w_ffn_in"]), t["w_ffn_out"], t["w_mix_o"], branch, t["w_ssm_glu"]]


def _native_layout(a):
    tr = lambda x: jnp.swapaxes(x, 1, 2)
    b = a[4]
    return {"w_in": tr(a[0]), "w_ffn_in": tr(a[1]), "w_ffn_out": a[2], "w_mix_o": a[3],
            "w_attn_o": tr(b[:, :, :WIDTH]), "w_conv_o": tr(b[:, :, WIDTH:2 * WIDTH]),
            "w_ssm_o": tr(b[:, :, 2 * WIDTH:]), "w_ssm_glu": a[5]}


def _embed(t):
    eye = jnp.eye(8, dtype=t.dtype)
    t = t.reshape(DEPTH, N_LANE_GROUPS, 8, SSM_GROUP, SSM_STATE)
    return (t[:, :, :, :, None, :] * eye[None, None, :, None, :, None]).reshape(DEPTH, N_LANE_GROUPS, 128, LANES_G)


def _diag_blocks(t):
    t = t.reshape(DEPTH, N_LANE_GROUPS, 8, SSM_GROUP, 8, SSM_STATE)
    return jnp.einsum("lgahap->lgahp", t).reshape(DEPTH, SSM_GROUPS, SSM_GROUP, SSM_STATE)


def _rope_tabs():
    pos = jnp.arange(SEQ, dtype=F32)
    inv_freq = ROPE_THETA ** (-jnp.arange(0, ROT_DIM, 2, dtype=F32) / ROT_DIM)
    ang = pos[:, None] * inv_freq[None, :]
    cos, sin = jnp.cos(ang), jnp.sin(ang)
    one, zero = jnp.ones((SEQ, HEAD_DIM - ROT_DIM), F32), jnp.zeros((SEQ, HEAD_DIM - ROT_DIM), F32)
    z8 = jnp.zeros((SEQ, 8), F32)
    head = lambda *p: jnp.tile(jnp.concatenate(p, axis=1), (1, 2))
    return head(cos, cos, one), head(-sin, z8, zero), head(z8, sin, zero)


def _ssm_mats(sp):
    lr, li, bbr, bbi = _ssm_prep(sp["a_re"], sp["a_im"], sp["log_dt"], sp["bt_re"], sp["bt_im"])
    lanes = SSM_GROUPS * SSM_STATE
    return {
        "a_re": lr.reshape(DEPTH, 1, lanes), "a_im": li.reshape(DEPTH, 1, lanes),
        "b_re": _embed(bbr).astype(BF16), "b_im": _embed(bbi).astype(BF16),
        "c_re": _embed(sp["c_re"]).astype(BF16), "c_im_neg": _embed(-sp["c_im"]).astype(BF16),
    }


def _layer_fwd(x, i, w, rp, mats, tabs, tie, hooks):
    q, kv, cbx, u, u16, glog, h = _rms_mm_in(x, rp["norm_mix"][i], w["win_t"], tie)
    o = _attn_fwd(q, kv, tabs, rp["attn_sinks"][i])
    cv = _conv_fwd(cbx, rp["conv_w"], i)
    u16, u = _scan_order(u16), _scan_order(u)
    x_re, x_im, y = _ssm_fwd(u16, u, mats, i, rp["ssm_d"])
    z = _time_order(_glu_fwd(y, w["wglu"]))
    x1 = _mix_fwd(x, o, cv, z, glog, rp["b_gate"], i, w["branch_t"], w["wmix"], hooks["early"](z))
    hooks["pre_ffn"](x1)
    gu, h2 = _rms_mm_ffn(x1, rp["norm_ffn"][i], w["wffn_t"])
    x2 = _ffn_out_fwd(x1, gu, w["wout"], hooks["mid"](h2))
    kept = dict(x=x, q=q, kv=kv, cbx=cbx, u=u, u16=u16, glog=glog, h=h, o=o, cv=cv, z=z, y=y,
                x_re=x_re, x_im=x_im, x1=x1, gu=gu, h2=h2)
    return x2, kept


def _layer_bwd(dx2, k, i, w, rp, mats, tabs, tie, hooks):
    tn = dict(ta=True, out_dtype=BF16)
    dgu, act = _ffn_out_bwd(dx2, k["gu"], w["wout"], tie)
    g_wout = _mm(act, dx2, tm=FFN_H // 2, tn=1024, tk=512, name="mm_tn_ffn_out", **tn)
    g_wffn_t = _mm(dgu, k["h2"], tm=FFN_H // 2, tn=1024, tk=512, name="mm_tn_ffn_in", **tn)
    dx1, d_norm_ffn = _mm_rmsbwd([dgu], w["wffn_t"], k["x1"], rp["norm_ffn"][i], dx2, "mm_rmsbwd_ffn")

    mg, dya, dyc, dys, do, dcv, dz, dgl, db_gate = _mix_bwd(
        dx1, k["o"], k["cv"], k["z"], k["glog"], rp["b_gate"], i, w["branch_t"], w["wmix"],
        hooks["mid"]((g_wffn_t, g_wout, d_norm_ffn)))
    g_wmix = _mm(mg, dx1, tm=1024, tn=1024, tk=512, name="mm_tn_mix", **tn)
    g_branch_t = _tn_branches((dya, dyc, dys), (k["o"], k["cv"], k["z"]))

    dy16, ys16, da16, dd = _glu_bwd(k["y"], w["wglu"], _scan_order(dz), k["u"])
    g_wglu = _mm(ys16, da16, tm=512, tn=512, tk=512, name="mm_tn_glu", **tn)
    du, da_re, da_im, db_re, db_im, dc_re, dc_im = _ssm_bwd(dy16, k["x_re"], k["x_im"], k["u16"], mats, i,
                                                             rp["ssm_d"])
    du = _time_order(du)

    dcb, dcc, dcx, d_conv_w = _conv_bwd(k["cbx"], rp["conv_w"], i, dcv, hooks["late"](du))
    dq, dkv, d_sinks = _attn_bwd(k["q"], k["kv"], tabs, rp["attn_sinks"][i], do)

    pieces = [dq, dkv, dcb, dcc, dcx, du, dgl]
    g_win_t = _tn_pieces(pieces, k["h"])
    dx, d_norm_mix = _mm_rmsbwd(pieces, w["win_t"], k["x"], rp["norm_mix"][i], dx1, "mm_rmsbwd_in")

    grads = [g_win_t, g_wffn_t, g_wout, g_wmix, g_branch_t, g_wglu]
    small = dict(norm_mix=d_norm_mix, b_gate=db_gate, attn_sinks=d_sinks, ssm_d=dd, norm_ffn=d_norm_ffn,
                 conv_w=d_conv_w, da_re=da_re, da_im=da_im, db_re=db_re, db_im=db_im, dc_re=dc_re, dc_im=dc_im)
    return dx, grads, small


def _replicated_grads(sg, sp):
    stack = lambda name: jnp.stack([sg[i][name] for i in range(DEPTH)])
    cots = (stack("da_re").reshape(DEPTH, *_GS), stack("da_im").reshape(DEPTH, *_GS),
            _diag_blocks(stack("db_re")), _diag_blocks(stack("db_im")))
    d_a_re, d_a_im, d_log_dt, d_bt_re, d_bt_im = _ssm_prep_bwd(
        sp["a_re"], sp["a_im"], sp["log_dt"], sp["bt_re"], sp["bt_im"], cots)
    sgrads = {"norm_mix": stack("norm_mix"), "b_gate": stack("b_gate"),
              "attn_sinks": stack("attn_sinks")[:, :, :N_Q_HEADS], "ssm_a_re": d_a_re, "ssm_a_im": d_a_im,
              "ssm_b_re": jnp.swapaxes(d_bt_re, 2, 3), "ssm_b_im": jnp.swapaxes(d_bt_im, 2, 3),
              "ssm_c_re": _diag_blocks(stack("dc_re")), "ssm_c_im": -_diag_blocks(stack("dc_im")),
              "ssm_d": stack("ssm_d"), "ssm_log_dt": d_log_dt, "norm_ffn": stack("norm_ffn")}
    return sgrads, stack("conv_w")[:, :3]


def kernel(x, norm_mix, w_in, b_gate, attn_sinks, w_attn_o, conv_w, w_conv_o, ssm_a_re, ssm_a_im, ssm_b_re, ssm_b_im, ssm_c_re, ssm_c_im, ssm_d, ssm_log_dt, w_ssm_glu, w_ssm_o, w_mix_o, norm_ffn, w_ffn_in, w_ffn_out, norm_final, loss_target, m_norm_mix, m_w_in, m_b_gate, m_attn_sinks, m_w_attn_o, m_conv_w, m_w_conv_o, m_ssm_a_re, m_ssm_a_im, m_ssm_b_re, m_ssm_b_im, m_ssm_c_re, m_ssm_c_im, m_ssm_d, m_ssm_log_dt, m_w_ssm_glu, m_w_ssm_o, m_w_mix_o, m_norm_ffn, m_w_ffn_in, m_w_ffn_out, m_norm_final, v_norm_mix, v_w_in, v_b_gate, v_attn_sinks, v_w_attn_o, v_conv_w, v_w_conv_o, v_ssm_a_re, v_ssm_a_im, v_ssm_b_re, v_ssm_b_im, v_ssm_c_re, v_ssm_c_im, v_ssm_d, v_ssm_log_dt, v_w_ssm_glu, v_w_ssm_o, v_w_mix_o, v_norm_ffn, v_w_ffn_in, v_w_ffn_out, v_norm_final):
    big = {"w": dict(w_in=w_in, w_attn_o=w_attn_o, w_conv_o=w_conv_o, w_ssm_glu=w_ssm_glu, w_ssm_o=w_ssm_o,
                     w_mix_o=w_mix_o, w_ffn_in=w_ffn_in, w_ffn_out=w_ffn_out),
           "m": dict(w_in=m_w_in, w_attn_o=m_w_attn_o, w_conv_o=m_w_conv_o, w_ssm_glu=m_w_ssm_glu,
                     w_ssm_o=m_w_ssm_o, w_mix_o=m_w_mix_o, w_ffn_in=m_w_ffn_in, w_ffn_out=m_w_ffn_out),
           "v": dict(w_in=v_w_in, w_attn_o=v_w_attn_o, w_conv_o=v_w_conv_o, w_ssm_glu=v_w_ssm_glu,
                     w_ssm_o=v_w_ssm_o, w_mix_o=v_w_mix_o, w_ffn_in=v_w_ffn_in, w_ffn_out=v_w_ffn_out)}
    small = {"w": dict(norm_mix=norm_mix, b_gate=b_gate, attn_sinks=attn_sinks, ssm_a_re=ssm_a_re,
                       ssm_a_im=ssm_a_im, ssm_b_re=ssm_b_re, ssm_b_im=ssm_b_im, ssm_c_re=ssm_c_re,
                       ssm_c_im=ssm_c_im, ssm_d=ssm_d, ssm_log_dt=ssm_log_dt, norm_ffn=norm_ffn),
             "m": dict(norm_mix=m_norm_mix, b_gate=m_b_gate, attn_sinks=m_attn_sinks, ssm_a_re=m_ssm_a_re,
                       ssm_a_im=m_ssm_a_im, ssm_b_re=m_ssm_b_re, ssm_b_im=m_ssm_b_im, ssm_c_re=m_ssm_c_re,
                       ssm_c_im=m_ssm_c_im, ssm_d=m_ssm_d, ssm_log_dt=m_ssm_log_dt, norm_ffn=m_norm_ffn),
             "v": dict(norm_mix=v_norm_mix, b_gate=v_b_gate, attn_sinks=v_attn_sinks, ssm_a_re=v_ssm_a_re,
                       ssm_a_im=v_ssm_a_im, ssm_b_re=v_ssm_b_re, ssm_b_im=v_ssm_b_im, ssm_c_re=v_ssm_c_re,
                       ssm_c_im=v_ssm_c_im, ssm_d=v_ssm_d, ssm_log_dt=v_ssm_log_dt, norm_ffn=v_norm_ffn)}
    finals = {"w": norm_final, "m": m_norm_final, "v": v_norm_final}
    convs = {"w": conv_w, "m": m_conv_w, "v": v_conv_w}
    mine = 4 * lax.axis_index("x") + 2 * lax.axis_index("y") + lax.axis_index("c")

    travel = {s: _travel_layout(big[s]) for s in "wmv"}
    stacked16 = [a.astype(BF16) for a in travel["w"]]
    rp = {"norm_mix": norm_mix[:, None], "norm_ffn": norm_ffn[:, None], "attn_sinks": attn_sinks[:, None],
          "b_gate": b_gate[:, None], "ssm_d": ssm_d[:, None]}
    sp = {"a_re": ssm_a_re, "a_im": ssm_a_im, "log_dt": ssm_log_dt[:, :, None],
          "bt_re": jnp.swapaxes(ssm_b_re, 2, 3), "bt_im": jnp.swapaxes(ssm_b_im, 2, 3),
          "c_re": ssm_c_re, "c_im": ssm_c_im}
    rows_tile = {"win_t": 368, "wffn_t": 352, "wout": 176, "wmix": 128, "branch_t": 128, "wglu": 64}
    core = lax.axis_index("c").astype(jnp.int32).reshape(1)
    no_tie = jnp.zeros((8, 128), F32)

    def place_own(srcs):
        return [lax.dynamic_update_slice(lax.empty((N_DEV,) + s.shape, s.dtype), s[None], (mine, 0, 0)) for s in srcs]

    def gather_chips(tag, i, kinds, after, extra=()):
        srcs = [stacked16[j][i] for j in kinds] + list(extra)
        s_sems, r_sems, arrays, token = _split_start(
            f"gather_chips_start_{tag}", srcs + place_own(srcs), 4 * len(srcs), _plan_gather_chips, after)
        return (tag, s_sems, r_sems, arrays), token

    def gather_pass(state, after):
        tag, s_sems, r_sems, arrays = state
        arrays = _split_wait(f"gather_chips_wait_{tag}", arrays, s_sems, r_sems, after, _plan_gather_chips)
        n = len(arrays) // 2
        s_sems, r_sems, lands, token = _split_start(
            f"gather_pass_start_{tag}", list(arrays[n:]), 3 * n, _plan_gather_pass)
        return (tag, s_sems, r_sems, lands), token

    def gather_done(state, after, kinds):
        tag, s_sems, r_sems, lands = state
        lands = _split_wait(f"gather_pass_wait_{tag}", lands, s_sems, r_sems, after, _plan_gather_pass)
        named = {KINDS[j][0]: a.reshape(N_DEV * KINDS[j][1], KINDS[j][2]) for a, j in zip(lands, kinds)}
        return named, list(lands[len(kinds):])

    all_kinds, mixer_kinds, ffn_kinds = tuple(range(len(KINDS))), (0, 3, 4, 5), (1, 2)
    no_hooks = {name: (lambda value: no_tie) for name in ("early", "pre_ffn", "mid", "late")}
    state, _ = gather_chips("0m", 0, mixer_kinds, None, extra=[jnp.pad(conv_w.reshape(6, 128), ((0, 2), (0, 0)))])
    mats = _ssm_mats(sp)
    tabs = _rope_tabs()
    state, _ = gather_pass(state, mats["c_im_neg"])
    ffn_state, tie = gather_chips("0f", 0, ffn_kinds, state[3][0])
    w_next, (conv_all,) = gather_done(state, tabs[2], mixer_kinds)
    conv_full = conv_all[:, :6].reshape(N_DEV, DEPTH, 3, 64).transpose(1, 2, 0, 3).reshape(DEPTH, 3, WIDTH)
    rp["conv_w"] = jnp.pad(conv_full, ((0, 0), (0, 5), (0, 0)))

    act = x[0]
    weights, kept = [], []
    for i in range(DEPTH):
        w_i, hooks, held = w_next, dict(no_hooks), {}
        if i == 0:
            def early(value, held=held):
                held["ffn"], token = gather_pass(ffn_state, value)
                return token

            def pre_ffn(value, w_i=w_i, held=held):
                w_i.update(gather_done(held["ffn"], value, ffn_kinds)[0])

            hooks.update(early=early, pre_ffn=pre_ffn)
        if i + 1 < DEPTH:
            state, tie_next = gather_chips(str(i + 1), i + 1, all_kinds, w_i["win_t"])
            tie = tie_next if i > 0 else tie

            def mid(value, state=state, held=held):
                held["next"], token = gather_pass(state, value)
                return token

            hooks.update(mid=mid)
        elif i > 0:
            tie = no_tie
        act, k = _layer_fwd(act, i, w_i, rp, mats, tabs, tie, hooks)
        if i + 1 < DEPTH:
            w_next, _ = gather_done(held["next"], act, all_kinds)
        weights.append(w_i)
        kept.append(k)
    loss_row, dx, d_norm_final = _loss_head(act, norm_final[None], loss_target[0])
    loss = lax.psum(loss_row[0, 0], ("x", "y", "c"))

    landings = [lax.empty((DEPTH, 4, r, c), BF16) for _, r, c in KINDS]
    landings0 = [lax.empty((1, 4, r, c), BF16) for _, r, c in KINDS]

    def scatter_pair(tag, kinds, grads, after):
        parts4 = [g.reshape(4, 2, KINDS[j][1], KINDS[j][2]) for g, j in zip(grads, kinds)]
        zones = [lax.empty((4, KINDS[j][1], KINDS[j][2]), BF16) for j in kinds]
        s_sems, r_sems, arrays, token = _split_start(
            f"scatter_pair_start_{tag}", parts4 + zones, 4 * len(kinds), _plan_scatter_pair, after)
        return (tag, kinds, s_sems, r_sems, arrays), token

    def scatter_chips(state, lands, slot, after):
        tag, kinds, s_sems, r_sems, arrays = state
        arrays = _split_wait(f"scatter_pair_wait_{tag}", arrays, s_sems, r_sems, after, _plan_scatter_pair)
        n = len(kinds)
        sums, mine_lands = [], []
        for k, j in enumerate(kinds):
            name = KINDS[j][0]
            chip_sum, land = _pair_sum(arrays[k], arrays[n + k], lands[j], slot, core, rows_tile[name],
                                       f"pair_sum_{name}")
            sums.append(chip_sum)
            mine_lands.append(land)
        s_sems, r_sems, arrays, token = _split_start(
            f"scatter_chips_start_{tag}", sums + mine_lands, 3 * n, _plan_scatter_chips(slot))
        return (tag, kinds, slot, s_sems, r_sems, arrays), token

    def scatter_done(state, lands, after):
        tag, kinds, slot, s_sems, r_sems, arrays = state
        arrays = _split_wait(f"scatter_chips_wait_{tag}", arrays, s_sems, r_sems, after, _plan_scatter_chips(slot))
        lands = list(lands)
        for k, j in enumerate(kinds):
            lands[j] = arrays[len(kinds) + k]
        return lands

    sg = [None] * DEPTH
    pending, tie = None, no_tie
    for i in reversed(range(DEPTH)):
        hooks, held = dict(no_hooks), {}
        if pending is not None:
            def mid(value, i=i, pending=pending, held=held):
                held["chips"], token = scatter_chips(pending, landings, i + 1, value[2])
                if i == 0:
                    held["ffn_pair"], _ = scatter_pair("0f", ffn_kinds, value[:2], value[2])
                return token

            hooks.update(mid=mid)
        if i == 0:
            def late(value, held=held):
                held["ffn_chips"], token = scatter_chips(held["ffn_pair"], landings0, 0, value)
                return token

            hooks.update(late=late)
        dx, grads, sg[i] = _layer_bwd(dx, kept[i], i, weights[i], rp, mats, tabs, tie, hooks)
        if pending is not None:
            landings = scatter_done(held["chips"], landings, dx)
        if i > 0:
            pending, tie = scatter_pair(str(i), all_kinds, grads, dx)
        else:
            pending, _ = scatter_pair("0m", mixer_kinds, [grads[j] for j in mixer_kinds], dx)

    sgrads, conv_grad = _replicated_grads(sg, sp)

    def pack_small(t, final, conv):
        flat = [t[name].reshape(DEPTH, n) for name, n in SMALL]
        flat = jnp.concatenate([jnp.concatenate(flat, axis=1).reshape(-1), final.reshape(-1), conv.reshape(-1)])
        return jnp.pad(flat, (0, SMALL_ROWS * 128 - flat.shape[0])).reshape(SMALL_ROWS, 128)

    small_src = [pack_small(sgrads, d_norm_final, conv_grad).astype(BF16)]
    last, tie = scatter_chips(pending, landings0, 0, small_src[0])
    s_sems, r_sems, arrays, _ = _split_start(
        "gather_small_chips_start", small_src + place_own(small_src), 4, _plan_gather_chips, last[5][0])
    small_state = ("small", s_sems, r_sems, arrays)

    big_out = [_adamw(landings[j], travel["w"][j], travel["m"][j], travel["v"][j], rows_tile[name],
                      "adamw_late_" + name, groups=(1, DEPTH), tie=tie) for j, (name, _, _) in enumerate(KINDS)]
    landings0 = scatter_done(held["ffn_chips"], landings0, big_out[-1][0])
    landings0 = scatter_done(last, landings0, big_out[-1][0])
    small_state, _ = gather_pass(small_state, landings0[0])
    big_out = [_adamw(landings0[j], travel["w"][j], travel["m"][j], travel["v"][j], rows_tile[name],
                      "adamw_first_" + name, groups=(0, 1), fill=big_out[j]) for j, (name, _, _) in enumerate(KINDS)]
    big_res = [_native_layout([big_out[j][kind] for j in range(len(KINDS))]) for kind in range(4)]

    zeros_conv = jnp.zeros((CONV_N,), F32)
    _, (sparts,) = gather_done(small_state, big_out[-1][0], ())
    sw, sm_, sv = (pack_small(small[s], finals[s], zeros_conv) for s in "wmv")
    small_out = _adamw(sparts[None], sw[None], sm_[None], sv[None], SMALL_ROWS // 8, "adamw_replicated")

    def unpack_small(p):
        flat = p.reshape(-1)
        per = flat[:DEPTH * SMALL_PER_LAYER].reshape(DEPTH, SMALL_PER_LAYER)
        out, off = {}, 0
        for name, n in SMALL:
            out[name] = per[:, off:off + n].reshape(small["w"][name].shape)
            off += n
        out["norm_final"] = flat[DEPTH * SMALL_PER_LAYER:DEPTH * SMALL_PER_LAYER + D_MODEL]
        return out

    small_res = [unpack_small(p) for p in small_out]

    conv_off = DEPTH * SMALL_PER_LAYER + D_MODEL
    conv_parts = sparts.reshape(N_DEV, -1)[:, conv_off:conv_off + CONV_N].reshape(N_DEV, DEPTH * 3, WIDTH)
    conv_parts = lax.dynamic_slice_in_dim(conv_parts, mine * 64, 64, axis=2)
    conv_res = _adamw(conv_parts[None], *(convs[s].reshape(1, DEPTH * 3, 64) for s in "wmv"), DEPTH * 3, "adamw_conv_w")

    order = ["norm_mix", "w_in", "b_gate", "attn_sinks", "w_attn_o", "conv_w", "w_conv_o", "ssm_a_re", "ssm_a_im",
             "ssm_b_re", "ssm_b_im", "ssm_c_re", "ssm_c_im", "ssm_d", "ssm_log_dt", "w_ssm_glu", "w_ssm_o",
             "w_mix_o", "norm_ffn", "w_ffn_in", "w_ffn_out", "norm_final"]
    outs = [loss, dx[None]]
    for kind in range(4):
        for name in order:
            if name == "conv_w":
                outs.append(conv_res[kind].reshape(DEPTH, 3, 64))
            elif name in big_res[kind]:
                outs.append(big_res[kind][name])
            else:
                outs.append(small_res[kind][name])
    return tuple(outs)
```

```python
import functools
import math

import jax
import jax.numpy as jnp
from jax import lax
from jax.experimental import pallas as pl
from jax.experimental.pallas import tpu as pltpu

F32 = jnp.float32
BF16 = jnp.bfloat16

N_DEV = 8
DEPTH = 4
SEQ = 2048
D_MODEL = 1024
N_Q_HEADS = 8
HEAD_DIM = 64
ATTN_W = 512
KV_W = 128
BLOCK = 128
N_BLOCKS = SEQ // BLOCK
ROPE_THETA = 500000.0
ROT_DIM = 16
NEG_INF = -1e30
WIDTH = 512
SSM_GROUPS = 32
SSM_GROUP = 16
SSM_STATE = 64
SLABS = 16
CHUNK = 256
N_CHUNKS = SEQ // CHUNK
GATE_W = 3 * D_MODEL
IN_COLS = 5888
FFN_H = 2816
NORM_EPS = 1e-6
LR, B1, B2, ADAM_EPS, WD, STEP = 0.001, 0.9, 0.999, 1e-08, 0.01, 10

COL_Q, COL_KV, COL_CBX, COL_U, COL_G = 0, 512, 768, 2304, 2816
PIECE_W = (512, 256, 512, 512, 512, 512, 3072)
PIECE_OFF = tuple(sum(PIECE_W[:i]) for i in range(len(PIECE_W)))

KINDS = (("win_t", 736, 1024), ("wffn_t", 704, 1024), ("wout", 352, 1024), ("wmix", 128, 1024),
         ("branch_t", 128, 1536), ("wglu", 64, 512))

SMALL = (("norm_mix", 1024), ("b_gate", 3072), ("attn_sinks", 8), ("ssm_a_re", 2048), ("ssm_a_im", 2048),
         ("ssm_b_re", 32768), ("ssm_b_im", 32768), ("ssm_c_re", 32768), ("ssm_c_im", 32768),
         ("ssm_d", 512), ("ssm_log_dt", 32), ("norm_ffn", 1024))
SMALL_PER_LAYER = sum(n for _, n in SMALL)
CONV_N = DEPTH * 3 * WIDTH
SMALL_ROWS = 4480

VMEM_LIMIT = 56 * 1024 * 1024
NT = (((1,), (1,)), ((), ()))
TN = (((0,), (0,)), ((), ()))
MESH_ID = pl.DeviceIdType.MESH
ANY = pl.BlockSpec(memory_space=pl.ANY)
HBM = pl.BlockSpec(memory_space=pltpu.HBM)
SEM = pl.BlockSpec(memory_space=pltpu.SEMAPHORE)
EFFECT = pltpu.SideEffectType.DATAFLOW_SIDE_EFFECTING


def _cp(**kw):
    return pltpu.CompilerParams(vmem_limit_bytes=VMEM_LIMIT, **kw)


def _full(shape):
    return pl.BlockSpec(shape, lambda *_: (0,) * len(shape))


def _mm(a, b, *, ta=False, tb=False, tm, tn, tk, out_dtype=F32, name):
    m = a.shape[1] if ta else a.shape[0]
    k = a.shape[0] if ta else a.shape[1]
    n = b.shape[0] if tb else b.shape[1]
    nk = k // tk
    dims = (((0 if ta else 1,), (1 if tb else 0,)), ((), ()))

    def body(a_ref, b_ref, o_ref, acc_ref):
        kk = pl.program_id(2)

        @pl.when(kk == 0)
        def _():
            acc_ref[...] = jnp.zeros_like(acc_ref)

        acc_ref[...] += lax.dot_general(a_ref[...].astype(BF16), b_ref[...].astype(BF16), dims,
                                        preferred_element_type=F32)

        @pl.when(kk == nk - 1)
        def _():
            o_ref[...] = acc_ref[...].astype(out_dtype)

    a_spec = pl.BlockSpec((tk, tm), lambda i, j, kk: (kk, i)) if ta else pl.BlockSpec((tm, tk), lambda i, j, kk: (i, kk))
    b_spec = pl.BlockSpec((tn, tk), lambda i, j, kk: (j, kk)) if tb else pl.BlockSpec((tk, tn), lambda i, j, kk: (kk, j))
    return pl.pallas_call(
        body, grid=(m // tm, n // tn, nk), in_specs=[a_spec, b_spec],
        out_specs=pl.BlockSpec((tm, tn), lambda i, j, kk: (i, j)),
        out_shape=jax.ShapeDtypeStruct((m, n), out_dtype),
        scratch_shapes=[pltpu.VMEM((tm, tn), F32)], compiler_params=_cp(), name=name)(a, b)


def _rms_rows(xv, g):
    r = lax.rsqrt(jnp.mean(xv * xv, axis=-1, keepdims=True) + NORM_EPS)
    return ((xv * r) * g).astype(BF16)


def _rms_mm_in(x, g, wt, tie):
    tt = 256
    widths = (ATTN_W, 2 * KV_W, 3 * WIDTH, WIDTH, GATE_W)
    offs = (COL_Q, COL_KV, COL_CBX, COL_U, COL_G)

    def body(x_ref, g_ref, w_ref, tie_ref, q_ref, kv_ref, cbx_ref, u_ref, u16_ref, gl_ref, h_ref):
        h = _rms_rows(x_ref[...], g_ref[...])
        h_ref[...] = h
        prod = lax.dot_general(h, w_ref[...], NT, preferred_element_type=F32)
        for ref, o, w in zip((q_ref, kv_ref, cbx_ref, u_ref, gl_ref), offs, widths):
            ref[...] = prod[:, o:o + w]
        u16_ref[...] = prod[:, COL_U:COL_U + WIDTH].astype(BF16)

    row = lambda w: pl.BlockSpec((tt, w), lambda i: (i, 0))
    sds = jax.ShapeDtypeStruct
    return pl.pallas_call(
        body, grid=(SEQ // tt,), in_specs=[row(D_MODEL), _full((1, D_MODEL)), _full((IN_COLS, D_MODEL)), ANY],
        out_specs=[row(ATTN_W), row(2 * KV_W), row(3 * WIDTH), row(WIDTH), row(WIDTH), row(GATE_W), row(D_MODEL)],
        out_shape=[sds((SEQ, ATTN_W), F32), sds((SEQ, 2 * KV_W), F32), sds((SEQ, 3 * WIDTH), F32),
                   sds((SEQ, WIDTH), F32), sds((SEQ, WIDTH), BF16), sds((SEQ, GATE_W), F32),
                   sds((SEQ, D_MODEL), BF16)],
        compiler_params=_cp(), name="rms_mm_in")(x, g, wt, tie)


def _rms_mm_ffn(x, g, wt):
    tt = 256

    def body(x_ref, g_ref, w_ref, o_ref, h_ref):
        h = _rms_rows(x_ref[...], g_ref[...])
        h_ref[...] = h
        o_ref[...] = lax.dot_general(h, w_ref[...], NT, preferred_element_type=F32)

    row = lambda w: pl.BlockSpec((tt, w), lambda i: (i, 0))
    return pl.pallas_call(
        body, grid=(SEQ // tt,), in_specs=[row(D_MODEL), _full((1, D_MODEL)), _full((2 * FFN_H, D_MODEL))],
        out_specs=[row(2 * FFN_H), row(D_MODEL)],
        out_shape=[jax.ShapeDtypeStruct((SEQ, 2 * FFN_H), F32), jax.ShapeDtypeStruct((SEQ, D_MODEL), BF16)],
        compiler_params=_cp(), name="rms_mm_ffn")(x, g, wt)


def _mm_rmsbwd(pieces, wt, x, g, dres, name):
    tt = 256
    widths = [p.shape[1] for p in pieces]
    offs = [sum(widths[:i]) for i in range(len(widths))]
    n = len(pieces)

    def body(*refs):
        p_refs, (w_ref, x_ref, g_ref, r_ref, dx_ref, dg_ref) = refs[:n], refs[n:]

        @pl.when(pl.program_id(0) == 0)
        def _():
            dg_ref[...] = jnp.zeros_like(dg_ref)

        dh = jnp.zeros((tt, D_MODEL), F32)
        for p_ref, o, w in zip(p_refs, offs, widths):
            dh += jnp.dot(p_ref[...], w_ref[o:o + w, :], preferred_element_type=F32)
        xv = x_ref[...]
        r = lax.rsqrt(jnp.mean(xv * xv, axis=-1, keepdims=True) + NORM_EPS)
        xh = xv * r
        gy = dh * g_ref[...]
        dx_ref[...] = r_ref[...] + r * (gy - xh * jnp.mean(gy * xh, axis=-1, keepdims=True))
        dg_ref[...] += jnp.sum(dh * xh, axis=0, keepdims=True)

    row = lambda w: pl.BlockSpec((tt, w), lambda i: (i, 0))
    return pl.pallas_call(
        body, grid=(SEQ // tt,),
        in_specs=[row(w) for w in widths] + [_full(wt.shape), row(D_MODEL), _full((1, D_MODEL)), row(D_MODEL)],
        out_specs=[row(D_MODEL), _full((1, D_MODEL))],
        out_shape=[jax.ShapeDtypeStruct((SEQ, D_MODEL), F32), jax.ShapeDtypeStruct((1, D_MODEL), F32)],
        compiler_params=_cp(), name=name)(*pieces, wt, x, g, dres)


def _tn_pieces(pieces, h):
    tk, tn = 512, 512
    nk = SEQ // tk
    n = len(pieces)

    def body(*refs):
        p_refs, (h_ref, o_ref, acc_ref) = refs[:n], refs[n:]
        kk = pl.program_id(1)

        @pl.when(kk == 0)
        def _():
            acc_ref[...] = jnp.zeros_like(acc_ref)

        hv = h_ref[...]
        for p_ref, o, w in zip(p_refs, PIECE_OFF, PIECE_W):
            acc_ref[o:o + w, :] += lax.dot_general(p_ref[...], hv, TN, preferred_element_type=F32)

        @pl.when(kk == nk - 1)
        def _():
            o_ref[...] = acc_ref[...].astype(BF16)

    return pl.pallas_call(
        body, grid=(D_MODEL // tn, nk),
        in_specs=[pl.BlockSpec((tk, w), lambda j, kk: (kk, 0)) for w in PIECE_W]
        + [pl.BlockSpec((tk, tn), lambda j, kk: (kk, j))],
        out_specs=pl.BlockSpec((IN_COLS, tn), lambda j, kk: (0, j)),
        out_shape=jax.ShapeDtypeStruct((IN_COLS, D_MODEL), BF16),
        scratch_shapes=[pltpu.VMEM((IN_COLS, tn), F32)], compiler_params=_cp(), name="tn_pieces")(*pieces, h)


def _tn_branches(dys, acts):
    tk = 512
    nk = SEQ // tk

    def body(d0, d1, d2, a0, a1, a2, o_ref, acc_ref):
        kk = pl.program_id(0)

        @pl.when(kk == 0)
        def _():
            acc_ref[...] = jnp.zeros_like(acc_ref)

        for j, (d, a) in enumerate(((d0, a0), (d1, a1), (d2, a2))):
            acc_ref[:, WIDTH * j:WIDTH * (j + 1)] += lax.dot_general(d[...], a[...], TN, preferred_element_type=F32)

        @pl.when(kk == nk - 1)
        def _():
            o_ref[...] = acc_ref[...].astype(BF16)

    row = lambda w: pl.BlockSpec((tk, w), lambda kk: (kk, 0))
    return pl.pallas_call(
        body, grid=(nk,), in_specs=[row(D_MODEL)] * 3 + [row(WIDTH)] * 3,
        out_specs=_full((D_MODEL, 3 * WIDTH)), out_shape=jax.ShapeDtypeStruct((D_MODEL, 3 * WIDTH), BF16),
        scratch_shapes=[pltpu.VMEM((D_MODEL, 3 * WIDTH), F32)], compiler_params=_cp(), name="tn_branches",
    )(*dys, *acts)


def _rope(t, c, a, b):
    return t * c + pltpu.roll(t, 120, axis=1) * a + pltpu.roll(t, 8, axis=1) * b


def _rope_t(d, c, a, b):
    return d * c + pltpu.roll(d * a, 8, axis=1) + pltpu.roll(d * b, 120, axis=1)


def _band_sides(band):
    left = lax.broadcasted_iota(jnp.int32, band.shape, 1) < HEAD_DIM
    h0 = jnp.where(left, band, 0.0)
    h1 = jnp.where(left, 0.0, band)
    r0 = pltpu.roll(h0, HEAD_DIM, axis=1)
    r1 = pltpu.roll(h1, HEAD_DIM, axis=1)
    return ((h0.astype(BF16), r0.astype(BF16)), (r1.astype(BF16), h1.astype(BF16)))


def _attn_mask(i):
    qi = lax.broadcasted_iota(jnp.int32, (2 * BLOCK, 2 * BLOCK), 0) % BLOCK
    kj = lax.broadcasted_iota(jnp.int32, (2 * BLOCK, 2 * BLOCK), 1)
    delta = qi + BLOCK - kj
    return (delta >= 0) & (delta < BLOCK) & ((kj >= BLOCK) | (i > 0))


def _attn_probs(s, ok, sink):
    s = jnp.where(ok, s * (HEAD_DIM ** -0.5), NEG_INF)
    m = jnp.maximum(jnp.max(s, axis=-1, keepdims=True), sink)
    p = jnp.exp(s - m)
    es = jnp.exp(sink - m)
    inv = 1.0 / (jnp.sum(p, axis=-1, keepdims=True) + es)
    return p * inv, es * inv


def _kv_group(qs, ks, vs, kh, sink_ref):
    q2 = jnp.concatenate([qs[2 * kh], qs[2 * kh + 1]], axis=0)
    kst = jnp.concatenate([ks[kh][0], ks[kh][1]], axis=0)
    vst = jnp.concatenate([vs[kh][0], vs[kh][1]], axis=0)
    top = lax.broadcasted_iota(jnp.int32, (2 * BLOCK, 1), 0) < BLOCK
    sinks = [jnp.where(top, sink_ref[0, 4 * kh + h], sink_ref[0, 4 * kh + 2 + h]) for h in range(2)]
    return q2, kst, vst, sinks


def _attn_load(q_ref, kvc_ref, kvp_ref, tc_ref, ta_ref, tb_ref, pc_ref, pa_ref, pb_ref):
    c, a, b = tc_ref[...], ta_ref[...], tb_ref[...]
    kc = _rope(kvc_ref[:, :KV_W], c, a, b)
    kp = _rope(kvp_ref[:, :KV_W], pc_ref[...], pa_ref[...], pb_ref[...])
    kband = jnp.concatenate([kp, kc], axis=0)
    vband = jnp.concatenate([kvp_ref[:, KV_W:], kvc_ref[:, KV_W:]], axis=0)
    qs = [_rope(q_ref[:, 128 * j:128 * (j + 1)], c, a, b).astype(BF16) for j in range(4)]
    return qs, _band_sides(kband), _band_sides(vband), (c, a, b)


def _attn_specs(clamp):
    cur = lambda i: (clamp(i), 0)
    prev = lambda i: (jnp.maximum(clamp(i) - 1, 0), 0)
    return [
        pl.BlockSpec((BLOCK, ATTN_W), cur), pl.BlockSpec((BLOCK, 2 * KV_W), cur),
        pl.BlockSpec((BLOCK, 2 * KV_W), prev),
        pl.BlockSpec((BLOCK, 128), cur), pl.BlockSpec((BLOCK, 128), cur), pl.BlockSpec((BLOCK, 128), cur),
        pl.BlockSpec((BLOCK, 128), prev), pl.BlockSpec((BLOCK, 128), prev), pl.BlockSpec((BLOCK, 128), prev),
        pl.BlockSpec(memory_space=pltpu.SMEM),
    ]


def _attn_fwd(q, kv, tabs, sinks):
    tc, ta, tb = tabs

    def body(q_ref, kvc_ref, kvp_ref, tc_ref, ta_ref, tb_ref, pc_ref, pa_ref, pb_ref, sink_ref, o_ref):
        i = pl.program_id(0)
        qs, ks, vs, _ = _attn_load(q_ref, kvc_ref, kvp_ref, tc_ref, ta_ref, tb_ref, pc_ref, pa_ref, pb_ref)
        ok = _attn_mask(i)
        for kh in range(2):
            q2, kst, vst, sinks = _kv_group(qs, ks, vs, kh, sink_ref)
            s = lax.dot_general(q2, kst, NT, preferred_element_type=F32)
            pn = [_attn_probs(s[:, 2 * BLOCK * h:2 * BLOCK * (h + 1)], ok, sinks[h])[0].astype(BF16) for h in range(2)]
            o2 = jnp.dot(jnp.concatenate(pn, axis=1), vst, preferred_element_type=F32).astype(BF16)
            for r in range(2):
                j = 2 * kh + r
                o_ref[:, 128 * j:128 * (j + 1)] = o2[BLOCK * r:BLOCK * (r + 1)]

    return pl.pallas_call(
        body, grid=(N_BLOCKS,), in_specs=_attn_specs(lambda i: i),
        out_specs=pl.BlockSpec((BLOCK, ATTN_W), lambda i: (i, 0)),
        out_shape=jax.ShapeDtypeStruct((SEQ, ATTN_W), BF16), compiler_params=_cp(), name="attn_fwd",
    )(q, kv, kv, tc, ta, tb, tc, ta, tb, sinks)


def _attn_bwd(q, kv, tabs, sinks, do):
    tc, ta, tb = tabs
    last = N_BLOCKS - 1
    clamp = lambda i: jnp.minimum(i, last)

    def place(full, side, kh):
        left = lax.broadcasted_iota(jnp.int32, full.shape, 1) < HEAD_DIM
        valid = jnp.where(left, full, 0.0) if side == 0 else jnp.where(left, 0.0, full)
        return valid if side == kh else pltpu.roll(valid, HEAD_DIM, axis=1)

    def body(q_ref, kvc_ref, kvp_ref, tc_ref, ta_ref, tb_ref, pc_ref, pa_ref, pb_ref, sink_ref, do_ref,
             dq_ref, dkv_ref, ds_ref, carry_ref):
        i = pl.program_id(0)

        @pl.when(i == 0)
        def _():
            ds_ref[...] = jnp.zeros_like(ds_ref)
            carry_ref[...] = jnp.zeros_like(carry_ref)

        @pl.when(i > last)
        def _():
            dkv_ref[...] = carry_ref[...].astype(BF16)

        @pl.when(i <= last)
        def _():
            qs, ks, vs, (c, a, b) = _attn_load(q_ref, kvc_ref, kvp_ref, tc_ref, ta_ref, tb_ref,
                                               pc_ref, pa_ref, pb_ref)
            ok = _attn_mask(i)
            dk = jnp.zeros((2 * BLOCK, 128), F32)
            dv = jnp.zeros((2 * BLOCK, 128), F32)
            dsink = jnp.zeros((1, 128), F32)
            lane = lax.broadcasted_iota(jnp.int32, (1, 128), 1)
            for kh in range(2):
                q2, kst, vst, sinks = _kv_group(qs, ks, vs, kh, sink_ref)
                do2 = jnp.concatenate([do_ref[:, 128 * (2 * kh + r):128 * (2 * kh + r + 1)] for r in range(2)],
                                      axis=0).astype(BF16)
                s = lax.dot_general(q2, kst, NT, preferred_element_type=F32)
                dp = lax.dot_general(do2, vst, NT, preferred_element_type=F32)
                pns, dss = [], []
                for h in range(2):
                    cols = slice(2 * BLOCK * h, 2 * BLOCK * (h + 1))
                    pn, ps = _attn_probs(s[:, cols], ok, sinks[h])
                    dr = jnp.sum(pn * dp[:, cols], axis=-1, keepdims=True)
                    pns.append(pn.astype(BF16))
                    dss.append((pn * (dp[:, cols] - dr) * (HEAD_DIM ** -0.5)).astype(BF16))
                    for r in range(2):
                        part = -jnp.sum((ps * dr)[BLOCK * r:BLOCK * (r + 1)])
                        dsink += jnp.where(lane == 4 * kh + 2 * r + h, part, 0.0)
                ds2, pn2 = jnp.concatenate(dss, axis=1), jnp.concatenate(pns, axis=1)
                dq2 = jnp.dot(ds2, kst, preferred_element_type=F32)
                dk2 = lax.dot_general(ds2, q2, TN, preferred_element_type=F32)
                dv2 = lax.dot_general(pn2, do2, TN, preferred_element_type=F32)
                for h in range(2):
                    dk += place(dk2[2 * BLOCK * h:2 * BLOCK * (h + 1)], h, kh)
                    dv += place(dv2[2 * BLOCK * h:2 * BLOCK * (h + 1)], h, kh)
                for r in range(2):
                    j = 2 * kh + r
                    dq_ref[:, 128 * j:128 * (j + 1)] = _rope_t(dq2[BLOCK * r:BLOCK * (r + 1)], c, a, b).astype(BF16)
            ds_ref[...] += dsink
            dk_prev = _rope_t(dk[:BLOCK], pc_ref[...], pa_ref[...], pb_ref[...])
            dk_cur = _rope_t(dk[BLOCK:], c, a, b)
            prev = jnp.concatenate([dk_prev, dv[:BLOCK]], axis=1)
            dkv_ref[...] = (carry_ref[...] + prev).astype(BF16)
            carry_ref[...] = jnp.concatenate([dk_cur, dv[BLOCK:]], axis=1)

    return pl.pallas_call(
        body, grid=(N_BLOCKS + 1,),
        in_specs=_attn_specs(clamp) + [pl.BlockSpec((BLOCK, ATTN_W), lambda i: (clamp(i), 0))],
        out_specs=[pl.BlockSpec((BLOCK, ATTN_W), lambda i: (clamp(i), 0)),
                   pl.BlockSpec((BLOCK, 2 * KV_W), lambda i: (jnp.maximum(i - 1, 0), 0)),
                   pl.BlockSpec((1, 128), lambda i: (0, 0))],
        out_shape=[jax.ShapeDtypeStruct((SEQ, ATTN_W), BF16), jax.ShapeDtypeStruct((SEQ, 2 * KV_W), BF16),
                   jax.ShapeDtypeStruct((1, 128), F32)],
        scratch_shapes=[pltpu.VMEM((BLOCK, 2 * KV_W), F32)], compiler_params=_cp(), name="attn_bwd",
    )(q, kv, kv, tc, ta, tb, tc, ta, tb, sinks, do)


def _shift_down(z, k):
    row = lax.broadcasted_iota(jnp.int32, z.shape, 0)
    return jnp.where(row < k, 0.0, pltpu.roll(z, k, axis=0))


def _shift_up(z, k):
    n = z.shape[0]
    row = lax.broadcasted_iota(jnp.int32, z.shape, 0)
    return jnp.where(row >= n - k, 0.0, pltpu.roll(z, n - k, axis=0))


def _conv_specs():
    nb = WIDTH // 128
    return [pl.BlockSpec((SEQ, 128), lambda j: (0, j)), pl.BlockSpec((SEQ, 128), lambda j: (0, nb + j)),
            pl.BlockSpec((SEQ, 128), lambda j: (0, 2 * nb + j)), pl.BlockSpec((None, 8, 128), lambda j: (0, 0, j))]


def _conv_fwd(cbx, cw, layer):
    def body(cb_ref, cc_ref, cx_ref, w_ref, o_ref):
        z = cc_ref[...] * cx_ref[...]
        s = w_ref[0:1, :] * _shift_down(z, 2) + w_ref[1:2, :] * _shift_down(z, 1) + w_ref[2:3, :] * z
        o_ref[...] = (cb_ref[...] * s).astype(BF16)

    specs = _conv_specs()
    specs[3] = pl.BlockSpec((None, 8, 128), lambda j: (layer, 0, j))
    return pl.pallas_call(
        body, grid=(WIDTH // 128,), in_specs=specs,
        out_specs=pl.BlockSpec((SEQ, 128), lambda j: (0, j)),
        out_shape=jax.ShapeDtypeStruct((SEQ, WIDTH), BF16), compiler_params=_cp(), name="conv_fwd",
    )(cbx, cbx, cbx, cw)


def _conv_bwd(cbx, cw, layer, dout, tie):
    def body(cb_ref, cc_ref, cx_ref, w_ref, do_ref, tie_ref, dcb_ref, dcc_ref, dcx_ref, dw_ref):
        cc, cx = cc_ref[...], cx_ref[...]
        z = cc * cx
        z1, z2 = _shift_down(z, 1), _shift_down(z, 2)
        w0, w1, w2 = w_ref[0:1, :], w_ref[1:2, :], w_ref[2:3, :]
        dout = do_ref[...]
        ds = dout * cb_ref[...]
        dcb_ref[...] = (dout * (w0 * z2 + w1 * z1 + w2 * z)).astype(BF16)
        dz = w2 * ds + w1 * _shift_up(ds, 1) + w0 * _shift_up(ds, 2)
        dcc_ref[...] = (dz * cx).astype(BF16)
        dcx_ref[...] = (dz * cc).astype(BF16)
        rows = [jnp.sum(ds * zz, axis=0, keepdims=True) for zz in (z2, z1, z)]
        dw_ref[...] = jnp.concatenate(rows + [jnp.zeros((5, 128), F32)], axis=0)

    col = lambda j: (0, j)
    specs = _conv_specs()
    specs[3] = pl.BlockSpec((None, 8, 128), lambda j: (layer, 0, j))
    return pl.pallas_call(
        body, grid=(WIDTH // 128,), in_specs=specs + [pl.BlockSpec((SEQ, 128), col), ANY],
        out_specs=[pl.BlockSpec((SEQ, 128), col), pl.BlockSpec((SEQ, 128), col), pl.BlockSpec((SEQ, 128), col),
                   pl.BlockSpec((8, 128), col)],
        out_shape=[jax.ShapeDtypeStruct((SEQ, WIDTH), BF16)] * 3 + [jax.ShapeDtypeStruct((8, WIDTH), F32)],
        compiler_params=_cp(), name="conv_bwd",
    )(cbx, cbx, cbx, cw, dout, tie)


def _ssm_prep_math(a_re, a_im, log_dt, bt_re, bt_im):
    dt = jnp.exp(log_dt)
    er = jnp.exp(a_re * dt)
    lr = er * jnp.cos(a_im * dt)
    li = er * jnp.sin(a_im * dt)
    n2 = a_re * a_re + a_im * a_im
    cr = ((lr - 1.0) * a_re + li * a_im) / n2
    ci = (li * a_re - (lr - 1.0) * a_im) / n2
    cr3, ci3 = cr[:, None, :], ci[:, None, :]
    return lr, li, cr3 * bt_re - ci3 * bt_im, cr3 * bt_im + ci3 * bt_re


_GS = (SSM_GROUPS, SSM_STATE)
_GHS = (SSM_GROUPS, SSM_GROUP, SSM_STATE)


def _layered(shape):
    return pl.BlockSpec((None,) + shape, lambda l: (l,) + (0,) * len(shape))


def _ssm_prep(a_re, a_im, log_dt, bt_re, bt_im):
    def body(ar, ai, ld, br, bi, o0, o1, o2, o3):
        outs = _ssm_prep_math(ar[...], ai[...], ld[...], br[...], bi[...])
        for o, v in zip((o0, o1, o2, o3), outs):
            o[...] = v

    shapes = [_GS, _GS, _GHS, _GHS]
    return pl.pallas_call(
        body, grid=(DEPTH,), in_specs=[_layered(s) for s in (_GS, _GS, (SSM_GROUPS, 1), _GHS, _GHS)],
        out_specs=[_layered(s) for s in shapes],
        out_shape=[jax.ShapeDtypeStruct((DEPTH,) + s, F32) for s in shapes],
        name="ssm_prep")(a_re, a_im, log_dt, bt_re, bt_im)


def _ssm_prep_bwd(a_re, a_im, log_dt, bt_re, bt_im, cots):
    def body(ar, ai, ld, br, bi, c0, c1, c2, c3, o0, o1, o2, o3, o4):
        _, vjp = jax.vjp(_ssm_prep_math, ar[...], ai[...], ld[...], br[...], bi[...])
        for o, v in zip((o0, o1, o2, o3, o4), vjp((c0[...], c1[...], c2[...], c3[...]))):
            o[...] = v

    ins = (_GS, _GS, (SSM_GROUPS, 1), _GHS, _GHS)
    return pl.pallas_call(
        body, grid=(DEPTH,), in_specs=[_layered(s) for s in ins + (_GS, _GS, _GHS, _GHS)],
        out_specs=[_layered(s) for s in ins],
        out_shape=[jax.ShapeDtypeStruct((DEPTH,) + s, F32) for s in ins],
        name="ssm_prep_bwd")(a_re, a_im, log_dt, bt_re, bt_im, *cots)


LANES_G = 512
N_LANE_GROUPS = SSM_GROUPS * SSM_STATE // LANES_G


def _scan_order(a):
    return a.reshape(N_CHUNKS, CHUNK, -1).transpose(1, 0, 2).reshape(a.shape)


def _time_order(a):
    return a.reshape(CHUNK, N_CHUNKS, -1).transpose(1, 0, 2).reshape(a.shape)


def _scan_in_place(xr_ref, xi_ref, ar, ai, reverse):
    shape = (N_CHUNKS, xr_ref.shape[1])
    ar, ai = jnp.broadcast_to(ar, shape), jnp.broadcast_to(ai, shape)

    def rows(tau):
        t = (CHUNK - 1 - tau) if reverse else tau
        return pl.ds(pl.multiple_of(t * N_CHUNKS, N_CHUNKS), N_CHUNKS)

    def step(tau, carry):
        sr, si = carry
        return ar * sr - ai * si + xr_ref[rows(tau), :], ar * si + ai * sr + xi_ref[rows(tau), :]

    zero = jnp.zeros(shape, F32)
    er, ei = lax.fori_loop(0, CHUNK, step, (zero, zero), unroll=8)
    qr, qi = ar, ai
    for _ in range(8):
        qr, qi = qr * qr - qi * qi, 2.0 * qr * qi
    shift = _shift_up if reverse else _shift_down
    for k in (1, 2, 4):
        sr, si = shift(er, k), shift(ei, k)
        er, ei = er + qr * sr - qi * si, ei + qr * si + qi * sr
        qr, qi = qr * qr - qi * qi, 2.0 * qr * qi
    start = (shift(er, 1), shift(ei, 1))

    def write(tau, carry):
        sr, si = step(tau, carry)
        xr_ref[rows(tau), :] = sr
        xi_ref[rows(tau), :] = si
        return sr, si

    return write, start


def _ssm_specs(layer):
    col = lambda w: pl.BlockSpec((SEQ, w), lambda g: (0, g))
    diag = pl.BlockSpec((None, None, 128, LANES_G), lambda g: (layer, g, 0, 0))
    vec = pl.BlockSpec((None, 1, LANES_G), lambda g: (layer, 0, g))
    return col, diag, vec


def _ssm_fwd(u16, u, mats, layer, d):
    def body(u16_ref, u_ref, d_ref, br_ref, bi_ref, cr_ref, ci_ref, ar_ref, ai_ref, xr_ref, xi_ref, y_ref):
        uv = u16_ref[...]
        xr_ref[...] = jnp.dot(uv, br_ref[...], preferred_element_type=F32)
        xi_ref[...] = jnp.dot(uv, bi_ref[...], preferred_element_type=F32)
        write, start = _scan_in_place(xr_ref, xi_ref, ar_ref[...], ai_ref[...], False)
        lax.fori_loop(0, CHUNK, write, start, unroll=8)
        y = lax.dot_general(xr_ref[...].astype(BF16), cr_ref[...], NT, preferred_element_type=F32)
        y += lax.dot_general(xi_ref[...].astype(BF16), ci_ref[...], NT, preferred_element_type=F32)
        y_ref[...] = y + d_ref[...] * u_ref[...]

    col, diag, vec = _ssm_specs(layer)
    return pl.pallas_call(
        body, grid=(N_LANE_GROUPS,),
        in_specs=[col(128), col(128), pl.BlockSpec((None, 1, 128), lambda g: (layer, 0, g)),
                  diag, diag, diag, diag, vec, vec],
        out_specs=[col(LANES_G), col(LANES_G), col(128)],
        out_shape=[jax.ShapeDtypeStruct((SEQ, SSM_GROUPS * SSM_STATE), F32)] * 2
        + [jax.ShapeDtypeStruct((SEQ, WIDTH), F32)],
        compiler_params=_cp(), name="ssm_fwd",
    )(u16, u, d, mats["b_re"], mats["b_im"], mats["c_re"], mats["c_im_neg"], mats["a_re"], mats["a_im"])


def _ssm_bwd(dy16, x_re, x_im, u16, mats, layer, d):
    def body(dy_ref, u_ref, d_ref, xr_ref, xi_ref, br_ref, bi_ref, cr_ref, ci_ref, ar_ref, ai_ref,
             du_ref, dar_ref, dai_ref, dbr_ref, dbi_ref, dcr_ref, dci_ref, lr_ref, li_ref):
        dy = dy_ref[...]
        lr_ref[...] = jnp.dot(dy, cr_ref[...], preferred_element_type=F32)
        li_ref[...] = jnp.dot(dy, ci_ref[...], preferred_element_type=F32)
        write, start = _scan_in_place(lr_ref, li_ref, ar_ref[...], -ai_ref[...], True)

        def rows(t):
            return pl.ds(pl.multiple_of(t * N_CHUNKS, N_CHUNKS), N_CHUNKS)

        def grad(acc, lam, xpr, xpi):
            return acc[0] + xpr * lam[0] + xpi * lam[1], acc[1] + xpr * lam[1] - xpi * lam[0]

        def down(tau, carry):
            lam = write(tau, carry[0])
            t = CHUNK - 2 - tau
            return lam, grad(carry[1], lam, xr_ref[rows(t), :], xi_ref[rows(t), :])

        zero = jnp.zeros((N_CHUNKS, LANES_G), F32)
        lam, acc = lax.fori_loop(0, CHUNK - 1, down, (start, (zero, zero)), unroll=5)
        lam = write(CHUNK - 1, lam)
        last = rows(CHUNK - 1)
        acc = grad(acc, lam, _shift_down(xr_ref[last, :], 1), _shift_down(xi_ref[last, :], 1))
        dar_ref[...] = jnp.sum(acc[0], axis=0, keepdims=True)
        dai_ref[...] = jnp.sum(acc[1], axis=0, keepdims=True)

        l_re, l_im = lr_ref[...].astype(BF16), li_ref[...].astype(BF16)
        du = lax.dot_general(l_re, br_ref[...], NT, preferred_element_type=F32)
        du += lax.dot_general(l_im, bi_ref[...], NT, preferred_element_type=F32)
        du_ref[...] = (du + dy.astype(F32) * d_ref[...]).astype(BF16)
        uv = u_ref[...]
        dbr_ref[...] = lax.dot_general(uv, l_re, TN, preferred_element_type=F32)
        dbi_ref[...] = lax.dot_general(uv, l_im, TN, preferred_element_type=F32)
        dcr_ref[...] = lax.dot_general(dy, xr_ref[...].astype(BF16), TN, preferred_element_type=F32)
        dci_ref[...] = lax.dot_general(dy, xi_ref[...].astype(BF16), TN, preferred_element_type=F32)

    col, diag, vec = _ssm_specs(layer)
    out_vec = pl.BlockSpec((1, LANES_G), lambda g: (0, g))
    out_blk = pl.BlockSpec((None, 128, LANES_G), lambda g: (g, 0, 0))
    sds = jax.ShapeDtypeStruct
    return pl.pallas_call(
        body, grid=(N_LANE_GROUPS,),
        in_specs=[col(128), col(128), pl.BlockSpec((None, 1, 128), lambda g: (layer, 0, g)),
                  col(LANES_G), col(LANES_G), diag, diag, diag, diag, vec, vec],
        out_specs=[col(128), out_vec, out_vec, out_blk, out_blk, out_blk, out_blk],
        out_shape=[sds((SEQ, WIDTH), BF16)] + [sds((1, SSM_GROUPS * SSM_STATE), F32)] * 2
        + [sds((N_LANE_GROUPS, 128, LANES_G), F32)] * 4,
        scratch_shapes=[pltpu.VMEM((SEQ, LANES_G), F32)] * 2, compiler_params=_cp(), name="ssm_bwd",
    )(dy16, u16, d, x_re, x_im, mats["b_re"], mats["b_im"], mats["c_re"], mats["c_im_neg"],
      mats["a_re"], mats["a_im"])


_GELU_C = math.sqrt(2.0 / math.pi)


def _gelu(y):
    return 0.5 * y * (1.0 + jnp.tanh(_GELU_C * (y + 0.044715 * (y * y * y))))


def _glu_fwd(y, wglu):
    tt = 512

    def body(y_ref, w_ref, z_ref):
        ys = _gelu(y_ref[...])
        a = jnp.dot(ys.astype(BF16), w_ref[...], preferred_element_type=F32)
        z_ref[...] = (ys * jax.nn.sigmoid(a)).astype(BF16)

    blk = pl.BlockSpec((tt, WIDTH), lambda i: (i, 0))
    return pl.pallas_call(body, grid=(SEQ // tt,), in_specs=[blk, _full((WIDTH, WIDTH))], out_specs=blk,
                          out_shape=jax.ShapeDtypeStruct((SEQ, WIDTH), BF16), compiler_params=_cp(),
                          name="glu_fwd")(y, wglu)


def _glu_bwd(y, wglu, dz, u):
    tt = 512

    def body(y_ref, w_ref, dz_ref, u_ref, dy_ref, ys_ref, da_ref, dd_ref):
        @pl.when(pl.program_id(0) == 0)
        def _():
            dd_ref[...] = jnp.zeros_like(dd_ref)

        yv = y_ref[...]
        t = jnp.tanh(_GELU_C * (yv + 0.044715 * (yv * yv * yv)))
        ys = 0.5 * yv * (1.0 + t)
        ysb = ys.astype(BF16)
        sg = jax.nn.sigmoid(jnp.dot(ysb, w_ref[...], preferred_element_type=F32))
        dz = dz_ref[...].astype(F32)
        da = (dz * ys * sg * (1.0 - sg)).astype(BF16)
        dys = dz * sg + lax.dot_general(da, w_ref[...], NT, preferred_element_type=F32)
        dy = dys * (0.5 * (1.0 + t) + 0.5 * yv * (1.0 - t * t) * _GELU_C * (1.0 + 3 * 0.044715 * (yv * yv)))
        dy_ref[...] = dy.astype(BF16)
        ys_ref[...] = ysb
        da_ref[...] = da
        dd_ref[...] += jnp.sum(dy * u_ref[...], axis=0, keepdims=True)

    blk = pl.BlockSpec((tt, WIDTH), lambda i: (i, 0))
    return pl.pallas_call(
        body, grid=(SEQ // tt,), in_specs=[blk, _full((WIDTH, WIDTH)), blk, blk],
        out_specs=[blk, blk, blk, _full((1, WIDTH))],
        out_shape=[jax.ShapeDtypeStruct((SEQ, WIDTH), BF16)] * 3 + [jax.ShapeDtypeStruct((1, WIDTH), F32)],
        compiler_params=_cp(), name="glu_bwd")(y, wglu, dz, u)


def _mix_specs(tt, layer):
    row = lambda w: pl.BlockSpec((tt, w), lambda i: (i, 0))
    gate = lambda j: pl.BlockSpec((tt, D_MODEL), lambda i: (i, j))
    wo = lambda j: pl.BlockSpec((D_MODEL, WIDTH), lambda i: (0, j))
    return [row(D_MODEL), row(WIDTH), row(WIDTH), row(WIDTH), gate(0), gate(1), gate(2),
            pl.BlockSpec((None, 1, GATE_W), lambda i: (layer, 0, 0)), wo(0), wo(1), wo(2),
            _full((D_MODEL, D_MODEL))]


def _mix_branches(o_ref, c_ref, z_ref, g_refs, b_ref, wa_ref, wc_ref, ws_ref):
    ys = [lax.dot_general(r[...], w[...], NT, preferred_element_type=F32)
          for r, w in ((o_ref, wa_ref), (c_ref, wc_ref), (z_ref, ws_ref))]
    gates = [jax.nn.sigmoid(g_refs[j][...] + b_ref[:, D_MODEL * j:D_MODEL * (j + 1)]) for j in range(3)]
    return ys, gates


def _mix_fwd(x, o, cv, z, glog, b_gate, layer, wbt, wmix, tie):
    tt = 256

    def body(x_ref, o_ref, c_ref, z_ref, g0, g1, g2, b_ref, wa_ref, wc_ref, ws_ref, wm_ref, tie_ref, x1_ref):
        ys, gates = _mix_branches(o_ref, c_ref, z_ref, (g0, g1, g2), b_ref, wa_ref, wc_ref, ws_ref)
        merged = gates[0] * ys[0] + gates[1] * ys[1] + gates[2] * ys[2]
        x1_ref[...] = x_ref[...] + jnp.dot(merged.astype(BF16), wm_ref[...], preferred_element_type=F32)

    return pl.pallas_call(
        body, grid=(SEQ // tt,), in_specs=_mix_specs(tt, layer) + [ANY],
        out_specs=pl.BlockSpec((tt, D_MODEL), lambda i: (i, 0)),
        out_shape=jax.ShapeDtypeStruct((SEQ, D_MODEL), F32), compiler_params=_cp(), name="mix_fwd",
    )(x, o, cv, z, glog, glog, glog, b_gate, wbt, wbt, wbt, wmix, tie)


def _mix_bwd(dx1, o, cv, z, glog, b_gate, layer, wbt, wmix, tie):
    tt = 256

    def body(dx_ref, o_ref, c_ref, z_ref, g0, g1, g2, b_ref, wa_ref, wc_ref, ws_ref, wm_ref, tie_ref,
             mg_ref, dya_ref, dyc_ref, dys_ref, do_ref, dc_ref, dz_ref, dgl_ref, db_ref):
        @pl.when(pl.program_id(0) == 0)
        def _():
            db_ref[...] = jnp.zeros_like(db_ref)

        ys, gates = _mix_branches(o_ref, c_ref, z_ref, (g0, g1, g2), b_ref, wa_ref, wc_ref, ws_ref)
        mg_ref[...] = (gates[0] * ys[0] + gates[1] * ys[1] + gates[2] * ys[2]).astype(BF16)
        dm = lax.dot_general(dx_ref[...].astype(BF16), wm_ref[...], NT, preferred_element_type=F32)
        for j, (dy_ref, w_ref, d_ref) in enumerate(((dya_ref, wa_ref, do_ref), (dyc_ref, wc_ref, dc_ref),
                                                    (dys_ref, ws_ref, dz_ref))):
            dy = (dm * gates[j]).astype(BF16)
            dy_ref[...] = dy
            d_ref[...] = jnp.dot(dy, w_ref[...], preferred_element_type=F32)
            dgl = dm * ys[j] * gates[j] * (1.0 - gates[j])
            dgl_ref[:, D_MODEL * j:D_MODEL * (j + 1)] = dgl.astype(BF16)
            db_ref[:, D_MODEL * j:D_MODEL * (j + 1)] += jnp.sum(dgl, axis=0, keepdims=True)

    row = lambda w: pl.BlockSpec((tt, w), lambda i: (i, 0))
    sds = jax.ShapeDtypeStruct
    return pl.pallas_call(
        body, grid=(SEQ // tt,), in_specs=_mix_specs(tt, layer) + [ANY],
        out_specs=[row(D_MODEL)] * 4 + [row(WIDTH)] * 3 + [row(GATE_W), _full((1, GATE_W))],
        out_shape=[sds((SEQ, D_MODEL), BF16)] * 4 + [sds((SEQ, WIDTH), F32)] * 3
        + [sds((SEQ, GATE_W), BF16), sds((1, GATE_W), F32)],
        compiler_params=_cp(), name="mix_bwd",
    )(dx1, o, cv, z, glog, glog, glog, b_gate, wbt, wbt, wbt, wmix, tie)


def _ffn_out_fwd(x1, gu, wout, tie):
    tt = 256

    def body(x_ref, gt_ref, up_ref, w_ref, tie_ref, o_ref):
        gt = gt_ref[...]
        act = (gt * jax.nn.sigmoid(gt) * up_ref[...]).astype(BF16)
        o_ref[...] = x_ref[...] + jnp.dot(act, w_ref[...], preferred_element_type=F32)

    return pl.pallas_call(
        body, grid=(SEQ // tt,),
        in_specs=[pl.BlockSpec((tt, D_MODEL), lambda i: (i, 0)), pl.BlockSpec((tt, FFN_H), lambda i: (i, 0)),
                  pl.BlockSpec((tt, FFN_H), lambda i: (i, 1)), _full((FFN_H, D_MODEL)), ANY],
        out_specs=pl.BlockSpec((tt, D_MODEL), lambda i: (i, 0)),
        out_shape=jax.ShapeDtypeStruct((SEQ, D_MODEL), F32), compiler_params=_cp(), name="ffn_out_fwd",
    )(x1, gu, gu, wout, tie)


def _ffn_out_bwd(dx2, gu, wout, tie):
    tt = 256

    def body(dx_ref, gt_ref, up_ref, w_ref, tie_ref, dgu_ref, act_ref):
        gt, up = gt_ref[...], up_ref[...]
        sg = jax.nn.sigmoid(gt)
        silu = gt * sg
        act_ref[...] = (silu * up).astype(BF16)
        dact = lax.dot_general(dx_ref[...].astype(BF16), w_ref[...], NT, preferred_element_type=F32)
        dgu_ref[:, :FFN_H] = (dact * up * (sg * (1.0 + gt * (1.0 - sg)))).astype(BF16)
        dgu_ref[:, FFN_H:] = (dact * silu).astype(BF16)

    return pl.pallas_call(
        body, grid=(SEQ // tt,),
        in_specs=[pl.BlockSpec((tt, D_MODEL), lambda i: (i, 0)), pl.BlockSpec((tt, FFN_H), lambda i: (i, 0)),
                  pl.BlockSpec((tt, FFN_H), lambda i: (i, 1)), _full((FFN_H, D_MODEL)), ANY],
        out_specs=[pl.BlockSpec((tt, 2 * FFN_H), lambda i: (i, 0)), pl.BlockSpec((tt, FFN_H), lambda i: (i, 0))],
        out_shape=[jax.ShapeDtypeStruct((SEQ, 2 * FFN_H), BF16), jax.ShapeDtypeStruct((SEQ, FFN_H), BF16)],
        compiler_params=_cp(), name="ffn_out_bwd",
    )(dx2, gu, gu, wout, tie)


def _loss_head(x, g, target):
    tt = 256

    def body(x_ref, g_ref, t_ref, loss_ref, dx_ref, dg_ref):
        @pl.when(pl.program_id(0) == 0)
        def _():
            loss_ref[...] = jnp.zeros_like(loss_ref)
            dg_ref[...] = jnp.zeros_like(dg_ref)

        xv = x_ref[...]
        r = lax.rsqrt(jnp.mean(xv * xv, axis=-1, keepdims=True) + NORM_EPS)
        xh = xv * r
        err = xh * g_ref[...] - t_ref[...]
        loss_ref[...] += 0.5 * jnp.sum(jnp.mean(err * err, axis=-1, keepdims=True))
        dy = err * (1.0 / D_MODEL)
        gy = dy * g_ref[...]
        dx_ref[...] = r * (gy - xh * jnp.mean(gy * xh, axis=-1, keepdims=True))
        dg_ref[...] += jnp.sum(dy * xh, axis=0, keepdims=True)

    row = pl.BlockSpec((tt, D_MODEL), lambda i: (i, 0))
    return pl.pallas_call(
        body, grid=(SEQ // tt,), in_specs=[row, _full((1, D_MODEL)), row],
        out_specs=[_full((1, 128)), row, _full((1, D_MODEL))],
        out_shape=[jax.ShapeDtypeStruct((1, 128), F32), jax.ShapeDtypeStruct((SEQ, D_MODEL), F32),
                   jax.ShapeDtypeStruct((1, D_MODEL), F32)],
        compiler_params=_cp(), name="loss_head")(x, g, target)


def _adamw(parts, w, m, v, tr, name, groups=None, fill=None, tie=None):
    n_groups, rows, cols = w.shape
    n_parts = parts.shape[1]
    lo, hi = groups if groups is not None else (0, n_groups)

    def body(p_ref, w_ref, m_ref, v_ref, *rest):
        g_ref, d_ref, nm_ref, nv_ref = rest[-4:]
        g = p_ref[0].astype(F32)
        for k in range(1, n_parts):
            g = g + p_ref[k].astype(F32)
        nm = B1 * m_ref[...] + (1.0 - B1) * g
        nv = B2 * v_ref[...] + (1.0 - B2) * (g * g)
        m_hat = nm / (1.0 - B1 ** STEP)
        v_hat = nv / (1.0 - B2 ** STEP)
        g_ref[...] = g
        d_ref[...] = -LR * (m_hat / (jnp.sqrt(v_hat) + ADAM_EPS) + WD * w_ref[...])
        nm_ref[...] = nm
        nv_ref[...] = nv

    blk = pl.BlockSpec((None, tr, cols), lambda l, i: (l + lo, i, 0))
    p_lo = lo if parts.shape[0] == n_groups else 0
    extra = ([] if fill is None else list(fill)) + ([] if tie is None else [tie])
    return pl.pallas_call(
        body, grid=(hi - lo, rows // tr),
        in_specs=[pl.BlockSpec((None, n_parts, tr, cols), lambda l, i: (l + p_lo, 0, i, 0)), blk, blk, blk]
        + [ANY] * len(extra),
        out_specs=[blk] * 4, out_shape=[jax.ShapeDtypeStruct((n_groups, rows, cols), F32)] * 4,
        input_output_aliases={} if fill is None else {4 + j: j for j in range(4)},
        compiler_params=_cp(), name=name)(parts, w, m, v, *extra)


def _split_start(name, arrays, n_sems, plan, after=None):
    n = len(arrays)
    order = [] if after is None else [after]
    n_in = n + len(order)

    def body(*refs):
        ins, send_sems, recv_sems, token = refs[:n], refs[n_in], refs[n_in + 1], refs[-1]
        for src, dst, k, to in plan(ins)[0]:
            pltpu.make_async_remote_copy(src_ref=src, dst_ref=dst, send_sem=send_sems.at[k], recv_sem=recv_sems.at[k],
                                         device_id=to, device_id_type=MESH_ID).start()
        token[...] = jnp.zeros_like(token)

    outs = pl.pallas_call(
        body, name=name,
        out_shape=(pltpu.SemaphoreType.DMA((n_sems,)), pltpu.SemaphoreType.DMA((n_sems,)),
                   *[pltpu.HBM(a.shape, a.dtype) for a in arrays], jax.ShapeDtypeStruct((8, 128), F32)),
        in_specs=[HBM] * n + [ANY] * len(order),
        out_specs=(SEM, SEM, *[HBM] * n, pl.BlockSpec(memory_space=pltpu.VMEM)),
        input_output_aliases={i: 2 + i for i in range(n)},
        compiler_params=pltpu.CompilerParams(has_side_effects=EFFECT),
    )(*[pltpu.with_memory_space_constraint(a, pltpu.HBM) for a in arrays], *order)
    return outs[0], outs[1], list(outs[2:2 + n]), outs[-1]


def _split_wait(name, arrays, send_sems, recv_sems, after, plan):
    n = len(arrays)

    def body(*refs):
        ins, s_sems, r_sems = refs[:n], refs[n], refs[n + 1]
        sends, arrivals = plan(ins)
        x, y, c = lax.axis_index("x"), lax.axis_index("y"), lax.axis_index("c")
        for src, dst, k, to in sends:
            pltpu.make_async_remote_copy(src_ref=src, dst_ref=dst, send_sem=s_sems.at[k], recv_sem=r_sems.at[k],
                                         device_id=to, device_id_type=MESH_ID).wait_send()
        for dst, k in arrivals:
            pltpu.make_async_remote_copy(src_ref=dst, dst_ref=dst, send_sem=s_sems.at[k], recv_sem=r_sems.at[k],
                                         device_id=(x, y, c), device_id_type=MESH_ID).wait_recv()

    return pl.pallas_call(
        body, name=name, out_shape=[pltpu.HBM(a.shape, a.dtype) for a in arrays],
        in_specs=[HBM] * n + [SEM, SEM, ANY], out_specs=[HBM] * n,
        input_output_aliases={i: i for i in range(n)},
        compiler_params=pltpu.CompilerParams(has_side_effects=EFFECT),
    )(*arrays, send_sems, recv_sems, after)


def _chips():
    x, y, c = lax.axis_index("x"), lax.axis_index("y"), lax.axis_index("c")
    return x, y, c, [(1 - x, y), (x, 1 - y), (1 - x, 1 - y)]


def _plan_gather_chips(refs):
    x, y, c, chips = _chips()
    me = 4 * x + 2 * y + c
    n = len(refs) // 2
    sends, arrivals = [], []
    for i in range(n):
        src, land = refs[i], refs[n + i]
        sends.append((src, land.at[me], 4 * i, (x, y, 1 - c)))
        arrivals.append((land.at[4 * x + 2 * y + 1 - c], 4 * i))
        for j, (px, py) in enumerate(chips):
            sends.append((src, land.at[me], 4 * i + 1 + j, (px, py, c)))
            arrivals.append((land.at[4 * px + 2 * py + c], 4 * i + 1 + j))
    return sends, arrivals


def _plan_gather_pass(refs):
    x, y, c, chips = _chips()
    sends, arrivals = [], []
    for i in range(len(refs)):
        for j, (px, py) in enumerate(chips):
            slot = refs[i].at[4 * px + 2 * py + c]
            sends.append((slot, slot, 3 * i + j, (x, y, 1 - c)))
            arrivals.append((refs[i].at[4 * px + 2 * py + 1 - c], 3 * i + j))
    return sends, arrivals


def _plan_scatter_pair(refs):
    x, y, c = lax.axis_index("x"), lax.axis_index("y"), lax.axis_index("c")
    n = len(refs) // 2
    sends, arrivals = [], []
    for i in range(n):
        for q in range(4):
            sends.append((refs[i].at[q, 1 - c], refs[n + i].at[q], 4 * i + q, (x, y, 1 - c)))
            arrivals.append((refs[n + i].at[q], 4 * i + q))
    return sends, arrivals


def _plan_scatter_chips(layer):
    def plan(refs):
        x, y, c, chips = _chips()
        n = len(refs) // 2
        sends, arrivals = [], []
        for i in range(n):
            for j, (px, py) in enumerate(chips):
                sends.append((refs[i].at[2 * px + py], refs[n + i].at[layer, 2 * x + y], 3 * i + j, (px, py, c)))
                arrivals.append((refs[n + i].at[layer, 2 * px + py], 3 * i + j))
        return sends, arrivals

    return plan


def _pair_sum(parts4, from_pair, landing, layer, core, tr, name):
    _, _, rows, cols = parts4.shape

    def body(c_ref, p_ref, s_ref, l_ref, sum_ref, land_ref):
        v = (p_ref[...].astype(F32) + s_ref[...].astype(F32)).astype(BF16)
        sum_ref[...] = v
        land_ref[...] = v

    blk = pl.BlockSpec((None, tr, cols), lambda q, i, c_ref: (q, i, 0))
    return pl.pallas_call(
        body,
        grid_spec=pltpu.PrefetchScalarGridSpec(
            num_scalar_prefetch=1, grid=(4, rows // tr),
            in_specs=[pl.BlockSpec((None, None, tr, cols), lambda q, i, c_ref: (q, c_ref[0], i, 0)), blk, ANY],
            out_specs=[blk, pl.BlockSpec((None, None, tr, cols), lambda q, i, c_ref: (layer, q, i, 0))]),
        out_shape=[jax.ShapeDtypeStruct((4, rows, cols), BF16), jax.ShapeDtypeStruct(landing.shape, BF16)],
        input_output_aliases={3: 1}, compiler_params=_cp(), name=name,
    )(core, parts4, from_pair, landing)


def _travel_layout(t):
    tr = lambda a: jnp.swapaxes(a, 1, 2)
    branch = jnp.concatenate([tr(t["w_attn_o"]), tr(t["w_conv_o"]), tr(t["w_ssm_o"])], axis=2)
    return [tr(t["w_in"]), tr(t["w_ffn_in"]), t["w_ffn_out"], t["w_mix_o"], branch, t["w_ssm_glu"]]


def _native_layout(a):
    tr = lambda x: jnp.swapaxes(x, 1, 2)
    b = a[4]
    return {"w_in": tr(a[0]), "w_ffn_in": tr(a[1]), "w_ffn_out": a[2], "w_mix_o": a[3],
            "w_attn_o": tr(b[:, :, :WIDTH]), "w_conv_o": tr(b[:, :, WIDTH:2 * WIDTH]),
            "w_ssm_o": tr(b[:, :, 2 * WIDTH:]), "w_ssm_glu": a[5]}


def _embed(t):
    eye = jnp.eye(8, dtype=t.dtype)
    t = t.reshape(DEPTH, N_LANE_GROUPS, 8, SSM_GROUP, SSM_STATE)
    return (t[:, :, :, :, None, :] * eye[None, None, :, None, :, None]).reshape(DEPTH, N_LANE_GROUPS, 128, LANES_G)


def _diag_blocks(t):
    t = t.reshape(DEPTH, N_LANE_GROUPS, 8, SSM_GROUP, 8, SSM_STATE)
    return jnp.einsum("lgahap->lgahp", t).reshape(DEPTH, SSM_GROUPS, SSM_GROUP, SSM_STATE)


def _rope_tabs():
    pos = jnp.arange(SEQ, dtype=F32)
    inv_freq = ROPE_THETA ** (-jnp.arange(0, ROT_DIM, 2, dtype=F32) / ROT_DIM)
    ang = pos[:, None] * inv_freq[None, :]
    cos, sin = jnp.cos(ang), jnp.sin(ang)
    one, zero = jnp.ones((SEQ, HEAD_DIM - ROT_DIM), F32), jnp.zeros((SEQ, HEAD_DIM - ROT_DIM), F32)
    z8 = jnp.zeros((SEQ, 8), F32)
    head = lambda *p: jnp.tile(jnp.concatenate(p, axis=1), (1, 2))
    return head(cos, cos, one), head(-sin, z8, zero), head(z8, sin, zero)


def _ssm_mats(sp):
    lr, li, bbr, bbi = _ssm_prep(sp["a_re"], sp["a_im"], sp["log_dt"], sp["bt_re"], sp["bt_im"])
    lanes = SSM_GROUPS * SSM_STATE
    return {
        "a_re": lr.reshape(DEPTH, 1, lanes), "a_im": li.reshape(DEPTH, 1, lanes),
        "b_re": _embed(bbr).astype(BF16), "b_im": _embed(bbi).astype(BF16),
        "c_re": _embed(sp["c_re"]).astype(BF16), "c_im_neg": _embed(-sp["c_im"]).astype(BF16),
    }


def _layer_fwd(x, i, w, rp, mats, tabs, tie, hooks):
    q, kv, cbx, u, u16, glog, h = _rms_mm_in(x, rp["norm_mix"][i], w["win_t"], tie)
    o = _attn_fwd(q, kv, tabs, rp["attn_sinks"][i])
    cv = _conv_fwd(cbx, rp["conv_w"], i)
    u16, u = _scan_order(u16), _scan_order(u)
    x_re, x_im, y = _ssm_fwd(u16, u, mats, i, rp["ssm_d"])
    z = _time_order(_glu_fwd(y, w["wglu"]))
    x1 = _mix_fwd(x, o, cv, z, glog, rp["b_gate"], i, w["branch_t"], w["wmix"], hooks["early"](z))
    hooks["pre_ffn"](x1)
    gu, h2 = _rms_mm_ffn(x1, rp["norm_ffn"][i], w["wffn_t"])
    x2 = _ffn_out_fwd(x1, gu, w["wout"], hooks["mid"](h2))
    kept = dict(x=x, q=q, kv=kv, cbx=cbx, u=u, u16=u16, glog=glog, h=h, o=o, cv=cv, z=z, y=y,
                x_re=x_re, x_im=x_im, x1=x1, gu=gu, h2=h2)
    return x2, kept


def _layer_bwd(dx2, k, i, w, rp, mats, tabs, tie, hooks):
    tn = dict(ta=True, out_dtype=BF16)
    dgu, act = _ffn_out_bwd(dx2, k["gu"], w["wout"], tie)
    g_wout = _mm(act, dx2, tm=FFN_H // 2, tn=1024, tk=512, name="mm_tn_ffn_out", **tn)
    g_wffn_t = _mm(dgu, k["h2"], tm=FFN_H // 2, tn=1024, tk=512, name="mm_tn_ffn_in", **tn)
    dx1, d_norm_ffn = _mm_rmsbwd([dgu], w["wffn_t"], k["x1"], rp["norm_ffn"][i], dx2, "mm_rmsbwd_ffn")

    mg, dya, dyc, dys, do, dcv, dz, dgl, db_gate = _mix_bwd(
        dx1, k["o"], k["cv"], k["z"], k["glog"], rp["b_gate"], i, w["branch_t"], w["wmix"],
        hooks["mid"]((g_wffn_t, g_wout, d_norm_ffn)))
    g_wmix = _mm(mg, dx1, tm=1024, tn=1024, tk=512, name="mm_tn_mix", **tn)
    g_branch_t = _tn_branches((dya, dyc, dys), (k["o"], k["cv"], k["z"]))

    dy16, ys16, da16, dd = _glu_bwd(k["y"], w["wglu"], _scan_order(dz), k["u"])
    g_wglu = _mm(ys16, da16, tm=512, tn=512, tk=512, name="mm_tn_glu", **tn)
    du, da_re, da_im, db_re, db_im, dc_re, dc_im = _ssm_bwd(dy16, k["x_re"], k["x_im"], k["u16"], mats, i,
                                                             rp["ssm_d"])
    du = _time_order(du)

    dcb, dcc, dcx, d_conv_w = _conv_bwd(k["cbx"], rp["conv_w"], i, dcv, hooks["late"](du))
    dq, dkv, d_sinks = _attn_bwd(k["q"], k["kv"], tabs, rp["attn_sinks"][i], do)

    pieces = [dq, dkv, dcb, dcc, dcx, du, dgl]
    g_win_t = _tn_pieces(pieces, k["h"])
    dx, d_norm_mix = _mm_rmsbwd(pieces, w["win_t"], k["x"], rp["norm_mix"][i], dx1, "mm_rmsbwd_in")

    grads = [g_win_t, g_wffn_t, g_wout, g_wmix, g_branch_t, g_wglu]
    small = dict(norm_mix=d_norm_mix, b_gate=db_gate, attn_sinks=d_sinks, ssm_d=dd, norm_ffn=d_norm_ffn,
                 conv_w=d_conv_w, da_re=da_re, da_im=da_im, db_re=db_re, db_im=db_im, dc_re=dc_re, dc_im=dc_im)
    return dx, grads, small


def _replicated_grads(sg, sp):
    stack = lambda name: jnp.stack([sg[i][name] for i in range(DEPTH)])
    cots = (stack("da_re").reshape(DEPTH, *_GS), stack("da_im").reshape(DEPTH, *_GS),
            _diag_blocks(stack("db_re")), _diag_blocks(stack("db_im")))
    d_a_re, d_a_im, d_log_dt, d_bt_re, d_bt_im = _ssm_prep_bwd(
        sp["a_re"], sp["a_im"], sp["log_dt"], sp["bt_re"], sp["bt_im"], cots)
    sgrads = {"norm_mix": stack("norm_mix"), "b_gate": stack("b_gate"),
              "attn_sinks": stack("attn_sinks")[:, :, :N_Q_HEADS], "ssm_a_re": d_a_re, "ssm_a_im": d_a_im,
              "ssm_b_re": jnp.swapaxes(d_bt_re, 2, 3), "ssm_b_im": jnp.swapaxes(d_bt_im, 2, 3),
              "ssm_c_re": _diag_blocks(stack("dc_re")), "ssm_c_im": -_diag_blocks(stack("dc_im")),
              "ssm_d": stack("ssm_d"), "ssm_log_dt": d_log_dt, "norm_ffn": stack("norm_ffn")}
    return sgrads, stack("conv_w")[:, :3]


def kernel(x, norm_mix, w_in, b_gate, attn_sinks, w_attn_o, conv_w, w_conv_o, ssm_a_re, ssm_a_im, ssm_b_re, ssm_b_im, ssm_c_re, ssm_c_im, ssm_d, ssm_log_dt, w_ssm_glu, w_ssm_o, w_mix_o, norm_ffn, w_ffn_in, w_ffn_out, norm_final, loss_target, m_norm_mix, m_w_in, m_b_gate, m_attn_sinks, m_w_attn_o, m_conv_w, m_w_conv_o, m_ssm_a_re, m_ssm_a_im, m_ssm_b_re, m_ssm_b_im, m_ssm_c_re, m_ssm_c_im, m_ssm_d, m_ssm_log_dt, m_w_ssm_glu, m_w_ssm_o, m_w_mix_o, m_norm_ffn, m_w_ffn_in, m_w_ffn_out, m_norm_final, v_norm_mix, v_w_in, v_b_gate, v_attn_sinks, v_w_attn_o, v_conv_w, v_w_conv_o, v_ssm_a_re, v_ssm_a_im, v_ssm_b_re, v_ssm_b_im, v_ssm_c_re, v_ssm_c_im, v_ssm_d, v_ssm_log_dt, v_w_ssm_glu, v_w_ssm_o, v_w_mix_o, v_norm_ffn, v_w_ffn_in, v_w_ffn_out, v_norm_final):
    big = {"w": dict(w_in=w_in, w_attn_o=w_attn_o, w_conv_o=w_conv_o, w_ssm_glu=w_ssm_glu, w_ssm_o=w_ssm_o,
                     w_mix_o=w_mix_o, w_ffn_in=w_ffn_in, w_ffn_out=w_ffn_out),
           "m": dict(w_in=m_w_in, w_attn_o=m_w_attn_o, w_conv_o=m_w_conv_o, w_ssm_glu=m_w_ssm_glu,
                     w_ssm_o=m_w_ssm_o, w_mix_o=m_w_mix_o, w_ffn_in=m_w_ffn_in, w_ffn_out=m_w_ffn_out),
           "v": dict(w_in=v_w_in, w_attn_o=v_w_attn_o, w_conv_o=v_w_conv_o, w_ssm_glu=v_w_ssm_glu,
                     w_ssm_o=v_w_ssm_o, w_mix_o=v_w_mix_o, w_ffn_in=v_w_ffn_in, w_ffn_out=v_w_ffn_out)}
    small = {"w": dict(norm_mix=norm_mix, b_gate=b_gate, attn_sinks=attn_sinks, ssm_a_re=ssm_a_re,
                       ssm_a_im=ssm_a_im, ssm_b_re=ssm_b_re, ssm_b_im=ssm_b_im, ssm_c_re=ssm_c_re,
                       ssm_c_im=ssm_c_im, ssm_d=ssm_d, ssm_log_dt=ssm_log_dt, norm_ffn=norm_ffn),
             "m": dict(norm_mix=m_norm_mix, b_gate=m_b_gate, attn_sinks=m_attn_sinks, ssm_a_re=m_ssm_a_re,
                       ssm_a_im=m_ssm_a_im, ssm_b_re=m_ssm_b_re, ssm_b_im=m_ssm_b_im, ssm_c_re=m_ssm_c_re,
                       ssm_c_im=m_ssm_c_im, ssm_d=m_ssm_d, ssm_log_dt=m_ssm_log_dt, norm_ffn=m_norm_ffn),
             "v": dict(norm_mix=v_norm_mix, b_gate=v_b_gate, attn_sinks=v_attn_sinks, ssm_a_re=v_ssm_a_re,
                       ssm_a_im=v_ssm_a_im, ssm_b_re=v_ssm_b_re, ssm_b_im=v_ssm_b_im, ssm_c_re=v_ssm_c_re,
                       ssm_c_im=v_ssm_c_im, ssm_d=v_ssm_d, ssm_log_dt=v_ssm_log_dt, norm_ffn=v_norm_ffn)}
    finals = {"w": norm_final, "m": m_norm_final, "v": v_norm_final}
    convs = {"w": conv_w, "m": m_conv_w, "v": v_conv_w}
    mine = 4 * lax.axis_index("x") + 2 * lax.axis_index("y") + lax.axis_index("c")

    travel = {s: _travel_layout(big[s]) for s in "wmv"}
    stacked16 = [a.astype(BF16) for a in travel["w"]]
    rp = {"norm_mix": norm_mix[:, None], "norm_ffn": norm_ffn[:, None], "attn_sinks": attn_sinks[:, None],
          "b_gate": b_gate[:, None], "ssm_d": ssm_d[:, None]}
    sp = {"a_re": ssm_a_re, "a_im": ssm_a_im, "log_dt": ssm_log_dt[:, :, None],
          "bt_re": jnp.swapaxes(ssm_b_re, 2, 3), "bt_im": jnp.swapaxes(ssm_b_im, 2, 3),
          "c_re": ssm_c_re, "c_im": ssm_c_im}
    rows_tile = {"win_t": 368, "wffn_t": 352, "wout": 176, "wmix": 128, "branch_t": 128, "wglu": 64}
    core = lax.axis_index("c").astype(jnp.int32).reshape(1)
    no_tie = jnp.zeros((8, 128), F32)

    def place_own(srcs):
        return [lax.dynamic_update_slice(lax.empty((N_DEV,) + s.shape, s.dtype), s[None], (mine, 0, 0)) for s in srcs]

    def gather_chips(tag, i, kinds, after, extra=()):
        srcs = [stacked16[j][i] for j in kinds] + list(extra)
        s_sems, r_sems, arrays, token = _split_start(
            f"gather_chips_start_{tag}", srcs + place_own(srcs), 4 * len(srcs), _plan_gather_chips, after)
        return (tag, s_sems, r_sems, arrays), token

    def gather_pass(state, after):
        tag, s_sems, r_sems, arrays = state
        arrays = _split_wait(f"gather_chips_wait_{tag}", arrays, s_sems, r_sems, after, _plan_gather_chips)
        n = len(arrays) // 2
        s_sems, r_sems, lands, token = _split_start(
            f"gather_pass_start_{tag}", list(arrays[n:]), 3 * n, _plan_gather_pass)
        return (tag, s_sems, r_sems, lands), token

    def gather_done(state, after, kinds):
        tag, s_sems, r_sems, lands = state
        lands = _split_wait(f"gather_pass_wait_{tag}", lands, s_sems, r_sems, after, _plan_gather_pass)
        named = {KINDS[j][0]: a.reshape(N_DEV * KINDS[j][1], KINDS[j][2]) for a, j in zip(lands, kinds)}
        return named, list(lands[len(kinds):])

    all_kinds, mixer_kinds, ffn_kinds = tuple(range(len(KINDS))), (0, 3, 4, 5), (1, 2)
    no_hooks = {name: (lambda value: no_tie) for name in ("early", "pre_ffn", "mid", "late")}
    state, _ = gather_chips("0m", 0, mixer_kinds, None, extra=[jnp.pad(conv_w.reshape(6, 128), ((0, 2), (0, 0)))])
    mats = _ssm_mats(sp)
    tabs = _rope_tabs()
    state, _ = gather_pass(state, mats["c_im_neg"])
    ffn_state, tie = gather_chips("0f", 0, ffn_kinds, state[3][0])
    w_next, (conv_all,) = gather_done(state, tabs[2], mixer_kinds)
    conv_full = conv_all[:, :6].reshape(N_DEV, DEPTH, 3, 64).transpose(1, 2, 0, 3).reshape(DEPTH, 3, WIDTH)
    rp["conv_w"] = jnp.pad(conv_full, ((0, 0), (0, 5), (0, 0)))

    act = x[0]
    weights, kept = [], []
    for i in range(DEPTH):
        w_i, hooks, held = w_next, dict(no_hooks), {}
        if i == 0:
            def early(value, held=held):
                held["ffn"], token = gather_pass(ffn_state, value)
                return token

            def pre_ffn(value, w_i=w_i, held=held):
                w_i.update(gather_done(held["ffn"], value, ffn_kinds)[0])

            hooks.update(early=early, pre_ffn=pre_ffn)
        if i + 1 < DEPTH:
            state, tie = gather_chips(str(i + 1), i + 1, all_kinds, w_i["win_t"] if i > 0 else tie)

            def mid(value, state=state, held=held):
                held["next"], token = gather_pass(state, value)
                return token

            hooks.update(mid=mid)
        elif i > 0:
            tie = no_tie
        act, k = _layer_fwd(act, i, w_i, rp, mats, tabs, tie, hooks)
        if i + 1 < DEPTH:
            w_next, _ = gather_done(held["next"], act, all_kinds)
        weights.append(w_i)
        kept.append(k)
    loss_row, dx, d_norm_final = _loss_head(act, norm_final[None], loss_target[0])
    loss = lax.psum(loss_row[0, 0], ("x", "y", "c"))

    landings = [lax.empty((DEPTH, 4, r, c), BF16) for _, r, c in KINDS]
    landings0 = [lax.empty((1, 4, r, c), BF16) for _, r, c in KINDS]

    def scatter_pair(tag, kinds, grads, after):
        parts4 = [g.reshape(4, 2, KINDS[j][1], KINDS[j][2]) for g, j in zip(grads, kinds)]
        zones = [lax.empty((4, KINDS[j][1], KINDS[j][2]), BF16) for j in kinds]
        s_sems, r_sems, arrays, token = _split_start(
            f"scatter_pair_start_{tag}", parts4 + zones, 4 * len(kinds), _plan_scatter_pair, after)
        return (tag, kinds, s_sems, r_sems, arrays), token

    def scatter_chips(state, lands, slot, after):
        tag, kinds, s_sems, r_sems, arrays = state
        arrays = _split_wait(f"scatter_pair_wait_{tag}", arrays, s_sems, r_sems, after, _plan_scatter_pair)
        n = len(kinds)
        sums, mine_lands = [], []
        for k, j in enumerate(kinds):
            name = KINDS[j][0]
            chip_sum, land = _pair_sum(arrays[k], arrays[n + k], lands[j], slot, core, rows_tile[name],
                                       f"pair_sum_{name}")
            sums.append(chip_sum)
            mine_lands.append(land)
        s_sems, r_sems, arrays, token = _split_start(
            f"scatter_chips_start_{tag}", sums + mine_lands, 3 * n, _plan_scatter_chips(slot))
        return (tag, kinds, slot, s_sems, r_sems, arrays), token

    def scatter_done(state, lands, after):
        tag, kinds, slot, s_sems, r_sems, arrays = state
        arrays = _split_wait(f"scatter_chips_wait_{tag}", arrays, s_sems, r_sems, after, _plan_scatter_chips(slot))
        lands = list(lands)
        for k, j in enumerate(kinds):
            lands[j] = arrays[len(kinds) + k]
        return lands

    sg = [None] * DEPTH
    pending, tie = None, no_tie
    for i in reversed(range(DEPTH)):
        hooks, held = dict(no_hooks), {}
        if pending is not None:
            def mid(value, i=i, pending=pending, held=held):
                held["chips"], token = scatter_chips(pending, landings, i + 1, value[2])
                if i == 0:
                    held["ffn_pair"], token = scatter_pair("0f", ffn_kinds, value[:2], token)
                return token

            hooks.update(mid=mid)
        if i == 0:
            def late(value, held=held):
                held["ffn_chips"], token = scatter_chips(held["ffn_pair"], landings0, 0, value)
                return token

            hooks.update(late=late)
        dx, grads, sg[i] = _layer_bwd(dx, kept[i], i, weights[i], rp, mats, tabs, tie, hooks)
        if pending is not None:
            landings = scatter_done(held["chips"], landings, dx)
        if i > 0:
            pending, tie = scatter_pair(str(i), all_kinds, grads, dx)
        else:
            pending, _ = scatter_pair("0m", mixer_kinds, [grads[j] for j in mixer_kinds], dx)

    sgrads, conv_grad = _replicated_grads(sg, sp)

    def pack_small(t, final, conv):
        flat = [t[name].reshape(DEPTH, n) for name, n in SMALL]
        flat = jnp.concatenate([jnp.concatenate(flat, axis=1).reshape(-1), final.reshape(-1), conv.reshape(-1)])
        return jnp.pad(flat, (0, SMALL_ROWS * 128 - flat.shape[0])).reshape(SMALL_ROWS, 128)

    small_src = [pack_small(sgrads, d_norm_final, conv_grad).astype(BF16)]
    last, tie = scatter_chips(pending, landings0, 0, small_src[0])
    s_sems, r_sems, arrays, tie = _split_start(
        "gather_small_chips_start", small_src + place_own(small_src), 4, _plan_gather_chips, tie)
    small_state = ("small", s_sems, r_sems, arrays)

    big_out = [_adamw(landings[j], travel["w"][j], travel["m"][j], travel["v"][j], rows_tile[name],
                      "adamw_late_" + name, groups=(1, DEPTH), tie=tie) for j, (name, _, _) in enumerate(KINDS)]
    landings0 = scatter_done(held["ffn_chips"], landings0, big_out[-1][0])
    landings0 = scatter_done(last, landings0, big_out[-1][0])
    small_state, _ = gather_pass(small_state, landings0[0])
    big_out = [_adamw(landings0[j], travel["w"][j], travel["m"][j], travel["v"][j], rows_tile[name],
                      "adamw_first_" + name, groups=(0, 1), fill=big_out[j]) for j, (name, _, _) in enumerate(KINDS)]
    big_res = [_native_layout([big_out[j][kind] for j in range(len(KINDS))]) for kind in range(4)]

    zeros_conv = jnp.zeros((CONV_N,), F32)
    _, (sparts,) = gather_done(small_state, big_out[-1][0], ())
    sw, sm_, sv = (pack_small(small[s], finals[s], zeros_conv) for s in "wmv")
    small_out = _adamw(sparts[None], sw[None], sm_[None], sv[None], SMALL_ROWS // 8, "adamw_replicated")

    def unpack_small(p):
        flat = p.reshape(-1)
        per = flat[:DEPTH * SMALL_PER_LAYER].reshape(DEPTH, SMALL_PER_LAYER)
        out, off = {}, 0
        for name, n in SMALL:
            out[name] = per[:, off:off + n].reshape(small["w"][name].shape)
            off += n
        out["norm_final"] = flat[DEPTH * SMALL_PER_LAYER:DEPTH * SMALL_PER_LAYER + D_MODEL]
        return out

    small_res = [unpack_small(p) for p in small_out]

    conv_off = DEPTH * SMALL_PER_LAYER + D_MODEL
    conv_parts = sparts.reshape(N_DEV, -1)[:, conv_off:conv_off + CONV_N].reshape(N_DEV, DEPTH * 3, WIDTH)
    conv_parts = lax.dynamic_slice_in_dim(conv_parts, mine * 64, 64, axis=2)
    conv_res = _adamw(conv_parts[None], *(convs[s].reshape(1, DEPTH * 3, 64) for s in "wmv"), DEPTH * 3, "adamw_conv_w")

    order = ["norm_mix", "w_in", "b_gate", "attn_sinks", "w_attn_o", "conv_w", "w_conv_o", "ssm_a_re", "ssm_a_im",
             "ssm_b_re", "ssm_b_im", "ssm_c_re", "ssm_c_im", "ssm_d", "ssm_log_dt", "w_ssm_glu", "w_ssm_o",
             "w_mix_o", "norm_ffn", "w_ffn_in", "w_ffn_out", "norm_final"]
    outs = [loss, dx[None]]
    for kind in range(4):
        for name in order:
            if name == "conv_w":
                outs.append(conv_res[kind].reshape(DEPTH, 3, 64))
            elif name in big_res[kind]:
                outs.append(big_res[kind][name])
            else:
                outs.append(small_res[kind][name])
    return tuple(outs)
```

```python
import functools
import math

import jax
import jax.numpy as jnp
from jax import lax
from jax.experimental import pallas as pl
from jax.experimental.pallas import tpu as pltpu

F32 = jnp.float32
BF16 = jnp.bfloat16

N_DEV = 8
DEPTH = 4
SEQ = 2048
D_MODEL = 1024
N_Q_HEADS = 8
HEAD_DIM = 64
ATTN_W = 512
KV_W = 128
BLOCK = 128
N_BLOCKS = SEQ // BLOCK
ROPE_THETA = 500000.0
ROT_DIM = 16
NEG_INF = -1e30
WIDTH = 512
SSM_GROUPS = 32
SSM_GROUP = 16
SSM_STATE = 64
SLABS = 16
CHUNK = 256
N_CHUNKS = SEQ // CHUNK
GATE_W = 3 * D_MODEL
IN_COLS = 5888
FFN_H = 2816
NORM_EPS = 1e-6
LR, B1, B2, ADAM_EPS, WD, STEP = 0.001, 0.9, 0.999, 1e-08, 0.01, 10

COL_Q, COL_KV, COL_CBX, COL_U, COL_G = 0, 512, 768, 2304, 2816
PIECE_W = (512, 256, 512, 512, 512, 512, 3072)
PIECE_OFF = tuple(sum(PIECE_W[:i]) for i in range(len(PIECE_W)))

KINDS = (("win_t", 736, 1024), ("wffn_t", 704, 1024), ("wout", 352, 1024), ("wmix", 128, 1024),
         ("branch_t", 128, 1536), ("wglu", 64, 512))

SMALL = (("norm_mix", 1024), ("b_gate", 3072), ("attn_sinks", 8), ("ssm_a_re", 2048), ("ssm_a_im", 2048),
         ("ssm_b_re", 32768), ("ssm_b_im", 32768), ("ssm_c_re", 32768), ("ssm_c_im", 32768),
         ("ssm_d", 512), ("ssm_log_dt", 32), ("norm_ffn", 1024))
SMALL_PER_LAYER = sum(n for _, n in SMALL)
CONV_N = DEPTH * 3 * WIDTH
SMALL_ROWS = 4480

VMEM_LIMIT = 56 * 1024 * 1024
NT = (((1,), (1,)), ((), ()))
TN = (((0,), (0,)), ((), ()))
MESH_ID = pl.DeviceIdType.MESH
ANY = pl.BlockSpec(memory_space=pl.ANY)
HBM = pl.BlockSpec(memory_space=pltpu.HBM)
SEM = pl.BlockSpec(memory_space=pltpu.SEMAPHORE)
EFFECT = pltpu.SideEffectType.DATAFLOW_SIDE_EFFECTING


def _cp(**kw):
    return pltpu.CompilerParams(vmem_limit_bytes=VMEM_LIMIT, **kw)


def _full(shape):
    return pl.BlockSpec(shape, lambda *_: (0,) * len(shape))


def _mm_tn(a, b, *, tm, tn, name):
    k, m = a.shape
    n = b.shape[1]

    def body(a_ref, b_ref, o_ref):
        o_ref[...] = lax.dot_general(a_ref[...].astype(BF16), b_ref[...].astype(BF16), TN,
                                     preferred_element_type=F32).astype(BF16)

    return pl.pallas_call(
        body, grid=(m // tm, n // tn),
        in_specs=[pl.BlockSpec((k, tm), lambda i, j: (0, i)), pl.BlockSpec((k, tn), lambda i, j: (0, j))],
        out_specs=pl.BlockSpec((tm, tn), lambda i, j: (i, j)),
        out_shape=jax.ShapeDtypeStruct((m, n), BF16), compiler_params=_cp(), name=name)(a, b)


def _rms_rows(xv, g):
    r = lax.rsqrt(jnp.mean(xv * xv, axis=-1, keepdims=True) + NORM_EPS)
    return ((xv * r) * g).astype(BF16)


def _rms_mm_in(x, g, wt, tie):
    tt = 256
    widths = (ATTN_W, 2 * KV_W, 3 * WIDTH, WIDTH, GATE_W)
    offs = (COL_Q, COL_KV, COL_CBX, COL_U, COL_G)

    def body(x_ref, g_ref, w_ref, tie_ref, q_ref, kv_ref, cbx_ref, u_ref, u16_ref, gl_ref, h_ref):
        h = _rms_rows(x_ref[...], g_ref[...])
        h_ref[...] = h
        prod = lax.dot_general(h, w_ref[...], NT, preferred_element_type=F32)
        for ref, o, w in zip((q_ref, kv_ref, cbx_ref, u_ref, gl_ref), offs, widths):
            ref[...] = prod[:, o:o + w]
        u16_ref[...] = prod[:, COL_U:COL_U + WIDTH].astype(BF16)

    row = lambda w: pl.BlockSpec((tt, w), lambda i: (i, 0))
    sds = jax.ShapeDtypeStruct
    return pl.pallas_call(
        body, grid=(SEQ // tt,), in_specs=[row(D_MODEL), _full((1, D_MODEL)), _full((IN_COLS, D_MODEL)), ANY],
        out_specs=[row(ATTN_W), row(2 * KV_W), row(3 * WIDTH), row(WIDTH), row(WIDTH), row(GATE_W), row(D_MODEL)],
        out_shape=[sds((SEQ, ATTN_W), F32), sds((SEQ, 2 * KV_W), F32), sds((SEQ, 3 * WIDTH), F32),
                   sds((SEQ, WIDTH), F32), sds((SEQ, WIDTH), BF16), sds((SEQ, GATE_W), F32),
                   sds((SEQ, D_MODEL), BF16)],
        compiler_params=_cp(), name="rms_mm_in")(x, g, wt, tie)


def _rms_mm_ffn(x, g, wt):
    tt = 256

    def body(x_ref, g_ref, w_ref, o_ref, h_ref):
        h = _rms_rows(x_ref[...], g_ref[...])
        h_ref[...] = h
        o_ref[...] = lax.dot_general(h, w_ref[...], NT, preferred_element_type=F32)

    row = lambda w: pl.BlockSpec((tt, w), lambda i: (i, 0))
    return pl.pallas_call(
        body, grid=(SEQ // tt,), in_specs=[row(D_MODEL), _full((1, D_MODEL)), _full((2 * FFN_H, D_MODEL))],
        out_specs=[row(2 * FFN_H), row(D_MODEL)],
        out_shape=[jax.ShapeDtypeStruct((SEQ, 2 * FFN_H), F32), jax.ShapeDtypeStruct((SEQ, D_MODEL), BF16)],
        compiler_params=_cp(), name="rms_mm_ffn")(x, g, wt)


def _mm_rmsbwd(pieces, wt, x, g, dres, name):
    tt = 256
    widths = [p.shape[1] for p in pieces]
    offs = [sum(widths[:i]) for i in range(len(widths))]
    n = len(pieces)

    def body(*refs):
        p_refs, (w_ref, x_ref, g_ref, r_ref, dx_ref, dg_ref) = refs[:n], refs[n:]

        @pl.when(pl.program_id(0) == 0)
        def _():
            dg_ref[...] = jnp.zeros_like(dg_ref)

        dh = jnp.zeros((tt, D_MODEL), F32)
        for p_ref, o, w in zip(p_refs, offs, widths):
            dh += jnp.dot(p_ref[...], w_ref[o:o + w, :], preferred_element_type=F32)
        xv = x_ref[...]
        r = lax.rsqrt(jnp.mean(xv * xv, axis=-1, keepdims=True) + NORM_EPS)
        xh = xv * r
        gy = dh * g_ref[...]
        dx_ref[...] = r_ref[...] + r * (gy - xh * jnp.mean(gy * xh, axis=-1, keepdims=True))
        dg_ref[...] += jnp.sum(dh * xh, axis=0, keepdims=True)

    row = lambda w: pl.BlockSpec((tt, w), lambda i: (i, 0))
    return pl.pallas_call(
        body, grid=(SEQ // tt,),
        in_specs=[row(w) for w in widths] + [_full(wt.shape), row(D_MODEL), _full((1, D_MODEL)), row(D_MODEL)],
        out_specs=[row(D_MODEL), _full((1, D_MODEL))],
        out_shape=[jax.ShapeDtypeStruct((SEQ, D_MODEL), F32), jax.ShapeDtypeStruct((1, D_MODEL), F32)],
        compiler_params=_cp(), name=name)(*pieces, wt, x, g, dres)


def _tn_pieces(pieces, h):
    tk, tn = SEQ // 2, 256
    nk = SEQ // tk
    n = len(pieces)

    def body(*refs):
        p_refs, (h_ref, o_ref, acc_ref) = refs[:n], refs[n:]
        kk = pl.program_id(1)
        hv = h_ref[...]
        for p_ref, o, w in zip(p_refs, PIECE_OFF, PIECE_W):
            part = lax.dot_general(p_ref[...], hv, TN, preferred_element_type=F32)

            @pl.when(kk == 0)
            def _():
                acc_ref[o:o + w, :] = part

            @pl.when(kk == nk - 1)
            def _():
                o_ref[o:o + w, :] = (acc_ref[o:o + w, :] + part).astype(BF16)

    return pl.pallas_call(
        body, grid=(D_MODEL // tn, nk),
        in_specs=[pl.BlockSpec((tk, w), lambda j, kk: (kk, 0)) for w in PIECE_W]
        + [pl.BlockSpec((tk, tn), lambda j, kk: (kk, j))],
        out_specs=pl.BlockSpec((IN_COLS, tn), lambda j, kk: (0, j)),
        out_shape=jax.ShapeDtypeStruct((IN_COLS, D_MODEL), BF16),
        scratch_shapes=[pltpu.VMEM((IN_COLS, tn), F32)], compiler_params=_cp(), name="tn_pieces")(*pieces, h)


def _tn_branches(dys, acts):
    tk = 512
    nk = SEQ // tk

    def body(d0, d1, d2, a0, a1, a2, o_ref, acc_ref):
        kk = pl.program_id(0)

        @pl.when(kk == 0)
        def _():
            acc_ref[...] = jnp.zeros_like(acc_ref)

        for j, (d, a) in enumerate(((d0, a0), (d1, a1), (d2, a2))):
            acc_ref[:, WIDTH * j:WIDTH * (j + 1)] += lax.dot_general(d[...], a[...], TN, preferred_element_type=F32)

        @pl.when(kk == nk - 1)
        def _():
            o_ref[...] = acc_ref[...].astype(BF16)

    row = lambda w: pl.BlockSpec((tk, w), lambda kk: (kk, 0))
    return pl.pallas_call(
        body, grid=(nk,), in_specs=[row(D_MODEL)] * 3 + [row(WIDTH)] * 3,
        out_specs=_full((D_MODEL, 3 * WIDTH)), out_shape=jax.ShapeDtypeStruct((D_MODEL, 3 * WIDTH), BF16),
        scratch_shapes=[pltpu.VMEM((D_MODEL, 3 * WIDTH), F32)], compiler_params=_cp(), name="tn_branches",
    )(*dys, *acts)


def _rope(t, c, a, b):
    return t * c + pltpu.roll(t, 120, axis=1) * a + pltpu.roll(t, 8, axis=1) * b


def _rope_t(d, c, a, b):
    return d * c + pltpu.roll(d * a, 8, axis=1) + pltpu.roll(d * b, 120, axis=1)


def _band_sides(band):
    left = lax.broadcasted_iota(jnp.int32, band.shape, 1) < HEAD_DIM
    h0 = jnp.where(left, band, 0.0)
    h1 = jnp.where(left, 0.0, band)
    r0 = pltpu.roll(h0, HEAD_DIM, axis=1)
    r1 = pltpu.roll(h1, HEAD_DIM, axis=1)
    return ((h0.astype(BF16), r0.astype(BF16)), (r1.astype(BF16), h1.astype(BF16)))


def _attn_mask(i):
    qi = lax.broadcasted_iota(jnp.int32, (2 * BLOCK, 2 * BLOCK), 0) % BLOCK
    kj = lax.broadcasted_iota(jnp.int32, (2 * BLOCK, 2 * BLOCK), 1)
    delta = qi + BLOCK - kj
    return (delta >= 0) & (delta < BLOCK) & ((kj >= BLOCK) | (i > 0))


def _attn_probs(s, ok, sink):
    s = jnp.where(ok, s * (HEAD_DIM ** -0.5), NEG_INF)
    m = jnp.maximum(jnp.max(s, axis=-1, keepdims=True), sink)
    p = jnp.exp(s - m)
    es = jnp.exp(sink - m)
    inv = 1.0 / (jnp.sum(p, axis=-1, keepdims=True) + es)
    return p * inv, es * inv


def _kv_group(qs, ks, vs, kh, sink_ref):
    q2 = jnp.concatenate([qs[2 * kh], qs[2 * kh + 1]], axis=0)
    kst = jnp.concatenate([ks[kh][0], ks[kh][1]], axis=0)
    vst = jnp.concatenate([vs[kh][0], vs[kh][1]], axis=0)
    top = lax.broadcasted_iota(jnp.int32, (2 * BLOCK, 1), 0) < BLOCK
    sinks = [jnp.where(top, sink_ref[0, 4 * kh + h], sink_ref[0, 4 * kh + 2 + h]) for h in range(2)]
    return q2, kst, vst, sinks


def _attn_load(q_ref, kvc_ref, kvp_ref, tc_ref, ta_ref, tb_ref, pc_ref, pa_ref, pb_ref):
    c, a, b = tc_ref[...], ta_ref[...], tb_ref[...]
    kc = _rope(kvc_ref[:, :KV_W], c, a, b)
    kp = _rope(kvp_ref[:, :KV_W], pc_ref[...], pa_ref[...], pb_ref[...])
    kband = jnp.concatenate([kp, kc], axis=0)
    vband = jnp.concatenate([kvp_ref[:, KV_W:], kvc_ref[:, KV_W:]], axis=0)
    qs = [_rope(q_ref[:, 128 * j:128 * (j + 1)], c, a, b).astype(BF16) for j in range(4)]
    return qs, _band_sides(kband), _band_sides(vband), (c, a, b)


def _attn_specs(clamp):
    cur = lambda i: (clamp(i), 0)
    prev = lambda i: (jnp.maximum(clamp(i) - 1, 0), 0)
    return [
        pl.BlockSpec((BLOCK, ATTN_W), cur), pl.BlockSpec((BLOCK, 2 * KV_W), cur),
        pl.BlockSpec((BLOCK, 2 * KV_W), prev),
        pl.BlockSpec((BLOCK, 128), cur), pl.BlockSpec((BLOCK, 128), cur), pl.BlockSpec((BLOCK, 128), cur),
        pl.BlockSpec((BLOCK, 128), prev), pl.BlockSpec((BLOCK, 128), prev), pl.BlockSpec((BLOCK, 128), prev),
        pl.BlockSpec(memory_space=pltpu.SMEM),
    ]


def _attn_fwd(q, kv, tabs, sinks):
    tc, ta, tb = tabs

    def body(q_ref, kvc_ref, kvp_ref, tc_ref, ta_ref, tb_ref, pc_ref, pa_ref, pb_ref, sink_ref, o_ref):
        i = pl.program_id(0)
        qs, ks, vs, _ = _attn_load(q_ref, kvc_ref, kvp_ref, tc_ref, ta_ref, tb_ref, pc_ref, pa_ref, pb_ref)
        ok = _attn_mask(i)
        for kh in range(2):
            q2, kst, vst, sinks = _kv_group(qs, ks, vs, kh, sink_ref)
            s = lax.dot_general(q2, kst, NT, preferred_element_type=F32)
            pn = [_attn_probs(s[:, 2 * BLOCK * h:2 * BLOCK * (h + 1)], ok, sinks[h])[0].astype(BF16) for h in range(2)]
            o2 = jnp.dot(jnp.concatenate(pn, axis=1), vst, preferred_element_type=F32).astype(BF16)
            for r in range(2):
                j = 2 * kh + r
                o_ref[:, 128 * j:128 * (j + 1)] = o2[BLOCK * r:BLOCK * (r + 1)]

    return pl.pallas_call(
        body, grid=(N_BLOCKS,), in_specs=_attn_specs(lambda i: i),
        out_specs=pl.BlockSpec((BLOCK, ATTN_W), lambda i: (i, 0)),
        out_shape=jax.ShapeDtypeStruct((SEQ, ATTN_W), BF16), compiler_params=_cp(), name="attn_fwd",
    )(q, kv, kv, tc, ta, tb, tc, ta, tb, sinks)


def _attn_bwd(q, kv, tabs, sinks, do):
    tc, ta, tb = tabs
    last = N_BLOCKS - 1
    clamp = lambda i: jnp.minimum(i, last)

    def place(full, side, kh):
        left = lax.broadcasted_iota(jnp.int32, full.shape, 1) < HEAD_DIM
        valid = jnp.where(left, full, 0.0) if side == 0 else jnp.where(left, 0.0, full)
        return valid if side == kh else pltpu.roll(valid, HEAD_DIM, axis=1)

    def body(q_ref, kvc_ref, kvp_ref, tc_ref, ta_ref, tb_ref, pc_ref, pa_ref, pb_ref, sink_ref, do_ref,
             dq_ref, dkv_ref, ds_ref, carry_ref):
        i = pl.program_id(0)

        @pl.when(i == 0)
        def _():
            ds_ref[...] = jnp.zeros_like(ds_ref)
            carry_ref[...] = jnp.zeros_like(carry_ref)

        @pl.when(i > last)
        def _():
            dkv_ref[...] = carry_ref[...].astype(BF16)

        @pl.when(i <= last)
        def _():
            qs, ks, vs, (c, a, b) = _attn_load(q_ref, kvc_ref, kvp_ref, tc_ref, ta_ref, tb_ref,
                                               pc_ref, pa_ref, pb_ref)
            ok = _attn_mask(i)
            dk = jnp.zeros((2 * BLOCK, 128), F32)
            dv = jnp.zeros((2 * BLOCK, 128), F32)
            dsink = jnp.zeros((1, 128), F32)
            lane = lax.broadcasted_iota(jnp.int32, (1, 128), 1)
            for kh in range(2):
                q2, kst, vst, sinks = _kv_group(qs, ks, vs, kh, sink_ref)
                do2 = jnp.concatenate([do_ref[:, 128 * (2 * kh + r):128 * (2 * kh + r + 1)] for r in range(2)],
                                      axis=0).astype(BF16)
                s = lax.dot_general(q2, kst, NT, preferred_element_type=F32)
                dp = lax.dot_general(do2, vst, NT, preferred_element_type=F32)
                pns, dss = [], []
                for h in range(2):
                    cols = slice(2 * BLOCK * h, 2 * BLOCK * (h + 1))
                    pn, ps = _attn_probs(s[:, cols], ok, sinks[h])
                    dr = jnp.sum(pn * dp[:, cols], axis=-1, keepdims=True)
                    pns.append(pn.astype(BF16))
                    dss.append((pn * (dp[:, cols] - dr) * (HEAD_DIM ** -0.5)).astype(BF16))
                    for r in range(2):
                        part = -jnp.sum((ps * dr)[BLOCK * r:BLOCK * (r + 1)])
                        dsink += jnp.where(lane == 4 * kh + 2 * r + h, part, 0.0)
                ds2, pn2 = jnp.concatenate(dss, axis=1), jnp.concatenate(pns, axis=1)
                dq2 = jnp.dot(ds2, kst, preferred_element_type=F32)
                dk2 = lax.dot_general(ds2, q2, TN, preferred_element_type=F32)
                dv2 = lax.dot_general(pn2, do2, TN, preferred_element_type=F32)
                for h in range(2):
                    dk += place(dk2[2 * BLOCK * h:2 * BLOCK * (h + 1)], h, kh)
                    dv += place(dv2[2 * BLOCK * h:2 * BLOCK * (h + 1)], h, kh)
                for r in range(2):
                    j = 2 * kh + r
                    dq_ref[:, 128 * j:128 * (j + 1)] = _rope_t(dq2[BLOCK * r:BLOCK * (r + 1)], c, a, b).astype(BF16)
            ds_ref[...] += dsink
            dk_prev = _rope_t(dk[:BLOCK], pc_ref[...], pa_ref[...], pb_ref[...])
            dk_cur = _rope_t(dk[BLOCK:], c, a, b)
            prev = jnp.concatenate([dk_prev, dv[:BLOCK]], axis=1)
            dkv_ref[...] = (carry_ref[...] + prev).astype(BF16)
            carry_ref[...] = jnp.concatenate([dk_cur, dv[BLOCK:]], axis=1)

    return pl.pallas_call(
        body, grid=(N_BLOCKS + 1,),
        in_specs=_attn_specs(clamp) + [pl.BlockSpec((BLOCK, ATTN_W), lambda i: (clamp(i), 0))],
        out_specs=[pl.BlockSpec((BLOCK, ATTN_W), lambda i: (clamp(i), 0)),
                   pl.BlockSpec((BLOCK, 2 * KV_W), lambda i: (jnp.maximum(i - 1, 0), 0)),
                   pl.BlockSpec((1, 128), lambda i: (0, 0))],
        out_shape=[jax.ShapeDtypeStruct((SEQ, ATTN_W), BF16), jax.ShapeDtypeStruct((SEQ, 2 * KV_W), BF16),
                   jax.ShapeDtypeStruct((1, 128), F32)],
        scratch_shapes=[pltpu.VMEM((BLOCK, 2 * KV_W), F32)], compiler_params=_cp(), name="attn_bwd",
    )(q, kv, kv, tc, ta, tb, tc, ta, tb, sinks, do)


def _shift_down(z, k):
    row = lax.broadcasted_iota(jnp.int32, z.shape, 0)
    return jnp.where(row < k, 0.0, pltpu.roll(z, k, axis=0))


def _shift_up(z, k):
    n = z.shape[0]
    row = lax.broadcasted_iota(jnp.int32, z.shape, 0)
    return jnp.where(row >= n - k, 0.0, pltpu.roll(z, n - k, axis=0))


def _conv_specs():
    nb = WIDTH // 128
    return [pl.BlockSpec((SEQ, 128), lambda j: (0, j)), pl.BlockSpec((SEQ, 128), lambda j: (0, nb + j)),
            pl.BlockSpec((SEQ, 128), lambda j: (0, 2 * nb + j)), pl.BlockSpec((None, 8, 128), lambda j: (0, 0, j))]


def _conv_fwd(cbx, cw, layer):
    def body(cb_ref, cc_ref, cx_ref, w_ref, o_ref):
        z = cc_ref[...] * cx_ref[...]
        s = w_ref[0:1, :] * _shift_down(z, 2) + w_ref[1:2, :] * _shift_down(z, 1) + w_ref[2:3, :] * z
        o_ref[...] = (cb_ref[...] * s).astype(BF16)

    specs = _conv_specs()
    specs[3] = pl.BlockSpec((None, 8, 128), lambda j: (layer, 0, j))
    return pl.pallas_call(
        body, grid=(WIDTH // 128,), in_specs=specs,
        out_specs=pl.BlockSpec((SEQ, 128), lambda j: (0, j)),
        out_shape=jax.ShapeDtypeStruct((SEQ, WIDTH), BF16), compiler_params=_cp(), name="conv_fwd",
    )(cbx, cbx, cbx, cw)


def _conv_bwd(cbx, cw, layer, dout, tie):
    def body(cb_ref, cc_ref, cx_ref, w_ref, do_ref, tie_ref, dcb_ref, dcc_ref, dcx_ref, dw_ref):
        cc, cx = cc_ref[...], cx_ref[...]
        z = cc * cx
        z1, z2 = _shift_down(z, 1), _shift_down(z, 2)
        w0, w1, w2 = w_ref[0:1, :], w_ref[1:2, :], w_ref[2:3, :]
        dout = do_ref[...]
        ds = dout * cb_ref[...]
        dcb_ref[...] = (dout * (w0 * z2 + w1 * z1 + w2 * z)).astype(BF16)
        dz = w2 * ds + w1 * _shift_up(ds, 1) + w0 * _shift_up(ds, 2)
        dcc_ref[...] = (dz * cx).astype(BF16)
        dcx_ref[...] = (dz * cc).astype(BF16)
        rows = [jnp.sum(ds * zz, axis=0, keepdims=True) for zz in (z2, z1, z)]
        dw_ref[...] = jnp.concatenate(rows + [jnp.zeros((5, 128), F32)], axis=0)

    col = lambda j: (0, j)
    specs = _conv_specs()
    specs[3] = pl.BlockSpec((None, 8, 128), lambda j: (layer, 0, j))
    return pl.pallas_call(
        body, grid=(WIDTH // 128,), in_specs=specs + [pl.BlockSpec((SEQ, 128), col), ANY],
        out_specs=[pl.BlockSpec((SEQ, 128), col), pl.BlockSpec((SEQ, 128), col), pl.BlockSpec((SEQ, 128), col),
                   pl.BlockSpec((8, 128), col)],
        out_shape=[jax.ShapeDtypeStruct((SEQ, WIDTH), BF16)] * 3 + [jax.ShapeDtypeStruct((8, WIDTH), F32)],
        compiler_params=_cp(), name="conv_bwd",
    )(cbx, cbx, cbx, cw, dout, tie)


def _ssm_prep_math(a_re, a_im, log_dt, bt_re, bt_im):
    dt = jnp.exp(log_dt)
    er = jnp.exp(a_re * dt)
    lr = er * jnp.cos(a_im * dt)
    li = er * jnp.sin(a_im * dt)
    n2 = a_re * a_re + a_im * a_im
    cr = ((lr - 1.0) * a_re + li * a_im) / n2
    ci = (li * a_re - (lr - 1.0) * a_im) / n2
    cr3, ci3 = cr[:, None, :], ci[:, None, :]
    return lr, li, cr3 * bt_re - ci3 * bt_im, cr3 * bt_im + ci3 * bt_re


_GS = (SSM_GROUPS, SSM_STATE)
_GHS = (SSM_GROUPS, SSM_GROUP, SSM_STATE)


def _layered(shape):
    return pl.BlockSpec((None,) + shape, lambda l: (l,) + (0,) * len(shape))


def _ssm_prep(a_re, a_im, log_dt, bt_re, bt_im):
    def body(ar, ai, ld, br, bi, o0, o1, o2, o3):
        outs = _ssm_prep_math(ar[...], ai[...], ld[...], br[...], bi[...])
        for o, v in zip((o0, o1, o2, o3), outs):
            o[...] = v

    shapes = [_GS, _GS, _GHS, _GHS]
    return pl.pallas_call(
        body, grid=(DEPTH,), in_specs=[_layered(s) for s in (_GS, _GS, (SSM_GROUPS, 1), _GHS, _GHS)],
        out_specs=[_layered(s) for s in shapes],
        out_shape=[jax.ShapeDtypeStruct((DEPTH,) + s, F32) for s in shapes],
        name="ssm_prep")(a_re, a_im, log_dt, bt_re, bt_im)


def _ssm_prep_bwd(a_re, a_im, log_dt, bt_re, bt_im, cots):
    def body(ar, ai, ld, br, bi, c0, c1, c2, c3, o0, o1, o2, o3, o4):
        _, vjp = jax.vjp(_ssm_prep_math, ar[...], ai[...], ld[...], br[...], bi[...])
        for o, v in zip((o0, o1, o2, o3, o4), vjp((c0[...], c1[...], c2[...], c3[...]))):
            o[...] = v

    ins = (_GS, _GS, (SSM_GROUPS, 1), _GHS, _GHS)
    return pl.pallas_call(
        body, grid=(DEPTH,), in_specs=[_layered(s) for s in ins + (_GS, _GS, _GHS, _GHS)],
        out_specs=[_layered(s) for s in ins],
        out_shape=[jax.ShapeDtypeStruct((DEPTH,) + s, F32) for s in ins],
        name="ssm_prep_bwd")(a_re, a_im, log_dt, bt_re, bt_im, *cots)


LANES_G = 512
N_LANE_GROUPS = SSM_GROUPS * SSM_STATE // LANES_G


def _scan_order(a):
    return a.reshape(N_CHUNKS, CHUNK, -1).transpose(1, 0, 2).reshape(a.shape)


def _time_order(a):
    return a.reshape(CHUNK, N_CHUNKS, -1).transpose(1, 0, 2).reshape(a.shape)


def _scan_in_place(xr_ref, xi_ref, ar, ai, reverse):
    shape = (N_CHUNKS, xr_ref.shape[1])
    ar, ai = jnp.broadcast_to(ar, shape), jnp.broadcast_to(ai, shape)

    def rows(tau):
        t = (CHUNK - 1 - tau) if reverse else tau
        return pl.ds(pl.multiple_of(t * N_CHUNKS, N_CHUNKS), N_CHUNKS)

    def step(tau, carry):
        sr, si = carry
        return ar * sr - ai * si + xr_ref[rows(tau), :], ar * si + ai * sr + xi_ref[rows(tau), :]

    zero = jnp.zeros(shape, F32)
    er, ei = lax.fori_loop(0, CHUNK, step, (zero, zero), unroll=8)
    qr, qi = ar, ai
    for _ in range(8):
        qr, qi = qr * qr - qi * qi, 2.0 * qr * qi
    shift = _shift_up if reverse else _shift_down
    for k in (1, 2, 4):
        sr, si = shift(er, k), shift(ei, k)
        er, ei = er + qr * sr - qi * si, ei + qr * si + qi * sr
        qr, qi = qr * qr - qi * qi, 2.0 * qr * qi
    start = (shift(er, 1), shift(ei, 1))

    def write(tau, carry):
        sr, si = step(tau, carry)
        xr_ref[rows(tau), :] = sr
        xi_ref[rows(tau), :] = si
        return sr, si

    return write, start


def _ssm_specs(layer):
    col = lambda w: pl.BlockSpec((SEQ, w), lambda g: (0, g))
    diag = pl.BlockSpec((None, None, 128, LANES_G), lambda g: (layer, g, 0, 0))
    vec = pl.BlockSpec((None, 1, LANES_G), lambda g: (layer, 0, g))
    return col, diag, vec


def _ssm_fwd(u16, u, mats, layer, d):
    def body(u16_ref, u_ref, d_ref, br_ref, bi_ref, cr_ref, ci_ref, ar_ref, ai_ref, xr_ref, xi_ref, y_ref):
        uv = u16_ref[...]
        xr_ref[...] = jnp.dot(uv, br_ref[...], preferred_element_type=F32)
        xi_ref[...] = jnp.dot(uv, bi_ref[...], preferred_element_type=F32)
        write, start = _scan_in_place(xr_ref, xi_ref, ar_ref[...], ai_ref[...], False)
        lax.fori_loop(0, CHUNK, write, start, unroll=8)
        y = lax.dot_general(xr_ref[...].astype(BF16), cr_ref[...], NT, preferred_element_type=F32)
        y += lax.dot_general(xi_ref[...].astype(BF16), ci_ref[...], NT, preferred_element_type=F32)
        y_ref[...] = y + d_ref[...] * u_ref[...]

    col, diag, vec = _ssm_specs(layer)
    return pl.pallas_call(
        body, grid=(N_LANE_GROUPS,),
        in_specs=[col(128), col(128), pl.BlockSpec((None, 1, 128), lambda g: (layer, 0, g)),
                  diag, diag, diag, diag, vec, vec],
        out_specs=[col(LANES_G), col(LANES_G), col(128)],
        out_shape=[jax.ShapeDtypeStruct((SEQ, SSM_GROUPS * SSM_STATE), F32)] * 2
        + [jax.ShapeDtypeStruct((SEQ, WIDTH), F32)],
        compiler_params=_cp(), name="ssm_fwd",
    )(u16, u, d, mats["b_re"], mats["b_im"], mats["c_re"], mats["c_im_neg"], mats["a_re"], mats["a_im"])


def _ssm_bwd(dy16, x_re, x_im, u16, mats, layer, d):
    def body(dy_ref, u_ref, d_ref, xr_ref, xi_ref, br_ref, bi_ref, cr_ref, ci_ref, ar_ref, ai_ref,
             du_ref, dar_ref, dai_ref, dbr_ref, dbi_ref, dcr_ref, dci_ref, lr_ref, li_ref):
        dy = dy_ref[...]
        lr_ref[...] = jnp.dot(dy, cr_ref[...], preferred_element_type=F32)
        li_ref[...] = jnp.dot(dy, ci_ref[...], preferred_element_type=F32)
        write, start = _scan_in_place(lr_ref, li_ref, ar_ref[...], -ai_ref[...], True)

        def rows(t):
            return pl.ds(pl.multiple_of(t * N_CHUNKS, N_CHUNKS), N_CHUNKS)

        def grad(acc, lam, xpr, xpi):
            return acc[0] + xpr * lam[0] + xpi * lam[1], acc[1] + xpr * lam[1] - xpi * lam[0]

        def down(tau, carry):
            lam = write(tau, carry[0])
            t = CHUNK - 2 - tau
            return lam, grad(carry[1], lam, xr_ref[rows(t), :], xi_ref[rows(t), :])

        zero = jnp.zeros((N_CHUNKS, LANES_G), F32)
        lam, acc = lax.fori_loop(0, CHUNK - 1, down, (start, (zero, zero)), unroll=5)
        lam = write(CHUNK - 1, lam)
        last = rows(CHUNK - 1)
        acc = grad(acc, lam, _shift_down(xr_ref[last, :], 1), _shift_down(xi_ref[last, :], 1))
        dar_ref[...] = jnp.sum(acc[0], axis=0, keepdims=True)
        dai_ref[...] = jnp.sum(acc[1], axis=0, keepdims=True)

        l_re, l_im = lr_ref[...].astype(BF16), li_ref[...].astype(BF16)
        du = lax.dot_general(l_re, br_ref[...], NT, preferred_element_type=F32)
        du += lax.dot_general(l_im, bi_ref[...], NT, preferred_element_type=F32)
        du_ref[...] = (du + dy.astype(F32) * d_ref[...]).astype(BF16)
        uv = u_ref[...]
        dbr_ref[...] = lax.dot_general(uv, l_re, TN, preferred_element_type=F32)
        dbi_ref[...] = lax.dot_general(uv, l_im, TN, preferred_element_type=F32)
        dcr_ref[...] = lax.dot_general(dy, xr_ref[...].astype(BF16), TN, preferred_element_type=F32)
        dci_ref[...] = lax.dot_general(dy, xi_ref[...].astype(BF16), TN, preferred_element_type=F32)

    col, diag, vec = _ssm_specs(layer)
    out_vec = pl.BlockSpec((1, LANES_G), lambda g: (0, g))
    out_blk = pl.BlockSpec((None, 128, LANES_G), lambda g: (g, 0, 0))
    sds = jax.ShapeDtypeStruct
    return pl.pallas_call(
        body, grid=(N_LANE_GROUPS,),
        in_specs=[col(128), col(128), pl.BlockSpec((None, 1, 128), lambda g: (layer, 0, g)),
                  col(LANES_G), col(LANES_G), diag, diag, diag, diag, vec, vec],
        out_specs=[col(128), out_vec, out_vec, out_blk, out_blk, out_blk, out_blk],
        out_shape=[sds((SEQ, WIDTH), BF16)] + [sds((1, SSM_GROUPS * SSM_STATE), F32)] * 2
        + [sds((N_LANE_GROUPS, 128, LANES_G), F32)] * 4,
        scratch_shapes=[pltpu.VMEM((SEQ, LANES_G), F32)] * 2, compiler_params=_cp(), name="ssm_bwd",
    )(dy16, u16, d, x_re, x_im, mats["b_re"], mats["b_im"], mats["c_re"], mats["c_im_neg"],
      mats["a_re"], mats["a_im"])


_GELU_C = math.sqrt(2.0 / math.pi)


def _gelu(y):
    return 0.5 * y * (1.0 + jnp.tanh(_GELU_C * (y + 0.044715 * (y * y * y))))


def _glu_fwd(y, wglu):
    tt = 512

    def body(y_ref, w_ref, z_ref):
        ys = _gelu(y_ref[...])
        a = jnp.dot(ys.astype(BF16), w_ref[...], preferred_element_type=F32)
        z_ref[...] = (ys * jax.nn.sigmoid(a)).astype(BF16)

    blk = pl.BlockSpec((tt, WIDTH), lambda i: (i, 0))
    return pl.pallas_call(body, grid=(SEQ // tt,), in_specs=[blk, _full((WIDTH, WIDTH))], out_specs=blk,
                          out_shape=jax.ShapeDtypeStruct((SEQ, WIDTH), BF16), compiler_params=_cp(),
                          name="glu_fwd")(y, wglu)


def _glu_bwd(y, wglu, dz, u):
    tt = 512

    def body(y_ref, w_ref, dz_ref, u_ref, dy_ref, ys_ref, da_ref, dd_ref):
        @pl.when(pl.program_id(0) == 0)
        def _():
            dd_ref[...] = jnp.zeros_like(dd_ref)

        yv = y_ref[...]
        t = jnp.tanh(_GELU_C * (yv + 0.044715 * (yv * yv * yv)))
        ys = 0.5 * yv * (1.0 + t)
        ysb = ys.astype(BF16)
        sg = jax.nn.sigmoid(jnp.dot(ysb, w_ref[...], preferred_element_type=F32))
        dz = dz_ref[...].astype(F32)
        da = (dz * ys * sg * (1.0 - sg)).astype(BF16)
        dys = dz * sg + lax.dot_general(da, w_ref[...], NT, preferred_element_type=F32)
        dy = dys * (0.5 * (1.0 + t) + 0.5 * yv * (1.0 - t * t) * _GELU_C * (1.0 + 3 * 0.044715 * (yv * yv)))
        dy_ref[...] = dy.astype(BF16)
        ys_ref[...] = ysb
        da_ref[...] = da
        dd_ref[...] += jnp.sum(dy * u_ref[...], axis=0, keepdims=True)

    blk = pl.BlockSpec((tt, WIDTH), lambda i: (i, 0))
    return pl.pallas_call(
        body, grid=(SEQ // tt,), in_specs=[blk, _full((WIDTH, WIDTH)), blk, blk],
        out_specs=[blk, blk, blk, _full((1, WIDTH))],
        out_shape=[jax.ShapeDtypeStruct((SEQ, WIDTH), BF16)] * 3 + [jax.ShapeDtypeStruct((1, WIDTH), F32)],
        compiler_params=_cp(), name="glu_bwd")(y, wglu, dz, u)


def _mix_specs(tt, layer):
    row = lambda w: pl.BlockSpec((tt, w), lambda i: (i, 0))
    gate = lambda j: pl.BlockSpec((tt, D_MODEL), lambda i: (i, j))
    wo = lambda j: pl.BlockSpec((D_MODEL, WIDTH), lambda i: (0, j))
    return [row(D_MODEL), row(WIDTH), row(WIDTH), row(WIDTH), gate(0), gate(1), gate(2),
            pl.BlockSpec((None, 1, GATE_W), lambda i: (layer, 0, 0)), wo(0), wo(1), wo(2),
            _full((D_MODEL, D_MODEL))]


def _mix_branches(o_ref, c_ref, z_ref, g_refs, b_ref, wa_ref, wc_ref, ws_ref):
    ys = [lax.dot_general(r[...], w[...], NT, preferred_element_type=F32)
          for r, w in ((o_ref, wa_ref), (c_ref, wc_ref), (z_ref, ws_ref))]
    gates = [jax.nn.sigmoid(g_refs[j][...] + b_ref[:, D_MODEL * j:D_MODEL * (j + 1)]) for j in range(3)]
    return ys, gates


def _mix_fwd(x, o, cv, z, glog, b_gate, layer, wbt, wmix, tie):
    tt = 256

    def body(x_ref, o_ref, c_ref, z_ref, g0, g1, g2, b_ref, wa_ref, wc_ref, ws_ref, wm_ref, tie_ref, x1_ref):
        ys, gates = _mix_branches(o_ref, c_ref, z_ref, (g0, g1, g2), b_ref, wa_ref, wc_ref, ws_ref)
        merged = gates[0] * ys[0] + gates[1] * ys[1] + gates[2] * ys[2]
        x1_ref[...] = x_ref[...] + jnp.dot(merged.astype(BF16), wm_ref[...], preferred_element_type=F32)

    return pl.pallas_call(
        body, grid=(SEQ // tt,), in_specs=_mix_specs(tt, layer) + [ANY],
        out_specs=pl.BlockSpec((tt, D_MODEL), lambda i: (i, 0)),
        out_shape=jax.ShapeDtypeStruct((SEQ, D_MODEL), F32), compiler_params=_cp(), name="mix_fwd",
    )(x, o, cv, z, glog, glog, glog, b_gate, wbt, wbt, wbt, wmix, tie)


def _mix_bwd(dx1, o, cv, z, glog, b_gate, layer, wbt, wmix, tie):
    tt = 256

    def body(dx_ref, o_ref, c_ref, z_ref, g0, g1, g2, b_ref, wa_ref, wc_ref, ws_ref, wm_ref, tie_ref,
             mg_ref, dya_ref, dyc_ref, dys_ref, do_ref, dc_ref, dz_ref, dgl_ref, db_ref):
        @pl.when(pl.program_id(0) == 0)
        def _():
            db_ref[...] = jnp.zeros_like(db_ref)

        ys, gates = _mix_branches(o_ref, c_ref, z_ref, (g0, g1, g2), b_ref, wa_ref, wc_ref, ws_ref)
        mg_ref[...] = (gates[0] * ys[0] + gates[1] * ys[1] + gates[2] * ys[2]).astype(BF16)
        dm = lax.dot_general(dx_ref[...].astype(BF16), wm_ref[...], NT, preferred_element_type=F32)
        for j, (dy_ref, w_ref, d_ref) in enumerate(((dya_ref, wa_ref, do_ref), (dyc_ref, wc_ref, dc_ref),
                                                    (dys_ref, ws_ref, dz_ref))):
            dy = (dm * gates[j]).astype(BF16)
            dy_ref[...] = dy
            d_ref[...] = jnp.dot(dy, w_ref[...], preferred_element_type=F32)
            dgl = dm * ys[j] * gates[j] * (1.0 - gates[j])
            dgl_ref[:, D_MODEL * j:D_MODEL * (j + 1)] = dgl.astype(BF16)
            db_ref[:, D_MODEL * j:D_MODEL * (j + 1)] += jnp.sum(dgl, axis=0, keepdims=True)

    row = lambda w: pl.BlockSpec((tt, w), lambda i: (i, 0))
    sds = jax.ShapeDtypeStruct
    return pl.pallas_call(
        body, grid=(SEQ // tt,), in_specs=_mix_specs(tt, layer) + [ANY],
        out_specs=[row(D_MODEL)] * 4 + [row(WIDTH)] * 3 + [row(GATE_W), _full((1, GATE_W))],
        out_shape=[sds((SEQ, D_MODEL), BF16)] * 4 + [sds((SEQ, WIDTH), F32)] * 3
        + [sds((SEQ, GATE_W), BF16), sds((1, GATE_W), F32)],
        compiler_params=_cp(), name="mix_bwd",
    )(dx1, o, cv, z, glog, glog, glog, b_gate, wbt, wbt, wbt, wmix, tie)


def _ffn_out_fwd(x1, gu, wout, tie):
    tt = 256

    def body(x_ref, gt_ref, up_ref, w_ref, tie_ref, o_ref):
        gt = gt_ref[...]
        act = (gt * jax.nn.sigmoid(gt) * up_ref[...]).astype(BF16)
        o_ref[...] = x_ref[...] + jnp.dot(act, w_ref[...], preferred_element_type=F32)

    return pl.pallas_call(
        body, grid=(SEQ // tt,),
        in_specs=[pl.BlockSpec((tt, D_MODEL), lambda i: (i, 0)), pl.BlockSpec((tt, FFN_H), lambda i: (i, 0)),
                  pl.BlockSpec((tt, FFN_H), lambda i: (i, 1)), _full((FFN_H, D_MODEL)), ANY],
        out_specs=pl.BlockSpec((tt, D_MODEL), lambda i: (i, 0)),
        out_shape=jax.ShapeDtypeStruct((SEQ, D_MODEL), F32), compiler_params=_cp(), name="ffn_out_fwd",
    )(x1, gu, gu, wout, tie)


def _ffn_out_bwd(dx2, gu, wout, tie):
    tt = 256

    def body(dx_ref, gt_ref, up_ref, w_ref, tie_ref, dgu_ref, act_ref):
        gt, up = gt_ref[...], up_ref[...]
        sg = jax.nn.sigmoid(gt)
        silu = gt * sg
        act_ref[...] = (silu * up).astype(BF16)
        dact = lax.dot_general(dx_ref[...].astype(BF16), w_ref[...], NT, preferred_element_type=F32)
        dgu_ref[:, :FFN_H] = (dact * up * (sg * (1.0 + gt * (1.0 - sg)))).astype(BF16)
        dgu_ref[:, FFN_H:] = (dact * silu).astype(BF16)

    return pl.pallas_call(
        body, grid=(SEQ // tt,),
        in_specs=[pl.BlockSpec((tt, D_MODEL), lambda i: (i, 0)), pl.BlockSpec((tt, FFN_H), lambda i: (i, 0)),
                  pl.BlockSpec((tt, FFN_H), lambda i: (i, 1)), _full((FFN_H, D_MODEL)), ANY],
        out_specs=[pl.BlockSpec((tt, 2 * FFN_H), lambda i: (i, 0)), pl.BlockSpec((tt, FFN_H), lambda i: (i, 0))],
        out_shape=[jax.ShapeDtypeStruct((SEQ, 2 * FFN_H), BF16), jax.ShapeDtypeStruct((SEQ, FFN_H), BF16)],
        compiler_params=_cp(), name="ffn_out_bwd",
    )(dx2, gu, gu, wout, tie)


def _loss_head(x, g, target):
    tt = 256

    def body(x_ref, g_ref, t_ref, loss_ref, dx_ref, dg_ref):
        @pl.when(pl.program_id(0) == 0)
        def _():
            loss_ref[...] = jnp.zeros_like(loss_ref)
            dg_ref[...] = jnp.zeros_like(dg_ref)

        xv = x_ref[...]
        r = lax.rsqrt(jnp.mean(xv * xv, axis=-1, keepdims=True) + NORM_EPS)
        xh = xv * r
        err = xh * g_ref[...] - t_ref[...]
        loss_ref[...] += 0.5 * jnp.sum(jnp.mean(err * err, axis=-1, keepdims=True))
        dy = err * (1.0 / D_MODEL)
        gy = dy * g_ref[...]
        dx_ref[...] = r * (gy - xh * jnp.mean(gy * xh, axis=-1, keepdims=True))
        dg_ref[...] += jnp.sum(dy * xh, axis=0, keepdims=True)

    row = pl.BlockSpec((tt, D_MODEL), lambda i: (i, 0))
    return pl.pallas_call(
        body, grid=(SEQ // tt,), in_specs=[row, _full((1, D_MODEL)), row],
        out_specs=[_full((1, 128)), row, _full((1, D_MODEL))],
        out_shape=[jax.ShapeDtypeStruct((1, 128), F32), jax.ShapeDtypeStruct((SEQ, D_MODEL), F32),
                   jax.ShapeDtypeStruct((1, D_MODEL), F32)],
        compiler_params=_cp(), name="loss_head")(x, g, target)


def _sum_parts(parts, lo, hi, tr, tie):
    n_groups, n_parts, rows, cols = parts.shape

    def body(p_ref, tie_ref, o_ref):
        g = p_ref[0].astype(F32)
        for k in range(1, n_parts):
            g = g + p_ref[k].astype(F32)
        o_ref[...] = g

    return pl.pallas_call(
        body, grid=(hi - lo, rows // tr),
        in_specs=[pl.BlockSpec((None, n_parts, tr, cols), lambda l, i: (l + lo, 0, i, 0)), ANY],
        out_specs=pl.BlockSpec((None, tr, cols), lambda l, i: (l, i, 0)),
        out_shape=jax.ShapeDtypeStruct((hi - lo, rows, cols), F32), compiler_params=_cp(), name="sum_parts",
    )(parts, tie)


def _adamw(parts, w, m, v, tr, name, groups=None, fill=None, tie=None):
    n_groups, rows, cols = w.shape
    n_parts = parts.shape[1]
    lo, hi = groups if groups is not None else (0, n_groups)

    def body(p_ref, w_ref, m_ref, v_ref, *rest):
        g_ref, d_ref, nm_ref, nv_ref = rest[-4:]
        g = p_ref[0].astype(F32)
        for k in range(1, n_parts):
            g = g + p_ref[k].astype(F32)
        nm = B1 * m_ref[...] + (1.0 - B1) * g
        nv = B2 * v_ref[...] + (1.0 - B2) * (g * g)
        m_hat = nm / (1.0 - B1 ** STEP)
        v_hat = nv / (1.0 - B2 ** STEP)
        g_ref[...] = g
        d_ref[...] = -LR * (m_hat / (jnp.sqrt(v_hat) + ADAM_EPS) + WD * w_ref[...])
        nm_ref[...] = nm
        nv_ref[...] = nv

    blk = pl.BlockSpec((None, tr, cols), lambda l, i: (l + lo, i, 0))
    p_lo = lo if parts.shape[0] == n_groups else 0
    extra = ([] if fill is None else list(fill)) + ([] if tie is None else [tie])
    return pl.pallas_call(
        body, grid=(hi - lo, rows // tr),
        in_specs=[pl.BlockSpec((None, n_parts, tr, cols), lambda l, i: (l + p_lo, 0, i, 0)), blk, blk, blk]
        + [ANY] * len(extra),
        out_specs=[blk] * 4, out_shape=[jax.ShapeDtypeStruct((n_groups, rows, cols), F32)] * 4,
        input_output_aliases={} if fill is None else {4 + j: j for j in range(4)},
        compiler_params=_cp(), name=name)(parts, w, m, v, *extra)


def _split_start(name, arrays, n_sems, plan, after=None):
    n = len(arrays)
    order = [] if after is None else [after]
    n_in = n + len(order)

    def body(*refs):
        ins, send_sems, recv_sems, token = refs[:n], refs[n_in], refs[n_in + 1], refs[-1]
        for src, dst, k, to in plan(ins)[0]:
            pltpu.make_async_remote_copy(src_ref=src, dst_ref=dst, send_sem=send_sems.at[k], recv_sem=recv_sems.at[k],
                                         device_id=to, device_id_type=MESH_ID).start()
        token[...] = jnp.zeros_like(token)

    outs = pl.pallas_call(
        body, name=name,
        out_shape=(pltpu.SemaphoreType.DMA((n_sems,)), pltpu.SemaphoreType.DMA((n_sems,)),
                   *[pltpu.HBM(a.shape, a.dtype) for a in arrays], jax.ShapeDtypeStruct((8, 128), F32)),
        in_specs=[HBM] * n + [ANY] * len(order),
        out_specs=(SEM, SEM, *[HBM] * n, pl.BlockSpec(memory_space=pltpu.VMEM)),
        input_output_aliases={i: 2 + i for i in range(n)},
        compiler_params=pltpu.CompilerParams(has_side_effects=EFFECT),
    )(*[pltpu.with_memory_space_constraint(a, pltpu.HBM) for a in arrays], *order)
    return outs[0], outs[1], list(outs[2:2 + n]), outs[-1]


def _split_wait(name, arrays, send_sems, recv_sems, after, plan):
    n = len(arrays)

    def body(*refs):
        ins, s_sems, r_sems = refs[:n], refs[n], refs[n + 1]
        sends, arrivals = plan(ins)
        x, y, c = lax.axis_index("x"), lax.axis_index("y"), lax.axis_index("c")
        for src, dst, k, to in sends:
            pltpu.make_async_remote_copy(src_ref=src, dst_ref=dst, send_sem=s_sems.at[k], recv_sem=r_sems.at[k],
                                         device_id=to, device_id_type=MESH_ID).wait_send()
        for dst, k in arrivals:
            pltpu.make_async_remote_copy(src_ref=dst, dst_ref=dst, send_sem=s_sems.at[k], recv_sem=r_sems.at[k],
                                         device_id=(x, y, c), device_id_type=MESH_ID).wait_recv()

    return pl.pallas_call(
        body, name=name, out_shape=[pltpu.HBM(a.shape, a.dtype) for a in arrays],
        in_specs=[HBM] * n + [SEM, SEM, ANY], out_specs=[HBM] * n,
        input_output_aliases={i: i for i in range(n)},
        compiler_params=pltpu.CompilerParams(has_side_effects=EFFECT),
    )(*arrays, send_sems, recv_sems, after)


def _chips():
    x, y, c = lax.axis_index("x"), lax.axis_index("y"), lax.axis_index("c")
    return x, y, c, [(1 - x, y), (x, 1 - y), (1 - x, 1 - y)]


def _plan_gather_chips(refs):
    x, y, c, chips = _chips()
    me = 4 * x + 2 * y + c
    n = len(refs) // 2
    sends, arrivals = [], []
    for i in range(n):
        src, land = refs[i], refs[n + i]
        sends.append((src, land.at[me], 4 * i, (x, y, 1 - c)))
        arrivals.append((land.at[4 * x + 2 * y + 1 - c], 4 * i))
        for j, (px, py) in enumerate(chips):
            sends.append((src, land.at[me], 4 * i + 1 + j, (px, py, c)))
            arrivals.append((land.at[4 * px + 2 * py + c], 4 * i + 1 + j))
    return sends, arrivals


def _plan_gather_pass(refs):
    x, y, c, chips = _chips()
    sends, arrivals = [], []
    for i in range(len(refs)):
        for j, (px, py) in enumerate(chips):
            slot = refs[i].at[4 * px + 2 * py + c]
            sends.append((slot, slot, 3 * i + j, (x, y, 1 - c)))
            arrivals.append((refs[i].at[4 * px + 2 * py + 1 - c], 3 * i + j))
    return sends, arrivals


def _plan_scatter_pair(refs):
    x, y, c = lax.axis_index("x"), lax.axis_index("y"), lax.axis_index("c")
    n = len(refs) // 2
    sends, arrivals = [], []
    for i in range(n):
        for q in range(4):
            sends.append((refs[i].at[q, 1 - c], refs[n + i].at[q], 4 * i + q, (x, y, 1 - c)))
            arrivals.append((refs[n + i].at[q], 4 * i + q))
    return sends, arrivals


def _plan_scatter_chips(layer):
    def plan(refs):
        x, y, c, chips = _chips()
        n = len(refs) // 2
        sends, arrivals = [], []
        for i in range(n):
            for j, (px, py) in enumerate(chips):
                sends.append((refs[i].at[2 * px + py], refs[n + i].at[layer, 2 * x + y], 3 * i + j, (px, py, c)))
                arrivals.append((refs[n + i].at[layer, 2 * px + py], 3 * i + j))
        return sends, arrivals

    return plan


def _pair_sum(parts4, from_pair, landing, layer, core, tr, name):
    _, _, rows, cols = parts4.shape

    def body(c_ref, p_ref, s_ref, l_ref, sum_ref, land_ref):
        v = (p_ref[...].astype(F32) + s_ref[...].astype(F32)).astype(BF16)
        sum_ref[...] = v
        land_ref[...] = v

    blk = pl.BlockSpec((None, tr, cols), lambda q, i, c_ref: (q, i, 0))
    return pl.pallas_call(
        body,
        grid_spec=pltpu.PrefetchScalarGridSpec(
            num_scalar_prefetch=1, grid=(4, rows // tr),
            in_specs=[pl.BlockSpec((None, None, tr, cols), lambda q, i, c_ref: (q, c_ref[0], i, 0)), blk, ANY],
            out_specs=[blk, pl.BlockSpec((None, None, tr, cols), lambda q, i, c_ref: (layer, q, i, 0))]),
        out_shape=[jax.ShapeDtypeStruct((4, rows, cols), BF16), jax.ShapeDtypeStruct(landing.shape, BF16)],
        input_output_aliases={3: 1}, compiler_params=_cp(), name=name,
    )(core, parts4, from_pair, landing)


def _travel_layout(t):
    tr = lambda a: jnp.swapaxes(a, 1, 2)
    branch = jnp.concatenate([tr(t["w_attn_o"]), tr(t["w_conv_o"]), tr(t["w_ssm_o"])], axis=2)
    return [tr(t["w_in"]), tr(t["w_ffn_in"]), t["w_ffn_out"], t["w_mix_o"], branch, t["w_ssm_glu"]]


TRANSPOSED = {"win_t": ("w_in",), "wffn_t": ("w_ffn_in",), "branch_t": ("w_attn_o", "w_conv_o", "w_ssm_o")}
AS_IS = {"wout": "w_ffn_out", "wmix": "w_mix_o", "wglu": "w_ssm_glu"}


def _adamw_sharded(lands, lo, hi, big, rows_tile, prev, tie, tag):
    res = {}
    for j, (kind, _, cols) in enumerate(KINDS):
        if kind in AS_IS:
            name = AS_IS[kind]
            res[name] = _adamw(lands[j], big["w"][name], big["m"][name], big["v"][name], rows_tile[kind],
                               f"adamw_{tag}_{name}", groups=(lo, hi), fill=prev and prev[name], tie=tie)
            continue
        g_t = _sum_parts(lands[j], lo if lands[j].shape[0] == DEPTH else 0, hi if lands[j].shape[0] == DEPTH else hi - lo,
                         rows_tile[kind], tie)
        names = TRANSPOSED[kind]
        width = cols // len(names)
        for k, name in enumerate(names):
            g = jnp.swapaxes(g_t[:, :, width * k:width * (k + 1)], 1, 2)
            res[name] = _adamw(g[:, None], big["w"][name], big["m"][name], big["v"][name], min(g.shape[1], 256),
                               f"adamw_{tag}_{name}", groups=(lo, hi), fill=prev and prev[name])
    return res


def _embed(t):
    eye = jnp.eye(8, dtype=t.dtype)
    t = t.reshape(DEPTH, N_LANE_GROUPS, 8, SSM_GROUP, SSM_STATE)
    return (t[:, :, :, :, None, :] * eye[None, None, :, None, :, None]).reshape(DEPTH, N_LANE_GROUPS, 128, LANES_G)


def _diag_blocks(t):
    t = t.reshape(DEPTH, N_LANE_GROUPS, 8, SSM_GROUP, 8, SSM_STATE)
    return jnp.einsum("lgahap->lgahp", t).reshape(DEPTH, SSM_GROUPS, SSM_GROUP, SSM_STATE)


def _rope_tabs():
    pos = jnp.arange(SEQ, dtype=F32)
    inv_freq = ROPE_THETA ** (-jnp.arange(0, ROT_DIM, 2, dtype=F32) / ROT_DIM)
    ang = pos[:, None] * inv_freq[None, :]
    cos, sin = jnp.cos(ang), jnp.sin(ang)
    one, zero = jnp.ones((SEQ, HEAD_DIM - ROT_DIM), F32), jnp.zeros((SEQ, HEAD_DIM - ROT_DIM), F32)
    z8 = jnp.zeros((SEQ, 8), F32)
    head = lambda *p: jnp.tile(jnp.concatenate(p, axis=1), (1, 2))
    return head(cos, cos, one), head(-sin, z8, zero), head(z8, sin, zero)


def _ssm_mats(sp):
    lr, li, bbr, bbi = _ssm_prep(sp["a_re"], sp["a_im"], sp["log_dt"], sp["bt_re"], sp["bt_im"])
    lanes = SSM_GROUPS * SSM_STATE
    return {
        "a_re": lr.reshape(DEPTH, 1, lanes), "a_im": li.reshape(DEPTH, 1, lanes),
        "b_re": _embed(bbr).astype(BF16), "b_im": _embed(bbi).astype(BF16),
        "c_re": _embed(sp["c_re"]).astype(BF16), "c_im_neg": _embed(-sp["c_im"]).astype(BF16),
    }


def _layer_fwd(x, i, w, rp, mats, tabs, tie, hooks):
    q, kv, cbx, u, u16, glog, h = _rms_mm_in(x, rp["norm_mix"][i], w["win_t"], tie)
    o = _attn_fwd(q, kv, tabs, rp["attn_sinks"][i])
    cv = _conv_fwd(cbx, rp["conv_w"], i)
    u16, u = _scan_order(u16), _scan_order(u)
    x_re, x_im, y = _ssm_fwd(u16, u, mats, i, rp["ssm_d"])
    z = _time_order(_glu_fwd(y, w["wglu"]))
    x1 = _mix_fwd(x, o, cv, z, glog, rp["b_gate"], i, w["branch_t"], w["wmix"], hooks["early"](z))
    hooks["pre_ffn"](x1)
    gu, h2 = _rms_mm_ffn(x1, rp["norm_ffn"][i], w["wffn_t"])
    x2 = _ffn_out_fwd(x1, gu, w["wout"], hooks["mid"](h2))
    kept = dict(x=x, q=q, kv=kv, cbx=cbx, u=u, u16=u16, glog=glog, h=h, o=o, cv=cv, z=z, y=y,
                x_re=x_re, x_im=x_im, x1=x1, gu=gu, h2=h2)
    return x2, kept


def _layer_bwd(dx2, k, i, w, rp, mats, tabs, tie, hooks):
    dgu, act = _ffn_out_bwd(dx2, k["gu"], w["wout"], tie)
    g_wout = _mm_tn(act, dx2, tm=FFN_H // 2, tn=512, name="mm_tn_ffn_out")
    g_wffn_t = _mm_tn(dgu, k["h2"], tm=FFN_H // 2, tn=1024, name="mm_tn_ffn_in")
    dx1, d_norm_ffn = _mm_rmsbwd([dgu], w["wffn_t"], k["x1"], rp["norm_ffn"][i], dx2, "mm_rmsbwd_ffn")

    mg, dya, dyc, dys, do, dcv, dz, dgl, db_gate = _mix_bwd(
        dx1, k["o"], k["cv"], k["z"], k["glog"], rp["b_gate"], i, w["branch_t"], w["wmix"],
        hooks["mid"]((g_wffn_t, g_wout, d_norm_ffn)))
    g_wmix = _mm_tn(mg, dx1, tm=512, tn=512, name="mm_tn_mix")
    g_branch_t = _tn_branches((dya, dyc, dys), (k["o"], k["cv"], k["z"]))

    dy16, ys16, da16, dd = _glu_bwd(k["y"], w["wglu"], _scan_order(dz), k["u"])
    g_wglu = _mm_tn(ys16, da16, tm=256, tn=512, name="mm_tn_glu")
    du, da_re, da_im, db_re, db_im, dc_re, dc_im = _ssm_bwd(dy16, k["x_re"], k["x_im"], k["u16"], mats, i,
                                                             rp["ssm_d"])
    du = _time_order(du)

    dcb, dcc, dcx, d_conv_w = _conv_bwd(k["cbx"], rp["conv_w"], i, dcv, hooks["late"](du))
    dq, dkv, d_sinks = _attn_bwd(k["q"], k["kv"], tabs, rp["attn_sinks"][i], do)

    pieces = [dq, dkv, dcb, dcc, dcx, du, dgl]
    g_win_t = _tn_pieces(pieces, k["h"])
    dx, d_norm_mix = _mm_rmsbwd(pieces, w["win_t"], k["x"], rp["norm_mix"][i], dx1, "mm_rmsbwd_in")

    grads = [g_win_t, g_wffn_t, g_wout, g_wmix, g_branch_t, g_wglu]
    small = dict(norm_mix=d_norm_mix, b_gate=db_gate, attn_sinks=d_sinks, ssm_d=dd, norm_ffn=d_norm_ffn,
                 conv_w=d_conv_w, da_re=da_re, da_im=da_im, db_re=db_re, db_im=db_im, dc_re=dc_re, dc_im=dc_im)
    return dx, grads, small


def _replicated_grads(sg, sp):
    stack = lambda name: jnp.stack([sg[i][name] for i in range(DEPTH)])
    cots = (stack("da_re").reshape(DEPTH, *_GS), stack("da_im").reshape(DEPTH, *_GS),
            _diag_blocks(stack("db_re")), _diag_blocks(stack("db_im")))
    d_a_re, d_a_im, d_log_dt, d_bt_re, d_bt_im = _ssm_prep_bwd(
        sp["a_re"], sp["a_im"], sp["log_dt"], sp["bt_re"], sp["bt_im"], cots)
    sgrads = {"norm_mix": stack("norm_mix"), "b_gate": stack("b_gate"),
              "attn_sinks": stack("attn_sinks")[:, :, :N_Q_HEADS], "ssm_a_re": d_a_re, "ssm_a_im": d_a_im,
              "ssm_b_re": jnp.swapaxes(d_bt_re, 2, 3), "ssm_b_im": jnp.swapaxes(d_bt_im, 2, 3),
              "ssm_c_re": _diag_blocks(stack("dc_re")), "ssm_c_im": -_diag_blocks(stack("dc_im")),
              "ssm_d": stack("ssm_d"), "ssm_log_dt": d_log_dt, "norm_ffn": stack("norm_ffn")}
    return sgrads, stack("conv_w")[:, :3]


def kernel(x, norm_mix, w_in, b_gate, attn_sinks, w_attn_o, conv_w, w_conv_o, ssm_a_re, ssm_a_im, ssm_b_re, ssm_b_im, ssm_c_re, ssm_c_im, ssm_d, ssm_log_dt, w_ssm_glu, w_ssm_o, w_mix_o, norm_ffn, w_ffn_in, w_ffn_out, norm_final, loss_target, m_norm_mix, m_w_in, m_b_gate, m_attn_sinks, m_w_attn_o, m_conv_w, m_w_conv_o, m_ssm_a_re, m_ssm_a_im, m_ssm_b_re, m_ssm_b_im, m_ssm_c_re, m_ssm_c_im, m_ssm_d, m_ssm_log_dt, m_w_ssm_glu, m_w_ssm_o, m_w_mix_o, m_norm_ffn, m_w_ffn_in, m_w_ffn_out, m_norm_final, v_norm_mix, v_w_in, v_b_gate, v_attn_sinks, v_w_attn_o, v_conv_w, v_w_conv_o, v_ssm_a_re, v_ssm_a_im, v_ssm_b_re, v_ssm_b_im, v_ssm_c_re, v_ssm_c_im, v_ssm_d, v_ssm_log_dt, v_w_ssm_glu, v_w_ssm_o, v_w_mix_o, v_norm_ffn, v_w_ffn_in, v_w_ffn_out, v_norm_final):
    big = {"w": dict(w_in=w_in, w_attn_o=w_attn_o, w_conv_o=w_conv_o, w_ssm_glu=w_ssm_glu, w_ssm_o=w_ssm_o,
                     w_mix_o=w_mix_o, w_ffn_in=w_ffn_in, w_ffn_out=w_ffn_out),
           "m": dict(w_in=m_w_in, w_attn_o=m_w_attn_o, w_conv_o=m_w_conv_o, w_ssm_glu=m_w_ssm_glu,
                     w_ssm_o=m_w_ssm_o, w_mix_o=m_w_mix_o, w_ffn_in=m_w_ffn_in, w_ffn_out=m_w_ffn_out),
           "v": dict(w_in=v_w_in, w_attn_o=v_w_attn_o, w_conv_o=v_w_conv_o, w_ssm_glu=v_w_ssm_glu,
                     w_ssm_o=v_w_ssm_o, w_mix_o=v_w_mix_o, w_ffn_in=v_w_ffn_in, w_ffn_out=v_w_ffn_out)}
    small = {"w": dict(norm_mix=norm_mix, b_gate=b_gate, attn_sinks=attn_sinks, ssm_a_re=ssm_a_re,
                       ssm_a_im=ssm_a_im, ssm_b_re=ssm_b_re, ssm_b_im=ssm_b_im, ssm_c_re=ssm_c_re,
                       ssm_c_im=ssm_c_im, ssm_d=ssm_d, ssm_log_dt=ssm_log_dt, norm_ffn=norm_ffn),
             "m": dict(norm_mix=m_norm_mix, b_gate=m_b_gate, attn_sinks=m_attn_sinks, ssm_a_re=m_ssm_a_re,
                       ssm_a_im=m_ssm_a_im, ssm_b_re=m_ssm_b_re, ssm_b_im=m_ssm_b_im, ssm_c_re=m_ssm_c_re,
                       ssm_c_im=m_ssm_c_im, ssm_d=m_ssm_d, ssm_log_dt=m_ssm_log_dt, norm_ffn=m_norm_ffn),
             "v": dict(norm_mix=v_norm_mix, b_gate=v_b_gate, attn_sinks=v_attn_sinks, ssm_a_re=v_ssm_a_re,
                       ssm_a_im=v_ssm_a_im, ssm_b_re=v_ssm_b_re, ssm_b_im=v_ssm_b_im, ssm_c_re=v_ssm_c_re,
                       ssm_c_im=v_ssm_c_im, ssm_d=v_ssm_d, ssm_log_dt=v_ssm_log_dt, norm_ffn=v_norm_ffn)}
    finals = {"w": norm_final, "m": m_norm_final, "v": v_norm_final}
    convs = {"w": conv_w, "m": m_conv_w, "v": v_conv_w}
    mine = 4 * lax.axis_index("x") + 2 * lax.axis_index("y") + lax.axis_index("c")

    stacked16 = _travel_layout({name: a.astype(BF16) for name, a in big["w"].items()})
    rp = {"norm_mix": norm_mix[:, None], "norm_ffn": norm_ffn[:, None], "attn_sinks": attn_sinks[:, None],
          "b_gate": b_gate[:, None], "ssm_d": ssm_d[:, None]}
    sp = {"a_re": ssm_a_re, "a_im": ssm_a_im, "log_dt": ssm_log_dt[:, :, None],
          "bt_re": jnp.swapaxes(ssm_b_re, 2, 3), "bt_im": jnp.swapaxes(ssm_b_im, 2, 3),
          "c_re": ssm_c_re, "c_im": ssm_c_im}
    rows_tile = {"win_t": 368, "wffn_t": 352, "wout": 176, "wmix": 128, "branch_t": 128, "wglu": 64}
    core = lax.axis_index("c").astype(jnp.int32).reshape(1)
    no_tie = jnp.zeros((8, 128), F32)

    def place_own(srcs):
        return [lax.dynamic_update_slice(lax.empty((N_DEV,) + s.shape, s.dtype), s[None], (mine, 0, 0)) for s in srcs]

    def gather_chips(tag, i, kinds, after, extra=()):
        srcs = [stacked16[j][i] for j in kinds] + list(extra)
        s_sems, r_sems, arrays, token = _split_start(
            f"gather_chips_start_{tag}", srcs + place_own(srcs), 4 * len(srcs), _plan_gather_chips, after)
        return (tag, s_sems, r_sems, arrays), token

    def gather_pass(state, after):
        tag, s_sems, r_sems, arrays = state
        arrays = _split_wait(f"gather_chips_wait_{tag}", arrays, s_sems, r_sems, after, _plan_gather_chips)
        n = len(arrays) // 2
        s_sems, r_sems, lands, token = _split_start(
            f"gather_pass_start_{tag}", list(arrays[n:]), 3 * n, _plan_gather_pass)
        return (tag, s_sems, r_sems, lands), token

    def gather_done(state, after, kinds):
        tag, s_sems, r_sems, lands = state
        lands = _split_wait(f"gather_pass_wait_{tag}", lands, s_sems, r_sems, after, _plan_gather_pass)
        named = {KINDS[j][0]: a.reshape(N_DEV * KINDS[j][1], KINDS[j][2]) for a, j in zip(lands, kinds)}
        return named, list(lands[len(kinds):])

    all_kinds, mixer_kinds, ffn_kinds = tuple(range(len(KINDS))), (0, 3, 4, 5), (1, 2)
    no_hooks = {name: (lambda value: no_tie) for name in ("early", "pre_ffn", "mid", "late")}
    state, _ = gather_chips("0m", 0, mixer_kinds, None, extra=[jnp.pad(conv_w.reshape(6, 128), ((0, 2), (0, 0)))])
    mats = _ssm_mats(sp)
    tabs = _rope_tabs()
    state, _ = gather_pass(state, mats["c_im_neg"])
    ffn_state, tie = gather_chips("0f", 0, ffn_kinds, state[3][0])
    w_next, (conv_all,) = gather_done(state, tabs[2], mixer_kinds)
    conv_full = conv_all[:, :6].reshape(N_DEV, DEPTH, 3, 64).transpose(1, 2, 0, 3).reshape(DEPTH, 3, WIDTH)
    rp["conv_w"] = jnp.pad(conv_full, ((0, 0), (0, 5), (0, 0)))

    act = x[0]
    weights, kept = [], []
    for i in range(DEPTH):
        w_i, hooks, held = w_next, dict(no_hooks), {}
        if i == 0:
            def early(value, held=held):
                held["ffn"], token = gather_pass(ffn_state, value)
                return token

            def pre_ffn(value, w_i=w_i, held=held):
                w_i.update(gather_done(held["ffn"], value, ffn_kinds)[0])

            hooks.update(early=early, pre_ffn=pre_ffn)
        if i + 1 < DEPTH:
            state, tie = gather_chips(str(i + 1), i + 1, all_kinds, w_i["win_t"] if i > 0 else tie)

            def mid(value, state=state, held=held):
                held["next"], token = gather_pass(state, value)
                return token

            hooks.update(mid=mid)
        elif i > 0:
            tie = no_tie
        act, k = _layer_fwd(act, i, w_i, rp, mats, tabs, tie, hooks)
        if i + 1 < DEPTH:
            w_next, _ = gather_done(held["next"], act, all_kinds)
        weights.append(w_i)
        kept.append(k)
    loss_row, dx, d_norm_final = _loss_head(act, norm_final[None], loss_target[0])
    loss = lax.psum(loss_row[0, 0], ("x", "y", "c"))

    landings = [lax.empty((DEPTH, 4, r, c), BF16) for _, r, c in KINDS]
    landings0 = [lax.empty((1, 4, r, c), BF16) for _, r, c in KINDS]

    def scatter_pair(tag, kinds, grads, after):
        parts4 = [g.reshape(4, 2, KINDS[j][1], KINDS[j][2]) for g, j in zip(grads, kinds)]
        zones = [lax.empty((4, KINDS[j][1], KINDS[j][2]), BF16) for j in kinds]
        s_sems, r_sems, arrays, token = _split_start(
            f"scatter_pair_start_{tag}", parts4 + zones, 4 * len(kinds), _plan_scatter_pair, after)
        return (tag, kinds, s_sems, r_sems, arrays), token

    def scatter_chips(state, lands, slot, after):
        tag, kinds, s_sems, r_sems, arrays = state
        arrays = _split_wait(f"scatter_pair_wait_{tag}", arrays, s_sems, r_sems, after, _plan_scatter_pair)
        n = len(kinds)
        sums, mine_lands = [], []
        for k, j in enumerate(kinds):
            name = KINDS[j][0]
            chip_sum, land = _pair_sum(arrays[k], arrays[n + k], lands[j], slot, core, rows_tile[name],
                                       f"pair_sum_{name}")
            sums.append(chip_sum)
            mine_lands.append(land)
        s_sems, r_sems, arrays, token = _split_start(
            f"scatter_chips_start_{tag}", sums + mine_lands, 3 * n, _plan_scatter_chips(slot))
        return (tag, kinds, slot, s_sems, r_sems, arrays), token

    def scatter_done(state, lands, after):
        tag, kinds, slot, s_sems, r_sems, arrays = state
        arrays = _split_wait(f"scatter_chips_wait_{tag}", arrays, s_sems, r_sems, after, _plan_scatter_chips(slot))
        lands = list(lands)
        for k, j in enumerate(kinds):
            lands[j] = arrays[len(kinds) + k]
        return lands

    sg = [None] * DEPTH
    pending, tie = None, no_tie
    for i in reversed(range(DEPTH)):
        hooks, held = dict(no_hooks), {}
        if pending is not None:
            def mid(value, i=i, pending=pending, held=held):
                held["chips"], token = scatter_chips(pending, landings, i + 1, value[2])
                if i == 0:
                    held["ffn_pair"], token = scatter_pair("0f", ffn_kinds, value[:2], token)
                return token

            hooks.update(mid=mid)
        if i == 0:
            def late(value, held=held):
                held["ffn_chips"], token = scatter_chips(held["ffn_pair"], landings0, 0, value)
                return token

            hooks.update(late=late)
        dx, grads, sg[i] = _layer_bwd(dx, kept[i], i, weights[i], rp, mats, tabs, tie, hooks)
        if pending is not None:
            landings = scatter_done(held["chips"], landings, dx)
        if i > 0:
            pending, tie = scatter_pair(str(i), all_kinds, grads, dx)
        else:
            pending, _ = scatter_pair("0m", mixer_kinds, [grads[j] for j in mixer_kinds], dx)

    sgrads, conv_grad = _replicated_grads(sg, sp)

    def pack_small(t, final, conv):
        flat = [t[name].reshape(DEPTH, n) for name, n in SMALL]
        flat = jnp.concatenate([jnp.concatenate(flat, axis=1).reshape(-1), final.reshape(-1), conv.reshape(-1)])
        return jnp.pad(flat, (0, SMALL_ROWS * 128 - flat.shape[0])).reshape(SMALL_ROWS, 128)

    small_src = [pack_small(sgrads, d_norm_final, conv_grad).astype(BF16)]
    last, tie = scatter_chips(pending, landings0, 0, small_src[0])
    s_sems, r_sems, arrays, tie = _split_start(
        "gather_small_chips_start", small_src + place_own(small_src), 4, _plan_gather_chips, tie)
    small_state = ("small", s_sems, r_sems, arrays)

    big_res = _adamw_sharded(landings, 1, DEPTH, big, rows_tile, None, tie, "late")
    landings0 = scatter_done(held["ffn_chips"], landings0, big_res["w_ssm_glu"][0])
    landings0 = scatter_done(last, landings0, big_res["w_ssm_glu"][0])
    small_state, tie = gather_pass(small_state, landings0[0])
    big_res = _adamw_sharded(landings0, 0, 1, big, rows_tile, big_res, tie, "first")

    zeros_conv = jnp.zeros((CONV_N,), F32)
    _, (sparts,) = gather_done(small_state, big_res["w_ssm_glu"][0], ())
    sw, sm_, sv = (pack_small(small[s], finals[s], zeros_conv) for s in "wmv")
    small_out = _adamw(sparts[None], sw[None], sm_[None], sv[None], SMALL_ROWS // 8, "adamw_replicated")

    def unpack_small(p):
        flat = p.reshape(-1)
        per = flat[:DEPTH * SMALL_PER_LAYER].reshape(DEPTH, SMALL_PER_LAYER)
        out, off = {}, 0
        for name, n in SMALL:
            out[name] = per[:, off:off + n].reshape(small["w"][name].shape)
            off += n
        out["norm_final"] = flat[DEPTH * SMALL_PER_LAYER:DEPTH * SMALL_PER_LAYER + D_MODEL]
        return out

    small_res = [unpack_small(p) for p in small_out]

    conv_off = DEPTH * SMALL_PER_LAYER + D_MODEL
    conv_parts = sparts.reshape(N_DEV, -1)[:, conv_off:conv_off + CONV_N].reshape(N_DEV, DEPTH * 3, WIDTH)
    conv_parts = lax.dynamic_slice_in_dim(conv_parts, mine * 64, 64, axis=2)
    conv_res = _adamw(conv_parts[None], *(convs[s].reshape(1, DEPTH * 3, 64) for s in "wmv"), DEPTH * 3, "adamw_conv_w")

    order = ["norm_mix", "w_in", "b_gate", "attn_sinks", "w_attn_o", "conv_w", "w_conv_o", "ssm_a_re", "ssm_a_im",
             "ssm_b_re", "ssm_b_im", "ssm_c_re", "ssm_c_im", "ssm_d", "ssm_log_dt", "w_ssm_glu", "w_ssm_o",
             "w_mix_o", "norm_ffn", "w_ffn_in", "w_ffn_out", "norm_final"]
    outs = [loss, dx[None]]
    for kind in range(4):
        for name in order:
            if name == "conv_w":
                outs.append(conv_res[kind].reshape(DEPTH, 3, 64))
            elif name in big_res:
                outs.append(big_res[name][kind])
            else:
                outs.append(small_res[kind][name])
    return tuple(outs)
```

```python
import functools
import math

import jax
import jax.numpy as jnp
from jax import lax
from jax.experimental import pallas as pl
from jax.experimental.pallas import tpu as pltpu

F32 = jnp.float32
BF16 = jnp.bfloat16

N_DEV = 8
DEPTH = 4
SEQ = 2048
D_MODEL = 1024
N_Q_HEADS = 8
HEAD_DIM = 64
ATTN_W = 512
KV_W = 128
BLOCK = 128
N_BLOCKS = SEQ // BLOCK
ROPE_THETA = 500000.0
ROT_DIM = 16
NEG_INF = -1e30
WIDTH = 512
SSM_GROUPS = 32
SSM_GROUP = 16
SSM_STATE = 64
SLABS = 16
CHUNK = 256
N_CHUNKS = SEQ // CHUNK
GATE_W = 3 * D_MODEL
IN_COLS = 5888
FFN_H = 2816
NORM_EPS = 1e-6
LR, B1, B2, ADAM_EPS, WD, STEP = 0.001, 0.9, 0.999, 1e-08, 0.01, 10

COL_Q, COL_KV, COL_CBX, COL_U, COL_G = 0, 512, 768, 2304, 2816
PIECE_W = (512, 256, 512, 512, 512, 512, 3072)
PIECE_OFF = tuple(sum(PIECE_W[:i]) for i in range(len(PIECE_W)))

KINDS = (("win_t", 736, 1024), ("wffn_t", 704, 1024), ("wout", 352, 1024), ("wmix", 128, 1024),
         ("branch_t", 128, 1536), ("wglu", 64, 512))

SMALL = (("norm_mix", 1024), ("b_gate", 3072), ("attn_sinks", 8), ("ssm_a_re", 2048), ("ssm_a_im", 2048),
         ("ssm_b_re", 32768), ("ssm_b_im", 32768), ("ssm_c_re", 32768), ("ssm_c_im", 32768),
         ("ssm_d", 512), ("ssm_log_dt", 32), ("norm_ffn", 1024))
SMALL_PER_LAYER = sum(n for _, n in SMALL)
CONV_N = DEPTH * 3 * WIDTH
SMALL_ROWS = 4480

VMEM_LIMIT = 56 * 1024 * 1024
NT = (((1,), (1,)), ((), ()))
TN = (((0,), (0,)), ((), ()))
MESH_ID = pl.DeviceIdType.MESH
ANY = pl.BlockSpec(memory_space=pl.ANY)
HBM = pl.BlockSpec(memory_space=pltpu.HBM)
SEM = pl.BlockSpec(memory_space=pltpu.SEMAPHORE)
EFFECT = pltpu.SideEffectType.DATAFLOW_SIDE_EFFECTING


def _cp(**kw):
    return pltpu.CompilerParams(vmem_limit_bytes=VMEM_LIMIT, **kw)


def _full(shape):
    return pl.BlockSpec(shape, lambda *_: (0,) * len(shape))


def _resident(shape):
    return pl.BlockSpec(shape, lambda *_: (0,) * len(shape), pipeline_mode=pl.Buffered(1))


def _mm_tn(a, b, *, tm, tn, name):
    k, m = a.shape
    n = b.shape[1]

    def body(a_ref, b_ref, o_ref):
        o_ref[...] = lax.dot_general(a_ref[...].astype(BF16), b_ref[...].astype(BF16), TN,
                                     preferred_element_type=F32).astype(BF16)

    return pl.pallas_call(
        body, grid=(m // tm, n // tn),
        in_specs=[pl.BlockSpec((k, tm), lambda i, j: (0, i)), pl.BlockSpec((k, tn), lambda i, j: (0, j))],
        out_specs=pl.BlockSpec((tm, tn), lambda i, j: (i, j)),
        out_shape=jax.ShapeDtypeStruct((m, n), BF16), compiler_params=_cp(), name=name)(a, b)


def _rms_rows(xv, g):
    r = lax.rsqrt(jnp.mean(xv * xv, axis=-1, keepdims=True) + NORM_EPS)
    return ((xv * r) * g).astype(BF16)


def _rms_mm_in(x, g, wt, tie):
    tt = 512
    widths = (ATTN_W, 2 * KV_W, 3 * WIDTH, WIDTH, GATE_W)
    offs = (COL_Q, COL_KV, COL_CBX, COL_U, COL_G)

    def body(x_ref, g_ref, w_ref, tie_ref, q_ref, kv_ref, cbx_ref, u_ref, gl_ref, h_ref):
        h = _rms_rows(x_ref[...], g_ref[...])
        h_ref[...] = h
        prod = lax.dot_general(h, w_ref[...], NT, preferred_element_type=F32)
        for ref, o, w in zip((q_ref, kv_ref, cbx_ref, u_ref, gl_ref), offs, widths):
            ref[...] = prod[:, o:o + w]

    row = lambda w: pl.BlockSpec((tt, w), lambda i: (i, 0))
    sds = jax.ShapeDtypeStruct
    return pl.pallas_call(
        body, grid=(SEQ // tt,), in_specs=[row(D_MODEL), _full((1, D_MODEL)), _resident((IN_COLS, D_MODEL)), ANY],
        out_specs=[row(ATTN_W), row(2 * KV_W), row(3 * WIDTH), row(WIDTH), row(GATE_W), row(D_MODEL)],
        out_shape=[sds((SEQ, ATTN_W), F32), sds((SEQ, 2 * KV_W), F32), sds((SEQ, 3 * WIDTH), F32),
                   sds((SEQ, WIDTH), F32), sds((SEQ, GATE_W), F32), sds((SEQ, D_MODEL), BF16)],
        compiler_params=_cp(), name="rms_mm_in")(x, g, wt, tie)


def _rms_mm_ffn(x, g, wt):
    tt = 512

    def body(x_ref, g_ref, w_ref, o_ref, h_ref):
        h = _rms_rows(x_ref[...], g_ref[...])
        h_ref[...] = h
        o_ref[...] = lax.dot_general(h, w_ref[...], NT, preferred_element_type=F32)

    row = lambda w: pl.BlockSpec((tt, w), lambda i: (i, 0))
    return pl.pallas_call(
        body, grid=(SEQ // tt,), in_specs=[row(D_MODEL), _full((1, D_MODEL)), _resident((2 * FFN_H, D_MODEL))],
        out_specs=[row(2 * FFN_H), row(D_MODEL)],
        out_shape=[jax.ShapeDtypeStruct((SEQ, 2 * FFN_H), F32), jax.ShapeDtypeStruct((SEQ, D_MODEL), BF16)],
        compiler_params=_cp(), name="rms_mm_ffn")(x, g, wt)


def _mm_rmsbwd(pieces, wt, x, g, dres, name):
    tt = 512
    widths = [p.shape[1] for p in pieces]
    offs = [sum(widths[:i]) for i in range(len(widths))]
    n = len(pieces)

    def body(*refs):
        p_refs, (w_ref, x_ref, g_ref, r_ref, dx_ref, dg_ref) = refs[:n], refs[n:]

        @pl.when(pl.program_id(0) == 0)
        def _():
            dg_ref[...] = jnp.zeros_like(dg_ref)

        dh = jnp.zeros((tt, D_MODEL), F32)
        for p_ref, o, w in zip(p_refs, offs, widths):
            dh += jnp.dot(p_ref[...], w_ref[o:o + w, :], preferred_element_type=F32)
        xv = x_ref[...]
        r = lax.rsqrt(jnp.mean(xv * xv, axis=-1, keepdims=True) + NORM_EPS)
        xh = xv * r
        gy = dh * g_ref[...]
        dx_ref[...] = r_ref[...] + r * (gy - xh * jnp.mean(gy * xh, axis=-1, keepdims=True))
        dg_ref[...] += jnp.sum(dh * xh, axis=0, keepdims=True)

    row = lambda w: pl.BlockSpec((tt, w), lambda i: (i, 0))
    return pl.pallas_call(
        body, grid=(SEQ // tt,),
        in_specs=[row(w) for w in widths] + [_resident(wt.shape), row(D_MODEL), _full((1, D_MODEL)), row(D_MODEL)],
        out_specs=[row(D_MODEL), _full((1, D_MODEL))],
        out_shape=[jax.ShapeDtypeStruct((SEQ, D_MODEL), F32), jax.ShapeDtypeStruct((1, D_MODEL), F32)],
        compiler_params=_cp(), name=name)(*pieces, wt, x, g, dres)


def _tn_pieces(pieces, h):
    tk, tn = 512, 512
    nk = SEQ // tk
    n = len(pieces)

    def body(*refs):
        p_refs, (h_ref, o_ref, acc_ref) = refs[:n], refs[n:]
        kk = pl.program_id(1)

        @pl.when(kk == 0)
        def _():
            acc_ref[...] = jnp.zeros_like(acc_ref)

        hv = h_ref[...]
        for p_ref, o, w in zip(p_refs, PIECE_OFF, PIECE_W):
            acc_ref[o:o + w, :] += lax.dot_general(p_ref[...], hv, TN, preferred_element_type=F32)

        @pl.when(kk == nk - 1)
        def _():
            o_ref[...] = acc_ref[...].astype(BF16)

    return pl.pallas_call(
        body, grid=(D_MODEL // tn, nk),
        in_specs=[pl.BlockSpec((tk, w), lambda j, kk: (kk, 0)) for w in PIECE_W]
        + [pl.BlockSpec((tk, tn), lambda j, kk: (kk, j))],
        out_specs=pl.BlockSpec((IN_COLS, tn), lambda j, kk: (0, j)),
        out_shape=jax.ShapeDtypeStruct((IN_COLS, D_MODEL), BF16),
        scratch_shapes=[pltpu.VMEM((IN_COLS, tn), F32)], compiler_params=_cp(), name="tn_pieces")(*pieces, h)


def _tn_branches(dys, acts):
    tk = 512
    nk = SEQ // tk

    def body(d0, d1, d2, a0, a1, a2, o_ref, acc_ref):
        kk = pl.program_id(0)

        @pl.when(kk == 0)
        def _():
            acc_ref[...] = jnp.zeros_like(acc_ref)

        for j, (d, a) in enumerate(((d0, a0), (d1, a1), (d2, a2))):
            acc_ref[:, WIDTH * j:WIDTH * (j + 1)] += lax.dot_general(d[...], a[...], TN, preferred_element_type=F32)

        @pl.when(kk == nk - 1)
        def _():
            o_ref[...] = acc_ref[...].astype(BF16)

    row = lambda w: pl.BlockSpec((tk, w), lambda kk: (kk, 0))
    return pl.pallas_call(
        body, grid=(nk,), in_specs=[row(D_MODEL)] * 3 + [row(WIDTH)] * 3,
        out_specs=_full((D_MODEL, 3 * WIDTH)), out_shape=jax.ShapeDtypeStruct((D_MODEL, 3 * WIDTH), BF16),
        scratch_shapes=[pltpu.VMEM((D_MODEL, 3 * WIDTH), F32)], compiler_params=_cp(), name="tn_branches",
    )(*dys, *acts)


def _rope(t, c, a, b):
    return t * c + pltpu.roll(t, 120, axis=1) * a + pltpu.roll(t, 8, axis=1) * b


def _rope_t(d, c, a, b):
    return d * c + pltpu.roll(d * a, 8, axis=1) + pltpu.roll(d * b, 120, axis=1)


def _band_sides(band):
    left = lax.broadcasted_iota(jnp.int32, band.shape, 1) < HEAD_DIM
    h0 = jnp.where(left, band, 0.0)
    h1 = jnp.where(left, 0.0, band)
    r0 = pltpu.roll(h0, HEAD_DIM, axis=1)
    r1 = pltpu.roll(h1, HEAD_DIM, axis=1)
    return ((h0.astype(BF16), r0.astype(BF16)), (r1.astype(BF16), h1.astype(BF16)))


def _attn_mask(i):
    qi = lax.broadcasted_iota(jnp.int32, (2 * BLOCK, 2 * BLOCK), 0) % BLOCK
    kj = lax.broadcasted_iota(jnp.int32, (2 * BLOCK, 2 * BLOCK), 1)
    delta = qi + BLOCK - kj
    return (delta >= 0) & (delta < BLOCK) & ((kj >= BLOCK) | (i > 0))


def _attn_probs(s, ok, sink):
    s = jnp.where(ok, s * (HEAD_DIM ** -0.5), NEG_INF)
    m = jnp.maximum(jnp.max(s, axis=-1, keepdims=True), sink)
    p = jnp.exp(s - m)
    es = jnp.exp(sink - m)
    inv = 1.0 / (jnp.sum(p, axis=-1, keepdims=True) + es)
    return p * inv, es * inv


def _kv_group(qs, ks, vs, kh, sink_ref):
    q2 = jnp.concatenate([qs[2 * kh], qs[2 * kh + 1]], axis=0)
    kst = jnp.concatenate([ks[kh][0], ks[kh][1]], axis=0)
    vst = jnp.concatenate([vs[kh][0], vs[kh][1]], axis=0)
    top = lax.broadcasted_iota(jnp.int32, (2 * BLOCK, 1), 0) < BLOCK
    sinks = [jnp.where(top, sink_ref[0, 4 * kh + h], sink_ref[0, 4 * kh + 2 + h]) for h in range(2)]
    return q2, kst, vst, sinks


def _attn_load(q_ref, kvc_ref, kvp_ref, tc_ref, ta_ref, tb_ref, pc_ref, pa_ref, pb_ref):
    c, a, b = tc_ref[...], ta_ref[...], tb_ref[...]
    kc = _rope(kvc_ref[:, :KV_W], c, a, b)
    kp = _rope(kvp_ref[:, :KV_W], pc_ref[...], pa_ref[...], pb_ref[...])
    kband = jnp.concatenate([kp, kc], axis=0)
    vband = jnp.concatenate([kvp_ref[:, KV_W:], kvc_ref[:, KV_W:]], axis=0)
    qs = [_rope(q_ref[:, 128 * j:128 * (j + 1)], c, a, b).astype(BF16) for j in range(4)]
    return qs, _band_sides(kband), _band_sides(vband), (c, a, b)


def _attn_specs(clamp):
    cur = lambda i: (clamp(i), 0)
    prev = lambda i: (jnp.maximum(clamp(i) - 1, 0), 0)
    return [
        pl.BlockSpec((BLOCK, ATTN_W), cur), pl.BlockSpec((BLOCK, 2 * KV_W), cur),
        pl.BlockSpec((BLOCK, 2 * KV_W), prev),
        pl.BlockSpec((BLOCK, 128), cur), pl.BlockSpec((BLOCK, 128), cur), pl.BlockSpec((BLOCK, 128), cur),
        pl.BlockSpec((BLOCK, 128), prev), pl.BlockSpec((BLOCK, 128), prev), pl.BlockSpec((BLOCK, 128), prev),
        pl.BlockSpec(memory_space=pltpu.SMEM),
    ]


def _attn_fwd(q, kv, tabs, sinks):
    tc, ta, tb = tabs

    def body(q_ref, kvc_ref, kvp_ref, tc_ref, ta_ref, tb_ref, pc_ref, pa_ref, pb_ref, sink_ref, o_ref):
        i = pl.program_id(0)
        qs, ks, vs, _ = _attn_load(q_ref, kvc_ref, kvp_ref, tc_ref, ta_ref, tb_ref, pc_ref, pa_ref, pb_ref)
        ok = _attn_mask(i)
        for kh in range(2):
            q2, kst, vst, sinks = _kv_group(qs, ks, vs, kh, sink_ref)
            s = lax.dot_general(q2, kst, NT, preferred_element_type=F32)
            pn = [_attn_probs(s[:, 2 * BLOCK * h:2 * BLOCK * (h + 1)], ok, sinks[h])[0].astype(BF16) for h in range(2)]
            o2 = jnp.dot(jnp.concatenate(pn, axis=1), vst, preferred_element_type=F32).astype(BF16)
            for r in range(2):
                j = 2 * kh + r
                o_ref[:, 128 * j:128 * (j + 1)] = o2[BLOCK * r:BLOCK * (r + 1)]

    return pl.pallas_call(
        body, grid=(N_BLOCKS,), in_specs=_attn_specs(lambda i: i),
        out_specs=pl.BlockSpec((BLOCK, ATTN_W), lambda i: (i, 0)),
        out_shape=jax.ShapeDtypeStruct((SEQ, ATTN_W), BF16), compiler_params=_cp(), name="attn_fwd",
    )(q, kv, kv, tc, ta, tb, tc, ta, tb, sinks)


def _attn_bwd(q, kv, tabs, sinks, do):
    tc, ta, tb = tabs
    last = N_BLOCKS - 1
    clamp = lambda i: jnp.minimum(i, last)

    def place(full, side, kh):
        left = lax.broadcasted_iota(jnp.int32, full.shape, 1) < HEAD_DIM
        valid = jnp.where(left, full, 0.0) if side == 0 else jnp.where(left, 0.0, full)
        return valid if side == kh else pltpu.roll(valid, HEAD_DIM, axis=1)

    def body(q_ref, kvc_ref, kvp_ref, tc_ref, ta_ref, tb_ref, pc_ref, pa_ref, pb_ref, sink_ref, do_ref,
             dq_ref, dkv_ref, ds_ref, carry_ref):
        i = pl.program_id(0)

        @pl.when(i == 0)
        def _():
            ds_ref[...] = jnp.zeros_like(ds_ref)
            carry_ref[...] = jnp.zeros_like(carry_ref)

        @pl.when(i > last)
        def _():
            dkv_ref[...] = carry_ref[...].astype(BF16)

        @pl.when(i <= last)
        def _():
            qs, ks, vs, (c, a, b) = _attn_load(q_ref, kvc_ref, kvp_ref, tc_ref, ta_ref, tb_ref,
                                               pc_ref, pa_ref, pb_ref)
            ok = _attn_mask(i)
            dk = jnp.zeros((2 * BLOCK, 128), F32)
            dv = jnp.zeros((2 * BLOCK, 128), F32)
            dsink = jnp.zeros((1, 128), F32)
            lane = lax.broadcasted_iota(jnp.int32, (1, 128), 1)
            for kh in range(2):
                q2, kst, vst, sinks = _kv_group(qs, ks, vs, kh, sink_ref)
                do2 = jnp.concatenate([do_ref[:, 128 * (2 * kh + r):128 * (2 * kh + r + 1)] for r in range(2)],
                                      axis=0).astype(BF16)
                s = lax.dot_general(q2, kst, NT, preferred_element_type=F32)
                dp = lax.dot_general(do2, vst, NT, preferred_element_type=F32)
                pns, dss = [], []
                for h in range(2):
                    cols = slice(2 * BLOCK * h, 2 * BLOCK * (h + 1))
                    pn, ps = _attn_probs(s[:, cols], ok, sinks[h])
                    dr = jnp.sum(pn * dp[:, cols], axis=-1, keepdims=True)
                    pns.append(pn.astype(BF16))
                    dss.append((pn * (dp[:, cols] - dr) * (HEAD_DIM ** -0.5)).astype(BF16))
                    for r in range(2):
                        part = -jnp.sum((ps * dr)[BLOCK * r:BLOCK * (r + 1)])
                        dsink += jnp.where(lane == 4 * kh + 2 * r + h, part, 0.0)
                ds2, pn2 = jnp.concatenate(dss, axis=1), jnp.concatenate(pns, axis=1)
                dq2 = jnp.dot(ds2, kst, preferred_element_type=F32)
                dk2 = lax.dot_general(ds2, q2, TN, preferred_element_type=F32)
                dv2 = lax.dot_general(pn2, do2, TN, preferred_element_type=F32)
                for h in range(2):
                    dk += place(dk2[2 * BLOCK * h:2 * BLOCK * (h + 1)], h, kh)
                    dv += place(dv2[2 * BLOCK * h:2 * BLOCK * (h + 1)], h, kh)
                for r in range(2):
                    j = 2 * kh + r
                    dq_ref[:, 128 * j:128 * (j + 1)] = _rope_t(dq2[BLOCK * r:BLOCK * (r + 1)], c, a, b).astype(BF16)
            ds_ref[...] += dsink
            dk_prev = _rope_t(dk[:BLOCK], pc_ref[...], pa_ref[...], pb_ref[...])
            dk_cur = _rope_t(dk[BLOCK:], c, a, b)
            prev = jnp.concatenate([dk_prev, dv[:BLOCK]], axis=1)
            dkv_ref[...] = (carry_ref[...] + prev).astype(BF16)
            carry_ref[...] = jnp.concatenate([dk_cur, dv[BLOCK:]], axis=1)

    return pl.pallas_call(
        body, grid=(N_BLOCKS + 1,),
        in_specs=_attn_specs(clamp) + [pl.BlockSpec((BLOCK, ATTN_W), lambda i: (clamp(i), 0))],
        out_specs=[pl.BlockSpec((BLOCK, ATTN_W), lambda i: (clamp(i), 0)),
                   pl.BlockSpec((BLOCK, 2 * KV_W), lambda i: (jnp.maximum(i - 1, 0), 0)),
                   pl.BlockSpec((1, 128), lambda i: (0, 0))],
        out_shape=[jax.ShapeDtypeStruct((SEQ, ATTN_W), BF16), jax.ShapeDtypeStruct((SEQ, 2 * KV_W), BF16),
                   jax.ShapeDtypeStruct((1, 128), F32)],
        scratch_shapes=[pltpu.VMEM((BLOCK, 2 * KV_W), F32)], compiler_params=_cp(), name="attn_bwd",
    )(q, kv, kv, tc, ta, tb, tc, ta, tb, sinks, do)


def _shift_down(z, k):
    row = lax.broadcasted_iota(jnp.int32, z.shape, 0)
    return jnp.where(row < k, 0.0, pltpu.roll(z, k, axis=0))


def _shift_up(z, k):
    n = z.shape[0]
    row = lax.broadcasted_iota(jnp.int32, z.shape, 0)
    return jnp.where(row >= n - k, 0.0, pltpu.roll(z, n - k, axis=0))


def _conv_specs():
    nb = WIDTH // 128
    return [pl.BlockSpec((SEQ, 128), lambda j: (0, j)), pl.BlockSpec((SEQ, 128), lambda j: (0, nb + j)),
            pl.BlockSpec((SEQ, 128), lambda j: (0, 2 * nb + j)), pl.BlockSpec((None, 8, 128), lambda j: (0, 0, j))]


def _conv_fwd(cbx, cw, layer):
    def body(cb_ref, cc_ref, cx_ref, w_ref, o_ref):
        z = cc_ref[...] * cx_ref[...]
        s = w_ref[0:1, :] * _shift_down(z, 2) + w_ref[1:2, :] * _shift_down(z, 1) + w_ref[2:3, :] * z
        o_ref[...] = (cb_ref[...] * s).astype(BF16)

    specs = _conv_specs()
    specs[3] = pl.BlockSpec((None, 8, 128), lambda j: (layer, 0, j))
    return pl.pallas_call(
        body, grid=(WIDTH // 128,), in_specs=specs,
        out_specs=pl.BlockSpec((SEQ, 128), lambda j: (0, j)),
        out_shape=jax.ShapeDtypeStruct((SEQ, WIDTH), BF16), compiler_params=_cp(), name="conv_fwd",
    )(cbx, cbx, cbx, cw)


def _conv_bwd(cbx, cw, layer, dout, tie):
    def body(cb_ref, cc_ref, cx_ref, w_ref, do_ref, tie_ref, dcb_ref, dcc_ref, dcx_ref, dw_ref):
        cc, cx = cc_ref[...], cx_ref[...]
        z = cc * cx
        z1, z2 = _shift_down(z, 1), _shift_down(z, 2)
        w0, w1, w2 = w_ref[0:1, :], w_ref[1:2, :], w_ref[2:3, :]
        dout = do_ref[...]
        ds = dout * cb_ref[...]
        dcb_ref[...] = (dout * (w0 * z2 + w1 * z1 + w2 * z)).astype(BF16)
        dz = w2 * ds + w1 * _shift_up(ds, 1) + w0 * _shift_up(ds, 2)
        dcc_ref[...] = (dz * cx).astype(BF16)
        dcx_ref[...] = (dz * cc).astype(BF16)
        rows = [jnp.sum(ds * zz, axis=0, keepdims=True) for zz in (z2, z1, z)]
        dw_ref[...] = jnp.concatenate(rows + [jnp.zeros((5, 128), F32)], axis=0)

    col = lambda j: (0, j)
    specs = _conv_specs()
    specs[3] = pl.BlockSpec((None, 8, 128), lambda j: (layer, 0, j))
    return pl.pallas_call(
        body, grid=(WIDTH // 128,), in_specs=specs + [pl.BlockSpec((SEQ, 128), col), ANY],
        out_specs=[pl.BlockSpec((SEQ, 128), col), pl.BlockSpec((SEQ, 128), col), pl.BlockSpec((SEQ, 128), col),
                   pl.BlockSpec((8, 128), col)],
        out_shape=[jax.ShapeDtypeStruct((SEQ, WIDTH), BF16)] * 3 + [jax.ShapeDtypeStruct((8, WIDTH), F32)],
        compiler_params=_cp(), name="conv_bwd",
    )(cbx, cbx, cbx, cw, dout, tie)


def _ssm_prep_math(a_re, a_im, log_dt, bt_re, bt_im):
    dt = jnp.exp(log_dt)
    er = jnp.exp(a_re * dt)
    lr = er * jnp.cos(a_im * dt)
    li = er * jnp.sin(a_im * dt)
    n2 = a_re * a_re + a_im * a_im
    cr = ((lr - 1.0) * a_re + li * a_im) / n2
    ci = (li * a_re - (lr - 1.0) * a_im) / n2
    cr3, ci3 = cr[:, None, :], ci[:, None, :]
    return lr, li, cr3 * bt_re - ci3 * bt_im, cr3 * bt_im + ci3 * bt_re


_GS = (SSM_GROUPS, SSM_STATE)
_GHS = (SSM_GROUPS, SSM_GROUP, SSM_STATE)


def _layered(shape):
    return pl.BlockSpec((None,) + shape, lambda l: (l,) + (0,) * len(shape))


def _ssm_prep(a_re, a_im, log_dt, bt_re, bt_im):
    def body(ar, ai, ld, br, bi, o0, o1, o2, o3):
        outs = _ssm_prep_math(ar[...], ai[...], ld[...], br[...], bi[...])
        for o, v in zip((o0, o1, o2, o3), outs):
            o[...] = v

    shapes = [_GS, _GS, _GHS, _GHS]
    return pl.pallas_call(
        body, grid=(DEPTH,), in_specs=[_layered(s) for s in (_GS, _GS, (SSM_GROUPS, 1), _GHS, _GHS)],
        out_specs=[_layered(s) for s in shapes],
        out_shape=[jax.ShapeDtypeStruct((DEPTH,) + s, F32) for s in shapes],
        name="ssm_prep")(a_re, a_im, log_dt, bt_re, bt_im)


def _ssm_prep_bwd(a_re, a_im, log_dt, bt_re, bt_im, cots):
    def body(ar, ai, ld, br, bi, c0, c1, c2, c3, o0, o1, o2, o3, o4):
        _, vjp = jax.vjp(_ssm_prep_math, ar[...], ai[...], ld[...], br[...], bi[...])
        for o, v in zip((o0, o1, o2, o3, o4), vjp((c0[...], c1[...], c2[...], c3[...]))):
            o[...] = v

    ins = (_GS, _GS, (SSM_GROUPS, 1), _GHS, _GHS)
    return pl.pallas_call(
        body, grid=(DEPTH,), in_specs=[_layered(s) for s in ins + (_GS, _GS, _GHS, _GHS)],
        out_specs=[_layered(s) for s in ins],
        out_shape=[jax.ShapeDtypeStruct((DEPTH,) + s, F32) for s in ins],
        name="ssm_prep_bwd")(a_re, a_im, log_dt, bt_re, bt_im, *cots)


LANES_G = 512
N_LANE_GROUPS = SSM_GROUPS * SSM_STATE // LANES_G


def _scan_order(a):
    return a.reshape(N_CHUNKS, CHUNK, -1).transpose(1, 0, 2).reshape(a.shape)


def _time_order(a):
    return a.reshape(CHUNK, N_CHUNKS, -1).transpose(1, 0, 2).reshape(a.shape)


def _scan_in_place(xr_ref, xi_ref, ar, ai, reverse):
    shape = (N_CHUNKS, xr_ref.shape[1])
    ar, ai = jnp.broadcast_to(ar, shape), jnp.broadcast_to(ai, shape)

    def rows(tau):
        t = (CHUNK - 1 - tau) if reverse else tau
        return pl.ds(pl.multiple_of(t * N_CHUNKS, N_CHUNKS), N_CHUNKS)

    def step(tau, carry):
        sr, si = carry
        return ar * sr - ai * si + xr_ref[rows(tau), :], ar * si + ai * sr + xi_ref[rows(tau), :]

    zero = jnp.zeros(shape, F32)
    er, ei = lax.fori_loop(0, CHUNK, step, (zero, zero), unroll=8)
    qr, qi = ar, ai
    for _ in range(8):
        qr, qi = qr * qr - qi * qi, 2.0 * qr * qi
    shift = _shift_up if reverse else _shift_down
    for k in (1, 2, 4):
        sr, si = shift(er, k), shift(ei, k)
        er, ei = er + qr * sr - qi * si, ei + qr * si + qi * sr
        qr, qi = qr * qr - qi * qi, 2.0 * qr * qi
    start = (shift(er, 1), shift(ei, 1))

    def write(tau, carry):
        sr, si = step(tau, carry)
        xr_ref[rows(tau), :] = sr
        xi_ref[rows(tau), :] = si
        return sr, si

    return write, start


def _ssm_specs(layer):
    col = lambda w: pl.BlockSpec((SEQ, w), lambda g: (0, g))
    diag = pl.BlockSpec((None, None, 128, LANES_G), lambda g: (layer, g, 0, 0))
    vec = pl.BlockSpec((None, 1, LANES_G), lambda g: (layer, 0, g))
    return col, diag, vec


def _ssm_fwd(u, mats, layer, d):
    def body(u_ref, d_ref, br_ref, bi_ref, cr_ref, ci_ref, ar_ref, ai_ref, xr_ref, xi_ref, y_ref):
        uv = u_ref[...].astype(BF16)
        xr_ref[...] = jnp.dot(uv, br_ref[...], preferred_element_type=F32)
        xi_ref[...] = jnp.dot(uv, bi_ref[...], preferred_element_type=F32)
        write, start = _scan_in_place(xr_ref, xi_ref, ar_ref[...], ai_ref[...], False)
        lax.fori_loop(0, CHUNK, write, start, unroll=8)
        y = lax.dot_general(xr_ref[...].astype(BF16), cr_ref[...], NT, preferred_element_type=F32)
        y += lax.dot_general(xi_ref[...].astype(BF16), ci_ref[...], NT, preferred_element_type=F32)
        y_ref[...] = y + d_ref[...] * u_ref[...]

    col, diag, vec = _ssm_specs(layer)
    return pl.pallas_call(
        body, grid=(N_LANE_GROUPS,),
        in_specs=[col(128), pl.BlockSpec((None, 1, 128), lambda g: (layer, 0, g)),
                  diag, diag, diag, diag, vec, vec],
        out_specs=[col(LANES_G), col(LANES_G), col(128)],
        out_shape=[jax.ShapeDtypeStruct((SEQ, SSM_GROUPS * SSM_STATE), F32)] * 2
        + [jax.ShapeDtypeStruct((SEQ, WIDTH), F32)],
        compiler_params=_cp(), name="ssm_fwd",
    )(u, d, mats["b_re"], mats["b_im"], mats["c_re"], mats["c_im_neg"], mats["a_re"], mats["a_im"])


def _ssm_bwd(dy16, x_re, x_im, u, mats, layer, d):
    def body(dy_ref, u_ref, d_ref, xr_ref, xi_ref, br_ref, bi_ref, cr_ref, ci_ref, ar_ref, ai_ref,
             du_ref, dar_ref, dai_ref, dbr_ref, dbi_ref, dcr_ref, dci_ref, lr_ref, li_ref):
        dy = dy_ref[...]
        lr_ref[...] = jnp.dot(dy, cr_ref[...], preferred_element_type=F32)
        li_ref[...] = jnp.dot(dy, ci_ref[...], preferred_element_type=F32)
        write, start = _scan_in_place(lr_ref, li_ref, ar_ref[...], -ai_ref[...], True)

        def rows(t):
            return pl.ds(pl.multiple_of(t * N_CHUNKS, N_CHUNKS), N_CHUNKS)

        def grad(acc, lam, xpr, xpi):
            return acc[0] + xpr * lam[0] + xpi * lam[1], acc[1] + xpr * lam[1] - xpi * lam[0]

        def down(tau, carry):
            lam = write(tau, carry[0])
            t = CHUNK - 2 - tau
            return lam, grad(carry[1], lam, xr_ref[rows(t), :], xi_ref[rows(t), :])

        zero = jnp.zeros((N_CHUNKS, LANES_G), F32)
        lam, acc = lax.fori_loop(0, CHUNK - 1, down, (start, (zero, zero)), unroll=5)
        lam = write(CHUNK - 1, lam)
        last = rows(CHUNK - 1)
        acc = grad(acc, lam, _shift_down(xr_ref[last, :], 1), _shift_down(xi_ref[last, :], 1))
        dar_ref[...] = jnp.sum(acc[0], axis=0, keepdims=True)
        dai_ref[...] = jnp.sum(acc[1], axis=0, keepdims=True)

        l_re, l_im = lr_ref[...].astype(BF16), li_ref[...].astype(BF16)
        du = lax.dot_general(l_re, br_ref[...], NT, preferred_element_type=F32)
        du += lax.dot_general(l_im, bi_ref[...], NT, preferred_element_type=F32)
        du_ref[...] = (du + dy.astype(F32) * d_ref[...]).astype(BF16)
        uv = u_ref[...].astype(BF16)
        dbr_ref[...] = lax.dot_general(uv, l_re, TN, preferred_element_type=F32)
        dbi_ref[...] = lax.dot_general(uv, l_im, TN, preferred_element_type=F32)
        dcr_ref[...] = lax.dot_general(dy, xr_ref[...].astype(BF16), TN, preferred_element_type=F32)
        dci_ref[...] = lax.dot_general(dy, xi_ref[...].astype(BF16), TN, preferred_element_type=F32)

    col, diag, vec = _ssm_specs(layer)
    out_vec = pl.BlockSpec((1, LANES_G), lambda g: (0, g))
    out_blk = pl.BlockSpec((None, 128, LANES_G), lambda g: (g, 0, 0))
    sds = jax.ShapeDtypeStruct
    return pl.pallas_call(
        body, grid=(N_LANE_GROUPS,),
        in_specs=[col(128), col(128), pl.BlockSpec((None, 1, 128), lambda g: (layer, 0, g)),
                  col(LANES_G), col(LANES_G), diag, diag, diag, diag, vec, vec],
        out_specs=[col(128), out_vec, out_vec, out_blk, out_blk, out_blk, out_blk],
        out_shape=[sds((SEQ, WIDTH), BF16)] + [sds((1, SSM_GROUPS * SSM_STATE), F32)] * 2
        + [sds((N_LANE_GROUPS, 128, LANES_G), F32)] * 4,
        scratch_shapes=[pltpu.VMEM((SEQ, LANES_G), F32)] * 2, compiler_params=_cp(), name="ssm_bwd",
    )(dy16, u, d, x_re, x_im, mats["b_re"], mats["b_im"], mats["c_re"], mats["c_im_neg"],
      mats["a_re"], mats["a_im"])


_GELU_C = math.sqrt(2.0 / math.pi)


def _gelu(y):
    return 0.5 * y * (1.0 + jnp.tanh(_GELU_C * (y + 0.044715 * (y * y * y))))


def _glu_fwd(y, wglu):
    tt = 512

    def body(y_ref, w_ref, z_ref):
        ys = _gelu(y_ref[...])
        a = jnp.dot(ys.astype(BF16), w_ref[...], preferred_element_type=F32)
        z_ref[...] = (ys * jax.nn.sigmoid(a)).astype(BF16)

    blk = pl.BlockSpec((tt, WIDTH), lambda i: (i, 0))
    return pl.pallas_call(body, grid=(SEQ // tt,), in_specs=[blk, _full((WIDTH, WIDTH))], out_specs=blk,
                          out_shape=jax.ShapeDtypeStruct((SEQ, WIDTH), BF16), compiler_params=_cp(),
                          name="glu_fwd")(y, wglu)


def _glu_bwd(y, wglu, dz, u):
    tt = 512

    def body(y_ref, w_ref, dz_ref, u_ref, dy_ref, ys_ref, da_ref, dd_ref):
        @pl.when(pl.program_id(0) == 0)
        def _():
            dd_ref[...] = jnp.zeros_like(dd_ref)

        yv = y_ref[...]
        t = jnp.tanh(_GELU_C * (yv + 0.044715 * (yv * yv * yv)))
        ys = 0.5 * yv * (1.0 + t)
        ysb = ys.astype(BF16)
        sg = jax.nn.sigmoid(jnp.dot(ysb, w_ref[...], preferred_element_type=F32))
        dz = dz_ref[...].astype(F32)
        da = (dz * ys * sg * (1.0 - sg)).astype(BF16)
        dys = dz * sg + lax.dot_general(da, w_ref[...], NT, preferred_element_type=F32)
        dy = dys * (0.5 * (1.0 + t) + 0.5 * yv * (1.0 - t * t) * _GELU_C * (1.0 + 3 * 0.044715 * (yv * yv)))
        dy_ref[...] = dy.astype(BF16)
        ys_ref[...] = ysb
        da_ref[...] = da
        dd_ref[...] += jnp.sum(dy * u_ref[...], axis=0, keepdims=True)

    blk = pl.BlockSpec((tt, WIDTH), lambda i: (i, 0))
    return pl.pallas_call(
        body, grid=(SEQ // tt,), in_specs=[blk, _full((WIDTH, WIDTH)), blk, blk],
        out_specs=[blk, blk, blk, _full((1, WIDTH))],
        out_shape=[jax.ShapeDtypeStruct((SEQ, WIDTH), BF16)] * 3 + [jax.ShapeDtypeStruct((1, WIDTH), F32)],
        compiler_params=_cp(), name="glu_bwd")(y, wglu, dz, u)


def _mix_specs(tt, layer):
    row = lambda w: pl.BlockSpec((tt, w), lambda i: (i, 0))
    gate = lambda j: pl.BlockSpec((tt, D_MODEL), lambda i: (i, j))
    wo = lambda j: pl.BlockSpec((D_MODEL, WIDTH), lambda i: (0, j))
    return [row(D_MODEL), row(WIDTH), row(WIDTH), row(WIDTH), gate(0), gate(1), gate(2),
            pl.BlockSpec((None, 1, GATE_W), lambda i: (layer, 0, 0)), wo(0), wo(1), wo(2),
            _full((D_MODEL, D_MODEL))]


def _mix_branches(o_ref, c_ref, z_ref, g_refs, b_ref, wa_ref, wc_ref, ws_ref):
    ys = [lax.dot_general(r[...], w[...], NT, preferred_element_type=F32)
          for r, w in ((o_ref, wa_ref), (c_ref, wc_ref), (z_ref, ws_ref))]
    gates = [jax.nn.sigmoid(g_refs[j][...] + b_ref[:, D_MODEL * j:D_MODEL * (j + 1)]) for j in range(3)]
    return ys, gates


def _mix_fwd(x, o, cv, z, glog, b_gate, layer, wbt, wmix, tie):
    tt = 256

    def body(x_ref, o_ref, c_ref, z_ref, g0, g1, g2, b_ref, wa_ref, wc_ref, ws_ref, wm_ref, tie_ref, x1_ref):
        ys, gates = _mix_branches(o_ref, c_ref, z_ref, (g0, g1, g2), b_ref, wa_ref, wc_ref, ws_ref)
        merged = gates[0] * ys[0] + gates[1] * ys[1] + gates[2] * ys[2]
        x1_ref[...] = x_ref[...] + jnp.dot(merged.astype(BF16), wm_ref[...], preferred_element_type=F32)

    return pl.pallas_call(
        body, grid=(SEQ // tt,), in_specs=_mix_specs(tt, layer) + [ANY],
        out_specs=pl.BlockSpec((tt, D_MODEL), lambda i: (i, 0)),
        out_shape=jax.ShapeDtypeStruct((SEQ, D_MODEL), F32), compiler_params=_cp(), name="mix_fwd",
    )(x, o, cv, z, glog, glog, glog, b_gate, wbt, wbt, wbt, wmix, tie)


def _mix_bwd(dx1, o, cv, z, glog, b_gate, layer, wbt, wmix, tie):
    tt = 256

    def body(dx_ref, o_ref, c_ref, z_ref, g0, g1, g2, b_ref, wa_ref, wc_ref, ws_ref, wm_ref, tie_ref,
             mg_ref, dya_ref, dyc_ref, dys_ref, do_ref, dc_ref, dz_ref, dgl_ref, db_ref):
        @pl.when(pl.program_id(0) == 0)
        def _():
            db_ref[...] = jnp.zeros_like(db_ref)

        ys, gates = _mix_branches(o_ref, c_ref, z_ref, (g0, g1, g2), b_ref, wa_ref, wc_ref, ws_ref)
        mg_ref[...] = (gates[0] * ys[0] + gates[1] * ys[1] + gates[2] * ys[2]).astype(BF16)
        dm = lax.dot_general(dx_ref[...].astype(BF16), wm_ref[...], NT, preferred_element_type=F32)
        for j, (dy_ref, w_ref, d_ref) in enumerate(((dya_ref, wa_ref, do_ref), (dyc_ref, wc_ref, dc_ref),
                                                    (dys_ref, ws_ref, dz_ref))):
            dy = (dm * gates[j]).astype(BF16)
            dy_ref[...] = dy
            d_ref[...] = jnp.dot(dy, w_ref[...], preferred_element_type=F32)
            dgl = dm * ys[j] * gates[j] * (1.0 - gates[j])
            dgl_ref[:, D_MODEL * j:D_MODEL * (j + 1)] = dgl.astype(BF16)
            db_ref[:, D_MODEL * j:D_MODEL * (j + 1)] += jnp.sum(dgl, axis=0, keepdims=True)

    row = lambda w: pl.BlockSpec((tt, w), lambda i: (i, 0))
    sds = jax.ShapeDtypeStruct
    return pl.pallas_call(
        body, grid=(SEQ // tt,), in_specs=_mix_specs(tt, layer) + [ANY],
        out_specs=[row(D_MODEL)] * 4 + [row(WIDTH)] * 3 + [row(GATE_W), _full((1, GATE_W))],
        out_shape=[sds((SEQ, D_MODEL), BF16)] * 4 + [sds((SEQ, WIDTH), F32)] * 3
        + [sds((SEQ, GATE_W), BF16), sds((1, GATE_W), F32)],
        compiler_params=_cp(), name="mix_bwd",
    )(dx1, o, cv, z, glog, glog, glog, b_gate, wbt, wbt, wbt, wmix, tie)


def _ffn_out_fwd(x1, gu, wout, tie):
    tt = 256

    def body(x_ref, gt_ref, up_ref, w_ref, tie_ref, o_ref):
        gt = gt_ref[...]
        act = (gt * jax.nn.sigmoid(gt) * up_ref[...]).astype(BF16)
        o_ref[...] = x_ref[...] + jnp.dot(act, w_ref[...], preferred_element_type=F32)

    return pl.pallas_call(
        body, grid=(SEQ // tt,),
        in_specs=[pl.BlockSpec((tt, D_MODEL), lambda i: (i, 0)), pl.BlockSpec((tt, FFN_H), lambda i: (i, 0)),
                  pl.BlockSpec((tt, FFN_H), lambda i: (i, 1)), _full((FFN_H, D_MODEL)), ANY],
        out_specs=pl.BlockSpec((tt, D_MODEL), lambda i: (i, 0)),
        out_shape=jax.ShapeDtypeStruct((SEQ, D_MODEL), F32), compiler_params=_cp(), name="ffn_out_fwd",
    )(x1, gu, gu, wout, tie)


def _ffn_out_bwd(dx2, gu, wout, tie):
    tt = 256

    def body(dx_ref, gt_ref, up_ref, w_ref, tie_ref, dgu_ref, act_ref):
        gt, up = gt_ref[...], up_ref[...]
        sg = jax.nn.sigmoid(gt)
        silu = gt * sg
        act_ref[...] = (silu * up).astype(BF16)
        dact = lax.dot_general(dx_ref[...].astype(BF16), w_ref[...], NT, preferred_element_type=F32)
        dgu_ref[:, :FFN_H] = (dact * up * (sg * (1.0 + gt * (1.0 - sg)))).astype(BF16)
        dgu_ref[:, FFN_H:] = (dact * silu).astype(BF16)

    return pl.pallas_call(
        body, grid=(SEQ // tt,),
        in_specs=[pl.BlockSpec((tt, D_MODEL), lambda i: (i, 0)), pl.BlockSpec((tt, FFN_H), lambda i: (i, 0)),
                  pl.BlockSpec((tt, FFN_H), lambda i: (i, 1)), _full((FFN_H, D_MODEL)), ANY],
        out_specs=[pl.BlockSpec((tt, 2 * FFN_H), lambda i: (i, 0)), pl.BlockSpec((tt, FFN_H), lambda i: (i, 0))],
        out_shape=[jax.ShapeDtypeStruct((SEQ, 2 * FFN_H), BF16), jax.ShapeDtypeStruct((SEQ, FFN_H), BF16)],
        compiler_params=_cp(), name="ffn_out_bwd",
    )(dx2, gu, gu, wout, tie)


def _loss_head(x, g, target):
    tt = 256

    def body(x_ref, g_ref, t_ref, loss_ref, dx_ref, dg_ref):
        @pl.when(pl.program_id(0) == 0)
        def _():
            loss_ref[...] = jnp.zeros_like(loss_ref)
            dg_ref[...] = jnp.zeros_like(dg_ref)

        xv = x_ref[...]
        r = lax.rsqrt(jnp.mean(xv * xv, axis=-1, keepdims=True) + NORM_EPS)
        xh = xv * r
        err = xh * g_ref[...] - t_ref[...]
        loss_ref[...] += 0.5 * jnp.sum(jnp.mean(err * err, axis=-1, keepdims=True))
        dy = err * (1.0 / D_MODEL)
        gy = dy * g_ref[...]
        dx_ref[...] = r * (gy - xh * jnp.mean(gy * xh, axis=-1, keepdims=True))
        dg_ref[...] += jnp.sum(dy * xh, axis=0, keepdims=True)

    row = pl.BlockSpec((tt, D_MODEL), lambda i: (i, 0))
    return pl.pallas_call(
        body, grid=(SEQ // tt,), in_specs=[row, _full((1, D_MODEL)), row],
        out_specs=[_full((1, 128)), row, _full((1, D_MODEL))],
        out_shape=[jax.ShapeDtypeStruct((1, 128), F32), jax.ShapeDtypeStruct((SEQ, D_MODEL), F32),
                   jax.ShapeDtypeStruct((1, D_MODEL), F32)],
        compiler_params=_cp(), name="loss_head")(x, g, target)


def _adamw(parts, w, m, v, tr, name, groups=None, fill=None, tie=None):
    n_groups, rows, cols = w.shape
    n_parts = parts.shape[1]
    lo, hi = groups if groups is not None else (0, n_groups)

    def body(p_ref, w_ref, m_ref, v_ref, *rest):
        g_ref, d_ref, nm_ref, nv_ref = rest[-4:]
        g = p_ref[0].astype(F32)
        for k in range(1, n_parts):
            g = g + p_ref[k].astype(F32)
        nm = B1 * m_ref[...] + (1.0 - B1) * g
        nv = B2 * v_ref[...] + (1.0 - B2) * (g * g)
        m_hat = nm / (1.0 - B1 ** STEP)
        v_hat = nv / (1.0 - B2 ** STEP)
        g_ref[...] = g
        d_ref[...] = -LR * (m_hat / (jnp.sqrt(v_hat) + ADAM_EPS) + WD * w_ref[...])
        nm_ref[...] = nm
        nv_ref[...] = nv

    blk = pl.BlockSpec((None, tr, cols), lambda l, i: (l + lo, i, 0))
    p_lo = lo if parts.shape[0] == n_groups else 0
    extra = ([] if fill is None else list(fill)) + ([] if tie is None else [tie])
    return pl.pallas_call(
        body, grid=(hi - lo, rows // tr),
        in_specs=[pl.BlockSpec((None, n_parts, tr, cols), lambda l, i: (l + p_lo, 0, i, 0)), blk, blk, blk]
        + [ANY] * len(extra),
        out_specs=[blk] * 4, out_shape=[jax.ShapeDtypeStruct((n_groups, rows, cols), F32)] * 4,
        input_output_aliases={} if fill is None else {4 + j: j for j in range(4)},
        compiler_params=_cp(), name=name)(parts, w, m, v, *extra)


def _split_start(name, arrays, n_sems, plan, after=None):
    n = len(arrays)
    order = [] if after is None else [after]
    n_in = n + len(order)

    def body(*refs):
        ins, send_sems, recv_sems, token = refs[:n], refs[n_in], refs[n_in + 1], refs[-1]
        for src, dst, k, to in plan(ins)[0]:
            pltpu.make_async_remote_copy(src_ref=src, dst_ref=dst, send_sem=send_sems.at[k], recv_sem=recv_sems.at[k],
                                         device_id=to, device_id_type=MESH_ID).start()
        token[...] = jnp.zeros_like(token)

    outs = pl.pallas_call(
        body, name=name,
        out_shape=(pltpu.SemaphoreType.DMA((n_sems,)), pltpu.SemaphoreType.DMA((n_sems,)),
                   *[pltpu.HBM(a.shape, a.dtype) for a in arrays], jax.ShapeDtypeStruct((8, 128), F32)),
        in_specs=[HBM] * n + [ANY] * len(order),
        out_specs=(SEM, SEM, *[HBM] * n, pl.BlockSpec(memory_space=pltpu.VMEM)),
        input_output_aliases={i: 2 + i for i in range(n)},
        compiler_params=pltpu.CompilerParams(has_side_effects=EFFECT),
    )(*[pltpu.with_memory_space_constraint(a, pltpu.HBM) for a in arrays], *order)
    return outs[0], outs[1], list(outs[2:2 + n]), outs[-1]


def _split_wait(name, arrays, send_sems, recv_sems, after, plan):
    n = len(arrays)

    def body(*refs):
        ins, s_sems, r_sems = refs[:n], refs[n], refs[n + 1]
        sends, arrivals = plan(ins)
        x, y, c = lax.axis_index("x"), lax.axis_index("y"), lax.axis_index("c")
        for src, dst, k, to in sends:
            pltpu.make_async_remote_copy(src_ref=src, dst_ref=dst, send_sem=s_sems.at[k], recv_sem=r_sems.at[k],
                                         device_id=to, device_id_type=MESH_ID).wait_send()
        for dst, k in arrivals:
            pltpu.make_async_remote_copy(src_ref=dst, dst_ref=dst, send_sem=s_sems.at[k], recv_sem=r_sems.at[k],
                                         device_id=(x, y, c), device_id_type=MESH_ID).wait_recv()

    return pl.pallas_call(
        body, name=name, out_shape=[pltpu.HBM(a.shape, a.dtype) for a in arrays],
        in_specs=[HBM] * n + [SEM, SEM, ANY], out_specs=[HBM] * n,
        input_output_aliases={i: i for i in range(n)},
        compiler_params=pltpu.CompilerParams(has_side_effects=EFFECT),
    )(*arrays, send_sems, recv_sems, after)


def _chips():
    x, y, c = lax.axis_index("x"), lax.axis_index("y"), lax.axis_index("c")
    return x, y, c, [(1 - x, y), (x, 1 - y), (1 - x, 1 - y)]


def _plan_gather_chips(refs):
    x, y, c, chips = _chips()
    me = 4 * x + 2 * y + c
    n = len(refs) // 2
    sends, arrivals = [], []
    for i in range(n):
        src, land = refs[i], refs[n + i]
        sends.append((src, land.at[me], 4 * i, (x, y, 1 - c)))
        arrivals.append((land.at[4 * x + 2 * y + 1 - c], 4 * i))
        for j, (px, py) in enumerate(chips):
            sends.append((src, land.at[me], 4 * i + 1 + j, (px, py, c)))
            arrivals.append((land.at[4 * px + 2 * py + c], 4 * i + 1 + j))
    return sends, arrivals


def _plan_gather_pass(refs):
    x, y, c, chips = _chips()
    sends, arrivals = [], []
    for i in range(len(refs)):
        for j, (px, py) in enumerate(chips):
            slot = refs[i].at[4 * px + 2 * py + c]
            sends.append((slot, slot, 4 * i + j, (x, y, 1 - c)))
            arrivals.append((refs[i].at[4 * px + 2 * py + 1 - c], 4 * i + j))
        back = refs[i].at[4 * x + 2 * y + 1 - c]
        sends.append((back, back, 4 * i + 3, (x, y, 1 - c)))
        arrivals.append((refs[i].at[4 * x + 2 * y + c], 4 * i + 3))
    return sends, arrivals


def _plan_scatter_pair(refs):
    x, y, c = lax.axis_index("x"), lax.axis_index("y"), lax.axis_index("c")
    n = len(refs) // 2
    sends, arrivals = [], []
    for i in range(n):
        for q in range(4):
            sends.append((refs[i].at[q, 1 - c], refs[n + i].at[q], 4 * i + q, (x, y, 1 - c)))
            arrivals.append((refs[n + i].at[q], 4 * i + q))
    return sends, arrivals


def _plan_scatter_chips(layer):
    def plan(refs):
        x, y, c, chips = _chips()
        n = len(refs) // 2
        sends, arrivals = [], []
        for i in range(n):
            for j, (px, py) in enumerate(chips):
                sends.append((refs[i].at[2 * px + py], refs[n + i].at[layer, 2 * x + y], 3 * i + j, (px, py, c)))
                arrivals.append((refs[n + i].at[layer, 2 * px + py], 3 * i + j))
        return sends, arrivals

    return plan


def _pair_sum(parts4, from_pair, landing, layer, core, tr, name):
    _, _, rows, cols = parts4.shape

    def body(c_ref, p_ref, s_ref, l_ref, sum_ref, land_ref):
        v = (p_ref[...].astype(F32) + s_ref[...].astype(F32)).astype(BF16)
        sum_ref[...] = v
        land_ref[...] = v

    blk = pl.BlockSpec((None, tr, cols), lambda q, i, c_ref: (q, i, 0))
    return pl.pallas_call(
        body,
        grid_spec=pltpu.PrefetchScalarGridSpec(
            num_scalar_prefetch=1, grid=(4, rows // tr),
            in_specs=[pl.BlockSpec((None, None, tr, cols), lambda q, i, c_ref: (q, c_ref[0], i, 0)), blk, ANY],
            out_specs=[blk, pl.BlockSpec((None, None, tr, cols), lambda q, i, c_ref: (layer, q, i, 0))]),
        out_shape=[jax.ShapeDtypeStruct((4, rows, cols), BF16), jax.ShapeDtypeStruct(landing.shape, BF16)],
        input_output_aliases={3: 1}, compiler_params=_cp(), name=name,
    )(core, parts4, from_pair, landing)


def _travel_layout(t):
    tr = lambda a: jnp.swapaxes(a, 1, 2)
    branch = jnp.concatenate([tr(t["w_attn_o"]), tr(t["w_conv_o"]), tr(t["w_ssm_o"])], axis=2)
    return [tr(t["w_in"]), tr(t["w_ffn_in"]), t["w_ffn_out"], t["w_mix_o"], branch, t["w_ssm_glu"]]


def _native_layout(a):
    tr = lambda x: jnp.swapaxes(x, 1, 2)
    b = a[4]
    return {"w_in": tr(a[0]), "w_ffn_in": tr(a[1]), "w_ffn_out": a[2], "w_mix_o": a[3],
            "w_attn_o": tr(b[:, :, :WIDTH]), "w_conv_o": tr(b[:, :, WIDTH:2 * WIDTH]),
            "w_ssm_o": tr(b[:, :, 2 * WIDTH:]), "w_ssm_glu": a[5]}


def _embed(t):
    eye = jnp.eye(8, dtype=t.dtype)
    t = t.reshape(DEPTH, N_LANE_GROUPS, 8, SSM_GROUP, SSM_STATE)
    return (t[:, :, :, :, None, :] * eye[None, None, :, None, :, None]).reshape(DEPTH, N_LANE_GROUPS, 128, LANES_G)


def _diag_blocks(t):
    t = t.reshape(DEPTH, N_LANE_GROUPS, 8, SSM_GROUP, 8, SSM_STATE)
    return jnp.einsum("lgahap->lgahp", t).reshape(DEPTH, SSM_GROUPS, SSM_GROUP, SSM_STATE)


def _rope_tabs():
    pos = jnp.arange(SEQ, dtype=F32)
    inv_freq = ROPE_THETA ** (-jnp.arange(0, ROT_DIM, 2, dtype=F32) / ROT_DIM)
    ang = pos[:, None] * inv_freq[None, :]
    cos, sin = jnp.cos(ang), jnp.sin(ang)
    one, zero = jnp.ones((SEQ, HEAD_DIM - ROT_DIM), F32), jnp.zeros((SEQ, HEAD_DIM - ROT_DIM), F32)
    z8 = jnp.zeros((SEQ, 8), F32)
    head = lambda *p: jnp.tile(jnp.concatenate(p, axis=1), (1, 2))
    return head(cos, cos, one), head(-sin, z8, zero), head(z8, sin, zero)


def _ssm_mats(sp):
    lr, li, bbr, bbi = _ssm_prep(sp["a_re"], sp["a_im"], sp["log_dt"], sp["bt_re"], sp["bt_im"])
    lanes = SSM_GROUPS * SSM_STATE
    return {
        "a_re": lr.reshape(DEPTH, 1, lanes), "a_im": li.reshape(DEPTH, 1, lanes),
        "b_re": _embed(bbr).astype(BF16), "b_im": _embed(bbi).astype(BF16),
        "c_re": _embed(sp["c_re"]).astype(BF16), "c_im_neg": _embed(-sp["c_im"]).astype(BF16),
    }


def _layer_fwd(x, i, w, rp, mats, tabs, tie, hooks):
    q, kv, cbx, u, glog, h = _rms_mm_in(x, rp["norm_mix"][i], w["win_t"], tie)
    o = _attn_fwd(q, kv, tabs, rp["attn_sinks"][i])
    cv = _conv_fwd(cbx, rp["conv_w"], i)
    u = _scan_order(u)
    x_re, x_im, y = _ssm_fwd(u, mats, i, rp["ssm_d"])
    z = _time_order(_glu_fwd(y, w["wglu"]))
    x1 = _mix_fwd(x, o, cv, z, glog, rp["b_gate"], i, w["branch_t"], w["wmix"], hooks["early"](z))
    hooks["pre_ffn"](x1)
    gu, h2 = _rms_mm_ffn(x1, rp["norm_ffn"][i], w["wffn_t"])
    x2 = _ffn_out_fwd(x1, gu, w["wout"], hooks["mid"](h2))
    kept = dict(x=x, q=q, kv=kv, cbx=cbx, u=u, glog=glog, h=h, o=o, cv=cv, z=z, y=y,
                x_re=x_re, x_im=x_im, x1=x1, gu=gu, h2=h2)
    return x2, kept


def _layer_bwd(dx2, k, i, w, rp, mats, tabs, tie, hooks):
    dgu, act = _ffn_out_bwd(dx2, k["gu"], w["wout"], tie)
    g_wout = _mm_tn(act, dx2, tm=FFN_H // 2, tn=1024, name="mm_tn_ffn_out")
    g_wffn_t = _mm_tn(dgu, k["h2"], tm=FFN_H // 2, tn=1024, name="mm_tn_ffn_in")
    dx1, d_norm_ffn = _mm_rmsbwd([dgu], w["wffn_t"], k["x1"], rp["norm_ffn"][i], dx2, "mm_rmsbwd_ffn")

    mg, dya, dyc, dys, do, dcv, dz, dgl, db_gate = _mix_bwd(
        dx1, k["o"], k["cv"], k["z"], k["glog"], rp["b_gate"], i, w["branch_t"], w["wmix"],
        hooks["mid"]((g_wffn_t, g_wout, d_norm_ffn)))
    g_wmix = _mm_tn(mg, dx1, tm=1024, tn=512, name="mm_tn_mix")
    g_branch_t = _tn_branches((dya, dyc, dys), (k["o"], k["cv"], k["z"]))

    dy16, ys16, da16, dd = _glu_bwd(k["y"], w["wglu"], _scan_order(dz), k["u"])
    g_wglu = _mm_tn(ys16, da16, tm=256, tn=512, name="mm_tn_glu")
    du, da_re, da_im, db_re, db_im, dc_re, dc_im = _ssm_bwd(dy16, k["x_re"], k["x_im"], k["u"], mats, i,
                                                             rp["ssm_d"])
    du = _time_order(du)

    dcb, dcc, dcx, d_conv_w = _conv_bwd(k["cbx"], rp["conv_w"], i, dcv, hooks["late"](du))
    dq, dkv, d_sinks = _attn_bwd(k["q"], k["kv"], tabs, rp["attn_sinks"][i], do)

    pieces = [dq, dkv, dcb, dcc, dcx, du, dgl]
    g_win_t = _tn_pieces(pieces, k["h"])
    dx, d_norm_mix = _mm_rmsbwd(pieces, w["win_t"], k["x"], rp["norm_mix"][i], dx1, "mm_rmsbwd_in")

    grads = [g_win_t, g_wffn_t, g_wout, g_wmix, g_branch_t, g_wglu]
    small = dict(norm_mix=d_norm_mix, b_gate=db_gate, attn_sinks=d_sinks, ssm_d=dd, norm_ffn=d_norm_ffn,
                 conv_w=d_conv_w, da_re=da_re, da_im=da_im, db_re=db_re, db_im=db_im, dc_re=dc_re, dc_im=dc_im)
    return dx, grads, small


def _replicated_grads(sg, sp):
    stack = lambda name: jnp.stack([sg[i][name] for i in range(DEPTH)])
    cots = (stack("da_re").reshape(DEPTH, *_GS), stack("da_im").reshape(DEPTH, *_GS),
            _diag_blocks(stack("db_re")), _diag_blocks(stack("db_im")))
    d_a_re, d_a_im, d_log_dt, d_bt_re, d_bt_im = _ssm_prep_bwd(
        sp["a_re"], sp["a_im"], sp["log_dt"], sp["bt_re"], sp["bt_im"], cots)
    sgrads = {"norm_mix": stack("norm_mix"), "b_gate": stack("b_gate"),
              "attn_sinks": stack("attn_sinks")[:, :, :N_Q_HEADS], "ssm_a_re": d_a_re, "ssm_a_im": d_a_im,
              "ssm_b_re": jnp.swapaxes(d_bt_re, 2, 3), "ssm_b_im": jnp.swapaxes(d_bt_im, 2, 3),
              "ssm_c_re": _diag_blocks(stack("dc_re")), "ssm_c_im": -_diag_blocks(stack("dc_im")),
              "ssm_d": stack("ssm_d"), "ssm_log_dt": d_log_dt, "norm_ffn": stack("norm_ffn")}
    return sgrads, stack("conv_w")[:, :3]


def kernel(x, norm_mix, w_in, b_gate, attn_sinks, w_attn_o, conv_w, w_conv_o, ssm_a_re, ssm_a_im, ssm_b_re, ssm_b_im, ssm_c_re, ssm_c_im, ssm_d, ssm_log_dt, w_ssm_glu, w_ssm_o, w_mix_o, norm_ffn, w_ffn_in, w_ffn_out, norm_final, loss_target, m_norm_mix, m_w_in, m_b_gate, m_attn_sinks, m_w_attn_o, m_conv_w, m_w_conv_o, m_ssm_a_re, m_ssm_a_im, m_ssm_b_re, m_ssm_b_im, m_ssm_c_re, m_ssm_c_im, m_ssm_d, m_ssm_log_dt, m_w_ssm_glu, m_w_ssm_o, m_w_mix_o, m_norm_ffn, m_w_ffn_in, m_w_ffn_out, m_norm_final, v_norm_mix, v_w_in, v_b_gate, v_attn_sinks, v_w_attn_o, v_conv_w, v_w_conv_o, v_ssm_a_re, v_ssm_a_im, v_ssm_b_re, v_ssm_b_im, v_ssm_c_re, v_ssm_c_im, v_ssm_d, v_ssm_log_dt, v_w_ssm_glu, v_w_ssm_o, v_w_mix_o, v_norm_ffn, v_w_ffn_in, v_w_ffn_out, v_norm_final):
    big = {"w": dict(w_in=w_in, w_attn_o=w_attn_o, w_conv_o=w_conv_o, w_ssm_glu=w_ssm_glu, w_ssm_o=w_ssm_o,
                     w_mix_o=w_mix_o, w_ffn_in=w_ffn_in, w_ffn_out=w_ffn_out),
           "m": dict(w_in=m_w_in, w_attn_o=m_w_attn_o, w_conv_o=m_w_conv_o, w_ssm_glu=m_w_ssm_glu,
                     w_ssm_o=m_w_ssm_o, w_mix_o=m_w_mix_o, w_ffn_in=m_w_ffn_in, w_ffn_out=m_w_ffn_out),
           "v": dict(w_in=v_w_in, w_attn_o=v_w_attn_o, w_conv_o=v_w_conv_o, w_ssm_glu=v_w_ssm_glu,
                     w_ssm_o=v_w_ssm_o, w_mix_o=v_w_mix_o, w_ffn_in=v_w_ffn_in, w_ffn_out=v_w_ffn_out)}
    small = {"w": dict(norm_mix=norm_mix, b_gate=b_gate, attn_sinks=attn_sinks, ssm_a_re=ssm_a_re,
                       ssm_a_im=ssm_a_im, ssm_b_re=ssm_b_re, ssm_b_im=ssm_b_im, ssm_c_re=ssm_c_re,
                       ssm_c_im=ssm_c_im, ssm_d=ssm_d, ssm_log_dt=ssm_log_dt, norm_ffn=norm_ffn),
             "m": dict(norm_mix=m_norm_mix, b_gate=m_b_gate, attn_sinks=m_attn_sinks, ssm_a_re=m_ssm_a_re,
                       ssm_a_im=m_ssm_a_im, ssm_b_re=m_ssm_b_re, ssm_b_im=m_ssm_b_im, ssm_c_re=m_ssm_c_re,
                       ssm_c_im=m_ssm_c_im, ssm_d=m_ssm_d, ssm_log_dt=m_ssm_log_dt, norm_ffn=m_norm_ffn),
             "v": dict(norm_mix=v_norm_mix, b_gate=v_b_gate, attn_sinks=v_attn_sinks, ssm_a_re=v_ssm_a_re,
                       ssm_a_im=v_ssm_a_im, ssm_b_re=v_ssm_b_re, ssm_b_im=v_ssm_b_im, ssm_c_re=v_ssm_c_re,
                       ssm_c_im=v_ssm_c_im, ssm_d=v_ssm_d, ssm_log_dt=v_ssm_log_dt, norm_ffn=v_norm_ffn)}
    finals = {"w": norm_final, "m": m_norm_final, "v": v_norm_final}
    convs = {"w": conv_w, "m": m_conv_w, "v": v_conv_w}
    mine = 4 * lax.axis_index("x") + 2 * lax.axis_index("y") + lax.axis_index("c")

    travel = {s: _travel_layout(big[s]) for s in "wmv"}
    stacked16 = [a.astype(BF16) for a in travel["w"]]
    rp = {"norm_mix": norm_mix[:, None], "norm_ffn": norm_ffn[:, None], "attn_sinks": attn_sinks[:, None],
          "b_gate": b_gate[:, None], "ssm_d": ssm_d[:, None]}
    sp = {"a_re": ssm_a_re, "a_im": ssm_a_im, "log_dt": ssm_log_dt[:, :, None],
          "bt_re": jnp.swapaxes(ssm_b_re, 2, 3), "bt_im": jnp.swapaxes(ssm_b_im, 2, 3),
          "c_re": ssm_c_re, "c_im": ssm_c_im}
    rows_tile = {"win_t": 368, "wffn_t": 352, "wout": 176, "wmix": 128, "branch_t": 128, "wglu": 64}
    core = lax.axis_index("c").astype(jnp.int32).reshape(1)
    no_tie = jnp.zeros((8, 128), F32)

    def place_own(srcs):
        return [lax.empty((N_DEV,) + s.shape, s.dtype) for s in srcs]

    def gather_chips(tag, i, kinds, after, extra=()):
        srcs = [stacked16[j][i] for j in kinds] + list(extra)
        s_sems, r_sems, arrays, token = _split_start(
            f"gather_chips_start_{tag}", srcs + place_own(srcs), 4 * len(srcs), _plan_gather_chips, after)
        return (tag, s_sems, r_sems, arrays), token

    def gather_pass(state, after):
        tag, s_sems, r_sems, arrays = state
        arrays = _split_wait(f"gather_chips_wait_{tag}", arrays, s_sems, r_sems, after, _plan_gather_chips)
        n = len(arrays) // 2
        s_sems, r_sems, lands, token = _split_start(
            f"gather_pass_start_{tag}", list(arrays[n:]), 4 * n, _plan_gather_pass)
        return (tag, s_sems, r_sems, lands), token

    def gather_done(state, after, kinds):
        tag, s_sems, r_sems, lands = state
        lands = _split_wait(f"gather_pass_wait_{tag}", lands, s_sems, r_sems, after, _plan_gather_pass)
        named = {KINDS[j][0]: a.reshape(N_DEV * KINDS[j][1], KINDS[j][2]) for a, j in zip(lands, kinds)}
        return named, list(lands[len(kinds):])

    all_kinds, mixer_kinds, ffn_kinds = tuple(range(len(KINDS))), (0, 3, 4, 5), (1, 2)
    no_hooks = {name: (lambda value: no_tie) for name in ("early", "pre_ffn", "mid", "late")}
    state, _ = gather_chips("0m", 0, mixer_kinds, None, extra=[jnp.pad(conv_w.reshape(6, 128), ((0, 2), (0, 0)))])
    mats = _ssm_mats(sp)
    tabs = _rope_tabs()
    state, _ = gather_pass(state, mats["c_im_neg"])
    ffn_state, tie = gather_chips("0f", 0, ffn_kinds, state[3][0])
    w_next, (conv_all,) = gather_done(state, tabs[2], mixer_kinds)
    conv_full = conv_all[:, :6].reshape(N_DEV, DEPTH, 3, 64).transpose(1, 2, 0, 3).reshape(DEPTH, 3, WIDTH)
    rp["conv_w"] = jnp.pad(conv_full, ((0, 0), (0, 5), (0, 0)))

    act = x[0]
    weights, kept = [], []
    for i in range(DEPTH):
        w_i, hooks, held = w_next, dict(no_hooks), {}
        if i == 0:
            def early(value, held=held):
                held["ffn"], token = gather_pass(ffn_state, value)
                return token

            def pre_ffn(value, w_i=w_i, held=held):
                w_i.update(gather_done(held["ffn"], value, ffn_kinds)[0])

            hooks.update(early=early, pre_ffn=pre_ffn)
        if i + 1 < DEPTH:
            state, tie = gather_chips(str(i + 1), i + 1, all_kinds, w_i["win_t"] if i > 0 else tie)

            def mid(value, state=state, held=held):
                held["next"], token = gather_pass(state, value)
                return token

            hooks.update(mid=mid)
        elif i > 0:
            tie = no_tie
        act, k = _layer_fwd(act, i, w_i, rp, mats, tabs, tie, hooks)
        if i + 1 < DEPTH:
            w_next, _ = gather_done(held["next"], act, all_kinds)
        weights.append(w_i)
        kept.append(k)
    loss_row, dx, d_norm_final = _loss_head(act, norm_final[None], loss_target[0])
    loss = lax.psum(loss_row[0, 0], ("x", "y", "c"))

    landings = [lax.empty((DEPTH, 4, r, c), BF16) for _, r, c in KINDS]
    landings0 = [lax.empty((1, 4, r, c), BF16) for _, r, c in KINDS]

    def scatter_pair(tag, kinds, grads, after):
        parts4 = [g.reshape(4, 2, KINDS[j][1], KINDS[j][2]) for g, j in zip(grads, kinds)]
        zones = [lax.empty((4, KINDS[j][1], KINDS[j][2]), BF16) for j in kinds]
        s_sems, r_sems, arrays, token = _split_start(
            f"scatter_pair_start_{tag}", parts4 + zones, 4 * len(kinds), _plan_scatter_pair, after)
        return (tag, kinds, s_sems, r_sems, arrays), token

    def scatter_chips(state, lands, slot, after):
        tag, kinds, s_sems, r_sems, arrays = state
        arrays = _split_wait(f"scatter_pair_wait_{tag}", arrays, s_sems, r_sems, after, _plan_scatter_pair)
        n = len(kinds)
        sums, mine_lands = [], []
        for k, j in enumerate(kinds):
            name = KINDS[j][0]
            chip_sum, land = _pair_sum(arrays[k], arrays[n + k], lands[j], slot, core, rows_tile[name],
                                       f"pair_sum_{name}")
            sums.append(chip_sum)
            mine_lands.append(land)
        s_sems, r_sems, arrays, token = _split_start(
            f"scatter_chips_start_{tag}", sums + mine_lands, 3 * n, _plan_scatter_chips(slot))
        return (tag, kinds, slot, s_sems, r_sems, arrays), token

    def scatter_done(state, lands, after):
        tag, kinds, slot, s_sems, r_sems, arrays = state
        arrays = _split_wait(f"scatter_chips_wait_{tag}", arrays, s_sems, r_sems, after, _plan_scatter_chips(slot))
        lands = list(lands)
        for k, j in enumerate(kinds):
            lands[j] = arrays[len(kinds) + k]
        return lands

    sg = [None] * DEPTH
    pending, tie = None, no_tie
    for i in reversed(range(DEPTH)):
        hooks, held = dict(no_hooks), {}
        if pending is not None:
            def mid(value, i=i, pending=pending, held=held):
                held["chips"], token = scatter_chips(pending, landings, i + 1, value[2])
                if i == 0:
                    held["ffn_pair"], token = scatter_pair("0f", ffn_kinds, value[:2], token)
                return token

            hooks.update(mid=mid)
        if i == 0:
            def late(value, held=held):
                held["ffn_chips"], token = scatter_chips(held["ffn_pair"], landings0, 0, value)
                return token

            hooks.update(late=late)
        dx, grads, sg[i] = _layer_bwd(dx, kept[i], i, weights[i], rp, mats, tabs, tie, hooks)
        if pending is not None:
            landings = scatter_done(held["chips"], landings, dx)
        if i > 0:
            pending, tie = scatter_pair(str(i), all_kinds, grads, dx)
        else:
            pending, _ = scatter_pair("0m", mixer_kinds, [grads[j] for j in mixer_kinds], dx)

    sgrads, conv_grad = _replicated_grads(sg, sp)

    def pack_small(t, final, conv):
        flat = [t[name].reshape(DEPTH, n) for name, n in SMALL]
        flat = jnp.concatenate([jnp.concatenate(flat, axis=1).reshape(-1), final.reshape(-1), conv.reshape(-1)])
        return jnp.pad(flat, (0, SMALL_ROWS * 128 - flat.shape[0])).reshape(SMALL_ROWS, 128)

    small_src = [pack_small(sgrads, d_norm_final, conv_grad).astype(BF16)]
    last, tie = scatter_chips(pending, landings0, 0, small_src[0])
    s_sems, r_sems, arrays, tie = _split_start(
        "gather_small_chips_start", small_src + place_own(small_src), 4, _plan_gather_chips, tie)
    small_state = ("small", s_sems, r_sems, arrays)

    big_out = [_adamw(landings[j], travel["w"][j], travel["m"][j], travel["v"][j], rows_tile[name],
                      "adamw_late_" + name, groups=(1, DEPTH), tie=tie) for j, (name, _, _) in enumerate(KINDS)]
    landings0 = scatter_done(held["ffn_chips"], landings0, big_out[-1][0])
    landings0 = scatter_done(last, landings0, big_out[-1][0])
    small_state, _ = gather_pass(small_state, landings0[0])
    big_out = [_adamw(landings0[j], travel["w"][j], travel["m"][j], travel["v"][j], rows_tile[name],
                      "adamw_first_" + name, groups=(0, 1), fill=big_out[j]) for j, (name, _, _) in enumerate(KINDS)]
    big_res = [_native_layout([big_out[j][kind] for j in range(len(KINDS))]) for kind in range(4)]

    zeros_conv = jnp.zeros((CONV_N,), F32)
    _, (sparts,) = gather_done(small_state, big_out[-1][0], ())
    sw, sm_, sv = (pack_small(small[s], finals[s], zeros_conv) for s in "wmv")
    small_out = _adamw(sparts[None], sw[None], sm_[None], sv[None], SMALL_ROWS // 8, "adamw_replicated")

    def unpack_small(p):
        flat = p.reshape(-1)
        per = flat[:DEPTH * SMALL_PER_LAYER].reshape(DEPTH, SMALL_PER_LAYER)
        out, off = {}, 0
        for name, n in SMALL:
            out[name] = per[:, off:off + n].reshape(small["w"][name].shape)
            off += n
        out["norm_final"] = flat[DEPTH * SMALL_PER_LAYER:DEPTH * SMALL_PER_LAYER + D_MODEL]
        return out

    small_res = [unpack_small(p) for p in small_out]

    conv_off = DEPTH * SMALL_PER_LAYER + D_MODEL
    conv_parts = sparts.reshape(N_DEV, -1)[:, conv_off:conv_off + CONV_N].reshape(N_DEV, DEPTH * 3, WIDTH)
    conv_parts = lax.dynamic_slice_in_dim(conv_parts, mine * 64, 64, axis=2)
    conv_res = _adamw(conv_parts[None], *(convs[s].reshape(1, DEPTH * 3, 64) for s in "wmv"), DEPTH * 3, "adamw_conv_w")

    order = ["norm_mix", "w_in", "b_gate", "attn_sinks", "w_attn_o", "conv_w", "w_conv_o", "ssm_a_re", "ssm_a_im",
             "ssm_b_re", "ssm_b_im", "ssm_c_re", "ssm_c_im", "ssm_d", "ssm_log_dt", "w_ssm_glu", "w_ssm_o",
             "w_mix_o", "norm_ffn", "w_ffn_in", "w_ffn_out", "norm_final"]
    outs = [loss, dx[None]]
    for kind in range(4):
        for name in order:
            if name == "conv_w":
                outs.append(conv_res[kind].reshape(DEPTH, 3, 64))
            elif name in big_res[kind]:
                outs.append(big_res[kind][name])
            else:
                outs.append(small_res[kind][name])
    return tuple(outs)
```

```python
import functools
import math

import jax
import jax.numpy as jnp
from jax import lax
from jax.experimental import pallas as pl
from jax.experimental.pallas import tpu as pltpu

F32 = jnp.float32
BF16 = jnp.bfloat16

N_DEV = 8
DEPTH = 4
SEQ = 2048
D_MODEL = 1024
N_Q_HEADS = 8
HEAD_DIM = 64
ATTN_W = 512
KV_W = 128
BLOCK = 128
N_BLOCKS = SEQ // BLOCK
ROPE_THETA = 500000.0
ROT_DIM = 16
NEG_INF = -1e30
WIDTH = 512
SSM_GROUPS = 32
SSM_GROUP = 16
SSM_STATE = 64
SLABS = 16
CHUNK = 256
N_CHUNKS = SEQ // CHUNK
GATE_W = 3 * D_MODEL
IN_COLS = 5888
FFN_H = 2816
NORM_EPS = 1e-6
LR, B1, B2, ADAM_EPS, WD, STEP = 0.001, 0.9, 0.999, 1e-08, 0.01, 10

COL_Q, COL_KV, COL_CBX, COL_U, COL_G = 0, 512, 768, 2304, 2816
PIECE_W = (512, 256, 512, 512, 512, 512, 3072)
PIECE_OFF = tuple(sum(PIECE_W[:i]) for i in range(len(PIECE_W)))

KINDS = (("win_t", 736, 1024), ("wffn_t", 704, 1024), ("wout", 352, 1024), ("wmix", 128, 1024),
         ("branch_t", 128, 1536), ("wglu", 64, 512))

SMALL = (("norm_mix", 1024), ("b_gate", 3072), ("attn_sinks", 8), ("ssm_a_re", 2048), ("ssm_a_im", 2048),
         ("ssm_b_re", 32768), ("ssm_b_im", 32768), ("ssm_c_re", 32768), ("ssm_c_im", 32768),
         ("ssm_d", 512), ("ssm_log_dt", 32), ("norm_ffn", 1024))
SMALL_PER_LAYER = sum(n for _, n in SMALL)
CONV_N = DEPTH * 3 * WIDTH
SMALL_ROWS = 4480

VMEM_LIMIT = 56 * 1024 * 1024
NT = (((1,), (1,)), ((), ()))
TN = (((0,), (0,)), ((), ()))
MESH_ID = pl.DeviceIdType.MESH
ANY = pl.BlockSpec(memory_space=pl.ANY)
HBM = pl.BlockSpec(memory_space=pltpu.HBM)
SEM = pl.BlockSpec(memory_space=pltpu.SEMAPHORE)
EFFECT = pltpu.SideEffectType.DATAFLOW_SIDE_EFFECTING


def _cp(**kw):
    return pltpu.CompilerParams(vmem_limit_bytes=VMEM_LIMIT, **kw)


def _full(shape):
    return pl.BlockSpec(shape, lambda *_: (0,) * len(shape))


def _resident(shape):
    return pl.BlockSpec(shape, lambda *_: (0,) * len(shape), pipeline_mode=pl.Buffered(1))


def _mm_tn(a, b, *, tm, tn, name):
    k, m = a.shape
    n = b.shape[1]

    def body(a_ref, b_ref, o_ref):
        o_ref[...] = lax.dot_general(a_ref[...].astype(BF16), b_ref[...].astype(BF16), TN,
                                     preferred_element_type=F32).astype(BF16)

    return pl.pallas_call(
        body, grid=(m // tm, n // tn),
        in_specs=[pl.BlockSpec((k, tm), lambda i, j: (0, i)), pl.BlockSpec((k, tn), lambda i, j: (0, j))],
        out_specs=pl.BlockSpec((tm, tn), lambda i, j: (i, j)),
        out_shape=jax.ShapeDtypeStruct((m, n), BF16), compiler_params=_cp(), name=name)(a, b)


def _rms_rows(xv, g):
    r = lax.rsqrt(jnp.mean(xv * xv, axis=-1, keepdims=True) + NORM_EPS)
    return ((xv * r) * g).astype(BF16)


def _rms_mm_in(x, g, wt, tie):
    tt = 512
    widths = (ATTN_W, 2 * KV_W, 3 * WIDTH, WIDTH, GATE_W)
    offs = (COL_Q, COL_KV, COL_CBX, COL_U, COL_G)

    def body(x_ref, g_ref, w_ref, tie_ref, q_ref, kv_ref, cbx_ref, u_ref, gl_ref, h_ref):
        h = _rms_rows(x_ref[...], g_ref[...])
        h_ref[...] = h
        prod = lax.dot_general(h, w_ref[...], NT, preferred_element_type=F32)
        for ref, o, w in zip((q_ref, kv_ref, cbx_ref, u_ref, gl_ref), offs, widths):
            ref[...] = prod[:, o:o + w]

    row = lambda w: pl.BlockSpec((tt, w), lambda i: (i, 0))
    sds = jax.ShapeDtypeStruct
    return pl.pallas_call(
        body, grid=(SEQ // tt,), in_specs=[row(D_MODEL), _full((1, D_MODEL)), _resident((IN_COLS, D_MODEL)), ANY],
        out_specs=[row(ATTN_W), row(2 * KV_W), row(3 * WIDTH), row(WIDTH), row(GATE_W), row(D_MODEL)],
        out_shape=[sds((SEQ, ATTN_W), F32), sds((SEQ, 2 * KV_W), F32), sds((SEQ, 3 * WIDTH), F32),
                   sds((SEQ, WIDTH), F32), sds((SEQ, GATE_W), F32), sds((SEQ, D_MODEL), BF16)],
        compiler_params=_cp(), name="rms_mm_in")(x, g, wt, tie)


def _rms_mm_ffn(x, g, wt):
    tt = 512

    def body(x_ref, g_ref, w_ref, o_ref, h_ref):
        h = _rms_rows(x_ref[...], g_ref[...])
        h_ref[...] = h
        o_ref[...] = lax.dot_general(h, w_ref[...], NT, preferred_element_type=F32)

    row = lambda w: pl.BlockSpec((tt, w), lambda i: (i, 0))
    return pl.pallas_call(
        body, grid=(SEQ // tt,), in_specs=[row(D_MODEL), _full((1, D_MODEL)), _resident((2 * FFN_H, D_MODEL))],
        out_specs=[row(2 * FFN_H), row(D_MODEL)],
        out_shape=[jax.ShapeDtypeStruct((SEQ, 2 * FFN_H), F32), jax.ShapeDtypeStruct((SEQ, D_MODEL), BF16)],
        compiler_params=_cp(), name="rms_mm_ffn")(x, g, wt)


def _mm_rmsbwd(pieces, wt, x, g, dres, name):
    tt = 512
    widths = [p.shape[1] for p in pieces]
    offs = [sum(widths[:i]) for i in range(len(widths))]
    n = len(pieces)

    def body(*refs):
        p_refs, (w_ref, x_ref, g_ref, r_ref, dx_ref, dg_ref) = refs[:n], refs[n:]

        @pl.when(pl.program_id(0) == 0)
        def _():
            dg_ref[...] = jnp.zeros_like(dg_ref)

        dh = jnp.zeros((tt, D_MODEL), F32)
        for p_ref, o, w in zip(p_refs, offs, widths):
            dh += jnp.dot(p_ref[...], w_ref[o:o + w, :], preferred_element_type=F32)
        xv = x_ref[...]
        r = lax.rsqrt(jnp.mean(xv * xv, axis=-1, keepdims=True) + NORM_EPS)
        xh = xv * r
        gy = dh * g_ref[...]
        dx_ref[...] = r_ref[...] + r * (gy - xh * jnp.mean(gy * xh, axis=-1, keepdims=True))
        dg_ref[...] += jnp.sum(dh * xh, axis=0, keepdims=True)

    row = lambda w: pl.BlockSpec((tt, w), lambda i: (i, 0))
    return pl.pallas_call(
        body, grid=(SEQ // tt,),
        in_specs=[row(w) for w in widths] + [_resident(wt.shape), row(D_MODEL), _full((1, D_MODEL)), row(D_MODEL)],
        out_specs=[row(D_MODEL), _full((1, D_MODEL))],
        out_shape=[jax.ShapeDtypeStruct((SEQ, D_MODEL), F32), jax.ShapeDtypeStruct((1, D_MODEL), F32)],
        compiler_params=_cp(), name=name)(*pieces, wt, x, g, dres)


def _tn_pieces(pieces, h):
    tk, tn = 512, 512
    nk = SEQ // tk
    n = len(pieces)

    def body(*refs):
        p_refs, (h_ref, o_ref, acc_ref) = refs[:n], refs[n:]
        kk = pl.program_id(1)

        @pl.when(kk == 0)
        def _():
            acc_ref[...] = jnp.zeros_like(acc_ref)

        hv = h_ref[...]
        for p_ref, o, w in zip(p_refs, PIECE_OFF, PIECE_W):
            acc_ref[o:o + w, :] += lax.dot_general(p_ref[...], hv, TN, preferred_element_type=F32)

        @pl.when(kk == nk - 1)
        def _():
            o_ref[...] = acc_ref[...].astype(BF16)

    return pl.pallas_call(
        body, grid=(D_MODEL // tn, nk),
        in_specs=[pl.BlockSpec((tk, w), lambda j, kk: (kk, 0)) for w in PIECE_W]
        + [pl.BlockSpec((tk, tn), lambda j, kk: (kk, j))],
        out_specs=pl.BlockSpec((IN_COLS, tn), lambda j, kk: (0, j)),
        out_shape=jax.ShapeDtypeStruct((IN_COLS, D_MODEL), BF16),
        scratch_shapes=[pltpu.VMEM((IN_COLS, tn), F32)], compiler_params=_cp(), name="tn_pieces")(*pieces, h)


def _tn_branches(dys, acts):
    tk = 512
    nk = SEQ // tk

    def body(d0, d1, d2, a0, a1, a2, o_ref, acc_ref):
        kk = pl.program_id(0)

        @pl.when(kk == 0)
        def _():
            acc_ref[...] = jnp.zeros_like(acc_ref)

        for j, (d, a) in enumerate(((d0, a0), (d1, a1), (d2, a2))):
            acc_ref[:, WIDTH * j:WIDTH * (j + 1)] += lax.dot_general(d[...], a[...], TN, preferred_element_type=F32)

        @pl.when(kk == nk - 1)
        def _():
            o_ref[...] = acc_ref[...].astype(BF16)

    row = lambda w: pl.BlockSpec((tk, w), lambda kk: (kk, 0))
    return pl.pallas_call(
        body, grid=(nk,), in_specs=[row(D_MODEL)] * 3 + [row(WIDTH)] * 3,
        out_specs=_full((D_MODEL, 3 * WIDTH)), out_shape=jax.ShapeDtypeStruct((D_MODEL, 3 * WIDTH), BF16),
        scratch_shapes=[pltpu.VMEM((D_MODEL, 3 * WIDTH), F32)], compiler_params=_cp(), name="tn_branches",
    )(*dys, *acts)


def _rope(t, c, a, b):
    return t * c + pltpu.roll(t, 120, axis=1) * a + pltpu.roll(t, 8, axis=1) * b


def _rope_t(d, c, a, b):
    return d * c + pltpu.roll(d * a, 8, axis=1) + pltpu.roll(d * b, 120, axis=1)


def _band_sides(band):
    left = lax.broadcasted_iota(jnp.int32, band.shape, 1) < HEAD_DIM
    h0 = jnp.where(left, band, 0.0)
    h1 = jnp.where(left, 0.0, band)
    r0 = pltpu.roll(h0, HEAD_DIM, axis=1)
    r1 = pltpu.roll(h1, HEAD_DIM, axis=1)
    return ((h0.astype(BF16), r0.astype(BF16)), (r1.astype(BF16), h1.astype(BF16)))


def _attn_mask(i):
    qi = lax.broadcasted_iota(jnp.int32, (2 * BLOCK, 2 * BLOCK), 0) % BLOCK
    kj = lax.broadcasted_iota(jnp.int32, (2 * BLOCK, 2 * BLOCK), 1)
    delta = qi + BLOCK - kj
    return (delta >= 0) & (delta < BLOCK) & ((kj >= BLOCK) | (i > 0))


def _attn_probs(s, ok, sink):
    s = jnp.where(ok, s * (HEAD_DIM ** -0.5), NEG_INF)
    m = jnp.maximum(jnp.max(s, axis=-1, keepdims=True), sink)
    p = jnp.exp(s - m)
    es = jnp.exp(sink - m)
    inv = 1.0 / (jnp.sum(p, axis=-1, keepdims=True) + es)
    return p * inv, es * inv


def _kv_group(qs, ks, vs, kh, sink_ref):
    q2 = jnp.concatenate([qs[2 * kh], qs[2 * kh + 1]], axis=0)
    kst = jnp.concatenate([ks[kh][0], ks[kh][1]], axis=0)
    vst = jnp.concatenate([vs[kh][0], vs[kh][1]], axis=0)
    top = lax.broadcasted_iota(jnp.int32, (2 * BLOCK, 1), 0) < BLOCK
    sinks = [jnp.where(top, sink_ref[0, 4 * kh + h], sink_ref[0, 4 * kh + 2 + h]) for h in range(2)]
    return q2, kst, vst, sinks


def _attn_load(q_ref, kvc_ref, kvp_ref, tc_ref, ta_ref, tb_ref, pc_ref, pa_ref, pb_ref):
    c, a, b = tc_ref[...], ta_ref[...], tb_ref[...]
    kc = _rope(kvc_ref[:, :KV_W], c, a, b)
    kp = _rope(kvp_ref[:, :KV_W], pc_ref[...], pa_ref[...], pb_ref[...])
    kband = jnp.concatenate([kp, kc], axis=0)
    vband = jnp.concatenate([kvp_ref[:, KV_W:], kvc_ref[:, KV_W:]], axis=0)
    qs = [_rope(q_ref[:, 128 * j:128 * (j + 1)], c, a, b).astype(BF16) for j in range(4)]
    return qs, _band_sides(kband), _band_sides(vband), (c, a, b)


def _attn_specs(clamp):
    cur = lambda i: (clamp(i), 0)
    prev = lambda i: (jnp.maximum(clamp(i) - 1, 0), 0)
    return [
        pl.BlockSpec((BLOCK, ATTN_W), cur), pl.BlockSpec((BLOCK, 2 * KV_W), cur),
        pl.BlockSpec((BLOCK, 2 * KV_W), prev),
        pl.BlockSpec((BLOCK, 128), cur), pl.BlockSpec((BLOCK, 128), cur), pl.BlockSpec((BLOCK, 128), cur),
        pl.BlockSpec((BLOCK, 128), prev), pl.BlockSpec((BLOCK, 128), prev), pl.BlockSpec((BLOCK, 128), prev),
        pl.BlockSpec(memory_space=pltpu.SMEM),
    ]


def _attn_fwd(q, kv, tabs, sinks):
    tc, ta, tb = tabs

    def body(q_ref, kvc_ref, kvp_ref, tc_ref, ta_ref, tb_ref, pc_ref, pa_ref, pb_ref, sink_ref, o_ref):
        i = pl.program_id(0)
        qs, ks, vs, _ = _attn_load(q_ref, kvc_ref, kvp_ref, tc_ref, ta_ref, tb_ref, pc_ref, pa_ref, pb_ref)
        ok = _attn_mask(i)
        for kh in range(2):
            q2, kst, vst, sinks = _kv_group(qs, ks, vs, kh, sink_ref)
            s = lax.dot_general(q2, kst, NT, preferred_element_type=F32)
            pn = [_attn_probs(s[:, 2 * BLOCK * h:2 * BLOCK * (h + 1)], ok, sinks[h])[0].astype(BF16) for h in range(2)]
            o2 = jnp.dot(jnp.concatenate(pn, axis=1), vst, preferred_element_type=F32).astype(BF16)
            for r in range(2):
                j = 2 * kh + r
                o_ref[:, 128 * j:128 * (j + 1)] = o2[BLOCK * r:BLOCK * (r + 1)]

    return pl.pallas_call(
        body, grid=(N_BLOCKS,), in_specs=_attn_specs(lambda i: i),
        out_specs=pl.BlockSpec((BLOCK, ATTN_W), lambda i: (i, 0)),
        out_shape=jax.ShapeDtypeStruct((SEQ, ATTN_W), BF16), compiler_params=_cp(), name="attn_fwd",
    )(q, kv, kv, tc, ta, tb, tc, ta, tb, sinks)


def _attn_bwd(q, kv, tabs, sinks, do):
    tc, ta, tb = tabs
    last = N_BLOCKS - 1
    clamp = lambda i: jnp.minimum(i, last)

    def place(full, side, kh):
        left = lax.broadcasted_iota(jnp.int32, full.shape, 1) < HEAD_DIM
        valid = jnp.where(left, full, 0.0) if side == 0 else jnp.where(left, 0.0, full)
        return valid if side == kh else pltpu.roll(valid, HEAD_DIM, axis=1)

    def body(q_ref, kvc_ref, kvp_ref, tc_ref, ta_ref, tb_ref, pc_ref, pa_ref, pb_ref, sink_ref, do_ref,
             dq_ref, dkv_ref, ds_ref, carry_ref):
        i = pl.program_id(0)

        @pl.when(i == 0)
        def _():
            ds_ref[...] = jnp.zeros_like(ds_ref)
            carry_ref[...] = jnp.zeros_like(carry_ref)

        @pl.when(i > last)
        def _():
            dkv_ref[...] = carry_ref[...].astype(BF16)

        @pl.when(i <= last)
        def _():
            qs, ks, vs, (c, a, b) = _attn_load(q_ref, kvc_ref, kvp_ref, tc_ref, ta_ref, tb_ref,
                                               pc_ref, pa_ref, pb_ref)
            ok = _attn_mask(i)
            dk = jnp.zeros((2 * BLOCK, 128), F32)
            dv = jnp.zeros((2 * BLOCK, 128), F32)
            dsink = jnp.zeros((1, 128), F32)
            lane = lax.broadcasted_iota(jnp.int32, (1, 128), 1)
            for kh in range(2):
                q2, kst, vst, sinks = _kv_group(qs, ks, vs, kh, sink_ref)
                do2 = jnp.concatenate([do_ref[:, 128 * (2 * kh + r):128 * (2 * kh + r + 1)] for r in range(2)],
                                      axis=0).astype(BF16)
                s = lax.dot_general(q2, kst, NT, preferred_element_type=F32)
                dp = lax.dot_general(do2, vst, NT, preferred_element_type=F32)
                pns, dss = [], []
                for h in range(2):
                    cols = slice(2 * BLOCK * h, 2 * BLOCK * (h + 1))
                    pn, ps = _attn_probs(s[:, cols], ok, sinks[h])
                    dr = jnp.sum(pn * dp[:, cols], axis=-1, keepdims=True)
                    pns.append(pn.astype(BF16))
                    dss.append((pn * (dp[:, cols] - dr) * (HEAD_DIM ** -0.5)).astype(BF16))
                    for r in range(2):
                        part = -jnp.sum((ps * dr)[BLOCK * r:BLOCK * (r + 1)])
                        dsink += jnp.where(lane == 4 * kh + 2 * r + h, part, 0.0)
                ds2, pn2 = jnp.concatenate(dss, axis=1), jnp.concatenate(pns, axis=1)
                dq2 = jnp.dot(ds2, kst, preferred_element_type=F32)
                dk2 = lax.dot_general(ds2, q2, TN, preferred_element_type=F32)
                dv2 = lax.dot_general(pn2, do2, TN, preferred_element_type=F32)
                for h in range(2):
                    dk += place(dk2[2 * BLOCK * h:2 * BLOCK * (h + 1)], h, kh)
                    dv += place(dv2[2 * BLOCK * h:2 * BLOCK * (h + 1)], h, kh)
                for r in range(2):
                    j = 2 * kh + r
                    dq_ref[:, 128 * j:128 * (j + 1)] = _rope_t(dq2[BLOCK * r:BLOCK * (r + 1)], c, a, b).astype(BF16)
            ds_ref[...] += dsink
            dk_prev = _rope_t(dk[:BLOCK], pc_ref[...], pa_ref[...], pb_ref[...])
            dk_cur = _rope_t(dk[BLOCK:], c, a, b)
            prev = jnp.concatenate([dk_prev, dv[:BLOCK]], axis=1)
            dkv_ref[...] = (carry_ref[...] + prev).astype(BF16)
            carry_ref[...] = jnp.concatenate([dk_cur, dv[BLOCK:]], axis=1)

    return pl.pallas_call(
        body, grid=(N_BLOCKS + 1,),
        in_specs=_attn_specs(clamp) + [pl.BlockSpec((BLOCK, ATTN_W), lambda i: (clamp(i), 0))],
        out_specs=[pl.BlockSpec((BLOCK, ATTN_W), lambda i: (clamp(i), 0)),
                   pl.BlockSpec((BLOCK, 2 * KV_W), lambda i: (jnp.maximum(i - 1, 0), 0)),
                   pl.BlockSpec((1, 128), lambda i: (0, 0))],
        out_shape=[jax.ShapeDtypeStruct((SEQ, ATTN_W), BF16), jax.ShapeDtypeStruct((SEQ, 2 * KV_W), BF16),
                   jax.ShapeDtypeStruct((1, 128), F32)],
        scratch_shapes=[pltpu.VMEM((BLOCK, 2 * KV_W), F32)], compiler_params=_cp(), name="attn_bwd",
    )(q, kv, kv, tc, ta, tb, tc, ta, tb, sinks, do)


def _shift_down(z, k):
    row = lax.broadcasted_iota(jnp.int32, z.shape, 0)
    return jnp.where(row < k, 0.0, pltpu.roll(z, k, axis=0))


def _shift_up(z, k):
    n = z.shape[0]
    row = lax.broadcasted_iota(jnp.int32, z.shape, 0)
    return jnp.where(row >= n - k, 0.0, pltpu.roll(z, n - k, axis=0))


def _conv_specs():
    nb = WIDTH // 128
    return [pl.BlockSpec((SEQ, 128), lambda j: (0, j)), pl.BlockSpec((SEQ, 128), lambda j: (0, nb + j)),
            pl.BlockSpec((SEQ, 128), lambda j: (0, 2 * nb + j)), pl.BlockSpec((None, 8, 128), lambda j: (0, 0, j))]


def _conv_fwd(cbx, cw, layer):
    def body(cb_ref, cc_ref, cx_ref, w_ref, o_ref):
        z = cc_ref[...] * cx_ref[...]
        s = w_ref[0:1, :] * _shift_down(z, 2) + w_ref[1:2, :] * _shift_down(z, 1) + w_ref[2:3, :] * z
        o_ref[...] = (cb_ref[...] * s).astype(BF16)

    specs = _conv_specs()
    specs[3] = pl.BlockSpec((None, 8, 128), lambda j: (layer, 0, j))
    return pl.pallas_call(
        body, grid=(WIDTH // 128,), in_specs=specs,
        out_specs=pl.BlockSpec((SEQ, 128), lambda j: (0, j)),
        out_shape=jax.ShapeDtypeStruct((SEQ, WIDTH), BF16), compiler_params=_cp(), name="conv_fwd",
    )(cbx, cbx, cbx, cw)


def _conv_bwd(cbx, cw, layer, dout, tie):
    def body(cb_ref, cc_ref, cx_ref, w_ref, do_ref, tie_ref, dcb_ref, dcc_ref, dcx_ref, dw_ref):
        cc, cx = cc_ref[...], cx_ref[...]
        z = cc * cx
        z1, z2 = _shift_down(z, 1), _shift_down(z, 2)
        w0, w1, w2 = w_ref[0:1, :], w_ref[1:2, :], w_ref[2:3, :]
        dout = do_ref[...]
        ds = dout * cb_ref[...]
        dcb_ref[...] = (dout * (w0 * z2 + w1 * z1 + w2 * z)).astype(BF16)
        dz = w2 * ds + w1 * _shift_up(ds, 1) + w0 * _shift_up(ds, 2)
        dcc_ref[...] = (dz * cx).astype(BF16)
        dcx_ref[...] = (dz * cc).astype(BF16)
        rows = [jnp.sum(ds * zz, axis=0, keepdims=True) for zz in (z2, z1, z)]
        dw_ref[...] = jnp.concatenate(rows + [jnp.zeros((5, 128), F32)], axis=0)

    col = lambda j: (0, j)
    specs = _conv_specs()
    specs[3] = pl.BlockSpec((None, 8, 128), lambda j: (layer, 0, j))
    return pl.pallas_call(
        body, grid=(WIDTH // 128,), in_specs=specs + [pl.BlockSpec((SEQ, 128), col), ANY],
        out_specs=[pl.BlockSpec((SEQ, 128), col), pl.BlockSpec((SEQ, 128), col), pl.BlockSpec((SEQ, 128), col),
                   pl.BlockSpec((8, 128), col)],
        out_shape=[jax.ShapeDtypeStruct((SEQ, WIDTH), BF16)] * 3 + [jax.ShapeDtypeStruct((8, WIDTH), F32)],
        compiler_params=_cp(), name="conv_bwd",
    )(cbx, cbx, cbx, cw, dout, tie)


def _ssm_prep_math(a_re, a_im, log_dt, bt_re, bt_im):
    dt = jnp.exp(log_dt)
    er = jnp.exp(a_re * dt)
    lr = er * jnp.cos(a_im * dt)
    li = er * jnp.sin(a_im * dt)
    n2 = a_re * a_re + a_im * a_im
    cr = ((lr - 1.0) * a_re + li * a_im) / n2
    ci = (li * a_re - (lr - 1.0) * a_im) / n2
    cr3, ci3 = cr[:, None, :], ci[:, None, :]
    return lr, li, cr3 * bt_re - ci3 * bt_im, cr3 * bt_im + ci3 * bt_re


_GS = (SSM_GROUPS, SSM_STATE)
_GHS = (SSM_GROUPS, SSM_GROUP, SSM_STATE)


def _layered(shape):
    return pl.BlockSpec((None,) + shape, lambda l: (l,) + (0,) * len(shape))


def _ssm_prep(a_re, a_im, log_dt, bt_re, bt_im):
    def body(ar, ai, ld, br, bi, o0, o1, o2, o3):
        outs = _ssm_prep_math(ar[...], ai[...], ld[...], br[...], bi[...])
        for o, v in zip((o0, o1, o2, o3), outs):
            o[...] = v

    shapes = [_GS, _GS, _GHS, _GHS]
    return pl.pallas_call(
        body, grid=(DEPTH,), in_specs=[_layered(s) for s in (_GS, _GS, (SSM_GROUPS, 1), _GHS, _GHS)],
        out_specs=[_layered(s) for s in shapes],
        out_shape=[jax.ShapeDtypeStruct((DEPTH,) + s, F32) for s in shapes],
        name="ssm_prep")(a_re, a_im, log_dt, bt_re, bt_im)


def _ssm_prep_bwd(a_re, a_im, log_dt, bt_re, bt_im, cots):
    def body(ar, ai, ld, br, bi, c0, c1, c2, c3, o0, o1, o2, o3, o4):
        _, vjp = jax.vjp(_ssm_prep_math, ar[...], ai[...], ld[...], br[...], bi[...])
        for o, v in zip((o0, o1, o2, o3, o4), vjp((c0[...], c1[...], c2[...], c3[...]))):
            o[...] = v

    ins = (_GS, _GS, (SSM_GROUPS, 1), _GHS, _GHS)
    return pl.pallas_call(
        body, grid=(DEPTH,), in_specs=[_layered(s) for s in ins + (_GS, _GS, _GHS, _GHS)],
        out_specs=[_layered(s) for s in ins],
        out_shape=[jax.ShapeDtypeStruct((DEPTH,) + s, F32) for s in ins],
        name="ssm_prep_bwd")(a_re, a_im, log_dt, bt_re, bt_im, *cots)


LANES_G = 512
N_LANE_GROUPS = SSM_GROUPS * SSM_STATE // LANES_G


def _scan_order(a):
    return a.reshape(N_CHUNKS, CHUNK, -1).transpose(1, 0, 2).reshape(a.shape)


def _time_order(a):
    return a.reshape(CHUNK, N_CHUNKS, -1).transpose(1, 0, 2).reshape(a.shape)


def _scan_in_place(xr_ref, xi_ref, ar, ai, reverse):
    shape = (N_CHUNKS, xr_ref.shape[1])
    ar, ai = jnp.broadcast_to(ar, shape), jnp.broadcast_to(ai, shape)

    def rows(tau):
        t = (CHUNK - 1 - tau) if reverse else tau
        return pl.ds(pl.multiple_of(t * N_CHUNKS, N_CHUNKS), N_CHUNKS)

    def step(tau, carry):
        sr, si = carry
        return ar * sr - ai * si + xr_ref[rows(tau), :], ar * si + ai * sr + xi_ref[rows(tau), :]

    zero = jnp.zeros(shape, F32)
    er, ei = lax.fori_loop(0, CHUNK, step, (zero, zero), unroll=8)
    qr, qi = ar, ai
    for _ in range(8):
        qr, qi = qr * qr - qi * qi, 2.0 * qr * qi
    shift = _shift_up if reverse else _shift_down
    for k in (1, 2, 4):
        sr, si = shift(er, k), shift(ei, k)
        er, ei = er + qr * sr - qi * si, ei + qr * si + qi * sr
        qr, qi = qr * qr - qi * qi, 2.0 * qr * qi
    start = (shift(er, 1), shift(ei, 1))

    def write(tau, carry):
        sr, si = step(tau, carry)
        xr_ref[rows(tau), :] = sr
        xi_ref[rows(tau), :] = si
        return sr, si

    return write, start


def _ssm_specs(layer):
    col = lambda w: pl.BlockSpec((SEQ, w), lambda g: (0, g))
    diag = pl.BlockSpec((None, None, 128, LANES_G), lambda g: (layer, g, 0, 0))
    vec = pl.BlockSpec((None, 1, LANES_G), lambda g: (layer, 0, g))
    return col, diag, vec


def _ssm_fwd(u, mats, layer, d):
    def body(u_ref, d_ref, br_ref, bi_ref, cr_ref, ci_ref, ar_ref, ai_ref, xr_ref, xi_ref, y_ref):
        uv = u_ref[...].astype(BF16)
        xr_ref[...] = jnp.dot(uv, br_ref[...], preferred_element_type=F32)
        xi_ref[...] = jnp.dot(uv, bi_ref[...], preferred_element_type=F32)
        write, start = _scan_in_place(xr_ref, xi_ref, ar_ref[...], ai_ref[...], False)
        lax.fori_loop(0, CHUNK, write, start, unroll=8)
        y = lax.dot_general(xr_ref[...].astype(BF16), cr_ref[...], NT, preferred_element_type=F32)
        y += lax.dot_general(xi_ref[...].astype(BF16), ci_ref[...], NT, preferred_element_type=F32)
        y_ref[...] = y + d_ref[...] * u_ref[...]

    col, diag, vec = _ssm_specs(layer)
    return pl.pallas_call(
        body, grid=(N_LANE_GROUPS,),
        in_specs=[col(128), pl.BlockSpec((None, 1, 128), lambda g: (layer, 0, g)),
                  diag, diag, diag, diag, vec, vec],
        out_specs=[col(LANES_G), col(LANES_G), col(128)],
        out_shape=[jax.ShapeDtypeStruct((SEQ, SSM_GROUPS * SSM_STATE), F32)] * 2
        + [jax.ShapeDtypeStruct((SEQ, WIDTH), F32)],
        compiler_params=_cp(), name="ssm_fwd",
    )(u, d, mats["b_re"], mats["b_im"], mats["c_re"], mats["c_im_neg"], mats["a_re"], mats["a_im"])


def _ssm_bwd(dy16, x_re, x_im, u, mats, layer, d):
    def body(dy_ref, u_ref, d_ref, xr_ref, xi_ref, br_ref, bi_ref, cr_ref, ci_ref, ar_ref, ai_ref,
             du_ref, dar_ref, dai_ref, dbr_ref, dbi_ref, dcr_ref, dci_ref, lr_ref, li_ref):
        dy = dy_ref[...]
        lr_ref[...] = jnp.dot(dy, cr_ref[...], preferred_element_type=F32)
        li_ref[...] = jnp.dot(dy, ci_ref[...], preferred_element_type=F32)
        write, start = _scan_in_place(lr_ref, li_ref, ar_ref[...], -ai_ref[...], True)

        def rows(t):
            return pl.ds(pl.multiple_of(t * N_CHUNKS, N_CHUNKS), N_CHUNKS)

        def grad(acc, lam, xpr, xpi):
            return acc[0] + xpr * lam[0] + xpi * lam[1], acc[1] + xpr * lam[1] - xpi * lam[0]

        def down(tau, carry):
            lam = write(tau, carry[0])
            t = CHUNK - 2 - tau
            return lam, grad(carry[1], lam, xr_ref[rows(t), :], xi_ref[rows(t), :])

        zero = jnp.zeros((N_CHUNKS, LANES_G), F32)
        lam, acc = lax.fori_loop(0, CHUNK - 1, down, (start, (zero, zero)), unroll=5)
        lam = write(CHUNK - 1, lam)
        last = rows(CHUNK - 1)
        acc = grad(acc, lam, _shift_down(xr_ref[last, :], 1), _shift_down(xi_ref[last, :], 1))
        dar_ref[...] = jnp.sum(acc[0], axis=0, keepdims=True)
        dai_ref[...] = jnp.sum(acc[1], axis=0, keepdims=True)

        l_re, l_im = lr_ref[...].astype(BF16), li_ref[...].astype(BF16)
        du = lax.dot_general(l_re, br_ref[...], NT, preferred_element_type=F32)
        du += lax.dot_general(l_im, bi_ref[...], NT, preferred_element_type=F32)
        du_ref[...] = (du + dy.astype(F32) * d_ref[...]).astype(BF16)
        uv = u_ref[...].astype(BF16)
        dbr_ref[...] = lax.dot_general(uv, l_re, TN, preferred_element_type=F32)
        dbi_ref[...] = lax.dot_general(uv, l_im, TN, preferred_element_type=F32)
        dcr_ref[...] = lax.dot_general(dy, xr_ref[...].astype(BF16), TN, preferred_element_type=F32)
        dci_ref[...] = lax.dot_general(dy, xi_ref[...].astype(BF16), TN, preferred_element_type=F32)

    col, diag, vec = _ssm_specs(layer)
    out_vec = pl.BlockSpec((1, LANES_G), lambda g: (0, g))
    out_blk = pl.BlockSpec((None, 128, LANES_G), lambda g: (g, 0, 0))
    sds = jax.ShapeDtypeStruct
    return pl.pallas_call(
        body, grid=(N_LANE_GROUPS,),
        in_specs=[col(128), col(128), pl.BlockSpec((None, 1, 128), lambda g: (layer, 0, g)),
                  col(LANES_G), col(LANES_G), diag, diag, diag, diag, vec, vec],
        out_specs=[col(128), out_vec, out_vec, out_blk, out_blk, out_blk, out_blk],
        out_shape=[sds((SEQ, WIDTH), BF16)] + [sds((1, SSM_GROUPS * SSM_STATE), F32)] * 2
        + [sds((N_LANE_GROUPS, 128, LANES_G), F32)] * 4,
        scratch_shapes=[pltpu.VMEM((SEQ, LANES_G), F32)] * 2, compiler_params=_cp(), name="ssm_bwd",
    )(dy16, u, d, x_re, x_im, mats["b_re"], mats["b_im"], mats["c_re"], mats["c_im_neg"],
      mats["a_re"], mats["a_im"])


_GELU_C = math.sqrt(2.0 / math.pi)


def _gelu(y):
    return 0.5 * y * (1.0 + jnp.tanh(_GELU_C * (y + 0.044715 * (y * y * y))))


def _glu_fwd(y, wglu):
    tt = 512

    def body(y_ref, w_ref, z_ref):
        ys = _gelu(y_ref[...])
        a = jnp.dot(ys.astype(BF16), w_ref[...], preferred_element_type=F32)
        z_ref[...] = (ys * jax.nn.sigmoid(a)).astype(BF16)

    blk = pl.BlockSpec((tt, WIDTH), lambda i: (i, 0))
    return pl.pallas_call(body, grid=(SEQ // tt,), in_specs=[blk, _full((WIDTH, WIDTH))], out_specs=blk,
                          out_shape=jax.ShapeDtypeStruct((SEQ, WIDTH), BF16), compiler_params=_cp(),
                          name="glu_fwd")(y, wglu)


def _glu_bwd(y, wglu, dz, u):
    tt = 512

    def body(y_ref, w_ref, dz_ref, u_ref, dy_ref, ys_ref, da_ref, dd_ref):
        @pl.when(pl.program_id(0) == 0)
        def _():
            dd_ref[...] = jnp.zeros_like(dd_ref)

        yv = y_ref[...]
        t = jnp.tanh(_GELU_C * (yv + 0.044715 * (yv * yv * yv)))
        ys = 0.5 * yv * (1.0 + t)
        ysb = ys.astype(BF16)
        sg = jax.nn.sigmoid(jnp.dot(ysb, w_ref[...], preferred_element_type=F32))
        dz = dz_ref[...].astype(F32)
        da = (dz * ys * sg * (1.0 - sg)).astype(BF16)
        dys = dz * sg + lax.dot_general(da, w_ref[...], NT, preferred_element_type=F32)
        dy = dys * (0.5 * (1.0 + t) + 0.5 * yv * (1.0 - t * t) * _GELU_C * (1.0 + 3 * 0.044715 * (yv * yv)))
        dy_ref[...] = dy.astype(BF16)
        ys_ref[...] = ysb
        da_ref[...] = da
        dd_ref[...] += jnp.sum(dy * u_ref[...], axis=0, keepdims=True)

    blk = pl.BlockSpec((tt, WIDTH), lambda i: (i, 0))
    return pl.pallas_call(
        body, grid=(SEQ // tt,), in_specs=[blk, _full((WIDTH, WIDTH)), blk, blk],
        out_specs=[blk, blk, blk, _full((1, WIDTH))],
        out_shape=[jax.ShapeDtypeStruct((SEQ, WIDTH), BF16)] * 3 + [jax.ShapeDtypeStruct((1, WIDTH), F32)],
        compiler_params=_cp(), name="glu_bwd")(y, wglu, dz, u)


def _mix_specs(tt, layer):
    row = lambda w: pl.BlockSpec((tt, w), lambda i: (i, 0))
    gate = lambda j: pl.BlockSpec((tt, D_MODEL), lambda i: (i, j))
    wo = lambda j: pl.BlockSpec((D_MODEL, WIDTH), lambda i: (0, j))
    return [row(D_MODEL), row(WIDTH), row(WIDTH), row(WIDTH), gate(0), gate(1), gate(2),
            pl.BlockSpec((None, 1, GATE_W), lambda i: (layer, 0, 0)), wo(0), wo(1), wo(2),
            _full((D_MODEL, D_MODEL))]


def _mix_branches(o_ref, c_ref, z_ref, g_refs, b_ref, wa_ref, wc_ref, ws_ref):
    ys = [lax.dot_general(r[...], w[...], NT, preferred_element_type=F32)
          for r, w in ((o_ref, wa_ref), (c_ref, wc_ref), (z_ref, ws_ref))]
    gates = [jax.nn.sigmoid(g_refs[j][...] + b_ref[:, D_MODEL * j:D_MODEL * (j + 1)]) for j in range(3)]
    return ys, gates


def _mix_fwd(x, o, cv, z, glog, b_gate, layer, wbt, wmix, tie):
    tt = 256

    def body(x_ref, o_ref, c_ref, z_ref, g0, g1, g2, b_ref, wa_ref, wc_ref, ws_ref, wm_ref, tie_ref, x1_ref):
        ys, gates = _mix_branches(o_ref, c_ref, z_ref, (g0, g1, g2), b_ref, wa_ref, wc_ref, ws_ref)
        merged = gates[0] * ys[0] + gates[1] * ys[1] + gates[2] * ys[2]
        x1_ref[...] = x_ref[...] + jnp.dot(merged.astype(BF16), wm_ref[...], preferred_element_type=F32)

    return pl.pallas_call(
        body, grid=(SEQ // tt,), in_specs=_mix_specs(tt, layer) + [ANY],
        out_specs=pl.BlockSpec((tt, D_MODEL), lambda i: (i, 0)),
        out_shape=jax.ShapeDtypeStruct((SEQ, D_MODEL), F32), compiler_params=_cp(), name="mix_fwd",
    )(x, o, cv, z, glog, glog, glog, b_gate, wbt, wbt, wbt, wmix, tie)


def _mix_bwd(dx1, o, cv, z, glog, b_gate, layer, wbt, wmix, tie):
    tt = 256

    def body(dx_ref, o_ref, c_ref, z_ref, g0, g1, g2, b_ref, wa_ref, wc_ref, ws_ref, wm_ref, tie_ref,
             mg_ref, dya_ref, dyc_ref, dys_ref, do_ref, dc_ref, dz_ref, dgl_ref, db_ref):
        @pl.when(pl.program_id(0) == 0)
        def _():
            db_ref[...] = jnp.zeros_like(db_ref)

        ys, gates = _mix_branches(o_ref, c_ref, z_ref, (g0, g1, g2), b_ref, wa_ref, wc_ref, ws_ref)
        mg_ref[...] = (gates[0] * ys[0] + gates[1] * ys[1] + gates[2] * ys[2]).astype(BF16)
        dm = lax.dot_general(dx_ref[...].astype(BF16), wm_ref[...], NT, preferred_element_type=F32)
        for j, (dy_ref, w_ref, d_ref) in enumerate(((dya_ref, wa_ref, do_ref), (dyc_ref, wc_ref, dc_ref),
                                                    (dys_ref, ws_ref, dz_ref))):
            dy = (dm * gates[j]).astype(BF16)
            dy_ref[...] = dy
            d_ref[...] = jnp.dot(dy, w_ref[...], preferred_element_type=F32)
            dgl = dm * ys[j] * gates[j] * (1.0 - gates[j])
            dgl_ref[:, D_MODEL * j:D_MODEL * (j + 1)] = dgl.astype(BF16)
            db_ref[:, D_MODEL * j:D_MODEL * (j + 1)] += jnp.sum(dgl, axis=0, keepdims=True)

    row = lambda w: pl.BlockSpec((tt, w), lambda i: (i, 0))
    sds = jax.ShapeDtypeStruct
    return pl.pallas_call(
        body, grid=(SEQ // tt,), in_specs=_mix_specs(tt, layer) + [ANY],
        out_specs=[row(D_MODEL)] * 4 + [row(WIDTH)] * 3 + [row(GATE_W), _full((1, GATE_W))],
        out_shape=[sds((SEQ, D_MODEL), BF16)] * 4 + [sds((SEQ, WIDTH), F32)] * 3
        + [sds((SEQ, GATE_W), BF16), sds((1, GATE_W), F32)],
        compiler_params=_cp(), name="mix_bwd",
    )(dx1, o, cv, z, glog, glog, glog, b_gate, wbt, wbt, wbt, wmix, tie)


def _ffn_out_fwd(x1, gu, wout, tie):
    tt = 256

    def body(x_ref, gt_ref, up_ref, w_ref, tie_ref, o_ref):
        gt = gt_ref[...]
        act = (gt * jax.nn.sigmoid(gt) * up_ref[...]).astype(BF16)
        o_ref[...] = x_ref[...] + jnp.dot(act, w_ref[...], preferred_element_type=F32)

    return pl.pallas_call(
        body, grid=(SEQ // tt,),
        in_specs=[pl.BlockSpec((tt, D_MODEL), lambda i: (i, 0)), pl.BlockSpec((tt, FFN_H), lambda i: (i, 0)),
                  pl.BlockSpec((tt, FFN_H), lambda i: (i, 1)), _full((FFN_H, D_MODEL)), ANY],
        out_specs=pl.BlockSpec((tt, D_MODEL), lambda i: (i, 0)),
        out_shape=jax.ShapeDtypeStruct((SEQ, D_MODEL), F32), compiler_params=_cp(), name="ffn_out_fwd",
    )(x1, gu, gu, wout, tie)


def _ffn_out_bwd(dx2, gu, wout, tie):
    tt = 256

    def body(dx_ref, gt_ref, up_ref, w_ref, tie_ref, dgu_ref, act_ref):
        gt, up = gt_ref[...], up_ref[...]
        sg = jax.nn.sigmoid(gt)
        silu = gt * sg
        act_ref[...] = (silu * up).astype(BF16)
        dact = lax.dot_general(dx_ref[...].astype(BF16), w_ref[...], NT, preferred_element_type=F32)
        dgu_ref[:, :FFN_H] = (dact * up * (sg * (1.0 + gt * (1.0 - sg)))).astype(BF16)
        dgu_ref[:, FFN_H:] = (dact * silu).astype(BF16)

    return pl.pallas_call(
        body, grid=(SEQ // tt,),
        in_specs=[pl.BlockSpec((tt, D_MODEL), lambda i: (i, 0)), pl.BlockSpec((tt, FFN_H), lambda i: (i, 0)),
                  pl.BlockSpec((tt, FFN_H), lambda i: (i, 1)), _full((FFN_H, D_MODEL)), ANY],
        out_specs=[pl.BlockSpec((tt, 2 * FFN_H), lambda i: (i, 0)), pl.BlockSpec((tt, FFN_H), lambda i: (i, 0))],
        out_shape=[jax.ShapeDtypeStruct((SEQ, 2 * FFN_H), BF16), jax.ShapeDtypeStruct((SEQ, FFN_H), BF16)],
        compiler_params=_cp(), name="ffn_out_bwd",
    )(dx2, gu, gu, wout, tie)


def _loss_head(x, g, target):
    tt = 256

    def body(x_ref, g_ref, t_ref, loss_ref, dx_ref, dg_ref):
        @pl.when(pl.program_id(0) == 0)
        def _():
            loss_ref[...] = jnp.zeros_like(loss_ref)
            dg_ref[...] = jnp.zeros_like(dg_ref)

        xv = x_ref[...]
        r = lax.rsqrt(jnp.mean(xv * xv, axis=-1, keepdims=True) + NORM_EPS)
        xh = xv * r
        err = xh * g_ref[...] - t_ref[...]
        loss_ref[...] += 0.5 * jnp.sum(jnp.mean(err * err, axis=-1, keepdims=True))
        dy = err * (1.0 / D_MODEL)
        gy = dy * g_ref[...]
        dx_ref[...] = r * (gy - xh * jnp.mean(gy * xh, axis=-1, keepdims=True))
        dg_ref[...] += jnp.sum(dy * xh, axis=0, keepdims=True)

    row = pl.BlockSpec((tt, D_MODEL), lambda i: (i, 0))
    return pl.pallas_call(
        body, grid=(SEQ // tt,), in_specs=[row, _full((1, D_MODEL)), row],
        out_specs=[_full((1, 128)), row, _full((1, D_MODEL))],
        out_shape=[jax.ShapeDtypeStruct((1, 128), F32), jax.ShapeDtypeStruct((SEQ, D_MODEL), F32),
                   jax.ShapeDtypeStruct((1, D_MODEL), F32)],
        compiler_params=_cp(), name="loss_head")(x, g, target)


def _adamw(parts, w, m, v, tr, name, groups=None, fill=None, tie=None):
    n_groups, rows, cols = w.shape
    n_parts = parts.shape[1]
    lo, hi = groups if groups is not None else (0, n_groups)

    def body(p_ref, w_ref, m_ref, v_ref, *rest):
        g_ref, d_ref, nm_ref, nv_ref = rest[-4:]
        g = p_ref[0].astype(F32)
        for k in range(1, n_parts):
            g = g + p_ref[k].astype(F32)
        nm = B1 * m_ref[...] + (1.0 - B1) * g
        nv = B2 * v_ref[...] + (1.0 - B2) * (g * g)
        m_hat = nm / (1.0 - B1 ** STEP)
        v_hat = nv / (1.0 - B2 ** STEP)
        g_ref[...] = g
        d_ref[...] = -LR * (m_hat / (jnp.sqrt(v_hat) + ADAM_EPS) + WD * w_ref[...])
        nm_ref[...] = nm
        nv_ref[...] = nv

    blk = pl.BlockSpec((None, tr, cols), lambda l, i: (l + lo, i, 0))
    p_lo = lo if parts.shape[0] == n_groups else 0
    extra = ([] if fill is None else list(fill)) + ([] if tie is None else [tie])
    return pl.pallas_call(
        body, grid=(hi - lo, rows // tr),
        in_specs=[pl.BlockSpec((None, n_parts, tr, cols), lambda l, i: (l + p_lo, 0, i, 0)), blk, blk, blk]
        + [ANY] * len(extra),
        out_specs=[blk] * 4, out_shape=[jax.ShapeDtypeStruct((n_groups, rows, cols), F32)] * 4,
        input_output_aliases={} if fill is None else {4 + j: j for j in range(4)},
        compiler_params=_cp(), name=name)(parts, w, m, v, *extra)


def _split_start(name, arrays, n_sems, plan, after=None):
    n = len(arrays)
    order = [] if after is None else [after]
    n_in = n + len(order)

    def body(*refs):
        ins, send_sems, recv_sems, token = refs[:n], refs[n_in], refs[n_in + 1], refs[-1]
        for src, dst, k, to in plan(ins)[0]:
            pltpu.make_async_remote_copy(src_ref=src, dst_ref=dst, send_sem=send_sems.at[k], recv_sem=recv_sems.at[k],
                                         device_id=to, device_id_type=MESH_ID).start()
        token[...] = jnp.zeros_like(token)

    outs = pl.pallas_call(
        body, name=name,
        out_shape=(pltpu.SemaphoreType.DMA((n_sems,)), pltpu.SemaphoreType.DMA((n_sems,)),
                   *[pltpu.HBM(a.shape, a.dtype) for a in arrays], jax.ShapeDtypeStruct((8, 128), F32)),
        in_specs=[HBM] * n + [ANY] * len(order),
        out_specs=(SEM, SEM, *[HBM] * n, pl.BlockSpec(memory_space=pltpu.VMEM)),
        input_output_aliases={i: 2 + i for i in range(n)},
        compiler_params=pltpu.CompilerParams(has_side_effects=EFFECT),
    )(*[pltpu.with_memory_space_constraint(a, pltpu.HBM) for a in arrays], *order)
    return outs[0], outs[1], list(outs[2:2 + n]), outs[-1]


def _split_wait(name, arrays, send_sems, recv_sems, after, plan):
    n = len(arrays)

    def body(*refs):
        ins, s_sems, r_sems = refs[:n], refs[n], refs[n + 1]
        sends, arrivals = plan(ins)
        x, y, c = lax.axis_index("x"), lax.axis_index("y"), lax.axis_index("c")
        for src, dst, k, to in sends:
            pltpu.make_async_remote_copy(src_ref=src, dst_ref=dst, send_sem=s_sems.at[k], recv_sem=r_sems.at[k],
                                         device_id=to, device_id_type=MESH_ID).wait_send()
        for dst, k in arrivals:
            pltpu.make_async_remote_copy(src_ref=dst, dst_ref=dst, send_sem=s_sems.at[k], recv_sem=r_sems.at[k],
                                         device_id=(x, y, c), device_id_type=MESH_ID).wait_recv()

    return pl.pallas_call(
        body, name=name, out_shape=[pltpu.HBM(a.shape, a.dtype) for a in arrays],
        in_specs=[HBM] * n + [SEM, SEM, ANY], out_specs=[HBM] * n,
        input_output_aliases={i: i for i in range(n)},
        compiler_params=pltpu.CompilerParams(has_side_effects=EFFECT),
    )(*arrays, send_sems, recv_sems, after)


def _chips():
    x, y, c = lax.axis_index("x"), lax.axis_index("y"), lax.axis_index("c")
    return x, y, c, [(1 - x, y), (x, 1 - y), (1 - x, 1 - y)]


def _plan_gather_chips(refs):
    x, y, c, chips = _chips()
    me = 4 * x + 2 * y + c
    n = len(refs) // 2
    sends, arrivals = [], []
    for i in range(n):
        src, land = refs[i], refs[n + i]
        sends.append((src, land.at[me], 4 * i, (x, y, 1 - c)))
        arrivals.append((land.at[4 * x + 2 * y + 1 - c], 4 * i))
        for j, (px, py) in enumerate(chips):
            sends.append((src, land.at[me], 4 * i + 1 + j, (px, py, c)))
            arrivals.append((land.at[4 * px + 2 * py + c], 4 * i + 1 + j))
    return sends, arrivals


def _plan_gather_pass(refs):
    x, y, c, chips = _chips()
    sends, arrivals = [], []
    for i in range(len(refs)):
        for j, (px, py) in enumerate(chips):
            slot = refs[i].at[4 * px + 2 * py + c]
            sends.append((slot, slot, 4 * i + j, (x, y, 1 - c)))
            arrivals.append((refs[i].at[4 * px + 2 * py + 1 - c], 4 * i + j))
        back = refs[i].at[4 * x + 2 * y + 1 - c]
        sends.append((back, back, 4 * i + 3, (x, y, 1 - c)))
        arrivals.append((refs[i].at[4 * x + 2 * y + c], 4 * i + 3))
    return sends, arrivals


def _plan_scatter_pair(refs):
    x, y, c = lax.axis_index("x"), lax.axis_index("y"), lax.axis_index("c")
    n = len(refs) // 2
    sends, arrivals = [], []
    for i in range(n):
        for q in range(4):
            sends.append((refs[i].at[q, 1 - c], refs[n + i].at[q], 4 * i + q, (x, y, 1 - c)))
            arrivals.append((refs[n + i].at[q], 4 * i + q))
    return sends, arrivals


def _plan_scatter_chips(layer):
    def plan(refs):
        x, y, c, chips = _chips()
        n = len(refs) // 2
        sends, arrivals = [], []
        for i in range(n):
            for j, (px, py) in enumerate(chips):
                sends.append((refs[i].at[2 * px + py], refs[n + i].at[layer, 2 * x + y], 3 * i + j, (px, py, c)))
                arrivals.append((refs[n + i].at[layer, 2 * px + py], 3 * i + j))
        return sends, arrivals

    return plan


def _pair_sum(parts4, from_pair, landing, layer, core, tr, name):
    _, _, rows, cols = parts4.shape

    def body(c_ref, p_ref, s_ref, l_ref, sum_ref, land_ref):
        v = (p_ref[...].astype(F32) + s_ref[...].astype(F32)).astype(BF16)
        sum_ref[...] = v
        land_ref[...] = v

    blk = pl.BlockSpec((None, tr, cols), lambda q, i, c_ref: (q, i, 0))
    return pl.pallas_call(
        body,
        grid_spec=pltpu.PrefetchScalarGridSpec(
            num_scalar_prefetch=1, grid=(4, rows // tr),
            in_specs=[pl.BlockSpec((None, None, tr, cols), lambda q, i, c_ref: (q, c_ref[0], i, 0)), blk, ANY],
            out_specs=[blk, pl.BlockSpec((None, None, tr, cols), lambda q, i, c_ref: (layer, q, i, 0))]),
        out_shape=[jax.ShapeDtypeStruct((4, rows, cols), BF16), jax.ShapeDtypeStruct(landing.shape, BF16)],
        input_output_aliases={3: 1}, compiler_params=_cp(), name=name,
    )(core, parts4, from_pair, landing)


def _travel_layout(t):
    tr = lambda a: jnp.swapaxes(a, 1, 2)
    branch = jnp.concatenate([tr(t["w_attn_o"]), tr(t["w_conv_o"]), tr(t["w_ssm_o"])], axis=2)
    return [tr(t["w_in"]), tr(t["w_ffn_in"]), t["w_ffn_out"], t["w_mix_o"], branch, t["w_ssm_glu"]]


def _native_layout(a):
    tr = lambda x: jnp.swapaxes(x, 1, 2)
    b = a[4]
    return {"w_in": tr(a[0]), "w_ffn_in": tr(a[1]), "w_ffn_out": a[2], "w_mix_o": a[3],
            "w_attn_o": tr(b[:, :, :WIDTH]), "w_conv_o": tr(b[:, :, WIDTH:2 * WIDTH]),
            "w_ssm_o": tr(b[:, :, 2 * WIDTH:]), "w_ssm_glu": a[5]}


def _embed(t):
    eye = jnp.eye(8, dtype=t.dtype)
    t = t.reshape(DEPTH, N_LANE_GROUPS, 8, SSM_GROUP, SSM_STATE)
    return (t[:, :, :, :, None, :] * eye[None, None, :, None, :, None]).reshape(DEPTH, N_LANE_GROUPS, 128, LANES_G)


def _diag_blocks(t):
    t = t.reshape(DEPTH, N_LANE_GROUPS, 8, SSM_GROUP, 8, SSM_STATE)
    return jnp.einsum("lgahap->lgahp", t).reshape(DEPTH, SSM_GROUPS, SSM_GROUP, SSM_STATE)


def _rope_tabs():
    pos = jnp.arange(SEQ, dtype=F32)
    inv_freq = ROPE_THETA ** (-jnp.arange(0, ROT_DIM, 2, dtype=F32) / ROT_DIM)
    ang = pos[:, None] * inv_freq[None, :]
    cos, sin = jnp.cos(ang), jnp.sin(ang)
    one, zero = jnp.ones((SEQ, HEAD_DIM - ROT_DIM), F32), jnp.zeros((SEQ, HEAD_DIM - ROT_DIM), F32)
    z8 = jnp.zeros((SEQ, 8), F32)
    head = lambda *p: jnp.tile(jnp.concatenate(p, axis=1), (1, 2))
    return head(cos, cos, one), head(-sin, z8, zero), head(z8, sin, zero)


def _ssm_mats(sp):
    lr, li, bbr, bbi = _ssm_prep(sp["a_re"], sp["a_im"], sp["log_dt"], sp["bt_re"], sp["bt_im"])
    lanes = SSM_GROUPS * SSM_STATE
    return {
        "a_re": lr.reshape(DEPTH, 1, lanes), "a_im": li.reshape(DEPTH, 1, lanes),
        "b_re": _embed(bbr).astype(BF16), "b_im": _embed(bbi).astype(BF16),
        "c_re": _embed(sp["c_re"]).astype(BF16), "c_im_neg": _embed(-sp["c_im"]).astype(BF16),
    }


def _layer_fwd(x, i, w, rp, mats, tabs, tie, hooks):
    q, kv, cbx, u, glog, h = _rms_mm_in(x, rp["norm_mix"][i], w["win_t"], tie)
    o = _attn_fwd(q, kv, tabs, rp["attn_sinks"][i])
    cv = _conv_fwd(cbx, rp["conv_w"], i)
    u = _scan_order(u)
    x_re, x_im, y = _ssm_fwd(u, mats, i, rp["ssm_d"])
    z = _time_order(_glu_fwd(y, w["wglu"]))
    x1 = _mix_fwd(x, o, cv, z, glog, rp["b_gate"], i, w["branch_t"], w["wmix"], hooks["early"](z))
    hooks["pre_ffn"](x1)
    gu, h2 = _rms_mm_ffn(x1, rp["norm_ffn"][i], w["wffn_t"])
    x2 = _ffn_out_fwd(x1, gu, w["wout"], hooks["mid"](h2))
    kept = dict(x=x, q=q, kv=kv, cbx=cbx, u=u, glog=glog, h=h, o=o, cv=cv, z=z, y=y,
                x_re=x_re, x_im=x_im, x1=x1, gu=gu, h2=h2)
    return x2, kept


def _layer_bwd(dx2, k, i, w, rp, mats, tabs, tie, hooks):
    dgu, act = _ffn_out_bwd(dx2, k["gu"], w["wout"], tie)
    g_wout = _mm_tn(act, dx2, tm=FFN_H // 2, tn=1024, name="mm_tn_ffn_out")
    g_wffn_t = _mm_tn(dgu, k["h2"], tm=FFN_H // 2, tn=1024, name="mm_tn_ffn_in")
    dx1, d_norm_ffn = _mm_rmsbwd([dgu], w["wffn_t"], k["x1"], rp["norm_ffn"][i], dx2, "mm_rmsbwd_ffn")

    mg, dya, dyc, dys, do, dcv, dz, dgl, db_gate = _mix_bwd(
        dx1, k["o"], k["cv"], k["z"], k["glog"], rp["b_gate"], i, w["branch_t"], w["wmix"],
        hooks["mid"]((g_wffn_t, g_wout, d_norm_ffn)))
    g_wmix = _mm_tn(mg, dx1, tm=1024, tn=512, name="mm_tn_mix")
    g_branch_t = _tn_branches((dya, dyc, dys), (k["o"], k["cv"], k["z"]))

    dy16, ys16, da16, dd = _glu_bwd(k["y"], w["wglu"], _scan_order(dz), k["u"])
    g_wglu = _mm_tn(ys16, da16, tm=256, tn=512, name="mm_tn_glu")
    du, da_re, da_im, db_re, db_im, dc_re, dc_im = _ssm_bwd(dy16, k["x_re"], k["x_im"], k["u"], mats, i,
                                                             rp["ssm_d"])
    du = _time_order(du)

    dcb, dcc, dcx, d_conv_w = _conv_bwd(k["cbx"], rp["conv_w"], i, dcv, hooks["late"](du))
    dq, dkv, d_sinks = _attn_bwd(k["q"], k["kv"], tabs, rp["attn_sinks"][i], do)

    pieces = [dq, dkv, dcb, dcc, dcx, du, dgl]
    g_win_t = _tn_pieces(pieces, k["h"])
    dx, d_norm_mix = _mm_rmsbwd(pieces, w["win_t"], k["x"], rp["norm_mix"][i], dx1, "mm_rmsbwd_in")

    grads = [g_win_t, g_wffn_t, g_wout, g_wmix, g_branch_t, g_wglu]
    small = dict(norm_mix=d_norm_mix, b_gate=db_gate, attn_sinks=d_sinks, ssm_d=dd, norm_ffn=d_norm_ffn,
                 conv_w=d_conv_w, da_re=da_re, da_im=da_im, db_re=db_re, db_im=db_im, dc_re=dc_re, dc_im=dc_im)
    return dx, grads, small


def _replicated_grads(sg, sp):
    stack = lambda name: jnp.stack([sg[i][name] for i in range(DEPTH)])
    cots = (stack("da_re").reshape(DEPTH, *_GS), stack("da_im").reshape(DEPTH, *_GS),
            _diag_blocks(stack("db_re")), _diag_blocks(stack("db_im")))
    d_a_re, d_a_im, d_log_dt, d_bt_re, d_bt_im = _ssm_prep_bwd(
        sp["a_re"], sp["a_im"], sp["log_dt"], sp["bt_re"], sp["bt_im"], cots)
    sgrads = {"norm_mix": stack("norm_mix"), "b_gate": stack("b_gate"),
              "attn_sinks": stack("attn_sinks")[:, :, :N_Q_HEADS], "ssm_a_re": d_a_re, "ssm_a_im": d_a_im,
              "ssm_b_re": jnp.swapaxes(d_bt_re, 2, 3), "ssm_b_im": jnp.swapaxes(d_bt_im, 2, 3),
              "ssm_c_re": _diag_blocks(stack("dc_re")), "ssm_c_im": -_diag_blocks(stack("dc_im")),
              "ssm_d": stack("ssm_d"), "ssm_log_dt": d_log_dt, "norm_ffn": stack("norm_ffn")}
    return sgrads, stack("conv_w")[:, :3]


def kernel(x, norm_mix, w_in, b_gate, attn_sinks, w_attn_o, conv_w, w_conv_o, ssm_a_re, ssm_a_im, ssm_b_re, ssm_b_im, ssm_c_re, ssm_c_im, ssm_d, ssm_log_dt, w_ssm_glu, w_ssm_o, w_mix_o, norm_ffn, w_ffn_in, w_ffn_out, norm_final, loss_target, m_norm_mix, m_w_in, m_b_gate, m_attn_sinks, m_w_attn_o, m_conv_w, m_w_conv_o, m_ssm_a_re, m_ssm_a_im, m_ssm_b_re, m_ssm_b_im, m_ssm_c_re, m_ssm_c_im, m_ssm_d, m_ssm_log_dt, m_w_ssm_glu, m_w_ssm_o, m_w_mix_o, m_norm_ffn, m_w_ffn_in, m_w_ffn_out, m_norm_final, v_norm_mix, v_w_in, v_b_gate, v_attn_sinks, v_w_attn_o, v_conv_w, v_w_conv_o, v_ssm_a_re, v_ssm_a_im, v_ssm_b_re, v_ssm_b_im, v_ssm_c_re, v_ssm_c_im, v_ssm_d, v_ssm_log_dt, v_w_ssm_glu, v_w_ssm_o, v_w_mix_o, v_norm_ffn, v_w_ffn_in, v_w_ffn_out, v_norm_final):
    big = {"w": dict(w_in=w_in, w_attn_o=w_attn_o, w_conv_o=w_conv_o, w_ssm_glu=w_ssm_glu, w_ssm_o=w_ssm_o,
                     w_mix_o=w_mix_o, w_ffn_in=w_ffn_in, w_ffn_out=w_ffn_out),
           "m": dict(w_in=m_w_in, w_attn_o=m_w_attn_o, w_conv_o=m_w_conv_o, w_ssm_glu=m_w_ssm_glu,
                     w_ssm_o=m_w_ssm_o, w_mix_o=m_w_mix_o, w_ffn_in=m_w_ffn_in, w_ffn_out=m_w_ffn_out),
           "v": dict(w_in=v_w_in, w_attn_o=v_w_attn_o, w_conv_o=v_w_conv_o, w_ssm_glu=v_w_ssm_glu,
                     w_ssm_o=v_w_ssm_o, w_mix_o=v_w_mix_o, w_ffn_in=v_w_ffn_in, w_ffn_out=v_w_ffn_out)}
    small = {"w": dict(norm_mix=norm_mix, b_gate=b_gate, attn_sinks=attn_sinks, ssm_a_re=ssm_a_re,
                       ssm_a_im=ssm_a_im, ssm_b_re=ssm_b_re, ssm_b_im=ssm_b_im, ssm_c_re=ssm_c_re,
                       ssm_c_im=ssm_c_im, ssm_d=ssm_d, ssm_log_dt=ssm_log_dt, norm_ffn=norm_ffn),
             "m": dict(norm_mix=m_norm_mix, b_gate=m_b_gate, attn_sinks=m_attn_sinks, ssm_a_re=m_ssm_a_re,
                       ssm_a_im=m_ssm_a_im, ssm_b_re=m_ssm_b_re, ssm_b_im=m_ssm_b_im, ssm_c_re=m_ssm_c_re,
                       ssm_c_im=m_ssm_c_im, ssm_d=m_ssm_d, ssm_log_dt=m_ssm_log_dt, norm_ffn=m_norm_ffn),
             "v": dict(norm_mix=v_norm_mix, b_gate=v_b_gate, attn_sinks=v_attn_sinks, ssm_a_re=v_ssm_a_re,
                       ssm_a_im=v_ssm_a_im, ssm_b_re=v_ssm_b_re, ssm_b_im=v_ssm_b_im, ssm_c_re=v_ssm_c_re,
                       ssm_c_im=v_ssm_c_im, ssm_d=v_ssm_d, ssm_log_dt=v_ssm_log_dt, norm_ffn=v_norm_ffn)}
    finals = {"w": norm_final, "m": m_norm_final, "v": v_norm_final}
    convs = {"w": conv_w, "m": m_conv_w, "v": v_conv_w}
    mine = 4 * lax.axis_index("x") + 2 * lax.axis_index("y") + lax.axis_index("c")

    travel = {s: _travel_layout(big[s]) for s in "wmv"}
    stacked16 = [a.astype(BF16) for a in travel["w"]]
    rp = {"norm_mix": norm_mix[:, None], "norm_ffn": norm_ffn[:, None], "attn_sinks": attn_sinks[:, None],
          "b_gate": b_gate[:, None], "ssm_d": ssm_d[:, None]}
    sp = {"a_re": ssm_a_re, "a_im": ssm_a_im, "log_dt": ssm_log_dt[:, :, None],
          "bt_re": jnp.swapaxes(ssm_b_re, 2, 3), "bt_im": jnp.swapaxes(ssm_b_im, 2, 3),
          "c_re": ssm_c_re, "c_im": ssm_c_im}
    rows_tile = {"win_t": 368, "wffn_t": 352, "wout": 176, "wmix": 128, "branch_t": 128, "wglu": 64}
    core = lax.axis_index("c").astype(jnp.int32).reshape(1)
    no_tie = jnp.zeros((8, 128), F32)

    def place_own(srcs):
        return [lax.empty((N_DEV,) + s.shape, s.dtype) for s in srcs]

    def gather_chips(tag, i, kinds, after, extra=()):
        srcs = [stacked16[j][i] for j in kinds] + list(extra)
        s_sems, r_sems, arrays, token = _split_start(
            f"gather_chips_start_{tag}", srcs + place_own(srcs), 4 * len(srcs), _plan_gather_chips, after)
        return (tag, s_sems, r_sems, arrays), token

    def gather_pass(state, after):
        tag, s_sems, r_sems, arrays = state
        arrays = _split_wait(f"gather_chips_wait_{tag}", arrays, s_sems, r_sems, after, _plan_gather_chips)
        n = len(arrays) // 2
        s_sems, r_sems, lands, token = _split_start(
            f"gather_pass_start_{tag}", list(arrays[n:]), 4 * n, _plan_gather_pass)
        return (tag, s_sems, r_sems, lands), token

    def gather_done(state, after, kinds):
        tag, s_sems, r_sems, lands = state
        lands = _split_wait(f"gather_pass_wait_{tag}", lands, s_sems, r_sems, after, _plan_gather_pass)
        named = {KINDS[j][0]: a.reshape(N_DEV * KINDS[j][1], KINDS[j][2]) for a, j in zip(lands, kinds)}
        return named, list(lands[len(kinds):])

    all_kinds, mixer_kinds, ffn_kinds = tuple(range(len(KINDS))), (0, 3, 4, 5), (1, 2)
    no_hooks = {name: (lambda value: no_tie) for name in ("early", "pre_ffn", "mid", "late")}
    state, _ = gather_chips("0m", 0, mixer_kinds, None, extra=[jnp.pad(conv_w.reshape(6, 128), ((0, 2), (0, 0)))])
    mats = _ssm_mats(sp)
    tabs = _rope_tabs()
    state, _ = gather_pass(state, mats["c_im_neg"])
    ffn_state, tie = gather_chips("0f", 0, ffn_kinds, state[3][0])
    w_next, (conv_all,) = gather_done(state, tabs[2], mixer_kinds)
    conv_full = conv_all[:, :6].reshape(N_DEV, DEPTH, 3, 64).transpose(1, 2, 0, 3).reshape(DEPTH, 3, WIDTH)
    rp["conv_w"] = jnp.pad(conv_full, ((0, 0), (0, 5), (0, 0)))

    act = x[0]
    weights, kept = [], []
    for i in range(DEPTH):
        w_i, hooks, held = w_next, dict(no_hooks), {}

        def early(value, ffn_state=ffn_state, held=held):
            held["ffn"], token = gather_pass(ffn_state, value)
            return token

        def pre_ffn(value, w_i=w_i, held=held):
            w_i.update(gather_done(held["ffn"], value, ffn_kinds)[0])

        hooks.update(early=early, pre_ffn=pre_ffn)
        if i + 1 < DEPTH:
            state, tie = gather_chips(f"{i + 1}m", i + 1, mixer_kinds, tie if i == 0 else w_i["win_t"])

            def mid(value, i=i, state=state, held=held):
                held["next"], token = gather_pass(state, value)
                held["next_ffn"], token = gather_chips(f"{i + 1}f", i + 1, ffn_kinds, token)
                return token

            hooks.update(mid=mid)
        act, k = _layer_fwd(act, i, w_i, rp, mats, tabs, tie, hooks)
        if i + 1 < DEPTH:
            w_next, _ = gather_done(held["next"], act, mixer_kinds)
            ffn_state, tie = held["next_ffn"], no_tie
        weights.append(w_i)
        kept.append(k)
    loss_row, dx, d_norm_final = _loss_head(act, norm_final[None], loss_target[0])
    loss = lax.psum(loss_row[0, 0], ("x", "y", "c"))

    landings = [lax.empty((DEPTH, 4, r, c), BF16) for _, r, c in KINDS]
    landings0 = [lax.empty((1, 4, r, c), BF16) for _, r, c in KINDS]

    def scatter_pair(tag, kinds, grads, after):
        parts4 = [g.reshape(4, 2, KINDS[j][1], KINDS[j][2]) for g, j in zip(grads, kinds)]
        zones = [lax.empty((4, KINDS[j][1], KINDS[j][2]), BF16) for j in kinds]
        s_sems, r_sems, arrays, token = _split_start(
            f"scatter_pair_start_{tag}", parts4 + zones, 4 * len(kinds), _plan_scatter_pair, after)
        return (tag, kinds, s_sems, r_sems, arrays), token

    def scatter_chips(state, lands, slot, after):
        tag, kinds, s_sems, r_sems, arrays = state
        arrays = _split_wait(f"scatter_pair_wait_{tag}", arrays, s_sems, r_sems, after, _plan_scatter_pair)
        n = len(kinds)
        sums, mine_lands = [], []
        for k, j in enumerate(kinds):
            name = KINDS[j][0]
            chip_sum, land = _pair_sum(arrays[k], arrays[n + k], lands[j], slot, core, rows_tile[name],
                                       f"pair_sum_{name}")
            sums.append(chip_sum)
            mine_lands.append(land)
        s_sems, r_sems, arrays, token = _split_start(
            f"scatter_chips_start_{tag}", sums + mine_lands, 3 * n, _plan_scatter_chips(slot))
        return (tag, kinds, slot, s_sems, r_sems, arrays), token

    def scatter_done(state, lands, after):
        tag, kinds, slot, s_sems, r_sems, arrays = state
        arrays = _split_wait(f"scatter_chips_wait_{tag}", arrays, s_sems, r_sems, after, _plan_scatter_chips(slot))
        lands = list(lands)
        for k, j in enumerate(kinds):
            lands[j] = arrays[len(kinds) + k]
        return lands

    sg = [None] * DEPTH
    pending, tie = None, no_tie
    for i in reversed(range(DEPTH)):
        hooks, held = dict(no_hooks), {}
        if pending is not None:
            def mid(value, i=i, pending=pending, held=held):
                held["chips"], token = scatter_chips(pending, landings, i + 1, value[2])
                if i == 0:
                    held["ffn_pair"], token = scatter_pair("0f", ffn_kinds, value[:2], token)
                return token

            hooks.update(mid=mid)
        if i == 0:
            def late(value, held=held):
                held["ffn_chips"], token = scatter_chips(held["ffn_pair"], landings0, 0, value)
                return token

            hooks.update(late=late)
        dx, grads, sg[i] = _layer_bwd(dx, kept[i], i, weights[i], rp, mats, tabs, tie, hooks)
        if pending is not None:
            landings = scatter_done(held["chips"], landings, dx)
        if i > 0:
            pending, tie = scatter_pair(str(i), all_kinds, grads, dx)
        else:
            pending, _ = scatter_pair("0m", mixer_kinds, [grads[j] for j in mixer_kinds], dx)

    sgrads, conv_grad = _replicated_grads(sg, sp)

    def pack_small(t, final, conv):
        flat = [t[name].reshape(DEPTH, n) for name, n in SMALL]
        flat = jnp.concatenate([jnp.concatenate(flat, axis=1).reshape(-1), final.reshape(-1), conv.reshape(-1)])
        return jnp.pad(flat, (0, SMALL_ROWS * 128 - flat.shape[0])).reshape(SMALL_ROWS, 128)

    small_src = [pack_small(sgrads, d_norm_final, conv_grad).astype(BF16)]
    last, tie = scatter_chips(pending, landings0, 0, small_src[0])
    s_sems, r_sems, arrays, tie = _split_start(
        "gather_small_chips_start", small_src + place_own(small_src), 4, _plan_gather_chips, tie)
    small_state = ("small", s_sems, r_sems, arrays)

    big_out = [_adamw(landings[j], travel["w"][j], travel["m"][j], travel["v"][j], rows_tile[name],
                      "adamw_late_" + name, groups=(1, DEPTH), tie=tie) for j, (name, _, _) in enumerate(KINDS)]
    landings0 = scatter_done(held["ffn_chips"], landings0, big_out[-1][0])
    landings0 = scatter_done(last, landings0, big_out[-1][0])
    small_state, _ = gather_pass(small_state, landings0[0])
    big_out = [_adamw(landings0[j], travel["w"][j], travel["m"][j], travel["v"][j], rows_tile[name],
                      "adamw_first_" + name, groups=(0, 1), fill=big_out[j]) for j, (name, _, _) in enumerate(KINDS)]
    big_res = [_native_layout([big_out[j][kind] for j in range(len(KINDS))]) for kind in range(4)]

    zeros_conv = jnp.zeros((CONV_N,), F32)
    _, (sparts,) = gather_done(small_state, big_out[-1][0], ())
    sw, sm_, sv = (pack_small(small[s], finals[s], zeros_conv) for s in "wmv")
    small_out = _adamw(sparts[None], sw[None], sm_[None], sv[None], SMALL_ROWS // 8, "adamw_replicated")

    def unpack_small(p):
        flat = p.reshape(-1)
        per = flat[:DEPTH * SMALL_PER_LAYER].reshape(DEPTH, SMALL_PER_LAYER)
        out, off = {}, 0
        for name, n in SMALL:
            out[name] = per[:, off:off + n].reshape(small["w"][name].shape)
            off += n
        out["norm_final"] = flat[DEPTH * SMALL_PER_LAYER:DEPTH * SMALL_PER_LAYER + D_MODEL]
        return out

    small_res = [unpack_small(p) for p in small_out]

    conv_off = DEPTH * SMALL_PER_LAYER + D_MODEL
    conv_parts = sparts.reshape(N_DEV, -1)[:, conv_off:conv_off + CONV_N].reshape(N_DEV, DEPTH * 3, WIDTH)
    conv_parts = lax.dynamic_slice_in_dim(conv_parts, mine * 64, 64, axis=2)
    conv_res = _adamw(conv_parts[None], *(convs[s].reshape(1, DEPTH * 3, 64) for s in "wmv"), DEPTH * 3, "adamw_conv_w")

    order = ["norm_mix", "w_in", "b_gate", "attn_sinks", "w_attn_o", "conv_w", "w_conv_o", "ssm_a_re", "ssm_a_im",
             "ssm_b_re", "ssm_b_im", "ssm_c_re", "ssm_c_im", "ssm_d", "ssm_log_dt", "w_ssm_glu", "w_ssm_o",
             "w_mix_o", "norm_ffn", "w_ffn_in", "w_ffn_out", "norm_final"]
    outs = [loss, dx[None]]
    for kind in range(4):
        for name in order:
            if name == "conv_w":
                outs.append(conv_res[kind].reshape(DEPTH, 3, 64))
            elif name in big_res[kind]:
                outs.append(big_res[kind][name])
            else:
                outs.append(small_res[kind][name])
    return tuple(outs)
```

```python
import functools
import math

import jax
import jax.numpy as jnp
from jax import lax
from jax.experimental import pallas as pl
from jax.experimental.pallas import tpu as pltpu

F32 = jnp.float32
BF16 = jnp.bfloat16

N_DEV = 8
DEPTH = 4
SEQ = 2048
D_MODEL = 1024
N_Q_HEADS = 8
HEAD_DIM = 64
ATTN_W = 512
KV_W = 128
BLOCK = 128
N_BLOCKS = SEQ // BLOCK
ROPE_THETA = 500000.0
ROT_DIM = 16
NEG_INF = -1e30
WIDTH = 512
SSM_GROUPS = 32
SSM_GROUP = 16
SSM_STATE = 64
SLABS = 16
CHUNK = 256
N_CHUNKS = SEQ // CHUNK
GATE_W = 3 * D_MODEL
IN_COLS = 5888
FFN_H = 2816
NORM_EPS = 1e-6
LR, B1, B2, ADAM_EPS, WD, STEP = 0.001, 0.9, 0.999, 1e-08, 0.01, 10

COL_Q, COL_KV, COL_CBX, COL_U, COL_G = 0, 512, 768, 2304, 2816
PIECE_W = (512, 256, 512, 512, 512, 512, 3072)
PIECE_OFF = tuple(sum(PIECE_W[:i]) for i in range(len(PIECE_W)))

KINDS = (("win_t", 736, 1024), ("wffn_t", 704, 1024), ("wout", 352, 1024), ("wmix", 128, 1024),
         ("branch_t", 128, 1536), ("wglu", 64, 512))

SMALL = (("norm_mix", 1024), ("b_gate", 3072), ("attn_sinks", 8), ("ssm_a_re", 2048), ("ssm_a_im", 2048),
         ("ssm_b_re", 32768), ("ssm_b_im", 32768), ("ssm_c_re", 32768), ("ssm_c_im", 32768),
         ("ssm_d", 512), ("ssm_log_dt", 32), ("norm_ffn", 1024))
SMALL_PER_LAYER = sum(n for _, n in SMALL)
CONV_N = DEPTH * 3 * WIDTH
SMALL_ROWS = 4480

VMEM_LIMIT = 56 * 1024 * 1024
NT = (((1,), (1,)), ((), ()))
TN = (((0,), (0,)), ((), ()))
MESH_ID = pl.DeviceIdType.MESH
ANY = pl.BlockSpec(memory_space=pl.ANY)
HBM = pl.BlockSpec(memory_space=pltpu.HBM)
SEM = pl.BlockSpec(memory_space=pltpu.SEMAPHORE)
EFFECT = pltpu.SideEffectType.DATAFLOW_SIDE_EFFECTING


def _cp(**kw):
    return pltpu.CompilerParams(vmem_limit_bytes=VMEM_LIMIT, **kw)


def _full(shape):
    return pl.BlockSpec(shape, lambda *_: (0,) * len(shape))


def _resident(shape):
    return pl.BlockSpec(shape, lambda *_: (0,) * len(shape), pipeline_mode=pl.Buffered(1))


def _mm_tn(a, b, *, tm, tn, name):
    k, m = a.shape
    n = b.shape[1]

    def body(a_ref, b_ref, o_ref):
        o_ref[...] = lax.dot_general(a_ref[...].astype(BF16), b_ref[...].astype(BF16), TN,
                                     preferred_element_type=F32).astype(BF16)

    return pl.pallas_call(
        body, grid=(m // tm, n // tn),
        in_specs=[pl.BlockSpec((k, tm), lambda i, j: (0, i)), pl.BlockSpec((k, tn), lambda i, j: (0, j))],
        out_specs=pl.BlockSpec((tm, tn), lambda i, j: (i, j)),
        out_shape=jax.ShapeDtypeStruct((m, n), BF16), compiler_params=_cp(), name=name)(a, b)


def _rms_rows(xv, g):
    r = lax.rsqrt(jnp.mean(xv * xv, axis=-1, keepdims=True) + NORM_EPS)
    return ((xv * r) * g).astype(BF16)


def _rms_mm_in(x, g, wt, tie):
    tt = 512
    widths = (ATTN_W, 2 * KV_W, 3 * WIDTH, WIDTH, GATE_W)
    offs = (COL_Q, COL_KV, COL_CBX, COL_U, COL_G)

    def body(x_ref, g_ref, w_ref, tie_ref, q_ref, kv_ref, cbx_ref, u_ref, gl_ref, h_ref):
        h = _rms_rows(x_ref[...], g_ref[...])
        h_ref[...] = h
        prod = lax.dot_general(h, w_ref[...], NT, preferred_element_type=F32)
        for ref, o, w in zip((q_ref, kv_ref, cbx_ref, u_ref, gl_ref), offs, widths):
            ref[...] = prod[:, o:o + w]

    row = lambda w: pl.BlockSpec((tt, w), lambda i: (i, 0))
    sds = jax.ShapeDtypeStruct
    return pl.pallas_call(
        body, grid=(SEQ // tt,), in_specs=[row(D_MODEL), _full((1, D_MODEL)), _resident((IN_COLS, D_MODEL)), ANY],
        out_specs=[row(ATTN_W), row(2 * KV_W), row(3 * WIDTH), row(WIDTH), row(GATE_W), row(D_MODEL)],
        out_shape=[sds((SEQ, ATTN_W), F32), sds((SEQ, 2 * KV_W), F32), sds((SEQ, 3 * WIDTH), F32),
                   sds((SEQ, WIDTH), F32), sds((SEQ, GATE_W), F32), sds((SEQ, D_MODEL), BF16)],
        compiler_params=_cp(), name="rms_mm_in")(x, g, wt, tie)


def _rms_mm_ffn(x, g, wt):
    tt = 512

    def body(x_ref, g_ref, w_ref, o_ref, h_ref):
        h = _rms_rows(x_ref[...], g_ref[...])
        h_ref[...] = h
        o_ref[...] = lax.dot_general(h, w_ref[...], NT, preferred_element_type=F32)

    row = lambda w: pl.BlockSpec((tt, w), lambda i: (i, 0))
    return pl.pallas_call(
        body, grid=(SEQ // tt,), in_specs=[row(D_MODEL), _full((1, D_MODEL)), _resident((2 * FFN_H, D_MODEL))],
        out_specs=[row(2 * FFN_H), row(D_MODEL)],
        out_shape=[jax.ShapeDtypeStruct((SEQ, 2 * FFN_H), F32), jax.ShapeDtypeStruct((SEQ, D_MODEL), BF16)],
        compiler_params=_cp(), name="rms_mm_ffn")(x, g, wt)


def _mm_rmsbwd(pieces, wt, x, g, dres, name):
    tt = 512
    widths = [p.shape[1] for p in pieces]
    offs = [sum(widths[:i]) for i in range(len(widths))]
    n = len(pieces)

    def body(*refs):
        p_refs, (w_ref, x_ref, g_ref, r_ref, dx_ref, dg_ref) = refs[:n], refs[n:]

        @pl.when(pl.program_id(0) == 0)
        def _():
            dg_ref[...] = jnp.zeros_like(dg_ref)

        dh = jnp.zeros((tt, D_MODEL), F32)
        for p_ref, o, w in zip(p_refs, offs, widths):
            dh += jnp.dot(p_ref[...], w_ref[o:o + w, :], preferred_element_type=F32)
        xv = x_ref[...]
        r = lax.rsqrt(jnp.mean(xv * xv, axis=-1, keepdims=True) + NORM_EPS)
        xh = xv * r
        gy = dh * g_ref[...]
        dx_ref[...] = r_ref[...] + r * (gy - xh * jnp.mean(gy * xh, axis=-1, keepdims=True))
        dg_ref[...] += jnp.sum(dh * xh, axis=0, keepdims=True)

    row = lambda w: pl.BlockSpec((tt, w), lambda i: (i, 0))
    return pl.pallas_call(
        body, grid=(SEQ // tt,),
        in_specs=[row(w) for w in widths] + [_resident(wt.shape), row(D_MODEL), _full((1, D_MODEL)), row(D_MODEL)],
        out_specs=[row(D_MODEL), _full((1, D_MODEL))],
        out_shape=[jax.ShapeDtypeStruct((SEQ, D_MODEL), F32), jax.ShapeDtypeStruct((1, D_MODEL), F32)],
        compiler_params=_cp(), name=name)(*pieces, wt, x, g, dres)


def _tn_pieces(pieces, h):
    tk, tn = 512, 512
    nk = SEQ // tk
    n = len(pieces)

    def body(*refs):
        p_refs, (h_ref, o_ref, acc_ref) = refs[:n], refs[n:]
        kk = pl.program_id(1)

        @pl.when(kk == 0)
        def _():
            acc_ref[...] = jnp.zeros_like(acc_ref)

        hv = h_ref[...]
        for p_ref, o, w in zip(p_refs, PIECE_OFF, PIECE_W):
            acc_ref[o:o + w, :] += lax.dot_general(p_ref[...], hv, TN, preferred_element_type=F32)

        @pl.when(kk == nk - 1)
        def _():
            o_ref[...] = acc_ref[...].astype(BF16)

    return pl.pallas_call(
        body, grid=(D_MODEL // tn, nk),
        in_specs=[pl.BlockSpec((tk, w), lambda j, kk: (kk, 0)) for w in PIECE_W]
        + [pl.BlockSpec((tk, tn), lambda j, kk: (kk, j))],
        out_specs=pl.BlockSpec((IN_COLS, tn), lambda j, kk: (0, j)),
        out_shape=jax.ShapeDtypeStruct((IN_COLS, D_MODEL), BF16),
        scratch_shapes=[pltpu.VMEM((IN_COLS, tn), F32)], compiler_params=_cp(), name="tn_pieces")(*pieces, h)


def _tn_branches(dys, acts):
    tk = 512
    nk = SEQ // tk

    def body(d0, d1, d2, a0, a1, a2, o_ref, acc_ref):
        kk = pl.program_id(0)

        @pl.when(kk == 0)
        def _():
            acc_ref[...] = jnp.zeros_like(acc_ref)

        for j, (d, a) in enumerate(((d0, a0), (d1, a1), (d2, a2))):
            acc_ref[:, WIDTH * j:WIDTH * (j + 1)] += lax.dot_general(d[...], a[...], TN, preferred_element_type=F32)

        @pl.when(kk == nk - 1)
        def _():
            o_ref[...] = acc_ref[...].astype(BF16)

    row = lambda w: pl.BlockSpec((tk, w), lambda kk: (kk, 0))
    return pl.pallas_call(
        body, grid=(nk,), in_specs=[row(D_MODEL)] * 3 + [row(WIDTH)] * 3,
        out_specs=_full((D_MODEL, 3 * WIDTH)), out_shape=jax.ShapeDtypeStruct((D_MODEL, 3 * WIDTH), BF16),
        scratch_shapes=[pltpu.VMEM((D_MODEL, 3 * WIDTH), F32)], compiler_params=_cp(), name="tn_branches",
    )(*dys, *acts)


def _rope(t, c, a, b):
    return t * c + pltpu.roll(t, 120, axis=1) * a + pltpu.roll(t, 8, axis=1) * b


def _rope_t(d, c, a, b):
    return d * c + pltpu.roll(d * a, 8, axis=1) + pltpu.roll(d * b, 120, axis=1)


def _band_sides(band):
    left = lax.broadcasted_iota(jnp.int32, band.shape, 1) < HEAD_DIM
    h0 = jnp.where(left, band, 0.0)
    h1 = jnp.where(left, 0.0, band)
    r0 = pltpu.roll(h0, HEAD_DIM, axis=1)
    r1 = pltpu.roll(h1, HEAD_DIM, axis=1)
    return ((h0.astype(BF16), r0.astype(BF16)), (r1.astype(BF16), h1.astype(BF16)))


def _attn_mask(i):
    qi = lax.broadcasted_iota(jnp.int32, (2 * BLOCK, 2 * BLOCK), 0) % BLOCK
    kj = lax.broadcasted_iota(jnp.int32, (2 * BLOCK, 2 * BLOCK), 1)
    delta = qi + BLOCK - kj
    return (delta >= 0) & (delta < BLOCK) & ((kj >= BLOCK) | (i > 0))


def _attn_probs(s, ok, sink):
    s = jnp.where(ok, s * (HEAD_DIM ** -0.5), NEG_INF)
    m = jnp.maximum(jnp.max(s, axis=-1, keepdims=True), sink)
    p = jnp.exp(s - m)
    es = jnp.exp(sink - m)
    inv = 1.0 / (jnp.sum(p, axis=-1, keepdims=True) + es)
    return p * inv, es * inv


def _kv_group(qs, ks, vs, kh, sink_ref):
    q2 = jnp.concatenate([qs[2 * kh], qs[2 * kh + 1]], axis=0)
    kst = jnp.concatenate([ks[kh][0], ks[kh][1]], axis=0)
    vst = jnp.concatenate([vs[kh][0], vs[kh][1]], axis=0)
    top = lax.broadcasted_iota(jnp.int32, (2 * BLOCK, 1), 0) < BLOCK
    sinks = [jnp.where(top, sink_ref[0, 4 * kh + h], sink_ref[0, 4 * kh + 2 + h]) for h in range(2)]
    return q2, kst, vst, sinks


def _attn_load(q_ref, kvc_ref, kvp_ref, tc_ref, ta_ref, tb_ref, pc_ref, pa_ref, pb_ref):
    c, a, b = tc_ref[...], ta_ref[...], tb_ref[...]
    kc = _rope(kvc_ref[:, :KV_W], c, a, b)
    kp = _rope(kvp_ref[:, :KV_W], pc_ref[...], pa_ref[...], pb_ref[...])
    kband = jnp.concatenate([kp, kc], axis=0)
    vband = jnp.concatenate([kvp_ref[:, KV_W:], kvc_ref[:, KV_W:]], axis=0)
    qs = [_rope(q_ref[:, 128 * j:128 * (j + 1)], c, a, b).astype(BF16) for j in range(4)]
    return qs, _band_sides(kband), _band_sides(vband), (c, a, b)


def _attn_specs(clamp):
    cur = lambda i: (clamp(i), 0)
    prev = lambda i: (jnp.maximum(clamp(i) - 1, 0), 0)
    return [
        pl.BlockSpec((BLOCK, ATTN_W), cur), pl.BlockSpec((BLOCK, 2 * KV_W), cur),
        pl.BlockSpec((BLOCK, 2 * KV_W), prev),
        pl.BlockSpec((BLOCK, 128), cur), pl.BlockSpec((BLOCK, 128), cur), pl.BlockSpec((BLOCK, 128), cur),
        pl.BlockSpec((BLOCK, 128), prev), pl.BlockSpec((BLOCK, 128), prev), pl.BlockSpec((BLOCK, 128), prev),
        pl.BlockSpec(memory_space=pltpu.SMEM),
    ]


def _attn_fwd(q, kv, tabs, sinks):
    tc, ta, tb = tabs

    def body(q_ref, kvc_ref, kvp_ref, tc_ref, ta_ref, tb_ref, pc_ref, pa_ref, pb_ref, sink_ref, o_ref):
        i = pl.program_id(0)
        qs, ks, vs, _ = _attn_load(q_ref, kvc_ref, kvp_ref, tc_ref, ta_ref, tb_ref, pc_ref, pa_ref, pb_ref)
        ok = _attn_mask(i)
        for kh in range(2):
            q2, kst, vst, sinks = _kv_group(qs, ks, vs, kh, sink_ref)
            s = lax.dot_general(q2, kst, NT, preferred_element_type=F32)
            pn = [_attn_probs(s[:, 2 * BLOCK * h:2 * BLOCK * (h + 1)], ok, sinks[h])[0].astype(BF16) for h in range(2)]
            o2 = jnp.dot(jnp.concatenate(pn, axis=1), vst, preferred_element_type=F32).astype(BF16)
            for r in range(2):
                j = 2 * kh + r
                o_ref[:, 128 * j:128 * (j + 1)] = o2[BLOCK * r:BLOCK * (r + 1)]

    return pl.pallas_call(
        body, grid=(N_BLOCKS,), in_specs=_attn_specs(lambda i: i),
        out_specs=pl.BlockSpec((BLOCK, ATTN_W), lambda i: (i, 0)),
        out_shape=jax.ShapeDtypeStruct((SEQ, ATTN_W), BF16), compiler_params=_cp(), name="attn_fwd",
    )(q, kv, kv, tc, ta, tb, tc, ta, tb, sinks)


def _attn_bwd(q, kv, tabs, sinks, do):
    tc, ta, tb = tabs
    last = N_BLOCKS - 1
    clamp = lambda i: jnp.minimum(i, last)

    def place(full, side, kh):
        left = lax.broadcasted_iota(jnp.int32, full.shape, 1) < HEAD_DIM
        valid = jnp.where(left, full, 0.0) if side == 0 else jnp.where(left, 0.0, full)
        return valid if side == kh else pltpu.roll(valid, HEAD_DIM, axis=1)

    def body(q_ref, kvc_ref, kvp_ref, tc_ref, ta_ref, tb_ref, pc_ref, pa_ref, pb_ref, sink_ref, do_ref,
             dq_ref, dkv_ref, ds_ref, carry_ref):
        i = pl.program_id(0)

        @pl.when(i == 0)
        def _():
            ds_ref[...] = jnp.zeros_like(ds_ref)
            carry_ref[...] = jnp.zeros_like(carry_ref)

        @pl.when(i > last)
        def _():
            dkv_ref[...] = carry_ref[...].astype(BF16)

        @pl.when(i <= last)
        def _():
            qs, ks, vs, (c, a, b) = _attn_load(q_ref, kvc_ref, kvp_ref, tc_ref, ta_ref, tb_ref,
                                               pc_ref, pa_ref, pb_ref)
            ok = _attn_mask(i)
            dk = jnp.zeros((2 * BLOCK, 128), F32)
            dv = jnp.zeros((2 * BLOCK, 128), F32)
            dsink = jnp.zeros((1, 128), F32)
            lane = lax.broadcasted_iota(jnp.int32, (1, 128), 1)
            for kh in range(2):
                q2, kst, vst, sinks = _kv_group(qs, ks, vs, kh, sink_ref)
                do2 = jnp.concatenate([do_ref[:, 128 * (2 * kh + r):128 * (2 * kh + r + 1)] for r in range(2)],
                                      axis=0).astype(BF16)
                s = lax.dot_general(q2, kst, NT, preferred_element_type=F32)
                dp = lax.dot_general(do2, vst, NT, preferred_element_type=F32)
                pns, dss = [], []
                for h in range(2):
                    cols = slice(2 * BLOCK * h, 2 * BLOCK * (h + 1))
                    pn, ps = _attn_probs(s[:, cols], ok, sinks[h])
                    dr = jnp.sum(pn * dp[:, cols], axis=-1, keepdims=True)
                    pns.append(pn.astype(BF16))
                    dss.append((pn * (dp[:, cols] - dr) * (HEAD_DIM ** -0.5)).astype(BF16))
                    for r in range(2):
                        part = -jnp.sum((ps * dr)[BLOCK * r:BLOCK * (r + 1)])
                        dsink += jnp.where(lane == 4 * kh + 2 * r + h, part, 0.0)
                ds2, pn2 = jnp.concatenate(dss, axis=1), jnp.concatenate(pns, axis=1)
                dq2 = jnp.dot(ds2, kst, preferred_element_type=F32)
                dk2 = lax.dot_general(ds2, q2, TN, preferred_element_type=F32)
                dv2 = lax.dot_general(pn2, do2, TN, preferred_element_type=F32)
                for h in range(2):
                    dk += place(dk2[2 * BLOCK * h:2 * BLOCK * (h + 1)], h, kh)
                    dv += place(dv2[2 * BLOCK * h:2 * BLOCK * (h + 1)], h, kh)
                for r in range(2):
                    j = 2 * kh + r
                    dq_ref[:, 128 * j:128 * (j + 1)] = _rope_t(dq2[BLOCK * r:BLOCK * (r + 1)], c, a, b).astype(BF16)
            ds_ref[...] += dsink
            dk_prev = _rope_t(dk[:BLOCK], pc_ref[...], pa_ref[...], pb_ref[...])
            dk_cur = _rope_t(dk[BLOCK:], c, a, b)
            prev = jnp.concatenate([dk_prev, dv[:BLOCK]], axis=1)
            dkv_ref[...] = (carry_ref[...] + prev).astype(BF16)
            carry_ref[...] = jnp.concatenate([dk_cur, dv[BLOCK:]], axis=1)

    return pl.pallas_call(
        body, grid=(N_BLOCKS + 1,),
        in_specs=_attn_specs(clamp) + [pl.BlockSpec((BLOCK, ATTN_W), lambda i: (clamp(i), 0))],
        out_specs=[pl.BlockSpec((BLOCK, ATTN_W), lambda i: (clamp(i), 0)),
                   pl.BlockSpec((BLOCK, 2 * KV_W), lambda i: (jnp.maximum(i - 1, 0), 0)),
                   pl.BlockSpec((1, 128), lambda i: (0, 0))],
        out_shape=[jax.ShapeDtypeStruct((SEQ, ATTN_W), BF16), jax.ShapeDtypeStruct((SEQ, 2 * KV_W), BF16),
                   jax.ShapeDtypeStruct((1, 128), F32)],
        scratch_shapes=[pltpu.VMEM((BLOCK, 2 * KV_W), F32)], compiler_params=_cp(), name="attn_bwd",
    )(q, kv, kv, tc, ta, tb, tc, ta, tb, sinks, do)


def _shift_down(z, k):
    row = lax.broadcasted_iota(jnp.int32, z.shape, 0)
    return jnp.where(row < k, 0.0, pltpu.roll(z, k, axis=0))


def _shift_up(z, k):
    n = z.shape[0]
    row = lax.broadcasted_iota(jnp.int32, z.shape, 0)
    return jnp.where(row >= n - k, 0.0, pltpu.roll(z, n - k, axis=0))


def _conv_specs():
    nb = WIDTH // 128
    return [pl.BlockSpec((SEQ, 128), lambda j: (0, j)), pl.BlockSpec((SEQ, 128), lambda j: (0, nb + j)),
            pl.BlockSpec((SEQ, 128), lambda j: (0, 2 * nb + j)), pl.BlockSpec((None, 8, 128), lambda j: (0, 0, j))]


def _conv_fwd(cbx, cw, layer):
    def body(cb_ref, cc_ref, cx_ref, w_ref, o_ref):
        z = cc_ref[...] * cx_ref[...]
        s = w_ref[0:1, :] * _shift_down(z, 2) + w_ref[1:2, :] * _shift_down(z, 1) + w_ref[2:3, :] * z
        o_ref[...] = (cb_ref[...] * s).astype(BF16)

    specs = _conv_specs()
    specs[3] = pl.BlockSpec((None, 8, 128), lambda j: (layer, 0, j))
    return pl.pallas_call(
        body, grid=(WIDTH // 128,), in_specs=specs,
        out_specs=pl.BlockSpec((SEQ, 128), lambda j: (0, j)),
        out_shape=jax.ShapeDtypeStruct((SEQ, WIDTH), BF16), compiler_params=_cp(), name="conv_fwd",
    )(cbx, cbx, cbx, cw)


def _conv_bwd(cbx, cw, layer, dout, tie):
    def body(cb_ref, cc_ref, cx_ref, w_ref, do_ref, tie_ref, dcb_ref, dcc_ref, dcx_ref, dw_ref):
        cc, cx = cc_ref[...], cx_ref[...]
        z = cc * cx
        z1, z2 = _shift_down(z, 1), _shift_down(z, 2)
        w0, w1, w2 = w_ref[0:1, :], w_ref[1:2, :], w_ref[2:3, :]
        dout = do_ref[...]
        ds = dout * cb_ref[...]
        dcb_ref[...] = (dout * (w0 * z2 + w1 * z1 + w2 * z)).astype(BF16)
        dz = w2 * ds + w1 * _shift_up(ds, 1) + w0 * _shift_up(ds, 2)
        dcc_ref[...] = (dz * cx).astype(BF16)
        dcx_ref[...] = (dz * cc).astype(BF16)
        rows = [jnp.sum(ds * zz, axis=0, keepdims=True) for zz in (z2, z1, z)]
        dw_ref[...] = jnp.concatenate(rows + [jnp.zeros((5, 128), F32)], axis=0)

    col = lambda j: (0, j)
    specs = _conv_specs()
    specs[3] = pl.BlockSpec((None, 8, 128), lambda j: (layer, 0, j))
    return pl.pallas_call(
        body, grid=(WIDTH // 128,), in_specs=specs + [pl.BlockSpec((SEQ, 128), col), ANY],
        out_specs=[pl.BlockSpec((SEQ, 128), col), pl.BlockSpec((SEQ, 128), col), pl.BlockSpec((SEQ, 128), col),
                   pl.BlockSpec((8, 128), col)],
        out_shape=[jax.ShapeDtypeStruct((SEQ, WIDTH), BF16)] * 3 + [jax.ShapeDtypeStruct((8, WIDTH), F32)],
        compiler_params=_cp(), name="conv_bwd",
    )(cbx, cbx, cbx, cw, dout, tie)


def _ssm_prep_math(a_re, a_im, log_dt, bt_re, bt_im):
    dt = jnp.exp(log_dt)
    er = jnp.exp(a_re * dt)
    lr = er * jnp.cos(a_im * dt)
    li = er * jnp.sin(a_im * dt)
    n2 = a_re * a_re + a_im * a_im
    cr = ((lr - 1.0) * a_re + li * a_im) / n2
    ci = (li * a_re - (lr - 1.0) * a_im) / n2
    cr3, ci3 = cr[:, None, :], ci[:, None, :]
    return lr, li, cr3 * bt_re - ci3 * bt_im, cr3 * bt_im + ci3 * bt_re


_GS = (SSM_GROUPS, SSM_STATE)
_GHS = (SSM_GROUPS, SSM_GROUP, SSM_STATE)


def _layered(shape):
    return pl.BlockSpec((None,) + shape, lambda l: (l,) + (0,) * len(shape))


def _ssm_prep(a_re, a_im, log_dt, bt_re, bt_im):
    def body(ar, ai, ld, br, bi, o0, o1, o2, o3):
        outs = _ssm_prep_math(ar[...], ai[...], ld[...], br[...], bi[...])
        for o, v in zip((o0, o1, o2, o3), outs):
            o[...] = v

    shapes = [_GS, _GS, _GHS, _GHS]
    return pl.pallas_call(
        body, grid=(DEPTH,), in_specs=[_layered(s) for s in (_GS, _GS, (SSM_GROUPS, 1), _GHS, _GHS)],
        out_specs=[_layered(s) for s in shapes],
        out_shape=[jax.ShapeDtypeStruct((DEPTH,) + s, F32) for s in shapes],
        name="ssm_prep")(a_re, a_im, log_dt, bt_re, bt_im)


def _ssm_prep_bwd(a_re, a_im, log_dt, bt_re, bt_im, cots):
    def body(ar, ai, ld, br, bi, c0, c1, c2, c3, o0, o1, o2, o3, o4):
        _, vjp = jax.vjp(_ssm_prep_math, ar[...], ai[...], ld[...], br[...], bi[...])
        for o, v in zip((o0, o1, o2, o3, o4), vjp((c0[...], c1[...], c2[...], c3[...]))):
            o[...] = v

    ins = (_GS, _GS, (SSM_GROUPS, 1), _GHS, _GHS)
    return pl.pallas_call(
        body, grid=(DEPTH,), in_specs=[_layered(s) for s in ins + (_GS, _GS, _GHS, _GHS)],
        out_specs=[_layered(s) for s in ins],
        out_shape=[jax.ShapeDtypeStruct((DEPTH,) + s, F32) for s in ins],
        name="ssm_prep_bwd")(a_re, a_im, log_dt, bt_re, bt_im, *cots)


LANES_G = 512
N_LANE_GROUPS = SSM_GROUPS * SSM_STATE // LANES_G


def _scan_order(a):
    return a.reshape(N_CHUNKS, CHUNK, -1).transpose(1, 0, 2).reshape(a.shape)


def _time_order(a):
    return a.reshape(CHUNK, N_CHUNKS, -1).transpose(1, 0, 2).reshape(a.shape)


def _scan_in_place(xr_ref, xi_ref, ar, ai, reverse):
    shape = (N_CHUNKS, xr_ref.shape[1])
    ar, ai = jnp.broadcast_to(ar, shape), jnp.broadcast_to(ai, shape)

    def rows(tau):
        t = (CHUNK - 1 - tau) if reverse else tau
        return pl.ds(pl.multiple_of(t * N_CHUNKS, N_CHUNKS), N_CHUNKS)

    def step(tau, carry):
        sr, si = carry
        return ar * sr - ai * si + xr_ref[rows(tau), :], ar * si + ai * sr + xi_ref[rows(tau), :]

    zero = jnp.zeros(shape, F32)
    er, ei = lax.fori_loop(0, CHUNK, step, (zero, zero), unroll=8)
    qr, qi = ar, ai
    for _ in range(8):
        qr, qi = qr * qr - qi * qi, 2.0 * qr * qi
    shift = _shift_up if reverse else _shift_down
    for k in (1, 2, 4):
        sr, si = shift(er, k), shift(ei, k)
        er, ei = er + qr * sr - qi * si, ei + qr * si + qi * sr
        qr, qi = qr * qr - qi * qi, 2.0 * qr * qi
    start = (shift(er, 1), shift(ei, 1))

    def write(tau, carry):
        sr, si = step(tau, carry)
        xr_ref[rows(tau), :] = sr
        xi_ref[rows(tau), :] = si
        return sr, si

    return write, start


def _ssm_specs(layer):
    col = lambda w: pl.BlockSpec((SEQ, w), lambda g: (0, g))
    diag = pl.BlockSpec((None, None, 128, LANES_G), lambda g: (layer, g, 0, 0))
    vec = pl.BlockSpec((None, 1, LANES_G), lambda g: (layer, 0, g))
    return col, diag, vec


def _ssm_fwd(u, mats, layer, d):
    def body(u_ref, d_ref, br_ref, bi_ref, cr_ref, ci_ref, ar_ref, ai_ref, xr_ref, xi_ref, y_ref):
        uv = u_ref[...].astype(BF16)
        xr_ref[...] = jnp.dot(uv, br_ref[...], preferred_element_type=F32)
        xi_ref[...] = jnp.dot(uv, bi_ref[...], preferred_element_type=F32)
        write, start = _scan_in_place(xr_ref, xi_ref, ar_ref[...], ai_ref[...], False)
        lax.fori_loop(0, CHUNK, write, start, unroll=8)
        y = lax.dot_general(xr_ref[...].astype(BF16), cr_ref[...], NT, preferred_element_type=F32)
        y += lax.dot_general(xi_ref[...].astype(BF16), ci_ref[...], NT, preferred_element_type=F32)
        y_ref[...] = y + d_ref[...] * u_ref[...]

    col, diag, vec = _ssm_specs(layer)
    return pl.pallas_call(
        body, grid=(N_LANE_GROUPS,),
        in_specs=[col(128), pl.BlockSpec((None, 1, 128), lambda g: (layer, 0, g)),
                  diag, diag, diag, diag, vec, vec],
        out_specs=[col(LANES_G), col(LANES_G), col(128)],
        out_shape=[jax.ShapeDtypeStruct((SEQ, SSM_GROUPS * SSM_STATE), F32)] * 2
        + [jax.ShapeDtypeStruct((SEQ, WIDTH), F32)],
        compiler_params=_cp(), name="ssm_fwd",
    )(u, d, mats["b_re"], mats["b_im"], mats["c_re"], mats["c_im_neg"], mats["a_re"], mats["a_im"])


def _ssm_bwd(dy16, x_re, x_im, u, mats, layer, d):
    def body(dy_ref, u_ref, d_ref, xr_ref, xi_ref, br_ref, bi_ref, cr_ref, ci_ref, ar_ref, ai_ref,
             du_ref, dar_ref, dai_ref, dbr_ref, dbi_ref, dcr_ref, dci_ref, lr_ref, li_ref):
        dy = dy_ref[...]
        lr_ref[...] = jnp.dot(dy, cr_ref[...], preferred_element_type=F32)
        li_ref[...] = jnp.dot(dy, ci_ref[...], preferred_element_type=F32)
        write, start = _scan_in_place(lr_ref, li_ref, ar_ref[...], -ai_ref[...], True)

        def rows(t):
            return pl.ds(pl.multiple_of(t * N_CHUNKS, N_CHUNKS), N_CHUNKS)

        def grad(acc, lam, xpr, xpi):
            return acc[0] + xpr * lam[0] + xpi * lam[1], acc[1] + xpr * lam[1] - xpi * lam[0]

        def down(tau, carry):
            lam = write(tau, carry[0])
            t = CHUNK - 2 - tau
            return lam, grad(carry[1], lam, xr_ref[rows(t), :], xi_ref[rows(t), :])

        zero = jnp.zeros((N_CHUNKS, LANES_G), F32)
        lam, acc = lax.fori_loop(0, CHUNK - 1, down, (start, (zero, zero)), unroll=5)
        lam = write(CHUNK - 1, lam)
        last = rows(CHUNK - 1)
        acc = grad(acc, lam, _shift_down(xr_ref[last, :], 1), _shift_down(xi_ref[last, :], 1))
        dar_ref[...] = jnp.sum(acc[0], axis=0, keepdims=True)
        dai_ref[...] = jnp.sum(acc[1], axis=0, keepdims=True)

        l_re, l_im = lr_ref[...].astype(BF16), li_ref[...].astype(BF16)
        du = lax.dot_general(l_re, br_ref[...], NT, preferred_element_type=F32)
        du += lax.dot_general(l_im, bi_ref[...], NT, preferred_element_type=F32)
        du_ref[...] = (du + dy.astype(F32) * d_ref[...]).astype(BF16)
        uv = u_ref[...].astype(BF16)
        dbr_ref[...] = lax.dot_general(uv, l_re, TN, preferred_element_type=F32)
        dbi_ref[...] = lax.dot_general(uv, l_im, TN, preferred_element_type=F32)
        dcr_ref[...] = lax.dot_general(dy, xr_ref[...].astype(BF16), TN, preferred_element_type=F32)
        dci_ref[...] = lax.dot_general(dy, xi_ref[...].astype(BF16), TN, preferred_element_type=F32)

    col, diag, vec = _ssm_specs(layer)
    out_vec = pl.BlockSpec((1, LANES_G), lambda g: (0, g))
    out_blk = pl.BlockSpec((None, 128, LANES_G), lambda g: (g, 0, 0))
    sds = jax.ShapeDtypeStruct
    return pl.pallas_call(
        body, grid=(N_LANE_GROUPS,),
        in_specs=[col(128), col(128), pl.BlockSpec((None, 1, 128), lambda g: (layer, 0, g)),
                  col(LANES_G), col(LANES_G), diag, diag, diag, diag, vec, vec],
        out_specs=[col(128), out_vec, out_vec, out_blk, out_blk, out_blk, out_blk],
        out_shape=[sds((SEQ, WIDTH), BF16)] + [sds((1, SSM_GROUPS * SSM_STATE), F32)] * 2
        + [sds((N_LANE_GROUPS, 128, LANES_G), F32)] * 4,
        scratch_shapes=[pltpu.VMEM((SEQ, LANES_G), F32)] * 2, compiler_params=_cp(), name="ssm_bwd",
    )(dy16, u, d, x_re, x_im, mats["b_re"], mats["b_im"], mats["c_re"], mats["c_im_neg"],
      mats["a_re"], mats["a_im"])


_GELU_C = math.sqrt(2.0 / math.pi)


def _gelu(y):
    return 0.5 * y * (1.0 + jnp.tanh(_GELU_C * (y + 0.044715 * (y * y * y))))


def _glu_fwd(y, wglu):
    tt = 512

    def body(y_ref, w_ref, z_ref):
        ys = _gelu(y_ref[...])
        a = jnp.dot(ys.astype(BF16), w_ref[...], preferred_element_type=F32)
        z_ref[...] = (ys * jax.nn.sigmoid(a)).astype(BF16)

    blk = pl.BlockSpec((tt, WIDTH), lambda i: (i, 0))
    return pl.pallas_call(body, grid=(SEQ // tt,), in_specs=[blk, _full((WIDTH, WIDTH))], out_specs=blk,
                          out_shape=jax.ShapeDtypeStruct((SEQ, WIDTH), BF16), compiler_params=_cp(),
                          name="glu_fwd")(y, wglu)


def _glu_bwd(y, wglu, dz, u):
    tt = 512

    def body(y_ref, w_ref, dz_ref, u_ref, dy_ref, ys_ref, da_ref, dd_ref):
        @pl.when(pl.program_id(0) == 0)
        def _():
            dd_ref[...] = jnp.zeros_like(dd_ref)

        yv = y_ref[...]
        t = jnp.tanh(_GELU_C * (yv + 0.044715 * (yv * yv * yv)))
        ys = 0.5 * yv * (1.0 + t)
        ysb = ys.astype(BF16)
        sg = jax.nn.sigmoid(jnp.dot(ysb, w_ref[...], preferred_element_type=F32))
        dz = dz_ref[...].astype(F32)
        da = (dz * ys * sg * (1.0 - sg)).astype(BF16)
        dys = dz * sg + lax.dot_general(da, w_ref[...], NT, preferred_element_type=F32)
        dy = dys * (0.5 * (1.0 + t) + 0.5 * yv * (1.0 - t * t) * _GELU_C * (1.0 + 3 * 0.044715 * (yv * yv)))
        dy_ref[...] = dy.astype(BF16)
        ys_ref[...] = ysb
        da_ref[...] = da
        dd_ref[...] += jnp.sum(dy * u_ref[...], axis=0, keepdims=True)

    blk = pl.BlockSpec((tt, WIDTH), lambda i: (i, 0))
    return pl.pallas_call(
        body, grid=(SEQ // tt,), in_specs=[blk, _full((WIDTH, WIDTH)), blk, blk],
        out_specs=[blk, blk, blk, _full((1, WIDTH))],
        out_shape=[jax.ShapeDtypeStruct((SEQ, WIDTH), BF16)] * 3 + [jax.ShapeDtypeStruct((1, WIDTH), F32)],
        compiler_params=_cp(), name="glu_bwd")(y, wglu, dz, u)


def _mix_specs(tt, layer):
    row = lambda w: pl.BlockSpec((tt, w), lambda i: (i, 0))
    gate = lambda j: pl.BlockSpec((tt, D_MODEL), lambda i: (i, j))
    wo = lambda j: pl.BlockSpec((D_MODEL, WIDTH), lambda i: (0, j))
    return [row(D_MODEL), row(WIDTH), row(WIDTH), row(WIDTH), gate(0), gate(1), gate(2),
            pl.BlockSpec((None, 1, GATE_W), lambda i: (layer, 0, 0)), wo(0), wo(1), wo(2),
            _full((D_MODEL, D_MODEL))]


def _mix_branches(o_ref, c_ref, z_ref, g_refs, b_ref, wa_ref, wc_ref, ws_ref):
    ys = [lax.dot_general(r[...], w[...], NT, preferred_element_type=F32)
          for r, w in ((o_ref, wa_ref), (c_ref, wc_ref), (z_ref, ws_ref))]
    gates = [jax.nn.sigmoid(g_refs[j][...] + b_ref[:, D_MODEL * j:D_MODEL * (j + 1)]) for j in range(3)]
    return ys, gates


def _mix_fwd(x, o, cv, z, glog, b_gate, layer, wbt, wmix, tie):
    tt = 256

    def body(x_ref, o_ref, c_ref, z_ref, g0, g1, g2, b_ref, wa_ref, wc_ref, ws_ref, wm_ref, tie_ref, x1_ref):
        ys, gates = _mix_branches(o_ref, c_ref, z_ref, (g0, g1, g2), b_ref, wa_ref, wc_ref, ws_ref)
        merged = gates[0] * ys[0] + gates[1] * ys[1] + gates[2] * ys[2]
        x1_ref[...] = x_ref[...] + jnp.dot(merged.astype(BF16), wm_ref[...], preferred_element_type=F32)

    return pl.pallas_call(
        body, grid=(SEQ // tt,), in_specs=_mix_specs(tt, layer) + [ANY],
        out_specs=pl.BlockSpec((tt, D_MODEL), lambda i: (i, 0)),
        out_shape=jax.ShapeDtypeStruct((SEQ, D_MODEL), F32), compiler_params=_cp(), name="mix_fwd",
    )(x, o, cv, z, glog, glog, glog, b_gate, wbt, wbt, wbt, wmix, tie)


def _mix_bwd(dx1, o, cv, z, glog, b_gate, layer, wbt, wmix, tie):
    tt = 256

    def body(dx_ref, o_ref, c_ref, z_ref, g0, g1, g2, b_ref, wa_ref, wc_ref, ws_ref, wm_ref, tie_ref,
             mg_ref, dya_ref, dyc_ref, dys_ref, do_ref, dc_ref, dz_ref, dgl_ref, db_ref):
        @pl.when(pl.program_id(0) == 0)
        def _():
            db_ref[...] = jnp.zeros_like(db_ref)

        ys, gates = _mix_branches(o_ref, c_ref, z_ref, (g0, g1, g2), b_ref, wa_ref, wc_ref, ws_ref)
        mg_ref[...] = (gates[0] * ys[0] + gates[1] * ys[1] + gates[2] * ys[2]).astype(BF16)
        dm = lax.dot_general(dx_ref[...].astype(BF16), wm_ref[...], NT, preferred_element_type=F32)
        for j, (dy_ref, w_ref, d_ref) in enumerate(((dya_ref, wa_ref, do_ref), (dyc_ref, wc_ref, dc_ref),
                                                    (dys_ref, ws_ref, dz_ref))):
            dy = (dm * gates[j]).astype(BF16)
            dy_ref[...] = dy
            d_ref[...] = jnp.dot(dy, w_ref[...], preferred_element_type=F32)
            dgl = dm * ys[j] * gates[j] * (1.0 - gates[j])
            dgl_ref[:, D_MODEL * j:D_MODEL * (j + 1)] = dgl.astype(BF16)
            db_ref[:, D_MODEL * j:D_MODEL * (j + 1)] += jnp.sum(dgl, axis=0, keepdims=True)

    row = lambda w: pl.BlockSpec((tt, w), lambda i: (i, 0))
    sds = jax.ShapeDtypeStruct
    return pl.pallas_call(
        body, grid=(SEQ // tt,), in_specs=_mix_specs(tt, layer) + [ANY],
        out_specs=[row(D_MODEL)] * 4 + [row(WIDTH)] * 3 + [row(GATE_W), _full((1, GATE_W))],
        out_shape=[sds((SEQ, D_MODEL), BF16)] * 4 + [sds((SEQ, WIDTH), F32)] * 3
        + [sds((SEQ, GATE_W), BF16), sds((1, GATE_W), F32)],
        compiler_params=_cp(), name="mix_bwd",
    )(dx1, o, cv, z, glog, glog, glog, b_gate, wbt, wbt, wbt, wmix, tie)


def _ffn_out_fwd(x1, gu, wout, tie):
    tt = 256

    def body(x_ref, gt_ref, up_ref, w_ref, tie_ref, o_ref):
        gt = gt_ref[...]
        act = (gt * jax.nn.sigmoid(gt) * up_ref[...]).astype(BF16)
        o_ref[...] = x_ref[...] + jnp.dot(act, w_ref[...], preferred_element_type=F32)

    return pl.pallas_call(
        body, grid=(SEQ // tt,),
        in_specs=[pl.BlockSpec((tt, D_MODEL), lambda i: (i, 0)), pl.BlockSpec((tt, FFN_H), lambda i: (i, 0)),
                  pl.BlockSpec((tt, FFN_H), lambda i: (i, 1)), _full((FFN_H, D_MODEL)), ANY],
        out_specs=pl.BlockSpec((tt, D_MODEL), lambda i: (i, 0)),
        out_shape=jax.ShapeDtypeStruct((SEQ, D_MODEL), F32), compiler_params=_cp(), name="ffn_out_fwd",
    )(x1, gu, gu, wout, tie)


def _ffn_out_bwd(dx2, gu, wout, tie):
    tt = 256

    def body(dx_ref, gt_ref, up_ref, w_ref, tie_ref, dgu_ref, act_ref):
        gt, up = gt_ref[...], up_ref[...]
        sg = jax.nn.sigmoid(gt)
        silu = gt * sg
        act_ref[...] = (silu * up).astype(BF16)
        dact = lax.dot_general(dx_ref[...].astype(BF16), w_ref[...], NT, preferred_element_type=F32)
        dgu_ref[:, :FFN_H] = (dact * up * (sg * (1.0 + gt * (1.0 - sg)))).astype(BF16)
        dgu_ref[:, FFN_H:] = (dact * silu).astype(BF16)

    return pl.pallas_call(
        body, grid=(SEQ // tt,),
        in_specs=[pl.BlockSpec((tt, D_MODEL), lambda i: (i, 0)), pl.BlockSpec((tt, FFN_H), lambda i: (i, 0)),
                  pl.BlockSpec((tt, FFN_H), lambda i: (i, 1)), _full((FFN_H, D_MODEL)), ANY],
        out_specs=[pl.BlockSpec((tt, 2 * FFN_H), lambda i: (i, 0)), pl.BlockSpec((tt, FFN_H), lambda i: (i, 0))],
        out_shape=[jax.ShapeDtypeStruct((SEQ, 2 * FFN_H), BF16), jax.ShapeDtypeStruct((SEQ, FFN_H), BF16)],
        compiler_params=_cp(), name="ffn_out_bwd",
    )(dx2, gu, gu, wout, tie)


def _loss_head(x, g, target):
    tt = 256

    def body(x_ref, g_ref, t_ref, loss_ref, dx_ref, dg_ref):
        @pl.when(pl.program_id(0) == 0)
        def _():
            loss_ref[...] = jnp.zeros_like(loss_ref)
            dg_ref[...] = jnp.zeros_like(dg_ref)

        xv = x_ref[...]
        r = lax.rsqrt(jnp.mean(xv * xv, axis=-1, keepdims=True) + NORM_EPS)
        xh = xv * r
        err = xh * g_ref[...] - t_ref[...]
        loss_ref[...] += 0.5 * jnp.sum(jnp.mean(err * err, axis=-1, keepdims=True))
        dy = err * (1.0 / D_MODEL)
        gy = dy * g_ref[...]
        dx_ref[...] = r * (gy - xh * jnp.mean(gy * xh, axis=-1, keepdims=True))
        dg_ref[...] += jnp.sum(dy * xh, axis=0, keepdims=True)

    row = pl.BlockSpec((tt, D_MODEL), lambda i: (i, 0))
    return pl.pallas_call(
        body, grid=(SEQ // tt,), in_specs=[row, _full((1, D_MODEL)), row],
        out_specs=[_full((1, 128)), row, _full((1, D_MODEL))],
        out_shape=[jax.ShapeDtypeStruct((1, 128), F32), jax.ShapeDtypeStruct((SEQ, D_MODEL), F32),
                   jax.ShapeDtypeStruct((1, D_MODEL), F32)],
        compiler_params=_cp(), name="loss_head")(x, g, target)


def _adamw(parts, w, m, v, tr, name, groups=None, fill=None, tie=None):
    n_groups, rows, cols = w.shape
    n_parts = parts.shape[1]
    lo, hi = groups if groups is not None else (0, n_groups)

    def body(p_ref, w_ref, m_ref, v_ref, *rest):
        g_ref, d_ref, nm_ref, nv_ref = rest[-4:]
        g = p_ref[0].astype(F32)
        for k in range(1, n_parts):
            g = g + p_ref[k].astype(F32)
        nm = B1 * m_ref[...] + (1.0 - B1) * g
        nv = B2 * v_ref[...] + (1.0 - B2) * (g * g)
        m_hat = nm / (1.0 - B1 ** STEP)
        v_hat = nv / (1.0 - B2 ** STEP)
        g_ref[...] = g
        d_ref[...] = -LR * (m_hat / (jnp.sqrt(v_hat) + ADAM_EPS) + WD * w_ref[...])
        nm_ref[...] = nm
        nv_ref[...] = nv

    blk = pl.BlockSpec((None, tr, cols), lambda l, i: (l + lo, i, 0))
    p_lo = lo if parts.shape[0] == n_groups else 0
    extra = ([] if fill is None else list(fill)) + ([] if tie is None else [tie])
    return pl.pallas_call(
        body, grid=(hi - lo, rows // tr),
        in_specs=[pl.BlockSpec((None, n_parts, tr, cols), lambda l, i: (l + p_lo, 0, i, 0)), blk, blk, blk]
        + [ANY] * len(extra),
        out_specs=[blk] * 4, out_shape=[jax.ShapeDtypeStruct((n_groups, rows, cols), F32)] * 4,
        input_output_aliases={} if fill is None else {4 + j: j for j in range(4)},
        compiler_params=_cp(), name=name)(parts, w, m, v, *extra)


def _split_start(name, arrays, n_sems, plan, after=None):
    n = len(arrays)
    order = [] if after is None else [after]
    n_in = n + len(order)

    def body(*refs):
        ins, send_sems, recv_sems, token = refs[:n], refs[n_in], refs[n_in + 1], refs[-1]
        for src, dst, k, to in plan(ins)[0]:
            pltpu.make_async_remote_copy(src_ref=src, dst_ref=dst, send_sem=send_sems.at[k], recv_sem=recv_sems.at[k],
                                         device_id=to, device_id_type=MESH_ID).start()
        token[...] = jnp.zeros_like(token)

    outs = pl.pallas_call(
        body, name=name,
        out_shape=(pltpu.SemaphoreType.DMA((n_sems,)), pltpu.SemaphoreType.DMA((n_sems,)),
                   *[pltpu.HBM(a.shape, a.dtype) for a in arrays], jax.ShapeDtypeStruct((8, 128), F32)),
        in_specs=[HBM] * n + [ANY] * len(order),
        out_specs=(SEM, SEM, *[HBM] * n, pl.BlockSpec(memory_space=pltpu.VMEM)),
        input_output_aliases={i: 2 + i for i in range(n)},
        compiler_params=pltpu.CompilerParams(has_side_effects=EFFECT),
    )(*[pltpu.with_memory_space_constraint(a, pltpu.HBM) for a in arrays], *order)
    return outs[0], outs[1], list(outs[2:2 + n]), outs[-1]


def _split_wait(name, arrays, send_sems, recv_sems, after, plan):
    n = len(arrays)

    def body(*refs):
        ins, s_sems, r_sems = refs[:n], refs[n], refs[n + 1]
        sends, arrivals = plan(ins)
        x, y, c = lax.axis_index("x"), lax.axis_index("y"), lax.axis_index("c")
        for src, dst, k, to in sends:
            pltpu.make_async_remote_copy(src_ref=src, dst_ref=dst, send_sem=s_sems.at[k], recv_sem=r_sems.at[k],
                                         device_id=to, device_id_type=MESH_ID).wait_send()
        for dst, k in arrivals:
            pltpu.make_async_remote_copy(src_ref=dst, dst_ref=dst, send_sem=s_sems.at[k], recv_sem=r_sems.at[k],
                                         device_id=(x, y, c), device_id_type=MESH_ID).wait_recv()

    return pl.pallas_call(
        body, name=name, out_shape=[pltpu.HBM(a.shape, a.dtype) for a in arrays],
        in_specs=[HBM] * n + [SEM, SEM, ANY], out_specs=[HBM] * n,
        input_output_aliases={i: i for i in range(n)},
        compiler_params=pltpu.CompilerParams(has_side_effects=EFFECT),
    )(*arrays, send_sems, recv_sems, after)


def _chips():
    x, y, c = lax.axis_index("x"), lax.axis_index("y"), lax.axis_index("c")
    return x, y, c, [(1 - x, y), (x, 1 - y), (1 - x, 1 - y)]


def _plan_gather_chips(refs):
    x, y, c, chips = _chips()
    me = 4 * x + 2 * y + c
    n = len(refs) // 2
    sends, arrivals = [], []
    for i in range(n):
        src, land = refs[i], refs[n + i]
        sends.append((src, land.at[me], 4 * i, (x, y, 1 - c)))
        arrivals.append((land.at[4 * x + 2 * y + 1 - c], 4 * i))
        for j, (px, py) in enumerate(chips):
            sends.append((src, land.at[me], 4 * i + 1 + j, (px, py, c)))
            arrivals.append((land.at[4 * px + 2 * py + c], 4 * i + 1 + j))
    return sends, arrivals


def _plan_gather_pass(refs):
    x, y, c, chips = _chips()
    sends, arrivals = [], []
    for i in range(len(refs)):
        for j, (px, py) in enumerate(chips):
            slot = refs[i].at[4 * px + 2 * py + c]
            sends.append((slot, slot, 4 * i + j, (x, y, 1 - c)))
            arrivals.append((refs[i].at[4 * px + 2 * py + 1 - c], 4 * i + j))
        back = refs[i].at[4 * x + 2 * y + 1 - c]
        sends.append((back, back, 4 * i + 3, (x, y, 1 - c)))
        arrivals.append((refs[i].at[4 * x + 2 * y + c], 4 * i + 3))
    return sends, arrivals


def _plan_scatter_pair(refs):
    x, y, c = lax.axis_index("x"), lax.axis_index("y"), lax.axis_index("c")
    n = len(refs) // 2
    sends, arrivals = [], []
    for i in range(n):
        for q in range(4):
            sends.append((refs[i].at[q, 1 - c], refs[n + i].at[q], 4 * i + q, (x, y, 1 - c)))
            arrivals.append((refs[n + i].at[q], 4 * i + q))
    return sends, arrivals


def _plan_scatter_chips(layer):
    def plan(refs):
        x, y, c, chips = _chips()
        n = len(refs) // 2
        sends, arrivals = [], []
        for i in range(n):
            for j, (px, py) in enumerate(chips):
                sends.append((refs[i].at[2 * px + py], refs[n + i].at[layer, 2 * x + y], 3 * i + j, (px, py, c)))
                arrivals.append((refs[n + i].at[layer, 2 * px + py], 3 * i + j))
        return sends, arrivals

    return plan


def _pair_sum(parts4, from_pair, landing, layer, core, tr, name):
    _, _, rows, cols = parts4.shape

    def body(c_ref, p_ref, s_ref, l_ref, sum_ref, land_ref):
        v = (p_ref[...].astype(F32) + s_ref[...].astype(F32)).astype(BF16)
        sum_ref[...] = v
        land_ref[...] = v

    blk = pl.BlockSpec((None, tr, cols), lambda q, i, c_ref: (q, i, 0))
    return pl.pallas_call(
        body,
        grid_spec=pltpu.PrefetchScalarGridSpec(
            num_scalar_prefetch=1, grid=(4, rows // tr),
            in_specs=[pl.BlockSpec((None, None, tr, cols), lambda q, i, c_ref: (q, c_ref[0], i, 0)), blk, ANY],
            out_specs=[blk, pl.BlockSpec((None, None, tr, cols), lambda q, i, c_ref: (layer, q, i, 0))]),
        out_shape=[jax.ShapeDtypeStruct((4, rows, cols), BF16), jax.ShapeDtypeStruct(landing.shape, BF16)],
        input_output_aliases={3: 1}, compiler_params=_cp(), name=name,
    )(core, parts4, from_pair, landing)


def _travel_layout(t):
    tr = lambda a: jnp.swapaxes(a, 1, 2)
    branch = jnp.concatenate([tr(t["w_attn_o"]), tr(t["w_conv_o"]), tr(t["w_ssm_o"])], axis=2)
    return [tr(t["w_in"]), tr(t["w_ffn_in"]), t["w_ffn_out"], t["w_mix_o"], branch, t["w_ssm_glu"]]


def _native_layout(a):
    tr = lambda x: jnp.swapaxes(x, 1, 2)
    b = a[4]
    return {"w_in": tr(a[0]), "w_ffn_in": tr(a[1]), "w_ffn_out": a[2], "w_mix_o": a[3],
            "w_attn_o": tr(b[:, :, :WIDTH]), "w_conv_o": tr(b[:, :, WIDTH:2 * WIDTH]),
            "w_ssm_o": tr(b[:, :, 2 * WIDTH:]), "w_ssm_glu": a[5]}


def _embed(t):
    eye = jnp.eye(8, dtype=t.dtype)
    t = t.reshape(DEPTH, N_LANE_GROUPS, 8, SSM_GROUP, SSM_STATE)
    return (t[:, :, :, :, None, :] * eye[None, None, :, None, :, None]).reshape(DEPTH, N_LANE_GROUPS, 128, LANES_G)


def _diag_blocks(t):
    t = t.reshape(DEPTH, N_LANE_GROUPS, 8, SSM_GROUP, 8, SSM_STATE)
    return jnp.einsum("lgahap->lgahp", t).reshape(DEPTH, SSM_GROUPS, SSM_GROUP, SSM_STATE)


def _rope_tabs():
    pos = jnp.arange(SEQ, dtype=F32)
    inv_freq = ROPE_THETA ** (-jnp.arange(0, ROT_DIM, 2, dtype=F32) / ROT_DIM)
    ang = pos[:, None] * inv_freq[None, :]
    cos, sin = jnp.cos(ang), jnp.sin(ang)
    one, zero = jnp.ones((SEQ, HEAD_DIM - ROT_DIM), F32), jnp.zeros((SEQ, HEAD_DIM - ROT_DIM), F32)
    z8 = jnp.zeros((SEQ, 8), F32)
    head = lambda *p: jnp.tile(jnp.concatenate(p, axis=1), (1, 2))
    return head(cos, cos, one), head(-sin, z8, zero), head(z8, sin, zero)


def _ssm_mats(sp):
    lr, li, bbr, bbi = _ssm_prep(sp["a_re"], sp["a_im"], sp["log_dt"], sp["bt_re"], sp["bt_im"])
    lanes = SSM_GROUPS * SSM_STATE
    return {
        "a_re": lr.reshape(DEPTH, 1, lanes), "a_im": li.reshape(DEPTH, 1, lanes),
        "b_re": _embed(bbr).astype(BF16), "b_im": _embed(bbi).astype(BF16),
        "c_re": _embed(sp["c_re"]).astype(BF16), "c_im_neg": _embed(-sp["c_im"]).astype(BF16),
    }


def _layer_fwd(x, i, w, rp, mats, tabs, tie, hooks):
    q, kv, cbx, u, glog, h = _rms_mm_in(x, rp["norm_mix"][i], w["win_t"], tie)
    o = _attn_fwd(q, kv, tabs, rp["attn_sinks"][i])
    cv = _conv_fwd(cbx, rp["conv_w"], i)
    u = _scan_order(u)
    x_re, x_im, y = _ssm_fwd(u, mats, i, rp["ssm_d"])
    z = _time_order(_glu_fwd(y, w["wglu"]))
    x1 = _mix_fwd(x, o, cv, z, glog, rp["b_gate"], i, w["branch_t"], w["wmix"], hooks["early"](z))
    hooks["pre_ffn"](x1)
    gu, h2 = _rms_mm_ffn(x1, rp["norm_ffn"][i], w["wffn_t"])
    x2 = _ffn_out_fwd(x1, gu, w["wout"], hooks["mid"](h2))
    kept = dict(x=x, q=q, kv=kv, cbx=cbx, u=u, glog=glog, h=h, o=o, cv=cv, z=z, y=y,
                x_re=x_re, x_im=x_im, x1=x1, gu=gu, h2=h2)
    return x2, kept


def _layer_bwd(dx2, k, i, w, rp, mats, tabs, tie, hooks):
    dgu, act = _ffn_out_bwd(dx2, k["gu"], w["wout"], tie)
    g_wout = _mm_tn(act, dx2, tm=FFN_H // 2, tn=1024, name="mm_tn_ffn_out")
    g_wffn_t = _mm_tn(dgu, k["h2"], tm=FFN_H // 2, tn=1024, name="mm_tn_ffn_in")
    dx1, d_norm_ffn = _mm_rmsbwd([dgu], w["wffn_t"], k["x1"], rp["norm_ffn"][i], dx2, "mm_rmsbwd_ffn")

    mg, dya, dyc, dys, do, dcv, dz, dgl, db_gate = _mix_bwd(
        dx1, k["o"], k["cv"], k["z"], k["glog"], rp["b_gate"], i, w["branch_t"], w["wmix"],
        hooks["mid"]((g_wffn_t, g_wout, d_norm_ffn)))
    g_wmix = _mm_tn(mg, dx1, tm=1024, tn=512, name="mm_tn_mix")
    g_branch_t = _tn_branches((dya, dyc, dys), (k["o"], k["cv"], k["z"]))

    dy16, ys16, da16, dd = _glu_bwd(k["y"], w["wglu"], _scan_order(dz), k["u"])
    g_wglu = _mm_tn(ys16, da16, tm=256, tn=512, name="mm_tn_glu")
    du, da_re, da_im, db_re, db_im, dc_re, dc_im = _ssm_bwd(dy16, k["x_re"], k["x_im"], k["u"], mats, i,
                                                             rp["ssm_d"])
    du = _time_order(du)

    dcb, dcc, dcx, d_conv_w = _conv_bwd(k["cbx"], rp["conv_w"], i, dcv, hooks["late"](du))
    dq, dkv, d_sinks = _attn_bwd(k["q"], k["kv"], tabs, rp["attn_sinks"][i], do)

    pieces = [dq, dkv, dcb, dcc, dcx, du, dgl]
    g_win_t = _tn_pieces(pieces, k["h"])
    dx, d_norm_mix = _mm_rmsbwd(pieces, w["win_t"], k["x"], rp["norm_mix"][i], dx1, "mm_rmsbwd_in")

    grads = [g_win_t, g_wffn_t, g_wout, g_wmix, g_branch_t, g_wglu]
    small = dict(norm_mix=d_norm_mix, b_gate=db_gate, attn_sinks=d_sinks, ssm_d=dd, norm_ffn=d_norm_ffn,
                 conv_w=d_conv_w, da_re=da_re, da_im=da_im, db_re=db_re, db_im=db_im, dc_re=dc_re, dc_im=dc_im)
    return dx, grads, small


def _replicated_grads(sg, sp):
    stack = lambda name: jnp.stack([sg[i][name] for i in range(DEPTH)])
    cots = (stack("da_re").reshape(DEPTH, *_GS), stack("da_im").reshape(DEPTH, *_GS),
            _diag_blocks(stack("db_re")), _diag_blocks(stack("db_im")))
    d_a_re, d_a_im, d_log_dt, d_bt_re, d_bt_im = _ssm_prep_bwd(
        sp["a_re"], sp["a_im"], sp["log_dt"], sp["bt_re"], sp["bt_im"], cots)
    sgrads = {"norm_mix": stack("norm_mix"), "b_gate": stack("b_gate"),
              "attn_sinks": stack("attn_sinks")[:, :, :N_Q_HEADS], "ssm_a_re": d_a_re, "ssm_a_im": d_a_im,
              "ssm_b_re": jnp.swapaxes(d_bt_re, 2, 3), "ssm_b_im": jnp.swapaxes(d_bt_im, 2, 3),
              "ssm_c_re": _diag_blocks(stack("dc_re")), "ssm_c_im": -_diag_blocks(stack("dc_im")),
              "ssm_d": stack("ssm_d"), "ssm_log_dt": d_log_dt, "norm_ffn": stack("norm_ffn")}
    return sgrads, stack("conv_w")[:, :3]


def kernel(x, norm_mix, w_in, b_gate, attn_sinks, w_attn_o, conv_w, w_conv_o, ssm_a_re, ssm_a_im, ssm_b_re, ssm_b_im, ssm_c_re, ssm_c_im, ssm_d, ssm_log_dt, w_ssm_glu, w_ssm_o, w_mix_o, norm_ffn, w_ffn_in, w_ffn_out, norm_final, loss_target, m_norm_mix, m_w_in, m_b_gate, m_attn_sinks, m_w_attn_o, m_conv_w, m_w_conv_o, m_ssm_a_re, m_ssm_a_im, m_ssm_b_re, m_ssm_b_im, m_ssm_c_re, m_ssm_c_im, m_ssm_d, m_ssm_log_dt, m_w_ssm_glu, m_w_ssm_o, m_w_mix_o, m_norm_ffn, m_w_ffn_in, m_w_ffn_out, m_norm_final, v_norm_mix, v_w_in, v_b_gate, v_attn_sinks, v_w_attn_o, v_conv_w, v_w_conv_o, v_ssm_a_re, v_ssm_a_im, v_ssm_b_re, v_ssm_b_im, v_ssm_c_re, v_ssm_c_im, v_ssm_d, v_ssm_log_dt, v_w_ssm_glu, v_w_ssm_o, v_w_mix_o, v_norm_ffn, v_w_ffn_in, v_w_ffn_out, v_norm_final):
    big = {"w": dict(w_in=w_in, w_attn_o=w_attn_o, w_conv_o=w_conv_o, w_ssm_glu=w_ssm_glu, w_ssm_o=w_ssm_o,
                     w_mix_o=w_mix_o, w_ffn_in=w_ffn_in, w_ffn_out=w_ffn_out),
           "m": dict(w_in=m_w_in, w_attn_o=m_w_attn_o, w_conv_o=m_w_conv_o, w_ssm_glu=m_w_ssm_glu,
                     w_ssm_o=m_w_ssm_o, w_mix_o=m_w_mix_o, w_ffn_in=m_w_ffn_in, w_ffn_out=m_w_ffn_out),
           "v": dict(w_in=v_w_in, w_attn_o=v_w_attn_o, w_conv_o=v_w_conv_o, w_ssm_glu=v_w_ssm_glu,
                     w_ssm_o=v_w_ssm_o, w_mix_o=v_w_mix_o, w_ffn_in=v_w_ffn_in, w_ffn_out=v_w_ffn_out)}
    small = {"w": dict(norm_mix=norm_mix, b_gate=b_gate, attn_sinks=attn_sinks, ssm_a_re=ssm_a_re,
                       ssm_a_im=ssm_a_im, ssm_b_re=ssm_b_re, ssm_b_im=ssm_b_im, ssm_c_re=ssm_c_re,
                       ssm_c_im=ssm_c_im, ssm_d=ssm_d, ssm_log_dt=ssm_log_dt, norm_ffn=norm_ffn),
             "m": dict(norm_mix=m_norm_mix, b_gate=m_b_gate, attn_sinks=m_attn_sinks, ssm_a_re=m_ssm_a_re,
                       ssm_a_im=m_ssm_a_im, ssm_b_re=m_ssm_b_re, ssm_b_im=m_ssm_b_im, ssm_c_re=m_ssm_c_re,
                       ssm_c_im=m_ssm_c_im, ssm_d=m_ssm_d, ssm_log_dt=m_ssm_log_dt, norm_ffn=m_norm_ffn),
             "v": dict(norm_mix=v_norm_mix, b_gate=v_b_gate, attn_sinks=v_attn_sinks, ssm_a_re=v_ssm_a_re,
                       ssm_a_im=v_ssm_a_im, ssm_b_re=v_ssm_b_re, ssm_b_im=v_ssm_b_im, ssm_c_re=v_ssm_c_re,
                       ssm_c_im=v_ssm_c_im, ssm_d=v_ssm_d, ssm_log_dt=v_ssm_log_dt, norm_ffn=v_norm_ffn)}
    finals = {"w": norm_final, "m": m_norm_final, "v": v_norm_final}
    convs = {"w": conv_w, "m": m_conv_w, "v": v_conv_w}
    mine = 4 * lax.axis_index("x") + 2 * lax.axis_index("y") + lax.axis_index("c")

    travel = {s: _travel_layout(big[s]) for s in "wmv"}
    stacked16 = [a.astype(BF16) for a in travel["w"]]
    rp = {"norm_mix": norm_mix[:, None], "norm_ffn": norm_ffn[:, None], "attn_sinks": attn_sinks[:, None],
          "b_gate": b_gate[:, None], "ssm_d": ssm_d[:, None]}
    sp = {"a_re": ssm_a_re, "a_im": ssm_a_im, "log_dt": ssm_log_dt[:, :, None],
          "bt_re": jnp.swapaxes(ssm_b_re, 2, 3), "bt_im": jnp.swapaxes(ssm_b_im, 2, 3),
          "c_re": ssm_c_re, "c_im": ssm_c_im}
    rows_tile = {"win_t": 368, "wffn_t": 352, "wout": 176, "wmix": 128, "branch_t": 128, "wglu": 64}
    core = lax.axis_index("c").astype(jnp.int32).reshape(1)
    no_tie = jnp.zeros((8, 128), F32)

    def place_own(srcs):
        return [lax.empty((N_DEV,) + s.shape, s.dtype) for s in srcs]

    def gather_chips(tag, i, kinds, after, extra=()):
        srcs = [stacked16[j][i] for j in kinds] + list(extra)
        s_sems, r_sems, arrays, token = _split_start(
            f"gather_chips_start_{tag}", srcs + place_own(srcs), 4 * len(srcs), _plan_gather_chips, after)
        return (tag, s_sems, r_sems, arrays), token

    def gather_pass(state, after):
        tag, s_sems, r_sems, arrays = state
        arrays = _split_wait(f"gather_chips_wait_{tag}", arrays, s_sems, r_sems, after, _plan_gather_chips)
        n = len(arrays) // 2
        s_sems, r_sems, lands, token = _split_start(
            f"gather_pass_start_{tag}", list(arrays[n:]), 4 * n, _plan_gather_pass)
        return (tag, s_sems, r_sems, lands), token

    def gather_done(state, after, kinds):
        tag, s_sems, r_sems, lands = state
        lands = _split_wait(f"gather_pass_wait_{tag}", lands, s_sems, r_sems, after, _plan_gather_pass)
        named = {KINDS[j][0]: a.reshape(N_DEV * KINDS[j][1], KINDS[j][2]) for a, j in zip(lands, kinds)}
        return named, list(lands[len(kinds):])

    all_kinds, mixer_kinds, ffn_kinds = tuple(range(len(KINDS))), (0, 3, 4, 5), (1, 2)
    no_hooks = {name: (lambda value: no_tie) for name in ("early", "pre_ffn", "mid", "late")}
    state, _ = gather_chips("0m", 0, mixer_kinds, None, extra=[jnp.pad(conv_w.reshape(6, 128), ((0, 2), (0, 0)))])
    mats = _ssm_mats(sp)
    tabs = _rope_tabs()
    ready = sum(a.reshape(-1)[:1].astype(F32) for a in list(mats.values()) + list(tabs))
    state, _ = gather_pass(state, ready)
    ffn_state, tie = gather_chips("0f", 0, ffn_kinds, state[3][0])
    w_next, (conv_all,) = gather_done(state, tabs[2], mixer_kinds)
    conv_full = conv_all[:, :6].reshape(N_DEV, DEPTH, 3, 64).transpose(1, 2, 0, 3).reshape(DEPTH, 3, WIDTH)
    rp["conv_w"] = jnp.pad(conv_full, ((0, 0), (0, 5), (0, 0)))

    act = x[0]
    weights, kept = [], []
    for i in range(DEPTH):
        w_i, hooks, held = w_next, dict(no_hooks), {}

        def early(value, ffn_state=ffn_state, held=held):
            held["ffn"], token = gather_pass(ffn_state, value)
            return token

        def pre_ffn(value, w_i=w_i, held=held):
            w_i.update(gather_done(held["ffn"], value, ffn_kinds)[0])

        hooks.update(early=early, pre_ffn=pre_ffn)
        if i + 1 < DEPTH:
            state, tie = gather_chips(f"{i + 1}m", i + 1, mixer_kinds, tie if i == 0 else w_i["win_t"])

            def mid(value, i=i, state=state, held=held):
                held["next"], token = gather_pass(state, value)
                held["next_ffn"], token = gather_chips(f"{i + 1}f", i + 1, ffn_kinds, token)
                return token

            hooks.update(mid=mid)
        act, k = _layer_fwd(act, i, w_i, rp, mats, tabs, tie, hooks)
        if i + 1 < DEPTH:
            w_next, _ = gather_done(held["next"], act, mixer_kinds)
            ffn_state, tie = held["next_ffn"], no_tie
        weights.append(w_i)
        kept.append(k)
    loss_row, dx, d_norm_final = _loss_head(act, norm_final[None], loss_target[0])
    loss = lax.psum(loss_row[0, 0], ("x", "y", "c"))

    landings = [lax.empty((DEPTH, 4, r, c), BF16) for _, r, c in KINDS]
    landings0 = [lax.empty((1, 4, r, c), BF16) for _, r, c in KINDS]

    def scatter_pair(tag, kinds, grads, after):
        parts4 = [g.reshape(4, 2, KINDS[j][1], KINDS[j][2]) for g, j in zip(grads, kinds)]
        zones = [lax.empty((4, KINDS[j][1], KINDS[j][2]), BF16) for j in kinds]
        s_sems, r_sems, arrays, token = _split_start(
            f"scatter_pair_start_{tag}", parts4 + zones, 4 * len(kinds), _plan_scatter_pair, after)
        return (tag, kinds, s_sems, r_sems, arrays), token

    def scatter_chips(state, lands, slot, after):
        tag, kinds, s_sems, r_sems, arrays = state
        arrays = _split_wait(f"scatter_pair_wait_{tag}", arrays, s_sems, r_sems, after, _plan_scatter_pair)
        n = len(kinds)
        sums, mine_lands = [], []
        for k, j in enumerate(kinds):
            name = KINDS[j][0]
            chip_sum, land = _pair_sum(arrays[k], arrays[n + k], lands[j], slot, core, rows_tile[name],
                                       f"pair_sum_{name}")
            sums.append(chip_sum)
            mine_lands.append(land)
        s_sems, r_sems, arrays, token = _split_start(
            f"scatter_chips_start_{tag}", sums + mine_lands, 3 * n, _plan_scatter_chips(slot))
        return (tag, kinds, slot, s_sems, r_sems, arrays), token

    def scatter_done(state, lands, after):
        tag, kinds, slot, s_sems, r_sems, arrays = state
        arrays = _split_wait(f"scatter_chips_wait_{tag}", arrays, s_sems, r_sems, after, _plan_scatter_chips(slot))
        lands = list(lands)
        for k, j in enumerate(kinds):
            lands[j] = arrays[len(kinds) + k]
        return lands

    sg = [None] * DEPTH
    pending, tie = None, no_tie
    for i in reversed(range(DEPTH)):
        hooks, held = dict(no_hooks), {}
        if pending is not None:
            def mid(value, i=i, pending=pending, held=held):
                held["chips"], token = scatter_chips(pending, landings, i + 1, value[2])
                if i == 0:
                    held["ffn_pair"], token = scatter_pair("0f", ffn_kinds, value[:2], token)
                return token

            hooks.update(mid=mid)
        if i == 0:
            def late(value, held=held):
                held["ffn_chips"], token = scatter_chips(held["ffn_pair"], landings0, 0, value)
                return token

            hooks.update(late=late)
        dx, grads, sg[i] = _layer_bwd(dx, kept[i], i, weights[i], rp, mats, tabs, tie, hooks)
        if pending is not None:
            landings = scatter_done(held["chips"], landings, dx)
        if i > 0:
            pending, tie = scatter_pair(str(i), all_kinds, grads, dx)
        else:
            pending, _ = scatter_pair("0m", mixer_kinds, [grads[j] for j in mixer_kinds], dx)

    sgrads, conv_grad = _replicated_grads(sg, sp)

    def pack_small(t, final, conv):
        flat = [t[name].reshape(DEPTH, n) for name, n in SMALL]
        flat = jnp.concatenate([jnp.concatenate(flat, axis=1).reshape(-1), final.reshape(-1), conv.reshape(-1)])
        return jnp.pad(flat, (0, SMALL_ROWS * 128 - flat.shape[0])).reshape(SMALL_ROWS, 128)

    small_src = [pack_small(sgrads, d_norm_final, conv_grad).astype(BF16)]
    last, tie = scatter_chips(pending, landings0, 0, small_src[0])
    s_sems, r_sems, arrays, tie = _split_start(
        "gather_small_chips_start", small_src + place_own(small_src), 4, _plan_gather_chips, tie)
    small_state = ("small", s_sems, r_sems, arrays)

    big_out = []
    for j, (name, _, _) in enumerate(KINDS):
        big_out.append(_adamw(landings[j], travel["w"][j], travel["m"][j], travel["v"][j], rows_tile[name],
                              "adamw_late_" + name, groups=(1, DEPTH), tie=tie))
        tie = big_out[-1][3]
    landings0 = scatter_done(held["ffn_chips"], landings0, tie)
    landings0 = scatter_done(last, landings0, tie)
    small_state, _ = gather_pass(small_state, landings0[0])
    big_out = [_adamw(landings0[j], travel["w"][j], travel["m"][j], travel["v"][j], rows_tile[name],
                      "adamw_first_" + name, groups=(0, 1), fill=big_out[j]) for j, (name, _, _) in enumerate(KINDS)]
    big_res = [_native_layout([big_out[j][kind] for j in range(len(KINDS))]) for kind in range(4)]

    zeros_conv = jnp.zeros((CONV_N,), F32)
    _, (sparts,) = gather_done(small_state, big_out[-1][0], ())
    sw, sm_, sv = (pack_small(small[s], finals[s], zeros_conv) for s in "wmv")
    small_out = _adamw(sparts[None], sw[None], sm_[None], sv[None], SMALL_ROWS // 8, "adamw_replicated")

    def unpack_small(p):
        flat = p.reshape(-1)
        per = flat[:DEPTH * SMALL_PER_LAYER].reshape(DEPTH, SMALL_PER_LAYER)
        out, off = {}, 0
        for name, n in SMALL:
            out[name] = per[:, off:off + n].reshape(small["w"][name].shape)
            off += n
        out["norm_final"] = flat[DEPTH * SMALL_PER_LAYER:DEPTH * SMALL_PER_LAYER + D_MODEL]
        return out

    small_res = [unpack_small(p) for p in small_out]

    conv_off = DEPTH * SMALL_PER_LAYER + D_MODEL
    conv_parts = sparts.reshape(N_DEV, -1)[:, conv_off:conv_off + CONV_N].reshape(N_DEV, DEPTH * 3, WIDTH)
    conv_parts = lax.dynamic_slice_in_dim(conv_parts, mine * 64, 64, axis=2)
    conv_res = _adamw(conv_parts[None], *(convs[s].reshape(1, DEPTH * 3, 64) for s in "wmv"), DEPTH * 3, "adamw_conv_w")

    order = ["norm_mix", "w_in", "b_gate", "attn_sinks", "w_attn_o", "conv_w", "w_conv_o", "ssm_a_re", "ssm_a_im",
             "ssm_b_re", "ssm_b_im", "ssm_c_re", "ssm_c_im", "ssm_d", "ssm_log_dt", "w_ssm_glu", "w_ssm_o",
             "w_mix_o", "norm_ffn", "w_ffn_in", "w_ffn_out", "norm_final"]
    outs = [loss, dx[None]]
    for kind in range(4):
        for name in order:
            if name == "conv_w":
                outs.append(conv_res[kind].reshape(DEPTH, 3, 64))
            elif name in big_res[kind]:
                outs.append(big_res[kind][name])
            else:
                outs.append(small_res[kind][name])
    return tuple(outs)
```

```python
import functools
import math

import jax
import jax.numpy as jnp
from jax import lax
from jax.experimental import pallas as pl
from jax.experimental.pallas import tpu as pltpu

F32 = jnp.float32
BF16 = jnp.bfloat16

N_DEV = 8
DEPTH = 4
SEQ = 2048
D_MODEL = 1024
N_Q_HEADS = 8
HEAD_DIM = 64
ATTN_W = 512
KV_W = 128
BLOCK = 128
N_BLOCKS = SEQ // BLOCK
ROPE_THETA = 500000.0
ROT_DIM = 16
NEG_INF = -1e30
WIDTH = 512
SSM_GROUPS = 32
SSM_GROUP = 16
SSM_STATE = 64
SLABS = 16
CHUNK = 256
N_CHUNKS = SEQ // CHUNK
GATE_W = 3 * D_MODEL
IN_COLS = 5888
FFN_H = 2816
NORM_EPS = 1e-6
LR, B1, B2, ADAM_EPS, WD, STEP = 0.001, 0.9, 0.999, 1e-08, 0.01, 10

COL_Q, COL_KV, COL_CBX, COL_U, COL_G = 0, 512, 768, 2304, 2816
PIECE_W = (512, 256, 512, 512, 512, 512, 3072)
PIECE_OFF = tuple(sum(PIECE_W[:i]) for i in range(len(PIECE_W)))

KINDS = (("win_t", 736, 1024), ("wffn_t", 704, 1024), ("wout", 352, 1024), ("wmix", 128, 1024),
         ("branch_t", 128, 1536), ("wglu", 64, 512))

SMALL = (("norm_mix", 1024), ("b_gate", 3072), ("attn_sinks", 8), ("ssm_a_re", 2048), ("ssm_a_im", 2048),
         ("ssm_b_re", 32768), ("ssm_b_im", 32768), ("ssm_c_re", 32768), ("ssm_c_im", 32768),
         ("ssm_d", 512), ("ssm_log_dt", 32), ("norm_ffn", 1024))
SMALL_PER_LAYER = sum(n for _, n in SMALL)
CONV_N = DEPTH * 3 * WIDTH
SMALL_ROWS = 4480

VMEM_LIMIT = 56 * 1024 * 1024
NT = (((1,), (1,)), ((), ()))
TN = (((0,), (0,)), ((), ()))
MESH_ID = pl.DeviceIdType.MESH
ANY = pl.BlockSpec(memory_space=pl.ANY)
HBM = pl.BlockSpec(memory_space=pltpu.HBM)
SEM = pl.BlockSpec(memory_space=pltpu.SEMAPHORE)
EFFECT = pltpu.SideEffectType.DATAFLOW_SIDE_EFFECTING


def _cp(**kw):
    return pltpu.CompilerParams(vmem_limit_bytes=VMEM_LIMIT, **kw)


def _full(shape):
    return pl.BlockSpec(shape, lambda *_: (0,) * len(shape))


def _resident(shape):
    return pl.BlockSpec(shape, lambda *_: (0,) * len(shape), pipeline_mode=pl.Buffered(1))


def _mm_tn(a, b, *, tm, tn, name):
    k, m = a.shape
    n = b.shape[1]

    def body(a_ref, b_ref, o_ref):
        o_ref[...] = lax.dot_general(a_ref[...].astype(BF16), b_ref[...].astype(BF16), TN,
                                     preferred_element_type=F32).astype(BF16)

    return pl.pallas_call(
        body, grid=(m // tm, n // tn),
        in_specs=[pl.BlockSpec((k, tm), lambda i, j: (0, i)), pl.BlockSpec((k, tn), lambda i, j: (0, j))],
        out_specs=pl.BlockSpec((tm, tn), lambda i, j: (i, j)),
        out_shape=jax.ShapeDtypeStruct((m, n), BF16), compiler_params=_cp(), name=name)(a, b)


def _rms_rows(xv, g):
    r = lax.rsqrt(jnp.mean(xv * xv, axis=-1, keepdims=True) + NORM_EPS)
    return ((xv * r) * g).astype(BF16)


def _rms_mm_in(x, g, wt, tie):
    tt = 512
    widths = (ATTN_W, 2 * KV_W, 3 * WIDTH, WIDTH, GATE_W)
    offs = (COL_Q, COL_KV, COL_CBX, COL_U, COL_G)

    def body(x_ref, g_ref, w_ref, tie_ref, q_ref, kv_ref, cbx_ref, u_ref, gl_ref, h_ref):
        h = _rms_rows(x_ref[...], g_ref[...])
        h_ref[...] = h
        prod = lax.dot_general(h, w_ref[...], NT, preferred_element_type=F32)
        for ref, o, w in zip((q_ref, kv_ref, cbx_ref, u_ref, gl_ref), offs, widths):
            ref[...] = prod[:, o:o + w]

    row = lambda w: pl.BlockSpec((tt, w), lambda i: (i, 0))
    sds = jax.ShapeDtypeStruct
    return pl.pallas_call(
        body, grid=(SEQ // tt,), in_specs=[row(D_MODEL), _full((1, D_MODEL)), _resident((IN_COLS, D_MODEL)), ANY],
        out_specs=[row(ATTN_W), row(2 * KV_W), row(3 * WIDTH), row(WIDTH), row(GATE_W), row(D_MODEL)],
        out_shape=[sds((SEQ, ATTN_W), F32), sds((SEQ, 2 * KV_W), F32), sds((SEQ, 3 * WIDTH), F32),
                   sds((SEQ, WIDTH), F32), sds((SEQ, GATE_W), F32), sds((SEQ, D_MODEL), BF16)],
        compiler_params=_cp(), name="rms_mm_in")(x, g, wt, tie)


def _rms_mm_ffn(x, g, wt):
    tt = 512

    def body(x_ref, g_ref, w_ref, o_ref, h_ref):
        h = _rms_rows(x_ref[...], g_ref[...])
        h_ref[...] = h
        o_ref[...] = lax.dot_general(h, w_ref[...], NT, preferred_element_type=F32)

    row = lambda w: pl.BlockSpec((tt, w), lambda i: (i, 0))
    return pl.pallas_call(
        body, grid=(SEQ // tt,), in_specs=[row(D_MODEL), _full((1, D_MODEL)), _resident((2 * FFN_H, D_MODEL))],
        out_specs=[row(2 * FFN_H), row(D_MODEL)],
        out_shape=[jax.ShapeDtypeStruct((SEQ, 2 * FFN_H), F32), jax.ShapeDtypeStruct((SEQ, D_MODEL), BF16)],
        compiler_params=_cp(), name="rms_mm_ffn")(x, g, wt)


def _mm_rmsbwd(pieces, wt, x, g, dres, name):
    tt = 512
    widths = [p.shape[1] for p in pieces]
    offs = [sum(widths[:i]) for i in range(len(widths))]
    n = len(pieces)

    def body(*refs):
        p_refs, (w_ref, x_ref, g_ref, r_ref, dx_ref, dg_ref) = refs[:n], refs[n:]

        @pl.when(pl.program_id(0) == 0)
        def _():
            dg_ref[...] = jnp.zeros_like(dg_ref)

        dh = jnp.zeros((tt, D_MODEL), F32)
        for p_ref, o, w in zip(p_refs, offs, widths):
            dh += jnp.dot(p_ref[...], w_ref[o:o + w, :], preferred_element_type=F32)
        xv = x_ref[...]
        r = lax.rsqrt(jnp.mean(xv * xv, axis=-1, keepdims=True) + NORM_EPS)
        xh = xv * r
        gy = dh * g_ref[...]
        dx_ref[...] = r_ref[...] + r * (gy - xh * jnp.mean(gy * xh, axis=-1, keepdims=True))
        dg_ref[...] += jnp.sum(dh * xh, axis=0, keepdims=True)

    row = lambda w: pl.BlockSpec((tt, w), lambda i: (i, 0))
    return pl.pallas_call(
        body, grid=(SEQ // tt,),
        in_specs=[row(w) for w in widths] + [_resident(wt.shape), row(D_MODEL), _full((1, D_MODEL)), row(D_MODEL)],
        out_specs=[row(D_MODEL), _full((1, D_MODEL))],
        out_shape=[jax.ShapeDtypeStruct((SEQ, D_MODEL), F32), jax.ShapeDtypeStruct((1, D_MODEL), F32)],
        compiler_params=_cp(), name=name)(*pieces, wt, x, g, dres)


def _tn_pieces(pieces, h):
    tk, tn = 512, 512
    nk = SEQ // tk
    n = len(pieces)

    def body(*refs):
        p_refs, (h_ref, o_ref, acc_ref) = refs[:n], refs[n:]
        kk = pl.program_id(1)

        @pl.when(kk == 0)
        def _():
            acc_ref[...] = jnp.zeros_like(acc_ref)

        hv = h_ref[...]
        for p_ref, o, w in zip(p_refs, PIECE_OFF, PIECE_W):
            acc_ref[o:o + w, :] += lax.dot_general(p_ref[...], hv, TN, preferred_element_type=F32)

        @pl.when(kk == nk - 1)
        def _():
            o_ref[...] = acc_ref[...].astype(BF16)

    return pl.pallas_call(
        body, grid=(D_MODEL // tn, nk),
        in_specs=[pl.BlockSpec((tk, w), lambda j, kk: (kk, 0)) for w in PIECE_W]
        + [pl.BlockSpec((tk, tn), lambda j, kk: (kk, j))],
        out_specs=pl.BlockSpec((IN_COLS, tn), lambda j, kk: (0, j)),
        out_shape=jax.ShapeDtypeStruct((IN_COLS, D_MODEL), BF16),
        scratch_shapes=[pltpu.VMEM((IN_COLS, tn), F32)], compiler_params=_cp(), name="tn_pieces")(*pieces, h)


def _tn_branches(dys, acts):
    tk = 512
    nk = SEQ // tk

    def body(d0, d1, d2, a0, a1, a2, o_ref, acc_ref):
        kk = pl.program_id(0)

        @pl.when(kk == 0)
        def _():
            acc_ref[...] = jnp.zeros_like(acc_ref)

        for j, (d, a) in enumerate(((d0, a0), (d1, a1), (d2, a2))):
            acc_ref[:, WIDTH * j:WIDTH * (j + 1)] += lax.dot_general(d[...], a[...], TN, preferred_element_type=F32)

        @pl.when(kk == nk - 1)
        def _():
            o_ref[...] = acc_ref[...].astype(BF16)

    row = lambda w: pl.BlockSpec((tk, w), lambda kk: (kk, 0))
    return pl.pallas_call(
        body, grid=(nk,), in_specs=[row(D_MODEL)] * 3 + [row(WIDTH)] * 3,
        out_specs=_full((D_MODEL, 3 * WIDTH)), out_shape=jax.ShapeDtypeStruct((D_MODEL, 3 * WIDTH), BF16),
        scratch_shapes=[pltpu.VMEM((D_MODEL, 3 * WIDTH), F32)], compiler_params=_cp(), name="tn_branches",
    )(*dys, *acts)


def _rope(t, c, a, b):
    return t * c + pltpu.roll(t, 120, axis=1) * a + pltpu.roll(t, 8, axis=1) * b


def _rope_t(d, c, a, b):
    return d * c + pltpu.roll(d * a, 8, axis=1) + pltpu.roll(d * b, 120, axis=1)


def _band_sides(band):
    left = lax.broadcasted_iota(jnp.int32, band.shape, 1) < HEAD_DIM
    h0 = jnp.where(left, band, 0.0)
    h1 = jnp.where(left, 0.0, band)
    r0 = pltpu.roll(h0, HEAD_DIM, axis=1)
    r1 = pltpu.roll(h1, HEAD_DIM, axis=1)
    return ((h0.astype(BF16), r0.astype(BF16)), (r1.astype(BF16), h1.astype(BF16)))


def _attn_mask(i):
    qi = lax.broadcasted_iota(jnp.int32, (2 * BLOCK, 2 * BLOCK), 0) % BLOCK
    kj = lax.broadcasted_iota(jnp.int32, (2 * BLOCK, 2 * BLOCK), 1)
    delta = qi + BLOCK - kj
    return (delta >= 0) & (delta < BLOCK) & ((kj >= BLOCK) | (i > 0))


def _attn_probs(s, ok, sink):
    s = jnp.where(ok, s * (HEAD_DIM ** -0.5), NEG_INF)
    m = jnp.maximum(jnp.max(s, axis=-1, keepdims=True), sink)
    p = jnp.exp(s - m)
    es = jnp.exp(sink - m)
    inv = 1.0 / (jnp.sum(p, axis=-1, keepdims=True) + es)
    return p * inv, es * inv


def _kv_group(qs, ks, vs, kh, sink_ref):
    q2 = jnp.concatenate([qs[2 * kh], qs[2 * kh + 1]], axis=0)
    kst = jnp.concatenate([ks[kh][0], ks[kh][1]], axis=0)
    vst = jnp.concatenate([vs[kh][0], vs[kh][1]], axis=0)
    top = lax.broadcasted_iota(jnp.int32, (2 * BLOCK, 1), 0) < BLOCK
    sinks = [jnp.where(top, sink_ref[0, 4 * kh + h], sink_ref[0, 4 * kh + 2 + h]) for h in range(2)]
    return q2, kst, vst, sinks


def _attn_load(q_ref, kvc_ref, kvp_ref, tc_ref, ta_ref, tb_ref, pc_ref, pa_ref, pb_ref):
    c, a, b = tc_ref[...], ta_ref[...], tb_ref[...]
    kc = _rope(kvc_ref[:, :KV_W], c, a, b)
    kp = _rope(kvp_ref[:, :KV_W], pc_ref[...], pa_ref[...], pb_ref[...])
    kband = jnp.concatenate([kp, kc], axis=0)
    vband = jnp.concatenate([kvp_ref[:, KV_W:], kvc_ref[:, KV_W:]], axis=0)
    qs = [_rope(q_ref[:, 128 * j:128 * (j + 1)], c, a, b).astype(BF16) for j in range(4)]
    return qs, _band_sides(kband), _band_sides(vband), (c, a, b)


def _attn_specs(clamp):
    cur = lambda i: (clamp(i), 0)
    prev = lambda i: (jnp.maximum(clamp(i) - 1, 0), 0)
    return [
        pl.BlockSpec((BLOCK, ATTN_W), cur), pl.BlockSpec((BLOCK, 2 * KV_W), cur),
        pl.BlockSpec((BLOCK, 2 * KV_W), prev),
        pl.BlockSpec((BLOCK, 128), cur), pl.BlockSpec((BLOCK, 128), cur), pl.BlockSpec((BLOCK, 128), cur),
        pl.BlockSpec((BLOCK, 128), prev), pl.BlockSpec((BLOCK, 128), prev), pl.BlockSpec((BLOCK, 128), prev),
        pl.BlockSpec(memory_space=pltpu.SMEM),
    ]


def _attn_fwd(q, kv, tabs, sinks):
    tc, ta, tb = tabs

    def body(q_ref, kvc_ref, kvp_ref, tc_ref, ta_ref, tb_ref, pc_ref, pa_ref, pb_ref, sink_ref, o_ref):
        i = pl.program_id(0)
        qs, ks, vs, _ = _attn_load(q_ref, kvc_ref, kvp_ref, tc_ref, ta_ref, tb_ref, pc_ref, pa_ref, pb_ref)
        ok = _attn_mask(i)
        for kh in range(2):
            q2, kst, vst, sinks = _kv_group(qs, ks, vs, kh, sink_ref)
            s = lax.dot_general(q2, kst, NT, preferred_element_type=F32)
            pn = [_attn_probs(s[:, 2 * BLOCK * h:2 * BLOCK * (h + 1)], ok, sinks[h])[0].astype(BF16) for h in range(2)]
            o2 = jnp.dot(jnp.concatenate(pn, axis=1), vst, preferred_element_type=F32).astype(BF16)
            for r in range(2):
                j = 2 * kh + r
                o_ref[:, 128 * j:128 * (j + 1)] = o2[BLOCK * r:BLOCK * (r + 1)]

    return pl.pallas_call(
        body, grid=(N_BLOCKS,), in_specs=_attn_specs(lambda i: i),
        out_specs=pl.BlockSpec((BLOCK, ATTN_W), lambda i: (i, 0)),
        out_shape=jax.ShapeDtypeStruct((SEQ, ATTN_W), BF16), compiler_params=_cp(), name="attn_fwd",
    )(q, kv, kv, tc, ta, tb, tc, ta, tb, sinks)


def _attn_bwd(q, kv, tabs, sinks, do):
    tc, ta, tb = tabs
    last = N_BLOCKS - 1
    clamp = lambda i: jnp.minimum(i, last)

    def place(full, side, kh):
        left = lax.broadcasted_iota(jnp.int32, full.shape, 1) < HEAD_DIM
        valid = jnp.where(left, full, 0.0) if side == 0 else jnp.where(left, 0.0, full)
        return valid if side == kh else pltpu.roll(valid, HEAD_DIM, axis=1)

    def body(q_ref, kvc_ref, kvp_ref, tc_ref, ta_ref, tb_ref, pc_ref, pa_ref, pb_ref, sink_ref, do_ref,
             dq_ref, dkv_ref, ds_ref, carry_ref):
        i = pl.program_id(0)

        @pl.when(i == 0)
        def _():
            ds_ref[...] = jnp.zeros_like(ds_ref)
            carry_ref[...] = jnp.zeros_like(carry_ref)

        @pl.when(i > last)
        def _():
            dkv_ref[...] = carry_ref[...].astype(BF16)

        @pl.when(i <= last)
        def _():
            qs, ks, vs, (c, a, b) = _attn_load(q_ref, kvc_ref, kvp_ref, tc_ref, ta_ref, tb_ref,
                                               pc_ref, pa_ref, pb_ref)
            ok = _attn_mask(i)
            dk = jnp.zeros((2 * BLOCK, 128), F32)
            dv = jnp.zeros((2 * BLOCK, 128), F32)
            dsink = jnp.zeros((1, 128), F32)
            lane = lax.broadcasted_iota(jnp.int32, (1, 128), 1)
            for kh in range(2):
                q2, kst, vst, sinks = _kv_group(qs, ks, vs, kh, sink_ref)
                do2 = jnp.concatenate([do_ref[:, 128 * (2 * kh + r):128 * (2 * kh + r + 1)] for r in range(2)],
                                      axis=0).astype(BF16)
                s = lax.dot_general(q2, kst, NT, preferred_element_type=F32)
                dp = lax.dot_general(do2, vst, NT, preferred_element_type=F32)
                pns, dss = [], []
                for h in range(2):
                    cols = slice(2 * BLOCK * h, 2 * BLOCK * (h + 1))
                    pn, ps = _attn_probs(s[:, cols], ok, sinks[h])
                    dr = jnp.sum(pn * dp[:, cols], axis=-1, keepdims=True)
                    pns.append(pn.astype(BF16))
                    dss.append((pn * (dp[:, cols] - dr) * (HEAD_DIM ** -0.5)).astype(BF16))
                    for r in range(2):
                        part = -jnp.sum((ps * dr)[BLOCK * r:BLOCK * (r + 1)])
                        dsink += jnp.where(lane == 4 * kh + 2 * r + h, part, 0.0)
                ds2, pn2 = jnp.concatenate(dss, axis=1), jnp.concatenate(pns, axis=1)
                dq2 = jnp.dot(ds2, kst, preferred_element_type=F32)
                dk2 = lax.dot_general(ds2, q2, TN, preferred_element_type=F32)
                dv2 = lax.dot_general(pn2, do2, TN, preferred_element_type=F32)
                for h in range(2):
                    dk += place(dk2[2 * BLOCK * h:2 * BLOCK * (h + 1)], h, kh)
                    dv += place(dv2[2 * BLOCK * h:2 * BLOCK * (h + 1)], h, kh)
                for r in range(2):
                    j = 2 * kh + r
                    dq_ref[:, 128 * j:128 * (j + 1)] = _rope_t(dq2[BLOCK * r:BLOCK * (r + 1)], c, a, b).astype(BF16)
            ds_ref[...] += dsink
            dk_prev = _rope_t(dk[:BLOCK], pc_ref[...], pa_ref[...], pb_ref[...])
            dk_cur = _rope_t(dk[BLOCK:], c, a, b)
            prev = jnp.concatenate([dk_prev, dv[:BLOCK]], axis=1)
            dkv_ref[...] = (carry_ref[...] + prev).astype(BF16)
            carry_ref[...] = jnp.concatenate([dk_cur, dv[BLOCK:]], axis=1)

    return pl.pallas_call(
        body, grid=(N_BLOCKS + 1,),
        in_specs=_attn_specs(clamp) + [pl.BlockSpec((BLOCK, ATTN_W), lambda i: (clamp(i), 0))],
        out_specs=[pl.BlockSpec((BLOCK, ATTN_W), lambda i: (clamp(i), 0)),
                   pl.BlockSpec((BLOCK, 2 * KV_W), lambda i: (jnp.maximum(i - 1, 0), 0)),
                   pl.BlockSpec((1, 128), lambda i: (0, 0))],
        out_shape=[jax.ShapeDtypeStruct((SEQ, ATTN_W), BF16), jax.ShapeDtypeStruct((SEQ, 2 * KV_W), BF16),
                   jax.ShapeDtypeStruct((1, 128), F32)],
        scratch_shapes=[pltpu.VMEM((BLOCK, 2 * KV_W), F32)], compiler_params=_cp(), name="attn_bwd",
    )(q, kv, kv, tc, ta, tb, tc, ta, tb, sinks, do)


def _shift_down(z, k):
    row = lax.broadcasted_iota(jnp.int32, z.shape, 0)
    return jnp.where(row < k, 0.0, pltpu.roll(z, k, axis=0))


def _shift_up(z, k):
    n = z.shape[0]
    row = lax.broadcasted_iota(jnp.int32, z.shape, 0)
    return jnp.where(row >= n - k, 0.0, pltpu.roll(z, n - k, axis=0))


def _conv_specs():
    nb = WIDTH // 128
    return [pl.BlockSpec((SEQ, 128), lambda j: (0, j)), pl.BlockSpec((SEQ, 128), lambda j: (0, nb + j)),
            pl.BlockSpec((SEQ, 128), lambda j: (0, 2 * nb + j)), pl.BlockSpec((None, 8, 128), lambda j: (0, 0, j))]


def _conv_fwd(cbx, cw, layer):
    def body(cb_ref, cc_ref, cx_ref, w_ref, o_ref):
        z = cc_ref[...] * cx_ref[...]
        s = w_ref[0:1, :] * _shift_down(z, 2) + w_ref[1:2, :] * _shift_down(z, 1) + w_ref[2:3, :] * z
        o_ref[...] = (cb_ref[...] * s).astype(BF16)

    specs = _conv_specs()
    specs[3] = pl.BlockSpec((None, 8, 128), lambda j: (layer, 0, j))
    return pl.pallas_call(
        body, grid=(WIDTH // 128,), in_specs=specs,
        out_specs=pl.BlockSpec((SEQ, 128), lambda j: (0, j)),
        out_shape=jax.ShapeDtypeStruct((SEQ, WIDTH), BF16), compiler_params=_cp(), name="conv_fwd",
    )(cbx, cbx, cbx, cw)


def _conv_bwd(cbx, cw, layer, dout, tie):
    def body(cb_ref, cc_ref, cx_ref, w_ref, do_ref, tie_ref, dcb_ref, dcc_ref, dcx_ref, dw_ref):
        cc, cx = cc_ref[...], cx_ref[...]
        z = cc * cx
        z1, z2 = _shift_down(z, 1), _shift_down(z, 2)
        w0, w1, w2 = w_ref[0:1, :], w_ref[1:2, :], w_ref[2:3, :]
        dout = do_ref[...]
        ds = dout * cb_ref[...]
        dcb_ref[...] = (dout * (w0 * z2 + w1 * z1 + w2 * z)).astype(BF16)
        dz = w2 * ds + w1 * _shift_up(ds, 1) + w0 * _shift_up(ds, 2)
        dcc_ref[...] = (dz * cx).astype(BF16)
        dcx_ref[...] = (dz * cc).astype(BF16)
        rows = [jnp.sum(ds * zz, axis=0, keepdims=True) for zz in (z2, z1, z)]
        dw_ref[...] = jnp.concatenate(rows + [jnp.zeros((5, 128), F32)], axis=0)

    col = lambda j: (0, j)
    specs = _conv_specs()
    specs[3] = pl.BlockSpec((None, 8, 128), lambda j: (layer, 0, j))
    return pl.pallas_call(
        body, grid=(WIDTH // 128,), in_specs=specs + [pl.BlockSpec((SEQ, 128), col), ANY],
        out_specs=[pl.BlockSpec((SEQ, 128), col), pl.BlockSpec((SEQ, 128), col), pl.BlockSpec((SEQ, 128), col),
                   pl.BlockSpec((8, 128), col)],
        out_shape=[jax.ShapeDtypeStruct((SEQ, WIDTH), BF16)] * 3 + [jax.ShapeDtypeStruct((8, WIDTH), F32)],
        compiler_params=_cp(), name="conv_bwd",
    )(cbx, cbx, cbx, cw, dout, tie)


def _ssm_prep_math(a_re, a_im, log_dt, bt_re, bt_im):
    dt = jnp.exp(log_dt)
    er = jnp.exp(a_re * dt)
    lr = er * jnp.cos(a_im * dt)
    li = er * jnp.sin(a_im * dt)
    n2 = a_re * a_re + a_im * a_im
    cr = ((lr - 1.0) * a_re + li * a_im) / n2
    ci = (li * a_re - (lr - 1.0) * a_im) / n2
    cr3, ci3 = cr[:, None, :], ci[:, None, :]
    return lr, li, cr3 * bt_re - ci3 * bt_im, cr3 * bt_im + ci3 * bt_re


_GS = (SSM_GROUPS, SSM_STATE)
_GHS = (SSM_GROUPS, SSM_GROUP, SSM_STATE)


def _layered(shape):
    return pl.BlockSpec((None,) + shape, lambda l: (l,) + (0,) * len(shape))


def _ssm_prep(a_re, a_im, log_dt, bt_re, bt_im):
    def body(ar, ai, ld, br, bi, o0, o1, o2, o3):
        outs = _ssm_prep_math(ar[...], ai[...], ld[...], br[...], bi[...])
        for o, v in zip((o0, o1, o2, o3), outs):
            o[...] = v

    shapes = [_GS, _GS, _GHS, _GHS]
    return pl.pallas_call(
        body, grid=(DEPTH,), in_specs=[_layered(s) for s in (_GS, _GS, (SSM_GROUPS, 1), _GHS, _GHS)],
        out_specs=[_layered(s) for s in shapes],
        out_shape=[jax.ShapeDtypeStruct((DEPTH,) + s, F32) for s in shapes],
        name="ssm_prep")(a_re, a_im, log_dt, bt_re, bt_im)


def _ssm_prep_bwd(a_re, a_im, log_dt, bt_re, bt_im, cots):
    def body(ar, ai, ld, br, bi, c0, c1, c2, c3, o0, o1, o2, o3, o4):
        _, vjp = jax.vjp(_ssm_prep_math, ar[...], ai[...], ld[...], br[...], bi[...])
        for o, v in zip((o0, o1, o2, o3, o4), vjp((c0[...], c1[...], c2[...], c3[...]))):
            o[...] = v

    ins = (_GS, _GS, (SSM_GROUPS, 1), _GHS, _GHS)
    return pl.pallas_call(
        body, grid=(DEPTH,), in_specs=[_layered(s) for s in ins + (_GS, _GS, _GHS, _GHS)],
        out_specs=[_layered(s) for s in ins],
        out_shape=[jax.ShapeDtypeStruct((DEPTH,) + s, F32) for s in ins],
        name="ssm_prep_bwd")(a_re, a_im, log_dt, bt_re, bt_im, *cots)


LANES_G = 512
N_LANE_GROUPS = SSM_GROUPS * SSM_STATE // LANES_G


def _scan_order(a):
    return a.reshape(N_CHUNKS, CHUNK, -1).transpose(1, 0, 2).reshape(a.shape)


def _time_order(a):
    return a.reshape(CHUNK, N_CHUNKS, -1).transpose(1, 0, 2).reshape(a.shape)


def _scan_in_place(xr_ref, xi_ref, ar, ai, reverse):
    shape = (N_CHUNKS, xr_ref.shape[1])
    ar, ai = jnp.broadcast_to(ar, shape), jnp.broadcast_to(ai, shape)

    def rows(tau):
        t = (CHUNK - 1 - tau) if reverse else tau
        return pl.ds(pl.multiple_of(t * N_CHUNKS, N_CHUNKS), N_CHUNKS)

    def step(tau, carry):
        sr, si = carry
        return ar * sr - ai * si + xr_ref[rows(tau), :], ar * si + ai * sr + xi_ref[rows(tau), :]

    zero = jnp.zeros(shape, F32)
    er, ei = lax.fori_loop(0, CHUNK, step, (zero, zero), unroll=8)
    qr, qi = ar, ai
    for _ in range(8):
        qr, qi = qr * qr - qi * qi, 2.0 * qr * qi
    shift = _shift_up if reverse else _shift_down
    for k in (1, 2, 4):
        sr, si = shift(er, k), shift(ei, k)
        er, ei = er + qr * sr - qi * si, ei + qr * si + qi * sr
        qr, qi = qr * qr - qi * qi, 2.0 * qr * qi
    start = (shift(er, 1), shift(ei, 1))

    def write(tau, carry):
        sr, si = step(tau, carry)
        xr_ref[rows(tau), :] = sr
        xi_ref[rows(tau), :] = si
        return sr, si

    return write, start


def _ssm_specs(layer):
    col = lambda w: pl.BlockSpec((SEQ, w), lambda g: (0, g))
    diag = pl.BlockSpec((None, None, 128, LANES_G), lambda g: (layer, g, 0, 0))
    vec = pl.BlockSpec((None, 1, LANES_G), lambda g: (layer, 0, g))
    return col, diag, vec


def _ssm_fwd(u, mats, layer, d):
    def body(u_ref, d_ref, br_ref, bi_ref, cr_ref, ci_ref, ar_ref, ai_ref, xr_ref, xi_ref, y_ref):
        uv = u_ref[...].astype(BF16)
        xr_ref[...] = jnp.dot(uv, br_ref[...], preferred_element_type=F32)
        xi_ref[...] = jnp.dot(uv, bi_ref[...], preferred_element_type=F32)
        write, start = _scan_in_place(xr_ref, xi_ref, ar_ref[...], ai_ref[...], False)
        lax.fori_loop(0, CHUNK, write, start, unroll=8)
        y = lax.dot_general(xr_ref[...].astype(BF16), cr_ref[...], NT, preferred_element_type=F32)
        y += lax.dot_general(xi_ref[...].astype(BF16), ci_ref[...], NT, preferred_element_type=F32)
        y_ref[...] = y + d_ref[...] * u_ref[...]

    col, diag, vec = _ssm_specs(layer)
    return pl.pallas_call(
        body, grid=(N_LANE_GROUPS,),
        in_specs=[col(128), pl.BlockSpec((None, 1, 128), lambda g: (layer, 0, g)),
                  diag, diag, diag, diag, vec, vec],
        out_specs=[col(LANES_G), col(LANES_G), col(128)],
        out_shape=[jax.ShapeDtypeStruct((SEQ, SSM_GROUPS * SSM_STATE), F32)] * 2
        + [jax.ShapeDtypeStruct((SEQ, WIDTH), F32)],
        compiler_params=_cp(), name="ssm_fwd",
    )(u, d, mats["b_re"], mats["b_im"], mats["c_re"], mats["c_im_neg"], mats["a_re"], mats["a_im"])


def _ssm_bwd(dy16, x_re, x_im, u, mats, layer, d):
    def body(dy_ref, u_ref, d_ref, xr_ref, xi_ref, br_ref, bi_ref, cr_ref, ci_ref, ar_ref, ai_ref,
             du_ref, dar_ref, dai_ref, dbr_ref, dbi_ref, dcr_ref, dci_ref, lr_ref, li_ref):
        dy = dy_ref[...]
        lr_ref[...] = jnp.dot(dy, cr_ref[...], preferred_element_type=F32)
        li_ref[...] = jnp.dot(dy, ci_ref[...], preferred_element_type=F32)
        write, start = _scan_in_place(lr_ref, li_ref, ar_ref[...], -ai_ref[...], True)

        def rows(t):
            return pl.ds(pl.multiple_of(t * N_CHUNKS, N_CHUNKS), N_CHUNKS)

        def grad(acc, lam, xpr, xpi):
            return acc[0] + xpr * lam[0] + xpi * lam[1], acc[1] + xpr * lam[1] - xpi * lam[0]

        def down(tau, carry):
            lam = write(tau, carry[0])
            t = CHUNK - 2 - tau
            return lam, grad(carry[1], lam, xr_ref[rows(t), :], xi_ref[rows(t), :])

        zero = jnp.zeros((N_CHUNKS, LANES_G), F32)
        lam, acc = lax.fori_loop(0, CHUNK - 1, down, (start, (zero, zero)), unroll=5)
        lam = write(CHUNK - 1, lam)
        last = rows(CHUNK - 1)
        acc = grad(acc, lam, _shift_down(xr_ref[last, :], 1), _shift_down(xi_ref[last, :], 1))
        dar_ref[...] = jnp.sum(acc[0], axis=0, keepdims=True)
        dai_ref[...] = jnp.sum(acc[1], axis=0, keepdims=True)

        l_re, l_im = lr_ref[...].astype(BF16), li_ref[...].astype(BF16)
        du = lax.dot_general(l_re, br_ref[...], NT, preferred_element_type=F32)
        du += lax.dot_general(l_im, bi_ref[...], NT, preferred_element_type=F32)
        du_ref[...] = (du + dy.astype(F32) * d_ref[...]).astype(BF16)
        uv = u_ref[...].astype(BF16)
        dbr_ref[...] = lax.dot_general(uv, l_re, TN, preferred_element_type=F32)
        dbi_ref[...] = lax.dot_general(uv, l_im, TN, preferred_element_type=F32)
        dcr_ref[...] = lax.dot_general(dy, xr_ref[...].astype(BF16), TN, preferred_element_type=F32)
        dci_ref[...] = lax.dot_general(dy, xi_ref[...].astype(BF16), TN, preferred_element_type=F32)

    col, diag, vec = _ssm_specs(layer)
    out_vec = pl.BlockSpec((1, LANES_G), lambda g: (0, g))
    out_blk = pl.BlockSpec((None, 128, LANES_G), lambda g: (g, 0, 0))
    sds = jax.ShapeDtypeStruct
    return pl.pallas_call(
        body, grid=(N_LANE_GROUPS,),
        in_specs=[col(128), col(128), pl.BlockSpec((None, 1, 128), lambda g: (layer, 0, g)),
                  col(LANES_G), col(LANES_G), diag, diag, diag, diag, vec, vec],
        out_specs=[col(128), out_vec, out_vec, out_blk, out_blk, out_blk, out_blk],
        out_shape=[sds((SEQ, WIDTH), BF16)] + [sds((1, SSM_GROUPS * SSM_STATE), F32)] * 2
        + [sds((N_LANE_GROUPS, 128, LANES_G), F32)] * 4,
        scratch_shapes=[pltpu.VMEM((SEQ, LANES_G), F32)] * 2, compiler_params=_cp(), name="ssm_bwd",
    )(dy16, u, d, x_re, x_im, mats["b_re"], mats["b_im"], mats["c_re"], mats["c_im_neg"],
      mats["a_re"], mats["a_im"])


_GELU_C = math.sqrt(2.0 / math.pi)


def _gelu(y):
    return 0.5 * y * (1.0 + jnp.tanh(_GELU_C * (y + 0.044715 * (y * y * y))))


def _glu_fwd(y, wglu):
    tt = 512

    def body(y_ref, w_ref, z_ref):
        ys = _gelu(y_ref[...])
        a = jnp.dot(ys.astype(BF16), w_ref[...], preferred_element_type=F32)
        z_ref[...] = (ys * jax.nn.sigmoid(a)).astype(BF16)

    blk = pl.BlockSpec((tt, WIDTH), lambda i: (i, 0))
    return pl.pallas_call(body, grid=(SEQ // tt,), in_specs=[blk, _full((WIDTH, WIDTH))], out_specs=blk,
                          out_shape=jax.ShapeDtypeStruct((SEQ, WIDTH), BF16), compiler_params=_cp(),
                          name="glu_fwd")(y, wglu)


def _glu_bwd(y, wglu, dz, u):
    tt = 512

    def body(y_ref, w_ref, dz_ref, u_ref, dy_ref, ys_ref, da_ref, dd_ref):
        @pl.when(pl.program_id(0) == 0)
        def _():
            dd_ref[...] = jnp.zeros_like(dd_ref)

        yv = y_ref[...]
        t = jnp.tanh(_GELU_C * (yv + 0.044715 * (yv * yv * yv)))
        ys = 0.5 * yv * (1.0 + t)
        ysb = ys.astype(BF16)
        sg = jax.nn.sigmoid(jnp.dot(ysb, w_ref[...], preferred_element_type=F32))
        dz = dz_ref[...].astype(F32)
        da = (dz * ys * sg * (1.0 - sg)).astype(BF16)
        dys = dz * sg + lax.dot_general(da, w_ref[...], NT, preferred_element_type=F32)
        dy = dys * (0.5 * (1.0 + t) + 0.5 * yv * (1.0 - t * t) * _GELU_C * (1.0 + 3 * 0.044715 * (yv * yv)))
        dy_ref[...] = dy.astype(BF16)
        ys_ref[...] = ysb
        da_ref[...] = da
        dd_ref[...] += jnp.sum(dy * u_ref[...], axis=0, keepdims=True)

    blk = pl.BlockSpec((tt, WIDTH), lambda i: (i, 0))
    return pl.pallas_call(
        body, grid=(SEQ // tt,), in_specs=[blk, _full((WIDTH, WIDTH)), blk, blk],
        out_specs=[blk, blk, blk, _full((1, WIDTH))],
        out_shape=[jax.ShapeDtypeStruct((SEQ, WIDTH), BF16)] * 3 + [jax.ShapeDtypeStruct((1, WIDTH), F32)],
        compiler_params=_cp(), name="glu_bwd")(y, wglu, dz, u)


def _mix_specs(tt, layer):
    row = lambda w: pl.BlockSpec((tt, w), lambda i: (i, 0))
    gate = lambda j: pl.BlockSpec((tt, D_MODEL), lambda i: (i, j))
    wo = lambda j: pl.BlockSpec((D_MODEL, WIDTH), lambda i: (0, j))
    return [row(D_MODEL), row(WIDTH), row(WIDTH), row(WIDTH), gate(0), gate(1), gate(2),
            pl.BlockSpec((None, 1, GATE_W), lambda i: (layer, 0, 0)), wo(0), wo(1), wo(2),
            _full((D_MODEL, D_MODEL))]


def _mix_branches(o_ref, c_ref, z_ref, g_refs, b_ref, wa_ref, wc_ref, ws_ref):
    ys = [lax.dot_general(r[...], w[...], NT, preferred_element_type=F32)
          for r, w in ((o_ref, wa_ref), (c_ref, wc_ref), (z_ref, ws_ref))]
    gates = [jax.nn.sigmoid(g_refs[j][...] + b_ref[:, D_MODEL * j:D_MODEL * (j + 1)]) for j in range(3)]
    return ys, gates


def _mix_fwd(x, o, cv, z, glog, b_gate, layer, wbt, wmix, tie):
    tt = 256

    def body(x_ref, o_ref, c_ref, z_ref, g0, g1, g2, b_ref, wa_ref, wc_ref, ws_ref, wm_ref, tie_ref, x1_ref):
        ys, gates = _mix_branches(o_ref, c_ref, z_ref, (g0, g1, g2), b_ref, wa_ref, wc_ref, ws_ref)
        merged = gates[0] * ys[0] + gates[1] * ys[1] + gates[2] * ys[2]
        x1_ref[...] = x_ref[...] + jnp.dot(merged.astype(BF16), wm_ref[...], preferred_element_type=F32)

    return pl.pallas_call(
        body, grid=(SEQ // tt,), in_specs=_mix_specs(tt, layer) + [ANY],
        out_specs=pl.BlockSpec((tt, D_MODEL), lambda i: (i, 0)),
        out_shape=jax.ShapeDtypeStruct((SEQ, D_MODEL), F32), compiler_params=_cp(), name="mix_fwd",
    )(x, o, cv, z, glog, glog, glog, b_gate, wbt, wbt, wbt, wmix, tie)


def _mix_bwd(dx1, o, cv, z, glog, b_gate, layer, wbt, wmix, tie):
    tt = 256

    def body(dx_ref, o_ref, c_ref, z_ref, g0, g1, g2, b_ref, wa_ref, wc_ref, ws_ref, wm_ref, tie_ref,
             mg_ref, dya_ref, dyc_ref, dys_ref, do_ref, dc_ref, dz_ref, dgl_ref, db_ref):
        @pl.when(pl.program_id(0) == 0)
        def _():
            db_ref[...] = jnp.zeros_like(db_ref)

        ys, gates = _mix_branches(o_ref, c_ref, z_ref, (g0, g1, g2), b_ref, wa_ref, wc_ref, ws_ref)
        mg_ref[...] = (gates[0] * ys[0] + gates[1] * ys[1] + gates[2] * ys[2]).astype(BF16)
        dm = lax.dot_general(dx_ref[...].astype(BF16), wm_ref[...], NT, preferred_element_type=F32)
        for j, (dy_ref, w_ref, d_ref) in enumerate(((dya_ref, wa_ref, do_ref), (dyc_ref, wc_ref, dc_ref),
                                                    (dys_ref, ws_ref, dz_ref))):
            dy = (dm * gates[j]).astype(BF16)
            dy_ref[...] = dy
            d_ref[...] = jnp.dot(dy, w_ref[...], preferred_element_type=F32)
            dgl = dm * ys[j] * gates[j] * (1.0 - gates[j])
            dgl_ref[:, D_MODEL * j:D_MODEL * (j + 1)] = dgl.astype(BF16)
            db_ref[:, D_MODEL * j:D_MODEL * (j + 1)] += jnp.sum(dgl, axis=0, keepdims=True)

    row = lambda w: pl.BlockSpec((tt, w), lambda i: (i, 0))
    sds = jax.ShapeDtypeStruct
    return pl.pallas_call(
        body, grid=(SEQ // tt,), in_specs=_mix_specs(tt, layer) + [ANY],
        out_specs=[row(D_MODEL)] * 4 + [row(WIDTH)] * 3 + [row(GATE_W), _full((1, GATE_W))],
        out_shape=[sds((SEQ, D_MODEL), BF16)] * 4 + [sds((SEQ, WIDTH), F32)] * 3
        + [sds((SEQ, GATE_W), BF16), sds((1, GATE_W), F32)],
        compiler_params=_cp(), name="mix_bwd",
    )(dx1, o, cv, z, glog, glog, glog, b_gate, wbt, wbt, wbt, wmix, tie)


def _ffn_out_fwd(x1, gu, wout, tie):
    tt = 256

    def body(x_ref, gt_ref, up_ref, w_ref, tie_ref, o_ref):
        gt = gt_ref[...]
        act = (gt * jax.nn.sigmoid(gt) * up_ref[...]).astype(BF16)
        o_ref[...] = x_ref[...] + jnp.dot(act, w_ref[...], preferred_element_type=F32)

    return pl.pallas_call(
        body, grid=(SEQ // tt,),
        in_specs=[pl.BlockSpec((tt, D_MODEL), lambda i: (i, 0)), pl.BlockSpec((tt, FFN_H), lambda i: (i, 0)),
                  pl.BlockSpec((tt, FFN_H), lambda i: (i, 1)), _full((FFN_H, D_MODEL)), ANY],
        out_specs=pl.BlockSpec((tt, D_MODEL), lambda i: (i, 0)),
        out_shape=jax.ShapeDtypeStruct((SEQ, D_MODEL), F32), compiler_params=_cp(), name="ffn_out_fwd",
    )(x1, gu, gu, wout, tie)


def _ffn_out_bwd(dx2, gu, wout, tie):
    tt = 256

    def body(dx_ref, gt_ref, up_ref, w_ref, tie_ref, dgu_ref, act_ref):
        gt, up = gt_ref[...], up_ref[...]
        sg = jax.nn.sigmoid(gt)
        silu = gt * sg
        act_ref[...] = (silu * up).astype(BF16)
        dact = lax.dot_general(dx_ref[...].astype(BF16), w_ref[...], NT, preferred_element_type=F32)
        dgu_ref[:, :FFN_H] = (dact * up * (sg * (1.0 + gt * (1.0 - sg)))).astype(BF16)
        dgu_ref[:, FFN_H:] = (dact * silu).astype(BF16)

    return pl.pallas_call(
        body, grid=(SEQ // tt,),
        in_specs=[pl.BlockSpec((tt, D_MODEL), lambda i: (i, 0)), pl.BlockSpec((tt, FFN_H), lambda i: (i, 0)),
                  pl.BlockSpec((tt, FFN_H), lambda i: (i, 1)), _full((FFN_H, D_MODEL)), ANY],
        out_specs=[pl.BlockSpec((tt, 2 * FFN_H), lambda i: (i, 0)), pl.BlockSpec((tt, FFN_H), lambda i: (i, 0))],
        out_shape=[jax.ShapeDtypeStruct((SEQ, 2 * FFN_H), BF16), jax.ShapeDtypeStruct((SEQ, FFN_H), BF16)],
        compiler_params=_cp(), name="ffn_out_bwd",
    )(dx2, gu, gu, wout, tie)


def _loss_head(x, g, target):
    tt = 256

    def body(x_ref, g_ref, t_ref, loss_ref, dx_ref, dg_ref):
        @pl.when(pl.program_id(0) == 0)
        def _():
            loss_ref[...] = jnp.zeros_like(loss_ref)
            dg_ref[...] = jnp.zeros_like(dg_ref)

        xv = x_ref[...]
        r = lax.rsqrt(jnp.mean(xv * xv, axis=-1, keepdims=True) + NORM_EPS)
        xh = xv * r
        err = xh * g_ref[...] - t_ref[...]
        loss_ref[...] += 0.5 * jnp.sum(jnp.mean(err * err, axis=-1, keepdims=True))
        dy = err * (1.0 / D_MODEL)
        gy = dy * g_ref[...]
        dx_ref[...] = r * (gy - xh * jnp.mean(gy * xh, axis=-1, keepdims=True))
        dg_ref[...] += jnp.sum(dy * xh, axis=0, keepdims=True)

    row = pl.BlockSpec((tt, D_MODEL), lambda i: (i, 0))
    return pl.pallas_call(
        body, grid=(SEQ // tt,), in_specs=[row, _full((1, D_MODEL)), row],
        out_specs=[_full((1, 128)), row, _full((1, D_MODEL))],
        out_shape=[jax.ShapeDtypeStruct((1, 128), F32), jax.ShapeDtypeStruct((SEQ, D_MODEL), F32),
                   jax.ShapeDtypeStruct((1, D_MODEL), F32)],
        compiler_params=_cp(), name="loss_head")(x, g, target)


def _adam_math(g, w, m, v):
    nm = B1 * m + (1.0 - B1) * g
    nv = B2 * v + (1.0 - B2) * (g * g)
    m_hat = nm / (1.0 - B1 ** STEP)
    v_hat = nv / (1.0 - B2 ** STEP)
    return -LR * (m_hat / (jnp.sqrt(v_hat) + ADAM_EPS) + WD * w), nm, nv


def _adamw_small(parts, w, m, v, name):
    def body(p_ref, w_ref, m_ref, v_ref, g_ref, d_ref, nm_ref, nv_ref):
        g = p_ref[0].astype(F32)
        for k in range(1, N_DEV):
            g = g + p_ref[k].astype(F32)
        g_ref[...] = g
        d_ref[...], nm_ref[...], nv_ref[...] = _adam_math(g, w_ref[...], m_ref[...], v_ref[...])

    out_shape = [jax.ShapeDtypeStruct(w.shape, F32)] * 4
    if w.ndim < 3:
        return pl.pallas_call(body, out_shape=out_shape, name=name)(parts, w, m, v)
    rest = w.shape[1:]
    zeros = (0,) * len(rest)
    blk = pl.BlockSpec((None,) + rest, lambda l: (l,) + zeros)
    return pl.pallas_call(
        body, grid=(w.shape[0],),
        in_specs=[pl.BlockSpec((N_DEV, None) + rest, lambda l: (0, l) + zeros), blk, blk, blk],
        out_specs=[blk] * 4, out_shape=out_shape, name=name)(parts, w, m, v)


def _adamw(parts, w, m, v, tr, name, groups=None, fill=None, tie=None):
    n_groups, rows, cols = w.shape
    n_parts = parts.shape[1]
    lo, hi = groups if groups is not None else (0, n_groups)

    def body(p_ref, w_ref, m_ref, v_ref, *rest):
        g_ref, d_ref, nm_ref, nv_ref = rest[-4:]
        g = p_ref[0].astype(F32)
        for k in range(1, n_parts):
            g = g + p_ref[k].astype(F32)
        nm = B1 * m_ref[...] + (1.0 - B1) * g
        nv = B2 * v_ref[...] + (1.0 - B2) * (g * g)
        m_hat = nm / (1.0 - B1 ** STEP)
        v_hat = nv / (1.0 - B2 ** STEP)
        g_ref[...] = g
        d_ref[...] = -LR * (m_hat / (jnp.sqrt(v_hat) + ADAM_EPS) + WD * w_ref[...])
        nm_ref[...] = nm
        nv_ref[...] = nv

    blk = pl.BlockSpec((None, tr, cols), lambda l, i: (l + lo, i, 0))
    p_lo = lo if parts.shape[0] == n_groups else 0
    extra = ([] if fill is None else list(fill)) + ([] if tie is None else [tie])
    return pl.pallas_call(
        body, grid=(hi - lo, rows // tr),
        in_specs=[pl.BlockSpec((None, n_parts, tr, cols), lambda l, i: (l + p_lo, 0, i, 0)), blk, blk, blk]
        + [ANY] * len(extra),
        out_specs=[blk] * 4, out_shape=[jax.ShapeDtypeStruct((n_groups, rows, cols), F32)] * 4,
        input_output_aliases={} if fill is None else {4 + j: j for j in range(4)},
        compiler_params=_cp(), name=name)(parts, w, m, v, *extra)


def _split_start(name, arrays, n_sems, plan, after=None):
    n = len(arrays)
    order = [] if after is None else [after]
    n_in = n + len(order)

    def body(*refs):
        ins, send_sems, recv_sems, token = refs[:n], refs[n_in], refs[n_in + 1], refs[-1]
        for src, dst, k, to in plan(ins)[0]:
            pltpu.make_async_remote_copy(src_ref=src, dst_ref=dst, send_sem=send_sems.at[k], recv_sem=recv_sems.at[k],
                                         device_id=to, device_id_type=MESH_ID).start()
        token[...] = jnp.zeros_like(token)

    outs = pl.pallas_call(
        body, name=name,
        out_shape=(pltpu.SemaphoreType.DMA((n_sems,)), pltpu.SemaphoreType.DMA((n_sems,)),
                   *[pltpu.HBM(a.shape, a.dtype) for a in arrays], jax.ShapeDtypeStruct((8, 128), F32)),
        in_specs=[HBM] * n + [ANY] * len(order),
        out_specs=(SEM, SEM, *[HBM] * n, pl.BlockSpec(memory_space=pltpu.VMEM)),
        input_output_aliases={i: 2 + i for i in range(n)},
        compiler_params=pltpu.CompilerParams(has_side_effects=EFFECT),
    )(*[pltpu.with_memory_space_constraint(a, pltpu.HBM) for a in arrays], *order)
    return outs[0], outs[1], list(outs[2:2 + n]), outs[-1]


def _split_wait(name, arrays, send_sems, recv_sems, after, plan):
    n = len(arrays)

    def body(*refs):
        ins, s_sems, r_sems = refs[:n], refs[n], refs[n + 1]
        sends, arrivals = plan(ins)
        x, y, c = lax.axis_index("x"), lax.axis_index("y"), lax.axis_index("c")
        for src, dst, k, to in sends:
            pltpu.make_async_remote_copy(src_ref=src, dst_ref=dst, send_sem=s_sems.at[k], recv_sem=r_sems.at[k],
                                         device_id=to, device_id_type=MESH_ID).wait_send()
        for dst, k in arrivals:
            pltpu.make_async_remote_copy(src_ref=dst, dst_ref=dst, send_sem=s_sems.at[k], recv_sem=r_sems.at[k],
                                         device_id=(x, y, c), device_id_type=MESH_ID).wait_recv()

    return pl.pallas_call(
        body, name=name, out_shape=[pltpu.HBM(a.shape, a.dtype) for a in arrays],
        in_specs=[HBM] * n + [SEM, SEM, ANY], out_specs=[HBM] * n,
        input_output_aliases={i: i for i in range(n)},
        compiler_params=pltpu.CompilerParams(has_side_effects=EFFECT),
    )(*arrays, send_sems, recv_sems, after)


def _chips():
    x, y, c = lax.axis_index("x"), lax.axis_index("y"), lax.axis_index("c")
    return x, y, c, [(1 - x, y), (x, 1 - y), (1 - x, 1 - y)]


def _plan_gather_chips(refs):
    x, y, c, chips = _chips()
    me = 4 * x + 2 * y + c
    n = len(refs) // 2
    sends, arrivals = [], []
    for i in range(n):
        src, land = refs[i], refs[n + i]
        sends.append((src, land.at[me], 4 * i, (x, y, 1 - c)))
        arrivals.append((land.at[4 * x + 2 * y + 1 - c], 4 * i))
        for j, (px, py) in enumerate(chips):
            sends.append((src, land.at[me], 4 * i + 1 + j, (px, py, c)))
            arrivals.append((land.at[4 * px + 2 * py + c], 4 * i + 1 + j))
    return sends, arrivals


def _plan_gather_pass(refs):
    x, y, c, chips = _chips()
    sends, arrivals = [], []
    for i in range(len(refs)):
        for j, (px, py) in enumerate(chips):
            slot = refs[i].at[4 * px + 2 * py + c]
            sends.append((slot, slot, 4 * i + j, (x, y, 1 - c)))
            arrivals.append((refs[i].at[4 * px + 2 * py + 1 - c], 4 * i + j))
        back = refs[i].at[4 * x + 2 * y + 1 - c]
        sends.append((back, back, 4 * i + 3, (x, y, 1 - c)))
        arrivals.append((refs[i].at[4 * x + 2 * y + c], 4 * i + 3))
    return sends, arrivals


def _plan_scatter_pair(refs):
    x, y, c = lax.axis_index("x"), lax.axis_index("y"), lax.axis_index("c")
    n = len(refs) // 2
    sends, arrivals = [], []
    for i in range(n):
        for q in range(4):
            sends.append((refs[i].at[q, 1 - c], refs[n + i].at[q], 4 * i + q, (x, y, 1 - c)))
            arrivals.append((refs[n + i].at[q], 4 * i + q))
    return sends, arrivals


def _plan_scatter_chips(layer):
    def plan(refs):
        x, y, c, chips = _chips()
        n = len(refs) // 2
        sends, arrivals = [], []
        for i in range(n):
            for j, (px, py) in enumerate(chips):
                sends.append((refs[i].at[2 * px + py], refs[n + i].at[layer, 2 * x + y], 3 * i + j, (px, py, c)))
                arrivals.append((refs[n + i].at[layer, 2 * px + py], 3 * i + j))
        return sends, arrivals

    return plan


def _pair_sum(parts4, from_pair, landing, layer, core, tr, name):
    _, _, rows, cols = parts4.shape

    def body(c_ref, p_ref, s_ref, l_ref, sum_ref, land_ref):
        v = (p_ref[...].astype(F32) + s_ref[...].astype(F32)).astype(BF16)
        sum_ref[...] = v
        land_ref[...] = v

    blk = pl.BlockSpec((None, tr, cols), lambda q, i, c_ref: (q, i, 0))
    return pl.pallas_call(
        body,
        grid_spec=pltpu.PrefetchScalarGridSpec(
            num_scalar_prefetch=1, grid=(4, rows // tr),
            in_specs=[pl.BlockSpec((None, None, tr, cols), lambda q, i, c_ref: (q, c_ref[0], i, 0)), blk, ANY],
            out_specs=[blk, pl.BlockSpec((None, None, tr, cols), lambda q, i, c_ref: (layer, q, i, 0))]),
        out_shape=[jax.ShapeDtypeStruct((4, rows, cols), BF16), jax.ShapeDtypeStruct(landing.shape, BF16)],
        input_output_aliases={3: 1}, compiler_params=_cp(), name=name,
    )(core, parts4, from_pair, landing)


def _travel_layout(t):
    tr = lambda a: jnp.swapaxes(a, 1, 2)
    branch = jnp.concatenate([tr(t["w_attn_o"]), tr(t["w_conv_o"]), tr(t["w_ssm_o"])], axis=2)
    return [tr(t["w_in"]), tr(t["w_ffn_in"]), t["w_ffn_out"], t["w_mix_o"], branch, t["w_ssm_glu"]]


def _native_layout(a):
    tr = lambda x: jnp.swapaxes(x, 1, 2)
    b = a[4]
    return {"w_in": tr(a[0]), "w_ffn_in": tr(a[1]), "w_ffn_out": a[2], "w_mix_o": a[3],
            "w_attn_o": tr(b[:, :, :WIDTH]), "w_conv_o": tr(b[:, :, WIDTH:2 * WIDTH]),
            "w_ssm_o": tr(b[:, :, 2 * WIDTH:]), "w_ssm_glu": a[5]}


def _embed(t):
    eye = jnp.eye(8, dtype=t.dtype)
    t = t.reshape(DEPTH, N_LANE_GROUPS, 8, SSM_GROUP, SSM_STATE)
    return (t[:, :, :, :, None, :] * eye[None, None, :, None, :, None]).reshape(DEPTH, N_LANE_GROUPS, 128, LANES_G)


def _diag_blocks(t):
    t = t.reshape(DEPTH, N_LANE_GROUPS, 8, SSM_GROUP, 8, SSM_STATE)
    return jnp.einsum("lgahap->lgahp", t).reshape(DEPTH, SSM_GROUPS, SSM_GROUP, SSM_STATE)


def _rope_tabs():
    pos = jnp.arange(SEQ, dtype=F32)
    inv_freq = ROPE_THETA ** (-jnp.arange(0, ROT_DIM, 2, dtype=F32) / ROT_DIM)
    ang = pos[:, None] * inv_freq[None, :]
    cos, sin = jnp.cos(ang), jnp.sin(ang)
    one, zero = jnp.ones((SEQ, HEAD_DIM - ROT_DIM), F32), jnp.zeros((SEQ, HEAD_DIM - ROT_DIM), F32)
    z8 = jnp.zeros((SEQ, 8), F32)
    head = lambda *p: jnp.tile(jnp.concatenate(p, axis=1), (1, 2))
    return head(cos, cos, one), head(-sin, z8, zero), head(z8, sin, zero)


def _ssm_mats(sp):
    lr, li, bbr, bbi = _ssm_prep(sp["a_re"], sp["a_im"], sp["log_dt"], sp["bt_re"], sp["bt_im"])
    lanes = SSM_GROUPS * SSM_STATE
    return {
        "a_re": lr.reshape(DEPTH, 1, lanes), "a_im": li.reshape(DEPTH, 1, lanes),
        "b_re": _embed(bbr).astype(BF16), "b_im": _embed(bbi).astype(BF16),
        "c_re": _embed(sp["c_re"]).astype(BF16), "c_im_neg": _embed(-sp["c_im"]).astype(BF16),
    }


def _layer_fwd(x, i, w, rp, mats, tabs, tie, hooks):
    q, kv, cbx, u, glog, h = _rms_mm_in(x, rp["norm_mix"][i], w["win_t"], tie)
    o = _attn_fwd(q, kv, tabs, rp["attn_sinks"][i])
    cv = _conv_fwd(cbx, rp["conv_w"], i)
    u = _scan_order(u)
    x_re, x_im, y = _ssm_fwd(u, mats, i, rp["ssm_d"])
    z = _time_order(_glu_fwd(y, w["wglu"]))
    x1 = _mix_fwd(x, o, cv, z, glog, rp["b_gate"], i, w["branch_t"], w["wmix"], hooks["early"](z))
    hooks["pre_ffn"](x1)
    gu, h2 = _rms_mm_ffn(x1, rp["norm_ffn"][i], w["wffn_t"])
    x2 = _ffn_out_fwd(x1, gu, w["wout"], hooks["mid"](h2))
    kept = dict(x=x, q=q, kv=kv, cbx=cbx, u=u, glog=glog, h=h, o=o, cv=cv, z=z, y=y,
                x_re=x_re, x_im=x_im, x1=x1, gu=gu, h2=h2)
    return x2, kept


def _layer_bwd(dx2, k, i, w, rp, mats, tabs, tie, hooks):
    dgu, act = _ffn_out_bwd(dx2, k["gu"], w["wout"], tie)
    g_wout = _mm_tn(act, dx2, tm=FFN_H // 2, tn=1024, name="mm_tn_ffn_out")
    g_wffn_t = _mm_tn(dgu, k["h2"], tm=FFN_H // 2, tn=1024, name="mm_tn_ffn_in")
    dx1, d_norm_ffn = _mm_rmsbwd([dgu], w["wffn_t"], k["x1"], rp["norm_ffn"][i], dx2, "mm_rmsbwd_ffn")

    mg, dya, dyc, dys, do, dcv, dz, dgl, db_gate = _mix_bwd(
        dx1, k["o"], k["cv"], k["z"], k["glog"], rp["b_gate"], i, w["branch_t"], w["wmix"],
        hooks["mid"]((g_wffn_t, g_wout, d_norm_ffn)))
    g_wmix = _mm_tn(mg, dx1, tm=1024, tn=512, name="mm_tn_mix")
    g_branch_t = _tn_branches((dya, dyc, dys), (k["o"], k["cv"], k["z"]))

    dy16, ys16, da16, dd = _glu_bwd(k["y"], w["wglu"], _scan_order(dz), k["u"])
    g_wglu = _mm_tn(ys16, da16, tm=256, tn=512, name="mm_tn_glu")
    du, da_re, da_im, db_re, db_im, dc_re, dc_im = _ssm_bwd(dy16, k["x_re"], k["x_im"], k["u"], mats, i,
                                                             rp["ssm_d"])
    du = _time_order(du)

    dcb, dcc, dcx, d_conv_w = _conv_bwd(k["cbx"], rp["conv_w"], i, dcv, hooks["late"](du))
    dq, dkv, d_sinks = _attn_bwd(k["q"], k["kv"], tabs, rp["attn_sinks"][i], do)

    pieces = [dq, dkv, dcb, dcc, dcx, du, dgl]
    g_win_t = _tn_pieces(pieces, k["h"])
    dx, d_norm_mix = _mm_rmsbwd(pieces, w["win_t"], k["x"], rp["norm_mix"][i], dx1, "mm_rmsbwd_in")

    grads = [g_win_t, g_wffn_t, g_wout, g_wmix, g_branch_t, g_wglu]
    small = dict(norm_mix=d_norm_mix, b_gate=db_gate, attn_sinks=d_sinks, ssm_d=dd, norm_ffn=d_norm_ffn,
                 conv_w=d_conv_w, da_re=da_re, da_im=da_im, db_re=db_re, db_im=db_im, dc_re=dc_re, dc_im=dc_im)
    return dx, grads, small


def _replicated_grads(sg, sp):
    stack = lambda name: jnp.stack([sg[i][name] for i in range(DEPTH)])
    cots = (stack("da_re").reshape(DEPTH, *_GS), stack("da_im").reshape(DEPTH, *_GS),
            _diag_blocks(stack("db_re")), _diag_blocks(stack("db_im")))
    d_a_re, d_a_im, d_log_dt, d_bt_re, d_bt_im = _ssm_prep_bwd(
        sp["a_re"], sp["a_im"], sp["log_dt"], sp["bt_re"], sp["bt_im"], cots)
    sgrads = {"norm_mix": stack("norm_mix"), "b_gate": stack("b_gate"),
              "attn_sinks": stack("attn_sinks")[:, :, :N_Q_HEADS], "ssm_a_re": d_a_re, "ssm_a_im": d_a_im,
              "ssm_b_re": jnp.swapaxes(d_bt_re, 2, 3), "ssm_b_im": jnp.swapaxes(d_bt_im, 2, 3),
              "ssm_c_re": _diag_blocks(stack("dc_re")), "ssm_c_im": -_diag_blocks(stack("dc_im")),
              "ssm_d": stack("ssm_d"), "ssm_log_dt": d_log_dt, "norm_ffn": stack("norm_ffn")}
    return sgrads, stack("conv_w")[:, :3]


def kernel(x, norm_mix, w_in, b_gate, attn_sinks, w_attn_o, conv_w, w_conv_o, ssm_a_re, ssm_a_im, ssm_b_re, ssm_b_im, ssm_c_re, ssm_c_im, ssm_d, ssm_log_dt, w_ssm_glu, w_ssm_o, w_mix_o, norm_ffn, w_ffn_in, w_ffn_out, norm_final, loss_target, m_norm_mix, m_w_in, m_b_gate, m_attn_sinks, m_w_attn_o, m_conv_w, m_w_conv_o, m_ssm_a_re, m_ssm_a_im, m_ssm_b_re, m_ssm_b_im, m_ssm_c_re, m_ssm_c_im, m_ssm_d, m_ssm_log_dt, m_w_ssm_glu, m_w_ssm_o, m_w_mix_o, m_norm_ffn, m_w_ffn_in, m_w_ffn_out, m_norm_final, v_norm_mix, v_w_in, v_b_gate, v_attn_sinks, v_w_attn_o, v_conv_w, v_w_conv_o, v_ssm_a_re, v_ssm_a_im, v_ssm_b_re, v_ssm_b_im, v_ssm_c_re, v_ssm_c_im, v_ssm_d, v_ssm_log_dt, v_w_ssm_glu, v_w_ssm_o, v_w_mix_o, v_norm_ffn, v_w_ffn_in, v_w_ffn_out, v_norm_final):
    big = {"w": dict(w_in=w_in, w_attn_o=w_attn_o, w_conv_o=w_conv_o, w_ssm_glu=w_ssm_glu, w_ssm_o=w_ssm_o,
                     w_mix_o=w_mix_o, w_ffn_in=w_ffn_in, w_ffn_out=w_ffn_out),
           "m": dict(w_in=m_w_in, w_attn_o=m_w_attn_o, w_conv_o=m_w_conv_o, w_ssm_glu=m_w_ssm_glu,
                     w_ssm_o=m_w_ssm_o, w_mix_o=m_w_mix_o, w_ffn_in=m_w_ffn_in, w_ffn_out=m_w_ffn_out),
           "v": dict(w_in=v_w_in, w_attn_o=v_w_attn_o, w_conv_o=v_w_conv_o, w_ssm_glu=v_w_ssm_glu,
                     w_ssm_o=v_w_ssm_o, w_mix_o=v_w_mix_o, w_ffn_in=v_w_ffn_in, w_ffn_out=v_w_ffn_out)}
    small = {"w": dict(norm_mix=norm_mix, b_gate=b_gate, attn_sinks=attn_sinks, ssm_a_re=ssm_a_re,
                       ssm_a_im=ssm_a_im, ssm_b_re=ssm_b_re, ssm_b_im=ssm_b_im, ssm_c_re=ssm_c_re,
                       ssm_c_im=ssm_c_im, ssm_d=ssm_d, ssm_log_dt=ssm_log_dt, norm_ffn=norm_ffn),
             "m": dict(norm_mix=m_norm_mix, b_gate=m_b_gate, attn_sinks=m_attn_sinks, ssm_a_re=m_ssm_a_re,
                       ssm_a_im=m_ssm_a_im, ssm_b_re=m_ssm_b_re, ssm_b_im=m_ssm_b_im, ssm_c_re=m_ssm_c_re,
                       ssm_c_im=m_ssm_c_im, ssm_d=m_ssm_d, ssm_log_dt=m_ssm_log_dt, norm_ffn=m_norm_ffn),
             "v": dict(norm_mix=v_norm_mix, b_gate=v_b_gate, attn_sinks=v_attn_sinks, ssm_a_re=v_ssm_a_re,
                       ssm_a_im=v_ssm_a_im, ssm_b_re=v_ssm_b_re, ssm_b_im=v_ssm_b_im, ssm_c_re=v_ssm_c_re,
                       ssm_c_im=v_ssm_c_im, ssm_d=v_ssm_d, ssm_log_dt=v_ssm_log_dt, norm_ffn=v_norm_ffn)}
    finals = {"w": norm_final, "m": m_norm_final, "v": v_norm_final}
    small_out_shapes = {name: a.shape for name, a in small["w"].items()}
    small_out_shapes.update(norm_final=(D_MODEL,), conv_w=(DEPTH, 3, 64))
    small_shapes = dict(small_out_shapes, norm_final=(1, D_MODEL), conv_w=(DEPTH, 3, WIDTH))
    convs = {"w": conv_w, "m": m_conv_w, "v": v_conv_w}
    mine = 4 * lax.axis_index("x") + 2 * lax.axis_index("y") + lax.axis_index("c")

    travel = {s: _travel_layout(big[s]) for s in "wmv"}
    stacked16 = [a.astype(BF16) for a in travel["w"]]
    rp = {"norm_mix": norm_mix[:, None], "norm_ffn": norm_ffn[:, None], "attn_sinks": attn_sinks[:, None],
          "b_gate": b_gate[:, None], "ssm_d": ssm_d[:, None]}
    sp = {"a_re": ssm_a_re, "a_im": ssm_a_im, "log_dt": ssm_log_dt[:, :, None],
          "bt_re": jnp.swapaxes(ssm_b_re, 2, 3), "bt_im": jnp.swapaxes(ssm_b_im, 2, 3),
          "c_re": ssm_c_re, "c_im": ssm_c_im}
    rows_tile = {"win_t": 368, "wffn_t": 352, "wout": 176, "wmix": 128, "branch_t": 128, "wglu": 64}
    core = lax.axis_index("c").astype(jnp.int32).reshape(1)
    no_tie = jnp.zeros((8, 128), F32)

    def place_own(srcs):
        return [lax.empty((N_DEV,) + s.shape, s.dtype) for s in srcs]

    def gather_chips(tag, i, kinds, after, extra=()):
        srcs = [stacked16[j][i] for j in kinds] + list(extra)
        s_sems, r_sems, arrays, token = _split_start(
            f"gather_chips_start_{tag}", srcs + place_own(srcs), 4 * len(srcs), _plan_gather_chips, after)
        return (tag, s_sems, r_sems, arrays), token

    def gather_pass(state, after):
        tag, s_sems, r_sems, arrays = state
        arrays = _split_wait(f"gather_chips_wait_{tag}", arrays, s_sems, r_sems, after, _plan_gather_chips)
        n = len(arrays) // 2
        s_sems, r_sems, lands, token = _split_start(
            f"gather_pass_start_{tag}", list(arrays[n:]), 4 * n, _plan_gather_pass)
        return (tag, s_sems, r_sems, lands), token

    def gather_done(state, after, kinds):
        tag, s_sems, r_sems, lands = state
        lands = _split_wait(f"gather_pass_wait_{tag}", lands, s_sems, r_sems, after, _plan_gather_pass)
        named = {KINDS[j][0]: a.reshape(N_DEV * KINDS[j][1], KINDS[j][2]) for a, j in zip(lands, kinds)}
        return named, list(lands[len(kinds):])

    all_kinds, mixer_kinds, ffn_kinds = tuple(range(len(KINDS))), (0, 3, 4, 5), (1, 2)
    no_hooks = {name: (lambda value: no_tie) for name in ("early", "pre_ffn", "mid", "late")}
    state, token = gather_chips("0m", 0, mixer_kinds, None, extra=[jnp.pad(conv_w.reshape(6, 128), ((0, 2), (0, 0)))])
    mats = _ssm_mats(dict(sp, log_dt=sp["log_dt"] + token[0, 0]))
    tabs = _rope_tabs()
    ready = sum(a.reshape(-1)[:1].astype(F32) for a in list(mats.values()) + list(tabs))
    state, _ = gather_pass(state, ready)
    ffn_state, tie = gather_chips("0f", 0, ffn_kinds, state[3][0])
    w_next, (conv_all,) = gather_done(state, tabs[2], mixer_kinds)
    conv_full = conv_all[:, :6].reshape(N_DEV, DEPTH, 3, 64).transpose(1, 2, 0, 3).reshape(DEPTH, 3, WIDTH)
    rp["conv_w"] = jnp.pad(conv_full, ((0, 0), (0, 5), (0, 0)))

    act = x[0]
    weights, kept = [], []
    for i in range(DEPTH):
        w_i, hooks, held = w_next, dict(no_hooks), {}

        def early(value, ffn_state=ffn_state, held=held):
            held["ffn"], token = gather_pass(ffn_state, value)
            return token

        def pre_ffn(value, w_i=w_i, held=held):
            w_i.update(gather_done(held["ffn"], value, ffn_kinds)[0])

        hooks.update(early=early, pre_ffn=pre_ffn)
        if i + 1 < DEPTH:
            state, tie = gather_chips(f"{i + 1}m", i + 1, mixer_kinds, tie if i == 0 else w_i["win_t"])

            def mid(value, i=i, state=state, held=held):
                held["next"], token = gather_pass(state, value)
                held["next_ffn"], token = gather_chips(f"{i + 1}f", i + 1, ffn_kinds, token)
                return token

            hooks.update(mid=mid)
        act, k = _layer_fwd(act, i, w_i, rp, mats, tabs, tie, hooks)
        if i + 1 < DEPTH:
            w_next, _ = gather_done(held["next"], act, mixer_kinds)
            ffn_state, tie = held["next_ffn"], no_tie
        weights.append(w_i)
        kept.append(k)
    loss_row, dx, d_norm_final = _loss_head(act, norm_final[None], loss_target[0])
    loss = lax.psum(loss_row[0, 0], ("x", "y", "c"))

    landings = [lax.empty((DEPTH, 4, r, c), BF16) for _, r, c in KINDS]
    landings0 = [lax.empty((1, 4, r, c), BF16) for _, r, c in KINDS]

    def scatter_pair(tag, kinds, grads, after):
        parts4 = [g.reshape(4, 2, KINDS[j][1], KINDS[j][2]) for g, j in zip(grads, kinds)]
        zones = [lax.empty((4, KINDS[j][1], KINDS[j][2]), BF16) for j in kinds]
        s_sems, r_sems, arrays, token = _split_start(
            f"scatter_pair_start_{tag}", parts4 + zones, 4 * len(kinds), _plan_scatter_pair, after)
        return (tag, kinds, s_sems, r_sems, arrays), token

    def scatter_chips(state, lands, slot, after):
        tag, kinds, s_sems, r_sems, arrays = state
        arrays = _split_wait(f"scatter_pair_wait_{tag}", arrays, s_sems, r_sems, after, _plan_scatter_pair)
        n = len(kinds)
        sums, mine_lands = [], []
        for k, j in enumerate(kinds):
            name = KINDS[j][0]
            chip_sum, land = _pair_sum(arrays[k], arrays[n + k], lands[j], slot, core, rows_tile[name],
                                       f"pair_sum_{name}")
            sums.append(chip_sum)
            mine_lands.append(land)
        s_sems, r_sems, arrays, token = _split_start(
            f"scatter_chips_start_{tag}", sums + mine_lands, 3 * n, _plan_scatter_chips(slot))
        return (tag, kinds, slot, s_sems, r_sems, arrays), token

    def scatter_done(state, lands, after):
        tag, kinds, slot, s_sems, r_sems, arrays = state
        arrays = _split_wait(f"scatter_chips_wait_{tag}", arrays, s_sems, r_sems, after, _plan_scatter_chips(slot))
        lands = list(lands)
        for k, j in enumerate(kinds):
            lands[j] = arrays[len(kinds) + k]
        return lands

    sg = [None] * DEPTH
    pending, tie = None, no_tie
    for i in reversed(range(DEPTH)):
        hooks, held = dict(no_hooks), {}
        if pending is not None:
            def mid(value, i=i, pending=pending, held=held):
                held["chips"], token = scatter_chips(pending, landings, i + 1, value[2])
                if i == 0:
                    held["ffn_pair"], token = scatter_pair("0f", ffn_kinds, value[:2], token)
                return token

            hooks.update(mid=mid)
        if i == 0:
            def late(value, held=held):
                held["ffn_chips"], token = scatter_chips(held["ffn_pair"], landings0, 0, value)
                return token

            hooks.update(late=late)
        dx, grads, sg[i] = _layer_bwd(dx, kept[i], i, weights[i], rp, mats, tabs, tie, hooks)
        if pending is not None:
            landings = scatter_done(held["chips"], landings, dx)
        if i > 0:
            pending, tie = scatter_pair(str(i), all_kinds, grads, dx)
        else:
            pending, _ = scatter_pair("0m", mixer_kinds, [grads[j] for j in mixer_kinds], dx)

    sgrads, conv_grad = _replicated_grads(sg, sp)

    small_names = [name for name, _ in SMALL] + ["norm_final", "conv_w"]
    sgrads.update(norm_final=d_norm_final, conv_w=conv_grad)
    small_src = [sgrads[name].reshape(small_shapes[name]).astype(BF16) for name in small_names]
    last, tie = scatter_chips(pending, landings0, 0, small_src[0])
    s_sems, r_sems, arrays, tie = _split_start(
        "gather_small_chips_start", small_src + place_own(small_src), 4 * len(small_src), _plan_gather_chips, tie)
    small_state = ("small", s_sems, r_sems, arrays)

    big_out = []
    for j, (name, _, _) in enumerate(KINDS):
        big_out.append(_adamw(landings[j], travel["w"][j], travel["m"][j], travel["v"][j], rows_tile[name],
                              "adamw_late_" + name, groups=(1, DEPTH), tie=tie))
        tie = big_out[-1][3]
    landings0 = scatter_done(held["ffn_chips"], landings0, tie)
    landings0 = scatter_done(last, landings0, tie)
    small_state, _ = gather_pass(small_state, landings0[0])
    big_out = [_adamw(landings0[j], travel["w"][j], travel["m"][j], travel["v"][j], rows_tile[name],
                      "adamw_first_" + name, groups=(0, 1), fill=big_out[j]) for j, (name, _, _) in enumerate(KINDS)]
    big_res = [_native_layout([big_out[j][kind] for j in range(len(KINDS))]) for kind in range(4)]

    _, sparts = gather_done(small_state, big_out[-1][0], ())
    sparts = dict(zip(small_names, sparts))
    sparts["conv_w"] = lax.dynamic_slice_in_dim(sparts["conv_w"], mine * 64, 64, axis=3)
    small_res = {}
    for name in small_names:
        shape = small_shapes[name] if name != "conv_w" else (DEPTH, 3, 64)
        state = [(convs[s] if name == "conv_w" else finals[s] if name == "norm_final" else small[s][name])
                 .reshape(shape) for s in "wmv"]
        res = _adamw_small(sparts[name], *state, "adamw_" + name)
        small_res[name] = [r.reshape(state_shape) for r, state_shape in zip(res, [small_out_shapes[name]] * 4)]

    order = ["norm_mix", "w_in", "b_gate", "attn_sinks", "w_attn_o", "conv_w", "w_conv_o", "ssm_a_re", "ssm_a_im",
             "ssm_b_re", "ssm_b_im", "ssm_c_re", "ssm_c_im", "ssm_d", "ssm_log_dt", "w_ssm_glu", "w_ssm_o",
             "w_mix_o", "norm_ffn", "w_ffn_in", "w_ffn_out", "norm_final"]
    outs = [loss, dx[None]]
    for kind in range(4):
        for name in order:
            outs.append(big_res[kind][name] if name in big_res[kind] else small_res[name][kind])
    return tuple(outs)
```

```python
import functools
import math

import jax
import jax.numpy as jnp
from jax import lax
from jax.experimental import pallas as pl
from jax.experimental.pallas import tpu as pltpu

F32 = jnp.float32
BF16 = jnp.bfloat16

N_DEV = 8
DEPTH = 4
SEQ = 2048
D_MODEL = 1024
N_Q_HEADS = 8
HEAD_DIM = 64
ATTN_W = 512
KV_W = 128
BLOCK = 128
N_BLOCKS = SEQ // BLOCK
ROPE_THETA = 500000.0
ROT_DIM = 16
NEG_INF = -1e30
WIDTH = 512
SSM_GROUPS = 32
SSM_GROUP = 16
SSM_STATE = 64
SLABS = 16
CHUNK = 256
N_CHUNKS = SEQ // CHUNK
GATE_W = 3 * D_MODEL
IN_COLS = 5888
FFN_H = 2816
NORM_EPS = 1e-6
LR, B1, B2, ADAM_EPS, WD, STEP = 0.001, 0.9, 0.999, 1e-08, 0.01, 10

COL_Q, COL_KV, COL_CBX, COL_U, COL_G = 0, 512, 768, 2304, 2816
PIECE_W = (512, 256, 512, 512, 512, 512, 3072)
PIECE_OFF = tuple(sum(PIECE_W[:i]) for i in range(len(PIECE_W)))

KINDS = (("win_t", 736, 1024), ("wffn_t", 704, 1024), ("wout", 352, 1024), ("wmix", 128, 1024),
         ("branch_t", 128, 1536), ("wglu", 64, 512))

SMALL = (("norm_mix", 1024), ("b_gate", 3072), ("attn_sinks", 8), ("ssm_a_re", 2048), ("ssm_a_im", 2048),
         ("ssm_b_re", 32768), ("ssm_b_im", 32768), ("ssm_c_re", 32768), ("ssm_c_im", 32768),
         ("ssm_d", 512), ("ssm_log_dt", 32), ("norm_ffn", 1024))
SMALL_PER_LAYER = sum(n for _, n in SMALL)
CONV_N = DEPTH * 3 * WIDTH
SMALL_ROWS = 4480

VMEM_LIMIT = 56 * 1024 * 1024
NT = (((1,), (1,)), ((), ()))
TN = (((0,), (0,)), ((), ()))
MESH_ID = pl.DeviceIdType.MESH
ANY = pl.BlockSpec(memory_space=pl.ANY)
HBM = pl.BlockSpec(memory_space=pltpu.HBM)
SEM = pl.BlockSpec(memory_space=pltpu.SEMAPHORE)
EFFECT = pltpu.SideEffectType.DATAFLOW_SIDE_EFFECTING


def _cp(**kw):
    return pltpu.CompilerParams(vmem_limit_bytes=VMEM_LIMIT, **kw)


def _full(shape):
    return pl.BlockSpec(shape, lambda *_: (0,) * len(shape))


def _resident(shape):
    return pl.BlockSpec(shape, lambda *_: (0,) * len(shape), pipeline_mode=pl.Buffered(1))


def _mm_tn(a, b, *, tm, tn, name):
    k, m = a.shape
    n = b.shape[1]

    def body(a_ref, b_ref, o_ref):
        o_ref[...] = lax.dot_general(a_ref[...].astype(BF16), b_ref[...].astype(BF16), TN,
                                     preferred_element_type=F32).astype(BF16)

    return pl.pallas_call(
        body, grid=(m // tm, n // tn),
        in_specs=[pl.BlockSpec((k, tm), lambda i, j: (0, i)), pl.BlockSpec((k, tn), lambda i, j: (0, j))],
        out_specs=pl.BlockSpec((tm, tn), lambda i, j: (i, j)),
        out_shape=jax.ShapeDtypeStruct((m, n), BF16), compiler_params=_cp(), name=name)(a, b)


def _rms_rows(xv, g):
    r = lax.rsqrt(jnp.mean(xv * xv, axis=-1, keepdims=True) + NORM_EPS)
    return ((xv * r) * g).astype(BF16)


def _rms_mm_in(x, g, wt, tie):
    tt = 512
    widths = (ATTN_W, 2 * KV_W, 3 * WIDTH, WIDTH, GATE_W)
    offs = (COL_Q, COL_KV, COL_CBX, COL_U, COL_G)

    def body(x_ref, g_ref, w_ref, tie_ref, q_ref, kv_ref, cbx_ref, u_ref, gl_ref, h_ref):
        h = _rms_rows(x_ref[...], g_ref[...])
        h_ref[...] = h
        prod = lax.dot_general(h, w_ref[...], NT, preferred_element_type=F32)
        for ref, o, w in zip((q_ref, kv_ref, cbx_ref, u_ref, gl_ref), offs, widths):
            ref[...] = prod[:, o:o + w]

    row = lambda w: pl.BlockSpec((tt, w), lambda i: (i, 0))
    sds = jax.ShapeDtypeStruct
    return pl.pallas_call(
        body, grid=(SEQ // tt,), in_specs=[row(D_MODEL), _full((1, D_MODEL)), _resident((IN_COLS, D_MODEL)), ANY],
        out_specs=[row(ATTN_W), row(2 * KV_W), row(3 * WIDTH), row(WIDTH), row(GATE_W), row(D_MODEL)],
        out_shape=[sds((SEQ, ATTN_W), F32), sds((SEQ, 2 * KV_W), F32), sds((SEQ, 3 * WIDTH), F32),
                   sds((SEQ, WIDTH), F32), sds((SEQ, GATE_W), F32), sds((SEQ, D_MODEL), BF16)],
        compiler_params=_cp(), name="rms_mm_in")(x, g, wt, tie)


def _rms_mm_ffn(x, g, wt):
    tt = 512

    def body(x_ref, g_ref, w_ref, o_ref, h_ref):
        h = _rms_rows(x_ref[...], g_ref[...])
        h_ref[...] = h
        o_ref[...] = lax.dot_general(h, w_ref[...], NT, preferred_element_type=F32)

    row = lambda w: pl.BlockSpec((tt, w), lambda i: (i, 0))
    return pl.pallas_call(
        body, grid=(SEQ // tt,), in_specs=[row(D_MODEL), _full((1, D_MODEL)), _resident((2 * FFN_H, D_MODEL))],
        out_specs=[row(2 * FFN_H), row(D_MODEL)],
        out_shape=[jax.ShapeDtypeStruct((SEQ, 2 * FFN_H), F32), jax.ShapeDtypeStruct((SEQ, D_MODEL), BF16)],
        compiler_params=_cp(), name="rms_mm_ffn")(x, g, wt)


def _mm_rmsbwd(pieces, wt, x, g, dres, name):
    tt = 512
    widths = [p.shape[1] for p in pieces]
    offs = [sum(widths[:i]) for i in range(len(widths))]
    n = len(pieces)

    def body(*refs):
        p_refs, (w_ref, x_ref, g_ref, r_ref, dx_ref, dg_ref) = refs[:n], refs[n:]

        @pl.when(pl.program_id(0) == 0)
        def _():
            dg_ref[...] = jnp.zeros_like(dg_ref)

        dh = jnp.zeros((tt, D_MODEL), F32)
        for p_ref, o, w in zip(p_refs, offs, widths):
            dh += jnp.dot(p_ref[...], w_ref[o:o + w, :], preferred_element_type=F32)
        xv = x_ref[...]
        r = lax.rsqrt(jnp.mean(xv * xv, axis=-1, keepdims=True) + NORM_EPS)
        xh = xv * r
        gy = dh * g_ref[...]
        dx_ref[...] = r_ref[...] + r * (gy - xh * jnp.mean(gy * xh, axis=-1, keepdims=True))
        dg_ref[...] += jnp.sum(dh * xh, axis=0, keepdims=True)

    row = lambda w: pl.BlockSpec((tt, w), lambda i: (i, 0))
    return pl.pallas_call(
        body, grid=(SEQ // tt,),
        in_specs=[row(w) for w in widths] + [_resident(wt.shape), row(D_MODEL), _full((1, D_MODEL)), row(D_MODEL)],
        out_specs=[row(D_MODEL), _full((1, D_MODEL))],
        out_shape=[jax.ShapeDtypeStruct((SEQ, D_MODEL), F32), jax.ShapeDtypeStruct((1, D_MODEL), F32)],
        compiler_params=_cp(), name=name)(*pieces, wt, x, g, dres)


def _tn_pieces(pieces, h):
    tk, tn = 512, 512
    nk = SEQ // tk
    n = len(pieces)

    def body(*refs):
        p_refs, (h_ref, o_ref, acc_ref) = refs[:n], refs[n:]
        kk = pl.program_id(1)

        @pl.when(kk == 0)
        def _():
            acc_ref[...] = jnp.zeros_like(acc_ref)

        hv = h_ref[...]
        for p_ref, o, w in zip(p_refs, PIECE_OFF, PIECE_W):
            acc_ref[o:o + w, :] += lax.dot_general(p_ref[...], hv, TN, preferred_element_type=F32)

        @pl.when(kk == nk - 1)
        def _():
            o_ref[...] = acc_ref[...].astype(BF16)

    return pl.pallas_call(
        body, grid=(D_MODEL // tn, nk),
        in_specs=[pl.BlockSpec((tk, w), lambda j, kk: (kk, 0)) for w in PIECE_W]
        + [pl.BlockSpec((tk, tn), lambda j, kk: (kk, j))],
        out_specs=pl.BlockSpec((IN_COLS, tn), lambda j, kk: (0, j)),
        out_shape=jax.ShapeDtypeStruct((IN_COLS, D_MODEL), BF16),
        scratch_shapes=[pltpu.VMEM((IN_COLS, tn), F32)], compiler_params=_cp(), name="tn_pieces")(*pieces, h)


def _tn_branches(dys, acts):
    tk = 512
    nk = SEQ // tk

    def body(d0, d1, d2, a0, a1, a2, o_ref, acc_ref):
        kk = pl.program_id(0)

        @pl.when(kk == 0)
        def _():
            acc_ref[...] = jnp.zeros_like(acc_ref)

        for j, (d, a) in enumerate(((d0, a0), (d1, a1), (d2, a2))):
            acc_ref[:, WIDTH * j:WIDTH * (j + 1)] += lax.dot_general(d[...], a[...], TN, preferred_element_type=F32)

        @pl.when(kk == nk - 1)
        def _():
            o_ref[...] = acc_ref[...].astype(BF16)

    row = lambda w: pl.BlockSpec((tk, w), lambda kk: (kk, 0))
    return pl.pallas_call(
        body, grid=(nk,), in_specs=[row(D_MODEL)] * 3 + [row(WIDTH)] * 3,
        out_specs=_full((D_MODEL, 3 * WIDTH)), out_shape=jax.ShapeDtypeStruct((D_MODEL, 3 * WIDTH), BF16),
        scratch_shapes=[pltpu.VMEM((D_MODEL, 3 * WIDTH), F32)], compiler_params=_cp(), name="tn_branches",
    )(*dys, *acts)


def _rope(t, c, a, b):
    return t * c + pltpu.roll(t, 120, axis=1) * a + pltpu.roll(t, 8, axis=1) * b


def _rope_t(d, c, a, b):
    return d * c + pltpu.roll(d * a, 8, axis=1) + pltpu.roll(d * b, 120, axis=1)


def _band_sides(band):
    left = lax.broadcasted_iota(jnp.int32, band.shape, 1) < HEAD_DIM
    h0 = jnp.where(left, band, 0.0)
    h1 = jnp.where(left, 0.0, band)
    r0 = pltpu.roll(h0, HEAD_DIM, axis=1)
    r1 = pltpu.roll(h1, HEAD_DIM, axis=1)
    return ((h0.astype(BF16), r0.astype(BF16)), (r1.astype(BF16), h1.astype(BF16)))


def _attn_mask(i):
    qi = lax.broadcasted_iota(jnp.int32, (2 * BLOCK, 2 * BLOCK), 0) % BLOCK
    kj = lax.broadcasted_iota(jnp.int32, (2 * BLOCK, 2 * BLOCK), 1)
    delta = qi + BLOCK - kj
    return (delta >= 0) & (delta < BLOCK) & ((kj >= BLOCK) | (i > 0))


def _attn_probs(s, ok, sink):
    s = jnp.where(ok, s * (HEAD_DIM ** -0.5), NEG_INF)
    m = jnp.maximum(jnp.max(s, axis=-1, keepdims=True), sink)
    p = jnp.exp(s - m)
    es = jnp.exp(sink - m)
    inv = 1.0 / (jnp.sum(p, axis=-1, keepdims=True) + es)
    return p * inv, es * inv


def _kv_group(qs, ks, vs, kh, sink_ref):
    q2 = jnp.concatenate([qs[2 * kh], qs[2 * kh + 1]], axis=0)
    kst = jnp.concatenate([ks[kh][0], ks[kh][1]], axis=0)
    vst = jnp.concatenate([vs[kh][0], vs[kh][1]], axis=0)
    top = lax.broadcasted_iota(jnp.int32, (2 * BLOCK, 1), 0) < BLOCK
    sinks = [jnp.where(top, sink_ref[0, 4 * kh + h], sink_ref[0, 4 * kh + 2 + h]) for h in range(2)]
    return q2, kst, vst, sinks


def _attn_load(q_ref, kvc_ref, kvp_ref, tc_ref, ta_ref, tb_ref, pc_ref, pa_ref, pb_ref):
    c, a, b = tc_ref[...], ta_ref[...], tb_ref[...]
    kc = _rope(kvc_ref[:, :KV_W], c, a, b)
    kp = _rope(kvp_ref[:, :KV_W], pc_ref[...], pa_ref[...], pb_ref[...])
    kband = jnp.concatenate([kp, kc], axis=0)
    vband = jnp.concatenate([kvp_ref[:, KV_W:], kvc_ref[:, KV_W:]], axis=0)
    qs = [_rope(q_ref[:, 128 * j:128 * (j + 1)], c, a, b).astype(BF16) for j in range(4)]
    return qs, _band_sides(kband), _band_sides(vband), (c, a, b)


def _attn_specs(clamp):
    cur = lambda i: (clamp(i), 0)
    prev = lambda i: (jnp.maximum(clamp(i) - 1, 0), 0)
    return [
        pl.BlockSpec((BLOCK, ATTN_W), cur), pl.BlockSpec((BLOCK, 2 * KV_W), cur),
        pl.BlockSpec((BLOCK, 2 * KV_W), prev),
        pl.BlockSpec((BLOCK, 128), cur), pl.BlockSpec((BLOCK, 128), cur), pl.BlockSpec((BLOCK, 128), cur),
        pl.BlockSpec((BLOCK, 128), prev), pl.BlockSpec((BLOCK, 128), prev), pl.BlockSpec((BLOCK, 128), prev),
        pl.BlockSpec(memory_space=pltpu.SMEM),
    ]


def _attn_fwd(q, kv, tabs, sinks):
    tc, ta, tb = tabs

    def body(q_ref, kvc_ref, kvp_ref, tc_ref, ta_ref, tb_ref, pc_ref, pa_ref, pb_ref, sink_ref, o_ref):
        i = pl.program_id(0)
        qs, ks, vs, _ = _attn_load(q_ref, kvc_ref, kvp_ref, tc_ref, ta_ref, tb_ref, pc_ref, pa_ref, pb_ref)
        ok = _attn_mask(i)
        for kh in range(2):
            q2, kst, vst, sinks = _kv_group(qs, ks, vs, kh, sink_ref)
            s = lax.dot_general(q2, kst, NT, preferred_element_type=F32)
            pn = [_attn_probs(s[:, 2 * BLOCK * h:2 * BLOCK * (h + 1)], ok, sinks[h])[0].astype(BF16) for h in range(2)]
            o2 = jnp.dot(jnp.concatenate(pn, axis=1), vst, preferred_element_type=F32).astype(BF16)
            for r in range(2):
                j = 2 * kh + r
                o_ref[:, 128 * j:128 * (j + 1)] = o2[BLOCK * r:BLOCK * (r + 1)]

    return pl.pallas_call(
        body, grid=(N_BLOCKS,), in_specs=_attn_specs(lambda i: i),
        out_specs=pl.BlockSpec((BLOCK, ATTN_W), lambda i: (i, 0)),
        out_shape=jax.ShapeDtypeStruct((SEQ, ATTN_W), BF16), compiler_params=_cp(), name="attn_fwd",
    )(q, kv, kv, tc, ta, tb, tc, ta, tb, sinks)


def _attn_bwd(q, kv, tabs, sinks, do):
    tc, ta, tb = tabs
    last = N_BLOCKS - 1
    clamp = lambda i: jnp.minimum(i, last)

    def place(full, side, kh):
        left = lax.broadcasted_iota(jnp.int32, full.shape, 1) < HEAD_DIM
        valid = jnp.where(left, full, 0.0) if side == 0 else jnp.where(left, 0.0, full)
        return valid if side == kh else pltpu.roll(valid, HEAD_DIM, axis=1)

    def body(q_ref, kvc_ref, kvp_ref, tc_ref, ta_ref, tb_ref, pc_ref, pa_ref, pb_ref, sink_ref, do_ref,
             dq_ref, dkv_ref, ds_ref, carry_ref):
        i = pl.program_id(0)

        @pl.when(i == 0)
        def _():
            ds_ref[...] = jnp.zeros_like(ds_ref)
            carry_ref[...] = jnp.zeros_like(carry_ref)

        @pl.when(i > last)
        def _():
            dkv_ref[...] = carry_ref[...].astype(BF16)

        @pl.when(i <= last)
        def _():
            qs, ks, vs, (c, a, b) = _attn_load(q_ref, kvc_ref, kvp_ref, tc_ref, ta_ref, tb_ref,
                                               pc_ref, pa_ref, pb_ref)
            ok = _attn_mask(i)
            dk = jnp.zeros((2 * BLOCK, 128), F32)
            dv = jnp.zeros((2 * BLOCK, 128), F32)
            dsink = jnp.zeros((1, 128), F32)
            lane = lax.broadcasted_iota(jnp.int32, (1, 128), 1)
            for kh in range(2):
                q2, kst, vst, sinks = _kv_group(qs, ks, vs, kh, sink_ref)
                do2 = jnp.concatenate([do_ref[:, 128 * (2 * kh + r):128 * (2 * kh + r + 1)] for r in range(2)],
                                      axis=0).astype(BF16)
                s = lax.dot_general(q2, kst, NT, preferred_element_type=F32)
                dp = lax.dot_general(do2, vst, NT, preferred_element_type=F32)
                pns, dss = [], []
                for h in range(2):
                    cols = slice(2 * BLOCK * h, 2 * BLOCK * (h + 1))
                    pn, ps = _attn_probs(s[:, cols], ok, sinks[h])
                    dr = jnp.sum(pn * dp[:, cols], axis=-1, keepdims=True)
                    pns.append(pn.astype(BF16))
                    dss.append((pn * (dp[:, cols] - dr) * (HEAD_DIM ** -0.5)).astype(BF16))
                    for r in range(2):
                        part = -jnp.sum((ps * dr)[BLOCK * r:BLOCK * (r + 1)])
                        dsink += jnp.where(lane == 4 * kh + 2 * r + h, part, 0.0)
                ds2, pn2 = jnp.concatenate(dss, axis=1), jnp.concatenate(pns, axis=1)
                dq2 = jnp.dot(ds2, kst, preferred_element_type=F32)
                dk2 = lax.dot_general(ds2, q2, TN, preferred_element_type=F32)
                dv2 = lax.dot_general(pn2, do2, TN, preferred_element_type=F32)
                for h in range(2):
                    dk += place(dk2[2 * BLOCK * h:2 * BLOCK * (h + 1)], h, kh)
                    dv += place(dv2[2 * BLOCK * h:2 * BLOCK * (h + 1)], h, kh)
                for r in range(2):
                    j = 2 * kh + r
                    dq_ref[:, 128 * j:128 * (j + 1)] = _rope_t(dq2[BLOCK * r:BLOCK * (r + 1)], c, a, b).astype(BF16)
            ds_ref[...] += dsink
            dk_prev = _rope_t(dk[:BLOCK], pc_ref[...], pa_ref[...], pb_ref[...])
            dk_cur = _rope_t(dk[BLOCK:], c, a, b)
            prev = jnp.concatenate([dk_prev, dv[:BLOCK]], axis=1)
            dkv_ref[...] = (carry_ref[...] + prev).astype(BF16)
            carry_ref[...] = jnp.concatenate([dk_cur, dv[BLOCK:]], axis=1)

    return pl.pallas_call(
        body, grid=(N_BLOCKS + 1,),
        in_specs=_attn_specs(clamp) + [pl.BlockSpec((BLOCK, ATTN_W), lambda i: (clamp(i), 0))],
        out_specs=[pl.BlockSpec((BLOCK, ATTN_W), lambda i: (clamp(i), 0)),
                   pl.BlockSpec((BLOCK, 2 * KV_W), lambda i: (jnp.maximum(i - 1, 0), 0)),
                   pl.BlockSpec((1, 128), lambda i: (0, 0))],
        out_shape=[jax.ShapeDtypeStruct((SEQ, ATTN_W), BF16), jax.ShapeDtypeStruct((SEQ, 2 * KV_W), BF16),
                   jax.ShapeDtypeStruct((1, 128), F32)],
        scratch_shapes=[pltpu.VMEM((BLOCK, 2 * KV_W), F32)], compiler_params=_cp(), name="attn_bwd",
    )(q, kv, kv, tc, ta, tb, tc, ta, tb, sinks, do)


def _shift_down(z, k):
    row = lax.broadcasted_iota(jnp.int32, z.shape, 0)
    return jnp.where(row < k, 0.0, pltpu.roll(z, k, axis=0))


def _shift_up(z, k):
    n = z.shape[0]
    row = lax.broadcasted_iota(jnp.int32, z.shape, 0)
    return jnp.where(row >= n - k, 0.0, pltpu.roll(z, n - k, axis=0))


def _conv_specs():
    nb = WIDTH // 128
    return [pl.BlockSpec((SEQ, 128), lambda j: (0, j)), pl.BlockSpec((SEQ, 128), lambda j: (0, nb + j)),
            pl.BlockSpec((SEQ, 128), lambda j: (0, 2 * nb + j)), pl.BlockSpec((None, 8, 128), lambda j: (0, 0, j))]


def _conv_fwd(cbx, cw, layer):
    def body(cb_ref, cc_ref, cx_ref, w_ref, o_ref):
        z = cc_ref[...] * cx_ref[...]
        s = w_ref[0:1, :] * _shift_down(z, 2) + w_ref[1:2, :] * _shift_down(z, 1) + w_ref[2:3, :] * z
        o_ref[...] = (cb_ref[...] * s).astype(BF16)

    specs = _conv_specs()
    specs[3] = pl.BlockSpec((None, 8, 128), lambda j: (layer, 0, j))
    return pl.pallas_call(
        body, grid=(WIDTH // 128,), in_specs=specs,
        out_specs=pl.BlockSpec((SEQ, 128), lambda j: (0, j)),
        out_shape=jax.ShapeDtypeStruct((SEQ, WIDTH), BF16), compiler_params=_cp(), name="conv_fwd",
    )(cbx, cbx, cbx, cw)


def _conv_bwd(cbx, cw, layer, dout, tie):
    def body(cb_ref, cc_ref, cx_ref, w_ref, do_ref, tie_ref, dcb_ref, dcc_ref, dcx_ref, dw_ref):
        cc, cx = cc_ref[...], cx_ref[...]
        z = cc * cx
        z1, z2 = _shift_down(z, 1), _shift_down(z, 2)
        w0, w1, w2 = w_ref[0:1, :], w_ref[1:2, :], w_ref[2:3, :]
        dout = do_ref[...]
        ds = dout * cb_ref[...]
        dcb_ref[...] = (dout * (w0 * z2 + w1 * z1 + w2 * z)).astype(BF16)
        dz = w2 * ds + w1 * _shift_up(ds, 1) + w0 * _shift_up(ds, 2)
        dcc_ref[...] = (dz * cx).astype(BF16)
        dcx_ref[...] = (dz * cc).astype(BF16)
        rows = [jnp.sum(ds * zz, axis=0, keepdims=True) for zz in (z2, z1, z)]
        dw_ref[...] = jnp.concatenate(rows + [jnp.zeros((5, 128), F32)], axis=0)

    col = lambda j: (0, j)
    specs = _conv_specs()
    specs[3] = pl.BlockSpec((None, 8, 128), lambda j: (layer, 0, j))
    return pl.pallas_call(
        body, grid=(WIDTH // 128,), in_specs=specs + [pl.BlockSpec((SEQ, 128), col), ANY],
        out_specs=[pl.BlockSpec((SEQ, 128), col), pl.BlockSpec((SEQ, 128), col), pl.BlockSpec((SEQ, 128), col),
                   pl.BlockSpec((8, 128), col)],
        out_shape=[jax.ShapeDtypeStruct((SEQ, WIDTH), BF16)] * 3 + [jax.ShapeDtypeStruct((8, WIDTH), F32)],
        compiler_params=_cp(), name="conv_bwd",
    )(cbx, cbx, cbx, cw, dout, tie)


def _ssm_prep_math(a_re, a_im, log_dt, bt_re, bt_im):
    dt = jnp.exp(log_dt)
    er = jnp.exp(a_re * dt)
    lr = er * jnp.cos(a_im * dt)
    li = er * jnp.sin(a_im * dt)
    n2 = a_re * a_re + a_im * a_im
    cr = ((lr - 1.0) * a_re + li * a_im) / n2
    ci = (li * a_re - (lr - 1.0) * a_im) / n2
    cr3, ci3 = cr[:, None, :], ci[:, None, :]
    return lr, li, cr3 * bt_re - ci3 * bt_im, cr3 * bt_im + ci3 * bt_re


_GS = (SSM_GROUPS, SSM_STATE)
_GHS = (SSM_GROUPS, SSM_GROUP, SSM_STATE)


def _layered(shape):
    return pl.BlockSpec((None,) + shape, lambda l: (l,) + (0,) * len(shape))


def _ssm_prep(a_re, a_im, log_dt, bt_re, bt_im):
    def body(ar, ai, ld, br, bi, o0, o1, o2, o3):
        outs = _ssm_prep_math(ar[...], ai[...], ld[...], br[...], bi[...])
        for o, v in zip((o0, o1, o2, o3), outs):
            o[...] = v

    shapes = [_GS, _GS, _GHS, _GHS]
    return pl.pallas_call(
        body, grid=(DEPTH,), in_specs=[_layered(s) for s in (_GS, _GS, (SSM_GROUPS, 1), _GHS, _GHS)],
        out_specs=[_layered(s) for s in shapes],
        out_shape=[jax.ShapeDtypeStruct((DEPTH,) + s, F32) for s in shapes],
        name="ssm_prep")(a_re, a_im, log_dt, bt_re, bt_im)


def _ssm_prep_bwd(a_re, a_im, log_dt, bt_re, bt_im, cots):
    def body(ar, ai, ld, br, bi, c0, c1, c2, c3, o0, o1, o2, o3, o4):
        _, vjp = jax.vjp(_ssm_prep_math, ar[...], ai[...], ld[...], br[...], bi[...])
        for o, v in zip((o0, o1, o2, o3, o4), vjp((c0[...], c1[...], c2[...], c3[...]))):
            o[...] = v

    ins = (_GS, _GS, (SSM_GROUPS, 1), _GHS, _GHS)
    return pl.pallas_call(
        body, grid=(DEPTH,), in_specs=[_layered(s) for s in ins + (_GS, _GS, _GHS, _GHS)],
        out_specs=[_layered(s) for s in ins],
        out_shape=[jax.ShapeDtypeStruct((DEPTH,) + s, F32) for s in ins],
        name="ssm_prep_bwd")(a_re, a_im, log_dt, bt_re, bt_im, *cots)


LANES_G = 512
N_LANE_GROUPS = SSM_GROUPS * SSM_STATE // LANES_G


def _scan_order(a):
    return a.reshape(N_CHUNKS, CHUNK, -1).transpose(1, 0, 2).reshape(a.shape)


def _time_order(a):
    return a.reshape(CHUNK, N_CHUNKS, -1).transpose(1, 0, 2).reshape(a.shape)


def _scan_in_place(xr_ref, xi_ref, ar, ai, reverse):
    shape = (N_CHUNKS, xr_ref.shape[1])
    ar, ai = jnp.broadcast_to(ar, shape), jnp.broadcast_to(ai, shape)

    def rows(tau):
        t = (CHUNK - 1 - tau) if reverse else tau
        return pl.ds(pl.multiple_of(t * N_CHUNKS, N_CHUNKS), N_CHUNKS)

    def step(tau, carry):
        sr, si = carry
        return ar * sr - ai * si + xr_ref[rows(tau), :], ar * si + ai * sr + xi_ref[rows(tau), :]

    zero = jnp.zeros(shape, F32)
    er, ei = lax.fori_loop(0, CHUNK, step, (zero, zero), unroll=8)
    qr, qi = ar, ai
    for _ in range(8):
        qr, qi = qr * qr - qi * qi, 2.0 * qr * qi
    shift = _shift_up if reverse else _shift_down
    for k in (1, 2, 4):
        sr, si = shift(er, k), shift(ei, k)
        er, ei = er + qr * sr - qi * si, ei + qr * si + qi * sr
        qr, qi = qr * qr - qi * qi, 2.0 * qr * qi
    start = (shift(er, 1), shift(ei, 1))

    def write(tau, carry):
        sr, si = step(tau, carry)
        xr_ref[rows(tau), :] = sr
        xi_ref[rows(tau), :] = si
        return sr, si

    return write, start


def _ssm_specs(layer):
    col = lambda w: pl.BlockSpec((SEQ, w), lambda g: (0, g))
    diag = pl.BlockSpec((None, None, 128, LANES_G), lambda g: (layer, g, 0, 0))
    vec = pl.BlockSpec((None, 1, LANES_G), lambda g: (layer, 0, g))
    return col, diag, vec


def _ssm_fwd(u, mats, layer, d):
    def body(u_ref, d_ref, br_ref, bi_ref, cr_ref, ci_ref, ar_ref, ai_ref, xr_ref, xi_ref, y_ref):
        uv = u_ref[...].astype(BF16)
        xr_ref[...] = jnp.dot(uv, br_ref[...], preferred_element_type=F32)
        xi_ref[...] = jnp.dot(uv, bi_ref[...], preferred_element_type=F32)
        write, start = _scan_in_place(xr_ref, xi_ref, ar_ref[...], ai_ref[...], False)
        lax.fori_loop(0, CHUNK, write, start, unroll=8)
        y = lax.dot_general(xr_ref[...].astype(BF16), cr_ref[...], NT, preferred_element_type=F32)
        y += lax.dot_general(xi_ref[...].astype(BF16), ci_ref[...], NT, preferred_element_type=F32)
        y_ref[...] = y + d_ref[...] * u_ref[...]

    col, diag, vec = _ssm_specs(layer)
    return pl.pallas_call(
        body, grid=(N_LANE_GROUPS,),
        in_specs=[col(128), pl.BlockSpec((None, 1, 128), lambda g: (layer, 0, g)),
                  diag, diag, diag, diag, vec, vec],
        out_specs=[col(LANES_G), col(LANES_G), col(128)],
        out_shape=[jax.ShapeDtypeStruct((SEQ, SSM_GROUPS * SSM_STATE), F32)] * 2
        + [jax.ShapeDtypeStruct((SEQ, WIDTH), F32)],
        compiler_params=_cp(), name="ssm_fwd",
    )(u, d, mats["b_re"], mats["b_im"], mats["c_re"], mats["c_im_neg"], mats["a_re"], mats["a_im"])


def _ssm_bwd(dy16, x_re, x_im, u, mats, layer, d):
    def body(dy_ref, u_ref, d_ref, xr_ref, xi_ref, br_ref, bi_ref, cr_ref, ci_ref, ar_ref, ai_ref,
             du_ref, dar_ref, dai_ref, dbr_ref, dbi_ref, dcr_ref, dci_ref, lr_ref, li_ref):
        dy = dy_ref[...]
        lr_ref[...] = jnp.dot(dy, cr_ref[...], preferred_element_type=F32)
        li_ref[...] = jnp.dot(dy, ci_ref[...], preferred_element_type=F32)
        write, start = _scan_in_place(lr_ref, li_ref, ar_ref[...], -ai_ref[...], True)

        def rows(t):
            return pl.ds(pl.multiple_of(t * N_CHUNKS, N_CHUNKS), N_CHUNKS)

        def grad(acc, lam, xpr, xpi):
            return acc[0] + xpr * lam[0] + xpi * lam[1], acc[1] + xpr * lam[1] - xpi * lam[0]

        def down(tau, carry):
            lam = write(tau, carry[0])
            t = CHUNK - 2 - tau
            return lam, grad(carry[1], lam, xr_ref[rows(t), :], xi_ref[rows(t), :])

        zero = jnp.zeros((N_CHUNKS, LANES_G), F32)
        lam, acc = lax.fori_loop(0, CHUNK - 1, down, (start, (zero, zero)), unroll=5)
        lam = write(CHUNK - 1, lam)
        last = rows(CHUNK - 1)
        acc = grad(acc, lam, _shift_down(xr_ref[last, :], 1), _shift_down(xi_ref[last, :], 1))
        dar_ref[...] = jnp.sum(acc[0], axis=0, keepdims=True)
        dai_ref[...] = jnp.sum(acc[1], axis=0, keepdims=True)

        l_re, l_im = lr_ref[...].astype(BF16), li_ref[...].astype(BF16)
        du = lax.dot_general(l_re, br_ref[...], NT, preferred_element_type=F32)
        du += lax.dot_general(l_im, bi_ref[...], NT, preferred_element_type=F32)
        du_ref[...] = (du + dy.astype(F32) * d_ref[...]).astype(BF16)
        uv = u_ref[...].astype(BF16)
        dbr_ref[...] = lax.dot_general(uv, l_re, TN, preferred_element_type=F32)
        dbi_ref[...] = lax.dot_general(uv, l_im, TN, preferred_element_type=F32)
        dcr_ref[...] = lax.dot_general(dy, xr_ref[...].astype(BF16), TN, preferred_element_type=F32)
        dci_ref[...] = lax.dot_general(dy, xi_ref[...].astype(BF16), TN, preferred_element_type=F32)

    col, diag, vec = _ssm_specs(layer)
    out_vec = pl.BlockSpec((1, LANES_G), lambda g: (0, g))
    out_blk = pl.BlockSpec((None, 128, LANES_G), lambda g: (g, 0, 0))
    sds = jax.ShapeDtypeStruct
    return pl.pallas_call(
        body, grid=(N_LANE_GROUPS,),
        in_specs=[col(128), col(128), pl.BlockSpec((None, 1, 128), lambda g: (layer, 0, g)),
                  col(LANES_G), col(LANES_G), diag, diag, diag, diag, vec, vec],
        out_specs=[col(128), out_vec, out_vec, out_blk, out_blk, out_blk, out_blk],
        out_shape=[sds((SEQ, WIDTH), BF16)] + [sds((1, SSM_GROUPS * SSM_STATE), F32)] * 2
        + [sds((N_LANE_GROUPS, 128, LANES_G), F32)] * 4,
        scratch_shapes=[pltpu.VMEM((SEQ, LANES_G), F32)] * 2, compiler_params=_cp(), name="ssm_bwd",
    )(dy16, u, d, x_re, x_im, mats["b_re"], mats["b_im"], mats["c_re"], mats["c_im_neg"],
      mats["a_re"], mats["a_im"])


_GELU_C = math.sqrt(2.0 / math.pi)


def _gelu(y):
    return 0.5 * y * (1.0 + jnp.tanh(_GELU_C * (y + 0.044715 * (y * y * y))))


def _glu_fwd(y, wglu):
    tt = 512

    def body(y_ref, w_ref, z_ref):
        ys = _gelu(y_ref[...])
        a = jnp.dot(ys.astype(BF16), w_ref[...], preferred_element_type=F32)
        z_ref[...] = (ys * jax.nn.sigmoid(a)).astype(BF16)

    blk = pl.BlockSpec((tt, WIDTH), lambda i: (i, 0))
    return pl.pallas_call(body, grid=(SEQ // tt,), in_specs=[blk, _full((WIDTH, WIDTH))], out_specs=blk,
                          out_shape=jax.ShapeDtypeStruct((SEQ, WIDTH), BF16), compiler_params=_cp(),
                          name="glu_fwd")(y, wglu)


def _glu_bwd(y, wglu, dz, u):
    tt = 512

    def body(y_ref, w_ref, dz_ref, u_ref, dy_ref, ys_ref, da_ref, dd_ref):
        @pl.when(pl.program_id(0) == 0)
        def _():
            dd_ref[...] = jnp.zeros_like(dd_ref)

        yv = y_ref[...]
        t = jnp.tanh(_GELU_C * (yv + 0.044715 * (yv * yv * yv)))
        ys = 0.5 * yv * (1.0 + t)
        ysb = ys.astype(BF16)
        sg = jax.nn.sigmoid(jnp.dot(ysb, w_ref[...], preferred_element_type=F32))
        dz = dz_ref[...].astype(F32)
        da = (dz * ys * sg * (1.0 - sg)).astype(BF16)
        dys = dz * sg + lax.dot_general(da, w_ref[...], NT, preferred_element_type=F32)
        dy = dys * (0.5 * (1.0 + t) + 0.5 * yv * (1.0 - t * t) * _GELU_C * (1.0 + 3 * 0.044715 * (yv * yv)))
        dy_ref[...] = dy.astype(BF16)
        ys_ref[...] = ysb
        da_ref[...] = da
        dd_ref[...] += jnp.sum(dy * u_ref[...], axis=0, keepdims=True)

    blk = pl.BlockSpec((tt, WIDTH), lambda i: (i, 0))
    return pl.pallas_call(
        body, grid=(SEQ // tt,), in_specs=[blk, _full((WIDTH, WIDTH)), blk, blk],
        out_specs=[blk, blk, blk, _full((1, WIDTH))],
        out_shape=[jax.ShapeDtypeStruct((SEQ, WIDTH), BF16)] * 3 + [jax.ShapeDtypeStruct((1, WIDTH), F32)],
        compiler_params=_cp(), name="glu_bwd")(y, wglu, dz, u)


def _mix_specs(tt, layer):
    row = lambda w: pl.BlockSpec((tt, w), lambda i: (i, 0))
    gate = lambda j: pl.BlockSpec((tt, D_MODEL), lambda i: (i, j))
    wo = lambda j: pl.BlockSpec((D_MODEL, WIDTH), lambda i: (0, j))
    return [row(D_MODEL), row(WIDTH), row(WIDTH), row(WIDTH), gate(0), gate(1), gate(2),
            pl.BlockSpec((None, 1, GATE_W), lambda i: (layer, 0, 0)), wo(0), wo(1), wo(2),
            _full((D_MODEL, D_MODEL))]


def _mix_branches(o_ref, c_ref, z_ref, g_refs, b_ref, wa_ref, wc_ref, ws_ref):
    ys = [lax.dot_general(r[...], w[...], NT, preferred_element_type=F32)
          for r, w in ((o_ref, wa_ref), (c_ref, wc_ref), (z_ref, ws_ref))]
    gates = [jax.nn.sigmoid(g_refs[j][...] + b_ref[:, D_MODEL * j:D_MODEL * (j + 1)]) for j in range(3)]
    return ys, gates


def _mix_fwd(x, o, cv, z, glog, b_gate, layer, wbt, wmix, tie):
    tt = 256

    def body(x_ref, o_ref, c_ref, z_ref, g0, g1, g2, b_ref, wa_ref, wc_ref, ws_ref, wm_ref, tie_ref, x1_ref):
        ys, gates = _mix_branches(o_ref, c_ref, z_ref, (g0, g1, g2), b_ref, wa_ref, wc_ref, ws_ref)
        merged = gates[0] * ys[0] + gates[1] * ys[1] + gates[2] * ys[2]
        x1_ref[...] = x_ref[...] + jnp.dot(merged.astype(BF16), wm_ref[...], preferred_element_type=F32)

    return pl.pallas_call(
        body, grid=(SEQ // tt,), in_specs=_mix_specs(tt, layer) + [ANY],
        out_specs=pl.BlockSpec((tt, D_MODEL), lambda i: (i, 0)),
        out_shape=jax.ShapeDtypeStruct((SEQ, D_MODEL), F32), compiler_params=_cp(), name="mix_fwd",
    )(x, o, cv, z, glog, glog, glog, b_gate, wbt, wbt, wbt, wmix, tie)


def _mix_bwd(dx1, o, cv, z, glog, b_gate, layer, wbt, wmix, tie):
    tt = 256

    def body(dx_ref, o_ref, c_ref, z_ref, g0, g1, g2, b_ref, wa_ref, wc_ref, ws_ref, wm_ref, tie_ref,
             mg_ref, dya_ref, dyc_ref, dys_ref, do_ref, dc_ref, dz_ref, dgl_ref, db_ref):
        @pl.when(pl.program_id(0) == 0)
        def _():
            db_ref[...] = jnp.zeros_like(db_ref)

        ys, gates = _mix_branches(o_ref, c_ref, z_ref, (g0, g1, g2), b_ref, wa_ref, wc_ref, ws_ref)
        mg_ref[...] = (gates[0] * ys[0] + gates[1] * ys[1] + gates[2] * ys[2]).astype(BF16)
        dm = lax.dot_general(dx_ref[...].astype(BF16), wm_ref[...], NT, preferred_element_type=F32)
        for j, (dy_ref, w_ref, d_ref) in enumerate(((dya_ref, wa_ref, do_ref), (dyc_ref, wc_ref, dc_ref),
                                                    (dys_ref, ws_ref, dz_ref))):
            dy = (dm * gates[j]).astype(BF16)
            dy_ref[...] = dy
            d_ref[...] = jnp.dot(dy, w_ref[...], preferred_element_type=F32)
            dgl = dm * ys[j] * gates[j] * (1.0 - gates[j])
            dgl_ref[:, D_MODEL * j:D_MODEL * (j + 1)] = dgl.astype(BF16)
            db_ref[:, D_MODEL * j:D_MODEL * (j + 1)] += jnp.sum(dgl, axis=0, keepdims=True)

    row = lambda w: pl.BlockSpec((tt, w), lambda i: (i, 0))
    sds = jax.ShapeDtypeStruct
    return pl.pallas_call(
        body, grid=(SEQ // tt,), in_specs=_mix_specs(tt, layer) + [ANY],
        out_specs=[row(D_MODEL)] * 4 + [row(WIDTH)] * 3 + [row(GATE_W), _full((1, GATE_W))],
        out_shape=[sds((SEQ, D_MODEL), BF16)] * 4 + [sds((SEQ, WIDTH), F32)] * 3
        + [sds((SEQ, GATE_W), BF16), sds((1, GATE_W), F32)],
        compiler_params=_cp(), name="mix_bwd",
    )(dx1, o, cv, z, glog, glog, glog, b_gate, wbt, wbt, wbt, wmix, tie)


def _ffn_out_fwd(x1, gu, wout, tie):
    tt = 256

    def body(x_ref, gt_ref, up_ref, w_ref, tie_ref, o_ref):
        gt = gt_ref[...]
        act = (gt * jax.nn.sigmoid(gt) * up_ref[...]).astype(BF16)
        o_ref[...] = x_ref[...] + jnp.dot(act, w_ref[...], preferred_element_type=F32)

    return pl.pallas_call(
        body, grid=(SEQ // tt,),
        in_specs=[pl.BlockSpec((tt, D_MODEL), lambda i: (i, 0)), pl.BlockSpec((tt, FFN_H), lambda i: (i, 0)),
                  pl.BlockSpec((tt, FFN_H), lambda i: (i, 1)), _full((FFN_H, D_MODEL)), ANY],
        out_specs=pl.BlockSpec((tt, D_MODEL), lambda i: (i, 0)),
        out_shape=jax.ShapeDtypeStruct((SEQ, D_MODEL), F32), compiler_params=_cp(), name="ffn_out_fwd",
    )(x1, gu, gu, wout, tie)


def _ffn_out_bwd(dx2, gu, wout, tie):
    tt = 256

    def body(dx_ref, gt_ref, up_ref, w_ref, tie_ref, dgu_ref, act_ref):
        gt, up = gt_ref[...], up_ref[...]
        sg = jax.nn.sigmoid(gt)
        silu = gt * sg
        act_ref[...] = (silu * up).astype(BF16)
        dact = lax.dot_general(dx_ref[...].astype(BF16), w_ref[...], NT, preferred_element_type=F32)
        dgu_ref[:, :FFN_H] = (dact * up * (sg * (1.0 + gt * (1.0 - sg)))).astype(BF16)
        dgu_ref[:, FFN_H:] = (dact * silu).astype(BF16)

    return pl.pallas_call(
        body, grid=(SEQ // tt,),
        in_specs=[pl.BlockSpec((tt, D_MODEL), lambda i: (i, 0)), pl.BlockSpec((tt, FFN_H), lambda i: (i, 0)),
                  pl.BlockSpec((tt, FFN_H), lambda i: (i, 1)), _full((FFN_H, D_MODEL)), ANY],
        out_specs=[pl.BlockSpec((tt, 2 * FFN_H), lambda i: (i, 0)), pl.BlockSpec((tt, FFN_H), lambda i: (i, 0))],
        out_shape=[jax.ShapeDtypeStruct((SEQ, 2 * FFN_H), BF16), jax.ShapeDtypeStruct((SEQ, FFN_H), BF16)],
        compiler_params=_cp(), name="ffn_out_bwd",
    )(dx2, gu, gu, wout, tie)


def _loss_head(x, g, target):
    tt = 256

    def body(x_ref, g_ref, t_ref, loss_ref, dx_ref, dg_ref):
        @pl.when(pl.program_id(0) == 0)
        def _():
            loss_ref[...] = jnp.zeros_like(loss_ref)
            dg_ref[...] = jnp.zeros_like(dg_ref)

        xv = x_ref[...]
        r = lax.rsqrt(jnp.mean(xv * xv, axis=-1, keepdims=True) + NORM_EPS)
        xh = xv * r
        err = xh * g_ref[...] - t_ref[...]
        loss_ref[...] += 0.5 * jnp.sum(jnp.mean(err * err, axis=-1, keepdims=True))
        dy = err * (1.0 / D_MODEL)
        gy = dy * g_ref[...]
        dx_ref[...] = r * (gy - xh * jnp.mean(gy * xh, axis=-1, keepdims=True))
        dg_ref[...] += jnp.sum(dy * xh, axis=0, keepdims=True)

    row = pl.BlockSpec((tt, D_MODEL), lambda i: (i, 0))
    return pl.pallas_call(
        body, grid=(SEQ // tt,), in_specs=[row, _full((1, D_MODEL)), row],
        out_specs=[_full((1, 128)), row, _full((1, D_MODEL))],
        out_shape=[jax.ShapeDtypeStruct((1, 128), F32), jax.ShapeDtypeStruct((SEQ, D_MODEL), F32),
                   jax.ShapeDtypeStruct((1, D_MODEL), F32)],
        compiler_params=_cp(), name="loss_head")(x, g, target)


def _adam_math(g, w, m, v):
    nm = B1 * m + (1.0 - B1) * g
    nv = B2 * v + (1.0 - B2) * (g * g)
    m_hat = nm / (1.0 - B1 ** STEP)
    v_hat = nv / (1.0 - B2 ** STEP)
    return -LR * (m_hat / (jnp.sqrt(v_hat) + ADAM_EPS) + WD * w), nm, nv


def _adamw_small(parts, w, m, v, name):
    def body(p_ref, w_ref, m_ref, v_ref, g_ref, d_ref, nm_ref, nv_ref):
        g = p_ref[0].astype(F32)
        for k in range(1, N_DEV):
            g = g + p_ref[k].astype(F32)
        g_ref[...] = g
        d_ref[...], nm_ref[...], nv_ref[...] = _adam_math(g, w_ref[...], m_ref[...], v_ref[...])

    out_shape = [jax.ShapeDtypeStruct(w.shape, F32)] * 4
    if w.ndim < 3:
        return pl.pallas_call(body, out_shape=out_shape, name=name)(parts, w, m, v)
    rest = w.shape[1:]
    zeros = (0,) * len(rest)
    blk = pl.BlockSpec((None,) + rest, lambda l: (l,) + zeros)
    return pl.pallas_call(
        body, grid=(w.shape[0],),
        in_specs=[pl.BlockSpec((N_DEV, None) + rest, lambda l: (0, l) + zeros), blk, blk, blk],
        out_specs=[blk] * 4, out_shape=out_shape, name=name)(parts, w, m, v)


def _adamw(parts, w, m, v, tr, name, groups=None, fill=None, tie=None):
    n_groups, rows, cols = w.shape
    n_parts = parts.shape[1]
    lo, hi = groups if groups is not None else (0, n_groups)

    def body(p_ref, w_ref, m_ref, v_ref, *rest):
        g_ref, d_ref, nm_ref, nv_ref = rest[-4:]
        g = p_ref[0].astype(F32)
        for k in range(1, n_parts):
            g = g + p_ref[k].astype(F32)
        nm = B1 * m_ref[...] + (1.0 - B1) * g
        nv = B2 * v_ref[...] + (1.0 - B2) * (g * g)
        m_hat = nm / (1.0 - B1 ** STEP)
        v_hat = nv / (1.0 - B2 ** STEP)
        g_ref[...] = g
        d_ref[...] = -LR * (m_hat / (jnp.sqrt(v_hat) + ADAM_EPS) + WD * w_ref[...])
        nm_ref[...] = nm
        nv_ref[...] = nv

    blk = pl.BlockSpec((None, tr, cols), lambda l, i: (l + lo, i, 0))
    p_lo = lo if parts.shape[0] == n_groups else 0
    extra = ([] if fill is None else list(fill)) + ([] if tie is None else [tie])
    return pl.pallas_call(
        body, grid=(hi - lo, rows // tr),
        in_specs=[pl.BlockSpec((None, n_parts, tr, cols), lambda l, i: (l + p_lo, 0, i, 0)), blk, blk, blk]
        + [ANY] * len(extra),
        out_specs=[blk] * 4, out_shape=[jax.ShapeDtypeStruct((n_groups, rows, cols), F32)] * 4,
        input_output_aliases={} if fill is None else {4 + j: j for j in range(4)},
        compiler_params=_cp(), name=name)(parts, w, m, v, *extra)


def _split_start(name, arrays, n_sems, plan, after=None):
    n = len(arrays)
    order = [] if after is None else [after]
    n_in = n + len(order)

    def body(*refs):
        ins, send_sems, recv_sems, token = refs[:n], refs[n_in], refs[n_in + 1], refs[-1]
        for src, dst, k, to in plan(ins)[0]:
            pltpu.make_async_remote_copy(src_ref=src, dst_ref=dst, send_sem=send_sems.at[k], recv_sem=recv_sems.at[k],
                                         device_id=to, device_id_type=MESH_ID).start()
        token[...] = jnp.zeros_like(token)

    outs = pl.pallas_call(
        body, name=name,
        out_shape=(pltpu.SemaphoreType.DMA((n_sems,)), pltpu.SemaphoreType.DMA((n_sems,)),
                   *[pltpu.HBM(a.shape, a.dtype) for a in arrays], jax.ShapeDtypeStruct((8, 128), F32)),
        in_specs=[HBM] * n + [ANY] * len(order),
        out_specs=(SEM, SEM, *[HBM] * n, pl.BlockSpec(memory_space=pltpu.VMEM)),
        input_output_aliases={i: 2 + i for i in range(n)},
        compiler_params=pltpu.CompilerParams(has_side_effects=EFFECT),
    )(*[pltpu.with_memory_space_constraint(a, pltpu.HBM) for a in arrays], *order)
    return outs[0], outs[1], list(outs[2:2 + n]), outs[-1]


def _split_wait(name, arrays, send_sems, recv_sems, after, plan):
    n = len(arrays)

    def body(*refs):
        ins, s_sems, r_sems = refs[:n], refs[n], refs[n + 1]
        sends, arrivals = plan(ins)
        x, y, c = lax.axis_index("x"), lax.axis_index("y"), lax.axis_index("c")
        for src, dst, k, to in sends:
            pltpu.make_async_remote_copy(src_ref=src, dst_ref=dst, send_sem=s_sems.at[k], recv_sem=r_sems.at[k],
                                         device_id=to, device_id_type=MESH_ID).wait_send()
        for dst, k in arrivals:
            pltpu.make_async_remote_copy(src_ref=dst, dst_ref=dst, send_sem=s_sems.at[k], recv_sem=r_sems.at[k],
                                         device_id=(x, y, c), device_id_type=MESH_ID).wait_recv()

    return pl.pallas_call(
        body, name=name, out_shape=[pltpu.HBM(a.shape, a.dtype) for a in arrays],
        in_specs=[HBM] * n + [SEM, SEM, ANY], out_specs=[HBM] * n,
        input_output_aliases={i: i for i in range(n)},
        compiler_params=pltpu.CompilerParams(has_side_effects=EFFECT),
    )(*arrays, send_sems, recv_sems, after)


def _chips():
    x, y, c = lax.axis_index("x"), lax.axis_index("y"), lax.axis_index("c")
    return x, y, c, [(1 - x, y), (x, 1 - y), (1 - x, 1 - y)]


def _plan_gather_chips(refs):
    x, y, c, chips = _chips()
    me = 4 * x + 2 * y + c
    n = len(refs) // 2
    sends, arrivals = [], []
    for i in range(n):
        src, land = refs[i], refs[n + i]
        sends.append((src, land.at[me], 4 * i, (x, y, 1 - c)))
        arrivals.append((land.at[4 * x + 2 * y + 1 - c], 4 * i))
        for j, (px, py) in enumerate(chips):
            sends.append((src, land.at[me], 4 * i + 1 + j, (px, py, c)))
            arrivals.append((land.at[4 * px + 2 * py + c], 4 * i + 1 + j))
    return sends, arrivals


def _plan_gather_pass(refs):
    x, y, c, chips = _chips()
    sends, arrivals = [], []
    for i in range(len(refs)):
        for j, (px, py) in enumerate(chips):
            slot = refs[i].at[4 * px + 2 * py + c]
            sends.append((slot, slot, 4 * i + j, (x, y, 1 - c)))
            arrivals.append((refs[i].at[4 * px + 2 * py + 1 - c], 4 * i + j))
        back = refs[i].at[4 * x + 2 * y + 1 - c]
        sends.append((back, back, 4 * i + 3, (x, y, 1 - c)))
        arrivals.append((refs[i].at[4 * x + 2 * y + c], 4 * i + 3))
    return sends, arrivals


def _plan_scatter_pair(refs):
    x, y, c = lax.axis_index("x"), lax.axis_index("y"), lax.axis_index("c")
    n = len(refs) // 2
    sends, arrivals = [], []
    for i in range(n):
        for q in range(4):
            sends.append((refs[i].at[q, 1 - c], refs[n + i].at[q], 4 * i + q, (x, y, 1 - c)))
            arrivals.append((refs[n + i].at[q], 4 * i + q))
    return sends, arrivals


def _plan_scatter_chips(layer):
    def plan(refs):
        x, y, c, chips = _chips()
        n = len(refs) // 2
        sends, arrivals = [], []
        for i in range(n):
            for j, (px, py) in enumerate(chips):
                sends.append((refs[i].at[2 * px + py], refs[n + i].at[layer, 2 * x + y], 3 * i + j, (px, py, c)))
                arrivals.append((refs[n + i].at[layer, 2 * px + py], 3 * i + j))
        return sends, arrivals

    return plan


def _pair_sum(parts4, from_pair, landing, layer, core, tr, name):
    _, _, rows, cols = parts4.shape

    def body(c_ref, p_ref, s_ref, l_ref, sum_ref, land_ref):
        v = (p_ref[...].astype(F32) + s_ref[...].astype(F32)).astype(BF16)
        sum_ref[...] = v
        land_ref[...] = v

    blk = pl.BlockSpec((None, tr, cols), lambda q, i, c_ref: (q, i, 0))
    return pl.pallas_call(
        body,
        grid_spec=pltpu.PrefetchScalarGridSpec(
            num_scalar_prefetch=1, grid=(4, rows // tr),
            in_specs=[pl.BlockSpec((None, None, tr, cols), lambda q, i, c_ref: (q, c_ref[0], i, 0)), blk, ANY],
            out_specs=[blk, pl.BlockSpec((None, None, tr, cols), lambda q, i, c_ref: (layer, q, i, 0))]),
        out_shape=[jax.ShapeDtypeStruct((4, rows, cols), BF16), jax.ShapeDtypeStruct(landing.shape, BF16)],
        input_output_aliases={3: 1}, compiler_params=_cp(), name=name,
    )(core, parts4, from_pair, landing)


def _travel_layout(t):
    tr = lambda a: jnp.swapaxes(a, 1, 2)
    branch = jnp.concatenate([tr(t["w_attn_o"]), tr(t["w_conv_o"]), tr(t["w_ssm_o"])], axis=2)
    return [tr(t["w_in"]), tr(t["w_ffn_in"]), t["w_ffn_out"], t["w_mix_o"], branch, t["w_ssm_glu"]]


def _native_layout(a):
    tr = lambda x: jnp.swapaxes(x, 1, 2)
    b = a[4]
    return {"w_in": tr(a[0]), "w_ffn_in": tr(a[1]), "w_ffn_out": a[2], "w_mix_o": a[3],
            "w_attn_o": tr(b[:, :, :WIDTH]), "w_conv_o": tr(b[:, :, WIDTH:2 * WIDTH]),
            "w_ssm_o": tr(b[:, :, 2 * WIDTH:]), "w_ssm_glu": a[5]}


def _embed(t):
    eye = jnp.eye(8, dtype=t.dtype)
    t = t.reshape(DEPTH, N_LANE_GROUPS, 8, SSM_GROUP, SSM_STATE)
    return (t[:, :, :, :, None, :] * eye[None, None, :, None, :, None]).reshape(DEPTH, N_LANE_GROUPS, 128, LANES_G)


def _diag_blocks(t):
    t = t.reshape(DEPTH, N_LANE_GROUPS, 8, SSM_GROUP, 8, SSM_STATE)
    return jnp.einsum("lgahap->lgahp", t).reshape(DEPTH, SSM_GROUPS, SSM_GROUP, SSM_STATE)


def _rope_tabs():
    pos = jnp.arange(SEQ, dtype=F32)
    inv_freq = ROPE_THETA ** (-jnp.arange(0, ROT_DIM, 2, dtype=F32) / ROT_DIM)
    ang = pos[:, None] * inv_freq[None, :]
    cos, sin = jnp.cos(ang), jnp.sin(ang)
    one, zero = jnp.ones((SEQ, HEAD_DIM - ROT_DIM), F32), jnp.zeros((SEQ, HEAD_DIM - ROT_DIM), F32)
    z8 = jnp.zeros((SEQ, 8), F32)
    head = lambda *p: jnp.tile(jnp.concatenate(p, axis=1), (1, 2))
    return head(cos, cos, one), head(-sin, z8, zero), head(z8, sin, zero)


def _ssm_mats(sp):
    lr, li, bbr, bbi = _ssm_prep(sp["a_re"], sp["a_im"], sp["log_dt"], sp["bt_re"], sp["bt_im"])
    lanes = SSM_GROUPS * SSM_STATE
    return {
        "a_re": lr.reshape(DEPTH, 1, lanes), "a_im": li.reshape(DEPTH, 1, lanes),
        "b_re": _embed(bbr).astype(BF16), "b_im": _embed(bbi).astype(BF16),
        "c_re": _embed(sp["c_re"]).astype(BF16), "c_im_neg": _embed(-sp["c_im"]).astype(BF16),
    }


def _layer_fwd(x, i, w, rp, mats, tabs, tie, hooks):
    q, kv, cbx, u, glog, h = _rms_mm_in(x, rp["norm_mix"][i], w["win_t"], tie)
    o = _attn_fwd(q, kv, tabs, rp["attn_sinks"][i])
    cv = _conv_fwd(cbx, rp["conv_w"], i)
    u = _scan_order(u)
    x_re, x_im, y = _ssm_fwd(u, mats, i, rp["ssm_d"])
    z = _time_order(_glu_fwd(y, w["wglu"]))
    x1 = _mix_fwd(x, o, cv, z, glog, rp["b_gate"], i, w["branch_t"], w["wmix"], hooks["early"](z))
    hooks["pre_ffn"](x1)
    gu, h2 = _rms_mm_ffn(x1, rp["norm_ffn"][i], w["wffn_t"])
    x2 = _ffn_out_fwd(x1, gu, w["wout"], hooks["mid"](h2))
    kept = dict(x=x, q=q, kv=kv, cbx=cbx, u=u, glog=glog, h=h, o=o, cv=cv, z=z, y=y,
                x_re=x_re, x_im=x_im, x1=x1, gu=gu, h2=h2)
    return x2, kept


def _layer_bwd(dx2, k, i, w, rp, mats, tabs, tie, hooks):
    dgu, act = _ffn_out_bwd(dx2, k["gu"], w["wout"], tie)
    g_wout = _mm_tn(act, dx2, tm=FFN_H // 2, tn=1024, name="mm_tn_ffn_out")
    g_wffn_t = _mm_tn(dgu, k["h2"], tm=FFN_H // 2, tn=1024, name="mm_tn_ffn_in")
    dx1, d_norm_ffn = _mm_rmsbwd([dgu], w["wffn_t"], k["x1"], rp["norm_ffn"][i], dx2, "mm_rmsbwd_ffn")

    mg, dya, dyc, dys, do, dcv, dz, dgl, db_gate = _mix_bwd(
        dx1, k["o"], k["cv"], k["z"], k["glog"], rp["b_gate"], i, w["branch_t"], w["wmix"],
        hooks["mid"]((g_wffn_t, g_wout, d_norm_ffn)))
    g_wmix = _mm_tn(mg, dx1, tm=1024, tn=512, name="mm_tn_mix")
    g_branch_t = _tn_branches((dya, dyc, dys), (k["o"], k["cv"], k["z"]))

    dy16, ys16, da16, dd = _glu_bwd(k["y"], w["wglu"], _scan_order(dz), k["u"])
    g_wglu = _mm_tn(ys16, da16, tm=256, tn=512, name="mm_tn_glu")
    du, da_re, da_im, db_re, db_im, dc_re, dc_im = _ssm_bwd(dy16, k["x_re"], k["x_im"], k["u"], mats, i,
                                                             rp["ssm_d"])
    du = _time_order(du)

    dcb, dcc, dcx, d_conv_w = _conv_bwd(k["cbx"], rp["conv_w"], i, dcv, hooks["late"](du))
    dq, dkv, d_sinks = _attn_bwd(k["q"], k["kv"], tabs, rp["attn_sinks"][i], do)

    pieces = [dq, dkv, dcb, dcc, dcx, du, dgl]
    g_win_t = _tn_pieces(pieces, k["h"])
    dx, d_norm_mix = _mm_rmsbwd(pieces, w["win_t"], k["x"], rp["norm_mix"][i], dx1, "mm_rmsbwd_in")

    grads = [g_win_t, g_wffn_t, g_wout, g_wmix, g_branch_t, g_wglu]
    small = dict(norm_mix=d_norm_mix, b_gate=db_gate, attn_sinks=d_sinks, ssm_d=dd, norm_ffn=d_norm_ffn,
                 conv_w=d_conv_w, da_re=da_re, da_im=da_im, db_re=db_re, db_im=db_im, dc_re=dc_re, dc_im=dc_im)
    return dx, grads, small


def _replicated_grads(sg, sp):
    stack = lambda name: jnp.stack([sg[i][name] for i in range(DEPTH)])
    cots = (stack("da_re").reshape(DEPTH, *_GS), stack("da_im").reshape(DEPTH, *_GS),
            _diag_blocks(stack("db_re")), _diag_blocks(stack("db_im")))
    d_a_re, d_a_im, d_log_dt, d_bt_re, d_bt_im = _ssm_prep_bwd(
        sp["a_re"], sp["a_im"], sp["log_dt"], sp["bt_re"], sp["bt_im"], cots)
    sgrads = {"norm_mix": stack("norm_mix"), "b_gate": stack("b_gate"),
              "attn_sinks": stack("attn_sinks")[:, :, :N_Q_HEADS], "ssm_a_re": d_a_re, "ssm_a_im": d_a_im,
              "ssm_b_re": jnp.swapaxes(d_bt_re, 2, 3), "ssm_b_im": jnp.swapaxes(d_bt_im, 2, 3),
              "ssm_c_re": _diag_blocks(stack("dc_re")), "ssm_c_im": -_diag_blocks(stack("dc_im")),
              "ssm_d": stack("ssm_d"), "ssm_log_dt": d_log_dt, "norm_ffn": stack("norm_ffn")}
    return sgrads, stack("conv_w")[:, :3]


def kernel(x, norm_mix, w_in, b_gate, attn_sinks, w_attn_o, conv_w, w_conv_o, ssm_a_re, ssm_a_im, ssm_b_re, ssm_b_im, ssm_c_re, ssm_c_im, ssm_d, ssm_log_dt, w_ssm_glu, w_ssm_o, w_mix_o, norm_ffn, w_ffn_in, w_ffn_out, norm_final, loss_target, m_norm_mix, m_w_in, m_b_gate, m_attn_sinks, m_w_attn_o, m_conv_w, m_w_conv_o, m_ssm_a_re, m_ssm_a_im, m_ssm_b_re, m_ssm_b_im, m_ssm_c_re, m_ssm_c_im, m_ssm_d, m_ssm_log_dt, m_w_ssm_glu, m_w_ssm_o, m_w_mix_o, m_norm_ffn, m_w_ffn_in, m_w_ffn_out, m_norm_final, v_norm_mix, v_w_in, v_b_gate, v_attn_sinks, v_w_attn_o, v_conv_w, v_w_conv_o, v_ssm_a_re, v_ssm_a_im, v_ssm_b_re, v_ssm_b_im, v_ssm_c_re, v_ssm_c_im, v_ssm_d, v_ssm_log_dt, v_w_ssm_glu, v_w_ssm_o, v_w_mix_o, v_norm_ffn, v_w_ffn_in, v_w_ffn_out, v_norm_final):
    big = {"w": dict(w_in=w_in, w_attn_o=w_attn_o, w_conv_o=w_conv_o, w_ssm_glu=w_ssm_glu, w_ssm_o=w_ssm_o,
                     w_mix_o=w_mix_o, w_ffn_in=w_ffn_in, w_ffn_out=w_ffn_out),
           "m": dict(w_in=m_w_in, w_attn_o=m_w_attn_o, w_conv_o=m_w_conv_o, w_ssm_glu=m_w_ssm_glu,
                     w_ssm_o=m_w_ssm_o, w_mix_o=m_w_mix_o, w_ffn_in=m_w_ffn_in, w_ffn_out=m_w_ffn_out),
           "v": dict(w_in=v_w_in, w_attn_o=v_w_attn_o, w_conv_o=v_w_conv_o, w_ssm_glu=v_w_ssm_glu,
                     w_ssm_o=v_w_ssm_o, w_mix_o=v_w_mix_o, w_ffn_in=v_w_ffn_in, w_ffn_out=v_w_ffn_out)}
    small = {"w": dict(norm_mix=norm_mix, b_gate=b_gate, attn_sinks=attn_sinks, ssm_a_re=ssm_a_re,
                       ssm_a_im=ssm_a_im, ssm_b_re=ssm_b_re, ssm_b_im=ssm_b_im, ssm_c_re=ssm_c_re,
                       ssm_c_im=ssm_c_im, ssm_d=ssm_d, ssm_log_dt=ssm_log_dt, norm_ffn=norm_ffn),
             "m": dict(norm_mix=m_norm_mix, b_gate=m_b_gate, attn_sinks=m_attn_sinks, ssm_a_re=m_ssm_a_re,
                       ssm_a_im=m_ssm_a_im, ssm_b_re=m_ssm_b_re, ssm_b_im=m_ssm_b_im, ssm_c_re=m_ssm_c_re,
                       ssm_c_im=m_ssm_c_im, ssm_d=m_ssm_d, ssm_log_dt=m_ssm_log_dt, norm_ffn=m_norm_ffn),
             "v": dict(norm_mix=v_norm_mix, b_gate=v_b_gate, attn_sinks=v_attn_sinks, ssm_a_re=v_ssm_a_re,
                       ssm_a_im=v_ssm_a_im, ssm_b_re=v_ssm_b_re, ssm_b_im=v_ssm_b_im, ssm_c_re=v_ssm_c_re,
                       ssm_c_im=v_ssm_c_im, ssm_d=v_ssm_d, ssm_log_dt=v_ssm_log_dt, norm_ffn=v_norm_ffn)}
    finals = {"w": norm_final, "m": m_norm_final, "v": v_norm_final}
    small_out_shapes = {name: a.shape for name, a in small["w"].items()}
    small_out_shapes.update(norm_final=(D_MODEL,), conv_w=(DEPTH, 3, 64))
    small_shapes = dict(small_out_shapes, norm_final=(1, D_MODEL), conv_w=(DEPTH, 3, WIDTH))
    for name in ("ssm_b_re", "ssm_b_im", "ssm_c_re", "ssm_c_im"):
        small_shapes[name] = (DEPTH, SSM_GROUPS, SSM_GROUP * SSM_STATE)
    convs = {"w": conv_w, "m": m_conv_w, "v": v_conv_w}
    mine = 4 * lax.axis_index("x") + 2 * lax.axis_index("y") + lax.axis_index("c")

    travel = {s: _travel_layout(big[s]) for s in "wmv"}
    stacked16 = list(zip(*[[a[0] for a in _travel_layout({n: w[i:i + 1].astype(BF16) for n, w in big["w"].items()})]
                           for i in range(DEPTH)]))
    rp = {"norm_mix": norm_mix[:, None], "norm_ffn": norm_ffn[:, None], "attn_sinks": attn_sinks[:, None],
          "b_gate": b_gate[:, None], "ssm_d": ssm_d[:, None]}
    sp = {"a_re": ssm_a_re, "a_im": ssm_a_im, "log_dt": ssm_log_dt[:, :, None],
          "bt_re": jnp.swapaxes(ssm_b_re, 2, 3), "bt_im": jnp.swapaxes(ssm_b_im, 2, 3),
          "c_re": ssm_c_re, "c_im": ssm_c_im}
    rows_tile = {"win_t": 368, "wffn_t": 352, "wout": 176, "wmix": 128, "branch_t": 128, "wglu": 64}
    core = lax.axis_index("c").astype(jnp.int32).reshape(1)
    no_tie = jnp.zeros((8, 128), F32)

    def place_own(srcs):
        return [lax.empty((N_DEV,) + s.shape, s.dtype) for s in srcs]

    def gather_chips(tag, i, kinds, after, extra=()):
        srcs = [stacked16[j][i] for j in kinds] + list(extra)
        s_sems, r_sems, arrays, token = _split_start(
            f"gather_chips_start_{tag}", srcs + place_own(srcs), 4 * len(srcs), _plan_gather_chips, after)
        return (tag, s_sems, r_sems, arrays), token

    def gather_pass(state, after):
        tag, s_sems, r_sems, arrays = state
        arrays = _split_wait(f"gather_chips_wait_{tag}", arrays, s_sems, r_sems, after, _plan_gather_chips)
        n = len(arrays) // 2
        s_sems, r_sems, lands, token = _split_start(
            f"gather_pass_start_{tag}", list(arrays[n:]), 4 * n, _plan_gather_pass)
        return (tag, s_sems, r_sems, lands), token

    def gather_done(state, after, kinds):
        tag, s_sems, r_sems, lands = state
        lands = _split_wait(f"gather_pass_wait_{tag}", lands, s_sems, r_sems, after, _plan_gather_pass)
        named = {KINDS[j][0]: a.reshape(N_DEV * KINDS[j][1], KINDS[j][2]) for a, j in zip(lands, kinds)}
        return named, list(lands[len(kinds):])

    all_kinds, mixer_kinds, ffn_kinds = tuple(range(len(KINDS))), (0, 3, 4, 5), (1, 2)
    no_hooks = {name: (lambda value: no_tie) for name in ("early", "pre_ffn", "mid", "late")}
    state, token = gather_chips("0m", 0, mixer_kinds, None, extra=[jnp.pad(conv_w.reshape(6, 128), ((0, 2), (0, 0)))])
    mats = _ssm_mats(dict(sp, log_dt=sp["log_dt"] + token[0, 0]))
    tabs = _rope_tabs()
    ready = sum(a.reshape(-1)[:1].astype(F32) for a in list(mats.values()) + list(tabs))
    state, _ = gather_pass(state, ready)
    ffn_state, tie = gather_chips("0f", 0, ffn_kinds, state[3][0])
    w_next, (conv_all,) = gather_done(state, tabs[2], mixer_kinds)
    conv_full = conv_all[:, :6].reshape(N_DEV, DEPTH, 3, 64).transpose(1, 2, 0, 3).reshape(DEPTH, 3, WIDTH)
    rp["conv_w"] = jnp.pad(conv_full, ((0, 0), (0, 5), (0, 0)))

    act = x[0]
    weights, kept = [], []
    for i in range(DEPTH):
        w_i, hooks, held = w_next, dict(no_hooks), {}

        def early(value, ffn_state=ffn_state, held=held):
            held["ffn"], token = gather_pass(ffn_state, value)
            return token

        def pre_ffn(value, w_i=w_i, held=held):
            w_i.update(gather_done(held["ffn"], value, ffn_kinds)[0])

        hooks.update(early=early, pre_ffn=pre_ffn)
        if i + 1 < DEPTH:
            state, tie = gather_chips(f"{i + 1}m", i + 1, mixer_kinds, tie if i == 0 else w_i["win_t"])

            def mid(value, i=i, state=state, held=held):
                held["next"], token = gather_pass(state, value)
                held["next_ffn"], token = gather_chips(f"{i + 1}f", i + 1, ffn_kinds, token)
                return token

            hooks.update(mid=mid)
        act, k = _layer_fwd(act, i, w_i, rp, mats, tabs, tie, hooks)
        if i + 1 < DEPTH:
            w_next, _ = gather_done(held["next"], act, mixer_kinds)
            ffn_state, tie = held["next_ffn"], no_tie
        weights.append(w_i)
        kept.append(k)
    loss_row, dx, d_norm_final = _loss_head(act, norm_final[None], loss_target[0])
    loss = lax.psum(loss_row[0, 0], ("x", "y", "c"))

    landings = [lax.empty((DEPTH, 4, r, c), BF16) for _, r, c in KINDS]
    landings0 = [lax.empty((1, 4, r, c), BF16) for _, r, c in KINDS]

    def scatter_pair(tag, kinds, grads, after):
        parts4 = [g.reshape(4, 2, KINDS[j][1], KINDS[j][2]) for g, j in zip(grads, kinds)]
        zones = [lax.empty((4, KINDS[j][1], KINDS[j][2]), BF16) for j in kinds]
        s_sems, r_sems, arrays, token = _split_start(
            f"scatter_pair_start_{tag}", parts4 + zones, 4 * len(kinds), _plan_scatter_pair, after)
        return (tag, kinds, s_sems, r_sems, arrays), token

    def scatter_chips(state, lands, slot, after):
        tag, kinds, s_sems, r_sems, arrays = state
        arrays = _split_wait(f"scatter_pair_wait_{tag}", arrays, s_sems, r_sems, after, _plan_scatter_pair)
        n = len(kinds)
        sums, mine_lands = [], []
        for k, j in enumerate(kinds):
            name = KINDS[j][0]
            chip_sum, land = _pair_sum(arrays[k], arrays[n + k], lands[j], slot, core, rows_tile[name],
                                       f"pair_sum_{name}")
            sums.append(chip_sum)
            mine_lands.append(land)
        s_sems, r_sems, arrays, token = _split_start(
            f"scatter_chips_start_{tag}", sums + mine_lands, 3 * n, _plan_scatter_chips(slot))
        return (tag, kinds, slot, s_sems, r_sems, arrays), token

    def scatter_done(state, lands, after):
        tag, kinds, slot, s_sems, r_sems, arrays = state
        arrays = _split_wait(f"scatter_chips_wait_{tag}", arrays, s_sems, r_sems, after, _plan_scatter_chips(slot))
        lands = list(lands)
        for k, j in enumerate(kinds):
            lands[j] = arrays[len(kinds) + k]
        return lands

    sg = [None] * DEPTH
    pending, tie = None, no_tie
    for i in reversed(range(DEPTH)):
        hooks, held = dict(no_hooks), {}
        if pending is not None:
            def mid(value, i=i, pending=pending, held=held):
                held["chips"], token = scatter_chips(pending, landings, i + 1, value[2])
                if i == 0:
                    held["ffn_pair"], token = scatter_pair("0f", ffn_kinds, value[:2], token)
                return token

            hooks.update(mid=mid)
        if i == 0:
            def late(value, held=held):
                held["ffn_chips"], token = scatter_chips(held["ffn_pair"], landings0, 0, value)
                return token

            hooks.update(late=late)
        dx, grads, sg[i] = _layer_bwd(dx, kept[i], i, weights[i], rp, mats, tabs, tie, hooks)
        if pending is not None:
            landings = scatter_done(held["chips"], landings, dx)
        if i > 0:
            pending, tie = scatter_pair(str(i), all_kinds, grads, dx)
        else:
            pending, _ = scatter_pair("0m", mixer_kinds, [grads[j] for j in mixer_kinds], dx)

    sgrads, conv_grad = _replicated_grads(sg, sp)

    small_names = [name for name, _ in SMALL] + ["norm_final", "conv_w"]
    sgrads.update(norm_final=d_norm_final, conv_w=conv_grad)
    small_src = [sgrads[name].reshape(small_shapes[name]).astype(BF16) for name in small_names]
    last, tie = scatter_chips(pending, landings0, 0, small_src[0])
    s_sems, r_sems, arrays, tie = _split_start(
        "gather_small_chips_start", small_src + place_own(small_src), 4 * len(small_src), _plan_gather_chips, tie)
    small_state = ("small", s_sems, r_sems, arrays)

    big_out = []
    for j, (name, _, _) in enumerate(KINDS):
        big_out.append(_adamw(landings[j], travel["w"][j], travel["m"][j], travel["v"][j], rows_tile[name],
                              "adamw_late_" + name, groups=(1, DEPTH), tie=tie))
        tie = big_out[-1][3]
    landings0 = scatter_done(held["ffn_chips"], landings0, tie)
    landings0 = scatter_done(last, landings0, tie)
    small_state, _ = gather_pass(small_state, landings0[0])
    big_out = [_adamw(landings0[j], travel["w"][j], travel["m"][j], travel["v"][j], rows_tile[name],
                      "adamw_first_" + name, groups=(0, 1), fill=big_out[j]) for j, (name, _, _) in enumerate(KINDS)]
    big_res = [_native_layout([big_out[j][kind] for j in range(len(KINDS))]) for kind in range(4)]

    _, sparts = gather_done(small_state, big_out[-1][0], ())
    sparts = dict(zip(small_names, sparts))
    sparts["conv_w"] = lax.dynamic_slice_in_dim(sparts["conv_w"], mine * 64, 64, axis=3)
    small_res = {}
    for name in small_names:
        shape = small_shapes[name] if name != "conv_w" else (DEPTH, 3, 64)
        state = [(convs[s] if name == "conv_w" else finals[s] if name == "norm_final" else small[s][name])
                 .reshape(shape) for s in "wmv"]
        res = _adamw_small(sparts[name], *state, "adamw_" + name)
        small_res[name] = [r.reshape(state_shape) for r, state_shape in zip(res, [small_out_shapes[name]] * 4)]

    order = ["norm_mix", "w_in", "b_gate", "attn_sinks", "w_attn_o", "conv_w", "w_conv_o", "ssm_a_re", "ssm_a_im",
             "ssm_b_re", "ssm_b_im", "ssm_c_re", "ssm_c_im", "ssm_d", "ssm_log_dt", "w_ssm_glu", "w_ssm_o",
             "w_mix_o", "norm_ffn", "w_ffn_in", "w_ffn_out", "norm_final"]
    outs = [loss, dx[None]]
    for kind in range(4):
        for name in order:
            outs.append(big_res[kind][name] if name in big_res[kind] else small_res[name][kind])
    return tuple(outs)
```

```python
import functools
import math

import jax
import jax.numpy as jnp
from jax import lax
from jax.experimental import pallas as pl
from jax.experimental.pallas import tpu as pltpu

F32 = jnp.float32
BF16 = jnp.bfloat16

N_DEV = 8
DEPTH = 4
SEQ = 2048
D_MODEL = 1024
N_Q_HEADS = 8
HEAD_DIM = 64
ATTN_W = 512
KV_W = 128
BLOCK = 128
N_BLOCKS = SEQ // BLOCK
ROPE_THETA = 500000.0
ROT_DIM = 16
NEG_INF = -1e30
WIDTH = 512
SSM_GROUPS = 32
SSM_GROUP = 16
SSM_STATE = 64
SLABS = 16
CHUNK = 256
N_CHUNKS = SEQ // CHUNK
GATE_W = 3 * D_MODEL
IN_COLS = 5888
FFN_H = 2816
NORM_EPS = 1e-6
LR, B1, B2, ADAM_EPS, WD, STEP = 0.001, 0.9, 0.999, 1e-08, 0.01, 10

COL_Q, COL_KV, COL_CBX, COL_U, COL_G = 0, 512, 768, 2304, 2816
PIECE_W = (512, 256, 512, 512, 512, 512, 3072)
PIECE_OFF = tuple(sum(PIECE_W[:i]) for i in range(len(PIECE_W)))

KINDS = (("win_t", 736, 1024), ("wffn_t", 704, 1024), ("wout", 352, 1024), ("wmix", 128, 1024),
         ("branch_t", 128, 1536), ("wglu", 64, 512))

SMALL = (("norm_mix", 1024), ("b_gate", 3072), ("attn_sinks", 8), ("ssm_a_re", 2048), ("ssm_a_im", 2048),
         ("ssm_b_re", 32768), ("ssm_b_im", 32768), ("ssm_c_re", 32768), ("ssm_c_im", 32768),
         ("ssm_d", 512), ("ssm_log_dt", 32), ("norm_ffn", 1024))
SMALL_PER_LAYER = sum(n for _, n in SMALL)
CONV_N = DEPTH * 3 * WIDTH
SMALL_ROWS = 4480

VMEM_LIMIT = 56 * 1024 * 1024
NT = (((1,), (1,)), ((), ()))
TN = (((0,), (0,)), ((), ()))
MESH_ID = pl.DeviceIdType.MESH
ANY = pl.BlockSpec(memory_space=pl.ANY)
HBM = pl.BlockSpec(memory_space=pltpu.HBM)
SEM = pl.BlockSpec(memory_space=pltpu.SEMAPHORE)
EFFECT = pltpu.SideEffectType.DATAFLOW_SIDE_EFFECTING


def _cp(**kw):
    return pltpu.CompilerParams(vmem_limit_bytes=VMEM_LIMIT, **kw)


def _full(shape):
    return pl.BlockSpec(shape, lambda *_: (0,) * len(shape))


def _resident(shape):
    return pl.BlockSpec(shape, lambda *_: (0,) * len(shape), pipeline_mode=pl.Buffered(1))


def _mm_tn(a, b, *, tm, tn, name):
    k, m = a.shape
    n = b.shape[1]

    def body(a_ref, b_ref, o_ref):
        o_ref[...] = lax.dot_general(a_ref[...].astype(BF16), b_ref[...].astype(BF16), TN,
                                     preferred_element_type=F32).astype(BF16)

    return pl.pallas_call(
        body, grid=(m // tm, n // tn),
        in_specs=[pl.BlockSpec((k, tm), lambda i, j: (0, i)), pl.BlockSpec((k, tn), lambda i, j: (0, j))],
        out_specs=pl.BlockSpec((tm, tn), lambda i, j: (i, j)),
        out_shape=jax.ShapeDtypeStruct((m, n), BF16), compiler_params=_cp(), name=name)(a, b)


def _rms_rows(xv, g):
    r = lax.rsqrt(jnp.mean(xv * xv, axis=-1, keepdims=True) + NORM_EPS)
    return ((xv * r) * g).astype(BF16)


def _rms_mm_in(x, g, wt, tie):
    tt = 512
    widths = (ATTN_W, 2 * KV_W, 3 * WIDTH, WIDTH, GATE_W)
    offs = (COL_Q, COL_KV, COL_CBX, COL_U, COL_G)

    def body(x_ref, g_ref, w_ref, tie_ref, q_ref, kv_ref, cbx_ref, u_ref, gl_ref, h_ref):
        h = _rms_rows(x_ref[...], g_ref[...])
        h_ref[...] = h
        prod = lax.dot_general(h, w_ref[...], NT, preferred_element_type=F32)
        for ref, o, w in zip((q_ref, kv_ref, cbx_ref, u_ref, gl_ref), offs, widths):
            ref[...] = prod[:, o:o + w]

    row = lambda w: pl.BlockSpec((tt, w), lambda i: (i, 0))
    sds = jax.ShapeDtypeStruct
    return pl.pallas_call(
        body, grid=(SEQ // tt,), in_specs=[row(D_MODEL), _full((1, D_MODEL)), _resident((IN_COLS, D_MODEL)), ANY],
        out_specs=[row(ATTN_W), row(2 * KV_W), row(3 * WIDTH), row(WIDTH), row(GATE_W), row(D_MODEL)],
        out_shape=[sds((SEQ, ATTN_W), F32), sds((SEQ, 2 * KV_W), F32), sds((SEQ, 3 * WIDTH), F32),
                   sds((SEQ, WIDTH), F32), sds((SEQ, GATE_W), F32), sds((SEQ, D_MODEL), BF16)],
        compiler_params=_cp(), name="rms_mm_in")(x, g, wt, tie)


def _rms_mm_ffn(x, g, wt):
    tt = 512

    def body(x_ref, g_ref, w_ref, o_ref, h_ref):
        h = _rms_rows(x_ref[...], g_ref[...])
        h_ref[...] = h
        o_ref[...] = lax.dot_general(h, w_ref[...], NT, preferred_element_type=F32)

    row = lambda w: pl.BlockSpec((tt, w), lambda i: (i, 0))
    return pl.pallas_call(
        body, grid=(SEQ // tt,), in_specs=[row(D_MODEL), _full((1, D_MODEL)), _resident((2 * FFN_H, D_MODEL))],
        out_specs=[row(2 * FFN_H), row(D_MODEL)],
        out_shape=[jax.ShapeDtypeStruct((SEQ, 2 * FFN_H), F32), jax.ShapeDtypeStruct((SEQ, D_MODEL), BF16)],
        compiler_params=_cp(), name="rms_mm_ffn")(x, g, wt)


def _mm_rmsbwd(pieces, wt, x, g, dres, name):
    tt = 512
    widths = [p.shape[1] for p in pieces]
    offs = [sum(widths[:i]) for i in range(len(widths))]
    n = len(pieces)

    def body(*refs):
        p_refs, (w_ref, x_ref, g_ref, r_ref, dx_ref, dg_ref) = refs[:n], refs[n:]

        @pl.when(pl.program_id(0) == 0)
        def _():
            dg_ref[...] = jnp.zeros_like(dg_ref)

        dh = jnp.zeros((tt, D_MODEL), F32)
        for p_ref, o, w in zip(p_refs, offs, widths):
            dh += jnp.dot(p_ref[...], w_ref[o:o + w, :], preferred_element_type=F32)
        xv = x_ref[...]
        r = lax.rsqrt(jnp.mean(xv * xv, axis=-1, keepdims=True) + NORM_EPS)
        xh = xv * r
        gy = dh * g_ref[...]
        dx_ref[...] = r_ref[...] + r * (gy - xh * jnp.mean(gy * xh, axis=-1, keepdims=True))
        dg_ref[...] += jnp.sum(dh * xh, axis=0, keepdims=True)

    row = lambda w: pl.BlockSpec((tt, w), lambda i: (i, 0))
    return pl.pallas_call(
        body, grid=(SEQ // tt,),
        in_specs=[row(w) for w in widths] + [_resident(wt.shape), row(D_MODEL), _full((1, D_MODEL)), row(D_MODEL)],
        out_specs=[row(D_MODEL), _full((1, D_MODEL))],
        out_shape=[jax.ShapeDtypeStruct((SEQ, D_MODEL), F32), jax.ShapeDtypeStruct((1, D_MODEL), F32)],
        compiler_params=_cp(), name=name)(*pieces, wt, x, g, dres)


def _tn_pieces(pieces, h):
    tk, tn = 512, 512
    nk = SEQ // tk
    n = len(pieces)

    def body(*refs):
        p_refs, (h_ref, o_ref, acc_ref) = refs[:n], refs[n:]
        kk = pl.program_id(1)

        @pl.when(kk == 0)
        def _():
            acc_ref[...] = jnp.zeros_like(acc_ref)

        hv = h_ref[...]
        for p_ref, o, w in zip(p_refs, PIECE_OFF, PIECE_W):
            acc_ref[o:o + w, :] += lax.dot_general(p_ref[...], hv, TN, preferred_element_type=F32)

        @pl.when(kk == nk - 1)
        def _():
            o_ref[...] = acc_ref[...].astype(BF16)

    return pl.pallas_call(
        body, grid=(D_MODEL // tn, nk),
        in_specs=[pl.BlockSpec((tk, w), lambda j, kk: (kk, 0)) for w in PIECE_W]
        + [pl.BlockSpec((tk, tn), lambda j, kk: (kk, j))],
        out_specs=pl.BlockSpec((IN_COLS, tn), lambda j, kk: (0, j)),
        out_shape=jax.ShapeDtypeStruct((IN_COLS, D_MODEL), BF16),
        scratch_shapes=[pltpu.VMEM((IN_COLS, tn), F32)], compiler_params=_cp(), name="tn_pieces")(*pieces, h)


def _tn_branches(dys, acts):
    tk = 512
    nk = SEQ // tk

    def body(d0, d1, d2, a0, a1, a2, o_ref, acc_ref):
        kk = pl.program_id(0)

        @pl.when(kk == 0)
        def _():
            acc_ref[...] = jnp.zeros_like(acc_ref)

        for j, (d, a) in enumerate(((d0, a0), (d1, a1), (d2, a2))):
            acc_ref[:, WIDTH * j:WIDTH * (j + 1)] += lax.dot_general(d[...], a[...], TN, preferred_element_type=F32)

        @pl.when(kk == nk - 1)
        def _():
            o_ref[...] = acc_ref[...].astype(BF16)

    row = lambda w: pl.BlockSpec((tk, w), lambda kk: (kk, 0))
    return pl.pallas_call(
        body, grid=(nk,), in_specs=[row(D_MODEL)] * 3 + [row(WIDTH)] * 3,
        out_specs=_full((D_MODEL, 3 * WIDTH)), out_shape=jax.ShapeDtypeStruct((D_MODEL, 3 * WIDTH), BF16),
        scratch_shapes=[pltpu.VMEM((D_MODEL, 3 * WIDTH), F32)], compiler_params=_cp(), name="tn_branches",
    )(*dys, *acts)


def _rope(t, c, a, b):
    return t * c + pltpu.roll(t, 120, axis=1) * a + pltpu.roll(t, 8, axis=1) * b


def _rope_t(d, c, a, b):
    return d * c + pltpu.roll(d * a, 8, axis=1) + pltpu.roll(d * b, 120, axis=1)


def _band_sides(band):
    left = lax.broadcasted_iota(jnp.int32, band.shape, 1) < HEAD_DIM
    h0 = jnp.where(left, band, 0.0)
    h1 = jnp.where(left, 0.0, band)
    r0 = pltpu.roll(h0, HEAD_DIM, axis=1)
    r1 = pltpu.roll(h1, HEAD_DIM, axis=1)
    return ((h0.astype(BF16), r0.astype(BF16)), (r1.astype(BF16), h1.astype(BF16)))


def _attn_mask(i):
    qi = lax.broadcasted_iota(jnp.int32, (2 * BLOCK, 2 * BLOCK), 0) % BLOCK
    kj = lax.broadcasted_iota(jnp.int32, (2 * BLOCK, 2 * BLOCK), 1)
    delta = qi + BLOCK - kj
    return (delta >= 0) & (delta < BLOCK) & ((kj >= BLOCK) | (i > 0))


def _attn_probs(s, ok, sink):
    s = jnp.where(ok, s * (HEAD_DIM ** -0.5), NEG_INF)
    m = jnp.maximum(jnp.max(s, axis=-1, keepdims=True), sink)
    p = jnp.exp(s - m)
    es = jnp.exp(sink - m)
    inv = 1.0 / (jnp.sum(p, axis=-1, keepdims=True) + es)
    return p * inv, es * inv


def _kv_group(qs, ks, vs, kh, sink_ref):
    q2 = jnp.concatenate([qs[2 * kh], qs[2 * kh + 1]], axis=0)
    kst = jnp.concatenate([ks[kh][0], ks[kh][1]], axis=0)
    vst = jnp.concatenate([vs[kh][0], vs[kh][1]], axis=0)
    top = lax.broadcasted_iota(jnp.int32, (2 * BLOCK, 1), 0) < BLOCK
    sinks = [jnp.where(top, sink_ref[0, 4 * kh + h], sink_ref[0, 4 * kh + 2 + h]) for h in range(2)]
    return q2, kst, vst, sinks


def _attn_load(q_ref, kvc_ref, kvp_ref, tc_ref, ta_ref, tb_ref, pc_ref, pa_ref, pb_ref):
    c, a, b = tc_ref[...], ta_ref[...], tb_ref[...]
    kc = _rope(kvc_ref[:, :KV_W], c, a, b)
    kp = _rope(kvp_ref[:, :KV_W], pc_ref[...], pa_ref[...], pb_ref[...])
    kband = jnp.concatenate([kp, kc], axis=0)
    vband = jnp.concatenate([kvp_ref[:, KV_W:], kvc_ref[:, KV_W:]], axis=0)
    qs = [_rope(q_ref[:, 128 * j:128 * (j + 1)], c, a, b).astype(BF16) for j in range(4)]
    return qs, _band_sides(kband), _band_sides(vband), (c, a, b)


def _attn_specs(clamp):
    cur = lambda i: (clamp(i), 0)
    prev = lambda i: (jnp.maximum(clamp(i) - 1, 0), 0)
    return [
        pl.BlockSpec((BLOCK, ATTN_W), cur), pl.BlockSpec((BLOCK, 2 * KV_W), cur),
        pl.BlockSpec((BLOCK, 2 * KV_W), prev),
        pl.BlockSpec((BLOCK, 128), cur), pl.BlockSpec((BLOCK, 128), cur), pl.BlockSpec((BLOCK, 128), cur),
        pl.BlockSpec((BLOCK, 128), prev), pl.BlockSpec((BLOCK, 128), prev), pl.BlockSpec((BLOCK, 128), prev),
        pl.BlockSpec(memory_space=pltpu.SMEM),
    ]


def _attn_fwd(q, kv, tabs, sinks):
    tc, ta, tb = tabs

    def body(q_ref, kvc_ref, kvp_ref, tc_ref, ta_ref, tb_ref, pc_ref, pa_ref, pb_ref, sink_ref, o_ref):
        i = pl.program_id(0)
        qs, ks, vs, _ = _attn_load(q_ref, kvc_ref, kvp_ref, tc_ref, ta_ref, tb_ref, pc_ref, pa_ref, pb_ref)
        ok = _attn_mask(i)
        for kh in range(2):
            q2, kst, vst, sinks = _kv_group(qs, ks, vs, kh, sink_ref)
            s = lax.dot_general(q2, kst, NT, preferred_element_type=F32)
            pn = [_attn_probs(s[:, 2 * BLOCK * h:2 * BLOCK * (h + 1)], ok, sinks[h])[0].astype(BF16) for h in range(2)]
            o2 = jnp.dot(jnp.concatenate(pn, axis=1), vst, preferred_element_type=F32).astype(BF16)
            for r in range(2):
                j = 2 * kh + r
                o_ref[:, 128 * j:128 * (j + 1)] = o2[BLOCK * r:BLOCK * (r + 1)]

    return pl.pallas_call(
        body, grid=(N_BLOCKS,), in_specs=_attn_specs(lambda i: i),
        out_specs=pl.BlockSpec((BLOCK, ATTN_W), lambda i: (i, 0)),
        out_shape=jax.ShapeDtypeStruct((SEQ, ATTN_W), BF16), compiler_params=_cp(), name="attn_fwd",
    )(q, kv, kv, tc, ta, tb, tc, ta, tb, sinks)


def _attn_bwd(q, kv, tabs, sinks, do):
    tc, ta, tb = tabs
    last = N_BLOCKS - 1
    clamp = lambda i: jnp.minimum(i, last)

    def place(full, side, kh):
        left = lax.broadcasted_iota(jnp.int32, full.shape, 1) < HEAD_DIM
        valid = jnp.where(left, full, 0.0) if side == 0 else jnp.where(left, 0.0, full)
        return valid if side == kh else pltpu.roll(valid, HEAD_DIM, axis=1)

    def body(q_ref, kvc_ref, kvp_ref, tc_ref, ta_ref, tb_ref, pc_ref, pa_ref, pb_ref, sink_ref, do_ref,
             dq_ref, dkv_ref, ds_ref, carry_ref):
        i = pl.program_id(0)

        @pl.when(i == 0)
        def _():
            ds_ref[...] = jnp.zeros_like(ds_ref)
            carry_ref[...] = jnp.zeros_like(carry_ref)

        @pl.when(i > last)
        def _():
            dkv_ref[...] = carry_ref[...].astype(BF16)

        @pl.when(i <= last)
        def _():
            qs, ks, vs, (c, a, b) = _attn_load(q_ref, kvc_ref, kvp_ref, tc_ref, ta_ref, tb_ref,
                                               pc_ref, pa_ref, pb_ref)
            ok = _attn_mask(i)
            dk = jnp.zeros((2 * BLOCK, 128), F32)
            dv = jnp.zeros((2 * BLOCK, 128), F32)
            dsink = jnp.zeros((1, 128), F32)
            lane = lax.broadcasted_iota(jnp.int32, (1, 128), 1)
            for kh in range(2):
                q2, kst, vst, sinks = _kv_group(qs, ks, vs, kh, sink_ref)
                do2 = jnp.concatenate([do_ref[:, 128 * (2 * kh + r):128 * (2 * kh + r + 1)] for r in range(2)],
                                      axis=0).astype(BF16)
                s = lax.dot_general(q2, kst, NT, preferred_element_type=F32)
                dp = lax.dot_general(do2, vst, NT, preferred_element_type=F32)
                pns, dss = [], []
                for h in range(2):
                    cols = slice(2 * BLOCK * h, 2 * BLOCK * (h + 1))
                    pn, ps = _attn_probs(s[:, cols], ok, sinks[h])
                    dr = jnp.sum(pn * dp[:, cols], axis=-1, keepdims=True)
                    pns.append(pn.astype(BF16))
                    dss.append((pn * (dp[:, cols] - dr) * (HEAD_DIM ** -0.5)).astype(BF16))
                    for r in range(2):
                        part = -jnp.sum((ps * dr)[BLOCK * r:BLOCK * (r + 1)])
                        dsink += jnp.where(lane == 4 * kh + 2 * r + h, part, 0.0)
                ds2, pn2 = jnp.concatenate(dss, axis=1), jnp.concatenate(pns, axis=1)
                dq2 = jnp.dot(ds2, kst, preferred_element_type=F32)
                dk2 = lax.dot_general(ds2, q2, TN, preferred_element_type=F32)
                dv2 = lax.dot_general(pn2, do2, TN, preferred_element_type=F32)
                for h in range(2):
                    dk += place(dk2[2 * BLOCK * h:2 * BLOCK * (h + 1)], h, kh)
                    dv += place(dv2[2 * BLOCK * h:2 * BLOCK * (h + 1)], h, kh)
                for r in range(2):
                    j = 2 * kh + r
                    dq_ref[:, 128 * j:128 * (j + 1)] = _rope_t(dq2[BLOCK * r:BLOCK * (r + 1)], c, a, b).astype(BF16)
            ds_ref[...] += dsink
            dk_prev = _rope_t(dk[:BLOCK], pc_ref[...], pa_ref[...], pb_ref[...])
            dk_cur = _rope_t(dk[BLOCK:], c, a, b)
            prev = jnp.concatenate([dk_prev, dv[:BLOCK]], axis=1)
            dkv_ref[...] = (carry_ref[...] + prev).astype(BF16)
            carry_ref[...] = jnp.concatenate([dk_cur, dv[BLOCK:]], axis=1)

    return pl.pallas_call(
        body, grid=(N_BLOCKS + 1,),
        in_specs=_attn_specs(clamp) + [pl.BlockSpec((BLOCK, ATTN_W), lambda i: (clamp(i), 0))],
        out_specs=[pl.BlockSpec((BLOCK, ATTN_W), lambda i: (clamp(i), 0)),
                   pl.BlockSpec((BLOCK, 2 * KV_W), lambda i: (jnp.maximum(i - 1, 0), 0)),
                   pl.BlockSpec((1, 128), lambda i: (0, 0))],
        out_shape=[jax.ShapeDtypeStruct((SEQ, ATTN_W), BF16), jax.ShapeDtypeStruct((SEQ, 2 * KV_W), BF16),
                   jax.ShapeDtypeStruct((1, 128), F32)],
        scratch_shapes=[pltpu.VMEM((BLOCK, 2 * KV_W), F32)], compiler_params=_cp(), name="attn_bwd",
    )(q, kv, kv, tc, ta, tb, tc, ta, tb, sinks, do)


def _shift_down(z, k):
    row = lax.broadcasted_iota(jnp.int32, z.shape, 0)
    return jnp.where(row < k, 0.0, pltpu.roll(z, k, axis=0))


def _shift_up(z, k):
    n = z.shape[0]
    row = lax.broadcasted_iota(jnp.int32, z.shape, 0)
    return jnp.where(row >= n - k, 0.0, pltpu.roll(z, n - k, axis=0))


def _conv_specs():
    nb = WIDTH // 128
    return [pl.BlockSpec((SEQ, 128), lambda j: (0, j)), pl.BlockSpec((SEQ, 128), lambda j: (0, nb + j)),
            pl.BlockSpec((SEQ, 128), lambda j: (0, 2 * nb + j)), pl.BlockSpec((None, 8, 128), lambda j: (0, 0, j))]


def _conv_fwd(cbx, cw, layer):
    def body(cb_ref, cc_ref, cx_ref, w_ref, o_ref):
        z = cc_ref[...] * cx_ref[...]
        s = w_ref[0:1, :] * _shift_down(z, 2) + w_ref[1:2, :] * _shift_down(z, 1) + w_ref[2:3, :] * z
        o_ref[...] = (cb_ref[...] * s).astype(BF16)

    specs = _conv_specs()
    specs[3] = pl.BlockSpec((None, 8, 128), lambda j: (layer, 0, j))
    return pl.pallas_call(
        body, grid=(WIDTH // 128,), in_specs=specs,
        out_specs=pl.BlockSpec((SEQ, 128), lambda j: (0, j)),
        out_shape=jax.ShapeDtypeStruct((SEQ, WIDTH), BF16), compiler_params=_cp(), name="conv_fwd",
    )(cbx, cbx, cbx, cw)


def _conv_bwd(cbx, cw, layer, dout, tie):
    def body(cb_ref, cc_ref, cx_ref, w_ref, do_ref, tie_ref, dcb_ref, dcc_ref, dcx_ref, dw_ref):
        cc, cx = cc_ref[...], cx_ref[...]
        z = cc * cx
        z1, z2 = _shift_down(z, 1), _shift_down(z, 2)
        w0, w1, w2 = w_ref[0:1, :], w_ref[1:2, :], w_ref[2:3, :]
        dout = do_ref[...]
        ds = dout * cb_ref[...]
        dcb_ref[...] = (dout * (w0 * z2 + w1 * z1 + w2 * z)).astype(BF16)
        dz = w2 * ds + w1 * _shift_up(ds, 1) + w0 * _shift_up(ds, 2)
        dcc_ref[...] = (dz * cx).astype(BF16)
        dcx_ref[...] = (dz * cc).astype(BF16)
        rows = [jnp.sum(ds * zz, axis=0, keepdims=True) for zz in (z2, z1, z)]
        dw_ref[...] = jnp.concatenate(rows + [jnp.zeros((5, 128), F32)], axis=0)

    col = lambda j: (0, j)
    specs = _conv_specs()
    specs[3] = pl.BlockSpec((None, 8, 128), lambda j: (layer, 0, j))
    return pl.pallas_call(
        body, grid=(WIDTH // 128,), in_specs=specs + [pl.BlockSpec((SEQ, 128), col), ANY],
        out_specs=[pl.BlockSpec((SEQ, 128), col), pl.BlockSpec((SEQ, 128), col), pl.BlockSpec((SEQ, 128), col),
                   pl.BlockSpec((8, 128), col)],
        out_shape=[jax.ShapeDtypeStruct((SEQ, WIDTH), BF16)] * 3 + [jax.ShapeDtypeStruct((8, WIDTH), F32)],
        compiler_params=_cp(), name="conv_bwd",
    )(cbx, cbx, cbx, cw, dout, tie)


def _ssm_prep_math(a_re, a_im, log_dt, bt_re, bt_im):
    dt = jnp.exp(log_dt)
    er = jnp.exp(a_re * dt)
    lr = er * jnp.cos(a_im * dt)
    li = er * jnp.sin(a_im * dt)
    n2 = a_re * a_re + a_im * a_im
    cr = ((lr - 1.0) * a_re + li * a_im) / n2
    ci = (li * a_re - (lr - 1.0) * a_im) / n2
    cr3, ci3 = cr[:, None, :], ci[:, None, :]
    return lr, li, cr3 * bt_re - ci3 * bt_im, cr3 * bt_im + ci3 * bt_re


_GS = (SSM_GROUPS, SSM_STATE)
_GHS = (SSM_GROUPS, SSM_GROUP, SSM_STATE)


def _layered(shape):
    return pl.BlockSpec((None,) + shape, lambda l: (l,) + (0,) * len(shape))


def _ssm_prep(a_re, a_im, log_dt, bt_re, bt_im):
    def body(ar, ai, ld, br, bi, o0, o1, o2, o3):
        outs = _ssm_prep_math(ar[...], ai[...], ld[...], br[...], bi[...])
        for o, v in zip((o0, o1, o2, o3), outs):
            o[...] = v

    shapes = [_GS, _GS, _GHS, _GHS]
    return pl.pallas_call(
        body, grid=(DEPTH,), in_specs=[_layered(s) for s in (_GS, _GS, (SSM_GROUPS, 1), _GHS, _GHS)],
        out_specs=[_layered(s) for s in shapes],
        out_shape=[jax.ShapeDtypeStruct((DEPTH,) + s, F32) for s in shapes],
        name="ssm_prep")(a_re, a_im, log_dt, bt_re, bt_im)


def _ssm_prep_bwd(a_re, a_im, log_dt, bt_re, bt_im, cots):
    def body(ar, ai, ld, br, bi, c0, c1, c2, c3, o0, o1, o2, o3, o4):
        _, vjp = jax.vjp(_ssm_prep_math, ar[...], ai[...], ld[...], br[...], bi[...])
        for o, v in zip((o0, o1, o2, o3, o4), vjp((c0[...], c1[...], c2[...], c3[...]))):
            o[...] = v

    ins = (_GS, _GS, (SSM_GROUPS, 1), _GHS, _GHS)
    return pl.pallas_call(
        body, grid=(DEPTH,), in_specs=[_layered(s) for s in ins + (_GS, _GS, _GHS, _GHS)],
        out_specs=[_layered(s) for s in ins],
        out_shape=[jax.ShapeDtypeStruct((DEPTH,) + s, F32) for s in ins],
        name="ssm_prep_bwd")(a_re, a_im, log_dt, bt_re, bt_im, *cots)


LANES_G = 512
N_LANE_GROUPS = SSM_GROUPS * SSM_STATE // LANES_G


def _scan_in_place(xr_ref, xi_ref, ar, ai, reverse):
    shape = (N_CHUNKS, xr_ref.shape[1])
    ar, ai = jnp.broadcast_to(ar, shape), jnp.broadcast_to(ai, shape)

    def rows(tau):
        t = (CHUNK - 1 - tau) if reverse else tau
        return pl.ds(pl.multiple_of(t * N_CHUNKS, N_CHUNKS), N_CHUNKS)

    def step(tau, carry):
        sr, si = carry
        return ar * sr - ai * si + xr_ref[rows(tau), :], ar * si + ai * sr + xi_ref[rows(tau), :]

    zero = jnp.zeros(shape, F32)
    er, ei = lax.fori_loop(0, CHUNK, step, (zero, zero), unroll=8)
    qr, qi = ar, ai
    for _ in range(8):
        qr, qi = qr * qr - qi * qi, 2.0 * qr * qi
    shift = _shift_up if reverse else _shift_down
    for k in (1, 2, 4):
        sr, si = shift(er, k), shift(ei, k)
        er, ei = er + qr * sr - qi * si, ei + qr * si + qi * sr
        qr, qi = qr * qr - qi * qi, 2.0 * qr * qi
    start = (shift(er, 1), shift(ei, 1))

    def write(tau, carry):
        sr, si = step(tau, carry)
        xr_ref[rows(tau), :] = sr
        xi_ref[rows(tau), :] = si
        return sr, si

    return write, start


def _ssm_specs(layer):
    col = lambda w: pl.BlockSpec((SEQ, w), lambda g: (0, g))
    diag = pl.BlockSpec((None, None, 128, LANES_G), lambda g: (layer, g, 0, 0))
    vec = pl.BlockSpec((None, 1, LANES_G), lambda g: (layer, 0, g))
    return col, diag, vec


def _to_scan_order(src_ref, dst_ref):
    def move(tau, _):
        dst_ref[pl.ds(pl.multiple_of(tau * N_CHUNKS, N_CHUNKS), N_CHUNKS), :] = src_ref[pl.ds(tau, N_CHUNKS, stride=CHUNK), :]
        return 0

    lax.fori_loop(0, CHUNK, move, 0, unroll=8)


def _to_time_order(src_ref, dst_ref, dtype):
    for j in range(N_CHUNKS):
        dst_ref[pl.ds(j * CHUNK, CHUNK), :] = src_ref[pl.ds(j, CHUNK, stride=N_CHUNKS), :].astype(dtype)


def _ssm_fwd(u, mats, layer, d):
    def body(u_ref, d_ref, br_ref, bi_ref, cr_ref, ci_ref, ar_ref, ai_ref, xr_ref, xi_ref, y_ref, us_ref):
        _to_scan_order(u_ref, us_ref)
        uv = us_ref[...].astype(BF16)
        xr_ref[...] = jnp.dot(uv, br_ref[...], preferred_element_type=F32)
        xi_ref[...] = jnp.dot(uv, bi_ref[...], preferred_element_type=F32)
        write, start = _scan_in_place(xr_ref, xi_ref, ar_ref[...], ai_ref[...], False)
        lax.fori_loop(0, CHUNK, write, start, unroll=8)
        y = lax.dot_general(xr_ref[...].astype(BF16), cr_ref[...], NT, preferred_element_type=F32)
        y += lax.dot_general(xi_ref[...].astype(BF16), ci_ref[...], NT, preferred_element_type=F32)
        us_ref[...] = y + d_ref[...] * us_ref[...]
        _to_time_order(us_ref, y_ref, F32)

    col, diag, vec = _ssm_specs(layer)
    return pl.pallas_call(
        body, grid=(N_LANE_GROUPS,),
        in_specs=[col(128), pl.BlockSpec((None, 1, 128), lambda g: (layer, 0, g)),
                  diag, diag, diag, diag, vec, vec],
        out_specs=[col(LANES_G), col(LANES_G), col(128)],
        out_shape=[jax.ShapeDtypeStruct((SEQ, SSM_GROUPS * SSM_STATE), F32)] * 2
        + [jax.ShapeDtypeStruct((SEQ, WIDTH), F32)],
        scratch_shapes=[pltpu.VMEM((SEQ, 128), F32)], compiler_params=_cp(), name="ssm_fwd",
    )(u, d, mats["b_re"], mats["b_im"], mats["c_re"], mats["c_im_neg"], mats["a_re"], mats["a_im"])


def _ssm_bwd(dy, x_re, x_im, u, mats, layer, d):
    def body(dyt_ref, ut_ref, d_ref, xr_ref, xi_ref, br_ref, bi_ref, cr_ref, ci_ref, ar_ref, ai_ref,
             du_ref, dar_ref, dai_ref, dbr_ref, dbi_ref, dcr_ref, dci_ref, lr_ref, li_ref, dys_ref, u_ref):
        _to_scan_order(dyt_ref, dys_ref)
        _to_scan_order(ut_ref, u_ref)
        dy = dys_ref[...].astype(BF16)
        lr_ref[...] = jnp.dot(dy, cr_ref[...], preferred_element_type=F32)
        li_ref[...] = jnp.dot(dy, ci_ref[...], preferred_element_type=F32)
        write, start = _scan_in_place(lr_ref, li_ref, ar_ref[...], -ai_ref[...], True)

        def rows(t):
            return pl.ds(pl.multiple_of(t * N_CHUNKS, N_CHUNKS), N_CHUNKS)

        def grad(acc, lam, xpr, xpi):
            return acc[0] + xpr * lam[0] + xpi * lam[1], acc[1] + xpr * lam[1] - xpi * lam[0]

        def down(tau, carry):
            lam = write(tau, carry[0])
            t = CHUNK - 2 - tau
            return lam, grad(carry[1], lam, xr_ref[rows(t), :], xi_ref[rows(t), :])

        zero = jnp.zeros((N_CHUNKS, LANES_G), F32)
        lam, acc = lax.fori_loop(0, CHUNK - 1, down, (start, (zero, zero)), unroll=5)
        lam = write(CHUNK - 1, lam)
        last = rows(CHUNK - 1)
        acc = grad(acc, lam, _shift_down(xr_ref[last, :], 1), _shift_down(xi_ref[last, :], 1))
        dar_ref[...] = jnp.sum(acc[0], axis=0, keepdims=True)
        dai_ref[...] = jnp.sum(acc[1], axis=0, keepdims=True)

        l_re, l_im = lr_ref[...].astype(BF16), li_ref[...].astype(BF16)
        du = lax.dot_general(l_re, br_ref[...], NT, preferred_element_type=F32)
        du += lax.dot_general(l_im, bi_ref[...], NT, preferred_element_type=F32)
        dys_ref[...] = du + dys_ref[...] * d_ref[...]
        _to_time_order(dys_ref, du_ref, BF16)
        uv = u_ref[...].astype(BF16)
        dbr_ref[...] = lax.dot_general(uv, l_re, TN, preferred_element_type=F32)
        dbi_ref[...] = lax.dot_general(uv, l_im, TN, preferred_element_type=F32)
        dcr_ref[...] = lax.dot_general(dy, xr_ref[...].astype(BF16), TN, preferred_element_type=F32)
        dci_ref[...] = lax.dot_general(dy, xi_ref[...].astype(BF16), TN, preferred_element_type=F32)

    col, diag, vec = _ssm_specs(layer)
    out_vec = pl.BlockSpec((1, LANES_G), lambda g: (0, g))
    out_blk = pl.BlockSpec((None, 128, LANES_G), lambda g: (g, 0, 0))
    sds = jax.ShapeDtypeStruct
    return pl.pallas_call(
        body, grid=(N_LANE_GROUPS,),
        in_specs=[col(128), col(128), pl.BlockSpec((None, 1, 128), lambda g: (layer, 0, g)),
                  col(LANES_G), col(LANES_G), diag, diag, diag, diag, vec, vec],
        out_specs=[col(128), out_vec, out_vec, out_blk, out_blk, out_blk, out_blk],
        out_shape=[sds((SEQ, WIDTH), BF16)] + [sds((1, SSM_GROUPS * SSM_STATE), F32)] * 2
        + [sds((N_LANE_GROUPS, 128, LANES_G), F32)] * 4,
        scratch_shapes=[pltpu.VMEM((SEQ, LANES_G), F32)] * 2 + [pltpu.VMEM((SEQ, 128), F32)] * 2,
        compiler_params=_cp(), name="ssm_bwd",
    )(dy, u, d, x_re, x_im, mats["b_re"], mats["b_im"], mats["c_re"], mats["c_im_neg"],
      mats["a_re"], mats["a_im"])


_GELU_C = math.sqrt(2.0 / math.pi)


def _gelu(y):
    return 0.5 * y * (1.0 + jnp.tanh(_GELU_C * (y + 0.044715 * (y * y * y))))


def _glu_fwd(y, wglu):
    tt = 512

    def body(y_ref, w_ref, z_ref):
        ys = _gelu(y_ref[...])
        a = jnp.dot(ys.astype(BF16), w_ref[...], preferred_element_type=F32)
        z_ref[...] = (ys * jax.nn.sigmoid(a)).astype(BF16)

    blk = pl.BlockSpec((tt, WIDTH), lambda i: (i, 0))
    return pl.pallas_call(body, grid=(SEQ // tt,), in_specs=[blk, _full((WIDTH, WIDTH))], out_specs=blk,
                          out_shape=jax.ShapeDtypeStruct((SEQ, WIDTH), BF16), compiler_params=_cp(),
                          name="glu_fwd")(y, wglu)


def _glu_bwd(y, wglu, dz, u):
    tt = 512

    def body(y_ref, w_ref, dz_ref, u_ref, dy_ref, ys_ref, da_ref, dd_ref):
        @pl.when(pl.program_id(0) == 0)
        def _():
            dd_ref[...] = jnp.zeros_like(dd_ref)

        yv = y_ref[...]
        t = jnp.tanh(_GELU_C * (yv + 0.044715 * (yv * yv * yv)))
        ys = 0.5 * yv * (1.0 + t)
        ysb = ys.astype(BF16)
        sg = jax.nn.sigmoid(jnp.dot(ysb, w_ref[...], preferred_element_type=F32))
        dz = dz_ref[...].astype(F32)
        da = (dz * ys * sg * (1.0 - sg)).astype(BF16)
        dys = dz * sg + lax.dot_general(da, w_ref[...], NT, preferred_element_type=F32)
        dy = dys * (0.5 * (1.0 + t) + 0.5 * yv * (1.0 - t * t) * _GELU_C * (1.0 + 3 * 0.044715 * (yv * yv)))
        dy_ref[...] = dy
        ys_ref[...] = ysb
        da_ref[...] = da
        dd_ref[...] += jnp.sum(dy * u_ref[...], axis=0, keepdims=True)

    blk = pl.BlockSpec((tt, WIDTH), lambda i: (i, 0))
    return pl.pallas_call(
        body, grid=(SEQ // tt,), in_specs=[blk, _full((WIDTH, WIDTH)), blk, blk],
        out_specs=[blk, blk, blk, _full((1, WIDTH))],
        out_shape=[jax.ShapeDtypeStruct((SEQ, WIDTH), F32)] + [jax.ShapeDtypeStruct((SEQ, WIDTH), BF16)] * 2
        + [jax.ShapeDtypeStruct((1, WIDTH), F32)],
        compiler_params=_cp(), name="glu_bwd")(y, wglu, dz, u)


def _mix_specs(tt, layer):
    row = lambda w: pl.BlockSpec((tt, w), lambda i: (i, 0))
    gate = lambda j: pl.BlockSpec((tt, D_MODEL), lambda i: (i, j))
    wo = lambda j: pl.BlockSpec((D_MODEL, WIDTH), lambda i: (0, j))
    return [row(D_MODEL), row(WIDTH), row(WIDTH), row(WIDTH), gate(0), gate(1), gate(2),
            pl.BlockSpec((None, 1, GATE_W), lambda i: (layer, 0, 0)), wo(0), wo(1), wo(2),
            _full((D_MODEL, D_MODEL))]


def _mix_branches(o_ref, c_ref, z_ref, g_refs, b_ref, wa_ref, wc_ref, ws_ref):
    ys = [lax.dot_general(r[...], w[...], NT, preferred_element_type=F32)
          for r, w in ((o_ref, wa_ref), (c_ref, wc_ref), (z_ref, ws_ref))]
    gates = [jax.nn.sigmoid(g_refs[j][...] + b_ref[:, D_MODEL * j:D_MODEL * (j + 1)]) for j in range(3)]
    return ys, gates


def _mix_fwd(x, o, cv, z, glog, b_gate, layer, wbt, wmix, tie):
    tt = 256

    def body(x_ref, o_ref, c_ref, z_ref, g0, g1, g2, b_ref, wa_ref, wc_ref, ws_ref, wm_ref, tie_ref, x1_ref):
        ys, gates = _mix_branches(o_ref, c_ref, z_ref, (g0, g1, g2), b_ref, wa_ref, wc_ref, ws_ref)
        merged = gates[0] * ys[0] + gates[1] * ys[1] + gates[2] * ys[2]
        x1_ref[...] = x_ref[...] + jnp.dot(merged.astype(BF16), wm_ref[...], preferred_element_type=F32)

    return pl.pallas_call(
        body, grid=(SEQ // tt,), in_specs=_mix_specs(tt, layer) + [ANY],
        out_specs=pl.BlockSpec((tt, D_MODEL), lambda i: (i, 0)),
        out_shape=jax.ShapeDtypeStruct((SEQ, D_MODEL), F32), compiler_params=_cp(), name="mix_fwd",
    )(x, o, cv, z, glog, glog, glog, b_gate, wbt, wbt, wbt, wmix, tie)


def _mix_bwd(dx1, o, cv, z, glog, b_gate, layer, wbt, wmix, tie):
    tt = 256

    def body(dx_ref, o_ref, c_ref, z_ref, g0, g1, g2, b_ref, wa_ref, wc_ref, ws_ref, wm_ref, tie_ref,
             mg_ref, dya_ref, dyc_ref, dys_ref, do_ref, dc_ref, dz_ref, dgl_ref, db_ref):
        @pl.when(pl.program_id(0) == 0)
        def _():
            db_ref[...] = jnp.zeros_like(db_ref)

        ys, gates = _mix_branches(o_ref, c_ref, z_ref, (g0, g1, g2), b_ref, wa_ref, wc_ref, ws_ref)
        mg_ref[...] = (gates[0] * ys[0] + gates[1] * ys[1] + gates[2] * ys[2]).astype(BF16)
        dm = lax.dot_general(dx_ref[...].astype(BF16), wm_ref[...], NT, preferred_element_type=F32)
        for j, (dy_ref, w_ref, d_ref) in enumerate(((dya_ref, wa_ref, do_ref), (dyc_ref, wc_ref, dc_ref),
                                                    (dys_ref, ws_ref, dz_ref))):
            dy = (dm * gates[j]).astype(BF16)
            dy_ref[...] = dy
            d_ref[...] = jnp.dot(dy, w_ref[...], preferred_element_type=F32)
            dgl = dm * ys[j] * gates[j] * (1.0 - gates[j])
            dgl_ref[:, D_MODEL * j:D_MODEL * (j + 1)] = dgl.astype(BF16)
            db_ref[:, D_MODEL * j:D_MODEL * (j + 1)] += jnp.sum(dgl, axis=0, keepdims=True)

    row = lambda w: pl.BlockSpec((tt, w), lambda i: (i, 0))
    sds = jax.ShapeDtypeStruct
    return pl.pallas_call(
        body, grid=(SEQ // tt,), in_specs=_mix_specs(tt, layer) + [ANY],
        out_specs=[row(D_MODEL)] * 4 + [row(WIDTH)] * 3 + [row(GATE_W), _full((1, GATE_W))],
        out_shape=[sds((SEQ, D_MODEL), BF16)] * 4 + [sds((SEQ, WIDTH), F32)] * 3
        + [sds((SEQ, GATE_W), BF16), sds((1, GATE_W), F32)],
        compiler_params=_cp(), name="mix_bwd",
    )(dx1, o, cv, z, glog, glog, glog, b_gate, wbt, wbt, wbt, wmix, tie)


def _ffn_out_fwd(x1, gu, wout, tie):
    tt = 256

    def body(x_ref, gt_ref, up_ref, w_ref, tie_ref, o_ref):
        gt = gt_ref[...]
        act = (gt * jax.nn.sigmoid(gt) * up_ref[...]).astype(BF16)
        o_ref[...] = x_ref[...] + jnp.dot(act, w_ref[...], preferred_element_type=F32)

    return pl.pallas_call(
        body, grid=(SEQ // tt,),
        in_specs=[pl.BlockSpec((tt, D_MODEL), lambda i: (i, 0)), pl.BlockSpec((tt, FFN_H), lambda i: (i, 0)),
                  pl.BlockSpec((tt, FFN_H), lambda i: (i, 1)), _full((FFN_H, D_MODEL)), ANY],
        out_specs=pl.BlockSpec((tt, D_MODEL), lambda i: (i, 0)),
        out_shape=jax.ShapeDtypeStruct((SEQ, D_MODEL), F32), compiler_params=_cp(), name="ffn_out_fwd",
    )(x1, gu, gu, wout, tie)


def _ffn_out_bwd(dx2, gu, wout, tie):
    tt = 256

    def body(dx_ref, gt_ref, up_ref, w_ref, tie_ref, dgu_ref, act_ref):
        gt, up = gt_ref[...], up_ref[...]
        sg = jax.nn.sigmoid(gt)
        silu = gt * sg
        act_ref[...] = (silu * up).astype(BF16)
        dact = lax.dot_general(dx_ref[...].astype(BF16), w_ref[...], NT, preferred_element_type=F32)
        dgu_ref[:, :FFN_H] = (dact * up * (sg * (1.0 + gt * (1.0 - sg)))).astype(BF16)
        dgu_ref[:, FFN_H:] = (dact * silu).astype(BF16)

    return pl.pallas_call(
        body, grid=(SEQ // tt,),
        in_specs=[pl.BlockSpec((tt, D_MODEL), lambda i: (i, 0)), pl.BlockSpec((tt, FFN_H), lambda i: (i, 0)),
                  pl.BlockSpec((tt, FFN_H), lambda i: (i, 1)), _full((FFN_H, D_MODEL)), ANY],
        out_specs=[pl.BlockSpec((tt, 2 * FFN_H), lambda i: (i, 0)), pl.BlockSpec((tt, FFN_H), lambda i: (i, 0))],
        out_shape=[jax.ShapeDtypeStruct((SEQ, 2 * FFN_H), BF16), jax.ShapeDtypeStruct((SEQ, FFN_H), BF16)],
        compiler_params=_cp(), name="ffn_out_bwd",
    )(dx2, gu, gu, wout, tie)


def _loss_head(x, g, target):
    tt = 256

    def body(x_ref, g_ref, t_ref, loss_ref, dx_ref, dg_ref):
        @pl.when(pl.program_id(0) == 0)
        def _():
            loss_ref[...] = jnp.zeros_like(loss_ref)
            dg_ref[...] = jnp.zeros_like(dg_ref)

        xv = x_ref[...]
        r = lax.rsqrt(jnp.mean(xv * xv, axis=-1, keepdims=True) + NORM_EPS)
        xh = xv * r
        err = xh * g_ref[...] - t_ref[...]
        loss_ref[...] += 0.5 * jnp.sum(jnp.mean(err * err, axis=-1, keepdims=True))
        dy = err * (1.0 / D_MODEL)
        gy = dy * g_ref[...]
        dx_ref[...] = r * (gy - xh * jnp.mean(gy * xh, axis=-1, keepdims=True))
        dg_ref[...] += jnp.sum(dy * xh, axis=0, keepdims=True)

    row = pl.BlockSpec((tt, D_MODEL), lambda i: (i, 0))
    return pl.pallas_call(
        body, grid=(SEQ // tt,), in_specs=[row, _full((1, D_MODEL)), row],
        out_specs=[_full((1, 128)), row, _full((1, D_MODEL))],
        out_shape=[jax.ShapeDtypeStruct((1, 128), F32), jax.ShapeDtypeStruct((SEQ, D_MODEL), F32),
                   jax.ShapeDtypeStruct((1, D_MODEL), F32)],
        compiler_params=_cp(), name="loss_head")(x, g, target)


def _adam_math(g, w, m, v):
    nm = B1 * m + (1.0 - B1) * g
    nv = B2 * v + (1.0 - B2) * (g * g)
    m_hat = nm / (1.0 - B1 ** STEP)
    v_hat = nv / (1.0 - B2 ** STEP)
    return -LR * (m_hat / (jnp.sqrt(v_hat) + ADAM_EPS) + WD * w), nm, nv


def _adamw_small(parts, w, m, v, name):
    def body(p_ref, w_ref, m_ref, v_ref, g_ref, d_ref, nm_ref, nv_ref):
        g = p_ref[0].astype(F32)
        for k in range(1, N_DEV):
            g = g + p_ref[k].astype(F32)
        g_ref[...] = g
        d_ref[...], nm_ref[...], nv_ref[...] = _adam_math(g, w_ref[...], m_ref[...], v_ref[...])

    out_shape = [jax.ShapeDtypeStruct(w.shape, F32)] * 4
    if w.ndim < 3:
        return pl.pallas_call(body, out_shape=out_shape, name=name)(parts, w, m, v)
    rest = w.shape[1:]
    zeros = (0,) * len(rest)
    blk = pl.BlockSpec((None,) + rest, lambda l: (l,) + zeros)
    return pl.pallas_call(
        body, grid=(w.shape[0],),
        in_specs=[pl.BlockSpec((N_DEV, None) + rest, lambda l: (0, l) + zeros), blk, blk, blk],
        out_specs=[blk] * 4, out_shape=out_shape, name=name)(parts, w, m, v)


def _adamw(parts, w, m, v, tr, name, groups=None, fill=None, tie=None):
    n_groups, rows, cols = w.shape
    n_parts = parts.shape[1]
    lo, hi = groups if groups is not None else (0, n_groups)

    def body(p_ref, w_ref, m_ref, v_ref, *rest):
        g_ref, d_ref, nm_ref, nv_ref = rest[-4:]
        g = p_ref[0].astype(F32)
        for k in range(1, n_parts):
            g = g + p_ref[k].astype(F32)
        nm = B1 * m_ref[...] + (1.0 - B1) * g
        nv = B2 * v_ref[...] + (1.0 - B2) * (g * g)
        m_hat = nm / (1.0 - B1 ** STEP)
        v_hat = nv / (1.0 - B2 ** STEP)
        g_ref[...] = g
        d_ref[...] = -LR * (m_hat / (jnp.sqrt(v_hat) + ADAM_EPS) + WD * w_ref[...])
        nm_ref[...] = nm
        nv_ref[...] = nv

    blk = pl.BlockSpec((None, tr, cols), lambda l, i: (l + lo, i, 0))
    p_lo = lo if parts.shape[0] == n_groups else 0
    extra = ([] if fill is None else list(fill)) + ([] if tie is None else [tie])
    return pl.pallas_call(
        body, grid=(hi - lo, rows // tr),
        in_specs=[pl.BlockSpec((None, n_parts, tr, cols), lambda l, i: (l + p_lo, 0, i, 0)), blk, blk, blk]
        + [ANY] * len(extra),
        out_specs=[blk] * 4, out_shape=[jax.ShapeDtypeStruct((n_groups, rows, cols), F32)] * 4,
        input_output_aliases={} if fill is None else {4 + j: j for j in range(4)},
        compiler_params=_cp(), name=name)(parts, w, m, v, *extra)


def _split_start(name, arrays, n_sems, plan, after=None):
    n = len(arrays)
    order = [] if after is None else [after]
    n_in = n + len(order)

    def body(*refs):
        ins, send_sems, recv_sems, token = refs[:n], refs[n_in], refs[n_in + 1], refs[-1]
        for src, dst, k, to in plan(ins)[0]:
            pltpu.make_async_remote_copy(src_ref=src, dst_ref=dst, send_sem=send_sems.at[k], recv_sem=recv_sems.at[k],
                                         device_id=to, device_id_type=MESH_ID).start()
        token[...] = jnp.zeros_like(token)

    outs = pl.pallas_call(
        body, name=name,
        out_shape=(pltpu.SemaphoreType.DMA((n_sems,)), pltpu.SemaphoreType.DMA((n_sems,)),
                   *[pltpu.HBM(a.shape, a.dtype) for a in arrays], jax.ShapeDtypeStruct((8, 128), F32)),
        in_specs=[HBM] * n + [ANY] * len(order),
        out_specs=(SEM, SEM, *[HBM] * n, pl.BlockSpec(memory_space=pltpu.VMEM)),
        input_output_aliases={i: 2 + i for i in range(n)},
        compiler_params=pltpu.CompilerParams(has_side_effects=EFFECT),
    )(*[pltpu.with_memory_space_constraint(a, pltpu.HBM) for a in arrays], *order)
    return outs[0], outs[1], list(outs[2:2 + n]), outs[-1]


def _split_wait(name, arrays, send_sems, recv_sems, after, plan):
    n = len(arrays)
    order = list(after) if isinstance(after, (list, tuple)) else [after]

    def body(*refs):
        ins, s_sems, r_sems = refs[:n], refs[n], refs[n + 1]
        sends, arrivals = plan(ins)
        x, y, c = lax.axis_index("x"), lax.axis_index("y"), lax.axis_index("c")
        for src, dst, k, to in sends:
            pltpu.make_async_remote_copy(src_ref=src, dst_ref=dst, send_sem=s_sems.at[k], recv_sem=r_sems.at[k],
                                         device_id=to, device_id_type=MESH_ID).wait_send()
        for dst, k in arrivals:
            pltpu.make_async_remote_copy(src_ref=dst, dst_ref=dst, send_sem=s_sems.at[k], recv_sem=r_sems.at[k],
                                         device_id=(x, y, c), device_id_type=MESH_ID).wait_recv()

    return pl.pallas_call(
        body, name=name, out_shape=[pltpu.HBM(a.shape, a.dtype) for a in arrays],
        in_specs=[HBM] * n + [SEM, SEM] + [ANY] * len(order), out_specs=[HBM] * n,
        input_output_aliases={i: i for i in range(n)},
        compiler_params=pltpu.CompilerParams(has_side_effects=EFFECT),
    )(*arrays, send_sems, recv_sems, *order)


def _chips():
    x, y, c = lax.axis_index("x"), lax.axis_index("y"), lax.axis_index("c")
    return x, y, c, [(1 - x, y), (x, 1 - y), (1 - x, 1 - y)]


def _plan_gather_chips(refs):
    x, y, c, chips = _chips()
    me = 4 * x + 2 * y + c
    n = len(refs) // 2
    sends, arrivals = [], []
    for i in range(n):
        src, land = refs[i], refs[n + i]
        sends.append((src, land.at[me], 4 * i, (x, y, 1 - c)))
        arrivals.append((land.at[4 * x + 2 * y + 1 - c], 4 * i))
        for j, (px, py) in enumerate(chips):
            sends.append((src, land.at[me], 4 * i + 1 + j, (px, py, c)))
            arrivals.append((land.at[4 * px + 2 * py + c], 4 * i + 1 + j))
    return sends, arrivals


def _plan_gather_pass(refs):
    x, y, c, chips = _chips()
    sends, arrivals = [], []
    for i in range(len(refs)):
        for j, (px, py) in enumerate(chips):
            slot = refs[i].at[4 * px + 2 * py + c]
            sends.append((slot, slot, 4 * i + j, (x, y, 1 - c)))
            arrivals.append((refs[i].at[4 * px + 2 * py + 1 - c], 4 * i + j))
        back = refs[i].at[4 * x + 2 * y + 1 - c]
        sends.append((back, back, 4 * i + 3, (x, y, 1 - c)))
        arrivals.append((refs[i].at[4 * x + 2 * y + c], 4 * i + 3))
    return sends, arrivals


def _plan_scatter_pair(refs):
    x, y, c = lax.axis_index("x"), lax.axis_index("y"), lax.axis_index("c")
    n = len(refs) // 2
    sends, arrivals = [], []
    for i in range(n):
        for q in range(4):
            sends.append((refs[i].at[q, 1 - c], refs[n + i].at[q], 4 * i + q, (x, y, 1 - c)))
            arrivals.append((refs[n + i].at[q], 4 * i + q))
    return sends, arrivals


def _plan_scatter_chips(layer):
    def plan(refs):
        x, y, c, chips = _chips()
        n = len(refs) // 2
        sends, arrivals = [], []
        for i in range(n):
            for j, (px, py) in enumerate(chips):
                sends.append((refs[i].at[2 * px + py], refs[n + i].at[layer, 2 * x + y], 3 * i + j, (px, py, c)))
                arrivals.append((refs[n + i].at[layer, 2 * px + py], 3 * i + j))
        return sends, arrivals

    return plan


def _pair_sum(parts4, from_pair, landing, layer, core, tr, name):
    _, _, rows, cols = parts4.shape

    def body(c_ref, p_ref, s_ref, l_ref, sum_ref, land_ref):
        v = (p_ref[...].astype(F32) + s_ref[...].astype(F32)).astype(BF16)
        sum_ref[...] = v
        land_ref[...] = v

    blk = pl.BlockSpec((None, tr, cols), lambda q, i, c_ref: (q, i, 0))
    return pl.pallas_call(
        body,
        grid_spec=pltpu.PrefetchScalarGridSpec(
            num_scalar_prefetch=1, grid=(4, rows // tr),
            in_specs=[pl.BlockSpec((None, None, tr, cols), lambda q, i, c_ref: (q, c_ref[0], i, 0)), blk, ANY],
            out_specs=[blk, pl.BlockSpec((None, None, tr, cols), lambda q, i, c_ref: (layer, q, i, 0))]),
        out_shape=[jax.ShapeDtypeStruct((4, rows, cols), BF16), jax.ShapeDtypeStruct(landing.shape, BF16)],
        input_output_aliases={3: 1}, compiler_params=_cp(), name=name,
    )(core, parts4, from_pair, landing)


def _travel_layout(t):
    tr = lambda a: jnp.swapaxes(a, 1, 2)
    branch = jnp.concatenate([tr(t["w_attn_o"]), tr(t["w_conv_o"]), tr(t["w_ssm_o"])], axis=2)
    return [tr(t["w_in"]), tr(t["w_ffn_in"]), t["w_ffn_out"], t["w_mix_o"], branch, t["w_ssm_glu"]]


def _native_layout(a):
    tr = lambda x: jnp.swapaxes(x, 1, 2)
    b = a[4]
    return {"w_in": tr(a[0]), "w_ffn_in": tr(a[1]), "w_ffn_out": a[2], "w_mix_o": a[3],
            "w_attn_o": tr(b[:, :, :WIDTH]), "w_conv_o": tr(b[:, :, WIDTH:2 * WIDTH]),
            "w_ssm_o": tr(b[:, :, 2 * WIDTH:]), "w_ssm_glu": a[5]}


def _embed(t):
    eye = jnp.eye(8, dtype=t.dtype)
    t = t.reshape(DEPTH, N_LANE_GROUPS, 8, SSM_GROUP, SSM_STATE)
    return (t[:, :, :, :, None, :] * eye[None, None, :, None, :, None]).reshape(DEPTH, N_LANE_GROUPS, 128, LANES_G)


def _diag_blocks(t):
    t = t.reshape(DEPTH, N_LANE_GROUPS, 8, SSM_GROUP, 8, SSM_STATE)
    return jnp.einsum("lgahap->lgahp", t).reshape(DEPTH, SSM_GROUPS, SSM_GROUP, SSM_STATE)


def _rope_tabs():
    pos = jnp.arange(SEQ, dtype=F32)
    inv_freq = ROPE_THETA ** (-jnp.arange(0, ROT_DIM, 2, dtype=F32) / ROT_DIM)
    ang = pos[:, None] * inv_freq[None, :]
    cos, sin = jnp.cos(ang), jnp.sin(ang)
    one, zero = jnp.ones((SEQ, HEAD_DIM - ROT_DIM), F32), jnp.zeros((SEQ, HEAD_DIM - ROT_DIM), F32)
    z8 = jnp.zeros((SEQ, 8), F32)
    head = lambda *p: jnp.tile(jnp.concatenate(p, axis=1), (1, 2))
    return head(cos, cos, one), head(-sin, z8, zero), head(z8, sin, zero)


def _ssm_mats(sp):
    lr, li, bbr, bbi = _ssm_prep(sp["a_re"], sp["a_im"], sp["log_dt"], sp["bt_re"], sp["bt_im"])
    lanes = SSM_GROUPS * SSM_STATE
    return {
        "a_re": lr.reshape(DEPTH, 1, lanes), "a_im": li.reshape(DEPTH, 1, lanes),
        "b_re": _embed(bbr).astype(BF16), "b_im": _embed(bbi).astype(BF16),
        "c_re": _embed(sp["c_re"]).astype(BF16), "c_im_neg": _embed(-sp["c_im"]).astype(BF16),
    }


def _layer_fwd(x, i, w, rp, mats, tabs, tie, hooks):
    q, kv, cbx, u, glog, h = _rms_mm_in(x, rp["norm_mix"][i], w["win_t"], tie)
    o = _attn_fwd(q, kv, tabs, rp["attn_sinks"][i])
    cv = _conv_fwd(cbx, rp["conv_w"], i)
    x_re, x_im, y = _ssm_fwd(u, mats, i, rp["ssm_d"])
    z = _glu_fwd(y, w["wglu"])
    x1 = _mix_fwd(x, o, cv, z, glog, rp["b_gate"], i, w["branch_t"], w["wmix"], hooks["early"](z))
    hooks["pre_ffn"](x1)
    gu, h2 = _rms_mm_ffn(x1, rp["norm_ffn"][i], w["wffn_t"])
    x2 = _ffn_out_fwd(x1, gu, w["wout"], hooks["mid"](h2))
    kept = dict(x=x, q=q, kv=kv, cbx=cbx, u=u, glog=glog, h=h, o=o, cv=cv, z=z, y=y,
                x_re=x_re, x_im=x_im, x1=x1, gu=gu, h2=h2)
    return x2, kept


def _layer_bwd(dx2, k, i, w, rp, mats, tabs, tie, hooks):
    dgu, act = _ffn_out_bwd(dx2, k["gu"], w["wout"], tie)
    g_wout = _mm_tn(act, dx2, tm=FFN_H // 2, tn=1024, name="mm_tn_ffn_out")
    g_wffn_t = _mm_tn(dgu, k["h2"], tm=FFN_H // 2, tn=1024, name="mm_tn_ffn_in")
    dx1, d_norm_ffn = _mm_rmsbwd([dgu], w["wffn_t"], k["x1"], rp["norm_ffn"][i], dx2, "mm_rmsbwd_ffn")

    mg, dya, dyc, dys, do, dcv, dz, dgl, db_gate = _mix_bwd(
        dx1, k["o"], k["cv"], k["z"], k["glog"], rp["b_gate"], i, w["branch_t"], w["wmix"],
        hooks["mid"]((g_wffn_t, g_wout, d_norm_ffn)))
    g_wmix = _mm_tn(mg, dx1, tm=1024, tn=512, name="mm_tn_mix")
    g_branch_t = _tn_branches((dya, dyc, dys), (k["o"], k["cv"], k["z"]))

    dy, ys16, da16, dd = _glu_bwd(k["y"], w["wglu"], dz, k["u"])
    g_wglu = _mm_tn(ys16, da16, tm=256, tn=512, name="mm_tn_glu")
    du, da_re, da_im, db_re, db_im, dc_re, dc_im = _ssm_bwd(dy, k["x_re"], k["x_im"], k["u"], mats, i, rp["ssm_d"])

    dcb, dcc, dcx, d_conv_w = _conv_bwd(k["cbx"], rp["conv_w"], i, dcv, hooks["late"](du))
    dq, dkv, d_sinks = _attn_bwd(k["q"], k["kv"], tabs, rp["attn_sinks"][i], do)

    pieces = [dq, dkv, dcb, dcc, dcx, du, dgl]
    g_win_t = _tn_pieces(pieces, k["h"])
    dx, d_norm_mix = _mm_rmsbwd(pieces, w["win_t"], k["x"], rp["norm_mix"][i], dx1, "mm_rmsbwd_in")

    grads = [g_win_t, g_wffn_t, g_wout, g_wmix, g_branch_t, g_wglu]
    small = dict(norm_mix=d_norm_mix, b_gate=db_gate, attn_sinks=d_sinks, ssm_d=dd, norm_ffn=d_norm_ffn,
                 conv_w=d_conv_w, da_re=da_re, da_im=da_im, db_re=db_re, db_im=db_im, dc_re=dc_re, dc_im=dc_im)
    return dx, grads, small


def _replicated_grads(sg, sp):
    stack = lambda name: jnp.stack([sg[i][name] for i in range(DEPTH)])
    cots = (stack("da_re").reshape(DEPTH, *_GS), stack("da_im").reshape(DEPTH, *_GS),
            _diag_blocks(stack("db_re")), _diag_blocks(stack("db_im")))
    d_a_re, d_a_im, d_log_dt, d_bt_re, d_bt_im = _ssm_prep_bwd(
        sp["a_re"], sp["a_im"], sp["log_dt"], sp["bt_re"], sp["bt_im"], cots)
    sgrads = {"norm_mix": stack("norm_mix"), "b_gate": stack("b_gate"),
              "attn_sinks": stack("attn_sinks")[:, :, :N_Q_HEADS], "ssm_a_re": d_a_re, "ssm_a_im": d_a_im,
              "ssm_b_re": jnp.swapaxes(d_bt_re, 2, 3), "ssm_b_im": jnp.swapaxes(d_bt_im, 2, 3),
              "ssm_c_re": _diag_blocks(stack("dc_re")), "ssm_c_im": -_diag_blocks(stack("dc_im")),
              "ssm_d": stack("ssm_d"), "ssm_log_dt": d_log_dt, "norm_ffn": stack("norm_ffn")}
    return sgrads, stack("conv_w")[:, :3]


def kernel(x, norm_mix, w_in, b_gate, attn_sinks, w_attn_o, conv_w, w_conv_o, ssm_a_re, ssm_a_im, ssm_b_re, ssm_b_im, ssm_c_re, ssm_c_im, ssm_d, ssm_log_dt, w_ssm_glu, w_ssm_o, w_mix_o, norm_ffn, w_ffn_in, w_ffn_out, norm_final, loss_target, m_norm_mix, m_w_in, m_b_gate, m_attn_sinks, m_w_attn_o, m_conv_w, m_w_conv_o, m_ssm_a_re, m_ssm_a_im, m_ssm_b_re, m_ssm_b_im, m_ssm_c_re, m_ssm_c_im, m_ssm_d, m_ssm_log_dt, m_w_ssm_glu, m_w_ssm_o, m_w_mix_o, m_norm_ffn, m_w_ffn_in, m_w_ffn_out, m_norm_final, v_norm_mix, v_w_in, v_b_gate, v_attn_sinks, v_w_attn_o, v_conv_w, v_w_conv_o, v_ssm_a_re, v_ssm_a_im, v_ssm_b_re, v_ssm_b_im, v_ssm_c_re, v_ssm_c_im, v_ssm_d, v_ssm_log_dt, v_w_ssm_glu, v_w_ssm_o, v_w_mix_o, v_norm_ffn, v_w_ffn_in, v_w_ffn_out, v_norm_final):
    big = {"w": dict(w_in=w_in, w_attn_o=w_attn_o, w_conv_o=w_conv_o, w_ssm_glu=w_ssm_glu, w_ssm_o=w_ssm_o,
                     w_mix_o=w_mix_o, w_ffn_in=w_ffn_in, w_ffn_out=w_ffn_out),
           "m": dict(w_in=m_w_in, w_attn_o=m_w_attn_o, w_conv_o=m_w_conv_o, w_ssm_glu=m_w_ssm_glu,
                     w_ssm_o=m_w_ssm_o, w_mix_o=m_w_mix_o, w_ffn_in=m_w_ffn_in, w_ffn_out=m_w_ffn_out),
           "v": dict(w_in=v_w_in, w_attn_o=v_w_attn_o, w_conv_o=v_w_conv_o, w_ssm_glu=v_w_ssm_glu,
                     w_ssm_o=v_w_ssm_o, w_mix_o=v_w_mix_o, w_ffn_in=v_w_ffn_in, w_ffn_out=v_w_ffn_out)}
    small = {"w": dict(norm_mix=norm_mix, b_gate=b_gate, attn_sinks=attn_sinks, ssm_a_re=ssm_a_re,
                       ssm_a_im=ssm_a_im, ssm_b_re=ssm_b_re, ssm_b_im=ssm_b_im, ssm_c_re=ssm_c_re,
                       ssm_c_im=ssm_c_im, ssm_d=ssm_d, ssm_log_dt=ssm_log_dt, norm_ffn=norm_ffn),
             "m": dict(norm_mix=m_norm_mix, b_gate=m_b_gate, attn_sinks=m_attn_sinks, ssm_a_re=m_ssm_a_re,
                       ssm_a_im=m_ssm_a_im, ssm_b_re=m_ssm_b_re, ssm_b_im=m_ssm_b_im, ssm_c_re=m_ssm_c_re,
                       ssm_c_im=m_ssm_c_im, ssm_d=m_ssm_d, ssm_log_dt=m_ssm_log_dt, norm_ffn=m_norm_ffn),
             "v": dict(norm_mix=v_norm_mix, b_gate=v_b_gate, attn_sinks=v_attn_sinks, ssm_a_re=v_ssm_a_re,
                       ssm_a_im=v_ssm_a_im, ssm_b_re=v_ssm_b_re, ssm_b_im=v_ssm_b_im, ssm_c_re=v_ssm_c_re,
                       ssm_c_im=v_ssm_c_im, ssm_d=v_ssm_d, ssm_log_dt=v_ssm_log_dt, norm_ffn=v_norm_ffn)}
    finals = {"w": norm_final, "m": m_norm_final, "v": v_norm_final}
    small_out_shapes = {name: a.shape for name, a in small["w"].items()}
    small_out_shapes.update(norm_final=(D_MODEL,), conv_w=(DEPTH, 3, 64))
    small_shapes = dict(small_out_shapes, norm_final=(1, D_MODEL), conv_w=(DEPTH, 3, WIDTH))
    for name in ("ssm_b_re", "ssm_b_im", "ssm_c_re", "ssm_c_im"):
        small_shapes[name] = (DEPTH, SSM_GROUPS, SSM_GROUP * SSM_STATE)
    convs = {"w": conv_w, "m": m_conv_w, "v": v_conv_w}
    mine = 4 * lax.axis_index("x") + 2 * lax.axis_index("y") + lax.axis_index("c")

    travel = {s: _travel_layout(big[s]) for s in "wmv"}
    stacked16 = list(zip(*[[a[0] for a in _travel_layout({n: w[i:i + 1].astype(BF16) for n, w in big["w"].items()})]
                           for i in range(DEPTH)]))
    rp = {"norm_mix": norm_mix[:, None], "norm_ffn": norm_ffn[:, None], "attn_sinks": attn_sinks[:, None],
          "b_gate": b_gate[:, None], "ssm_d": ssm_d[:, None]}
    sp = {"a_re": ssm_a_re, "a_im": ssm_a_im, "log_dt": ssm_log_dt[:, :, None],
          "bt_re": jnp.swapaxes(ssm_b_re, 2, 3), "bt_im": jnp.swapaxes(ssm_b_im, 2, 3),
          "c_re": ssm_c_re, "c_im": ssm_c_im}
    rows_tile = {"win_t": 368, "wffn_t": 352, "wout": 176, "wmix": 128, "branch_t": 128, "wglu": 64}
    core = lax.axis_index("c").astype(jnp.int32).reshape(1)
    no_tie = jnp.zeros((8, 128), F32)

    def place_own(srcs):
        return [lax.empty((N_DEV,) + s.shape, s.dtype) for s in srcs]

    def gather_chips(tag, i, kinds, after, extra=()):
        srcs = [stacked16[j][i] for j in kinds] + list(extra)
        s_sems, r_sems, arrays, token = _split_start(
            f"gather_chips_start_{tag}", srcs + place_own(srcs), 4 * len(srcs), _plan_gather_chips, after)
        return (tag, s_sems, r_sems, arrays), token

    def gather_pass(state, after):
        tag, s_sems, r_sems, arrays = state
        arrays = _split_wait(f"gather_chips_wait_{tag}", arrays, s_sems, r_sems, after, _plan_gather_chips)
        n = len(arrays) // 2
        s_sems, r_sems, lands, token = _split_start(
            f"gather_pass_start_{tag}", list(arrays[n:]), 4 * n, _plan_gather_pass)
        return (tag, s_sems, r_sems, lands), token

    def gather_done(state, after, kinds):
        tag, s_sems, r_sems, lands = state
        lands = _split_wait(f"gather_pass_wait_{tag}", lands, s_sems, r_sems, after, _plan_gather_pass)
        named = {KINDS[j][0]: a.reshape(N_DEV * KINDS[j][1], KINDS[j][2]) for a, j in zip(lands, kinds)}
        return named, list(lands[len(kinds):])

    all_kinds, mixer_kinds, ffn_kinds = tuple(range(len(KINDS))), (0, 3, 4, 5), (1, 2)
    no_hooks = {name: (lambda value: no_tie) for name in ("early", "pre_ffn", "mid", "late")}
    state, token = gather_chips("0m", 0, mixer_kinds, None, extra=[jnp.pad(conv_w.reshape(6, 128), ((0, 2), (0, 0)))])
    mats = _ssm_mats(dict(sp, log_dt=sp["log_dt"] + token[0, 0]))
    tabs = _rope_tabs()
    state, _ = gather_pass(state, list(mats.values()) + list(tabs))
    ffn_state, tie = gather_chips("0f", 0, ffn_kinds, state[3][0])
    w_next, (conv_all,) = gather_done(state, tabs[2], mixer_kinds)
    conv_full = conv_all[:, :6].reshape(N_DEV, DEPTH, 3, 64).transpose(1, 2, 0, 3).reshape(DEPTH, 3, WIDTH)
    rp["conv_w"] = jnp.pad(conv_full, ((0, 0), (0, 5), (0, 0)))

    act = x[0]
    weights, kept = [], []
    for i in range(DEPTH):
        w_i, hooks, held = w_next, dict(no_hooks), {}

        def early(value, ffn_state=ffn_state, held=held):
            held["ffn"], token = gather_pass(ffn_state, value)
            return token

        def pre_ffn(value, w_i=w_i, held=held):
            w_i.update(gather_done(held["ffn"], value, ffn_kinds)[0])

        hooks.update(early=early, pre_ffn=pre_ffn)
        if i + 1 < DEPTH:
            state, tie = gather_chips(f"{i + 1}m", i + 1, mixer_kinds, tie if i == 0 else w_i["win_t"])

            def mid(value, i=i, state=state, held=held):
                held["next"], token = gather_pass(state, value)
                held["next_ffn"], token = gather_chips(f"{i + 1}f", i + 1, ffn_kinds, token)
                return token

            hooks.update(mid=mid)
        act, k = _layer_fwd(act, i, w_i, rp, mats, tabs, tie, hooks)
        if i + 1 < DEPTH:
            w_next, _ = gather_done(held["next"], act, mixer_kinds)
            ffn_state, tie = held["next_ffn"], no_tie
        weights.append(w_i)
        kept.append(k)
    loss_row, dx, d_norm_final = _loss_head(act, norm_final[None], loss_target[0])
    loss = lax.psum(loss_row[0, 0], ("x", "y", "c"))

    landings = [lax.empty((DEPTH, 4, r, c), BF16) for _, r, c in KINDS]
    landings0 = [lax.empty((1, 4, r, c), BF16) for _, r, c in KINDS]

    def scatter_pair(tag, kinds, grads, after):
        parts4 = [g.reshape(4, 2, KINDS[j][1], KINDS[j][2]) for g, j in zip(grads, kinds)]
        zones = [lax.empty((4, KINDS[j][1], KINDS[j][2]), BF16) for j in kinds]
        s_sems, r_sems, arrays, token = _split_start(
            f"scatter_pair_start_{tag}", parts4 + zones, 4 * len(kinds), _plan_scatter_pair, after)
        return (tag, kinds, s_sems, r_sems, arrays), token

    def scatter_chips(state, lands, slot, after):
        tag, kinds, s_sems, r_sems, arrays = state
        arrays = _split_wait(f"scatter_pair_wait_{tag}", arrays, s_sems, r_sems, after, _plan_scatter_pair)
        n = len(kinds)
        sums, mine_lands = [], []
        for k, j in enumerate(kinds):
            name = KINDS[j][0]
            chip_sum, land = _pair_sum(arrays[k], arrays[n + k], lands[j], slot, core, rows_tile[name],
                                       f"pair_sum_{name}")
            sums.append(chip_sum)
            mine_lands.append(land)
        s_sems, r_sems, arrays, token = _split_start(
            f"scatter_chips_start_{tag}", sums + mine_lands, 3 * n, _plan_scatter_chips(slot))
        return (tag, kinds, slot, s_sems, r_sems, arrays), token

    def scatter_done(state, lands, after):
        tag, kinds, slot, s_sems, r_sems, arrays = state
        arrays = _split_wait(f"scatter_chips_wait_{tag}", arrays, s_sems, r_sems, after, _plan_scatter_chips(slot))
        lands = list(lands)
        for k, j in enumerate(kinds):
            lands[j] = arrays[len(kinds) + k]
        return lands

    sg = [None] * DEPTH
    pending, tie = None, no_tie
    for i in reversed(range(DEPTH)):
        hooks, held = dict(no_hooks), {}
        if pending is not None:
            def mid(value, i=i, pending=pending, held=held):
                held["chips"], token = scatter_chips(pending, landings, i + 1, value[2])
                if i == 0:
                    held["ffn_pair"], token = scatter_pair("0f", ffn_kinds, value[:2], token)
                return token

            hooks.update(mid=mid)
        if i == 0:
            def late(value, held=held):
                held["ffn_chips"], token = scatter_chips(held["ffn_pair"], landings0, 0, value)
                return token

            hooks.update(late=late)
        dx, grads, sg[i] = _layer_bwd(dx, kept[i], i, weights[i], rp, mats, tabs, tie, hooks)
        if pending is not None:
            landings = scatter_done(held["chips"], landings, dx)
        if i > 0:
            pending, tie = scatter_pair(str(i), all_kinds, grads, dx)
        else:
            pending, _ = scatter_pair("0m", mixer_kinds, [grads[j] for j in mixer_kinds], dx)

    sgrads, conv_grad = _replicated_grads(sg, sp)

    small_names = [name for name, _ in SMALL] + ["norm_final", "conv_w"]
    sgrads.update(norm_final=d_norm_final, conv_w=conv_grad)
    small_src = [sgrads[name].reshape(small_shapes[name]).astype(BF16) for name in small_names]
    last, tie = scatter_chips(pending, landings0, 0, small_src[0])
    s_sems, r_sems, arrays, tie = _split_start(
        "gather_small_chips_start", small_src + place_own(small_src), 4 * len(small_src), _plan_gather_chips, tie)
    small_state = ("small", s_sems, r_sems, arrays)

    big_out = []
    for j, (name, _, _) in enumerate(KINDS):
        big_out.append(_adamw(landings[j], travel["w"][j], travel["m"][j], travel["v"][j], rows_tile[name],
                              "adamw_late_" + name, groups=(1, DEPTH), tie=tie))
        tie = big_out[-1][3]
    landings0 = scatter_done(held["ffn_chips"], landings0, tie)
    landings0 = scatter_done(last, landings0, tie)
    small_state, _ = gather_pass(small_state, landings0[0])
    big_out = [_adamw(landings0[j], travel["w"][j], travel["m"][j], travel["v"][j], rows_tile[name],
                      "adamw_first_" + name, groups=(0, 1), fill=big_out[j]) for j, (name, _, _) in enumerate(KINDS)]
    big_res = [_native_layout([big_out[j][kind] for j in range(len(KINDS))]) for kind in range(4)]

    _, sparts = gather_done(small_state, big_out[-1][0], ())
    sparts = dict(zip(small_names, sparts))
    sparts["conv_w"] = lax.dynamic_slice_in_dim(sparts["conv_w"], mine * 64, 64, axis=3)
    small_res = {}
    for name in small_names:
        shape = small_shapes[name] if name != "conv_w" else (DEPTH, 3, 64)
        state = [(convs[s] if name == "conv_w" else finals[s] if name == "norm_final" else small[s][name])
                 .reshape(shape) for s in "wmv"]
        res = _adamw_small(sparts[name], *state, "adamw_" + name)
        small_res[name] = [r.reshape(state_shape) for r, state_shape in zip(res, [small_out_shapes[name]] * 4)]

    order = ["norm_mix", "w_in", "b_gate", "attn_sinks", "w_attn_o", "conv_w", "w_conv_o", "ssm_a_re", "ssm_a_im",
             "ssm_b_re", "ssm_b_im", "ssm_c_re", "ssm_c_im", "ssm_d", "ssm_log_dt", "w_ssm_glu", "w_ssm_o",
             "w_mix_o", "norm_ffn", "w_ffn_in", "w_ffn_out", "norm_final"]
    outs = [loss, dx[None]]
    for kind in range(4):
        for name in order:
            outs.append(big_res[kind][name] if name in big_res[kind] else small_res[name][kind])
    return tuple(outs)
```

```python
import functools
import math

import jax
import jax.numpy as jnp
from jax import lax
from jax.experimental import pallas as pl
from jax.experimental.pallas import tpu as pltpu

F32 = jnp.float32
BF16 = jnp.bfloat16

N_DEV = 8
DEPTH = 4
SEQ = 2048
D_MODEL = 1024
N_Q_HEADS = 8
HEAD_DIM = 64
ATTN_W = 512
KV_W = 128
BLOCK = 128
N_BLOCKS = SEQ // BLOCK
ROPE_THETA = 500000.0
ROT_DIM = 16
NEG_INF = -1e30
WIDTH = 512
SSM_GROUPS = 32
SSM_GROUP = 16
SSM_STATE = 64
SLABS = 16
CHUNK = 256
N_CHUNKS = SEQ // CHUNK
GATE_W = 3 * D_MODEL
IN_COLS = 5888
FFN_H = 2816
NORM_EPS = 1e-6
LR, B1, B2, ADAM_EPS, WD, STEP = 0.001, 0.9, 0.999, 1e-08, 0.01, 10

COL_Q, COL_KV, COL_CBX, COL_U, COL_G = 0, 512, 768, 2304, 2816
PIECE_W = (512, 256, 512, 512, 512, 512, 3072)
PIECE_OFF = tuple(sum(PIECE_W[:i]) for i in range(len(PIECE_W)))

KINDS = (("win_t", 736, 1024), ("wffn_t", 704, 1024), ("wout", 352, 1024), ("wmix", 128, 1024),
         ("branch_t", 128, 1536), ("wglu", 64, 512))

SMALL = (("norm_mix", 1024), ("b_gate", 3072), ("attn_sinks", 8), ("ssm_a_re", 2048), ("ssm_a_im", 2048),
         ("ssm_b_re", 32768), ("ssm_b_im", 32768), ("ssm_c_re", 32768), ("ssm_c_im", 32768),
         ("ssm_d", 512), ("ssm_log_dt", 32), ("norm_ffn", 1024))
SMALL_PER_LAYER = sum(n for _, n in SMALL)
CONV_N = DEPTH * 3 * WIDTH
SMALL_ROWS = 4480

VMEM_LIMIT = 56 * 1024 * 1024
NT = (((1,), (1,)), ((), ()))
TN = (((0,), (0,)), ((), ()))
MESH_ID = pl.DeviceIdType.MESH
ANY = pl.BlockSpec(memory_space=pl.ANY)
HBM = pl.BlockSpec(memory_space=pltpu.HBM)
SEM = pl.BlockSpec(memory_space=pltpu.SEMAPHORE)
EFFECT = pltpu.SideEffectType.DATAFLOW_SIDE_EFFECTING


def _cp(**kw):
    return pltpu.CompilerParams(vmem_limit_bytes=VMEM_LIMIT, **kw)


def _full(shape):
    return pl.BlockSpec(shape, lambda *_: (0,) * len(shape))


def _resident(shape):
    return pl.BlockSpec(shape, lambda *_: (0,) * len(shape), pipeline_mode=pl.Buffered(1))


def _mm_tn(a, b, *, tm, tn, name):
    k, m = a.shape
    n = b.shape[1]

    def body(a_ref, b_ref, o_ref):
        o_ref[...] = lax.dot_general(a_ref[...].astype(BF16), b_ref[...].astype(BF16), TN,
                                     preferred_element_type=F32).astype(BF16)

    return pl.pallas_call(
        body, grid=(m // tm, n // tn),
        in_specs=[pl.BlockSpec((k, tm), lambda i, j: (0, i)), pl.BlockSpec((k, tn), lambda i, j: (0, j))],
        out_specs=pl.BlockSpec((tm, tn), lambda i, j: (i, j)),
        out_shape=jax.ShapeDtypeStruct((m, n), BF16), compiler_params=_cp(), name=name)(a, b)


def _rms_rows(xv, g):
    r = lax.rsqrt(jnp.mean(xv * xv, axis=-1, keepdims=True) + NORM_EPS)
    return ((xv * r) * g).astype(BF16)


def _rms_mm_in(x, g, wt, tie):
    tt = 512
    widths = (ATTN_W, 2 * KV_W, 3 * WIDTH, WIDTH, GATE_W)
    offs = (COL_Q, COL_KV, COL_CBX, COL_U, COL_G)

    def body(x_ref, g_ref, w_ref, tie_ref, q_ref, kv_ref, cbx_ref, u_ref, gl_ref, h_ref):
        h = _rms_rows(x_ref[...], g_ref[...])
        h_ref[...] = h
        prod = lax.dot_general(h, w_ref[...], NT, preferred_element_type=F32)
        for ref, o, w in zip((q_ref, kv_ref, cbx_ref, u_ref, gl_ref), offs, widths):
            ref[...] = prod[:, o:o + w]

    row = lambda w: pl.BlockSpec((tt, w), lambda i: (i, 0))
    sds = jax.ShapeDtypeStruct
    return pl.pallas_call(
        body, grid=(SEQ // tt,), in_specs=[row(D_MODEL), _full((1, D_MODEL)), _resident((IN_COLS, D_MODEL)), ANY],
        out_specs=[row(ATTN_W), row(2 * KV_W), row(3 * WIDTH), row(WIDTH), row(GATE_W), row(D_MODEL)],
        out_shape=[sds((SEQ, ATTN_W), F32), sds((SEQ, 2 * KV_W), F32), sds((SEQ, 3 * WIDTH), F32),
                   sds((SEQ, WIDTH), F32), sds((SEQ, GATE_W), F32), sds((SEQ, D_MODEL), BF16)],
        compiler_params=_cp(), name="rms_mm_in")(x, g, wt, tie)


def _rms_mm_ffn(x, g, wt):
    tt = 256

    def body(x_ref, g_ref, w_ref, act_ref, up_ref, silu_ref, dsilu_ref, h_ref):
        h = _rms_rows(x_ref[...], g_ref[...])
        h_ref[...] = h
        prod = lax.dot_general(h, w_ref[...], NT, preferred_element_type=F32)
        gt, up = prod[:, :FFN_H], prod[:, FFN_H:]
        sg = jax.nn.sigmoid(gt)
        silu = gt * sg
        act_ref[...] = (silu * up).astype(BF16)
        up_ref[...] = up.astype(BF16)
        silu_ref[...] = silu.astype(BF16)
        dsilu_ref[...] = (sg + silu * (1.0 - sg)).astype(BF16)

    row = lambda w: pl.BlockSpec((tt, w), lambda i: (i, 0))
    return pl.pallas_call(
        body, grid=(SEQ // tt,), in_specs=[row(D_MODEL), _full((1, D_MODEL)), _resident((2 * FFN_H, D_MODEL))],
        out_specs=[row(FFN_H)] * 4 + [row(D_MODEL)],
        out_shape=[jax.ShapeDtypeStruct((SEQ, FFN_H), BF16)] * 4 + [jax.ShapeDtypeStruct((SEQ, D_MODEL), BF16)],
        compiler_params=_cp(), name="rms_mm_ffn")(x, g, wt)


def _mm_rmsbwd(pieces, wt, x, g, dres, name):
    tt = 512
    widths = [p.shape[1] for p in pieces]
    offs = [sum(widths[:i]) for i in range(len(widths))]
    n = len(pieces)

    def body(*refs):
        p_refs, (w_ref, x_ref, g_ref, r_ref, dx_ref, dg_ref) = refs[:n], refs[n:]

        @pl.when(pl.program_id(0) == 0)
        def _():
            dg_ref[...] = jnp.zeros_like(dg_ref)

        dh = jnp.zeros((tt, D_MODEL), F32)
        for p_ref, o, w in zip(p_refs, offs, widths):
            dh += jnp.dot(p_ref[...], w_ref[o:o + w, :], preferred_element_type=F32)
        xv = x_ref[...]
        r = lax.rsqrt(jnp.mean(xv * xv, axis=-1, keepdims=True) + NORM_EPS)
        xh = xv * r
        gy = dh * g_ref[...]
        dx_ref[...] = r_ref[...] + r * (gy - xh * jnp.mean(gy * xh, axis=-1, keepdims=True))
        dg_ref[...] += jnp.sum(dh * xh, axis=0, keepdims=True)

    row = lambda w: pl.BlockSpec((tt, w), lambda i: (i, 0))
    return pl.pallas_call(
        body, grid=(SEQ // tt,),
        in_specs=[row(w) for w in widths] + [_resident(wt.shape), row(D_MODEL), _full((1, D_MODEL)), row(D_MODEL)],
        out_specs=[row(D_MODEL), _full((1, D_MODEL))],
        out_shape=[jax.ShapeDtypeStruct((SEQ, D_MODEL), F32), jax.ShapeDtypeStruct((1, D_MODEL), F32)],
        compiler_params=_cp(), name=name)(*pieces, wt, x, g, dres)


def _tn_pieces(pieces, h):
    tk, tn = 512, 512
    nk = SEQ // tk
    n = len(pieces)

    def body(*refs):
        p_refs, (h_ref, o_ref, acc_ref) = refs[:n], refs[n:]
        kk = pl.program_id(1)

        @pl.when(kk == 0)
        def _():
            acc_ref[...] = jnp.zeros_like(acc_ref)

        hv = h_ref[...]
        for p_ref, o, w in zip(p_refs, PIECE_OFF, PIECE_W):
            acc_ref[o:o + w, :] += lax.dot_general(p_ref[...], hv, TN, preferred_element_type=F32)

        @pl.when(kk == nk - 1)
        def _():
            o_ref[...] = acc_ref[...].astype(BF16)

    return pl.pallas_call(
        body, grid=(D_MODEL // tn, nk),
        in_specs=[pl.BlockSpec((tk, w), lambda j, kk: (kk, 0)) for w in PIECE_W]
        + [pl.BlockSpec((tk, tn), lambda j, kk: (kk, j))],
        out_specs=pl.BlockSpec((IN_COLS, tn), lambda j, kk: (0, j)),
        out_shape=jax.ShapeDtypeStruct((IN_COLS, D_MODEL), BF16),
        scratch_shapes=[pltpu.VMEM((IN_COLS, tn), F32)], compiler_params=_cp(), name="tn_pieces")(*pieces, h)


def _tn_branches(dys, acts):
    tk = 512
    nk = SEQ // tk

    def body(d0, d1, d2, a0, a1, a2, o_ref, acc_ref):
        kk = pl.program_id(0)

        @pl.when(kk == 0)
        def _():
            acc_ref[...] = jnp.zeros_like(acc_ref)

        for j, (d, a) in enumerate(((d0, a0), (d1, a1), (d2, a2))):
            acc_ref[:, WIDTH * j:WIDTH * (j + 1)] += lax.dot_general(d[...], a[...], TN, preferred_element_type=F32)

        @pl.when(kk == nk - 1)
        def _():
            o_ref[...] = acc_ref[...].astype(BF16)

    row = lambda w: pl.BlockSpec((tk, w), lambda kk: (kk, 0))
    return pl.pallas_call(
        body, grid=(nk,), in_specs=[row(D_MODEL)] * 3 + [row(WIDTH)] * 3,
        out_specs=_full((D_MODEL, 3 * WIDTH)), out_shape=jax.ShapeDtypeStruct((D_MODEL, 3 * WIDTH), BF16),
        scratch_shapes=[pltpu.VMEM((D_MODEL, 3 * WIDTH), F32)], compiler_params=_cp(), name="tn_branches",
    )(*dys, *acts)


def _rope(t, c, a, b):
    return t * c + pltpu.roll(t, 120, axis=1) * a + pltpu.roll(t, 8, axis=1) * b


def _rope_t(d, c, a, b):
    return d * c + pltpu.roll(d * a, 8, axis=1) + pltpu.roll(d * b, 120, axis=1)


def _band_sides(band):
    left = lax.broadcasted_iota(jnp.int32, band.shape, 1) < HEAD_DIM
    h0 = jnp.where(left, band, 0.0)
    h1 = jnp.where(left, 0.0, band)
    r0 = pltpu.roll(h0, HEAD_DIM, axis=1)
    r1 = pltpu.roll(h1, HEAD_DIM, axis=1)
    return ((h0.astype(BF16), r0.astype(BF16)), (r1.astype(BF16), h1.astype(BF16)))


def _attn_mask(i):
    qi = lax.broadcasted_iota(jnp.int32, (2 * BLOCK, 2 * BLOCK), 0) % BLOCK
    kj = lax.broadcasted_iota(jnp.int32, (2 * BLOCK, 2 * BLOCK), 1)
    delta = qi + BLOCK - kj
    return (delta >= 0) & (delta < BLOCK) & ((kj >= BLOCK) | (i > 0))


def _attn_probs(s, ok, sink):
    s = jnp.where(ok, s * (HEAD_DIM ** -0.5), NEG_INF)
    m = jnp.maximum(jnp.max(s, axis=-1, keepdims=True), sink)
    p = jnp.exp(s - m)
    es = jnp.exp(sink - m)
    inv = 1.0 / (jnp.sum(p, axis=-1, keepdims=True) + es)
    return p * inv, es * inv


def _kv_group(qs, ks, vs, kh, sink_ref):
    q2 = jnp.concatenate([qs[2 * kh], qs[2 * kh + 1]], axis=0)
    kst = jnp.concatenate([ks[kh][0], ks[kh][1]], axis=0)
    vst = jnp.concatenate([vs[kh][0], vs[kh][1]], axis=0)
    top = lax.broadcasted_iota(jnp.int32, (2 * BLOCK, 1), 0) < BLOCK
    sinks = [jnp.where(top, sink_ref[0, 4 * kh + h], sink_ref[0, 4 * kh + 2 + h]) for h in range(2)]
    return q2, kst, vst, sinks


def _attn_load(q_ref, kvc_ref, kvp_ref, tc_ref, ta_ref, tb_ref, pc_ref, pa_ref, pb_ref):
    c, a, b = tc_ref[...], ta_ref[...], tb_ref[...]
    kc = _rope(kvc_ref[:, :KV_W], c, a, b)
    kp = _rope(kvp_ref[:, :KV_W], pc_ref[...], pa_ref[...], pb_ref[...])
    kband = jnp.concatenate([kp, kc], axis=0)
    vband = jnp.concatenate([kvp_ref[:, KV_W:], kvc_ref[:, KV_W:]], axis=0)
    qs = [_rope(q_ref[:, 128 * j:128 * (j + 1)], c, a, b).astype(BF16) for j in range(4)]
    return qs, _band_sides(kband), _band_sides(vband), (c, a, b)


def _attn_specs(clamp):
    cur = lambda i: (clamp(i), 0)
    prev = lambda i: (jnp.maximum(clamp(i) - 1, 0), 0)
    return [
        pl.BlockSpec((BLOCK, ATTN_W), cur), pl.BlockSpec((BLOCK, 2 * KV_W), cur),
        pl.BlockSpec((BLOCK, 2 * KV_W), prev),
        pl.BlockSpec((BLOCK, 128), cur), pl.BlockSpec((BLOCK, 128), cur), pl.BlockSpec((BLOCK, 128), cur),
        pl.BlockSpec((BLOCK, 128), prev), pl.BlockSpec((BLOCK, 128), prev), pl.BlockSpec((BLOCK, 128), prev),
        pl.BlockSpec(memory_space=pltpu.SMEM),
    ]


def _attn_fwd(q, kv, tabs, sinks):
    tc, ta, tb = tabs

    def body(q_ref, kvc_ref, kvp_ref, tc_ref, ta_ref, tb_ref, pc_ref, pa_ref, pb_ref, sink_ref, o_ref):
        i = pl.program_id(0)
        qs, ks, vs, _ = _attn_load(q_ref, kvc_ref, kvp_ref, tc_ref, ta_ref, tb_ref, pc_ref, pa_ref, pb_ref)
        ok = _attn_mask(i)
        for kh in range(2):
            q2, kst, vst, sinks = _kv_group(qs, ks, vs, kh, sink_ref)
            s = lax.dot_general(q2, kst, NT, preferred_element_type=F32)
            pn = [_attn_probs(s[:, 2 * BLOCK * h:2 * BLOCK * (h + 1)], ok, sinks[h])[0].astype(BF16) for h in range(2)]
            o2 = jnp.dot(jnp.concatenate(pn, axis=1), vst, preferred_element_type=F32).astype(BF16)
            for r in range(2):
                j = 2 * kh + r
                o_ref[:, 128 * j:128 * (j + 1)] = o2[BLOCK * r:BLOCK * (r + 1)]

    return pl.pallas_call(
        body, grid=(N_BLOCKS,), in_specs=_attn_specs(lambda i: i),
        out_specs=pl.BlockSpec((BLOCK, ATTN_W), lambda i: (i, 0)),
        out_shape=jax.ShapeDtypeStruct((SEQ, ATTN_W), BF16), compiler_params=_cp(), name="attn_fwd",
    )(q, kv, kv, tc, ta, tb, tc, ta, tb, sinks)


def _attn_bwd(q, kv, tabs, sinks, do):
    tc, ta, tb = tabs
    last = N_BLOCKS - 1
    clamp = lambda i: jnp.minimum(i, last)

    def place(full, side, kh):
        left = lax.broadcasted_iota(jnp.int32, full.shape, 1) < HEAD_DIM
        valid = jnp.where(left, full, 0.0) if side == 0 else jnp.where(left, 0.0, full)
        return valid if side == kh else pltpu.roll(valid, HEAD_DIM, axis=1)

    def body(q_ref, kvc_ref, kvp_ref, tc_ref, ta_ref, tb_ref, pc_ref, pa_ref, pb_ref, sink_ref, do_ref,
             dq_ref, dkv_ref, ds_ref, carry_ref):
        i = pl.program_id(0)

        @pl.when(i == 0)
        def _():
            ds_ref[...] = jnp.zeros_like(ds_ref)
            carry_ref[...] = jnp.zeros_like(carry_ref)

        @pl.when(i > last)
        def _():
            dkv_ref[...] = carry_ref[...].astype(BF16)

        @pl.when(i <= last)
        def _():
            qs, ks, vs, (c, a, b) = _attn_load(q_ref, kvc_ref, kvp_ref, tc_ref, ta_ref, tb_ref,
                                               pc_ref, pa_ref, pb_ref)
            ok = _attn_mask(i)
            dk = jnp.zeros((2 * BLOCK, 128), F32)
            dv = jnp.zeros((2 * BLOCK, 128), F32)
            dsink = jnp.zeros((1, 128), F32)
            lane = lax.broadcasted_iota(jnp.int32, (1, 128), 1)
            for kh in range(2):
                q2, kst, vst, sinks = _kv_group(qs, ks, vs, kh, sink_ref)
                do2 = jnp.concatenate([do_ref[:, 128 * (2 * kh + r):128 * (2 * kh + r + 1)] for r in range(2)],
                                      axis=0).astype(BF16)
                s = lax.dot_general(q2, kst, NT, preferred_element_type=F32)
                dp = lax.dot_general(do2, vst, NT, preferred_element_type=F32)
                pns, dss = [], []
                for h in range(2):
                    cols = slice(2 * BLOCK * h, 2 * BLOCK * (h + 1))
                    pn, ps = _attn_probs(s[:, cols], ok, sinks[h])
                    dr = jnp.sum(pn * dp[:, cols], axis=-1, keepdims=True)
                    pns.append(pn.astype(BF16))
                    dss.append((pn * (dp[:, cols] - dr) * (HEAD_DIM ** -0.5)).astype(BF16))
                    for r in range(2):
                        part = -jnp.sum((ps * dr)[BLOCK * r:BLOCK * (r + 1)])
                        dsink += jnp.where(lane == 4 * kh + 2 * r + h, part, 0.0)
                ds2, pn2 = jnp.concatenate(dss, axis=1), jnp.concatenate(pns, axis=1)
                dq2 = jnp.dot(ds2, kst, preferred_element_type=F32)
                dk2 = lax.dot_general(ds2, q2, TN, preferred_element_type=F32)
                dv2 = lax.dot_general(pn2, do2, TN, preferred_element_type=F32)
                for h in range(2):
                    dk += place(dk2[2 * BLOCK * h:2 * BLOCK * (h + 1)], h, kh)
                    dv += place(dv2[2 * BLOCK * h:2 * BLOCK * (h + 1)], h, kh)
                for r in range(2):
                    j = 2 * kh + r
                    dq_ref[:, 128 * j:128 * (j + 1)] = _rope_t(dq2[BLOCK * r:BLOCK * (r + 1)], c, a, b).astype(BF16)
            ds_ref[...] += dsink
            dk_prev = _rope_t(dk[:BLOCK], pc_ref[...], pa_ref[...], pb_ref[...])
            dk_cur = _rope_t(dk[BLOCK:], c, a, b)
            prev = jnp.concatenate([dk_prev, dv[:BLOCK]], axis=1)
            dkv_ref[...] = (carry_ref[...] + prev).astype(BF16)
            carry_ref[...] = jnp.concatenate([dk_cur, dv[BLOCK:]], axis=1)

    return pl.pallas_call(
        body, grid=(N_BLOCKS + 1,),
        in_specs=_attn_specs(clamp) + [pl.BlockSpec((BLOCK, ATTN_W), lambda i: (clamp(i), 0))],
        out_specs=[pl.BlockSpec((BLOCK, ATTN_W), lambda i: (clamp(i), 0)),
                   pl.BlockSpec((BLOCK, 2 * KV_W), lambda i: (jnp.maximum(i - 1, 0), 0)),
                   pl.BlockSpec((1, 128), lambda i: (0, 0))],
        out_shape=[jax.ShapeDtypeStruct((SEQ, ATTN_W), BF16), jax.ShapeDtypeStruct((SEQ, 2 * KV_W), BF16),
                   jax.ShapeDtypeStruct((1, 128), F32)],
        scratch_shapes=[pltpu.VMEM((BLOCK, 2 * KV_W), F32)], compiler_params=_cp(), name="attn_bwd",
    )(q, kv, kv, tc, ta, tb, tc, ta, tb, sinks, do)


def _shift_down(z, k):
    row = lax.broadcasted_iota(jnp.int32, z.shape, 0)
    return jnp.where(row < k, 0.0, pltpu.roll(z, k, axis=0))


def _shift_up(z, k):
    n = z.shape[0]
    row = lax.broadcasted_iota(jnp.int32, z.shape, 0)
    return jnp.where(row >= n - k, 0.0, pltpu.roll(z, n - k, axis=0))


def _conv_specs():
    nb = WIDTH // 128
    return [pl.BlockSpec((SEQ, 128), lambda j: (0, j)), pl.BlockSpec((SEQ, 128), lambda j: (0, nb + j)),
            pl.BlockSpec((SEQ, 128), lambda j: (0, 2 * nb + j)), pl.BlockSpec((None, 8, 128), lambda j: (0, 0, j))]


def _conv_fwd(cbx, cw, layer):
    def body(cb_ref, cc_ref, cx_ref, w_ref, o_ref):
        z = cc_ref[...] * cx_ref[...]
        s = w_ref[0:1, :] * _shift_down(z, 2) + w_ref[1:2, :] * _shift_down(z, 1) + w_ref[2:3, :] * z
        o_ref[...] = (cb_ref[...] * s).astype(BF16)

    specs = _conv_specs()
    specs[3] = pl.BlockSpec((None, 8, 128), lambda j: (layer, 0, j))
    return pl.pallas_call(
        body, grid=(WIDTH // 128,), in_specs=specs,
        out_specs=pl.BlockSpec((SEQ, 128), lambda j: (0, j)),
        out_shape=jax.ShapeDtypeStruct((SEQ, WIDTH), BF16), compiler_params=_cp(), name="conv_fwd",
    )(cbx, cbx, cbx, cw)


def _conv_bwd(cbx, cw, layer, dout, tie):
    def body(cb_ref, cc_ref, cx_ref, w_ref, do_ref, tie_ref, dcb_ref, dcc_ref, dcx_ref, dw_ref):
        cc, cx = cc_ref[...], cx_ref[...]
        z = cc * cx
        z1, z2 = _shift_down(z, 1), _shift_down(z, 2)
        w0, w1, w2 = w_ref[0:1, :], w_ref[1:2, :], w_ref[2:3, :]
        dout = do_ref[...]
        ds = dout * cb_ref[...]
        dcb_ref[...] = (dout * (w0 * z2 + w1 * z1 + w2 * z)).astype(BF16)
        dz = w2 * ds + w1 * _shift_up(ds, 1) + w0 * _shift_up(ds, 2)
        dcc_ref[...] = (dz * cx).astype(BF16)
        dcx_ref[...] = (dz * cc).astype(BF16)
        rows = [jnp.sum(ds * zz, axis=0, keepdims=True) for zz in (z2, z1, z)]
        dw_ref[...] = jnp.concatenate(rows + [jnp.zeros((5, 128), F32)], axis=0)

    col = lambda j: (0, j)
    specs = _conv_specs()
    specs[3] = pl.BlockSpec((None, 8, 128), lambda j: (layer, 0, j))
    return pl.pallas_call(
        body, grid=(WIDTH // 128,), in_specs=specs + [pl.BlockSpec((SEQ, 128), col), ANY],
        out_specs=[pl.BlockSpec((SEQ, 128), col), pl.BlockSpec((SEQ, 128), col), pl.BlockSpec((SEQ, 128), col),
                   pl.BlockSpec((8, 128), col)],
        out_shape=[jax.ShapeDtypeStruct((SEQ, WIDTH), BF16)] * 3 + [jax.ShapeDtypeStruct((8, WIDTH), F32)],
        compiler_params=_cp(), name="conv_bwd",
    )(cbx, cbx, cbx, cw, dout, tie)


def _ssm_prep_math(a_re, a_im, log_dt, bt_re, bt_im):
    dt = jnp.exp(log_dt)
    er = jnp.exp(a_re * dt)
    lr = er * jnp.cos(a_im * dt)
    li = er * jnp.sin(a_im * dt)
    n2 = a_re * a_re + a_im * a_im
    cr = ((lr - 1.0) * a_re + li * a_im) / n2
    ci = (li * a_re - (lr - 1.0) * a_im) / n2
    cr3, ci3 = cr[:, None, :], ci[:, None, :]
    return lr, li, cr3 * bt_re - ci3 * bt_im, cr3 * bt_im + ci3 * bt_re


_GS = (SSM_GROUPS, SSM_STATE)
_GHS = (SSM_GROUPS, SSM_GROUP, SSM_STATE)


def _layered(shape):
    return pl.BlockSpec((None,) + shape, lambda l: (l,) + (0,) * len(shape))


def _ssm_prep(a_re, a_im, log_dt, bt_re, bt_im):
    def body(ar, ai, ld, br, bi, o0, o1, o2, o3):
        outs = _ssm_prep_math(ar[...], ai[...], ld[...], br[...], bi[...])
        for o, v in zip((o0, o1, o2, o3), outs):
            o[...] = v

    shapes = [_GS, _GS, _GHS, _GHS]
    return pl.pallas_call(
        body, grid=(DEPTH,), in_specs=[_layered(s) for s in (_GS, _GS, (SSM_GROUPS, 1), _GHS, _GHS)],
        out_specs=[_layered(s) for s in shapes],
        out_shape=[jax.ShapeDtypeStruct((DEPTH,) + s, F32) for s in shapes],
        name="ssm_prep")(a_re, a_im, log_dt, bt_re, bt_im)


def _ssm_prep_bwd(a_re, a_im, log_dt, bt_re, bt_im, cots):
    def body(ar, ai, ld, br, bi, c0, c1, c2, c3, o0, o1, o2, o3, o4):
        _, vjp = jax.vjp(_ssm_prep_math, ar[...], ai[...], ld[...], br[...], bi[...])
        for o, v in zip((o0, o1, o2, o3, o4), vjp((c0[...], c1[...], c2[...], c3[...]))):
            o[...] = v

    ins = (_GS, _GS, (SSM_GROUPS, 1), _GHS, _GHS)
    return pl.pallas_call(
        body, grid=(DEPTH,), in_specs=[_layered(s) for s in ins + (_GS, _GS, _GHS, _GHS)],
        out_specs=[_layered(s) for s in ins],
        out_shape=[jax.ShapeDtypeStruct((DEPTH,) + s, F32) for s in ins],
        name="ssm_prep_bwd")(a_re, a_im, log_dt, bt_re, bt_im, *cots)


LANES_G = 512
N_LANE_GROUPS = SSM_GROUPS * SSM_STATE // LANES_G


def _scan_in_place(xr_ref, xi_ref, ar, ai, reverse):
    shape = (N_CHUNKS, xr_ref.shape[1])
    ar, ai = jnp.broadcast_to(ar, shape), jnp.broadcast_to(ai, shape)

    def rows(tau):
        t = (CHUNK - 1 - tau) if reverse else tau
        return pl.ds(pl.multiple_of(t * N_CHUNKS, N_CHUNKS), N_CHUNKS)

    def step(tau, carry):
        sr, si = carry
        return ar * sr - ai * si + xr_ref[rows(tau), :], ar * si + ai * sr + xi_ref[rows(tau), :]

    zero = jnp.zeros(shape, F32)
    er, ei = lax.fori_loop(0, CHUNK, step, (zero, zero), unroll=8)
    qr, qi = ar, ai
    for _ in range(8):
        qr, qi = qr * qr - qi * qi, 2.0 * qr * qi
    shift = _shift_up if reverse else _shift_down
    for k in (1, 2, 4):
        sr, si = shift(er, k), shift(ei, k)
        er, ei = er + qr * sr - qi * si, ei + qr * si + qi * sr
        qr, qi = qr * qr - qi * qi, 2.0 * qr * qi
    start = (shift(er, 1), shift(ei, 1))

    def write(tau, carry):
        sr, si = step(tau, carry)
        xr_ref[rows(tau), :] = sr
        xi_ref[rows(tau), :] = si
        return sr, si

    return write, start


def _ssm_specs(layer):
    col = lambda w: pl.BlockSpec((SEQ, w), lambda g: (0, g))
    diag = pl.BlockSpec((None, None, 128, LANES_G), lambda g: (layer, g, 0, 0))
    vec = pl.BlockSpec((None, 1, LANES_G), lambda g: (layer, 0, g))
    return col, diag, vec


def _to_scan_order(src_ref, dst_ref):
    def move(tau, _):
        dst_ref[pl.ds(pl.multiple_of(tau * N_CHUNKS, N_CHUNKS), N_CHUNKS), :] = src_ref[pl.ds(tau, N_CHUNKS, stride=CHUNK), :]
        return 0

    lax.fori_loop(0, CHUNK, move, 0, unroll=8)


def _to_time_order(src_ref, dst_ref, dtype):
    for j in range(N_CHUNKS):
        dst_ref[pl.ds(j * CHUNK, CHUNK), :] = src_ref[pl.ds(j, CHUNK, stride=N_CHUNKS), :].astype(dtype)


def _ssm_fwd(u, mats, layer, d):
    def body(u_ref, d_ref, br_ref, bi_ref, cr_ref, ci_ref, ar_ref, ai_ref, xr_ref, xi_ref, y_ref, us_ref):
        _to_scan_order(u_ref, us_ref)
        uv = us_ref[...].astype(BF16)
        xr_ref[...] = jnp.dot(uv, br_ref[...], preferred_element_type=F32)
        xi_ref[...] = jnp.dot(uv, bi_ref[...], preferred_element_type=F32)
        write, start = _scan_in_place(xr_ref, xi_ref, ar_ref[...], ai_ref[...], False)
        lax.fori_loop(0, CHUNK, write, start, unroll=8)
        y = lax.dot_general(xr_ref[...].astype(BF16), cr_ref[...], NT, preferred_element_type=F32)
        y += lax.dot_general(xi_ref[...].astype(BF16), ci_ref[...], NT, preferred_element_type=F32)
        us_ref[...] = y + d_ref[...] * us_ref[...]
        _to_time_order(us_ref, y_ref, F32)

    col, diag, vec = _ssm_specs(layer)
    return pl.pallas_call(
        body, grid=(N_LANE_GROUPS,),
        in_specs=[col(128), pl.BlockSpec((None, 1, 128), lambda g: (layer, 0, g)),
                  diag, diag, diag, diag, vec, vec],
        out_specs=[col(LANES_G), col(LANES_G), col(128)],
        out_shape=[jax.ShapeDtypeStruct((SEQ, SSM_GROUPS * SSM_STATE), F32)] * 2
        + [jax.ShapeDtypeStruct((SEQ, WIDTH), F32)],
        scratch_shapes=[pltpu.VMEM((SEQ, 128), F32)], compiler_params=_cp(), name="ssm_fwd",
    )(u, d, mats["b_re"], mats["b_im"], mats["c_re"], mats["c_im_neg"], mats["a_re"], mats["a_im"])


def _ssm_bwd(dy, x_re, x_im, u, mats, layer, d):
    def body(dyt_ref, ut_ref, d_ref, xr_ref, xi_ref, br_ref, bi_ref, cr_ref, ci_ref, ar_ref, ai_ref,
             du_ref, dar_ref, dai_ref, dbr_ref, dbi_ref, dcr_ref, dci_ref, lr_ref, li_ref, dys_ref, u_ref):
        _to_scan_order(dyt_ref, dys_ref)
        _to_scan_order(ut_ref, u_ref)
        dy = dys_ref[...].astype(BF16)
        lr_ref[...] = jnp.dot(dy, cr_ref[...], preferred_element_type=F32)
        li_ref[...] = jnp.dot(dy, ci_ref[...], preferred_element_type=F32)
        write, start = _scan_in_place(lr_ref, li_ref, ar_ref[...], -ai_ref[...], True)

        def rows(t):
            return pl.ds(pl.multiple_of(t * N_CHUNKS, N_CHUNKS), N_CHUNKS)

        def grad(acc, lam, xpr, xpi):
            return acc[0] + xpr * lam[0] + xpi * lam[1], acc[1] + xpr * lam[1] - xpi * lam[0]

        def down(tau, carry):
            lam = write(tau, carry[0])
            t = CHUNK - 2 - tau
            return lam, grad(carry[1], lam, xr_ref[rows(t), :], xi_ref[rows(t), :])

        zero = jnp.zeros((N_CHUNKS, LANES_G), F32)
        lam, acc = lax.fori_loop(0, CHUNK - 1, down, (start, (zero, zero)), unroll=5)
        lam = write(CHUNK - 1, lam)
        last = rows(CHUNK - 1)
        acc = grad(acc, lam, _shift_down(xr_ref[last, :], 1), _shift_down(xi_ref[last, :], 1))
        dar_ref[...] = jnp.sum(acc[0], axis=0, keepdims=True)
        dai_ref[...] = jnp.sum(acc[1], axis=0, keepdims=True)

        l_re, l_im = lr_ref[...].astype(BF16), li_ref[...].astype(BF16)
        du = lax.dot_general(l_re, br_ref[...], NT, preferred_element_type=F32)
        du += lax.dot_general(l_im, bi_ref[...], NT, preferred_element_type=F32)
        dys_ref[...] = du + dys_ref[...] * d_ref[...]
        _to_time_order(dys_ref, du_ref, BF16)
        uv = u_ref[...].astype(BF16)
        dbr_ref[...] = lax.dot_general(uv, l_re, TN, preferred_element_type=F32)
        dbi_ref[...] = lax.dot_general(uv, l_im, TN, preferred_element_type=F32)
        dcr_ref[...] = lax.dot_general(dy, xr_ref[...].astype(BF16), TN, preferred_element_type=F32)
        dci_ref[...] = lax.dot_general(dy, xi_ref[...].astype(BF16), TN, preferred_element_type=F32)

    col, diag, vec = _ssm_specs(layer)
    out_vec = pl.BlockSpec((1, LANES_G), lambda g: (0, g))
    out_blk = pl.BlockSpec((None, 128, LANES_G), lambda g: (g, 0, 0))
    sds = jax.ShapeDtypeStruct
    return pl.pallas_call(
        body, grid=(N_LANE_GROUPS,),
        in_specs=[col(128), col(128), pl.BlockSpec((None, 1, 128), lambda g: (layer, 0, g)),
                  col(LANES_G), col(LANES_G), diag, diag, diag, diag, vec, vec],
        out_specs=[col(128), out_vec, out_vec, out_blk, out_blk, out_blk, out_blk],
        out_shape=[sds((SEQ, WIDTH), BF16)] + [sds((1, SSM_GROUPS * SSM_STATE), F32)] * 2
        + [sds((N_LANE_GROUPS, 128, LANES_G), F32)] * 4,
        scratch_shapes=[pltpu.VMEM((SEQ, LANES_G), F32)] * 2 + [pltpu.VMEM((SEQ, 128), F32)] * 2,
        compiler_params=_cp(), name="ssm_bwd",
    )(dy, u, d, x_re, x_im, mats["b_re"], mats["b_im"], mats["c_re"], mats["c_im_neg"],
      mats["a_re"], mats["a_im"])


_GELU_C = math.sqrt(2.0 / math.pi)


def _gelu(y):
    return 0.5 * y * (1.0 + jnp.tanh(_GELU_C * (y + 0.044715 * (y * y * y))))


def _glu_fwd(y, wglu):
    tt = 512

    def body(y_ref, w_ref, z_ref):
        ys = _gelu(y_ref[...])
        a = jnp.dot(ys.astype(BF16), w_ref[...], preferred_element_type=F32)
        z_ref[...] = (ys * jax.nn.sigmoid(a)).astype(BF16)

    blk = pl.BlockSpec((tt, WIDTH), lambda i: (i, 0))
    return pl.pallas_call(body, grid=(SEQ // tt,), in_specs=[blk, _full((WIDTH, WIDTH))], out_specs=blk,
                          out_shape=jax.ShapeDtypeStruct((SEQ, WIDTH), BF16), compiler_params=_cp(),
                          name="glu_fwd")(y, wglu)


def _glu_bwd(y, wglu, dz, u):
    tt = 512

    def body(y_ref, w_ref, dz_ref, u_ref, dy_ref, ys_ref, da_ref, dd_ref):
        @pl.when(pl.program_id(0) == 0)
        def _():
            dd_ref[...] = jnp.zeros_like(dd_ref)

        yv = y_ref[...]
        t = jnp.tanh(_GELU_C * (yv + 0.044715 * (yv * yv * yv)))
        ys = 0.5 * yv * (1.0 + t)
        ysb = ys.astype(BF16)
        sg = jax.nn.sigmoid(jnp.dot(ysb, w_ref[...], preferred_element_type=F32))
        dz = dz_ref[...].astype(F32)
        da = (dz * ys * sg * (1.0 - sg)).astype(BF16)
        dys = dz * sg + lax.dot_general(da, w_ref[...], NT, preferred_element_type=F32)
        dy = dys * (0.5 * (1.0 + t) + 0.5 * yv * (1.0 - t * t) * _GELU_C * (1.0 + 3 * 0.044715 * (yv * yv)))
        dy_ref[...] = dy
        ys_ref[...] = ysb
        da_ref[...] = da
        dd_ref[...] += jnp.sum(dy * u_ref[...], axis=0, keepdims=True)

    blk = pl.BlockSpec((tt, WIDTH), lambda i: (i, 0))
    return pl.pallas_call(
        body, grid=(SEQ // tt,), in_specs=[blk, _full((WIDTH, WIDTH)), blk, blk],
        out_specs=[blk, blk, blk, _full((1, WIDTH))],
        out_shape=[jax.ShapeDtypeStruct((SEQ, WIDTH), F32)] + [jax.ShapeDtypeStruct((SEQ, WIDTH), BF16)] * 2
        + [jax.ShapeDtypeStruct((1, WIDTH), F32)],
        compiler_params=_cp(), name="glu_bwd")(y, wglu, dz, u)


def _mix_specs(tt, layer):
    row = lambda w: pl.BlockSpec((tt, w), lambda i: (i, 0))
    gate = lambda j: pl.BlockSpec((tt, D_MODEL), lambda i: (i, j))
    wo = lambda j: pl.BlockSpec((D_MODEL, WIDTH), lambda i: (0, j))
    return [row(D_MODEL), row(WIDTH), row(WIDTH), row(WIDTH), gate(0), gate(1), gate(2),
            pl.BlockSpec((None, 1, GATE_W), lambda i: (layer, 0, 0)), wo(0), wo(1), wo(2),
            _full((D_MODEL, D_MODEL))]


def _mix_branches(o_ref, c_ref, z_ref, g_refs, b_ref, wa_ref, wc_ref, ws_ref):
    ys = [lax.dot_general(r[...], w[...], NT, preferred_element_type=F32)
          for r, w in ((o_ref, wa_ref), (c_ref, wc_ref), (z_ref, ws_ref))]
    gates = [jax.nn.sigmoid(g_refs[j][...] + b_ref[:, D_MODEL * j:D_MODEL * (j + 1)]) for j in range(3)]
    return ys, gates


def _mix_fwd(x, o, cv, z, glog, b_gate, layer, wbt, wmix, tie):
    tt = 256

    def body(x_ref, o_ref, c_ref, z_ref, g0, g1, g2, b_ref, wa_ref, wc_ref, ws_ref, wm_ref, tie_ref, x1_ref):
        ys, gates = _mix_branches(o_ref, c_ref, z_ref, (g0, g1, g2), b_ref, wa_ref, wc_ref, ws_ref)
        merged = gates[0] * ys[0] + gates[1] * ys[1] + gates[2] * ys[2]
        x1_ref[...] = x_ref[...] + jnp.dot(merged.astype(BF16), wm_ref[...], preferred_element_type=F32)

    return pl.pallas_call(
        body, grid=(SEQ // tt,), in_specs=_mix_specs(tt, layer) + [ANY],
        out_specs=pl.BlockSpec((tt, D_MODEL), lambda i: (i, 0)),
        out_shape=jax.ShapeDtypeStruct((SEQ, D_MODEL), F32), compiler_params=_cp(), name="mix_fwd",
    )(x, o, cv, z, glog, glog, glog, b_gate, wbt, wbt, wbt, wmix, tie)


def _mix_bwd(dx1, o, cv, z, glog, b_gate, layer, wbt, wmix, tie):
    tt = 256

    def body(dx_ref, o_ref, c_ref, z_ref, g0, g1, g2, b_ref, wa_ref, wc_ref, ws_ref, wm_ref, tie_ref,
             mg_ref, dya_ref, dyc_ref, dys_ref, do_ref, dc_ref, dz_ref, dgl_ref, db_ref):
        @pl.when(pl.program_id(0) == 0)
        def _():
            db_ref[...] = jnp.zeros_like(db_ref)

        ys, gates = _mix_branches(o_ref, c_ref, z_ref, (g0, g1, g2), b_ref, wa_ref, wc_ref, ws_ref)
        mg_ref[...] = (gates[0] * ys[0] + gates[1] * ys[1] + gates[2] * ys[2]).astype(BF16)
        dm = lax.dot_general(dx_ref[...].astype(BF16), wm_ref[...], NT, preferred_element_type=F32)
        for j, (dy_ref, w_ref, d_ref) in enumerate(((dya_ref, wa_ref, do_ref), (dyc_ref, wc_ref, dc_ref),
                                                    (dys_ref, ws_ref, dz_ref))):
            dy = (dm * gates[j]).astype(BF16)
            dy_ref[...] = dy
            d_ref[...] = jnp.dot(dy, w_ref[...], preferred_element_type=F32)
            dgl = dm * ys[j] * gates[j] * (1.0 - gates[j])
            dgl_ref[:, D_MODEL * j:D_MODEL * (j + 1)] = dgl.astype(BF16)
            db_ref[:, D_MODEL * j:D_MODEL * (j + 1)] += jnp.sum(dgl, axis=0, keepdims=True)

    row = lambda w: pl.BlockSpec((tt, w), lambda i: (i, 0))
    sds = jax.ShapeDtypeStruct
    return pl.pallas_call(
        body, grid=(SEQ // tt,), in_specs=_mix_specs(tt, layer) + [ANY],
        out_specs=[row(D_MODEL)] * 4 + [row(WIDTH)] * 3 + [row(GATE_W), _full((1, GATE_W))],
        out_shape=[sds((SEQ, D_MODEL), BF16)] * 4 + [sds((SEQ, WIDTH), F32)] * 3
        + [sds((SEQ, GATE_W), BF16), sds((1, GATE_W), F32)],
        compiler_params=_cp(), name="mix_bwd",
    )(dx1, o, cv, z, glog, glog, glog, b_gate, wbt, wbt, wbt, wmix, tie)


def _ffn_out_fwd(x1, act, wout, tie):
    tt = 512

    def body(x_ref, a_ref, w_ref, tie_ref, o_ref):
        o_ref[...] = x_ref[...] + jnp.dot(a_ref[...], w_ref[...], preferred_element_type=F32)

    row = lambda w: pl.BlockSpec((tt, w), lambda i: (i, 0))
    return pl.pallas_call(
        body, grid=(SEQ // tt,), in_specs=[row(D_MODEL), row(FFN_H), _full((FFN_H, D_MODEL)), ANY],
        out_specs=row(D_MODEL), out_shape=jax.ShapeDtypeStruct((SEQ, D_MODEL), F32),
        compiler_params=_cp(), name="ffn_out_fwd")(x1, act, wout, tie)


def _ffn_out_bwd(dx2, up, silu, dsilu, wout, tie):
    tt = 256

    def body(dx_ref, up_ref, silu_ref, dsilu_ref, w_ref, tie_ref, dgu_ref):
        dact = lax.dot_general(dx_ref[...].astype(BF16), w_ref[...], NT, preferred_element_type=F32)
        dgu_ref[:, :FFN_H] = (dact * up_ref[...].astype(F32) * dsilu_ref[...].astype(F32)).astype(BF16)
        dgu_ref[:, FFN_H:] = (dact * silu_ref[...].astype(F32)).astype(BF16)

    row = lambda w: pl.BlockSpec((tt, w), lambda i: (i, 0))
    return pl.pallas_call(
        body, grid=(SEQ // tt,),
        in_specs=[row(D_MODEL), row(FFN_H), row(FFN_H), row(FFN_H), _full((FFN_H, D_MODEL)), ANY],
        out_specs=row(2 * FFN_H), out_shape=jax.ShapeDtypeStruct((SEQ, 2 * FFN_H), BF16),
        compiler_params=_cp(), name="ffn_out_bwd")(dx2, up, silu, dsilu, wout, tie)


def _loss_head(x, g, target):
    tt = 256

    def body(x_ref, g_ref, t_ref, loss_ref, dx_ref, dg_ref):
        @pl.when(pl.program_id(0) == 0)
        def _():
            loss_ref[...] = jnp.zeros_like(loss_ref)
            dg_ref[...] = jnp.zeros_like(dg_ref)

        xv = x_ref[...]
        r = lax.rsqrt(jnp.mean(xv * xv, axis=-1, keepdims=True) + NORM_EPS)
        xh = xv * r
        err = xh * g_ref[...] - t_ref[...]
        loss_ref[...] += 0.5 * jnp.sum(jnp.mean(err * err, axis=-1, keepdims=True))
        dy = err * (1.0 / D_MODEL)
        gy = dy * g_ref[...]
        dx_ref[...] = r * (gy - xh * jnp.mean(gy * xh, axis=-1, keepdims=True))
        dg_ref[...] += jnp.sum(dy * xh, axis=0, keepdims=True)

    row = pl.BlockSpec((tt, D_MODEL), lambda i: (i, 0))
    return pl.pallas_call(
        body, grid=(SEQ // tt,), in_specs=[row, _full((1, D_MODEL)), row],
        out_specs=[_full((1, 128)), row, _full((1, D_MODEL))],
        out_shape=[jax.ShapeDtypeStruct((1, 128), F32), jax.ShapeDtypeStruct((SEQ, D_MODEL), F32),
                   jax.ShapeDtypeStruct((1, D_MODEL), F32)],
        compiler_params=_cp(), name="loss_head")(x, g, target)


def _adam_math(g, w, m, v):
    nm = B1 * m + (1.0 - B1) * g
    nv = B2 * v + (1.0 - B2) * (g * g)
    m_hat = nm / (1.0 - B1 ** STEP)
    v_hat = nv / (1.0 - B2 ** STEP)
    return -LR * (m_hat / (jnp.sqrt(v_hat) + ADAM_EPS) + WD * w), nm, nv


def _adamw_small(parts, w, m, v, name):
    def body(p_ref, w_ref, m_ref, v_ref, g_ref, d_ref, nm_ref, nv_ref):
        g = p_ref[0].astype(F32)
        for k in range(1, N_DEV):
            g = g + p_ref[k].astype(F32)
        g_ref[...] = g
        d_ref[...], nm_ref[...], nv_ref[...] = _adam_math(g, w_ref[...], m_ref[...], v_ref[...])

    out_shape = [jax.ShapeDtypeStruct(w.shape, F32)] * 4
    if w.ndim < 3:
        return pl.pallas_call(body, out_shape=out_shape, name=name)(parts, w, m, v)
    rest = w.shape[1:]
    zeros = (0,) * len(rest)
    blk = pl.BlockSpec((None,) + rest, lambda l: (l,) + zeros)
    return pl.pallas_call(
        body, grid=(w.shape[0],),
        in_specs=[pl.BlockSpec((N_DEV, None) + rest, lambda l: (0, l) + zeros), blk, blk, blk],
        out_specs=[blk] * 4, out_shape=out_shape, name=name)(parts, w, m, v)


def _adamw(parts, w, m, v, tr, name, groups=None, fill=None, tie=None):
    n_groups, rows, cols = w.shape
    n_parts = parts.shape[1]
    lo, hi = groups if groups is not None else (0, n_groups)

    def body(p_ref, w_ref, m_ref, v_ref, *rest):
        g_ref, d_ref, nm_ref, nv_ref = rest[-4:]
        g = p_ref[0].astype(F32)
        for k in range(1, n_parts):
            g = g + p_ref[k].astype(F32)
        nm = B1 * m_ref[...] + (1.0 - B1) * g
        nv = B2 * v_ref[...] + (1.0 - B2) * (g * g)
        m_hat = nm / (1.0 - B1 ** STEP)
        v_hat = nv / (1.0 - B2 ** STEP)
        g_ref[...] = g
        d_ref[...] = -LR * (m_hat / (jnp.sqrt(v_hat) + ADAM_EPS) + WD * w_ref[...])
        nm_ref[...] = nm
        nv_ref[...] = nv

    blk = pl.BlockSpec((None, tr, cols), lambda l, i: (l + lo, i, 0))
    p_lo = lo if parts.shape[0] == n_groups else 0
    extra = ([] if fill is None else list(fill)) + ([] if tie is None else [tie])
    return pl.pallas_call(
        body, grid=(hi - lo, rows // tr),
        in_specs=[pl.BlockSpec((None, n_parts, tr, cols), lambda l, i: (l + p_lo, 0, i, 0)), blk, blk, blk]
        + [ANY] * len(extra),
        out_specs=[blk] * 4, out_shape=[jax.ShapeDtypeStruct((n_groups, rows, cols), F32)] * 4,
        input_output_aliases={} if fill is None else {4 + j: j for j in range(4)},
        compiler_params=_cp(), name=name)(parts, w, m, v, *extra)


def _split_start(name, arrays, n_sems, plan, after=None):
    n = len(arrays)
    order = [] if after is None else [after]
    n_in = n + len(order)

    def body(*refs):
        ins, send_sems, recv_sems, token = refs[:n], refs[n_in], refs[n_in + 1], refs[-1]
        for src, dst, k, to in plan(ins)[0]:
            pltpu.make_async_remote_copy(src_ref=src, dst_ref=dst, send_sem=send_sems.at[k], recv_sem=recv_sems.at[k],
                                         device_id=to, device_id_type=MESH_ID).start()
        token[...] = jnp.zeros_like(token)

    outs = pl.pallas_call(
        body, name=name,
        out_shape=(pltpu.SemaphoreType.DMA((n_sems,)), pltpu.SemaphoreType.DMA((n_sems,)),
                   *[pltpu.HBM(a.shape, a.dtype) for a in arrays], jax.ShapeDtypeStruct((8, 128), F32)),
        in_specs=[HBM] * n + [ANY] * len(order),
        out_specs=(SEM, SEM, *[HBM] * n, pl.BlockSpec(memory_space=pltpu.VMEM)),
        input_output_aliases={i: 2 + i for i in range(n)},
        compiler_params=pltpu.CompilerParams(has_side_effects=EFFECT),
    )(*[pltpu.with_memory_space_constraint(a, pltpu.HBM) for a in arrays], *order)
    return outs[0], outs[1], list(outs[2:2 + n]), outs[-1]


def _split_wait(name, arrays, send_sems, recv_sems, after, plan):
    n = len(arrays)
    order = list(after) if isinstance(after, (list, tuple)) else [after]

    def body(*refs):
        ins, s_sems, r_sems = refs[:n], refs[n], refs[n + 1]
        sends, arrivals = plan(ins)
        x, y, c = lax.axis_index("x"), lax.axis_index("y"), lax.axis_index("c")
        for src, dst, k, to in sends:
            pltpu.make_async_remote_copy(src_ref=src, dst_ref=dst, send_sem=s_sems.at[k], recv_sem=r_sems.at[k],
                                         device_id=to, device_id_type=MESH_ID).wait_send()
        for dst, k in arrivals:
            pltpu.make_async_remote_copy(src_ref=dst, dst_ref=dst, send_sem=s_sems.at[k], recv_sem=r_sems.at[k],
                                         device_id=(x, y, c), device_id_type=MESH_ID).wait_recv()

    return pl.pallas_call(
        body, name=name, out_shape=[pltpu.HBM(a.shape, a.dtype) for a in arrays],
        in_specs=[HBM] * n + [SEM, SEM] + [ANY] * len(order), out_specs=[HBM] * n,
        input_output_aliases={i: i for i in range(n)},
        compiler_params=pltpu.CompilerParams(has_side_effects=EFFECT),
    )(*arrays, send_sems, recv_sems, *order)


def _chips():
    x, y, c = lax.axis_index("x"), lax.axis_index("y"), lax.axis_index("c")
    return x, y, c, [(1 - x, y), (x, 1 - y), (1 - x, 1 - y)]


def _plan_gather_chips(refs):
    x, y, c, chips = _chips()
    me = 4 * x + 2 * y + c
    n = len(refs) // 2
    sends, arrivals = [], []
    for i in range(n):
        src, land = refs[i], refs[n + i]
        sends.append((src, land.at[me], 4 * i, (x, y, 1 - c)))
        arrivals.append((land.at[4 * x + 2 * y + 1 - c], 4 * i))
        for j, (px, py) in enumerate(chips):
            sends.append((src, land.at[me], 4 * i + 1 + j, (px, py, c)))
            arrivals.append((land.at[4 * px + 2 * py + c], 4 * i + 1 + j))
    return sends, arrivals


def _plan_gather_pass(refs):
    x, y, c, chips = _chips()
    sends, arrivals = [], []
    for i in range(len(refs)):
        for j, (px, py) in enumerate(chips):
            slot = refs[i].at[4 * px + 2 * py + c]
            sends.append((slot, slot, 4 * i + j, (x, y, 1 - c)))
            arrivals.append((refs[i].at[4 * px + 2 * py + 1 - c], 4 * i + j))
        back = refs[i].at[4 * x + 2 * y + 1 - c]
        sends.append((back, back, 4 * i + 3, (x, y, 1 - c)))
        arrivals.append((refs[i].at[4 * x + 2 * y + c], 4 * i + 3))
    return sends, arrivals


def _plan_scatter_pair(refs):
    x, y, c = lax.axis_index("x"), lax.axis_index("y"), lax.axis_index("c")
    n = len(refs) // 2
    sends, arrivals = [], []
    for i in range(n):
        for q in range(4):
            sends.append((refs[i].at[q, 1 - c], refs[n + i].at[q], 4 * i + q, (x, y, 1 - c)))
            arrivals.append((refs[n + i].at[q], 4 * i + q))
    return sends, arrivals


def _plan_scatter_chips(layer):
    def plan(refs):
        x, y, c, chips = _chips()
        n = len(refs) // 2
        sends, arrivals = [], []
        for i in range(n):
            for j, (px, py) in enumerate(chips):
                sends.append((refs[i].at[2 * px + py], refs[n + i].at[layer, 2 * x + y], 3 * i + j, (px, py, c)))
                arrivals.append((refs[n + i].at[layer, 2 * px + py], 3 * i + j))
        return sends, arrivals

    return plan


def _pair_sum(parts4, from_pair, landing, layer, core, tr, name):
    _, _, rows, cols = parts4.shape

    def body(c_ref, p_ref, s_ref, l_ref, sum_ref, land_ref):
        v = (p_ref[...].astype(F32) + s_ref[...].astype(F32)).astype(BF16)
        sum_ref[...] = v
        land_ref[...] = v

    blk = pl.BlockSpec((None, tr, cols), lambda q, i, c_ref: (q, i, 0))
    return pl.pallas_call(
        body,
        grid_spec=pltpu.PrefetchScalarGridSpec(
            num_scalar_prefetch=1, grid=(4, rows // tr),
            in_specs=[pl.BlockSpec((None, None, tr, cols), lambda q, i, c_ref: (q, c_ref[0], i, 0)), blk, ANY],
            out_specs=[blk, pl.BlockSpec((None, None, tr, cols), lambda q, i, c_ref: (layer, q, i, 0))]),
        out_shape=[jax.ShapeDtypeStruct((4, rows, cols), BF16), jax.ShapeDtypeStruct(landing.shape, BF16)],
        input_output_aliases={3: 1}, compiler_params=_cp(), name=name,
    )(core, parts4, from_pair, landing)


def _travel_layout(t):
    tr = lambda a: jnp.swapaxes(a, 1, 2)
    branch = jnp.concatenate([tr(t["w_attn_o"]), tr(t["w_conv_o"]), tr(t["w_ssm_o"])], axis=2)
    return [tr(t["w_in"]), tr(t["w_ffn_in"]), t["w_ffn_out"], t["w_mix_o"], branch, t["w_ssm_glu"]]


def _native_layout(a):
    tr = lambda x: jnp.swapaxes(x, 1, 2)
    b = a[4]
    return {"w_in": tr(a[0]), "w_ffn_in": tr(a[1]), "w_ffn_out": a[2], "w_mix_o": a[3],
            "w_attn_o": tr(b[:, :, :WIDTH]), "w_conv_o": tr(b[:, :, WIDTH:2 * WIDTH]),
            "w_ssm_o": tr(b[:, :, 2 * WIDTH:]), "w_ssm_glu": a[5]}


def _embed(t):
    eye = jnp.eye(8, dtype=t.dtype)
    t = t.reshape(DEPTH, N_LANE_GROUPS, 8, SSM_GROUP, SSM_STATE)
    return (t[:, :, :, :, None, :] * eye[None, None, :, None, :, None]).reshape(DEPTH, N_LANE_GROUPS, 128, LANES_G)


def _diag_blocks(t):
    t = t.reshape(DEPTH, N_LANE_GROUPS, 8, SSM_GROUP, 8, SSM_STATE)
    return jnp.einsum("lgahap->lgahp", t).reshape(DEPTH, SSM_GROUPS, SSM_GROUP, SSM_STATE)


def _rope_tabs():
    pos = jnp.arange(SEQ, dtype=F32)
    inv_freq = ROPE_THETA ** (-jnp.arange(0, ROT_DIM, 2, dtype=F32) / ROT_DIM)
    ang = pos[:, None] * inv_freq[None, :]
    cos, sin = jnp.cos(ang), jnp.sin(ang)
    one, zero = jnp.ones((SEQ, HEAD_DIM - ROT_DIM), F32), jnp.zeros((SEQ, HEAD_DIM - ROT_DIM), F32)
    z8 = jnp.zeros((SEQ, 8), F32)
    head = lambda *p: jnp.tile(jnp.concatenate(p, axis=1), (1, 2))
    return head(cos, cos, one), head(-sin, z8, zero), head(z8, sin, zero)


def _ssm_mats(sp):
    lr, li, bbr, bbi = _ssm_prep(sp["a_re"], sp["a_im"], sp["log_dt"], sp["bt_re"], sp["bt_im"])
    lanes = SSM_GROUPS * SSM_STATE
    return {
        "a_re": lr.reshape(DEPTH, 1, lanes), "a_im": li.reshape(DEPTH, 1, lanes),
        "b_re": _embed(bbr).astype(BF16), "b_im": _embed(bbi).astype(BF16),
        "c_re": _embed(sp["c_re"]).astype(BF16), "c_im_neg": _embed(-sp["c_im"]).astype(BF16),
    }


def _layer_fwd(x, i, w, rp, mats, tabs, tie, hooks):
    q, kv, cbx, u, glog, h = _rms_mm_in(x, rp["norm_mix"][i], w["win_t"], tie)
    o = _attn_fwd(q, kv, tabs, rp["attn_sinks"][i])
    cv = _conv_fwd(cbx, rp["conv_w"], i)
    x_re, x_im, y = _ssm_fwd(u, mats, i, rp["ssm_d"])
    z = _glu_fwd(y, w["wglu"])
    x1 = _mix_fwd(x, o, cv, z, glog, rp["b_gate"], i, w["branch_t"], w["wmix"], hooks["early"](z))
    hooks["pre_ffn"](x1)
    act, up, silu, dsilu, h2 = _rms_mm_ffn(x1, rp["norm_ffn"][i], w["wffn_t"])
    x2 = _ffn_out_fwd(x1, act, w["wout"], hooks["mid"](h2))
    kept = dict(x=x, q=q, kv=kv, cbx=cbx, u=u, glog=glog, h=h, o=o, cv=cv, z=z, y=y,
                x_re=x_re, x_im=x_im, x1=x1, act=act, up=up, silu=silu, dsilu=dsilu, h2=h2)
    return x2, kept


def _layer_bwd(dx2, k, i, w, rp, mats, tabs, tie, hooks):
    dgu = _ffn_out_bwd(dx2, k["up"], k["silu"], k["dsilu"], w["wout"], tie)
    g_wout = _mm_tn(k["act"], dx2, tm=FFN_H // 2, tn=1024, name="mm_tn_ffn_out")
    g_wffn_t = _mm_tn(dgu, k["h2"], tm=FFN_H // 2, tn=1024, name="mm_tn_ffn_in")
    dx1, d_norm_ffn = _mm_rmsbwd([dgu], w["wffn_t"], k["x1"], rp["norm_ffn"][i], dx2, "mm_rmsbwd_ffn")

    mg, dya, dyc, dys, do, dcv, dz, dgl, db_gate = _mix_bwd(
        dx1, k["o"], k["cv"], k["z"], k["glog"], rp["b_gate"], i, w["branch_t"], w["wmix"],
        hooks["mid"]((g_wffn_t, g_wout, d_norm_ffn)))
    g_wmix = _mm_tn(mg, dx1, tm=1024, tn=512, name="mm_tn_mix")
    g_branch_t = _tn_branches((dya, dyc, dys), (k["o"], k["cv"], k["z"]))

    dy, ys16, da16, dd = _glu_bwd(k["y"], w["wglu"], dz, k["u"])
    g_wglu = _mm_tn(ys16, da16, tm=256, tn=512, name="mm_tn_glu")
    du, da_re, da_im, db_re, db_im, dc_re, dc_im = _ssm_bwd(dy, k["x_re"], k["x_im"], k["u"], mats, i, rp["ssm_d"])

    dcb, dcc, dcx, d_conv_w = _conv_bwd(k["cbx"], rp["conv_w"], i, dcv, hooks["late"](du))
    dq, dkv, d_sinks = _attn_bwd(k["q"], k["kv"], tabs, rp["attn_sinks"][i], do)

    pieces = [dq, dkv, dcb, dcc, dcx, du, dgl]
    g_win_t = _tn_pieces(pieces, k["h"])
    dx, d_norm_mix = _mm_rmsbwd(pieces, w["win_t"], k["x"], rp["norm_mix"][i], dx1, "mm_rmsbwd_in")

    grads = [g_win_t, g_wffn_t, g_wout, g_wmix, g_branch_t, g_wglu]
    small = dict(norm_mix=d_norm_mix, b_gate=db_gate, attn_sinks=d_sinks, ssm_d=dd, norm_ffn=d_norm_ffn,
                 conv_w=d_conv_w, da_re=da_re, da_im=da_im, db_re=db_re, db_im=db_im, dc_re=dc_re, dc_im=dc_im)
    return dx, grads, small


def _replicated_grads(sg, sp):
    stack = lambda name: jnp.stack([sg[i][name] for i in range(DEPTH)])
    cots = (stack("da_re").reshape(DEPTH, *_GS), stack("da_im").reshape(DEPTH, *_GS),
            _diag_blocks(stack("db_re")), _diag_blocks(stack("db_im")))
    d_a_re, d_a_im, d_log_dt, d_bt_re, d_bt_im = _ssm_prep_bwd(
        sp["a_re"], sp["a_im"], sp["log_dt"], sp["bt_re"], sp["bt_im"], cots)
    sgrads = {"norm_mix": stack("norm_mix"), "b_gate": stack("b_gate"),
              "attn_sinks": stack("attn_sinks")[:, :, :N_Q_HEADS], "ssm_a_re": d_a_re, "ssm_a_im": d_a_im,
              "ssm_b_re": jnp.swapaxes(d_bt_re, 2, 3), "ssm_b_im": jnp.swapaxes(d_bt_im, 2, 3),
              "ssm_c_re": _diag_blocks(stack("dc_re")), "ssm_c_im": -_diag_blocks(stack("dc_im")),
              "ssm_d": stack("ssm_d"), "ssm_log_dt": d_log_dt, "norm_ffn": stack("norm_ffn")}
    return sgrads, stack("conv_w")[:, :3]


def kernel(x, norm_mix, w_in, b_gate, attn_sinks, w_attn_o, conv_w, w_conv_o, ssm_a_re, ssm_a_im, ssm_b_re, ssm_b_im, ssm_c_re, ssm_c_im, ssm_d, ssm_log_dt, w_ssm_glu, w_ssm_o, w_mix_o, norm_ffn, w_ffn_in, w_ffn_out, norm_final, loss_target, m_norm_mix, m_w_in, m_b_gate, m_attn_sinks, m_w_attn_o, m_conv_w, m_w_conv_o, m_ssm_a_re, m_ssm_a_im, m_ssm_b_re, m_ssm_b_im, m_ssm_c_re, m_ssm_c_im, m_ssm_d, m_ssm_log_dt, m_w_ssm_glu, m_w_ssm_o, m_w_mix_o, m_norm_ffn, m_w_ffn_in, m_w_ffn_out, m_norm_final, v_norm_mix, v_w_in, v_b_gate, v_attn_sinks, v_w_attn_o, v_conv_w, v_w_conv_o, v_ssm_a_re, v_ssm_a_im, v_ssm_b_re, v_ssm_b_im, v_ssm_c_re, v_ssm_c_im, v_ssm_d, v_ssm_log_dt, v_w_ssm_glu, v_w_ssm_o, v_w_mix_o, v_norm_ffn, v_w_ffn_in, v_w_ffn_out, v_norm_final):
    big = {"w": dict(w_in=w_in, w_attn_o=w_attn_o, w_conv_o=w_conv_o, w_ssm_glu=w_ssm_glu, w_ssm_o=w_ssm_o,
                     w_mix_o=w_mix_o, w_ffn_in=w_ffn_in, w_ffn_out=w_ffn_out),
           "m": dict(w_in=m_w_in, w_attn_o=m_w_attn_o, w_conv_o=m_w_conv_o, w_ssm_glu=m_w_ssm_glu,
                     w_ssm_o=m_w_ssm_o, w_mix_o=m_w_mix_o, w_ffn_in=m_w_ffn_in, w_ffn_out=m_w_ffn_out),
           "v": dict(w_in=v_w_in, w_attn_o=v_w_attn_o, w_conv_o=v_w_conv_o, w_ssm_glu=v_w_ssm_glu,
                     w_ssm_o=v_w_ssm_o, w_mix_o=v_w_mix_o, w_ffn_in=v_w_ffn_in, w_ffn_out=v_w_ffn_out)}
    small = {"w": dict(norm_mix=norm_mix, b_gate=b_gate, attn_sinks=attn_sinks, ssm_a_re=ssm_a_re,
                       ssm_a_im=ssm_a_im, ssm_b_re=ssm_b_re, ssm_b_im=ssm_b_im, ssm_c_re=ssm_c_re,
                       ssm_c_im=ssm_c_im, ssm_d=ssm_d, ssm_log_dt=ssm_log_dt, norm_ffn=norm_ffn),
             "m": dict(norm_mix=m_norm_mix, b_gate=m_b_gate, attn_sinks=m_attn_sinks, ssm_a_re=m_ssm_a_re,
                       ssm_a_im=m_ssm_a_im, ssm_b_re=m_ssm_b_re, ssm_b_im=m_ssm_b_im, ssm_c_re=m_ssm_c_re,
                       ssm_c_im=m_ssm_c_im, ssm_d=m_ssm_d, ssm_log_dt=m_ssm_log_dt, norm_ffn=m_norm_ffn),
             "v": dict(norm_mix=v_norm_mix, b_gate=v_b_gate, attn_sinks=v_attn_sinks, ssm_a_re=v_ssm_a_re,
                       ssm_a_im=v_ssm_a_im, ssm_b_re=v_ssm_b_re, ssm_b_im=v_ssm_b_im, ssm_c_re=v_ssm_c_re,
                       ssm_c_im=v_ssm_c_im, ssm_d=v_ssm_d, ssm_log_dt=v_ssm_log_dt, norm_ffn=v_norm_ffn)}
    finals = {"w": norm_final, "m": m_norm_final, "v": v_norm_final}
    small_out_shapes = {name: a.shape for name, a in small["w"].items()}
    small_out_shapes.update(norm_final=(D_MODEL,), conv_w=(DEPTH, 3, 64))
    small_shapes = dict(small_out_shapes, norm_final=(1, D_MODEL), conv_w=(DEPTH, 3, WIDTH))
    for name in ("ssm_b_re", "ssm_b_im", "ssm_c_re", "ssm_c_im"):
        small_shapes[name] = (DEPTH, SSM_GROUPS, SSM_GROUP * SSM_STATE)
    convs = {"w": conv_w, "m": m_conv_w, "v": v_conv_w}
    mine = 4 * lax.axis_index("x") + 2 * lax.axis_index("y") + lax.axis_index("c")

    travel = {s: _travel_layout(big[s]) for s in "wmv"}
    stacked16 = list(zip(*[[a[0] for a in _travel_layout({n: w[i:i + 1].astype(BF16) for n, w in big["w"].items()})]
                           for i in range(DEPTH)]))
    rp = {"norm_mix": norm_mix[:, None], "norm_ffn": norm_ffn[:, None], "attn_sinks": attn_sinks[:, None],
          "b_gate": b_gate[:, None], "ssm_d": ssm_d[:, None]}
    sp = {"a_re": ssm_a_re, "a_im": ssm_a_im, "log_dt": ssm_log_dt[:, :, None],
          "bt_re": jnp.swapaxes(ssm_b_re, 2, 3), "bt_im": jnp.swapaxes(ssm_b_im, 2, 3),
          "c_re": ssm_c_re, "c_im": ssm_c_im}
    rows_tile = {"win_t": 368, "wffn_t": 352, "wout": 176, "wmix": 128, "branch_t": 128, "wglu": 64}
    core = lax.axis_index("c").astype(jnp.int32).reshape(1)
    no_tie = jnp.zeros((8, 128), F32)

    def place_own(srcs):
        return [lax.empty((N_DEV,) + s.shape, s.dtype) for s in srcs]

    def gather_chips(tag, i, kinds, after, extra=()):
        srcs = [stacked16[j][i] for j in kinds] + list(extra)
        s_sems, r_sems, arrays, token = _split_start(
            f"gather_chips_start_{tag}", srcs + place_own(srcs), 4 * len(srcs), _plan_gather_chips, after)
        return (tag, s_sems, r_sems, arrays), token

    def gather_pass(state, after):
        tag, s_sems, r_sems, arrays = state
        arrays = _split_wait(f"gather_chips_wait_{tag}", arrays, s_sems, r_sems, after, _plan_gather_chips)
        n = len(arrays) // 2
        s_sems, r_sems, lands, token = _split_start(
            f"gather_pass_start_{tag}", list(arrays[n:]), 4 * n, _plan_gather_pass)
        return (tag, s_sems, r_sems, lands), token

    def gather_done(state, after, kinds):
        tag, s_sems, r_sems, lands = state
        lands = _split_wait(f"gather_pass_wait_{tag}", lands, s_sems, r_sems, after, _plan_gather_pass)
        named = {KINDS[j][0]: a.reshape(N_DEV * KINDS[j][1], KINDS[j][2]) for a, j in zip(lands, kinds)}
        return named, list(lands[len(kinds):])

    all_kinds, mixer_kinds, ffn_kinds = tuple(range(len(KINDS))), (0, 3, 4, 5), (1, 2)
    no_hooks = {name: (lambda value: no_tie) for name in ("early", "pre_ffn", "mid", "late")}
    state, token = gather_chips("0m", 0, mixer_kinds, None, extra=[jnp.pad(conv_w.reshape(6, 128), ((0, 2), (0, 0)))])
    mats = _ssm_mats(dict(sp, log_dt=sp["log_dt"] + token[0, 0]))
    tabs = _rope_tabs()
    state, _ = gather_pass(state, list(mats.values()) + list(tabs))
    ffn_state, tie = gather_chips("0f", 0, ffn_kinds, state[3][0])
    w_next, (conv_all,) = gather_done(state, tabs[2], mixer_kinds)
    conv_full = conv_all[:, :6].reshape(N_DEV, DEPTH, 3, 64).transpose(1, 2, 0, 3).reshape(DEPTH, 3, WIDTH)
    rp["conv_w"] = jnp.pad(conv_full, ((0, 0), (0, 5), (0, 0)))

    act = x[0]
    weights, kept = [], []
    for i in range(DEPTH):
        w_i, hooks, held = w_next, dict(no_hooks), {}

        def early(value, ffn_state=ffn_state, held=held):
            held["ffn"], token = gather_pass(ffn_state, value)
            return token

        def pre_ffn(value, w_i=w_i, held=held):
            w_i.update(gather_done(held["ffn"], value, ffn_kinds)[0])

        hooks.update(early=early, pre_ffn=pre_ffn)
        if i + 1 < DEPTH:
            state, tie = gather_chips(f"{i + 1}m", i + 1, mixer_kinds, tie if i == 0 else w_i["win_t"])

            def mid(value, i=i, state=state, held=held):
                held["next"], token = gather_pass(state, value)
                held["next_ffn"], token = gather_chips(f"{i + 1}f", i + 1, ffn_kinds, token)
                return token

            hooks.update(mid=mid)
        act, k = _layer_fwd(act, i, w_i, rp, mats, tabs, tie, hooks)
        if i + 1 < DEPTH:
            w_next, _ = gather_done(held["next"], act, mixer_kinds)
            ffn_state, tie = held["next_ffn"], no_tie
        weights.append(w_i)
        kept.append(k)
    loss_row, dx, d_norm_final = _loss_head(act, norm_final[None], loss_target[0])
    loss = lax.psum(loss_row[0, 0], ("x", "y", "c"))

    landings = [lax.empty((DEPTH, 4, r, c), BF16) for _, r, c in KINDS]
    landings0 = [lax.empty((1, 4, r, c), BF16) for _, r, c in KINDS]

    def scatter_pair(tag, kinds, grads, after):
        parts4 = [g.reshape(4, 2, KINDS[j][1], KINDS[j][2]) for g, j in zip(grads, kinds)]
        zones = [lax.empty((4, KINDS[j][1], KINDS[j][2]), BF16) for j in kinds]
        s_sems, r_sems, arrays, token = _split_start(
            f"scatter_pair_start_{tag}", parts4 + zones, 4 * len(kinds), _plan_scatter_pair, after)
        return (tag, kinds, s_sems, r_sems, arrays), token

    def scatter_chips(state, lands, slot, after):
        tag, kinds, s_sems, r_sems, arrays = state
        arrays = _split_wait(f"scatter_pair_wait_{tag}", arrays, s_sems, r_sems, after, _plan_scatter_pair)
        n = len(kinds)
        sums, mine_lands = [], []
        for k, j in enumerate(kinds):
            name = KINDS[j][0]
            chip_sum, land = _pair_sum(arrays[k], arrays[n + k], lands[j], slot, core, rows_tile[name],
                                       f"pair_sum_{name}")
            sums.append(chip_sum)
            mine_lands.append(land)
        s_sems, r_sems, arrays, token = _split_start(
            f"scatter_chips_start_{tag}", sums + mine_lands, 3 * n, _plan_scatter_chips(slot))
        return (tag, kinds, slot, s_sems, r_sems, arrays), token

    def scatter_done(state, lands, after):
        tag, kinds, slot, s_sems, r_sems, arrays = state
        arrays = _split_wait(f"scatter_chips_wait_{tag}", arrays, s_sems, r_sems, after, _plan_scatter_chips(slot))
        lands = list(lands)
        for k, j in enumerate(kinds):
            lands[j] = arrays[len(kinds) + k]
        return lands

    sg = [None] * DEPTH
    pending, tie = None, no_tie
    for i in reversed(range(DEPTH)):
        hooks, held = dict(no_hooks), {}
        if pending is not None:
            def mid(value, i=i, pending=pending, held=held):
                held["chips"], token = scatter_chips(pending, landings, i + 1, value[2])
                if i == 0:
                    held["ffn_pair"], token = scatter_pair("0f", ffn_kinds, value[:2], token)
                return token

            hooks.update(mid=mid)
        if i == 0:
            def late(value, held=held):
                held["ffn_chips"], token = scatter_chips(held["ffn_pair"], landings0, 0, value)
                return token

            hooks.update(late=late)
        dx, grads, sg[i] = _layer_bwd(dx, kept[i], i, weights[i], rp, mats, tabs, tie, hooks)
        if pending is not None:
            landings = scatter_done(held["chips"], landings, dx)
        if i > 0:
            pending, tie = scatter_pair(str(i), all_kinds, grads, dx)
        else:
            pending, _ = scatter_pair("0m", mixer_kinds, [grads[j] for j in mixer_kinds], dx)

    sgrads, conv_grad = _replicated_grads(sg, sp)

    small_names = [name for name, _ in SMALL] + ["norm_final", "conv_w"]
    sgrads.update(norm_final=d_norm_final, conv_w=conv_grad)
    small_src = [sgrads[name].reshape(small_shapes[name]).astype(BF16) for name in small_names]
    last, tie = scatter_chips(pending, landings0, 0, small_src[0])
    s_sems, r_sems, arrays, tie = _split_start(
        "gather_small_chips_start", small_src + place_own(small_src), 4 * len(small_src), _plan_gather_chips, tie)
    small_state = ("small", s_sems, r_sems, arrays)

    big_out = []
    for j, (name, _, _) in enumerate(KINDS):
        big_out.append(_adamw(landings[j], travel["w"][j], travel["m"][j], travel["v"][j], rows_tile[name],
                              "adamw_late_" + name, groups=(1, DEPTH), tie=tie))
        tie = big_out[-1][3]
    landings0 = scatter_done(held["ffn_chips"], landings0, tie)
    landings0 = scatter_done(last, landings0, tie)
    small_state, _ = gather_pass(small_state, landings0[0])
    big_out = [_adamw(landings0[j], travel["w"][j], travel["m"][j], travel["v"][j], rows_tile[name],
                      "adamw_first_" + name, groups=(0, 1), fill=big_out[j]) for j, (name, _, _) in enumerate(KINDS)]
    big_res = [_native_layout([big_out[j][kind] for j in range(len(KINDS))]) for kind in range(4)]

    _, sparts = gather_done(small_state, big_out[-1][0], ())
    sparts = dict(zip(small_names, sparts))
    sparts["conv_w"] = lax.dynamic_slice_in_dim(sparts["conv_w"], mine * 64, 64, axis=3)
    small_res = {}
    for name in small_names:
        shape = small_shapes[name] if name != "conv_w" else (DEPTH, 3, 64)
        state = [(convs[s] if name == "conv_w" else finals[s] if name == "norm_final" else small[s][name])
                 .reshape(shape) for s in "wmv"]
        res = _adamw_small(sparts[name], *state, "adamw_" + name)
        small_res[name] = [r.reshape(state_shape) for r, state_shape in zip(res, [small_out_shapes[name]] * 4)]

    order = ["norm_mix", "w_in", "b_gate", "attn_sinks", "w_attn_o", "conv_w", "w_conv_o", "ssm_a_re", "ssm_a_im",
             "ssm_b_re", "ssm_b_im", "ssm_c_re", "ssm_c_im", "ssm_d", "ssm_log_dt", "w_ssm_glu", "w_ssm_o",
             "w_mix_o", "norm_ffn", "w_ffn_in", "w_ffn_out", "norm_final"]
    outs = [loss, dx[None]]
    for kind in range(4):
        for name in order:
            outs.append(big_res[kind][name] if name in big_res[kind] else small_res[name][kind])
    return tuple(outs)
```

```python
import functools
import math

import jax
import jax.numpy as jnp
from jax import lax
from jax.experimental import pallas as pl
from jax.experimental.pallas import tpu as pltpu

F32 = jnp.float32
BF16 = jnp.bfloat16

N_DEV = 8
DEPTH = 4
SEQ = 2048
D_MODEL = 1024
N_Q_HEADS = 8
HEAD_DIM = 64
ATTN_W = 512
KV_W = 128
BLOCK = 128
N_BLOCKS = SEQ // BLOCK
ROPE_THETA = 500000.0
ROT_DIM = 16
NEG_INF = -1e30
WIDTH = 512
SSM_GROUPS = 32
SSM_GROUP = 16
SSM_STATE = 64
SLABS = 16
CHUNK = 256
N_CHUNKS = SEQ // CHUNK
GATE_W = 3 * D_MODEL
IN_COLS = 5888
FFN_H = 2816
NORM_EPS = 1e-6
LR, B1, B2, ADAM_EPS, WD, STEP = 0.001, 0.9, 0.999, 1e-08, 0.01, 10

COL_Q, COL_KV, COL_CBX, COL_U, COL_G = 0, 512, 768, 2304, 2816
PIECE_W = (512, 256, 512, 512, 512, 512, 3072)
PIECE_OFF = tuple(sum(PIECE_W[:i]) for i in range(len(PIECE_W)))

KINDS = (("win_t", 736, 1024), ("wffn_t", 704, 1024), ("wout", 352, 1024), ("wmix", 128, 1024),
         ("branch_t", 128, 1536), ("wglu", 64, 512))

SMALL = (("norm_mix", 1024), ("b_gate", 3072), ("attn_sinks", 8), ("ssm_a_re", 2048), ("ssm_a_im", 2048),
         ("ssm_b_re", 32768), ("ssm_b_im", 32768), ("ssm_c_re", 32768), ("ssm_c_im", 32768),
         ("ssm_d", 512), ("ssm_log_dt", 32), ("norm_ffn", 1024))
SMALL_PER_LAYER = sum(n for _, n in SMALL)
CONV_N = DEPTH * 3 * WIDTH
SMALL_ROWS = 4480

VMEM_LIMIT = 56 * 1024 * 1024
NT = (((1,), (1,)), ((), ()))
TN = (((0,), (0,)), ((), ()))
MESH_ID = pl.DeviceIdType.MESH
ANY = pl.BlockSpec(memory_space=pl.ANY)
HBM = pl.BlockSpec(memory_space=pltpu.HBM)
SEM = pl.BlockSpec(memory_space=pltpu.SEMAPHORE)
EFFECT = pltpu.SideEffectType.DATAFLOW_SIDE_EFFECTING


def _cp(**kw):
    return pltpu.CompilerParams(vmem_limit_bytes=VMEM_LIMIT, **kw)


def _full(shape):
    return pl.BlockSpec(shape, lambda *_: (0,) * len(shape))


def _resident(shape):
    return pl.BlockSpec(shape, lambda *_: (0,) * len(shape), pipeline_mode=pl.Buffered(1))


def _mm_tn(a, b, *, tm, tn, name):
    k, m = a.shape
    n = b.shape[1]

    def body(a_ref, b_ref, o_ref):
        o_ref[...] = lax.dot_general(a_ref[...].astype(BF16), b_ref[...].astype(BF16), TN,
                                     preferred_element_type=F32).astype(BF16)

    return pl.pallas_call(
        body, grid=(m // tm, n // tn),
        in_specs=[pl.BlockSpec((k, tm), lambda i, j: (0, i)), pl.BlockSpec((k, tn), lambda i, j: (0, j))],
        out_specs=pl.BlockSpec((tm, tn), lambda i, j: (i, j)),
        out_shape=jax.ShapeDtypeStruct((m, n), BF16), compiler_params=_cp(), name=name)(a, b)


def _rms_rows(xv, g):
    r = lax.rsqrt(jnp.mean(xv * xv, axis=-1, keepdims=True) + NORM_EPS)
    return ((xv * r) * g).astype(BF16)


def _rms_mm_in(x, g, wt, tie):
    tt = 512
    widths = (ATTN_W, 2 * KV_W, 3 * WIDTH, WIDTH, GATE_W)
    offs = (COL_Q, COL_KV, COL_CBX, COL_U, COL_G)

    def body(x_ref, g_ref, w_ref, tie_ref, q_ref, kv_ref, cbx_ref, u_ref, gl_ref, h_ref):
        h = _rms_rows(x_ref[...], g_ref[...])
        h_ref[...] = h
        prod = lax.dot_general(h, w_ref[...], NT, preferred_element_type=F32)
        for ref, o, w in zip((q_ref, kv_ref, cbx_ref, u_ref, gl_ref), offs, widths):
            ref[...] = prod[:, o:o + w]

    row = lambda w: pl.BlockSpec((tt, w), lambda i: (i, 0))
    sds = jax.ShapeDtypeStruct
    return pl.pallas_call(
        body, grid=(SEQ // tt,), in_specs=[row(D_MODEL), _full((1, D_MODEL)), _resident((IN_COLS, D_MODEL)), ANY],
        out_specs=[row(ATTN_W), row(2 * KV_W), row(3 * WIDTH), row(WIDTH), row(GATE_W), row(D_MODEL)],
        out_shape=[sds((SEQ, ATTN_W), F32), sds((SEQ, 2 * KV_W), F32), sds((SEQ, 3 * WIDTH), F32),
                   sds((SEQ, WIDTH), F32), sds((SEQ, GATE_W), F32), sds((SEQ, D_MODEL), BF16)],
        compiler_params=_cp(), name="rms_mm_in")(x, g, wt, tie)


def _rms_mm_ffn(x, g, wt):
    tt = 256

    def body(x_ref, g_ref, w_ref, act_ref, up_ref, silu_ref, dsilu_ref, h_ref):
        h = _rms_rows(x_ref[...], g_ref[...])
        h_ref[...] = h
        prod = lax.dot_general(h, w_ref[...], NT, preferred_element_type=F32)
        gt, up = prod[:, :FFN_H], prod[:, FFN_H:]
        sg = jax.nn.sigmoid(gt)
        silu = gt * sg
        act_ref[...] = (silu * up).astype(BF16)
        up_ref[...] = up.astype(BF16)
        silu_ref[...] = silu.astype(BF16)
        dsilu_ref[...] = (sg + silu * (1.0 - sg)).astype(BF16)

    row = lambda w: pl.BlockSpec((tt, w), lambda i: (i, 0))
    return pl.pallas_call(
        body, grid=(SEQ // tt,), in_specs=[row(D_MODEL), _full((1, D_MODEL)), _resident((2 * FFN_H, D_MODEL))],
        out_specs=[row(FFN_H)] * 4 + [row(D_MODEL)],
        out_shape=[jax.ShapeDtypeStruct((SEQ, FFN_H), BF16)] * 4 + [jax.ShapeDtypeStruct((SEQ, D_MODEL), BF16)],
        compiler_params=_cp(), name="rms_mm_ffn")(x, g, wt)


def _mm_rmsbwd(pieces, wt, x, g, dres, name):
    tt = 512
    widths = [p.shape[1] for p in pieces]
    offs = [sum(widths[:i]) for i in range(len(widths))]
    n = len(pieces)

    def body(*refs):
        p_refs, (w_ref, x_ref, g_ref, r_ref, dx_ref, dg_ref) = refs[:n], refs[n:]

        @pl.when(pl.program_id(0) == 0)
        def _():
            dg_ref[...] = jnp.zeros_like(dg_ref)

        dh = jnp.zeros((tt, D_MODEL), F32)
        for p_ref, o, w in zip(p_refs, offs, widths):
            dh += jnp.dot(p_ref[...], w_ref[o:o + w, :], preferred_element_type=F32)
        xv = x_ref[...]
        r = lax.rsqrt(jnp.mean(xv * xv, axis=-1, keepdims=True) + NORM_EPS)
        xh = xv * r
        gy = dh * g_ref[...]
        dx_ref[...] = r_ref[...] + r * (gy - xh * jnp.mean(gy * xh, axis=-1, keepdims=True))
        dg_ref[...] += jnp.sum(dh * xh, axis=0, keepdims=True)

    row = lambda w: pl.BlockSpec((tt, w), lambda i: (i, 0))
    return pl.pallas_call(
        body, grid=(SEQ // tt,),
        in_specs=[row(w) for w in widths] + [_resident(wt.shape), row(D_MODEL), _full((1, D_MODEL)), row(D_MODEL)],
        out_specs=[row(D_MODEL), _full((1, D_MODEL))],
        out_shape=[jax.ShapeDtypeStruct((SEQ, D_MODEL), F32), jax.ShapeDtypeStruct((1, D_MODEL), F32)],
        compiler_params=_cp(), name=name)(*pieces, wt, x, g, dres)


def _tn_pieces(pieces, h):
    tk, tn = 512, 512
    nk = SEQ // tk
    n = len(pieces)

    def body(*refs):
        p_refs, (h_ref, o_ref, acc_ref) = refs[:n], refs[n:]
        kk = pl.program_id(1)

        @pl.when(kk == 0)
        def _():
            acc_ref[...] = jnp.zeros_like(acc_ref)

        hv = h_ref[...]
        for p_ref, o, w in zip(p_refs, PIECE_OFF, PIECE_W):
            acc_ref[o:o + w, :] += lax.dot_general(p_ref[...], hv, TN, preferred_element_type=F32)

        @pl.when(kk == nk - 1)
        def _():
            o_ref[...] = acc_ref[...].astype(BF16)

    return pl.pallas_call(
        body, grid=(D_MODEL // tn, nk),
        in_specs=[pl.BlockSpec((tk, w), lambda j, kk: (kk, 0)) for w in PIECE_W]
        + [pl.BlockSpec((tk, tn), lambda j, kk: (kk, j))],
        out_specs=pl.BlockSpec((IN_COLS, tn), lambda j, kk: (0, j)),
        out_shape=jax.ShapeDtypeStruct((IN_COLS, D_MODEL), BF16),
        scratch_shapes=[pltpu.VMEM((IN_COLS, tn), F32)], compiler_params=_cp(), name="tn_pieces")(*pieces, h)


def _tn_branches(dys, acts):
    tk = 512
    nk = SEQ // tk

    def body(d0, d1, d2, a0, a1, a2, o_ref, acc_ref):
        kk = pl.program_id(0)

        @pl.when(kk == 0)
        def _():
            acc_ref[...] = jnp.zeros_like(acc_ref)

        for j, (d, a) in enumerate(((d0, a0), (d1, a1), (d2, a2))):
            acc_ref[:, WIDTH * j:WIDTH * (j + 1)] += lax.dot_general(d[...], a[...], TN, preferred_element_type=F32)

        @pl.when(kk == nk - 1)
        def _():
            o_ref[...] = acc_ref[...].astype(BF16)

    row = lambda w: pl.BlockSpec((tk, w), lambda kk: (kk, 0))
    return pl.pallas_call(
        body, grid=(nk,), in_specs=[row(D_MODEL)] * 3 + [row(WIDTH)] * 3,
        out_specs=_full((D_MODEL, 3 * WIDTH)), out_shape=jax.ShapeDtypeStruct((D_MODEL, 3 * WIDTH), BF16),
        scratch_shapes=[pltpu.VMEM((D_MODEL, 3 * WIDTH), F32)], compiler_params=_cp(), name="tn_branches",
    )(*dys, *acts)


def _rope(t, c, a, b):
    return t * c + pltpu.roll(t, 120, axis=1) * a + pltpu.roll(t, 8, axis=1) * b


def _rope_t(d, c, a, b):
    return d * c + pltpu.roll(d * a, 8, axis=1) + pltpu.roll(d * b, 120, axis=1)


def _band_sides(band):
    left = lax.broadcasted_iota(jnp.int32, band.shape, 1) < HEAD_DIM
    h0 = jnp.where(left, band, 0.0)
    h1 = jnp.where(left, 0.0, band)
    r0 = pltpu.roll(h0, HEAD_DIM, axis=1)
    r1 = pltpu.roll(h1, HEAD_DIM, axis=1)
    return ((h0.astype(BF16), r0.astype(BF16)), (r1.astype(BF16), h1.astype(BF16)))


def _attn_mask(i):
    qi = lax.broadcasted_iota(jnp.int32, (2 * BLOCK, 2 * BLOCK), 0) % BLOCK
    kj = lax.broadcasted_iota(jnp.int32, (2 * BLOCK, 2 * BLOCK), 1)
    delta = qi + BLOCK - kj
    return (delta >= 0) & (delta < BLOCK) & ((kj >= BLOCK) | (i > 0))


def _attn_probs(s, ok, sink):
    s = jnp.where(ok, s * (HEAD_DIM ** -0.5), NEG_INF)
    m = jnp.maximum(jnp.max(s, axis=-1, keepdims=True), sink)
    p = jnp.exp(s - m)
    es = jnp.exp(sink - m)
    inv = 1.0 / (jnp.sum(p, axis=-1, keepdims=True) + es)
    return p * inv, es * inv


def _kv_group(qs, ks, vs, kh, sink_ref):
    q2 = jnp.concatenate([qs[2 * kh], qs[2 * kh + 1]], axis=0)
    kst = jnp.concatenate([ks[kh][0], ks[kh][1]], axis=0)
    vst = jnp.concatenate([vs[kh][0], vs[kh][1]], axis=0)
    top = lax.broadcasted_iota(jnp.int32, (2 * BLOCK, 1), 0) < BLOCK
    sinks = [jnp.where(top, sink_ref[0, 4 * kh + h], sink_ref[0, 4 * kh + 2 + h]) for h in range(2)]
    return q2, kst, vst, sinks


def _attn_load(q_ref, kvc_ref, kvp_ref, tc_ref, ta_ref, tb_ref, pc_ref, pa_ref, pb_ref):
    c, a, b = tc_ref[...], ta_ref[...], tb_ref[...]
    kc = _rope(kvc_ref[:, :KV_W], c, a, b)
    kp = _rope(kvp_ref[:, :KV_W], pc_ref[...], pa_ref[...], pb_ref[...])
    kband = jnp.concatenate([kp, kc], axis=0)
    vband = jnp.concatenate([kvp_ref[:, KV_W:], kvc_ref[:, KV_W:]], axis=0)
    qs = [_rope(q_ref[:, 128 * j:128 * (j + 1)], c, a, b).astype(BF16) for j in range(4)]
    return qs, _band_sides(kband), _band_sides(vband), (c, a, b)


def _attn_specs(clamp):
    cur = lambda i: (clamp(i), 0)
    prev = lambda i: (jnp.maximum(clamp(i) - 1, 0), 0)
    return [
        pl.BlockSpec((BLOCK, ATTN_W), cur), pl.BlockSpec((BLOCK, 2 * KV_W), cur),
        pl.BlockSpec((BLOCK, 2 * KV_W), prev),
        pl.BlockSpec((BLOCK, 128), cur), pl.BlockSpec((BLOCK, 128), cur), pl.BlockSpec((BLOCK, 128), cur),
        pl.BlockSpec((BLOCK, 128), prev), pl.BlockSpec((BLOCK, 128), prev), pl.BlockSpec((BLOCK, 128), prev),
        pl.BlockSpec(memory_space=pltpu.SMEM),
    ]


def _attn_fwd(q, kv, tabs, sinks):
    tc, ta, tb = tabs

    def body(q_ref, kvc_ref, kvp_ref, tc_ref, ta_ref, tb_ref, pc_ref, pa_ref, pb_ref, sink_ref, o_ref):
        i = pl.program_id(0)
        qs, ks, vs, _ = _attn_load(q_ref, kvc_ref, kvp_ref, tc_ref, ta_ref, tb_ref, pc_ref, pa_ref, pb_ref)
        ok = _attn_mask(i)
        for kh in range(2):
            q2, kst, vst, sinks = _kv_group(qs, ks, vs, kh, sink_ref)
            s = lax.dot_general(q2, kst, NT, preferred_element_type=F32)
            pn = [_attn_probs(s[:, 2 * BLOCK * h:2 * BLOCK * (h + 1)], ok, sinks[h])[0].astype(BF16) for h in range(2)]
            o2 = jnp.dot(jnp.concatenate(pn, axis=1), vst, preferred_element_type=F32).astype(BF16)
            for r in range(2):
                j = 2 * kh + r
                o_ref[:, 128 * j:128 * (j + 1)] = o2[BLOCK * r:BLOCK * (r + 1)]

    return pl.pallas_call(
        body, grid=(N_BLOCKS,), in_specs=_attn_specs(lambda i: i),
        out_specs=pl.BlockSpec((BLOCK, ATTN_W), lambda i: (i, 0)),
        out_shape=jax.ShapeDtypeStruct((SEQ, ATTN_W), BF16), compiler_params=_cp(), name="attn_fwd",
    )(q, kv, kv, tc, ta, tb, tc, ta, tb, sinks)


def _attn_bwd(q, kv, tabs, sinks, do):
    tc, ta, tb = tabs
    last = N_BLOCKS - 1
    clamp = lambda i: jnp.minimum(i, last)

    def place(full, side, kh):
        left = lax.broadcasted_iota(jnp.int32, full.shape, 1) < HEAD_DIM
        valid = jnp.where(left, full, 0.0) if side == 0 else jnp.where(left, 0.0, full)
        return valid if side == kh else pltpu.roll(valid, HEAD_DIM, axis=1)

    def body(q_ref, kvc_ref, kvp_ref, tc_ref, ta_ref, tb_ref, pc_ref, pa_ref, pb_ref, sink_ref, do_ref,
             dq_ref, dkv_ref, ds_ref, carry_ref):
        i = pl.program_id(0)

        @pl.when(i == 0)
        def _():
            ds_ref[...] = jnp.zeros_like(ds_ref)
            carry_ref[...] = jnp.zeros_like(carry_ref)

        @pl.when(i > last)
        def _():
            dkv_ref[...] = carry_ref[...].astype(BF16)

        @pl.when(i <= last)
        def _():
            qs, ks, vs, (c, a, b) = _attn_load(q_ref, kvc_ref, kvp_ref, tc_ref, ta_ref, tb_ref,
                                               pc_ref, pa_ref, pb_ref)
            ok = _attn_mask(i)
            dk = jnp.zeros((2 * BLOCK, 128), F32)
            dv = jnp.zeros((2 * BLOCK, 128), F32)
            dsink = jnp.zeros((1, 128), F32)
            lane = lax.broadcasted_iota(jnp.int32, (1, 128), 1)
            for kh in range(2):
                q2, kst, vst, sinks = _kv_group(qs, ks, vs, kh, sink_ref)
                do2 = jnp.concatenate([do_ref[:, 128 * (2 * kh + r):128 * (2 * kh + r + 1)] for r in range(2)],
                                      axis=0).astype(BF16)
                s = lax.dot_general(q2, kst, NT, preferred_element_type=F32)
                dp = lax.dot_general(do2, vst, NT, preferred_element_type=F32)
                pns, dss = [], []
                for h in range(2):
                    cols = slice(2 * BLOCK * h, 2 * BLOCK * (h + 1))
                    pn, ps = _attn_probs(s[:, cols], ok, sinks[h])
                    dr = jnp.sum(pn * dp[:, cols], axis=-1, keepdims=True)
                    pns.append(pn.astype(BF16))
                    dss.append((pn * (dp[:, cols] - dr) * (HEAD_DIM ** -0.5)).astype(BF16))
                    for r in range(2):
                        part = -jnp.sum((ps * dr)[BLOCK * r:BLOCK * (r + 1)])
                        dsink += jnp.where(lane == 4 * kh + 2 * r + h, part, 0.0)
                ds2, pn2 = jnp.concatenate(dss, axis=1), jnp.concatenate(pns, axis=1)
                dq2 = jnp.dot(ds2, kst, preferred_element_type=F32)
                dk2 = lax.dot_general(ds2, q2, TN, preferred_element_type=F32)
                dv2 = lax.dot_general(pn2, do2, TN, preferred_element_type=F32)
                for h in range(2):
                    dk += place(dk2[2 * BLOCK * h:2 * BLOCK * (h + 1)], h, kh)
                    dv += place(dv2[2 * BLOCK * h:2 * BLOCK * (h + 1)], h, kh)
                for r in range(2):
                    j = 2 * kh + r
                    dq_ref[:, 128 * j:128 * (j + 1)] = _rope_t(dq2[BLOCK * r:BLOCK * (r + 1)], c, a, b).astype(BF16)
            ds_ref[...] += dsink
            dk_prev = _rope_t(dk[:BLOCK], pc_ref[...], pa_ref[...], pb_ref[...])
            dk_cur = _rope_t(dk[BLOCK:], c, a, b)
            prev = jnp.concatenate([dk_prev, dv[:BLOCK]], axis=1)
            dkv_ref[...] = (carry_ref[...] + prev).astype(BF16)
            carry_ref[...] = jnp.concatenate([dk_cur, dv[BLOCK:]], axis=1)

    return pl.pallas_call(
        body, grid=(N_BLOCKS + 1,),
        in_specs=_attn_specs(clamp) + [pl.BlockSpec((BLOCK, ATTN_W), lambda i: (clamp(i), 0))],
        out_specs=[pl.BlockSpec((BLOCK, ATTN_W), lambda i: (clamp(i), 0)),
                   pl.BlockSpec((BLOCK, 2 * KV_W), lambda i: (jnp.maximum(i - 1, 0), 0)),
                   pl.BlockSpec((1, 128), lambda i: (0, 0))],
        out_shape=[jax.ShapeDtypeStruct((SEQ, ATTN_W), BF16), jax.ShapeDtypeStruct((SEQ, 2 * KV_W), BF16),
                   jax.ShapeDtypeStruct((1, 128), F32)],
        scratch_shapes=[pltpu.VMEM((BLOCK, 2 * KV_W), F32)], compiler_params=_cp(), name="attn_bwd",
    )(q, kv, kv, tc, ta, tb, tc, ta, tb, sinks, do)


def _shift_down(z, k):
    row = lax.broadcasted_iota(jnp.int32, z.shape, 0)
    return jnp.where(row < k, 0.0, pltpu.roll(z, k, axis=0))


def _shift_up(z, k):
    n = z.shape[0]
    row = lax.broadcasted_iota(jnp.int32, z.shape, 0)
    return jnp.where(row >= n - k, 0.0, pltpu.roll(z, n - k, axis=0))


def _conv_specs():
    nb = WIDTH // 128
    return [pl.BlockSpec((SEQ, 128), lambda j: (0, j)), pl.BlockSpec((SEQ, 128), lambda j: (0, nb + j)),
            pl.BlockSpec((SEQ, 128), lambda j: (0, 2 * nb + j)), pl.BlockSpec((None, 8, 128), lambda j: (0, 0, j))]


def _conv_fwd(cbx, cw, layer):
    def body(cb_ref, cc_ref, cx_ref, w_ref, o_ref):
        z = cc_ref[...] * cx_ref[...]
        s = w_ref[0:1, :] * _shift_down(z, 2) + w_ref[1:2, :] * _shift_down(z, 1) + w_ref[2:3, :] * z
        o_ref[...] = (cb_ref[...] * s).astype(BF16)

    specs = _conv_specs()
    specs[3] = pl.BlockSpec((None, 8, 128), lambda j: (layer, 0, j))
    return pl.pallas_call(
        body, grid=(WIDTH // 128,), in_specs=specs,
        out_specs=pl.BlockSpec((SEQ, 128), lambda j: (0, j)),
        out_shape=jax.ShapeDtypeStruct((SEQ, WIDTH), BF16), compiler_params=_cp(), name="conv_fwd",
    )(cbx, cbx, cbx, cw)


def _conv_bwd(cbx, cw, layer, dout, tie):
    def body(cb_ref, cc_ref, cx_ref, w_ref, do_ref, tie_ref, dcb_ref, dcc_ref, dcx_ref, dw_ref):
        cc, cx = cc_ref[...], cx_ref[...]
        z = cc * cx
        z1, z2 = _shift_down(z, 1), _shift_down(z, 2)
        w0, w1, w2 = w_ref[0:1, :], w_ref[1:2, :], w_ref[2:3, :]
        dout = do_ref[...]
        ds = dout * cb_ref[...]
        dcb_ref[...] = (dout * (w0 * z2 + w1 * z1 + w2 * z)).astype(BF16)
        dz = w2 * ds + w1 * _shift_up(ds, 1) + w0 * _shift_up(ds, 2)
        dcc_ref[...] = (dz * cx).astype(BF16)
        dcx_ref[...] = (dz * cc).astype(BF16)
        rows = [jnp.sum(ds * zz, axis=0, keepdims=True) for zz in (z2, z1, z)]
        dw_ref[...] = jnp.concatenate(rows + [jnp.zeros((5, 128), F32)], axis=0)

    col = lambda j: (0, j)
    specs = _conv_specs()
    specs[3] = pl.BlockSpec((None, 8, 128), lambda j: (layer, 0, j))
    return pl.pallas_call(
        body, grid=(WIDTH // 128,), in_specs=specs + [pl.BlockSpec((SEQ, 128), col), ANY],
        out_specs=[pl.BlockSpec((SEQ, 128), col), pl.BlockSpec((SEQ, 128), col), pl.BlockSpec((SEQ, 128), col),
                   pl.BlockSpec((8, 128), col)],
        out_shape=[jax.ShapeDtypeStruct((SEQ, WIDTH), BF16)] * 3 + [jax.ShapeDtypeStruct((8, WIDTH), F32)],
        compiler_params=_cp(), name="conv_bwd",
    )(cbx, cbx, cbx, cw, dout, tie)


def _ssm_prep_math(a_re, a_im, log_dt, bt_re, bt_im):
    dt = jnp.exp(log_dt)
    er = jnp.exp(a_re * dt)
    lr = er * jnp.cos(a_im * dt)
    li = er * jnp.sin(a_im * dt)
    n2 = a_re * a_re + a_im * a_im
    cr = ((lr - 1.0) * a_re + li * a_im) / n2
    ci = (li * a_re - (lr - 1.0) * a_im) / n2
    cr3, ci3 = cr[:, None, :], ci[:, None, :]
    return lr, li, cr3 * bt_re - ci3 * bt_im, cr3 * bt_im + ci3 * bt_re


_GS = (SSM_GROUPS, SSM_STATE)
_GHS = (SSM_GROUPS, SSM_GROUP, SSM_STATE)


def _layered(shape):
    return pl.BlockSpec((None,) + shape, lambda l: (l,) + (0,) * len(shape))


def _ssm_prep(a_re, a_im, log_dt, bt_re, bt_im):
    def body(ar, ai, ld, br, bi, o0, o1, o2, o3):
        outs = _ssm_prep_math(ar[...], ai[...], ld[...], br[...], bi[...])
        for o, v in zip((o0, o1, o2, o3), outs):
            o[...] = v

    shapes = [_GS, _GS, _GHS, _GHS]
    return pl.pallas_call(
        body, grid=(DEPTH,), in_specs=[_layered(s) for s in (_GS, _GS, (SSM_GROUPS, 1), _GHS, _GHS)],
        out_specs=[_layered(s) for s in shapes],
        out_shape=[jax.ShapeDtypeStruct((DEPTH,) + s, F32) for s in shapes],
        name="ssm_prep")(a_re, a_im, log_dt, bt_re, bt_im)


def _ssm_prep_bwd(a_re, a_im, log_dt, bt_re, bt_im, cots):
    def body(ar, ai, ld, br, bi, c0, c1, c2, c3, o0, o1, o2, o3, o4):
        _, vjp = jax.vjp(_ssm_prep_math, ar[...], ai[...], ld[...], br[...], bi[...])
        for o, v in zip((o0, o1, o2, o3, o4), vjp((c0[...], c1[...], c2[...], c3[...]))):
            o[...] = v

    ins = (_GS, _GS, (SSM_GROUPS, 1), _GHS, _GHS)
    return pl.pallas_call(
        body, grid=(DEPTH,), in_specs=[_layered(s) for s in ins + (_GS, _GS, _GHS, _GHS)],
        out_specs=[_layered(s) for s in ins],
        out_shape=[jax.ShapeDtypeStruct((DEPTH,) + s, F32) for s in ins],
        name="ssm_prep_bwd")(a_re, a_im, log_dt, bt_re, bt_im, *cots)


LANES_G = 512
N_LANE_GROUPS = SSM_GROUPS * SSM_STATE // LANES_G


def _scan_in_place(xr_ref, xi_ref, ar, ai, reverse):
    shape = (N_CHUNKS, xr_ref.shape[1])
    ar, ai = jnp.broadcast_to(ar, shape), jnp.broadcast_to(ai, shape)

    def rows(tau):
        t = (CHUNK - 1 - tau) if reverse else tau
        return pl.ds(pl.multiple_of(t * N_CHUNKS, N_CHUNKS), N_CHUNKS)

    def step(tau, carry):
        sr, si = carry
        return ar * sr - ai * si + xr_ref[rows(tau), :], ar * si + ai * sr + xi_ref[rows(tau), :]

    zero = jnp.zeros(shape, F32)
    er, ei = lax.fori_loop(0, CHUNK, step, (zero, zero), unroll=8)
    qr, qi = ar, ai
    for _ in range(8):
        qr, qi = qr * qr - qi * qi, 2.0 * qr * qi
    shift = _shift_up if reverse else _shift_down
    for k in (1, 2, 4):
        sr, si = shift(er, k), shift(ei, k)
        er, ei = er + qr * sr - qi * si, ei + qr * si + qi * sr
        qr, qi = qr * qr - qi * qi, 2.0 * qr * qi
    start = (shift(er, 1), shift(ei, 1))

    def write(tau, carry):
        sr, si = step(tau, carry)
        xr_ref[rows(tau), :] = sr
        xi_ref[rows(tau), :] = si
        return sr, si

    return write, start


def _ssm_specs(layer):
    col = lambda w: pl.BlockSpec((SEQ, w), lambda g: (0, g))
    diag = pl.BlockSpec((None, None, 128, LANES_G), lambda g: (layer, g, 0, 0))
    vec = pl.BlockSpec((None, 1, LANES_G), lambda g: (layer, 0, g))
    return col, diag, vec


def _to_scan_order(src_ref, dst_ref):
    def move(tau, _):
        dst_ref[pl.ds(pl.multiple_of(tau * N_CHUNKS, N_CHUNKS), N_CHUNKS), :] = src_ref[pl.ds(tau, N_CHUNKS, stride=CHUNK), :]
        return 0

    lax.fori_loop(0, CHUNK, move, 0, unroll=8)


def _to_time_order(src_ref, dst_ref, dtype):
    for j in range(N_CHUNKS):
        dst_ref[pl.ds(j * CHUNK, CHUNK), :] = src_ref[pl.ds(j, CHUNK, stride=N_CHUNKS), :].astype(dtype)


def _ssm_fwd(u, mats, layer, d):
    def body(u_ref, d_ref, br_ref, bi_ref, cr_ref, ci_ref, ar_ref, ai_ref, xr_ref, xi_ref, y_ref, us_ref):
        _to_scan_order(u_ref, us_ref)
        uv = us_ref[...].astype(BF16)
        xr_ref[...] = jnp.dot(uv, br_ref[...], preferred_element_type=F32)
        xi_ref[...] = jnp.dot(uv, bi_ref[...], preferred_element_type=F32)
        write, start = _scan_in_place(xr_ref, xi_ref, ar_ref[...], ai_ref[...], False)
        lax.fori_loop(0, CHUNK, write, start, unroll=8)
        y = lax.dot_general(xr_ref[...].astype(BF16), cr_ref[...], NT, preferred_element_type=F32)
        y += lax.dot_general(xi_ref[...].astype(BF16), ci_ref[...], NT, preferred_element_type=F32)
        us_ref[...] = y + d_ref[...] * us_ref[...]
        _to_time_order(us_ref, y_ref, F32)

    col, diag, vec = _ssm_specs(layer)
    return pl.pallas_call(
        body, grid=(N_LANE_GROUPS,),
        in_specs=[col(128), pl.BlockSpec((None, 1, 128), lambda g: (layer, 0, g)),
                  diag, diag, diag, diag, vec, vec],
        out_specs=[col(LANES_G), col(LANES_G), col(128)],
        out_shape=[jax.ShapeDtypeStruct((SEQ, SSM_GROUPS * SSM_STATE), F32)] * 2
        + [jax.ShapeDtypeStruct((SEQ, WIDTH), F32)],
        scratch_shapes=[pltpu.VMEM((SEQ, 128), F32)], compiler_params=_cp(), name="ssm_fwd",
    )(u, d, mats["b_re"], mats["b_im"], mats["c_re"], mats["c_im_neg"], mats["a_re"], mats["a_im"])


def _ssm_bwd(dy, x_re, x_im, u, mats, layer, d):
    def body(dyt_ref, ut_ref, d_ref, xr_ref, xi_ref, br_ref, bi_ref, cr_ref, ci_ref, ar_ref, ai_ref,
             du_ref, dar_ref, dai_ref, dbr_ref, dbi_ref, dcr_ref, dci_ref, lr_ref, li_ref, dys_ref, u_ref):
        _to_scan_order(dyt_ref, dys_ref)
        _to_scan_order(ut_ref, u_ref)
        dy = dys_ref[...].astype(BF16)
        lr_ref[...] = jnp.dot(dy, cr_ref[...], preferred_element_type=F32)
        li_ref[...] = jnp.dot(dy, ci_ref[...], preferred_element_type=F32)
        write, start = _scan_in_place(lr_ref, li_ref, ar_ref[...], -ai_ref[...], True)

        def rows(t):
            return pl.ds(pl.multiple_of(t * N_CHUNKS, N_CHUNKS), N_CHUNKS)

        def grad(acc, lam, xpr, xpi):
            return acc[0] + xpr * lam[0] + xpi * lam[1], acc[1] + xpr * lam[1] - xpi * lam[0]

        def down(tau, carry):
            lam = write(tau, carry[0])
            t = CHUNK - 2 - tau
            return lam, grad(carry[1], lam, xr_ref[rows(t), :], xi_ref[rows(t), :])

        zero = jnp.zeros((N_CHUNKS, LANES_G), F32)
        lam, acc = lax.fori_loop(0, CHUNK - 1, down, (start, (zero, zero)), unroll=5)
        lam = write(CHUNK - 1, lam)
        last = rows(CHUNK - 1)
        acc = grad(acc, lam, _shift_down(xr_ref[last, :], 1), _shift_down(xi_ref[last, :], 1))
        dar_ref[...] = jnp.sum(acc[0], axis=0, keepdims=True)
        dai_ref[...] = jnp.sum(acc[1], axis=0, keepdims=True)

        l_re, l_im = lr_ref[...].astype(BF16), li_ref[...].astype(BF16)
        du = lax.dot_general(l_re, br_ref[...], NT, preferred_element_type=F32)
        du += lax.dot_general(l_im, bi_ref[...], NT, preferred_element_type=F32)
        dys_ref[...] = du + dys_ref[...] * d_ref[...]
        _to_time_order(dys_ref, du_ref, BF16)
        uv = u_ref[...].astype(BF16)
        dbr_ref[...] = lax.dot_general(uv, l_re, TN, preferred_element_type=F32)
        dbi_ref[...] = lax.dot_general(uv, l_im, TN, preferred_element_type=F32)
        dcr_ref[...] = lax.dot_general(dy, xr_ref[...].astype(BF16), TN, preferred_element_type=F32)
        dci_ref[...] = lax.dot_general(dy, xi_ref[...].astype(BF16), TN, preferred_element_type=F32)

    col, diag, vec = _ssm_specs(layer)
    out_vec = pl.BlockSpec((1, LANES_G), lambda g: (0, g))
    out_blk = pl.BlockSpec((None, 128, LANES_G), lambda g: (g, 0, 0))
    sds = jax.ShapeDtypeStruct
    return pl.pallas_call(
        body, grid=(N_LANE_GROUPS,),
        in_specs=[col(128), col(128), pl.BlockSpec((None, 1, 128), lambda g: (layer, 0, g)),
                  col(LANES_G), col(LANES_G), diag, diag, diag, diag, vec, vec],
        out_specs=[col(128), out_vec, out_vec, out_blk, out_blk, out_blk, out_blk],
        out_shape=[sds((SEQ, WIDTH), BF16)] + [sds((1, SSM_GROUPS * SSM_STATE), F32)] * 2
        + [sds((N_LANE_GROUPS, 128, LANES_G), F32)] * 4,
        scratch_shapes=[pltpu.VMEM((SEQ, LANES_G), F32)] * 2 + [pltpu.VMEM((SEQ, 128), F32)] * 2,
        compiler_params=_cp(), name="ssm_bwd",
    )(dy, u, d, x_re, x_im, mats["b_re"], mats["b_im"], mats["c_re"], mats["c_im_neg"],
      mats["a_re"], mats["a_im"])


_GELU_C = math.sqrt(2.0 / math.pi)


def _gelu(y):
    return 0.5 * y * (1.0 + jnp.tanh(_GELU_C * (y + 0.044715 * (y * y * y))))


def _glu_fwd(y, wglu):
    tt = 512

    def body(y_ref, w_ref, z_ref):
        ys = _gelu(y_ref[...])
        a = jnp.dot(ys.astype(BF16), w_ref[...], preferred_element_type=F32)
        z_ref[...] = (ys * jax.nn.sigmoid(a)).astype(BF16)

    blk = pl.BlockSpec((tt, WIDTH), lambda i: (i, 0))
    return pl.pallas_call(body, grid=(SEQ // tt,), in_specs=[blk, _full((WIDTH, WIDTH))], out_specs=blk,
                          out_shape=jax.ShapeDtypeStruct((SEQ, WIDTH), BF16), compiler_params=_cp(),
                          name="glu_fwd")(y, wglu)


def _glu_bwd(y, wglu, dz, u):
    tt = 512

    def body(y_ref, w_ref, dz_ref, u_ref, dy_ref, ys_ref, da_ref, dd_ref):
        @pl.when(pl.program_id(0) == 0)
        def _():
            dd_ref[...] = jnp.zeros_like(dd_ref)

        yv = y_ref[...]
        t = jnp.tanh(_GELU_C * (yv + 0.044715 * (yv * yv * yv)))
        ys = 0.5 * yv * (1.0 + t)
        ysb = ys.astype(BF16)
        sg = jax.nn.sigmoid(jnp.dot(ysb, w_ref[...], preferred_element_type=F32))
        dz = dz_ref[...].astype(F32)
        da = (dz * ys * sg * (1.0 - sg)).astype(BF16)
        dys = dz * sg + lax.dot_general(da, w_ref[...], NT, preferred_element_type=F32)
        dy = dys * (0.5 * (1.0 + t) + 0.5 * yv * (1.0 - t * t) * _GELU_C * (1.0 + 3 * 0.044715 * (yv * yv)))
        dy_ref[...] = dy
        ys_ref[...] = ysb
        da_ref[...] = da
        dd_ref[...] += jnp.sum(dy * u_ref[...], axis=0, keepdims=True)

    blk = pl.BlockSpec((tt, WIDTH), lambda i: (i, 0))
    return pl.pallas_call(
        body, grid=(SEQ // tt,), in_specs=[blk, _full((WIDTH, WIDTH)), blk, blk],
        out_specs=[blk, blk, blk, _full((1, WIDTH))],
        out_shape=[jax.ShapeDtypeStruct((SEQ, WIDTH), F32)] + [jax.ShapeDtypeStruct((SEQ, WIDTH), BF16)] * 2
        + [jax.ShapeDtypeStruct((1, WIDTH), F32)],
        compiler_params=_cp(), name="glu_bwd")(y, wglu, dz, u)


def _mix_specs(tt, layer):
    row = lambda w: pl.BlockSpec((tt, w), lambda i: (i, 0))
    gate = lambda j: pl.BlockSpec((tt, D_MODEL), lambda i: (i, j))
    wo = lambda j: pl.BlockSpec((D_MODEL, WIDTH), lambda i: (0, j))
    return [row(D_MODEL), row(WIDTH), row(WIDTH), row(WIDTH), gate(0), gate(1), gate(2),
            pl.BlockSpec((None, 1, GATE_W), lambda i: (layer, 0, 0)), wo(0), wo(1), wo(2),
            _full((D_MODEL, D_MODEL))]


def _mix_branches(o_ref, c_ref, z_ref, g_refs, b_ref, wa_ref, wc_ref, ws_ref):
    ys = [lax.dot_general(r[...], w[...], NT, preferred_element_type=F32)
          for r, w in ((o_ref, wa_ref), (c_ref, wc_ref), (z_ref, ws_ref))]
    gates = [jax.nn.sigmoid(g_refs[j][...] + b_ref[:, D_MODEL * j:D_MODEL * (j + 1)]) for j in range(3)]
    return ys, gates


def _mix_fwd(x, o, cv, z, glog, b_gate, layer, wbt, wmix, tie):
    tt = 256

    def body(x_ref, o_ref, c_ref, z_ref, g0, g1, g2, b_ref, wa_ref, wc_ref, ws_ref, wm_ref, tie_ref, x1_ref):
        ys, gates = _mix_branches(o_ref, c_ref, z_ref, (g0, g1, g2), b_ref, wa_ref, wc_ref, ws_ref)
        merged = gates[0] * ys[0] + gates[1] * ys[1] + gates[2] * ys[2]
        x1_ref[...] = x_ref[...] + jnp.dot(merged.astype(BF16), wm_ref[...], preferred_element_type=F32)

    return pl.pallas_call(
        body, grid=(SEQ // tt,), in_specs=_mix_specs(tt, layer) + [ANY],
        out_specs=pl.BlockSpec((tt, D_MODEL), lambda i: (i, 0)),
        out_shape=jax.ShapeDtypeStruct((SEQ, D_MODEL), F32), compiler_params=_cp(), name="mix_fwd",
    )(x, o, cv, z, glog, glog, glog, b_gate, wbt, wbt, wbt, wmix, tie)


def _mix_bwd(dx1, o, cv, z, glog, b_gate, layer, wbt, wmix, tie):
    tt = 256

    def body(dx_ref, o_ref, c_ref, z_ref, g0, g1, g2, b_ref, wa_ref, wc_ref, ws_ref, wm_ref, tie_ref,
             mg_ref, dya_ref, dyc_ref, dys_ref, do_ref, dc_ref, dz_ref, dgl_ref, db_ref):
        @pl.when(pl.program_id(0) == 0)
        def _():
            db_ref[...] = jnp.zeros_like(db_ref)

        ys, gates = _mix_branches(o_ref, c_ref, z_ref, (g0, g1, g2), b_ref, wa_ref, wc_ref, ws_ref)
        mg_ref[...] = (gates[0] * ys[0] + gates[1] * ys[1] + gates[2] * ys[2]).astype(BF16)
        dm = lax.dot_general(dx_ref[...].astype(BF16), wm_ref[...], NT, preferred_element_type=F32)
        for j, (dy_ref, w_ref, d_ref) in enumerate(((dya_ref, wa_ref, do_ref), (dyc_ref, wc_ref, dc_ref),
                                                    (dys_ref, ws_ref, dz_ref))):
            dy = (dm * gates[j]).astype(BF16)
            dy_ref[...] = dy
            d_ref[...] = jnp.dot(dy, w_ref[...], preferred_element_type=F32)
            dgl = dm * ys[j] * gates[j] * (1.0 - gates[j])
            dgl_ref[:, D_MODEL * j:D_MODEL * (j + 1)] = dgl.astype(BF16)
            db_ref[:, D_MODEL * j:D_MODEL * (j + 1)] += jnp.sum(dgl, axis=0, keepdims=True)

    row = lambda w: pl.BlockSpec((tt, w), lambda i: (i, 0))
    sds = jax.ShapeDtypeStruct
    return pl.pallas_call(
        body, grid=(SEQ // tt,), in_specs=_mix_specs(tt, layer) + [ANY],
        out_specs=[row(D_MODEL)] * 4 + [row(WIDTH)] * 3 + [row(GATE_W), _full((1, GATE_W))],
        out_shape=[sds((SEQ, D_MODEL), BF16)] * 4 + [sds((SEQ, WIDTH), F32)] * 3
        + [sds((SEQ, GATE_W), BF16), sds((1, GATE_W), F32)],
        compiler_params=_cp(), name="mix_bwd",
    )(dx1, o, cv, z, glog, glog, glog, b_gate, wbt, wbt, wbt, wmix, tie)


def _ffn_out_fwd(x1, act, wout, tie):
    tt = 512

    def body(x_ref, a_ref, w_ref, tie_ref, o_ref):
        o_ref[...] = x_ref[...] + jnp.dot(a_ref[...], w_ref[...], preferred_element_type=F32)

    row = lambda w: pl.BlockSpec((tt, w), lambda i: (i, 0))
    return pl.pallas_call(
        body, grid=(SEQ // tt,), in_specs=[row(D_MODEL), row(FFN_H), _full((FFN_H, D_MODEL)), ANY],
        out_specs=row(D_MODEL), out_shape=jax.ShapeDtypeStruct((SEQ, D_MODEL), F32),
        compiler_params=_cp(), name="ffn_out_fwd")(x1, act, wout, tie)


def _ffn_out_bwd(dx2, up, silu, dsilu, wout, tie):
    tt = 256

    def body(dx_ref, up_ref, silu_ref, dsilu_ref, w_ref, tie_ref, dgu_ref):
        dact = lax.dot_general(dx_ref[...].astype(BF16), w_ref[...], NT, preferred_element_type=F32).astype(BF16)
        dgu_ref[:, :FFN_H] = dact * up_ref[...] * dsilu_ref[...]
        dgu_ref[:, FFN_H:] = dact * silu_ref[...]

    row = lambda w: pl.BlockSpec((tt, w), lambda i: (i, 0))
    return pl.pallas_call(
        body, grid=(SEQ // tt,),
        in_specs=[row(D_MODEL), row(FFN_H), row(FFN_H), row(FFN_H), _full((FFN_H, D_MODEL)), ANY],
        out_specs=row(2 * FFN_H), out_shape=jax.ShapeDtypeStruct((SEQ, 2 * FFN_H), BF16),
        compiler_params=_cp(), name="ffn_out_bwd")(dx2, up, silu, dsilu, wout, tie)


def _loss_head(x, g, target):
    tt = 256

    def body(x_ref, g_ref, t_ref, loss_ref, dx_ref, dg_ref):
        @pl.when(pl.program_id(0) == 0)
        def _():
            loss_ref[...] = jnp.zeros_like(loss_ref)
            dg_ref[...] = jnp.zeros_like(dg_ref)

        xv = x_ref[...]
        r = lax.rsqrt(jnp.mean(xv * xv, axis=-1, keepdims=True) + NORM_EPS)
        xh = xv * r
        err = xh * g_ref[...] - t_ref[...]
        loss_ref[...] += 0.5 * jnp.sum(jnp.mean(err * err, axis=-1, keepdims=True))
        dy = err * (1.0 / D_MODEL)
        gy = dy * g_ref[...]
        dx_ref[...] = r * (gy - xh * jnp.mean(gy * xh, axis=-1, keepdims=True))
        dg_ref[...] += jnp.sum(dy * xh, axis=0, keepdims=True)

    row = pl.BlockSpec((tt, D_MODEL), lambda i: (i, 0))
    return pl.pallas_call(
        body, grid=(SEQ // tt,), in_specs=[row, _full((1, D_MODEL)), row],
        out_specs=[_full((1, 128)), row, _full((1, D_MODEL))],
        out_shape=[jax.ShapeDtypeStruct((1, 128), F32), jax.ShapeDtypeStruct((SEQ, D_MODEL), F32),
                   jax.ShapeDtypeStruct((1, D_MODEL), F32)],
        compiler_params=_cp(), name="loss_head")(x, g, target)


def _adam_math(g, w, m, v):
    nm = B1 * m + (1.0 - B1) * g
    nv = B2 * v + (1.0 - B2) * (g * g)
    m_hat = nm / (1.0 - B1 ** STEP)
    v_hat = nv / (1.0 - B2 ** STEP)
    return -LR * (m_hat / (jnp.sqrt(v_hat) + ADAM_EPS) + WD * w), nm, nv


def _adamw_small(parts, w, m, v, name):
    def body(p_ref, w_ref, m_ref, v_ref, g_ref, d_ref, nm_ref, nv_ref):
        g = p_ref[0].astype(F32)
        for k in range(1, N_DEV):
            g = g + p_ref[k].astype(F32)
        g_ref[...] = g
        d_ref[...], nm_ref[...], nv_ref[...] = _adam_math(g, w_ref[...], m_ref[...], v_ref[...])

    out_shape = [jax.ShapeDtypeStruct(w.shape, F32)] * 4
    if w.ndim < 3:
        return pl.pallas_call(body, out_shape=out_shape, name=name)(parts, w, m, v)
    rest = w.shape[1:]
    zeros = (0,) * len(rest)
    blk = pl.BlockSpec((None,) + rest, lambda l: (l,) + zeros)
    return pl.pallas_call(
        body, grid=(w.shape[0],),
        in_specs=[pl.BlockSpec((N_DEV, None) + rest, lambda l: (0, l) + zeros), blk, blk, blk],
        out_specs=[blk] * 4, out_shape=out_shape, name=name)(parts, w, m, v)


def _adamw(parts, w, m, v, tr, name, groups=None, fill=None, tie=None):
    n_groups, rows, cols = w.shape
    n_parts = parts.shape[1]
    lo, hi = groups if groups is not None else (0, n_groups)

    def body(p_ref, w_ref, m_ref, v_ref, *rest):
        g_ref, d_ref, nm_ref, nv_ref = rest[-4:]
        g = p_ref[0].astype(F32)
        for k in range(1, n_parts):
            g = g + p_ref[k].astype(F32)
        nm = B1 * m_ref[...] + (1.0 - B1) * g
        nv = B2 * v_ref[...] + (1.0 - B2) * (g * g)
        m_hat = nm / (1.0 - B1 ** STEP)
        v_hat = nv / (1.0 - B2 ** STEP)
        g_ref[...] = g
        d_ref[...] = -LR * (m_hat / (jnp.sqrt(v_hat) + ADAM_EPS) + WD * w_ref[...])
        nm_ref[...] = nm
        nv_ref[...] = nv

    blk = pl.BlockSpec((None, tr, cols), lambda l, i: (l + lo, i, 0))
    p_lo = lo if parts.shape[0] == n_groups else 0
    extra = ([] if fill is None else list(fill)) + ([] if tie is None else [tie])
    return pl.pallas_call(
        body, grid=(hi - lo, rows // tr),
        in_specs=[pl.BlockSpec((None, n_parts, tr, cols), lambda l, i: (l + p_lo, 0, i, 0)), blk, blk, blk]
        + [ANY] * len(extra),
        out_specs=[blk] * 4, out_shape=[jax.ShapeDtypeStruct((n_groups, rows, cols), F32)] * 4,
        input_output_aliases={} if fill is None else {4 + j: j for j in range(4)},
        compiler_params=_cp(), name=name)(parts, w, m, v, *extra)


def _split_start(name, arrays, n_sems, plan, after=None):
    n = len(arrays)
    order = [] if after is None else [after]
    n_in = n + len(order)

    def body(*refs):
        ins, send_sems, recv_sems, token = refs[:n], refs[n_in], refs[n_in + 1], refs[-1]
        for src, dst, k, to in plan(ins)[0]:
            pltpu.make_async_remote_copy(src_ref=src, dst_ref=dst, send_sem=send_sems.at[k], recv_sem=recv_sems.at[k],
                                         device_id=to, device_id_type=MESH_ID).start()
        token[...] = jnp.zeros_like(token)

    outs = pl.pallas_call(
        body, name=name,
        out_shape=(pltpu.SemaphoreType.DMA((n_sems,)), pltpu.SemaphoreType.DMA((n_sems,)),
                   *[pltpu.HBM(a.shape, a.dtype) for a in arrays], jax.ShapeDtypeStruct((8, 128), F32)),
        in_specs=[HBM] * n + [ANY] * len(order),
        out_specs=(SEM, SEM, *[HBM] * n, pl.BlockSpec(memory_space=pltpu.VMEM)),
        input_output_aliases={i: 2 + i for i in range(n)},
        compiler_params=pltpu.CompilerParams(has_side_effects=EFFECT),
    )(*[pltpu.with_memory_space_constraint(a, pltpu.HBM) for a in arrays], *order)
    return outs[0], outs[1], list(outs[2:2 + n]), outs[-1]


def _split_wait(name, arrays, send_sems, recv_sems, after, plan):
    n = len(arrays)
    order = list(after) if isinstance(after, (list, tuple)) else [after]

    def body(*refs):
        ins, s_sems, r_sems = refs[:n], refs[n], refs[n + 1]
        sends, arrivals = plan(ins)
        x, y, c = lax.axis_index("x"), lax.axis_index("y"), lax.axis_index("c")
        for src, dst, k, to in sends:
            pltpu.make_async_remote_copy(src_ref=src, dst_ref=dst, send_sem=s_sems.at[k], recv_sem=r_sems.at[k],
                                         device_id=to, device_id_type=MESH_ID).wait_send()
        for dst, k in arrivals:
            pltpu.make_async_remote_copy(src_ref=dst, dst_ref=dst, send_sem=s_sems.at[k], recv_sem=r_sems.at[k],
                                         device_id=(x, y, c), device_id_type=MESH_ID).wait_recv()

    return pl.pallas_call(
        body, name=name, out_shape=[pltpu.HBM(a.shape, a.dtype) for a in arrays],
        in_specs=[HBM] * n + [SEM, SEM] + [ANY] * len(order), out_specs=[HBM] * n,
        input_output_aliases={i: i for i in range(n)},
        compiler_params=pltpu.CompilerParams(has_side_effects=EFFECT),
    )(*arrays, send_sems, recv_sems, *order)


def _chips():
    x, y, c = lax.axis_index("x"), lax.axis_index("y"), lax.axis_index("c")
    return x, y, c, [(1 - x, y), (x, 1 - y), (1 - x, 1 - y)]


def _plan_gather_chips(refs):
    x, y, c, chips = _chips()
    me = 4 * x + 2 * y + c
    n = len(refs) // 2
    sends, arrivals = [], []
    for i in range(n):
        src, land = refs[i], refs[n + i]
        sends.append((src, land.at[me], 4 * i, (x, y, 1 - c)))
        arrivals.append((land.at[4 * x + 2 * y + 1 - c], 4 * i))
        for j, (px, py) in enumerate(chips):
            sends.append((src, land.at[me], 4 * i + 1 + j, (px, py, c)))
            arrivals.append((land.at[4 * px + 2 * py + c], 4 * i + 1 + j))
    return sends, arrivals


def _plan_gather_pass(refs):
    x, y, c, chips = _chips()
    sends, arrivals = [], []
    for i in range(len(refs)):
        for j, (px, py) in enumerate(chips):
            slot = refs[i].at[4 * px + 2 * py + c]
            sends.append((slot, slot, 4 * i + j, (x, y, 1 - c)))
            arrivals.append((refs[i].at[4 * px + 2 * py + 1 - c], 4 * i + j))
        back = refs[i].at[4 * x + 2 * y + 1 - c]
        sends.append((back, back, 4 * i + 3, (x, y, 1 - c)))
        arrivals.append((refs[i].at[4 * x + 2 * y + c], 4 * i + 3))
    return sends, arrivals


def _plan_scatter_pair(refs):
    x, y, c = lax.axis_index("x"), lax.axis_index("y"), lax.axis_index("c")
    n = len(refs) // 2
    sends, arrivals = [], []
    for i in range(n):
        for q in range(4):
            sends.append((refs[i].at[q, 1 - c], refs[n + i].at[q], 4 * i + q, (x, y, 1 - c)))
            arrivals.append((refs[n + i].at[q], 4 * i + q))
    return sends, arrivals


def _plan_scatter_chips(layer):
    def plan(refs):
        x, y, c, chips = _chips()
        n = len(refs) // 2
        sends, arrivals = [], []
        for i in range(n):
            for j, (px, py) in enumerate(chips):
                sends.append((refs[i].at[2 * px + py], refs[n + i].at[layer, 2 * x + y], 3 * i + j, (px, py, c)))
                arrivals.append((refs[n + i].at[layer, 2 * px + py], 3 * i + j))
        return sends, arrivals

    return plan


def _pair_sum(parts4, from_pair, landing, layer, core, tr, name):
    _, _, rows, cols = parts4.shape

    def body(c_ref, p_ref, s_ref, l_ref, sum_ref, land_ref):
        v = (p_ref[...].astype(F32) + s_ref[...].astype(F32)).astype(BF16)
        sum_ref[...] = v
        land_ref[...] = v

    blk = pl.BlockSpec((None, tr, cols), lambda q, i, c_ref: (q, i, 0))
    return pl.pallas_call(
        body,
        grid_spec=pltpu.PrefetchScalarGridSpec(
            num_scalar_prefetch=1, grid=(4, rows // tr),
            in_specs=[pl.BlockSpec((None, None, tr, cols), lambda q, i, c_ref: (q, c_ref[0], i, 0)), blk, ANY],
            out_specs=[blk, pl.BlockSpec((None, None, tr, cols), lambda q, i, c_ref: (layer, q, i, 0))]),
        out_shape=[jax.ShapeDtypeStruct((4, rows, cols), BF16), jax.ShapeDtypeStruct(landing.shape, BF16)],
        input_output_aliases={3: 1}, compiler_params=_cp(), name=name,
    )(core, parts4, from_pair, landing)


def _travel_layout(t):
    tr = lambda a: jnp.swapaxes(a, 1, 2)
    branch = jnp.concatenate([tr(t["w_attn_o"]), tr(t["w_conv_o"]), tr(t["w_ssm_o"])], axis=2)
    return [tr(t["w_in"]), tr(t["w_ffn_in"]), t["w_ffn_out"], t["w_mix_o"], branch, t["w_ssm_glu"]]


def _native_layout(a):
    tr = lambda x: jnp.swapaxes(x, 1, 2)
    b = a[4]
    return {"w_in": tr(a[0]), "w_ffn_in": tr(a[1]), "w_ffn_out": a[2], "w_mix_o": a[3],
            "w_attn_o": tr(b[:, :, :WIDTH]), "w_conv_o": tr(b[:, :, WIDTH:2 * WIDTH]),
            "w_ssm_o": tr(b[:, :, 2 * WIDTH:]), "w_ssm_glu": a[5]}


def _embed(t):
    eye = jnp.eye(8, dtype=t.dtype)
    t = t.reshape(DEPTH, N_LANE_GROUPS, 8, SSM_GROUP, SSM_STATE)
    return (t[:, :, :, :, None, :] * eye[None, None, :, None, :, None]).reshape(DEPTH, N_LANE_GROUPS, 128, LANES_G)


def _diag_blocks(t):
    t = t.reshape(DEPTH, N_LANE_GROUPS, 8, SSM_GROUP, 8, SSM_STATE)
    return jnp.einsum("lgahap->lgahp", t).reshape(DEPTH, SSM_GROUPS, SSM_GROUP, SSM_STATE)


def _rope_tabs():
    pos = jnp.arange(SEQ, dtype=F32)
    inv_freq = ROPE_THETA ** (-jnp.arange(0, ROT_DIM, 2, dtype=F32) / ROT_DIM)
    ang = pos[:, None] * inv_freq[None, :]
    cos, sin = jnp.cos(ang), jnp.sin(ang)
    one, zero = jnp.ones((SEQ, HEAD_DIM - ROT_DIM), F32), jnp.zeros((SEQ, HEAD_DIM - ROT_DIM), F32)
    z8 = jnp.zeros((SEQ, 8), F32)
    head = lambda *p: jnp.tile(jnp.concatenate(p, axis=1), (1, 2))
    return head(cos, cos, one), head(-sin, z8, zero), head(z8, sin, zero)


def _ssm_mats(sp):
    lr, li, bbr, bbi = _ssm_prep(sp["a_re"], sp["a_im"], sp["log_dt"], sp["bt_re"], sp["bt_im"])
    lanes = SSM_GROUPS * SSM_STATE
    return {
        "a_re": lr.reshape(DEPTH, 1, lanes), "a_im": li.reshape(DEPTH, 1, lanes),
        "b_re": _embed(bbr).astype(BF16), "b_im": _embed(bbi).astype(BF16),
        "c_re": _embed(sp["c_re"]).astype(BF16), "c_im_neg": _embed(-sp["c_im"]).astype(BF16),
    }


def _layer_fwd(x, i, w, rp, mats, tabs, tie, hooks):
    q, kv, cbx, u, glog, h = _rms_mm_in(x, rp["norm_mix"][i], w["win_t"], tie)
    o = _attn_fwd(q, kv, tabs, rp["attn_sinks"][i])
    cv = _conv_fwd(cbx, rp["conv_w"], i)
    x_re, x_im, y = _ssm_fwd(u, mats, i, rp["ssm_d"])
    z = _glu_fwd(y, w["wglu"])
    x1 = _mix_fwd(x, o, cv, z, glog, rp["b_gate"], i, w["branch_t"], w["wmix"], hooks["early"](z))
    hooks["pre_ffn"](x1)
    act, up, silu, dsilu, h2 = _rms_mm_ffn(x1, rp["norm_ffn"][i], w["wffn_t"])
    x2 = _ffn_out_fwd(x1, act, w["wout"], hooks["mid"](h2))
    kept = dict(x=x, q=q, kv=kv, cbx=cbx, u=u, glog=glog, h=h, o=o, cv=cv, z=z, y=y,
                x_re=x_re, x_im=x_im, x1=x1, act=act, up=up, silu=silu, dsilu=dsilu, h2=h2)
    return x2, kept


def _layer_bwd(dx2, k, i, w, rp, mats, tabs, tie, hooks):
    dgu = _ffn_out_bwd(dx2, k["up"], k["silu"], k["dsilu"], w["wout"], tie)
    g_wout = _mm_tn(k["act"], dx2, tm=FFN_H // 2, tn=1024, name="mm_tn_ffn_out")
    g_wffn_t = _mm_tn(dgu, k["h2"], tm=FFN_H // 2, tn=1024, name="mm_tn_ffn_in")
    dx1, d_norm_ffn = _mm_rmsbwd([dgu], w["wffn_t"], k["x1"], rp["norm_ffn"][i], dx2, "mm_rmsbwd_ffn")

    mg, dya, dyc, dys, do, dcv, dz, dgl, db_gate = _mix_bwd(
        dx1, k["o"], k["cv"], k["z"], k["glog"], rp["b_gate"], i, w["branch_t"], w["wmix"],
        hooks["mid"]((g_wffn_t, g_wout, d_norm_ffn)))
    g_wmix = _mm_tn(mg, dx1, tm=1024, tn=512, name="mm_tn_mix")
    g_branch_t = _tn_branches((dya, dyc, dys), (k["o"], k["cv"], k["z"]))

    dy, ys16, da16, dd = _glu_bwd(k["y"], w["wglu"], dz, k["u"])
    g_wglu = _mm_tn(ys16, da16, tm=256, tn=512, name="mm_tn_glu")
    du, da_re, da_im, db_re, db_im, dc_re, dc_im = _ssm_bwd(dy, k["x_re"], k["x_im"], k["u"], mats, i, rp["ssm_d"])

    dcb, dcc, dcx, d_conv_w = _conv_bwd(k["cbx"], rp["conv_w"], i, dcv, hooks["late"](du))
    dq, dkv, d_sinks = _attn_bwd(k["q"], k["kv"], tabs, rp["attn_sinks"][i], do)

    pieces = [dq, dkv, dcb, dcc, dcx, du, dgl]
    g_win_t = _tn_pieces(pieces, k["h"])
    dx, d_norm_mix = _mm_rmsbwd(pieces, w["win_t"], k["x"], rp["norm_mix"][i], dx1, "mm_rmsbwd_in")

    grads = [g_win_t, g_wffn_t, g_wout, g_wmix, g_branch_t, g_wglu]
    small = dict(norm_mix=d_norm_mix, b_gate=db_gate, attn_sinks=d_sinks, ssm_d=dd, norm_ffn=d_norm_ffn,
                 conv_w=d_conv_w, da_re=da_re, da_im=da_im, db_re=db_re, db_im=db_im, dc_re=dc_re, dc_im=dc_im)
    return dx, grads, small


def _replicated_grads(sg, sp):
    stack = lambda name: jnp.stack([sg[i][name] for i in range(DEPTH)])
    cots = (stack("da_re").reshape(DEPTH, *_GS), stack("da_im").reshape(DEPTH, *_GS),
            _diag_blocks(stack("db_re")), _diag_blocks(stack("db_im")))
    d_a_re, d_a_im, d_log_dt, d_bt_re, d_bt_im = _ssm_prep_bwd(
        sp["a_re"], sp["a_im"], sp["log_dt"], sp["bt_re"], sp["bt_im"], cots)
    sgrads = {"norm_mix": stack("norm_mix"), "b_gate": stack("b_gate"),
              "attn_sinks": stack("attn_sinks")[:, :, :N_Q_HEADS], "ssm_a_re": d_a_re, "ssm_a_im": d_a_im,
              "ssm_b_re": jnp.swapaxes(d_bt_re, 2, 3), "ssm_b_im": jnp.swapaxes(d_bt_im, 2, 3),
              "ssm_c_re": _diag_blocks(stack("dc_re")), "ssm_c_im": -_diag_blocks(stack("dc_im")),
              "ssm_d": stack("ssm_d"), "ssm_log_dt": d_log_dt, "norm_ffn": stack("norm_ffn")}
    return sgrads, stack("conv_w")[:, :3]


def kernel(x, norm_mix, w_in, b_gate, attn_sinks, w_attn_o, conv_w, w_conv_o, ssm_a_re, ssm_a_im, ssm_b_re, ssm_b_im, ssm_c_re, ssm_c_im, ssm_d, ssm_log_dt, w_ssm_glu, w_ssm_o, w_mix_o, norm_ffn, w_ffn_in, w_ffn_out, norm_final, loss_target, m_norm_mix, m_w_in, m_b_gate, m_attn_sinks, m_w_attn_o, m_conv_w, m_w_conv_o, m_ssm_a_re, m_ssm_a_im, m_ssm_b_re, m_ssm_b_im, m_ssm_c_re, m_ssm_c_im, m_ssm_d, m_ssm_log_dt, m_w_ssm_glu, m_w_ssm_o, m_w_mix_o, m_norm_ffn, m_w_ffn_in, m_w_ffn_out, m_norm_final, v_norm_mix, v_w_in, v_b_gate, v_attn_sinks, v_w_attn_o, v_conv_w, v_w_conv_o, v_ssm_a_re, v_ssm_a_im, v_ssm_b_re, v_ssm_b_im, v_ssm_c_re, v_ssm_c_im, v_ssm_d, v_ssm_log_dt, v_w_ssm_glu, v_w_ssm_o, v_w_mix_o, v_norm_ffn, v_w_ffn_in, v_w_ffn_out, v_norm_final):
    big = {"w": dict(w_in=w_in, w_attn_o=w_attn_o, w_conv_o=w_conv_o, w_ssm_glu=w_ssm_glu, w_ssm_o=w_ssm_o,
                     w_mix_o=w_mix_o, w_ffn_in=w_ffn_in, w_ffn_out=w_ffn_out),
           "m": dict(w_in=m_w_in, w_attn_o=m_w_attn_o, w_conv_o=m_w_conv_o, w_ssm_glu=m_w_ssm_glu,
                     w_ssm_o=m_w_ssm_o, w_mix_o=m_w_mix_o, w_ffn_in=m_w_ffn_in, w_ffn_out=m_w_ffn_out),
           "v": dict(w_in=v_w_in, w_attn_o=v_w_attn_o, w_conv_o=v_w_conv_o, w_ssm_glu=v_w_ssm_glu,
                     w_ssm_o=v_w_ssm_o, w_mix_o=v_w_mix_o, w_ffn_in=v_w_ffn_in, w_ffn_out=v_w_ffn_out)}
    small = {"w": dict(norm_mix=norm_mix, b_gate=b_gate, attn_sinks=attn_sinks, ssm_a_re=ssm_a_re,
                       ssm_a_im=ssm_a_im, ssm_b_re=ssm_b_re, ssm_b_im=ssm_b_im, ssm_c_re=ssm_c_re,
                       ssm_c_im=ssm_c_im, ssm_d=ssm_d, ssm_log_dt=ssm_log_dt, norm_ffn=norm_ffn),
             "m": dict(norm_mix=m_norm_mix, b_gate=m_b_gate, attn_sinks=m_attn_sinks, ssm_a_re=m_ssm_a_re,
                       ssm_a_im=m_ssm_a_im, ssm_b_re=m_ssm_b_re, ssm_b_im=m_ssm_b_im, ssm_c_re=m_ssm_c_re,
                       ssm_c_im=m_ssm_c_im, ssm_d=m_ssm_d, ssm_log_dt=m_ssm_log_dt, norm_ffn=m_norm_ffn),
             "v": dict(norm_mix=v_norm_mix, b_gate=v_b_gate, attn_sinks=v_attn_sinks, ssm_a_re=v_ssm_a_re,
                       ssm_a_im=v_ssm_a_im, ssm_b_re=v_ssm_b_re, ssm_b_im=v_ssm_b_im, ssm_c_re=v_ssm_c_re,
                       ssm_c_im=v_ssm_c_im, ssm_d=v_ssm_d, ssm_log_dt=v_ssm_log_dt, norm_ffn=v_norm_ffn)}
    finals = {"w": norm_final, "m": m_norm_final, "v": v_norm_final}
    convs = {"w": conv_w, "m": m_conv_w, "v": v_conv_w}
    small_out_shapes = {name: a.shape for name, a in small["w"].items()}
    small_out_shapes.update(norm_final=(D_MODEL,), conv_w=(DEPTH, 3, 64))
    small_shapes = dict(small_out_shapes, norm_final=(1, D_MODEL), conv_w=(DEPTH, 3, WIDTH))
    dense = ("ssm_b_re", "ssm_b_im", "ssm_c_re", "ssm_c_im")
    for name in dense:
        small_shapes[name] = (DEPTH, SSM_GROUPS, SSM_GROUP * SSM_STATE)
    small_wmv = {name: [(convs[s] if name == "conv_w" else finals[s] if name == "norm_final" else small[s][name])
                        .reshape((DEPTH, 3, 64) if name == "conv_w" else small_shapes[name]) for s in "wmv"]
                 for name in small_shapes}
    mine = 4 * lax.axis_index("x") + 2 * lax.axis_index("y") + lax.axis_index("c")

    travel = {s: _travel_layout(big[s]) for s in "wmv"}
    stacked16 = list(zip(*[[a[0] for a in _travel_layout({n: w[i:i + 1].astype(BF16) for n, w in big["w"].items()})]
                           for i in range(DEPTH)]))
    rp = {"norm_mix": norm_mix[:, None], "norm_ffn": norm_ffn[:, None], "attn_sinks": attn_sinks[:, None],
          "b_gate": b_gate[:, None], "ssm_d": ssm_d[:, None]}
    sp = {"a_re": ssm_a_re, "a_im": ssm_a_im, "log_dt": ssm_log_dt[:, :, None],
          "bt_re": jnp.swapaxes(ssm_b_re, 2, 3), "bt_im": jnp.swapaxes(ssm_b_im, 2, 3),
          "c_re": ssm_c_re, "c_im": ssm_c_im}
    rows_tile = {"win_t": 368, "wffn_t": 352, "wout": 176, "wmix": 128, "branch_t": 128, "wglu": 64}
    core = lax.axis_index("c").astype(jnp.int32).reshape(1)
    no_tie = jnp.zeros((8, 128), F32)

    def place_own(srcs):
        return [lax.empty((N_DEV,) + s.shape, s.dtype) for s in srcs]

    def gather_chips(tag, i, kinds, after, extra=()):
        srcs = [stacked16[j][i] for j in kinds] + list(extra)
        s_sems, r_sems, arrays, token = _split_start(
            f"gather_chips_start_{tag}", srcs + place_own(srcs), 4 * len(srcs), _plan_gather_chips, after)
        return (tag, s_sems, r_sems, arrays), token

    def gather_pass(state, after):
        tag, s_sems, r_sems, arrays = state
        arrays = _split_wait(f"gather_chips_wait_{tag}", arrays, s_sems, r_sems, after, _plan_gather_chips)
        n = len(arrays) // 2
        s_sems, r_sems, lands, token = _split_start(
            f"gather_pass_start_{tag}", list(arrays[n:]), 4 * n, _plan_gather_pass)
        return (tag, s_sems, r_sems, lands), token

    def gather_done(state, after, kinds):
        tag, s_sems, r_sems, lands = state
        lands = _split_wait(f"gather_pass_wait_{tag}", lands, s_sems, r_sems, after, _plan_gather_pass)
        named = {KINDS[j][0]: a.reshape(N_DEV * KINDS[j][1], KINDS[j][2]) for a, j in zip(lands, kinds)}
        return named, list(lands[len(kinds):])

    all_kinds, mixer_kinds, ffn_kinds = tuple(range(len(KINDS))), (0, 3, 4, 5), (1, 2)
    no_hooks = {name: (lambda value: no_tie) for name in ("early", "pre_ffn", "mid", "late")}
    state, token = gather_chips("0m", 0, mixer_kinds, None, extra=[jnp.pad(conv_w.reshape(6, 128), ((0, 2), (0, 0)))])
    mats = _ssm_mats(dict(sp, log_dt=sp["log_dt"] + token[0, 0]))
    tabs = _rope_tabs()
    early_work = list(mats.values()) + list(tabs) + [a for name in dense for a in small_wmv[name]]
    state, _ = gather_pass(state, early_work)
    ffn_state, tie = gather_chips("0f", 0, ffn_kinds, state[3][0])
    w_next, (conv_all,) = gather_done(state, tabs[2], mixer_kinds)
    conv_full = conv_all[:, :6].reshape(N_DEV, DEPTH, 3, 64).transpose(1, 2, 0, 3).reshape(DEPTH, 3, WIDTH)
    rp["conv_w"] = jnp.pad(conv_full, ((0, 0), (0, 5), (0, 0)))

    act = x[0]
    weights, kept = [], []
    for i in range(DEPTH):
        w_i, hooks, held = w_next, dict(no_hooks), {}

        def early(value, ffn_state=ffn_state, held=held):
            held["ffn"], token = gather_pass(ffn_state, value)
            return token

        def pre_ffn(value, w_i=w_i, held=held):
            w_i.update(gather_done(held["ffn"], value, ffn_kinds)[0])

        hooks.update(early=early, pre_ffn=pre_ffn)
        if i + 1 < DEPTH:
            state, tie = gather_chips(f"{i + 1}m", i + 1, mixer_kinds, tie if i == 0 else w_i["win_t"])

            def mid(value, i=i, state=state, held=held):
                held["next"], token = gather_pass(state, value)
                held["next_ffn"], token = gather_chips(f"{i + 1}f", i + 1, ffn_kinds, token)
                return token

            hooks.update(mid=mid)
        act, k = _layer_fwd(act, i, w_i, rp, mats, tabs, tie, hooks)
        if i + 1 < DEPTH:
            w_next, _ = gather_done(held["next"], act, mixer_kinds)
            ffn_state, tie = held["next_ffn"], no_tie
        weights.append(w_i)
        kept.append(k)
    loss_row, dx, d_norm_final = _loss_head(act, norm_final[None], loss_target[0])

    landings = [lax.empty((DEPTH, 4, r, c), BF16) for _, r, c in KINDS]
    landings0 = [lax.empty((1, 4, r, c), BF16) for _, r, c in KINDS]

    def scatter_pair(tag, kinds, grads, after):
        parts4 = [g.reshape(4, 2, KINDS[j][1], KINDS[j][2]) for g, j in zip(grads, kinds)]
        zones = [lax.empty((4, KINDS[j][1], KINDS[j][2]), BF16) for j in kinds]
        s_sems, r_sems, arrays, token = _split_start(
            f"scatter_pair_start_{tag}", parts4 + zones, 4 * len(kinds), _plan_scatter_pair, after)
        return (tag, kinds, s_sems, r_sems, arrays), token

    def scatter_chips(state, lands, slot, after):
        tag, kinds, s_sems, r_sems, arrays = state
        arrays = _split_wait(f"scatter_pair_wait_{tag}", arrays, s_sems, r_sems, after, _plan_scatter_pair)
        n = len(kinds)
        sums, mine_lands = [], []
        for k, j in enumerate(kinds):
            name = KINDS[j][0]
            chip_sum, land = _pair_sum(arrays[k], arrays[n + k], lands[j], slot, core, rows_tile[name],
                                       f"pair_sum_{name}")
            sums.append(chip_sum)
            mine_lands.append(land)
        s_sems, r_sems, arrays, token = _split_start(
            f"scatter_chips_start_{tag}", sums + mine_lands, 3 * n, _plan_scatter_chips(slot))
        return (tag, kinds, slot, s_sems, r_sems, arrays), token

    def scatter_done(state, lands, after):
        tag, kinds, slot, s_sems, r_sems, arrays = state
        arrays = _split_wait(f"scatter_chips_wait_{tag}", arrays, s_sems, r_sems, after, _plan_scatter_chips(slot))
        lands = list(lands)
        for k, j in enumerate(kinds):
            lands[j] = arrays[len(kinds) + k]
        return lands

    sg = [None] * DEPTH
    pending, tie = None, no_tie
    for i in reversed(range(DEPTH)):
        hooks, held = dict(no_hooks), {}
        if pending is not None:
            def mid(value, i=i, pending=pending, held=held):
                held["chips"], token = scatter_chips(pending, landings, i + 1, value[2])
                if i == 0:
                    held["ffn_pair"], token = scatter_pair("0f", ffn_kinds, value[:2], token)
                return token

            hooks.update(mid=mid)
        if i == 0:
            def late(value, held=held):
                held["ffn_chips"], token = scatter_chips(held["ffn_pair"], landings0, 0, value)
                return token

            hooks.update(late=late)
        dx, grads, sg[i] = _layer_bwd(dx, kept[i], i, weights[i], rp, mats, tabs, tie, hooks)
        if pending is not None:
            landings = scatter_done(held["chips"], landings, dx)
        if i > 0:
            pending, tie = scatter_pair(str(i), all_kinds, grads, dx)
        else:
            pending, _ = scatter_pair("0m", mixer_kinds, [grads[j] for j in mixer_kinds], dx)

    sgrads, conv_grad = _replicated_grads(sg, sp)

    small_names = [name for name, _ in SMALL] + ["norm_final", "conv_w"]
    sgrads.update(norm_final=d_norm_final, conv_w=conv_grad)
    small_src = [sgrads[name].reshape(small_shapes[name]).astype(BF16) for name in small_names]
    small_src.append(jnp.broadcast_to(loss_row[:, :1], (8, 128)))
    last, tie = scatter_chips(pending, landings0, 0, small_src[0])
    s_sems, r_sems, arrays, tie = _split_start(
        "gather_small_chips_start", small_src + place_own(small_src), 4 * len(small_src), _plan_gather_chips, tie)
    small_state = ("small", s_sems, r_sems, arrays)

    big_out = []
    for j, (name, _, _) in enumerate(KINDS):
        big_out.append(_adamw(landings[j], travel["w"][j], travel["m"][j], travel["v"][j], rows_tile[name],
                              "adamw_late_" + name, groups=(1, DEPTH), tie=tie))
        tie = big_out[-1][3]
    landings0 = scatter_done(held["ffn_chips"], landings0, tie)
    landings0 = scatter_done(last, landings0, tie)
    small_state, _ = gather_pass(small_state, landings0[0])
    big_out = [_adamw(landings0[j], travel["w"][j], travel["m"][j], travel["v"][j], rows_tile[name],
                      "adamw_first_" + name, groups=(0, 1), fill=big_out[j]) for j, (name, _, _) in enumerate(KINDS)]
    big_res = [_native_layout([big_out[j][kind] for j in range(len(KINDS))]) for kind in range(4)]

    _, sparts = gather_done(small_state, big_out[-1][0], ())
    loss = jnp.sum(sparts[-1][:, 0, 0])
    sparts = dict(zip(small_names, sparts))
    sparts["conv_w"] = lax.dynamic_slice_in_dim(sparts["conv_w"], mine * 64, 64, axis=3)
    small_res = {}
    for name in small_names:
        res = _adamw_small(sparts[name], *small_wmv[name], "adamw_" + name)
        small_res[name] = [r.reshape(small_out_shapes[name]) for r in res]

    order = ["norm_mix", "w_in", "b_gate", "attn_sinks", "w_attn_o", "conv_w", "w_conv_o", "ssm_a_re", "ssm_a_im",
             "ssm_b_re", "ssm_b_im", "ssm_c_re", "ssm_c_im", "ssm_d", "ssm_log_dt", "w_ssm_glu", "w_ssm_o",
             "w_mix_o", "norm_ffn", "w_ffn_in", "w_ffn_out", "norm_final"]
    outs = [loss, dx[None]]
    for kind in range(4):
        for name in order:
            outs.append(big_res[kind][name] if name in big_res[kind] else small_res[name][kind])
    return tuple(outs)
```

```python
import math

import jax
import jax.numpy as jnp
from jax import lax
from jax.experimental import pallas as pl
from jax.experimental.pallas import tpu as pltpu

F32 = jnp.float32
BF16 = jnp.bfloat16

N_DEV = 8
DEPTH = 4
SEQ = 2048
D_MODEL = 1024
N_Q_HEADS = 8
HEAD_DIM = 64
ATTN_W = 512
KV_W = 128
BLOCK = 128
N_BLOCKS = SEQ // BLOCK
ROPE_THETA = 500000.0
ROT_DIM = 16
NEG_INF = -1e30
WIDTH = 512
SSM_GROUPS = 32
SSM_GROUP = 16
SSM_STATE = 64
CHUNK = 256
N_CHUNKS = SEQ // CHUNK
GATE_W = 3 * D_MODEL
IN_COLS = 5888
FFN_H = 2816
NORM_EPS = 1e-6
LR, B1, B2, ADAM_EPS, WD, STEP = 0.001, 0.9, 0.999, 1e-08, 0.01, 10

COL_Q, COL_KV, COL_CBX, COL_U, COL_G = 0, 512, 768, 2304, 2816
PIECE_W = (512, 256, 512, 512, 512, 512, 3072)
PIECE_OFF = tuple(sum(PIECE_W[:i]) for i in range(len(PIECE_W)))

KINDS = (("win_t", 736, 1024), ("wffn_t", 704, 1024), ("wout", 352, 1024), ("wmix", 128, 1024),
         ("branch_t", 128, 1536), ("wglu", 64, 512))

REPLICATED = ("norm_mix", "b_gate", "attn_sinks", "ssm_a_re", "ssm_a_im", "ssm_b_re", "ssm_b_im", "ssm_c_re",
              "ssm_c_im", "ssm_d", "ssm_log_dt", "norm_ffn")

VMEM_LIMIT = 56 * 1024 * 1024
NT = (((1,), (1,)), ((), ()))
TN = (((0,), (0,)), ((), ()))
MESH_ID = pl.DeviceIdType.MESH
ANY = pl.BlockSpec(memory_space=pl.ANY)
HBM = pl.BlockSpec(memory_space=pltpu.HBM)
SEM = pl.BlockSpec(memory_space=pltpu.SEMAPHORE)
EFFECT = pltpu.SideEffectType.DATAFLOW_SIDE_EFFECTING


def _cp(**kw):
    return pltpu.CompilerParams(vmem_limit_bytes=VMEM_LIMIT, **kw)


def _full(shape):
    return pl.BlockSpec(shape, lambda *_: (0,) * len(shape))


def _resident(shape):
    return pl.BlockSpec(shape, lambda *_: (0,) * len(shape), pipeline_mode=pl.Buffered(1))


def _mm_tn(a, b, *, tm, tn, name):
    k, m = a.shape
    n = b.shape[1]

    def body(a_ref, b_ref, o_ref):
        o_ref[...] = lax.dot_general(a_ref[...].astype(BF16), b_ref[...].astype(BF16), TN,
                                     preferred_element_type=F32).astype(BF16)

    return pl.pallas_call(
        body, grid=(m // tm, n // tn),
        in_specs=[pl.BlockSpec((k, tm), lambda i, j: (0, i)), pl.BlockSpec((k, tn), lambda i, j: (0, j))],
        out_specs=pl.BlockSpec((tm, tn), lambda i, j: (i, j)),
        out_shape=jax.ShapeDtypeStruct((m, n), BF16), compiler_params=_cp(), name=name)(a, b)


def _rms_rows(xv, g):
    r = lax.rsqrt(jnp.mean(xv * xv, axis=-1, keepdims=True) + NORM_EPS)
    return ((xv * r) * g).astype(BF16)


def _rms_mm_in(x, g, wt, tabs, tie):
    tt = 512
    widths = (3 * WIDTH, WIDTH, GATE_W)
    offs = (COL_CBX, COL_U, COL_G)

    def body(x_ref, g_ref, w_ref, tc_ref, ta_ref, tb_ref, tie_ref, q_ref, kv_ref, cbx_ref, u_ref, gl_ref, h_ref):
        h = _rms_rows(x_ref[...], g_ref[...])
        h_ref[...] = h
        prod = lax.dot_general(h, w_ref[...], NT, preferred_element_type=F32)
        for ref, o, w in zip((cbx_ref, u_ref, gl_ref), offs, widths):
            ref[...] = prod[:, o:o + w]
        c, a, b = tc_ref[...], ta_ref[...], tb_ref[...]
        for j in range(ATTN_W // 128):
            q_ref[:, 128 * j:128 * (j + 1)] = _rope(prod[:, 128 * j:128 * (j + 1)], c, a, b) * (HEAD_DIM ** -0.5)
        kv_ref[:, :KV_W] = _rope(prod[:, COL_KV:COL_KV + KV_W], c, a, b)
        kv_ref[:, KV_W:] = prod[:, COL_KV + KV_W:COL_CBX]

    row = lambda w: pl.BlockSpec((tt, w), lambda i: (i, 0))
    sds = jax.ShapeDtypeStruct
    return pl.pallas_call(
        body, grid=(SEQ // tt,),
        in_specs=[row(D_MODEL), _full((1, D_MODEL)), _resident((IN_COLS, D_MODEL)), row(128), row(128), row(128), ANY],
        out_specs=[row(ATTN_W), row(2 * KV_W), row(3 * WIDTH), row(WIDTH), row(GATE_W), row(D_MODEL)],
        out_shape=[sds((SEQ, ATTN_W), F32), sds((SEQ, 2 * KV_W), F32), sds((SEQ, 3 * WIDTH), F32),
                   sds((SEQ, WIDTH), F32), sds((SEQ, GATE_W), F32), sds((SEQ, D_MODEL), BF16)],
        compiler_params=_cp(), name="rms_mm_in")(x, g, wt, *tabs, tie)


def _rms_mm_ffn(x, g, wt):
    tt = 256

    def body(x_ref, g_ref, w_ref, act_ref, up_ref, silu_ref, dsilu_ref, h_ref):
        h = _rms_rows(x_ref[...], g_ref[...])
        h_ref[...] = h
        prod = lax.dot_general(h, w_ref[...], NT, preferred_element_type=F32)
        gt, up = prod[:, :FFN_H], prod[:, FFN_H:]
        sg = jax.nn.sigmoid(gt)
        silu = gt * sg
        act_ref[...] = (silu * up).astype(BF16)
        up_ref[...] = up.astype(BF16)
        silu_ref[...] = silu.astype(BF16)
        dsilu_ref[...] = (sg + silu * (1.0 - sg)).astype(BF16)

    row = lambda w: pl.BlockSpec((tt, w), lambda i: (i, 0))
    return pl.pallas_call(
        body, grid=(SEQ // tt,), in_specs=[row(D_MODEL), _full((1, D_MODEL)), _resident((2 * FFN_H, D_MODEL))],
        out_specs=[row(FFN_H)] * 4 + [row(D_MODEL)],
        out_shape=[jax.ShapeDtypeStruct((SEQ, FFN_H), BF16)] * 4 + [jax.ShapeDtypeStruct((SEQ, D_MODEL), BF16)],
        compiler_params=_cp(), name="rms_mm_ffn")(x, g, wt)


def _mm_rmsbwd(pieces, wt, x, g, dres, name):
    tt = 512
    widths = [p.shape[1] for p in pieces]
    offs = [sum(widths[:i]) for i in range(len(widths))]
    n = len(pieces)

    def body(*refs):
        p_refs, (w_ref, x_ref, g_ref, r_ref, dx_ref, dg_ref) = refs[:n], refs[n:]

        @pl.when(pl.program_id(0) == 0)
        def _():
            dg_ref[...] = jnp.zeros_like(dg_ref)

        dh = jnp.zeros((tt, D_MODEL), F32)
        for p_ref, o, w in zip(p_refs, offs, widths):
            dh += jnp.dot(p_ref[...], w_ref[o:o + w, :], preferred_element_type=F32)
        xv = x_ref[...]
        r = lax.rsqrt(jnp.mean(xv * xv, axis=-1, keepdims=True) + NORM_EPS)
        xh = xv * r
        gy = dh * g_ref[...]
        dx_ref[...] = r_ref[...] + r * (gy - xh * jnp.mean(gy * xh, axis=-1, keepdims=True))
        dg_ref[...] += jnp.sum(dh * xh, axis=0, keepdims=True)

    row = lambda w: pl.BlockSpec((tt, w), lambda i: (i, 0))
    return pl.pallas_call(
        body, grid=(SEQ // tt,),
        in_specs=[row(w) for w in widths] + [_resident(wt.shape), row(D_MODEL), _full((1, D_MODEL)), row(D_MODEL)],
        out_specs=[row(D_MODEL), _full((1, D_MODEL))],
        out_shape=[jax.ShapeDtypeStruct((SEQ, D_MODEL), F32), jax.ShapeDtypeStruct((1, D_MODEL), F32)],
        compiler_params=_cp(), name=name)(*pieces, wt, x, g, dres)


def _tn_pieces(pieces, h):
    tk, tn = 512, 512
    nk = SEQ // tk
    n = len(pieces)

    def body(*refs):
        p_refs, (h_ref, o_ref, acc_ref) = refs[:n], refs[n:]
        kk = pl.program_id(1)

        @pl.when(kk == 0)
        def _():
            acc_ref[...] = jnp.zeros_like(acc_ref)

        hv = h_ref[...]
        for p_ref, o, w in zip(p_refs, PIECE_OFF, PIECE_W):
            acc_ref[o:o + w, :] += lax.dot_general(p_ref[...], hv, TN, preferred_element_type=F32)

        @pl.when(kk == nk - 1)
        def _():
            o_ref[...] = acc_ref[...].astype(BF16)

    return pl.pallas_call(
        body, grid=(D_MODEL // tn, nk),
        in_specs=[pl.BlockSpec((tk, w), lambda j, kk: (kk, 0)) for w in PIECE_W]
        + [pl.BlockSpec((tk, tn), lambda j, kk: (kk, j))],
        out_specs=pl.BlockSpec((IN_COLS, tn), lambda j, kk: (0, j)),
        out_shape=jax.ShapeDtypeStruct((IN_COLS, D_MODEL), BF16),
        scratch_shapes=[pltpu.VMEM((IN_COLS, tn), F32)], compiler_params=_cp(), name="tn_pieces")(*pieces, h)


def _tn_branches(dys, acts):
    tk = 512
    nk = SEQ // tk

    def body(d0, d1, d2, a0, a1, a2, o_ref, acc_ref):
        kk = pl.program_id(0)

        @pl.when(kk == 0)
        def _():
            acc_ref[...] = jnp.zeros_like(acc_ref)

        for j, (d, a) in enumerate(((d0, a0), (d1, a1), (d2, a2))):
            acc_ref[:, WIDTH * j:WIDTH * (j + 1)] += lax.dot_general(d[...], a[...], TN, preferred_element_type=F32)

        @pl.when(kk == nk - 1)
        def _():
            o_ref[...] = acc_ref[...].astype(BF16)

    row = lambda w: pl.BlockSpec((tk, w), lambda kk: (kk, 0))
    return pl.pallas_call(
        body, grid=(nk,), in_specs=[row(D_MODEL)] * 3 + [row(WIDTH)] * 3,
        out_specs=_full((D_MODEL, 3 * WIDTH)), out_shape=jax.ShapeDtypeStruct((D_MODEL, 3 * WIDTH), BF16),
        scratch_shapes=[pltpu.VMEM((D_MODEL, 3 * WIDTH), F32)], compiler_params=_cp(), name="tn_branches",
    )(*dys, *acts)


def _rope(t, c, a, b):
    return t * c + pltpu.roll(t, 120, axis=1) * a + pltpu.roll(t, 8, axis=1) * b


def _rope_t(d, c, a, b):
    return d * c + pltpu.roll(d * a, 8, axis=1) + pltpu.roll(d * b, 120, axis=1)


def _band_sides(band):
    left = lax.broadcasted_iota(jnp.int32, band.shape, 1) < HEAD_DIM
    h0 = jnp.where(left, band, 0.0)
    h1 = jnp.where(left, 0.0, band)
    r0 = pltpu.roll(h0, HEAD_DIM, axis=1)
    r1 = pltpu.roll(h1, HEAD_DIM, axis=1)
    return ((h0.astype(BF16), r0.astype(BF16)), (r1.astype(BF16), h1.astype(BF16)))


def _attn_mask(i):
    qi = lax.broadcasted_iota(jnp.int32, (2 * BLOCK, 2 * BLOCK), 0) % BLOCK
    kj = lax.broadcasted_iota(jnp.int32, (2 * BLOCK, 2 * BLOCK), 1)
    delta = qi + BLOCK - kj
    return (delta >= 0) & (delta < BLOCK) & ((kj >= BLOCK) | (i > 0))


def _attn_probs(s, ok, sink):
    s = jnp.where(ok, s, NEG_INF)
    m = jnp.maximum(jnp.max(s, axis=-1, keepdims=True), sink)
    p = jnp.exp(s - m)
    es = jnp.exp(sink - m)
    inv = 1.0 / (jnp.sum(p, axis=-1, keepdims=True) + es)
    return p * inv, es * inv


def _kv_group(qs, ks, vs, kh, sink_ref):
    q2 = jnp.concatenate([qs[2 * kh], qs[2 * kh + 1]], axis=0)
    kst = jnp.concatenate([ks[kh][0], ks[kh][1]], axis=0)
    vst = jnp.concatenate([vs[kh][0], vs[kh][1]], axis=0)
    top = lax.broadcasted_iota(jnp.int32, (2 * BLOCK, 1), 0) < BLOCK
    sinks = [jnp.where(top, sink_ref[0, 4 * kh + h], sink_ref[0, 4 * kh + 2 + h]) for h in range(2)]
    return q2, kst, vst, sinks


def _attn_load(q_ref, kvc_ref, kvp_ref, tc_ref, ta_ref, tb_ref, pc_ref, pa_ref, pb_ref):
    c, a, b = tc_ref[...], ta_ref[...], tb_ref[...]
    kband = jnp.concatenate([kvp_ref[:, :KV_W], kvc_ref[:, :KV_W]], axis=0)
    vband = jnp.concatenate([kvp_ref[:, KV_W:], kvc_ref[:, KV_W:]], axis=0)
    qs = [q_ref[:, 128 * j:128 * (j + 1)].astype(BF16) for j in range(4)]
    return qs, _band_sides(kband), _band_sides(vband), (c, a, b)


def _attn_specs(clamp):
    cur = lambda i: (clamp(i), 0)
    prev = lambda i: (jnp.maximum(clamp(i) - 1, 0), 0)
    return [
        pl.BlockSpec((BLOCK, ATTN_W), cur), pl.BlockSpec((BLOCK, 2 * KV_W), cur),
        pl.BlockSpec((BLOCK, 2 * KV_W), prev),
        pl.BlockSpec((BLOCK, 128), cur), pl.BlockSpec((BLOCK, 128), cur), pl.BlockSpec((BLOCK, 128), cur),
        pl.BlockSpec((BLOCK, 128), prev), pl.BlockSpec((BLOCK, 128), prev), pl.BlockSpec((BLOCK, 128), prev),
        pl.BlockSpec(memory_space=pltpu.SMEM),
    ]


def _attn_fwd(q, kv, tabs, sinks):
    tc, ta, tb = tabs

    def body(q_ref, kvc_ref, kvp_ref, tc_ref, ta_ref, tb_ref, pc_ref, pa_ref, pb_ref, sink_ref, o_ref):
        i = pl.program_id(0)
        qs, ks, vs, _ = _attn_load(q_ref, kvc_ref, kvp_ref, tc_ref, ta_ref, tb_ref, pc_ref, pa_ref, pb_ref)
        ok = _attn_mask(i)
        for kh in range(2):
            q2, kst, vst, sinks = _kv_group(qs, ks, vs, kh, sink_ref)
            s = lax.dot_general(q2, kst, NT, preferred_element_type=F32)
            pn = [_attn_probs(s[:, 2 * BLOCK * h:2 * BLOCK * (h + 1)], ok, sinks[h])[0].astype(BF16) for h in range(2)]
            o2 = jnp.dot(jnp.concatenate(pn, axis=1), vst, preferred_element_type=F32).astype(BF16)
            for r in range(2):
                j = 2 * kh + r
                o_ref[:, 128 * j:128 * (j + 1)] = o2[BLOCK * r:BLOCK * (r + 1)]

    return pl.pallas_call(
        body, grid=(N_BLOCKS,), in_specs=_attn_specs(lambda i: i),
        out_specs=pl.BlockSpec((BLOCK, ATTN_W), lambda i: (i, 0)),
        out_shape=jax.ShapeDtypeStruct((SEQ, ATTN_W), BF16), compiler_params=_cp(), name="attn_fwd",
    )(q, kv, kv, tc, ta, tb, tc, ta, tb, sinks)


def _attn_bwd(q, kv, tabs, sinks, do):
    tc, ta, tb = tabs
    last = N_BLOCKS - 1
    clamp = lambda i: jnp.minimum(i, last)

    def place(full, side, kh):
        left = lax.broadcasted_iota(jnp.int32, full.shape, 1) < HEAD_DIM
        valid = jnp.where(left, full, 0.0) if side == 0 else jnp.where(left, 0.0, full)
        return valid if side == kh else pltpu.roll(valid, HEAD_DIM, axis=1)

    def body(q_ref, kvc_ref, kvp_ref, tc_ref, ta_ref, tb_ref, pc_ref, pa_ref, pb_ref, sink_ref, do_ref,
             dq_ref, dkv_ref, ds_ref, carry_ref):
        i = pl.program_id(0)

        @pl.when(i == 0)
        def _():
            ds_ref[...] = jnp.zeros_like(ds_ref)
            carry_ref[...] = jnp.zeros_like(carry_ref)

        @pl.when(i > last)
        def _():
            dkv_ref[...] = carry_ref[...].astype(BF16)

        @pl.when(i <= last)
        def _():
            qs, ks, vs, (c, a, b) = _attn_load(q_ref, kvc_ref, kvp_ref, tc_ref, ta_ref, tb_ref,
                                               pc_ref, pa_ref, pb_ref)
            ok = _attn_mask(i)
            dk = jnp.zeros((2 * BLOCK, 128), F32)
            dv = jnp.zeros((2 * BLOCK, 128), F32)
            dsink = jnp.zeros((1, 128), F32)
            lane = lax.broadcasted_iota(jnp.int32, (1, 128), 1)
            for kh in range(2):
                q2, kst, vst, sinks = _kv_group(qs, ks, vs, kh, sink_ref)
                do2 = jnp.concatenate([do_ref[:, 128 * (2 * kh + r):128 * (2 * kh + r + 1)] for r in range(2)],
                                      axis=0).astype(BF16)
                s = lax.dot_general(q2, kst, NT, preferred_element_type=F32)
                dp = lax.dot_general(do2, vst, NT, preferred_element_type=F32)
                pns, dss = [], []
                for h in range(2):
                    cols = slice(2 * BLOCK * h, 2 * BLOCK * (h + 1))
                    pn, ps = _attn_probs(s[:, cols], ok, sinks[h])
                    dr = jnp.sum(pn * dp[:, cols], axis=-1, keepdims=True)
                    pns.append(pn.astype(BF16))
                    dss.append((pn * (dp[:, cols] - dr)).astype(BF16))
                    for r in range(2):
                        part = -jnp.sum((ps * dr)[BLOCK * r:BLOCK * (r + 1)])
                        dsink += jnp.where(lane == 4 * kh + 2 * r + h, part, 0.0)
                ds2, pn2 = jnp.concatenate(dss, axis=1), jnp.concatenate(pns, axis=1)
                dq2 = jnp.dot(ds2, kst, preferred_element_type=F32) * (HEAD_DIM ** -0.5)
                dk2 = lax.dot_general(ds2, q2, TN, preferred_element_type=F32)
                dv2 = lax.dot_general(pn2, do2, TN, preferred_element_type=F32)
                for h in range(2):
                    dk += place(dk2[2 * BLOCK * h:2 * BLOCK * (h + 1)], h, kh)
                    dv += place(dv2[2 * BLOCK * h:2 * BLOCK * (h + 1)], h, kh)
                for r in range(2):
                    j = 2 * kh + r
                    dq_ref[:, 128 * j:128 * (j + 1)] = _rope_t(dq2[BLOCK * r:BLOCK * (r + 1)], c, a, b).astype(BF16)
            ds_ref[...] += dsink
            dk_prev = _rope_t(dk[:BLOCK], pc_ref[...], pa_ref[...], pb_ref[...])
            dk_cur = _rope_t(dk[BLOCK:], c, a, b)
            prev = jnp.concatenate([dk_prev, dv[:BLOCK]], axis=1)
            dkv_ref[...] = (carry_ref[...] + prev).astype(BF16)
            carry_ref[...] = jnp.concatenate([dk_cur, dv[BLOCK:]], axis=1)

    return pl.pallas_call(
        body, grid=(N_BLOCKS + 1,),
        in_specs=_attn_specs(clamp) + [pl.BlockSpec((BLOCK, ATTN_W), lambda i: (clamp(i), 0))],
        out_specs=[pl.BlockSpec((BLOCK, ATTN_W), lambda i: (clamp(i), 0)),
                   pl.BlockSpec((BLOCK, 2 * KV_W), lambda i: (jnp.maximum(i - 1, 0), 0)),
                   pl.BlockSpec((1, 128), lambda i: (0, 0))],
        out_shape=[jax.ShapeDtypeStruct((SEQ, ATTN_W), BF16), jax.ShapeDtypeStruct((SEQ, 2 * KV_W), BF16),
                   jax.ShapeDtypeStruct((1, 128), F32)],
        scratch_shapes=[pltpu.VMEM((BLOCK, 2 * KV_W), F32)], compiler_params=_cp(), name="attn_bwd",
    )(q, kv, kv, tc, ta, tb, tc, ta, tb, sinks, do)


def _shift_down(z, k):
    row = lax.broadcasted_iota(jnp.int32, z.shape, 0)
    return jnp.where(row < k, 0.0, pltpu.roll(z, k, axis=0))


def _shift_up(z, k):
    n = z.shape[0]
    row = lax.broadcasted_iota(jnp.int32, z.shape, 0)
    return jnp.where(row >= n - k, 0.0, pltpu.roll(z, n - k, axis=0))


def _conv_specs():
    nb = WIDTH // 128
    return [pl.BlockSpec((SEQ, 128), lambda j: (0, j)), pl.BlockSpec((SEQ, 128), lambda j: (0, nb + j)),
            pl.BlockSpec((SEQ, 128), lambda j: (0, 2 * nb + j)), pl.BlockSpec((None, 8, 128), lambda j: (0, 0, j))]


def _conv_fwd(cbx, cw, layer):
    def body(cb_ref, cc_ref, cx_ref, w_ref, o_ref):
        z = cc_ref[...] * cx_ref[...]
        s = w_ref[0:1, :] * _shift_down(z, 2) + w_ref[1:2, :] * _shift_down(z, 1) + w_ref[2:3, :] * z
        o_ref[...] = (cb_ref[...] * s).astype(BF16)

    specs = _conv_specs()
    specs[3] = pl.BlockSpec((None, 8, 128), lambda j: (layer, 0, j))
    return pl.pallas_call(
        body, grid=(WIDTH // 128,), in_specs=specs,
        out_specs=pl.BlockSpec((SEQ, 128), lambda j: (0, j)),
        out_shape=jax.ShapeDtypeStruct((SEQ, WIDTH), BF16), compiler_params=_cp(), name="conv_fwd",
    )(cbx, cbx, cbx, cw)


def _conv_bwd(cbx, cw, layer, dout, tie):
    def body(cb_ref, cc_ref, cx_ref, w_ref, do_ref, tie_ref, dcb_ref, dcc_ref, dcx_ref, dw_ref):
        cc, cx = cc_ref[...], cx_ref[...]
        z = cc * cx
        z1, z2 = _shift_down(z, 1), _shift_down(z, 2)
        w0, w1, w2 = w_ref[0:1, :], w_ref[1:2, :], w_ref[2:3, :]
        dout = do_ref[...]
        ds = dout * cb_ref[...]
        dcb_ref[...] = (dout * (w0 * z2 + w1 * z1 + w2 * z)).astype(BF16)
        dz = w2 * ds + w1 * _shift_up(ds, 1) + w0 * _shift_up(ds, 2)
        dcc_ref[...] = (dz * cx).astype(BF16)
        dcx_ref[...] = (dz * cc).astype(BF16)
        rows = [jnp.sum(ds * zz, axis=0, keepdims=True) for zz in (z2, z1, z)]
        dw_ref[...] = jnp.concatenate(rows + [jnp.zeros((5, 128), F32)], axis=0)

    col = lambda j: (0, j)
    specs = _conv_specs()
    specs[3] = pl.BlockSpec((None, 8, 128), lambda j: (layer, 0, j))
    return pl.pallas_call(
        body, grid=(WIDTH // 128,), in_specs=specs + [pl.BlockSpec((SEQ, 128), col), ANY],
        out_specs=[pl.BlockSpec((SEQ, 128), col), pl.BlockSpec((SEQ, 128), col), pl.BlockSpec((SEQ, 128), col),
                   pl.BlockSpec((8, 128), col)],
        out_shape=[jax.ShapeDtypeStruct((SEQ, WIDTH), BF16)] * 3 + [jax.ShapeDtypeStruct((8, WIDTH), F32)],
        compiler_params=_cp(), name="conv_bwd",
    )(cbx, cbx, cbx, cw, dout, tie)


def _ssm_prep_math(a_re, a_im, log_dt, bt_re, bt_im):
    dt = jnp.exp(log_dt)
    er = jnp.exp(a_re * dt)
    lr = er * jnp.cos(a_im * dt)
    li = er * jnp.sin(a_im * dt)
    n2 = a_re * a_re + a_im * a_im
    cr = ((lr - 1.0) * a_re + li * a_im) / n2
    ci = (li * a_re - (lr - 1.0) * a_im) / n2
    cr3, ci3 = cr[:, None, :], ci[:, None, :]
    return lr, li, cr3 * bt_re - ci3 * bt_im, cr3 * bt_im + ci3 * bt_re


_GS = (SSM_GROUPS, SSM_STATE)
_GHS = (SSM_GROUPS, SSM_GROUP, SSM_STATE)


def _layered(shape):
    return pl.BlockSpec((None,) + shape, lambda l: (l,) + (0,) * len(shape))


def _ssm_prep(a_re, a_im, log_dt, bt_re, bt_im):
    def body(ar, ai, ld, br, bi, o0, o1, o2, o3):
        outs = _ssm_prep_math(ar[...], ai[...], ld[...], br[...], bi[...])
        for o, v in zip((o0, o1, o2, o3), outs):
            o[...] = v

    shapes = [_GS, _GS, _GHS, _GHS]
    return pl.pallas_call(
        body, grid=(DEPTH,), in_specs=[_layered(s) for s in (_GS, _GS, (SSM_GROUPS, 1), _GHS, _GHS)],
        out_specs=[_layered(s) for s in shapes],
        out_shape=[jax.ShapeDtypeStruct((DEPTH,) + s, F32) for s in shapes],
        name="ssm_prep")(a_re, a_im, log_dt, bt_re, bt_im)


def _ssm_prep_bwd(a_re, a_im, log_dt, bt_re, bt_im, cots):
    def body(ar, ai, ld, br, bi, c0, c1, c2, c3, o0, o1, o2, o3, o4):
        _, vjp = jax.vjp(_ssm_prep_math, ar[...], ai[...], ld[...], br[...], bi[...])
        for o, v in zip((o0, o1, o2, o3, o4), vjp((c0[...], c1[...], c2[...], c3[...]))):
            o[...] = v

    ins = (_GS, _GS, (SSM_GROUPS, 1), _GHS, _GHS)
    return pl.pallas_call(
        body, grid=(DEPTH,), in_specs=[_layered(s) for s in ins + (_GS, _GS, _GHS, _GHS)],
        out_specs=[_layered(s) for s in ins],
        out_shape=[jax.ShapeDtypeStruct((DEPTH,) + s, F32) for s in ins],
        name="ssm_prep_bwd")(a_re, a_im, log_dt, bt_re, bt_im, *cots)


LANES_G = 512
N_LANE_GROUPS = SSM_GROUPS * SSM_STATE // LANES_G


def _scan_in_place(xr_ref, xi_ref, ar, ai, reverse):
    shape = (N_CHUNKS, xr_ref.shape[1])
    ar, ai = jnp.broadcast_to(ar, shape), jnp.broadcast_to(ai, shape)

    def rows(tau):
        t = (CHUNK - 1 - tau) if reverse else tau
        return pl.ds(pl.multiple_of(t * N_CHUNKS, N_CHUNKS), N_CHUNKS)

    def step(tau, carry):
        sr, si = carry
        return ar * sr - ai * si + xr_ref[rows(tau), :], ar * si + ai * sr + xi_ref[rows(tau), :]

    zero = jnp.zeros(shape, F32)
    er, ei = lax.fori_loop(0, CHUNK, step, (zero, zero), unroll=8)
    qr, qi = ar, ai
    for _ in range(8):
        qr, qi = qr * qr - qi * qi, 2.0 * qr * qi
    shift = _shift_up if reverse else _shift_down
    for k in (1, 2, 4):
        sr, si = shift(er, k), shift(ei, k)
        er, ei = er + qr * sr - qi * si, ei + qr * si + qi * sr
        qr, qi = qr * qr - qi * qi, 2.0 * qr * qi
    start = (shift(er, 1), shift(ei, 1))

    def write(tau, carry):
        sr, si = step(tau, carry)
        xr_ref[rows(tau), :] = sr
        xi_ref[rows(tau), :] = si
        return sr, si

    return write, start


def _ssm_specs(layer):
    col = lambda w: pl.BlockSpec((SEQ, w), lambda g: (0, g))
    diag = pl.BlockSpec((None, None, 128, LANES_G), lambda g: (layer, g, 0, 0))
    vec = pl.BlockSpec((None, 1, LANES_G), lambda g: (layer, 0, g))
    return col, diag, vec


def _to_scan_order(src_ref, dst_ref):
    def move(tau, _):
        dst_ref[pl.ds(pl.multiple_of(tau * N_CHUNKS, N_CHUNKS), N_CHUNKS), :] = src_ref[pl.ds(tau, N_CHUNKS, stride=CHUNK), :]
        return 0

    lax.fori_loop(0, CHUNK, move, 0, unroll=8)


def _to_time_order(src_ref, dst_ref, dtype):
    for j in range(N_CHUNKS):
        dst_ref[pl.ds(j * CHUNK, CHUNK), :] = src_ref[pl.ds(j, CHUNK, stride=N_CHUNKS), :].astype(dtype)


def _ssm_fwd(u, mats, layer, d):
    def body(u_ref, d_ref, br_ref, bi_ref, cr_ref, ci_ref, ar_ref, ai_ref, xr_ref, xi_ref, y_ref, us_ref):
        _to_scan_order(u_ref, us_ref)
        uv = us_ref[...].astype(BF16)
        xr_ref[...] = jnp.dot(uv, br_ref[...], preferred_element_type=F32)
        xi_ref[...] = jnp.dot(uv, bi_ref[...], preferred_element_type=F32)
        write, start = _scan_in_place(xr_ref, xi_ref, ar_ref[...], ai_ref[...], False)
        lax.fori_loop(0, CHUNK, write, start, unroll=8)
        y = lax.dot_general(xr_ref[...].astype(BF16), cr_ref[...], NT, preferred_element_type=F32)
        y += lax.dot_general(xi_ref[...].astype(BF16), ci_ref[...], NT, preferred_element_type=F32)
        us_ref[...] = y + d_ref[...] * us_ref[...]
        _to_time_order(us_ref, y_ref, F32)

    col, diag, vec = _ssm_specs(layer)
    return pl.pallas_call(
        body, grid=(N_LANE_GROUPS,),
        in_specs=[col(128), pl.BlockSpec((None, 1, 128), lambda g: (layer, 0, g)),
                  diag, diag, diag, diag, vec, vec],
        out_specs=[col(LANES_G), col(LANES_G), col(128)],
        out_shape=[jax.ShapeDtypeStruct((SEQ, SSM_GROUPS * SSM_STATE), F32)] * 2
        + [jax.ShapeDtypeStruct((SEQ, WIDTH), F32)],
        scratch_shapes=[pltpu.VMEM((SEQ, 128), F32)], compiler_params=_cp(), name="ssm_fwd",
    )(u, d, mats["b_re"], mats["b_im"], mats["c_re"], mats["c_im_neg"], mats["a_re"], mats["a_im"])


def _ssm_bwd(dy, x_re, x_im, u, mats, layer, d):
    def body(dyt_ref, ut_ref, d_ref, xr_ref, xi_ref, br_ref, bi_ref, cr_ref, ci_ref, ar_ref, ai_ref,
             du_ref, dar_ref, dai_ref, dbr_ref, dbi_ref, dcr_ref, dci_ref, lr_ref, li_ref, dys_ref, u_ref):
        _to_scan_order(dyt_ref, dys_ref)
        _to_scan_order(ut_ref, u_ref)
        dy = dys_ref[...].astype(BF16)
        lr_ref[...] = jnp.dot(dy, cr_ref[...], preferred_element_type=F32)
        li_ref[...] = jnp.dot(dy, ci_ref[...], preferred_element_type=F32)
        write, start = _scan_in_place(lr_ref, li_ref, ar_ref[...], -ai_ref[...], True)

        def rows(t):
            return pl.ds(pl.multiple_of(t * N_CHUNKS, N_CHUNKS), N_CHUNKS)

        def grad(acc, lam, xpr, xpi):
            return acc[0] + xpr * lam[0] + xpi * lam[1], acc[1] + xpr * lam[1] - xpi * lam[0]

        def down(tau, carry):
            lam = write(tau, carry[0])
            t = CHUNK - 2 - tau
            return lam, grad(carry[1], lam, xr_ref[rows(t), :], xi_ref[rows(t), :])

        zero = jnp.zeros((N_CHUNKS, LANES_G), F32)
        lam, acc = lax.fori_loop(0, CHUNK - 1, down, (start, (zero, zero)), unroll=5)
        lam = write(CHUNK - 1, lam)
        last = rows(CHUNK - 1)
        acc = grad(acc, lam, _shift_down(xr_ref[last, :], 1), _shift_down(xi_ref[last, :], 1))
        dar_ref[...] = jnp.sum(acc[0], axis=0, keepdims=True)
        dai_ref[...] = jnp.sum(acc[1], axis=0, keepdims=True)

        l_re, l_im = lr_ref[...].astype(BF16), li_ref[...].astype(BF16)
        du = lax.dot_general(l_re, br_ref[...], NT, preferred_element_type=F32)
        du += lax.dot_general(l_im, bi_ref[...], NT, preferred_element_type=F32)
        dys_ref[...] = du + dys_ref[...] * d_ref[...]
        _to_time_order(dys_ref, du_ref, BF16)
        uv = u_ref[...].astype(BF16)
        dbr_ref[...] = lax.dot_general(uv, l_re, TN, preferred_element_type=F32)
        dbi_ref[...] = lax.dot_general(uv, l_im, TN, preferred_element_type=F32)
        dcr_ref[...] = lax.dot_general(dy, xr_ref[...].astype(BF16), TN, preferred_element_type=F32)
        dci_ref[...] = lax.dot_general(dy, xi_ref[...].astype(BF16), TN, preferred_element_type=F32)

    col, diag, vec = _ssm_specs(layer)
    out_vec = pl.BlockSpec((1, LANES_G), lambda g: (0, g))
    out_blk = pl.BlockSpec((None, 128, LANES_G), lambda g: (g, 0, 0))
    sds = jax.ShapeDtypeStruct
    return pl.pallas_call(
        body, grid=(N_LANE_GROUPS,),
        in_specs=[col(128), col(128), pl.BlockSpec((None, 1, 128), lambda g: (layer, 0, g)),
                  col(LANES_G), col(LANES_G), diag, diag, diag, diag, vec, vec],
        out_specs=[col(128), out_vec, out_vec, out_blk, out_blk, out_blk, out_blk],
        out_shape=[sds((SEQ, WIDTH), BF16)] + [sds((1, SSM_GROUPS * SSM_STATE), F32)] * 2
        + [sds((N_LANE_GROUPS, 128, LANES_G), F32)] * 4,
        scratch_shapes=[pltpu.VMEM((SEQ, LANES_G), F32)] * 2 + [pltpu.VMEM((SEQ, 128), F32)] * 2,
        compiler_params=_cp(), name="ssm_bwd",
    )(dy, u, d, x_re, x_im, mats["b_re"], mats["b_im"], mats["c_re"], mats["c_im_neg"],
      mats["a_re"], mats["a_im"])


_GELU_C = math.sqrt(2.0 / math.pi)


def _gelu(y):
    return 0.5 * y * (1.0 + jnp.tanh(_GELU_C * (y + 0.044715 * (y * y * y))))


def _glu_fwd(y, wglu):
    tt = 512

    def body(y_ref, w_ref, z_ref):
        ys = _gelu(y_ref[...])
        a = jnp.dot(ys.astype(BF16), w_ref[...], preferred_element_type=F32)
        z_ref[...] = (ys * jax.nn.sigmoid(a)).astype(BF16)

    blk = pl.BlockSpec((tt, WIDTH), lambda i: (i, 0))
    return pl.pallas_call(body, grid=(SEQ // tt,), in_specs=[blk, _full((WIDTH, WIDTH))], out_specs=blk,
                          out_shape=jax.ShapeDtypeStruct((SEQ, WIDTH), BF16), compiler_params=_cp(),
                          name="glu_fwd")(y, wglu)


def _glu_bwd(y, wglu, dz, u):
    tt = 512

    def body(y_ref, w_ref, dz_ref, u_ref, dy_ref, ys_ref, da_ref, dd_ref):
        @pl.when(pl.program_id(0) == 0)
        def _():
            dd_ref[...] = jnp.zeros_like(dd_ref)

        yv = y_ref[...]
        t = jnp.tanh(_GELU_C * (yv + 0.044715 * (yv * yv * yv)))
        ys = 0.5 * yv * (1.0 + t)
        ysb = ys.astype(BF16)
        sg = jax.nn.sigmoid(jnp.dot(ysb, w_ref[...], preferred_element_type=F32))
        dz = dz_ref[...].astype(F32)
        da = (dz * ys * sg * (1.0 - sg)).astype(BF16)
        dys = dz * sg + lax.dot_general(da, w_ref[...], NT, preferred_element_type=F32)
        dy = dys * (0.5 * (1.0 + t) + 0.5 * yv * (1.0 - t * t) * _GELU_C * (1.0 + 3 * 0.044715 * (yv * yv)))
        dy_ref[...] = dy
        ys_ref[...] = ysb
        da_ref[...] = da
        dd_ref[...] += jnp.sum(dy * u_ref[...], axis=0, keepdims=True)

    blk = pl.BlockSpec((tt, WIDTH), lambda i: (i, 0))
    return pl.pallas_call(
        body, grid=(SEQ // tt,), in_specs=[blk, _full((WIDTH, WIDTH)), blk, blk],
        out_specs=[blk, blk, blk, _full((1, WIDTH))],
        out_shape=[jax.ShapeDtypeStruct((SEQ, WIDTH), F32)] + [jax.ShapeDtypeStruct((SEQ, WIDTH), BF16)] * 2
        + [jax.ShapeDtypeStruct((1, WIDTH), F32)],
        compiler_params=_cp(), name="glu_bwd")(y, wglu, dz, u)


def _mix_specs(tt, layer):
    row = lambda w: pl.BlockSpec((tt, w), lambda i: (i, 0))
    gate = lambda j: pl.BlockSpec((tt, D_MODEL), lambda i: (i, j))
    wo = lambda j: pl.BlockSpec((D_MODEL, WIDTH), lambda i: (0, j))
    return [row(D_MODEL), row(WIDTH), row(WIDTH), row(WIDTH), gate(0), gate(1), gate(2),
            pl.BlockSpec((None, 1, GATE_W), lambda i: (layer, 0, 0)), wo(0), wo(1), wo(2),
            _full((D_MODEL, D_MODEL))]


def _mix_branches(o_ref, c_ref, z_ref, g_refs, b_ref, wa_ref, wc_ref, ws_ref):
    ys = [lax.dot_general(r[...], w[...], NT, preferred_element_type=F32)
          for r, w in ((o_ref, wa_ref), (c_ref, wc_ref), (z_ref, ws_ref))]
    gates = [jax.nn.sigmoid(g_refs[j][...] + b_ref[:, D_MODEL * j:D_MODEL * (j + 1)]) for j in range(3)]
    return ys, gates


def _mix_fwd(x, o, cv, z, glog, b_gate, layer, wbt, wmix, tie):
    tt = 256

    def body(x_ref, o_ref, c_ref, z_ref, g0, g1, g2, b_ref, wa_ref, wc_ref, ws_ref, wm_ref, tie_ref, x1_ref):
        ys, gates = _mix_branches(o_ref, c_ref, z_ref, (g0, g1, g2), b_ref, wa_ref, wc_ref, ws_ref)
        merged = gates[0] * ys[0] + gates[1] * ys[1] + gates[2] * ys[2]
        x1_ref[...] = x_ref[...] + jnp.dot(merged.astype(BF16), wm_ref[...], preferred_element_type=F32)

    return pl.pallas_call(
        body, grid=(SEQ // tt,), in_specs=_mix_specs(tt, layer) + [ANY],
        out_specs=pl.BlockSpec((tt, D_MODEL), lambda i: (i, 0)),
        out_shape=jax.ShapeDtypeStruct((SEQ, D_MODEL), F32), compiler_params=_cp(), name="mix_fwd",
    )(x, o, cv, z, glog, glog, glog, b_gate, wbt, wbt, wbt, wmix, tie)


def _mix_bwd(dx1, o, cv, z, glog, b_gate, layer, wbt, wmix, tie):
    tt = 256

    def body(dx_ref, o_ref, c_ref, z_ref, g0, g1, g2, b_ref, wa_ref, wc_ref, ws_ref, wm_ref, tie_ref,
             mg_ref, dya_ref, dyc_ref, dys_ref, do_ref, dc_ref, dz_ref, dgl_ref, db_ref):
        @pl.when(pl.program_id(0) == 0)
        def _():
            db_ref[...] = jnp.zeros_like(db_ref)

        ys, gates = _mix_branches(o_ref, c_ref, z_ref, (g0, g1, g2), b_ref, wa_ref, wc_ref, ws_ref)
        mg_ref[...] = (gates[0] * ys[0] + gates[1] * ys[1] + gates[2] * ys[2]).astype(BF16)
        dm = lax.dot_general(dx_ref[...].astype(BF16), wm_ref[...], NT, preferred_element_type=F32)
        for j, (dy_ref, w_ref, d_ref) in enumerate(((dya_ref, wa_ref, do_ref), (dyc_ref, wc_ref, dc_ref),
                                                    (dys_ref, ws_ref, dz_ref))):
            dy = (dm * gates[j]).astype(BF16)
            dy_ref[...] = dy
            d_ref[...] = jnp.dot(dy, w_ref[...], preferred_element_type=F32)
            dgl = dm * ys[j] * gates[j] * (1.0 - gates[j])
            dgl_ref[:, D_MODEL * j:D_MODEL * (j + 1)] = dgl.astype(BF16)
            db_ref[:, D_MODEL * j:D_MODEL * (j + 1)] += jnp.sum(dgl, axis=0, keepdims=True)

    row = lambda w: pl.BlockSpec((tt, w), lambda i: (i, 0))
    sds = jax.ShapeDtypeStruct
    return pl.pallas_call(
        body, grid=(SEQ // tt,), in_specs=_mix_specs(tt, layer) + [ANY],
        out_specs=[row(D_MODEL)] * 4 + [row(WIDTH)] * 3 + [row(GATE_W), _full((1, GATE_W))],
        out_shape=[sds((SEQ, D_MODEL), BF16)] * 4 + [sds((SEQ, WIDTH), F32)] * 3
        + [sds((SEQ, GATE_W), BF16), sds((1, GATE_W), F32)],
        compiler_params=_cp(), name="mix_bwd",
    )(dx1, o, cv, z, glog, glog, glog, b_gate, wbt, wbt, wbt, wmix, tie)


def _ffn_out_fwd(x1, act, wout, tie):
    tt = 512

    def body(x_ref, a_ref, w_ref, tie_ref, o_ref):
        o_ref[...] = x_ref[...] + jnp.dot(a_ref[...], w_ref[...], preferred_element_type=F32)

    row = lambda w: pl.BlockSpec((tt, w), lambda i: (i, 0))
    return pl.pallas_call(
        body, grid=(SEQ // tt,), in_specs=[row(D_MODEL), row(FFN_H), _full((FFN_H, D_MODEL)), ANY],
        out_specs=row(D_MODEL), out_shape=jax.ShapeDtypeStruct((SEQ, D_MODEL), F32),
        compiler_params=_cp(), name="ffn_out_fwd")(x1, act, wout, tie)


def _ffn_out_bwd(dx2, up, silu, dsilu, wout, tie):
    tt = 256

    def body(dx_ref, up_ref, silu_ref, dsilu_ref, w_ref, tie_ref, dgu_ref):
        dact = lax.dot_general(dx_ref[...].astype(BF16), w_ref[...], NT, preferred_element_type=F32).astype(BF16)
        dgu_ref[:, :FFN_H] = dact * up_ref[...] * dsilu_ref[...]
        dgu_ref[:, FFN_H:] = dact * silu_ref[...]

    row = lambda w: pl.BlockSpec((tt, w), lambda i: (i, 0))
    return pl.pallas_call(
        body, grid=(SEQ // tt,),
        in_specs=[row(D_MODEL), row(FFN_H), row(FFN_H), row(FFN_H), _full((FFN_H, D_MODEL)), ANY],
        out_specs=row(2 * FFN_H), out_shape=jax.ShapeDtypeStruct((SEQ, 2 * FFN_H), BF16),
        compiler_params=_cp(), name="ffn_out_bwd")(dx2, up, silu, dsilu, wout, tie)


def _loss_head(x, g, target):
    tt = 256

    def body(x_ref, g_ref, t_ref, loss_ref, dx_ref, dg_ref):
        @pl.when(pl.program_id(0) == 0)
        def _():
            loss_ref[...] = jnp.zeros_like(loss_ref)
            dg_ref[...] = jnp.zeros_like(dg_ref)

        xv = x_ref[...]
        r = lax.rsqrt(jnp.mean(xv * xv, axis=-1, keepdims=True) + NORM_EPS)
        xh = xv * r
        err = xh * g_ref[...] - t_ref[...]
        loss_ref[...] += 0.5 * jnp.sum(jnp.mean(err * err, axis=-1, keepdims=True))
        dy = err * (1.0 / D_MODEL)
        gy = dy * g_ref[...]
        dx_ref[...] = r * (gy - xh * jnp.mean(gy * xh, axis=-1, keepdims=True))
        dg_ref[...] += jnp.sum(dy * xh, axis=0, keepdims=True)

    row = pl.BlockSpec((tt, D_MODEL), lambda i: (i, 0))
    return pl.pallas_call(
        body, grid=(SEQ // tt,), in_specs=[row, _full((1, D_MODEL)), row],
        out_specs=[_full((1, 128)), row, _full((1, D_MODEL))],
        out_shape=[jax.ShapeDtypeStruct((1, 128), F32), jax.ShapeDtypeStruct((SEQ, D_MODEL), F32),
                   jax.ShapeDtypeStruct((1, D_MODEL), F32)],
        compiler_params=_cp(), name="loss_head")(x, g, target)


def _adam_math(g, w, m, v):
    nm = B1 * m + (1.0 - B1) * g
    nv = B2 * v + (1.0 - B2) * (g * g)
    m_hat = nm / (1.0 - B1 ** STEP)
    v_hat = nv / (1.0 - B2 ** STEP)
    return -LR * (m_hat / (jnp.sqrt(v_hat) + ADAM_EPS) + WD * w), nm, nv


def _adamw_small(parts, w, m, v, name):
    def body(p_ref, w_ref, m_ref, v_ref, g_ref, d_ref, nm_ref, nv_ref):
        g = p_ref[0].astype(F32)
        for k in range(1, N_DEV):
            g = g + p_ref[k].astype(F32)
        g_ref[...] = g
        d_ref[...], nm_ref[...], nv_ref[...] = _adam_math(g, w_ref[...], m_ref[...], v_ref[...])

    out_shape = [jax.ShapeDtypeStruct(w.shape, F32)] * 4
    if w.ndim < 3:
        return pl.pallas_call(body, out_shape=out_shape, name=name)(parts, w, m, v)
    rest = w.shape[1:]
    zeros = (0,) * len(rest)
    blk = pl.BlockSpec((None,) + rest, lambda l: (l,) + zeros)
    return pl.pallas_call(
        body, grid=(w.shape[0],),
        in_specs=[pl.BlockSpec((N_DEV, None) + rest, lambda l: (0, l) + zeros), blk, blk, blk],
        out_specs=[blk] * 4, out_shape=out_shape, name=name)(parts, w, m, v)


def _adamw(parts, w, m, v, tr, name, groups=None, fill=None, tie=None):
    n_groups, rows, cols = w.shape
    n_parts = parts.shape[1]
    lo, hi = groups if groups is not None else (0, n_groups)

    def body(p_ref, w_ref, m_ref, v_ref, *rest):
        g_ref, d_ref, nm_ref, nv_ref = rest[-4:]
        g = p_ref[0].astype(F32)
        for k in range(1, n_parts):
            g = g + p_ref[k].astype(F32)
        nm = B1 * m_ref[...] + (1.0 - B1) * g
        nv = B2 * v_ref[...] + (1.0 - B2) * (g * g)
        m_hat = nm / (1.0 - B1 ** STEP)
        v_hat = nv / (1.0 - B2 ** STEP)
        g_ref[...] = g
        d_ref[...] = -LR * (m_hat / (jnp.sqrt(v_hat) + ADAM_EPS) + WD * w_ref[...])
        nm_ref[...] = nm
        nv_ref[...] = nv

    blk = pl.BlockSpec((None, tr, cols), lambda l, i: (l + lo, i, 0))
    p_lo = lo if parts.shape[0] == n_groups else 0
    extra = ([] if fill is None else list(fill)) + ([] if tie is None else [tie])
    return pl.pallas_call(
        body, grid=(hi - lo, rows // tr),
        in_specs=[pl.BlockSpec((None, n_parts, tr, cols), lambda l, i: (l + p_lo, 0, i, 0)), blk, blk, blk]
        + [ANY] * len(extra),
        out_specs=[blk] * 4, out_shape=[jax.ShapeDtypeStruct((n_groups, rows, cols), F32)] * 4,
        input_output_aliases={} if fill is None else {4 + j: j for j in range(4)},
        compiler_params=_cp(), name=name)(parts, w, m, v, *extra)


def _split_start(name, arrays, n_sems, plan, after=None):
    n = len(arrays)
    order = [] if after is None else [after]
    n_in = n + len(order)

    def body(*refs):
        ins, send_sems, recv_sems, token = refs[:n], refs[n_in], refs[n_in + 1], refs[-1]
        for src, dst, k, to in plan(ins)[0]:
            pltpu.make_async_remote_copy(src_ref=src, dst_ref=dst, send_sem=send_sems.at[k], recv_sem=recv_sems.at[k],
                                         device_id=to, device_id_type=MESH_ID).start()
        token[...] = jnp.zeros_like(token)

    outs = pl.pallas_call(
        body, name=name,
        out_shape=(pltpu.SemaphoreType.DMA((n_sems,)), pltpu.SemaphoreType.DMA((n_sems,)),
                   *[pltpu.HBM(a.shape, a.dtype) for a in arrays], jax.ShapeDtypeStruct((8, 128), F32)),
        in_specs=[HBM] * n + [ANY] * len(order),
        out_specs=(SEM, SEM, *[HBM] * n, pl.BlockSpec(memory_space=pltpu.VMEM)),
        input_output_aliases={i: 2 + i for i in range(n)},
        compiler_params=pltpu.CompilerParams(has_side_effects=EFFECT),
    )(*[pltpu.with_memory_space_constraint(a, pltpu.HBM) for a in arrays], *order)
    return outs[0], outs[1], list(outs[2:2 + n]), outs[-1]


def _split_wait(name, arrays, send_sems, recv_sems, after, plan):
    n = len(arrays)
    order = list(after) if isinstance(after, (list, tuple)) else [after]

    def body(*refs):
        ins, s_sems, r_sems = refs[:n], refs[n], refs[n + 1]
        sends, arrivals = plan(ins)
        x, y, c = lax.axis_index("x"), lax.axis_index("y"), lax.axis_index("c")
        for src, dst, k, to in sends:
            pltpu.make_async_remote_copy(src_ref=src, dst_ref=dst, send_sem=s_sems.at[k], recv_sem=r_sems.at[k],
                                         device_id=to, device_id_type=MESH_ID).wait_send()
        for dst, k in arrivals:
            pltpu.make_async_remote_copy(src_ref=dst, dst_ref=dst, send_sem=s_sems.at[k], recv_sem=r_sems.at[k],
                                         device_id=(x, y, c), device_id_type=MESH_ID).wait_recv()

    return pl.pallas_call(
        body, name=name, out_shape=[pltpu.HBM(a.shape, a.dtype) for a in arrays],
        in_specs=[HBM] * n + [SEM, SEM] + [ANY] * len(order), out_specs=[HBM] * n,
        input_output_aliases={i: i for i in range(n)},
        compiler_params=pltpu.CompilerParams(has_side_effects=EFFECT),
    )(*arrays, send_sems, recv_sems, *order)


def _chips():
    x, y, c = lax.axis_index("x"), lax.axis_index("y"), lax.axis_index("c")
    return x, y, c, [(1 - x, y), (x, 1 - y), (1 - x, 1 - y)]


def _plan_gather_chips(refs):
    x, y, c, chips = _chips()
    me = 4 * x + 2 * y + c
    n = len(refs) // 2
    sends, arrivals = [], []
    for i in range(n):
        src, land = refs[i], refs[n + i]
        sends.append((src, land.at[me], 4 * i, (x, y, 1 - c)))
        arrivals.append((land.at[4 * x + 2 * y + 1 - c], 4 * i))
        for j, (px, py) in enumerate(chips):
            sends.append((src, land.at[me], 4 * i + 1 + j, (px, py, c)))
            arrivals.append((land.at[4 * px + 2 * py + c], 4 * i + 1 + j))
    return sends, arrivals


def _plan_gather_pass(refs):
    x, y, c, chips = _chips()
    sends, arrivals = [], []
    for i in range(len(refs)):
        for j, (px, py) in enumerate(chips):
            slot = refs[i].at[4 * px + 2 * py + c]
            sends.append((slot, slot, 4 * i + j, (x, y, 1 - c)))
            arrivals.append((refs[i].at[4 * px + 2 * py + 1 - c], 4 * i + j))
        back = refs[i].at[4 * x + 2 * y + 1 - c]
        sends.append((back, back, 4 * i + 3, (x, y, 1 - c)))
        arrivals.append((refs[i].at[4 * x + 2 * y + c], 4 * i + 3))
    return sends, arrivals


def _plan_scatter_pair(refs):
    x, y, c = lax.axis_index("x"), lax.axis_index("y"), lax.axis_index("c")
    n = len(refs) // 2
    sends, arrivals = [], []
    for i in range(n):
        for q in range(4):
            sends.append((refs[i].at[q, 1 - c], refs[n + i].at[q], 4 * i + q, (x, y, 1 - c)))
            arrivals.append((refs[n + i].at[q], 4 * i + q))
    return sends, arrivals


def _plan_scatter_chips(layer):
    def plan(refs):
        x, y, c, chips = _chips()
        n = len(refs) // 2
        sends, arrivals = [], []
        for i in range(n):
            for j, (px, py) in enumerate(chips):
                sends.append((refs[i].at[2 * px + py], refs[n + i].at[layer, 2 * x + y], 3 * i + j, (px, py, c)))
                arrivals.append((refs[n + i].at[layer, 2 * px + py], 3 * i + j))
        return sends, arrivals

    return plan


def _pair_sum(parts4, from_pair, landing, layer, core, tr, name):
    _, _, rows, cols = parts4.shape

    def body(c_ref, p_ref, s_ref, l_ref, sum_ref, land_ref):
        v = (p_ref[...].astype(F32) + s_ref[...].astype(F32)).astype(BF16)
        sum_ref[...] = v
        land_ref[...] = v

    blk = pl.BlockSpec((None, tr, cols), lambda q, i, c_ref: (q, i, 0))
    return pl.pallas_call(
        body,
        grid_spec=pltpu.PrefetchScalarGridSpec(
            num_scalar_prefetch=1, grid=(4, rows // tr),
            in_specs=[pl.BlockSpec((None, None, tr, cols), lambda q, i, c_ref: (q, c_ref[0], i, 0)), blk, ANY],
            out_specs=[blk, pl.BlockSpec((None, None, tr, cols), lambda q, i, c_ref: (layer, q, i, 0))]),
        out_shape=[jax.ShapeDtypeStruct((4, rows, cols), BF16), jax.ShapeDtypeStruct(landing.shape, BF16)],
        input_output_aliases={3: 1}, compiler_params=_cp(), name=name,
    )(core, parts4, from_pair, landing)


def _travel_layout(t):
    tr = lambda a: jnp.swapaxes(a, 1, 2)
    branch = jnp.concatenate([tr(t["w_attn_o"]), tr(t["w_conv_o"]), tr(t["w_ssm_o"])], axis=2)
    return [tr(t["w_in"]), tr(t["w_ffn_in"]), t["w_ffn_out"], t["w_mix_o"], branch, t["w_ssm_glu"]]


def _native_layout(a):
    tr = lambda x: jnp.swapaxes(x, 1, 2)
    b = a[4]
    return {"w_in": tr(a[0]), "w_ffn_in": tr(a[1]), "w_ffn_out": a[2], "w_mix_o": a[3],
            "w_attn_o": tr(b[:, :, :WIDTH]), "w_conv_o": tr(b[:, :, WIDTH:2 * WIDTH]),
            "w_ssm_o": tr(b[:, :, 2 * WIDTH:]), "w_ssm_glu": a[5]}


def _embed(t):
    eye = jnp.eye(8, dtype=t.dtype)
    t = t.reshape(DEPTH, N_LANE_GROUPS, 8, SSM_GROUP, SSM_STATE)
    return (t[:, :, :, :, None, :] * eye[None, None, :, None, :, None]).reshape(DEPTH, N_LANE_GROUPS, 128, LANES_G)


def _diag_blocks(t):
    t = t.reshape(DEPTH, N_LANE_GROUPS, 8, SSM_GROUP, 8, SSM_STATE)
    return jnp.einsum("lgahap->lgahp", t).reshape(DEPTH, SSM_GROUPS, SSM_GROUP, SSM_STATE)


def _rope_tabs():
    pos = jnp.arange(SEQ, dtype=F32)
    inv_freq = ROPE_THETA ** (-jnp.arange(0, ROT_DIM, 2, dtype=F32) / ROT_DIM)
    ang = pos[:, None] * inv_freq[None, :]
    cos, sin = jnp.cos(ang), jnp.sin(ang)
    one, zero = jnp.ones((SEQ, HEAD_DIM - ROT_DIM), F32), jnp.zeros((SEQ, HEAD_DIM - ROT_DIM), F32)
    z8 = jnp.zeros((SEQ, 8), F32)
    head = lambda *p: jnp.tile(jnp.concatenate(p, axis=1), (1, 2))
    return head(cos, cos, one), head(-sin, z8, zero), head(z8, sin, zero)


def _ssm_mats(sp):
    lr, li, bbr, bbi = _ssm_prep(sp["a_re"], sp["a_im"], sp["log_dt"], sp["bt_re"], sp["bt_im"])
    lanes = SSM_GROUPS * SSM_STATE
    return {
        "a_re": lr.reshape(DEPTH, 1, lanes), "a_im": li.reshape(DEPTH, 1, lanes),
        "b_re": _embed(bbr).astype(BF16), "b_im": _embed(bbi).astype(BF16),
        "c_re": _embed(sp["c_re"]).astype(BF16), "c_im_neg": _embed(-sp["c_im"]).astype(BF16),
    }


def _layer_fwd(x, i, w, rp, mats, tabs, tie, hooks):
    q, kv, cbx, u, glog, h = _rms_mm_in(x, rp["norm_mix"][i], w["win_t"], tabs, tie)
    o = _attn_fwd(q, kv, tabs, rp["attn_sinks"][i])
    cv = _conv_fwd(cbx, rp["conv_w"], i)
    x_re, x_im, y = _ssm_fwd(u, mats, i, rp["ssm_d"])
    z = _glu_fwd(y, w["wglu"])
    x1 = _mix_fwd(x, o, cv, z, glog, rp["b_gate"], i, w["branch_t"], w["wmix"], hooks["early"](z))
    hooks["pre_ffn"](x1)
    act, up, silu, dsilu, h2 = _rms_mm_ffn(x1, rp["norm_ffn"][i], w["wffn_t"])
    x2 = _ffn_out_fwd(x1, act, w["wout"], hooks["mid"](h2))
    kept = dict(x=x, q=q, kv=kv, cbx=cbx, u=u, glog=glog, h=h, o=o, cv=cv, z=z, y=y,
                x_re=x_re, x_im=x_im, x1=x1, act=act, up=up, silu=silu, dsilu=dsilu, h2=h2)
    return x2, kept


def _layer_bwd(dx2, k, i, w, rp, mats, tabs, tie, hooks):
    dgu = _ffn_out_bwd(dx2, k["up"], k["silu"], k["dsilu"], w["wout"], tie)
    g_wout = _mm_tn(k["act"], dx2, tm=FFN_H // 2, tn=1024, name="mm_tn_ffn_out")
    g_wffn_t = _mm_tn(dgu, k["h2"], tm=FFN_H // 2, tn=1024, name="mm_tn_ffn_in")
    dx1, d_norm_ffn = _mm_rmsbwd([dgu], w["wffn_t"], k["x1"], rp["norm_ffn"][i], dx2, "mm_rmsbwd_ffn")

    mg, dya, dyc, dys, do, dcv, dz, dgl, db_gate = _mix_bwd(
        dx1, k["o"], k["cv"], k["z"], k["glog"], rp["b_gate"], i, w["branch_t"], w["wmix"],
        hooks["mid"]((g_wffn_t, g_wout, d_norm_ffn)))
    g_wmix = _mm_tn(mg, dx1, tm=1024, tn=512, name="mm_tn_mix")
    g_branch_t = _tn_branches((dya, dyc, dys), (k["o"], k["cv"], k["z"]))

    dy, ys16, da16, dd = _glu_bwd(k["y"], w["wglu"], dz, k["u"])
    g_wglu = _mm_tn(ys16, da16, tm=256, tn=512, name="mm_tn_glu")
    du, da_re, da_im, db_re, db_im, dc_re, dc_im = _ssm_bwd(dy, k["x_re"], k["x_im"], k["u"], mats, i, rp["ssm_d"])

    dcb, dcc, dcx, d_conv_w = _conv_bwd(k["cbx"], rp["conv_w"], i, dcv, hooks["late"](du))
    dq, dkv, d_sinks = _attn_bwd(k["q"], k["kv"], tabs, rp["attn_sinks"][i], do)

    pieces = [dq, dkv, dcb, dcc, dcx, du, dgl]
    g_win_t = _tn_pieces(pieces, k["h"])
    dx, d_norm_mix = _mm_rmsbwd(pieces, w["win_t"], k["x"], rp["norm_mix"][i], dx1, "mm_rmsbwd_in")

    grads = [g_win_t, g_wffn_t, g_wout, g_wmix, g_branch_t, g_wglu]
    small = dict(norm_mix=d_norm_mix, b_gate=db_gate, attn_sinks=d_sinks, ssm_d=dd, norm_ffn=d_norm_ffn,
                 conv_w=d_conv_w, da_re=da_re, da_im=da_im, db_re=db_re, db_im=db_im, dc_re=dc_re, dc_im=dc_im)
    return dx, grads, small


def _replicated_grads(sg, sp):
    stack = lambda name: jnp.stack([sg[i][name] for i in range(DEPTH)])
    cots = (stack("da_re").reshape(DEPTH, *_GS), stack("da_im").reshape(DEPTH, *_GS),
            _diag_blocks(stack("db_re")), _diag_blocks(stack("db_im")))
    d_a_re, d_a_im, d_log_dt, d_bt_re, d_bt_im = _ssm_prep_bwd(
        sp["a_re"], sp["a_im"], sp["log_dt"], sp["bt_re"], sp["bt_im"], cots)
    sgrads = {"norm_mix": stack("norm_mix"), "b_gate": stack("b_gate"),
              "attn_sinks": stack("attn_sinks")[:, :, :N_Q_HEADS], "ssm_a_re": d_a_re, "ssm_a_im": d_a_im,
              "ssm_b_re": jnp.swapaxes(d_bt_re, 2, 3), "ssm_b_im": jnp.swapaxes(d_bt_im, 2, 3),
              "ssm_c_re": _diag_blocks(stack("dc_re")), "ssm_c_im": -_diag_blocks(stack("dc_im")),
              "ssm_d": stack("ssm_d"), "ssm_log_dt": d_log_dt, "norm_ffn": stack("norm_ffn")}
    return sgrads, stack("conv_w")[:, :3]


def kernel(x, norm_mix, w_in, b_gate, attn_sinks, w_attn_o, conv_w, w_conv_o, ssm_a_re, ssm_a_im, ssm_b_re, ssm_b_im, ssm_c_re, ssm_c_im, ssm_d, ssm_log_dt, w_ssm_glu, w_ssm_o, w_mix_o, norm_ffn, w_ffn_in, w_ffn_out, norm_final, loss_target, m_norm_mix, m_w_in, m_b_gate, m_attn_sinks, m_w_attn_o, m_conv_w, m_w_conv_o, m_ssm_a_re, m_ssm_a_im, m_ssm_b_re, m_ssm_b_im, m_ssm_c_re, m_ssm_c_im, m_ssm_d, m_ssm_log_dt, m_w_ssm_glu, m_w_ssm_o, m_w_mix_o, m_norm_ffn, m_w_ffn_in, m_w_ffn_out, m_norm_final, v_norm_mix, v_w_in, v_b_gate, v_attn_sinks, v_w_attn_o, v_conv_w, v_w_conv_o, v_ssm_a_re, v_ssm_a_im, v_ssm_b_re, v_ssm_b_im, v_ssm_c_re, v_ssm_c_im, v_ssm_d, v_ssm_log_dt, v_w_ssm_glu, v_w_ssm_o, v_w_mix_o, v_norm_ffn, v_w_ffn_in, v_w_ffn_out, v_norm_final):
    big = {"w": dict(w_in=w_in, w_attn_o=w_attn_o, w_conv_o=w_conv_o, w_ssm_glu=w_ssm_glu, w_ssm_o=w_ssm_o,
                     w_mix_o=w_mix_o, w_ffn_in=w_ffn_in, w_ffn_out=w_ffn_out),
           "m": dict(w_in=m_w_in, w_attn_o=m_w_attn_o, w_conv_o=m_w_conv_o, w_ssm_glu=m_w_ssm_glu,
                     w_ssm_o=m_w_ssm_o, w_mix_o=m_w_mix_o, w_ffn_in=m_w_ffn_in, w_ffn_out=m_w_ffn_out),
           "v": dict(w_in=v_w_in, w_attn_o=v_w_attn_o, w_conv_o=v_w_conv_o, w_ssm_glu=v_w_ssm_glu,
                     w_ssm_o=v_w_ssm_o, w_mix_o=v_w_mix_o, w_ffn_in=v_w_ffn_in, w_ffn_out=v_w_ffn_out)}
    small = {"w": dict(norm_mix=norm_mix, b_gate=b_gate, attn_sinks=attn_sinks, ssm_a_re=ssm_a_re,
                       ssm_a_im=ssm_a_im, ssm_b_re=ssm_b_re, ssm_b_im=ssm_b_im, ssm_c_re=ssm_c_re,
                       ssm_c_im=ssm_c_im, ssm_d=ssm_d, ssm_log_dt=ssm_log_dt, norm_ffn=norm_ffn),
             "m": dict(norm_mix=m_norm_mix, b_gate=m_b_gate, attn_sinks=m_attn_sinks, ssm_a_re=m_ssm_a_re,
                       ssm_a_im=m_ssm_a_im, ssm_b_re=m_ssm_b_re, ssm_b_im=m_ssm_b_im, ssm_c_re=m_ssm_c_re,
                       ssm_c_im=m_ssm_c_im, ssm_d=m_ssm_d, ssm_log_dt=m_ssm_log_dt, norm_ffn=m_norm_ffn),
             "v": dict(norm_mix=v_norm_mix, b_gate=v_b_gate, attn_sinks=v_attn_sinks, ssm_a_re=v_ssm_a_re,
                       ssm_a_im=v_ssm_a_im, ssm_b_re=v_ssm_b_re, ssm_b_im=v_ssm_b_im, ssm_c_re=v_ssm_c_re,
                       ssm_c_im=v_ssm_c_im, ssm_d=v_ssm_d, ssm_log_dt=v_ssm_log_dt, norm_ffn=v_norm_ffn)}
    finals = {"w": norm_final, "m": m_norm_final, "v": v_norm_final}
    convs = {"w": conv_w, "m": m_conv_w, "v": v_conv_w}
    small_out_shapes = {name: a.shape for name, a in small["w"].items()}
    small_out_shapes.update(norm_final=(D_MODEL,), conv_w=(DEPTH, 3, 64))
    small_shapes = dict(small_out_shapes, norm_final=(1, D_MODEL), conv_w=(DEPTH, 3, WIDTH))
    dense = ("ssm_b_re", "ssm_b_im", "ssm_c_re", "ssm_c_im")
    for name in dense:
        small_shapes[name] = (DEPTH, SSM_GROUPS, SSM_GROUP * SSM_STATE)
    small_wmv = {name: [(convs[s] if name == "conv_w" else finals[s] if name == "norm_final" else small[s][name])
                        .reshape((DEPTH, 3, 64) if name == "conv_w" else small_shapes[name]) for s in "wmv"]
                 for name in small_shapes}
    mine = 4 * lax.axis_index("x") + 2 * lax.axis_index("y") + lax.axis_index("c")

    travel = {s: _travel_layout(big[s]) for s in "wmv"}
    stacked16 = list(zip(*[[a[0] for a in _travel_layout({n: w[i:i + 1].astype(BF16) for n, w in big["w"].items()})]
                           for i in range(DEPTH)]))
    rp = {"norm_mix": norm_mix[:, None], "norm_ffn": norm_ffn[:, None], "attn_sinks": attn_sinks[:, None],
          "b_gate": b_gate[:, None], "ssm_d": ssm_d[:, None]}
    sp = {"a_re": ssm_a_re, "a_im": ssm_a_im, "log_dt": ssm_log_dt[:, :, None],
          "bt_re": jnp.swapaxes(ssm_b_re, 2, 3), "bt_im": jnp.swapaxes(ssm_b_im, 2, 3),
          "c_re": ssm_c_re, "c_im": ssm_c_im}
    rows_tile = {"win_t": 368, "wffn_t": 352, "wout": 176, "wmix": 128, "branch_t": 128, "wglu": 64}
    core = lax.axis_index("c").astype(jnp.int32).reshape(1)
    no_tie = jnp.zeros((8, 128), F32)

    def landing_zones(srcs):
        return [lax.empty((N_DEV,) + s.shape, s.dtype) for s in srcs]

    def gather_chips(tag, i, kinds, after, extra=()):
        srcs = [stacked16[j][i] for j in kinds] + list(extra)
        s_sems, r_sems, arrays, token = _split_start(
            f"gather_chips_start_{tag}", srcs + landing_zones(srcs), 4 * len(srcs), _plan_gather_chips, after)
        return (tag, s_sems, r_sems, arrays), token

    def gather_pass(state, after):
        tag, s_sems, r_sems, arrays = state
        arrays = _split_wait(f"gather_chips_wait_{tag}", arrays, s_sems, r_sems, after, _plan_gather_chips)
        n = len(arrays) // 2
        s_sems, r_sems, lands, token = _split_start(
            f"gather_pass_start_{tag}", list(arrays[n:]), 4 * n, _plan_gather_pass)
        return (tag, s_sems, r_sems, lands), token

    def gather_done(state, after, kinds):
        tag, s_sems, r_sems, lands = state
        lands = _split_wait(f"gather_pass_wait_{tag}", lands, s_sems, r_sems, after, _plan_gather_pass)
        named = {KINDS[j][0]: a.reshape(N_DEV * KINDS[j][1], KINDS[j][2]) for a, j in zip(lands, kinds)}
        return named, list(lands[len(kinds):])

    all_kinds, mixer_kinds, ffn_kinds = tuple(range(len(KINDS))), (0, 3, 4, 5), (1, 2)
    no_hooks = {name: (lambda value: no_tie) for name in ("early", "pre_ffn", "mid", "late")}
    state, token = gather_chips("0m", 0, mixer_kinds, None, extra=[jnp.pad(conv_w.reshape(6, 128), ((0, 2), (0, 0)))])
    mats = _ssm_mats(dict(sp, log_dt=sp["log_dt"] + token[0, 0]))
    tabs = _rope_tabs()
    early_work = list(mats.values()) + list(tabs) + [a for name in dense for a in small_wmv[name]]
    state, _ = gather_pass(state, early_work)
    ffn_state, tie = gather_chips("0f", 0, ffn_kinds, state[3][0])
    w_next, (conv_all,) = gather_done(state, tabs[2], mixer_kinds)
    conv_full = conv_all[:, :6].reshape(N_DEV, DEPTH, 3, 64).transpose(1, 2, 0, 3).reshape(DEPTH, 3, WIDTH)
    rp["conv_w"] = jnp.pad(conv_full, ((0, 0), (0, 5), (0, 0)))

    act = x[0]
    weights, kept = [], []
    for i in range(DEPTH):
        w_i, hooks, held = w_next, dict(no_hooks), {}

        def early(value, ffn_state=ffn_state, held=held):
            held["ffn"], token = gather_pass(ffn_state, value)
            return token

        def pre_ffn(value, w_i=w_i, held=held):
            w_i.update(gather_done(held["ffn"], value, ffn_kinds)[0])

        hooks.update(early=early, pre_ffn=pre_ffn)
        if i + 1 < DEPTH:
            state, tie = gather_chips(f"{i + 1}m", i + 1, mixer_kinds, tie if i == 0 else w_i["win_t"])

            def mid(value, i=i, state=state, held=held):
                held["next"], token = gather_pass(state, value)
                held["next_ffn"], token = gather_chips(f"{i + 1}f", i + 1, ffn_kinds, token)
                return token

            hooks.update(mid=mid)
        act, k = _layer_fwd(act, i, w_i, rp, mats, tabs, tie, hooks)
        if i + 1 < DEPTH:
            w_next, _ = gather_done(held["next"], act, mixer_kinds)
            ffn_state, tie = held["next_ffn"], no_tie
        weights.append(w_i)
        kept.append(k)
    loss_row, dx, d_norm_final = _loss_head(act, norm_final[None], loss_target[0])

    landings = [lax.empty((DEPTH, 4, r, c), BF16) for _, r, c in KINDS]
    landings0 = [lax.empty((1, 4, r, c), BF16) for _, r, c in KINDS]

    def scatter_pair(tag, kinds, grads, after):
        parts4 = [g.reshape(4, 2, KINDS[j][1], KINDS[j][2]) for g, j in zip(grads, kinds)]
        zones = [lax.empty((4, KINDS[j][1], KINDS[j][2]), BF16) for j in kinds]
        s_sems, r_sems, arrays, token = _split_start(
            f"scatter_pair_start_{tag}", parts4 + zones, 4 * len(kinds), _plan_scatter_pair, after)
        return (tag, kinds, s_sems, r_sems, arrays), token

    def scatter_chips(state, lands, slot, after):
        tag, kinds, s_sems, r_sems, arrays = state
        arrays = _split_wait(f"scatter_pair_wait_{tag}", arrays, s_sems, r_sems, after, _plan_scatter_pair)
        n = len(kinds)
        sums, mine_lands = [], []
        for k, j in enumerate(kinds):
            name = KINDS[j][0]
            chip_sum, land = _pair_sum(arrays[k], arrays[n + k], lands[j], slot, core, rows_tile[name],
                                       f"pair_sum_{name}")
            sums.append(chip_sum)
            mine_lands.append(land)
        s_sems, r_sems, arrays, token = _split_start(
            f"scatter_chips_start_{tag}", sums + mine_lands, 3 * n, _plan_scatter_chips(slot))
        return (tag, kinds, slot, s_sems, r_sems, arrays), token

    def scatter_done(state, lands, after):
        tag, kinds, slot, s_sems, r_sems, arrays = state
        arrays = _split_wait(f"scatter_chips_wait_{tag}", arrays, s_sems, r_sems, after, _plan_scatter_chips(slot))
        lands = list(lands)
        for k, j in enumerate(kinds):
            lands[j] = arrays[len(kinds) + k]
        return lands

    sg = [None] * DEPTH
    pending, tie = None, no_tie
    for i in reversed(range(DEPTH)):
        hooks, held = dict(no_hooks), {}
        if pending is not None:
            def mid(value, i=i, pending=pending, held=held):
                held["chips"], token = scatter_chips(pending, landings, i + 1, value[2])
                if i == 0:
                    held["ffn_pair"], token = scatter_pair("0f", ffn_kinds, value[:2], token)
                return token

            hooks.update(mid=mid)
        if i == 0:
            def late(value, held=held):
                held["ffn_chips"], token = scatter_chips(held["ffn_pair"], landings0, 0, value)
                return token

            hooks.update(late=late)
        dx, grads, sg[i] = _layer_bwd(dx, kept[i], i, weights[i], rp, mats, tabs, tie, hooks)
        if pending is not None:
            landings = scatter_done(held["chips"], landings, dx)
        if i > 0:
            pending, tie = scatter_pair(str(i), all_kinds, grads, dx)
        else:
            pending, _ = scatter_pair("0m", mixer_kinds, [grads[j] for j in mixer_kinds], dx)

    sgrads, conv_grad = _replicated_grads(sg, sp)

    small_names = list(REPLICATED) + ["norm_final", "conv_w"]
    sgrads.update(norm_final=d_norm_final, conv_w=conv_grad)
    small_src = [sgrads[name].reshape(small_shapes[name]).astype(BF16) for name in small_names]
    small_src.append(jnp.broadcast_to(loss_row[:, :1], (8, 128)))
    last, tie = scatter_chips(pending, landings0, 0, small_src[0])
    s_sems, r_sems, arrays, tie = _split_start(
        "gather_small_chips_start", small_src + landing_zones(small_src), 4 * len(small_src), _plan_gather_chips, tie)
    small_state = ("small", s_sems, r_sems, arrays)

    big_out = []
    for j, (name, _, _) in enumerate(KINDS):
        big_out.append(_adamw(landings[j], travel["w"][j], travel["m"][j], travel["v"][j], rows_tile[name],
                              "adamw_late_" + name, groups=(1, DEPTH), tie=tie))
        tie = big_out[-1][3]
    landings0 = scatter_done(held["ffn_chips"], landings0, tie)
    landings0 = scatter_done(last, landings0, tie)
    small_state, _ = gather_pass(small_state, landings0[0])
    big_out = [_adamw(landings0[j], travel["w"][j], travel["m"][j], travel["v"][j], rows_tile[name],
                      "adamw_first_" + name, groups=(0, 1), fill=big_out[j]) for j, (name, _, _) in enumerate(KINDS)]
    big_res = [_native_layout([big_out[j][kind] for j in range(len(KINDS))]) for kind in range(4)]

    _, sparts = gather_done(small_state, big_out[-1][0], ())
    loss = jnp.sum(sparts[-1][:, 0, 0])
    sparts = dict(zip(small_names, sparts))
    sparts["conv_w"] = lax.dynamic_slice_in_dim(sparts["conv_w"], mine * 64, 64, axis=3)
    small_res = {}
    for name in small_names:
        res = _adamw_small(sparts[name], *small_wmv[name], "adamw_" + name)
        small_res[name] = [r.reshape(small_out_shapes[name]) for r in res]

    order = ["norm_mix", "w_in", "b_gate", "attn_sinks", "w_attn_o", "conv_w", "w_conv_o", "ssm_a_re", "ssm_a_im",
             "ssm_b_re", "ssm_b_im", "ssm_c_re", "ssm_c_im", "ssm_d", "ssm_log_dt", "w_ssm_glu", "w_ssm_o",
             "w_mix_o", "norm_ffn", "w_ffn_in", "w_ffn_out", "norm_final"]
    outs = [loss, dx[None]]
    for kind in range(4):
        for name in order:
            outs.append(big_res[kind][name] if name in big_res[kind] else small_res[name][kind])
    return tuple(outs)
```

```python
import math

import jax
import jax.numpy as jnp
from jax import lax
from jax.experimental import pallas as pl
from jax.experimental.pallas import tpu as pltpu

F32 = jnp.float32
BF16 = jnp.bfloat16

N_DEV = 8
DEPTH = 4
SEQ = 2048
D_MODEL = 1024
N_Q_HEADS = 8
HEAD_DIM = 64
ATTN_W = 512
KV_W = 128
BLOCK = 128
N_BLOCKS = SEQ // BLOCK
ROPE_THETA = 500000.0
ROT_DIM = 16
NEG_INF = -1e30
WIDTH = 512
SSM_GROUPS = 32
SSM_GROUP = 16
SSM_STATE = 64
CHUNK = 256
N_CHUNKS = SEQ // CHUNK
GATE_W = 3 * D_MODEL
IN_COLS = 5888
FFN_H = 2816
NORM_EPS = 1e-6
LR, B1, B2, ADAM_EPS, WD, STEP = 0.001, 0.9, 0.999, 1e-08, 0.01, 10

COL_Q, COL_KV, COL_CBX, COL_U, COL_G = 0, 512, 768, 2304, 2816
PIECE_W = (512, 256, 512, 512, 512, 512, 3072)
PIECE_OFF = tuple(sum(PIECE_W[:i]) for i in range(len(PIECE_W)))

KINDS = (("win_t", 736, 1024), ("wffn_t", 704, 1024), ("wout", 352, 1024), ("wmix", 128, 1024),
         ("branch_t", 128, 1536), ("wglu", 64, 512))

REPLICATED = ("norm_mix", "b_gate", "attn_sinks", "ssm_a_re", "ssm_a_im", "ssm_b_re", "ssm_b_im", "ssm_c_re",
              "ssm_c_im", "ssm_d", "ssm_log_dt", "norm_ffn")

VMEM_LIMIT = 56 * 1024 * 1024
NT = (((1,), (1,)), ((), ()))
TN = (((0,), (0,)), ((), ()))
MESH_ID = pl.DeviceIdType.MESH
ANY = pl.BlockSpec(memory_space=pl.ANY)
HBM = pl.BlockSpec(memory_space=pltpu.HBM)
SEM = pl.BlockSpec(memory_space=pltpu.SEMAPHORE)
EFFECT = pltpu.SideEffectType.DATAFLOW_SIDE_EFFECTING


def _cp(**kw):
    return pltpu.CompilerParams(vmem_limit_bytes=VMEM_LIMIT, **kw)


def _full(shape):
    return pl.BlockSpec(shape, lambda *_: (0,) * len(shape))


def _resident(shape):
    return pl.BlockSpec(shape, lambda *_: (0,) * len(shape), pipeline_mode=pl.Buffered(1))


def _mm_tn(a, b, *, tm, tn, name):
    k, m = a.shape
    n = b.shape[1]

    def body(a_ref, b_ref, o_ref):
        o_ref[...] = lax.dot_general(a_ref[...].astype(BF16), b_ref[...].astype(BF16), TN,
                                     preferred_element_type=F32).astype(BF16)

    return pl.pallas_call(
        body, grid=(m // tm, n // tn),
        in_specs=[pl.BlockSpec((k, tm), lambda i, j: (0, i)), pl.BlockSpec((k, tn), lambda i, j: (0, j))],
        out_specs=pl.BlockSpec((tm, tn), lambda i, j: (i, j)),
        out_shape=jax.ShapeDtypeStruct((m, n), BF16), compiler_params=_cp(), name=name)(a, b)


def _rms_rows(xv, g):
    r = lax.rsqrt(jnp.mean(xv * xv, axis=-1, keepdims=True) + NORM_EPS)
    return ((xv * r) * g).astype(BF16)


def _rms_mm_in(x, g, wt, tabs, cw, layer, tie):
    tt = 512
    widths = (3 * WIDTH, WIDTH, GATE_W)
    offs = (COL_CBX, COL_U, COL_G)

    def body(x_ref, g_ref, w_ref, tc_ref, ta_ref, tb_ref, cw_ref, tie_ref,
             q_ref, kv_ref, cbx_ref, u_ref, gl_ref, cv_ref, h_ref, tail_ref):
        @pl.when(pl.program_id(0) == 0)
        def _():
            tail_ref[...] = jnp.zeros_like(tail_ref)

        h = _rms_rows(x_ref[...], g_ref[...])
        h_ref[...] = h
        prod = lax.dot_general(h, w_ref[...], NT, preferred_element_type=F32)
        for ref, o, w in zip((cbx_ref, u_ref, gl_ref), offs, widths):
            ref[...] = prod[:, o:o + w]
        c, a, b = tc_ref[...], ta_ref[...], tb_ref[...]
        for j in range(ATTN_W // 128):
            q_ref[:, 128 * j:128 * (j + 1)] = _rope(prod[:, 128 * j:128 * (j + 1)], c, a, b) * (HEAD_DIM ** -0.5)
        kv_ref[:, :KV_W] = _rope(prod[:, COL_KV:COL_KV + KV_W], c, a, b)
        kv_ref[:, KV_W:] = prod[:, COL_KV + KV_W:COL_CBX]

        row = lax.broadcasted_iota(jnp.int32, (tt, 128), 0)
        for j in range(WIDTH // 128):
            cols = slice(128 * j, 128 * (j + 1))
            cb = prod[:, COL_CBX + 128 * j:COL_CBX + 128 * (j + 1)]
            z = prod[:, COL_CBX + WIDTH + 128 * j:COL_CBX + WIDTH + 128 * (j + 1)] \
                * prod[:, COL_CBX + 2 * WIDTH + 128 * j:COL_CBX + 2 * WIDTH + 128 * (j + 1)]
            before1, before2 = tail_ref[7:8, cols], tail_ref[6:7, cols]
            z1 = jnp.where(row == 0, before1, pltpu.roll(z, 1, axis=0))
            z2 = jnp.where(row == 0, before2, jnp.where(row == 1, before1, pltpu.roll(z, 2, axis=0)))
            s = cw_ref[0:1, cols] * z2 + cw_ref[1:2, cols] * z1 + cw_ref[2:3, cols] * z
            cv_ref[:, cols] = (cb * s).astype(BF16)
            tail_ref[:, cols] = z[tt - 8:, :]

    row_spec = lambda w: pl.BlockSpec((tt, w), lambda i: (i, 0))
    sds = jax.ShapeDtypeStruct
    return pl.pallas_call(
        body, grid=(SEQ // tt,),
        in_specs=[row_spec(D_MODEL), _full((1, D_MODEL)), _resident((IN_COLS, D_MODEL)),
                  row_spec(128), row_spec(128), row_spec(128),
                  pl.BlockSpec((None, 8, WIDTH), lambda i: (layer, 0, 0)), ANY],
        out_specs=[row_spec(ATTN_W), row_spec(2 * KV_W), row_spec(3 * WIDTH), row_spec(WIDTH), row_spec(GATE_W),
                   row_spec(WIDTH), row_spec(D_MODEL)],
        out_shape=[sds((SEQ, ATTN_W), F32), sds((SEQ, 2 * KV_W), F32), sds((SEQ, 3 * WIDTH), F32),
                   sds((SEQ, WIDTH), F32), sds((SEQ, GATE_W), F32), sds((SEQ, WIDTH), BF16),
                   sds((SEQ, D_MODEL), BF16)],
        scratch_shapes=[pltpu.VMEM((8, WIDTH), F32)], compiler_params=_cp(), name="rms_mm_in",
    )(x, g, wt, *tabs, cw, tie)


def _rms_mm_ffn(x, g, wt):
    tt = 256

    def body(x_ref, g_ref, w_ref, act_ref, up_ref, silu_ref, dsilu_ref, h_ref):
        h = _rms_rows(x_ref[...], g_ref[...])
        h_ref[...] = h
        prod = lax.dot_general(h, w_ref[...], NT, preferred_element_type=F32)
        gt, up = prod[:, :FFN_H], prod[:, FFN_H:]
        sg = jax.nn.sigmoid(gt)
        silu = gt * sg
        act_ref[...] = (silu * up).astype(BF16)
        up_ref[...] = up.astype(BF16)
        silu_ref[...] = silu.astype(BF16)
        dsilu_ref[...] = (sg + silu * (1.0 - sg)).astype(BF16)

    row = lambda w: pl.BlockSpec((tt, w), lambda i: (i, 0))
    return pl.pallas_call(
        body, grid=(SEQ // tt,), in_specs=[row(D_MODEL), _full((1, D_MODEL)), _resident((2 * FFN_H, D_MODEL))],
        out_specs=[row(FFN_H)] * 4 + [row(D_MODEL)],
        out_shape=[jax.ShapeDtypeStruct((SEQ, FFN_H), BF16)] * 4 + [jax.ShapeDtypeStruct((SEQ, D_MODEL), BF16)],
        compiler_params=_cp(), name="rms_mm_ffn")(x, g, wt)


def _mm_rmsbwd(pieces, wt, x, g, dres, name):
    tt = 512
    widths = [p.shape[1] for p in pieces]
    offs = [sum(widths[:i]) for i in range(len(widths))]
    n = len(pieces)

    def body(*refs):
        p_refs, (w_ref, x_ref, g_ref, r_ref, dx_ref, dg_ref) = refs[:n], refs[n:]

        @pl.when(pl.program_id(0) == 0)
        def _():
            dg_ref[...] = jnp.zeros_like(dg_ref)

        dh = jnp.zeros((tt, D_MODEL), F32)
        for p_ref, o, w in zip(p_refs, offs, widths):
            dh += jnp.dot(p_ref[...], w_ref[o:o + w, :], preferred_element_type=F32)
        xv = x_ref[...]
        r = lax.rsqrt(jnp.mean(xv * xv, axis=-1, keepdims=True) + NORM_EPS)
        xh = xv * r
        gy = dh * g_ref[...]
        dx_ref[...] = r_ref[...] + r * (gy - xh * jnp.mean(gy * xh, axis=-1, keepdims=True))
        dg_ref[...] += jnp.sum(dh * xh, axis=0, keepdims=True)

    row = lambda w: pl.BlockSpec((tt, w), lambda i: (i, 0))
    return pl.pallas_call(
        body, grid=(SEQ // tt,),
        in_specs=[row(w) for w in widths] + [_resident(wt.shape), row(D_MODEL), _full((1, D_MODEL)), row(D_MODEL)],
        out_specs=[row(D_MODEL), _full((1, D_MODEL))],
        out_shape=[jax.ShapeDtypeStruct((SEQ, D_MODEL), F32), jax.ShapeDtypeStruct((1, D_MODEL), F32)],
        compiler_params=_cp(), name=name)(*pieces, wt, x, g, dres)


def _tn_pieces(pieces, h):
    tk, tn = 512, 512
    nk = SEQ // tk
    n = len(pieces)

    def body(*refs):
        p_refs, (h_ref, o_ref, acc_ref) = refs[:n], refs[n:]
        kk = pl.program_id(1)

        @pl.when(kk == 0)
        def _():
            acc_ref[...] = jnp.zeros_like(acc_ref)

        hv = h_ref[...]
        for p_ref, o, w in zip(p_refs, PIECE_OFF, PIECE_W):
            acc_ref[o:o + w, :] += lax.dot_general(p_ref[...], hv, TN, preferred_element_type=F32)

        @pl.when(kk == nk - 1)
        def _():
            o_ref[...] = acc_ref[...].astype(BF16)

    return pl.pallas_call(
        body, grid=(D_MODEL // tn, nk),
        in_specs=[pl.BlockSpec((tk, w), lambda j, kk: (kk, 0)) for w in PIECE_W]
        + [pl.BlockSpec((tk, tn), lambda j, kk: (kk, j))],
        out_specs=pl.BlockSpec((IN_COLS, tn), lambda j, kk: (0, j)),
        out_shape=jax.ShapeDtypeStruct((IN_COLS, D_MODEL), BF16),
        scratch_shapes=[pltpu.VMEM((IN_COLS, tn), F32)], compiler_params=_cp(), name="tn_pieces")(*pieces, h)


def _tn_branches(dys, acts):
    tk = 512
    nk = SEQ // tk

    def body(d0, d1, d2, a0, a1, a2, o_ref, acc_ref):
        kk = pl.program_id(0)

        @pl.when(kk == 0)
        def _():
            acc_ref[...] = jnp.zeros_like(acc_ref)

        for j, (d, a) in enumerate(((d0, a0), (d1, a1), (d2, a2))):
            acc_ref[:, WIDTH * j:WIDTH * (j + 1)] += lax.dot_general(d[...], a[...], TN, preferred_element_type=F32)

        @pl.when(kk == nk - 1)
        def _():
            o_ref[...] = acc_ref[...].astype(BF16)

    row = lambda w: pl.BlockSpec((tk, w), lambda kk: (kk, 0))
    return pl.pallas_call(
        body, grid=(nk,), in_specs=[row(D_MODEL)] * 3 + [row(WIDTH)] * 3,
        out_specs=_full((D_MODEL, 3 * WIDTH)), out_shape=jax.ShapeDtypeStruct((D_MODEL, 3 * WIDTH), BF16),
        scratch_shapes=[pltpu.VMEM((D_MODEL, 3 * WIDTH), F32)], compiler_params=_cp(), name="tn_branches",
    )(*dys, *acts)


def _rope(t, c, a, b):
    return t * c + pltpu.roll(t, 120, axis=1) * a + pltpu.roll(t, 8, axis=1) * b


def _rope_t(d, c, a, b):
    return d * c + pltpu.roll(d * a, 8, axis=1) + pltpu.roll(d * b, 120, axis=1)


def _band_sides(band):
    left = lax.broadcasted_iota(jnp.int32, band.shape, 1) < HEAD_DIM
    h0 = jnp.where(left, band, 0.0)
    h1 = jnp.where(left, 0.0, band)
    r0 = pltpu.roll(h0, HEAD_DIM, axis=1)
    r1 = pltpu.roll(h1, HEAD_DIM, axis=1)
    return ((h0.astype(BF16), r0.astype(BF16)), (r1.astype(BF16), h1.astype(BF16)))


def _attn_mask(i):
    qi = lax.broadcasted_iota(jnp.int32, (2 * BLOCK, 2 * BLOCK), 0) % BLOCK
    kj = lax.broadcasted_iota(jnp.int32, (2 * BLOCK, 2 * BLOCK), 1)
    delta = qi + BLOCK - kj
    return (delta >= 0) & (delta < BLOCK) & ((kj >= BLOCK) | (i > 0))


def _attn_probs(s, ok, sink):
    s = jnp.where(ok, s, NEG_INF)
    m = jnp.maximum(jnp.max(s, axis=-1, keepdims=True), sink)
    p = jnp.exp(s - m)
    es = jnp.exp(sink - m)
    inv = 1.0 / (jnp.sum(p, axis=-1, keepdims=True) + es)
    return p * inv, es * inv


def _kv_group(qs, ks, vs, kh, sink_ref):
    q2 = jnp.concatenate([qs[2 * kh], qs[2 * kh + 1]], axis=0)
    kst = jnp.concatenate([ks[kh][0], ks[kh][1]], axis=0)
    vst = jnp.concatenate([vs[kh][0], vs[kh][1]], axis=0)
    top = lax.broadcasted_iota(jnp.int32, (2 * BLOCK, 1), 0) < BLOCK
    sinks = [jnp.where(top, sink_ref[0, 4 * kh + h], sink_ref[0, 4 * kh + 2 + h]) for h in range(2)]
    return q2, kst, vst, sinks


def _attn_load(q_ref, kvc_ref, kvp_ref, tc_ref, ta_ref, tb_ref, pc_ref, pa_ref, pb_ref):
    c, a, b = tc_ref[...], ta_ref[...], tb_ref[...]
    kband = jnp.concatenate([kvp_ref[:, :KV_W], kvc_ref[:, :KV_W]], axis=0)
    vband = jnp.concatenate([kvp_ref[:, KV_W:], kvc_ref[:, KV_W:]], axis=0)
    qs = [q_ref[:, 128 * j:128 * (j + 1)].astype(BF16) for j in range(4)]
    return qs, _band_sides(kband), _band_sides(vband), (c, a, b)


def _attn_specs(clamp):
    cur = lambda i: (clamp(i), 0)
    prev = lambda i: (jnp.maximum(clamp(i) - 1, 0), 0)
    return [
        pl.BlockSpec((BLOCK, ATTN_W), cur), pl.BlockSpec((BLOCK, 2 * KV_W), cur),
        pl.BlockSpec((BLOCK, 2 * KV_W), prev),
        pl.BlockSpec((BLOCK, 128), cur), pl.BlockSpec((BLOCK, 128), cur), pl.BlockSpec((BLOCK, 128), cur),
        pl.BlockSpec((BLOCK, 128), prev), pl.BlockSpec((BLOCK, 128), prev), pl.BlockSpec((BLOCK, 128), prev),
        pl.BlockSpec(memory_space=pltpu.SMEM),
    ]


def _attn_fwd(q, kv, tabs, sinks):
    tc, ta, tb = tabs

    def body(q_ref, kvc_ref, kvp_ref, tc_ref, ta_ref, tb_ref, pc_ref, pa_ref, pb_ref, sink_ref, o_ref):
        i = pl.program_id(0)
        qs, ks, vs, _ = _attn_load(q_ref, kvc_ref, kvp_ref, tc_ref, ta_ref, tb_ref, pc_ref, pa_ref, pb_ref)
        ok = _attn_mask(i)
        for kh in range(2):
            q2, kst, vst, sinks = _kv_group(qs, ks, vs, kh, sink_ref)
            s = lax.dot_general(q2, kst, NT, preferred_element_type=F32)
            pn = [_attn_probs(s[:, 2 * BLOCK * h:2 * BLOCK * (h + 1)], ok, sinks[h])[0].astype(BF16) for h in range(2)]
            o2 = jnp.dot(jnp.concatenate(pn, axis=1), vst, preferred_element_type=F32).astype(BF16)
            for r in range(2):
                j = 2 * kh + r
                o_ref[:, 128 * j:128 * (j + 1)] = o2[BLOCK * r:BLOCK * (r + 1)]

    return pl.pallas_call(
        body, grid=(N_BLOCKS,), in_specs=_attn_specs(lambda i: i),
        out_specs=pl.BlockSpec((BLOCK, ATTN_W), lambda i: (i, 0)),
        out_shape=jax.ShapeDtypeStruct((SEQ, ATTN_W), BF16), compiler_params=_cp(), name="attn_fwd",
    )(q, kv, kv, tc, ta, tb, tc, ta, tb, sinks)


def _attn_bwd(q, kv, tabs, sinks, do):
    tc, ta, tb = tabs
    last = N_BLOCKS - 1
    clamp = lambda i: jnp.minimum(i, last)

    def place(full, side, kh):
        left = lax.broadcasted_iota(jnp.int32, full.shape, 1) < HEAD_DIM
        valid = jnp.where(left, full, 0.0) if side == 0 else jnp.where(left, 0.0, full)
        return valid if side == kh else pltpu.roll(valid, HEAD_DIM, axis=1)

    def body(q_ref, kvc_ref, kvp_ref, tc_ref, ta_ref, tb_ref, pc_ref, pa_ref, pb_ref, sink_ref, do_ref,
             dq_ref, dkv_ref, ds_ref, carry_ref):
        i = pl.program_id(0)

        @pl.when(i == 0)
        def _():
            ds_ref[...] = jnp.zeros_like(ds_ref)
            carry_ref[...] = jnp.zeros_like(carry_ref)

        @pl.when(i > last)
        def _():
            dkv_ref[...] = carry_ref[...].astype(BF16)

        @pl.when(i <= last)
        def _():
            qs, ks, vs, (c, a, b) = _attn_load(q_ref, kvc_ref, kvp_ref, tc_ref, ta_ref, tb_ref,
                                               pc_ref, pa_ref, pb_ref)
            ok = _attn_mask(i)
            dk = jnp.zeros((2 * BLOCK, 128), F32)
            dv = jnp.zeros((2 * BLOCK, 128), F32)
            dsink = jnp.zeros((1, 128), F32)
            lane = lax.broadcasted_iota(jnp.int32, (1, 128), 1)
            for kh in range(2):
                q2, kst, vst, sinks = _kv_group(qs, ks, vs, kh, sink_ref)
                do2 = jnp.concatenate([do_ref[:, 128 * (2 * kh + r):128 * (2 * kh + r + 1)] for r in range(2)],
                                      axis=0).astype(BF16)
                s = lax.dot_general(q2, kst, NT, preferred_element_type=F32)
                dp = lax.dot_general(do2, vst, NT, preferred_element_type=F32)
                pns, dss = [], []
                for h in range(2):
                    cols = slice(2 * BLOCK * h, 2 * BLOCK * (h + 1))
                    pn, ps = _attn_probs(s[:, cols], ok, sinks[h])
                    dr = jnp.sum(pn * dp[:, cols], axis=-1, keepdims=True)
                    pns.append(pn.astype(BF16))
                    dss.append((pn * (dp[:, cols] - dr)).astype(BF16))
                    for r in range(2):
                        part = -jnp.sum((ps * dr)[BLOCK * r:BLOCK * (r + 1)])
                        dsink += jnp.where(lane == 4 * kh + 2 * r + h, part, 0.0)
                ds2, pn2 = jnp.concatenate(dss, axis=1), jnp.concatenate(pns, axis=1)
                dq2 = jnp.dot(ds2, kst, preferred_element_type=F32) * (HEAD_DIM ** -0.5)
                dk2 = lax.dot_general(ds2, q2, TN, preferred_element_type=F32)
                dv2 = lax.dot_general(pn2, do2, TN, preferred_element_type=F32)
                for h in range(2):
                    dk += place(dk2[2 * BLOCK * h:2 * BLOCK * (h + 1)], h, kh)
                    dv += place(dv2[2 * BLOCK * h:2 * BLOCK * (h + 1)], h, kh)
                for r in range(2):
                    j = 2 * kh + r
                    dq_ref[:, 128 * j:128 * (j + 1)] = _rope_t(dq2[BLOCK * r:BLOCK * (r + 1)], c, a, b).astype(BF16)
            ds_ref[...] += dsink
            dk_prev = _rope_t(dk[:BLOCK], pc_ref[...], pa_ref[...], pb_ref[...])
            dk_cur = _rope_t(dk[BLOCK:], c, a, b)
            prev = jnp.concatenate([dk_prev, dv[:BLOCK]], axis=1)
            dkv_ref[...] = (carry_ref[...] + prev).astype(BF16)
            carry_ref[...] = jnp.concatenate([dk_cur, dv[BLOCK:]], axis=1)

    return pl.pallas_call(
        body, grid=(N_BLOCKS + 1,),
        in_specs=_attn_specs(clamp) + [pl.BlockSpec((BLOCK, ATTN_W), lambda i: (clamp(i), 0))],
        out_specs=[pl.BlockSpec((BLOCK, ATTN_W), lambda i: (clamp(i), 0)),
                   pl.BlockSpec((BLOCK, 2 * KV_W), lambda i: (jnp.maximum(i - 1, 0), 0)),
                   pl.BlockSpec((1, 128), lambda i: (0, 0))],
        out_shape=[jax.ShapeDtypeStruct((SEQ, ATTN_W), BF16), jax.ShapeDtypeStruct((SEQ, 2 * KV_W), BF16),
                   jax.ShapeDtypeStruct((1, 128), F32)],
        scratch_shapes=[pltpu.VMEM((BLOCK, 2 * KV_W), F32)], compiler_params=_cp(), name="attn_bwd",
    )(q, kv, kv, tc, ta, tb, tc, ta, tb, sinks, do)


def _shift_down(z, k):
    row = lax.broadcasted_iota(jnp.int32, z.shape, 0)
    return jnp.where(row < k, 0.0, pltpu.roll(z, k, axis=0))


def _shift_up(z, k):
    n = z.shape[0]
    row = lax.broadcasted_iota(jnp.int32, z.shape, 0)
    return jnp.where(row >= n - k, 0.0, pltpu.roll(z, n - k, axis=0))


def _conv_specs():
    nb = WIDTH // 128
    return [pl.BlockSpec((SEQ, 128), lambda j: (0, j)), pl.BlockSpec((SEQ, 128), lambda j: (0, nb + j)),
            pl.BlockSpec((SEQ, 128), lambda j: (0, 2 * nb + j)), pl.BlockSpec((None, 8, 128), lambda j: (0, 0, j))]


def _conv_bwd(cbx, cw, layer, dout, tie):
    def body(cb_ref, cc_ref, cx_ref, w_ref, do_ref, tie_ref, dcb_ref, dcc_ref, dcx_ref, dw_ref):
        cc, cx = cc_ref[...], cx_ref[...]
        z = cc * cx
        z1, z2 = _shift_down(z, 1), _shift_down(z, 2)
        w0, w1, w2 = w_ref[0:1, :], w_ref[1:2, :], w_ref[2:3, :]
        dout = do_ref[...]
        ds = dout * cb_ref[...]
        dcb_ref[...] = (dout * (w0 * z2 + w1 * z1 + w2 * z)).astype(BF16)
        dz = w2 * ds + w1 * _shift_up(ds, 1) + w0 * _shift_up(ds, 2)
        dcc_ref[...] = (dz * cx).astype(BF16)
        dcx_ref[...] = (dz * cc).astype(BF16)
        rows = [jnp.sum(ds * zz, axis=0, keepdims=True) for zz in (z2, z1, z)]
        dw_ref[...] = jnp.concatenate(rows + [jnp.zeros((5, 128), F32)], axis=0)

    col = lambda j: (0, j)
    specs = _conv_specs()
    specs[3] = pl.BlockSpec((None, 8, 128), lambda j: (layer, 0, j))
    return pl.pallas_call(
        body, grid=(WIDTH // 128,), in_specs=specs + [pl.BlockSpec((SEQ, 128), col), ANY],
        out_specs=[pl.BlockSpec((SEQ, 128), col), pl.BlockSpec((SEQ, 128), col), pl.BlockSpec((SEQ, 128), col),
                   pl.BlockSpec((8, 128), col)],
        out_shape=[jax.ShapeDtypeStruct((SEQ, WIDTH), BF16)] * 3 + [jax.ShapeDtypeStruct((8, WIDTH), F32)],
        compiler_params=_cp(), name="conv_bwd",
    )(cbx, cbx, cbx, cw, dout, tie)


def _ssm_prep_math(a_re, a_im, log_dt, bt_re, bt_im):
    dt = jnp.exp(log_dt)
    er = jnp.exp(a_re * dt)
    lr = er * jnp.cos(a_im * dt)
    li = er * jnp.sin(a_im * dt)
    n2 = a_re * a_re + a_im * a_im
    cr = ((lr - 1.0) * a_re + li * a_im) / n2
    ci = (li * a_re - (lr - 1.0) * a_im) / n2
    cr3, ci3 = cr[:, None, :], ci[:, None, :]
    return lr, li, cr3 * bt_re - ci3 * bt_im, cr3 * bt_im + ci3 * bt_re


_GS = (SSM_GROUPS, SSM_STATE)
_GHS = (SSM_GROUPS, SSM_GROUP, SSM_STATE)


def _layered(shape):
    return pl.BlockSpec((None,) + shape, lambda l: (l,) + (0,) * len(shape))


def _ssm_prep(a_re, a_im, log_dt, bt_re, bt_im):
    def body(ar, ai, ld, br, bi, o0, o1, o2, o3):
        outs = _ssm_prep_math(ar[...], ai[...], ld[...], br[...], bi[...])
        for o, v in zip((o0, o1, o2, o3), outs):
            o[...] = v

    shapes = [_GS, _GS, _GHS, _GHS]
    return pl.pallas_call(
        body, grid=(DEPTH,), in_specs=[_layered(s) for s in (_GS, _GS, (SSM_GROUPS, 1), _GHS, _GHS)],
        out_specs=[_layered(s) for s in shapes],
        out_shape=[jax.ShapeDtypeStruct((DEPTH,) + s, F32) for s in shapes],
        name="ssm_prep")(a_re, a_im, log_dt, bt_re, bt_im)


def _ssm_prep_bwd(a_re, a_im, log_dt, bt_re, bt_im, cots):
    def body(ar, ai, ld, br, bi, c0, c1, c2, c3, o0, o1, o2, o3, o4):
        _, vjp = jax.vjp(_ssm_prep_math, ar[...], ai[...], ld[...], br[...], bi[...])
        for o, v in zip((o0, o1, o2, o3, o4), vjp((c0[...], c1[...], c2[...], c3[...]))):
            o[...] = v

    ins = (_GS, _GS, (SSM_GROUPS, 1), _GHS, _GHS)
    return pl.pallas_call(
        body, grid=(DEPTH,), in_specs=[_layered(s) for s in ins + (_GS, _GS, _GHS, _GHS)],
        out_specs=[_layered(s) for s in ins],
        out_shape=[jax.ShapeDtypeStruct((DEPTH,) + s, F32) for s in ins],
        name="ssm_prep_bwd")(a_re, a_im, log_dt, bt_re, bt_im, *cots)


LANES_G = 512
N_LANE_GROUPS = SSM_GROUPS * SSM_STATE // LANES_G


def _scan_in_place(xr_ref, xi_ref, ar, ai, reverse):
    shape = (N_CHUNKS, xr_ref.shape[1])
    ar, ai = jnp.broadcast_to(ar, shape), jnp.broadcast_to(ai, shape)

    def rows(tau):
        t = (CHUNK - 1 - tau) if reverse else tau
        return pl.ds(pl.multiple_of(t * N_CHUNKS, N_CHUNKS), N_CHUNKS)

    def step(tau, carry):
        sr, si = carry
        return ar * sr - ai * si + xr_ref[rows(tau), :], ar * si + ai * sr + xi_ref[rows(tau), :]

    zero = jnp.zeros(shape, F32)
    er, ei = lax.fori_loop(0, CHUNK, step, (zero, zero), unroll=8)
    qr, qi = ar, ai
    for _ in range(8):
        qr, qi = qr * qr - qi * qi, 2.0 * qr * qi
    shift = _shift_up if reverse else _shift_down
    for k in (1, 2, 4):
        sr, si = shift(er, k), shift(ei, k)
        er, ei = er + qr * sr - qi * si, ei + qr * si + qi * sr
        qr, qi = qr * qr - qi * qi, 2.0 * qr * qi
    start = (shift(er, 1), shift(ei, 1))

    def write(tau, carry):
        sr, si = step(tau, carry)
        xr_ref[rows(tau), :] = sr
        xi_ref[rows(tau), :] = si
        return sr, si

    return write, start


def _ssm_specs(layer):
    col = lambda w: pl.BlockSpec((SEQ, w), lambda g: (0, g))
    diag = pl.BlockSpec((None, None, 128, LANES_G), lambda g: (layer, g, 0, 0))
    vec = pl.BlockSpec((None, 1, LANES_G), lambda g: (layer, 0, g))
    return col, diag, vec


def _to_scan_order(src_ref, dst_ref):
    def move(tau, _):
        dst_ref[pl.ds(pl.multiple_of(tau * N_CHUNKS, N_CHUNKS), N_CHUNKS), :] = src_ref[pl.ds(tau, N_CHUNKS, stride=CHUNK), :]
        return 0

    lax.fori_loop(0, CHUNK, move, 0, unroll=8)


def _to_time_order(src_ref, dst_ref, dtype):
    for j in range(N_CHUNKS):
        dst_ref[pl.ds(j * CHUNK, CHUNK), :] = src_ref[pl.ds(j, CHUNK, stride=N_CHUNKS), :].astype(dtype)


def _ssm_fwd(u, mats, layer, d):
    def body(u_ref, d_ref, br_ref, bi_ref, cr_ref, ci_ref, ar_ref, ai_ref, xr_ref, xi_ref, y_ref, us_ref):
        _to_scan_order(u_ref, us_ref)
        uv = us_ref[...].astype(BF16)
        xr_ref[...] = jnp.dot(uv, br_ref[...], preferred_element_type=F32)
        xi_ref[...] = jnp.dot(uv, bi_ref[...], preferred_element_type=F32)
        write, start = _scan_in_place(xr_ref, xi_ref, ar_ref[...], ai_ref[...], False)
        lax.fori_loop(0, CHUNK, write, start, unroll=8)
        y = lax.dot_general(xr_ref[...].astype(BF16), cr_ref[...], NT, preferred_element_type=F32)
        y += lax.dot_general(xi_ref[...].astype(BF16), ci_ref[...], NT, preferred_element_type=F32)
        us_ref[...] = y + d_ref[...] * us_ref[...]
        _to_time_order(us_ref, y_ref, F32)

    col, diag, vec = _ssm_specs(layer)
    return pl.pallas_call(
        body, grid=(N_LANE_GROUPS,),
        in_specs=[col(128), pl.BlockSpec((None, 1, 128), lambda g: (layer, 0, g)),
                  diag, diag, diag, diag, vec, vec],
        out_specs=[col(LANES_G), col(LANES_G), col(128)],
        out_shape=[jax.ShapeDtypeStruct((SEQ, SSM_GROUPS * SSM_STATE), F32)] * 2
        + [jax.ShapeDtypeStruct((SEQ, WIDTH), F32)],
        scratch_shapes=[pltpu.VMEM((SEQ, 128), F32)], compiler_params=_cp(), name="ssm_fwd",
    )(u, d, mats["b_re"], mats["b_im"], mats["c_re"], mats["c_im_neg"], mats["a_re"], mats["a_im"])


def _ssm_bwd(dy, x_re, x_im, u, mats, layer, d):
    def body(dyt_ref, ut_ref, d_ref, xr_ref, xi_ref, br_ref, bi_ref, cr_ref, ci_ref, ar_ref, ai_ref,
             du_ref, dar_ref, dai_ref, dbr_ref, dbi_ref, dcr_ref, dci_ref, lr_ref, li_ref, dys_ref, u_ref):
        _to_scan_order(dyt_ref, dys_ref)
        _to_scan_order(ut_ref, u_ref)
        dy = dys_ref[...].astype(BF16)
        lr_ref[...] = jnp.dot(dy, cr_ref[...], preferred_element_type=F32)
        li_ref[...] = jnp.dot(dy, ci_ref[...], preferred_element_type=F32)
        write, start = _scan_in_place(lr_ref, li_ref, ar_ref[...], -ai_ref[...], True)

        def rows(t):
            return pl.ds(pl.multiple_of(t * N_CHUNKS, N_CHUNKS), N_CHUNKS)

        def grad(acc, lam, xpr, xpi):
            return acc[0] + xpr * lam[0] + xpi * lam[1], acc[1] + xpr * lam[1] - xpi * lam[0]

        def down(tau, carry):
            lam = write(tau, carry[0])
            t = CHUNK - 2 - tau
            return lam, grad(carry[1], lam, xr_ref[rows(t), :], xi_ref[rows(t), :])

        zero = jnp.zeros((N_CHUNKS, LANES_G), F32)
        lam, acc = lax.fori_loop(0, CHUNK - 1, down, (start, (zero, zero)), unroll=5)
        lam = write(CHUNK - 1, lam)
        last = rows(CHUNK - 1)
        acc = grad(acc, lam, _shift_down(xr_ref[last, :], 1), _shift_down(xi_ref[last, :], 1))
        dar_ref[...] = jnp.sum(acc[0], axis=0, keepdims=True)
        dai_ref[...] = jnp.sum(acc[1], axis=0, keepdims=True)

        l_re, l_im = lr_ref[...].astype(BF16), li_ref[...].astype(BF16)
        du = lax.dot_general(l_re, br_ref[...], NT, preferred_element_type=F32)
        du += lax.dot_general(l_im, bi_ref[...], NT, preferred_element_type=F32)
        dys_ref[...] = du + dys_ref[...] * d_ref[...]
        _to_time_order(dys_ref, du_ref, BF16)
        uv = u_ref[...].astype(BF16)
        dbr_ref[...] = lax.dot_general(uv, l_re, TN, preferred_element_type=F32)
        dbi_ref[...] = lax.dot_general(uv, l_im, TN, preferred_element_type=F32)
        dcr_ref[...] = lax.dot_general(dy, xr_ref[...].astype(BF16), TN, preferred_element_type=F32)
        dci_ref[...] = lax.dot_general(dy, xi_ref[...].astype(BF16), TN, preferred_element_type=F32)

    col, diag, vec = _ssm_specs(layer)
    out_vec = pl.BlockSpec((1, LANES_G), lambda g: (0, g))
    out_blk = pl.BlockSpec((None, 128, LANES_G), lambda g: (g, 0, 0))
    sds = jax.ShapeDtypeStruct
    return pl.pallas_call(
        body, grid=(N_LANE_GROUPS,),
        in_specs=[col(128), col(128), pl.BlockSpec((None, 1, 128), lambda g: (layer, 0, g)),
                  col(LANES_G), col(LANES_G), diag, diag, diag, diag, vec, vec],
        out_specs=[col(128), out_vec, out_vec, out_blk, out_blk, out_blk, out_blk],
        out_shape=[sds((SEQ, WIDTH), BF16)] + [sds((1, SSM_GROUPS * SSM_STATE), F32)] * 2
        + [sds((N_LANE_GROUPS, 128, LANES_G), F32)] * 4,
        scratch_shapes=[pltpu.VMEM((SEQ, LANES_G), F32)] * 2 + [pltpu.VMEM((SEQ, 128), F32)] * 2,
        compiler_params=_cp(), name="ssm_bwd",
    )(dy, u, d, x_re, x_im, mats["b_re"], mats["b_im"], mats["c_re"], mats["c_im_neg"],
      mats["a_re"], mats["a_im"])


_GELU_C = math.sqrt(2.0 / math.pi)


def _gelu(y):
    return 0.5 * y * (1.0 + jnp.tanh(_GELU_C * (y + 0.044715 * (y * y * y))))


def _glu_fwd(y, wglu):
    tt = 512

    def body(y_ref, w_ref, z_ref):
        ys = _gelu(y_ref[...])
        a = jnp.dot(ys.astype(BF16), w_ref[...], preferred_element_type=F32)
        z_ref[...] = (ys * jax.nn.sigmoid(a)).astype(BF16)

    blk = pl.BlockSpec((tt, WIDTH), lambda i: (i, 0))
    return pl.pallas_call(body, grid=(SEQ // tt,), in_specs=[blk, _full((WIDTH, WIDTH))], out_specs=blk,
                          out_shape=jax.ShapeDtypeStruct((SEQ, WIDTH), BF16), compiler_params=_cp(),
                          name="glu_fwd")(y, wglu)


def _glu_bwd(y, wglu, dz, u):
    tt = 512

    def body(y_ref, w_ref, dz_ref, u_ref, dy_ref, ys_ref, da_ref, dd_ref):
        @pl.when(pl.program_id(0) == 0)
        def _():
            dd_ref[...] = jnp.zeros_like(dd_ref)

        yv = y_ref[...]
        t = jnp.tanh(_GELU_C * (yv + 0.044715 * (yv * yv * yv)))
        ys = 0.5 * yv * (1.0 + t)
        ysb = ys.astype(BF16)
        sg = jax.nn.sigmoid(jnp.dot(ysb, w_ref[...], preferred_element_type=F32))
        dz = dz_ref[...].astype(F32)
        da = (dz * ys * sg * (1.0 - sg)).astype(BF16)
        dys = dz * sg + lax.dot_general(da, w_ref[...], NT, preferred_element_type=F32)
        dy = dys * (0.5 * (1.0 + t) + 0.5 * yv * (1.0 - t * t) * _GELU_C * (1.0 + 3 * 0.044715 * (yv * yv)))
        dy_ref[...] = dy
        ys_ref[...] = ysb
        da_ref[...] = da
        dd_ref[...] += jnp.sum(dy * u_ref[...], axis=0, keepdims=True)

    blk = pl.BlockSpec((tt, WIDTH), lambda i: (i, 0))
    return pl.pallas_call(
        body, grid=(SEQ // tt,), in_specs=[blk, _full((WIDTH, WIDTH)), blk, blk],
        out_specs=[blk, blk, blk, _full((1, WIDTH))],
        out_shape=[jax.ShapeDtypeStruct((SEQ, WIDTH), F32)] + [jax.ShapeDtypeStruct((SEQ, WIDTH), BF16)] * 2
        + [jax.ShapeDtypeStruct((1, WIDTH), F32)],
        compiler_params=_cp(), name="glu_bwd")(y, wglu, dz, u)


def _mix_specs(tt, layer):
    row = lambda w: pl.BlockSpec((tt, w), lambda i: (i, 0))
    gate = lambda j: pl.BlockSpec((tt, D_MODEL), lambda i: (i, j))
    wo = lambda j: pl.BlockSpec((D_MODEL, WIDTH), lambda i: (0, j))
    return [row(D_MODEL), row(WIDTH), row(WIDTH), row(WIDTH), gate(0), gate(1), gate(2),
            pl.BlockSpec((None, 1, GATE_W), lambda i: (layer, 0, 0)), wo(0), wo(1), wo(2),
            _full((D_MODEL, D_MODEL))]


def _mix_branches(o_ref, c_ref, z_ref, g_refs, b_ref, wa_ref, wc_ref, ws_ref):
    ys = [lax.dot_general(r[...], w[...], NT, preferred_element_type=F32)
          for r, w in ((o_ref, wa_ref), (c_ref, wc_ref), (z_ref, ws_ref))]
    gates = [jax.nn.sigmoid(g_refs[j][...] + b_ref[:, D_MODEL * j:D_MODEL * (j + 1)]) for j in range(3)]
    return ys, gates


def _mix_fwd(x, o, cv, z, glog, b_gate, layer, wbt, wmix, tie):
    tt = 256

    def body(x_ref, o_ref, c_ref, z_ref, g0, g1, g2, b_ref, wa_ref, wc_ref, ws_ref, wm_ref, tie_ref, x1_ref):
        ys, gates = _mix_branches(o_ref, c_ref, z_ref, (g0, g1, g2), b_ref, wa_ref, wc_ref, ws_ref)
        merged = gates[0] * ys[0] + gates[1] * ys[1] + gates[2] * ys[2]
        x1_ref[...] = x_ref[...] + jnp.dot(merged.astype(BF16), wm_ref[...], preferred_element_type=F32)

    return pl.pallas_call(
        body, grid=(SEQ // tt,), in_specs=_mix_specs(tt, layer) + [ANY],
        out_specs=pl.BlockSpec((tt, D_MODEL), lambda i: (i, 0)),
        out_shape=jax.ShapeDtypeStruct((SEQ, D_MODEL), F32), compiler_params=_cp(), name="mix_fwd",
    )(x, o, cv, z, glog, glog, glog, b_gate, wbt, wbt, wbt, wmix, tie)


def _mix_bwd(dx1, o, cv, z, glog, b_gate, layer, wbt, wmix, tie):
    tt = 256

    def body(dx_ref, o_ref, c_ref, z_ref, g0, g1, g2, b_ref, wa_ref, wc_ref, ws_ref, wm_ref, tie_ref,
             mg_ref, dya_ref, dyc_ref, dys_ref, do_ref, dc_ref, dz_ref, dgl_ref, db_ref):
        @pl.when(pl.program_id(0) == 0)
        def _():
            db_ref[...] = jnp.zeros_like(db_ref)

        ys, gates = _mix_branches(o_ref, c_ref, z_ref, (g0, g1, g2), b_ref, wa_ref, wc_ref, ws_ref)
        mg_ref[...] = (gates[0] * ys[0] + gates[1] * ys[1] + gates[2] * ys[2]).astype(BF16)
        dm = lax.dot_general(dx_ref[...].astype(BF16), wm_ref[...], NT, preferred_element_type=F32)
        for j, (dy_ref, w_ref, d_ref) in enumerate(((dya_ref, wa_ref, do_ref), (dyc_ref, wc_ref, dc_ref),
                                                    (dys_ref, ws_ref, dz_ref))):
            dy = (dm * gates[j]).astype(BF16)
            dy_ref[...] = dy
            d_ref[...] = jnp.dot(dy, w_ref[...], preferred_element_type=F32)
            dgl = dm * ys[j] * gates[j] * (1.0 - gates[j])
            dgl_ref[:, D_MODEL * j:D_MODEL * (j + 1)] = dgl.astype(BF16)
            db_ref[:, D_MODEL * j:D_MODEL * (j + 1)] += jnp.sum(dgl, axis=0, keepdims=True)

    row = lambda w: pl.BlockSpec((tt, w), lambda i: (i, 0))
    sds = jax.ShapeDtypeStruct
    return pl.pallas_call(
        body, grid=(SEQ // tt,), in_specs=_mix_specs(tt, layer) + [ANY],
        out_specs=[row(D_MODEL)] * 4 + [row(WIDTH)] * 3 + [row(GATE_W), _full((1, GATE_W))],
        out_shape=[sds((SEQ, D_MODEL), BF16)] * 4 + [sds((SEQ, WIDTH), F32)] * 3
        + [sds((SEQ, GATE_W), BF16), sds((1, GATE_W), F32)],
        compiler_params=_cp(), name="mix_bwd",
    )(dx1, o, cv, z, glog, glog, glog, b_gate, wbt, wbt, wbt, wmix, tie)


def _ffn_out_fwd(x1, act, wout, tie):
    tt = 512

    def body(x_ref, a_ref, w_ref, tie_ref, o_ref):
        o_ref[...] = x_ref[...] + jnp.dot(a_ref[...], w_ref[...], preferred_element_type=F32)

    row = lambda w: pl.BlockSpec((tt, w), lambda i: (i, 0))
    return pl.pallas_call(
        body, grid=(SEQ // tt,), in_specs=[row(D_MODEL), row(FFN_H), _full((FFN_H, D_MODEL)), ANY],
        out_specs=row(D_MODEL), out_shape=jax.ShapeDtypeStruct((SEQ, D_MODEL), F32),
        compiler_params=_cp(), name="ffn_out_fwd")(x1, act, wout, tie)


def _ffn_out_bwd(dx2, up, silu, dsilu, wout, tie):
    tt = 256

    def body(dx_ref, up_ref, silu_ref, dsilu_ref, w_ref, tie_ref, dgu_ref):
        dact = lax.dot_general(dx_ref[...].astype(BF16), w_ref[...], NT, preferred_element_type=F32).astype(BF16)
        dgu_ref[:, :FFN_H] = dact * up_ref[...] * dsilu_ref[...]
        dgu_ref[:, FFN_H:] = dact * silu_ref[...]

    row = lambda w: pl.BlockSpec((tt, w), lambda i: (i, 0))
    return pl.pallas_call(
        body, grid=(SEQ // tt,),
        in_specs=[row(D_MODEL), row(FFN_H), row(FFN_H), row(FFN_H), _full((FFN_H, D_MODEL)), ANY],
        out_specs=row(2 * FFN_H), out_shape=jax.ShapeDtypeStruct((SEQ, 2 * FFN_H), BF16),
        compiler_params=_cp(), name="ffn_out_bwd")(dx2, up, silu, dsilu, wout, tie)


def _loss_head(x, g, target):
    tt = 256

    def body(x_ref, g_ref, t_ref, loss_ref, dx_ref, dg_ref):
        @pl.when(pl.program_id(0) == 0)
        def _():
            loss_ref[...] = jnp.zeros_like(loss_ref)
            dg_ref[...] = jnp.zeros_like(dg_ref)

        xv = x_ref[...]
        r = lax.rsqrt(jnp.mean(xv * xv, axis=-1, keepdims=True) + NORM_EPS)
        xh = xv * r
        err = xh * g_ref[...] - t_ref[...]
        loss_ref[...] += 0.5 * jnp.sum(jnp.mean(err * err, axis=-1, keepdims=True))
        dy = err * (1.0 / D_MODEL)
        gy = dy * g_ref[...]
        dx_ref[...] = r * (gy - xh * jnp.mean(gy * xh, axis=-1, keepdims=True))
        dg_ref[...] += jnp.sum(dy * xh, axis=0, keepdims=True)

    row = pl.BlockSpec((tt, D_MODEL), lambda i: (i, 0))
    return pl.pallas_call(
        body, grid=(SEQ // tt,), in_specs=[row, _full((1, D_MODEL)), row],
        out_specs=[_full((1, 128)), row, _full((1, D_MODEL))],
        out_shape=[jax.ShapeDtypeStruct((1, 128), F32), jax.ShapeDtypeStruct((SEQ, D_MODEL), F32),
                   jax.ShapeDtypeStruct((1, D_MODEL), F32)],
        compiler_params=_cp(), name="loss_head")(x, g, target)


def _adam_math(g, w, m, v):
    nm = B1 * m + (1.0 - B1) * g
    nv = B2 * v + (1.0 - B2) * (g * g)
    m_hat = nm / (1.0 - B1 ** STEP)
    v_hat = nv / (1.0 - B2 ** STEP)
    return -LR * (m_hat / (jnp.sqrt(v_hat) + ADAM_EPS) + WD * w), nm, nv


def _adamw_small(parts, w, m, v, name):
    def body(p_ref, w_ref, m_ref, v_ref, g_ref, d_ref, nm_ref, nv_ref):
        g = p_ref[0].astype(F32)
        for k in range(1, N_DEV):
            g = g + p_ref[k].astype(F32)
        g_ref[...] = g
        d_ref[...], nm_ref[...], nv_ref[...] = _adam_math(g, w_ref[...], m_ref[...], v_ref[...])

    out_shape = [jax.ShapeDtypeStruct(w.shape, F32)] * 4
    if w.ndim < 3:
        return pl.pallas_call(body, out_shape=out_shape, name=name)(parts, w, m, v)
    rest = w.shape[1:]
    zeros = (0,) * len(rest)
    blk = pl.BlockSpec((None,) + rest, lambda l: (l,) + zeros)
    return pl.pallas_call(
        body, grid=(w.shape[0],),
        in_specs=[pl.BlockSpec((N_DEV, None) + rest, lambda l: (0, l) + zeros), blk, blk, blk],
        out_specs=[blk] * 4, out_shape=out_shape, name=name)(parts, w, m, v)


def _adamw(parts, w, m, v, tr, name, groups=None, fill=None, tie=None):
    n_groups, rows, cols = w.shape
    n_parts = parts.shape[1]
    lo, hi = groups if groups is not None else (0, n_groups)

    def body(p_ref, w_ref, m_ref, v_ref, *rest):
        g_ref, d_ref, nm_ref, nv_ref = rest[-4:]
        g = p_ref[0].astype(F32)
        for k in range(1, n_parts):
            g = g + p_ref[k].astype(F32)
        nm = B1 * m_ref[...] + (1.0 - B1) * g
        nv = B2 * v_ref[...] + (1.0 - B2) * (g * g)
        m_hat = nm / (1.0 - B1 ** STEP)
        v_hat = nv / (1.0 - B2 ** STEP)
        g_ref[...] = g
        d_ref[...] = -LR * (m_hat / (jnp.sqrt(v_hat) + ADAM_EPS) + WD * w_ref[...])
        nm_ref[...] = nm
        nv_ref[...] = nv

    blk = pl.BlockSpec((None, tr, cols), lambda l, i: (l + lo, i, 0))
    p_lo = lo if parts.shape[0] == n_groups else 0
    extra = ([] if fill is None else list(fill)) + ([] if tie is None else [tie])
    return pl.pallas_call(
        body, grid=(hi - lo, rows // tr),
        in_specs=[pl.BlockSpec((None, n_parts, tr, cols), lambda l, i: (l + p_lo, 0, i, 0)), blk, blk, blk]
        + [ANY] * len(extra),
        out_specs=[blk] * 4, out_shape=[jax.ShapeDtypeStruct((n_groups, rows, cols), F32)] * 4,
        input_output_aliases={} if fill is None else {4 + j: j for j in range(4)},
        compiler_params=_cp(), name=name)(parts, w, m, v, *extra)


def _split_start(name, arrays, n_sems, plan, after=None):
    n = len(arrays)
    order = [] if after is None else [after]
    n_in = n + len(order)

    def body(*refs):
        ins, send_sems, recv_sems, token = refs[:n], refs[n_in], refs[n_in + 1], refs[-1]
        for src, dst, k, to in plan(ins)[0]:
            pltpu.make_async_remote_copy(src_ref=src, dst_ref=dst, send_sem=send_sems.at[k], recv_sem=recv_sems.at[k],
                                         device_id=to, device_id_type=MESH_ID).start()
        token[...] = jnp.zeros_like(token)

    outs = pl.pallas_call(
        body, name=name,
        out_shape=(pltpu.SemaphoreType.DMA((n_sems,)), pltpu.SemaphoreType.DMA((n_sems,)),
                   *[pltpu.HBM(a.shape, a.dtype) for a in arrays], jax.ShapeDtypeStruct((8, 128), F32)),
        in_specs=[HBM] * n + [ANY] * len(order),
        out_specs=(SEM, SEM, *[HBM] * n, pl.BlockSpec(memory_space=pltpu.VMEM)),
        input_output_aliases={i: 2 + i for i in range(n)},
        compiler_params=pltpu.CompilerParams(has_side_effects=EFFECT),
    )(*[pltpu.with_memory_space_constraint(a, pltpu.HBM) for a in arrays], *order)
    return outs[0], outs[1], list(outs[2:2 + n]), outs[-1]


def _split_wait(name, arrays, send_sems, recv_sems, after, plan):
    n = len(arrays)
    order = list(after) if isinstance(after, (list, tuple)) else [after]

    def body(*refs):
        ins, s_sems, r_sems = refs[:n], refs[n], refs[n + 1]
        sends, arrivals = plan(ins)
        x, y, c = lax.axis_index("x"), lax.axis_index("y"), lax.axis_index("c")
        for src, dst, k, to in sends:
            pltpu.make_async_remote_copy(src_ref=src, dst_ref=dst, send_sem=s_sems.at[k], recv_sem=r_sems.at[k],
                                         device_id=to, device_id_type=MESH_ID).wait_send()
        for dst, k in arrivals:
            pltpu.make_async_remote_copy(src_ref=dst, dst_ref=dst, send_sem=s_sems.at[k], recv_sem=r_sems.at[k],
                                         device_id=(x, y, c), device_id_type=MESH_ID).wait_recv()

    return pl.pallas_call(
        body, name=name, out_shape=[pltpu.HBM(a.shape, a.dtype) for a in arrays],
        in_specs=[HBM] * n + [SEM, SEM] + [ANY] * len(order), out_specs=[HBM] * n,
        input_output_aliases={i: i for i in range(n)},
        compiler_params=pltpu.CompilerParams(has_side_effects=EFFECT),
    )(*arrays, send_sems, recv_sems, *order)


def _chips():
    x, y, c = lax.axis_index("x"), lax.axis_index("y"), lax.axis_index("c")
    return x, y, c, [(1 - x, y), (x, 1 - y), (1 - x, 1 - y)]


def _plan_gather_chips(refs):
    x, y, c, chips = _chips()
    me = 4 * x + 2 * y + c
    n = len(refs) // 2
    sends, arrivals = [], []
    for i in range(n):
        src, land = refs[i], refs[n + i]
        sends.append((src, land.at[me], 4 * i, (x, y, 1 - c)))
        arrivals.append((land.at[4 * x + 2 * y + 1 - c], 4 * i))
        for j, (px, py) in enumerate(chips):
            sends.append((src, land.at[me], 4 * i + 1 + j, (px, py, c)))
            arrivals.append((land.at[4 * px + 2 * py + c], 4 * i + 1 + j))
    return sends, arrivals


def _plan_gather_pass(refs):
    x, y, c, chips = _chips()
    sends, arrivals = [], []
    for i in range(len(refs)):
        for j, (px, py) in enumerate(chips):
            slot = refs[i].at[4 * px + 2 * py + c]
            sends.append((slot, slot, 4 * i + j, (x, y, 1 - c)))
            arrivals.append((refs[i].at[4 * px + 2 * py + 1 - c], 4 * i + j))
        back = refs[i].at[4 * x + 2 * y + 1 - c]
        sends.append((back, back, 4 * i + 3, (x, y, 1 - c)))
        arrivals.append((refs[i].at[4 * x + 2 * y + c], 4 * i + 3))
    return sends, arrivals


def _plan_scatter_pair(refs):
    x, y, c = lax.axis_index("x"), lax.axis_index("y"), lax.axis_index("c")
    n = len(refs) // 2
    sends, arrivals = [], []
    for i in range(n):
        for q in range(4):
            sends.append((refs[i].at[q, 1 - c], refs[n + i].at[q], 4 * i + q, (x, y, 1 - c)))
            arrivals.append((refs[n + i].at[q], 4 * i + q))
    return sends, arrivals


def _plan_scatter_chips(layer):
    def plan(refs):
        x, y, c, chips = _chips()
        n = len(refs) // 2
        sends, arrivals = [], []
        for i in range(n):
            for j, (px, py) in enumerate(chips):
                sends.append((refs[i].at[2 * px + py], refs[n + i].at[layer, 2 * x + y], 3 * i + j, (px, py, c)))
                arrivals.append((refs[n + i].at[layer, 2 * px + py], 3 * i + j))
        return sends, arrivals

    return plan


def _pair_sum(parts4, from_pair, landing, layer, core, tr, name):
    _, _, rows, cols = parts4.shape

    def body(c_ref, p_ref, s_ref, l_ref, sum_ref, land_ref):
        v = (p_ref[...].astype(F32) + s_ref[...].astype(F32)).astype(BF16)
        sum_ref[...] = v
        land_ref[...] = v

    blk = pl.BlockSpec((None, tr, cols), lambda q, i, c_ref: (q, i, 0))
    return pl.pallas_call(
        body,
        grid_spec=pltpu.PrefetchScalarGridSpec(
            num_scalar_prefetch=1, grid=(4, rows // tr),
            in_specs=[pl.BlockSpec((None, None, tr, cols), lambda q, i, c_ref: (q, c_ref[0], i, 0)), blk, ANY],
            out_specs=[blk, pl.BlockSpec((None, None, tr, cols), lambda q, i, c_ref: (layer, q, i, 0))]),
        out_shape=[jax.ShapeDtypeStruct((4, rows, cols), BF16), jax.ShapeDtypeStruct(landing.shape, BF16)],
        input_output_aliases={3: 1}, compiler_params=_cp(), name=name,
    )(core, parts4, from_pair, landing)


def _travel_layout(t):
    tr = lambda a: jnp.swapaxes(a, 1, 2)
    branch = jnp.concatenate([tr(t["w_attn_o"]), tr(t["w_conv_o"]), tr(t["w_ssm_o"])], axis=2)
    return [tr(t["w_in"]), tr(t["w_ffn_in"]), t["w_ffn_out"], t["w_mix_o"], branch, t["w_ssm_glu"]]


def _native_layout(a):
    tr = lambda x: jnp.swapaxes(x, 1, 2)
    b = a[4]
    return {"w_in": tr(a[0]), "w_ffn_in": tr(a[1]), "w_ffn_out": a[2], "w_mix_o": a[3],
            "w_attn_o": tr(b[:, :, :WIDTH]), "w_conv_o": tr(b[:, :, WIDTH:2 * WIDTH]),
            "w_ssm_o": tr(b[:, :, 2 * WIDTH:]), "w_ssm_glu": a[5]}


def _embed(t):
    eye = jnp.eye(8, dtype=t.dtype)
    t = t.reshape(DEPTH, N_LANE_GROUPS, 8, SSM_GROUP, SSM_STATE)
    return (t[:, :, :, :, None, :] * eye[None, None, :, None, :, None]).reshape(DEPTH, N_LANE_GROUPS, 128, LANES_G)


def _diag_blocks(t):
    t = t.reshape(DEPTH, N_LANE_GROUPS, 8, SSM_GROUP, 8, SSM_STATE)
    return jnp.einsum("lgahap->lgahp", t).reshape(DEPTH, SSM_GROUPS, SSM_GROUP, SSM_STATE)


def _rope_tabs():
    pos = jnp.arange(SEQ, dtype=F32)
    inv_freq = ROPE_THETA ** (-jnp.arange(0, ROT_DIM, 2, dtype=F32) / ROT_DIM)
    ang = pos[:, None] * inv_freq[None, :]
    cos, sin = jnp.cos(ang), jnp.sin(ang)
    one, zero = jnp.ones((SEQ, HEAD_DIM - ROT_DIM), F32), jnp.zeros((SEQ, HEAD_DIM - ROT_DIM), F32)
    z8 = jnp.zeros((SEQ, 8), F32)
    head = lambda *p: jnp.tile(jnp.concatenate(p, axis=1), (1, 2))
    return head(cos, cos, one), head(-sin, z8, zero), head(z8, sin, zero)


def _ssm_mats(sp):
    lr, li, bbr, bbi = _ssm_prep(sp["a_re"], sp["a_im"], sp["log_dt"], sp["bt_re"], sp["bt_im"])
    lanes = SSM_GROUPS * SSM_STATE
    return {
        "a_re": lr.reshape(DEPTH, 1, lanes), "a_im": li.reshape(DEPTH, 1, lanes),
        "b_re": _embed(bbr).astype(BF16), "b_im": _embed(bbi).astype(BF16),
        "c_re": _embed(sp["c_re"]).astype(BF16), "c_im_neg": _embed(-sp["c_im"]).astype(BF16),
    }


def _layer_fwd(x, i, w, rp, mats, tabs, tie, hooks):
    q, kv, cbx, u, glog, cv, h = _rms_mm_in(x, rp["norm_mix"][i], w["win_t"], tabs, rp["conv_w"], i, tie)
    o = _attn_fwd(q, kv, tabs, rp["attn_sinks"][i])
    x_re, x_im, y = _ssm_fwd(u, mats, i, rp["ssm_d"])
    z = _glu_fwd(y, w["wglu"])
    x1 = _mix_fwd(x, o, cv, z, glog, rp["b_gate"], i, w["branch_t"], w["wmix"], hooks["early"](z))
    hooks["pre_ffn"](x1)
    act, up, silu, dsilu, h2 = _rms_mm_ffn(x1, rp["norm_ffn"][i], w["wffn_t"])
    x2 = _ffn_out_fwd(x1, act, w["wout"], hooks["mid"](h2))
    kept = dict(x=x, q=q, kv=kv, cbx=cbx, u=u, glog=glog, h=h, o=o, cv=cv, z=z, y=y,
                x_re=x_re, x_im=x_im, x1=x1, act=act, up=up, silu=silu, dsilu=dsilu, h2=h2)
    return x2, kept


def _layer_bwd(dx2, k, i, w, rp, mats, tabs, tie, hooks):
    dgu = _ffn_out_bwd(dx2, k["up"], k["silu"], k["dsilu"], w["wout"], tie)
    g_wout = _mm_tn(k["act"], dx2, tm=FFN_H // 2, tn=1024, name="mm_tn_ffn_out")
    g_wffn_t = _mm_tn(dgu, k["h2"], tm=FFN_H // 2, tn=1024, name="mm_tn_ffn_in")
    dx1, d_norm_ffn = _mm_rmsbwd([dgu], w["wffn_t"], k["x1"], rp["norm_ffn"][i], dx2, "mm_rmsbwd_ffn")

    mg, dya, dyc, dys, do, dcv, dz, dgl, db_gate = _mix_bwd(
        dx1, k["o"], k["cv"], k["z"], k["glog"], rp["b_gate"], i, w["branch_t"], w["wmix"],
        hooks["mid"]((g_wffn_t, g_wout, d_norm_ffn)))
    g_wmix = _mm_tn(mg, dx1, tm=1024, tn=512, name="mm_tn_mix")
    g_branch_t = _tn_branches((dya, dyc, dys), (k["o"], k["cv"], k["z"]))

    dy, ys16, da16, dd = _glu_bwd(k["y"], w["wglu"], dz, k["u"])
    g_wglu = _mm_tn(ys16, da16, tm=256, tn=512, name="mm_tn_glu")
    du, da_re, da_im, db_re, db_im, dc_re, dc_im = _ssm_bwd(dy, k["x_re"], k["x_im"], k["u"], mats, i, rp["ssm_d"])

    dcb, dcc, dcx, d_conv_w = _conv_bwd(k["cbx"], rp["conv_w"], i, dcv, hooks["late"](du))
    dq, dkv, d_sinks = _attn_bwd(k["q"], k["kv"], tabs, rp["attn_sinks"][i], do)

    pieces = [dq, dkv, dcb, dcc, dcx, du, dgl]
    g_win_t = _tn_pieces(pieces, k["h"])
    dx, d_norm_mix = _mm_rmsbwd(pieces, w["win_t"], k["x"], rp["norm_mix"][i], dx1, "mm_rmsbwd_in")

    grads = [g_win_t, g_wffn_t, g_wout, g_wmix, g_branch_t, g_wglu]
    small = dict(norm_mix=d_norm_mix, b_gate=db_gate, attn_sinks=d_sinks, ssm_d=dd, norm_ffn=d_norm_ffn,
                 conv_w=d_conv_w, da_re=da_re, da_im=da_im, db_re=db_re, db_im=db_im, dc_re=dc_re, dc_im=dc_im)
    return dx, grads, small


def _replicated_grads(sg, sp):
    stack = lambda name: jnp.stack([sg[i][name] for i in range(DEPTH)])
    cots = (stack("da_re").reshape(DEPTH, *_GS), stack("da_im").reshape(DEPTH, *_GS),
            _diag_blocks(stack("db_re")), _diag_blocks(stack("db_im")))
    d_a_re, d_a_im, d_log_dt, d_bt_re, d_bt_im = _ssm_prep_bwd(
        sp["a_re"], sp["a_im"], sp["log_dt"], sp["bt_re"], sp["bt_im"], cots)
    sgrads = {"norm_mix": stack("norm_mix"), "b_gate": stack("b_gate"),
              "attn_sinks": stack("attn_sinks")[:, :, :N_Q_HEADS], "ssm_a_re": d_a_re, "ssm_a_im": d_a_im,
              "ssm_b_re": jnp.swapaxes(d_bt_re, 2, 3), "ssm_b_im": jnp.swapaxes(d_bt_im, 2, 3),
              "ssm_c_re": _diag_blocks(stack("dc_re")), "ssm_c_im": -_diag_blocks(stack("dc_im")),
              "ssm_d": stack("ssm_d"), "ssm_log_dt": d_log_dt, "norm_ffn": stack("norm_ffn")}
    return sgrads, stack("conv_w")[:, :3]


def kernel(x, norm_mix, w_in, b_gate, attn_sinks, w_attn_o, conv_w, w_conv_o, ssm_a_re, ssm_a_im, ssm_b_re, ssm_b_im, ssm_c_re, ssm_c_im, ssm_d, ssm_log_dt, w_ssm_glu, w_ssm_o, w_mix_o, norm_ffn, w_ffn_in, w_ffn_out, norm_final, loss_target, m_norm_mix, m_w_in, m_b_gate, m_attn_sinks, m_w_attn_o, m_conv_w, m_w_conv_o, m_ssm_a_re, m_ssm_a_im, m_ssm_b_re, m_ssm_b_im, m_ssm_c_re, m_ssm_c_im, m_ssm_d, m_ssm_log_dt, m_w_ssm_glu, m_w_ssm_o, m_w_mix_o, m_norm_ffn, m_w_ffn_in, m_w_ffn_out, m_norm_final, v_norm_mix, v_w_in, v_b_gate, v_attn_sinks, v_w_attn_o, v_conv_w, v_w_conv_o, v_ssm_a_re, v_ssm_a_im, v_ssm_b_re, v_ssm_b_im, v_ssm_c_re, v_ssm_c_im, v_ssm_d, v_ssm_log_dt, v_w_ssm_glu, v_w_ssm_o, v_w_mix_o, v_norm_ffn, v_w_ffn_in, v_w_ffn_out, v_norm_final):
    big = {"w": dict(w_in=w_in, w_attn_o=w_attn_o, w_conv_o=w_conv_o, w_ssm_glu=w_ssm_glu, w_ssm_o=w_ssm_o,
                     w_mix_o=w_mix_o, w_ffn_in=w_ffn_in, w_ffn_out=w_ffn_out),
           "m": dict(w_in=m_w_in, w_attn_o=m_w_attn_o, w_conv_o=m_w_conv_o, w_ssm_glu=m_w_ssm_glu,
                     w_ssm_o=m_w_ssm_o, w_mix_o=m_w_mix_o, w_ffn_in=m_w_ffn_in, w_ffn_out=m_w_ffn_out),
           "v": dict(w_in=v_w_in, w_attn_o=v_w_attn_o, w_conv_o=v_w_conv_o, w_ssm_glu=v_w_ssm_glu,
                     w_ssm_o=v_w_ssm_o, w_mix_o=v_w_mix_o, w_ffn_in=v_w_ffn_in, w_ffn_out=v_w_ffn_out)}
    small = {"w": dict(norm_mix=norm_mix, b_gate=b_gate, attn_sinks=attn_sinks, ssm_a_re=ssm_a_re,
                       ssm_a_im=ssm_a_im, ssm_b_re=ssm_b_re, ssm_b_im=ssm_b_im, ssm_c_re=ssm_c_re,
                       ssm_c_im=ssm_c_im, ssm_d=ssm_d, ssm_log_dt=ssm_log_dt, norm_ffn=norm_ffn),
             "m": dict(norm_mix=m_norm_mix, b_gate=m_b_gate, attn_sinks=m_attn_sinks, ssm_a_re=m_ssm_a_re,
                       ssm_a_im=m_ssm_a_im, ssm_b_re=m_ssm_b_re, ssm_b_im=m_ssm_b_im, ssm_c_re=m_ssm_c_re,
                       ssm_c_im=m_ssm_c_im, ssm_d=m_ssm_d, ssm_log_dt=m_ssm_log_dt, norm_ffn=m_norm_ffn),
             "v": dict(norm_mix=v_norm_mix, b_gate=v_b_gate, attn_sinks=v_attn_sinks, ssm_a_re=v_ssm_a_re,
                       ssm_a_im=v_ssm_a_im, ssm_b_re=v_ssm_b_re, ssm_b_im=v_ssm_b_im, ssm_c_re=v_ssm_c_re,
                       ssm_c_im=v_ssm_c_im, ssm_d=v_ssm_d, ssm_log_dt=v_ssm_log_dt, norm_ffn=v_norm_ffn)}
    finals = {"w": norm_final, "m": m_norm_final, "v": v_norm_final}
    convs = {"w": conv_w, "m": m_conv_w, "v": v_conv_w}
    small_out_shapes = {name: a.shape for name, a in small["w"].items()}
    small_out_shapes.update(norm_final=(D_MODEL,), conv_w=(DEPTH, 3, 64))
    small_shapes = dict(small_out_shapes, norm_final=(1, D_MODEL), conv_w=(DEPTH, 3, WIDTH))
    dense = ("ssm_b_re", "ssm_b_im", "ssm_c_re", "ssm_c_im")
    for name in dense:
        small_shapes[name] = (DEPTH, SSM_GROUPS, SSM_GROUP * SSM_STATE)
    small_wmv = {name: [(convs[s] if name == "conv_w" else finals[s] if name == "norm_final" else small[s][name])
                        .reshape((DEPTH, 3, 64) if name == "conv_w" else small_shapes[name]) for s in "wmv"]
                 for name in small_shapes}
    mine = 4 * lax.axis_index("x") + 2 * lax.axis_index("y") + lax.axis_index("c")

    travel = {s: _travel_layout(big[s]) for s in "wmv"}
    stacked16 = list(zip(*[[a[0] for a in _travel_layout({n: w[i:i + 1].astype(BF16) for n, w in big["w"].items()})]
                           for i in range(DEPTH)]))
    rp = {"norm_mix": norm_mix[:, None], "norm_ffn": norm_ffn[:, None], "attn_sinks": attn_sinks[:, None],
          "b_gate": b_gate[:, None], "ssm_d": ssm_d[:, None]}
    sp = {"a_re": ssm_a_re, "a_im": ssm_a_im, "log_dt": ssm_log_dt[:, :, None],
          "bt_re": jnp.swapaxes(ssm_b_re, 2, 3), "bt_im": jnp.swapaxes(ssm_b_im, 2, 3),
          "c_re": ssm_c_re, "c_im": ssm_c_im}
    rows_tile = {"win_t": 368, "wffn_t": 352, "wout": 176, "wmix": 128, "branch_t": 128, "wglu": 64}
    core = lax.axis_index("c").astype(jnp.int32).reshape(1)
    no_tie = jnp.zeros((8, 128), F32)

    def landing_zones(srcs):
        return [lax.empty((N_DEV,) + s.shape, s.dtype) for s in srcs]

    def gather_chips(tag, i, kinds, after, extra=()):
        srcs = [stacked16[j][i] for j in kinds] + list(extra)
        s_sems, r_sems, arrays, token = _split_start(
            f"gather_chips_start_{tag}", srcs + landing_zones(srcs), 4 * len(srcs), _plan_gather_chips, after)
        return (tag, s_sems, r_sems, arrays), token

    def gather_pass(state, after):
        tag, s_sems, r_sems, arrays = state
        arrays = _split_wait(f"gather_chips_wait_{tag}", arrays, s_sems, r_sems, after, _plan_gather_chips)
        n = len(arrays) // 2
        s_sems, r_sems, lands, token = _split_start(
            f"gather_pass_start_{tag}", list(arrays[n:]), 4 * n, _plan_gather_pass)
        return (tag, s_sems, r_sems, lands), token

    def gather_done(state, after, kinds):
        tag, s_sems, r_sems, lands = state
        lands = _split_wait(f"gather_pass_wait_{tag}", lands, s_sems, r_sems, after, _plan_gather_pass)
        named = {KINDS[j][0]: a.reshape(N_DEV * KINDS[j][1], KINDS[j][2]) for a, j in zip(lands, kinds)}
        return named, list(lands[len(kinds):])

    all_kinds, mixer_kinds, ffn_kinds = tuple(range(len(KINDS))), (0, 3, 4, 5), (1, 2)
    no_hooks = {name: (lambda value: no_tie) for name in ("early", "pre_ffn", "mid", "late")}
    state, token = gather_chips("0m", 0, mixer_kinds, None, extra=[jnp.pad(conv_w.reshape(6, 128), ((0, 2), (0, 0)))])
    mats = _ssm_mats(dict(sp, log_dt=sp["log_dt"] + token[0, 0]))
    tabs = _rope_tabs()
    early_work = list(mats.values()) + list(tabs) + [a for name in dense for a in small_wmv[name]]
    early_work += [stacked16[j][0] for j in ffn_kinds] + [stacked16[j][1] for j in mixer_kinds]
    state, _ = gather_pass(state, early_work)
    ffn_state, tie = gather_chips("0f", 0, ffn_kinds, state[3][0])
    w_next, (conv_all,) = gather_done(state, tabs[2], mixer_kinds)
    conv_full = conv_all[:, :6].reshape(N_DEV, DEPTH, 3, 64).transpose(1, 2, 0, 3).reshape(DEPTH, 3, WIDTH)
    rp["conv_w"] = jnp.pad(conv_full, ((0, 0), (0, 5), (0, 0)))

    act = x[0]
    weights, kept = [], []
    for i in range(DEPTH):
        w_i, hooks, held = w_next, dict(no_hooks), {}

        def early(value, ffn_state=ffn_state, held=held):
            held["ffn"], token = gather_pass(ffn_state, value)
            return token

        def pre_ffn(value, w_i=w_i, held=held):
            w_i.update(gather_done(held["ffn"], value, ffn_kinds)[0])

        hooks.update(early=early, pre_ffn=pre_ffn)
        if i + 1 < DEPTH:
            state, tie = gather_chips(f"{i + 1}m", i + 1, mixer_kinds, tie if i == 0 else w_i["win_t"])

            def mid(value, i=i, state=state, held=held):
                held["next"], token = gather_pass(state, value)
                held["next_ffn"], token = gather_chips(f"{i + 1}f", i + 1, ffn_kinds, token)
                return token

            hooks.update(mid=mid)
        act, k = _layer_fwd(act, i, w_i, rp, mats, tabs, tie, hooks)
        if i + 1 < DEPTH:
            w_next, _ = gather_done(held["next"], act, mixer_kinds)
            ffn_state, tie = held["next_ffn"], no_tie
        weights.append(w_i)
        kept.append(k)
    loss_row, dx, d_norm_final = _loss_head(act, norm_final[None], loss_target[0])

    landings = [lax.empty((DEPTH, 4, r, c), BF16) for _, r, c in KINDS]
    landings0 = [lax.empty((1, 4, r, c), BF16) for _, r, c in KINDS]

    def scatter_pair(tag, kinds, grads, after):
        parts4 = [g.reshape(4, 2, KINDS[j][1], KINDS[j][2]) for g, j in zip(grads, kinds)]
        zones = [lax.empty((4, KINDS[j][1], KINDS[j][2]), BF16) for j in kinds]
        s_sems, r_sems, arrays, token = _split_start(
            f"scatter_pair_start_{tag}", parts4 + zones, 4 * len(kinds), _plan_scatter_pair, after)
        return (tag, kinds, s_sems, r_sems, arrays), token

    def scatter_chips(state, lands, slot, after):
        tag, kinds, s_sems, r_sems, arrays = state
        arrays = _split_wait(f"scatter_pair_wait_{tag}", arrays, s_sems, r_sems, after, _plan_scatter_pair)
        n = len(kinds)
        sums, mine_lands = [], []
        for k, j in enumerate(kinds):
            name = KINDS[j][0]
            chip_sum, land = _pair_sum(arrays[k], arrays[n + k], lands[j], slot, core, rows_tile[name],
                                       f"pair_sum_{name}")
            sums.append(chip_sum)
            mine_lands.append(land)
        s_sems, r_sems, arrays, token = _split_start(
            f"scatter_chips_start_{tag}", sums + mine_lands, 3 * n, _plan_scatter_chips(slot))
        return (tag, kinds, slot, s_sems, r_sems, arrays), token

    def scatter_done(state, lands, after):
        tag, kinds, slot, s_sems, r_sems, arrays = state
        arrays = _split_wait(f"scatter_chips_wait_{tag}", arrays, s_sems, r_sems, after, _plan_scatter_chips(slot))
        lands = list(lands)
        for k, j in enumerate(kinds):
            lands[j] = arrays[len(kinds) + k]
        return lands

    sg = [None] * DEPTH
    pending, tie = None, no_tie
    for i in reversed(range(DEPTH)):
        hooks, held = dict(no_hooks), {}
        if pending is not None:
            def mid(value, i=i, pending=pending, held=held):
                held["chips"], token = scatter_chips(pending, landings, i + 1, value[2])
                if i == 0:
                    held["ffn_pair"], token = scatter_pair("0f", ffn_kinds, value[:2], token)
                return token

            hooks.update(mid=mid)
        if i == 0:
            def late(value, held=held):
                held["ffn_chips"], token = scatter_chips(held["ffn_pair"], landings0, 0, value)
                return token

            hooks.update(late=late)
        dx, grads, sg[i] = _layer_bwd(dx, kept[i], i, weights[i], rp, mats, tabs, tie, hooks)
        if pending is not None:
            landings = scatter_done(held["chips"], landings, dx)
        if i > 0:
            pending, tie = scatter_pair(str(i), all_kinds, grads, dx)
        else:
            pending, _ = scatter_pair("0m", mixer_kinds, [grads[j] for j in mixer_kinds], dx)

    sgrads, conv_grad = _replicated_grads(sg, sp)

    small_names = list(REPLICATED) + ["norm_final", "conv_w"]
    sgrads.update(norm_final=d_norm_final, conv_w=conv_grad)
    small_src = [sgrads[name].reshape(small_shapes[name]).astype(BF16) for name in small_names]
    small_src.append(jnp.broadcast_to(loss_row[:, :1], (8, 128)))
    last, tie = scatter_chips(pending, landings0, 0, small_src[0])
    s_sems, r_sems, arrays, tie = _split_start(
        "gather_small_chips_start", small_src + landing_zones(small_src), 4 * len(small_src), _plan_gather_chips, tie)
    small_state = ("small", s_sems, r_sems, arrays)

    big_out = []
    for j, (name, _, _) in enumerate(KINDS):
        big_out.append(_adamw(landings[j], travel["w"][j], travel["m"][j], travel["v"][j], rows_tile[name],
                              "adamw_late_" + name, groups=(1, DEPTH), tie=tie))
        tie = big_out[-1][3]
    landings0 = scatter_done(held["ffn_chips"], landings0, tie)
    landings0 = scatter_done(last, landings0, tie)
    small_state, _ = gather_pass(small_state, landings0[0])
    big_out = [_adamw(landings0[j], travel["w"][j], travel["m"][j], travel["v"][j], rows_tile[name],
                      "adamw_first_" + name, groups=(0, 1), fill=big_out[j]) for j, (name, _, _) in enumerate(KINDS)]
    big_res = [_native_layout([big_out[j][kind] for j in range(len(KINDS))]) for kind in range(4)]

    _, sparts = gather_done(small_state, big_out[-1][0], ())
    loss = jnp.sum(sparts[-1][:, 0, 0])
    sparts = dict(zip(small_names, sparts))
    sparts["conv_w"] = lax.dynamic_slice_in_dim(sparts["conv_w"], mine * 64, 64, axis=3)
    small_res = {}
    for name in small_names:
        res = _adamw_small(sparts[name], *small_wmv[name], "adamw_" + name)
        small_res[name] = [r.reshape(small_out_shapes[name]) for r in res]

    order = ["norm_mix", "w_in", "b_gate", "attn_sinks", "w_attn_o", "conv_w", "w_conv_o", "ssm_a_re", "ssm_a_im",
             "ssm_b_re", "ssm_b_im", "ssm_c_re", "ssm_c_im", "ssm_d", "ssm_log_dt", "w_ssm_glu", "w_ssm_o",
             "w_mix_o", "norm_ffn", "w_ffn_in", "w_ffn_out", "norm_final"]
    outs = [loss, dx[None]]
    for kind in range(4):
        for name in order:
            outs.append(big_res[kind][name] if name in big_res[kind] else small_res[name][kind])
    return tuple(outs)
```

```python
import math

import jax
import jax.numpy as jnp
from jax import lax
from jax.experimental import pallas as pl
from jax.experimental.pallas import tpu as pltpu

F32 = jnp.float32
BF16 = jnp.bfloat16

N_DEV = 8
DEPTH = 4
SEQ = 2048
D_MODEL = 1024
N_Q_HEADS = 8
HEAD_DIM = 64
ATTN_W = 512
KV_W = 128
BLOCK = 128
N_BLOCKS = SEQ // BLOCK
ROPE_THETA = 500000.0
ROT_DIM = 16
NEG_INF = -1e30
WIDTH = 512
SSM_GROUPS = 32
SSM_GROUP = 16
SSM_STATE = 64
CHUNK = 256
N_CHUNKS = SEQ // CHUNK
GATE_W = 3 * D_MODEL
IN_COLS = 5888
FFN_H = 2816
NORM_EPS = 1e-6
LR, B1, B2, ADAM_EPS, WD, STEP = 0.001, 0.9, 0.999, 1e-08, 0.01, 10

COL_Q, COL_KV, COL_CBX, COL_U, COL_G = 0, 512, 768, 2304, 2816
PIECE_W = (512, 256, 512, 512, 512, 512, 3072)
PIECE_OFF = tuple(sum(PIECE_W[:i]) for i in range(len(PIECE_W)))

KINDS = (("win_t", 736, 1024), ("wffn_t", 704, 1024), ("wout", 352, 1024), ("wmix", 128, 1024),
         ("branch_t", 128, 1536), ("wglu", 64, 512))

REPLICATED = ("norm_mix", "b_gate", "attn_sinks", "ssm_a_re", "ssm_a_im", "ssm_b_re", "ssm_b_im", "ssm_c_re",
              "ssm_c_im", "ssm_d", "ssm_log_dt", "norm_ffn")

VMEM_LIMIT = 56 * 1024 * 1024
NT = (((1,), (1,)), ((), ()))
TN = (((0,), (0,)), ((), ()))
MESH_ID = pl.DeviceIdType.MESH
ANY = pl.BlockSpec(memory_space=pl.ANY)
HBM = pl.BlockSpec(memory_space=pltpu.HBM)
SEM = pl.BlockSpec(memory_space=pltpu.SEMAPHORE)
EFFECT = pltpu.SideEffectType.DATAFLOW_SIDE_EFFECTING


def _cp(**kw):
    return pltpu.CompilerParams(vmem_limit_bytes=VMEM_LIMIT, **kw)


def _full(shape):
    return pl.BlockSpec(shape, lambda *_: (0,) * len(shape))


def _resident(shape):
    return pl.BlockSpec(shape, lambda *_: (0,) * len(shape), pipeline_mode=pl.Buffered(1))


def _mm_tn(a, b, *, tm, tn, name):
    k, m = a.shape
    n = b.shape[1]

    def body(a_ref, b_ref, o_ref):
        o_ref[...] = lax.dot_general(a_ref[...].astype(BF16), b_ref[...].astype(BF16), TN,
                                     preferred_element_type=F32).astype(BF16)

    return pl.pallas_call(
        body, grid=(m // tm, n // tn),
        in_specs=[pl.BlockSpec((k, tm), lambda i, j: (0, i)), pl.BlockSpec((k, tn), lambda i, j: (0, j))],
        out_specs=pl.BlockSpec((tm, tn), lambda i, j: (i, j)),
        out_shape=jax.ShapeDtypeStruct((m, n), BF16), compiler_params=_cp(), name=name)(a, b)


def _rms_rows(xv, g):
    r = lax.rsqrt(jnp.mean(xv * xv, axis=-1, keepdims=True) + NORM_EPS)
    return ((xv * r) * g).astype(BF16)


def _rms_mm_in(x, g, wt, tabs, cw, layer, tie):
    tt = 512
    widths = (3 * WIDTH, WIDTH, GATE_W)
    offs = (COL_CBX, COL_U, COL_G)

    def body(x_ref, g_ref, w_ref, tc_ref, ta_ref, tb_ref, cw_ref, tie_ref,
             q_ref, kv_ref, cbx_ref, u_ref, gl_ref, cv_ref, h_ref, tail_ref):
        @pl.when(pl.program_id(0) == 0)
        def _():
            tail_ref[...] = jnp.zeros_like(tail_ref)

        h = _rms_rows(x_ref[...], g_ref[...])
        h_ref[...] = h
        prod = lax.dot_general(h, w_ref[...], NT, preferred_element_type=F32)
        for ref, o, w in zip((cbx_ref, u_ref, gl_ref), offs, widths):
            ref[...] = prod[:, o:o + w]
        c, a, b = tc_ref[...], ta_ref[...], tb_ref[...]
        for j in range(ATTN_W // 128):
            q_ref[:, 128 * j:128 * (j + 1)] = _rope(prod[:, 128 * j:128 * (j + 1)], c, a, b) * (HEAD_DIM ** -0.5)
        kv_ref[:, :KV_W] = _rope(prod[:, COL_KV:COL_KV + KV_W], c, a, b)
        kv_ref[:, KV_W:] = prod[:, COL_KV + KV_W:COL_CBX]

        row = lax.broadcasted_iota(jnp.int32, (tt, 128), 0)
        for j in range(WIDTH // 128):
            cols = slice(128 * j, 128 * (j + 1))
            cb = prod[:, COL_CBX + 128 * j:COL_CBX + 128 * (j + 1)]
            z = prod[:, COL_CBX + WIDTH + 128 * j:COL_CBX + WIDTH + 128 * (j + 1)] \
                * prod[:, COL_CBX + 2 * WIDTH + 128 * j:COL_CBX + 2 * WIDTH + 128 * (j + 1)]
            before1, before2 = tail_ref[7:8, cols], tail_ref[6:7, cols]
            z1 = jnp.where(row == 0, before1, pltpu.roll(z, 1, axis=0))
            z2 = jnp.where(row == 0, before2, jnp.where(row == 1, before1, pltpu.roll(z, 2, axis=0)))
            s = cw_ref[0:1, cols] * z2 + cw_ref[1:2, cols] * z1 + cw_ref[2:3, cols] * z
            cv_ref[:, cols] = (cb * s).astype(BF16)
            tail_ref[:, cols] = z[tt - 8:, :]

    row_spec = lambda w: pl.BlockSpec((tt, w), lambda i: (i, 0))
    sds = jax.ShapeDtypeStruct
    return pl.pallas_call(
        body, grid=(SEQ // tt,),
        in_specs=[row_spec(D_MODEL), _full((1, D_MODEL)), _resident((IN_COLS, D_MODEL)),
                  row_spec(128), row_spec(128), row_spec(128),
                  pl.BlockSpec((None, 8, WIDTH), lambda i: (layer, 0, 0)), ANY],
        out_specs=[row_spec(ATTN_W), row_spec(2 * KV_W), row_spec(3 * WIDTH), row_spec(WIDTH), row_spec(GATE_W),
                   row_spec(WIDTH), row_spec(D_MODEL)],
        out_shape=[sds((SEQ, ATTN_W), F32), sds((SEQ, 2 * KV_W), F32), sds((SEQ, 3 * WIDTH), F32),
                   sds((SEQ, WIDTH), F32), sds((SEQ, GATE_W), F32), sds((SEQ, WIDTH), BF16),
                   sds((SEQ, D_MODEL), BF16)],
        scratch_shapes=[pltpu.VMEM((8, WIDTH), F32)], compiler_params=_cp(), name="rms_mm_in",
    )(x, g, wt, *tabs, cw, tie)


def _rms_mm_ffn(x, g, wt):
    tt = 256

    def body(x_ref, g_ref, w_ref, act_ref, up_ref, silu_ref, dsilu_ref, h_ref):
        h = _rms_rows(x_ref[...], g_ref[...])
        h_ref[...] = h
        prod = lax.dot_general(h, w_ref[...], NT, preferred_element_type=F32)
        gt, up = prod[:, :FFN_H], prod[:, FFN_H:]
        sg = jax.nn.sigmoid(gt)
        silu = gt * sg
        act_ref[...] = (silu * up).astype(BF16)
        up_ref[...] = up.astype(BF16)
        silu_ref[...] = silu.astype(BF16)
        dsilu_ref[...] = (sg + silu * (1.0 - sg)).astype(BF16)

    row = lambda w: pl.BlockSpec((tt, w), lambda i: (i, 0))
    return pl.pallas_call(
        body, grid=(SEQ // tt,), in_specs=[row(D_MODEL), _full((1, D_MODEL)), _resident((2 * FFN_H, D_MODEL))],
        out_specs=[row(FFN_H)] * 4 + [row(D_MODEL)],
        out_shape=[jax.ShapeDtypeStruct((SEQ, FFN_H), BF16)] * 4 + [jax.ShapeDtypeStruct((SEQ, D_MODEL), BF16)],
        compiler_params=_cp(), name="rms_mm_ffn")(x, g, wt)


def _mm_rmsbwd(pieces, wt, x, g, dres, name):
    tt = 512
    widths = [p.shape[1] for p in pieces]
    offs = [sum(widths[:i]) for i in range(len(widths))]
    n = len(pieces)

    def body(*refs):
        p_refs, (w_ref, x_ref, g_ref, r_ref, dx_ref, dg_ref) = refs[:n], refs[n:]

        @pl.when(pl.program_id(0) == 0)
        def _():
            dg_ref[...] = jnp.zeros_like(dg_ref)

        dh = jnp.zeros((tt, D_MODEL), F32)
        for p_ref, o, w in zip(p_refs, offs, widths):
            dh += jnp.dot(p_ref[...], w_ref[o:o + w, :], preferred_element_type=F32)
        xv = x_ref[...]
        r = lax.rsqrt(jnp.mean(xv * xv, axis=-1, keepdims=True) + NORM_EPS)
        xh = xv * r
        gy = dh * g_ref[...]
        dx_ref[...] = r_ref[...] + r * (gy - xh * jnp.mean(gy * xh, axis=-1, keepdims=True))
        dg_ref[...] += jnp.sum(dh * xh, axis=0, keepdims=True)

    row = lambda w: pl.BlockSpec((tt, w), lambda i: (i, 0))
    return pl.pallas_call(
        body, grid=(SEQ // tt,),
        in_specs=[row(w) for w in widths] + [_resident(wt.shape), row(D_MODEL), _full((1, D_MODEL)), row(D_MODEL)],
        out_specs=[row(D_MODEL), _full((1, D_MODEL))],
        out_shape=[jax.ShapeDtypeStruct((SEQ, D_MODEL), F32), jax.ShapeDtypeStruct((1, D_MODEL), F32)],
        compiler_params=_cp(), name=name)(*pieces, wt, x, g, dres)


def _tn_pieces(pieces, h):
    tk, tn = 512, 512
    nk = SEQ // tk
    n = len(pieces)

    def body(*refs):
        p_refs, (h_ref, o_ref, acc_ref) = refs[:n], refs[n:]
        kk = pl.program_id(1)

        @pl.when(kk == 0)
        def _():
            acc_ref[...] = jnp.zeros_like(acc_ref)

        hv = h_ref[...]
        for p_ref, o, w in zip(p_refs, PIECE_OFF, PIECE_W):
            acc_ref[o:o + w, :] += lax.dot_general(p_ref[...], hv, TN, preferred_element_type=F32)

        @pl.when(kk == nk - 1)
        def _():
            o_ref[...] = acc_ref[...].astype(BF16)

    return pl.pallas_call(
        body, grid=(D_MODEL // tn, nk),
        in_specs=[pl.BlockSpec((tk, w), lambda j, kk: (kk, 0)) for w in PIECE_W]
        + [pl.BlockSpec((tk, tn), lambda j, kk: (kk, j))],
        out_specs=pl.BlockSpec((IN_COLS, tn), lambda j, kk: (0, j)),
        out_shape=jax.ShapeDtypeStruct((IN_COLS, D_MODEL), BF16),
        scratch_shapes=[pltpu.VMEM((IN_COLS, tn), F32)], compiler_params=_cp(), name="tn_pieces")(*pieces, h)


def _tn_branches(dys, acts):
    tk = 512
    nk = SEQ // tk

    def body(d0, d1, d2, a0, a1, a2, o_ref, acc_ref):
        kk = pl.program_id(0)

        @pl.when(kk == 0)
        def _():
            acc_ref[...] = jnp.zeros_like(acc_ref)

        for j, (d, a) in enumerate(((d0, a0), (d1, a1), (d2, a2))):
            acc_ref[:, WIDTH * j:WIDTH * (j + 1)] += lax.dot_general(d[...], a[...], TN, preferred_element_type=F32)

        @pl.when(kk == nk - 1)
        def _():
            o_ref[...] = acc_ref[...].astype(BF16)

    row = lambda w: pl.BlockSpec((tk, w), lambda kk: (kk, 0))
    return pl.pallas_call(
        body, grid=(nk,), in_specs=[row(D_MODEL)] * 3 + [row(WIDTH)] * 3,
        out_specs=_full((D_MODEL, 3 * WIDTH)), out_shape=jax.ShapeDtypeStruct((D_MODEL, 3 * WIDTH), BF16),
        scratch_shapes=[pltpu.VMEM((D_MODEL, 3 * WIDTH), F32)], compiler_params=_cp(), name="tn_branches",
    )(*dys, *acts)


def _rope(t, c, a, b):
    return t * c + pltpu.roll(t, 120, axis=1) * a + pltpu.roll(t, 8, axis=1) * b


def _rope_t(d, c, a, b):
    return d * c + pltpu.roll(d * a, 8, axis=1) + pltpu.roll(d * b, 120, axis=1)


def _band_sides(band):
    left = lax.broadcasted_iota(jnp.int32, band.shape, 1) < HEAD_DIM
    h0 = jnp.where(left, band, 0.0)
    h1 = jnp.where(left, 0.0, band)
    r0 = pltpu.roll(h0, HEAD_DIM, axis=1)
    r1 = pltpu.roll(h1, HEAD_DIM, axis=1)
    return ((h0.astype(BF16), r0.astype(BF16)), (r1.astype(BF16), h1.astype(BF16)))


def _attn_mask(i):
    qi = lax.broadcasted_iota(jnp.int32, (2 * BLOCK, 2 * BLOCK), 0) % BLOCK
    kj = lax.broadcasted_iota(jnp.int32, (2 * BLOCK, 2 * BLOCK), 1)
    delta = qi + BLOCK - kj
    return (delta >= 0) & (delta < BLOCK) & ((kj >= BLOCK) | (i > 0))


def _attn_probs(s, ok, sink):
    s = jnp.where(ok, s, NEG_INF)
    m = jnp.maximum(jnp.max(s, axis=-1, keepdims=True), sink)
    p = jnp.exp(s - m)
    es = jnp.exp(sink - m)
    inv = 1.0 / (jnp.sum(p, axis=-1, keepdims=True) + es)
    return p * inv, es * inv


def _kv_group(qs, ks, vs, kh, sink_ref):
    q2 = jnp.concatenate([qs[2 * kh], qs[2 * kh + 1]], axis=0)
    kst = jnp.concatenate([ks[kh][0], ks[kh][1]], axis=0)
    vst = jnp.concatenate([vs[kh][0], vs[kh][1]], axis=0)
    top = lax.broadcasted_iota(jnp.int32, (2 * BLOCK, 1), 0) < BLOCK
    sinks = [jnp.where(top, sink_ref[0, 4 * kh + h], sink_ref[0, 4 * kh + 2 + h]) for h in range(2)]
    return q2, kst, vst, sinks


def _attn_load(q_ref, kvc_ref, kvp_ref, tc_ref, ta_ref, tb_ref, pc_ref, pa_ref, pb_ref):
    c, a, b = tc_ref[...], ta_ref[...], tb_ref[...]
    kband = jnp.concatenate([kvp_ref[:, :KV_W], kvc_ref[:, :KV_W]], axis=0)
    vband = jnp.concatenate([kvp_ref[:, KV_W:], kvc_ref[:, KV_W:]], axis=0)
    qs = [q_ref[:, 128 * j:128 * (j + 1)].astype(BF16) for j in range(4)]
    return qs, _band_sides(kband), _band_sides(vband), (c, a, b)


def _attn_specs(clamp):
    cur = lambda i: (clamp(i), 0)
    prev = lambda i: (jnp.maximum(clamp(i) - 1, 0), 0)
    return [
        pl.BlockSpec((BLOCK, ATTN_W), cur), pl.BlockSpec((BLOCK, 2 * KV_W), cur),
        pl.BlockSpec((BLOCK, 2 * KV_W), prev),
        pl.BlockSpec((BLOCK, 128), cur), pl.BlockSpec((BLOCK, 128), cur), pl.BlockSpec((BLOCK, 128), cur),
        pl.BlockSpec((BLOCK, 128), prev), pl.BlockSpec((BLOCK, 128), prev), pl.BlockSpec((BLOCK, 128), prev),
        pl.BlockSpec(memory_space=pltpu.SMEM),
    ]


def _attn_fwd(q, kv, tabs, sinks):
    tc, ta, tb = tabs

    def body(q_ref, kvc_ref, kvp_ref, tc_ref, ta_ref, tb_ref, pc_ref, pa_ref, pb_ref, sink_ref, o_ref):
        i = pl.program_id(0)
        qs, ks, vs, _ = _attn_load(q_ref, kvc_ref, kvp_ref, tc_ref, ta_ref, tb_ref, pc_ref, pa_ref, pb_ref)
        ok = _attn_mask(i)
        for kh in range(2):
            q2, kst, vst, sinks = _kv_group(qs, ks, vs, kh, sink_ref)
            s = lax.dot_general(q2, kst, NT, preferred_element_type=F32)
            pn = [_attn_probs(s[:, 2 * BLOCK * h:2 * BLOCK * (h + 1)], ok, sinks[h])[0].astype(BF16) for h in range(2)]
            o2 = jnp.dot(jnp.concatenate(pn, axis=1), vst, preferred_element_type=F32).astype(BF16)
            for r in range(2):
                j = 2 * kh + r
                o_ref[:, 128 * j:128 * (j + 1)] = o2[BLOCK * r:BLOCK * (r + 1)]

    return pl.pallas_call(
        body, grid=(N_BLOCKS,), in_specs=_attn_specs(lambda i: i),
        out_specs=pl.BlockSpec((BLOCK, ATTN_W), lambda i: (i, 0)),
        out_shape=jax.ShapeDtypeStruct((SEQ, ATTN_W), BF16), compiler_params=_cp(), name="attn_fwd",
    )(q, kv, kv, tc, ta, tb, tc, ta, tb, sinks)


def _attn_bwd(q, kv, tabs, sinks, do):
    tc, ta, tb = tabs
    last = N_BLOCKS - 1
    clamp = lambda i: jnp.minimum(i, last)

    def place(full, side, kh):
        left = lax.broadcasted_iota(jnp.int32, full.shape, 1) < HEAD_DIM
        valid = jnp.where(left, full, 0.0) if side == 0 else jnp.where(left, 0.0, full)
        return valid if side == kh else pltpu.roll(valid, HEAD_DIM, axis=1)

    def body(q_ref, kvc_ref, kvp_ref, tc_ref, ta_ref, tb_ref, pc_ref, pa_ref, pb_ref, sink_ref, do_ref,
             dq_ref, dkv_ref, ds_ref, carry_ref):
        i = pl.program_id(0)

        @pl.when(i == 0)
        def _():
            ds_ref[...] = jnp.zeros_like(ds_ref)
            carry_ref[...] = jnp.zeros_like(carry_ref)

        @pl.when(i > last)
        def _():
            dkv_ref[...] = carry_ref[...].astype(BF16)

        @pl.when(i <= last)
        def _():
            qs, ks, vs, (c, a, b) = _attn_load(q_ref, kvc_ref, kvp_ref, tc_ref, ta_ref, tb_ref,
                                               pc_ref, pa_ref, pb_ref)
            ok = _attn_mask(i)
            dk = jnp.zeros((2 * BLOCK, 128), F32)
            dv = jnp.zeros((2 * BLOCK, 128), F32)
            dsink = jnp.zeros((1, 128), F32)
            lane = lax.broadcasted_iota(jnp.int32, (1, 128), 1)
            for kh in range(2):
                q2, kst, vst, sinks = _kv_group(qs, ks, vs, kh, sink_ref)
                do2 = jnp.concatenate([do_ref[:, 128 * (2 * kh + r):128 * (2 * kh + r + 1)] for r in range(2)],
                                      axis=0).astype(BF16)
                s = lax.dot_general(q2, kst, NT, preferred_element_type=F32)
                dp = lax.dot_general(do2, vst, NT, preferred_element_type=F32)
                pns, dss = [], []
                for h in range(2):
                    cols = slice(2 * BLOCK * h, 2 * BLOCK * (h + 1))
                    pn, ps = _attn_probs(s[:, cols], ok, sinks[h])
                    dr = jnp.sum(pn * dp[:, cols], axis=-1, keepdims=True)
                    pns.append(pn.astype(BF16))
                    dss.append((pn * (dp[:, cols] - dr)).astype(BF16))
                    for r in range(2):
                        part = -jnp.sum((ps * dr)[BLOCK * r:BLOCK * (r + 1)])
                        dsink += jnp.where(lane == 4 * kh + 2 * r + h, part, 0.0)
                ds2, pn2 = jnp.concatenate(dss, axis=1), jnp.concatenate(pns, axis=1)
                dq2 = jnp.dot(ds2, kst, preferred_element_type=F32) * (HEAD_DIM ** -0.5)
                dk2 = lax.dot_general(ds2, q2, TN, preferred_element_type=F32)
                dv2 = lax.dot_general(pn2, do2, TN, preferred_element_type=F32)
                for h in range(2):
                    dk += place(dk2[2 * BLOCK * h:2 * BLOCK * (h + 1)], h, kh)
                    dv += place(dv2[2 * BLOCK * h:2 * BLOCK * (h + 1)], h, kh)
                for r in range(2):
                    j = 2 * kh + r
                    dq_ref[:, 128 * j:128 * (j + 1)] = _rope_t(dq2[BLOCK * r:BLOCK * (r + 1)], c, a, b).astype(BF16)
            ds_ref[...] += dsink
            dk_prev = _rope_t(dk[:BLOCK], pc_ref[...], pa_ref[...], pb_ref[...])
            dk_cur = _rope_t(dk[BLOCK:], c, a, b)
            prev = jnp.concatenate([dk_prev, dv[:BLOCK]], axis=1)
            dkv_ref[...] = (carry_ref[...] + prev).astype(BF16)
            carry_ref[...] = jnp.concatenate([dk_cur, dv[BLOCK:]], axis=1)

    return pl.pallas_call(
        body, grid=(N_BLOCKS + 1,),
        in_specs=_attn_specs(clamp) + [pl.BlockSpec((BLOCK, ATTN_W), lambda i: (clamp(i), 0))],
        out_specs=[pl.BlockSpec((BLOCK, ATTN_W), lambda i: (clamp(i), 0)),
                   pl.BlockSpec((BLOCK, 2 * KV_W), lambda i: (jnp.maximum(i - 1, 0), 0)),
                   pl.BlockSpec((1, 128), lambda i: (0, 0))],
        out_shape=[jax.ShapeDtypeStruct((SEQ, ATTN_W), BF16), jax.ShapeDtypeStruct((SEQ, 2 * KV_W), BF16),
                   jax.ShapeDtypeStruct((1, 128), F32)],
        scratch_shapes=[pltpu.VMEM((BLOCK, 2 * KV_W), F32)], compiler_params=_cp(), name="attn_bwd",
    )(q, kv, kv, tc, ta, tb, tc, ta, tb, sinks, do)


def _shift_down(z, k):
    row = lax.broadcasted_iota(jnp.int32, z.shape, 0)
    return jnp.where(row < k, 0.0, pltpu.roll(z, k, axis=0))


def _shift_up(z, k):
    n = z.shape[0]
    row = lax.broadcasted_iota(jnp.int32, z.shape, 0)
    return jnp.where(row >= n - k, 0.0, pltpu.roll(z, n - k, axis=0))


def _conv_specs():
    nb = WIDTH // 128
    return [pl.BlockSpec((SEQ, 128), lambda j: (0, j)), pl.BlockSpec((SEQ, 128), lambda j: (0, nb + j)),
            pl.BlockSpec((SEQ, 128), lambda j: (0, 2 * nb + j)), pl.BlockSpec((None, 8, 128), lambda j: (0, 0, j))]


def _conv_bwd(cbx, cw, layer, dout, tie):
    def body(cb_ref, cc_ref, cx_ref, w_ref, do_ref, tie_ref, dcb_ref, dcc_ref, dcx_ref, dw_ref):
        cc, cx = cc_ref[...], cx_ref[...]
        z = cc * cx
        z1, z2 = _shift_down(z, 1), _shift_down(z, 2)
        w0, w1, w2 = w_ref[0:1, :], w_ref[1:2, :], w_ref[2:3, :]
        dout = do_ref[...]
        ds = dout * cb_ref[...]
        dcb_ref[...] = (dout * (w0 * z2 + w1 * z1 + w2 * z)).astype(BF16)
        dz = w2 * ds + w1 * _shift_up(ds, 1) + w0 * _shift_up(ds, 2)
        dcc_ref[...] = (dz * cx).astype(BF16)
        dcx_ref[...] = (dz * cc).astype(BF16)
        rows = [jnp.sum(ds * zz, axis=0, keepdims=True) for zz in (z2, z1, z)]
        dw_ref[...] = jnp.concatenate(rows + [jnp.zeros((5, 128), F32)], axis=0)

    col = lambda j: (0, j)
    specs = _conv_specs()
    specs[3] = pl.BlockSpec((None, 8, 128), lambda j: (layer, 0, j))
    return pl.pallas_call(
        body, grid=(WIDTH // 128,), in_specs=specs + [pl.BlockSpec((SEQ, 128), col), ANY],
        out_specs=[pl.BlockSpec((SEQ, 128), col), pl.BlockSpec((SEQ, 128), col), pl.BlockSpec((SEQ, 128), col),
                   pl.BlockSpec((8, 128), col)],
        out_shape=[jax.ShapeDtypeStruct((SEQ, WIDTH), BF16)] * 3 + [jax.ShapeDtypeStruct((8, WIDTH), F32)],
        compiler_params=_cp(), name="conv_bwd",
    )(cbx, cbx, cbx, cw, dout, tie)


def _ssm_prep_math(a_re, a_im, log_dt, bt_re, bt_im):
    dt = jnp.exp(log_dt)
    er = jnp.exp(a_re * dt)
    lr = er * jnp.cos(a_im * dt)
    li = er * jnp.sin(a_im * dt)
    n2 = a_re * a_re + a_im * a_im
    cr = ((lr - 1.0) * a_re + li * a_im) / n2
    ci = (li * a_re - (lr - 1.0) * a_im) / n2
    cr3, ci3 = cr[:, None, :], ci[:, None, :]
    return lr, li, cr3 * bt_re - ci3 * bt_im, cr3 * bt_im + ci3 * bt_re


_GS = (SSM_GROUPS, SSM_STATE)
_GHS = (SSM_GROUPS, SSM_GROUP, SSM_STATE)


def _layered(shape):
    return pl.BlockSpec((None,) + shape, lambda l: (l,) + (0,) * len(shape))


def _ssm_prep(a_re, a_im, log_dt, bt_re, bt_im):
    def body(ar, ai, ld, br, bi, o0, o1, o2, o3):
        outs = _ssm_prep_math(ar[...], ai[...], ld[...], br[...], bi[...])
        for o, v in zip((o0, o1, o2, o3), outs):
            o[...] = v

    shapes = [_GS, _GS, _GHS, _GHS]
    return pl.pallas_call(
        body, grid=(DEPTH,), in_specs=[_layered(s) for s in (_GS, _GS, (SSM_GROUPS, 1), _GHS, _GHS)],
        out_specs=[_layered(s) for s in shapes],
        out_shape=[jax.ShapeDtypeStruct((DEPTH,) + s, F32) for s in shapes],
        name="ssm_prep")(a_re, a_im, log_dt, bt_re, bt_im)


def _ssm_prep_bwd(a_re, a_im, log_dt, bt_re, bt_im, cots):
    def body(ar, ai, ld, br, bi, c0, c1, c2, c3, o0, o1, o2, o3, o4):
        _, vjp = jax.vjp(_ssm_prep_math, ar[...], ai[...], ld[...], br[...], bi[...])
        for o, v in zip((o0, o1, o2, o3, o4), vjp((c0[...], c1[...], c2[...], c3[...]))):
            o[...] = v

    ins = (_GS, _GS, (SSM_GROUPS, 1), _GHS, _GHS)
    return pl.pallas_call(
        body, grid=(DEPTH,), in_specs=[_layered(s) for s in ins + (_GS, _GS, _GHS, _GHS)],
        out_specs=[_layered(s) for s in ins],
        out_shape=[jax.ShapeDtypeStruct((DEPTH,) + s, F32) for s in ins],
        name="ssm_prep_bwd")(a_re, a_im, log_dt, bt_re, bt_im, *cots)


LANES_G = 512
N_LANE_GROUPS = SSM_GROUPS * SSM_STATE // LANES_G


def _scan_in_place(xr_ref, xi_ref, ar, ai, reverse):
    shape = (N_CHUNKS, xr_ref.shape[1])
    ar, ai = jnp.broadcast_to(ar, shape), jnp.broadcast_to(ai, shape)

    def rows(tau):
        t = (CHUNK - 1 - tau) if reverse else tau
        return pl.ds(pl.multiple_of(t * N_CHUNKS, N_CHUNKS), N_CHUNKS)

    def step(tau, carry):
        sr, si = carry
        return ar * sr - ai * si + xr_ref[rows(tau), :], ar * si + ai * sr + xi_ref[rows(tau), :]

    zero = jnp.zeros(shape, F32)
    er, ei = lax.fori_loop(0, CHUNK, step, (zero, zero), unroll=8)
    qr, qi = ar, ai
    for _ in range(8):
        qr, qi = qr * qr - qi * qi, 2.0 * qr * qi
    shift = _shift_up if reverse else _shift_down
    for k in (1, 2, 4):
        sr, si = shift(er, k), shift(ei, k)
        er, ei = er + qr * sr - qi * si, ei + qr * si + qi * sr
        qr, qi = qr * qr - qi * qi, 2.0 * qr * qi
    start = (shift(er, 1), shift(ei, 1))

    def write(tau, carry):
        sr, si = step(tau, carry)
        xr_ref[rows(tau), :] = sr
        xi_ref[rows(tau), :] = si
        return sr, si

    return write, start


def _ssm_specs(layer):
    col = lambda w: pl.BlockSpec((SEQ, w), lambda g: (0, g))
    diag = pl.BlockSpec((None, None, 128, LANES_G), lambda g: (layer, g, 0, 0))
    vec = pl.BlockSpec((None, 1, LANES_G), lambda g: (layer, 0, g))
    return col, diag, vec


def _to_scan_order(src_ref, dst_ref):
    def move(tau, _):
        dst_ref[pl.ds(pl.multiple_of(tau * N_CHUNKS, N_CHUNKS), N_CHUNKS), :] = src_ref[pl.ds(tau, N_CHUNKS, stride=CHUNK), :]
        return 0

    lax.fori_loop(0, CHUNK, move, 0, unroll=8)


def _to_time_order(src_ref, dst_ref, dtype):
    for j in range(N_CHUNKS):
        dst_ref[pl.ds(j * CHUNK, CHUNK), :] = src_ref[pl.ds(j, CHUNK, stride=N_CHUNKS), :].astype(dtype)


def _ssm_fwd(u, mats, layer, d):
    def body(u_ref, d_ref, br_ref, bi_ref, cr_ref, ci_ref, ar_ref, ai_ref, xr_ref, xi_ref, y_ref, us_ref):
        _to_scan_order(u_ref, us_ref)
        uv = us_ref[...].astype(BF16)
        xr_ref[...] = jnp.dot(uv, br_ref[...], preferred_element_type=F32)
        xi_ref[...] = jnp.dot(uv, bi_ref[...], preferred_element_type=F32)
        write, start = _scan_in_place(xr_ref, xi_ref, ar_ref[...], ai_ref[...], False)
        lax.fori_loop(0, CHUNK, write, start, unroll=8)
        y = lax.dot_general(xr_ref[...].astype(BF16), cr_ref[...], NT, preferred_element_type=F32)
        y += lax.dot_general(xi_ref[...].astype(BF16), ci_ref[...], NT, preferred_element_type=F32)
        us_ref[...] = y + d_ref[...] * us_ref[...]
        _to_time_order(us_ref, y_ref, F32)

    col, diag, vec = _ssm_specs(layer)
    return pl.pallas_call(
        body, grid=(N_LANE_GROUPS,),
        in_specs=[col(128), pl.BlockSpec((None, 1, 128), lambda g: (layer, 0, g)),
                  diag, diag, diag, diag, vec, vec],
        out_specs=[col(LANES_G), col(LANES_G), col(128)],
        out_shape=[jax.ShapeDtypeStruct((SEQ, SSM_GROUPS * SSM_STATE), F32)] * 2
        + [jax.ShapeDtypeStruct((SEQ, WIDTH), F32)],
        scratch_shapes=[pltpu.VMEM((SEQ, 128), F32)], compiler_params=_cp(), name="ssm_fwd",
    )(u, d, mats["b_re"], mats["b_im"], mats["c_re"], mats["c_im_neg"], mats["a_re"], mats["a_im"])


def _ssm_bwd(dy, x_re, x_im, u, mats, layer, d):
    def body(dyt_ref, ut_ref, d_ref, xr_ref, xi_ref, br_ref, bi_ref, cr_ref, ci_ref, ar_ref, ai_ref,
             du_ref, dar_ref, dai_ref, dbr_ref, dbi_ref, dcr_ref, dci_ref, lr_ref, li_ref, dys_ref, u_ref):
        _to_scan_order(dyt_ref, dys_ref)
        _to_scan_order(ut_ref, u_ref)
        dy = dys_ref[...].astype(BF16)
        lr_ref[...] = jnp.dot(dy, cr_ref[...], preferred_element_type=F32)
        li_ref[...] = jnp.dot(dy, ci_ref[...], preferred_element_type=F32)
        write, start = _scan_in_place(lr_ref, li_ref, ar_ref[...], -ai_ref[...], True)

        def rows(t):
            return pl.ds(pl.multiple_of(t * N_CHUNKS, N_CHUNKS), N_CHUNKS)

        def grad(acc, lam, xpr, xpi):
            return acc[0] + xpr * lam[0] + xpi * lam[1], acc[1] + xpr * lam[1] - xpi * lam[0]

        def down(tau, carry):
            lam = write(tau, carry[0])
            t = CHUNK - 2 - tau
            return lam, grad(carry[1], lam, xr_ref[rows(t), :], xi_ref[rows(t), :])

        zero = jnp.zeros((N_CHUNKS, LANES_G), F32)
        lam, acc = lax.fori_loop(0, CHUNK - 1, down, (start, (zero, zero)), unroll=5)
        lam = write(CHUNK - 1, lam)
        last = rows(CHUNK - 1)
        acc = grad(acc, lam, _shift_down(xr_ref[last, :], 1), _shift_down(xi_ref[last, :], 1))
        dar_ref[...] = jnp.sum(acc[0], axis=0, keepdims=True)
        dai_ref[...] = jnp.sum(acc[1], axis=0, keepdims=True)

        l_re, l_im = lr_ref[...].astype(BF16), li_ref[...].astype(BF16)
        du = lax.dot_general(l_re, br_ref[...], NT, preferred_element_type=F32)
        du += lax.dot_general(l_im, bi_ref[...], NT, preferred_element_type=F32)
        dys_ref[...] = du + dys_ref[...] * d_ref[...]
        _to_time_order(dys_ref, du_ref, BF16)
        uv = u_ref[...].astype(BF16)
        dbr_ref[...] = lax.dot_general(uv, l_re, TN, preferred_element_type=F32)
        dbi_ref[...] = lax.dot_general(uv, l_im, TN, preferred_element_type=F32)
        dcr_ref[...] = lax.dot_general(dy, xr_ref[...].astype(BF16), TN, preferred_element_type=F32)
        dci_ref[...] = lax.dot_general(dy, xi_ref[...].astype(BF16), TN, preferred_element_type=F32)

    col, diag, vec = _ssm_specs(layer)
    out_vec = pl.BlockSpec((1, LANES_G), lambda g: (0, g))
    out_blk = pl.BlockSpec((None, 128, LANES_G), lambda g: (g, 0, 0))
    sds = jax.ShapeDtypeStruct
    return pl.pallas_call(
        body, grid=(N_LANE_GROUPS,),
        in_specs=[col(128), col(128), pl.BlockSpec((None, 1, 128), lambda g: (layer, 0, g)),
                  col(LANES_G), col(LANES_G), diag, diag, diag, diag, vec, vec],
        out_specs=[col(128), out_vec, out_vec, out_blk, out_blk, out_blk, out_blk],
        out_shape=[sds((SEQ, WIDTH), BF16)] + [sds((1, SSM_GROUPS * SSM_STATE), F32)] * 2
        + [sds((N_LANE_GROUPS, 128, LANES_G), F32)] * 4,
        scratch_shapes=[pltpu.VMEM((SEQ, LANES_G), F32)] * 2 + [pltpu.VMEM((SEQ, 128), F32)] * 2,
        compiler_params=_cp(), name="ssm_bwd",
    )(dy, u, d, x_re, x_im, mats["b_re"], mats["b_im"], mats["c_re"], mats["c_im_neg"],
      mats["a_re"], mats["a_im"])


_GELU_C = math.sqrt(2.0 / math.pi)


def _gelu(y):
    return 0.5 * y * (1.0 + jnp.tanh(_GELU_C * (y + 0.044715 * (y * y * y))))


def _glu_fwd(y, wglu):
    tt = 512

    def body(y_ref, w_ref, z_ref):
        ys = _gelu(y_ref[...])
        a = jnp.dot(ys.astype(BF16), w_ref[...], preferred_element_type=F32)
        z_ref[...] = (ys * jax.nn.sigmoid(a)).astype(BF16)

    blk = pl.BlockSpec((tt, WIDTH), lambda i: (i, 0))
    return pl.pallas_call(body, grid=(SEQ // tt,), in_specs=[blk, _full((WIDTH, WIDTH))], out_specs=blk,
                          out_shape=jax.ShapeDtypeStruct((SEQ, WIDTH), BF16), compiler_params=_cp(),
                          name="glu_fwd")(y, wglu)


def _glu_bwd(y, wglu, dz, u):
    tt = 512

    def body(y_ref, w_ref, dz_ref, u_ref, dy_ref, ys_ref, da_ref, dd_ref):
        @pl.when(pl.program_id(0) == 0)
        def _():
            dd_ref[...] = jnp.zeros_like(dd_ref)

        yv = y_ref[...]
        t = jnp.tanh(_GELU_C * (yv + 0.044715 * (yv * yv * yv)))
        ys = 0.5 * yv * (1.0 + t)
        ysb = ys.astype(BF16)
        sg = jax.nn.sigmoid(jnp.dot(ysb, w_ref[...], preferred_element_type=F32))
        dz = dz_ref[...].astype(F32)
        da = (dz * ys * sg * (1.0 - sg)).astype(BF16)
        dys = dz * sg + lax.dot_general(da, w_ref[...], NT, preferred_element_type=F32)
        dy = dys * (0.5 * (1.0 + t) + 0.5 * yv * (1.0 - t * t) * _GELU_C * (1.0 + 3 * 0.044715 * (yv * yv)))
        dy_ref[...] = dy
        ys_ref[...] = ysb
        da_ref[...] = da
        dd_ref[...] += jnp.sum(dy * u_ref[...], axis=0, keepdims=True)

    blk = pl.BlockSpec((tt, WIDTH), lambda i: (i, 0))
    return pl.pallas_call(
        body, grid=(SEQ // tt,), in_specs=[blk, _full((WIDTH, WIDTH)), blk, blk],
        out_specs=[blk, blk, blk, _full((1, WIDTH))],
        out_shape=[jax.ShapeDtypeStruct((SEQ, WIDTH), F32)] + [jax.ShapeDtypeStruct((SEQ, WIDTH), BF16)] * 2
        + [jax.ShapeDtypeStruct((1, WIDTH), F32)],
        compiler_params=_cp(), name="glu_bwd")(y, wglu, dz, u)


def _mix_specs(tt, layer):
    row = lambda w: pl.BlockSpec((tt, w), lambda i: (i, 0))
    gate = lambda j: pl.BlockSpec((tt, D_MODEL), lambda i: (i, j))
    wo = lambda j: pl.BlockSpec((D_MODEL, WIDTH), lambda i: (0, j))
    return [row(D_MODEL), row(WIDTH), row(WIDTH), row(WIDTH), gate(0), gate(1), gate(2),
            pl.BlockSpec((None, 1, GATE_W), lambda i: (layer, 0, 0)), wo(0), wo(1), wo(2),
            _full((D_MODEL, D_MODEL))]


def _mix_branches(o_ref, c_ref, z_ref, g_refs, b_ref, wa_ref, wc_ref, ws_ref):
    ys = [lax.dot_general(r[...], w[...], NT, preferred_element_type=F32)
          for r, w in ((o_ref, wa_ref), (c_ref, wc_ref), (z_ref, ws_ref))]
    gates = [jax.nn.sigmoid(g_refs[j][...] + b_ref[:, D_MODEL * j:D_MODEL * (j + 1)]) for j in range(3)]
    return ys, gates


def _mix_fwd(x, o, cv, z, glog, b_gate, layer, wbt, wmix, tie):
    tt = 256

    def body(x_ref, o_ref, c_ref, z_ref, g0, g1, g2, b_ref, wa_ref, wc_ref, ws_ref, wm_ref, tie_ref, x1_ref):
        ys, gates = _mix_branches(o_ref, c_ref, z_ref, (g0, g1, g2), b_ref, wa_ref, wc_ref, ws_ref)
        merged = gates[0] * ys[0] + gates[1] * ys[1] + gates[2] * ys[2]
        x1_ref[...] = x_ref[...] + jnp.dot(merged.astype(BF16), wm_ref[...], preferred_element_type=F32)

    return pl.pallas_call(
        body, grid=(SEQ // tt,), in_specs=_mix_specs(tt, layer) + [ANY],
        out_specs=pl.BlockSpec((tt, D_MODEL), lambda i: (i, 0)),
        out_shape=jax.ShapeDtypeStruct((SEQ, D_MODEL), F32), compiler_params=_cp(), name="mix_fwd",
    )(x, o, cv, z, glog, glog, glog, b_gate, wbt, wbt, wbt, wmix, tie)


def _mix_bwd(dx1, o, cv, z, glog, b_gate, layer, wbt, wmix, tie):
    tt = 256

    def body(dx_ref, o_ref, c_ref, z_ref, g0, g1, g2, b_ref, wa_ref, wc_ref, ws_ref, wm_ref, tie_ref,
             mg_ref, dya_ref, dyc_ref, dys_ref, do_ref, dc_ref, dz_ref, dgl_ref, db_ref):
        @pl.when(pl.program_id(0) == 0)
        def _():
            db_ref[...] = jnp.zeros_like(db_ref)

        ys, gates = _mix_branches(o_ref, c_ref, z_ref, (g0, g1, g2), b_ref, wa_ref, wc_ref, ws_ref)
        mg_ref[...] = (gates[0] * ys[0] + gates[1] * ys[1] + gates[2] * ys[2]).astype(BF16)
        dm = lax.dot_general(dx_ref[...].astype(BF16), wm_ref[...], NT, preferred_element_type=F32)
        for j, (dy_ref, w_ref, d_ref) in enumerate(((dya_ref, wa_ref, do_ref), (dyc_ref, wc_ref, dc_ref),
                                                    (dys_ref, ws_ref, dz_ref))):
            dy = (dm * gates[j]).astype(BF16)
            dy_ref[...] = dy
            d_ref[...] = jnp.dot(dy, w_ref[...], preferred_element_type=F32)
            dgl = dm * ys[j] * gates[j] * (1.0 - gates[j])
            dgl_ref[:, D_MODEL * j:D_MODEL * (j + 1)] = dgl.astype(BF16)
            db_ref[:, D_MODEL * j:D_MODEL * (j + 1)] += jnp.sum(dgl, axis=0, keepdims=True)

    row = lambda w: pl.BlockSpec((tt, w), lambda i: (i, 0))
    sds = jax.ShapeDtypeStruct
    return pl.pallas_call(
        body, grid=(SEQ // tt,), in_specs=_mix_specs(tt, layer) + [ANY],
        out_specs=[row(D_MODEL)] * 4 + [row(WIDTH)] * 3 + [row(GATE_W), _full((1, GATE_W))],
        out_shape=[sds((SEQ, D_MODEL), BF16)] * 4 + [sds((SEQ, WIDTH), F32)] * 3
        + [sds((SEQ, GATE_W), BF16), sds((1, GATE_W), F32)],
        compiler_params=_cp(), name="mix_bwd",
    )(dx1, o, cv, z, glog, glog, glog, b_gate, wbt, wbt, wbt, wmix, tie)


def _ffn_out_fwd(x1, act, wout, tie):
    tt = 512

    def body(x_ref, a_ref, w_ref, tie_ref, o_ref):
        o_ref[...] = x_ref[...] + jnp.dot(a_ref[...], w_ref[...], preferred_element_type=F32)

    row = lambda w: pl.BlockSpec((tt, w), lambda i: (i, 0))
    return pl.pallas_call(
        body, grid=(SEQ // tt,), in_specs=[row(D_MODEL), row(FFN_H), _full((FFN_H, D_MODEL)), ANY],
        out_specs=row(D_MODEL), out_shape=jax.ShapeDtypeStruct((SEQ, D_MODEL), F32),
        compiler_params=_cp(), name="ffn_out_fwd")(x1, act, wout, tie)


def _ffn_out_bwd(dx2, up, silu, dsilu, wout, tie):
    tt = 256

    def body(dx_ref, up_ref, silu_ref, dsilu_ref, w_ref, tie_ref, dgu_ref):
        dact = lax.dot_general(dx_ref[...].astype(BF16), w_ref[...], NT, preferred_element_type=F32).astype(BF16)
        dgu_ref[:, :FFN_H] = dact * up_ref[...] * dsilu_ref[...]
        dgu_ref[:, FFN_H:] = dact * silu_ref[...]

    row = lambda w: pl.BlockSpec((tt, w), lambda i: (i, 0))
    return pl.pallas_call(
        body, grid=(SEQ // tt,),
        in_specs=[row(D_MODEL), row(FFN_H), row(FFN_H), row(FFN_H), _full((FFN_H, D_MODEL)), ANY],
        out_specs=row(2 * FFN_H), out_shape=jax.ShapeDtypeStruct((SEQ, 2 * FFN_H), BF16),
        compiler_params=_cp(), name="ffn_out_bwd")(dx2, up, silu, dsilu, wout, tie)


def _loss_head(x, g, target):
    tt = 256

    def body(x_ref, g_ref, t_ref, loss_ref, dx_ref, dg_ref):
        @pl.when(pl.program_id(0) == 0)
        def _():
            loss_ref[...] = jnp.zeros_like(loss_ref)
            dg_ref[...] = jnp.zeros_like(dg_ref)

        xv = x_ref[...]
        r = lax.rsqrt(jnp.mean(xv * xv, axis=-1, keepdims=True) + NORM_EPS)
        xh = xv * r
        err = xh * g_ref[...] - t_ref[...]
        loss_ref[...] += 0.5 * jnp.sum(jnp.mean(err * err, axis=-1, keepdims=True))
        dy = err * (1.0 / D_MODEL)
        gy = dy * g_ref[...]
        dx_ref[...] = r * (gy - xh * jnp.mean(gy * xh, axis=-1, keepdims=True))
        dg_ref[...] += jnp.sum(dy * xh, axis=0, keepdims=True)

    row = pl.BlockSpec((tt, D_MODEL), lambda i: (i, 0))
    return pl.pallas_call(
        body, grid=(SEQ // tt,), in_specs=[row, _full((1, D_MODEL)), row],
        out_specs=[_full((1, 128)), row, _full((1, D_MODEL))],
        out_shape=[jax.ShapeDtypeStruct((1, 128), F32), jax.ShapeDtypeStruct((SEQ, D_MODEL), F32),
                   jax.ShapeDtypeStruct((1, D_MODEL), F32)],
        compiler_params=_cp(), name="loss_head")(x, g, target)


def _adam_math(g, w, m, v):
    nm = B1 * m + (1.0 - B1) * g
    nv = B2 * v + (1.0 - B2) * (g * g)
    m_hat = nm / (1.0 - B1 ** STEP)
    v_hat = nv / (1.0 - B2 ** STEP)
    return -LR * (m_hat / (jnp.sqrt(v_hat) + ADAM_EPS) + WD * w), nm, nv


def _adamw_small(parts, w, m, v, name):
    def body(p_ref, w_ref, m_ref, v_ref, g_ref, d_ref, nm_ref, nv_ref):
        g = p_ref[0].astype(F32)
        for k in range(1, N_DEV):
            g = g + p_ref[k].astype(F32)
        g_ref[...] = g
        d_ref[...], nm_ref[...], nv_ref[...] = _adam_math(g, w_ref[...], m_ref[...], v_ref[...])

    out_shape = [jax.ShapeDtypeStruct(w.shape, F32)] * 4
    if w.ndim < 3:
        return pl.pallas_call(body, out_shape=out_shape, name=name)(parts, w, m, v)
    rest = w.shape[1:]
    zeros = (0,) * len(rest)
    blk = pl.BlockSpec((None,) + rest, lambda l: (l,) + zeros)
    return pl.pallas_call(
        body, grid=(w.shape[0],),
        in_specs=[pl.BlockSpec((N_DEV, None) + rest, lambda l: (0, l) + zeros), blk, blk, blk],
        out_specs=[blk] * 4, out_shape=out_shape, name=name)(parts, w, m, v)


def _adamw(parts, w, m, v, tr, name, groups=None, fill=None, tie=None):
    n_groups, rows, cols = w.shape
    n_parts = parts.shape[1]
    lo, hi = groups if groups is not None else (0, n_groups)

    def body(p_ref, w_ref, m_ref, v_ref, *rest):
        g_ref, d_ref, nm_ref, nv_ref = rest[-4:]
        g = p_ref[0].astype(F32)
        for k in range(1, n_parts):
            g = g + p_ref[k].astype(F32)
        nm = B1 * m_ref[...] + (1.0 - B1) * g
        nv = B2 * v_ref[...] + (1.0 - B2) * (g * g)
        m_hat = nm / (1.0 - B1 ** STEP)
        v_hat = nv / (1.0 - B2 ** STEP)
        g_ref[...] = g
        d_ref[...] = -LR * (m_hat / (jnp.sqrt(v_hat) + ADAM_EPS) + WD * w_ref[...])
        nm_ref[...] = nm
        nv_ref[...] = nv

    blk = pl.BlockSpec((None, tr, cols), lambda l, i: (l + lo, i, 0))
    p_lo = lo if parts.shape[0] == n_groups else 0
    extra = ([] if fill is None else list(fill)) + ([] if tie is None else [tie])
    return pl.pallas_call(
        body, grid=(hi - lo, rows // tr),
        in_specs=[pl.BlockSpec((None, n_parts, tr, cols), lambda l, i: (l + p_lo, 0, i, 0)), blk, blk, blk]
        + [ANY] * len(extra),
        out_specs=[blk] * 4, out_shape=[jax.ShapeDtypeStruct((n_groups, rows, cols), F32)] * 4,
        input_output_aliases={} if fill is None else {4 + j: j for j in range(4)},
        compiler_params=_cp(), name=name)(parts, w, m, v, *extra)


def _split_start(name, arrays, n_sems, plan, after=None):
    n = len(arrays)
    order = [] if after is None else [after]
    n_in = n + len(order)

    def body(*refs):
        ins, send_sems, recv_sems, token = refs[:n], refs[n_in], refs[n_in + 1], refs[-1]
        for src, dst, k, to in plan(ins)[0]:
            pltpu.make_async_remote_copy(src_ref=src, dst_ref=dst, send_sem=send_sems.at[k], recv_sem=recv_sems.at[k],
                                         device_id=to, device_id_type=MESH_ID).start()
        token[...] = jnp.zeros_like(token)

    outs = pl.pallas_call(
        body, name=name,
        out_shape=(pltpu.SemaphoreType.DMA((n_sems,)), pltpu.SemaphoreType.DMA((n_sems,)),
                   *[pltpu.HBM(a.shape, a.dtype) for a in arrays], jax.ShapeDtypeStruct((8, 128), F32)),
        in_specs=[HBM] * n + [ANY] * len(order),
        out_specs=(SEM, SEM, *[HBM] * n, pl.BlockSpec(memory_space=pltpu.VMEM)),
        input_output_aliases={i: 2 + i for i in range(n)},
        compiler_params=pltpu.CompilerParams(has_side_effects=EFFECT),
    )(*[pltpu.with_memory_space_constraint(a, pltpu.HBM) for a in arrays], *order)
    return outs[0], outs[1], list(outs[2:2 + n]), outs[-1]


def _split_wait(name, arrays, send_sems, recv_sems, after, plan):
    n = len(arrays)
    order = list(after) if isinstance(after, (list, tuple)) else [after]

    def body(*refs):
        ins, s_sems, r_sems = refs[:n], refs[n], refs[n + 1]
        sends, arrivals = plan(ins)
        x, y, c = lax.axis_index("x"), lax.axis_index("y"), lax.axis_index("c")
        for src, dst, k, to in sends:
            pltpu.make_async_remote_copy(src_ref=src, dst_ref=dst, send_sem=s_sems.at[k], recv_sem=r_sems.at[k],
                                         device_id=to, device_id_type=MESH_ID).wait_send()
        for dst, k in arrivals:
            pltpu.make_async_remote_copy(src_ref=dst, dst_ref=dst, send_sem=s_sems.at[k], recv_sem=r_sems.at[k],
                                         device_id=(x, y, c), device_id_type=MESH_ID).wait_recv()

    return pl.pallas_call(
        body, name=name, out_shape=[pltpu.HBM(a.shape, a.dtype) for a in arrays],
        in_specs=[HBM] * n + [SEM, SEM] + [ANY] * len(order), out_specs=[HBM] * n,
        input_output_aliases={i: i for i in range(n)},
        compiler_params=pltpu.CompilerParams(has_side_effects=EFFECT),
    )(*arrays, send_sems, recv_sems, *order)


def _chips():
    x, y, c = lax.axis_index("x"), lax.axis_index("y"), lax.axis_index("c")
    return x, y, c, [(1 - x, y), (x, 1 - y), (1 - x, 1 - y)]


def _plan_gather_chips(refs):
    x, y, c, chips = _chips()
    me = 4 * x + 2 * y + c
    n = len(refs) // 2
    sends, arrivals = [], []
    for i in range(n):
        src, land = refs[i], refs[n + i]
        sends.append((src, land.at[me], 4 * i, (x, y, 1 - c)))
        arrivals.append((land.at[4 * x + 2 * y + 1 - c], 4 * i))
        for j, (px, py) in enumerate(chips):
            sends.append((src, land.at[me], 4 * i + 1 + j, (px, py, c)))
            arrivals.append((land.at[4 * px + 2 * py + c], 4 * i + 1 + j))
    return sends, arrivals


def _plan_gather_pass(refs):
    x, y, c, chips = _chips()
    sends, arrivals = [], []
    for i in range(len(refs)):
        for j, (px, py) in enumerate(chips):
            slot = refs[i].at[4 * px + 2 * py + c]
            sends.append((slot, slot, 4 * i + j, (x, y, 1 - c)))
            arrivals.append((refs[i].at[4 * px + 2 * py + 1 - c], 4 * i + j))
        back = refs[i].at[4 * x + 2 * y + 1 - c]
        sends.append((back, back, 4 * i + 3, (x, y, 1 - c)))
        arrivals.append((refs[i].at[4 * x + 2 * y + c], 4 * i + 3))
    return sends, arrivals


def _plan_scatter_pair(refs):
    x, y, c = lax.axis_index("x"), lax.axis_index("y"), lax.axis_index("c")
    n = len(refs) // 2
    sends, arrivals = [], []
    for i in range(n):
        for q in range(4):
            sends.append((refs[i].at[q, 1 - c], refs[n + i].at[q], 4 * i + q, (x, y, 1 - c)))
            arrivals.append((refs[n + i].at[q], 4 * i + q))
    return sends, arrivals


def _plan_scatter_chips(layer):
    def plan(refs):
        x, y, c, chips = _chips()
        n = len(refs) // 2
        sends, arrivals = [], []
        for i in range(n):
            for j, (px, py) in enumerate(chips):
                sends.append((refs[i].at[2 * px + py], refs[n + i].at[layer, 2 * x + y], 3 * i + j, (px, py, c)))
                arrivals.append((refs[n + i].at[layer, 2 * px + py], 3 * i + j))
        return sends, arrivals

    return plan


def _pair_sum(parts4, from_pair, landing, layer, core, tr, name):
    _, _, rows, cols = parts4.shape

    def body(c_ref, p_ref, s_ref, l_ref, sum_ref, land_ref):
        v = (p_ref[...].astype(F32) + s_ref[...].astype(F32)).astype(BF16)
        sum_ref[...] = v
        land_ref[...] = v

    blk = pl.BlockSpec((None, tr, cols), lambda q, i, c_ref: (q, i, 0))
    return pl.pallas_call(
        body,
        grid_spec=pltpu.PrefetchScalarGridSpec(
            num_scalar_prefetch=1, grid=(4, rows // tr),
            in_specs=[pl.BlockSpec((None, None, tr, cols), lambda q, i, c_ref: (q, c_ref[0], i, 0)), blk, ANY],
            out_specs=[blk, pl.BlockSpec((None, None, tr, cols), lambda q, i, c_ref: (layer, q, i, 0))]),
        out_shape=[jax.ShapeDtypeStruct((4, rows, cols), BF16), jax.ShapeDtypeStruct(landing.shape, BF16)],
        input_output_aliases={3: 1}, compiler_params=_cp(), name=name,
    )(core, parts4, from_pair, landing)


def _travel_layout(t):
    tr = lambda a: jnp.swapaxes(a, 1, 2)
    branch = jnp.concatenate([tr(t["w_attn_o"]), tr(t["w_conv_o"]), tr(t["w_ssm_o"])], axis=2)
    return [tr(t["w_in"]), tr(t["w_ffn_in"]), t["w_ffn_out"], t["w_mix_o"], branch, t["w_ssm_glu"]]


def _native_layout(a):
    tr = lambda x: jnp.swapaxes(x, 1, 2)
    b = a[4]
    return {"w_in": tr(a[0]), "w_ffn_in": tr(a[1]), "w_ffn_out": a[2], "w_mix_o": a[3],
            "w_attn_o": tr(b[:, :, :WIDTH]), "w_conv_o": tr(b[:, :, WIDTH:2 * WIDTH]),
            "w_ssm_o": tr(b[:, :, 2 * WIDTH:]), "w_ssm_glu": a[5]}


def _embed(t):
    eye = jnp.eye(8, dtype=t.dtype)
    t = t.reshape(DEPTH, N_LANE_GROUPS, 8, SSM_GROUP, SSM_STATE)
    return (t[:, :, :, :, None, :] * eye[None, None, :, None, :, None]).reshape(DEPTH, N_LANE_GROUPS, 128, LANES_G)


def _diag_blocks(t):
    t = t.reshape(DEPTH, N_LANE_GROUPS, 8, SSM_GROUP, 8, SSM_STATE)
    return jnp.einsum("lgahap->lgahp", t).reshape(DEPTH, SSM_GROUPS, SSM_GROUP, SSM_STATE)


def _rope_tabs():
    pos = jnp.arange(SEQ, dtype=F32)
    inv_freq = ROPE_THETA ** (-jnp.arange(0, ROT_DIM, 2, dtype=F32) / ROT_DIM)
    ang = pos[:, None] * inv_freq[None, :]
    cos, sin = jnp.cos(ang), jnp.sin(ang)
    one, zero = jnp.ones((SEQ, HEAD_DIM - ROT_DIM), F32), jnp.zeros((SEQ, HEAD_DIM - ROT_DIM), F32)
    z8 = jnp.zeros((SEQ, 8), F32)
    head = lambda *p: jnp.tile(jnp.concatenate(p, axis=1), (1, 2))
    return head(cos, cos, one), head(-sin, z8, zero), head(z8, sin, zero)


def _ssm_mats(sp):
    lr, li, bbr, bbi = _ssm_prep(sp["a_re"], sp["a_im"], sp["log_dt"], sp["bt_re"], sp["bt_im"])
    lanes = SSM_GROUPS * SSM_STATE
    return {
        "a_re": lr.reshape(DEPTH, 1, lanes), "a_im": li.reshape(DEPTH, 1, lanes),
        "b_re": _embed(bbr).astype(BF16), "b_im": _embed(bbi).astype(BF16),
        "c_re": _embed(sp["c_re"]).astype(BF16), "c_im_neg": _embed(-sp["c_im"]).astype(BF16),
    }


def _layer_fwd(x, i, w, rp, mats, tabs, tie, hooks):
    q, kv, cbx, u, glog, cv, h = _rms_mm_in(x, rp["norm_mix"][i], w["win_t"], tabs, rp["conv_w"], i, tie)
    o = _attn_fwd(q, kv, tabs, rp["attn_sinks"][i])
    x_re, x_im, y = _ssm_fwd(u, mats, i, rp["ssm_d"])
    z = _glu_fwd(y, w["wglu"])
    x1 = _mix_fwd(x, o, cv, z, glog, rp["b_gate"], i, w["branch_t"], w["wmix"], hooks["early"](z))
    hooks["pre_ffn"](x1)
    act, up, silu, dsilu, h2 = _rms_mm_ffn(x1, rp["norm_ffn"][i], w["wffn_t"])
    x2 = _ffn_out_fwd(x1, act, w["wout"], hooks["mid"](h2))
    kept = dict(x=x, q=q, kv=kv, cbx=cbx, u=u, glog=glog, h=h, o=o, cv=cv, z=z, y=y,
                x_re=x_re, x_im=x_im, x1=x1, act=act, up=up, silu=silu, dsilu=dsilu, h2=h2)
    return x2, kept


def _layer_bwd(dx2, k, i, w, rp, mats, tabs, tie, hooks):
    dgu = _ffn_out_bwd(dx2, k["up"], k["silu"], k["dsilu"], w["wout"], tie)
    g_wout = _mm_tn(k["act"], dx2, tm=FFN_H // 2, tn=1024, name="mm_tn_ffn_out")
    g_wffn_t = _mm_tn(dgu, k["h2"], tm=FFN_H // 2, tn=1024, name="mm_tn_ffn_in")
    dx1, d_norm_ffn = _mm_rmsbwd([dgu], w["wffn_t"], k["x1"], rp["norm_ffn"][i], dx2, "mm_rmsbwd_ffn")

    mg, dya, dyc, dys, do, dcv, dz, dgl, db_gate = _mix_bwd(
        dx1, k["o"], k["cv"], k["z"], k["glog"], rp["b_gate"], i, w["branch_t"], w["wmix"],
        hooks["mid"]((g_wffn_t, g_wout, d_norm_ffn)))
    g_wmix = _mm_tn(mg, dx1, tm=1024, tn=512, name="mm_tn_mix")
    g_branch_t = _tn_branches((dya, dyc, dys), (k["o"], k["cv"], k["z"]))

    dy, ys16, da16, dd = _glu_bwd(k["y"], w["wglu"], dz, k["u"])
    g_wglu = _mm_tn(ys16, da16, tm=256, tn=512, name="mm_tn_glu")
    du, da_re, da_im, db_re, db_im, dc_re, dc_im = _ssm_bwd(dy, k["x_re"], k["x_im"], k["u"], mats, i, rp["ssm_d"])

    dcb, dcc, dcx, d_conv_w = _conv_bwd(k["cbx"], rp["conv_w"], i, dcv, hooks["late"](du))
    dq, dkv, d_sinks = _attn_bwd(k["q"], k["kv"], tabs, rp["attn_sinks"][i], do)

    pieces = [dq, dkv, dcb, dcc, dcx, du, dgl]
    g_win_t = _tn_pieces(pieces, k["h"])
    dx, d_norm_mix = _mm_rmsbwd(pieces, w["win_t"], k["x"], rp["norm_mix"][i], dx1, "mm_rmsbwd_in")

    grads = [g_win_t, g_wffn_t, g_wout, g_wmix, g_branch_t, g_wglu]
    small = dict(norm_mix=d_norm_mix, b_gate=db_gate, attn_sinks=d_sinks, ssm_d=dd, norm_ffn=d_norm_ffn,
                 conv_w=d_conv_w, da_re=da_re, da_im=da_im, db_re=db_re, db_im=db_im, dc_re=dc_re, dc_im=dc_im)
    return dx, grads, small


def _replicated_grads(sg, sp):
    stack = lambda name: jnp.stack([sg[i][name] for i in range(DEPTH)])
    cots = (stack("da_re").reshape(DEPTH, *_GS), stack("da_im").reshape(DEPTH, *_GS),
            _diag_blocks(stack("db_re")), _diag_blocks(stack("db_im")))
    d_a_re, d_a_im, d_log_dt, d_bt_re, d_bt_im = _ssm_prep_bwd(
        sp["a_re"], sp["a_im"], sp["log_dt"], sp["bt_re"], sp["bt_im"], cots)
    sgrads = {"norm_mix": stack("norm_mix"), "b_gate": stack("b_gate"),
              "attn_sinks": stack("attn_sinks")[:, :, :N_Q_HEADS], "ssm_a_re": d_a_re, "ssm_a_im": d_a_im,
              "ssm_b_re": jnp.swapaxes(d_bt_re, 2, 3), "ssm_b_im": jnp.swapaxes(d_bt_im, 2, 3),
              "ssm_c_re": _diag_blocks(stack("dc_re")), "ssm_c_im": -_diag_blocks(stack("dc_im")),
              "ssm_d": stack("ssm_d"), "ssm_log_dt": d_log_dt, "norm_ffn": stack("norm_ffn")}
    return sgrads, stack("conv_w")[:, :3]


def kernel(x, norm_mix, w_in, b_gate, attn_sinks, w_attn_o, conv_w, w_conv_o, ssm_a_re, ssm_a_im, ssm_b_re, ssm_b_im, ssm_c_re, ssm_c_im, ssm_d, ssm_log_dt, w_ssm_glu, w_ssm_o, w_mix_o, norm_ffn, w_ffn_in, w_ffn_out, norm_final, loss_target, m_norm_mix, m_w_in, m_b_gate, m_attn_sinks, m_w_attn_o, m_conv_w, m_w_conv_o, m_ssm_a_re, m_ssm_a_im, m_ssm_b_re, m_ssm_b_im, m_ssm_c_re, m_ssm_c_im, m_ssm_d, m_ssm_log_dt, m_w_ssm_glu, m_w_ssm_o, m_w_mix_o, m_norm_ffn, m_w_ffn_in, m_w_ffn_out, m_norm_final, v_norm_mix, v_w_in, v_b_gate, v_attn_sinks, v_w_attn_o, v_conv_w, v_w_conv_o, v_ssm_a_re, v_ssm_a_im, v_ssm_b_re, v_ssm_b_im, v_ssm_c_re, v_ssm_c_im, v_ssm_d, v_ssm_log_dt, v_w_ssm_glu, v_w_ssm_o, v_w_mix_o, v_norm_ffn, v_w_ffn_in, v_w_ffn_out, v_norm_final):
    big = {"w": dict(w_in=w_in, w_attn_o=w_attn_o, w_conv_o=w_conv_o, w_ssm_glu=w_ssm_glu, w_ssm_o=w_ssm_o,
                     w_mix_o=w_mix_o, w_ffn_in=w_ffn_in, w_ffn_out=w_ffn_out),
           "m": dict(w_in=m_w_in, w_attn_o=m_w_attn_o, w_conv_o=m_w_conv_o, w_ssm_glu=m_w_ssm_glu,
                     w_ssm_o=m_w_ssm_o, w_mix_o=m_w_mix_o, w_ffn_in=m_w_ffn_in, w_ffn_out=m_w_ffn_out),
           "v": dict(w_in=v_w_in, w_attn_o=v_w_attn_o, w_conv_o=v_w_conv_o, w_ssm_glu=v_w_ssm_glu,
                     w_ssm_o=v_w_ssm_o, w_mix_o=v_w_mix_o, w_ffn_in=v_w_ffn_in, w_ffn_out=v_w_ffn_out)}
    small = {"w": dict(norm_mix=norm_mix, b_gate=b_gate, attn_sinks=attn_sinks, ssm_a_re=ssm_a_re,
                       ssm_a_im=ssm_a_im, ssm_b_re=ssm_b_re, ssm_b_im=ssm_b_im, ssm_c_re=ssm_c_re,
                       ssm_c_im=ssm_c_im, ssm_d=ssm_d, ssm_log_dt=ssm_log_dt, norm_ffn=norm_ffn),
             "m": dict(norm_mix=m_norm_mix, b_gate=m_b_gate, attn_sinks=m_attn_sinks, ssm_a_re=m_ssm_a_re,
                       ssm_a_im=m_ssm_a_im, ssm_b_re=m_ssm_b_re, ssm_b_im=m_ssm_b_im, ssm_c_re=m_ssm_c_re,
                       ssm_c_im=m_ssm_c_im, ssm_d=m_ssm_d, ssm_log_dt=m_ssm_log_dt, norm_ffn=m_norm_ffn),
             "v": dict(norm_mix=v_norm_mix, b_gate=v_b_gate, attn_sinks=v_attn_sinks, ssm_a_re=v_ssm_a_re,
                       ssm_a_im=v_ssm_a_im, ssm_b_re=v_ssm_b_re, ssm_b_im=v_ssm_b_im, ssm_c_re=v_ssm_c_re,
                       ssm_c_im=v_ssm_c_im, ssm_d=v_ssm_d, ssm_log_dt=v_ssm_log_dt, norm_ffn=v_norm_ffn)}
    finals = {"w": norm_final, "m": m_norm_final, "v": v_norm_final}
    convs = {"w": conv_w, "m": m_conv_w, "v": v_conv_w}
    small_out_shapes = {name: a.shape for name, a in small["w"].items()}
    small_out_shapes.update(norm_final=(D_MODEL,), conv_w=(DEPTH, 3, 64))
    small_shapes = dict(small_out_shapes, norm_final=(1, D_MODEL), conv_w=(DEPTH, 3, WIDTH))
    dense = ("ssm_b_re", "ssm_b_im", "ssm_c_re", "ssm_c_im")
    for name in dense:
        small_shapes[name] = (DEPTH, SSM_GROUPS, SSM_GROUP * SSM_STATE)
    small_wmv = {name: [(convs[s] if name == "conv_w" else finals[s] if name == "norm_final" else small[s][name])
                        .reshape((DEPTH, 3, 64) if name == "conv_w" else small_shapes[name]) for s in "wmv"]
                 for name in small_shapes}
    mine = 4 * lax.axis_index("x") + 2 * lax.axis_index("y") + lax.axis_index("c")

    travel = {s: _travel_layout(big[s]) for s in "wmv"}
    stacked16 = list(zip(*[[a[0] for a in _travel_layout({n: w[i:i + 1].astype(BF16) for n, w in big["w"].items()})]
                           for i in range(DEPTH)]))
    rp = {"norm_mix": norm_mix[:, None], "norm_ffn": norm_ffn[:, None], "attn_sinks": attn_sinks[:, None],
          "b_gate": b_gate[:, None], "ssm_d": ssm_d[:, None]}
    sp = {"a_re": ssm_a_re, "a_im": ssm_a_im, "log_dt": ssm_log_dt[:, :, None],
          "bt_re": jnp.swapaxes(ssm_b_re, 2, 3), "bt_im": jnp.swapaxes(ssm_b_im, 2, 3),
          "c_re": ssm_c_re, "c_im": ssm_c_im}
    rows_tile = {"win_t": 368, "wffn_t": 352, "wout": 352, "wmix": 128, "branch_t": 128, "wglu": 64}
    core = lax.axis_index("c").astype(jnp.int32).reshape(1)
    no_tie = jnp.zeros((8, 128), F32)

    def landing_zones(srcs):
        return [lax.empty((N_DEV,) + s.shape, s.dtype) for s in srcs]

    def gather_chips(tag, i, kinds, after, extra=()):
        srcs = [stacked16[j][i] for j in kinds] + list(extra)
        s_sems, r_sems, arrays, token = _split_start(
            f"gather_chips_start_{tag}", srcs + landing_zones(srcs), 4 * len(srcs), _plan_gather_chips, after)
        return (tag, s_sems, r_sems, arrays), token

    def gather_pass(state, after):
        tag, s_sems, r_sems, arrays = state
        arrays = _split_wait(f"gather_chips_wait_{tag}", arrays, s_sems, r_sems, after, _plan_gather_chips)
        n = len(arrays) // 2
        s_sems, r_sems, lands, token = _split_start(
            f"gather_pass_start_{tag}", list(arrays[n:]), 4 * n, _plan_gather_pass)
        return (tag, s_sems, r_sems, lands), token

    def gather_done(state, after, kinds):
        tag, s_sems, r_sems, lands = state
        lands = _split_wait(f"gather_pass_wait_{tag}", lands, s_sems, r_sems, after, _plan_gather_pass)
        named = {KINDS[j][0]: a.reshape(N_DEV * KINDS[j][1], KINDS[j][2]) for a, j in zip(lands, kinds)}
        return named, list(lands[len(kinds):])

    all_kinds, mixer_kinds, ffn_kinds = tuple(range(len(KINDS))), (0, 3, 4, 5), (1, 2)
    no_hooks = {name: (lambda value: no_tie) for name in ("early", "pre_ffn", "mid", "late")}
    state, token = gather_chips("0m", 0, mixer_kinds, None, extra=[jnp.pad(conv_w.reshape(6, 128), ((0, 2), (0, 0)))])
    mats = _ssm_mats(dict(sp, log_dt=sp["log_dt"] + token[0, 0]))
    tabs = _rope_tabs()
    early_work = list(mats.values()) + list(tabs) + [a for name in dense for a in small_wmv[name]]
    early_work += [stacked16[j][0] for j in ffn_kinds] + [stacked16[j][1] for j in mixer_kinds]
    state, _ = gather_pass(state, early_work)
    ffn_state, tie = gather_chips("0f", 0, ffn_kinds, state[3][0])
    w_next, (conv_all,) = gather_done(state, tabs[2], mixer_kinds)
    conv_full = conv_all[:, :6].reshape(N_DEV, DEPTH, 3, 64).transpose(1, 2, 0, 3).reshape(DEPTH, 3, WIDTH)
    rp["conv_w"] = jnp.pad(conv_full, ((0, 0), (0, 5), (0, 0)))

    act = x[0]
    weights, kept = [], []
    for i in range(DEPTH):
        w_i, hooks, held = w_next, dict(no_hooks), {}

        def early(value, ffn_state=ffn_state, held=held):
            held["ffn"], token = gather_pass(ffn_state, value)
            return token

        def pre_ffn(value, w_i=w_i, held=held):
            w_i.update(gather_done(held["ffn"], value, ffn_kinds)[0])

        hooks.update(early=early, pre_ffn=pre_ffn)
        if i + 1 < DEPTH:
            state, tie = gather_chips(f"{i + 1}m", i + 1, mixer_kinds, tie if i == 0 else w_i["win_t"])

            def mid(value, i=i, state=state, held=held):
                held["next"], token = gather_pass(state, value)
                held["next_ffn"], token = gather_chips(f"{i + 1}f", i + 1, ffn_kinds, token)
                return token

            hooks.update(mid=mid)
        act, k = _layer_fwd(act, i, w_i, rp, mats, tabs, tie, hooks)
        if i + 1 < DEPTH:
            w_next, _ = gather_done(held["next"], act, mixer_kinds)
            ffn_state, tie = held["next_ffn"], no_tie
        weights.append(w_i)
        kept.append(k)
    loss_row, dx, d_norm_final = _loss_head(act, norm_final[None], loss_target[0])

    landings = [lax.empty((DEPTH, 4, r, c), BF16) for _, r, c in KINDS]
    landings0 = [lax.empty((1, 4, r, c), BF16) for _, r, c in KINDS]

    def scatter_pair(tag, kinds, grads, after):
        parts4 = [g.reshape(4, 2, KINDS[j][1], KINDS[j][2]) for g, j in zip(grads, kinds)]
        zones = [lax.empty((4, KINDS[j][1], KINDS[j][2]), BF16) for j in kinds]
        s_sems, r_sems, arrays, token = _split_start(
            f"scatter_pair_start_{tag}", parts4 + zones, 4 * len(kinds), _plan_scatter_pair, after)
        return (tag, kinds, s_sems, r_sems, arrays), token

    def scatter_chips(state, lands, slot, after):
        tag, kinds, s_sems, r_sems, arrays = state
        arrays = _split_wait(f"scatter_pair_wait_{tag}", arrays, s_sems, r_sems, after, _plan_scatter_pair)
        n = len(kinds)
        sums, mine_lands = [], []
        for k, j in enumerate(kinds):
            name = KINDS[j][0]
            chip_sum, land = _pair_sum(arrays[k], arrays[n + k], lands[j], slot, core, KINDS[j][1],
                                       f"pair_sum_{name}")
            sums.append(chip_sum)
            mine_lands.append(land)
        s_sems, r_sems, arrays, token = _split_start(
            f"scatter_chips_start_{tag}", sums + mine_lands, 3 * n, _plan_scatter_chips(slot))
        return (tag, kinds, slot, s_sems, r_sems, arrays), token

    def scatter_done(state, lands, after):
        tag, kinds, slot, s_sems, r_sems, arrays = state
        arrays = _split_wait(f"scatter_chips_wait_{tag}", arrays, s_sems, r_sems, after, _plan_scatter_chips(slot))
        lands = list(lands)
        for k, j in enumerate(kinds):
            lands[j] = arrays[len(kinds) + k]
        return lands

    sg = [None] * DEPTH
    pending, tie = None, no_tie
    for i in reversed(range(DEPTH)):
        hooks, held = dict(no_hooks), {}
        if pending is not None:
            def mid(value, i=i, pending=pending, held=held):
                held["chips"], token = scatter_chips(pending, landings, i + 1, value[2])
                if i == 0:
                    held["ffn_pair"], token = scatter_pair("0f", ffn_kinds, value[:2], token)
                return token

            hooks.update(mid=mid)
        if i == 0:
            def late(value, held=held):
                held["ffn_chips"], token = scatter_chips(held["ffn_pair"], landings0, 0, value)
                return token

            hooks.update(late=late)
        dx, grads, sg[i] = _layer_bwd(dx, kept[i], i, weights[i], rp, mats, tabs, tie, hooks)
        if pending is not None:
            landings = scatter_done(held["chips"], landings, dx)
        if i > 0:
            pending, tie = scatter_pair(str(i), all_kinds, grads, dx)
        else:
            pending, _ = scatter_pair("0m", mixer_kinds, [grads[j] for j in mixer_kinds], dx)

    sgrads, conv_grad = _replicated_grads(sg, sp)

    small_names = list(REPLICATED) + ["norm_final", "conv_w"]
    sgrads.update(norm_final=d_norm_final, conv_w=conv_grad)
    small_src = [sgrads[name].reshape(small_shapes[name]).astype(BF16) for name in small_names]
    small_src.append(jnp.broadcast_to(loss_row[:, :1], (8, 128)))
    last, tie = scatter_chips(pending, landings0, 0, small_src[0])
    s_sems, r_sems, arrays, tie = _split_start(
        "gather_small_chips_start", small_src + landing_zones(small_src), 4 * len(small_src), _plan_gather_chips, tie)
    small_state = ("small", s_sems, r_sems, arrays)

    big_out = []
    for j, (name, _, _) in enumerate(KINDS):
        big_out.append(_adamw(landings[j], travel["w"][j], travel["m"][j], travel["v"][j], rows_tile[name],
                              "adamw_late_" + name, groups=(1, DEPTH), tie=tie))
        tie = big_out[-1][3]
    landings0 = scatter_done(held["ffn_chips"], landings0, tie)
    landings0 = scatter_done(last, landings0, tie)
    small_state, _ = gather_pass(small_state, landings0[0])
    big_out = [_adamw(landings0[j], travel["w"][j], travel["m"][j], travel["v"][j], rows_tile[name],
                      "adamw_first_" + name, groups=(0, 1), fill=big_out[j]) for j, (name, _, _) in enumerate(KINDS)]
    big_res = [_native_layout([big_out[j][kind] for j in range(len(KINDS))]) for kind in range(4)]

    _, sparts = gather_done(small_state, big_out[-1][0], ())
    loss = jnp.sum(sparts[-1][:, 0, 0])
    sparts = dict(zip(small_names, sparts))
    sparts["conv_w"] = lax.dynamic_slice_in_dim(sparts["conv_w"], mine * 64, 64, axis=3)
    small_res = {}
    for name in small_names:
        res = _adamw_small(sparts[name], *small_wmv[name], "adamw_" + name)
        small_res[name] = [r.reshape(small_out_shapes[name]) for r in res]

    order = ["norm_mix", "w_in", "b_gate", "attn_sinks", "w_attn_o", "conv_w", "w_conv_o", "ssm_a_re", "ssm_a_im",
             "ssm_b_re", "ssm_b_im", "ssm_c_re", "ssm_c_im", "ssm_d", "ssm_log_dt", "w_ssm_glu", "w_ssm_o",
             "w_mix_o", "norm_ffn", "w_ffn_in", "w_ffn_out", "norm_final"]
    outs = [loss, dx[None]]
    for kind in range(4):
        for name in order:
            outs.append(big_res[kind][name] if name in big_res[kind] else small_res[name][kind])
    return tuple(outs)
```

```python
import math

import jax
import jax.numpy as jnp
from jax import lax
from jax.experimental import pallas as pl
from jax.experimental.pallas import tpu as pltpu

F32 = jnp.float32
BF16 = jnp.bfloat16

N_DEV = 8
DEPTH = 4
SEQ = 2048
D_MODEL = 1024
N_Q_HEADS = 8
HEAD_DIM = 64
ATTN_W = 512
KV_W = 128
BLOCK = 128
N_BLOCKS = SEQ // BLOCK
ROPE_THETA = 500000.0
ROT_DIM = 16
NEG_INF = -1e30
WIDTH = 512
SSM_GROUPS = 32
SSM_GROUP = 16
SSM_STATE = 64
CHUNK = 256
N_CHUNKS = SEQ // CHUNK
GATE_W = 3 * D_MODEL
IN_COLS = 5888
FFN_H = 2816
NORM_EPS = 1e-6
LR, B1, B2, ADAM_EPS, WD, STEP = 0.001, 0.9, 0.999, 1e-08, 0.01, 10

COL_Q, COL_KV, COL_CBX, COL_U, COL_G = 0, 512, 768, 2304, 2816
PIECE_W = (512, 256, 512, 512, 512, 512, 3072)
PIECE_OFF = tuple(sum(PIECE_W[:i]) for i in range(len(PIECE_W)))

KINDS = (("win_t", 736, 1024), ("wffn_t", 704, 1024), ("wout", 352, 1024), ("wmix", 128, 1024),
         ("branch_t", 128, 1536), ("wglu", 64, 512))

REPLICATED = ("norm_mix", "b_gate", "attn_sinks", "ssm_a_re", "ssm_a_im", "ssm_b_re", "ssm_b_im", "ssm_c_re",
              "ssm_c_im", "ssm_d", "ssm_log_dt", "norm_ffn")

VMEM_LIMIT = 56 * 1024 * 1024
NT = (((1,), (1,)), ((), ()))
TN = (((0,), (0,)), ((), ()))
MESH_ID = pl.DeviceIdType.MESH
ANY = pl.BlockSpec(memory_space=pl.ANY)
HBM = pl.BlockSpec(memory_space=pltpu.HBM)
SEM = pl.BlockSpec(memory_space=pltpu.SEMAPHORE)
EFFECT = pltpu.SideEffectType.DATAFLOW_SIDE_EFFECTING


def _cp(**kw):
    return pltpu.CompilerParams(vmem_limit_bytes=VMEM_LIMIT, **kw)


def _full(shape):
    return pl.BlockSpec(shape, lambda *_: (0,) * len(shape))


def _resident(shape):
    return pl.BlockSpec(shape, lambda *_: (0,) * len(shape), pipeline_mode=pl.Buffered(1))


def _mm_tn(a, b, *, tm, tn, name):
    k, m = a.shape
    n = b.shape[1]

    def body(a_ref, b_ref, o_ref):
        o_ref[...] = lax.dot_general(a_ref[...].astype(BF16), b_ref[...].astype(BF16), TN,
                                     preferred_element_type=F32).astype(BF16)

    return pl.pallas_call(
        body, grid=(m // tm, n // tn),
        in_specs=[pl.BlockSpec((k, tm), lambda i, j: (0, i)), pl.BlockSpec((k, tn), lambda i, j: (0, j))],
        out_specs=pl.BlockSpec((tm, tn), lambda i, j: (i, j)),
        out_shape=jax.ShapeDtypeStruct((m, n), BF16), compiler_params=_cp(), name=name)(a, b)


def _rms_rows(xv, g):
    r = lax.rsqrt(jnp.mean(xv * xv, axis=-1, keepdims=True) + NORM_EPS)
    return ((xv * r) * g).astype(BF16)


def _rms_mm_in(x, g, wt, tabs, cw, layer, tie):
    tt = 512
    widths = (3 * WIDTH, WIDTH, GATE_W)
    offs = (COL_CBX, COL_U, COL_G)

    def body(x_ref, g_ref, w_ref, tc_ref, ta_ref, tb_ref, cw_ref, tie_ref,
             q_ref, kv_ref, cbx_ref, u_ref, gl_ref, cv_ref, h_ref, tail_ref):
        @pl.when(pl.program_id(0) == 0)
        def _():
            tail_ref[...] = jnp.zeros_like(tail_ref)

        h = _rms_rows(x_ref[...], g_ref[...])
        h_ref[...] = h
        prod = lax.dot_general(h, w_ref[...], NT, preferred_element_type=F32)
        for ref, o, w in zip((cbx_ref, u_ref, gl_ref), offs, widths):
            ref[...] = prod[:, o:o + w]
        c, a, b = tc_ref[...], ta_ref[...], tb_ref[...]
        for j in range(ATTN_W // 128):
            q_ref[:, 128 * j:128 * (j + 1)] = _rope(prod[:, 128 * j:128 * (j + 1)], c, a, b) * (HEAD_DIM ** -0.5)
        kv_ref[:, :KV_W] = _rope(prod[:, COL_KV:COL_KV + KV_W], c, a, b)
        kv_ref[:, KV_W:] = prod[:, COL_KV + KV_W:COL_CBX]

        row = lax.broadcasted_iota(jnp.int32, (tt, 128), 0)
        for j in range(WIDTH // 128):
            cols = slice(128 * j, 128 * (j + 1))
            cb = prod[:, COL_CBX + 128 * j:COL_CBX + 128 * (j + 1)]
            z = prod[:, COL_CBX + WIDTH + 128 * j:COL_CBX + WIDTH + 128 * (j + 1)] \
                * prod[:, COL_CBX + 2 * WIDTH + 128 * j:COL_CBX + 2 * WIDTH + 128 * (j + 1)]
            before1, before2 = tail_ref[7:8, cols], tail_ref[6:7, cols]
            z1 = jnp.where(row == 0, before1, pltpu.roll(z, 1, axis=0))
            z2 = jnp.where(row == 0, before2, jnp.where(row == 1, before1, pltpu.roll(z, 2, axis=0)))
            s = cw_ref[0:1, cols] * z2 + cw_ref[1:2, cols] * z1 + cw_ref[2:3, cols] * z
            cv_ref[:, cols] = (cb * s).astype(BF16)
            tail_ref[:, cols] = z[tt - 8:, :]

    row_spec = lambda w: pl.BlockSpec((tt, w), lambda i: (i, 0))
    sds = jax.ShapeDtypeStruct
    return pl.pallas_call(
        body, grid=(SEQ // tt,),
        in_specs=[row_spec(D_MODEL), _full((1, D_MODEL)), _resident((IN_COLS, D_MODEL)),
                  row_spec(128), row_spec(128), row_spec(128),
                  pl.BlockSpec((None, 8, WIDTH), lambda i: (layer, 0, 0)), ANY],
        out_specs=[row_spec(ATTN_W), row_spec(2 * KV_W), row_spec(3 * WIDTH), row_spec(WIDTH), row_spec(GATE_W),
                   row_spec(WIDTH), row_spec(D_MODEL)],
        out_shape=[sds((SEQ, ATTN_W), F32), sds((SEQ, 2 * KV_W), F32), sds((SEQ, 3 * WIDTH), F32),
                   sds((SEQ, WIDTH), F32), sds((SEQ, GATE_W), F32), sds((SEQ, WIDTH), BF16),
                   sds((SEQ, D_MODEL), BF16)],
        scratch_shapes=[pltpu.VMEM((8, WIDTH), F32)], compiler_params=_cp(), name="rms_mm_in",
    )(x, g, wt, *tabs, cw, tie)


def _rms_mm_ffn(x, g, wt):
    tt = 256

    def body(x_ref, g_ref, w_ref, act_ref, up_ref, silu_ref, dsilu_ref, h_ref):
        h = _rms_rows(x_ref[...], g_ref[...])
        h_ref[...] = h
        prod = lax.dot_general(h, w_ref[...], NT, preferred_element_type=F32)
        gt, up = prod[:, :FFN_H], prod[:, FFN_H:]
        sg = jax.nn.sigmoid(gt)
        silu = gt * sg
        act_ref[...] = (silu * up).astype(BF16)
        up_ref[...] = up.astype(BF16)
        silu_ref[...] = silu.astype(BF16)
        dsilu_ref[...] = (sg + silu * (1.0 - sg)).astype(BF16)

    row = lambda w: pl.BlockSpec((tt, w), lambda i: (i, 0))
    return pl.pallas_call(
        body, grid=(SEQ // tt,), in_specs=[row(D_MODEL), _full((1, D_MODEL)), _resident((2 * FFN_H, D_MODEL))],
        out_specs=[row(FFN_H)] * 4 + [row(D_MODEL)],
        out_shape=[jax.ShapeDtypeStruct((SEQ, FFN_H), BF16)] * 4 + [jax.ShapeDtypeStruct((SEQ, D_MODEL), BF16)],
        compiler_params=_cp(), name="rms_mm_ffn")(x, g, wt)


def _mm_rmsbwd(pieces, wt, x, g, dres, name):
    tt = 512
    widths = [p.shape[1] for p in pieces]
    offs = [sum(widths[:i]) for i in range(len(widths))]
    n = len(pieces)

    def body(*refs):
        p_refs, (w_ref, x_ref, g_ref, r_ref, dx_ref, dg_ref) = refs[:n], refs[n:]

        @pl.when(pl.program_id(0) == 0)
        def _():
            dg_ref[...] = jnp.zeros_like(dg_ref)

        dh = jnp.zeros((tt, D_MODEL), F32)
        for p_ref, o, w in zip(p_refs, offs, widths):
            dh += jnp.dot(p_ref[...], w_ref[o:o + w, :], preferred_element_type=F32)
        xv = x_ref[...]
        r = lax.rsqrt(jnp.mean(xv * xv, axis=-1, keepdims=True) + NORM_EPS)
        xh = xv * r
        gy = dh * g_ref[...]
        dx_ref[...] = r_ref[...] + r * (gy - xh * jnp.mean(gy * xh, axis=-1, keepdims=True))
        dg_ref[...] += jnp.sum(dh * xh, axis=0, keepdims=True)

    row = lambda w: pl.BlockSpec((tt, w), lambda i: (i, 0))
    return pl.pallas_call(
        body, grid=(SEQ // tt,),
        in_specs=[row(w) for w in widths] + [_resident(wt.shape), row(D_MODEL), _full((1, D_MODEL)), row(D_MODEL)],
        out_specs=[row(D_MODEL), _full((1, D_MODEL))],
        out_shape=[jax.ShapeDtypeStruct((SEQ, D_MODEL), F32), jax.ShapeDtypeStruct((1, D_MODEL), F32)],
        compiler_params=_cp(), name=name)(*pieces, wt, x, g, dres)


def _tn_pieces(pieces, h):
    tn = 512
    narrow, gates = pieces[:-1], pieces[-1]
    n = len(narrow)

    def body(*refs):
        p_refs, (h_ref, o_ref) = refs[:n], refs[n:]
        hv = h_ref[...]
        for p_ref, o, w in zip(p_refs, PIECE_OFF, PIECE_W):
            o_ref[o:o + w, :] = lax.dot_general(p_ref[...], hv, TN, preferred_element_type=F32).astype(BF16)

    top = pl.pallas_call(
        body, grid=(D_MODEL // tn,),
        in_specs=[_full((SEQ, w)) for w in PIECE_W[:-1]] + [pl.BlockSpec((SEQ, tn), lambda j: (0, j))],
        out_specs=pl.BlockSpec((COL_G, tn), lambda j: (0, j)),
        out_shape=jax.ShapeDtypeStruct((IN_COLS, D_MODEL), BF16), compiler_params=_cp(), name="tn_pieces",
    )(*narrow, h)

    tm = 256

    def fill(a_ref, b_ref, prev_ref, o_ref):
        o_ref[...] = lax.dot_general(a_ref[...], b_ref[...], TN, preferred_element_type=F32).astype(BF16)

    return pl.pallas_call(
        fill, grid=(D_MODEL // tn, GATE_W // tm),
        in_specs=[pl.BlockSpec((SEQ, tm), lambda j, i: (0, i)), pl.BlockSpec((SEQ, tn), lambda j, i: (0, j)), ANY],
        out_specs=pl.BlockSpec((tm, tn), lambda j, i: (COL_G // tm + i, j)),
        out_shape=jax.ShapeDtypeStruct((IN_COLS, D_MODEL), BF16), input_output_aliases={2: 0},
        compiler_params=_cp(), name="tn_gates")(gates, h, top)


def _tn_branches(dys, acts):
    tk = 512
    nk = SEQ // tk

    def body(d0, d1, d2, a0, a1, a2, o_ref, acc_ref):
        kk = pl.program_id(0)

        @pl.when(kk == 0)
        def _():
            acc_ref[...] = jnp.zeros_like(acc_ref)

        for j, (d, a) in enumerate(((d0, a0), (d1, a1), (d2, a2))):
            acc_ref[:, WIDTH * j:WIDTH * (j + 1)] += lax.dot_general(d[...], a[...], TN, preferred_element_type=F32)

        @pl.when(kk == nk - 1)
        def _():
            o_ref[...] = acc_ref[...].astype(BF16)

    row = lambda w: pl.BlockSpec((tk, w), lambda kk: (kk, 0))
    return pl.pallas_call(
        body, grid=(nk,), in_specs=[row(D_MODEL)] * 3 + [row(WIDTH)] * 3,
        out_specs=_full((D_MODEL, 3 * WIDTH)), out_shape=jax.ShapeDtypeStruct((D_MODEL, 3 * WIDTH), BF16),
        scratch_shapes=[pltpu.VMEM((D_MODEL, 3 * WIDTH), F32)], compiler_params=_cp(), name="tn_branches",
    )(*dys, *acts)


def _rope(t, c, a, b):
    return t * c + pltpu.roll(t, 120, axis=1) * a + pltpu.roll(t, 8, axis=1) * b


def _rope_t(d, c, a, b):
    return d * c + pltpu.roll(d * a, 8, axis=1) + pltpu.roll(d * b, 120, axis=1)


def _band_sides(band):
    left = lax.broadcasted_iota(jnp.int32, band.shape, 1) < HEAD_DIM
    h0 = jnp.where(left, band, 0.0)
    h1 = jnp.where(left, 0.0, band)
    r0 = pltpu.roll(h0, HEAD_DIM, axis=1)
    r1 = pltpu.roll(h1, HEAD_DIM, axis=1)
    return ((h0.astype(BF16), r0.astype(BF16)), (r1.astype(BF16), h1.astype(BF16)))


def _attn_mask(i):
    qi = lax.broadcasted_iota(jnp.int32, (2 * BLOCK, 2 * BLOCK), 0) % BLOCK
    kj = lax.broadcasted_iota(jnp.int32, (2 * BLOCK, 2 * BLOCK), 1)
    delta = qi + BLOCK - kj
    return (delta >= 0) & (delta < BLOCK) & ((kj >= BLOCK) | (i > 0))


def _attn_probs(s, ok, sink):
    s = jnp.where(ok, s, NEG_INF)
    m = jnp.maximum(jnp.max(s, axis=-1, keepdims=True), sink)
    p = jnp.exp(s - m)
    es = jnp.exp(sink - m)
    inv = 1.0 / (jnp.sum(p, axis=-1, keepdims=True) + es)
    return p * inv, es * inv


def _kv_group(qs, ks, vs, kh, sink_ref):
    q2 = jnp.concatenate([qs[2 * kh], qs[2 * kh + 1]], axis=0)
    kst = jnp.concatenate([ks[kh][0], ks[kh][1]], axis=0)
    vst = jnp.concatenate([vs[kh][0], vs[kh][1]], axis=0)
    top = lax.broadcasted_iota(jnp.int32, (2 * BLOCK, 1), 0) < BLOCK
    sinks = [jnp.where(top, sink_ref[0, 4 * kh + h], sink_ref[0, 4 * kh + 2 + h]) for h in range(2)]
    return q2, kst, vst, sinks


def _attn_load(q_ref, kvc_ref, kvp_ref, tc_ref, ta_ref, tb_ref, pc_ref, pa_ref, pb_ref):
    c, a, b = tc_ref[...], ta_ref[...], tb_ref[...]
    kband = jnp.concatenate([kvp_ref[:, :KV_W], kvc_ref[:, :KV_W]], axis=0)
    vband = jnp.concatenate([kvp_ref[:, KV_W:], kvc_ref[:, KV_W:]], axis=0)
    qs = [q_ref[:, 128 * j:128 * (j + 1)].astype(BF16) for j in range(4)]
    return qs, _band_sides(kband), _band_sides(vband), (c, a, b)


def _attn_specs(clamp):
    cur = lambda i: (clamp(i), 0)
    prev = lambda i: (jnp.maximum(clamp(i) - 1, 0), 0)
    return [
        pl.BlockSpec((BLOCK, ATTN_W), cur), pl.BlockSpec((BLOCK, 2 * KV_W), cur),
        pl.BlockSpec((BLOCK, 2 * KV_W), prev),
        pl.BlockSpec((BLOCK, 128), cur), pl.BlockSpec((BLOCK, 128), cur), pl.BlockSpec((BLOCK, 128), cur),
        pl.BlockSpec((BLOCK, 128), prev), pl.BlockSpec((BLOCK, 128), prev), pl.BlockSpec((BLOCK, 128), prev),
        pl.BlockSpec(memory_space=pltpu.SMEM),
    ]


def _attn_fwd(q, kv, tabs, sinks):
    tc, ta, tb = tabs

    def body(q_ref, kvc_ref, kvp_ref, tc_ref, ta_ref, tb_ref, pc_ref, pa_ref, pb_ref, sink_ref, o_ref):
        i = pl.program_id(0)
        qs, ks, vs, _ = _attn_load(q_ref, kvc_ref, kvp_ref, tc_ref, ta_ref, tb_ref, pc_ref, pa_ref, pb_ref)
        ok = _attn_mask(i)
        for kh in range(2):
            q2, kst, vst, sinks = _kv_group(qs, ks, vs, kh, sink_ref)
            s = lax.dot_general(q2, kst, NT, preferred_element_type=F32)
            pn = [_attn_probs(s[:, 2 * BLOCK * h:2 * BLOCK * (h + 1)], ok, sinks[h])[0].astype(BF16) for h in range(2)]
            o2 = jnp.dot(jnp.concatenate(pn, axis=1), vst, preferred_element_type=F32).astype(BF16)
            for r in range(2):
                j = 2 * kh + r
                o_ref[:, 128 * j:128 * (j + 1)] = o2[BLOCK * r:BLOCK * (r + 1)]

    return pl.pallas_call(
        body, grid=(N_BLOCKS,), in_specs=_attn_specs(lambda i: i),
        out_specs=pl.BlockSpec((BLOCK, ATTN_W), lambda i: (i, 0)),
        out_shape=jax.ShapeDtypeStruct((SEQ, ATTN_W), BF16), compiler_params=_cp(), name="attn_fwd",
    )(q, kv, kv, tc, ta, tb, tc, ta, tb, sinks)


def _attn_bwd(q, kv, tabs, sinks, do):
    tc, ta, tb = tabs
    last = N_BLOCKS - 1
    clamp = lambda i: jnp.minimum(i, last)

    def place(full, side, kh):
        left = lax.broadcasted_iota(jnp.int32, full.shape, 1) < HEAD_DIM
        valid = jnp.where(left, full, 0.0) if side == 0 else jnp.where(left, 0.0, full)
        return valid if side == kh else pltpu.roll(valid, HEAD_DIM, axis=1)

    def body(q_ref, kvc_ref, kvp_ref, tc_ref, ta_ref, tb_ref, pc_ref, pa_ref, pb_ref, sink_ref, do_ref,
             dq_ref, dkv_ref, ds_ref, carry_ref):
        i = pl.program_id(0)

        @pl.when(i == 0)
        def _():
            ds_ref[...] = jnp.zeros_like(ds_ref)
            carry_ref[...] = jnp.zeros_like(carry_ref)

        @pl.when(i > last)
        def _():
            dkv_ref[...] = carry_ref[...].astype(BF16)

        @pl.when(i <= last)
        def _():
            qs, ks, vs, (c, a, b) = _attn_load(q_ref, kvc_ref, kvp_ref, tc_ref, ta_ref, tb_ref,
                                               pc_ref, pa_ref, pb_ref)
            ok = _attn_mask(i)
            dk = jnp.zeros((2 * BLOCK, 128), F32)
            dv = jnp.zeros((2 * BLOCK, 128), F32)
            dsink = jnp.zeros((1, 128), F32)
            lane = lax.broadcasted_iota(jnp.int32, (1, 128), 1)
            for kh in range(2):
                q2, kst, vst, sinks = _kv_group(qs, ks, vs, kh, sink_ref)
                do2 = jnp.concatenate([do_ref[:, 128 * (2 * kh + r):128 * (2 * kh + r + 1)] for r in range(2)],
                                      axis=0).astype(BF16)
                s = lax.dot_general(q2, kst, NT, preferred_element_type=F32)
                dp = lax.dot_general(do2, vst, NT, preferred_element_type=F32)
                pns, dss = [], []
                for h in range(2):
                    cols = slice(2 * BLOCK * h, 2 * BLOCK * (h + 1))
                    pn, ps = _attn_probs(s[:, cols], ok, sinks[h])
                    dr = jnp.sum(pn * dp[:, cols], axis=-1, keepdims=True)
                    pns.append(pn.astype(BF16))
                    dss.append((pn * (dp[:, cols] - dr)).astype(BF16))
                    for r in range(2):
                        part = -jnp.sum((ps * dr)[BLOCK * r:BLOCK * (r + 1)])
                        dsink += jnp.where(lane == 4 * kh + 2 * r + h, part, 0.0)
                ds2, pn2 = jnp.concatenate(dss, axis=1), jnp.concatenate(pns, axis=1)
                dq2 = jnp.dot(ds2, kst, preferred_element_type=F32) * (HEAD_DIM ** -0.5)
                dk2 = lax.dot_general(ds2, q2, TN, preferred_element_type=F32)
                dv2 = lax.dot_general(pn2, do2, TN, preferred_element_type=F32)
                for h in range(2):
                    dk += place(dk2[2 * BLOCK * h:2 * BLOCK * (h + 1)], h, kh)
                    dv += place(dv2[2 * BLOCK * h:2 * BLOCK * (h + 1)], h, kh)
                for r in range(2):
                    j = 2 * kh + r
                    dq_ref[:, 128 * j:128 * (j + 1)] = _rope_t(dq2[BLOCK * r:BLOCK * (r + 1)], c, a, b).astype(BF16)
            ds_ref[...] += dsink
            dk_prev = _rope_t(dk[:BLOCK], pc_ref[...], pa_ref[...], pb_ref[...])
            dk_cur = _rope_t(dk[BLOCK:], c, a, b)
            prev = jnp.concatenate([dk_prev, dv[:BLOCK]], axis=1)
            dkv_ref[...] = (carry_ref[...] + prev).astype(BF16)
            carry_ref[...] = jnp.concatenate([dk_cur, dv[BLOCK:]], axis=1)

    return pl.pallas_call(
        body, grid=(N_BLOCKS + 1,),
        in_specs=_attn_specs(clamp) + [pl.BlockSpec((BLOCK, ATTN_W), lambda i: (clamp(i), 0))],
        out_specs=[pl.BlockSpec((BLOCK, ATTN_W), lambda i: (clamp(i), 0)),
                   pl.BlockSpec((BLOCK, 2 * KV_W), lambda i: (jnp.maximum(i - 1, 0), 0)),
                   pl.BlockSpec((1, 128), lambda i: (0, 0))],
        out_shape=[jax.ShapeDtypeStruct((SEQ, ATTN_W), BF16), jax.ShapeDtypeStruct((SEQ, 2 * KV_W), BF16),
                   jax.ShapeDtypeStruct((1, 128), F32)],
        scratch_shapes=[pltpu.VMEM((BLOCK, 2 * KV_W), F32)], compiler_params=_cp(), name="attn_bwd",
    )(q, kv, kv, tc, ta, tb, tc, ta, tb, sinks, do)


def _shift_down(z, k):
    row = lax.broadcasted_iota(jnp.int32, z.shape, 0)
    return jnp.where(row < k, 0.0, pltpu.roll(z, k, axis=0))


def _shift_up(z, k):
    n = z.shape[0]
    row = lax.broadcasted_iota(jnp.int32, z.shape, 0)
    return jnp.where(row >= n - k, 0.0, pltpu.roll(z, n - k, axis=0))


def _conv_specs():
    nb = WIDTH // 128
    return [pl.BlockSpec((SEQ, 128), lambda j: (0, j)), pl.BlockSpec((SEQ, 128), lambda j: (0, nb + j)),
            pl.BlockSpec((SEQ, 128), lambda j: (0, 2 * nb + j)), pl.BlockSpec((None, 8, 128), lambda j: (0, 0, j))]


def _conv_bwd(cbx, cw, layer, dout, tie):
    def body(cb_ref, cc_ref, cx_ref, w_ref, do_ref, tie_ref, dcb_ref, dcc_ref, dcx_ref, dw_ref):
        cc, cx = cc_ref[...], cx_ref[...]
        z = cc * cx
        z1, z2 = _shift_down(z, 1), _shift_down(z, 2)
        w0, w1, w2 = w_ref[0:1, :], w_ref[1:2, :], w_ref[2:3, :]
        dout = do_ref[...]
        ds = dout * cb_ref[...]
        dcb_ref[...] = (dout * (w0 * z2 + w1 * z1 + w2 * z)).astype(BF16)
        dz = w2 * ds + w1 * _shift_up(ds, 1) + w0 * _shift_up(ds, 2)
        dcc_ref[...] = (dz * cx).astype(BF16)
        dcx_ref[...] = (dz * cc).astype(BF16)
        rows = [jnp.sum(ds * zz, axis=0, keepdims=True) for zz in (z2, z1, z)]
        dw_ref[...] = jnp.concatenate(rows + [jnp.zeros((5, 128), F32)], axis=0)

    col = lambda j: (0, j)
    specs = _conv_specs()
    specs[3] = pl.BlockSpec((None, 8, 128), lambda j: (layer, 0, j))
    return pl.pallas_call(
        body, grid=(WIDTH // 128,), in_specs=specs + [pl.BlockSpec((SEQ, 128), col), ANY],
        out_specs=[pl.BlockSpec((SEQ, 128), col), pl.BlockSpec((SEQ, 128), col), pl.BlockSpec((SEQ, 128), col),
                   pl.BlockSpec((8, 128), col)],
        out_shape=[jax.ShapeDtypeStruct((SEQ, WIDTH), BF16)] * 3 + [jax.ShapeDtypeStruct((8, WIDTH), F32)],
        compiler_params=_cp(), name="conv_bwd",
    )(cbx, cbx, cbx, cw, dout, tie)


def _ssm_prep_math(a_re, a_im, log_dt, bt_re, bt_im):
    dt = jnp.exp(log_dt)
    er = jnp.exp(a_re * dt)
    lr = er * jnp.cos(a_im * dt)
    li = er * jnp.sin(a_im * dt)
    n2 = a_re * a_re + a_im * a_im
    cr = ((lr - 1.0) * a_re + li * a_im) / n2
    ci = (li * a_re - (lr - 1.0) * a_im) / n2
    cr3, ci3 = cr[:, None, :], ci[:, None, :]
    return lr, li, cr3 * bt_re - ci3 * bt_im, cr3 * bt_im + ci3 * bt_re


_GS = (SSM_GROUPS, SSM_STATE)
_GHS = (SSM_GROUPS, SSM_GROUP, SSM_STATE)


def _layered(shape):
    return pl.BlockSpec((None,) + shape, lambda l: (l,) + (0,) * len(shape))


def _ssm_prep(a_re, a_im, log_dt, bt_re, bt_im):
    def body(ar, ai, ld, br, bi, o0, o1, o2, o3):
        outs = _ssm_prep_math(ar[...], ai[...], ld[...], br[...], bi[...])
        for o, v in zip((o0, o1, o2, o3), outs):
            o[...] = v

    shapes = [_GS, _GS, _GHS, _GHS]
    return pl.pallas_call(
        body, grid=(DEPTH,), in_specs=[_layered(s) for s in (_GS, _GS, (SSM_GROUPS, 1), _GHS, _GHS)],
        out_specs=[_layered(s) for s in shapes],
        out_shape=[jax.ShapeDtypeStruct((DEPTH,) + s, F32) for s in shapes],
        name="ssm_prep")(a_re, a_im, log_dt, bt_re, bt_im)


def _ssm_prep_bwd(a_re, a_im, log_dt, bt_re, bt_im, cots):
    def body(ar, ai, ld, br, bi, c0, c1, c2, c3, o0, o1, o2, o3, o4):
        _, vjp = jax.vjp(_ssm_prep_math, ar[...], ai[...], ld[...], br[...], bi[...])
        for o, v in zip((o0, o1, o2, o3, o4), vjp((c0[...], c1[...], c2[...], c3[...]))):
            o[...] = v

    ins = (_GS, _GS, (SSM_GROUPS, 1), _GHS, _GHS)
    return pl.pallas_call(
        body, grid=(DEPTH,), in_specs=[_layered(s) for s in ins + (_GS, _GS, _GHS, _GHS)],
        out_specs=[_layered(s) for s in ins],
        out_shape=[jax.ShapeDtypeStruct((DEPTH,) + s, F32) for s in ins],
        name="ssm_prep_bwd")(a_re, a_im, log_dt, bt_re, bt_im, *cots)


LANES_G = 512
N_LANE_GROUPS = SSM_GROUPS * SSM_STATE // LANES_G


def _scan_in_place(xr_ref, xi_ref, ar, ai, reverse):
    shape = (N_CHUNKS, xr_ref.shape[1])
    ar, ai = jnp.broadcast_to(ar, shape), jnp.broadcast_to(ai, shape)

    def rows(tau):
        t = (CHUNK - 1 - tau) if reverse else tau
        return pl.ds(pl.multiple_of(t * N_CHUNKS, N_CHUNKS), N_CHUNKS)

    def step(tau, carry):
        sr, si = carry
        return ar * sr - ai * si + xr_ref[rows(tau), :], ar * si + ai * sr + xi_ref[rows(tau), :]

    zero = jnp.zeros(shape, F32)
    er, ei = lax.fori_loop(0, CHUNK, step, (zero, zero), unroll=8)
    qr, qi = ar, ai
    for _ in range(8):
        qr, qi = qr * qr - qi * qi, 2.0 * qr * qi
    shift = _shift_up if reverse else _shift_down
    for k in (1, 2, 4):
        sr, si = shift(er, k), shift(ei, k)
        er, ei = er + qr * sr - qi * si, ei + qr * si + qi * sr
        qr, qi = qr * qr - qi * qi, 2.0 * qr * qi
    start = (shift(er, 1), shift(ei, 1))

    def write(tau, carry):
        sr, si = step(tau, carry)
        xr_ref[rows(tau), :] = sr
        xi_ref[rows(tau), :] = si
        return sr, si

    return write, start


def _ssm_specs(layer):
    col = lambda w: pl.BlockSpec((SEQ, w), lambda g: (0, g))
    diag = pl.BlockSpec((None, None, 128, LANES_G), lambda g: (layer, g, 0, 0))
    vec = pl.BlockSpec((None, 1, LANES_G), lambda g: (layer, 0, g))
    return col, diag, vec


def _to_scan_order(src_ref, dst_ref):
    def move(tau, _):
        dst_ref[pl.ds(pl.multiple_of(tau * N_CHUNKS, N_CHUNKS), N_CHUNKS), :] = src_ref[pl.ds(tau, N_CHUNKS, stride=CHUNK), :]
        return 0

    lax.fori_loop(0, CHUNK, move, 0, unroll=8)


def _to_time_order(src_ref, dst_ref, dtype):
    for j in range(N_CHUNKS):
        dst_ref[pl.ds(j * CHUNK, CHUNK), :] = src_ref[pl.ds(j, CHUNK, stride=N_CHUNKS), :].astype(dtype)


def _ssm_fwd(u, mats, layer, d):
    def body(u_ref, d_ref, br_ref, bi_ref, cr_ref, ci_ref, ar_ref, ai_ref, xr_ref, xi_ref, y_ref, us_ref):
        _to_scan_order(u_ref, us_ref)
        uv = us_ref[...].astype(BF16)
        xr_ref[...] = jnp.dot(uv, br_ref[...], preferred_element_type=F32)
        xi_ref[...] = jnp.dot(uv, bi_ref[...], preferred_element_type=F32)
        write, start = _scan_in_place(xr_ref, xi_ref, ar_ref[...], ai_ref[...], False)
        lax.fori_loop(0, CHUNK, write, start, unroll=8)
        y = lax.dot_general(xr_ref[...].astype(BF16), cr_ref[...], NT, preferred_element_type=F32)
        y += lax.dot_general(xi_ref[...].astype(BF16), ci_ref[...], NT, preferred_element_type=F32)
        us_ref[...] = y + d_ref[...] * us_ref[...]
        _to_time_order(us_ref, y_ref, F32)

    col, diag, vec = _ssm_specs(layer)
    return pl.pallas_call(
        body, grid=(N_LANE_GROUPS,),
        in_specs=[col(128), pl.BlockSpec((None, 1, 128), lambda g: (layer, 0, g)),
                  diag, diag, diag, diag, vec, vec],
        out_specs=[col(LANES_G), col(LANES_G), col(128)],
        out_shape=[jax.ShapeDtypeStruct((SEQ, SSM_GROUPS * SSM_STATE), F32)] * 2
        + [jax.ShapeDtypeStruct((SEQ, WIDTH), F32)],
        scratch_shapes=[pltpu.VMEM((SEQ, 128), F32)], compiler_params=_cp(), name="ssm_fwd",
    )(u, d, mats["b_re"], mats["b_im"], mats["c_re"], mats["c_im_neg"], mats["a_re"], mats["a_im"])


def _ssm_bwd(dy, x_re, x_im, u, mats, layer, d):
    def body(dyt_ref, ut_ref, d_ref, xr_ref, xi_ref, br_ref, bi_ref, cr_ref, ci_ref, ar_ref, ai_ref,
             du_ref, dar_ref, dai_ref, dbr_ref, dbi_ref, dcr_ref, dci_ref, lr_ref, li_ref, dys_ref, u_ref):
        _to_scan_order(dyt_ref, dys_ref)
        _to_scan_order(ut_ref, u_ref)
        dy = dys_ref[...].astype(BF16)
        lr_ref[...] = jnp.dot(dy, cr_ref[...], preferred_element_type=F32)
        li_ref[...] = jnp.dot(dy, ci_ref[...], preferred_element_type=F32)
        write, start = _scan_in_place(lr_ref, li_ref, ar_ref[...], -ai_ref[...], True)

        def rows(t):
            return pl.ds(pl.multiple_of(t * N_CHUNKS, N_CHUNKS), N_CHUNKS)

        def grad(acc, lam, xpr, xpi):
            return acc[0] + xpr * lam[0] + xpi * lam[1], acc[1] + xpr * lam[1] - xpi * lam[0]

        def down(tau, carry):
            lam = write(tau, carry[0])
            t = CHUNK - 2 - tau
            return lam, grad(carry[1], lam, xr_ref[rows(t), :], xi_ref[rows(t), :])

        zero = jnp.zeros((N_CHUNKS, LANES_G), F32)
        lam, acc = lax.fori_loop(0, CHUNK - 1, down, (start, (zero, zero)), unroll=5)
        lam = write(CHUNK - 1, lam)
        last = rows(CHUNK - 1)
        acc = grad(acc, lam, _shift_down(xr_ref[last, :], 1), _shift_down(xi_ref[last, :], 1))
        dar_ref[...] = jnp.sum(acc[0], axis=0, keepdims=True)
        dai_ref[...] = jnp.sum(acc[1], axis=0, keepdims=True)

        l_re, l_im = lr_ref[...].astype(BF16), li_ref[...].astype(BF16)
        du = lax.dot_general(l_re, br_ref[...], NT, preferred_element_type=F32)
        du += lax.dot_general(l_im, bi_ref[...], NT, preferred_element_type=F32)
        dys_ref[...] = du + dys_ref[...] * d_ref[...]
        _to_time_order(dys_ref, du_ref, BF16)
        uv = u_ref[...].astype(BF16)
        dbr_ref[...] = lax.dot_general(uv, l_re, TN, preferred_element_type=F32)
        dbi_ref[...] = lax.dot_general(uv, l_im, TN, preferred_element_type=F32)
        dcr_ref[...] = lax.dot_general(dy, xr_ref[...].astype(BF16), TN, preferred_element_type=F32)
        dci_ref[...] = lax.dot_general(dy, xi_ref[...].astype(BF16), TN, preferred_element_type=F32)

    col, diag, vec = _ssm_specs(layer)
    out_vec = pl.BlockSpec((1, LANES_G), lambda g: (0, g))
    out_blk = pl.BlockSpec((None, 128, LANES_G), lambda g: (g, 0, 0))
    sds = jax.ShapeDtypeStruct
    return pl.pallas_call(
        body, grid=(N_LANE_GROUPS,),
        in_specs=[col(128), col(128), pl.BlockSpec((None, 1, 128), lambda g: (layer, 0, g)),
                  col(LANES_G), col(LANES_G), diag, diag, diag, diag, vec, vec],
        out_specs=[col(128), out_vec, out_vec, out_blk, out_blk, out_blk, out_blk],
        out_shape=[sds((SEQ, WIDTH), BF16)] + [sds((1, SSM_GROUPS * SSM_STATE), F32)] * 2
        + [sds((N_LANE_GROUPS, 128, LANES_G), F32)] * 4,
        scratch_shapes=[pltpu.VMEM((SEQ, LANES_G), F32)] * 2 + [pltpu.VMEM((SEQ, 128), F32)] * 2,
        compiler_params=_cp(), name="ssm_bwd",
    )(dy, u, d, x_re, x_im, mats["b_re"], mats["b_im"], mats["c_re"], mats["c_im_neg"],
      mats["a_re"], mats["a_im"])


_GELU_C = math.sqrt(2.0 / math.pi)


def _gelu(y):
    return 0.5 * y * (1.0 + jnp.tanh(_GELU_C * (y + 0.044715 * (y * y * y))))


def _glu_fwd(y, wglu):
    tt = 512

    def body(y_ref, w_ref, z_ref):
        ys = _gelu(y_ref[...])
        a = jnp.dot(ys.astype(BF16), w_ref[...], preferred_element_type=F32)
        z_ref[...] = (ys * jax.nn.sigmoid(a)).astype(BF16)

    blk = pl.BlockSpec((tt, WIDTH), lambda i: (i, 0))
    return pl.pallas_call(body, grid=(SEQ // tt,), in_specs=[blk, _full((WIDTH, WIDTH))], out_specs=blk,
                          out_shape=jax.ShapeDtypeStruct((SEQ, WIDTH), BF16), compiler_params=_cp(),
                          name="glu_fwd")(y, wglu)


def _glu_bwd(y, wglu, dz, u):
    tt = 512

    def body(y_ref, w_ref, dz_ref, u_ref, dy_ref, ys_ref, da_ref, dd_ref):
        @pl.when(pl.program_id(0) == 0)
        def _():
            dd_ref[...] = jnp.zeros_like(dd_ref)

        yv = y_ref[...]
        t = jnp.tanh(_GELU_C * (yv + 0.044715 * (yv * yv * yv)))
        ys = 0.5 * yv * (1.0 + t)
        ysb = ys.astype(BF16)
        sg = jax.nn.sigmoid(jnp.dot(ysb, w_ref[...], preferred_element_type=F32))
        dz = dz_ref[...].astype(F32)
        da = (dz * ys * sg * (1.0 - sg)).astype(BF16)
        dys = dz * sg + lax.dot_general(da, w_ref[...], NT, preferred_element_type=F32)
        dy = dys * (0.5 * (1.0 + t) + 0.5 * yv * (1.0 - t * t) * _GELU_C * (1.0 + 3 * 0.044715 * (yv * yv)))
        dy_ref[...] = dy
        ys_ref[...] = ysb
        da_ref[...] = da
        dd_ref[...] += jnp.sum(dy * u_ref[...], axis=0, keepdims=True)

    blk = pl.BlockSpec((tt, WIDTH), lambda i: (i, 0))
    return pl.pallas_call(
        body, grid=(SEQ // tt,), in_specs=[blk, _full((WIDTH, WIDTH)), blk, blk],
        out_specs=[blk, blk, blk, _full((1, WIDTH))],
        out_shape=[jax.ShapeDtypeStruct((SEQ, WIDTH), F32)] + [jax.ShapeDtypeStruct((SEQ, WIDTH), BF16)] * 2
        + [jax.ShapeDtypeStruct((1, WIDTH), F32)],
        compiler_params=_cp(), name="glu_bwd")(y, wglu, dz, u)


def _mix_specs(tt, layer):
    row = lambda w: pl.BlockSpec((tt, w), lambda i: (i, 0))
    gate = lambda j: pl.BlockSpec((tt, D_MODEL), lambda i: (i, j))
    wo = lambda j: pl.BlockSpec((D_MODEL, WIDTH), lambda i: (0, j))
    return [row(D_MODEL), row(WIDTH), row(WIDTH), row(WIDTH), gate(0), gate(1), gate(2),
            pl.BlockSpec((None, 1, GATE_W), lambda i: (layer, 0, 0)), wo(0), wo(1), wo(2),
            _full((D_MODEL, D_MODEL))]


def _mix_branches(o_ref, c_ref, z_ref, g_refs, b_ref, wa_ref, wc_ref, ws_ref):
    ys = [lax.dot_general(r[...], w[...], NT, preferred_element_type=F32)
          for r, w in ((o_ref, wa_ref), (c_ref, wc_ref), (z_ref, ws_ref))]
    gates = [jax.nn.sigmoid(g_refs[j][...] + b_ref[:, D_MODEL * j:D_MODEL * (j + 1)]) for j in range(3)]
    return ys, gates


def _mix_fwd(x, o, cv, z, glog, b_gate, layer, wbt, wmix, tie):
    tt = 256

    def body(x_ref, o_ref, c_ref, z_ref, g0, g1, g2, b_ref, wa_ref, wc_ref, ws_ref, wm_ref, tie_ref, x1_ref):
        ys, gates = _mix_branches(o_ref, c_ref, z_ref, (g0, g1, g2), b_ref, wa_ref, wc_ref, ws_ref)
        merged = gates[0] * ys[0] + gates[1] * ys[1] + gates[2] * ys[2]
        x1_ref[...] = x_ref[...] + jnp.dot(merged.astype(BF16), wm_ref[...], preferred_element_type=F32)

    return pl.pallas_call(
        body, grid=(SEQ // tt,), in_specs=_mix_specs(tt, layer) + [ANY],
        out_specs=pl.BlockSpec((tt, D_MODEL), lambda i: (i, 0)),
        out_shape=jax.ShapeDtypeStruct((SEQ, D_MODEL), F32), compiler_params=_cp(), name="mix_fwd",
    )(x, o, cv, z, glog, glog, glog, b_gate, wbt, wbt, wbt, wmix, tie)


def _mix_bwd(dx1, o, cv, z, glog, b_gate, layer, wbt, wmix, tie):
    tt = 256

    def body(dx_ref, o_ref, c_ref, z_ref, g0, g1, g2, b_ref, wa_ref, wc_ref, ws_ref, wm_ref, tie_ref,
             mg_ref, dya_ref, dyc_ref, dys_ref, do_ref, dc_ref, dz_ref, dgl_ref, db_ref):
        @pl.when(pl.program_id(0) == 0)
        def _():
            db_ref[...] = jnp.zeros_like(db_ref)

        ys, gates = _mix_branches(o_ref, c_ref, z_ref, (g0, g1, g2), b_ref, wa_ref, wc_ref, ws_ref)
        mg_ref[...] = (gates[0] * ys[0] + gates[1] * ys[1] + gates[2] * ys[2]).astype(BF16)
        dm = lax.dot_general(dx_ref[...].astype(BF16), wm_ref[...], NT, preferred_element_type=F32)
        for j, (dy_ref, w_ref, d_ref) in enumerate(((dya_ref, wa_ref, do_ref), (dyc_ref, wc_ref, dc_ref),
                                                    (dys_ref, ws_ref, dz_ref))):
            dy = (dm * gates[j]).astype(BF16)
            dy_ref[...] = dy
            d_ref[...] = jnp.dot(dy, w_ref[...], preferred_element_type=F32)
            dgl = dm * ys[j] * gates[j] * (1.0 - gates[j])
            dgl_ref[:, D_MODEL * j:D_MODEL * (j + 1)] = dgl.astype(BF16)
            db_ref[:, D_MODEL * j:D_MODEL * (j + 1)] += jnp.sum(dgl, axis=0, keepdims=True)

    row = lambda w: pl.BlockSpec((tt, w), lambda i: (i, 0))
    sds = jax.ShapeDtypeStruct
    return pl.pallas_call(
        body, grid=(SEQ // tt,), in_specs=_mix_specs(tt, layer) + [ANY],
        out_specs=[row(D_MODEL)] * 4 + [row(WIDTH)] * 3 + [row(GATE_W), _full((1, GATE_W))],
        out_shape=[sds((SEQ, D_MODEL), BF16)] * 4 + [sds((SEQ, WIDTH), F32)] * 3
        + [sds((SEQ, GATE_W), BF16), sds((1, GATE_W), F32)],
        compiler_params=_cp(), name="mix_bwd",
    )(dx1, o, cv, z, glog, glog, glog, b_gate, wbt, wbt, wbt, wmix, tie)


def _ffn_out_fwd(x1, act, wout, tie):
    tt = 512

    def body(x_ref, a_ref, w_ref, tie_ref, o_ref):
        o_ref[...] = x_ref[...] + jnp.dot(a_ref[...], w_ref[...], preferred_element_type=F32)

    row = lambda w: pl.BlockSpec((tt, w), lambda i: (i, 0))
    return pl.pallas_call(
        body, grid=(SEQ // tt,), in_specs=[row(D_MODEL), row(FFN_H), _full((FFN_H, D_MODEL)), ANY],
        out_specs=row(D_MODEL), out_shape=jax.ShapeDtypeStruct((SEQ, D_MODEL), F32),
        compiler_params=_cp(), name="ffn_out_fwd")(x1, act, wout, tie)


def _ffn_out_bwd(dx2, up, silu, dsilu, wout, tie):
    tt = 256

    def body(dx_ref, up_ref, silu_ref, dsilu_ref, w_ref, tie_ref, dgu_ref):
        dact = lax.dot_general(dx_ref[...].astype(BF16), w_ref[...], NT, preferred_element_type=F32).astype(BF16)
        dgu_ref[:, :FFN_H] = dact * up_ref[...] * dsilu_ref[...]
        dgu_ref[:, FFN_H:] = dact * silu_ref[...]

    row = lambda w: pl.BlockSpec((tt, w), lambda i: (i, 0))
    return pl.pallas_call(
        body, grid=(SEQ // tt,),
        in_specs=[row(D_MODEL), row(FFN_H), row(FFN_H), row(FFN_H), _full((FFN_H, D_MODEL)), ANY],
        out_specs=row(2 * FFN_H), out_shape=jax.ShapeDtypeStruct((SEQ, 2 * FFN_H), BF16),
        compiler_params=_cp(), name="ffn_out_bwd")(dx2, up, silu, dsilu, wout, tie)


def _loss_head(x, g, target):
    tt = 256

    def body(x_ref, g_ref, t_ref, loss_ref, dx_ref, dg_ref):
        @pl.when(pl.program_id(0) == 0)
        def _():
            loss_ref[...] = jnp.zeros_like(loss_ref)
            dg_ref[...] = jnp.zeros_like(dg_ref)

        xv = x_ref[...]
        r = lax.rsqrt(jnp.mean(xv * xv, axis=-1, keepdims=True) + NORM_EPS)
        xh = xv * r
        err = xh * g_ref[...] - t_ref[...]
        loss_ref[...] += 0.5 * jnp.sum(jnp.mean(err * err, axis=-1, keepdims=True))
        dy = err * (1.0 / D_MODEL)
        gy = dy * g_ref[...]
        dx_ref[...] = r * (gy - xh * jnp.mean(gy * xh, axis=-1, keepdims=True))
        dg_ref[...] += jnp.sum(dy * xh, axis=0, keepdims=True)

    row = pl.BlockSpec((tt, D_MODEL), lambda i: (i, 0))
    return pl.pallas_call(
        body, grid=(SEQ // tt,), in_specs=[row, _full((1, D_MODEL)), row],
        out_specs=[_full((1, 128)), row, _full((1, D_MODEL))],
        out_shape=[jax.ShapeDtypeStruct((1, 128), F32), jax.ShapeDtypeStruct((SEQ, D_MODEL), F32),
                   jax.ShapeDtypeStruct((1, D_MODEL), F32)],
        compiler_params=_cp(), name="loss_head")(x, g, target)


def _adam_math(g, w, m, v):
    nm = B1 * m + (1.0 - B1) * g
    nv = B2 * v + (1.0 - B2) * (g * g)
    m_hat = nm / (1.0 - B1 ** STEP)
    v_hat = nv / (1.0 - B2 ** STEP)
    return -LR * (m_hat / (jnp.sqrt(v_hat) + ADAM_EPS) + WD * w), nm, nv


def _adamw_small(parts, w, m, v, name):
    def body(p_ref, w_ref, m_ref, v_ref, g_ref, d_ref, nm_ref, nv_ref):
        g = p_ref[0].astype(F32)
        for k in range(1, N_DEV):
            g = g + p_ref[k].astype(F32)
        g_ref[...] = g
        d_ref[...], nm_ref[...], nv_ref[...] = _adam_math(g, w_ref[...], m_ref[...], v_ref[...])

    out_shape = [jax.ShapeDtypeStruct(w.shape, F32)] * 4
    if w.ndim < 3:
        return pl.pallas_call(body, out_shape=out_shape, name=name)(parts, w, m, v)
    rest = w.shape[1:]
    zeros = (0,) * len(rest)
    blk = pl.BlockSpec((None,) + rest, lambda l: (l,) + zeros)
    return pl.pallas_call(
        body, grid=(w.shape[0],),
        in_specs=[pl.BlockSpec((N_DEV, None) + rest, lambda l: (0, l) + zeros), blk, blk, blk],
        out_specs=[blk] * 4, out_shape=out_shape, name=name)(parts, w, m, v)


def _adamw(parts, w, m, v, tr, name, groups=None, fill=None, tie=None):
    n_groups, rows, cols = w.shape
    n_parts = parts.shape[1]
    lo, hi = groups if groups is not None else (0, n_groups)

    def body(p_ref, w_ref, m_ref, v_ref, *rest):
        g_ref, d_ref, nm_ref, nv_ref = rest[-4:]
        g = p_ref[0].astype(F32)
        for k in range(1, n_parts):
            g = g + p_ref[k].astype(F32)
        nm = B1 * m_ref[...] + (1.0 - B1) * g
        nv = B2 * v_ref[...] + (1.0 - B2) * (g * g)
        m_hat = nm / (1.0 - B1 ** STEP)
        v_hat = nv / (1.0 - B2 ** STEP)
        g_ref[...] = g
        d_ref[...] = -LR * (m_hat / (jnp.sqrt(v_hat) + ADAM_EPS) + WD * w_ref[...])
        nm_ref[...] = nm
        nv_ref[...] = nv

    blk = pl.BlockSpec((None, tr, cols), lambda l, i: (l + lo, i, 0))
    p_lo = lo if parts.shape[0] == n_groups else 0
    extra = ([] if fill is None else list(fill)) + ([] if tie is None else [tie])
    return pl.pallas_call(
        body, grid=(hi - lo, rows // tr),
        in_specs=[pl.BlockSpec((None, n_parts, tr, cols), lambda l, i: (l + p_lo, 0, i, 0)), blk, blk, blk]
        + [ANY] * len(extra),
        out_specs=[blk] * 4, out_shape=[jax.ShapeDtypeStruct((n_groups, rows, cols), F32)] * 4,
        input_output_aliases={} if fill is None else {4 + j: j for j in range(4)},
        compiler_params=_cp(), name=name)(parts, w, m, v, *extra)


def _split_start(name, arrays, n_sems, plan, after=None):
    n = len(arrays)
    order = [] if after is None else [after]
    n_in = n + len(order)

    def body(*refs):
        ins, send_sems, recv_sems, token = refs[:n], refs[n_in], refs[n_in + 1], refs[-1]
        for src, dst, k, to in plan(ins)[0]:
            pltpu.make_async_remote_copy(src_ref=src, dst_ref=dst, send_sem=send_sems.at[k], recv_sem=recv_sems.at[k],
                                         device_id=to, device_id_type=MESH_ID).start()
        token[...] = jnp.zeros_like(token)

    outs = pl.pallas_call(
        body, name=name,
        out_shape=(pltpu.SemaphoreType.DMA((n_sems,)), pltpu.SemaphoreType.DMA((n_sems,)),
                   *[pltpu.HBM(a.shape, a.dtype) for a in arrays], jax.ShapeDtypeStruct((8, 128), F32)),
        in_specs=[HBM] * n + [ANY] * len(order),
        out_specs=(SEM, SEM, *[HBM] * n, pl.BlockSpec(memory_space=pltpu.VMEM)),
        input_output_aliases={i: 2 + i for i in range(n)},
        compiler_params=pltpu.CompilerParams(has_side_effects=EFFECT),
    )(*[pltpu.with_memory_space_constraint(a, pltpu.HBM) for a in arrays], *order)
    return outs[0], outs[1], list(outs[2:2 + n]), outs[-1]


def _split_wait(name, arrays, send_sems, recv_sems, after, plan):
    n = len(arrays)
    order = list(after) if isinstance(after, (list, tuple)) else [after]

    def body(*refs):
        ins, s_sems, r_sems = refs[:n], refs[n], refs[n + 1]
        sends, arrivals = plan(ins)
        x, y, c = lax.axis_index("x"), lax.axis_index("y"), lax.axis_index("c")
        for src, dst, k, to in sends:
            pltpu.make_async_remote_copy(src_ref=src, dst_ref=dst, send_sem=s_sems.at[k], recv_sem=r_sems.at[k],
                                         device_id=to, device_id_type=MESH_ID).wait_send()
        for dst, k in arrivals:
            pltpu.make_async_remote_copy(src_ref=dst, dst_ref=dst, send_sem=s_sems.at[k], recv_sem=r_sems.at[k],
                                         device_id=(x, y, c), device_id_type=MESH_ID).wait_recv()

    return pl.pallas_call(
        body, name=name, out_shape=[pltpu.HBM(a.shape, a.dtype) for a in arrays],
        in_specs=[HBM] * n + [SEM, SEM] + [ANY] * len(order), out_specs=[HBM] * n,
        input_output_aliases={i: i for i in range(n)},
        compiler_params=pltpu.CompilerParams(has_side_effects=EFFECT),
    )(*arrays, send_sems, recv_sems, *order)


def _chips():
    x, y, c = lax.axis_index("x"), lax.axis_index("y"), lax.axis_index("c")
    return x, y, c, [(1 - x, y), (x, 1 - y), (1 - x, 1 - y)]


def _plan_gather_chips(refs):
    x, y, c, chips = _chips()
    me = 4 * x + 2 * y + c
    n = len(refs) // 2
    sends, arrivals = [], []
    for i in range(n):
        src, land = refs[i], refs[n + i]
        sends.append((src, land.at[me], 4 * i, (x, y, 1 - c)))
        arrivals.append((land.at[4 * x + 2 * y + 1 - c], 4 * i))
        for j, (px, py) in enumerate(chips):
            sends.append((src, land.at[me], 4 * i + 1 + j, (px, py, c)))
            arrivals.append((land.at[4 * px + 2 * py + c], 4 * i + 1 + j))
    return sends, arrivals


def _plan_gather_pass(refs):
    x, y, c, chips = _chips()
    sends, arrivals = [], []
    for i in range(len(refs)):
        for j, (px, py) in enumerate(chips):
            slot = refs[i].at[4 * px + 2 * py + c]
            sends.append((slot, slot, 4 * i + j, (x, y, 1 - c)))
            arrivals.append((refs[i].at[4 * px + 2 * py + 1 - c], 4 * i + j))
        back = refs[i].at[4 * x + 2 * y + 1 - c]
        sends.append((back, back, 4 * i + 3, (x, y, 1 - c)))
        arrivals.append((refs[i].at[4 * x + 2 * y + c], 4 * i + 3))
    return sends, arrivals


def _plan_scatter_pair(refs):
    x, y, c = lax.axis_index("x"), lax.axis_index("y"), lax.axis_index("c")
    n = len(refs) // 2
    sends, arrivals = [], []
    for i in range(n):
        for q in range(4):
            sends.append((refs[i].at[q, 1 - c], refs[n + i].at[q], 4 * i + q, (x, y, 1 - c)))
            arrivals.append((refs[n + i].at[q], 4 * i + q))
    return sends, arrivals


def _plan_scatter_chips(layer):
    def plan(refs):
        x, y, c, chips = _chips()
        n = len(refs) // 2
        sends, arrivals = [], []
        for i in range(n):
            for j, (px, py) in enumerate(chips):
                sends.append((refs[i].at[2 * px + py], refs[n + i].at[layer, 2 * x + y], 3 * i + j, (px, py, c)))
                arrivals.append((refs[n + i].at[layer, 2 * px + py], 3 * i + j))
        return sends, arrivals

    return plan


def _pair_sum(parts4, from_pair, landing, layer, core, tr, name):
    _, _, rows, cols = parts4.shape

    def body(c_ref, p_ref, s_ref, l_ref, sum_ref, land_ref):
        v = (p_ref[...].astype(F32) + s_ref[...].astype(F32)).astype(BF16)
        sum_ref[...] = v
        land_ref[...] = v

    blk = pl.BlockSpec((None, tr, cols), lambda q, i, c_ref: (q, i, 0))
    return pl.pallas_call(
        body,
        grid_spec=pltpu.PrefetchScalarGridSpec(
            num_scalar_prefetch=1, grid=(4, rows // tr),
            in_specs=[pl.BlockSpec((None, None, tr, cols), lambda q, i, c_ref: (q, c_ref[0], i, 0)), blk, ANY],
            out_specs=[blk, pl.BlockSpec((None, None, tr, cols), lambda q, i, c_ref: (layer, q, i, 0))]),
        out_shape=[jax.ShapeDtypeStruct((4, rows, cols), BF16), jax.ShapeDtypeStruct(landing.shape, BF16)],
        input_output_aliases={3: 1}, compiler_params=_cp(), name=name,
    )(core, parts4, from_pair, landing)


def _travel_layout(t):
    tr = lambda a: jnp.swapaxes(a, 1, 2)
    branch = jnp.concatenate([tr(t["w_attn_o"]), tr(t["w_conv_o"]), tr(t["w_ssm_o"])], axis=2)
    return [tr(t["w_in"]), tr(t["w_ffn_in"]), t["w_ffn_out"], t["w_mix_o"], branch, t["w_ssm_glu"]]


def _native_layout(a):
    tr = lambda x: jnp.swapaxes(x, 1, 2)
    b = a[4]
    return {"w_in": tr(a[0]), "w_ffn_in": tr(a[1]), "w_ffn_out": a[2], "w_mix_o": a[3],
            "w_attn_o": tr(b[:, :, :WIDTH]), "w_conv_o": tr(b[:, :, WIDTH:2 * WIDTH]),
            "w_ssm_o": tr(b[:, :, 2 * WIDTH:]), "w_ssm_glu": a[5]}


def _embed(t):
    eye = jnp.eye(8, dtype=t.dtype)
    t = t.reshape(DEPTH, N_LANE_GROUPS, 8, SSM_GROUP, SSM_STATE)
    return (t[:, :, :, :, None, :] * eye[None, None, :, None, :, None]).reshape(DEPTH, N_LANE_GROUPS, 128, LANES_G)


def _diag_blocks(t):
    t = t.reshape(DEPTH, N_LANE_GROUPS, 8, SSM_GROUP, 8, SSM_STATE)
    return jnp.einsum("lgahap->lgahp", t).reshape(DEPTH, SSM_GROUPS, SSM_GROUP, SSM_STATE)


def _rope_tabs():
    pos = jnp.arange(SEQ, dtype=F32)
    inv_freq = ROPE_THETA ** (-jnp.arange(0, ROT_DIM, 2, dtype=F32) / ROT_DIM)
    ang = pos[:, None] * inv_freq[None, :]
    cos, sin = jnp.cos(ang), jnp.sin(ang)
    one, zero = jnp.ones((SEQ, HEAD_DIM - ROT_DIM), F32), jnp.zeros((SEQ, HEAD_DIM - ROT_DIM), F32)
    z8 = jnp.zeros((SEQ, 8), F32)
    head = lambda *p: jnp.tile(jnp.concatenate(p, axis=1), (1, 2))
    return head(cos, cos, one), head(-sin, z8, zero), head(z8, sin, zero)


def _ssm_mats(sp):
    lr, li, bbr, bbi = _ssm_prep(sp["a_re"], sp["a_im"], sp["log_dt"], sp["bt_re"], sp["bt_im"])
    lanes = SSM_GROUPS * SSM_STATE
    return {
        "a_re": lr.reshape(DEPTH, 1, lanes), "a_im": li.reshape(DEPTH, 1, lanes),
        "b_re": _embed(bbr).astype(BF16), "b_im": _embed(bbi).astype(BF16),
        "c_re": _embed(sp["c_re"]).astype(BF16), "c_im_neg": _embed(-sp["c_im"]).astype(BF16),
    }


def _layer_fwd(x, i, w, rp, mats, tabs, tie, hooks):
    q, kv, cbx, u, glog, cv, h = _rms_mm_in(x, rp["norm_mix"][i], w["win_t"], tabs, rp["conv_w"], i, tie)
    o = _attn_fwd(q, kv, tabs, rp["attn_sinks"][i])
    x_re, x_im, y = _ssm_fwd(u, mats, i, rp["ssm_d"])
    z = _glu_fwd(y, w["wglu"])
    x1 = _mix_fwd(x, o, cv, z, glog, rp["b_gate"], i, w["branch_t"], w["wmix"], hooks["early"](z))
    hooks["pre_ffn"](x1)
    act, up, silu, dsilu, h2 = _rms_mm_ffn(x1, rp["norm_ffn"][i], w["wffn_t"])
    x2 = _ffn_out_fwd(x1, act, w["wout"], hooks["mid"](h2))
    kept = dict(x=x, q=q, kv=kv, cbx=cbx, u=u, glog=glog, h=h, o=o, cv=cv, z=z, y=y,
                x_re=x_re, x_im=x_im, x1=x1, act=act, up=up, silu=silu, dsilu=dsilu, h2=h2)
    return x2, kept


def _layer_bwd(dx2, k, i, w, rp, mats, tabs, tie, hooks):
    dgu = _ffn_out_bwd(dx2, k["up"], k["silu"], k["dsilu"], w["wout"], tie)
    g_wout = _mm_tn(k["act"], dx2, tm=FFN_H // 2, tn=1024, name="mm_tn_ffn_out")
    g_wffn_t = _mm_tn(dgu, k["h2"], tm=FFN_H // 2, tn=1024, name="mm_tn_ffn_in")
    dx1, d_norm_ffn = _mm_rmsbwd([dgu], w["wffn_t"], k["x1"], rp["norm_ffn"][i], dx2, "mm_rmsbwd_ffn")

    mg, dya, dyc, dys, do, dcv, dz, dgl, db_gate = _mix_bwd(
        dx1, k["o"], k["cv"], k["z"], k["glog"], rp["b_gate"], i, w["branch_t"], w["wmix"],
        hooks["mid"]((g_wffn_t, g_wout, d_norm_ffn)))
    g_wmix = _mm_tn(mg, dx1, tm=1024, tn=512, name="mm_tn_mix")
    g_branch_t = _tn_branches((dya, dyc, dys), (k["o"], k["cv"], k["z"]))

    dy, ys16, da16, dd = _glu_bwd(k["y"], w["wglu"], dz, k["u"])
    g_wglu = _mm_tn(ys16, da16, tm=256, tn=512, name="mm_tn_glu")
    du, da_re, da_im, db_re, db_im, dc_re, dc_im = _ssm_bwd(dy, k["x_re"], k["x_im"], k["u"], mats, i, rp["ssm_d"])

    dcb, dcc, dcx, d_conv_w = _conv_bwd(k["cbx"], rp["conv_w"], i, dcv, hooks["late"](du))
    dq, dkv, d_sinks = _attn_bwd(k["q"], k["kv"], tabs, rp["attn_sinks"][i], do)

    pieces = [dq, dkv, dcb, dcc, dcx, du, dgl]
    g_win_t = _tn_pieces(pieces, k["h"])
    dx, d_norm_mix = _mm_rmsbwd(pieces, w["win_t"], k["x"], rp["norm_mix"][i], dx1, "mm_rmsbwd_in")

    grads = [g_win_t, g_wffn_t, g_wout, g_wmix, g_branch_t, g_wglu]
    small = dict(norm_mix=d_norm_mix, b_gate=db_gate, attn_sinks=d_sinks, ssm_d=dd, norm_ffn=d_norm_ffn,
                 conv_w=d_conv_w, da_re=da_re, da_im=da_im, db_re=db_re, db_im=db_im, dc_re=dc_re, dc_im=dc_im)
    return dx, grads, small


def _replicated_grads(sg, sp):
    stack = lambda name: jnp.stack([sg[i][name] for i in range(DEPTH)])
    cots = (stack("da_re").reshape(DEPTH, *_GS), stack("da_im").reshape(DEPTH, *_GS),
            _diag_blocks(stack("db_re")), _diag_blocks(stack("db_im")))
    d_a_re, d_a_im, d_log_dt, d_bt_re, d_bt_im = _ssm_prep_bwd(
        sp["a_re"], sp["a_im"], sp["log_dt"], sp["bt_re"], sp["bt_im"], cots)
    sgrads = {"norm_mix": stack("norm_mix"), "b_gate": stack("b_gate"),
              "attn_sinks": stack("attn_sinks")[:, :, :N_Q_HEADS], "ssm_a_re": d_a_re, "ssm_a_im": d_a_im,
              "ssm_b_re": jnp.swapaxes(d_bt_re, 2, 3), "ssm_b_im": jnp.swapaxes(d_bt_im, 2, 3),
              "ssm_c_re": _diag_blocks(stack("dc_re")), "ssm_c_im": -_diag_blocks(stack("dc_im")),
              "ssm_d": stack("ssm_d"), "ssm_log_dt": d_log_dt, "norm_ffn": stack("norm_ffn")}
    return sgrads, stack("conv_w")[:, :3]


def kernel(x, norm_mix, w_in, b_gate, attn_sinks, w_attn_o, conv_w, w_conv_o, ssm_a_re, ssm_a_im, ssm_b_re, ssm_b_im, ssm_c_re, ssm_c_im, ssm_d, ssm_log_dt, w_ssm_glu, w_ssm_o, w_mix_o, norm_ffn, w_ffn_in, w_ffn_out, norm_final, loss_target, m_norm_mix, m_w_in, m_b_gate, m_attn_sinks, m_w_attn_o, m_conv_w, m_w_conv_o, m_ssm_a_re, m_ssm_a_im, m_ssm_b_re, m_ssm_b_im, m_ssm_c_re, m_ssm_c_im, m_ssm_d, m_ssm_log_dt, m_w_ssm_glu, m_w_ssm_o, m_w_mix_o, m_norm_ffn, m_w_ffn_in, m_w_ffn_out, m_norm_final, v_norm_mix, v_w_in, v_b_gate, v_attn_sinks, v_w_attn_o, v_conv_w, v_w_conv_o, v_ssm_a_re, v_ssm_a_im, v_ssm_b_re, v_ssm_b_im, v_ssm_c_re, v_ssm_c_im, v_ssm_d, v_ssm_log_dt, v_w_ssm_glu, v_w_ssm_o, v_w_mix_o, v_norm_ffn, v_w_ffn_in, v_w_ffn_out, v_norm_final):
    big = {"w": dict(w_in=w_in, w_attn_o=w_attn_o, w_conv_o=w_conv_o, w_ssm_glu=w_ssm_glu, w_ssm_o=w_ssm_o,
                     w_mix_o=w_mix_o, w_ffn_in=w_ffn_in, w_ffn_out=w_ffn_out),
           "m": dict(w_in=m_w_in, w_attn_o=m_w_attn_o, w_conv_o=m_w_conv_o, w_ssm_glu=m_w_ssm_glu,
                     w_ssm_o=m_w_ssm_o, w_mix_o=m_w_mix_o, w_ffn_in=m_w_ffn_in, w_ffn_out=m_w_ffn_out),
           "v": dict(w_in=v_w_in, w_attn_o=v_w_attn_o, w_conv_o=v_w_conv_o, w_ssm_glu=v_w_ssm_glu,
                     w_ssm_o=v_w_ssm_o, w_mix_o=v_w_mix_o, w_ffn_in=v_w_ffn_in, w_ffn_out=v_w_ffn_out)}
    small = {"w": dict(norm_mix=norm_mix, b_gate=b_gate, attn_sinks=attn_sinks, ssm_a_re=ssm_a_re,
                       ssm_a_im=ssm_a_im, ssm_b_re=ssm_b_re, ssm_b_im=ssm_b_im, ssm_c_re=ssm_c_re,
                       ssm_c_im=ssm_c_im, ssm_d=ssm_d, ssm_log_dt=ssm_log_dt, norm_ffn=norm_ffn),
             "m": dict(norm_mix=m_norm_mix, b_gate=m_b_gate, attn_sinks=m_attn_sinks, ssm_a_re=m_ssm_a_re,
                       ssm_a_im=m_ssm_a_im, ssm_b_re=m_ssm_b_re, ssm_b_im=m_ssm_b_im, ssm_c_re=m_ssm_c_re,
                       ssm_c_im=m_ssm_c_im, ssm_d=m_ssm_d, ssm_log_dt=m_ssm_log_dt, norm_ffn=m_norm_ffn),
             "v": dict(norm_mix=v_norm_mix, b_gate=v_b_gate, attn_sinks=v_attn_sinks, ssm_a_re=v_ssm_a_re,
                       ssm_a_im=v_ssm_a_im, ssm_b_re=v_ssm_b_re, ssm_b_im=v_ssm_b_im, ssm_c_re=v_ssm_c_re,
                       ssm_c_im=v_ssm_c_im, ssm_d=v_ssm_d, ssm_log_dt=v_ssm_log_dt, norm_ffn=v_norm_ffn)}
    finals = {"w": norm_final, "m": m_norm_final, "v": v_norm_final}
    convs = {"w": conv_w, "m": m_conv_w, "v": v_conv_w}
    small_out_shapes = {name: a.shape for name, a in small["w"].items()}
    small_out_shapes.update(norm_final=(D_MODEL,), conv_w=(DEPTH, 3, 64))
    small_shapes = dict(small_out_shapes, norm_final=(1, D_MODEL), conv_w=(DEPTH, 3, WIDTH))
    dense = ("ssm_b_re", "ssm_b_im", "ssm_c_re", "ssm_c_im")
    for name in dense:
        small_shapes[name] = (DEPTH, SSM_GROUPS, SSM_GROUP * SSM_STATE)
    small_wmv = {name: [(convs[s] if name == "conv_w" else finals[s] if name == "norm_final" else small[s][name])
                        .reshape((DEPTH, 3, 64) if name == "conv_w" else small_shapes[name]) for s in "wmv"]
                 for name in small_shapes}
    mine = 4 * lax.axis_index("x") + 2 * lax.axis_index("y") + lax.axis_index("c")

    travel = {s: _travel_layout(big[s]) for s in "wmv"}
    stacked16 = list(zip(*[[a[0] for a in _travel_layout({n: w[i:i + 1].astype(BF16) for n, w in big["w"].items()})]
                           for i in range(DEPTH)]))
    rp = {"norm_mix": norm_mix[:, None], "norm_ffn": norm_ffn[:, None], "attn_sinks": attn_sinks[:, None],
          "b_gate": b_gate[:, None], "ssm_d": ssm_d[:, None]}
    sp = {"a_re": ssm_a_re, "a_im": ssm_a_im, "log_dt": ssm_log_dt[:, :, None],
          "bt_re": jnp.swapaxes(ssm_b_re, 2, 3), "bt_im": jnp.swapaxes(ssm_b_im, 2, 3),
          "c_re": ssm_c_re, "c_im": ssm_c_im}
    rows_tile = {"win_t": 368, "wffn_t": 352, "wout": 352, "wmix": 128, "branch_t": 128, "wglu": 64}
    core = lax.axis_index("c").astype(jnp.int32).reshape(1)
    no_tie = jnp.zeros((8, 128), F32)

    def landing_zones(srcs):
        return [lax.empty((N_DEV,) + s.shape, s.dtype) for s in srcs]

    def gather_chips(tag, i, kinds, after, extra=()):
        srcs = [stacked16[j][i] for j in kinds] + list(extra)
        s_sems, r_sems, arrays, token = _split_start(
            f"gather_chips_start_{tag}", srcs + landing_zones(srcs), 4 * len(srcs), _plan_gather_chips, after)
        return (tag, s_sems, r_sems, arrays), token

    def gather_pass(state, after):
        tag, s_sems, r_sems, arrays = state
        arrays = _split_wait(f"gather_chips_wait_{tag}", arrays, s_sems, r_sems, after, _plan_gather_chips)
        n = len(arrays) // 2
        s_sems, r_sems, lands, token = _split_start(
            f"gather_pass_start_{tag}", list(arrays[n:]), 4 * n, _plan_gather_pass)
        return (tag, s_sems, r_sems, lands), token

    def gather_done(state, after, kinds):
        tag, s_sems, r_sems, lands = state
        lands = _split_wait(f"gather_pass_wait_{tag}", lands, s_sems, r_sems, after, _plan_gather_pass)
        named = {KINDS[j][0]: a.reshape(N_DEV * KINDS[j][1], KINDS[j][2]) for a, j in zip(lands, kinds)}
        return named, list(lands[len(kinds):])

    all_kinds, mixer_kinds, ffn_kinds = tuple(range(len(KINDS))), (0, 3, 4, 5), (1, 2)
    no_hooks = {name: (lambda value: no_tie) for name in ("early", "pre_ffn", "mid", "late")}
    state, token = gather_chips("0m", 0, mixer_kinds, None, extra=[jnp.pad(conv_w.reshape(6, 128), ((0, 2), (0, 0)))])
    mats = _ssm_mats(dict(sp, log_dt=sp["log_dt"] + token[0, 0]))
    tabs = _rope_tabs()
    early_work = list(mats.values()) + list(tabs) + [a for name in dense for a in small_wmv[name]]
    early_work += [stacked16[j][0] for j in ffn_kinds] + [stacked16[j][1] for j in mixer_kinds]
    state, _ = gather_pass(state, early_work)
    ffn_state, tie = gather_chips("0f", 0, ffn_kinds, state[3][0])
    w_next, (conv_all,) = gather_done(state, tabs[2], mixer_kinds)
    conv_full = conv_all[:, :6].reshape(N_DEV, DEPTH, 3, 64).transpose(1, 2, 0, 3).reshape(DEPTH, 3, WIDTH)
    rp["conv_w"] = jnp.pad(conv_full, ((0, 0), (0, 5), (0, 0)))

    act = x[0]
    weights, kept = [], []
    for i in range(DEPTH):
        w_i, hooks, held = w_next, dict(no_hooks), {}

        def early(value, ffn_state=ffn_state, held=held):
            held["ffn"], token = gather_pass(ffn_state, value)
            return token

        def pre_ffn(value, w_i=w_i, held=held):
            w_i.update(gather_done(held["ffn"], value, ffn_kinds)[0])

        hooks.update(early=early, pre_ffn=pre_ffn)
        if i + 1 < DEPTH:
            state, tie = gather_chips(f"{i + 1}m", i + 1, mixer_kinds, tie if i == 0 else w_i["win_t"])

            def mid(value, i=i, state=state, held=held):
                held["next"], token = gather_pass(state, value)
                held["next_ffn"], token = gather_chips(f"{i + 1}f", i + 1, ffn_kinds, token)
                return token

            hooks.update(mid=mid)
        act, k = _layer_fwd(act, i, w_i, rp, mats, tabs, tie, hooks)
        if i + 1 < DEPTH:
            w_next, _ = gather_done(held["next"], act, mixer_kinds)
            ffn_state, tie = held["next_ffn"], no_tie
        weights.append(w_i)
        kept.append(k)
    loss_row, dx, d_norm_final = _loss_head(act, norm_final[None], loss_target[0])

    landings = [lax.empty((DEPTH, 4, r, c), BF16) for _, r, c in KINDS]
    landings0 = [lax.empty((1, 4, r, c), BF16) for _, r, c in KINDS]

    def scatter_pair(tag, kinds, grads, after):
        parts4 = [g.reshape(4, 2, KINDS[j][1], KINDS[j][2]) for g, j in zip(grads, kinds)]
        zones = [lax.empty((4, KINDS[j][1], KINDS[j][2]), BF16) for j in kinds]
        s_sems, r_sems, arrays, token = _split_start(
            f"scatter_pair_start_{tag}", parts4 + zones, 4 * len(kinds), _plan_scatter_pair, after)
        return (tag, kinds, s_sems, r_sems, arrays), token

    def scatter_chips(state, lands, slot, after):
        tag, kinds, s_sems, r_sems, arrays = state
        arrays = _split_wait(f"scatter_pair_wait_{tag}", arrays, s_sems, r_sems, after, _plan_scatter_pair)
        n = len(kinds)
        sums, mine_lands = [], []
        for k, j in enumerate(kinds):
            name = KINDS[j][0]
            chip_sum, land = _pair_sum(arrays[k], arrays[n + k], lands[j], slot, core, KINDS[j][1],
                                       f"pair_sum_{name}")
            sums.append(chip_sum)
            mine_lands.append(land)
        s_sems, r_sems, arrays, token = _split_start(
            f"scatter_chips_start_{tag}", sums + mine_lands, 3 * n, _plan_scatter_chips(slot))
        return (tag, kinds, slot, s_sems, r_sems, arrays), token

    def scatter_done(state, lands, after):
        tag, kinds, slot, s_sems, r_sems, arrays = state
        arrays = _split_wait(f"scatter_chips_wait_{tag}", arrays, s_sems, r_sems, after, _plan_scatter_chips(slot))
        lands = list(lands)
        for k, j in enumerate(kinds):
            lands[j] = arrays[len(kinds) + k]
        return lands

    sg = [None] * DEPTH
    pending, tie = None, no_tie
    for i in reversed(range(DEPTH)):
        hooks, held = dict(no_hooks), {}
        if pending is not None:
            def mid(value, i=i, pending=pending, held=held):
                held["chips"], token = scatter_chips(pending, landings, i + 1, value[2])
                if i == 0:
                    held["ffn_pair"], token = scatter_pair("0f", ffn_kinds, value[:2], token)
                return token

            hooks.update(mid=mid)
        if i == 0:
            def late(value, held=held):
                held["ffn_chips"], token = scatter_chips(held["ffn_pair"], landings0, 0, value)
                return token

            hooks.update(late=late)
        dx, grads, sg[i] = _layer_bwd(dx, kept[i], i, weights[i], rp, mats, tabs, tie, hooks)
        if pending is not None:
            landings = scatter_done(held["chips"], landings, dx)
        if i > 0:
            pending, tie = scatter_pair(str(i), all_kinds, grads, dx)
        else:
            pending, _ = scatter_pair("0m", mixer_kinds, [grads[j] for j in mixer_kinds], dx)

    sgrads, conv_grad = _replicated_grads(sg, sp)

    small_names = list(REPLICATED) + ["norm_final", "conv_w"]
    sgrads.update(norm_final=d_norm_final, conv_w=conv_grad)
    small_src = [sgrads[name].reshape(small_shapes[name]).astype(BF16) for name in small_names]
    small_src.append(jnp.broadcast_to(loss_row[:, :1], (8, 128)))
    last, tie = scatter_chips(pending, landings0, 0, small_src[0])
    s_sems, r_sems, arrays, tie = _split_start(
        "gather_small_chips_start", small_src + landing_zones(small_src), 4 * len(small_src), _plan_gather_chips, tie)
    small_state = ("small", s_sems, r_sems, arrays)

    big_out = []
    for j, (name, _, _) in enumerate(KINDS):
        big_out.append(_adamw(landings[j], travel["w"][j], travel["m"][j], travel["v"][j], rows_tile[name],
                              "adamw_late_" + name, groups=(1, DEPTH), tie=tie))
        tie = big_out[-1][3]
    landings0 = scatter_done(held["ffn_chips"], landings0, tie)
    landings0 = scatter_done(last, landings0, tie)
    small_state, _ = gather_pass(small_state, landings0[0])
    big_out = [_adamw(landings0[j], travel["w"][j], travel["m"][j], travel["v"][j], rows_tile[name],
                      "adamw_first_" + name, groups=(0, 1), fill=big_out[j]) for j, (name, _, _) in enumerate(KINDS)]
    big_res = [_native_layout([big_out[j][kind] for j in range(len(KINDS))]) for kind in range(4)]

    _, sparts = gather_done(small_state, big_out[-1][0], ())
    loss = jnp.sum(sparts[-1][:, 0, 0])
    sparts = dict(zip(small_names, sparts))
    sparts["conv_w"] = lax.dynamic_slice_in_dim(sparts["conv_w"], mine * 64, 64, axis=3)
    small_res = {}
    for name in small_names:
        res = _adamw_small(sparts[name], *small_wmv[name], "adamw_" + name)
        small_res[name] = [r.reshape(small_out_shapes[name]) for r in res]

    order = ["norm_mix", "w_in", "b_gate", "attn_sinks", "w_attn_o", "conv_w", "w_conv_o", "ssm_a_re", "ssm_a_im",
             "ssm_b_re", "ssm_b_im", "ssm_c_re", "ssm_c_im", "ssm_d", "ssm_log_dt", "w_ssm_glu", "w_ssm_o",
             "w_mix_o", "norm_ffn", "w_ffn_in", "w_ffn_out", "norm_final"]
    outs = [loss, dx[None]]
    for kind in range(4):
        for name in order:
            outs.append(big_res[kind][name] if name in big_res[kind] else small_res[name][kind])
    return tuple(outs)
```

```python
import math

import jax
import jax.numpy as jnp
from jax import lax
from jax.experimental import pallas as pl
from jax.experimental.pallas import tpu as pltpu

F32 = jnp.float32
BF16 = jnp.bfloat16

N_DEV = 8
DEPTH = 4
SEQ = 2048
D_MODEL = 1024
N_Q_HEADS = 8
HEAD_DIM = 64
ATTN_W = 512
KV_W = 128
BLOCK = 128
N_BLOCKS = SEQ // BLOCK
ROPE_THETA = 500000.0
ROT_DIM = 16
NEG_INF = -1e30
WIDTH = 512
SSM_GROUPS = 32
SSM_GROUP = 16
SSM_STATE = 64
CHUNK = 256
N_CHUNKS = SEQ // CHUNK
GATE_W = 3 * D_MODEL
IN_COLS = 5888
FFN_H = 2816
NORM_EPS = 1e-6
LR, B1, B2, ADAM_EPS, WD, STEP = 0.001, 0.9, 0.999, 1e-08, 0.01, 10

COL_Q, COL_KV, COL_CBX, COL_U, COL_G = 0, 512, 768, 2304, 2816
PIECE_W = (512, 256, 512, 512, 512, 512, 3072)
PIECE_OFF = tuple(sum(PIECE_W[:i]) for i in range(len(PIECE_W)))

KINDS = (("win_t", 736, 1024), ("wffn_t", 704, 1024), ("wout", 352, 1024), ("wmix", 128, 1024),
         ("branch_t", 128, 1536), ("wglu", 64, 512))

REPLICATED = ("norm_mix", "b_gate", "attn_sinks", "ssm_a_re", "ssm_a_im", "ssm_b_re", "ssm_b_im", "ssm_c_re",
              "ssm_c_im", "ssm_d", "ssm_log_dt", "norm_ffn")

VMEM_LIMIT = 56 * 1024 * 1024
NT = (((1,), (1,)), ((), ()))
TN = (((0,), (0,)), ((), ()))
MESH_ID = pl.DeviceIdType.MESH
ANY = pl.BlockSpec(memory_space=pl.ANY)
HBM = pl.BlockSpec(memory_space=pltpu.HBM)
SEM = pl.BlockSpec(memory_space=pltpu.SEMAPHORE)
EFFECT = pltpu.SideEffectType.DATAFLOW_SIDE_EFFECTING


def _cp(**kw):
    return pltpu.CompilerParams(vmem_limit_bytes=VMEM_LIMIT, **kw)


def _full(shape):
    return pl.BlockSpec(shape, lambda *_: (0,) * len(shape))


def _resident(shape):
    return pl.BlockSpec(shape, lambda *_: (0,) * len(shape), pipeline_mode=pl.Buffered(1))


def _mm_tn(a, b, *, tm, tn, name):
    k, m = a.shape
    n = b.shape[1]

    def body(a_ref, b_ref, o_ref):
        o_ref[...] = lax.dot_general(a_ref[...].astype(BF16), b_ref[...].astype(BF16), TN,
                                     preferred_element_type=F32).astype(BF16)

    return pl.pallas_call(
        body, grid=(m // tm, n // tn),
        in_specs=[pl.BlockSpec((k, tm), lambda i, j: (0, i)), pl.BlockSpec((k, tn), lambda i, j: (0, j))],
        out_specs=pl.BlockSpec((tm, tn), lambda i, j: (i, j)),
        out_shape=jax.ShapeDtypeStruct((m, n), BF16), compiler_params=_cp(), name=name)(a, b)


def _rms_rows(xv, g):
    r = lax.rsqrt(jnp.mean(xv * xv, axis=-1, keepdims=True) + NORM_EPS)
    return ((xv * r) * g).astype(BF16)


def _rms_mm_in(x, g, wt, tabs, cw, layer, tie):
    tt = 512
    widths = (3 * WIDTH, WIDTH, GATE_W)
    offs = (COL_CBX, COL_U, COL_G)

    def body(x_ref, g_ref, w_ref, tc_ref, ta_ref, tb_ref, cw_ref, tie_ref,
             q_ref, kv_ref, cbx_ref, u_ref, gl_ref, cv_ref, h_ref, tail_ref):
        @pl.when(pl.program_id(0) == 0)
        def _():
            tail_ref[...] = jnp.zeros_like(tail_ref)

        h = _rms_rows(x_ref[...], g_ref[...])
        h_ref[...] = h
        prod = lax.dot_general(h, w_ref[...], NT, preferred_element_type=F32)
        for ref, o, w in zip((cbx_ref, u_ref, gl_ref), offs, widths):
            ref[...] = prod[:, o:o + w]
        c, a, b = tc_ref[...], ta_ref[...], tb_ref[...]
        for j in range(ATTN_W // 128):
            q_ref[:, 128 * j:128 * (j + 1)] = _rope(prod[:, 128 * j:128 * (j + 1)], c, a, b) * (HEAD_DIM ** -0.5)
        kv_ref[:, :KV_W] = _rope(prod[:, COL_KV:COL_KV + KV_W], c, a, b)
        kv_ref[:, KV_W:] = prod[:, COL_KV + KV_W:COL_CBX]

        row = lax.broadcasted_iota(jnp.int32, (tt, 128), 0)
        for j in range(WIDTH // 128):
            cols = slice(128 * j, 128 * (j + 1))
            cb = prod[:, COL_CBX + 128 * j:COL_CBX + 128 * (j + 1)]
            z = prod[:, COL_CBX + WIDTH + 128 * j:COL_CBX + WIDTH + 128 * (j + 1)] \
                * prod[:, COL_CBX + 2 * WIDTH + 128 * j:COL_CBX + 2 * WIDTH + 128 * (j + 1)]
            before1, before2 = tail_ref[7:8, cols], tail_ref[6:7, cols]
            z1 = jnp.where(row == 0, before1, pltpu.roll(z, 1, axis=0))
            z2 = jnp.where(row == 0, before2, jnp.where(row == 1, before1, pltpu.roll(z, 2, axis=0)))
            s = cw_ref[0:1, cols] * z2 + cw_ref[1:2, cols] * z1 + cw_ref[2:3, cols] * z
            cv_ref[:, cols] = (cb * s).astype(BF16)
            tail_ref[:, cols] = z[tt - 8:, :]

    row_spec = lambda w: pl.BlockSpec((tt, w), lambda i: (i, 0))
    sds = jax.ShapeDtypeStruct
    return pl.pallas_call(
        body, grid=(SEQ // tt,),
        in_specs=[row_spec(D_MODEL), _full((1, D_MODEL)), _resident((IN_COLS, D_MODEL)),
                  row_spec(128), row_spec(128), row_spec(128),
                  pl.BlockSpec((None, 8, WIDTH), lambda i: (layer, 0, 0)), ANY],
        out_specs=[row_spec(ATTN_W), row_spec(2 * KV_W), row_spec(3 * WIDTH), row_spec(WIDTH), row_spec(GATE_W),
                   row_spec(WIDTH), row_spec(D_MODEL)],
        out_shape=[sds((SEQ, ATTN_W), F32), sds((SEQ, 2 * KV_W), F32), sds((SEQ, 3 * WIDTH), F32),
                   sds((SEQ, WIDTH), F32), sds((SEQ, GATE_W), F32), sds((SEQ, WIDTH), BF16),
                   sds((SEQ, D_MODEL), BF16)],
        scratch_shapes=[pltpu.VMEM((8, WIDTH), F32)], compiler_params=_cp(), name="rms_mm_in",
    )(x, g, wt, *tabs, cw, tie)


def _rms_mm_ffn(x, g, wt):
    tt = 256

    def body(x_ref, g_ref, w_ref, act_ref, up_ref, silu_ref, dsilu_ref, h_ref):
        h = _rms_rows(x_ref[...], g_ref[...])
        h_ref[...] = h
        prod = lax.dot_general(h, w_ref[...], NT, preferred_element_type=F32)
        gt, up = prod[:, :FFN_H], prod[:, FFN_H:]
        sg = jax.nn.sigmoid(gt)
        silu = gt * sg
        act_ref[...] = (silu * up).astype(BF16)
        up_ref[...] = up.astype(BF16)
        silu_ref[...] = silu.astype(BF16)
        dsilu_ref[...] = (sg + silu * (1.0 - sg)).astype(BF16)

    row = lambda w: pl.BlockSpec((tt, w), lambda i: (i, 0))
    return pl.pallas_call(
        body, grid=(SEQ // tt,), in_specs=[row(D_MODEL), _full((1, D_MODEL)), _resident((2 * FFN_H, D_MODEL))],
        out_specs=[row(FFN_H)] * 4 + [row(D_MODEL)],
        out_shape=[jax.ShapeDtypeStruct((SEQ, FFN_H), BF16)] * 4 + [jax.ShapeDtypeStruct((SEQ, D_MODEL), BF16)],
        compiler_params=_cp(), name="rms_mm_ffn")(x, g, wt)


def _mm_rmsbwd(pieces, wt, x, g, dres, name):
    tt = 512
    widths = [p.shape[1] for p in pieces]
    offs = [sum(widths[:i]) for i in range(len(widths))]
    n = len(pieces)

    def body(*refs):
        p_refs, (w_ref, x_ref, g_ref, r_ref, dx_ref, dg_ref) = refs[:n], refs[n:]

        @pl.when(pl.program_id(0) == 0)
        def _():
            dg_ref[...] = jnp.zeros_like(dg_ref)

        dh = jnp.zeros((tt, D_MODEL), F32)
        for p_ref, o, w in zip(p_refs, offs, widths):
            dh += jnp.dot(p_ref[...], w_ref[o:o + w, :], preferred_element_type=F32)
        xv = x_ref[...]
        r = lax.rsqrt(jnp.mean(xv * xv, axis=-1, keepdims=True) + NORM_EPS)
        xh = xv * r
        gy = dh * g_ref[...]
        dx_ref[...] = r_ref[...] + r * (gy - xh * jnp.mean(gy * xh, axis=-1, keepdims=True))
        dg_ref[...] += jnp.sum(dh * xh, axis=0, keepdims=True)

    row = lambda w: pl.BlockSpec((tt, w), lambda i: (i, 0))
    return pl.pallas_call(
        body, grid=(SEQ // tt,),
        in_specs=[row(w) for w in widths] + [_resident(wt.shape), row(D_MODEL), _full((1, D_MODEL)), row(D_MODEL)],
        out_specs=[row(D_MODEL), _full((1, D_MODEL))],
        out_shape=[jax.ShapeDtypeStruct((SEQ, D_MODEL), F32), jax.ShapeDtypeStruct((1, D_MODEL), F32)],
        compiler_params=_cp(), name=name)(*pieces, wt, x, g, dres)


def _tn_pieces(pieces, h):
    tk, tn = 512, 512
    nk = SEQ // tk
    n = len(pieces)

    def body(*refs):
        p_refs, (h_ref, o_ref, acc_ref) = refs[:n], refs[n:]
        kk = pl.program_id(1)

        @pl.when(kk == 0)
        def _():
            acc_ref[...] = jnp.zeros_like(acc_ref)

        hv = h_ref[...]
        for p_ref, o, w in zip(p_refs, PIECE_OFF, PIECE_W):
            acc_ref[o:o + w, :] += lax.dot_general(p_ref[...], hv, TN, preferred_element_type=F32)

        @pl.when(kk == nk - 1)
        def _():
            o_ref[...] = acc_ref[...].astype(BF16)

    return pl.pallas_call(
        body, grid=(D_MODEL // tn, nk),
        in_specs=[pl.BlockSpec((tk, w), lambda j, kk: (kk, 0)) for w in PIECE_W]
        + [pl.BlockSpec((tk, tn), lambda j, kk: (kk, j))],
        out_specs=pl.BlockSpec((IN_COLS, tn), lambda j, kk: (0, j)),
        out_shape=jax.ShapeDtypeStruct((IN_COLS, D_MODEL), BF16),
        scratch_shapes=[pltpu.VMEM((IN_COLS, tn), F32)], compiler_params=_cp(), name="tn_pieces")(*pieces, h)


def _tn_branches(dys, acts):
    tk = 512
    nk = SEQ // tk

    def body(d0, d1, d2, a0, a1, a2, o_ref, acc_ref):
        kk = pl.program_id(0)

        @pl.when(kk == 0)
        def _():
            acc_ref[...] = jnp.zeros_like(acc_ref)

        for j, (d, a) in enumerate(((d0, a0), (d1, a1), (d2, a2))):
            acc_ref[:, WIDTH * j:WIDTH * (j + 1)] += lax.dot_general(d[...], a[...], TN, preferred_element_type=F32)

        @pl.when(kk == nk - 1)
        def _():
            o_ref[...] = acc_ref[...].astype(BF16)

    row = lambda w: pl.BlockSpec((tk, w), lambda kk: (kk, 0))
    return pl.pallas_call(
        body, grid=(nk,), in_specs=[row(D_MODEL)] * 3 + [row(WIDTH)] * 3,
        out_specs=_full((D_MODEL, 3 * WIDTH)), out_shape=jax.ShapeDtypeStruct((D_MODEL, 3 * WIDTH), BF16),
        scratch_shapes=[pltpu.VMEM((D_MODEL, 3 * WIDTH), F32)], compiler_params=_cp(), name="tn_branches",
    )(*dys, *acts)


def _rope(t, c, a, b):
    return t * c + pltpu.roll(t, 120, axis=1) * a + pltpu.roll(t, 8, axis=1) * b


def _rope_t(d, c, a, b):
    return d * c + pltpu.roll(d * a, 8, axis=1) + pltpu.roll(d * b, 120, axis=1)


def _band_sides(band):
    left = lax.broadcasted_iota(jnp.int32, band.shape, 1) < HEAD_DIM
    h0 = jnp.where(left, band, 0.0)
    h1 = jnp.where(left, 0.0, band)
    r0 = pltpu.roll(h0, HEAD_DIM, axis=1)
    r1 = pltpu.roll(h1, HEAD_DIM, axis=1)
    return ((h0.astype(BF16), r0.astype(BF16)), (r1.astype(BF16), h1.astype(BF16)))


def _attn_mask(i):
    qi = lax.broadcasted_iota(jnp.int32, (2 * BLOCK, 2 * BLOCK), 0) % BLOCK
    kj = lax.broadcasted_iota(jnp.int32, (2 * BLOCK, 2 * BLOCK), 1)
    delta = qi + BLOCK - kj
    return (delta >= 0) & (delta < BLOCK) & ((kj >= BLOCK) | (i > 0))


def _attn_probs(s, ok, sink):
    s = jnp.where(ok, s, NEG_INF)
    m = jnp.maximum(jnp.max(s, axis=-1, keepdims=True), sink)
    p = jnp.exp(s - m)
    es = jnp.exp(sink - m)
    inv = 1.0 / (jnp.sum(p, axis=-1, keepdims=True) + es)
    return p * inv, es * inv


def _kv_group(qs, ks, vs, kh, sink_ref):
    q2 = jnp.concatenate([qs[2 * kh], qs[2 * kh + 1]], axis=0)
    kst = jnp.concatenate([ks[kh][0], ks[kh][1]], axis=0)
    vst = jnp.concatenate([vs[kh][0], vs[kh][1]], axis=0)
    top = lax.broadcasted_iota(jnp.int32, (2 * BLOCK, 1), 0) < BLOCK
    sinks = [jnp.where(top, sink_ref[0, 4 * kh + h], sink_ref[0, 4 * kh + 2 + h]) for h in range(2)]
    return q2, kst, vst, sinks


def _attn_load(q_ref, kvc_ref, kvp_ref, tc_ref, ta_ref, tb_ref, pc_ref, pa_ref, pb_ref):
    c, a, b = tc_ref[...], ta_ref[...], tb_ref[...]
    kband = jnp.concatenate([kvp_ref[:, :KV_W], kvc_ref[:, :KV_W]], axis=0)
    vband = jnp.concatenate([kvp_ref[:, KV_W:], kvc_ref[:, KV_W:]], axis=0)
    qs = [q_ref[:, 128 * j:128 * (j + 1)].astype(BF16) for j in range(4)]
    return qs, _band_sides(kband), _band_sides(vband), (c, a, b)


def _attn_specs(clamp):
    cur = lambda i: (clamp(i), 0)
    prev = lambda i: (jnp.maximum(clamp(i) - 1, 0), 0)
    return [
        pl.BlockSpec((BLOCK, ATTN_W), cur), pl.BlockSpec((BLOCK, 2 * KV_W), cur),
        pl.BlockSpec((BLOCK, 2 * KV_W), prev),
        pl.BlockSpec((BLOCK, 128), cur), pl.BlockSpec((BLOCK, 128), cur), pl.BlockSpec((BLOCK, 128), cur),
        pl.BlockSpec((BLOCK, 128), prev), pl.BlockSpec((BLOCK, 128), prev), pl.BlockSpec((BLOCK, 128), prev),
        pl.BlockSpec(memory_space=pltpu.SMEM),
    ]


def _attn_fwd(q, kv, tabs, sinks):
    tc, ta, tb = tabs

    def body(q_ref, kvc_ref, kvp_ref, tc_ref, ta_ref, tb_ref, pc_ref, pa_ref, pb_ref, sink_ref, o_ref):
        i = pl.program_id(0)
        qs, ks, vs, _ = _attn_load(q_ref, kvc_ref, kvp_ref, tc_ref, ta_ref, tb_ref, pc_ref, pa_ref, pb_ref)
        ok = _attn_mask(i)
        for kh in range(2):
            q2, kst, vst, sinks = _kv_group(qs, ks, vs, kh, sink_ref)
            s = lax.dot_general(q2, kst, NT, preferred_element_type=F32)
            pn = [_attn_probs(s[:, 2 * BLOCK * h:2 * BLOCK * (h + 1)], ok, sinks[h])[0].astype(BF16) for h in range(2)]
            o2 = jnp.dot(jnp.concatenate(pn, axis=1), vst, preferred_element_type=F32).astype(BF16)
            for r in range(2):
                j = 2 * kh + r
                o_ref[:, 128 * j:128 * (j + 1)] = o2[BLOCK * r:BLOCK * (r + 1)]

    return pl.pallas_call(
        body, grid=(N_BLOCKS,), in_specs=_attn_specs(lambda i: i),
        out_specs=pl.BlockSpec((BLOCK, ATTN_W), lambda i: (i, 0)),
        out_shape=jax.ShapeDtypeStruct((SEQ, ATTN_W), BF16), compiler_params=_cp(), name="attn_fwd",
    )(q, kv, kv, tc, ta, tb, tc, ta, tb, sinks)


def _attn_bwd(q, kv, tabs, sinks, do):
    tc, ta, tb = tabs
    last = N_BLOCKS - 1
    clamp = lambda i: jnp.minimum(i, last)

    def place(full, side, kh):
        left = lax.broadcasted_iota(jnp.int32, full.shape, 1) < HEAD_DIM
        valid = jnp.where(left, full, 0.0) if side == 0 else jnp.where(left, 0.0, full)
        return valid if side == kh else pltpu.roll(valid, HEAD_DIM, axis=1)

    def body(q_ref, kvc_ref, kvp_ref, tc_ref, ta_ref, tb_ref, pc_ref, pa_ref, pb_ref, sink_ref, do_ref,
             dq_ref, dkv_ref, ds_ref, carry_ref):
        i = pl.program_id(0)

        @pl.when(i == 0)
        def _():
            ds_ref[...] = jnp.zeros_like(ds_ref)
            carry_ref[...] = jnp.zeros_like(carry_ref)

        @pl.when(i > last)
        def _():
            dkv_ref[...] = carry_ref[...].astype(BF16)

        @pl.when(i <= last)
        def _():
            qs, ks, vs, (c, a, b) = _attn_load(q_ref, kvc_ref, kvp_ref, tc_ref, ta_ref, tb_ref,
                                               pc_ref, pa_ref, pb_ref)
            ok = _attn_mask(i)
            dk = jnp.zeros((2 * BLOCK, 128), F32)
            dv = jnp.zeros((2 * BLOCK, 128), F32)
            dsink = jnp.zeros((1, 128), F32)
            lane = lax.broadcasted_iota(jnp.int32, (1, 128), 1)
            for kh in range(2):
                q2, kst, vst, sinks = _kv_group(qs, ks, vs, kh, sink_ref)
                do2 = jnp.concatenate([do_ref[:, 128 * (2 * kh + r):128 * (2 * kh + r + 1)] for r in range(2)],
                                      axis=0).astype(BF16)
                s = lax.dot_general(q2, kst, NT, preferred_element_type=F32)
                dp = lax.dot_general(do2, vst, NT, preferred_element_type=F32)
                pns, dss = [], []
                for h in range(2):
                    cols = slice(2 * BLOCK * h, 2 * BLOCK * (h + 1))
                    pn, ps = _attn_probs(s[:, cols], ok, sinks[h])
                    dr = jnp.sum(pn * dp[:, cols], axis=-1, keepdims=True)
                    pns.append(pn.astype(BF16))
                    dss.append((pn * (dp[:, cols] - dr)).astype(BF16))
                    for r in range(2):
                        part = -jnp.sum((ps * dr)[BLOCK * r:BLOCK * (r + 1)])
                        dsink += jnp.where(lane == 4 * kh + 2 * r + h, part, 0.0)
                ds2, pn2 = jnp.concatenate(dss, axis=1), jnp.concatenate(pns, axis=1)
                dq2 = jnp.dot(ds2, kst, preferred_element_type=F32) * (HEAD_DIM ** -0.5)
                dk2 = lax.dot_general(ds2, q2, TN, preferred_element_type=F32)
                dv2 = lax.dot_general(pn2, do2, TN, preferred_element_type=F32)
                for h in range(2):
                    dk += place(dk2[2 * BLOCK * h:2 * BLOCK * (h + 1)], h, kh)
                    dv += place(dv2[2 * BLOCK * h:2 * BLOCK * (h + 1)], h, kh)
                for r in range(2):
                    j = 2 * kh + r
                    dq_ref[:, 128 * j:128 * (j + 1)] = _rope_t(dq2[BLOCK * r:BLOCK * (r + 1)], c, a, b).astype(BF16)
            ds_ref[...] += dsink
            dk_prev = _rope_t(dk[:BLOCK], pc_ref[...], pa_ref[...], pb_ref[...])
            dk_cur = _rope_t(dk[BLOCK:], c, a, b)
            prev = jnp.concatenate([dk_prev, dv[:BLOCK]], axis=1)
            dkv_ref[...] = (carry_ref[...] + prev).astype(BF16)
            carry_ref[...] = jnp.concatenate([dk_cur, dv[BLOCK:]], axis=1)

    return pl.pallas_call(
        body, grid=(N_BLOCKS + 1,),
        in_specs=_attn_specs(clamp) + [pl.BlockSpec((BLOCK, ATTN_W), lambda i: (clamp(i), 0))],
        out_specs=[pl.BlockSpec((BLOCK, ATTN_W), lambda i: (clamp(i), 0)),
                   pl.BlockSpec((BLOCK, 2 * KV_W), lambda i: (jnp.maximum(i - 1, 0), 0)),
                   pl.BlockSpec((1, 128), lambda i: (0, 0))],
        out_shape=[jax.ShapeDtypeStruct((SEQ, ATTN_W), BF16), jax.ShapeDtypeStruct((SEQ, 2 * KV_W), BF16),
                   jax.ShapeDtypeStruct((1, 128), F32)],
        scratch_shapes=[pltpu.VMEM((BLOCK, 2 * KV_W), F32)], compiler_params=_cp(), name="attn_bwd",
    )(q, kv, kv, tc, ta, tb, tc, ta, tb, sinks, do)


def _shift_down(z, k):
    row = lax.broadcasted_iota(jnp.int32, z.shape, 0)
    return jnp.where(row < k, 0.0, pltpu.roll(z, k, axis=0))


def _shift_up(z, k):
    n = z.shape[0]
    row = lax.broadcasted_iota(jnp.int32, z.shape, 0)
    return jnp.where(row >= n - k, 0.0, pltpu.roll(z, n - k, axis=0))


def _conv_specs():
    nb = WIDTH // 128
    return [pl.BlockSpec((SEQ, 128), lambda j: (0, j)), pl.BlockSpec((SEQ, 128), lambda j: (0, nb + j)),
            pl.BlockSpec((SEQ, 128), lambda j: (0, 2 * nb + j)), pl.BlockSpec((None, 8, 128), lambda j: (0, 0, j))]


def _conv_bwd(cbx, cw, layer, dout, tie):
    def body(cb_ref, cc_ref, cx_ref, w_ref, do_ref, tie_ref, dcb_ref, dcc_ref, dcx_ref, dw_ref):
        cc, cx = cc_ref[...], cx_ref[...]
        z = cc * cx
        z1, z2 = _shift_down(z, 1), _shift_down(z, 2)
        w0, w1, w2 = w_ref[0:1, :], w_ref[1:2, :], w_ref[2:3, :]
        dout = do_ref[...]
        ds = dout * cb_ref[...]
        dcb_ref[...] = (dout * (w0 * z2 + w1 * z1 + w2 * z)).astype(BF16)
        dz = w2 * ds + w1 * _shift_up(ds, 1) + w0 * _shift_up(ds, 2)
        dcc_ref[...] = (dz * cx).astype(BF16)
        dcx_ref[...] = (dz * cc).astype(BF16)
        rows = [jnp.sum(ds * zz, axis=0, keepdims=True) for zz in (z2, z1, z)]
        dw_ref[...] = jnp.concatenate(rows + [jnp.zeros((5, 128), F32)], axis=0)

    col = lambda j: (0, j)
    specs = _conv_specs()
    specs[3] = pl.BlockSpec((None, 8, 128), lambda j: (layer, 0, j))
    return pl.pallas_call(
        body, grid=(WIDTH // 128,), in_specs=specs + [pl.BlockSpec((SEQ, 128), col), ANY],
        out_specs=[pl.BlockSpec((SEQ, 128), col), pl.BlockSpec((SEQ, 128), col), pl.BlockSpec((SEQ, 128), col),
                   pl.BlockSpec((8, 128), col)],
        out_shape=[jax.ShapeDtypeStruct((SEQ, WIDTH), BF16)] * 3 + [jax.ShapeDtypeStruct((8, WIDTH), F32)],
        compiler_params=_cp(), name="conv_bwd",
    )(cbx, cbx, cbx, cw, dout, tie)


def _ssm_prep_math(a_re, a_im, log_dt, bt_re, bt_im):
    dt = jnp.exp(log_dt)
    er = jnp.exp(a_re * dt)
    lr = er * jnp.cos(a_im * dt)
    li = er * jnp.sin(a_im * dt)
    n2 = a_re * a_re + a_im * a_im
    cr = ((lr - 1.0) * a_re + li * a_im) / n2
    ci = (li * a_re - (lr - 1.0) * a_im) / n2
    cr3, ci3 = cr[:, None, :], ci[:, None, :]
    return lr, li, cr3 * bt_re - ci3 * bt_im, cr3 * bt_im + ci3 * bt_re


_GS = (SSM_GROUPS, SSM_STATE)
_GHS = (SSM_GROUPS, SSM_GROUP, SSM_STATE)


def _layered(shape):
    return pl.BlockSpec((None,) + shape, lambda l: (l,) + (0,) * len(shape))


def _ssm_prep(a_re, a_im, log_dt, bt_re, bt_im):
    def body(ar, ai, ld, br, bi, o0, o1, o2, o3):
        outs = _ssm_prep_math(ar[...], ai[...], ld[...], br[...], bi[...])
        for o, v in zip((o0, o1, o2, o3), outs):
            o[...] = v

    shapes = [_GS, _GS, _GHS, _GHS]
    return pl.pallas_call(
        body, grid=(DEPTH,), in_specs=[_layered(s) for s in (_GS, _GS, (SSM_GROUPS, 1), _GHS, _GHS)],
        out_specs=[_layered(s) for s in shapes],
        out_shape=[jax.ShapeDtypeStruct((DEPTH,) + s, F32) for s in shapes],
        name="ssm_prep")(a_re, a_im, log_dt, bt_re, bt_im)


def _ssm_prep_bwd(a_re, a_im, log_dt, bt_re, bt_im, cots):
    def body(ar, ai, ld, br, bi, c0, c1, c2, c3, o0, o1, o2, o3, o4):
        _, vjp = jax.vjp(_ssm_prep_math, ar[...], ai[...], ld[...], br[...], bi[...])
        for o, v in zip((o0, o1, o2, o3, o4), vjp((c0[...], c1[...], c2[...], c3[...]))):
            o[...] = v

    ins = (_GS, _GS, (SSM_GROUPS, 1), _GHS, _GHS)
    return pl.pallas_call(
        body, grid=(DEPTH,), in_specs=[_layered(s) for s in ins + (_GS, _GS, _GHS, _GHS)],
        out_specs=[_layered(s) for s in ins],
        out_shape=[jax.ShapeDtypeStruct((DEPTH,) + s, F32) for s in ins],
        name="ssm_prep_bwd")(a_re, a_im, log_dt, bt_re, bt_im, *cots)


LANES_G = 512
N_LANE_GROUPS = SSM_GROUPS * SSM_STATE // LANES_G


def _scan_in_place(xr_ref, xi_ref, ar, ai, reverse):
    shape = (N_CHUNKS, xr_ref.shape[1])
    ar, ai = jnp.broadcast_to(ar, shape), jnp.broadcast_to(ai, shape)

    def rows(tau):
        t = (CHUNK - 1 - tau) if reverse else tau
        return pl.ds(pl.multiple_of(t * N_CHUNKS, N_CHUNKS), N_CHUNKS)

    def step(tau, carry):
        sr, si = carry
        return ar * sr - ai * si + xr_ref[rows(tau), :], ar * si + ai * sr + xi_ref[rows(tau), :]

    zero = jnp.zeros(shape, F32)
    er, ei = lax.fori_loop(0, CHUNK, step, (zero, zero), unroll=8)
    qr, qi = ar, ai
    for _ in range(8):
        qr, qi = qr * qr - qi * qi, 2.0 * qr * qi
    shift = _shift_up if reverse else _shift_down
    for k in (1, 2, 4):
        sr, si = shift(er, k), shift(ei, k)
        er, ei = er + qr * sr - qi * si, ei + qr * si + qi * sr
        qr, qi = qr * qr - qi * qi, 2.0 * qr * qi
    start = (shift(er, 1), shift(ei, 1))

    def write(tau, carry):
        sr, si = step(tau, carry)
        xr_ref[rows(tau), :] = sr
        xi_ref[rows(tau), :] = si
        return sr, si

    return write, start


def _ssm_specs(layer):
    col = lambda w: pl.BlockSpec((SEQ, w), lambda g: (0, g))
    diag = pl.BlockSpec((None, None, 128, LANES_G), lambda g: (layer, g, 0, 0))
    vec = pl.BlockSpec((None, 1, LANES_G), lambda g: (layer, 0, g))
    return col, diag, vec


def _to_scan_order(src_ref, dst_ref):
    def move(tau, _):
        dst_ref[pl.ds(pl.multiple_of(tau * N_CHUNKS, N_CHUNKS), N_CHUNKS), :] = src_ref[pl.ds(tau, N_CHUNKS, stride=CHUNK), :]
        return 0

    lax.fori_loop(0, CHUNK, move, 0, unroll=8)


def _to_time_order(src_ref, dst_ref, dtype):
    for j in range(N_CHUNKS):
        dst_ref[pl.ds(j * CHUNK, CHUNK), :] = src_ref[pl.ds(j, CHUNK, stride=N_CHUNKS), :].astype(dtype)


def _ssm_fwd(u, mats, layer, d):
    def body(u_ref, d_ref, br_ref, bi_ref, cr_ref, ci_ref, ar_ref, ai_ref, xr_ref, xi_ref, y_ref, us_ref):
        _to_scan_order(u_ref, us_ref)
        uv = us_ref[...].astype(BF16)
        xr_ref[...] = jnp.dot(uv, br_ref[...], preferred_element_type=F32)
        xi_ref[...] = jnp.dot(uv, bi_ref[...], preferred_element_type=F32)
        write, start = _scan_in_place(xr_ref, xi_ref, ar_ref[...], ai_ref[...], False)
        lax.fori_loop(0, CHUNK, write, start, unroll=8)
        y = lax.dot_general(xr_ref[...].astype(BF16), cr_ref[...], NT, preferred_element_type=F32)
        y += lax.dot_general(xi_ref[...].astype(BF16), ci_ref[...], NT, preferred_element_type=F32)
        us_ref[...] = y + d_ref[...] * us_ref[...]
        _to_time_order(us_ref, y_ref, F32)

    col, diag, vec = _ssm_specs(layer)
    return pl.pallas_call(
        body, grid=(N_LANE_GROUPS,),
        in_specs=[col(128), pl.BlockSpec((None, 1, 128), lambda g: (layer, 0, g)),
                  diag, diag, diag, diag, vec, vec],
        out_specs=[col(LANES_G), col(LANES_G), col(128)],
        out_shape=[jax.ShapeDtypeStruct((SEQ, SSM_GROUPS * SSM_STATE), F32)] * 2
        + [jax.ShapeDtypeStruct((SEQ, WIDTH), F32)],
        scratch_shapes=[pltpu.VMEM((SEQ, 128), F32)], compiler_params=_cp(), name="ssm_fwd",
    )(u, d, mats["b_re"], mats["b_im"], mats["c_re"], mats["c_im_neg"], mats["a_re"], mats["a_im"])


def _ssm_bwd(dy, x_re, x_im, u, mats, layer, d):
    def body(dyt_ref, ut_ref, d_ref, xr_ref, xi_ref, br_ref, bi_ref, cr_ref, ci_ref, ar_ref, ai_ref,
             du_ref, dar_ref, dai_ref, dbr_ref, dbi_ref, dcr_ref, dci_ref, lr_ref, li_ref, dys_ref, u_ref):
        _to_scan_order(dyt_ref, dys_ref)
        _to_scan_order(ut_ref, u_ref)
        dy = dys_ref[...].astype(BF16)
        lr_ref[...] = jnp.dot(dy, cr_ref[...], preferred_element_type=F32)
        li_ref[...] = jnp.dot(dy, ci_ref[...], preferred_element_type=F32)
        write, start = _scan_in_place(lr_ref, li_ref, ar_ref[...], -ai_ref[...], True)

        def rows(t):
            return pl.ds(pl.multiple_of(t * N_CHUNKS, N_CHUNKS), N_CHUNKS)

        def grad(acc, lam, xpr, xpi):
            return acc[0] + xpr * lam[0] + xpi * lam[1], acc[1] + xpr * lam[1] - xpi * lam[0]

        def down(tau, carry):
            lam = write(tau, carry[0])
            t = CHUNK - 2 - tau
            return lam, grad(carry[1], lam, xr_ref[rows(t), :], xi_ref[rows(t), :])

        zero = jnp.zeros((N_CHUNKS, LANES_G), F32)
        lam, acc = lax.fori_loop(0, CHUNK - 1, down, (start, (zero, zero)), unroll=5)
        lam = write(CHUNK - 1, lam)
        last = rows(CHUNK - 1)
        acc = grad(acc, lam, _shift_down(xr_ref[last, :], 1), _shift_down(xi_ref[last, :], 1))
        dar_ref[...] = jnp.sum(acc[0], axis=0, keepdims=True)
        dai_ref[...] = jnp.sum(acc[1], axis=0, keepdims=True)

        l_re, l_im = lr_ref[...].astype(BF16), li_ref[...].astype(BF16)
        du = lax.dot_general(l_re, br_ref[...], NT, preferred_element_type=F32)
        du += lax.dot_general(l_im, bi_ref[...], NT, preferred_element_type=F32)
        dys_ref[...] = du + dys_ref[...] * d_ref[...]
        _to_time_order(dys_ref, du_ref, BF16)
        uv = u_ref[...].astype(BF16)
        dbr_ref[...] = lax.dot_general(uv, l_re, TN, preferred_element_type=F32)
        dbi_ref[...] = lax.dot_general(uv, l_im, TN, preferred_element_type=F32)
        dcr_ref[...] = lax.dot_general(dy, xr_ref[...].astype(BF16), TN, preferred_element_type=F32)
        dci_ref[...] = lax.dot_general(dy, xi_ref[...].astype(BF16), TN, preferred_element_type=F32)

    col, diag, vec = _ssm_specs(layer)
    out_vec = pl.BlockSpec((1, LANES_G), lambda g: (0, g))
    out_blk = pl.BlockSpec((None, 128, LANES_G), lambda g: (g, 0, 0))
    sds = jax.ShapeDtypeStruct
    return pl.pallas_call(
        body, grid=(N_LANE_GROUPS,),
        in_specs=[col(128), col(128), pl.BlockSpec((None, 1, 128), lambda g: (layer, 0, g)),
                  col(LANES_G), col(LANES_G), diag, diag, diag, diag, vec, vec],
        out_specs=[col(128), out_vec, out_vec, out_blk, out_blk, out_blk, out_blk],
        out_shape=[sds((SEQ, WIDTH), BF16)] + [sds((1, SSM_GROUPS * SSM_STATE), F32)] * 2
        + [sds((N_LANE_GROUPS, 128, LANES_G), F32)] * 4,
        scratch_shapes=[pltpu.VMEM((SEQ, LANES_G), F32)] * 2 + [pltpu.VMEM((SEQ, 128), F32)] * 2,
        compiler_params=_cp(), name="ssm_bwd",
    )(dy, u, d, x_re, x_im, mats["b_re"], mats["b_im"], mats["c_re"], mats["c_im_neg"],
      mats["a_re"], mats["a_im"])


_GELU_C = math.sqrt(2.0 / math.pi)


def _gelu(y):
    return 0.5 * y * (1.0 + jnp.tanh(_GELU_C * (y + 0.044715 * (y * y * y))))


def _glu_fwd(y, wglu):
    tt = 512

    def body(y_ref, w_ref, z_ref):
        ys = _gelu(y_ref[...])
        a = jnp.dot(ys.astype(BF16), w_ref[...], preferred_element_type=F32)
        z_ref[...] = (ys * jax.nn.sigmoid(a)).astype(BF16)

    blk = pl.BlockSpec((tt, WIDTH), lambda i: (i, 0))
    return pl.pallas_call(body, grid=(SEQ // tt,), in_specs=[blk, _full((WIDTH, WIDTH))], out_specs=blk,
                          out_shape=jax.ShapeDtypeStruct((SEQ, WIDTH), BF16), compiler_params=_cp(),
                          name="glu_fwd")(y, wglu)


def _glu_bwd(y, wglu, dz, u):
    tt = 512

    def body(y_ref, w_ref, dz_ref, u_ref, dy_ref, ys_ref, da_ref, dd_ref):
        @pl.when(pl.program_id(0) == 0)
        def _():
            dd_ref[...] = jnp.zeros_like(dd_ref)

        yv = y_ref[...]
        t = jnp.tanh(_GELU_C * (yv + 0.044715 * (yv * yv * yv)))
        ys = 0.5 * yv * (1.0 + t)
        ysb = ys.astype(BF16)
        sg = jax.nn.sigmoid(jnp.dot(ysb, w_ref[...], preferred_element_type=F32))
        dz = dz_ref[...].astype(F32)
        da = (dz * ys * sg * (1.0 - sg)).astype(BF16)
        dys = dz * sg + lax.dot_general(da, w_ref[...], NT, preferred_element_type=F32)
        dy = dys * (0.5 * (1.0 + t) + 0.5 * yv * (1.0 - t * t) * _GELU_C * (1.0 + 3 * 0.044715 * (yv * yv)))
        dy_ref[...] = dy
        ys_ref[...] = ysb
        da_ref[...] = da
        dd_ref[...] += jnp.sum(dy * u_ref[...], axis=0, keepdims=True)

    blk = pl.BlockSpec((tt, WIDTH), lambda i: (i, 0))
    return pl.pallas_call(
        body, grid=(SEQ // tt,), in_specs=[blk, _full((WIDTH, WIDTH)), blk, blk],
        out_specs=[blk, blk, blk, _full((1, WIDTH))],
        out_shape=[jax.ShapeDtypeStruct((SEQ, WIDTH), F32)] + [jax.ShapeDtypeStruct((SEQ, WIDTH), BF16)] * 2
        + [jax.ShapeDtypeStruct((1, WIDTH), F32)],
        compiler_params=_cp(), name="glu_bwd")(y, wglu, dz, u)


def _mix_specs(tt, layer):
    row = lambda w: pl.BlockSpec((tt, w), lambda i: (i, 0))
    gate = lambda j: pl.BlockSpec((tt, D_MODEL), lambda i: (i, j))
    wo = lambda j: pl.BlockSpec((D_MODEL, WIDTH), lambda i: (0, j))
    return [row(D_MODEL), row(WIDTH), row(WIDTH), row(WIDTH), gate(0), gate(1), gate(2),
            pl.BlockSpec((None, 1, GATE_W), lambda i: (layer, 0, 0)), wo(0), wo(1), wo(2),
            _full((D_MODEL, D_MODEL))]


def _mix_branches(o_ref, c_ref, z_ref, g_refs, b_ref, wa_ref, wc_ref, ws_ref):
    ys = [lax.dot_general(r[...], w[...], NT, preferred_element_type=F32)
          for r, w in ((o_ref, wa_ref), (c_ref, wc_ref), (z_ref, ws_ref))]
    gates = [jax.nn.sigmoid(g_refs[j][...] + b_ref[:, D_MODEL * j:D_MODEL * (j + 1)]) for j in range(3)]
    return ys, gates


def _mix_fwd(x, o, cv, z, glog, b_gate, layer, wbt, wmix, tie):
    tt = 256

    def body(x_ref, o_ref, c_ref, z_ref, g0, g1, g2, b_ref, wa_ref, wc_ref, ws_ref, wm_ref, tie_ref, x1_ref):
        ys, gates = _mix_branches(o_ref, c_ref, z_ref, (g0, g1, g2), b_ref, wa_ref, wc_ref, ws_ref)
        merged = gates[0] * ys[0] + gates[1] * ys[1] + gates[2] * ys[2]
        x1_ref[...] = x_ref[...] + jnp.dot(merged.astype(BF16), wm_ref[...], preferred_element_type=F32)

    return pl.pallas_call(
        body, grid=(SEQ // tt,), in_specs=_mix_specs(tt, layer) + [ANY],
        out_specs=pl.BlockSpec((tt, D_MODEL), lambda i: (i, 0)),
        out_shape=jax.ShapeDtypeStruct((SEQ, D_MODEL), F32), compiler_params=_cp(), name="mix_fwd",
    )(x, o, cv, z, glog, glog, glog, b_gate, wbt, wbt, wbt, wmix, tie)


def _mix_bwd(dx1, o, cv, z, glog, b_gate, layer, wbt, wmix, tie):
    tt = 256

    def body(dx_ref, o_ref, c_ref, z_ref, g0, g1, g2, b_ref, wa_ref, wc_ref, ws_ref, wm_ref, tie_ref,
             mg_ref, dya_ref, dyc_ref, dys_ref, do_ref, dc_ref, dz_ref, dgl_ref, db_ref):
        @pl.when(pl.program_id(0) == 0)
        def _():
            db_ref[...] = jnp.zeros_like(db_ref)

        ys, gates = _mix_branches(o_ref, c_ref, z_ref, (g0, g1, g2), b_ref, wa_ref, wc_ref, ws_ref)
        mg_ref[...] = (gates[0] * ys[0] + gates[1] * ys[1] + gates[2] * ys[2]).astype(BF16)
        dm = lax.dot_general(dx_ref[...].astype(BF16), wm_ref[...], NT, preferred_element_type=F32)
        for j, (dy_ref, w_ref, d_ref) in enumerate(((dya_ref, wa_ref, do_ref), (dyc_ref, wc_ref, dc_ref),
                                                    (dys_ref, ws_ref, dz_ref))):
            dy = (dm * gates[j]).astype(BF16)
            dy_ref[...] = dy
            d_ref[...] = jnp.dot(dy, w_ref[...], preferred_element_type=F32)
            dgl = dm * ys[j] * gates[j] * (1.0 - gates[j])
            dgl_ref[:, D_MODEL * j:D_MODEL * (j + 1)] = dgl.astype(BF16)
            db_ref[:, D_MODEL * j:D_MODEL * (j + 1)] += jnp.sum(dgl, axis=0, keepdims=True)

    row = lambda w: pl.BlockSpec((tt, w), lambda i: (i, 0))
    sds = jax.ShapeDtypeStruct
    return pl.pallas_call(
        body, grid=(SEQ // tt,), in_specs=_mix_specs(tt, layer) + [ANY],
        out_specs=[row(D_MODEL)] * 4 + [row(WIDTH)] * 3 + [row(GATE_W), _full((1, GATE_W))],
        out_shape=[sds((SEQ, D_MODEL), BF16)] * 4 + [sds((SEQ, WIDTH), F32)] * 3
        + [sds((SEQ, GATE_W), BF16), sds((1, GATE_W), F32)],
        compiler_params=_cp(), name="mix_bwd",
    )(dx1, o, cv, z, glog, glog, glog, b_gate, wbt, wbt, wbt, wmix, tie)


def _ffn_out_fwd(x1, act, wout, tie):
    tt = 512

    def body(x_ref, a_ref, w_ref, tie_ref, o_ref):
        o_ref[...] = x_ref[...] + jnp.dot(a_ref[...], w_ref[...], preferred_element_type=F32)

    row = lambda w: pl.BlockSpec((tt, w), lambda i: (i, 0))
    return pl.pallas_call(
        body, grid=(SEQ // tt,), in_specs=[row(D_MODEL), row(FFN_H), _full((FFN_H, D_MODEL)), ANY],
        out_specs=row(D_MODEL), out_shape=jax.ShapeDtypeStruct((SEQ, D_MODEL), F32),
        compiler_params=_cp(), name="ffn_out_fwd")(x1, act, wout, tie)


def _ffn_out_bwd(dx2, up, silu, dsilu, wout, tie):
    tt = 512

    def body(dx_ref, up_ref, silu_ref, dsilu_ref, w_ref, tie_ref, dgu_ref):
        dact = lax.dot_general(dx_ref[...].astype(BF16), w_ref[...], NT, preferred_element_type=F32).astype(BF16)
        dgu_ref[:, :FFN_H] = dact * up_ref[...] * dsilu_ref[...]
        dgu_ref[:, FFN_H:] = dact * silu_ref[...]

    row = lambda w: pl.BlockSpec((tt, w), lambda i: (i, 0))
    return pl.pallas_call(
        body, grid=(SEQ // tt,),
        in_specs=[row(D_MODEL), row(FFN_H), row(FFN_H), row(FFN_H), _resident((FFN_H, D_MODEL)), ANY],
        out_specs=row(2 * FFN_H), out_shape=jax.ShapeDtypeStruct((SEQ, 2 * FFN_H), BF16),
        compiler_params=_cp(), name="ffn_out_bwd")(dx2, up, silu, dsilu, wout, tie)


def _loss_head(x, g, target):
    tt = 256

    def body(x_ref, g_ref, t_ref, loss_ref, dx_ref, dg_ref):
        @pl.when(pl.program_id(0) == 0)
        def _():
            loss_ref[...] = jnp.zeros_like(loss_ref)
            dg_ref[...] = jnp.zeros_like(dg_ref)

        xv = x_ref[...]
        r = lax.rsqrt(jnp.mean(xv * xv, axis=-1, keepdims=True) + NORM_EPS)
        xh = xv * r
        err = xh * g_ref[...] - t_ref[...]
        loss_ref[...] += 0.5 * jnp.sum(jnp.mean(err * err, axis=-1, keepdims=True))
        dy = err * (1.0 / D_MODEL)
        gy = dy * g_ref[...]
        dx_ref[...] = r * (gy - xh * jnp.mean(gy * xh, axis=-1, keepdims=True))
        dg_ref[...] += jnp.sum(dy * xh, axis=0, keepdims=True)

    row = pl.BlockSpec((tt, D_MODEL), lambda i: (i, 0))
    return pl.pallas_call(
        body, grid=(SEQ // tt,), in_specs=[row, _full((1, D_MODEL)), row],
        out_specs=[_full((1, 128)), row, _full((1, D_MODEL))],
        out_shape=[jax.ShapeDtypeStruct((1, 128), F32), jax.ShapeDtypeStruct((SEQ, D_MODEL), F32),
                   jax.ShapeDtypeStruct((1, D_MODEL), F32)],
        compiler_params=_cp(), name="loss_head")(x, g, target)


def _adam_math(g, w, m, v):
    nm = B1 * m + (1.0 - B1) * g
    nv = B2 * v + (1.0 - B2) * (g * g)
    m_hat = nm / (1.0 - B1 ** STEP)
    v_hat = nv / (1.0 - B2 ** STEP)
    return -LR * (m_hat / (jnp.sqrt(v_hat) + ADAM_EPS) + WD * w), nm, nv


def _adamw_small(parts, w, m, v, name):
    def body(p_ref, w_ref, m_ref, v_ref, g_ref, d_ref, nm_ref, nv_ref):
        g = p_ref[0].astype(F32)
        for k in range(1, N_DEV):
            g = g + p_ref[k].astype(F32)
        g_ref[...] = g
        d_ref[...], nm_ref[...], nv_ref[...] = _adam_math(g, w_ref[...], m_ref[...], v_ref[...])

    out_shape = [jax.ShapeDtypeStruct(w.shape, F32)] * 4
    if w.ndim < 3:
        return pl.pallas_call(body, out_shape=out_shape, name=name)(parts, w, m, v)
    rest = w.shape[1:]
    zeros = (0,) * len(rest)
    blk = pl.BlockSpec((None,) + rest, lambda l: (l,) + zeros)
    return pl.pallas_call(
        body, grid=(w.shape[0],),
        in_specs=[pl.BlockSpec((N_DEV, None) + rest, lambda l: (0, l) + zeros), blk, blk, blk],
        out_specs=[blk] * 4, out_shape=out_shape, name=name)(parts, w, m, v)


def _adamw(parts, w, m, v, tr, name, groups=None, fill=None, tie=None):
    n_groups, rows, cols = w.shape
    n_parts = parts.shape[1]
    lo, hi = groups if groups is not None else (0, n_groups)

    def body(p_ref, w_ref, m_ref, v_ref, *rest):
        g_ref, d_ref, nm_ref, nv_ref = rest[-4:]
        g = p_ref[0].astype(F32)
        for k in range(1, n_parts):
            g = g + p_ref[k].astype(F32)
        nm = B1 * m_ref[...] + (1.0 - B1) * g
        nv = B2 * v_ref[...] + (1.0 - B2) * (g * g)
        m_hat = nm / (1.0 - B1 ** STEP)
        v_hat = nv / (1.0 - B2 ** STEP)
        g_ref[...] = g
        d_ref[...] = -LR * (m_hat / (jnp.sqrt(v_hat) + ADAM_EPS) + WD * w_ref[...])
        nm_ref[...] = nm
        nv_ref[...] = nv

    blk = pl.BlockSpec((None, tr, cols), lambda l, i: (l + lo, i, 0))
    p_lo = lo if parts.shape[0] == n_groups else 0
    extra = ([] if fill is None else list(fill)) + ([] if tie is None else [tie])
    return pl.pallas_call(
        body, grid=(hi - lo, rows // tr),
        in_specs=[pl.BlockSpec((None, n_parts, tr, cols), lambda l, i: (l + p_lo, 0, i, 0)), blk, blk, blk]
        + [ANY] * len(extra),
        out_specs=[blk] * 4, out_shape=[jax.ShapeDtypeStruct((n_groups, rows, cols), F32)] * 4,
        input_output_aliases={} if fill is None else {4 + j: j for j in range(4)},
        compiler_params=_cp(), name=name)(parts, w, m, v, *extra)


def _split_start(name, arrays, n_sems, plan, after=None):
    n = len(arrays)
    order = [] if after is None else [after]
    n_in = n + len(order)

    def body(*refs):
        ins, send_sems, recv_sems, token = refs[:n], refs[n_in], refs[n_in + 1], refs[-1]
        for src, dst, k, to in plan(ins)[0]:
            pltpu.make_async_remote_copy(src_ref=src, dst_ref=dst, send_sem=send_sems.at[k], recv_sem=recv_sems.at[k],
                                         device_id=to, device_id_type=MESH_ID).start()
        token[...] = jnp.zeros_like(token)

    outs = pl.pallas_call(
        body, name=name,
        out_shape=(pltpu.SemaphoreType.DMA((n_sems,)), pltpu.SemaphoreType.DMA((n_sems,)),
                   *[pltpu.HBM(a.shape, a.dtype) for a in arrays], jax.ShapeDtypeStruct((8, 128), F32)),
        in_specs=[HBM] * n + [ANY] * len(order),
        out_specs=(SEM, SEM, *[HBM] * n, pl.BlockSpec(memory_space=pltpu.VMEM)),
        input_output_aliases={i: 2 + i for i in range(n)},
        compiler_params=pltpu.CompilerParams(has_side_effects=EFFECT),
    )(*[pltpu.with_memory_space_constraint(a, pltpu.HBM) for a in arrays], *order)
    return outs[0], outs[1], list(outs[2:2 + n]), outs[-1]


def _split_wait(name, arrays, send_sems, recv_sems, after, plan):
    n = len(arrays)
    order = list(after) if isinstance(after, (list, tuple)) else [after]

    def body(*refs):
        ins, s_sems, r_sems = refs[:n], refs[n], refs[n + 1]
        sends, arrivals = plan(ins)
        x, y, c = lax.axis_index("x"), lax.axis_index("y"), lax.axis_index("c")
        for src, dst, k, to in sends:
            pltpu.make_async_remote_copy(src_ref=src, dst_ref=dst, send_sem=s_sems.at[k], recv_sem=r_sems.at[k],
                                         device_id=to, device_id_type=MESH_ID).wait_send()
        for dst, k in arrivals:
            pltpu.make_async_remote_copy(src_ref=dst, dst_ref=dst, send_sem=s_sems.at[k], recv_sem=r_sems.at[k],
                                         device_id=(x, y, c), device_id_type=MESH_ID).wait_recv()

    return pl.pallas_call(
        body, name=name, out_shape=[pltpu.HBM(a.shape, a.dtype) for a in arrays],
        in_specs=[HBM] * n + [SEM, SEM] + [ANY] * len(order), out_specs=[HBM] * n,
        input_output_aliases={i: i for i in range(n)},
        compiler_params=pltpu.CompilerParams(has_side_effects=EFFECT),
    )(*arrays, send_sems, recv_sems, *order)


def _chips():
    x, y, c = lax.axis_index("x"), lax.axis_index("y"), lax.axis_index("c")
    return x, y, c, [(1 - x, y), (x, 1 - y), (1 - x, 1 - y)]


def _plan_gather_chips(refs):
    x, y, c, chips = _chips()
    me = 4 * x + 2 * y + c
    n = len(refs) // 2
    sends, arrivals = [], []
    for i in range(n):
        src, land = refs[i], refs[n + i]
        sends.append((src, land.at[me], 4 * i, (x, y, 1 - c)))
        arrivals.append((land.at[4 * x + 2 * y + 1 - c], 4 * i))
        for j, (px, py) in enumerate(chips):
            sends.append((src, land.at[me], 4 * i + 1 + j, (px, py, c)))
            arrivals.append((land.at[4 * px + 2 * py + c], 4 * i + 1 + j))
    return sends, arrivals


def _plan_gather_pass(refs):
    x, y, c, chips = _chips()
    sends, arrivals = [], []
    for i in range(len(refs)):
        for j, (px, py) in enumerate(chips):
            slot = refs[i].at[4 * px + 2 * py + c]
            sends.append((slot, slot, 4 * i + j, (x, y, 1 - c)))
            arrivals.append((refs[i].at[4 * px + 2 * py + 1 - c], 4 * i + j))
        back = refs[i].at[4 * x + 2 * y + 1 - c]
        sends.append((back, back, 4 * i + 3, (x, y, 1 - c)))
        arrivals.append((refs[i].at[4 * x + 2 * y + c], 4 * i + 3))
    return sends, arrivals


def _plan_scatter_pair(refs):
    x, y, c = lax.axis_index("x"), lax.axis_index("y"), lax.axis_index("c")
    n = len(refs) // 2
    sends, arrivals = [], []
    for i in range(n):
        for q in range(4):
            sends.append((refs[i].at[q, 1 - c], refs[n + i].at[q], 4 * i + q, (x, y, 1 - c)))
            arrivals.append((refs[n + i].at[q], 4 * i + q))
    return sends, arrivals


def _plan_scatter_chips(layer):
    def plan(refs):
        x, y, c, chips = _chips()
        n = len(refs) // 2
        sends, arrivals = [], []
        for i in range(n):
            for j, (px, py) in enumerate(chips):
                sends.append((refs[i].at[2 * px + py], refs[n + i].at[layer, 2 * x + y], 3 * i + j, (px, py, c)))
                arrivals.append((refs[n + i].at[layer, 2 * px + py], 3 * i + j))
        return sends, arrivals

    return plan


def _pair_sum(parts4, from_pair, landing, layer, core, tr, name):
    _, _, rows, cols = parts4.shape

    def body(c_ref, p_ref, s_ref, l_ref, sum_ref, land_ref):
        v = (p_ref[...].astype(F32) + s_ref[...].astype(F32)).astype(BF16)
        sum_ref[...] = v
        land_ref[...] = v

    blk = pl.BlockSpec((None, tr, cols), lambda q, i, c_ref: (q, i, 0))
    return pl.pallas_call(
        body,
        grid_spec=pltpu.PrefetchScalarGridSpec(
            num_scalar_prefetch=1, grid=(4, rows // tr),
            in_specs=[pl.BlockSpec((None, None, tr, cols), lambda q, i, c_ref: (q, c_ref[0], i, 0)), blk, ANY],
            out_specs=[blk, pl.BlockSpec((None, None, tr, cols), lambda q, i, c_ref: (layer, q, i, 0))]),
        out_shape=[jax.ShapeDtypeStruct((4, rows, cols), BF16), jax.ShapeDtypeStruct(landing.shape, BF16)],
        input_output_aliases={3: 1}, compiler_params=_cp(), name=name,
    )(core, parts4, from_pair, landing)


def _travel_layout(t):
    tr = lambda a: jnp.swapaxes(a, 1, 2)
    branch = jnp.concatenate([tr(t["w_attn_o"]), tr(t["w_conv_o"]), tr(t["w_ssm_o"])], axis=2)
    return [tr(t["w_in"]), tr(t["w_ffn_in"]), t["w_ffn_out"], t["w_mix_o"], branch, t["w_ssm_glu"]]


def _native_layout(a):
    tr = lambda x: jnp.swapaxes(x, 1, 2)
    b = a[4]
    return {"w_in": tr(a[0]), "w_ffn_in": tr(a[1]), "w_ffn_out": a[2], "w_mix_o": a[3],
            "w_attn_o": tr(b[:, :, :WIDTH]), "w_conv_o": tr(b[:, :, WIDTH:2 * WIDTH]),
            "w_ssm_o": tr(b[:, :, 2 * WIDTH:]), "w_ssm_glu": a[5]}


def _embed(t):
    eye = jnp.eye(8, dtype=t.dtype)
    t = t.reshape(DEPTH, N_LANE_GROUPS, 8, SSM_GROUP, SSM_STATE)
    return (t[:, :, :, :, None, :] * eye[None, None, :, None, :, None]).reshape(DEPTH, N_LANE_GROUPS, 128, LANES_G)


def _diag_blocks(t):
    t = t.reshape(DEPTH, N_LANE_GROUPS, 8, SSM_GROUP, 8, SSM_STATE)
    return jnp.einsum("lgahap->lgahp", t).reshape(DEPTH, SSM_GROUPS, SSM_GROUP, SSM_STATE)


def _rope_tabs():
    pos = jnp.arange(SEQ, dtype=F32)
    inv_freq = ROPE_THETA ** (-jnp.arange(0, ROT_DIM, 2, dtype=F32) / ROT_DIM)
    ang = pos[:, None] * inv_freq[None, :]
    cos, sin = jnp.cos(ang), jnp.sin(ang)
    one, zero = jnp.ones((SEQ, HEAD_DIM - ROT_DIM), F32), jnp.zeros((SEQ, HEAD_DIM - ROT_DIM), F32)
    z8 = jnp.zeros((SEQ, 8), F32)
    head = lambda *p: jnp.tile(jnp.concatenate(p, axis=1), (1, 2))
    return head(cos, cos, one), head(-sin, z8, zero), head(z8, sin, zero)


def _ssm_mats(sp):
    lr, li, bbr, bbi = _ssm_prep(sp["a_re"], sp["a_im"], sp["log_dt"], sp["bt_re"], sp["bt_im"])
    lanes = SSM_GROUPS * SSM_STATE
    return {
        "a_re": lr.reshape(DEPTH, 1, lanes), "a_im": li.reshape(DEPTH, 1, lanes),
        "b_re": _embed(bbr).astype(BF16), "b_im": _embed(bbi).astype(BF16),
        "c_re": _embed(sp["c_re"]).astype(BF16), "c_im_neg": _embed(-sp["c_im"]).astype(BF16),
    }


def _layer_fwd(x, i, w, rp, mats, tabs, tie, hooks):
    q, kv, cbx, u, glog, cv, h = _rms_mm_in(x, rp["norm_mix"][i], w["win_t"], tabs, rp["conv_w"], i, tie)
    o = _attn_fwd(q, kv, tabs, rp["attn_sinks"][i])
    x_re, x_im, y = _ssm_fwd(u, mats, i, rp["ssm_d"])
    z = _glu_fwd(y, w["wglu"])
    x1 = _mix_fwd(x, o, cv, z, glog, rp["b_gate"], i, w["branch_t"], w["wmix"], hooks["early"](z))
    hooks["pre_ffn"](x1)
    act, up, silu, dsilu, h2 = _rms_mm_ffn(x1, rp["norm_ffn"][i], w["wffn_t"])
    x2 = _ffn_out_fwd(x1, act, w["wout"], hooks["mid"](h2))
    kept = dict(x=x, q=q, kv=kv, cbx=cbx, u=u, glog=glog, h=h, o=o, cv=cv, z=z, y=y,
                x_re=x_re, x_im=x_im, x1=x1, act=act, up=up, silu=silu, dsilu=dsilu, h2=h2)
    return x2, kept


def _layer_bwd(dx2, k, i, w, rp, mats, tabs, tie, hooks):
    dgu = _ffn_out_bwd(dx2, k["up"], k["silu"], k["dsilu"], w["wout"], tie)
    g_wout = _mm_tn(k["act"], dx2, tm=FFN_H // 2, tn=1024, name="mm_tn_ffn_out")
    g_wffn_t = _mm_tn(dgu, k["h2"], tm=FFN_H // 2, tn=1024, name="mm_tn_ffn_in")
    dx1, d_norm_ffn = _mm_rmsbwd([dgu], w["wffn_t"], k["x1"], rp["norm_ffn"][i], dx2, "mm_rmsbwd_ffn")

    mg, dya, dyc, dys, do, dcv, dz, dgl, db_gate = _mix_bwd(
        dx1, k["o"], k["cv"], k["z"], k["glog"], rp["b_gate"], i, w["branch_t"], w["wmix"],
        hooks["mid"]((g_wffn_t, g_wout, d_norm_ffn)))
    g_wmix = _mm_tn(mg, dx1, tm=1024, tn=512, name="mm_tn_mix")
    g_branch_t = _tn_branches((dya, dyc, dys), (k["o"], k["cv"], k["z"]))

    dy, ys16, da16, dd = _glu_bwd(k["y"], w["wglu"], dz, k["u"])
    g_wglu = _mm_tn(ys16, da16, tm=256, tn=512, name="mm_tn_glu")
    du, da_re, da_im, db_re, db_im, dc_re, dc_im = _ssm_bwd(dy, k["x_re"], k["x_im"], k["u"], mats, i, rp["ssm_d"])

    dcb, dcc, dcx, d_conv_w = _conv_bwd(k["cbx"], rp["conv_w"], i, dcv, hooks["late"](du))
    dq, dkv, d_sinks = _attn_bwd(k["q"], k["kv"], tabs, rp["attn_sinks"][i], do)

    pieces = [dq, dkv, dcb, dcc, dcx, du, dgl]
    g_win_t = _tn_pieces(pieces, k["h"])
    dx, d_norm_mix = _mm_rmsbwd(pieces, w["win_t"], k["x"], rp["norm_mix"][i], dx1, "mm_rmsbwd_in")

    grads = [g_win_t, g_wffn_t, g_wout, g_wmix, g_branch_t, g_wglu]
    small = dict(norm_mix=d_norm_mix, b_gate=db_gate, attn_sinks=d_sinks, ssm_d=dd, norm_ffn=d_norm_ffn,
                 conv_w=d_conv_w, da_re=da_re, da_im=da_im, db_re=db_re, db_im=db_im, dc_re=dc_re, dc_im=dc_im)
    return dx, grads, small


def _replicated_grads(sg, sp):
    stack = lambda name: jnp.stack([sg[i][name] for i in range(DEPTH)])
    cots = (stack("da_re").reshape(DEPTH, *_GS), stack("da_im").reshape(DEPTH, *_GS),
            _diag_blocks(stack("db_re")), _diag_blocks(stack("db_im")))
    d_a_re, d_a_im, d_log_dt, d_bt_re, d_bt_im = _ssm_prep_bwd(
        sp["a_re"], sp["a_im"], sp["log_dt"], sp["bt_re"], sp["bt_im"], cots)
    sgrads = {"norm_mix": stack("norm_mix"), "b_gate": stack("b_gate"),
              "attn_sinks": stack("attn_sinks")[:, :, :N_Q_HEADS], "ssm_a_re": d_a_re, "ssm_a_im": d_a_im,
              "ssm_b_re": jnp.swapaxes(d_bt_re, 2, 3), "ssm_b_im": jnp.swapaxes(d_bt_im, 2, 3),
              "ssm_c_re": _diag_blocks(stack("dc_re")), "ssm_c_im": -_diag_blocks(stack("dc_im")),
              "ssm_d": stack("ssm_d"), "ssm_log_dt": d_log_dt, "norm_ffn": stack("norm_ffn")}
    return sgrads, stack("conv_w")[:, :3]


def kernel(x, norm_mix, w_in, b_gate, attn_sinks, w_attn_o, conv_w, w_conv_o, ssm_a_re, ssm_a_im, ssm_b_re, ssm_b_im, ssm_c_re, ssm_c_im, ssm_d, ssm_log_dt, w_ssm_glu, w_ssm_o, w_mix_o, norm_ffn, w_ffn_in, w_ffn_out, norm_final, loss_target, m_norm_mix, m_w_in, m_b_gate, m_attn_sinks, m_w_attn_o, m_conv_w, m_w_conv_o, m_ssm_a_re, m_ssm_a_im, m_ssm_b_re, m_ssm_b_im, m_ssm_c_re, m_ssm_c_im, m_ssm_d, m_ssm_log_dt, m_w_ssm_glu, m_w_ssm_o, m_w_mix_o, m_norm_ffn, m_w_ffn_in, m_w_ffn_out, m_norm_final, v_norm_mix, v_w_in, v_b_gate, v_attn_sinks, v_w_attn_o, v_conv_w, v_w_conv_o, v_ssm_a_re, v_ssm_a_im, v_ssm_b_re, v_ssm_b_im, v_ssm_c_re, v_ssm_c_im, v_ssm_d, v_ssm_log_dt, v_w_ssm_glu, v_w_ssm_o, v_w_mix_o, v_norm_ffn, v_w_ffn_in, v_w_ffn_out, v_norm_final):
    big = {"w": dict(w_in=w_in, w_attn_o=w_attn_o, w_conv_o=w_conv_o, w_ssm_glu=w_ssm_glu, w_ssm_o=w_ssm_o,
                     w_mix_o=w_mix_o, w_ffn_in=w_ffn_in, w_ffn_out=w_ffn_out),
           "m": dict(w_in=m_w_in, w_attn_o=m_w_attn_o, w_conv_o=m_w_conv_o, w_ssm_glu=m_w_ssm_glu,
                     w_ssm_o=m_w_ssm_o, w_mix_o=m_w_mix_o, w_ffn_in=m_w_ffn_in, w_ffn_out=m_w_ffn_out),
           "v": dict(w_in=v_w_in, w_attn_o=v_w_attn_o, w_conv_o=v_w_conv_o, w_ssm_glu=v_w_ssm_glu,
                     w_ssm_o=v_w_ssm_o, w_mix_o=v_w_mix_o, w_ffn_in=v_w_ffn_in, w_ffn_out=v_w_ffn_out)}
    small = {"w": dict(norm_mix=norm_mix, b_gate=b_gate, attn_sinks=attn_sinks, ssm_a_re=ssm_a_re,
                       ssm_a_im=ssm_a_im, ssm_b_re=ssm_b_re, ssm_b_im=ssm_b_im, ssm_c_re=ssm_c_re,
                       ssm_c_im=ssm_c_im, ssm_d=ssm_d, ssm_log_dt=ssm_log_dt, norm_ffn=norm_ffn),
             "m": dict(norm_mix=m_norm_mix, b_gate=m_b_gate, attn_sinks=m_attn_sinks, ssm_a_re=m_ssm_a_re,
                       ssm_a_im=m_ssm_a_im, ssm_b_re=m_ssm_b_re, ssm_b_im=m_ssm_b_im, ssm_c_re=m_ssm_c_re,
                       ssm_c_im=m_ssm_c_im, ssm_d=m_ssm_d, ssm_log_dt=m_ssm_log_dt, norm_ffn=m_norm_ffn),
             "v": dict(norm_mix=v_norm_mix, b_gate=v_b_gate, attn_sinks=v_attn_sinks, ssm_a_re=v_ssm_a_re,
                       ssm_a_im=v_ssm_a_im, ssm_b_re=v_ssm_b_re, ssm_b_im=v_ssm_b_im, ssm_c_re=v_ssm_c_re,
                       ssm_c_im=v_ssm_c_im, ssm_d=v_ssm_d, ssm_log_dt=v_ssm_log_dt, norm_ffn=v_norm_ffn)}
    finals = {"w": norm_final, "m": m_norm_final, "v": v_norm_final}
    convs = {"w": conv_w, "m": m_conv_w, "v": v_conv_w}
    small_out_shapes = {name: a.shape for name, a in small["w"].items()}
    small_out_shapes.update(norm_final=(D_MODEL,), conv_w=(DEPTH, 3, 64))
    small_shapes = dict(small_out_shapes, norm_final=(1, D_MODEL), conv_w=(DEPTH, 3, WIDTH))
    dense = ("ssm_b_re", "ssm_b_im", "ssm_c_re", "ssm_c_im")
    for name in dense:
        small_shapes[name] = (DEPTH, SSM_GROUPS, SSM_GROUP * SSM_STATE)
    small_wmv = {name: [(convs[s] if name == "conv_w" else finals[s] if name == "norm_final" else small[s][name])
                        .reshape((DEPTH, 3, 64) if name == "conv_w" else small_shapes[name]) for s in "wmv"]
                 for name in small_shapes}
    mine = 4 * lax.axis_index("x") + 2 * lax.axis_index("y") + lax.axis_index("c")

    travel = {s: _travel_layout(big[s]) for s in "wmv"}
    stacked16 = list(zip(*[[a[0] for a in _travel_layout({n: w[i:i + 1].astype(BF16) for n, w in big["w"].items()})]
                           for i in range(DEPTH)]))
    rp = {"norm_mix": norm_mix[:, None], "norm_ffn": norm_ffn[:, None], "attn_sinks": attn_sinks[:, None],
          "b_gate": b_gate[:, None], "ssm_d": ssm_d[:, None]}
    sp = {"a_re": ssm_a_re, "a_im": ssm_a_im, "log_dt": ssm_log_dt[:, :, None],
          "bt_re": jnp.swapaxes(ssm_b_re, 2, 3), "bt_im": jnp.swapaxes(ssm_b_im, 2, 3),
          "c_re": ssm_c_re, "c_im": ssm_c_im}
    rows_tile = {"win_t": 368, "wffn_t": 352, "wout": 352, "wmix": 128, "branch_t": 128, "wglu": 64}
    core = lax.axis_index("c").astype(jnp.int32).reshape(1)
    no_tie = jnp.zeros((8, 128), F32)

    def landing_zones(srcs):
        return [lax.empty((N_DEV,) + s.shape, s.dtype) for s in srcs]

    def gather_chips(tag, i, kinds, after, extra=()):
        srcs = [stacked16[j][i] for j in kinds] + list(extra)
        s_sems, r_sems, arrays, token = _split_start(
            f"gather_chips_start_{tag}", srcs + landing_zones(srcs), 4 * len(srcs), _plan_gather_chips, after)
        return (tag, s_sems, r_sems, arrays), token

    def gather_pass(state, after):
        tag, s_sems, r_sems, arrays = state
        arrays = _split_wait(f"gather_chips_wait_{tag}", arrays, s_sems, r_sems, after, _plan_gather_chips)
        n = len(arrays) // 2
        s_sems, r_sems, lands, token = _split_start(
            f"gather_pass_start_{tag}", list(arrays[n:]), 4 * n, _plan_gather_pass)
        return (tag, s_sems, r_sems, lands), token

    def gather_done(state, after, kinds):
        tag, s_sems, r_sems, lands = state
        lands = _split_wait(f"gather_pass_wait_{tag}", lands, s_sems, r_sems, after, _plan_gather_pass)
        named = {KINDS[j][0]: a.reshape(N_DEV * KINDS[j][1], KINDS[j][2]) for a, j in zip(lands, kinds)}
        return named, list(lands[len(kinds):])

    all_kinds, mixer_kinds, ffn_kinds = tuple(range(len(KINDS))), (0, 3, 4, 5), (1, 2)
    no_hooks = {name: (lambda value: no_tie) for name in ("early", "pre_ffn", "mid", "late")}
    state, token = gather_chips("0m", 0, mixer_kinds, None, extra=[jnp.pad(conv_w.reshape(6, 128), ((0, 2), (0, 0)))])
    mats = _ssm_mats(dict(sp, log_dt=sp["log_dt"] + token[0, 0]))
    tabs = _rope_tabs()
    early_work = list(mats.values()) + list(tabs) + [a for name in dense for a in small_wmv[name]]
    early_work += [stacked16[j][0] for j in ffn_kinds] + [stacked16[j][1] for j in mixer_kinds]
    state, _ = gather_pass(state, early_work)
    ffn_state, tie = gather_chips("0f", 0, ffn_kinds, state[3][0])
    w_next, (conv_all,) = gather_done(state, tabs[2], mixer_kinds)
    conv_full = conv_all[:, :6].reshape(N_DEV, DEPTH, 3, 64).transpose(1, 2, 0, 3).reshape(DEPTH, 3, WIDTH)
    rp["conv_w"] = jnp.pad(conv_full, ((0, 0), (0, 5), (0, 0)))

    act = x[0]
    weights, kept = [], []
    for i in range(DEPTH):
        w_i, hooks, held = w_next, dict(no_hooks), {}

        def early(value, ffn_state=ffn_state, held=held):
            held["ffn"], token = gather_pass(ffn_state, value)
            return token

        def pre_ffn(value, w_i=w_i, held=held):
            w_i.update(gather_done(held["ffn"], value, ffn_kinds)[0])

        hooks.update(early=early, pre_ffn=pre_ffn)
        if i + 1 < DEPTH:
            state, tie = gather_chips(f"{i + 1}m", i + 1, mixer_kinds, tie if i == 0 else w_i["win_t"])

            def mid(value, i=i, state=state, held=held):
                held["next"], token = gather_pass(state, value)
                held["next_ffn"], token = gather_chips(f"{i + 1}f", i + 1, ffn_kinds, token)
                return token

            hooks.update(mid=mid)
        act, k = _layer_fwd(act, i, w_i, rp, mats, tabs, tie, hooks)
        if i + 1 < DEPTH:
            w_next, _ = gather_done(held["next"], act, mixer_kinds)
            ffn_state, tie = held["next_ffn"], no_tie
        weights.append(w_i)
        kept.append(k)
    loss_row, dx, d_norm_final = _loss_head(act, norm_final[None], loss_target[0])

    landings = [lax.empty((DEPTH, 4, r, c), BF16) for _, r, c in KINDS]
    landings0 = [lax.empty((1, 4, r, c), BF16) for _, r, c in KINDS]

    def scatter_pair(tag, kinds, grads, after):
        parts4 = [g.reshape(4, 2, KINDS[j][1], KINDS[j][2]) for g, j in zip(grads, kinds)]
        zones = [lax.empty((4, KINDS[j][1], KINDS[j][2]), BF16) for j in kinds]
        s_sems, r_sems, arrays, token = _split_start(
            f"scatter_pair_start_{tag}", parts4 + zones, 4 * len(kinds), _plan_scatter_pair, after)
        return (tag, kinds, s_sems, r_sems, arrays), token

    def scatter_chips(state, lands, slot, after):
        tag, kinds, s_sems, r_sems, arrays = state
        arrays = _split_wait(f"scatter_pair_wait_{tag}", arrays, s_sems, r_sems, after, _plan_scatter_pair)
        n = len(kinds)
        sums, mine_lands = [], []
        for k, j in enumerate(kinds):
            name = KINDS[j][0]
            chip_sum, land = _pair_sum(arrays[k], arrays[n + k], lands[j], slot, core, KINDS[j][1],
                                       f"pair_sum_{name}")
            sums.append(chip_sum)
            mine_lands.append(land)
        s_sems, r_sems, arrays, token = _split_start(
            f"scatter_chips_start_{tag}", sums + mine_lands, 3 * n, _plan_scatter_chips(slot))
        return (tag, kinds, slot, s_sems, r_sems, arrays), token

    def scatter_done(state, lands, after):
        tag, kinds, slot, s_sems, r_sems, arrays = state
        arrays = _split_wait(f"scatter_chips_wait_{tag}", arrays, s_sems, r_sems, after, _plan_scatter_chips(slot))
        lands = list(lands)
        for k, j in enumerate(kinds):
            lands[j] = arrays[len(kinds) + k]
        return lands

    sg = [None] * DEPTH
    pending, tie = None, no_tie
    for i in reversed(range(DEPTH)):
        hooks, held = dict(no_hooks), {}
        if pending is not None:
            def mid(value, i=i, pending=pending, held=held):
                held["chips"], token = scatter_chips(pending, landings, i + 1, value[2])
                if i == 0:
                    held["ffn_pair"], token = scatter_pair("0f", ffn_kinds, value[:2], token)
                return token

            hooks.update(mid=mid)
        if i == 0:
            def late(value, held=held):
                held["ffn_chips"], token = scatter_chips(held["ffn_pair"], landings0, 0, value)
                return token

            hooks.update(late=late)
        dx, grads, sg[i] = _layer_bwd(dx, kept[i], i, weights[i], rp, mats, tabs, tie, hooks)
        if pending is not None:
            landings = scatter_done(held["chips"], landings, dx)
        if i > 0:
            pending, tie = scatter_pair(str(i), all_kinds, grads, dx)
        else:
            pending, _ = scatter_pair("0m", mixer_kinds, [grads[j] for j in mixer_kinds], dx)

    sgrads, conv_grad = _replicated_grads(sg, sp)

    small_names = list(REPLICATED) + ["norm_final", "conv_w"]
    sgrads.update(norm_final=d_norm_final, conv_w=conv_grad)
    small_src = [sgrads[name].reshape(small_shapes[name]).astype(BF16) for name in small_names]
    small_src.append(jnp.broadcast_to(loss_row[:, :1], (8, 128)))
    last, tie = scatter_chips(pending, landings0, 0, small_src[0])
    s_sems, r_sems, arrays, tie = _split_start(
        "gather_small_chips_start", small_src + landing_zones(small_src), 4 * len(small_src), _plan_gather_chips, tie)
    small_state = ("small", s_sems, r_sems, arrays)

    big_out = []
    for j, (name, _, _) in enumerate(KINDS):
        big_out.append(_adamw(landings[j], travel["w"][j], travel["m"][j], travel["v"][j], rows_tile[name],
                              "adamw_late_" + name, groups=(1, DEPTH), tie=tie))
        tie = big_out[-1][3]
    landings0 = scatter_done(held["ffn_chips"], landings0, tie)
    landings0 = scatter_done(last, landings0, tie)
    small_state, _ = gather_pass(small_state, landings0[0])
    big_out = [_adamw(landings0[j], travel["w"][j], travel["m"][j], travel["v"][j], rows_tile[name],
                      "adamw_first_" + name, groups=(0, 1), fill=big_out[j]) for j, (name, _, _) in enumerate(KINDS)]
    big_res = [_native_layout([big_out[j][kind] for j in range(len(KINDS))]) for kind in range(4)]

    _, sparts = gather_done(small_state, big_out[-1][0], ())
    loss = jnp.sum(sparts[-1][:, 0, 0])
    sparts = dict(zip(small_names, sparts))
    sparts["conv_w"] = lax.dynamic_slice_in_dim(sparts["conv_w"], mine * 64, 64, axis=3)
    small_res = {}
    for name in small_names:
        res = _adamw_small(sparts[name], *small_wmv[name], "adamw_" + name)
        small_res[name] = [r.reshape(small_out_shapes[name]) for r in res]

    order = ["norm_mix", "w_in", "b_gate", "attn_sinks", "w_attn_o", "conv_w", "w_conv_o", "ssm_a_re", "ssm_a_im",
             "ssm_b_re", "ssm_b_im", "ssm_c_re", "ssm_c_im", "ssm_d", "ssm_log_dt", "w_ssm_glu", "w_ssm_o",
             "w_mix_o", "norm_ffn", "w_ffn_in", "w_ffn_out", "norm_final"]
    outs = [loss, dx[None]]
    for kind in range(4):
        for name in order:
            outs.append(big_res[kind][name] if name in big_res[kind] else small_res[name][kind])
    return tuple(outs)
```

```python
import math

import jax
import jax.numpy as jnp
from jax import lax
from jax.experimental import pallas as pl
from jax.experimental.pallas import tpu as pltpu

F32 = jnp.float32
BF16 = jnp.bfloat16

N_DEV = 8
DEPTH = 4
SEQ = 2048
D_MODEL = 1024
N_Q_HEADS = 8
HEAD_DIM = 64
ATTN_W = 512
KV_W = 128
BLOCK = 128
N_BLOCKS = SEQ // BLOCK
ROPE_THETA = 500000.0
ROT_DIM = 16
NEG_INF = -1e30
WIDTH = 512
SSM_GROUPS = 32
SSM_GROUP = 16
SSM_STATE = 64
CHUNK = 256
N_CHUNKS = SEQ // CHUNK
GATE_W = 3 * D_MODEL
IN_COLS = 5888
FFN_H = 2816
NORM_EPS = 1e-6
LR, B1, B2, ADAM_EPS, WD, STEP = 0.001, 0.9, 0.999, 1e-08, 0.01, 10

COL_Q, COL_KV, COL_CBX, COL_U, COL_G = 0, 512, 768, 2304, 2816
PIECE_W = (512, 256, 512, 512, 512, 512, 3072)
PIECE_OFF = tuple(sum(PIECE_W[:i]) for i in range(len(PIECE_W)))

KINDS = (("win_t", 736, 1024), ("wffn_t", 704, 1024), ("wout", 352, 1024), ("wmix", 128, 1024),
         ("branch_t", 128, 1536), ("wglu", 64, 512))

REPLICATED = ("norm_mix", "b_gate", "attn_sinks", "ssm_a_re", "ssm_a_im", "ssm_b_re", "ssm_b_im", "ssm_c_re",
              "ssm_c_im", "ssm_d", "ssm_log_dt", "norm_ffn")

VMEM_LIMIT = 56 * 1024 * 1024
NT = (((1,), (1,)), ((), ()))
TN = (((0,), (0,)), ((), ()))
MESH_ID = pl.DeviceIdType.MESH
ANY = pl.BlockSpec(memory_space=pl.ANY)
HBM = pl.BlockSpec(memory_space=pltpu.HBM)
SEM = pl.BlockSpec(memory_space=pltpu.SEMAPHORE)
EFFECT = pltpu.SideEffectType.DATAFLOW_SIDE_EFFECTING


def _cp(**kw):
    return pltpu.CompilerParams(vmem_limit_bytes=VMEM_LIMIT, **kw)


def _full(shape):
    return pl.BlockSpec(shape, lambda *_: (0,) * len(shape))


def _resident(shape):
    return pl.BlockSpec(shape, lambda *_: (0,) * len(shape), pipeline_mode=pl.Buffered(1))


def _mm_tn(a, b, *, tm, tn, name):
    k, m = a.shape
    n = b.shape[1]

    def body(a_ref, b_ref, o_ref):
        o_ref[...] = lax.dot_general(a_ref[...].astype(BF16), b_ref[...].astype(BF16), TN,
                                     preferred_element_type=F32).astype(BF16)

    return pl.pallas_call(
        body, grid=(m // tm, n // tn),
        in_specs=[pl.BlockSpec((k, tm), lambda i, j: (0, i)), pl.BlockSpec((k, tn), lambda i, j: (0, j))],
        out_specs=pl.BlockSpec((tm, tn), lambda i, j: (i, j)),
        out_shape=jax.ShapeDtypeStruct((m, n), BF16), compiler_params=_cp(), name=name)(a, b)


def _rms_rows(xv, g):
    r = lax.rsqrt(jnp.mean(xv * xv, axis=-1, keepdims=True) + NORM_EPS)
    return ((xv * r) * g).astype(BF16)


def _rms_mm_in(x, g, wt, tabs, cw, layer, tie):
    tt = 512
    widths = (3 * WIDTH, WIDTH, GATE_W)
    offs = (COL_CBX, COL_U, COL_G)

    def body(x_ref, g_ref, w_ref, tc_ref, ta_ref, tb_ref, cw_ref, tie_ref,
             q_ref, kv_ref, cbx_ref, u_ref, gl_ref, cv_ref, h_ref, tail_ref):
        @pl.when(pl.program_id(0) == 0)
        def _():
            tail_ref[...] = jnp.zeros_like(tail_ref)

        h = _rms_rows(x_ref[...], g_ref[...])
        h_ref[...] = h
        prod = lax.dot_general(h, w_ref[...], NT, preferred_element_type=F32)
        for ref, o, w in zip((cbx_ref, u_ref, gl_ref), offs, widths):
            ref[...] = prod[:, o:o + w]
        c, a, b = tc_ref[...], ta_ref[...], tb_ref[...]
        for j in range(ATTN_W // 128):
            q_ref[:, 128 * j:128 * (j + 1)] = _rope(prod[:, 128 * j:128 * (j + 1)], c, a, b) * (HEAD_DIM ** -0.5)
        kv_ref[:, :KV_W] = _rope(prod[:, COL_KV:COL_KV + KV_W], c, a, b)
        kv_ref[:, KV_W:] = prod[:, COL_KV + KV_W:COL_CBX]

        row = lax.broadcasted_iota(jnp.int32, (tt, 128), 0)
        for j in range(WIDTH // 128):
            cols = slice(128 * j, 128 * (j + 1))
            cb = prod[:, COL_CBX + 128 * j:COL_CBX + 128 * (j + 1)]
            z = prod[:, COL_CBX + WIDTH + 128 * j:COL_CBX + WIDTH + 128 * (j + 1)] \
                * prod[:, COL_CBX + 2 * WIDTH + 128 * j:COL_CBX + 2 * WIDTH + 128 * (j + 1)]
            before1, before2 = tail_ref[7:8, cols], tail_ref[6:7, cols]
            z1 = jnp.where(row == 0, before1, pltpu.roll(z, 1, axis=0))
            z2 = jnp.where(row == 0, before2, jnp.where(row == 1, before1, pltpu.roll(z, 2, axis=0)))
            s = cw_ref[0:1, cols] * z2 + cw_ref[1:2, cols] * z1 + cw_ref[2:3, cols] * z
            cv_ref[:, cols] = (cb * s).astype(BF16)
            tail_ref[:, cols] = z[tt - 8:, :]

    row_spec = lambda w: pl.BlockSpec((tt, w), lambda i: (i, 0))
    sds = jax.ShapeDtypeStruct
    return pl.pallas_call(
        body, grid=(SEQ // tt,),
        in_specs=[row_spec(D_MODEL), _full((1, D_MODEL)), _resident((IN_COLS, D_MODEL)),
                  row_spec(128), row_spec(128), row_spec(128),
                  pl.BlockSpec((None, 8, WIDTH), lambda i: (layer, 0, 0)), ANY],
        out_specs=[row_spec(ATTN_W), row_spec(2 * KV_W), row_spec(3 * WIDTH), row_spec(WIDTH), row_spec(GATE_W),
                   row_spec(WIDTH), row_spec(D_MODEL)],
        out_shape=[sds((SEQ, ATTN_W), F32), sds((SEQ, 2 * KV_W), F32), sds((SEQ, 3 * WIDTH), F32),
                   sds((SEQ, WIDTH), F32), sds((SEQ, GATE_W), F32), sds((SEQ, WIDTH), BF16),
                   sds((SEQ, D_MODEL), BF16)],
        scratch_shapes=[pltpu.VMEM((8, WIDTH), F32)], compiler_params=_cp(), name="rms_mm_in",
    )(x, g, wt, *tabs, cw, tie)


def _rms_mm_ffn(x, g, wt):
    tt = 256

    def body(x_ref, g_ref, w_ref, act_ref, up_ref, silu_ref, dsilu_ref, h_ref):
        h = _rms_rows(x_ref[...], g_ref[...])
        h_ref[...] = h
        prod = lax.dot_general(h, w_ref[...], NT, preferred_element_type=F32)
        gt, up = prod[:, :FFN_H], prod[:, FFN_H:]
        sg = jax.nn.sigmoid(gt)
        silu = gt * sg
        act_ref[...] = (silu * up).astype(BF16)
        up_ref[...] = up.astype(BF16)
        silu_ref[...] = silu.astype(BF16)
        dsilu_ref[...] = (sg + silu * (1.0 - sg)).astype(BF16)

    row = lambda w: pl.BlockSpec((tt, w), lambda i: (i, 0))
    return pl.pallas_call(
        body, grid=(SEQ // tt,), in_specs=[row(D_MODEL), _full((1, D_MODEL)), _resident((2 * FFN_H, D_MODEL))],
        out_specs=[row(FFN_H)] * 4 + [row(D_MODEL)],
        out_shape=[jax.ShapeDtypeStruct((SEQ, FFN_H), BF16)] * 4 + [jax.ShapeDtypeStruct((SEQ, D_MODEL), BF16)],
        compiler_params=_cp(), name="rms_mm_ffn")(x, g, wt)


def _mm_rmsbwd(pieces, wt, x, g, dres, name):
    tt = 512
    widths = [p.shape[1] for p in pieces]
    offs = [sum(widths[:i]) for i in range(len(widths))]
    n = len(pieces)

    def body(*refs):
        p_refs, (w_ref, x_ref, g_ref, r_ref, dx_ref, dg_ref) = refs[:n], refs[n:]

        @pl.when(pl.program_id(0) == 0)
        def _():
            dg_ref[...] = jnp.zeros_like(dg_ref)

        dh = jnp.zeros((tt, D_MODEL), F32)
        for p_ref, o, w in zip(p_refs, offs, widths):
            dh += jnp.dot(p_ref[...], w_ref[o:o + w, :], preferred_element_type=F32)
        xv = x_ref[...]
        r = lax.rsqrt(jnp.mean(xv * xv, axis=-1, keepdims=True) + NORM_EPS)
        xh = xv * r
        gy = dh * g_ref[...]
        dx_ref[...] = r_ref[...] + r * (gy - xh * jnp.mean(gy * xh, axis=-1, keepdims=True))
        dg_ref[...] += jnp.sum(dh * xh, axis=0, keepdims=True)

    row = lambda w: pl.BlockSpec((tt, w), lambda i: (i, 0))
    return pl.pallas_call(
        body, grid=(SEQ // tt,),
        in_specs=[row(w) for w in widths] + [_resident(wt.shape), row(D_MODEL), _full((1, D_MODEL)), row(D_MODEL)],
        out_specs=[row(D_MODEL), _full((1, D_MODEL))],
        out_shape=[jax.ShapeDtypeStruct((SEQ, D_MODEL), F32), jax.ShapeDtypeStruct((1, D_MODEL), F32)],
        compiler_params=_cp(), name=name)(*pieces, wt, x, g, dres)


def _tn_pieces(pieces, h):
    tk, tn = 512, 512
    nk = SEQ // tk
    n = len(pieces)

    def body(*refs):
        p_refs, (h_ref, o_ref, acc_ref) = refs[:n], refs[n:]
        kk = pl.program_id(1)

        @pl.when(kk == 0)
        def _():
            acc_ref[...] = jnp.zeros_like(acc_ref)

        hv = h_ref[...]
        for p_ref, o, w in zip(p_refs, PIECE_OFF, PIECE_W):
            acc_ref[o:o + w, :] += lax.dot_general(p_ref[...], hv, TN, preferred_element_type=F32)

        @pl.when(kk == nk - 1)
        def _():
            o_ref[...] = acc_ref[...].astype(BF16)

    return pl.pallas_call(
        body, grid=(D_MODEL // tn, nk),
        in_specs=[pl.BlockSpec((tk, w), lambda j, kk: (kk, 0)) for w in PIECE_W]
        + [pl.BlockSpec((tk, tn), lambda j, kk: (kk, j))],
        out_specs=pl.BlockSpec((IN_COLS, tn), lambda j, kk: (0, j)),
        out_shape=jax.ShapeDtypeStruct((IN_COLS, D_MODEL), BF16),
        scratch_shapes=[pltpu.VMEM((IN_COLS, tn), F32)], compiler_params=_cp(), name="tn_pieces")(*pieces, h)


def _tn_branches(dys, acts):
    tk = 512
    nk = SEQ // tk

    def body(d0, d1, d2, a0, a1, a2, o_ref, acc_ref):
        kk = pl.program_id(0)

        @pl.when(kk == 0)
        def _():
            acc_ref[...] = jnp.zeros_like(acc_ref)

        for j, (d, a) in enumerate(((d0, a0), (d1, a1), (d2, a2))):
            acc_ref[:, WIDTH * j:WIDTH * (j + 1)] += lax.dot_general(d[...], a[...], TN, preferred_element_type=F32)

        @pl.when(kk == nk - 1)
        def _():
            o_ref[...] = acc_ref[...].astype(BF16)

    row = lambda w: pl.BlockSpec((tk, w), lambda kk: (kk, 0))
    return pl.pallas_call(
        body, grid=(nk,), in_specs=[row(D_MODEL)] * 3 + [row(WIDTH)] * 3,
        out_specs=_full((D_MODEL, 3 * WIDTH)), out_shape=jax.ShapeDtypeStruct((D_MODEL, 3 * WIDTH), BF16),
        scratch_shapes=[pltpu.VMEM((D_MODEL, 3 * WIDTH), F32)], compiler_params=_cp(), name="tn_branches",
    )(*dys, *acts)


def _rope(t, c, a, b):
    return t * c + pltpu.roll(t, 120, axis=1) * a + pltpu.roll(t, 8, axis=1) * b


def _rope_t(d, c, a, b):
    return d * c + pltpu.roll(d * a, 8, axis=1) + pltpu.roll(d * b, 120, axis=1)


def _band_sides(band):
    left = lax.broadcasted_iota(jnp.int32, band.shape, 1) < HEAD_DIM
    h0 = jnp.where(left, band, 0.0)
    h1 = jnp.where(left, 0.0, band)
    r0 = pltpu.roll(h0, HEAD_DIM, axis=1)
    r1 = pltpu.roll(h1, HEAD_DIM, axis=1)
    return ((h0.astype(BF16), r0.astype(BF16)), (r1.astype(BF16), h1.astype(BF16)))


def _attn_mask(i):
    qi = lax.broadcasted_iota(jnp.int32, (2 * BLOCK, 2 * BLOCK), 0) % BLOCK
    kj = lax.broadcasted_iota(jnp.int32, (2 * BLOCK, 2 * BLOCK), 1)
    delta = qi + BLOCK - kj
    return (delta >= 0) & (delta < BLOCK) & ((kj >= BLOCK) | (i > 0))


def _attn_probs(s, ok, sink):
    s = jnp.where(ok, s, NEG_INF)
    m = jnp.maximum(jnp.max(s, axis=-1, keepdims=True), sink)
    p = jnp.exp(s - m)
    es = jnp.exp(sink - m)
    inv = 1.0 / (jnp.sum(p, axis=-1, keepdims=True) + es)
    return p * inv, es * inv


def _kv_group(qs, ks, vs, kh, sink_ref):
    q2 = jnp.concatenate([qs[2 * kh], qs[2 * kh + 1]], axis=0)
    kst = jnp.concatenate([ks[kh][0], ks[kh][1]], axis=0)
    vst = jnp.concatenate([vs[kh][0], vs[kh][1]], axis=0)
    top = lax.broadcasted_iota(jnp.int32, (2 * BLOCK, 1), 0) < BLOCK
    sinks = [jnp.where(top, sink_ref[0, 4 * kh + h], sink_ref[0, 4 * kh + 2 + h]) for h in range(2)]
    return q2, kst, vst, sinks


def _attn_load(q_ref, kvc_ref, kvp_ref, tc_ref, ta_ref, tb_ref, pc_ref, pa_ref, pb_ref):
    c, a, b = tc_ref[...], ta_ref[...], tb_ref[...]
    kband = jnp.concatenate([kvp_ref[:, :KV_W], kvc_ref[:, :KV_W]], axis=0)
    vband = jnp.concatenate([kvp_ref[:, KV_W:], kvc_ref[:, KV_W:]], axis=0)
    qs = [q_ref[:, 128 * j:128 * (j + 1)].astype(BF16) for j in range(4)]
    return qs, _band_sides(kband), _band_sides(vband), (c, a, b)


def _attn_specs(clamp):
    cur = lambda i: (clamp(i), 0)
    prev = lambda i: (jnp.maximum(clamp(i) - 1, 0), 0)
    return [
        pl.BlockSpec((BLOCK, ATTN_W), cur), pl.BlockSpec((BLOCK, 2 * KV_W), cur),
        pl.BlockSpec((BLOCK, 2 * KV_W), prev),
        pl.BlockSpec((BLOCK, 128), cur), pl.BlockSpec((BLOCK, 128), cur), pl.BlockSpec((BLOCK, 128), cur),
        pl.BlockSpec((BLOCK, 128), prev), pl.BlockSpec((BLOCK, 128), prev), pl.BlockSpec((BLOCK, 128), prev),
        pl.BlockSpec(memory_space=pltpu.SMEM),
    ]


def _attn_fwd(q, kv, tabs, sinks):
    tc, ta, tb = tabs

    def body(q_ref, kvc_ref, kvp_ref, tc_ref, ta_ref, tb_ref, pc_ref, pa_ref, pb_ref, sink_ref, o_ref):
        i = pl.program_id(0)
        qs, ks, vs, _ = _attn_load(q_ref, kvc_ref, kvp_ref, tc_ref, ta_ref, tb_ref, pc_ref, pa_ref, pb_ref)
        ok = _attn_mask(i)
        for kh in range(2):
            q2, kst, vst, sinks = _kv_group(qs, ks, vs, kh, sink_ref)
            s = lax.dot_general(q2, kst, NT, preferred_element_type=F32)
            pn = [_attn_probs(s[:, 2 * BLOCK * h:2 * BLOCK * (h + 1)], ok, sinks[h])[0].astype(BF16) for h in range(2)]
            o2 = jnp.dot(jnp.concatenate(pn, axis=1), vst, preferred_element_type=F32).astype(BF16)
            for r in range(2):
                j = 2 * kh + r
                o_ref[:, 128 * j:128 * (j + 1)] = o2[BLOCK * r:BLOCK * (r + 1)]

    return pl.pallas_call(
        body, grid=(N_BLOCKS,), in_specs=_attn_specs(lambda i: i),
        out_specs=pl.BlockSpec((BLOCK, ATTN_W), lambda i: (i, 0)),
        out_shape=jax.ShapeDtypeStruct((SEQ, ATTN_W), BF16), compiler_params=_cp(), name="attn_fwd",
    )(q, kv, kv, tc, ta, tb, tc, ta, tb, sinks)


def _attn_bwd(q, kv, tabs, sinks, do):
    tc, ta, tb = tabs
    last = N_BLOCKS - 1
    clamp = lambda i: jnp.minimum(i, last)

    def place(full, side, kh):
        left = lax.broadcasted_iota(jnp.int32, full.shape, 1) < HEAD_DIM
        valid = jnp.where(left, full, 0.0) if side == 0 else jnp.where(left, 0.0, full)
        return valid if side == kh else pltpu.roll(valid, HEAD_DIM, axis=1)

    def body(q_ref, kvc_ref, kvp_ref, tc_ref, ta_ref, tb_ref, pc_ref, pa_ref, pb_ref, sink_ref, do_ref,
             dq_ref, dkv_ref, ds_ref, carry_ref):
        i = pl.program_id(0)

        @pl.when(i == 0)
        def _():
            ds_ref[...] = jnp.zeros_like(ds_ref)
            carry_ref[...] = jnp.zeros_like(carry_ref)

        @pl.when(i > last)
        def _():
            dkv_ref[...] = carry_ref[...].astype(BF16)

        @pl.when(i <= last)
        def _():
            qs, ks, vs, (c, a, b) = _attn_load(q_ref, kvc_ref, kvp_ref, tc_ref, ta_ref, tb_ref,
                                               pc_ref, pa_ref, pb_ref)
            ok = _attn_mask(i)
            dk = jnp.zeros((2 * BLOCK, 128), F32)
            dv = jnp.zeros((2 * BLOCK, 128), F32)
            dsink = jnp.zeros((1, 128), F32)
            lane = lax.broadcasted_iota(jnp.int32, (1, 128), 1)
            for kh in range(2):
                q2, kst, vst, sinks = _kv_group(qs, ks, vs, kh, sink_ref)
                do2 = jnp.concatenate([do_ref[:, 128 * (2 * kh + r):128 * (2 * kh + r + 1)] for r in range(2)],
                                      axis=0).astype(BF16)
                s = lax.dot_general(q2, kst, NT, preferred_element_type=F32)
                dp = lax.dot_general(do2, vst, NT, preferred_element_type=F32)
                pns, dss = [], []
                for h in range(2):
                    cols = slice(2 * BLOCK * h, 2 * BLOCK * (h + 1))
                    pn, ps = _attn_probs(s[:, cols], ok, sinks[h])
                    dr = jnp.sum(pn * dp[:, cols], axis=-1, keepdims=True)
                    pns.append(pn.astype(BF16))
                    dss.append((pn * (dp[:, cols] - dr)).astype(BF16))
                    for r in range(2):
                        part = -jnp.sum((ps * dr)[BLOCK * r:BLOCK * (r + 1)])
                        dsink += jnp.where(lane == 4 * kh + 2 * r + h, part, 0.0)
                ds2, pn2 = jnp.concatenate(dss, axis=1), jnp.concatenate(pns, axis=1)
                dq2 = jnp.dot(ds2, kst, preferred_element_type=F32) * (HEAD_DIM ** -0.5)
                dk2 = lax.dot_general(ds2, q2, TN, preferred_element_type=F32)
                dv2 = lax.dot_general(pn2, do2, TN, preferred_element_type=F32)
                for h in range(2):
                    dk += place(dk2[2 * BLOCK * h:2 * BLOCK * (h + 1)], h, kh)
                    dv += place(dv2[2 * BLOCK * h:2 * BLOCK * (h + 1)], h, kh)
                for r in range(2):
                    j = 2 * kh + r
                    dq_ref[:, 128 * j:128 * (j + 1)] = _rope_t(dq2[BLOCK * r:BLOCK * (r + 1)], c, a, b).astype(BF16)
            ds_ref[...] += dsink
            dk_prev = _rope_t(dk[:BLOCK], pc_ref[...], pa_ref[...], pb_ref[...])
            dk_cur = _rope_t(dk[BLOCK:], c, a, b)
            prev = jnp.concatenate([dk_prev, dv[:BLOCK]], axis=1)
            dkv_ref[...] = (carry_ref[...] + prev).astype(BF16)
            carry_ref[...] = jnp.concatenate([dk_cur, dv[BLOCK:]], axis=1)

    return pl.pallas_call(
        body, grid=(N_BLOCKS + 1,),
        in_specs=_attn_specs(clamp) + [pl.BlockSpec((BLOCK, ATTN_W), lambda i: (clamp(i), 0))],
        out_specs=[pl.BlockSpec((BLOCK, ATTN_W), lambda i: (clamp(i), 0)),
                   pl.BlockSpec((BLOCK, 2 * KV_W), lambda i: (jnp.maximum(i - 1, 0), 0)),
                   pl.BlockSpec((1, 128), lambda i: (0, 0))],
        out_shape=[jax.ShapeDtypeStruct((SEQ, ATTN_W), BF16), jax.ShapeDtypeStruct((SEQ, 2 * KV_W), BF16),
                   jax.ShapeDtypeStruct((1, 128), F32)],
        scratch_shapes=[pltpu.VMEM((BLOCK, 2 * KV_W), F32)], compiler_params=_cp(), name="attn_bwd",
    )(q, kv, kv, tc, ta, tb, tc, ta, tb, sinks, do)


def _shift_down(z, k):
    row = lax.broadcasted_iota(jnp.int32, z.shape, 0)
    return jnp.where(row < k, 0.0, pltpu.roll(z, k, axis=0))


def _shift_up(z, k):
    n = z.shape[0]
    row = lax.broadcasted_iota(jnp.int32, z.shape, 0)
    return jnp.where(row >= n - k, 0.0, pltpu.roll(z, n - k, axis=0))


def _conv_specs():
    nb = WIDTH // 128
    return [pl.BlockSpec((SEQ, 128), lambda j: (0, j)), pl.BlockSpec((SEQ, 128), lambda j: (0, nb + j)),
            pl.BlockSpec((SEQ, 128), lambda j: (0, 2 * nb + j)), pl.BlockSpec((None, 8, 128), lambda j: (0, 0, j))]


def _conv_bwd(cbx, cw, layer, dout, tie):
    def body(cb_ref, cc_ref, cx_ref, w_ref, do_ref, tie_ref, dcb_ref, dcc_ref, dcx_ref, dw_ref):
        cc, cx = cc_ref[...], cx_ref[...]
        z = cc * cx
        z1, z2 = _shift_down(z, 1), _shift_down(z, 2)
        w0, w1, w2 = w_ref[0:1, :], w_ref[1:2, :], w_ref[2:3, :]
        dout = do_ref[...]
        ds = dout * cb_ref[...]
        dcb_ref[...] = (dout * (w0 * z2 + w1 * z1 + w2 * z)).astype(BF16)
        dz = w2 * ds + w1 * _shift_up(ds, 1) + w0 * _shift_up(ds, 2)
        dcc_ref[...] = (dz * cx).astype(BF16)
        dcx_ref[...] = (dz * cc).astype(BF16)
        rows = [jnp.sum(ds * zz, axis=0, keepdims=True) for zz in (z2, z1, z)]
        dw_ref[...] = jnp.concatenate(rows + [jnp.zeros((5, 128), F32)], axis=0)

    col = lambda j: (0, j)
    specs = _conv_specs()
    specs[3] = pl.BlockSpec((None, 8, 128), lambda j: (layer, 0, j))
    return pl.pallas_call(
        body, grid=(WIDTH // 128,), in_specs=specs + [pl.BlockSpec((SEQ, 128), col), ANY],
        out_specs=[pl.BlockSpec((SEQ, 128), col), pl.BlockSpec((SEQ, 128), col), pl.BlockSpec((SEQ, 128), col),
                   pl.BlockSpec((8, 128), col)],
        out_shape=[jax.ShapeDtypeStruct((SEQ, WIDTH), BF16)] * 3 + [jax.ShapeDtypeStruct((8, WIDTH), F32)],
        compiler_params=_cp(), name="conv_bwd",
    )(cbx, cbx, cbx, cw, dout, tie)


def _ssm_prep_math(a_re, a_im, log_dt, bt_re, bt_im):
    dt = jnp.exp(log_dt)
    er = jnp.exp(a_re * dt)
    lr = er * jnp.cos(a_im * dt)
    li = er * jnp.sin(a_im * dt)
    n2 = a_re * a_re + a_im * a_im
    cr = ((lr - 1.0) * a_re + li * a_im) / n2
    ci = (li * a_re - (lr - 1.0) * a_im) / n2
    cr3, ci3 = cr[:, None, :], ci[:, None, :]
    return lr, li, cr3 * bt_re - ci3 * bt_im, cr3 * bt_im + ci3 * bt_re


_GS = (SSM_GROUPS, SSM_STATE)
_GHS = (SSM_GROUPS, SSM_GROUP, SSM_STATE)


def _layered(shape):
    return pl.BlockSpec((None,) + shape, lambda l: (l,) + (0,) * len(shape))


def _ssm_prep(a_re, a_im, log_dt, bt_re, bt_im):
    def body(ar, ai, ld, br, bi, o0, o1, o2, o3):
        outs = _ssm_prep_math(ar[...], ai[...], ld[...], br[...], bi[...])
        for o, v in zip((o0, o1, o2, o3), outs):
            o[...] = v

    shapes = [_GS, _GS, _GHS, _GHS]
    return pl.pallas_call(
        body, grid=(DEPTH,), in_specs=[_layered(s) for s in (_GS, _GS, (SSM_GROUPS, 1), _GHS, _GHS)],
        out_specs=[_layered(s) for s in shapes],
        out_shape=[jax.ShapeDtypeStruct((DEPTH,) + s, F32) for s in shapes],
        name="ssm_prep")(a_re, a_im, log_dt, bt_re, bt_im)


def _ssm_prep_bwd(a_re, a_im, log_dt, bt_re, bt_im, cots):
    def body(ar, ai, ld, br, bi, c0, c1, c2, c3, o0, o1, o2, o3, o4):
        _, vjp = jax.vjp(_ssm_prep_math, ar[...], ai[...], ld[...], br[...], bi[...])
        for o, v in zip((o0, o1, o2, o3, o4), vjp((c0[...], c1[...], c2[...], c3[...]))):
            o[...] = v

    ins = (_GS, _GS, (SSM_GROUPS, 1), _GHS, _GHS)
    return pl.pallas_call(
        body, grid=(DEPTH,), in_specs=[_layered(s) for s in ins + (_GS, _GS, _GHS, _GHS)],
        out_specs=[_layered(s) for s in ins],
        out_shape=[jax.ShapeDtypeStruct((DEPTH,) + s, F32) for s in ins],
        name="ssm_prep_bwd")(a_re, a_im, log_dt, bt_re, bt_im, *cots)


LANES_G = 512
N_LANE_GROUPS = SSM_GROUPS * SSM_STATE // LANES_G


def _scan_in_place(xr_ref, xi_ref, ar, ai, reverse):
    shape = (N_CHUNKS, xr_ref.shape[1])
    ar, ai = jnp.broadcast_to(ar, shape), jnp.broadcast_to(ai, shape)

    def rows(tau):
        t = (CHUNK - 1 - tau) if reverse else tau
        return pl.ds(pl.multiple_of(t * N_CHUNKS, N_CHUNKS), N_CHUNKS)

    def step(tau, carry):
        sr, si = carry
        return ar * sr - ai * si + xr_ref[rows(tau), :], ar * si + ai * sr + xi_ref[rows(tau), :]

    zero = jnp.zeros(shape, F32)
    er, ei = lax.fori_loop(0, CHUNK, step, (zero, zero), unroll=8)
    qr, qi = ar, ai
    for _ in range(8):
        qr, qi = qr * qr - qi * qi, 2.0 * qr * qi
    shift = _shift_up if reverse else _shift_down
    for k in (1, 2, 4):
        sr, si = shift(er, k), shift(ei, k)
        er, ei = er + qr * sr - qi * si, ei + qr * si + qi * sr
        qr, qi = qr * qr - qi * qi, 2.0 * qr * qi
    start = (shift(er, 1), shift(ei, 1))

    def write(tau, carry):
        sr, si = step(tau, carry)
        xr_ref[rows(tau), :] = sr
        xi_ref[rows(tau), :] = si
        return sr, si

    return write, start


def _ssm_specs(layer):
    col = lambda w: pl.BlockSpec((SEQ, w), lambda g: (0, g))
    diag = pl.BlockSpec((None, None, 128, LANES_G), lambda g: (layer, g, 0, 0))
    vec = pl.BlockSpec((None, 1, LANES_G), lambda g: (layer, 0, g))
    return col, diag, vec


def _to_scan_order(src_ref, dst_ref):
    for tau in range(CHUNK):
        dst_ref[pl.ds(tau * N_CHUNKS, N_CHUNKS), :] = src_ref[pl.ds(tau, N_CHUNKS, stride=CHUNK), :]


def _to_time_order(src_ref, dst_ref, dtype):
    for j in range(N_CHUNKS):
        dst_ref[pl.ds(j * CHUNK, CHUNK), :] = src_ref[pl.ds(j, CHUNK, stride=N_CHUNKS), :].astype(dtype)


def _ssm_fwd(u, mats, layer, d):
    def body(u_ref, d_ref, br_ref, bi_ref, cr_ref, ci_ref, ar_ref, ai_ref, xr_ref, xi_ref, y_ref, us_ref):
        _to_scan_order(u_ref, us_ref)
        uv = us_ref[...].astype(BF16)
        xr_ref[...] = jnp.dot(uv, br_ref[...], preferred_element_type=F32)
        xi_ref[...] = jnp.dot(uv, bi_ref[...], preferred_element_type=F32)
        write, start = _scan_in_place(xr_ref, xi_ref, ar_ref[...], ai_ref[...], False)
        lax.fori_loop(0, CHUNK, write, start, unroll=8)
        y = lax.dot_general(xr_ref[...].astype(BF16), cr_ref[...], NT, preferred_element_type=F32)
        y += lax.dot_general(xi_ref[...].astype(BF16), ci_ref[...], NT, preferred_element_type=F32)
        us_ref[...] = y + d_ref[...] * us_ref[...]
        _to_time_order(us_ref, y_ref, F32)

    col, diag, vec = _ssm_specs(layer)
    return pl.pallas_call(
        body, grid=(N_LANE_GROUPS,),
        in_specs=[col(128), pl.BlockSpec((None, 1, 128), lambda g: (layer, 0, g)),
                  diag, diag, diag, diag, vec, vec],
        out_specs=[col(LANES_G), col(LANES_G), col(128)],
        out_shape=[jax.ShapeDtypeStruct((SEQ, SSM_GROUPS * SSM_STATE), F32)] * 2
        + [jax.ShapeDtypeStruct((SEQ, WIDTH), F32)],
        scratch_shapes=[pltpu.VMEM((SEQ, 128), F32)], compiler_params=_cp(), name="ssm_fwd",
    )(u, d, mats["b_re"], mats["b_im"], mats["c_re"], mats["c_im_neg"], mats["a_re"], mats["a_im"])


def _ssm_bwd(dy, x_re, x_im, u, mats, layer, d):
    def body(dyt_ref, ut_ref, d_ref, xr_ref, xi_ref, br_ref, bi_ref, cr_ref, ci_ref, ar_ref, ai_ref,
             du_ref, dar_ref, dai_ref, dbr_ref, dbi_ref, dcr_ref, dci_ref, lr_ref, li_ref, dys_ref, u_ref):
        _to_scan_order(dyt_ref, dys_ref)
        _to_scan_order(ut_ref, u_ref)
        dy = dys_ref[...].astype(BF16)
        lr_ref[...] = jnp.dot(dy, cr_ref[...], preferred_element_type=F32)
        li_ref[...] = jnp.dot(dy, ci_ref[...], preferred_element_type=F32)
        write, start = _scan_in_place(lr_ref, li_ref, ar_ref[...], -ai_ref[...], True)

        def rows(t):
            return pl.ds(pl.multiple_of(t * N_CHUNKS, N_CHUNKS), N_CHUNKS)

        def grad(acc, lam, xpr, xpi):
            return acc[0] + xpr * lam[0] + xpi * lam[1], acc[1] + xpr * lam[1] - xpi * lam[0]

        def down(tau, carry):
            lam = write(tau, carry[0])
            t = CHUNK - 2 - tau
            return lam, grad(carry[1], lam, xr_ref[rows(t), :], xi_ref[rows(t), :])

        zero = jnp.zeros((N_CHUNKS, LANES_G), F32)
        lam, acc = lax.fori_loop(0, CHUNK - 1, down, (start, (zero, zero)), unroll=5)
        lam = write(CHUNK - 1, lam)
        last = rows(CHUNK - 1)
        acc = grad(acc, lam, _shift_down(xr_ref[last, :], 1), _shift_down(xi_ref[last, :], 1))
        dar_ref[...] = jnp.sum(acc[0], axis=0, keepdims=True)
        dai_ref[...] = jnp.sum(acc[1], axis=0, keepdims=True)

        l_re, l_im = lr_ref[...].astype(BF16), li_ref[...].astype(BF16)
        du = lax.dot_general(l_re, br_ref[...], NT, preferred_element_type=F32)
        du += lax.dot_general(l_im, bi_ref[...], NT, preferred_element_type=F32)
        dys_ref[...] = du + dys_ref[...] * d_ref[...]
        _to_time_order(dys_ref, du_ref, BF16)
        uv = u_ref[...].astype(BF16)
        dbr_ref[...] = lax.dot_general(uv, l_re, TN, preferred_element_type=F32)
        dbi_ref[...] = lax.dot_general(uv, l_im, TN, preferred_element_type=F32)
        dcr_ref[...] = lax.dot_general(dy, xr_ref[...].astype(BF16), TN, preferred_element_type=F32)
        dci_ref[...] = lax.dot_general(dy, xi_ref[...].astype(BF16), TN, preferred_element_type=F32)

    col, diag, vec = _ssm_specs(layer)
    out_vec = pl.BlockSpec((1, LANES_G), lambda g: (0, g))
    out_blk = pl.BlockSpec((None, 128, LANES_G), lambda g: (g, 0, 0))
    sds = jax.ShapeDtypeStruct
    return pl.pallas_call(
        body, grid=(N_LANE_GROUPS,),
        in_specs=[col(128), col(128), pl.BlockSpec((None, 1, 128), lambda g: (layer, 0, g)),
                  col(LANES_G), col(LANES_G), diag, diag, diag, diag, vec, vec],
        out_specs=[col(128), out_vec, out_vec, out_blk, out_blk, out_blk, out_blk],
        out_shape=[sds((SEQ, WIDTH), BF16)] + [sds((1, SSM_GROUPS * SSM_STATE), F32)] * 2
        + [sds((N_LANE_GROUPS, 128, LANES_G), F32)] * 4,
        scratch_shapes=[pltpu.VMEM((SEQ, LANES_G), F32)] * 2 + [pltpu.VMEM((SEQ, 128), F32)] * 2,
        compiler_params=_cp(), name="ssm_bwd",
    )(dy, u, d, x_re, x_im, mats["b_re"], mats["b_im"], mats["c_re"], mats["c_im_neg"],
      mats["a_re"], mats["a_im"])


_GELU_C = math.sqrt(2.0 / math.pi)


def _gelu(y):
    return 0.5 * y * (1.0 + jnp.tanh(_GELU_C * (y + 0.044715 * (y * y * y))))


def _glu_fwd(y, wglu):
    tt = 512

    def body(y_ref, w_ref, z_ref):
        ys = _gelu(y_ref[...])
        a = jnp.dot(ys.astype(BF16), w_ref[...], preferred_element_type=F32)
        z_ref[...] = (ys * jax.nn.sigmoid(a)).astype(BF16)

    blk = pl.BlockSpec((tt, WIDTH), lambda i: (i, 0))
    return pl.pallas_call(body, grid=(SEQ // tt,), in_specs=[blk, _full((WIDTH, WIDTH))], out_specs=blk,
                          out_shape=jax.ShapeDtypeStruct((SEQ, WIDTH), BF16), compiler_params=_cp(),
                          name="glu_fwd")(y, wglu)


def _glu_bwd(y, wglu, dz, u):
    tt = 512

    def body(y_ref, w_ref, dz_ref, u_ref, dy_ref, ys_ref, da_ref, dd_ref):
        @pl.when(pl.program_id(0) == 0)
        def _():
            dd_ref[...] = jnp.zeros_like(dd_ref)

        yv = y_ref[...]
        t = jnp.tanh(_GELU_C * (yv + 0.044715 * (yv * yv * yv)))
        ys = 0.5 * yv * (1.0 + t)
        ysb = ys.astype(BF16)
        sg = jax.nn.sigmoid(jnp.dot(ysb, w_ref[...], preferred_element_type=F32))
        dz = dz_ref[...].astype(F32)
        da = (dz * ys * sg * (1.0 - sg)).astype(BF16)
        dys = dz * sg + lax.dot_general(da, w_ref[...], NT, preferred_element_type=F32)
        dy = dys * (0.5 * (1.0 + t) + 0.5 * yv * (1.0 - t * t) * _GELU_C * (1.0 + 3 * 0.044715 * (yv * yv)))
        dy_ref[...] = dy
        ys_ref[...] = ysb
        da_ref[...] = da
        dd_ref[...] += jnp.sum(dy * u_ref[...], axis=0, keepdims=True)

    blk = pl.BlockSpec((tt, WIDTH), lambda i: (i, 0))
    return pl.pallas_call(
        body, grid=(SEQ // tt,), in_specs=[blk, _full((WIDTH, WIDTH)), blk, blk],
        out_specs=[blk, blk, blk, _full((1, WIDTH))],
        out_shape=[jax.ShapeDtypeStruct((SEQ, WIDTH), F32)] + [jax.ShapeDtypeStruct((SEQ, WIDTH), BF16)] * 2
        + [jax.ShapeDtypeStruct((1, WIDTH), F32)],
        compiler_params=_cp(), name="glu_bwd")(y, wglu, dz, u)


def _mix_specs(tt, layer):
    row = lambda w: pl.BlockSpec((tt, w), lambda i: (i, 0))
    gate = lambda j: pl.BlockSpec((tt, D_MODEL), lambda i: (i, j))
    wo = lambda j: pl.BlockSpec((D_MODEL, WIDTH), lambda i: (0, j))
    return [row(D_MODEL), row(WIDTH), row(WIDTH), row(WIDTH), gate(0), gate(1), gate(2),
            pl.BlockSpec((None, 1, GATE_W), lambda i: (layer, 0, 0)), wo(0), wo(1), wo(2),
            _full((D_MODEL, D_MODEL))]


def _mix_branches(o_ref, c_ref, z_ref, g_refs, b_ref, wa_ref, wc_ref, ws_ref):
    ys = [lax.dot_general(r[...], w[...], NT, preferred_element_type=F32)
          for r, w in ((o_ref, wa_ref), (c_ref, wc_ref), (z_ref, ws_ref))]
    gates = [jax.nn.sigmoid(g_refs[j][...] + b_ref[:, D_MODEL * j:D_MODEL * (j + 1)]) for j in range(3)]
    return ys, gates


def _mix_fwd(x, o, cv, z, glog, b_gate, layer, wbt, wmix, tie):
    tt = 256

    def body(x_ref, o_ref, c_ref, z_ref, g0, g1, g2, b_ref, wa_ref, wc_ref, ws_ref, wm_ref, tie_ref, x1_ref):
        ys, gates = _mix_branches(o_ref, c_ref, z_ref, (g0, g1, g2), b_ref, wa_ref, wc_ref, ws_ref)
        merged = gates[0] * ys[0] + gates[1] * ys[1] + gates[2] * ys[2]
        x1_ref[...] = x_ref[...] + jnp.dot(merged.astype(BF16), wm_ref[...], preferred_element_type=F32)

    return pl.pallas_call(
        body, grid=(SEQ // tt,), in_specs=_mix_specs(tt, layer) + [ANY],
        out_specs=pl.BlockSpec((tt, D_MODEL), lambda i: (i, 0)),
        out_shape=jax.ShapeDtypeStruct((SEQ, D_MODEL), F32), compiler_params=_cp(), name="mix_fwd",
    )(x, o, cv, z, glog, glog, glog, b_gate, wbt, wbt, wbt, wmix, tie)


def _mix_bwd(dx1, o, cv, z, glog, b_gate, layer, wbt, wmix, tie):
    tt = 256

    def body(dx_ref, o_ref, c_ref, z_ref, g0, g1, g2, b_ref, wa_ref, wc_ref, ws_ref, wm_ref, tie_ref,
             mg_ref, dya_ref, dyc_ref, dys_ref, do_ref, dc_ref, dz_ref, dgl_ref, db_ref):
        @pl.when(pl.program_id(0) == 0)
        def _():
            db_ref[...] = jnp.zeros_like(db_ref)

        ys, gates = _mix_branches(o_ref, c_ref, z_ref, (g0, g1, g2), b_ref, wa_ref, wc_ref, ws_ref)
        mg_ref[...] = (gates[0] * ys[0] + gates[1] * ys[1] + gates[2] * ys[2]).astype(BF16)
        dm = lax.dot_general(dx_ref[...].astype(BF16), wm_ref[...], NT, preferred_element_type=F32)
        for j, (dy_ref, w_ref, d_ref) in enumerate(((dya_ref, wa_ref, do_ref), (dyc_ref, wc_ref, dc_ref),
                                                    (dys_ref, ws_ref, dz_ref))):
            dy = (dm * gates[j]).astype(BF16)
            dy_ref[...] = dy
            d_ref[...] = jnp.dot(dy, w_ref[...], preferred_element_type=F32)
            dgl = dm * ys[j] * gates[j] * (1.0 - gates[j])
            dgl_ref[:, D_MODEL * j:D_MODEL * (j + 1)] = dgl.astype(BF16)
            db_ref[:, D_MODEL * j:D_MODEL * (j + 1)] += jnp.sum(dgl, axis=0, keepdims=True)

    row = lambda w: pl.BlockSpec((tt, w), lambda i: (i, 0))
    sds = jax.ShapeDtypeStruct
    return pl.pallas_call(
        body, grid=(SEQ // tt,), in_specs=_mix_specs(tt, layer) + [ANY],
        out_specs=[row(D_MODEL)] * 4 + [row(WIDTH)] * 3 + [row(GATE_W), _full((1, GATE_W))],
        out_shape=[sds((SEQ, D_MODEL), BF16)] * 4 + [sds((SEQ, WIDTH), F32)] * 3
        + [sds((SEQ, GATE_W), BF16), sds((1, GATE_W), F32)],
        compiler_params=_cp(), name="mix_bwd",
    )(dx1, o, cv, z, glog, glog, glog, b_gate, wbt, wbt, wbt, wmix, tie)


def _ffn_out_fwd(x1, act, wout, tie):
    tt = 512

    def body(x_ref, a_ref, w_ref, tie_ref, o_ref):
        o_ref[...] = x_ref[...] + jnp.dot(a_ref[...], w_ref[...], preferred_element_type=F32)

    row = lambda w: pl.BlockSpec((tt, w), lambda i: (i, 0))
    return pl.pallas_call(
        body, grid=(SEQ // tt,), in_specs=[row(D_MODEL), row(FFN_H), _full((FFN_H, D_MODEL)), ANY],
        out_specs=row(D_MODEL), out_shape=jax.ShapeDtypeStruct((SEQ, D_MODEL), F32),
        compiler_params=_cp(), name="ffn_out_fwd")(x1, act, wout, tie)


def _ffn_out_bwd(dx2, up, silu, dsilu, wout, tie):
    tt = 512

    def body(dx_ref, up_ref, silu_ref, dsilu_ref, w_ref, tie_ref, dgu_ref):
        dact = lax.dot_general(dx_ref[...].astype(BF16), w_ref[...], NT, preferred_element_type=F32).astype(BF16)
        dgu_ref[:, :FFN_H] = dact * up_ref[...] * dsilu_ref[...]
        dgu_ref[:, FFN_H:] = dact * silu_ref[...]

    row = lambda w: pl.BlockSpec((tt, w), lambda i: (i, 0))
    return pl.pallas_call(
        body, grid=(SEQ // tt,),
        in_specs=[row(D_MODEL), row(FFN_H), row(FFN_H), row(FFN_H), _resident((FFN_H, D_MODEL)), ANY],
        out_specs=row(2 * FFN_H), out_shape=jax.ShapeDtypeStruct((SEQ, 2 * FFN_H), BF16),
        compiler_params=_cp(), name="ffn_out_bwd")(dx2, up, silu, dsilu, wout, tie)


def _loss_head(x, g, target):
    tt = 256

    def body(x_ref, g_ref, t_ref, loss_ref, dx_ref, dg_ref):
        @pl.when(pl.program_id(0) == 0)
        def _():
            loss_ref[...] = jnp.zeros_like(loss_ref)
            dg_ref[...] = jnp.zeros_like(dg_ref)

        xv = x_ref[...]
        r = lax.rsqrt(jnp.mean(xv * xv, axis=-1, keepdims=True) + NORM_EPS)
        xh = xv * r
        err = xh * g_ref[...] - t_ref[...]
        loss_ref[...] += 0.5 * jnp.sum(jnp.mean(err * err, axis=-1, keepdims=True))
        dy = err * (1.0 / D_MODEL)
        gy = dy * g_ref[...]
        dx_ref[...] = r * (gy - xh * jnp.mean(gy * xh, axis=-1, keepdims=True))
        dg_ref[...] += jnp.sum(dy * xh, axis=0, keepdims=True)

    row = pl.BlockSpec((tt, D_MODEL), lambda i: (i, 0))
    return pl.pallas_call(
        body, grid=(SEQ // tt,), in_specs=[row, _full((1, D_MODEL)), row],
        out_specs=[_full((1, 128)), row, _full((1, D_MODEL))],
        out_shape=[jax.ShapeDtypeStruct((1, 128), F32), jax.ShapeDtypeStruct((SEQ, D_MODEL), F32),
                   jax.ShapeDtypeStruct((1, D_MODEL), F32)],
        compiler_params=_cp(), name="loss_head")(x, g, target)


def _adam_math(g, w, m, v):
    nm = B1 * m + (1.0 - B1) * g
    nv = B2 * v + (1.0 - B2) * (g * g)
    m_hat = nm / (1.0 - B1 ** STEP)
    v_hat = nv / (1.0 - B2 ** STEP)
    return -LR * (m_hat / (jnp.sqrt(v_hat) + ADAM_EPS) + WD * w), nm, nv


def _adamw_small(parts, w, m, v, name):
    def body(p_ref, w_ref, m_ref, v_ref, g_ref, d_ref, nm_ref, nv_ref):
        g = p_ref[0].astype(F32)
        for k in range(1, N_DEV):
            g = g + p_ref[k].astype(F32)
        g_ref[...] = g
        d_ref[...], nm_ref[...], nv_ref[...] = _adam_math(g, w_ref[...], m_ref[...], v_ref[...])

    out_shape = [jax.ShapeDtypeStruct(w.shape, F32)] * 4
    if w.ndim < 3:
        return pl.pallas_call(body, out_shape=out_shape, name=name)(parts, w, m, v)
    rest = w.shape[1:]
    zeros = (0,) * len(rest)
    blk = pl.BlockSpec((None,) + rest, lambda l: (l,) + zeros)
    return pl.pallas_call(
        body, grid=(w.shape[0],),
        in_specs=[pl.BlockSpec((N_DEV, None) + rest, lambda l: (0, l) + zeros), blk, blk, blk],
        out_specs=[blk] * 4, out_shape=out_shape, name=name)(parts, w, m, v)


def _adamw(parts, w, m, v, tr, name, groups=None, fill=None, tie=None):
    n_groups, rows, cols = w.shape
    n_parts = parts.shape[1]
    lo, hi = groups if groups is not None else (0, n_groups)

    def body(p_ref, w_ref, m_ref, v_ref, *rest):
        g_ref, d_ref, nm_ref, nv_ref = rest[-4:]
        g = p_ref[0].astype(F32)
        for k in range(1, n_parts):
            g = g + p_ref[k].astype(F32)
        nm = B1 * m_ref[...] + (1.0 - B1) * g
        nv = B2 * v_ref[...] + (1.0 - B2) * (g * g)
        m_hat = nm / (1.0 - B1 ** STEP)
        v_hat = nv / (1.0 - B2 ** STEP)
        g_ref[...] = g
        d_ref[...] = -LR * (m_hat / (jnp.sqrt(v_hat) + ADAM_EPS) + WD * w_ref[...])
        nm_ref[...] = nm
        nv_ref[...] = nv

    blk = pl.BlockSpec((None, tr, cols), lambda l, i: (l + lo, i, 0))
    p_lo = lo if parts.shape[0] == n_groups else 0
    extra = ([] if fill is None else list(fill)) + ([] if tie is None else [tie])
    return pl.pallas_call(
        body, grid=(hi - lo, rows // tr),
        in_specs=[pl.BlockSpec((None, n_parts, tr, cols), lambda l, i: (l + p_lo, 0, i, 0)), blk, blk, blk]
        + [ANY] * len(extra),
        out_specs=[blk] * 4, out_shape=[jax.ShapeDtypeStruct((n_groups, rows, cols), F32)] * 4,
        input_output_aliases={} if fill is None else {4 + j: j for j in range(4)},
        compiler_params=_cp(), name=name)(parts, w, m, v, *extra)


def _split_start(name, arrays, n_sems, plan, after=None):
    n = len(arrays)
    order = [] if after is None else [after]
    n_in = n + len(order)

    def body(*refs):
        ins, send_sems, recv_sems, token = refs[:n], refs[n_in], refs[n_in + 1], refs[-1]
        for src, dst, k, to in plan(ins)[0]:
            pltpu.make_async_remote_copy(src_ref=src, dst_ref=dst, send_sem=send_sems.at[k], recv_sem=recv_sems.at[k],
                                         device_id=to, device_id_type=MESH_ID).start()
        token[...] = jnp.zeros_like(token)

    outs = pl.pallas_call(
        body, name=name,
        out_shape=(pltpu.SemaphoreType.DMA((n_sems,)), pltpu.SemaphoreType.DMA((n_sems,)),
                   *[pltpu.HBM(a.shape, a.dtype) for a in arrays], jax.ShapeDtypeStruct((8, 128), F32)),
        in_specs=[HBM] * n + [ANY] * len(order),
        out_specs=(SEM, SEM, *[HBM] * n, pl.BlockSpec(memory_space=pltpu.VMEM)),
        input_output_aliases={i: 2 + i for i in range(n)},
        compiler_params=pltpu.CompilerParams(has_side_effects=EFFECT),
    )(*[pltpu.with_memory_space_constraint(a, pltpu.HBM) for a in arrays], *order)
    return outs[0], outs[1], list(outs[2:2 + n]), outs[-1]


def _split_wait(name, arrays, send_sems, recv_sems, after, plan):
    n = len(arrays)
    order = list(after) if isinstance(after, (list, tuple)) else [after]

    def body(*refs):
        ins, s_sems, r_sems = refs[:n], refs[n], refs[n + 1]
        sends, arrivals = plan(ins)
        x, y, c = lax.axis_index("x"), lax.axis_index("y"), lax.axis_index("c")
        for src, dst, k, to in sends:
            pltpu.make_async_remote_copy(src_ref=src, dst_ref=dst, send_sem=s_sems.at[k], recv_sem=r_sems.at[k],
                                         device_id=to, device_id_type=MESH_ID).wait_send()
        for dst, k in arrivals:
            pltpu.make_async_remote_copy(src_ref=dst, dst_ref=dst, send_sem=s_sems.at[k], recv_sem=r_sems.at[k],
                                         device_id=(x, y, c), device_id_type=MESH_ID).wait_recv()

    return pl.pallas_call(
        body, name=name, out_shape=[pltpu.HBM(a.shape, a.dtype) for a in arrays],
        in_specs=[HBM] * n + [SEM, SEM] + [ANY] * len(order), out_specs=[HBM] * n,
        input_output_aliases={i: i for i in range(n)},
        compiler_params=pltpu.CompilerParams(has_side_effects=EFFECT),
    )(*arrays, send_sems, recv_sems, *order)


def _chips():
    x, y, c = lax.axis_index("x"), lax.axis_index("y"), lax.axis_index("c")
    return x, y, c, [(1 - x, y), (x, 1 - y), (1 - x, 1 - y)]


def _plan_gather_chips(refs):
    x, y, c, chips = _chips()
    me = 4 * x + 2 * y + c
    n = len(refs) // 2
    sends, arrivals = [], []
    for i in range(n):
        src, land = refs[i], refs[n + i]
        sends.append((src, land.at[me], 4 * i, (x, y, 1 - c)))
        arrivals.append((land.at[4 * x + 2 * y + 1 - c], 4 * i))
        for j, (px, py) in enumerate(chips):
            sends.append((src, land.at[me], 4 * i + 1 + j, (px, py, c)))
            arrivals.append((land.at[4 * px + 2 * py + c], 4 * i + 1 + j))
    return sends, arrivals


def _plan_gather_pass(refs):
    x, y, c, chips = _chips()
    sends, arrivals = [], []
    for i in range(len(refs)):
        for j, (px, py) in enumerate(chips):
            slot = refs[i].at[4 * px + 2 * py + c]
            sends.append((slot, slot, 4 * i + j, (x, y, 1 - c)))
            arrivals.append((refs[i].at[4 * px + 2 * py + 1 - c], 4 * i + j))
        back = refs[i].at[4 * x + 2 * y + 1 - c]
        sends.append((back, back, 4 * i + 3, (x, y, 1 - c)))
        arrivals.append((refs[i].at[4 * x + 2 * y + c], 4 * i + 3))
    return sends, arrivals


def _plan_scatter_pair(refs):
    x, y, c = lax.axis_index("x"), lax.axis_index("y"), lax.axis_index("c")
    n = len(refs) // 2
    sends, arrivals = [], []
    for i in range(n):
        for q in range(4):
            sends.append((refs[i].at[q, 1 - c], refs[n + i].at[q], 4 * i + q, (x, y, 1 - c)))
            arrivals.append((refs[n + i].at[q], 4 * i + q))
    return sends, arrivals


def _plan_scatter_chips(layer):
    def plan(refs):
        x, y, c, chips = _chips()
        n = len(refs) // 2
        sends, arrivals = [], []
        for i in range(n):
            for j, (px, py) in enumerate(chips):
                sends.append((refs[i].at[2 * px + py], refs[n + i].at[layer, 2 * x + y], 3 * i + j, (px, py, c)))
                arrivals.append((refs[n + i].at[layer, 2 * px + py], 3 * i + j))
        return sends, arrivals

    return plan


def _pair_sum(parts4, from_pair, landing, layer, core, tr, name):
    _, _, rows, cols = parts4.shape

    def body(c_ref, p_ref, s_ref, l_ref, sum_ref, land_ref):
        v = (p_ref[...].astype(F32) + s_ref[...].astype(F32)).astype(BF16)
        sum_ref[...] = v
        land_ref[...] = v

    blk = pl.BlockSpec((None, tr, cols), lambda q, i, c_ref: (q, i, 0))
    return pl.pallas_call(
        body,
        grid_spec=pltpu.PrefetchScalarGridSpec(
            num_scalar_prefetch=1, grid=(4, rows // tr),
            in_specs=[pl.BlockSpec((None, None, tr, cols), lambda q, i, c_ref: (q, c_ref[0], i, 0)), blk, ANY],
            out_specs=[blk, pl.BlockSpec((None, None, tr, cols), lambda q, i, c_ref: (layer, q, i, 0))]),
        out_shape=[jax.ShapeDtypeStruct((4, rows, cols), BF16), jax.ShapeDtypeStruct(landing.shape, BF16)],
        input_output_aliases={3: 1}, compiler_params=_cp(), name=name,
    )(core, parts4, from_pair, landing)


def _travel_layout(t):
    tr = lambda a: jnp.swapaxes(a, 1, 2)
    branch = jnp.concatenate([tr(t["w_attn_o"]), tr(t["w_conv_o"]), tr(t["w_ssm_o"])], axis=2)
    return [tr(t["w_in"]), tr(t["w_ffn_in"]), t["w_ffn_out"], t["w_mix_o"], branch, t["w_ssm_glu"]]


def _native_layout(a):
    tr = lambda x: jnp.swapaxes(x, 1, 2)
    b = a[4]
    return {"w_in": tr(a[0]), "w_ffn_in": tr(a[1]), "w_ffn_out": a[2], "w_mix_o": a[3],
            "w_attn_o": tr(b[:, :, :WIDTH]), "w_conv_o": tr(b[:, :, WIDTH:2 * WIDTH]),
            "w_ssm_o": tr(b[:, :, 2 * WIDTH:]), "w_ssm_glu": a[5]}


def _embed(t):
    eye = jnp.eye(8, dtype=t.dtype)
    t = t.reshape(DEPTH, N_LANE_GROUPS, 8, SSM_GROUP, SSM_STATE)
    return (t[:, :, :, :, None, :] * eye[None, None, :, None, :, None]).reshape(DEPTH, N_LANE_GROUPS, 128, LANES_G)


def _diag_blocks(t):
    t = t.reshape(DEPTH, N_LANE_GROUPS, 8, SSM_GROUP, 8, SSM_STATE)
    return jnp.einsum("lgahap->lgahp", t).reshape(DEPTH, SSM_GROUPS, SSM_GROUP, SSM_STATE)


def _rope_tabs():
    pos = jnp.arange(SEQ, dtype=F32)
    inv_freq = ROPE_THETA ** (-jnp.arange(0, ROT_DIM, 2, dtype=F32) / ROT_DIM)
    ang = pos[:, None] * inv_freq[None, :]
    cos, sin = jnp.cos(ang), jnp.sin(ang)
    one, zero = jnp.ones((SEQ, HEAD_DIM - ROT_DIM), F32), jnp.zeros((SEQ, HEAD_DIM - ROT_DIM), F32)
    z8 = jnp.zeros((SEQ, 8), F32)
    head = lambda *p: jnp.tile(jnp.concatenate(p, axis=1), (1, 2))
    return head(cos, cos, one), head(-sin, z8, zero), head(z8, sin, zero)


def _ssm_mats(sp):
    lr, li, bbr, bbi = _ssm_prep(sp["a_re"], sp["a_im"], sp["log_dt"], sp["bt_re"], sp["bt_im"])
    lanes = SSM_GROUPS * SSM_STATE
    return {
        "a_re": lr.reshape(DEPTH, 1, lanes), "a_im": li.reshape(DEPTH, 1, lanes),
        "b_re": _embed(bbr).astype(BF16), "b_im": _embed(bbi).astype(BF16),
        "c_re": _embed(sp["c_re"]).astype(BF16), "c_im_neg": _embed(-sp["c_im"]).astype(BF16),
    }


def _layer_fwd(x, i, w, rp, mats, tabs, tie, hooks):
    q, kv, cbx, u, glog, cv, h = _rms_mm_in(x, rp["norm_mix"][i], w["win_t"], tabs, rp["conv_w"], i, tie)
    o = _attn_fwd(q, kv, tabs, rp["attn_sinks"][i])
    x_re, x_im, y = _ssm_fwd(u, mats, i, rp["ssm_d"])
    z = _glu_fwd(y, w["wglu"])
    x1 = _mix_fwd(x, o, cv, z, glog, rp["b_gate"], i, w["branch_t"], w["wmix"], hooks["early"](z))
    hooks["pre_ffn"](x1)
    act, up, silu, dsilu, h2 = _rms_mm_ffn(x1, rp["norm_ffn"][i], w["wffn_t"])
    x2 = _ffn_out_fwd(x1, act, w["wout"], hooks["mid"](h2))
    kept = dict(x=x, q=q, kv=kv, cbx=cbx, u=u, glog=glog, h=h, o=o, cv=cv, z=z, y=y,
                x_re=x_re, x_im=x_im, x1=x1, act=act, up=up, silu=silu, dsilu=dsilu, h2=h2)
    return x2, kept


def _layer_bwd(dx2, k, i, w, rp, mats, tabs, tie, hooks):
    dgu = _ffn_out_bwd(dx2, k["up"], k["silu"], k["dsilu"], w["wout"], tie)
    g_wout = _mm_tn(k["act"], dx2, tm=FFN_H // 2, tn=1024, name="mm_tn_ffn_out")
    g_wffn_t = _mm_tn(dgu, k["h2"], tm=FFN_H // 2, tn=1024, name="mm_tn_ffn_in")
    dx1, d_norm_ffn = _mm_rmsbwd([dgu], w["wffn_t"], k["x1"], rp["norm_ffn"][i], dx2, "mm_rmsbwd_ffn")

    mg, dya, dyc, dys, do, dcv, dz, dgl, db_gate = _mix_bwd(
        dx1, k["o"], k["cv"], k["z"], k["glog"], rp["b_gate"], i, w["branch_t"], w["wmix"],
        hooks["mid"]((g_wffn_t, g_wout, d_norm_ffn)))
    g_wmix = _mm_tn(mg, dx1, tm=1024, tn=512, name="mm_tn_mix")
    g_branch_t = _tn_branches((dya, dyc, dys), (k["o"], k["cv"], k["z"]))

    dy, ys16, da16, dd = _glu_bwd(k["y"], w["wglu"], dz, k["u"])
    g_wglu = _mm_tn(ys16, da16, tm=256, tn=512, name="mm_tn_glu")
    du, da_re, da_im, db_re, db_im, dc_re, dc_im = _ssm_bwd(dy, k["x_re"], k["x_im"], k["u"], mats, i, rp["ssm_d"])

    dcb, dcc, dcx, d_conv_w = _conv_bwd(k["cbx"], rp["conv_w"], i, dcv, hooks["late"](du))
    dq, dkv, d_sinks = _attn_bwd(k["q"], k["kv"], tabs, rp["attn_sinks"][i], do)

    pieces = [dq, dkv, dcb, dcc, dcx, du, dgl]
    g_win_t = _tn_pieces(pieces, k["h"])
    dx, d_norm_mix = _mm_rmsbwd(pieces, w["win_t"], k["x"], rp["norm_mix"][i], dx1, "mm_rmsbwd_in")

    grads = [g_win_t, g_wffn_t, g_wout, g_wmix, g_branch_t, g_wglu]
    small = dict(norm_mix=d_norm_mix, b_gate=db_gate, attn_sinks=d_sinks, ssm_d=dd, norm_ffn=d_norm_ffn,
                 conv_w=d_conv_w, da_re=da_re, da_im=da_im, db_re=db_re, db_im=db_im, dc_re=dc_re, dc_im=dc_im)
    return dx, grads, small


def _replicated_grads(sg, sp):
    stack = lambda name: jnp.stack([sg[i][name] for i in range(DEPTH)])
    cots = (stack("da_re").reshape(DEPTH, *_GS), stack("da_im").reshape(DEPTH, *_GS),
            _diag_blocks(stack("db_re")), _diag_blocks(stack("db_im")))
    d_a_re, d_a_im, d_log_dt, d_bt_re, d_bt_im = _ssm_prep_bwd(
        sp["a_re"], sp["a_im"], sp["log_dt"], sp["bt_re"], sp["bt_im"], cots)
    sgrads = {"norm_mix": stack("norm_mix"), "b_gate": stack("b_gate"),
              "attn_sinks": stack("attn_sinks")[:, :, :N_Q_HEADS], "ssm_a_re": d_a_re, "ssm_a_im": d_a_im,
              "ssm_b_re": jnp.swapaxes(d_bt_re, 2, 3), "ssm_b_im": jnp.swapaxes(d_bt_im, 2, 3),
              "ssm_c_re": _diag_blocks(stack("dc_re")), "ssm_c_im": -_diag_blocks(stack("dc_im")),
              "ssm_d": stack("ssm_d"), "ssm_log_dt": d_log_dt, "norm_ffn": stack("norm_ffn")}
    return sgrads, stack("conv_w")[:, :3]


def kernel(x, norm_mix, w_in, b_gate, attn_sinks, w_attn_o, conv_w, w_conv_o, ssm_a_re, ssm_a_im, ssm_b_re, ssm_b_im, ssm_c_re, ssm_c_im, ssm_d, ssm_log_dt, w_ssm_glu, w_ssm_o, w_mix_o, norm_ffn, w_ffn_in, w_ffn_out, norm_final, loss_target, m_norm_mix, m_w_in, m_b_gate, m_attn_sinks, m_w_attn_o, m_conv_w, m_w_conv_o, m_ssm_a_re, m_ssm_a_im, m_ssm_b_re, m_ssm_b_im, m_ssm_c_re, m_ssm_c_im, m_ssm_d, m_ssm_log_dt, m_w_ssm_glu, m_w_ssm_o, m_w_mix_o, m_norm_ffn, m_w_ffn_in, m_w_ffn_out, m_norm_final, v_norm_mix, v_w_in, v_b_gate, v_attn_sinks, v_w_attn_o, v_conv_w, v_w_conv_o, v_ssm_a_re, v_ssm_a_im, v_ssm_b_re, v_ssm_b_im, v_ssm_c_re, v_ssm_c_im, v_ssm_d, v_ssm_log_dt, v_w_ssm_glu, v_w_ssm_o, v_w_mix_o, v_norm_ffn, v_w_ffn_in, v_w_ffn_out, v_norm_final):
    big = {"w": dict(w_in=w_in, w_attn_o=w_attn_o, w_conv_o=w_conv_o, w_ssm_glu=w_ssm_glu, w_ssm_o=w_ssm_o,
                     w_mix_o=w_mix_o, w_ffn_in=w_ffn_in, w_ffn_out=w_ffn_out),
           "m": dict(w_in=m_w_in, w_attn_o=m_w_attn_o, w_conv_o=m_w_conv_o, w_ssm_glu=m_w_ssm_glu,
                     w_ssm_o=m_w_ssm_o, w_mix_o=m_w_mix_o, w_ffn_in=m_w_ffn_in, w_ffn_out=m_w_ffn_out),
           "v": dict(w_in=v_w_in, w_attn_o=v_w_attn_o, w_conv_o=v_w_conv_o, w_ssm_glu=v_w_ssm_glu,
                     w_ssm_o=v_w_ssm_o, w_mix_o=v_w_mix_o, w_ffn_in=v_w_ffn_in, w_ffn_out=v_w_ffn_out)}
    small = {"w": dict(norm_mix=norm_mix, b_gate=b_gate, attn_sinks=attn_sinks, ssm_a_re=ssm_a_re,
                       ssm_a_im=ssm_a_im, ssm_b_re=ssm_b_re, ssm_b_im=ssm_b_im, ssm_c_re=ssm_c_re,
                       ssm_c_im=ssm_c_im, ssm_d=ssm_d, ssm_log_dt=ssm_log_dt, norm_ffn=norm_ffn),
             "m": dict(norm_mix=m_norm_mix, b_gate=m_b_gate, attn_sinks=m_attn_sinks, ssm_a_re=m_ssm_a_re,
                       ssm_a_im=m_ssm_a_im, ssm_b_re=m_ssm_b_re, ssm_b_im=m_ssm_b_im, ssm_c_re=m_ssm_c_re,
                       ssm_c_im=m_ssm_c_im, ssm_d=m_ssm_d, ssm_log_dt=m_ssm_log_dt, norm_ffn=m_norm_ffn),
             "v": dict(norm_mix=v_norm_mix, b_gate=v_b_gate, attn_sinks=v_attn_sinks, ssm_a_re=v_ssm_a_re,
                       ssm_a_im=v_ssm_a_im, ssm_b_re=v_ssm_b_re, ssm_b_im=v_ssm_b_im, ssm_c_re=v_ssm_c_re,
                       ssm_c_im=v_ssm_c_im, ssm_d=v_ssm_d, ssm_log_dt=v_ssm_log_dt, norm_ffn=v_norm_ffn)}
    finals = {"w": norm_final, "m": m_norm_final, "v": v_norm_final}
    convs = {"w": conv_w, "m": m_conv_w, "v": v_conv_w}
    small_out_shapes = {name: a.shape for name, a in small["w"].items()}
    small_out_shapes.update(norm_final=(D_MODEL,), conv_w=(DEPTH, 3, 64))
    small_shapes = dict(small_out_shapes, norm_final=(1, D_MODEL), conv_w=(DEPTH, 3, WIDTH))
    dense = ("ssm_b_re", "ssm_b_im", "ssm_c_re", "ssm_c_im")
    for name in dense:
        small_shapes[name] = (DEPTH, SSM_GROUPS, SSM_GROUP * SSM_STATE)
    small_wmv = {name: [(convs[s] if name == "conv_w" else finals[s] if name == "norm_final" else small[s][name])
                        .reshape((DEPTH, 3, 64) if name == "conv_w" else small_shapes[name]) for s in "wmv"]
                 for name in small_shapes}
    mine = 4 * lax.axis_index("x") + 2 * lax.axis_index("y") + lax.axis_index("c")

    travel = {s: _travel_layout(big[s]) for s in "wmv"}
    stacked16 = list(zip(*[[a[0] for a in _travel_layout({n: w[i:i + 1].astype(BF16) for n, w in big["w"].items()})]
                           for i in range(DEPTH)]))
    rp = {"norm_mix": norm_mix[:, None], "norm_ffn": norm_ffn[:, None], "attn_sinks": attn_sinks[:, None],
          "b_gate": b_gate[:, None], "ssm_d": ssm_d[:, None]}
    sp = {"a_re": ssm_a_re, "a_im": ssm_a_im, "log_dt": ssm_log_dt[:, :, None],
          "bt_re": jnp.swapaxes(ssm_b_re, 2, 3), "bt_im": jnp.swapaxes(ssm_b_im, 2, 3),
          "c_re": ssm_c_re, "c_im": ssm_c_im}
    rows_tile = {"win_t": 368, "wffn_t": 352, "wout": 352, "wmix": 128, "branch_t": 128, "wglu": 64}
    core = lax.axis_index("c").astype(jnp.int32).reshape(1)
    no_tie = jnp.zeros((8, 128), F32)

    def landing_zones(srcs):
        return [lax.empty((N_DEV,) + s.shape, s.dtype) for s in srcs]

    def gather_chips(tag, i, kinds, after, extra=()):
        srcs = [stacked16[j][i] for j in kinds] + list(extra)
        s_sems, r_sems, arrays, token = _split_start(
            f"gather_chips_start_{tag}", srcs + landing_zones(srcs), 4 * len(srcs), _plan_gather_chips, after)
        return (tag, s_sems, r_sems, arrays), token

    def gather_pass(state, after):
        tag, s_sems, r_sems, arrays = state
        arrays = _split_wait(f"gather_chips_wait_{tag}", arrays, s_sems, r_sems, after, _plan_gather_chips)
        n = len(arrays) // 2
        s_sems, r_sems, lands, token = _split_start(
            f"gather_pass_start_{tag}", list(arrays[n:]), 4 * n, _plan_gather_pass)
        return (tag, s_sems, r_sems, lands), token

    def gather_done(state, after, kinds):
        tag, s_sems, r_sems, lands = state
        lands = _split_wait(f"gather_pass_wait_{tag}", lands, s_sems, r_sems, after, _plan_gather_pass)
        named = {KINDS[j][0]: a.reshape(N_DEV * KINDS[j][1], KINDS[j][2]) for a, j in zip(lands, kinds)}
        return named, list(lands[len(kinds):])

    all_kinds, mixer_kinds, ffn_kinds = tuple(range(len(KINDS))), (0, 3, 4, 5), (1, 2)
    no_hooks = {name: (lambda value: no_tie) for name in ("early", "pre_ffn", "mid", "late")}
    state, token = gather_chips("0m", 0, mixer_kinds, None, extra=[jnp.pad(conv_w.reshape(6, 128), ((0, 2), (0, 0)))])
    mats = _ssm_mats(dict(sp, log_dt=sp["log_dt"] + token[0, 0]))
    tabs = _rope_tabs()
    early_work = list(mats.values()) + list(tabs) + [a for name in dense for a in small_wmv[name]]
    early_work += [stacked16[j][0] for j in ffn_kinds] + [stacked16[j][1] for j in mixer_kinds]
    state, _ = gather_pass(state, early_work)
    ffn_state, tie = gather_chips("0f", 0, ffn_kinds, state[3][0])
    w_next, (conv_all,) = gather_done(state, tabs[2], mixer_kinds)
    conv_full = conv_all[:, :6].reshape(N_DEV, DEPTH, 3, 64).transpose(1, 2, 0, 3).reshape(DEPTH, 3, WIDTH)
    rp["conv_w"] = jnp.pad(conv_full, ((0, 0), (0, 5), (0, 0)))

    act = x[0]
    weights, kept = [], []
    for i in range(DEPTH):
        w_i, hooks, held = w_next, dict(no_hooks), {}

        def early(value, ffn_state=ffn_state, held=held):
            held["ffn"], token = gather_pass(ffn_state, value)
            return token

        def pre_ffn(value, w_i=w_i, held=held):
            w_i.update(gather_done(held["ffn"], value, ffn_kinds)[0])

        hooks.update(early=early, pre_ffn=pre_ffn)
        if i + 1 < DEPTH:
            state, tie = gather_chips(f"{i + 1}m", i + 1, mixer_kinds, tie if i == 0 else w_i["win_t"])

            def mid(value, i=i, state=state, held=held):
                held["next"], token = gather_pass(state, value)
                held["next_ffn"], token = gather_chips(f"{i + 1}f", i + 1, ffn_kinds, token)
                return token

            hooks.update(mid=mid)
        act, k = _layer_fwd(act, i, w_i, rp, mats, tabs, tie, hooks)
        if i + 1 < DEPTH:
            w_next, _ = gather_done(held["next"], act, mixer_kinds)
            ffn_state, tie = held["next_ffn"], no_tie
        weights.append(w_i)
        kept.append(k)
    loss_row, dx, d_norm_final = _loss_head(act, norm_final[None], loss_target[0])

    landings = [lax.empty((DEPTH, 4, r, c), BF16) for _, r, c in KINDS]
    landings0 = [lax.empty((1, 4, r, c), BF16) for _, r, c in KINDS]

    def scatter_pair(tag, kinds, grads, after):
        parts4 = [g.reshape(4, 2, KINDS[j][1], KINDS[j][2]) for g, j in zip(grads, kinds)]
        zones = [lax.empty((4, KINDS[j][1], KINDS[j][2]), BF16) for j in kinds]
        s_sems, r_sems, arrays, token = _split_start(
            f"scatter_pair_start_{tag}", parts4 + zones, 4 * len(kinds), _plan_scatter_pair, after)
        return (tag, kinds, s_sems, r_sems, arrays), token

    def scatter_chips(state, lands, slot, after):
        tag, kinds, s_sems, r_sems, arrays = state
        arrays = _split_wait(f"scatter_pair_wait_{tag}", arrays, s_sems, r_sems, after, _plan_scatter_pair)
        n = len(kinds)
        sums, mine_lands = [], []
        for k, j in enumerate(kinds):
            name = KINDS[j][0]
            chip_sum, land = _pair_sum(arrays[k], arrays[n + k], lands[j], slot, core, KINDS[j][1],
                                       f"pair_sum_{name}")
            sums.append(chip_sum)
            mine_lands.append(land)
        s_sems, r_sems, arrays, token = _split_start(
            f"scatter_chips_start_{tag}", sums + mine_lands, 3 * n, _plan_scatter_chips(slot))
        return (tag, kinds, slot, s_sems, r_sems, arrays), token

    def scatter_done(state, lands, after):
        tag, kinds, slot, s_sems, r_sems, arrays = state
        arrays = _split_wait(f"scatter_chips_wait_{tag}", arrays, s_sems, r_sems, after, _plan_scatter_chips(slot))
        lands = list(lands)
        for k, j in enumerate(kinds):
            lands[j] = arrays[len(kinds) + k]
        return lands

    sg = [None] * DEPTH
    pending, tie = None, no_tie
    for i in reversed(range(DEPTH)):
        hooks, held = dict(no_hooks), {}
        if pending is not None:
            def mid(value, i=i, pending=pending, held=held):
                held["chips"], token = scatter_chips(pending, landings, i + 1, value[2])
                if i == 0:
                    held["ffn_pair"], token = scatter_pair("0f", ffn_kinds, value[:2], token)
                return token

            hooks.update(mid=mid)
        if i == 0:
            def late(value, held=held):
                held["ffn_chips"], token = scatter_chips(held["ffn_pair"], landings0, 0, value)
                return token

            hooks.update(late=late)
        dx, grads, sg[i] = _layer_bwd(dx, kept[i], i, weights[i], rp, mats, tabs, tie, hooks)
        if pending is not None:
            landings = scatter_done(held["chips"], landings, dx)
        if i > 0:
            pending, tie = scatter_pair(str(i), all_kinds, grads, dx)
        else:
            pending, _ = scatter_pair("0m", mixer_kinds, [grads[j] for j in mixer_kinds], dx)

    sgrads, conv_grad = _replicated_grads(sg, sp)

    small_names = list(REPLICATED) + ["norm_final", "conv_w"]
    sgrads.update(norm_final=d_norm_final, conv_w=conv_grad)
    small_src = [sgrads[name].reshape(small_shapes[name]).astype(BF16) for name in small_names]
    small_src.append(jnp.broadcast_to(loss_row[:, :1], (8, 128)))
    last, tie = scatter_chips(pending, landings0, 0, small_src[0])
    s_sems, r_sems, arrays, tie = _split_start(
        "gather_small_chips_start", small_src + landing_zones(small_src), 4 * len(small_src), _plan_gather_chips, tie)
    small_state = ("small", s_sems, r_sems, arrays)

    big_out = []
    for j, (name, _, _) in enumerate(KINDS):
        big_out.append(_adamw(landings[j], travel["w"][j], travel["m"][j], travel["v"][j], rows_tile[name],
                              "adamw_late_" + name, groups=(1, DEPTH), tie=tie))
        tie = big_out[-1][3]
    landings0 = scatter_done(held["ffn_chips"], landings0, tie)
    landings0 = scatter_done(last, landings0, tie)
    small_state, _ = gather_pass(small_state, landings0[0])
    big_out = [_adamw(landings0[j], travel["w"][j], travel["m"][j], travel["v"][j], rows_tile[name],
                      "adamw_first_" + name, groups=(0, 1), fill=big_out[j]) for j, (name, _, _) in enumerate(KINDS)]
    big_res = [_native_layout([big_out[j][kind] for j in range(len(KINDS))]) for kind in range(4)]

    _, sparts = gather_done(small_state, big_out[-1][0], ())
    loss = jnp.sum(sparts[-1][:, 0, 0])
    sparts = dict(zip(small_names, sparts))
    sparts["conv_w"] = lax.dynamic_slice_in_dim(sparts["conv_w"], mine * 64, 64, axis=3)
    small_res = {}
    for name in small_names:
        res = _adamw_small(sparts[name], *small_wmv[name], "adamw_" + name)
        small_res[name] = [r.reshape(small_out_shapes[name]) for r in res]

    order = ["norm_mix", "w_in", "b_gate", "attn_sinks", "w_attn_o", "conv_w", "w_conv_o", "ssm_a_re", "ssm_a_im",
             "ssm_b_re", "ssm_b_im", "ssm_c_re", "ssm_c_im", "ssm_d", "ssm_log_dt", "w_ssm_glu", "w_ssm_o",
             "w_mix_o", "norm_ffn", "w_ffn_in", "w_ffn_out", "norm_final"]
    outs = [loss, dx[None]]
    for kind in range(4):
        for name in order:
            outs.append(big_res[kind][name] if name in big_res[kind] else small_res[name][kind])
    return tuple(outs)
```

```python
import math

import jax
import jax.numpy as jnp
from jax import lax
from jax.experimental import pallas as pl
from jax.experimental.pallas import tpu as pltpu

F32 = jnp.float32
BF16 = jnp.bfloat16

N_DEV = 8
DEPTH = 4
SEQ = 2048
D_MODEL = 1024
N_Q_HEADS = 8
HEAD_DIM = 64
ATTN_W = 512
KV_W = 128
BLOCK = 128
N_BLOCKS = SEQ // BLOCK
ROPE_THETA = 500000.0
ROT_DIM = 16
NEG_INF = -1e30
WIDTH = 512
SSM_GROUPS = 32
SSM_GROUP = 16
SSM_STATE = 64
CHUNK = 256
N_CHUNKS = SEQ // CHUNK
GATE_W = 3 * D_MODEL
IN_COLS = 5888
FFN_H = 2816
NORM_EPS = 1e-6
LR, B1, B2, ADAM_EPS, WD, STEP = 0.001, 0.9, 0.999, 1e-08, 0.01, 10

COL_Q, COL_KV, COL_CBX, COL_U, COL_G = 0, 512, 768, 2304, 2816
PIECE_W = (512, 256, 512, 512, 512, 512, 3072)
PIECE_OFF = tuple(sum(PIECE_W[:i]) for i in range(len(PIECE_W)))

KINDS = (("win_t", 736, 1024), ("wffn_t", 704, 1024), ("wout", 352, 1024), ("wmix", 128, 1024),
         ("branch_t", 128, 1536), ("wglu", 64, 512))

REPLICATED = ("norm_mix", "b_gate", "attn_sinks", "ssm_a_re", "ssm_a_im", "ssm_b_re", "ssm_b_im", "ssm_c_re",
              "ssm_c_im", "ssm_d", "ssm_log_dt", "norm_ffn")

VMEM_LIMIT = 56 * 1024 * 1024
NT = (((1,), (1,)), ((), ()))
TN = (((0,), (0,)), ((), ()))
MESH_ID = pl.DeviceIdType.MESH
ANY = pl.BlockSpec(memory_space=pl.ANY)
HBM = pl.BlockSpec(memory_space=pltpu.HBM)
SEM = pl.BlockSpec(memory_space=pltpu.SEMAPHORE)
EFFECT = pltpu.SideEffectType.DATAFLOW_SIDE_EFFECTING


def _cp(**kw):
    return pltpu.CompilerParams(vmem_limit_bytes=VMEM_LIMIT, **kw)


def _full(shape):
    return pl.BlockSpec(shape, lambda *_: (0,) * len(shape))


def _resident(shape):
    return pl.BlockSpec(shape, lambda *_: (0,) * len(shape), pipeline_mode=pl.Buffered(1))


def _mm_tn(a, b, *, tm, tn, name):
    k, m = a.shape
    n = b.shape[1]

    def body(a_ref, b_ref, o_ref):
        o_ref[...] = lax.dot_general(a_ref[...].astype(BF16), b_ref[...].astype(BF16), TN,
                                     preferred_element_type=F32).astype(BF16)

    return pl.pallas_call(
        body, grid=(m // tm, n // tn),
        in_specs=[pl.BlockSpec((k, tm), lambda i, j: (0, i)), pl.BlockSpec((k, tn), lambda i, j: (0, j))],
        out_specs=pl.BlockSpec((tm, tn), lambda i, j: (i, j)),
        out_shape=jax.ShapeDtypeStruct((m, n), BF16), compiler_params=_cp(), name=name)(a, b)


def _rms_rows(xv, g):
    r = lax.rsqrt(jnp.mean(xv * xv, axis=-1, keepdims=True) + NORM_EPS)
    return ((xv * r) * g).astype(BF16)


def _rms_mm_in(x, g, wt, tabs, cw, layer, tie):
    tt = 512
    widths = (3 * WIDTH, WIDTH, GATE_W)
    offs = (COL_CBX, COL_U, COL_G)

    def body(x_ref, g_ref, w_ref, tc_ref, ta_ref, tb_ref, cw_ref, tie_ref,
             q_ref, kv_ref, cbx_ref, u_ref, gl_ref, cv_ref, h_ref, tail_ref):
        @pl.when(pl.program_id(0) == 0)
        def _():
            tail_ref[...] = jnp.zeros_like(tail_ref)

        h = _rms_rows(x_ref[...], g_ref[...])
        h_ref[...] = h
        prod = lax.dot_general(h, w_ref[...], NT, preferred_element_type=F32)
        for ref, o, w in zip((cbx_ref, u_ref, gl_ref), offs, widths):
            ref[...] = prod[:, o:o + w]
        c, a, b = tc_ref[...], ta_ref[...], tb_ref[...]
        for j in range(ATTN_W // 128):
            q_ref[:, 128 * j:128 * (j + 1)] = _rope(prod[:, 128 * j:128 * (j + 1)], c, a, b) * (HEAD_DIM ** -0.5)
        kv_ref[:, :KV_W] = _rope(prod[:, COL_KV:COL_KV + KV_W], c, a, b)
        kv_ref[:, KV_W:] = prod[:, COL_KV + KV_W:COL_CBX]

        row = lax.broadcasted_iota(jnp.int32, (tt, 128), 0)
        for j in range(WIDTH // 128):
            cols = slice(128 * j, 128 * (j + 1))
            cb = prod[:, COL_CBX + 128 * j:COL_CBX + 128 * (j + 1)]
            z = prod[:, COL_CBX + WIDTH + 128 * j:COL_CBX + WIDTH + 128 * (j + 1)] \
                * prod[:, COL_CBX + 2 * WIDTH + 128 * j:COL_CBX + 2 * WIDTH + 128 * (j + 1)]
            before1, before2 = tail_ref[7:8, cols], tail_ref[6:7, cols]
            z1 = jnp.where(row == 0, before1, pltpu.roll(z, 1, axis=0))
            z2 = jnp.where(row == 0, before2, jnp.where(row == 1, before1, pltpu.roll(z, 2, axis=0)))
            s = cw_ref[0:1, cols] * z2 + cw_ref[1:2, cols] * z1 + cw_ref[2:3, cols] * z
            cv_ref[:, cols] = (cb * s).astype(BF16)
            tail_ref[:, cols] = z[tt - 8:, :]

    row_spec = lambda w: pl.BlockSpec((tt, w), lambda i: (i, 0))
    sds = jax.ShapeDtypeStruct
    return pl.pallas_call(
        body, grid=(SEQ // tt,),
        in_specs=[row_spec(D_MODEL), _full((1, D_MODEL)), _resident((IN_COLS, D_MODEL)),
                  row_spec(128), row_spec(128), row_spec(128),
                  pl.BlockSpec((None, 8, WIDTH), lambda i: (layer, 0, 0)), ANY],
        out_specs=[row_spec(ATTN_W), row_spec(2 * KV_W), row_spec(3 * WIDTH), row_spec(WIDTH), row_spec(GATE_W),
                   row_spec(WIDTH), row_spec(D_MODEL)],
        out_shape=[sds((SEQ, ATTN_W), F32), sds((SEQ, 2 * KV_W), F32), sds((SEQ, 3 * WIDTH), F32),
                   sds((SEQ, WIDTH), F32), sds((SEQ, GATE_W), F32), sds((SEQ, WIDTH), BF16),
                   sds((SEQ, D_MODEL), BF16)],
        scratch_shapes=[pltpu.VMEM((8, WIDTH), F32)], compiler_params=_cp(), name="rms_mm_in",
    )(x, g, wt, *tabs, cw, tie)


def _rms_mm_ffn(x, g, wt):
    tt = 256

    def body(x_ref, g_ref, w_ref, act_ref, up_ref, silu_ref, dsilu_ref, h_ref):
        h = _rms_rows(x_ref[...], g_ref[...])
        h_ref[...] = h
        prod = lax.dot_general(h, w_ref[...], NT, preferred_element_type=F32)
        gt, up = prod[:, :FFN_H], prod[:, FFN_H:]
        sg = jax.nn.sigmoid(gt)
        silu = gt * sg
        act_ref[...] = (silu * up).astype(BF16)
        up_ref[...] = up.astype(BF16)
        silu_ref[...] = silu.astype(BF16)
        dsilu_ref[...] = (sg + silu * (1.0 - sg)).astype(BF16)

    row = lambda w: pl.BlockSpec((tt, w), lambda i: (i, 0))
    return pl.pallas_call(
        body, grid=(SEQ // tt,), in_specs=[row(D_MODEL), _full((1, D_MODEL)), _resident((2 * FFN_H, D_MODEL))],
        out_specs=[row(FFN_H)] * 4 + [row(D_MODEL)],
        out_shape=[jax.ShapeDtypeStruct((SEQ, FFN_H), BF16)] * 4 + [jax.ShapeDtypeStruct((SEQ, D_MODEL), BF16)],
        compiler_params=_cp(), name="rms_mm_ffn")(x, g, wt)


def _mm_rmsbwd(pieces, wt, x, g, dres, name):
    tt = 512
    widths = [p.shape[1] for p in pieces]
    offs = [sum(widths[:i]) for i in range(len(widths))]
    n = len(pieces)

    def body(*refs):
        p_refs, (w_ref, x_ref, g_ref, r_ref, dx_ref, dg_ref) = refs[:n], refs[n:]

        @pl.when(pl.program_id(0) == 0)
        def _():
            dg_ref[...] = jnp.zeros_like(dg_ref)

        dh = jnp.zeros((tt, D_MODEL), F32)
        for p_ref, o, w in zip(p_refs, offs, widths):
            dh += jnp.dot(p_ref[...], w_ref[o:o + w, :], preferred_element_type=F32)
        xv = x_ref[...]
        r = lax.rsqrt(jnp.mean(xv * xv, axis=-1, keepdims=True) + NORM_EPS)
        xh = xv * r
        gy = dh * g_ref[...]
        dx_ref[...] = r_ref[...] + r * (gy - xh * jnp.mean(gy * xh, axis=-1, keepdims=True))
        dg_ref[...] += jnp.sum(dh * xh, axis=0, keepdims=True)

    row = lambda w: pl.BlockSpec((tt, w), lambda i: (i, 0))
    return pl.pallas_call(
        body, grid=(SEQ // tt,),
        in_specs=[row(w) for w in widths] + [_resident(wt.shape), row(D_MODEL), _full((1, D_MODEL)), row(D_MODEL)],
        out_specs=[row(D_MODEL), _full((1, D_MODEL))],
        out_shape=[jax.ShapeDtypeStruct((SEQ, D_MODEL), F32), jax.ShapeDtypeStruct((1, D_MODEL), F32)],
        compiler_params=_cp(), name=name)(*pieces, wt, x, g, dres)


def _tn_pieces(pieces, h):
    tk, tn = 512, 512
    nk = SEQ // tk
    n = len(pieces)

    def body(*refs):
        p_refs, (h_ref, o_ref, acc_ref) = refs[:n], refs[n:]
        kk = pl.program_id(1)

        @pl.when(kk == 0)
        def _():
            acc_ref[...] = jnp.zeros_like(acc_ref)

        hv = h_ref[...]
        for p_ref, o, w in zip(p_refs, PIECE_OFF, PIECE_W):
            acc_ref[o:o + w, :] += lax.dot_general(p_ref[...], hv, TN, preferred_element_type=F32)

        @pl.when(kk == nk - 1)
        def _():
            o_ref[...] = acc_ref[...].astype(BF16)

    return pl.pallas_call(
        body, grid=(D_MODEL // tn, nk),
        in_specs=[pl.BlockSpec((tk, w), lambda j, kk: (kk, 0)) for w in PIECE_W]
        + [pl.BlockSpec((tk, tn), lambda j, kk: (kk, j))],
        out_specs=pl.BlockSpec((IN_COLS, tn), lambda j, kk: (0, j)),
        out_shape=jax.ShapeDtypeStruct((IN_COLS, D_MODEL), BF16),
        scratch_shapes=[pltpu.VMEM((IN_COLS, tn), F32)], compiler_params=_cp(), name="tn_pieces")(*pieces, h)


def _tn_branches(dys, acts):
    tk = 512
    nk = SEQ // tk

    def body(d0, d1, d2, a0, a1, a2, o_ref, acc_ref):
        kk = pl.program_id(0)

        @pl.when(kk == 0)
        def _():
            acc_ref[...] = jnp.zeros_like(acc_ref)

        for j, (d, a) in enumerate(((d0, a0), (d1, a1), (d2, a2))):
            acc_ref[:, WIDTH * j:WIDTH * (j + 1)] += lax.dot_general(d[...], a[...], TN, preferred_element_type=F32)

        @pl.when(kk == nk - 1)
        def _():
            o_ref[...] = acc_ref[...].astype(BF16)

    row = lambda w: pl.BlockSpec((tk, w), lambda kk: (kk, 0))
    return pl.pallas_call(
        body, grid=(nk,), in_specs=[row(D_MODEL)] * 3 + [row(WIDTH)] * 3,
        out_specs=_full((D_MODEL, 3 * WIDTH)), out_shape=jax.ShapeDtypeStruct((D_MODEL, 3 * WIDTH), BF16),
        scratch_shapes=[pltpu.VMEM((D_MODEL, 3 * WIDTH), F32)], compiler_params=_cp(), name="tn_branches",
    )(*dys, *acts)


def _rope(t, c, a, b):
    return t * c + pltpu.roll(t, 120, axis=1) * a + pltpu.roll(t, 8, axis=1) * b


def _rope_t(d, c, a, b):
    return d * c + pltpu.roll(d * a, 8, axis=1) + pltpu.roll(d * b, 120, axis=1)


def _band_sides(band):
    left = lax.broadcasted_iota(jnp.int32, band.shape, 1) < HEAD_DIM
    h0 = jnp.where(left, band, 0.0)
    h1 = jnp.where(left, 0.0, band)
    r0 = pltpu.roll(h0, HEAD_DIM, axis=1)
    r1 = pltpu.roll(h1, HEAD_DIM, axis=1)
    return ((h0.astype(BF16), r0.astype(BF16)), (r1.astype(BF16), h1.astype(BF16)))


def _attn_mask(i):
    qi = lax.broadcasted_iota(jnp.int32, (2 * BLOCK, 2 * BLOCK), 0) % BLOCK
    kj = lax.broadcasted_iota(jnp.int32, (2 * BLOCK, 2 * BLOCK), 1)
    delta = qi + BLOCK - kj
    return (delta >= 0) & (delta < BLOCK) & ((kj >= BLOCK) | (i > 0))


def _attn_probs(s, ok, sink):
    s = jnp.where(ok, s, NEG_INF)
    m = jnp.maximum(jnp.max(s, axis=-1, keepdims=True), sink)
    p = jnp.exp(s - m)
    es = jnp.exp(sink - m)
    inv = 1.0 / (jnp.sum(p, axis=-1, keepdims=True) + es)
    return p * inv, es * inv


def _kv_group(qs, ks, vs, kh, sink_ref):
    q2 = jnp.concatenate([qs[2 * kh], qs[2 * kh + 1]], axis=0)
    kst = jnp.concatenate([ks[kh][0], ks[kh][1]], axis=0)
    vst = jnp.concatenate([vs[kh][0], vs[kh][1]], axis=0)
    top = lax.broadcasted_iota(jnp.int32, (2 * BLOCK, 1), 0) < BLOCK
    sinks = [jnp.where(top, sink_ref[0, 4 * kh + h], sink_ref[0, 4 * kh + 2 + h]) for h in range(2)]
    return q2, kst, vst, sinks


def _attn_load(q_ref, kvc_ref, kvp_ref, tc_ref, ta_ref, tb_ref, pc_ref, pa_ref, pb_ref):
    c, a, b = tc_ref[...], ta_ref[...], tb_ref[...]
    kband = jnp.concatenate([kvp_ref[:, :KV_W], kvc_ref[:, :KV_W]], axis=0)
    vband = jnp.concatenate([kvp_ref[:, KV_W:], kvc_ref[:, KV_W:]], axis=0)
    qs = [q_ref[:, 128 * j:128 * (j + 1)].astype(BF16) for j in range(4)]
    return qs, _band_sides(kband), _band_sides(vband), (c, a, b)


def _attn_specs(clamp):
    cur = lambda i: (clamp(i), 0)
    prev = lambda i: (jnp.maximum(clamp(i) - 1, 0), 0)
    return [
        pl.BlockSpec((BLOCK, ATTN_W), cur), pl.BlockSpec((BLOCK, 2 * KV_W), cur),
        pl.BlockSpec((BLOCK, 2 * KV_W), prev),
        pl.BlockSpec((BLOCK, 128), cur), pl.BlockSpec((BLOCK, 128), cur), pl.BlockSpec((BLOCK, 128), cur),
        pl.BlockSpec((BLOCK, 128), prev), pl.BlockSpec((BLOCK, 128), prev), pl.BlockSpec((BLOCK, 128), prev),
        pl.BlockSpec(memory_space=pltpu.SMEM),
    ]


def _attn_fwd(q, kv, tabs, sinks):
    tc, ta, tb = tabs

    def body(q_ref, kvc_ref, kvp_ref, tc_ref, ta_ref, tb_ref, pc_ref, pa_ref, pb_ref, sink_ref, o_ref):
        i = pl.program_id(0)
        qs, ks, vs, _ = _attn_load(q_ref, kvc_ref, kvp_ref, tc_ref, ta_ref, tb_ref, pc_ref, pa_ref, pb_ref)
        ok = _attn_mask(i)
        for kh in range(2):
            q2, kst, vst, sinks = _kv_group(qs, ks, vs, kh, sink_ref)
            s = lax.dot_general(q2, kst, NT, preferred_element_type=F32)
            pn = [_attn_probs(s[:, 2 * BLOCK * h:2 * BLOCK * (h + 1)], ok, sinks[h])[0].astype(BF16) for h in range(2)]
            o2 = jnp.dot(jnp.concatenate(pn, axis=1), vst, preferred_element_type=F32).astype(BF16)
            for r in range(2):
                j = 2 * kh + r
                o_ref[:, 128 * j:128 * (j + 1)] = o2[BLOCK * r:BLOCK * (r + 1)]

    return pl.pallas_call(
        body, grid=(N_BLOCKS,), in_specs=_attn_specs(lambda i: i),
        out_specs=pl.BlockSpec((BLOCK, ATTN_W), lambda i: (i, 0)),
        out_shape=jax.ShapeDtypeStruct((SEQ, ATTN_W), BF16), compiler_params=_cp(), name="attn_fwd",
    )(q, kv, kv, tc, ta, tb, tc, ta, tb, sinks)


def _attn_bwd(q, kv, tabs, sinks, do):
    tc, ta, tb = tabs
    last = N_BLOCKS - 1
    clamp = lambda i: jnp.minimum(i, last)

    def place(full, side, kh):
        left = lax.broadcasted_iota(jnp.int32, full.shape, 1) < HEAD_DIM
        valid = jnp.where(left, full, 0.0) if side == 0 else jnp.where(left, 0.0, full)
        return valid if side == kh else pltpu.roll(valid, HEAD_DIM, axis=1)

    def body(q_ref, kvc_ref, kvp_ref, tc_ref, ta_ref, tb_ref, pc_ref, pa_ref, pb_ref, sink_ref, do_ref,
             dq_ref, dkv_ref, ds_ref, carry_ref):
        i = pl.program_id(0)

        @pl.when(i == 0)
        def _():
            ds_ref[...] = jnp.zeros_like(ds_ref)
            carry_ref[...] = jnp.zeros_like(carry_ref)

        @pl.when(i > last)
        def _():
            dkv_ref[...] = carry_ref[...].astype(BF16)

        @pl.when(i <= last)
        def _():
            qs, ks, vs, (c, a, b) = _attn_load(q_ref, kvc_ref, kvp_ref, tc_ref, ta_ref, tb_ref,
                                               pc_ref, pa_ref, pb_ref)
            ok = _attn_mask(i)
            dk = jnp.zeros((2 * BLOCK, 128), F32)
            dv = jnp.zeros((2 * BLOCK, 128), F32)
            dsink = jnp.zeros((1, 128), F32)
            lane = lax.broadcasted_iota(jnp.int32, (1, 128), 1)
            for kh in range(2):
                q2, kst, vst, sinks = _kv_group(qs, ks, vs, kh, sink_ref)
                do2 = jnp.concatenate([do_ref[:, 128 * (2 * kh + r):128 * (2 * kh + r + 1)] for r in range(2)],
                                      axis=0).astype(BF16)
                s = lax.dot_general(q2, kst, NT, preferred_element_type=F32)
                dp = lax.dot_general(do2, vst, NT, preferred_element_type=F32)
                pns, dss = [], []
                for h in range(2):
                    cols = slice(2 * BLOCK * h, 2 * BLOCK * (h + 1))
                    pn, ps = _attn_probs(s[:, cols], ok, sinks[h])
                    dr = jnp.sum(pn * dp[:, cols], axis=-1, keepdims=True)
                    pns.append(pn.astype(BF16))
                    dss.append((pn * (dp[:, cols] - dr)).astype(BF16))
                    for r in range(2):
                        part = -jnp.sum((ps * dr)[BLOCK * r:BLOCK * (r + 1)])
                        dsink += jnp.where(lane == 4 * kh + 2 * r + h, part, 0.0)
                ds2, pn2 = jnp.concatenate(dss, axis=1), jnp.concatenate(pns, axis=1)
                dq2 = jnp.dot(ds2, kst, preferred_element_type=F32) * (HEAD_DIM ** -0.5)
                dk2 = lax.dot_general(ds2, q2, TN, preferred_element_type=F32)
                dv2 = lax.dot_general(pn2, do2, TN, preferred_element_type=F32)
                for h in range(2):
                    dk += place(dk2[2 * BLOCK * h:2 * BLOCK * (h + 1)], h, kh)
                    dv += place(dv2[2 * BLOCK * h:2 * BLOCK * (h + 1)], h, kh)
                for r in range(2):
                    j = 2 * kh + r
                    dq_ref[:, 128 * j:128 * (j + 1)] = _rope_t(dq2[BLOCK * r:BLOCK * (r + 1)], c, a, b).astype(BF16)
            ds_ref[...] += dsink
            dk_prev = _rope_t(dk[:BLOCK], pc_ref[...], pa_ref[...], pb_ref[...])
            dk_cur = _rope_t(dk[BLOCK:], c, a, b)
            prev = jnp.concatenate([dk_prev, dv[:BLOCK]], axis=1)
            dkv_ref[...] = (carry_ref[...] + prev).astype(BF16)
            carry_ref[...] = jnp.concatenate([dk_cur, dv[BLOCK:]], axis=1)

    return pl.pallas_call(
        body, grid=(N_BLOCKS + 1,),
        in_specs=_attn_specs(clamp) + [pl.BlockSpec((BLOCK, ATTN_W), lambda i: (clamp(i), 0))],
        out_specs=[pl.BlockSpec((BLOCK, ATTN_W), lambda i: (clamp(i), 0)),
                   pl.BlockSpec((BLOCK, 2 * KV_W), lambda i: (jnp.maximum(i - 1, 0), 0)),
                   pl.BlockSpec((1, 128), lambda i: (0, 0))],
        out_shape=[jax.ShapeDtypeStruct((SEQ, ATTN_W), BF16), jax.ShapeDtypeStruct((SEQ, 2 * KV_W), BF16),
                   jax.ShapeDtypeStruct((1, 128), F32)],
        scratch_shapes=[pltpu.VMEM((BLOCK, 2 * KV_W), F32)], compiler_params=_cp(), name="attn_bwd",
    )(q, kv, kv, tc, ta, tb, tc, ta, tb, sinks, do)


def _shift_down(z, k):
    row = lax.broadcasted_iota(jnp.int32, z.shape, 0)
    return jnp.where(row < k, 0.0, pltpu.roll(z, k, axis=0))


def _shift_up(z, k):
    n = z.shape[0]
    row = lax.broadcasted_iota(jnp.int32, z.shape, 0)
    return jnp.where(row >= n - k, 0.0, pltpu.roll(z, n - k, axis=0))


def _conv_specs():
    nb = WIDTH // 128
    return [pl.BlockSpec((SEQ, 128), lambda j: (0, j)), pl.BlockSpec((SEQ, 128), lambda j: (0, nb + j)),
            pl.BlockSpec((SEQ, 128), lambda j: (0, 2 * nb + j)), pl.BlockSpec((None, 8, 128), lambda j: (0, 0, j))]


def _conv_bwd(cbx, cw, layer, dout, tie):
    def body(cb_ref, cc_ref, cx_ref, w_ref, do_ref, tie_ref, dcb_ref, dcc_ref, dcx_ref, dw_ref):
        cc, cx = cc_ref[...], cx_ref[...]
        z = cc * cx
        z1, z2 = _shift_down(z, 1), _shift_down(z, 2)
        w0, w1, w2 = w_ref[0:1, :], w_ref[1:2, :], w_ref[2:3, :]
        dout = do_ref[...]
        ds = dout * cb_ref[...]
        dcb_ref[...] = (dout * (w0 * z2 + w1 * z1 + w2 * z)).astype(BF16)
        dz = w2 * ds + w1 * _shift_up(ds, 1) + w0 * _shift_up(ds, 2)
        dcc_ref[...] = (dz * cx).astype(BF16)
        dcx_ref[...] = (dz * cc).astype(BF16)
        rows = [jnp.sum(ds * zz, axis=0, keepdims=True) for zz in (z2, z1, z)]
        dw_ref[...] = jnp.concatenate(rows + [jnp.zeros((5, 128), F32)], axis=0)

    col = lambda j: (0, j)
    specs = _conv_specs()
    specs[3] = pl.BlockSpec((None, 8, 128), lambda j: (layer, 0, j))
    return pl.pallas_call(
        body, grid=(WIDTH // 128,), in_specs=specs + [pl.BlockSpec((SEQ, 128), col), ANY],
        out_specs=[pl.BlockSpec((SEQ, 128), col), pl.BlockSpec((SEQ, 128), col), pl.BlockSpec((SEQ, 128), col),
                   pl.BlockSpec((8, 128), col)],
        out_shape=[jax.ShapeDtypeStruct((SEQ, WIDTH), BF16)] * 3 + [jax.ShapeDtypeStruct((8, WIDTH), F32)],
        compiler_params=_cp(), name="conv_bwd",
    )(cbx, cbx, cbx, cw, dout, tie)


def _ssm_prep_math(a_re, a_im, log_dt, bt_re, bt_im):
    dt = jnp.exp(log_dt)
    er = jnp.exp(a_re * dt)
    lr = er * jnp.cos(a_im * dt)
    li = er * jnp.sin(a_im * dt)
    n2 = a_re * a_re + a_im * a_im
    cr = ((lr - 1.0) * a_re + li * a_im) / n2
    ci = (li * a_re - (lr - 1.0) * a_im) / n2
    cr3, ci3 = cr[:, None, :], ci[:, None, :]
    return lr, li, cr3 * bt_re - ci3 * bt_im, cr3 * bt_im + ci3 * bt_re


_GS = (SSM_GROUPS, SSM_STATE)
_GHS = (SSM_GROUPS, SSM_GROUP, SSM_STATE)


def _layered(shape):
    return pl.BlockSpec((None,) + shape, lambda l: (l,) + (0,) * len(shape))


def _ssm_prep(a_re, a_im, log_dt, bt_re, bt_im):
    def body(ar, ai, ld, br, bi, o0, o1, o2, o3):
        outs = _ssm_prep_math(ar[...], ai[...], ld[...], br[...], bi[...])
        for o, v in zip((o0, o1, o2, o3), outs):
            o[...] = v

    shapes = [_GS, _GS, _GHS, _GHS]
    return pl.pallas_call(
        body, grid=(DEPTH,), in_specs=[_layered(s) for s in (_GS, _GS, (SSM_GROUPS, 1), _GHS, _GHS)],
        out_specs=[_layered(s) for s in shapes],
        out_shape=[jax.ShapeDtypeStruct((DEPTH,) + s, F32) for s in shapes],
        name="ssm_prep")(a_re, a_im, log_dt, bt_re, bt_im)


def _ssm_prep_bwd(a_re, a_im, log_dt, bt_re, bt_im, cots):
    def body(ar, ai, ld, br, bi, c0, c1, c2, c3, o0, o1, o2, o3, o4):
        _, vjp = jax.vjp(_ssm_prep_math, ar[...], ai[...], ld[...], br[...], bi[...])
        for o, v in zip((o0, o1, o2, o3, o4), vjp((c0[...], c1[...], c2[...], c3[...]))):
            o[...] = v

    ins = (_GS, _GS, (SSM_GROUPS, 1), _GHS, _GHS)
    return pl.pallas_call(
        body, grid=(DEPTH,), in_specs=[_layered(s) for s in ins + (_GS, _GS, _GHS, _GHS)],
        out_specs=[_layered(s) for s in ins],
        out_shape=[jax.ShapeDtypeStruct((DEPTH,) + s, F32) for s in ins],
        name="ssm_prep_bwd")(a_re, a_im, log_dt, bt_re, bt_im, *cots)


LANES_G = 512
N_LANE_GROUPS = SSM_GROUPS * SSM_STATE // LANES_G


def _ssm_embed(b_re, b_im, c_re, c_im):
    rows = SSM_GROUPS * SSM_GROUP

    def body(br, bi, cr, ci, o0, o1, o2, o3):
        state = lax.broadcasted_iota(jnp.int32, (SSM_STATE, LANES_G), 0)
        lane = lax.broadcasted_iota(jnp.int32, (SSM_STATE, LANES_G), 1)
        spread = (lane % SSM_STATE == state).astype(BF16)
        r = lax.broadcasted_iota(jnp.int32, (rows, LANES_G), 0)
        c = lax.broadcasted_iota(jnp.int32, (rows, LANES_G), 1)
        own = (r % 128) // SSM_GROUP == c // SSM_STATE
        for ref, o, sign in ((br, o0, 1.0), (bi, o1, 1.0), (cr, o2, 1.0), (ci, o3, -1.0)):
            t = (sign * ref[...]).reshape(rows, SSM_STATE).astype(BF16)
            wide = jnp.dot(t, spread, preferred_element_type=F32)
            o[...] = jnp.where(own, wide, 0.0).astype(BF16).reshape(N_LANE_GROUPS, 128, LANES_G)

    out = (N_LANE_GROUPS, 128, LANES_G)
    return pl.pallas_call(
        body, grid=(DEPTH,), in_specs=[_layered(_GHS)] * 4, out_specs=[_layered(out)] * 4,
        out_shape=[jax.ShapeDtypeStruct((DEPTH,) + out, BF16)] * 4, name="ssm_embed")(b_re, b_im, c_re, c_im)


def _scan_in_place(xr_ref, xi_ref, ar, ai, reverse):
    shape = (N_CHUNKS, xr_ref.shape[1])
    ar, ai = jnp.broadcast_to(ar, shape), jnp.broadcast_to(ai, shape)

    def rows(tau):
        t = (CHUNK - 1 - tau) if reverse else tau
        return pl.ds(pl.multiple_of(t * N_CHUNKS, N_CHUNKS), N_CHUNKS)

    def step(tau, carry):
        sr, si = carry
        return ar * sr - ai * si + xr_ref[rows(tau), :], ar * si + ai * sr + xi_ref[rows(tau), :]

    zero = jnp.zeros(shape, F32)
    er, ei = lax.fori_loop(0, CHUNK, step, (zero, zero), unroll=8)
    qr, qi = ar, ai
    for _ in range(8):
        qr, qi = qr * qr - qi * qi, 2.0 * qr * qi
    shift = _shift_up if reverse else _shift_down
    for k in (1, 2, 4):
        sr, si = shift(er, k), shift(ei, k)
        er, ei = er + qr * sr - qi * si, ei + qr * si + qi * sr
        qr, qi = qr * qr - qi * qi, 2.0 * qr * qi
    start = (shift(er, 1), shift(ei, 1))

    def write(tau, carry):
        sr, si = step(tau, carry)
        xr_ref[rows(tau), :] = sr
        xi_ref[rows(tau), :] = si
        return sr, si

    return write, start


def _ssm_specs(layer):
    col = lambda w: pl.BlockSpec((SEQ, w), lambda g: (0, g))
    diag = pl.BlockSpec((None, None, 128, LANES_G), lambda g: (layer, g, 0, 0))
    vec = pl.BlockSpec((None, 1, LANES_G), lambda g: (layer, 0, g))
    return col, diag, vec


def _to_scan_order(src_ref, dst_ref):
    for tau in range(CHUNK):
        dst_ref[pl.ds(tau * N_CHUNKS, N_CHUNKS), :] = src_ref[pl.ds(tau, N_CHUNKS, stride=CHUNK), :]


def _to_time_order(src_ref, dst_ref, dtype):
    for j in range(N_CHUNKS):
        dst_ref[pl.ds(j * CHUNK, CHUNK), :] = src_ref[pl.ds(j, CHUNK, stride=N_CHUNKS), :].astype(dtype)


def _ssm_fwd(u, mats, layer, d):
    def body(u_ref, d_ref, br_ref, bi_ref, cr_ref, ci_ref, ar_ref, ai_ref, xr_ref, xi_ref, y_ref, us_ref):
        _to_scan_order(u_ref, us_ref)
        uv = us_ref[...].astype(BF16)
        xr_ref[...] = jnp.dot(uv, br_ref[...], preferred_element_type=F32)
        xi_ref[...] = jnp.dot(uv, bi_ref[...], preferred_element_type=F32)
        write, start = _scan_in_place(xr_ref, xi_ref, ar_ref[...], ai_ref[...], False)
        lax.fori_loop(0, CHUNK, write, start, unroll=8)
        y = lax.dot_general(xr_ref[...].astype(BF16), cr_ref[...], NT, preferred_element_type=F32)
        y += lax.dot_general(xi_ref[...].astype(BF16), ci_ref[...], NT, preferred_element_type=F32)
        us_ref[...] = y + d_ref[...] * us_ref[...]
        _to_time_order(us_ref, y_ref, F32)

    col, diag, vec = _ssm_specs(layer)
    return pl.pallas_call(
        body, grid=(N_LANE_GROUPS,),
        in_specs=[col(128), pl.BlockSpec((None, 1, 128), lambda g: (layer, 0, g)),
                  diag, diag, diag, diag, vec, vec],
        out_specs=[col(LANES_G), col(LANES_G), col(128)],
        out_shape=[jax.ShapeDtypeStruct((SEQ, SSM_GROUPS * SSM_STATE), F32)] * 2
        + [jax.ShapeDtypeStruct((SEQ, WIDTH), F32)],
        scratch_shapes=[pltpu.VMEM((SEQ, 128), F32)], compiler_params=_cp(), name="ssm_fwd",
    )(u, d, mats["b_re"], mats["b_im"], mats["c_re"], mats["c_im_neg"], mats["a_re"], mats["a_im"])


def _ssm_bwd(dy, x_re, x_im, u, mats, layer, d):
    def body(dyt_ref, ut_ref, d_ref, xr_ref, xi_ref, br_ref, bi_ref, cr_ref, ci_ref, ar_ref, ai_ref,
             du_ref, dar_ref, dai_ref, dbr_ref, dbi_ref, dcr_ref, dci_ref, lr_ref, li_ref, dys_ref, u_ref):
        _to_scan_order(dyt_ref, dys_ref)
        _to_scan_order(ut_ref, u_ref)
        dy = dys_ref[...].astype(BF16)
        lr_ref[...] = jnp.dot(dy, cr_ref[...], preferred_element_type=F32)
        li_ref[...] = jnp.dot(dy, ci_ref[...], preferred_element_type=F32)
        write, start = _scan_in_place(lr_ref, li_ref, ar_ref[...], -ai_ref[...], True)

        def rows(t):
            return pl.ds(pl.multiple_of(t * N_CHUNKS, N_CHUNKS), N_CHUNKS)

        def grad(acc, lam, xpr, xpi):
            return acc[0] + xpr * lam[0] + xpi * lam[1], acc[1] + xpr * lam[1] - xpi * lam[0]

        def down(tau, carry):
            lam = write(tau, carry[0])
            t = CHUNK - 2 - tau
            return lam, grad(carry[1], lam, xr_ref[rows(t), :], xi_ref[rows(t), :])

        zero = jnp.zeros((N_CHUNKS, LANES_G), F32)
        lam, acc = lax.fori_loop(0, CHUNK - 1, down, (start, (zero, zero)), unroll=5)
        lam = write(CHUNK - 1, lam)
        last = rows(CHUNK - 1)
        acc = grad(acc, lam, _shift_down(xr_ref[last, :], 1), _shift_down(xi_ref[last, :], 1))
        dar_ref[...] = jnp.sum(acc[0], axis=0, keepdims=True)
        dai_ref[...] = jnp.sum(acc[1], axis=0, keepdims=True)

        l_re, l_im = lr_ref[...].astype(BF16), li_ref[...].astype(BF16)
        du = lax.dot_general(l_re, br_ref[...], NT, preferred_element_type=F32)
        du += lax.dot_general(l_im, bi_ref[...], NT, preferred_element_type=F32)
        dys_ref[...] = du + dys_ref[...] * d_ref[...]
        _to_time_order(dys_ref, du_ref, BF16)
        uv = u_ref[...].astype(BF16)
        dbr_ref[...] = lax.dot_general(uv, l_re, TN, preferred_element_type=F32)
        dbi_ref[...] = lax.dot_general(uv, l_im, TN, preferred_element_type=F32)
        dcr_ref[...] = lax.dot_general(dy, xr_ref[...].astype(BF16), TN, preferred_element_type=F32)
        dci_ref[...] = lax.dot_general(dy, xi_ref[...].astype(BF16), TN, preferred_element_type=F32)

    col, diag, vec = _ssm_specs(layer)
    out_vec = pl.BlockSpec((1, LANES_G), lambda g: (0, g))
    out_blk = pl.BlockSpec((None, 128, LANES_G), lambda g: (g, 0, 0))
    sds = jax.ShapeDtypeStruct
    return pl.pallas_call(
        body, grid=(N_LANE_GROUPS,),
        in_specs=[col(128), col(128), pl.BlockSpec((None, 1, 128), lambda g: (layer, 0, g)),
                  col(LANES_G), col(LANES_G), diag, diag, diag, diag, vec, vec],
        out_specs=[col(128), out_vec, out_vec, out_blk, out_blk, out_blk, out_blk],
        out_shape=[sds((SEQ, WIDTH), BF16)] + [sds((1, SSM_GROUPS * SSM_STATE), F32)] * 2
        + [sds((N_LANE_GROUPS, 128, LANES_G), F32)] * 4,
        scratch_shapes=[pltpu.VMEM((SEQ, LANES_G), F32)] * 2 + [pltpu.VMEM((SEQ, 128), F32)] * 2,
        compiler_params=_cp(), name="ssm_bwd",
    )(dy, u, d, x_re, x_im, mats["b_re"], mats["b_im"], mats["c_re"], mats["c_im_neg"],
      mats["a_re"], mats["a_im"])


_GELU_C = math.sqrt(2.0 / math.pi)


def _gelu(y):
    return 0.5 * y * (1.0 + jnp.tanh(_GELU_C * (y + 0.044715 * (y * y * y))))


def _glu_fwd(y, wglu):
    tt = 512

    def body(y_ref, w_ref, z_ref):
        ys = _gelu(y_ref[...])
        a = jnp.dot(ys.astype(BF16), w_ref[...], preferred_element_type=F32)
        z_ref[...] = (ys * jax.nn.sigmoid(a)).astype(BF16)

    blk = pl.BlockSpec((tt, WIDTH), lambda i: (i, 0))
    return pl.pallas_call(body, grid=(SEQ // tt,), in_specs=[blk, _full((WIDTH, WIDTH))], out_specs=blk,
                          out_shape=jax.ShapeDtypeStruct((SEQ, WIDTH), BF16), compiler_params=_cp(),
                          name="glu_fwd")(y, wglu)


def _glu_bwd(y, wglu, dz, u):
    tt = 512

    def body(y_ref, w_ref, dz_ref, u_ref, dy_ref, ys_ref, da_ref, dd_ref):
        @pl.when(pl.program_id(0) == 0)
        def _():
            dd_ref[...] = jnp.zeros_like(dd_ref)

        yv = y_ref[...]
        t = jnp.tanh(_GELU_C * (yv + 0.044715 * (yv * yv * yv)))
        ys = 0.5 * yv * (1.0 + t)
        ysb = ys.astype(BF16)
        sg = jax.nn.sigmoid(jnp.dot(ysb, w_ref[...], preferred_element_type=F32))
        dz = dz_ref[...].astype(F32)
        da = (dz * ys * sg * (1.0 - sg)).astype(BF16)
        dys = dz * sg + lax.dot_general(da, w_ref[...], NT, preferred_element_type=F32)
        dy = dys * (0.5 * (1.0 + t) + 0.5 * yv * (1.0 - t * t) * _GELU_C * (1.0 + 3 * 0.044715 * (yv * yv)))
        dy_ref[...] = dy
        ys_ref[...] = ysb
        da_ref[...] = da
        dd_ref[...] += jnp.sum(dy * u_ref[...], axis=0, keepdims=True)

    blk = pl.BlockSpec((tt, WIDTH), lambda i: (i, 0))
    return pl.pallas_call(
        body, grid=(SEQ // tt,), in_specs=[blk, _full((WIDTH, WIDTH)), blk, blk],
        out_specs=[blk, blk, blk, _full((1, WIDTH))],
        out_shape=[jax.ShapeDtypeStruct((SEQ, WIDTH), F32)] + [jax.ShapeDtypeStruct((SEQ, WIDTH), BF16)] * 2
        + [jax.ShapeDtypeStruct((1, WIDTH), F32)],
        compiler_params=_cp(), name="glu_bwd")(y, wglu, dz, u)


def _mix_specs(tt, layer):
    row = lambda w: pl.BlockSpec((tt, w), lambda i: (i, 0))
    gate = lambda j: pl.BlockSpec((tt, D_MODEL), lambda i: (i, j))
    wo = lambda j: pl.BlockSpec((D_MODEL, WIDTH), lambda i: (0, j))
    return [row(D_MODEL), row(WIDTH), row(WIDTH), row(WIDTH), gate(0), gate(1), gate(2),
            pl.BlockSpec((None, 1, GATE_W), lambda i: (layer, 0, 0)), wo(0), wo(1), wo(2),
            _full((D_MODEL, D_MODEL))]


def _mix_branches(o_ref, c_ref, z_ref, g_refs, b_ref, wa_ref, wc_ref, ws_ref):
    ys = [lax.dot_general(r[...], w[...], NT, preferred_element_type=F32)
          for r, w in ((o_ref, wa_ref), (c_ref, wc_ref), (z_ref, ws_ref))]
    gates = [jax.nn.sigmoid(g_refs[j][...] + b_ref[:, D_MODEL * j:D_MODEL * (j + 1)]) for j in range(3)]
    return ys, gates


def _mix_fwd(x, o, cv, z, glog, b_gate, layer, wbt, wmix, tie):
    tt = 256

    def body(x_ref, o_ref, c_ref, z_ref, g0, g1, g2, b_ref, wa_ref, wc_ref, ws_ref, wm_ref, tie_ref, x1_ref):
        ys, gates = _mix_branches(o_ref, c_ref, z_ref, (g0, g1, g2), b_ref, wa_ref, wc_ref, ws_ref)
        merged = gates[0] * ys[0] + gates[1] * ys[1] + gates[2] * ys[2]
        x1_ref[...] = x_ref[...] + jnp.dot(merged.astype(BF16), wm_ref[...], preferred_element_type=F32)

    return pl.pallas_call(
        body, grid=(SEQ // tt,), in_specs=_mix_specs(tt, layer) + [ANY],
        out_specs=pl.BlockSpec((tt, D_MODEL), lambda i: (i, 0)),
        out_shape=jax.ShapeDtypeStruct((SEQ, D_MODEL), F32), compiler_params=_cp(), name="mix_fwd",
    )(x, o, cv, z, glog, glog, glog, b_gate, wbt, wbt, wbt, wmix, tie)


def _mix_bwd(dx1, o, cv, z, glog, b_gate, layer, wbt, wmix, tie):
    tt = 256

    def body(dx_ref, o_ref, c_ref, z_ref, g0, g1, g2, b_ref, wa_ref, wc_ref, ws_ref, wm_ref, tie_ref,
             mg_ref, dya_ref, dyc_ref, dys_ref, do_ref, dc_ref, dz_ref, dgl_ref, db_ref):
        @pl.when(pl.program_id(0) == 0)
        def _():
            db_ref[...] = jnp.zeros_like(db_ref)

        ys, gates = _mix_branches(o_ref, c_ref, z_ref, (g0, g1, g2), b_ref, wa_ref, wc_ref, ws_ref)
        mg_ref[...] = (gates[0] * ys[0] + gates[1] * ys[1] + gates[2] * ys[2]).astype(BF16)
        dm = lax.dot_general(dx_ref[...].astype(BF16), wm_ref[...], NT, preferred_element_type=F32)
        for j, (dy_ref, w_ref, d_ref) in enumerate(((dya_ref, wa_ref, do_ref), (dyc_ref, wc_ref, dc_ref),
                                                    (dys_ref, ws_ref, dz_ref))):
            dy = (dm * gates[j]).astype(BF16)
            dy_ref[...] = dy
            d_ref[...] = jnp.dot(dy, w_ref[...], preferred_element_type=F32)
            dgl = dm * ys[j] * gates[j] * (1.0 - gates[j])
            dgl_ref[:, D_MODEL * j:D_MODEL * (j + 1)] = dgl.astype(BF16)
            db_ref[:, D_MODEL * j:D_MODEL * (j + 1)] += jnp.sum(dgl, axis=0, keepdims=True)

    row = lambda w: pl.BlockSpec((tt, w), lambda i: (i, 0))
    sds = jax.ShapeDtypeStruct
    return pl.pallas_call(
        body, grid=(SEQ // tt,), in_specs=_mix_specs(tt, layer) + [ANY],
        out_specs=[row(D_MODEL)] * 4 + [row(WIDTH)] * 3 + [row(GATE_W), _full((1, GATE_W))],
        out_shape=[sds((SEQ, D_MODEL), BF16)] * 4 + [sds((SEQ, WIDTH), F32)] * 3
        + [sds((SEQ, GATE_W), BF16), sds((1, GATE_W), F32)],
        compiler_params=_cp(), name="mix_bwd",
    )(dx1, o, cv, z, glog, glog, glog, b_gate, wbt, wbt, wbt, wmix, tie)


def _ffn_out_fwd(x1, act, wout, tie):
    tt = 512

    def body(x_ref, a_ref, w_ref, tie_ref, o_ref):
        o_ref[...] = x_ref[...] + jnp.dot(a_ref[...], w_ref[...], preferred_element_type=F32)

    row = lambda w: pl.BlockSpec((tt, w), lambda i: (i, 0))
    return pl.pallas_call(
        body, grid=(SEQ // tt,), in_specs=[row(D_MODEL), row(FFN_H), _full((FFN_H, D_MODEL)), ANY],
        out_specs=row(D_MODEL), out_shape=jax.ShapeDtypeStruct((SEQ, D_MODEL), F32),
        compiler_params=_cp(), name="ffn_out_fwd")(x1, act, wout, tie)


def _ffn_out_bwd(dx2, up, silu, dsilu, wout, tie):
    tt = 512

    def body(dx_ref, up_ref, silu_ref, dsilu_ref, w_ref, tie_ref, dgu_ref):
        dact = lax.dot_general(dx_ref[...].astype(BF16), w_ref[...], NT, preferred_element_type=F32).astype(BF16)
        dgu_ref[:, :FFN_H] = dact * up_ref[...] * dsilu_ref[...]
        dgu_ref[:, FFN_H:] = dact * silu_ref[...]

    row = lambda w: pl.BlockSpec((tt, w), lambda i: (i, 0))
    return pl.pallas_call(
        body, grid=(SEQ // tt,),
        in_specs=[row(D_MODEL), row(FFN_H), row(FFN_H), row(FFN_H), _resident((FFN_H, D_MODEL)), ANY],
        out_specs=row(2 * FFN_H), out_shape=jax.ShapeDtypeStruct((SEQ, 2 * FFN_H), BF16),
        compiler_params=_cp(), name="ffn_out_bwd")(dx2, up, silu, dsilu, wout, tie)


def _loss_head(x, g, target):
    tt = 256

    def body(x_ref, g_ref, t_ref, loss_ref, dx_ref, dg_ref):
        @pl.when(pl.program_id(0) == 0)
        def _():
            loss_ref[...] = jnp.zeros_like(loss_ref)
            dg_ref[...] = jnp.zeros_like(dg_ref)

        xv = x_ref[...]
        r = lax.rsqrt(jnp.mean(xv * xv, axis=-1, keepdims=True) + NORM_EPS)
        xh = xv * r
        err = xh * g_ref[...] - t_ref[...]
        loss_ref[...] += 0.5 * jnp.sum(jnp.mean(err * err, axis=-1, keepdims=True))
        dy = err * (1.0 / D_MODEL)
        gy = dy * g_ref[...]
        dx_ref[...] = r * (gy - xh * jnp.mean(gy * xh, axis=-1, keepdims=True))
        dg_ref[...] += jnp.sum(dy * xh, axis=0, keepdims=True)

    row = pl.BlockSpec((tt, D_MODEL), lambda i: (i, 0))
    return pl.pallas_call(
        body, grid=(SEQ // tt,), in_specs=[row, _full((1, D_MODEL)), row],
        out_specs=[_full((1, 128)), row, _full((1, D_MODEL))],
        out_shape=[jax.ShapeDtypeStruct((1, 128), F32), jax.ShapeDtypeStruct((SEQ, D_MODEL), F32),
                   jax.ShapeDtypeStruct((1, D_MODEL), F32)],
        compiler_params=_cp(), name="loss_head")(x, g, target)


def _adam_math(g, w, m, v):
    nm = B1 * m + (1.0 - B1) * g
    nv = B2 * v + (1.0 - B2) * (g * g)
    m_hat = nm / (1.0 - B1 ** STEP)
    v_hat = nv / (1.0 - B2 ** STEP)
    return -LR * (m_hat / (jnp.sqrt(v_hat) + ADAM_EPS) + WD * w), nm, nv


def _adamw_small(parts, w, m, v, name):
    def body(p_ref, w_ref, m_ref, v_ref, g_ref, d_ref, nm_ref, nv_ref):
        g = p_ref[0].astype(F32)
        for k in range(1, N_DEV):
            g = g + p_ref[k].astype(F32)
        g_ref[...] = g
        d_ref[...], nm_ref[...], nv_ref[...] = _adam_math(g, w_ref[...], m_ref[...], v_ref[...])

    out_shape = [jax.ShapeDtypeStruct(w.shape, F32)] * 4
    if w.ndim < 3:
        return pl.pallas_call(body, out_shape=out_shape, name=name)(parts, w, m, v)
    rest = w.shape[1:]
    zeros = (0,) * len(rest)
    blk = pl.BlockSpec((None,) + rest, lambda l: (l,) + zeros)
    return pl.pallas_call(
        body, grid=(w.shape[0],),
        in_specs=[pl.BlockSpec((N_DEV, None) + rest, lambda l: (0, l) + zeros), blk, blk, blk],
        out_specs=[blk] * 4, out_shape=out_shape, name=name)(parts, w, m, v)


def _adamw(parts, w, m, v, tr, name, groups=None, fill=None, tie=None):
    n_groups, rows, cols = w.shape
    n_parts = parts.shape[1]
    lo, hi = groups if groups is not None else (0, n_groups)

    def body(p_ref, w_ref, m_ref, v_ref, *rest):
        g_ref, d_ref, nm_ref, nv_ref = rest[-4:]
        g = p_ref[0].astype(F32)
        for k in range(1, n_parts):
            g = g + p_ref[k].astype(F32)
        nm = B1 * m_ref[...] + (1.0 - B1) * g
        nv = B2 * v_ref[...] + (1.0 - B2) * (g * g)
        m_hat = nm / (1.0 - B1 ** STEP)
        v_hat = nv / (1.0 - B2 ** STEP)
        g_ref[...] = g
        d_ref[...] = -LR * (m_hat / (jnp.sqrt(v_hat) + ADAM_EPS) + WD * w_ref[...])
        nm_ref[...] = nm
        nv_ref[...] = nv

    blk = pl.BlockSpec((None, tr, cols), lambda l, i: (l + lo, i, 0))
    p_lo = lo if parts.shape[0] == n_groups else 0
    extra = ([] if fill is None else list(fill)) + ([] if tie is None else [tie])
    return pl.pallas_call(
        body, grid=(hi - lo, rows // tr),
        in_specs=[pl.BlockSpec((None, n_parts, tr, cols), lambda l, i: (l + p_lo, 0, i, 0)), blk, blk, blk]
        + [ANY] * len(extra),
        out_specs=[blk] * 4, out_shape=[jax.ShapeDtypeStruct((n_groups, rows, cols), F32)] * 4,
        input_output_aliases={} if fill is None else {4 + j: j for j in range(4)},
        compiler_params=_cp(), name=name)(parts, w, m, v, *extra)


def _split_start(name, arrays, n_sems, plan, after=None):
    n = len(arrays)
    order = [] if after is None else [after]
    n_in = n + len(order)

    def body(*refs):
        ins, send_sems, recv_sems, token = refs[:n], refs[n_in], refs[n_in + 1], refs[-1]
        for src, dst, k, to in plan(ins)[0]:
            pltpu.make_async_remote_copy(src_ref=src, dst_ref=dst, send_sem=send_sems.at[k], recv_sem=recv_sems.at[k],
                                         device_id=to, device_id_type=MESH_ID).start()
        token[...] = jnp.zeros_like(token)

    outs = pl.pallas_call(
        body, name=name,
        out_shape=(pltpu.SemaphoreType.DMA((n_sems,)), pltpu.SemaphoreType.DMA((n_sems,)),
                   *[pltpu.HBM(a.shape, a.dtype) for a in arrays], jax.ShapeDtypeStruct((8, 128), F32)),
        in_specs=[HBM] * n + [ANY] * len(order),
        out_specs=(SEM, SEM, *[HBM] * n, pl.BlockSpec(memory_space=pltpu.VMEM)),
        input_output_aliases={i: 2 + i for i in range(n)},
        compiler_params=pltpu.CompilerParams(has_side_effects=EFFECT),
    )(*[pltpu.with_memory_space_constraint(a, pltpu.HBM) for a in arrays], *order)
    return outs[0], outs[1], list(outs[2:2 + n]), outs[-1]


def _split_wait(name, arrays, send_sems, recv_sems, after, plan):
    n = len(arrays)
    order = list(after) if isinstance(after, (list, tuple)) else [after]

    def body(*refs):
        ins, s_sems, r_sems = refs[:n], refs[n], refs[n + 1]
        sends, arrivals = plan(ins)
        x, y, c = lax.axis_index("x"), lax.axis_index("y"), lax.axis_index("c")
        for src, dst, k, to in sends:
            pltpu.make_async_remote_copy(src_ref=src, dst_ref=dst, send_sem=s_sems.at[k], recv_sem=r_sems.at[k],
                                         device_id=to, device_id_type=MESH_ID).wait_send()
        for dst, k in arrivals:
            pltpu.make_async_remote_copy(src_ref=dst, dst_ref=dst, send_sem=s_sems.at[k], recv_sem=r_sems.at[k],
                                         device_id=(x, y, c), device_id_type=MESH_ID).wait_recv()

    return pl.pallas_call(
        body, name=name, out_shape=[pltpu.HBM(a.shape, a.dtype) for a in arrays],
        in_specs=[HBM] * n + [SEM, SEM] + [ANY] * len(order), out_specs=[HBM] * n,
        input_output_aliases={i: i for i in range(n)},
        compiler_params=pltpu.CompilerParams(has_side_effects=EFFECT),
    )(*arrays, send_sems, recv_sems, *order)


def _chips():
    x, y, c = lax.axis_index("x"), lax.axis_index("y"), lax.axis_index("c")
    return x, y, c, [(1 - x, y), (x, 1 - y), (1 - x, 1 - y)]


def _plan_gather_chips(refs):
    x, y, c, chips = _chips()
    me = 4 * x + 2 * y + c
    n = len(refs) // 2
    sends, arrivals = [], []
    for i in range(n):
        src, land = refs[i], refs[n + i]
        sends.append((src, land.at[me], 4 * i, (x, y, 1 - c)))
        arrivals.append((land.at[4 * x + 2 * y + 1 - c], 4 * i))
        for j, (px, py) in enumerate(chips):
            sends.append((src, land.at[me], 4 * i + 1 + j, (px, py, c)))
            arrivals.append((land.at[4 * px + 2 * py + c], 4 * i + 1 + j))
    return sends, arrivals


def _plan_gather_pass(refs):
    x, y, c, chips = _chips()
    sends, arrivals = [], []
    for i in range(len(refs)):
        for j, (px, py) in enumerate(chips):
            slot = refs[i].at[4 * px + 2 * py + c]
            sends.append((slot, slot, 4 * i + j, (x, y, 1 - c)))
            arrivals.append((refs[i].at[4 * px + 2 * py + 1 - c], 4 * i + j))
        back = refs[i].at[4 * x + 2 * y + 1 - c]
        sends.append((back, back, 4 * i + 3, (x, y, 1 - c)))
        arrivals.append((refs[i].at[4 * x + 2 * y + c], 4 * i + 3))
    return sends, arrivals


def _plan_scatter_pair(refs):
    x, y, c = lax.axis_index("x"), lax.axis_index("y"), lax.axis_index("c")
    n = len(refs) // 2
    sends, arrivals = [], []
    for i in range(n):
        for q in range(4):
            sends.append((refs[i].at[q, 1 - c], refs[n + i].at[q], 4 * i + q, (x, y, 1 - c)))
            arrivals.append((refs[n + i].at[q], 4 * i + q))
    return sends, arrivals


def _plan_scatter_chips(layer):
    def plan(refs):
        x, y, c, chips = _chips()
        n = len(refs) // 2
        sends, arrivals = [], []
        for i in range(n):
            for j, (px, py) in enumerate(chips):
                sends.append((refs[i].at[2 * px + py], refs[n + i].at[layer, 2 * x + y], 3 * i + j, (px, py, c)))
                arrivals.append((refs[n + i].at[layer, 2 * px + py], 3 * i + j))
        return sends, arrivals

    return plan


def _pair_sum(parts4, from_pair, landing, layer, core, tr, name):
    _, _, rows, cols = parts4.shape

    def body(c_ref, p_ref, s_ref, l_ref, sum_ref, land_ref):
        v = (p_ref[...].astype(F32) + s_ref[...].astype(F32)).astype(BF16)
        sum_ref[...] = v
        land_ref[...] = v

    blk = pl.BlockSpec((None, tr, cols), lambda q, i, c_ref: (q, i, 0))
    return pl.pallas_call(
        body,
        grid_spec=pltpu.PrefetchScalarGridSpec(
            num_scalar_prefetch=1, grid=(4, rows // tr),
            in_specs=[pl.BlockSpec((None, None, tr, cols), lambda q, i, c_ref: (q, c_ref[0], i, 0)), blk, ANY],
            out_specs=[blk, pl.BlockSpec((None, None, tr, cols), lambda q, i, c_ref: (layer, q, i, 0))]),
        out_shape=[jax.ShapeDtypeStruct((4, rows, cols), BF16), jax.ShapeDtypeStruct(landing.shape, BF16)],
        input_output_aliases={3: 1}, compiler_params=_cp(), name=name,
    )(core, parts4, from_pair, landing)


def _travel_layout(t):
    tr = lambda a: jnp.swapaxes(a, 1, 2)
    branch = jnp.concatenate([tr(t["w_attn_o"]), tr(t["w_conv_o"]), tr(t["w_ssm_o"])], axis=2)
    return [tr(t["w_in"]), tr(t["w_ffn_in"]), t["w_ffn_out"], t["w_mix_o"], branch, t["w_ssm_glu"]]


def _native_layout(a):
    tr = lambda x: jnp.swapaxes(x, 1, 2)
    b = a[4]
    return {"w_in": tr(a[0]), "w_ffn_in": tr(a[1]), "w_ffn_out": a[2], "w_mix_o": a[3],
            "w_attn_o": tr(b[:, :, :WIDTH]), "w_conv_o": tr(b[:, :, WIDTH:2 * WIDTH]),
            "w_ssm_o": tr(b[:, :, 2 * WIDTH:]), "w_ssm_glu": a[5]}


def _diag_blocks(t):
    t = t.reshape(DEPTH, N_LANE_GROUPS, 8, SSM_GROUP, 8, SSM_STATE)
    return jnp.einsum("lgahap->lgahp", t).reshape(DEPTH, SSM_GROUPS, SSM_GROUP, SSM_STATE)


def _rope_tabs():
    pos = jnp.arange(SEQ, dtype=F32)
    inv_freq = ROPE_THETA ** (-jnp.arange(0, ROT_DIM, 2, dtype=F32) / ROT_DIM)
    ang = pos[:, None] * inv_freq[None, :]
    cos, sin = jnp.cos(ang), jnp.sin(ang)
    one, zero = jnp.ones((SEQ, HEAD_DIM - ROT_DIM), F32), jnp.zeros((SEQ, HEAD_DIM - ROT_DIM), F32)
    z8 = jnp.zeros((SEQ, 8), F32)
    head = lambda *p: jnp.tile(jnp.concatenate(p, axis=1), (1, 2))
    return head(cos, cos, one), head(-sin, z8, zero), head(z8, sin, zero)


def _ssm_mats(sp):
    lr, li, bbr, bbi = _ssm_prep(sp["a_re"], sp["a_im"], sp["log_dt"], sp["bt_re"], sp["bt_im"])
    lanes = SSM_GROUPS * SSM_STATE
    b_re, b_im, c_re, c_im_neg = _ssm_embed(bbr, bbi, sp["c_re"], sp["c_im"])
    return {
        "a_re": lr.reshape(DEPTH, 1, lanes), "a_im": li.reshape(DEPTH, 1, lanes),
        "b_re": b_re, "b_im": b_im, "c_re": c_re, "c_im_neg": c_im_neg,
    }


def _layer_fwd(x, i, w, rp, mats, tabs, tie, hooks):
    q, kv, cbx, u, glog, cv, h = _rms_mm_in(x, rp["norm_mix"][i], w["win_t"], tabs, rp["conv_w"], i, tie)
    o = _attn_fwd(q, kv, tabs, rp["attn_sinks"][i])
    x_re, x_im, y = _ssm_fwd(u, mats, i, rp["ssm_d"])
    z = _glu_fwd(y, w["wglu"])
    x1 = _mix_fwd(x, o, cv, z, glog, rp["b_gate"], i, w["branch_t"], w["wmix"], hooks["early"](z))
    hooks["pre_ffn"](x1)
    act, up, silu, dsilu, h2 = _rms_mm_ffn(x1, rp["norm_ffn"][i], w["wffn_t"])
    x2 = _ffn_out_fwd(x1, act, w["wout"], hooks["mid"](h2))
    kept = dict(x=x, q=q, kv=kv, cbx=cbx, u=u, glog=glog, h=h, o=o, cv=cv, z=z, y=y,
                x_re=x_re, x_im=x_im, x1=x1, act=act, up=up, silu=silu, dsilu=dsilu, h2=h2)
    return x2, kept


def _layer_bwd(dx2, k, i, w, rp, mats, tabs, tie, hooks):
    dgu = _ffn_out_bwd(dx2, k["up"], k["silu"], k["dsilu"], w["wout"], tie)
    g_wout = _mm_tn(k["act"], dx2, tm=FFN_H // 2, tn=1024, name="mm_tn_ffn_out")
    g_wffn_t = _mm_tn(dgu, k["h2"], tm=FFN_H // 2, tn=1024, name="mm_tn_ffn_in")
    dx1, d_norm_ffn = _mm_rmsbwd([dgu], w["wffn_t"], k["x1"], rp["norm_ffn"][i], dx2, "mm_rmsbwd_ffn")

    mg, dya, dyc, dys, do, dcv, dz, dgl, db_gate = _mix_bwd(
        dx1, k["o"], k["cv"], k["z"], k["glog"], rp["b_gate"], i, w["branch_t"], w["wmix"],
        hooks["mid"]((g_wffn_t, g_wout, d_norm_ffn)))
    g_wmix = _mm_tn(mg, dx1, tm=1024, tn=512, name="mm_tn_mix")
    g_branch_t = _tn_branches((dya, dyc, dys), (k["o"], k["cv"], k["z"]))

    dy, ys16, da16, dd = _glu_bwd(k["y"], w["wglu"], dz, k["u"])
    g_wglu = _mm_tn(ys16, da16, tm=256, tn=512, name="mm_tn_glu")
    du, da_re, da_im, db_re, db_im, dc_re, dc_im = _ssm_bwd(dy, k["x_re"], k["x_im"], k["u"], mats, i, rp["ssm_d"])

    dcb, dcc, dcx, d_conv_w = _conv_bwd(k["cbx"], rp["conv_w"], i, dcv, hooks["late"](du))
    dq, dkv, d_sinks = _attn_bwd(k["q"], k["kv"], tabs, rp["attn_sinks"][i], do)

    pieces = [dq, dkv, dcb, dcc, dcx, du, dgl]
    g_win_t = _tn_pieces(pieces, k["h"])
    dx, d_norm_mix = _mm_rmsbwd(pieces, w["win_t"], k["x"], rp["norm_mix"][i], dx1, "mm_rmsbwd_in")

    grads = [g_win_t, g_wffn_t, g_wout, g_wmix, g_branch_t, g_wglu]
    small = dict(norm_mix=d_norm_mix, b_gate=db_gate, attn_sinks=d_sinks, ssm_d=dd, norm_ffn=d_norm_ffn,
                 conv_w=d_conv_w, da_re=da_re, da_im=da_im, db_re=db_re, db_im=db_im, dc_re=dc_re, dc_im=dc_im)
    return dx, grads, small


def _replicated_grads(sg, sp):
    stack = lambda name: jnp.stack([sg[i][name] for i in range(DEPTH)])
    cots = (stack("da_re").reshape(DEPTH, *_GS), stack("da_im").reshape(DEPTH, *_GS),
            _diag_blocks(stack("db_re")), _diag_blocks(stack("db_im")))
    d_a_re, d_a_im, d_log_dt, d_bt_re, d_bt_im = _ssm_prep_bwd(
        sp["a_re"], sp["a_im"], sp["log_dt"], sp["bt_re"], sp["bt_im"], cots)
    sgrads = {"norm_mix": stack("norm_mix"), "b_gate": stack("b_gate"),
              "attn_sinks": stack("attn_sinks")[:, :, :N_Q_HEADS], "ssm_a_re": d_a_re, "ssm_a_im": d_a_im,
              "ssm_b_re": jnp.swapaxes(d_bt_re, 2, 3), "ssm_b_im": jnp.swapaxes(d_bt_im, 2, 3),
              "ssm_c_re": _diag_blocks(stack("dc_re")), "ssm_c_im": -_diag_blocks(stack("dc_im")),
              "ssm_d": stack("ssm_d"), "ssm_log_dt": d_log_dt, "norm_ffn": stack("norm_ffn")}
    return sgrads, stack("conv_w")[:, :3]


def kernel(x, norm_mix, w_in, b_gate, attn_sinks, w_attn_o, conv_w, w_conv_o, ssm_a_re, ssm_a_im, ssm_b_re, ssm_b_im, ssm_c_re, ssm_c_im, ssm_d, ssm_log_dt, w_ssm_glu, w_ssm_o, w_mix_o, norm_ffn, w_ffn_in, w_ffn_out, norm_final, loss_target, m_norm_mix, m_w_in, m_b_gate, m_attn_sinks, m_w_attn_o, m_conv_w, m_w_conv_o, m_ssm_a_re, m_ssm_a_im, m_ssm_b_re, m_ssm_b_im, m_ssm_c_re, m_ssm_c_im, m_ssm_d, m_ssm_log_dt, m_w_ssm_glu, m_w_ssm_o, m_w_mix_o, m_norm_ffn, m_w_ffn_in, m_w_ffn_out, m_norm_final, v_norm_mix, v_w_in, v_b_gate, v_attn_sinks, v_w_attn_o, v_conv_w, v_w_conv_o, v_ssm_a_re, v_ssm_a_im, v_ssm_b_re, v_ssm_b_im, v_ssm_c_re, v_ssm_c_im, v_ssm_d, v_ssm_log_dt, v_w_ssm_glu, v_w_ssm_o, v_w_mix_o, v_norm_ffn, v_w_ffn_in, v_w_ffn_out, v_norm_final):
    big = {"w": dict(w_in=w_in, w_attn_o=w_attn_o, w_conv_o=w_conv_o, w_ssm_glu=w_ssm_glu, w_ssm_o=w_ssm_o,
                     w_mix_o=w_mix_o, w_ffn_in=w_ffn_in, w_ffn_out=w_ffn_out),
           "m": dict(w_in=m_w_in, w_attn_o=m_w_attn_o, w_conv_o=m_w_conv_o, w_ssm_glu=m_w_ssm_glu,
                     w_ssm_o=m_w_ssm_o, w_mix_o=m_w_mix_o, w_ffn_in=m_w_ffn_in, w_ffn_out=m_w_ffn_out),
           "v": dict(w_in=v_w_in, w_attn_o=v_w_attn_o, w_conv_o=v_w_conv_o, w_ssm_glu=v_w_ssm_glu,
                     w_ssm_o=v_w_ssm_o, w_mix_o=v_w_mix_o, w_ffn_in=v_w_ffn_in, w_ffn_out=v_w_ffn_out)}
    small = {"w": dict(norm_mix=norm_mix, b_gate=b_gate, attn_sinks=attn_sinks, ssm_a_re=ssm_a_re,
                       ssm_a_im=ssm_a_im, ssm_b_re=ssm_b_re, ssm_b_im=ssm_b_im, ssm_c_re=ssm_c_re,
                       ssm_c_im=ssm_c_im, ssm_d=ssm_d, ssm_log_dt=ssm_log_dt, norm_ffn=norm_ffn),
             "m": dict(norm_mix=m_norm_mix, b_gate=m_b_gate, attn_sinks=m_attn_sinks, ssm_a_re=m_ssm_a_re,
                       ssm_a_im=m_ssm_a_im, ssm_b_re=m_ssm_b_re, ssm_b_im=m_ssm_b_im, ssm_c_re=m_ssm_c_re,
                       ssm_c_im=m_ssm_c_im, ssm_d=m_ssm_d, ssm_log_dt=m_ssm_log_dt, norm_ffn=m_norm_ffn),
             "v": dict(norm_mix=v_norm_mix, b_gate=v_b_gate, attn_sinks=v_attn_sinks, ssm_a_re=v_ssm_a_re,
                       ssm_a_im=v_ssm_a_im, ssm_b_re=v_ssm_b_re, ssm_b_im=v_ssm_b_im, ssm_c_re=v_ssm_c_re,
                       ssm_c_im=v_ssm_c_im, ssm_d=v_ssm_d, ssm_log_dt=v_ssm_log_dt, norm_ffn=v_norm_ffn)}
    finals = {"w": norm_final, "m": m_norm_final, "v": v_norm_final}
    convs = {"w": conv_w, "m": m_conv_w, "v": v_conv_w}
    small_out_shapes = {name: a.shape for name, a in small["w"].items()}
    small_out_shapes.update(norm_final=(D_MODEL,), conv_w=(DEPTH, 3, 64))
    small_shapes = dict(small_out_shapes, norm_final=(1, D_MODEL), conv_w=(DEPTH, 3, WIDTH))
    dense = ("ssm_b_re", "ssm_b_im", "ssm_c_re", "ssm_c_im")
    for name in dense:
        small_shapes[name] = (DEPTH, SSM_GROUPS, SSM_GROUP * SSM_STATE)
    small_wmv = {name: [(convs[s] if name == "conv_w" else finals[s] if name == "norm_final" else small[s][name])
                        .reshape((DEPTH, 3, 64) if name == "conv_w" else small_shapes[name]) for s in "wmv"]
                 for name in small_shapes}
    mine = 4 * lax.axis_index("x") + 2 * lax.axis_index("y") + lax.axis_index("c")

    travel = {s: _travel_layout(big[s]) for s in "wmv"}
    stacked16 = list(zip(*[[a[0] for a in _travel_layout({n: w[i:i + 1].astype(BF16) for n, w in big["w"].items()})]
                           for i in range(DEPTH)]))
    rp = {"norm_mix": norm_mix[:, None], "norm_ffn": norm_ffn[:, None], "attn_sinks": attn_sinks[:, None],
          "b_gate": b_gate[:, None], "ssm_d": ssm_d[:, None]}
    sp = {"a_re": ssm_a_re, "a_im": ssm_a_im, "log_dt": ssm_log_dt[:, :, None],
          "bt_re": jnp.swapaxes(ssm_b_re, 2, 3), "bt_im": jnp.swapaxes(ssm_b_im, 2, 3),
          "c_re": ssm_c_re, "c_im": ssm_c_im}
    rows_tile = {"win_t": 368, "wffn_t": 352, "wout": 352, "wmix": 128, "branch_t": 128, "wglu": 64}
    core = lax.axis_index("c").astype(jnp.int32).reshape(1)
    no_tie = jnp.zeros((8, 128), F32)

    def landing_zones(srcs):
        return [lax.empty((N_DEV,) + s.shape, s.dtype) for s in srcs]

    def gather_chips(tag, i, kinds, after, extra=()):
        srcs = [stacked16[j][i] for j in kinds] + list(extra)
        s_sems, r_sems, arrays, token = _split_start(
            f"gather_chips_start_{tag}", srcs + landing_zones(srcs), 4 * len(srcs), _plan_gather_chips, after)
        return (tag, s_sems, r_sems, arrays), token

    def gather_pass(state, after):
        tag, s_sems, r_sems, arrays = state
        arrays = _split_wait(f"gather_chips_wait_{tag}", arrays, s_sems, r_sems, after, _plan_gather_chips)
        n = len(arrays) // 2
        s_sems, r_sems, lands, token = _split_start(
            f"gather_pass_start_{tag}", list(arrays[n:]), 4 * n, _plan_gather_pass)
        return (tag, s_sems, r_sems, lands), token

    def gather_done(state, after, kinds):
        tag, s_sems, r_sems, lands = state
        lands = _split_wait(f"gather_pass_wait_{tag}", lands, s_sems, r_sems, after, _plan_gather_pass)
        named = {KINDS[j][0]: a.reshape(N_DEV * KINDS[j][1], KINDS[j][2]) for a, j in zip(lands, kinds)}
        return named, list(lands[len(kinds):])

    all_kinds, mixer_kinds, ffn_kinds = tuple(range(len(KINDS))), (0, 3, 4, 5), (1, 2)
    no_hooks = {name: (lambda value: no_tie) for name in ("early", "pre_ffn", "mid", "late")}
    state, token = gather_chips("0m", 0, mixer_kinds, None, extra=[jnp.pad(conv_w.reshape(6, 128), ((0, 2), (0, 0)))])
    mats = _ssm_mats(dict(sp, log_dt=sp["log_dt"] + token[0, 0]))
    tabs = _rope_tabs()
    early_work = list(mats.values()) + list(tabs) + [a for name in dense for a in small_wmv[name]]
    early_work += [stacked16[j][0] for j in ffn_kinds] + [stacked16[j][1] for j in mixer_kinds]
    state, _ = gather_pass(state, early_work)
    ffn_state, tie = gather_chips("0f", 0, ffn_kinds, state[3][0])
    w_next, (conv_all,) = gather_done(state, tabs[2], mixer_kinds)
    conv_full = conv_all[:, :6].reshape(N_DEV, DEPTH, 3, 64).transpose(1, 2, 0, 3).reshape(DEPTH, 3, WIDTH)
    rp["conv_w"] = jnp.pad(conv_full, ((0, 0), (0, 5), (0, 0)))

    act = x[0]
    weights, kept = [], []
    for i in range(DEPTH):
        w_i, hooks, held = w_next, dict(no_hooks), {}

        def early(value, ffn_state=ffn_state, held=held):
            held["ffn"], token = gather_pass(ffn_state, value)
            return token

        def pre_ffn(value, w_i=w_i, held=held):
            w_i.update(gather_done(held["ffn"], value, ffn_kinds)[0])

        hooks.update(early=early, pre_ffn=pre_ffn)
        if i + 1 < DEPTH:
            state, tie = gather_chips(f"{i + 1}m", i + 1, mixer_kinds, tie if i == 0 else w_i["win_t"])

            def mid(value, i=i, state=state, held=held):
                held["next"], token = gather_pass(state, value)
                held["next_ffn"], token = gather_chips(f"{i + 1}f", i + 1, ffn_kinds, token)
                return token

            hooks.update(mid=mid)
        act, k = _layer_fwd(act, i, w_i, rp, mats, tabs, tie, hooks)
        if i + 1 < DEPTH:
            w_next, _ = gather_done(held["next"], act, mixer_kinds)
            ffn_state, tie = held["next_ffn"], no_tie
        weights.append(w_i)
        kept.append(k)
    loss_row, dx, d_norm_final = _loss_head(act, norm_final[None], loss_target[0])

    landings = [lax.empty((DEPTH, 4, r, c), BF16) for _, r, c in KINDS]
    landings0 = [lax.empty((1, 4, r, c), BF16) for _, r, c in KINDS]

    def scatter_pair(tag, kinds, grads, after):
        parts4 = [g.reshape(4, 2, KINDS[j][1], KINDS[j][2]) for g, j in zip(grads, kinds)]
        zones = [lax.empty((4, KINDS[j][1], KINDS[j][2]), BF16) for j in kinds]
        s_sems, r_sems, arrays, token = _split_start(
            f"scatter_pair_start_{tag}", parts4 + zones, 4 * len(kinds), _plan_scatter_pair, after)
        return (tag, kinds, s_sems, r_sems, arrays), token

    def scatter_chips(state, lands, slot, after):
        tag, kinds, s_sems, r_sems, arrays = state
        arrays = _split_wait(f"scatter_pair_wait_{tag}", arrays, s_sems, r_sems, after, _plan_scatter_pair)
        n = len(kinds)
        sums, mine_lands = [], []
        for k, j in enumerate(kinds):
            name = KINDS[j][0]
            chip_sum, land = _pair_sum(arrays[k], arrays[n + k], lands[j], slot, core, KINDS[j][1],
                                       f"pair_sum_{name}")
            sums.append(chip_sum)
            mine_lands.append(land)
        s_sems, r_sems, arrays, token = _split_start(
            f"scatter_chips_start_{tag}", sums + mine_lands, 3 * n, _plan_scatter_chips(slot))
        return (tag, kinds, slot, s_sems, r_sems, arrays), token

    def scatter_done(state, lands, after):
        tag, kinds, slot, s_sems, r_sems, arrays = state
        arrays = _split_wait(f"scatter_chips_wait_{tag}", arrays, s_sems, r_sems, after, _plan_scatter_chips(slot))
        lands = list(lands)
        for k, j in enumerate(kinds):
            lands[j] = arrays[len(kinds) + k]
        return lands

    sg = [None] * DEPTH
    pending, tie = None, no_tie
    for i in reversed(range(DEPTH)):
        hooks, held = dict(no_hooks), {}
        if pending is not None:
            def mid(value, i=i, pending=pending, held=held):
                held["chips"], token = scatter_chips(pending, landings, i + 1, value[2])
                if i == 0:
                    held["ffn_pair"], token = scatter_pair("0f", ffn_kinds, value[:2], token)
                return token

            hooks.update(mid=mid)
        if i == 0:
            def late(value, held=held):
                held["ffn_chips"], token = scatter_chips(held["ffn_pair"], landings0, 0, value)
                return token

            hooks.update(late=late)
        dx, grads, sg[i] = _layer_bwd(dx, kept[i], i, weights[i], rp, mats, tabs, tie, hooks)
        if pending is not None:
            landings = scatter_done(held["chips"], landings, dx)
        if i > 0:
            pending, tie = scatter_pair(str(i), all_kinds, grads, dx)
        else:
            pending, _ = scatter_pair("0m", mixer_kinds, [grads[j] for j in mixer_kinds], dx)

    sgrads, conv_grad = _replicated_grads(sg, sp)

    small_names = list(REPLICATED) + ["norm_final", "conv_w"]
    sgrads.update(norm_final=d_norm_final, conv_w=conv_grad)
    small_src = [sgrads[name].reshape(small_shapes[name]).astype(BF16) for name in small_names]
    small_src.append(jnp.broadcast_to(loss_row[:, :1], (8, 128)))
    last, tie = scatter_chips(pending, landings0, 0, small_src[0])
    s_sems, r_sems, arrays, tie = _split_start(
        "gather_small_chips_start", small_src + landing_zones(small_src), 4 * len(small_src), _plan_gather_chips, tie)
    small_state = ("small", s_sems, r_sems, arrays)

    big_out = []
    for j, (name, _, _) in enumerate(KINDS):
        big_out.append(_adamw(landings[j], travel["w"][j], travel["m"][j], travel["v"][j], rows_tile[name],
                              "adamw_late_" + name, groups=(1, DEPTH), tie=tie))
        tie = big_out[-1][3]
    landings0 = scatter_done(held["ffn_chips"], landings0, tie)
    landings0 = scatter_done(last, landings0, tie)
    small_state, _ = gather_pass(small_state, landings0[0])
    big_out = [_adamw(landings0[j], travel["w"][j], travel["m"][j], travel["v"][j], rows_tile[name],
                      "adamw_first_" + name, groups=(0, 1), fill=big_out[j]) for j, (name, _, _) in enumerate(KINDS)]
    big_res = [_native_layout([big_out[j][kind] for j in range(len(KINDS))]) for kind in range(4)]

    _, sparts = gather_done(small_state, big_out[-1][0], ())
    loss = jnp.sum(sparts[-1][:, 0, 0])
    sparts = dict(zip(small_names, sparts))
    sparts["conv_w"] = lax.dynamic_slice_in_dim(sparts["conv_w"], mine * 64, 64, axis=3)
    small_res = {}
    for name in small_names:
        res = _adamw_small(sparts[name], *small_wmv[name], "adamw_" + name)
        small_res[name] = [r.reshape(small_out_shapes[name]) for r in res]

    order = ["norm_mix", "w_in", "b_gate", "attn_sinks", "w_attn_o", "conv_w", "w_conv_o", "ssm_a_re", "ssm_a_im",
             "ssm_b_re", "ssm_b_im", "ssm_c_re", "ssm_c_im", "ssm_d", "ssm_log_dt", "w_ssm_glu", "w_ssm_o",
             "w_mix_o", "norm_ffn", "w_ffn_in", "w_ffn_out", "norm_final"]
    outs = [loss, dx[None]]
    for kind in range(4):
        for name in order:
            outs.append(big_res[kind][name] if name in big_res[kind] else small_res[name][kind])
    return tuple(outs)
```

```python
import math

import jax
import jax.numpy as jnp
from jax import lax
from jax.experimental import pallas as pl
from jax.experimental.pallas import tpu as pltpu

F32 = jnp.float32
BF16 = jnp.bfloat16

N_DEV = 8
DEPTH = 4
SEQ = 2048
D_MODEL = 1024
N_Q_HEADS = 8
HEAD_DIM = 64
ATTN_W = 512
KV_W = 128
BLOCK = 128
N_BLOCKS = SEQ // BLOCK
ROPE_THETA = 500000.0
ROT_DIM = 16
NEG_INF = -1e30
WIDTH = 512
SSM_GROUPS = 32
SSM_GROUP = 16
SSM_STATE = 64
CHUNK = 256
N_CHUNKS = SEQ // CHUNK
GATE_W = 3 * D_MODEL
IN_COLS = 5888
FFN_H = 2816
NORM_EPS = 1e-6
LR, B1, B2, ADAM_EPS, WD, STEP = 0.001, 0.9, 0.999, 1e-08, 0.01, 10

COL_Q, COL_KV, COL_CBX, COL_U, COL_G = 0, 512, 768, 2304, 2816
PIECE_W = (512, 256, 512, 512, 512, 512, 3072)
PIECE_OFF = tuple(sum(PIECE_W[:i]) for i in range(len(PIECE_W)))

KINDS = (("win_t", 736, 1024), ("wffn_t", 704, 1024), ("wout", 352, 1024), ("wmix", 128, 1024),
         ("branch_t", 128, 1536), ("wglu", 64, 512))

REPLICATED = ("norm_mix", "b_gate", "attn_sinks", "ssm_a_re", "ssm_a_im", "ssm_b_re", "ssm_b_im", "ssm_c_re",
              "ssm_c_im", "ssm_d", "ssm_log_dt", "norm_ffn")

VMEM_LIMIT = 56 * 1024 * 1024
NT = (((1,), (1,)), ((), ()))
TN = (((0,), (0,)), ((), ()))
MESH_ID = pl.DeviceIdType.MESH
ANY = pl.BlockSpec(memory_space=pl.ANY)
HBM = pl.BlockSpec(memory_space=pltpu.HBM)
SEM = pl.BlockSpec(memory_space=pltpu.SEMAPHORE)
EFFECT = pltpu.SideEffectType.DATAFLOW_SIDE_EFFECTING


def _cp(**kw):
    return pltpu.CompilerParams(vmem_limit_bytes=VMEM_LIMIT, **kw)


def _full(shape):
    return pl.BlockSpec(shape, lambda *_: (0,) * len(shape))


def _resident(shape):
    return pl.BlockSpec(shape, lambda *_: (0,) * len(shape), pipeline_mode=pl.Buffered(1))


def _mm_tn(a, b, *, tm, tn, name):
    k, m = a.shape
    n = b.shape[1]

    def body(a_ref, b_ref, o_ref):
        o_ref[...] = lax.dot_general(a_ref[...].astype(BF16), b_ref[...].astype(BF16), TN,
                                     preferred_element_type=F32).astype(BF16)

    return pl.pallas_call(
        body, grid=(m // tm, n // tn),
        in_specs=[pl.BlockSpec((k, tm), lambda i, j: (0, i)), pl.BlockSpec((k, tn), lambda i, j: (0, j))],
        out_specs=pl.BlockSpec((tm, tn), lambda i, j: (i, j)),
        out_shape=jax.ShapeDtypeStruct((m, n), BF16), compiler_params=_cp(), name=name)(a, b)


def _rms_rows(xv, g):
    r = lax.rsqrt(jnp.mean(xv * xv, axis=-1, keepdims=True) + NORM_EPS)
    return ((xv * r) * g).astype(BF16)


def _rms_mm_in(x, g, wt, tabs, cw, layer, tie):
    tt = 512
    widths = (3 * WIDTH, WIDTH, GATE_W)
    offs = (COL_CBX, COL_U, COL_G)

    def body(x_ref, g_ref, w_ref, tc_ref, ta_ref, tb_ref, cw_ref, tie_ref,
             q_ref, kv_ref, cbx_ref, u_ref, gl_ref, cv_ref, h_ref, tail_ref):
        @pl.when(pl.program_id(0) == 0)
        def _():
            tail_ref[...] = jnp.zeros_like(tail_ref)

        h = _rms_rows(x_ref[...], g_ref[...])
        h_ref[...] = h
        prod = lax.dot_general(h, w_ref[...], NT, preferred_element_type=F32)
        for ref, o, w in zip((cbx_ref, u_ref, gl_ref), offs, widths):
            ref[...] = prod[:, o:o + w]
        c, a, b = tc_ref[...], ta_ref[...], tb_ref[...]
        for j in range(ATTN_W // 128):
            q_ref[:, 128 * j:128 * (j + 1)] = _rope(prod[:, 128 * j:128 * (j + 1)], c, a, b) * (HEAD_DIM ** -0.5)
        kv_ref[:, :KV_W] = _rope(prod[:, COL_KV:COL_KV + KV_W], c, a, b)
        kv_ref[:, KV_W:] = prod[:, COL_KV + KV_W:COL_CBX]

        row = lax.broadcasted_iota(jnp.int32, (tt, 128), 0)
        for j in range(WIDTH // 128):
            cols = slice(128 * j, 128 * (j + 1))
            cb = prod[:, COL_CBX + 128 * j:COL_CBX + 128 * (j + 1)]
            z = prod[:, COL_CBX + WIDTH + 128 * j:COL_CBX + WIDTH + 128 * (j + 1)] \
                * prod[:, COL_CBX + 2 * WIDTH + 128 * j:COL_CBX + 2 * WIDTH + 128 * (j + 1)]
            before1, before2 = tail_ref[7:8, cols], tail_ref[6:7, cols]
            z1 = jnp.where(row == 0, before1, pltpu.roll(z, 1, axis=0))
            z2 = jnp.where(row == 0, before2, jnp.where(row == 1, before1, pltpu.roll(z, 2, axis=0)))
            s = cw_ref[0:1, cols] * z2 + cw_ref[1:2, cols] * z1 + cw_ref[2:3, cols] * z
            cv_ref[:, cols] = (cb * s).astype(BF16)
            tail_ref[:, cols] = z[tt - 8:, :]

    row_spec = lambda w: pl.BlockSpec((tt, w), lambda i: (i, 0))
    sds = jax.ShapeDtypeStruct
    return pl.pallas_call(
        body, grid=(SEQ // tt,),
        in_specs=[row_spec(D_MODEL), _full((1, D_MODEL)), _resident((IN_COLS, D_MODEL)),
                  row_spec(128), row_spec(128), row_spec(128),
                  pl.BlockSpec((None, 8, WIDTH), lambda i: (layer, 0, 0)), ANY],
        out_specs=[row_spec(ATTN_W), row_spec(2 * KV_W), row_spec(3 * WIDTH), row_spec(WIDTH), row_spec(GATE_W),
                   row_spec(WIDTH), row_spec(D_MODEL)],
        out_shape=[sds((SEQ, ATTN_W), F32), sds((SEQ, 2 * KV_W), F32), sds((SEQ, 3 * WIDTH), F32),
                   sds((SEQ, WIDTH), F32), sds((SEQ, GATE_W), F32), sds((SEQ, WIDTH), BF16),
                   sds((SEQ, D_MODEL), BF16)],
        scratch_shapes=[pltpu.VMEM((8, WIDTH), F32)], compiler_params=_cp(), name="rms_mm_in",
    )(x, g, wt, *tabs, cw, tie)


def _rms_mm_ffn(x, g, wt):
    tt = 256

    def body(x_ref, g_ref, w_ref, act_ref, up_ref, silu_ref, dsilu_ref, h_ref):
        h = _rms_rows(x_ref[...], g_ref[...])
        h_ref[...] = h
        prod = lax.dot_general(h, w_ref[...], NT, preferred_element_type=F32)
        gt, up = prod[:, :FFN_H], prod[:, FFN_H:]
        sg = jax.nn.sigmoid(gt)
        silu = gt * sg
        act_ref[...] = (silu * up).astype(BF16)
        up_ref[...] = up.astype(BF16)
        silu_ref[...] = silu.astype(BF16)
        dsilu_ref[...] = (sg + silu * (1.0 - sg)).astype(BF16)

    row = lambda w: pl.BlockSpec((tt, w), lambda i: (i, 0))
    return pl.pallas_call(
        body, grid=(SEQ // tt,), in_specs=[row(D_MODEL), _full((1, D_MODEL)), _resident((2 * FFN_H, D_MODEL))],
        out_specs=[row(FFN_H)] * 4 + [row(D_MODEL)],
        out_shape=[jax.ShapeDtypeStruct((SEQ, FFN_H), BF16)] * 4 + [jax.ShapeDtypeStruct((SEQ, D_MODEL), BF16)],
        compiler_params=_cp(), name="rms_mm_ffn")(x, g, wt)


def _mm_rmsbwd(pieces, wt, x, g, dres, name):
    tt = 512
    widths = [p.shape[1] for p in pieces]
    offs = [sum(widths[:i]) for i in range(len(widths))]
    n = len(pieces)

    def body(*refs):
        p_refs, (w_ref, x_ref, g_ref, r_ref, dx_ref, dg_ref) = refs[:n], refs[n:]

        @pl.when(pl.program_id(0) == 0)
        def _():
            dg_ref[...] = jnp.zeros_like(dg_ref)

        dh = jnp.zeros((tt, D_MODEL), F32)
        for p_ref, o, w in zip(p_refs, offs, widths):
            dh += jnp.dot(p_ref[...], w_ref[o:o + w, :], preferred_element_type=F32)
        xv = x_ref[...]
        r = lax.rsqrt(jnp.mean(xv * xv, axis=-1, keepdims=True) + NORM_EPS)
        xh = xv * r
        gy = dh * g_ref[...]
        dx_ref[...] = r_ref[...] + r * (gy - xh * jnp.mean(gy * xh, axis=-1, keepdims=True))
        dg_ref[...] += jnp.sum(dh * xh, axis=0, keepdims=True)

    row = lambda w: pl.BlockSpec((tt, w), lambda i: (i, 0))
    return pl.pallas_call(
        body, grid=(SEQ // tt,),
        in_specs=[row(w) for w in widths] + [_resident(wt.shape), row(D_MODEL), _full((1, D_MODEL)), row(D_MODEL)],
        out_specs=[row(D_MODEL), _full((1, D_MODEL))],
        out_shape=[jax.ShapeDtypeStruct((SEQ, D_MODEL), F32), jax.ShapeDtypeStruct((1, D_MODEL), F32)],
        compiler_params=_cp(), name=name)(*pieces, wt, x, g, dres)


def _tn_pieces(pieces, h):
    tk, tn = 512, 512
    nk = SEQ // tk
    n = len(pieces)

    def body(*refs):
        p_refs, (h_ref, o_ref, acc_ref) = refs[:n], refs[n:]
        kk = pl.program_id(1)

        @pl.when(kk == 0)
        def _():
            acc_ref[...] = jnp.zeros_like(acc_ref)

        hv = h_ref[...]
        for p_ref, o, w in zip(p_refs, PIECE_OFF, PIECE_W):
            acc_ref[o:o + w, :] += lax.dot_general(p_ref[...], hv, TN, preferred_element_type=F32)

        @pl.when(kk == nk - 1)
        def _():
            o_ref[...] = acc_ref[...].astype(BF16)

    return pl.pallas_call(
        body, grid=(D_MODEL // tn, nk),
        in_specs=[pl.BlockSpec((tk, w), lambda j, kk: (kk, 0)) for w in PIECE_W]
        + [pl.BlockSpec((tk, tn), lambda j, kk: (kk, j))],
        out_specs=pl.BlockSpec((IN_COLS, tn), lambda j, kk: (0, j)),
        out_shape=jax.ShapeDtypeStruct((IN_COLS, D_MODEL), BF16),
        scratch_shapes=[pltpu.VMEM((IN_COLS, tn), F32)], compiler_params=_cp(), name="tn_pieces")(*pieces, h)


def _tn_branches(dys, acts):
    tk = 512
    nk = SEQ // tk

    def body(d0, d1, d2, a0, a1, a2, o_ref, acc_ref):
        kk = pl.program_id(0)

        @pl.when(kk == 0)
        def _():
            acc_ref[...] = jnp.zeros_like(acc_ref)

        for j, (d, a) in enumerate(((d0, a0), (d1, a1), (d2, a2))):
            acc_ref[:, WIDTH * j:WIDTH * (j + 1)] += lax.dot_general(d[...], a[...], TN, preferred_element_type=F32)

        @pl.when(kk == nk - 1)
        def _():
            o_ref[...] = acc_ref[...].astype(BF16)

    row = lambda w: pl.BlockSpec((tk, w), lambda kk: (kk, 0))
    return pl.pallas_call(
        body, grid=(nk,), in_specs=[row(D_MODEL)] * 3 + [row(WIDTH)] * 3,
        out_specs=_full((D_MODEL, 3 * WIDTH)), out_shape=jax.ShapeDtypeStruct((D_MODEL, 3 * WIDTH), BF16),
        scratch_shapes=[pltpu.VMEM((D_MODEL, 3 * WIDTH), F32)], compiler_params=_cp(), name="tn_branches",
    )(*dys, *acts)


def _rope(t, c, a, b):
    return t * c + pltpu.roll(t, 120, axis=1) * a + pltpu.roll(t, 8, axis=1) * b


def _rope_t(d, c, a, b):
    return d * c + pltpu.roll(d * a, 8, axis=1) + pltpu.roll(d * b, 120, axis=1)


def _band_sides(band):
    left = lax.broadcasted_iota(jnp.int32, band.shape, 1) < HEAD_DIM
    h0 = jnp.where(left, band, 0.0)
    h1 = jnp.where(left, 0.0, band)
    r0 = pltpu.roll(h0, HEAD_DIM, axis=1)
    r1 = pltpu.roll(h1, HEAD_DIM, axis=1)
    return ((h0.astype(BF16), r0.astype(BF16)), (r1.astype(BF16), h1.astype(BF16)))


def _attn_mask(i):
    qi = lax.broadcasted_iota(jnp.int32, (2 * BLOCK, 2 * BLOCK), 0) % BLOCK
    kj = lax.broadcasted_iota(jnp.int32, (2 * BLOCK, 2 * BLOCK), 1)
    delta = qi + BLOCK - kj
    return (delta >= 0) & (delta < BLOCK) & ((kj >= BLOCK) | (i > 0))


def _attn_probs(s, ok, sink):
    s = jnp.where(ok, s, NEG_INF)
    m = jnp.maximum(jnp.max(s, axis=-1, keepdims=True), sink)
    p = jnp.exp(s - m)
    es = jnp.exp(sink - m)
    inv = 1.0 / (jnp.sum(p, axis=-1, keepdims=True) + es)
    return p * inv, es * inv


def _kv_group(qs, ks, vs, kh, sink_ref):
    q2 = jnp.concatenate([qs[2 * kh], qs[2 * kh + 1]], axis=0)
    kst = jnp.concatenate([ks[kh][0], ks[kh][1]], axis=0)
    vst = jnp.concatenate([vs[kh][0], vs[kh][1]], axis=0)
    top = lax.broadcasted_iota(jnp.int32, (2 * BLOCK, 1), 0) < BLOCK
    sinks = [jnp.where(top, sink_ref[0, 4 * kh + h], sink_ref[0, 4 * kh + 2 + h]) for h in range(2)]
    return q2, kst, vst, sinks


def _attn_load(q_ref, kvc_ref, kvp_ref, tc_ref, ta_ref, tb_ref, pc_ref, pa_ref, pb_ref):
    c, a, b = tc_ref[...], ta_ref[...], tb_ref[...]
    kband = jnp.concatenate([kvp_ref[:, :KV_W], kvc_ref[:, :KV_W]], axis=0)
    vband = jnp.concatenate([kvp_ref[:, KV_W:], kvc_ref[:, KV_W:]], axis=0)
    qs = [q_ref[:, 128 * j:128 * (j + 1)].astype(BF16) for j in range(4)]
    return qs, _band_sides(kband), _band_sides(vband), (c, a, b)


def _attn_specs(clamp):
    cur = lambda i: (clamp(i), 0)
    prev = lambda i: (jnp.maximum(clamp(i) - 1, 0), 0)
    return [
        pl.BlockSpec((BLOCK, ATTN_W), cur), pl.BlockSpec((BLOCK, 2 * KV_W), cur),
        pl.BlockSpec((BLOCK, 2 * KV_W), prev),
        pl.BlockSpec((BLOCK, 128), cur), pl.BlockSpec((BLOCK, 128), cur), pl.BlockSpec((BLOCK, 128), cur),
        pl.BlockSpec((BLOCK, 128), prev), pl.BlockSpec((BLOCK, 128), prev), pl.BlockSpec((BLOCK, 128), prev),
        pl.BlockSpec(memory_space=pltpu.SMEM),
    ]


def _attn_fwd(q, kv, tabs, sinks):
    tc, ta, tb = tabs

    def body(q_ref, kvc_ref, kvp_ref, tc_ref, ta_ref, tb_ref, pc_ref, pa_ref, pb_ref, sink_ref, o_ref):
        i = pl.program_id(0)
        qs, ks, vs, _ = _attn_load(q_ref, kvc_ref, kvp_ref, tc_ref, ta_ref, tb_ref, pc_ref, pa_ref, pb_ref)
        ok = _attn_mask(i)
        for kh in range(2):
            q2, kst, vst, sinks = _kv_group(qs, ks, vs, kh, sink_ref)
            s = lax.dot_general(q2, kst, NT, preferred_element_type=F32)
            pn = [_attn_probs(s[:, 2 * BLOCK * h:2 * BLOCK * (h + 1)], ok, sinks[h])[0].astype(BF16) for h in range(2)]
            o2 = jnp.dot(jnp.concatenate(pn, axis=1), vst, preferred_element_type=F32).astype(BF16)
            for r in range(2):
                j = 2 * kh + r
                o_ref[:, 128 * j:128 * (j + 1)] = o2[BLOCK * r:BLOCK * (r + 1)]

    return pl.pallas_call(
        body, grid=(N_BLOCKS,), in_specs=_attn_specs(lambda i: i),
        out_specs=pl.BlockSpec((BLOCK, ATTN_W), lambda i: (i, 0)),
        out_shape=jax.ShapeDtypeStruct((SEQ, ATTN_W), BF16), compiler_params=_cp(), name="attn_fwd",
    )(q, kv, kv, tc, ta, tb, tc, ta, tb, sinks)


def _attn_bwd(q, kv, tabs, sinks, do):
    tc, ta, tb = tabs
    last = N_BLOCKS - 1
    clamp = lambda i: jnp.minimum(i, last)

    def place(full, side, kh):
        left = lax.broadcasted_iota(jnp.int32, full.shape, 1) < HEAD_DIM
        valid = jnp.where(left, full, 0.0) if side == 0 else jnp.where(left, 0.0, full)
        return valid if side == kh else pltpu.roll(valid, HEAD_DIM, axis=1)

    def body(q_ref, kvc_ref, kvp_ref, tc_ref, ta_ref, tb_ref, pc_ref, pa_ref, pb_ref, sink_ref, do_ref,
             dq_ref, dkv_ref, ds_ref, carry_ref):
        i = pl.program_id(0)

        @pl.when(i == 0)
        def _():
            ds_ref[...] = jnp.zeros_like(ds_ref)
            carry_ref[...] = jnp.zeros_like(carry_ref)

        @pl.when(i > last)
        def _():
            dkv_ref[...] = carry_ref[...].astype(BF16)

        @pl.when(i <= last)
        def _():
            qs, ks, vs, (c, a, b) = _attn_load(q_ref, kvc_ref, kvp_ref, tc_ref, ta_ref, tb_ref,
                                               pc_ref, pa_ref, pb_ref)
            ok = _attn_mask(i)
            dk = jnp.zeros((2 * BLOCK, 128), F32)
            dv = jnp.zeros((2 * BLOCK, 128), F32)
            dsink = jnp.zeros((1, 128), F32)
            lane = lax.broadcasted_iota(jnp.int32, (1, 128), 1)
            for kh in range(2):
                q2, kst, vst, sinks = _kv_group(qs, ks, vs, kh, sink_ref)
                do2 = jnp.concatenate([do_ref[:, 128 * (2 * kh + r):128 * (2 * kh + r + 1)] for r in range(2)],
                                      axis=0).astype(BF16)
                s = lax.dot_general(q2, kst, NT, preferred_element_type=F32)
                dp = lax.dot_general(do2, vst, NT, preferred_element_type=F32)
                pns, dss = [], []
                for h in range(2):
                    cols = slice(2 * BLOCK * h, 2 * BLOCK * (h + 1))
                    pn, ps = _attn_probs(s[:, cols], ok, sinks[h])
                    dr = jnp.sum(pn * dp[:, cols], axis=-1, keepdims=True)
                    pns.append(pn.astype(BF16))
                    dss.append((pn * (dp[:, cols] - dr)).astype(BF16))
                    for r in range(2):
                        part = -jnp.sum((ps * dr)[BLOCK * r:BLOCK * (r + 1)])
                        dsink += jnp.where(lane == 4 * kh + 2 * r + h, part, 0.0)
                ds2, pn2 = jnp.concatenate(dss, axis=1), jnp.concatenate(pns, axis=1)
                dq2 = jnp.dot(ds2, kst, preferred_element_type=F32) * (HEAD_DIM ** -0.5)
                dk2 = lax.dot_general(ds2, q2, TN, preferred_element_type=F32)
                dv2 = lax.dot_general(pn2, do2, TN, preferred_element_type=F32)
                for h in range(2):
                    dk += place(dk2[2 * BLOCK * h:2 * BLOCK * (h + 1)], h, kh)
                    dv += place(dv2[2 * BLOCK * h:2 * BLOCK * (h + 1)], h, kh)
                for r in range(2):
                    j = 2 * kh + r
                    dq_ref[:, 128 * j:128 * (j + 1)] = _rope_t(dq2[BLOCK * r:BLOCK * (r + 1)], c, a, b).astype(BF16)
            ds_ref[...] += dsink
            dk_prev = _rope_t(dk[:BLOCK], pc_ref[...], pa_ref[...], pb_ref[...])
            dk_cur = _rope_t(dk[BLOCK:], c, a, b)
            prev = jnp.concatenate([dk_prev, dv[:BLOCK]], axis=1)
            dkv_ref[...] = (carry_ref[...] + prev).astype(BF16)
            carry_ref[...] = jnp.concatenate([dk_cur, dv[BLOCK:]], axis=1)

    return pl.pallas_call(
        body, grid=(N_BLOCKS + 1,),
        in_specs=_attn_specs(clamp) + [pl.BlockSpec((BLOCK, ATTN_W), lambda i: (clamp(i), 0))],
        out_specs=[pl.BlockSpec((BLOCK, ATTN_W), lambda i: (clamp(i), 0)),
                   pl.BlockSpec((BLOCK, 2 * KV_W), lambda i: (jnp.maximum(i - 1, 0), 0)),
                   pl.BlockSpec((1, 128), lambda i: (0, 0))],
        out_shape=[jax.ShapeDtypeStruct((SEQ, ATTN_W), BF16), jax.ShapeDtypeStruct((SEQ, 2 * KV_W), BF16),
                   jax.ShapeDtypeStruct((1, 128), F32)],
        scratch_shapes=[pltpu.VMEM((BLOCK, 2 * KV_W), F32)], compiler_params=_cp(), name="attn_bwd",
    )(q, kv, kv, tc, ta, tb, tc, ta, tb, sinks, do)


def _shift_down(z, k):
    row = lax.broadcasted_iota(jnp.int32, z.shape, 0)
    return jnp.where(row < k, 0.0, pltpu.roll(z, k, axis=0))


def _shift_up(z, k):
    n = z.shape[0]
    row = lax.broadcasted_iota(jnp.int32, z.shape, 0)
    return jnp.where(row >= n - k, 0.0, pltpu.roll(z, n - k, axis=0))


def _conv_specs():
    nb = WIDTH // 128
    return [pl.BlockSpec((SEQ, 128), lambda j: (0, j)), pl.BlockSpec((SEQ, 128), lambda j: (0, nb + j)),
            pl.BlockSpec((SEQ, 128), lambda j: (0, 2 * nb + j)), pl.BlockSpec((None, 8, 128), lambda j: (0, 0, j))]


def _conv_bwd(cbx, cw, layer, dout, tie):
    def body(cb_ref, cc_ref, cx_ref, w_ref, do_ref, tie_ref, dcb_ref, dcc_ref, dcx_ref, dw_ref):
        cc, cx = cc_ref[...], cx_ref[...]
        z = cc * cx
        z1, z2 = _shift_down(z, 1), _shift_down(z, 2)
        w0, w1, w2 = w_ref[0:1, :], w_ref[1:2, :], w_ref[2:3, :]
        dout = do_ref[...]
        ds = dout * cb_ref[...]
        dcb_ref[...] = (dout * (w0 * z2 + w1 * z1 + w2 * z)).astype(BF16)
        dz = w2 * ds + w1 * _shift_up(ds, 1) + w0 * _shift_up(ds, 2)
        dcc_ref[...] = (dz * cx).astype(BF16)
        dcx_ref[...] = (dz * cc).astype(BF16)
        rows = [jnp.sum(ds * zz, axis=0, keepdims=True) for zz in (z2, z1, z)]
        dw_ref[...] = jnp.concatenate(rows + [jnp.zeros((5, 128), F32)], axis=0)

    col = lambda j: (0, j)
    specs = _conv_specs()
    specs[3] = pl.BlockSpec((None, 8, 128), lambda j: (layer, 0, j))
    return pl.pallas_call(
        body, grid=(WIDTH // 128,), in_specs=specs + [pl.BlockSpec((SEQ, 128), col), ANY],
        out_specs=[pl.BlockSpec((SEQ, 128), col), pl.BlockSpec((SEQ, 128), col), pl.BlockSpec((SEQ, 128), col),
                   pl.BlockSpec((8, 128), col)],
        out_shape=[jax.ShapeDtypeStruct((SEQ, WIDTH), BF16)] * 3 + [jax.ShapeDtypeStruct((8, WIDTH), F32)],
        compiler_params=_cp(), name="conv_bwd",
    )(cbx, cbx, cbx, cw, dout, tie)


def _ssm_prep_math(a_re, a_im, log_dt, bt_re, bt_im):
    dt = jnp.exp(log_dt)
    er = jnp.exp(a_re * dt)
    lr = er * jnp.cos(a_im * dt)
    li = er * jnp.sin(a_im * dt)
    n2 = a_re * a_re + a_im * a_im
    cr = ((lr - 1.0) * a_re + li * a_im) / n2
    ci = (li * a_re - (lr - 1.0) * a_im) / n2
    cr3, ci3 = cr[:, None, :], ci[:, None, :]
    return lr, li, cr3 * bt_re - ci3 * bt_im, cr3 * bt_im + ci3 * bt_re


_GS = (SSM_GROUPS, SSM_STATE)
_GHS = (SSM_GROUPS, SSM_GROUP, SSM_STATE)


def _layered(shape):
    return pl.BlockSpec((None,) + shape, lambda l: (l,) + (0,) * len(shape))


def _ssm_prep(a_re, a_im, log_dt, bt_re, bt_im):
    def body(ar, ai, ld, br, bi, o0, o1, o2, o3):
        outs = _ssm_prep_math(ar[...], ai[...], ld[...], br[...], bi[...])
        for o, v in zip((o0, o1, o2, o3), outs):
            o[...] = v

    shapes = [_GS, _GS, _GHS, _GHS]
    return pl.pallas_call(
        body, grid=(DEPTH,), in_specs=[_layered(s) for s in (_GS, _GS, (SSM_GROUPS, 1), _GHS, _GHS)],
        out_specs=[_layered(s) for s in shapes],
        out_shape=[jax.ShapeDtypeStruct((DEPTH,) + s, F32) for s in shapes],
        name="ssm_prep")(a_re, a_im, log_dt, bt_re, bt_im)


def _ssm_prep_bwd(a_re, a_im, log_dt, bt_re, bt_im, cots):
    def body(ar, ai, ld, br, bi, c0, c1, c2, c3, o0, o1, o2, o3, o4):
        _, vjp = jax.vjp(_ssm_prep_math, ar[...], ai[...], ld[...], br[...], bi[...])
        for o, v in zip((o0, o1, o2, o3, o4), vjp((c0[...], c1[...], c2[...], c3[...]))):
            o[...] = v

    ins = (_GS, _GS, (SSM_GROUPS, 1), _GHS, _GHS)
    return pl.pallas_call(
        body, grid=(DEPTH,), in_specs=[_layered(s) for s in ins + (_GS, _GS, _GHS, _GHS)],
        out_specs=[_layered(s) for s in ins],
        out_shape=[jax.ShapeDtypeStruct((DEPTH,) + s, F32) for s in ins],
        name="ssm_prep_bwd")(a_re, a_im, log_dt, bt_re, bt_im, *cots)


LANES_G = 512
N_LANE_GROUPS = SSM_GROUPS * SSM_STATE // LANES_G


def _ssm_embed(b_re, b_im, c_re, c_im):
    rows = SSM_GROUPS * SSM_GROUP

    def body(br, bi, cr, ci, o0, o1, o2, o3):
        state = lax.broadcasted_iota(jnp.int32, (SSM_STATE, LANES_G), 0)
        lane = lax.broadcasted_iota(jnp.int32, (SSM_STATE, LANES_G), 1)
        spread = (lane % SSM_STATE == state).astype(BF16)
        r = lax.broadcasted_iota(jnp.int32, (rows, LANES_G), 0)
        c = lax.broadcasted_iota(jnp.int32, (rows, LANES_G), 1)
        own = (r % 128) // SSM_GROUP == c // SSM_STATE
        for ref, o, sign in ((br, o0, 1.0), (bi, o1, 1.0), (cr, o2, 1.0), (ci, o3, -1.0)):
            t = (sign * ref[...]).reshape(rows, SSM_STATE).astype(BF16)
            wide = jnp.dot(t, spread, preferred_element_type=F32)
            o[...] = jnp.where(own, wide, 0.0).astype(BF16).reshape(N_LANE_GROUPS, 128, LANES_G)

    out = (N_LANE_GROUPS, 128, LANES_G)
    return pl.pallas_call(
        body, grid=(DEPTH,), in_specs=[_layered(_GHS)] * 4, out_specs=[_layered(out)] * 4,
        out_shape=[jax.ShapeDtypeStruct((DEPTH,) + out, BF16)] * 4, name="ssm_embed")(b_re, b_im, c_re, c_im)


def _scan_in_place(xr_ref, xi_ref, ar, ai, reverse):
    shape = (N_CHUNKS, xr_ref.shape[1])
    ar, ai = jnp.broadcast_to(ar, shape), jnp.broadcast_to(ai, shape)

    def rows(tau):
        t = (CHUNK - 1 - tau) if reverse else tau
        return pl.ds(pl.multiple_of(t * N_CHUNKS, N_CHUNKS), N_CHUNKS)

    def step(tau, carry):
        sr, si = carry
        return ar * sr - ai * si + xr_ref[rows(tau), :], ar * si + ai * sr + xi_ref[rows(tau), :]

    zero = jnp.zeros(shape, F32)
    er, ei = lax.fori_loop(0, CHUNK, step, (zero, zero), unroll=8)
    qr, qi = ar, ai
    for _ in range(8):
        qr, qi = qr * qr - qi * qi, 2.0 * qr * qi
    shift = _shift_up if reverse else _shift_down
    for k in (1, 2, 4):
        sr, si = shift(er, k), shift(ei, k)
        er, ei = er + qr * sr - qi * si, ei + qr * si + qi * sr
        qr, qi = qr * qr - qi * qi, 2.0 * qr * qi
    start = (shift(er, 1), shift(ei, 1))

    def write(tau, carry):
        sr, si = step(tau, carry)
        xr_ref[rows(tau), :] = sr
        xi_ref[rows(tau), :] = si
        return sr, si

    return write, start


def _ssm_specs(layer):
    col = lambda w: pl.BlockSpec((SEQ, w), lambda g: (0, g))
    diag = pl.BlockSpec((None, None, 128, LANES_G), lambda g: (layer, g, 0, 0))
    vec = pl.BlockSpec((None, 1, LANES_G), lambda g: (layer, 0, g))
    return col, diag, vec


def _to_scan_order(src_ref, dst_ref):
    for tau in range(CHUNK):
        dst_ref[pl.ds(tau * N_CHUNKS, N_CHUNKS), :] = src_ref[pl.ds(tau, N_CHUNKS, stride=CHUNK), :]


def _to_time_order(src_ref, dst_ref, dtype):
    for j in range(N_CHUNKS):
        dst_ref[pl.ds(j * CHUNK, CHUNK), :] = src_ref[pl.ds(j, CHUNK, stride=N_CHUNKS), :].astype(dtype)


def _ssm_fwd(u, mats, layer, d):
    def body(u_ref, d_ref, br_ref, bi_ref, cr_ref, ci_ref, ar_ref, ai_ref, xr_ref, xi_ref, y_ref, us_ref):
        _to_scan_order(u_ref, us_ref)
        uv = us_ref[...].astype(BF16)
        xr_ref[...] = jnp.dot(uv, br_ref[...], preferred_element_type=F32)
        xi_ref[...] = jnp.dot(uv, bi_ref[...], preferred_element_type=F32)
        write, start = _scan_in_place(xr_ref, xi_ref, ar_ref[...], ai_ref[...], False)
        lax.fori_loop(0, CHUNK, write, start, unroll=8)
        y = lax.dot_general(xr_ref[...].astype(BF16), cr_ref[...], NT, preferred_element_type=F32)
        y += lax.dot_general(xi_ref[...].astype(BF16), ci_ref[...], NT, preferred_element_type=F32)
        us_ref[...] = y + d_ref[...] * us_ref[...]
        _to_time_order(us_ref, y_ref, F32)

    col, diag, vec = _ssm_specs(layer)
    return pl.pallas_call(
        body, grid=(N_LANE_GROUPS,),
        in_specs=[col(128), pl.BlockSpec((None, 1, 128), lambda g: (layer, 0, g)),
                  diag, diag, diag, diag, vec, vec],
        out_specs=[col(LANES_G), col(LANES_G), col(128)],
        out_shape=[jax.ShapeDtypeStruct((SEQ, SSM_GROUPS * SSM_STATE), F32)] * 2
        + [jax.ShapeDtypeStruct((SEQ, WIDTH), F32)],
        scratch_shapes=[pltpu.VMEM((SEQ, 128), F32)], compiler_params=_cp(), name="ssm_fwd",
    )(u, d, mats["b_re"], mats["b_im"], mats["c_re"], mats["c_im_neg"], mats["a_re"], mats["a_im"])


def _ssm_bwd(dy, x_re, x_im, u, mats, layer, d):
    def body(dyt_ref, ut_ref, d_ref, xr_ref, xi_ref, br_ref, bi_ref, cr_ref, ci_ref, ar_ref, ai_ref,
             du_ref, dar_ref, dai_ref, dbr_ref, dbi_ref, dcr_ref, dci_ref, lr_ref, li_ref, dys_ref, u_ref):
        _to_scan_order(dyt_ref, dys_ref)
        _to_scan_order(ut_ref, u_ref)
        dy = dys_ref[...].astype(BF16)
        lr_ref[...] = jnp.dot(dy, cr_ref[...], preferred_element_type=F32)
        li_ref[...] = jnp.dot(dy, ci_ref[...], preferred_element_type=F32)
        write, start = _scan_in_place(lr_ref, li_ref, ar_ref[...], -ai_ref[...], True)

        def rows(t):
            return pl.ds(pl.multiple_of(t * N_CHUNKS, N_CHUNKS), N_CHUNKS)

        def grad(acc, lam, xpr, xpi):
            return acc[0] + xpr * lam[0] + xpi * lam[1], acc[1] + xpr * lam[1] - xpi * lam[0]

        def down(tau, carry):
            lam = write(tau, carry[0])
            t = CHUNK - 2 - tau
            return lam, grad(carry[1], lam, xr_ref[rows(t), :], xi_ref[rows(t), :])

        zero = jnp.zeros((N_CHUNKS, LANES_G), F32)
        lam, acc = lax.fori_loop(0, CHUNK - 1, down, (start, (zero, zero)), unroll=5)
        lam = write(CHUNK - 1, lam)
        last = rows(CHUNK - 1)
        acc = grad(acc, lam, _shift_down(xr_ref[last, :], 1), _shift_down(xi_ref[last, :], 1))
        dar_ref[...] = jnp.sum(acc[0], axis=0, keepdims=True)
        dai_ref[...] = jnp.sum(acc[1], axis=0, keepdims=True)

        l_re, l_im = lr_ref[...].astype(BF16), li_ref[...].astype(BF16)
        du = lax.dot_general(l_re, br_ref[...], NT, preferred_element_type=F32)
        du += lax.dot_general(l_im, bi_ref[...], NT, preferred_element_type=F32)
        dys_ref[...] = du + dys_ref[...] * d_ref[...]
        _to_time_order(dys_ref, du_ref, BF16)
        uv = u_ref[...].astype(BF16)
        dbr_ref[...] = lax.dot_general(uv, l_re, TN, preferred_element_type=F32)
        dbi_ref[...] = lax.dot_general(uv, l_im, TN, preferred_element_type=F32)
        dcr_ref[...] = lax.dot_general(dy, xr_ref[...].astype(BF16), TN, preferred_element_type=F32)
        dci_ref[...] = lax.dot_general(dy, xi_ref[...].astype(BF16), TN, preferred_element_type=F32)

    col, diag, vec = _ssm_specs(layer)
    out_vec = pl.BlockSpec((1, LANES_G), lambda g: (0, g))
    out_blk = pl.BlockSpec((None, 128, LANES_G), lambda g: (g, 0, 0))
    sds = jax.ShapeDtypeStruct
    return pl.pallas_call(
        body, grid=(N_LANE_GROUPS,),
        in_specs=[col(128), col(128), pl.BlockSpec((None, 1, 128), lambda g: (layer, 0, g)),
                  col(LANES_G), col(LANES_G), diag, diag, diag, diag, vec, vec],
        out_specs=[col(128), out_vec, out_vec, out_blk, out_blk, out_blk, out_blk],
        out_shape=[sds((SEQ, WIDTH), BF16)] + [sds((1, SSM_GROUPS * SSM_STATE), F32)] * 2
        + [sds((N_LANE_GROUPS, 128, LANES_G), F32)] * 4,
        scratch_shapes=[pltpu.VMEM((SEQ, LANES_G), F32)] * 2 + [pltpu.VMEM((SEQ, 128), F32)] * 2,
        compiler_params=_cp(), name="ssm_bwd",
    )(dy, u, d, x_re, x_im, mats["b_re"], mats["b_im"], mats["c_re"], mats["c_im_neg"],
      mats["a_re"], mats["a_im"])


_GELU_C = math.sqrt(2.0 / math.pi)


def _gelu(y):
    return 0.5 * y * (1.0 + jnp.tanh(_GELU_C * (y + 0.044715 * (y * y * y))))


def _glu_fwd(y, wglu):
    tt = 512

    def body(y_ref, w_ref, z_ref):
        ys = _gelu(y_ref[...])
        a = jnp.dot(ys.astype(BF16), w_ref[...], preferred_element_type=F32)
        z_ref[...] = (ys * jax.nn.sigmoid(a)).astype(BF16)

    blk = pl.BlockSpec((tt, WIDTH), lambda i: (i, 0))
    return pl.pallas_call(body, grid=(SEQ // tt,), in_specs=[blk, _full((WIDTH, WIDTH))], out_specs=blk,
                          out_shape=jax.ShapeDtypeStruct((SEQ, WIDTH), BF16), compiler_params=_cp(),
                          name="glu_fwd")(y, wglu)


def _glu_bwd(y, wglu, dz, u):
    tt = 512

    def body(y_ref, w_ref, dz_ref, u_ref, dy_ref, ys_ref, da_ref, dd_ref):
        @pl.when(pl.program_id(0) == 0)
        def _():
            dd_ref[...] = jnp.zeros_like(dd_ref)

        yv = y_ref[...]
        t = jnp.tanh(_GELU_C * (yv + 0.044715 * (yv * yv * yv)))
        ys = 0.5 * yv * (1.0 + t)
        ysb = ys.astype(BF16)
        sg = jax.nn.sigmoid(jnp.dot(ysb, w_ref[...], preferred_element_type=F32))
        dz = dz_ref[...].astype(F32)
        da = (dz * ys * sg * (1.0 - sg)).astype(BF16)
        dys = dz * sg + lax.dot_general(da, w_ref[...], NT, preferred_element_type=F32)
        dy = dys * (0.5 * (1.0 + t) + 0.5 * yv * (1.0 - t * t) * _GELU_C * (1.0 + 3 * 0.044715 * (yv * yv)))
        dy_ref[...] = dy
        ys_ref[...] = ysb
        da_ref[...] = da
        dd_ref[...] += jnp.sum(dy * u_ref[...], axis=0, keepdims=True)

    blk = pl.BlockSpec((tt, WIDTH), lambda i: (i, 0))
    return pl.pallas_call(
        body, grid=(SEQ // tt,), in_specs=[blk, _full((WIDTH, WIDTH)), blk, blk],
        out_specs=[blk, blk, blk, _full((1, WIDTH))],
        out_shape=[jax.ShapeDtypeStruct((SEQ, WIDTH), F32)] + [jax.ShapeDtypeStruct((SEQ, WIDTH), BF16)] * 2
        + [jax.ShapeDtypeStruct((1, WIDTH), F32)],
        compiler_params=_cp(), name="glu_bwd")(y, wglu, dz, u)


def _mix_specs(tt, layer):
    row = lambda w: pl.BlockSpec((tt, w), lambda i: (i, 0))
    gate = lambda j: pl.BlockSpec((tt, D_MODEL), lambda i: (i, j))
    wo = lambda j: pl.BlockSpec((D_MODEL, WIDTH), lambda i: (0, j))
    return [row(D_MODEL), row(WIDTH), row(WIDTH), row(WIDTH), gate(0), gate(1), gate(2),
            pl.BlockSpec((None, 1, GATE_W), lambda i: (layer, 0, 0)), wo(0), wo(1), wo(2),
            _full((D_MODEL, D_MODEL))]


def _mix_branches(o_ref, c_ref, z_ref, g_refs, b_ref, wa_ref, wc_ref, ws_ref):
    ys = [lax.dot_general(r[...], w[...], NT, preferred_element_type=F32)
          for r, w in ((o_ref, wa_ref), (c_ref, wc_ref), (z_ref, ws_ref))]
    gates = [jax.nn.sigmoid(g_refs[j][...] + b_ref[:, D_MODEL * j:D_MODEL * (j + 1)]) for j in range(3)]
    return ys, gates


def _mix_fwd(x, o, cv, z, glog, b_gate, layer, wbt, wmix, tie):
    tt = 256

    def body(x_ref, o_ref, c_ref, z_ref, g0, g1, g2, b_ref, wa_ref, wc_ref, ws_ref, wm_ref, tie_ref, x1_ref):
        ys, gates = _mix_branches(o_ref, c_ref, z_ref, (g0, g1, g2), b_ref, wa_ref, wc_ref, ws_ref)
        merged = gates[0] * ys[0] + gates[1] * ys[1] + gates[2] * ys[2]
        x1_ref[...] = x_ref[...] + jnp.dot(merged.astype(BF16), wm_ref[...], preferred_element_type=F32)

    return pl.pallas_call(
        body, grid=(SEQ // tt,), in_specs=_mix_specs(tt, layer) + [ANY],
        out_specs=pl.BlockSpec((tt, D_MODEL), lambda i: (i, 0)),
        out_shape=jax.ShapeDtypeStruct((SEQ, D_MODEL), F32), compiler_params=_cp(), name="mix_fwd",
    )(x, o, cv, z, glog, glog, glog, b_gate, wbt, wbt, wbt, wmix, tie)


def _mix_bwd(dx1, o, cv, z, glog, b_gate, layer, wbt, wmix, tie):
    tt = 256

    def body(dx_ref, o_ref, c_ref, z_ref, g0, g1, g2, b_ref, wa_ref, wc_ref, ws_ref, wm_ref, tie_ref,
             mg_ref, dya_ref, dyc_ref, dys_ref, do_ref, dc_ref, dz_ref, dgl_ref, db_ref):
        @pl.when(pl.program_id(0) == 0)
        def _():
            db_ref[...] = jnp.zeros_like(db_ref)

        ys, gates = _mix_branches(o_ref, c_ref, z_ref, (g0, g1, g2), b_ref, wa_ref, wc_ref, ws_ref)
        mg_ref[...] = (gates[0] * ys[0] + gates[1] * ys[1] + gates[2] * ys[2]).astype(BF16)
        dm = lax.dot_general(dx_ref[...].astype(BF16), wm_ref[...], NT, preferred_element_type=F32)
        for j, (dy_ref, w_ref, d_ref) in enumerate(((dya_ref, wa_ref, do_ref), (dyc_ref, wc_ref, dc_ref),
                                                    (dys_ref, ws_ref, dz_ref))):
            dy = (dm * gates[j]).astype(BF16)
            dy_ref[...] = dy
            d_ref[...] = jnp.dot(dy, w_ref[...], preferred_element_type=F32)
            dgl = dm * ys[j] * gates[j] * (1.0 - gates[j])
            dgl_ref[:, D_MODEL * j:D_MODEL * (j + 1)] = dgl.astype(BF16)
            db_ref[:, D_MODEL * j:D_MODEL * (j + 1)] += jnp.sum(dgl, axis=0, keepdims=True)

    row = lambda w: pl.BlockSpec((tt, w), lambda i: (i, 0))
    sds = jax.ShapeDtypeStruct
    return pl.pallas_call(
        body, grid=(SEQ // tt,), in_specs=_mix_specs(tt, layer) + [ANY],
        out_specs=[row(D_MODEL)] * 4 + [row(WIDTH)] * 3 + [row(GATE_W), _full((1, GATE_W))],
        out_shape=[sds((SEQ, D_MODEL), BF16)] * 4 + [sds((SEQ, WIDTH), F32)] * 3
        + [sds((SEQ, GATE_W), BF16), sds((1, GATE_W), F32)],
        compiler_params=_cp(), name="mix_bwd",
    )(dx1, o, cv, z, glog, glog, glog, b_gate, wbt, wbt, wbt, wmix, tie)


def _ffn_out_fwd(x1, act, wout, tie):
    tt = 512

    def body(x_ref, a_ref, w_ref, tie_ref, o_ref):
        o_ref[...] = x_ref[...] + jnp.dot(a_ref[...], w_ref[...], preferred_element_type=F32)

    row = lambda w: pl.BlockSpec((tt, w), lambda i: (i, 0))
    return pl.pallas_call(
        body, grid=(SEQ // tt,), in_specs=[row(D_MODEL), row(FFN_H), _full((FFN_H, D_MODEL)), ANY],
        out_specs=row(D_MODEL), out_shape=jax.ShapeDtypeStruct((SEQ, D_MODEL), F32),
        compiler_params=_cp(), name="ffn_out_fwd")(x1, act, wout, tie)


def _ffn_out_bwd(dx2, up, silu, dsilu, wout, tie):
    tt = 512

    def body(dx_ref, up_ref, silu_ref, dsilu_ref, w_ref, tie_ref, dgu_ref):
        dact = lax.dot_general(dx_ref[...].astype(BF16), w_ref[...], NT, preferred_element_type=F32).astype(BF16)
        dgu_ref[:, :FFN_H] = dact * up_ref[...] * dsilu_ref[...]
        dgu_ref[:, FFN_H:] = dact * silu_ref[...]

    row = lambda w: pl.BlockSpec((tt, w), lambda i: (i, 0))
    return pl.pallas_call(
        body, grid=(SEQ // tt,),
        in_specs=[row(D_MODEL), row(FFN_H), row(FFN_H), row(FFN_H), _resident((FFN_H, D_MODEL)), ANY],
        out_specs=row(2 * FFN_H), out_shape=jax.ShapeDtypeStruct((SEQ, 2 * FFN_H), BF16),
        compiler_params=_cp(), name="ffn_out_bwd")(dx2, up, silu, dsilu, wout, tie)


def _loss_head(x, g, target):
    tt = 256

    def body(x_ref, g_ref, t_ref, loss_ref, dx_ref, dg_ref):
        @pl.when(pl.program_id(0) == 0)
        def _():
            loss_ref[...] = jnp.zeros_like(loss_ref)
            dg_ref[...] = jnp.zeros_like(dg_ref)

        xv = x_ref[...]
        r = lax.rsqrt(jnp.mean(xv * xv, axis=-1, keepdims=True) + NORM_EPS)
        xh = xv * r
        err = xh * g_ref[...] - t_ref[...]
        loss_ref[...] += 0.5 * jnp.sum(jnp.mean(err * err, axis=-1, keepdims=True))
        dy = err * (1.0 / D_MODEL)
        gy = dy * g_ref[...]
        dx_ref[...] = r * (gy - xh * jnp.mean(gy * xh, axis=-1, keepdims=True))
        dg_ref[...] += jnp.sum(dy * xh, axis=0, keepdims=True)

    row = pl.BlockSpec((tt, D_MODEL), lambda i: (i, 0))
    return pl.pallas_call(
        body, grid=(SEQ // tt,), in_specs=[row, _full((1, D_MODEL)), row],
        out_specs=[_full((1, 128)), row, _full((1, D_MODEL))],
        out_shape=[jax.ShapeDtypeStruct((1, 128), F32), jax.ShapeDtypeStruct((SEQ, D_MODEL), F32),
                   jax.ShapeDtypeStruct((1, D_MODEL), F32)],
        compiler_params=_cp(), name="loss_head")(x, g, target)


def _adam_math(g, w, m, v):
    nm = B1 * m + (1.0 - B1) * g
    nv = B2 * v + (1.0 - B2) * (g * g)
    m_hat = nm / (1.0 - B1 ** STEP)
    v_hat = nv / (1.0 - B2 ** STEP)
    return -LR * (m_hat / (jnp.sqrt(v_hat) + ADAM_EPS) + WD * w), nm, nv


def _adamw_small(parts, w, m, v, name):
    def body(p_ref, w_ref, m_ref, v_ref, g_ref, d_ref, nm_ref, nv_ref):
        g = p_ref[0].astype(F32)
        for k in range(1, N_DEV):
            g = g + p_ref[k].astype(F32)
        g_ref[...] = g
        d_ref[...], nm_ref[...], nv_ref[...] = _adam_math(g, w_ref[...], m_ref[...], v_ref[...])

    out_shape = [jax.ShapeDtypeStruct(w.shape, F32)] * 4
    if w.ndim < 3:
        return pl.pallas_call(body, out_shape=out_shape, name=name)(parts, w, m, v)
    rest = w.shape[1:]
    zeros = (0,) * len(rest)
    blk = pl.BlockSpec((None,) + rest, lambda l: (l,) + zeros)
    return pl.pallas_call(
        body, grid=(w.shape[0],),
        in_specs=[pl.BlockSpec((N_DEV, None) + rest, lambda l: (0, l) + zeros), blk, blk, blk],
        out_specs=[blk] * 4, out_shape=out_shape, name=name)(parts, w, m, v)


def _adamw(parts, w, m, v, tr, name, groups=None, fill=None, tie=None):
    n_groups, rows, cols = w.shape
    n_parts = parts.shape[1]
    lo, hi = groups if groups is not None else (0, n_groups)

    def body(p_ref, w_ref, m_ref, v_ref, *rest):
        g_ref, d_ref, nm_ref, nv_ref = rest[-4:]
        g = p_ref[0].astype(F32)
        for k in range(1, n_parts):
            g = g + p_ref[k].astype(F32)
        nm = B1 * m_ref[...] + (1.0 - B1) * g
        nv = B2 * v_ref[...] + (1.0 - B2) * (g * g)
        m_hat = nm / (1.0 - B1 ** STEP)
        v_hat = nv / (1.0 - B2 ** STEP)
        g_ref[...] = g
        d_ref[...] = -LR * (m_hat / (jnp.sqrt(v_hat) + ADAM_EPS) + WD * w_ref[...])
        nm_ref[...] = nm
        nv_ref[...] = nv

    blk = pl.BlockSpec((None, tr, cols), lambda l, i: (l + lo, i, 0))
    p_lo = lo if parts.shape[0] == n_groups else 0
    extra = ([] if fill is None else list(fill)) + ([] if tie is None else [tie])
    return pl.pallas_call(
        body, grid=(hi - lo, rows // tr),
        in_specs=[pl.BlockSpec((None, n_parts, tr, cols), lambda l, i: (l + p_lo, 0, i, 0)), blk, blk, blk]
        + [ANY] * len(extra),
        out_specs=[blk] * 4, out_shape=[jax.ShapeDtypeStruct((n_groups, rows, cols), F32)] * 4,
        input_output_aliases={} if fill is None else {4 + j: j for j in range(4)},
        compiler_params=_cp(), name=name)(parts, w, m, v, *extra)


BRANCHES = ("w_attn_o", "w_conv_o", "w_ssm_o")


def _adamw_branches(parts, wmv, name, groups, fill=None, tie=None):
    n_parts = parts.shape[1]
    lo, hi = groups

    def body(p_ref, *refs):
        ins, outs = refs[:9], refs[-12:]
        g = p_ref[0].astype(F32)
        for k in range(1, n_parts):
            g = g + p_ref[k].astype(F32)
        for j in range(3):
            gj = g[:, j * WIDTH:(j + 1) * WIDTH].T
            w_ref, m_ref, v_ref = ins[3 * j:3 * j + 3]
            d, nm, nv = _adam_math(gj, w_ref[...], m_ref[...], v_ref[...])
            for o, val in zip(outs[4 * j:4 * j + 4], (gj, d, nm, nv)):
                o[...] = val

    shard = wmv[0].shape[1:]
    blk = pl.BlockSpec((None,) + shard, lambda l: (l + lo, 0, 0))
    p_lo = lo if parts.shape[0] == DEPTH else 0
    extra = ([] if fill is None else list(fill)) + ([] if tie is None else [tie])
    return pl.pallas_call(
        body, grid=(hi - lo,),
        in_specs=[pl.BlockSpec((None,) + parts.shape[1:], lambda l: (l + p_lo, 0, 0, 0))] + [blk] * 9
        + [ANY] * len(extra),
        out_specs=[blk] * 12, out_shape=[jax.ShapeDtypeStruct((DEPTH,) + shard, F32)] * 12,
        input_output_aliases={} if fill is None else {10 + j: j for j in range(12)},
        compiler_params=_cp(), name=name)(parts, *wmv, *extra)


def _split_start(name, arrays, n_sems, plan, after=None):
    n = len(arrays)
    order = [] if after is None else [after]
    n_in = n + len(order)

    def body(*refs):
        ins, send_sems, recv_sems, token = refs[:n], refs[n_in], refs[n_in + 1], refs[-1]
        for src, dst, k, to in plan(ins)[0]:
            pltpu.make_async_remote_copy(src_ref=src, dst_ref=dst, send_sem=send_sems.at[k], recv_sem=recv_sems.at[k],
                                         device_id=to, device_id_type=MESH_ID).start()
        token[...] = jnp.zeros_like(token)

    outs = pl.pallas_call(
        body, name=name,
        out_shape=(pltpu.SemaphoreType.DMA((n_sems,)), pltpu.SemaphoreType.DMA((n_sems,)),
                   *[pltpu.HBM(a.shape, a.dtype) for a in arrays], jax.ShapeDtypeStruct((8, 128), F32)),
        in_specs=[HBM] * n + [ANY] * len(order),
        out_specs=(SEM, SEM, *[HBM] * n, pl.BlockSpec(memory_space=pltpu.VMEM)),
        input_output_aliases={i: 2 + i for i in range(n)},
        compiler_params=pltpu.CompilerParams(has_side_effects=EFFECT),
    )(*[pltpu.with_memory_space_constraint(a, pltpu.HBM) for a in arrays], *order)
    return outs[0], outs[1], list(outs[2:2 + n]), outs[-1]


def _split_wait(name, arrays, send_sems, recv_sems, after, plan):
    n = len(arrays)
    order = list(after) if isinstance(after, (list, tuple)) else [after]

    def body(*refs):
        ins, s_sems, r_sems = refs[:n], refs[n], refs[n + 1]
        sends, arrivals = plan(ins)
        x, y, c = lax.axis_index("x"), lax.axis_index("y"), lax.axis_index("c")
        for src, dst, k, to in sends:
            pltpu.make_async_remote_copy(src_ref=src, dst_ref=dst, send_sem=s_sems.at[k], recv_sem=r_sems.at[k],
                                         device_id=to, device_id_type=MESH_ID).wait_send()
        for dst, k in arrivals:
            pltpu.make_async_remote_copy(src_ref=dst, dst_ref=dst, send_sem=s_sems.at[k], recv_sem=r_sems.at[k],
                                         device_id=(x, y, c), device_id_type=MESH_ID).wait_recv()

    return pl.pallas_call(
        body, name=name, out_shape=[pltpu.HBM(a.shape, a.dtype) for a in arrays],
        in_specs=[HBM] * n + [SEM, SEM] + [ANY] * len(order), out_specs=[HBM] * n,
        input_output_aliases={i: i for i in range(n)},
        compiler_params=pltpu.CompilerParams(has_side_effects=EFFECT),
    )(*arrays, send_sems, recv_sems, *order)


def _chips():
    x, y, c = lax.axis_index("x"), lax.axis_index("y"), lax.axis_index("c")
    return x, y, c, [(1 - x, y), (x, 1 - y), (1 - x, 1 - y)]


def _plan_gather_chips(refs):
    x, y, c, chips = _chips()
    me = 4 * x + 2 * y + c
    n = len(refs) // 2
    sends, arrivals = [], []
    for i in range(n):
        src, land = refs[i], refs[n + i]
        sends.append((src, land.at[me], 4 * i, (x, y, 1 - c)))
        arrivals.append((land.at[4 * x + 2 * y + 1 - c], 4 * i))
        for j, (px, py) in enumerate(chips):
            sends.append((src, land.at[me], 4 * i + 1 + j, (px, py, c)))
            arrivals.append((land.at[4 * px + 2 * py + c], 4 * i + 1 + j))
    return sends, arrivals


def _plan_gather_pass(refs):
    x, y, c, chips = _chips()
    sends, arrivals = [], []
    for i in range(len(refs)):
        for j, (px, py) in enumerate(chips):
            slot = refs[i].at[4 * px + 2 * py + c]
            sends.append((slot, slot, 4 * i + j, (x, y, 1 - c)))
            arrivals.append((refs[i].at[4 * px + 2 * py + 1 - c], 4 * i + j))
        back = refs[i].at[4 * x + 2 * y + 1 - c]
        sends.append((back, back, 4 * i + 3, (x, y, 1 - c)))
        arrivals.append((refs[i].at[4 * x + 2 * y + c], 4 * i + 3))
    return sends, arrivals


def _plan_scatter_pair(refs):
    x, y, c = lax.axis_index("x"), lax.axis_index("y"), lax.axis_index("c")
    n = len(refs) // 2
    sends, arrivals = [], []
    for i in range(n):
        for q in range(4):
            sends.append((refs[i].at[q, 1 - c], refs[n + i].at[q], 4 * i + q, (x, y, 1 - c)))
            arrivals.append((refs[n + i].at[q], 4 * i + q))
    return sends, arrivals


def _plan_scatter_chips(layer):
    def plan(refs):
        x, y, c, chips = _chips()
        n = len(refs) // 2
        sends, arrivals = [], []
        for i in range(n):
            for j, (px, py) in enumerate(chips):
                sends.append((refs[i].at[2 * px + py], refs[n + i].at[layer, 2 * x + y], 3 * i + j, (px, py, c)))
                arrivals.append((refs[n + i].at[layer, 2 * px + py], 3 * i + j))
        return sends, arrivals

    return plan


def _pair_sum(parts4, from_pair, landing, layer, core, tr, name):
    _, _, rows, cols = parts4.shape

    def body(c_ref, p_ref, s_ref, l_ref, sum_ref, land_ref):
        v = (p_ref[...].astype(F32) + s_ref[...].astype(F32)).astype(BF16)
        sum_ref[...] = v
        land_ref[...] = v

    blk = pl.BlockSpec((None, tr, cols), lambda q, i, c_ref: (q, i, 0))
    return pl.pallas_call(
        body,
        grid_spec=pltpu.PrefetchScalarGridSpec(
            num_scalar_prefetch=1, grid=(4, rows // tr),
            in_specs=[pl.BlockSpec((None, None, tr, cols), lambda q, i, c_ref: (q, c_ref[0], i, 0)), blk, ANY],
            out_specs=[blk, pl.BlockSpec((None, None, tr, cols), lambda q, i, c_ref: (layer, q, i, 0))]),
        out_shape=[jax.ShapeDtypeStruct((4, rows, cols), BF16), jax.ShapeDtypeStruct(landing.shape, BF16)],
        input_output_aliases={3: 1}, compiler_params=_cp(), name=name,
    )(core, parts4, from_pair, landing)


def _travel_layout(t):
    tr = lambda a: jnp.swapaxes(a, 1, 2)
    branch = jnp.concatenate([tr(t["w_attn_o"]), tr(t["w_conv_o"]), tr(t["w_ssm_o"])], axis=2)
    return [tr(t["w_in"]), tr(t["w_ffn_in"]), t["w_ffn_out"], t["w_mix_o"], branch, t["w_ssm_glu"]]


def _native_layout(a):
    tr = lambda x: jnp.swapaxes(x, 1, 2)
    return {"w_in": tr(a[0]), "w_ffn_in": tr(a[1]), "w_ffn_out": a[2], "w_mix_o": a[3], "w_ssm_glu": a[5]}


def _diag_blocks(t):
    t = t.reshape(DEPTH, N_LANE_GROUPS, 8, SSM_GROUP, 8, SSM_STATE)
    return jnp.einsum("lgahap->lgahp", t).reshape(DEPTH, SSM_GROUPS, SSM_GROUP, SSM_STATE)


def _rope_tabs():
    pos = jnp.arange(SEQ, dtype=F32)
    inv_freq = ROPE_THETA ** (-jnp.arange(0, ROT_DIM, 2, dtype=F32) / ROT_DIM)
    ang = pos[:, None] * inv_freq[None, :]
    cos, sin = jnp.cos(ang), jnp.sin(ang)
    one, zero = jnp.ones((SEQ, HEAD_DIM - ROT_DIM), F32), jnp.zeros((SEQ, HEAD_DIM - ROT_DIM), F32)
    z8 = jnp.zeros((SEQ, 8), F32)
    head = lambda *p: jnp.tile(jnp.concatenate(p, axis=1), (1, 2))
    return head(cos, cos, one), head(-sin, z8, zero), head(z8, sin, zero)


def _ssm_mats(sp):
    lr, li, bbr, bbi = _ssm_prep(sp["a_re"], sp["a_im"], sp["log_dt"], sp["bt_re"], sp["bt_im"])
    lanes = SSM_GROUPS * SSM_STATE
    b_re, b_im, c_re, c_im_neg = _ssm_embed(bbr, bbi, sp["c_re"], sp["c_im"])
    return {
        "a_re": lr.reshape(DEPTH, 1, lanes), "a_im": li.reshape(DEPTH, 1, lanes),
        "b_re": b_re, "b_im": b_im, "c_re": c_re, "c_im_neg": c_im_neg,
    }


def _layer_fwd(x, i, w, rp, mats, tabs, tie, hooks):
    q, kv, cbx, u, glog, cv, h = _rms_mm_in(x, rp["norm_mix"][i], w["win_t"], tabs, rp["conv_w"], i, tie)
    o = _attn_fwd(q, kv, tabs, rp["attn_sinks"][i])
    x_re, x_im, y = _ssm_fwd(u, mats, i, rp["ssm_d"])
    z = _glu_fwd(y, w["wglu"])
    x1 = _mix_fwd(x, o, cv, z, glog, rp["b_gate"], i, w["branch_t"], w["wmix"], hooks["early"](z))
    hooks["pre_ffn"](x1)
    act, up, silu, dsilu, h2 = _rms_mm_ffn(x1, rp["norm_ffn"][i], w["wffn_t"])
    x2 = _ffn_out_fwd(x1, act, w["wout"], hooks["mid"](h2))
    kept = dict(x=x, q=q, kv=kv, cbx=cbx, u=u, glog=glog, h=h, o=o, cv=cv, z=z, y=y,
                x_re=x_re, x_im=x_im, x1=x1, act=act, up=up, silu=silu, dsilu=dsilu, h2=h2)
    return x2, kept


def _layer_bwd(dx2, k, i, w, rp, mats, tabs, tie, hooks):
    dgu = _ffn_out_bwd(dx2, k["up"], k["silu"], k["dsilu"], w["wout"], tie)
    g_wout = _mm_tn(k["act"], dx2, tm=FFN_H // 2, tn=1024, name="mm_tn_ffn_out")
    g_wffn_t = _mm_tn(dgu, k["h2"], tm=FFN_H // 2, tn=1024, name="mm_tn_ffn_in")
    dx1, d_norm_ffn = _mm_rmsbwd([dgu], w["wffn_t"], k["x1"], rp["norm_ffn"][i], dx2, "mm_rmsbwd_ffn")

    mg, dya, dyc, dys, do, dcv, dz, dgl, db_gate = _mix_bwd(
        dx1, k["o"], k["cv"], k["z"], k["glog"], rp["b_gate"], i, w["branch_t"], w["wmix"],
        hooks["mid"]((g_wffn_t, g_wout, d_norm_ffn)))
    g_wmix = _mm_tn(mg, dx1, tm=1024, tn=512, name="mm_tn_mix")
    g_branch_t = _tn_branches((dya, dyc, dys), (k["o"], k["cv"], k["z"]))

    dy, ys16, da16, dd = _glu_bwd(k["y"], w["wglu"], dz, k["u"])
    g_wglu = _mm_tn(ys16, da16, tm=256, tn=512, name="mm_tn_glu")
    du, da_re, da_im, db_re, db_im, dc_re, dc_im = _ssm_bwd(dy, k["x_re"], k["x_im"], k["u"], mats, i, rp["ssm_d"])

    dcb, dcc, dcx, d_conv_w = _conv_bwd(k["cbx"], rp["conv_w"], i, dcv, hooks["late"](du))
    dq, dkv, d_sinks = _attn_bwd(k["q"], k["kv"], tabs, rp["attn_sinks"][i], do)

    pieces = [dq, dkv, dcb, dcc, dcx, du, dgl]
    g_win_t = _tn_pieces(pieces, k["h"])
    dx, d_norm_mix = _mm_rmsbwd(pieces, w["win_t"], k["x"], rp["norm_mix"][i], dx1, "mm_rmsbwd_in")

    grads = [g_win_t, g_wffn_t, g_wout, g_wmix, g_branch_t, g_wglu]
    small = dict(norm_mix=d_norm_mix, b_gate=db_gate, attn_sinks=d_sinks, ssm_d=dd, norm_ffn=d_norm_ffn,
                 conv_w=d_conv_w, da_re=da_re, da_im=da_im, db_re=db_re, db_im=db_im, dc_re=dc_re, dc_im=dc_im)
    return dx, grads, small


def _replicated_grads(sg, sp):
    stack = lambda name: jnp.stack([sg[i][name] for i in range(DEPTH)])
    cots = (stack("da_re").reshape(DEPTH, *_GS), stack("da_im").reshape(DEPTH, *_GS),
            _diag_blocks(stack("db_re")), _diag_blocks(stack("db_im")))
    d_a_re, d_a_im, d_log_dt, d_bt_re, d_bt_im = _ssm_prep_bwd(
        sp["a_re"], sp["a_im"], sp["log_dt"], sp["bt_re"], sp["bt_im"], cots)
    sgrads = {"norm_mix": stack("norm_mix"), "b_gate": stack("b_gate"),
              "attn_sinks": stack("attn_sinks")[:, :, :N_Q_HEADS], "ssm_a_re": d_a_re, "ssm_a_im": d_a_im,
              "ssm_b_re": jnp.swapaxes(d_bt_re, 2, 3), "ssm_b_im": jnp.swapaxes(d_bt_im, 2, 3),
              "ssm_c_re": _diag_blocks(stack("dc_re")), "ssm_c_im": -_diag_blocks(stack("dc_im")),
              "ssm_d": stack("ssm_d"), "ssm_log_dt": d_log_dt, "norm_ffn": stack("norm_ffn")}
    return sgrads, stack("conv_w")[:, :3]


def kernel(x, norm_mix, w_in, b_gate, attn_sinks, w_attn_o, conv_w, w_conv_o, ssm_a_re, ssm_a_im, ssm_b_re, ssm_b_im, ssm_c_re, ssm_c_im, ssm_d, ssm_log_dt, w_ssm_glu, w_ssm_o, w_mix_o, norm_ffn, w_ffn_in, w_ffn_out, norm_final, loss_target, m_norm_mix, m_w_in, m_b_gate, m_attn_sinks, m_w_attn_o, m_conv_w, m_w_conv_o, m_ssm_a_re, m_ssm_a_im, m_ssm_b_re, m_ssm_b_im, m_ssm_c_re, m_ssm_c_im, m_ssm_d, m_ssm_log_dt, m_w_ssm_glu, m_w_ssm_o, m_w_mix_o, m_norm_ffn, m_w_ffn_in, m_w_ffn_out, m_norm_final, v_norm_mix, v_w_in, v_b_gate, v_attn_sinks, v_w_attn_o, v_conv_w, v_w_conv_o, v_ssm_a_re, v_ssm_a_im, v_ssm_b_re, v_ssm_b_im, v_ssm_c_re, v_ssm_c_im, v_ssm_d, v_ssm_log_dt, v_w_ssm_glu, v_w_ssm_o, v_w_mix_o, v_norm_ffn, v_w_ffn_in, v_w_ffn_out, v_norm_final):
    big = {"w": dict(w_in=w_in, w_attn_o=w_attn_o, w_conv_o=w_conv_o, w_ssm_glu=w_ssm_glu, w_ssm_o=w_ssm_o,
                     w_mix_o=w_mix_o, w_ffn_in=w_ffn_in, w_ffn_out=w_ffn_out),
           "m": dict(w_in=m_w_in, w_attn_o=m_w_attn_o, w_conv_o=m_w_conv_o, w_ssm_glu=m_w_ssm_glu,
                     w_ssm_o=m_w_ssm_o, w_mix_o=m_w_mix_o, w_ffn_in=m_w_ffn_in, w_ffn_out=m_w_ffn_out),
           "v": dict(w_in=v_w_in, w_attn_o=v_w_attn_o, w_conv_o=v_w_conv_o, w_ssm_glu=v_w_ssm_glu,
                     w_ssm_o=v_w_ssm_o, w_mix_o=v_w_mix_o, w_ffn_in=v_w_ffn_in, w_ffn_out=v_w_ffn_out)}
    small = {"w": dict(norm_mix=norm_mix, b_gate=b_gate, attn_sinks=attn_sinks, ssm_a_re=ssm_a_re,
                       ssm_a_im=ssm_a_im, ssm_b_re=ssm_b_re, ssm_b_im=ssm_b_im, ssm_c_re=ssm_c_re,
                       ssm_c_im=ssm_c_im, ssm_d=ssm_d, ssm_log_dt=ssm_log_dt, norm_ffn=norm_ffn),
             "m": dict(norm_mix=m_norm_mix, b_gate=m_b_gate, attn_sinks=m_attn_sinks, ssm_a_re=m_ssm_a_re,
                       ssm_a_im=m_ssm_a_im, ssm_b_re=m_ssm_b_re, ssm_b_im=m_ssm_b_im, ssm_c_re=m_ssm_c_re,
                       ssm_c_im=m_ssm_c_im, ssm_d=m_ssm_d, ssm_log_dt=m_ssm_log_dt, norm_ffn=m_norm_ffn),
             "v": dict(norm_mix=v_norm_mix, b_gate=v_b_gate, attn_sinks=v_attn_sinks, ssm_a_re=v_ssm_a_re,
                       ssm_a_im=v_ssm_a_im, ssm_b_re=v_ssm_b_re, ssm_b_im=v_ssm_b_im, ssm_c_re=v_ssm_c_re,
                       ssm_c_im=v_ssm_c_im, ssm_d=v_ssm_d, ssm_log_dt=v_ssm_log_dt, norm_ffn=v_norm_ffn)}
    finals = {"w": norm_final, "m": m_norm_final, "v": v_norm_final}
    convs = {"w": conv_w, "m": m_conv_w, "v": v_conv_w}
    small_out_shapes = {name: a.shape for name, a in small["w"].items()}
    small_out_shapes.update(norm_final=(D_MODEL,), conv_w=(DEPTH, 3, 64))
    small_shapes = dict(small_out_shapes, norm_final=(1, D_MODEL), conv_w=(DEPTH, 3, WIDTH))
    dense = ("ssm_b_re", "ssm_b_im", "ssm_c_re", "ssm_c_im")
    for name in dense:
        small_shapes[name] = (DEPTH, SSM_GROUPS, SSM_GROUP * SSM_STATE)
    small_wmv = {name: [(convs[s] if name == "conv_w" else finals[s] if name == "norm_final" else small[s][name])
                        .reshape((DEPTH, 3, 64) if name == "conv_w" else small_shapes[name]) for s in "wmv"]
                 for name in small_shapes}
    mine = 4 * lax.axis_index("x") + 2 * lax.axis_index("y") + lax.axis_index("c")

    travel = {s: _travel_layout(big[s]) for s in "wmv"}
    stacked16 = list(zip(*[[a[0] for a in _travel_layout({n: w[i:i + 1].astype(BF16) for n, w in big["w"].items()})]
                           for i in range(DEPTH)]))
    rp = {"norm_mix": norm_mix[:, None], "norm_ffn": norm_ffn[:, None], "attn_sinks": attn_sinks[:, None],
          "b_gate": b_gate[:, None], "ssm_d": ssm_d[:, None]}
    sp = {"a_re": ssm_a_re, "a_im": ssm_a_im, "log_dt": ssm_log_dt[:, :, None],
          "bt_re": jnp.swapaxes(ssm_b_re, 2, 3), "bt_im": jnp.swapaxes(ssm_b_im, 2, 3),
          "c_re": ssm_c_re, "c_im": ssm_c_im}
    rows_tile = {"win_t": 368, "wffn_t": 352, "wout": 352, "wmix": 128, "branch_t": 128, "wglu": 64}
    core = lax.axis_index("c").astype(jnp.int32).reshape(1)
    no_tie = jnp.zeros((8, 128), F32)

    def landing_zones(srcs):
        return [lax.empty((N_DEV,) + s.shape, s.dtype) for s in srcs]

    def gather_chips(tag, i, kinds, after, extra=()):
        srcs = [stacked16[j][i] for j in kinds] + list(extra)
        s_sems, r_sems, arrays, token = _split_start(
            f"gather_chips_start_{tag}", srcs + landing_zones(srcs), 4 * len(srcs), _plan_gather_chips, after)
        return (tag, s_sems, r_sems, arrays), token

    def gather_pass(state, after):
        tag, s_sems, r_sems, arrays = state
        arrays = _split_wait(f"gather_chips_wait_{tag}", arrays, s_sems, r_sems, after, _plan_gather_chips)
        n = len(arrays) // 2
        s_sems, r_sems, lands, token = _split_start(
            f"gather_pass_start_{tag}", list(arrays[n:]), 4 * n, _plan_gather_pass)
        return (tag, s_sems, r_sems, lands), token

    def gather_done(state, after, kinds):
        tag, s_sems, r_sems, lands = state
        lands = _split_wait(f"gather_pass_wait_{tag}", lands, s_sems, r_sems, after, _plan_gather_pass)
        named = {KINDS[j][0]: a.reshape(N_DEV * KINDS[j][1], KINDS[j][2]) for a, j in zip(lands, kinds)}
        return named, list(lands[len(kinds):])

    all_kinds, mixer_kinds, ffn_kinds = tuple(range(len(KINDS))), (0, 3, 4, 5), (1, 2)
    no_hooks = {name: (lambda value: no_tie) for name in ("early", "pre_ffn", "mid", "late")}
    state, token = gather_chips("0m", 0, mixer_kinds, None, extra=[jnp.pad(conv_w.reshape(6, 128), ((0, 2), (0, 0)))])
    mats = _ssm_mats(dict(sp, log_dt=sp["log_dt"] + token[0, 0]))
    tabs = _rope_tabs()
    early_work = list(mats.values()) + list(tabs) + [a for name in dense for a in small_wmv[name]]
    early_work += [stacked16[j][0] for j in ffn_kinds] + [stacked16[j][1] for j in mixer_kinds]
    state, _ = gather_pass(state, early_work)
    ffn_state, tie = gather_chips("0f", 0, ffn_kinds, state[3][0])
    w_next, (conv_all,) = gather_done(state, tabs[2], mixer_kinds)
    conv_full = conv_all[:, :6].reshape(N_DEV, DEPTH, 3, 64).transpose(1, 2, 0, 3).reshape(DEPTH, 3, WIDTH)
    rp["conv_w"] = jnp.pad(conv_full, ((0, 0), (0, 5), (0, 0)))

    act = x[0]
    weights, kept = [], []
    for i in range(DEPTH):
        w_i, hooks, held = w_next, dict(no_hooks), {}

        def early(value, ffn_state=ffn_state, held=held):
            held["ffn"], token = gather_pass(ffn_state, value)
            return token

        def pre_ffn(value, w_i=w_i, held=held):
            w_i.update(gather_done(held["ffn"], value, ffn_kinds)[0])

        hooks.update(early=early, pre_ffn=pre_ffn)
        if i + 1 < DEPTH:
            state, tie = gather_chips(f"{i + 1}m", i + 1, mixer_kinds, tie if i == 0 else w_i["win_t"])

            def mid(value, i=i, state=state, held=held):
                held["next"], token = gather_pass(state, value)
                held["next_ffn"], token = gather_chips(f"{i + 1}f", i + 1, ffn_kinds, token)
                return token

            hooks.update(mid=mid)
        act, k = _layer_fwd(act, i, w_i, rp, mats, tabs, tie, hooks)
        if i + 1 < DEPTH:
            w_next, _ = gather_done(held["next"], act, mixer_kinds)
            ffn_state, tie = held["next_ffn"], no_tie
        weights.append(w_i)
        kept.append(k)
    loss_row, dx, d_norm_final = _loss_head(act, norm_final[None], loss_target[0])

    landings = [lax.empty((DEPTH, 4, r, c), BF16) for _, r, c in KINDS]
    landings0 = [lax.empty((1, 4, r, c), BF16) for _, r, c in KINDS]

    def scatter_pair(tag, kinds, grads, after):
        parts4 = [g.reshape(4, 2, KINDS[j][1], KINDS[j][2]) for g, j in zip(grads, kinds)]
        zones = [lax.empty((4, KINDS[j][1], KINDS[j][2]), BF16) for j in kinds]
        s_sems, r_sems, arrays, token = _split_start(
            f"scatter_pair_start_{tag}", parts4 + zones, 4 * len(kinds), _plan_scatter_pair, after)
        return (tag, kinds, s_sems, r_sems, arrays), token

    def scatter_chips(state, lands, slot, after):
        tag, kinds, s_sems, r_sems, arrays = state
        arrays = _split_wait(f"scatter_pair_wait_{tag}", arrays, s_sems, r_sems, after, _plan_scatter_pair)
        n = len(kinds)
        sums, mine_lands = [], []
        for k, j in enumerate(kinds):
            name = KINDS[j][0]
            chip_sum, land = _pair_sum(arrays[k], arrays[n + k], lands[j], slot, core, KINDS[j][1],
                                       f"pair_sum_{name}")
            sums.append(chip_sum)
            mine_lands.append(land)
        s_sems, r_sems, arrays, token = _split_start(
            f"scatter_chips_start_{tag}", sums + mine_lands, 3 * n, _plan_scatter_chips(slot))
        return (tag, kinds, slot, s_sems, r_sems, arrays), token

    def scatter_done(state, lands, after):
        tag, kinds, slot, s_sems, r_sems, arrays = state
        arrays = _split_wait(f"scatter_chips_wait_{tag}", arrays, s_sems, r_sems, after, _plan_scatter_chips(slot))
        lands = list(lands)
        for k, j in enumerate(kinds):
            lands[j] = arrays[len(kinds) + k]
        return lands

    sg = [None] * DEPTH
    pending, tie = None, no_tie
    for i in reversed(range(DEPTH)):
        hooks, held = dict(no_hooks), {}
        if pending is not None:
            def mid(value, i=i, pending=pending, held=held):
                held["chips"], token = scatter_chips(pending, landings, i + 1, value[2])
                if i == 0:
                    held["ffn_pair"], token = scatter_pair("0f", ffn_kinds, value[:2], token)
                return token

            hooks.update(mid=mid)
        if i == 0:
            def late(value, held=held):
                held["ffn_chips"], token = scatter_chips(held["ffn_pair"], landings0, 0, value)
                return token

            hooks.update(late=late)
        dx, grads, sg[i] = _layer_bwd(dx, kept[i], i, weights[i], rp, mats, tabs, tie, hooks)
        if pending is not None:
            landings = scatter_done(held["chips"], landings, dx)
        if i > 0:
            pending, tie = scatter_pair(str(i), all_kinds, grads, dx)
        else:
            pending, _ = scatter_pair("0m", mixer_kinds, [grads[j] for j in mixer_kinds], dx)

    sgrads, conv_grad = _replicated_grads(sg, sp)

    small_names = list(REPLICATED) + ["norm_final", "conv_w"]
    sgrads.update(norm_final=d_norm_final, conv_w=conv_grad)
    small_src = [sgrads[name].reshape(small_shapes[name]).astype(BF16) for name in small_names]
    small_src.append(jnp.broadcast_to(loss_row[:, :1], (8, 128)))
    last, tie = scatter_chips(pending, landings0, 0, small_src[0])
    s_sems, r_sems, arrays, tie = _split_start(
        "gather_small_chips_start", small_src + landing_zones(small_src), 4 * len(small_src), _plan_gather_chips, tie)
    small_state = ("small", s_sems, r_sems, arrays)

    branch_wmv = [big[s][n] for n in BRANCHES for s in "wmv"]
    jb = [name for name, _, _ in KINDS].index("branch_t")

    def adamw(j, name, parts, label, groups, **kw):
        if j == jb:
            return _adamw_branches(parts, branch_wmv, label + name, groups, **kw)
        return _adamw(parts, travel["w"][j], travel["m"][j], travel["v"][j], rows_tile[name], label + name,
                      groups=groups, **kw)

    big_out = []
    for j, (name, _, _) in enumerate(KINDS):
        big_out.append(adamw(j, name, landings[j], "adamw_late_", (1, DEPTH), tie=tie))
        tie = big_out[-1][-1]
    landings0 = scatter_done(held["ffn_chips"], landings0, tie)
    landings0 = scatter_done(last, landings0, tie)
    small_state, _ = gather_pass(small_state, landings0[0])
    big_out = [adamw(j, name, landings0[j], "adamw_first_", (0, 1), fill=big_out[j])
               for j, (name, _, _) in enumerate(KINDS)]
    big_res = []
    for kind in range(4):
        res = _native_layout([None if j == jb else big_out[j][kind] for j in range(len(KINDS))])
        res.update({n: big_out[jb][4 * b + kind] for b, n in enumerate(BRANCHES)})
        big_res.append(res)

    _, sparts = gather_done(small_state, big_out[-1][0], ())
    loss = jnp.sum(sparts[-1][:, 0, 0])
    sparts = dict(zip(small_names, sparts))
    sparts["conv_w"] = lax.dynamic_slice_in_dim(sparts["conv_w"], mine * 64, 64, axis=3)
    small_res = {}
    for name in small_names:
        res = _adamw_small(sparts[name], *small_wmv[name], "adamw_" + name)
        small_res[name] = [r.reshape(small_out_shapes[name]) for r in res]

    order = ["norm_mix", "w_in", "b_gate", "attn_sinks", "w_attn_o", "conv_w", "w_conv_o", "ssm_a_re", "ssm_a_im",
             "ssm_b_re", "ssm_b_im", "ssm_c_re", "ssm_c_im", "ssm_d", "ssm_log_dt", "w_ssm_glu", "w_ssm_o",
             "w_mix_o", "norm_ffn", "w_ffn_in", "w_ffn_out", "norm_final"]
    outs = [loss, dx[None]]
    for kind in range(4):
        for name in order:
            outs.append(big_res[kind][name] if name in big_res[kind] else small_res[name][kind])
    return tuple(outs)
```

```python
import math

import jax
import jax.numpy as jnp
from jax import lax
from jax.experimental import pallas as pl
from jax.experimental.pallas import tpu as pltpu

F32 = jnp.float32
BF16 = jnp.bfloat16

N_DEV = 8
DEPTH = 4
SEQ = 2048
D_MODEL = 1024
N_Q_HEADS = 8
HEAD_DIM = 64
ATTN_W = 512
KV_W = 128
BLOCK = 128
N_BLOCKS = SEQ // BLOCK
ROPE_THETA = 500000.0
ROT_DIM = 16
NEG_INF = -1e30
WIDTH = 512
SSM_GROUPS = 32
SSM_GROUP = 16
SSM_STATE = 64
CHUNK = 256
N_CHUNKS = SEQ // CHUNK
GATE_W = 3 * D_MODEL
IN_COLS = 5888
FFN_H = 2816
NORM_EPS = 1e-6
LR, B1, B2, ADAM_EPS, WD, STEP = 0.001, 0.9, 0.999, 1e-08, 0.01, 10

COL_Q, COL_KV, COL_CBX, COL_U, COL_G = 0, 512, 768, 2304, 2816
PIECE_W = (512, 256, 512, 512, 512, 512, 3072)
PIECE_OFF = tuple(sum(PIECE_W[:i]) for i in range(len(PIECE_W)))

KINDS = (("win_t", 736, 1024), ("wffn_t", 704, 1024), ("wout", 352, 1024), ("wmix", 128, 1024),
         ("branch_t", 128, 1536), ("wglu", 64, 512))

REPLICATED = ("norm_mix", "b_gate", "attn_sinks", "ssm_a_re", "ssm_a_im", "ssm_b_re", "ssm_b_im", "ssm_c_re",
              "ssm_c_im", "ssm_d", "ssm_log_dt", "norm_ffn")

VMEM_LIMIT = 56 * 1024 * 1024
NT = (((1,), (1,)), ((), ()))
TN = (((0,), (0,)), ((), ()))
MESH_ID = pl.DeviceIdType.MESH
ANY = pl.BlockSpec(memory_space=pl.ANY)
HBM = pl.BlockSpec(memory_space=pltpu.HBM)
SEM = pl.BlockSpec(memory_space=pltpu.SEMAPHORE)
EFFECT = pltpu.SideEffectType.DATAFLOW_SIDE_EFFECTING


def _cp(**kw):
    return pltpu.CompilerParams(vmem_limit_bytes=VMEM_LIMIT, **kw)


def _full(shape):
    return pl.BlockSpec(shape, lambda *_: (0,) * len(shape))


def _resident(shape):
    return pl.BlockSpec(shape, lambda *_: (0,) * len(shape), pipeline_mode=pl.Buffered(1))


def _mm_tn(a, b, *, tm, tn, name):
    k, m = a.shape
    n = b.shape[1]

    def body(a_ref, b_ref, o_ref):
        o_ref[...] = lax.dot_general(a_ref[...].astype(BF16), b_ref[...].astype(BF16), TN,
                                     preferred_element_type=F32).astype(BF16)

    return pl.pallas_call(
        body, grid=(m // tm, n // tn),
        in_specs=[pl.BlockSpec((k, tm), lambda i, j: (0, i)), pl.BlockSpec((k, tn), lambda i, j: (0, j))],
        out_specs=pl.BlockSpec((tm, tn), lambda i, j: (i, j)),
        out_shape=jax.ShapeDtypeStruct((m, n), BF16), compiler_params=_cp(), name=name)(a, b)


def _rms_rows(xv, g):
    r = lax.rsqrt(jnp.mean(xv * xv, axis=-1, keepdims=True) + NORM_EPS)
    return ((xv * r) * g).astype(BF16)


def _rms_mm_in(x, g, wt, tabs, cw, layer, tie):
    tt = 512
    widths = (3 * WIDTH, WIDTH, GATE_W)
    offs = (COL_CBX, COL_U, COL_G)

    def body(x_ref, g_ref, w_ref, tc_ref, ta_ref, tb_ref, cw_ref, tie_ref,
             q_ref, kv_ref, cbx_ref, u_ref, gl_ref, cv_ref, h_ref, tail_ref):
        @pl.when(pl.program_id(0) == 0)
        def _():
            tail_ref[...] = jnp.zeros_like(tail_ref)

        h = _rms_rows(x_ref[...], g_ref[...])
        h_ref[...] = h
        prod = lax.dot_general(h, w_ref[...], NT, preferred_element_type=F32)
        for ref, o, w in zip((cbx_ref, u_ref, gl_ref), offs, widths):
            ref[...] = prod[:, o:o + w]
        c, a, b = tc_ref[...], ta_ref[...], tb_ref[...]
        for j in range(ATTN_W // 128):
            q_ref[:, 128 * j:128 * (j + 1)] = _rope(prod[:, 128 * j:128 * (j + 1)], c, a, b) * (HEAD_DIM ** -0.5)
        kv_ref[:, :KV_W] = _rope(prod[:, COL_KV:COL_KV + KV_W], c, a, b)
        kv_ref[:, KV_W:] = prod[:, COL_KV + KV_W:COL_CBX]

        row = lax.broadcasted_iota(jnp.int32, (tt, 128), 0)
        for j in range(WIDTH // 128):
            cols = slice(128 * j, 128 * (j + 1))
            cb = prod[:, COL_CBX + 128 * j:COL_CBX + 128 * (j + 1)]
            z = prod[:, COL_CBX + WIDTH + 128 * j:COL_CBX + WIDTH + 128 * (j + 1)] \
                * prod[:, COL_CBX + 2 * WIDTH + 128 * j:COL_CBX + 2 * WIDTH + 128 * (j + 1)]
            before1, before2 = tail_ref[7:8, cols], tail_ref[6:7, cols]
            z1 = jnp.where(row == 0, before1, pltpu.roll(z, 1, axis=0))
            z2 = jnp.where(row == 0, before2, jnp.where(row == 1, before1, pltpu.roll(z, 2, axis=0)))
            s = cw_ref[0:1, cols] * z2 + cw_ref[1:2, cols] * z1 + cw_ref[2:3, cols] * z
            cv_ref[:, cols] = (cb * s).astype(BF16)
            tail_ref[:, cols] = z[tt - 8:, :]

    row_spec = lambda w: pl.BlockSpec((tt, w), lambda i: (i, 0))
    sds = jax.ShapeDtypeStruct
    return pl.pallas_call(
        body, grid=(SEQ // tt,),
        in_specs=[row_spec(D_MODEL), _full((1, D_MODEL)), _resident((IN_COLS, D_MODEL)),
                  row_spec(128), row_spec(128), row_spec(128),
                  pl.BlockSpec((None, 8, WIDTH), lambda i: (layer, 0, 0)), ANY],
        out_specs=[row_spec(ATTN_W), row_spec(2 * KV_W), row_spec(3 * WIDTH), row_spec(WIDTH), row_spec(GATE_W),
                   row_spec(WIDTH), row_spec(D_MODEL)],
        out_shape=[sds((SEQ, ATTN_W), F32), sds((SEQ, 2 * KV_W), F32), sds((SEQ, 3 * WIDTH), F32),
                   sds((SEQ, WIDTH), F32), sds((SEQ, GATE_W), F32), sds((SEQ, WIDTH), BF16),
                   sds((SEQ, D_MODEL), BF16)],
        scratch_shapes=[pltpu.VMEM((8, WIDTH), F32)], compiler_params=_cp(), name="rms_mm_in",
    )(x, g, wt, *tabs, cw, tie)


def _rms_mm_ffn(x, g, wt):
    tt = 256

    def body(x_ref, g_ref, w_ref, act_ref, up_ref, silu_ref, dsilu_ref, h_ref):
        h = _rms_rows(x_ref[...], g_ref[...])
        h_ref[...] = h
        prod = lax.dot_general(h, w_ref[...], NT, preferred_element_type=F32)
        gt, up = prod[:, :FFN_H], prod[:, FFN_H:]
        sg = jax.nn.sigmoid(gt)
        silu = gt * sg
        act_ref[...] = (silu * up).astype(BF16)
        up_ref[...] = up.astype(BF16)
        silu_ref[...] = silu.astype(BF16)
        dsilu_ref[...] = (sg + silu * (1.0 - sg)).astype(BF16)

    row = lambda w: pl.BlockSpec((tt, w), lambda i: (i, 0))
    return pl.pallas_call(
        body, grid=(SEQ // tt,), in_specs=[row(D_MODEL), _full((1, D_MODEL)), _resident((2 * FFN_H, D_MODEL))],
        out_specs=[row(FFN_H)] * 4 + [row(D_MODEL)],
        out_shape=[jax.ShapeDtypeStruct((SEQ, FFN_H), BF16)] * 4 + [jax.ShapeDtypeStruct((SEQ, D_MODEL), BF16)],
        compiler_params=_cp(), name="rms_mm_ffn")(x, g, wt)


def _mm_rmsbwd(pieces, wt, x, g, dres, name):
    tt = 512
    widths = [p.shape[1] for p in pieces]
    offs = [sum(widths[:i]) for i in range(len(widths))]
    n = len(pieces)

    def body(*refs):
        p_refs, (w_ref, x_ref, g_ref, r_ref, dx_ref, dg_ref) = refs[:n], refs[n:]

        @pl.when(pl.program_id(0) == 0)
        def _():
            dg_ref[...] = jnp.zeros_like(dg_ref)

        dh = jnp.zeros((tt, D_MODEL), F32)
        for p_ref, o, w in zip(p_refs, offs, widths):
            dh += jnp.dot(p_ref[...], w_ref[o:o + w, :], preferred_element_type=F32)
        xv = x_ref[...]
        r = lax.rsqrt(jnp.mean(xv * xv, axis=-1, keepdims=True) + NORM_EPS)
        xh = xv * r
        gy = dh * g_ref[...]
        dx_ref[...] = r_ref[...] + r * (gy - xh * jnp.mean(gy * xh, axis=-1, keepdims=True))
        dg_ref[...] += jnp.sum(dh * xh, axis=0, keepdims=True)

    row = lambda w: pl.BlockSpec((tt, w), lambda i: (i, 0))
    return pl.pallas_call(
        body, grid=(SEQ // tt,),
        in_specs=[row(w) for w in widths] + [_resident(wt.shape), row(D_MODEL), _full((1, D_MODEL)), row(D_MODEL)],
        out_specs=[row(D_MODEL), _full((1, D_MODEL))],
        out_shape=[jax.ShapeDtypeStruct((SEQ, D_MODEL), F32), jax.ShapeDtypeStruct((1, D_MODEL), F32)],
        compiler_params=_cp(), name=name)(*pieces, wt, x, g, dres)


def _tn_pieces(pieces, h):
    tk, tn = 512, 512
    nk = SEQ // tk
    n = len(pieces)

    def body(*refs):
        p_refs, (h_ref, o_ref, acc_ref) = refs[:n], refs[n:]
        kk = pl.program_id(1)

        @pl.when(kk == 0)
        def _():
            acc_ref[...] = jnp.zeros_like(acc_ref)

        hv = h_ref[...]
        for p_ref, o, w in zip(p_refs, PIECE_OFF, PIECE_W):
            acc_ref[o:o + w, :] += lax.dot_general(p_ref[...], hv, TN, preferred_element_type=F32)

        @pl.when(kk == nk - 1)
        def _():
            o_ref[...] = acc_ref[...].astype(BF16)

    return pl.pallas_call(
        body, grid=(D_MODEL // tn, nk),
        in_specs=[pl.BlockSpec((tk, w), lambda j, kk: (kk, 0)) for w in PIECE_W]
        + [pl.BlockSpec((tk, tn), lambda j, kk: (kk, j))],
        out_specs=pl.BlockSpec((IN_COLS, tn), lambda j, kk: (0, j)),
        out_shape=jax.ShapeDtypeStruct((IN_COLS, D_MODEL), BF16),
        scratch_shapes=[pltpu.VMEM((IN_COLS, tn), F32)], compiler_params=_cp(), name="tn_pieces")(*pieces, h)


def _tn_branches(dys, acts):
    tk = 512
    nk = SEQ // tk

    def body(d0, d1, d2, a0, a1, a2, o_ref, acc_ref):
        kk = pl.program_id(0)

        @pl.when(kk == 0)
        def _():
            acc_ref[...] = jnp.zeros_like(acc_ref)

        for j, (d, a) in enumerate(((d0, a0), (d1, a1), (d2, a2))):
            acc_ref[:, WIDTH * j:WIDTH * (j + 1)] += lax.dot_general(d[...], a[...], TN, preferred_element_type=F32)

        @pl.when(kk == nk - 1)
        def _():
            o_ref[...] = acc_ref[...].astype(BF16)

    row = lambda w: pl.BlockSpec((tk, w), lambda kk: (kk, 0))
    return pl.pallas_call(
        body, grid=(nk,), in_specs=[row(D_MODEL)] * 3 + [row(WIDTH)] * 3,
        out_specs=_full((D_MODEL, 3 * WIDTH)), out_shape=jax.ShapeDtypeStruct((D_MODEL, 3 * WIDTH), BF16),
        scratch_shapes=[pltpu.VMEM((D_MODEL, 3 * WIDTH), F32)], compiler_params=_cp(), name="tn_branches",
    )(*dys, *acts)


def _rope(t, c, a, b):
    return t * c + pltpu.roll(t, 120, axis=1) * a + pltpu.roll(t, 8, axis=1) * b


def _rope_t(d, c, a, b):
    return d * c + pltpu.roll(d * a, 8, axis=1) + pltpu.roll(d * b, 120, axis=1)


def _band_sides(band):
    left = lax.broadcasted_iota(jnp.int32, band.shape, 1) < HEAD_DIM
    h0 = jnp.where(left, band, 0.0)
    h1 = jnp.where(left, 0.0, band)
    r0 = pltpu.roll(h0, HEAD_DIM, axis=1)
    r1 = pltpu.roll(h1, HEAD_DIM, axis=1)
    return ((h0.astype(BF16), r0.astype(BF16)), (r1.astype(BF16), h1.astype(BF16)))


def _attn_mask(i):
    qi = lax.broadcasted_iota(jnp.int32, (2 * BLOCK, 2 * BLOCK), 0) % BLOCK
    kj = lax.broadcasted_iota(jnp.int32, (2 * BLOCK, 2 * BLOCK), 1)
    delta = qi + BLOCK - kj
    return (delta >= 0) & (delta < BLOCK) & ((kj >= BLOCK) | (i > 0))


def _attn_probs(s, ok, sink):
    s = jnp.where(ok, s, NEG_INF)
    m = jnp.maximum(jnp.max(s, axis=-1, keepdims=True), sink)
    p = jnp.exp(s - m)
    es = jnp.exp(sink - m)
    inv = 1.0 / (jnp.sum(p, axis=-1, keepdims=True) + es)
    return p * inv, es * inv


def _kv_group(qs, ks, vs, kh, sink_ref):
    q2 = jnp.concatenate([qs[2 * kh], qs[2 * kh + 1]], axis=0)
    kst = jnp.concatenate([ks[kh][0], ks[kh][1]], axis=0)
    vst = jnp.concatenate([vs[kh][0], vs[kh][1]], axis=0)
    top = lax.broadcasted_iota(jnp.int32, (2 * BLOCK, 1), 0) < BLOCK
    sinks = [jnp.where(top, sink_ref[0, 4 * kh + h], sink_ref[0, 4 * kh + 2 + h]) for h in range(2)]
    return q2, kst, vst, sinks


def _attn_load(q_ref, kvc_ref, kvp_ref, tc_ref, ta_ref, tb_ref, pc_ref, pa_ref, pb_ref):
    c, a, b = tc_ref[...], ta_ref[...], tb_ref[...]
    kband = jnp.concatenate([kvp_ref[:, :KV_W], kvc_ref[:, :KV_W]], axis=0)
    vband = jnp.concatenate([kvp_ref[:, KV_W:], kvc_ref[:, KV_W:]], axis=0)
    qs = [q_ref[:, 128 * j:128 * (j + 1)].astype(BF16) for j in range(4)]
    return qs, _band_sides(kband), _band_sides(vband), (c, a, b)


def _attn_specs(clamp):
    cur = lambda i: (clamp(i), 0)
    prev = lambda i: (jnp.maximum(clamp(i) - 1, 0), 0)
    return [
        pl.BlockSpec((BLOCK, ATTN_W), cur), pl.BlockSpec((BLOCK, 2 * KV_W), cur),
        pl.BlockSpec((BLOCK, 2 * KV_W), prev),
        pl.BlockSpec((BLOCK, 128), cur), pl.BlockSpec((BLOCK, 128), cur), pl.BlockSpec((BLOCK, 128), cur),
        pl.BlockSpec((BLOCK, 128), prev), pl.BlockSpec((BLOCK, 128), prev), pl.BlockSpec((BLOCK, 128), prev),
        pl.BlockSpec(memory_space=pltpu.SMEM),
    ]


def _attn_fwd(q, kv, tabs, sinks):
    tc, ta, tb = tabs

    def body(q_ref, kvc_ref, kvp_ref, tc_ref, ta_ref, tb_ref, pc_ref, pa_ref, pb_ref, sink_ref, o_ref):
        i = pl.program_id(0)
        qs, ks, vs, _ = _attn_load(q_ref, kvc_ref, kvp_ref, tc_ref, ta_ref, tb_ref, pc_ref, pa_ref, pb_ref)
        ok = _attn_mask(i)
        for kh in range(2):
            q2, kst, vst, sinks = _kv_group(qs, ks, vs, kh, sink_ref)
            s = lax.dot_general(q2, kst, NT, preferred_element_type=F32)
            pn = [_attn_probs(s[:, 2 * BLOCK * h:2 * BLOCK * (h + 1)], ok, sinks[h])[0].astype(BF16) for h in range(2)]
            o2 = jnp.dot(jnp.concatenate(pn, axis=1), vst, preferred_element_type=F32).astype(BF16)
            for r in range(2):
                j = 2 * kh + r
                o_ref[:, 128 * j:128 * (j + 1)] = o2[BLOCK * r:BLOCK * (r + 1)]

    return pl.pallas_call(
        body, grid=(N_BLOCKS,), in_specs=_attn_specs(lambda i: i),
        out_specs=pl.BlockSpec((BLOCK, ATTN_W), lambda i: (i, 0)),
        out_shape=jax.ShapeDtypeStruct((SEQ, ATTN_W), BF16), compiler_params=_cp(), name="attn_fwd",
    )(q, kv, kv, tc, ta, tb, tc, ta, tb, sinks)


def _attn_bwd(q, kv, tabs, sinks, do):
    tc, ta, tb = tabs
    last = N_BLOCKS - 1
    clamp = lambda i: jnp.minimum(i, last)

    def place(full, side, kh):
        left = lax.broadcasted_iota(jnp.int32, full.shape, 1) < HEAD_DIM
        valid = jnp.where(left, full, 0.0) if side == 0 else jnp.where(left, 0.0, full)
        return valid if side == kh else pltpu.roll(valid, HEAD_DIM, axis=1)

    def body(q_ref, kvc_ref, kvp_ref, tc_ref, ta_ref, tb_ref, pc_ref, pa_ref, pb_ref, sink_ref, do_ref,
             dq_ref, dkv_ref, ds_ref, carry_ref):
        i = pl.program_id(0)

        @pl.when(i == 0)
        def _():
            ds_ref[...] = jnp.zeros_like(ds_ref)
            carry_ref[...] = jnp.zeros_like(carry_ref)

        @pl.when(i > last)
        def _():
            dkv_ref[...] = carry_ref[...].astype(BF16)

        @pl.when(i <= last)
        def _():
            qs, ks, vs, (c, a, b) = _attn_load(q_ref, kvc_ref, kvp_ref, tc_ref, ta_ref, tb_ref,
                                               pc_ref, pa_ref, pb_ref)
            ok = _attn_mask(i)
            dk = jnp.zeros((2 * BLOCK, 128), F32)
            dv = jnp.zeros((2 * BLOCK, 128), F32)
            dsink = jnp.zeros((1, 128), F32)
            lane = lax.broadcasted_iota(jnp.int32, (1, 128), 1)
            for kh in range(2):
                q2, kst, vst, sinks = _kv_group(qs, ks, vs, kh, sink_ref)
                do2 = jnp.concatenate([do_ref[:, 128 * (2 * kh + r):128 * (2 * kh + r + 1)] for r in range(2)],
                                      axis=0).astype(BF16)
                s = lax.dot_general(q2, kst, NT, preferred_element_type=F32)
                dp = lax.dot_general(do2, vst, NT, preferred_element_type=F32)
                pns, dss = [], []
                for h in range(2):
                    cols = slice(2 * BLOCK * h, 2 * BLOCK * (h + 1))
                    pn, ps = _attn_probs(s[:, cols], ok, sinks[h])
                    dr = jnp.sum(pn * dp[:, cols], axis=-1, keepdims=True)
                    pns.append(pn.astype(BF16))
                    dss.append((pn * (dp[:, cols] - dr)).astype(BF16))
                    for r in range(2):
                        part = -jnp.sum((ps * dr)[BLOCK * r:BLOCK * (r + 1)])
                        dsink += jnp.where(lane == 4 * kh + 2 * r + h, part, 0.0)
                ds2, pn2 = jnp.concatenate(dss, axis=1), jnp.concatenate(pns, axis=1)
                dq2 = jnp.dot(ds2, kst, preferred_element_type=F32) * (HEAD_DIM ** -0.5)
                dk2 = lax.dot_general(ds2, q2, TN, preferred_element_type=F32)
                dv2 = lax.dot_general(pn2, do2, TN, preferred_element_type=F32)
                for h in range(2):
                    dk += place(dk2[2 * BLOCK * h:2 * BLOCK * (h + 1)], h, kh)
                    dv += place(dv2[2 * BLOCK * h:2 * BLOCK * (h + 1)], h, kh)
                for r in range(2):
                    j = 2 * kh + r
                    dq_ref[:, 128 * j:128 * (j + 1)] = _rope_t(dq2[BLOCK * r:BLOCK * (r + 1)], c, a, b).astype(BF16)
            ds_ref[...] += dsink
            dk_prev = _rope_t(dk[:BLOCK], pc_ref[...], pa_ref[...], pb_ref[...])
            dk_cur = _rope_t(dk[BLOCK:], c, a, b)
            prev = jnp.concatenate([dk_prev, dv[:BLOCK]], axis=1)
            dkv_ref[...] = (carry_ref[...] + prev).astype(BF16)
            carry_ref[...] = jnp.concatenate([dk_cur, dv[BLOCK:]], axis=1)

    return pl.pallas_call(
        body, grid=(N_BLOCKS + 1,),
        in_specs=_attn_specs(clamp) + [pl.BlockSpec((BLOCK, ATTN_W), lambda i: (clamp(i), 0))],
        out_specs=[pl.BlockSpec((BLOCK, ATTN_W), lambda i: (clamp(i), 0)),
                   pl.BlockSpec((BLOCK, 2 * KV_W), lambda i: (jnp.maximum(i - 1, 0), 0)),
                   pl.BlockSpec((1, 128), lambda i: (0, 0))],
        out_shape=[jax.ShapeDtypeStruct((SEQ, ATTN_W), BF16), jax.ShapeDtypeStruct((SEQ, 2 * KV_W), BF16),
                   jax.ShapeDtypeStruct((1, 128), F32)],
        scratch_shapes=[pltpu.VMEM((BLOCK, 2 * KV_W), F32)], compiler_params=_cp(), name="attn_bwd",
    )(q, kv, kv, tc, ta, tb, tc, ta, tb, sinks, do)


def _shift_down(z, k):
    row = lax.broadcasted_iota(jnp.int32, z.shape, 0)
    return jnp.where(row < k, 0.0, pltpu.roll(z, k, axis=0))


def _shift_up(z, k):
    n = z.shape[0]
    row = lax.broadcasted_iota(jnp.int32, z.shape, 0)
    return jnp.where(row >= n - k, 0.0, pltpu.roll(z, n - k, axis=0))


def _conv_specs():
    nb = WIDTH // 128
    return [pl.BlockSpec((SEQ, 128), lambda j: (0, j)), pl.BlockSpec((SEQ, 128), lambda j: (0, nb + j)),
            pl.BlockSpec((SEQ, 128), lambda j: (0, 2 * nb + j)), pl.BlockSpec((None, 8, 128), lambda j: (0, 0, j))]


def _conv_bwd(cbx, cw, layer, dout, tie):
    def body(cb_ref, cc_ref, cx_ref, w_ref, do_ref, tie_ref, dcb_ref, dcc_ref, dcx_ref, dw_ref):
        cc, cx = cc_ref[...], cx_ref[...]
        z = cc * cx
        z1, z2 = _shift_down(z, 1), _shift_down(z, 2)
        w0, w1, w2 = w_ref[0:1, :], w_ref[1:2, :], w_ref[2:3, :]
        dout = do_ref[...]
        ds = dout * cb_ref[...]
        dcb_ref[...] = (dout * (w0 * z2 + w1 * z1 + w2 * z)).astype(BF16)
        dz = w2 * ds + w1 * _shift_up(ds, 1) + w0 * _shift_up(ds, 2)
        dcc_ref[...] = (dz * cx).astype(BF16)
        dcx_ref[...] = (dz * cc).astype(BF16)
        rows = [jnp.sum(ds * zz, axis=0, keepdims=True) for zz in (z2, z1, z)]
        dw_ref[...] = jnp.concatenate(rows + [jnp.zeros((5, 128), F32)], axis=0)

    col = lambda j: (0, j)
    specs = _conv_specs()
    specs[3] = pl.BlockSpec((None, 8, 128), lambda j: (layer, 0, j))
    return pl.pallas_call(
        body, grid=(WIDTH // 128,), in_specs=specs + [pl.BlockSpec((SEQ, 128), col), ANY],
        out_specs=[pl.BlockSpec((SEQ, 128), col), pl.BlockSpec((SEQ, 128), col), pl.BlockSpec((SEQ, 128), col),
                   pl.BlockSpec((8, 128), col)],
        out_shape=[jax.ShapeDtypeStruct((SEQ, WIDTH), BF16)] * 3 + [jax.ShapeDtypeStruct((8, WIDTH), F32)],
        compiler_params=_cp(), name="conv_bwd",
    )(cbx, cbx, cbx, cw, dout, tie)


def _ssm_prep_math(a_re, a_im, log_dt, bt_re, bt_im):
    dt = jnp.exp(log_dt)
    er = jnp.exp(a_re * dt)
    lr = er * jnp.cos(a_im * dt)
    li = er * jnp.sin(a_im * dt)
    n2 = a_re * a_re + a_im * a_im
    cr = ((lr - 1.0) * a_re + li * a_im) / n2
    ci = (li * a_re - (lr - 1.0) * a_im) / n2
    cr3, ci3 = cr[:, None, :], ci[:, None, :]
    return lr, li, cr3 * bt_re - ci3 * bt_im, cr3 * bt_im + ci3 * bt_re


_GS = (SSM_GROUPS, SSM_STATE)
_GHS = (SSM_GROUPS, SSM_GROUP, SSM_STATE)


def _layered(shape):
    return pl.BlockSpec((None,) + shape, lambda l: (l,) + (0,) * len(shape))


def _ssm_prep(a_re, a_im, log_dt, bt_re, bt_im):
    def body(ar, ai, ld, br, bi, o0, o1, o2, o3):
        outs = _ssm_prep_math(ar[...], ai[...], ld[...], br[...], bi[...])
        for o, v in zip((o0, o1, o2, o3), outs):
            o[...] = v

    shapes = [_GS, _GS, _GHS, _GHS]
    return pl.pallas_call(
        body, grid=(DEPTH,), in_specs=[_layered(s) for s in (_GS, _GS, (SSM_GROUPS, 1), _GHS, _GHS)],
        out_specs=[_layered(s) for s in shapes],
        out_shape=[jax.ShapeDtypeStruct((DEPTH,) + s, F32) for s in shapes],
        name="ssm_prep")(a_re, a_im, log_dt, bt_re, bt_im)


def _ssm_prep_bwd(a_re, a_im, log_dt, bt_re, bt_im, cots):
    def body(ar, ai, ld, br, bi, c0, c1, c2, c3, o0, o1, o2, o3, o4):
        _, vjp = jax.vjp(_ssm_prep_math, ar[...], ai[...], ld[...], br[...], bi[...])
        for o, v in zip((o0, o1, o2, o3, o4), vjp((c0[...], c1[...], c2[...], c3[...]))):
            o[...] = v

    ins = (_GS, _GS, (SSM_GROUPS, 1), _GHS, _GHS)
    return pl.pallas_call(
        body, grid=(DEPTH,), in_specs=[_layered(s) for s in ins + (_GS, _GS, _GHS, _GHS)],
        out_specs=[_layered(s) for s in ins],
        out_shape=[jax.ShapeDtypeStruct((DEPTH,) + s, F32) for s in ins],
        name="ssm_prep_bwd")(a_re, a_im, log_dt, bt_re, bt_im, *cots)


LANES_G = 512
N_LANE_GROUPS = SSM_GROUPS * SSM_STATE // LANES_G


def _ssm_embed(b_re, b_im, c_re, c_im):
    rows = SSM_GROUPS * SSM_GROUP

    def body(br, bi, cr, ci, o0, o1, o2, o3):
        state = lax.broadcasted_iota(jnp.int32, (SSM_STATE, LANES_G), 0)
        lane = lax.broadcasted_iota(jnp.int32, (SSM_STATE, LANES_G), 1)
        spread = (lane % SSM_STATE == state).astype(BF16)
        r = lax.broadcasted_iota(jnp.int32, (rows, LANES_G), 0)
        c = lax.broadcasted_iota(jnp.int32, (rows, LANES_G), 1)
        own = (r % 128) // SSM_GROUP == c // SSM_STATE
        for ref, o, sign in ((br, o0, 1.0), (bi, o1, 1.0), (cr, o2, 1.0), (ci, o3, -1.0)):
            t = (sign * ref[...]).reshape(rows, SSM_STATE).astype(BF16)
            wide = jnp.dot(t, spread, preferred_element_type=F32)
            o[...] = jnp.where(own, wide, 0.0).astype(BF16).reshape(N_LANE_GROUPS, 128, LANES_G)

    out = (N_LANE_GROUPS, 128, LANES_G)
    return pl.pallas_call(
        body, grid=(DEPTH,), in_specs=[_layered(_GHS)] * 4, out_specs=[_layered(out)] * 4,
        out_shape=[jax.ShapeDtypeStruct((DEPTH,) + out, BF16)] * 4, name="ssm_embed")(b_re, b_im, c_re, c_im)


def _scan_in_place(xr_ref, xi_ref, ar, ai, reverse):
    shape = (N_CHUNKS, xr_ref.shape[1])
    ar, ai = jnp.broadcast_to(ar, shape), jnp.broadcast_to(ai, shape)

    def rows(tau):
        t = (CHUNK - 1 - tau) if reverse else tau
        return pl.ds(pl.multiple_of(t * N_CHUNKS, N_CHUNKS), N_CHUNKS)

    def step(tau, carry):
        sr, si = carry
        return ar * sr - ai * si + xr_ref[rows(tau), :], ar * si + ai * sr + xi_ref[rows(tau), :]

    zero = jnp.zeros(shape, F32)
    er, ei = lax.fori_loop(0, CHUNK, step, (zero, zero), unroll=8)
    qr, qi = ar, ai
    for _ in range(8):
        qr, qi = qr * qr - qi * qi, 2.0 * qr * qi
    shift = _shift_up if reverse else _shift_down
    for k in (1, 2, 4):
        sr, si = shift(er, k), shift(ei, k)
        er, ei = er + qr * sr - qi * si, ei + qr * si + qi * sr
        qr, qi = qr * qr - qi * qi, 2.0 * qr * qi
    start = (shift(er, 1), shift(ei, 1))

    def write(tau, carry):
        sr, si = step(tau, carry)
        xr_ref[rows(tau), :] = sr
        xi_ref[rows(tau), :] = si
        return sr, si

    return write, start


def _ssm_specs(layer):
    col = lambda w: pl.BlockSpec((SEQ, w), lambda g: (0, g))
    diag = pl.BlockSpec((None, None, 128, LANES_G), lambda g: (layer, g, 0, 0))
    vec = pl.BlockSpec((None, 1, LANES_G), lambda g: (layer, 0, g))
    return col, diag, vec


def _to_scan_order(src_ref, dst_ref):
    for tau in range(CHUNK):
        dst_ref[pl.ds(tau * N_CHUNKS, N_CHUNKS), :] = src_ref[pl.ds(tau, N_CHUNKS, stride=CHUNK), :]


def _to_time_order(src_ref, dst_ref, dtype):
    for j in range(N_CHUNKS):
        dst_ref[pl.ds(j * CHUNK, CHUNK), :] = src_ref[pl.ds(j, CHUNK, stride=N_CHUNKS), :].astype(dtype)


def _ssm_fwd(u, mats, layer, d):
    def body(u_ref, d_ref, br_ref, bi_ref, cr_ref, ci_ref, ar_ref, ai_ref, xr_ref, xi_ref, y_ref, us_ref):
        _to_scan_order(u_ref, us_ref)
        uv = us_ref[...].astype(BF16)
        xr_ref[...] = jnp.dot(uv, br_ref[...], preferred_element_type=F32)
        xi_ref[...] = jnp.dot(uv, bi_ref[...], preferred_element_type=F32)
        write, start = _scan_in_place(xr_ref, xi_ref, ar_ref[...], ai_ref[...], False)
        lax.fori_loop(0, CHUNK, write, start, unroll=8)
        y = lax.dot_general(xr_ref[...].astype(BF16), cr_ref[...], NT, preferred_element_type=F32)
        y += lax.dot_general(xi_ref[...].astype(BF16), ci_ref[...], NT, preferred_element_type=F32)
        us_ref[...] = y + d_ref[...] * us_ref[...]
        _to_time_order(us_ref, y_ref, F32)

    col, diag, vec = _ssm_specs(layer)
    return pl.pallas_call(
        body, grid=(N_LANE_GROUPS,),
        in_specs=[col(128), pl.BlockSpec((None, 1, 128), lambda g: (layer, 0, g)),
                  diag, diag, diag, diag, vec, vec],
        out_specs=[col(LANES_G), col(LANES_G), col(128)],
        out_shape=[jax.ShapeDtypeStruct((SEQ, SSM_GROUPS * SSM_STATE), F32)] * 2
        + [jax.ShapeDtypeStruct((SEQ, WIDTH), F32)],
        scratch_shapes=[pltpu.VMEM((SEQ, 128), F32)], compiler_params=_cp(), name="ssm_fwd",
    )(u, d, mats["b_re"], mats["b_im"], mats["c_re"], mats["c_im_neg"], mats["a_re"], mats["a_im"])


def _ssm_bwd(dy, x_re, x_im, u, mats, layer, d):
    def body(dyt_ref, ut_ref, d_ref, xr_ref, xi_ref, br_ref, bi_ref, cr_ref, ci_ref, ar_ref, ai_ref,
             du_ref, dar_ref, dai_ref, dbr_ref, dbi_ref, dcr_ref, dci_ref, lr_ref, li_ref, dys_ref, u_ref):
        _to_scan_order(dyt_ref, dys_ref)
        _to_scan_order(ut_ref, u_ref)
        dy = dys_ref[...].astype(BF16)
        lr_ref[...] = jnp.dot(dy, cr_ref[...], preferred_element_type=F32)
        li_ref[...] = jnp.dot(dy, ci_ref[...], preferred_element_type=F32)
        write, start = _scan_in_place(lr_ref, li_ref, ar_ref[...], -ai_ref[...], True)

        def rows(t):
            return pl.ds(pl.multiple_of(t * N_CHUNKS, N_CHUNKS), N_CHUNKS)

        def grad(acc, lam, xpr, xpi):
            return acc[0] + xpr * lam[0] + xpi * lam[1], acc[1] + xpr * lam[1] - xpi * lam[0]

        def down(tau, carry):
            lam = write(tau, carry[0])
            t = CHUNK - 2 - tau
            return lam, grad(carry[1], lam, xr_ref[rows(t), :], xi_ref[rows(t), :])

        zero = jnp.zeros((N_CHUNKS, LANES_G), F32)
        lam, acc = lax.fori_loop(0, CHUNK - 1, down, (start, (zero, zero)), unroll=5)
        lam = write(CHUNK - 1, lam)
        last = rows(CHUNK - 1)
        acc = grad(acc, lam, _shift_down(xr_ref[last, :], 1), _shift_down(xi_ref[last, :], 1))
        dar_ref[...] = jnp.sum(acc[0], axis=0, keepdims=True)
        dai_ref[...] = jnp.sum(acc[1], axis=0, keepdims=True)

        l_re, l_im = lr_ref[...].astype(BF16), li_ref[...].astype(BF16)
        du = lax.dot_general(l_re, br_ref[...], NT, preferred_element_type=F32)
        du += lax.dot_general(l_im, bi_ref[...], NT, preferred_element_type=F32)
        dys_ref[...] = du + dys_ref[...] * d_ref[...]
        _to_time_order(dys_ref, du_ref, BF16)
        uv = u_ref[...].astype(BF16)
        dbr_ref[...] = lax.dot_general(uv, l_re, TN, preferred_element_type=F32)
        dbi_ref[...] = lax.dot_general(uv, l_im, TN, preferred_element_type=F32)
        dcr_ref[...] = lax.dot_general(dy, xr_ref[...].astype(BF16), TN, preferred_element_type=F32)
        dci_ref[...] = lax.dot_general(dy, xi_ref[...].astype(BF16), TN, preferred_element_type=F32)

    col, diag, vec = _ssm_specs(layer)
    out_vec = pl.BlockSpec((1, LANES_G), lambda g: (0, g))
    out_blk = pl.BlockSpec((None, 128, LANES_G), lambda g: (g, 0, 0))
    sds = jax.ShapeDtypeStruct
    return pl.pallas_call(
        body, grid=(N_LANE_GROUPS,),
        in_specs=[col(128), col(128), pl.BlockSpec((None, 1, 128), lambda g: (layer, 0, g)),
                  col(LANES_G), col(LANES_G), diag, diag, diag, diag, vec, vec],
        out_specs=[col(128), out_vec, out_vec, out_blk, out_blk, out_blk, out_blk],
        out_shape=[sds((SEQ, WIDTH), BF16)] + [sds((1, SSM_GROUPS * SSM_STATE), F32)] * 2
        + [sds((N_LANE_GROUPS, 128, LANES_G), F32)] * 4,
        scratch_shapes=[pltpu.VMEM((SEQ, LANES_G), F32)] * 2 + [pltpu.VMEM((SEQ, 128), F32)] * 2,
        compiler_params=_cp(), name="ssm_bwd",
    )(dy, u, d, x_re, x_im, mats["b_re"], mats["b_im"], mats["c_re"], mats["c_im_neg"],
      mats["a_re"], mats["a_im"])


_GELU_C = math.sqrt(2.0 / math.pi)


def _gelu(y):
    return 0.5 * y * (1.0 + jnp.tanh(_GELU_C * (y + 0.044715 * (y * y * y))))


def _glu_fwd(y, wglu):
    tt = 512

    def body(y_ref, w_ref, z_ref):
        ys = _gelu(y_ref[...])
        a = jnp.dot(ys.astype(BF16), w_ref[...], preferred_element_type=F32)
        z_ref[...] = (ys * jax.nn.sigmoid(a)).astype(BF16)

    blk = pl.BlockSpec((tt, WIDTH), lambda i: (i, 0))
    return pl.pallas_call(body, grid=(SEQ // tt,), in_specs=[blk, _full((WIDTH, WIDTH))], out_specs=blk,
                          out_shape=jax.ShapeDtypeStruct((SEQ, WIDTH), BF16), compiler_params=_cp(),
                          name="glu_fwd")(y, wglu)


def _glu_bwd(y, wglu, dz, u):
    tt = 512

    def body(y_ref, w_ref, dz_ref, u_ref, dy_ref, ys_ref, da_ref, dd_ref):
        @pl.when(pl.program_id(0) == 0)
        def _():
            dd_ref[...] = jnp.zeros_like(dd_ref)

        yv = y_ref[...]
        t = jnp.tanh(_GELU_C * (yv + 0.044715 * (yv * yv * yv)))
        ys = 0.5 * yv * (1.0 + t)
        ysb = ys.astype(BF16)
        sg = jax.nn.sigmoid(jnp.dot(ysb, w_ref[...], preferred_element_type=F32))
        dz = dz_ref[...].astype(F32)
        da = (dz * ys * sg * (1.0 - sg)).astype(BF16)
        dys = dz * sg + lax.dot_general(da, w_ref[...], NT, preferred_element_type=F32)
        dy = dys * (0.5 * (1.0 + t) + 0.5 * yv * (1.0 - t * t) * _GELU_C * (1.0 + 3 * 0.044715 * (yv * yv)))
        dy_ref[...] = dy
        ys_ref[...] = ysb
        da_ref[...] = da
        dd_ref[...] += jnp.sum(dy * u_ref[...], axis=0, keepdims=True)

    blk = pl.BlockSpec((tt, WIDTH), lambda i: (i, 0))
    return pl.pallas_call(
        body, grid=(SEQ // tt,), in_specs=[blk, _full((WIDTH, WIDTH)), blk, blk],
        out_specs=[blk, blk, blk, _full((1, WIDTH))],
        out_shape=[jax.ShapeDtypeStruct((SEQ, WIDTH), F32)] + [jax.ShapeDtypeStruct((SEQ, WIDTH), BF16)] * 2
        + [jax.ShapeDtypeStruct((1, WIDTH), F32)],
        compiler_params=_cp(), name="glu_bwd")(y, wglu, dz, u)


def _mix_specs(tt, layer):
    row = lambda w: pl.BlockSpec((tt, w), lambda i: (i, 0))
    gate = lambda j: pl.BlockSpec((tt, D_MODEL), lambda i: (i, j))
    wo = lambda j: pl.BlockSpec((D_MODEL, WIDTH), lambda i: (0, j))
    return [row(D_MODEL), row(WIDTH), row(WIDTH), row(WIDTH), gate(0), gate(1), gate(2),
            pl.BlockSpec((None, 1, GATE_W), lambda i: (layer, 0, 0)), wo(0), wo(1), wo(2),
            _full((D_MODEL, D_MODEL))]


def _mix_branches(o_ref, c_ref, z_ref, g_refs, b_ref, wa_ref, wc_ref, ws_ref):
    ys = [lax.dot_general(r[...], w[...], NT, preferred_element_type=F32)
          for r, w in ((o_ref, wa_ref), (c_ref, wc_ref), (z_ref, ws_ref))]
    gates = [jax.nn.sigmoid(g_refs[j][...] + b_ref[:, D_MODEL * j:D_MODEL * (j + 1)]) for j in range(3)]
    return ys, gates


def _mix_fwd(x, o, cv, z, glog, b_gate, layer, wbt, wmix, tie):
    tt = 256

    def body(x_ref, o_ref, c_ref, z_ref, g0, g1, g2, b_ref, wa_ref, wc_ref, ws_ref, wm_ref, tie_ref, x1_ref):
        ys, gates = _mix_branches(o_ref, c_ref, z_ref, (g0, g1, g2), b_ref, wa_ref, wc_ref, ws_ref)
        merged = gates[0] * ys[0] + gates[1] * ys[1] + gates[2] * ys[2]
        x1_ref[...] = x_ref[...] + jnp.dot(merged.astype(BF16), wm_ref[...], preferred_element_type=F32)

    return pl.pallas_call(
        body, grid=(SEQ // tt,), in_specs=_mix_specs(tt, layer) + [ANY],
        out_specs=pl.BlockSpec((tt, D_MODEL), lambda i: (i, 0)),
        out_shape=jax.ShapeDtypeStruct((SEQ, D_MODEL), F32), compiler_params=_cp(), name="mix_fwd",
    )(x, o, cv, z, glog, glog, glog, b_gate, wbt, wbt, wbt, wmix, tie)


def _mix_bwd(dx1, o, cv, z, glog, b_gate, layer, wbt, wmix, tie):
    tt = 256

    def body(dx_ref, o_ref, c_ref, z_ref, g0, g1, g2, b_ref, wa_ref, wc_ref, ws_ref, wm_ref, tie_ref,
             mg_ref, dya_ref, dyc_ref, dys_ref, do_ref, dc_ref, dz_ref, dgl_ref, db_ref):
        @pl.when(pl.program_id(0) == 0)
        def _():
            db_ref[...] = jnp.zeros_like(db_ref)

        ys, gates = _mix_branches(o_ref, c_ref, z_ref, (g0, g1, g2), b_ref, wa_ref, wc_ref, ws_ref)
        mg_ref[...] = (gates[0] * ys[0] + gates[1] * ys[1] + gates[2] * ys[2]).astype(BF16)
        dm = lax.dot_general(dx_ref[...].astype(BF16), wm_ref[...], NT, preferred_element_type=F32)
        for j, (dy_ref, w_ref, d_ref) in enumerate(((dya_ref, wa_ref, do_ref), (dyc_ref, wc_ref, dc_ref),
                                                    (dys_ref, ws_ref, dz_ref))):
            dy = (dm * gates[j]).astype(BF16)
            dy_ref[...] = dy
            d_ref[...] = jnp.dot(dy, w_ref[...], preferred_element_type=F32)
            dgl = dm * ys[j] * gates[j] * (1.0 - gates[j])
            dgl_ref[:, D_MODEL * j:D_MODEL * (j + 1)] = dgl.astype(BF16)
            db_ref[:, D_MODEL * j:D_MODEL * (j + 1)] += jnp.sum(dgl, axis=0, keepdims=True)

    row = lambda w: pl.BlockSpec((tt, w), lambda i: (i, 0))
    sds = jax.ShapeDtypeStruct
    return pl.pallas_call(
        body, grid=(SEQ // tt,), in_specs=_mix_specs(tt, layer) + [ANY],
        out_specs=[row(D_MODEL)] * 4 + [row(WIDTH)] * 3 + [row(GATE_W), _full((1, GATE_W))],
        out_shape=[sds((SEQ, D_MODEL), BF16)] * 4 + [sds((SEQ, WIDTH), F32)] * 3
        + [sds((SEQ, GATE_W), BF16), sds((1, GATE_W), F32)],
        compiler_params=_cp(), name="mix_bwd",
    )(dx1, o, cv, z, glog, glog, glog, b_gate, wbt, wbt, wbt, wmix, tie)


def _ffn_out_fwd(x1, act, wout, tie):
    tt = 512

    def body(x_ref, a_ref, w_ref, tie_ref, o_ref):
        o_ref[...] = x_ref[...] + jnp.dot(a_ref[...], w_ref[...], preferred_element_type=F32)

    row = lambda w: pl.BlockSpec((tt, w), lambda i: (i, 0))
    return pl.pallas_call(
        body, grid=(SEQ // tt,), in_specs=[row(D_MODEL), row(FFN_H), _full((FFN_H, D_MODEL)), ANY],
        out_specs=row(D_MODEL), out_shape=jax.ShapeDtypeStruct((SEQ, D_MODEL), F32),
        compiler_params=_cp(), name="ffn_out_fwd")(x1, act, wout, tie)


def _ffn_out_bwd(dx2, up, silu, dsilu, wout, tie):
    tt = 512

    def body(dx_ref, up_ref, silu_ref, dsilu_ref, w_ref, tie_ref, dgu_ref):
        dact = lax.dot_general(dx_ref[...].astype(BF16), w_ref[...], NT, preferred_element_type=F32).astype(BF16)
        dgu_ref[:, :FFN_H] = dact * up_ref[...] * dsilu_ref[...]
        dgu_ref[:, FFN_H:] = dact * silu_ref[...]

    row = lambda w: pl.BlockSpec((tt, w), lambda i: (i, 0))
    return pl.pallas_call(
        body, grid=(SEQ // tt,),
        in_specs=[row(D_MODEL), row(FFN_H), row(FFN_H), row(FFN_H), _resident((FFN_H, D_MODEL)), ANY],
        out_specs=row(2 * FFN_H), out_shape=jax.ShapeDtypeStruct((SEQ, 2 * FFN_H), BF16),
        compiler_params=_cp(), name="ffn_out_bwd")(dx2, up, silu, dsilu, wout, tie)


def _loss_head(x, g, target):
    tt = 256

    def body(x_ref, g_ref, t_ref, loss_ref, dx_ref, dg_ref):
        @pl.when(pl.program_id(0) == 0)
        def _():
            loss_ref[...] = jnp.zeros_like(loss_ref)
            dg_ref[...] = jnp.zeros_like(dg_ref)

        xv = x_ref[...]
        r = lax.rsqrt(jnp.mean(xv * xv, axis=-1, keepdims=True) + NORM_EPS)
        xh = xv * r
        err = xh * g_ref[...] - t_ref[...]
        loss_ref[...] += 0.5 * jnp.sum(jnp.mean(err * err, axis=-1, keepdims=True))
        dy = err * (1.0 / D_MODEL)
        gy = dy * g_ref[...]
        dx_ref[...] = r * (gy - xh * jnp.mean(gy * xh, axis=-1, keepdims=True))
        dg_ref[...] += jnp.sum(dy * xh, axis=0, keepdims=True)

    row = pl.BlockSpec((tt, D_MODEL), lambda i: (i, 0))
    return pl.pallas_call(
        body, grid=(SEQ // tt,), in_specs=[row, _full((1, D_MODEL)), row],
        out_specs=[_full((1, 128)), row, _full((1, D_MODEL))],
        out_shape=[jax.ShapeDtypeStruct((1, 128), F32), jax.ShapeDtypeStruct((SEQ, D_MODEL), F32),
                   jax.ShapeDtypeStruct((1, D_MODEL), F32)],
        compiler_params=_cp(), name="loss_head")(x, g, target)


def _adam_math(g, w, m, v):
    nm = B1 * m + (1.0 - B1) * g
    nv = B2 * v + (1.0 - B2) * (g * g)
    m_hat = nm / (1.0 - B1 ** STEP)
    v_hat = nv / (1.0 - B2 ** STEP)
    return -LR * (m_hat / (jnp.sqrt(v_hat) + ADAM_EPS) + WD * w), nm, nv


def _adamw_small(parts, w, m, v, name):
    def body(p_ref, w_ref, m_ref, v_ref, g_ref, d_ref, nm_ref, nv_ref):
        g = p_ref[0].astype(F32)
        for k in range(1, N_DEV):
            g = g + p_ref[k].astype(F32)
        g_ref[...] = g
        d_ref[...], nm_ref[...], nv_ref[...] = _adam_math(g, w_ref[...], m_ref[...], v_ref[...])

    out_shape = [jax.ShapeDtypeStruct(w.shape, F32)] * 4
    if w.ndim < 3:
        return pl.pallas_call(body, out_shape=out_shape, name=name)(parts, w, m, v)
    rest = w.shape[1:]
    zeros = (0,) * len(rest)
    blk = pl.BlockSpec((None,) + rest, lambda l: (l,) + zeros)
    return pl.pallas_call(
        body, grid=(w.shape[0],),
        in_specs=[pl.BlockSpec((N_DEV, None) + rest, lambda l: (0, l) + zeros), blk, blk, blk],
        out_specs=[blk] * 4, out_shape=out_shape, name=name)(parts, w, m, v)


def _adamw(parts, w, m, v, tr, name, groups=None, fill=None, tie=None):
    n_groups, rows, cols = w.shape
    n_parts = parts.shape[1]
    lo, hi = groups if groups is not None else (0, n_groups)

    def body(p_ref, w_ref, m_ref, v_ref, *rest):
        g_ref, d_ref, nm_ref, nv_ref = rest[-4:]
        g = p_ref[0].astype(F32)
        for k in range(1, n_parts):
            g = g + p_ref[k].astype(F32)
        nm = B1 * m_ref[...] + (1.0 - B1) * g
        nv = B2 * v_ref[...] + (1.0 - B2) * (g * g)
        m_hat = nm / (1.0 - B1 ** STEP)
        v_hat = nv / (1.0 - B2 ** STEP)
        g_ref[...] = g
        d_ref[...] = -LR * (m_hat / (jnp.sqrt(v_hat) + ADAM_EPS) + WD * w_ref[...])
        nm_ref[...] = nm
        nv_ref[...] = nv

    blk = pl.BlockSpec((None, tr, cols), lambda l, i: (l + lo, i, 0))
    p_lo = lo if parts.shape[0] == n_groups else 0
    extra = ([] if fill is None else list(fill)) + ([] if tie is None else [tie])
    return pl.pallas_call(
        body, grid=(hi - lo, rows // tr),
        in_specs=[pl.BlockSpec((None, n_parts, tr, cols), lambda l, i: (l + p_lo, 0, i, 0)), blk, blk, blk]
        + [ANY] * len(extra),
        out_specs=[blk] * 4, out_shape=[jax.ShapeDtypeStruct((n_groups, rows, cols), F32)] * 4,
        input_output_aliases={} if fill is None else {4 + j: j for j in range(4)},
        compiler_params=_cp(), name=name)(parts, w, m, v, *extra)


BRANCHES = ("w_attn_o", "w_conv_o", "w_ssm_o")


def _adamw_branches(parts, wmv, name, groups, fill=None, tie=None):
    n_parts = parts.shape[1]
    lo, hi = groups

    def body(p_ref, *refs):
        ins, outs = refs[:9], refs[-12:]
        g = p_ref[0].astype(F32)
        for k in range(1, n_parts):
            g = g + p_ref[k].astype(F32)
        for j in range(3):
            gj = g[:, j * WIDTH:(j + 1) * WIDTH].T
            w_ref, m_ref, v_ref = ins[3 * j:3 * j + 3]
            d, nm, nv = _adam_math(gj, w_ref[...], m_ref[...], v_ref[...])
            for o, val in zip(outs[4 * j:4 * j + 4], (gj, d, nm, nv)):
                o[...] = val

    shard = wmv[0].shape[1:]
    blk = pl.BlockSpec((None,) + shard, lambda l: (l + lo, 0, 0))
    p_lo = lo if parts.shape[0] == DEPTH else 0
    extra = ([] if fill is None else list(fill)) + ([] if tie is None else [tie])
    return pl.pallas_call(
        body, grid=(hi - lo,),
        in_specs=[pl.BlockSpec((None,) + parts.shape[1:], lambda l: (l + p_lo, 0, 0, 0))] + [blk] * 9
        + [ANY] * len(extra),
        out_specs=[blk] * 12, out_shape=[jax.ShapeDtypeStruct((DEPTH,) + shard, F32)] * 12,
        input_output_aliases={} if fill is None else {10 + j: j for j in range(12)},
        compiler_params=_cp(), name=name)(parts, *wmv, *extra)


def _split_start(name, arrays, n_sems, plan, after=None):
    n = len(arrays)
    order = [] if after is None else [after]
    n_in = n + len(order)

    def body(*refs):
        ins, send_sems, recv_sems, token = refs[:n], refs[n_in], refs[n_in + 1], refs[-1]
        for src, dst, k, to in plan(ins)[0]:
            pltpu.make_async_remote_copy(src_ref=src, dst_ref=dst, send_sem=send_sems.at[k], recv_sem=recv_sems.at[k],
                                         device_id=to, device_id_type=MESH_ID).start()
        token[...] = jnp.zeros_like(token)

    outs = pl.pallas_call(
        body, name=name,
        out_shape=(pltpu.SemaphoreType.DMA((n_sems,)), pltpu.SemaphoreType.DMA((n_sems,)),
                   *[pltpu.HBM(a.shape, a.dtype) for a in arrays], jax.ShapeDtypeStruct((8, 128), F32)),
        in_specs=[HBM] * n + [ANY] * len(order),
        out_specs=(SEM, SEM, *[HBM] * n, pl.BlockSpec(memory_space=pltpu.VMEM)),
        input_output_aliases={i: 2 + i for i in range(n)},
        compiler_params=pltpu.CompilerParams(has_side_effects=EFFECT),
    )(*[pltpu.with_memory_space_constraint(a, pltpu.HBM) for a in arrays], *order)
    return outs[0], outs[1], list(outs[2:2 + n]), outs[-1]


def _split_wait(name, arrays, send_sems, recv_sems, after, plan):
    n = len(arrays)
    order = list(after) if isinstance(after, (list, tuple)) else [after]

    def body(*refs):
        ins, s_sems, r_sems = refs[:n], refs[n], refs[n + 1]
        sends, arrivals = plan(ins)
        x, y, c = lax.axis_index("x"), lax.axis_index("y"), lax.axis_index("c")
        for src, dst, k, to in sends:
            pltpu.make_async_remote_copy(src_ref=src, dst_ref=dst, send_sem=s_sems.at[k], recv_sem=r_sems.at[k],
                                         device_id=to, device_id_type=MESH_ID).wait_send()
        for dst, k in arrivals:
            pltpu.make_async_remote_copy(src_ref=dst, dst_ref=dst, send_sem=s_sems.at[k], recv_sem=r_sems.at[k],
                                         device_id=(x, y, c), device_id_type=MESH_ID).wait_recv()

    return pl.pallas_call(
        body, name=name, out_shape=[pltpu.HBM(a.shape, a.dtype) for a in arrays],
        in_specs=[HBM] * n + [SEM, SEM] + [ANY] * len(order), out_specs=[HBM] * n,
        input_output_aliases={i: i for i in range(n)},
        compiler_params=pltpu.CompilerParams(has_side_effects=EFFECT),
    )(*arrays, send_sems, recv_sems, *order)


def _chips():
    x, y, c = lax.axis_index("x"), lax.axis_index("y"), lax.axis_index("c")
    return x, y, c, [(1 - x, y), (x, 1 - y), (1 - x, 1 - y)]


def _plan_gather_chips(refs):
    x, y, c, chips = _chips()
    me = 4 * x + 2 * y + c
    n = len(refs) // 2
    sends, arrivals = [], []
    for i in range(n):
        src, land = refs[i], refs[n + i]
        sends.append((src, land.at[me], 4 * i, (x, y, 1 - c)))
        arrivals.append((land.at[4 * x + 2 * y + 1 - c], 4 * i))
        for j, (px, py) in enumerate(chips):
            sends.append((src, land.at[me], 4 * i + 1 + j, (px, py, c)))
            arrivals.append((land.at[4 * px + 2 * py + c], 4 * i + 1 + j))
    return sends, arrivals


def _plan_gather_pass(refs):
    x, y, c, chips = _chips()
    sends, arrivals = [], []
    for i in range(len(refs)):
        for j, (px, py) in enumerate(chips):
            slot = refs[i].at[4 * px + 2 * py + c]
            sends.append((slot, slot, 4 * i + j, (x, y, 1 - c)))
            arrivals.append((refs[i].at[4 * px + 2 * py + 1 - c], 4 * i + j))
        back = refs[i].at[4 * x + 2 * y + 1 - c]
        sends.append((back, back, 4 * i + 3, (x, y, 1 - c)))
        arrivals.append((refs[i].at[4 * x + 2 * y + c], 4 * i + 3))
    return sends, arrivals


def _plan_scatter_pair(refs):
    x, y, c = lax.axis_index("x"), lax.axis_index("y"), lax.axis_index("c")
    n = len(refs) // 2
    sends, arrivals = [], []
    for i in range(n):
        for q in range(4):
            sends.append((refs[i].at[q, 1 - c], refs[n + i].at[q], 4 * i + q, (x, y, 1 - c)))
            arrivals.append((refs[n + i].at[q], 4 * i + q))
    return sends, arrivals


def _plan_scatter_chips(layer):
    def plan(refs):
        x, y, c, chips = _chips()
        n = len(refs) // 2
        sends, arrivals = [], []
        for i in range(n):
            for j, (px, py) in enumerate(chips):
                sends.append((refs[i].at[2 * px + py], refs[n + i].at[layer, 2 * x + y], 3 * i + j, (px, py, c)))
                arrivals.append((refs[n + i].at[layer, 2 * px + py], 3 * i + j))
        return sends, arrivals

    return plan


def _pair_sum(parts4, from_pair, landing, layer, core, tr, name):
    _, _, rows, cols = parts4.shape

    def body(c_ref, p_ref, s_ref, l_ref, sum_ref, land_ref):
        v = (p_ref[...].astype(F32) + s_ref[...].astype(F32)).astype(BF16)
        sum_ref[...] = v
        land_ref[...] = v

    blk = pl.BlockSpec((None, tr, cols), lambda q, i, c_ref: (q, i, 0))
    return pl.pallas_call(
        body,
        grid_spec=pltpu.PrefetchScalarGridSpec(
            num_scalar_prefetch=1, grid=(4, rows // tr),
            in_specs=[pl.BlockSpec((None, None, tr, cols), lambda q, i, c_ref: (q, c_ref[0], i, 0)), blk, ANY],
            out_specs=[blk, pl.BlockSpec((None, None, tr, cols), lambda q, i, c_ref: (layer, q, i, 0))]),
        out_shape=[jax.ShapeDtypeStruct((4, rows, cols), BF16), jax.ShapeDtypeStruct(landing.shape, BF16)],
        input_output_aliases={3: 1}, compiler_params=_cp(), name=name,
    )(core, parts4, from_pair, landing)


def _travel_layout(t):
    tr = lambda a: jnp.swapaxes(a, 1, 2)
    branch = jnp.concatenate([tr(t["w_attn_o"]), tr(t["w_conv_o"]), tr(t["w_ssm_o"])], axis=2)
    return [tr(t["w_in"]), tr(t["w_ffn_in"]), t["w_ffn_out"], t["w_mix_o"], branch, t["w_ssm_glu"]]


def _native_layout(a):
    tr = lambda x: jnp.swapaxes(x, 1, 2)
    return {"w_in": tr(a[0]), "w_ffn_in": tr(a[1]), "w_ffn_out": a[2], "w_mix_o": a[3], "w_ssm_glu": a[5]}


def _rope_tabs():
    pos = jnp.arange(SEQ, dtype=F32)
    inv_freq = ROPE_THETA ** (-jnp.arange(0, ROT_DIM, 2, dtype=F32) / ROT_DIM)
    ang = pos[:, None] * inv_freq[None, :]
    cos, sin = jnp.cos(ang), jnp.sin(ang)
    one, zero = jnp.ones((SEQ, HEAD_DIM - ROT_DIM), F32), jnp.zeros((SEQ, HEAD_DIM - ROT_DIM), F32)
    z8 = jnp.zeros((SEQ, 8), F32)
    head = lambda *p: jnp.tile(jnp.concatenate(p, axis=1), (1, 2))
    return head(cos, cos, one), head(-sin, z8, zero), head(z8, sin, zero)


def _ssm_mats(sp):
    lr, li, bbr, bbi = _ssm_prep(sp["a_re"], sp["a_im"], sp["log_dt"], sp["bt_re"], sp["bt_im"])
    lanes = SSM_GROUPS * SSM_STATE
    b_re, b_im, c_re, c_im_neg = _ssm_embed(bbr, bbi, sp["c_re"], sp["c_im"])
    return {
        "a_re": lr.reshape(DEPTH, 1, lanes), "a_im": li.reshape(DEPTH, 1, lanes),
        "b_re": b_re, "b_im": b_im, "c_re": c_re, "c_im_neg": c_im_neg,
    }


def _layer_fwd(x, i, w, rp, mats, tabs, tie, hooks):
    q, kv, cbx, u, glog, cv, h = _rms_mm_in(x, rp["norm_mix"][i], w["win_t"], tabs, rp["conv_w"], i, tie)
    o = _attn_fwd(q, kv, tabs, rp["attn_sinks"][i])
    x_re, x_im, y = _ssm_fwd(u, mats, i, rp["ssm_d"])
    z = _glu_fwd(y, w["wglu"])
    x1 = _mix_fwd(x, o, cv, z, glog, rp["b_gate"], i, w["branch_t"], w["wmix"], hooks["early"](z))
    hooks["pre_ffn"](x1)
    act, up, silu, dsilu, h2 = _rms_mm_ffn(x1, rp["norm_ffn"][i], w["wffn_t"])
    x2 = _ffn_out_fwd(x1, act, w["wout"], hooks["mid"](h2))
    kept = dict(x=x, q=q, kv=kv, cbx=cbx, u=u, glog=glog, h=h, o=o, cv=cv, z=z, y=y,
                x_re=x_re, x_im=x_im, x1=x1, act=act, up=up, silu=silu, dsilu=dsilu, h2=h2)
    return x2, kept


def _layer_bwd(dx2, k, i, w, rp, mats, tabs, tie, hooks):
    dgu = _ffn_out_bwd(dx2, k["up"], k["silu"], k["dsilu"], w["wout"], tie)
    g_wout = _mm_tn(k["act"], dx2, tm=FFN_H // 2, tn=1024, name="mm_tn_ffn_out")
    g_wffn_t = _mm_tn(dgu, k["h2"], tm=FFN_H // 2, tn=1024, name="mm_tn_ffn_in")
    dx1, d_norm_ffn = _mm_rmsbwd([dgu], w["wffn_t"], k["x1"], rp["norm_ffn"][i], dx2, "mm_rmsbwd_ffn")

    mg, dya, dyc, dys, do, dcv, dz, dgl, db_gate = _mix_bwd(
        dx1, k["o"], k["cv"], k["z"], k["glog"], rp["b_gate"], i, w["branch_t"], w["wmix"],
        hooks["mid"]((g_wffn_t, g_wout, d_norm_ffn)))
    g_wmix = _mm_tn(mg, dx1, tm=1024, tn=512, name="mm_tn_mix")
    g_branch_t = _tn_branches((dya, dyc, dys), (k["o"], k["cv"], k["z"]))

    dy, ys16, da16, dd = _glu_bwd(k["y"], w["wglu"], dz, k["u"])
    g_wglu = _mm_tn(ys16, da16, tm=256, tn=512, name="mm_tn_glu")
    du, da_re, da_im, db_re, db_im, dc_re, dc_im = _ssm_bwd(dy, k["x_re"], k["x_im"], k["u"], mats, i, rp["ssm_d"])

    dcb, dcc, dcx, d_conv_w = _conv_bwd(k["cbx"], rp["conv_w"], i, dcv, hooks["late"](du))
    dq, dkv, d_sinks = _attn_bwd(k["q"], k["kv"], tabs, rp["attn_sinks"][i], do)

    pieces = [dq, dkv, dcb, dcc, dcx, du, dgl]
    g_win_t = _tn_pieces(pieces, k["h"])
    dx, d_norm_mix = _mm_rmsbwd(pieces, w["win_t"], k["x"], rp["norm_mix"][i], dx1, "mm_rmsbwd_in")

    grads = [g_win_t, g_wffn_t, g_wout, g_wmix, g_branch_t, g_wglu]
    small = dict(norm_mix=d_norm_mix, b_gate=db_gate, attn_sinks=d_sinks, ssm_d=dd, norm_ffn=d_norm_ffn,
                 conv_w=d_conv_w, da_re=da_re, da_im=da_im, db_re=db_re, db_im=db_im, dc_re=dc_re, dc_im=dc_im)
    return dx, grads, small


def _ssm_diag(layers, signs):
    n_in = len(layers) * DEPTH

    def body(*refs):
        for k, out in enumerate(refs[n_in:]):
            for l in range(DEPTH):
                src = refs[k * DEPTH + l]
                for group in range(SSM_GROUPS):
                    g, a = divmod(group, LANES_G // SSM_STATE)
                    blk = src[g, pl.ds(a * SSM_GROUP, SSM_GROUP), pl.ds(a * SSM_STATE, SSM_STATE)]
                    out[l, group] = blk if signs[k] > 0 else -blk

    return pl.pallas_call(
        body, out_shape=[jax.ShapeDtypeStruct((DEPTH,) + _GHS, F32)] * len(layers),
        compiler_params=_cp(), name="ssm_diag")(*[x for kind in layers for x in kind])


def _replicated_grads(sg, sp):
    stack = lambda name: jnp.stack([sg[i][name] for i in range(DEPTH)])
    per_layer = lambda name: [sg[i][name] for i in range(DEPTH)]
    db_re, db_im, dc_re, dc_im = _ssm_diag([per_layer(n) for n in ("db_re", "db_im", "dc_re", "dc_im")],
                                           (1, 1, 1, -1))
    cots = (stack("da_re").reshape(DEPTH, *_GS), stack("da_im").reshape(DEPTH, *_GS), db_re, db_im)
    d_a_re, d_a_im, d_log_dt, d_bt_re, d_bt_im = _ssm_prep_bwd(
        sp["a_re"], sp["a_im"], sp["log_dt"], sp["bt_re"], sp["bt_im"], cots)
    sgrads = {"norm_mix": stack("norm_mix"), "b_gate": stack("b_gate"),
              "attn_sinks": stack("attn_sinks")[:, :, :N_Q_HEADS], "ssm_a_re": d_a_re, "ssm_a_im": d_a_im,
              "ssm_b_re": jnp.swapaxes(d_bt_re, 2, 3), "ssm_b_im": jnp.swapaxes(d_bt_im, 2, 3),
              "ssm_c_re": dc_re, "ssm_c_im": dc_im,
              "ssm_d": stack("ssm_d"), "ssm_log_dt": d_log_dt, "norm_ffn": stack("norm_ffn")}
    return sgrads, stack("conv_w")[:, :3]


def kernel(x, norm_mix, w_in, b_gate, attn_sinks, w_attn_o, conv_w, w_conv_o, ssm_a_re, ssm_a_im, ssm_b_re, ssm_b_im, ssm_c_re, ssm_c_im, ssm_d, ssm_log_dt, w_ssm_glu, w_ssm_o, w_mix_o, norm_ffn, w_ffn_in, w_ffn_out, norm_final, loss_target, m_norm_mix, m_w_in, m_b_gate, m_attn_sinks, m_w_attn_o, m_conv_w, m_w_conv_o, m_ssm_a_re, m_ssm_a_im, m_ssm_b_re, m_ssm_b_im, m_ssm_c_re, m_ssm_c_im, m_ssm_d, m_ssm_log_dt, m_w_ssm_glu, m_w_ssm_o, m_w_mix_o, m_norm_ffn, m_w_ffn_in, m_w_ffn_out, m_norm_final, v_norm_mix, v_w_in, v_b_gate, v_attn_sinks, v_w_attn_o, v_conv_w, v_w_conv_o, v_ssm_a_re, v_ssm_a_im, v_ssm_b_re, v_ssm_b_im, v_ssm_c_re, v_ssm_c_im, v_ssm_d, v_ssm_log_dt, v_w_ssm_glu, v_w_ssm_o, v_w_mix_o, v_norm_ffn, v_w_ffn_in, v_w_ffn_out, v_norm_final):
    big = {"w": dict(w_in=w_in, w_attn_o=w_attn_o, w_conv_o=w_conv_o, w_ssm_glu=w_ssm_glu, w_ssm_o=w_ssm_o,
                     w_mix_o=w_mix_o, w_ffn_in=w_ffn_in, w_ffn_out=w_ffn_out),
           "m": dict(w_in=m_w_in, w_attn_o=m_w_attn_o, w_conv_o=m_w_conv_o, w_ssm_glu=m_w_ssm_glu,
                     w_ssm_o=m_w_ssm_o, w_mix_o=m_w_mix_o, w_ffn_in=m_w_ffn_in, w_ffn_out=m_w_ffn_out),
           "v": dict(w_in=v_w_in, w_attn_o=v_w_attn_o, w_conv_o=v_w_conv_o, w_ssm_glu=v_w_ssm_glu,
                     w_ssm_o=v_w_ssm_o, w_mix_o=v_w_mix_o, w_ffn_in=v_w_ffn_in, w_ffn_out=v_w_ffn_out)}
    small = {"w": dict(norm_mix=norm_mix, b_gate=b_gate, attn_sinks=attn_sinks, ssm_a_re=ssm_a_re,
                       ssm_a_im=ssm_a_im, ssm_b_re=ssm_b_re, ssm_b_im=ssm_b_im, ssm_c_re=ssm_c_re,
                       ssm_c_im=ssm_c_im, ssm_d=ssm_d, ssm_log_dt=ssm_log_dt, norm_ffn=norm_ffn),
             "m": dict(norm_mix=m_norm_mix, b_gate=m_b_gate, attn_sinks=m_attn_sinks, ssm_a_re=m_ssm_a_re,
                       ssm_a_im=m_ssm_a_im, ssm_b_re=m_ssm_b_re, ssm_b_im=m_ssm_b_im, ssm_c_re=m_ssm_c_re,
                       ssm_c_im=m_ssm_c_im, ssm_d=m_ssm_d, ssm_log_dt=m_ssm_log_dt, norm_ffn=m_norm_ffn),
             "v": dict(norm_mix=v_norm_mix, b_gate=v_b_gate, attn_sinks=v_attn_sinks, ssm_a_re=v_ssm_a_re,
                       ssm_a_im=v_ssm_a_im, ssm_b_re=v_ssm_b_re, ssm_b_im=v_ssm_b_im, ssm_c_re=v_ssm_c_re,
                       ssm_c_im=v_ssm_c_im, ssm_d=v_ssm_d, ssm_log_dt=v_ssm_log_dt, norm_ffn=v_norm_ffn)}
    finals = {"w": norm_final, "m": m_norm_final, "v": v_norm_final}
    convs = {"w": conv_w, "m": m_conv_w, "v": v_conv_w}
    small_out_shapes = {name: a.shape for name, a in small["w"].items()}
    small_out_shapes.update(norm_final=(D_MODEL,), conv_w=(DEPTH, 3, 64))
    small_shapes = dict(small_out_shapes, norm_final=(1, D_MODEL), conv_w=(DEPTH, 3, WIDTH))
    dense = ("ssm_b_re", "ssm_b_im", "ssm_c_re", "ssm_c_im")
    for name in dense:
        small_shapes[name] = (DEPTH, SSM_GROUPS, SSM_GROUP * SSM_STATE)
    small_wmv = {name: [(convs[s] if name == "conv_w" else finals[s] if name == "norm_final" else small[s][name])
                        .reshape((DEPTH, 3, 64) if name == "conv_w" else small_shapes[name]) for s in "wmv"]
                 for name in small_shapes}
    mine = 4 * lax.axis_index("x") + 2 * lax.axis_index("y") + lax.axis_index("c")

    travel = {s: _travel_layout(big[s]) for s in "wmv"}
    stacked16 = list(zip(*[[a[0] for a in _travel_layout({n: w[i:i + 1].astype(BF16) for n, w in big["w"].items()})]
                           for i in range(DEPTH)]))
    rp = {"norm_mix": norm_mix[:, None], "norm_ffn": norm_ffn[:, None], "attn_sinks": attn_sinks[:, None],
          "b_gate": b_gate[:, None], "ssm_d": ssm_d[:, None]}
    sp = {"a_re": ssm_a_re, "a_im": ssm_a_im, "log_dt": ssm_log_dt[:, :, None],
          "bt_re": jnp.swapaxes(ssm_b_re, 2, 3), "bt_im": jnp.swapaxes(ssm_b_im, 2, 3),
          "c_re": ssm_c_re, "c_im": ssm_c_im}
    rows_tile = {"win_t": 368, "wffn_t": 352, "wout": 352, "wmix": 128, "branch_t": 128, "wglu": 64}
    core = lax.axis_index("c").astype(jnp.int32).reshape(1)
    no_tie = jnp.zeros((8, 128), F32)

    def landing_zones(srcs):
        return [lax.empty((N_DEV,) + s.shape, s.dtype) for s in srcs]

    def gather_chips(tag, i, kinds, after, extra=()):
        srcs = [stacked16[j][i] for j in kinds] + list(extra)
        s_sems, r_sems, arrays, token = _split_start(
            f"gather_chips_start_{tag}", srcs + landing_zones(srcs), 4 * len(srcs), _plan_gather_chips, after)
        return (tag, s_sems, r_sems, arrays), token

    def gather_pass(state, after):
        tag, s_sems, r_sems, arrays = state
        arrays = _split_wait(f"gather_chips_wait_{tag}", arrays, s_sems, r_sems, after, _plan_gather_chips)
        n = len(arrays) // 2
        s_sems, r_sems, lands, token = _split_start(
            f"gather_pass_start_{tag}", list(arrays[n:]), 4 * n, _plan_gather_pass)
        return (tag, s_sems, r_sems, lands), token

    def gather_done(state, after, kinds):
        tag, s_sems, r_sems, lands = state
        lands = _split_wait(f"gather_pass_wait_{tag}", lands, s_sems, r_sems, after, _plan_gather_pass)
        named = {KINDS[j][0]: a.reshape(N_DEV * KINDS[j][1], KINDS[j][2]) for a, j in zip(lands, kinds)}
        return named, list(lands[len(kinds):])

    all_kinds, mixer_kinds, ffn_kinds = tuple(range(len(KINDS))), (0, 3, 4, 5), (1, 2)
    no_hooks = {name: (lambda value: no_tie) for name in ("early", "pre_ffn", "mid", "late")}
    state, token = gather_chips("0m", 0, mixer_kinds, None, extra=[jnp.pad(conv_w.reshape(6, 128), ((0, 2), (0, 0)))])
    mats = _ssm_mats(dict(sp, log_dt=sp["log_dt"] + token[0, 0]))
    tabs = _rope_tabs()
    early_work = list(mats.values()) + list(tabs) + [a for name in dense for a in small_wmv[name]]
    early_work += [stacked16[j][0] for j in ffn_kinds] + [stacked16[j][1] for j in mixer_kinds]
    state, _ = gather_pass(state, early_work)
    ffn_state, tie = gather_chips("0f", 0, ffn_kinds, state[3][0])
    w_next, (conv_all,) = gather_done(state, tabs[2], mixer_kinds)
    conv_full = conv_all[:, :6].reshape(N_DEV, DEPTH, 3, 64).transpose(1, 2, 0, 3).reshape(DEPTH, 3, WIDTH)
    rp["conv_w"] = jnp.pad(conv_full, ((0, 0), (0, 5), (0, 0)))

    act = x[0]
    weights, kept = [], []
    for i in range(DEPTH):
        w_i, hooks, held = w_next, dict(no_hooks), {}

        def early(value, ffn_state=ffn_state, held=held):
            held["ffn"], token = gather_pass(ffn_state, value)
            return token

        def pre_ffn(value, w_i=w_i, held=held):
            w_i.update(gather_done(held["ffn"], value, ffn_kinds)[0])

        hooks.update(early=early, pre_ffn=pre_ffn)
        if i + 1 < DEPTH:
            state, tie = gather_chips(f"{i + 1}m", i + 1, mixer_kinds, tie if i == 0 else w_i["win_t"])

            def mid(value, i=i, state=state, held=held):
                held["next"], token = gather_pass(state, value)
                held["next_ffn"], token = gather_chips(f"{i + 1}f", i + 1, ffn_kinds, token)
                return token

            hooks.update(mid=mid)
        act, k = _layer_fwd(act, i, w_i, rp, mats, tabs, tie, hooks)
        if i + 1 < DEPTH:
            w_next, _ = gather_done(held["next"], act, mixer_kinds)
            ffn_state, tie = held["next_ffn"], no_tie
        weights.append(w_i)
        kept.append(k)
    loss_row, dx, d_norm_final = _loss_head(act, norm_final[None], loss_target[0])

    landings = [lax.empty((DEPTH, 4, r, c), BF16) for _, r, c in KINDS]
    landings0 = [lax.empty((1, 4, r, c), BF16) for _, r, c in KINDS]

    def scatter_pair(tag, kinds, grads, after):
        parts4 = [g.reshape(4, 2, KINDS[j][1], KINDS[j][2]) for g, j in zip(grads, kinds)]
        zones = [lax.empty((4, KINDS[j][1], KINDS[j][2]), BF16) for j in kinds]
        s_sems, r_sems, arrays, token = _split_start(
            f"scatter_pair_start_{tag}", parts4 + zones, 4 * len(kinds), _plan_scatter_pair, after)
        return (tag, kinds, s_sems, r_sems, arrays), token

    def scatter_chips(state, lands, slot, after):
        tag, kinds, s_sems, r_sems, arrays = state
        arrays = _split_wait(f"scatter_pair_wait_{tag}", arrays, s_sems, r_sems, after, _plan_scatter_pair)
        n = len(kinds)
        sums, mine_lands = [], []
        for k, j in enumerate(kinds):
            name = KINDS[j][0]
            chip_sum, land = _pair_sum(arrays[k], arrays[n + k], lands[j], slot, core, KINDS[j][1],
                                       f"pair_sum_{name}")
            sums.append(chip_sum)
            mine_lands.append(land)
        s_sems, r_sems, arrays, token = _split_start(
            f"scatter_chips_start_{tag}", sums + mine_lands, 3 * n, _plan_scatter_chips(slot))
        return (tag, kinds, slot, s_sems, r_sems, arrays), token

    def scatter_done(state, lands, after):
        tag, kinds, slot, s_sems, r_sems, arrays = state
        arrays = _split_wait(f"scatter_chips_wait_{tag}", arrays, s_sems, r_sems, after, _plan_scatter_chips(slot))
        lands = list(lands)
        for k, j in enumerate(kinds):
            lands[j] = arrays[len(kinds) + k]
        return lands

    sg = [None] * DEPTH
    pending, tie = None, no_tie
    for i in reversed(range(DEPTH)):
        hooks, held = dict(no_hooks), {}
        if pending is not None:
            def mid(value, i=i, pending=pending, held=held):
                held["chips"], token = scatter_chips(pending, landings, i + 1, value[2])
                if i == 0:
                    held["ffn_pair"], token = scatter_pair("0f", ffn_kinds, value[:2], token)
                return token

            hooks.update(mid=mid)
        if i == 0:
            def late(value, held=held):
                held["ffn_chips"], token = scatter_chips(held["ffn_pair"], landings0, 0, value)
                return token

            hooks.update(late=late)
        dx, grads, sg[i] = _layer_bwd(dx, kept[i], i, weights[i], rp, mats, tabs, tie, hooks)
        if pending is not None:
            landings = scatter_done(held["chips"], landings, dx)
        if i > 0:
            pending, tie = scatter_pair(str(i), all_kinds, grads, dx)
        else:
            pending, _ = scatter_pair("0m", mixer_kinds, [grads[j] for j in mixer_kinds], dx)

    sgrads, conv_grad = _replicated_grads(sg, sp)

    small_names = list(REPLICATED) + ["norm_final", "conv_w"]
    sgrads.update(norm_final=d_norm_final, conv_w=conv_grad)
    small_src = [sgrads[name].reshape(small_shapes[name]).astype(BF16) for name in small_names]
    small_src.append(jnp.broadcast_to(loss_row[:, :1], (8, 128)))
    last, tie = scatter_chips(pending, landings0, 0, small_src[0])
    s_sems, r_sems, arrays, tie = _split_start(
        "gather_small_chips_start", small_src + landing_zones(small_src), 4 * len(small_src), _plan_gather_chips, tie)
    small_state = ("small", s_sems, r_sems, arrays)

    branch_wmv = [big[s][n] for n in BRANCHES for s in "wmv"]
    jb = [name for name, _, _ in KINDS].index("branch_t")

    def adamw(j, name, parts, label, groups, **kw):
        if j == jb:
            return _adamw_branches(parts, branch_wmv, label + name, groups, **kw)
        return _adamw(parts, travel["w"][j], travel["m"][j], travel["v"][j], rows_tile[name], label + name,
                      groups=groups, **kw)

    big_out = []
    for j, (name, _, _) in enumerate(KINDS):
        big_out.append(adamw(j, name, landings[j], "adamw_late_", (1, DEPTH), tie=tie))
        tie = big_out[-1][-1]
    landings0 = scatter_done(held["ffn_chips"], landings0, tie)
    landings0 = scatter_done(last, landings0, tie)
    small_state, _ = gather_pass(small_state, landings0[0])
    big_out = [adamw(j, name, landings0[j], "adamw_first_", (0, 1), fill=big_out[j])
               for j, (name, _, _) in enumerate(KINDS)]
    big_res = []
    for kind in range(4):
        res = _native_layout([None if j == jb else big_out[j][kind] for j in range(len(KINDS))])
        res.update({n: big_out[jb][4 * b + kind] for b, n in enumerate(BRANCHES)})
        big_res.append(res)

    _, sparts = gather_done(small_state, big_out[-1][0], ())
    loss = jnp.sum(sparts[-1][:, 0, 0])
    sparts = dict(zip(small_names, sparts))
    sparts["conv_w"] = lax.dynamic_slice_in_dim(sparts["conv_w"], mine * 64, 64, axis=3)
    small_res = {}
    for name in small_names:
        res = _adamw_small(sparts[name], *small_wmv[name], "adamw_" + name)
        small_res[name] = [r.reshape(small_out_shapes[name]) for r in res]

    order = ["norm_mix", "w_in", "b_gate", "attn_sinks", "w_attn_o", "conv_w", "w_conv_o", "ssm_a_re", "ssm_a_im",
             "ssm_b_re", "ssm_b_im", "ssm_c_re", "ssm_c_im", "ssm_d", "ssm_log_dt", "w_ssm_glu", "w_ssm_o",
             "w_mix_o", "norm_ffn", "w_ffn_in", "w_ffn_out", "norm_final"]
    outs = [loss, dx[None]]
    for kind in range(4):
        for name in order:
            outs.append(big_res[kind][name] if name in big_res[kind] else small_res[name][kind])
    return tuple(outs)
```

```python
import math

import jax
import jax.numpy as jnp
from jax import lax
from jax.experimental import pallas as pl
from jax.experimental.pallas import tpu as pltpu

F32 = jnp.float32
BF16 = jnp.bfloat16

N_DEV = 8
DEPTH = 4
SEQ = 2048
D_MODEL = 1024
N_Q_HEADS = 8
HEAD_DIM = 64
ATTN_W = 512
KV_W = 128
BLOCK = 128
N_BLOCKS = SEQ // BLOCK
ROPE_THETA = 500000.0
ROT_DIM = 16
NEG_INF = -1e30
WIDTH = 512
SSM_GROUPS = 32
SSM_GROUP = 16
SSM_STATE = 64
CHUNK = 256
N_CHUNKS = SEQ // CHUNK
GATE_W = 3 * D_MODEL
IN_COLS = 5888
FFN_H = 2816
NORM_EPS = 1e-6
LR, B1, B2, ADAM_EPS, WD, STEP = 0.001, 0.9, 0.999, 1e-08, 0.01, 10

COL_Q, COL_KV, COL_CBX, COL_U, COL_G = 0, 512, 768, 2304, 2816
PIECE_W = (512, 256, 512, 512, 512, 512, 3072)
PIECE_OFF = tuple(sum(PIECE_W[:i]) for i in range(len(PIECE_W)))

KINDS = (("win_t", 736, 1024), ("wffn_t", 704, 1024), ("wout", 352, 1024), ("wmix", 128, 1024),
         ("branch_t", 128, 1536), ("wglu", 64, 512))

REPLICATED = ("norm_mix", "b_gate", "attn_sinks", "ssm_a_re", "ssm_a_im", "ssm_b_re", "ssm_b_im", "ssm_c_re",
              "ssm_c_im", "ssm_d", "ssm_log_dt", "norm_ffn")

VMEM_LIMIT = 56 * 1024 * 1024
NT = (((1,), (1,)), ((), ()))
TN = (((0,), (0,)), ((), ()))
MESH_ID = pl.DeviceIdType.MESH
ANY = pl.BlockSpec(memory_space=pl.ANY)
HBM = pl.BlockSpec(memory_space=pltpu.HBM)
SEM = pl.BlockSpec(memory_space=pltpu.SEMAPHORE)
EFFECT = pltpu.SideEffectType.DATAFLOW_SIDE_EFFECTING


def _cp(**kw):
    return pltpu.CompilerParams(vmem_limit_bytes=VMEM_LIMIT, **kw)


def _full(shape):
    return pl.BlockSpec(shape, lambda *_: (0,) * len(shape))


def _resident(shape):
    return pl.BlockSpec(shape, lambda *_: (0,) * len(shape), pipeline_mode=pl.Buffered(1))


def _mm_tn(a, b, *, tm, tn, name):
    k, m = a.shape
    n = b.shape[1]

    def body(a_ref, b_ref, o_ref):
        o_ref[...] = lax.dot_general(a_ref[...].astype(BF16), b_ref[...].astype(BF16), TN,
                                     preferred_element_type=F32).astype(BF16)

    return pl.pallas_call(
        body, grid=(m // tm, n // tn),
        in_specs=[pl.BlockSpec((k, tm), lambda i, j: (0, i)), pl.BlockSpec((k, tn), lambda i, j: (0, j))],
        out_specs=pl.BlockSpec((tm, tn), lambda i, j: (i, j)),
        out_shape=jax.ShapeDtypeStruct((m, n), BF16), compiler_params=_cp(), name=name)(a, b)


def _rms_rows(xv, g):
    r = lax.rsqrt(jnp.mean(xv * xv, axis=-1, keepdims=True) + NORM_EPS)
    return ((xv * r) * g).astype(BF16)


def _rms_mm_in(x, g, wt, tabs, cw, layer, tie):
    tt = 512
    widths = (3 * WIDTH, WIDTH, GATE_W)
    offs = (COL_CBX, COL_U, COL_G)

    def body(x_ref, g_ref, w_ref, tc_ref, ta_ref, tb_ref, cw_ref, tie_ref,
             q_ref, kv_ref, cbx_ref, u_ref, gl_ref, cv_ref, h_ref, tail_ref):
        @pl.when(pl.program_id(0) == 0)
        def _():
            tail_ref[...] = jnp.zeros_like(tail_ref)

        h = _rms_rows(x_ref[...], g_ref[...])
        h_ref[...] = h
        prod = lax.dot_general(h, w_ref[...], NT, preferred_element_type=F32)
        for ref, o, w in zip((cbx_ref, u_ref, gl_ref), offs, widths):
            ref[...] = prod[:, o:o + w]
        c, a, b = tc_ref[...], ta_ref[...], tb_ref[...]
        for j in range(ATTN_W // 128):
            q_ref[:, 128 * j:128 * (j + 1)] = _rope(prod[:, 128 * j:128 * (j + 1)], c, a, b) * (HEAD_DIM ** -0.5)
        kv_ref[:, :KV_W] = _rope(prod[:, COL_KV:COL_KV + KV_W], c, a, b)
        kv_ref[:, KV_W:] = prod[:, COL_KV + KV_W:COL_CBX]

        row = lax.broadcasted_iota(jnp.int32, (tt, 128), 0)
        for j in range(WIDTH // 128):
            cols = slice(128 * j, 128 * (j + 1))
            cb = prod[:, COL_CBX + 128 * j:COL_CBX + 128 * (j + 1)]
            z = prod[:, COL_CBX + WIDTH + 128 * j:COL_CBX + WIDTH + 128 * (j + 1)] \
                * prod[:, COL_CBX + 2 * WIDTH + 128 * j:COL_CBX + 2 * WIDTH + 128 * (j + 1)]
            before1, before2 = tail_ref[7:8, cols], tail_ref[6:7, cols]
            z1 = jnp.where(row == 0, before1, pltpu.roll(z, 1, axis=0))
            z2 = jnp.where(row == 0, before2, jnp.where(row == 1, before1, pltpu.roll(z, 2, axis=0)))
            s = cw_ref[0:1, cols] * z2 + cw_ref[1:2, cols] * z1 + cw_ref[2:3, cols] * z
            cv_ref[:, cols] = (cb * s).astype(BF16)
            tail_ref[:, cols] = z[tt - 8:, :]

    row_spec = lambda w: pl.BlockSpec((tt, w), lambda i: (i, 0))
    sds = jax.ShapeDtypeStruct
    return pl.pallas_call(
        body, grid=(SEQ // tt,),
        in_specs=[row_spec(D_MODEL), _full((1, D_MODEL)), _resident((IN_COLS, D_MODEL)),
                  row_spec(128), row_spec(128), row_spec(128),
                  pl.BlockSpec((None, 8, WIDTH), lambda i: (layer, 0, 0)), ANY],
        out_specs=[row_spec(ATTN_W), row_spec(2 * KV_W), row_spec(3 * WIDTH), row_spec(WIDTH), row_spec(GATE_W),
                   row_spec(WIDTH), row_spec(D_MODEL)],
        out_shape=[sds((SEQ, ATTN_W), F32), sds((SEQ, 2 * KV_W), F32), sds((SEQ, 3 * WIDTH), F32),
                   sds((SEQ, WIDTH), F32), sds((SEQ, GATE_W), F32), sds((SEQ, WIDTH), BF16),
                   sds((SEQ, D_MODEL), BF16)],
        scratch_shapes=[pltpu.VMEM((8, WIDTH), F32)], compiler_params=_cp(), name="rms_mm_in",
    )(x, g, wt, *tabs, cw, tie)


def _rms_mm_ffn(x, g, wt):
    tt = 256

    def body(x_ref, g_ref, w_ref, act_ref, up_ref, silu_ref, dsilu_ref, h_ref):
        h = _rms_rows(x_ref[...], g_ref[...])
        h_ref[...] = h
        prod = lax.dot_general(h, w_ref[...], NT, preferred_element_type=F32)
        gt, up = prod[:, :FFN_H], prod[:, FFN_H:]
        sg = jax.nn.sigmoid(gt)
        silu = gt * sg
        act_ref[...] = (silu * up).astype(BF16)
        up_ref[...] = up.astype(BF16)
        silu_ref[...] = silu.astype(BF16)
        dsilu_ref[...] = (sg + silu * (1.0 - sg)).astype(BF16)

    row = lambda w: pl.BlockSpec((tt, w), lambda i: (i, 0))
    return pl.pallas_call(
        body, grid=(SEQ // tt,), in_specs=[row(D_MODEL), _full((1, D_MODEL)), _resident((2 * FFN_H, D_MODEL))],
        out_specs=[row(FFN_H)] * 4 + [row(D_MODEL)],
        out_shape=[jax.ShapeDtypeStruct((SEQ, FFN_H), BF16)] * 4 + [jax.ShapeDtypeStruct((SEQ, D_MODEL), BF16)],
        compiler_params=_cp(), name="rms_mm_ffn")(x, g, wt)


def _mm_rmsbwd(pieces, wt, x, g, dres, name):
    tt = 512
    widths = [p.shape[1] for p in pieces]
    offs = [sum(widths[:i]) for i in range(len(widths))]
    n = len(pieces)

    def body(*refs):
        p_refs, (w_ref, x_ref, g_ref, r_ref, dx_ref, dg_ref) = refs[:n], refs[n:]

        @pl.when(pl.program_id(0) == 0)
        def _():
            dg_ref[...] = jnp.zeros_like(dg_ref)

        dh = jnp.zeros((tt, D_MODEL), F32)
        for p_ref, o, w in zip(p_refs, offs, widths):
            dh += jnp.dot(p_ref[...], w_ref[o:o + w, :], preferred_element_type=F32)
        xv = x_ref[...]
        r = lax.rsqrt(jnp.mean(xv * xv, axis=-1, keepdims=True) + NORM_EPS)
        xh = xv * r
        gy = dh * g_ref[...]
        dx_ref[...] = r_ref[...] + r * (gy - xh * jnp.mean(gy * xh, axis=-1, keepdims=True))
        dg_ref[...] += jnp.sum(dh * xh, axis=0, keepdims=True)

    row = lambda w: pl.BlockSpec((tt, w), lambda i: (i, 0))
    return pl.pallas_call(
        body, grid=(SEQ // tt,),
        in_specs=[row(w) for w in widths] + [_resident(wt.shape), row(D_MODEL), _full((1, D_MODEL)), row(D_MODEL)],
        out_specs=[row(D_MODEL), _full((1, D_MODEL))],
        out_shape=[jax.ShapeDtypeStruct((SEQ, D_MODEL), F32), jax.ShapeDtypeStruct((1, D_MODEL), F32)],
        compiler_params=_cp(), name=name)(*pieces, wt, x, g, dres)


def _tn_pieces(pieces, h):
    tk, tn = 512, 512
    nk = SEQ // tk
    n = len(pieces)

    def body(*refs):
        p_refs, (h_ref, o_ref, acc_ref) = refs[:n], refs[n:]
        kk = pl.program_id(1)

        @pl.when(kk == 0)
        def _():
            acc_ref[...] = jnp.zeros_like(acc_ref)

        hv = h_ref[...]
        for p_ref, o, w in zip(p_refs, PIECE_OFF, PIECE_W):
            acc_ref[o:o + w, :] += lax.dot_general(p_ref[...], hv, TN, preferred_element_type=F32)

        @pl.when(kk == nk - 1)
        def _():
            o_ref[...] = acc_ref[...].astype(BF16)

    return pl.pallas_call(
        body, grid=(D_MODEL // tn, nk),
        in_specs=[pl.BlockSpec((tk, w), lambda j, kk: (kk, 0)) for w in PIECE_W]
        + [pl.BlockSpec((tk, tn), lambda j, kk: (kk, j))],
        out_specs=pl.BlockSpec((IN_COLS, tn), lambda j, kk: (0, j)),
        out_shape=jax.ShapeDtypeStruct((IN_COLS, D_MODEL), BF16),
        scratch_shapes=[pltpu.VMEM((IN_COLS, tn), F32)], compiler_params=_cp(), name="tn_pieces")(*pieces, h)


def _tn_branches(dys, acts):
    tk = 512
    nk = SEQ // tk

    def body(d0, d1, d2, a0, a1, a2, o_ref, acc_ref):
        kk = pl.program_id(0)

        @pl.when(kk == 0)
        def _():
            acc_ref[...] = jnp.zeros_like(acc_ref)

        for j, (d, a) in enumerate(((d0, a0), (d1, a1), (d2, a2))):
            acc_ref[:, WIDTH * j:WIDTH * (j + 1)] += lax.dot_general(d[...], a[...], TN, preferred_element_type=F32)

        @pl.when(kk == nk - 1)
        def _():
            o_ref[...] = acc_ref[...].astype(BF16)

    row = lambda w: pl.BlockSpec((tk, w), lambda kk: (kk, 0))
    return pl.pallas_call(
        body, grid=(nk,), in_specs=[row(D_MODEL)] * 3 + [row(WIDTH)] * 3,
        out_specs=_full((D_MODEL, 3 * WIDTH)), out_shape=jax.ShapeDtypeStruct((D_MODEL, 3 * WIDTH), BF16),
        scratch_shapes=[pltpu.VMEM((D_MODEL, 3 * WIDTH), F32)], compiler_params=_cp(), name="tn_branches",
    )(*dys, *acts)


def _rope(t, c, a, b):
    return t * c + pltpu.roll(t, 120, axis=1) * a + pltpu.roll(t, 8, axis=1) * b


def _rope_t(d, c, a, b):
    return d * c + pltpu.roll(d * a, 8, axis=1) + pltpu.roll(d * b, 120, axis=1)


def _band_sides(band):
    left = lax.broadcasted_iota(jnp.int32, band.shape, 1) < HEAD_DIM
    h0 = jnp.where(left, band, 0.0)
    h1 = jnp.where(left, 0.0, band)
    r0 = pltpu.roll(h0, HEAD_DIM, axis=1)
    r1 = pltpu.roll(h1, HEAD_DIM, axis=1)
    return ((h0.astype(BF16), r0.astype(BF16)), (r1.astype(BF16), h1.astype(BF16)))


def _attn_mask(i):
    qi = lax.broadcasted_iota(jnp.int32, (2 * BLOCK, 2 * BLOCK), 0) % BLOCK
    kj = lax.broadcasted_iota(jnp.int32, (2 * BLOCK, 2 * BLOCK), 1)
    delta = qi + BLOCK - kj
    return (delta >= 0) & (delta < BLOCK) & ((kj >= BLOCK) | (i > 0))


def _attn_probs(s, ok, sink):
    s = jnp.where(ok, s, NEG_INF)
    m = jnp.maximum(jnp.max(s, axis=-1, keepdims=True), sink)
    p = jnp.exp(s - m)
    es = jnp.exp(sink - m)
    inv = 1.0 / (jnp.sum(p, axis=-1, keepdims=True) + es)
    return p * inv, es * inv


def _kv_group(qs, ks, vs, kh, sink_ref):
    q2 = jnp.concatenate([qs[2 * kh], qs[2 * kh + 1]], axis=0)
    kst = jnp.concatenate([ks[kh][0], ks[kh][1]], axis=0)
    vst = jnp.concatenate([vs[kh][0], vs[kh][1]], axis=0)
    top = lax.broadcasted_iota(jnp.int32, (2 * BLOCK, 1), 0) < BLOCK
    sinks = [jnp.where(top, sink_ref[0, 4 * kh + h], sink_ref[0, 4 * kh + 2 + h]) for h in range(2)]
    return q2, kst, vst, sinks


def _attn_load(q_ref, kvc_ref, kvp_ref, tc_ref, ta_ref, tb_ref, pc_ref, pa_ref, pb_ref):
    c, a, b = tc_ref[...], ta_ref[...], tb_ref[...]
    kband = jnp.concatenate([kvp_ref[:, :KV_W], kvc_ref[:, :KV_W]], axis=0)
    vband = jnp.concatenate([kvp_ref[:, KV_W:], kvc_ref[:, KV_W:]], axis=0)
    qs = [q_ref[:, 128 * j:128 * (j + 1)].astype(BF16) for j in range(4)]
    return qs, _band_sides(kband), _band_sides(vband), (c, a, b)


def _attn_specs(clamp):
    cur = lambda i: (clamp(i), 0)
    prev = lambda i: (jnp.maximum(clamp(i) - 1, 0), 0)
    return [
        pl.BlockSpec((BLOCK, ATTN_W), cur), pl.BlockSpec((BLOCK, 2 * KV_W), cur),
        pl.BlockSpec((BLOCK, 2 * KV_W), prev),
        pl.BlockSpec((BLOCK, 128), cur), pl.BlockSpec((BLOCK, 128), cur), pl.BlockSpec((BLOCK, 128), cur),
        pl.BlockSpec((BLOCK, 128), prev), pl.BlockSpec((BLOCK, 128), prev), pl.BlockSpec((BLOCK, 128), prev),
        pl.BlockSpec(memory_space=pltpu.SMEM),
    ]


def _attn_fwd(q, kv, tabs, sinks):
    tc, ta, tb = tabs

    def body(q_ref, kvc_ref, kvp_ref, tc_ref, ta_ref, tb_ref, pc_ref, pa_ref, pb_ref, sink_ref, o_ref):
        i = pl.program_id(0)
        qs, ks, vs, _ = _attn_load(q_ref, kvc_ref, kvp_ref, tc_ref, ta_ref, tb_ref, pc_ref, pa_ref, pb_ref)
        ok = _attn_mask(i)
        for kh in range(2):
            q2, kst, vst, sinks = _kv_group(qs, ks, vs, kh, sink_ref)
            s = lax.dot_general(q2, kst, NT, preferred_element_type=F32)
            pn = [_attn_probs(s[:, 2 * BLOCK * h:2 * BLOCK * (h + 1)], ok, sinks[h])[0].astype(BF16) for h in range(2)]
            o2 = jnp.dot(jnp.concatenate(pn, axis=1), vst, preferred_element_type=F32).astype(BF16)
            for r in range(2):
                j = 2 * kh + r
                o_ref[:, 128 * j:128 * (j + 1)] = o2[BLOCK * r:BLOCK * (r + 1)]

    return pl.pallas_call(
        body, grid=(N_BLOCKS,), in_specs=_attn_specs(lambda i: i),
        out_specs=pl.BlockSpec((BLOCK, ATTN_W), lambda i: (i, 0)),
        out_shape=jax.ShapeDtypeStruct((SEQ, ATTN_W), BF16), compiler_params=_cp(), name="attn_fwd",
    )(q, kv, kv, tc, ta, tb, tc, ta, tb, sinks)


def _attn_bwd(q, kv, tabs, sinks, do):
    tc, ta, tb = tabs
    last = N_BLOCKS - 1
    clamp = lambda i: jnp.minimum(i, last)

    def place(full, side, kh):
        left = lax.broadcasted_iota(jnp.int32, full.shape, 1) < HEAD_DIM
        valid = jnp.where(left, full, 0.0) if side == 0 else jnp.where(left, 0.0, full)
        return valid if side == kh else pltpu.roll(valid, HEAD_DIM, axis=1)

    def body(q_ref, kvc_ref, kvp_ref, tc_ref, ta_ref, tb_ref, pc_ref, pa_ref, pb_ref, sink_ref, do_ref,
             dq_ref, dkv_ref, ds_ref, carry_ref):
        i = pl.program_id(0)

        @pl.when(i == 0)
        def _():
            ds_ref[...] = jnp.zeros_like(ds_ref)
            carry_ref[...] = jnp.zeros_like(carry_ref)

        @pl.when(i > last)
        def _():
            dkv_ref[...] = carry_ref[...].astype(BF16)

        @pl.when(i <= last)
        def _():
            qs, ks, vs, (c, a, b) = _attn_load(q_ref, kvc_ref, kvp_ref, tc_ref, ta_ref, tb_ref,
                                               pc_ref, pa_ref, pb_ref)
            ok = _attn_mask(i)
            dk = jnp.zeros((2 * BLOCK, 128), F32)
            dv = jnp.zeros((2 * BLOCK, 128), F32)
            dsink = jnp.zeros((1, 128), F32)
            lane = lax.broadcasted_iota(jnp.int32, (1, 128), 1)
            for kh in range(2):
                q2, kst, vst, sinks = _kv_group(qs, ks, vs, kh, sink_ref)
                do2 = jnp.concatenate([do_ref[:, 128 * (2 * kh + r):128 * (2 * kh + r + 1)] for r in range(2)],
                                      axis=0).astype(BF16)
                s = lax.dot_general(q2, kst, NT, preferred_element_type=F32)
                dp = lax.dot_general(do2, vst, NT, preferred_element_type=F32)
                pns, dss = [], []
                for h in range(2):
                    cols = slice(2 * BLOCK * h, 2 * BLOCK * (h + 1))
                    pn, ps = _attn_probs(s[:, cols], ok, sinks[h])
                    dr = jnp.sum(pn * dp[:, cols], axis=-1, keepdims=True)
                    pns.append(pn.astype(BF16))
                    dss.append((pn * (dp[:, cols] - dr)).astype(BF16))
                    for r in range(2):
                        part = -jnp.sum((ps * dr)[BLOCK * r:BLOCK * (r + 1)])
                        dsink += jnp.where(lane == 4 * kh + 2 * r + h, part, 0.0)
                ds2, pn2 = jnp.concatenate(dss, axis=1), jnp.concatenate(pns, axis=1)
                dq2 = jnp.dot(ds2, kst, preferred_element_type=F32) * (HEAD_DIM ** -0.5)
                dk2 = lax.dot_general(ds2, q2, TN, preferred_element_type=F32)
                dv2 = lax.dot_general(pn2, do2, TN, preferred_element_type=F32)
                for h in range(2):
                    dk += place(dk2[2 * BLOCK * h:2 * BLOCK * (h + 1)], h, kh)
                    dv += place(dv2[2 * BLOCK * h:2 * BLOCK * (h + 1)], h, kh)
                for r in range(2):
                    j = 2 * kh + r
                    dq_ref[:, 128 * j:128 * (j + 1)] = _rope_t(dq2[BLOCK * r:BLOCK * (r + 1)], c, a, b).astype(BF16)
            ds_ref[...] += dsink
            dk_prev = _rope_t(dk[:BLOCK], pc_ref[...], pa_ref[...], pb_ref[...])
            dk_cur = _rope_t(dk[BLOCK:], c, a, b)
            prev = jnp.concatenate([dk_prev, dv[:BLOCK]], axis=1)
            dkv_ref[...] = (carry_ref[...] + prev).astype(BF16)
            carry_ref[...] = jnp.concatenate([dk_cur, dv[BLOCK:]], axis=1)

    return pl.pallas_call(
        body, grid=(N_BLOCKS + 1,),
        in_specs=_attn_specs(clamp) + [pl.BlockSpec((BLOCK, ATTN_W), lambda i: (clamp(i), 0))],
        out_specs=[pl.BlockSpec((BLOCK, ATTN_W), lambda i: (clamp(i), 0)),
                   pl.BlockSpec((BLOCK, 2 * KV_W), lambda i: (jnp.maximum(i - 1, 0), 0)),
                   pl.BlockSpec((1, 128), lambda i: (0, 0))],
        out_shape=[jax.ShapeDtypeStruct((SEQ, ATTN_W), BF16), jax.ShapeDtypeStruct((SEQ, 2 * KV_W), BF16),
                   jax.ShapeDtypeStruct((1, 128), F32)],
        scratch_shapes=[pltpu.VMEM((BLOCK, 2 * KV_W), F32)], compiler_params=_cp(), name="attn_bwd",
    )(q, kv, kv, tc, ta, tb, tc, ta, tb, sinks, do)


def _shift_down(z, k):
    row = lax.broadcasted_iota(jnp.int32, z.shape, 0)
    return jnp.where(row < k, 0.0, pltpu.roll(z, k, axis=0))


def _shift_up(z, k):
    n = z.shape[0]
    row = lax.broadcasted_iota(jnp.int32, z.shape, 0)
    return jnp.where(row >= n - k, 0.0, pltpu.roll(z, n - k, axis=0))


def _conv_specs():
    nb = WIDTH // 128
    return [pl.BlockSpec((SEQ, 128), lambda j: (0, j)), pl.BlockSpec((SEQ, 128), lambda j: (0, nb + j)),
            pl.BlockSpec((SEQ, 128), lambda j: (0, 2 * nb + j)), pl.BlockSpec((None, 8, 128), lambda j: (0, 0, j))]


def _conv_bwd(cbx, cw, layer, dout, tie):
    def body(cb_ref, cc_ref, cx_ref, w_ref, do_ref, tie_ref, dcb_ref, dcc_ref, dcx_ref, dw_ref):
        cc, cx = cc_ref[...], cx_ref[...]
        z = cc * cx
        z1, z2 = _shift_down(z, 1), _shift_down(z, 2)
        w0, w1, w2 = w_ref[0:1, :], w_ref[1:2, :], w_ref[2:3, :]
        dout = do_ref[...]
        ds = dout * cb_ref[...]
        dcb_ref[...] = (dout * (w0 * z2 + w1 * z1 + w2 * z)).astype(BF16)
        dz = w2 * ds + w1 * _shift_up(ds, 1) + w0 * _shift_up(ds, 2)
        dcc_ref[...] = (dz * cx).astype(BF16)
        dcx_ref[...] = (dz * cc).astype(BF16)
        rows = [jnp.sum(ds * zz, axis=0, keepdims=True) for zz in (z2, z1, z)]
        dw_ref[...] = jnp.concatenate(rows + [jnp.zeros((5, 128), F32)], axis=0)

    col = lambda j: (0, j)
    specs = _conv_specs()
    specs[3] = pl.BlockSpec((None, 8, 128), lambda j: (layer, 0, j))
    return pl.pallas_call(
        body, grid=(WIDTH // 128,), in_specs=specs + [pl.BlockSpec((SEQ, 128), col), ANY],
        out_specs=[pl.BlockSpec((SEQ, 128), col), pl.BlockSpec((SEQ, 128), col), pl.BlockSpec((SEQ, 128), col),
                   pl.BlockSpec((8, 128), col)],
        out_shape=[jax.ShapeDtypeStruct((SEQ, WIDTH), BF16)] * 3 + [jax.ShapeDtypeStruct((8, WIDTH), F32)],
        compiler_params=_cp(), name="conv_bwd",
    )(cbx, cbx, cbx, cw, dout, tie)


def _ssm_prep_math(a_re, a_im, log_dt, bt_re, bt_im):
    dt = jnp.exp(log_dt)
    er = jnp.exp(a_re * dt)
    lr = er * jnp.cos(a_im * dt)
    li = er * jnp.sin(a_im * dt)
    n2 = a_re * a_re + a_im * a_im
    cr = ((lr - 1.0) * a_re + li * a_im) / n2
    ci = (li * a_re - (lr - 1.0) * a_im) / n2
    cr3, ci3 = cr[:, None, :], ci[:, None, :]
    return lr, li, cr3 * bt_re - ci3 * bt_im, cr3 * bt_im + ci3 * bt_re


_GS = (SSM_GROUPS, SSM_STATE)
_GHS = (SSM_GROUPS, SSM_GROUP, SSM_STATE)


def _layered(shape):
    return pl.BlockSpec((None,) + shape, lambda l: (l,) + (0,) * len(shape))


def _ssm_prep(a_re, a_im, log_dt, bt_re, bt_im):
    def body(ar, ai, ld, br, bi, o0, o1, o2, o3):
        outs = _ssm_prep_math(ar[...], ai[...], ld[...], br[...], bi[...])
        for o, v in zip((o0, o1, o2, o3), outs):
            o[...] = v

    shapes = [_GS, _GS, _GHS, _GHS]
    return pl.pallas_call(
        body, grid=(DEPTH,), in_specs=[_layered(s) for s in (_GS, _GS, (SSM_GROUPS, 1), _GHS, _GHS)],
        out_specs=[_layered(s) for s in shapes],
        out_shape=[jax.ShapeDtypeStruct((DEPTH,) + s, F32) for s in shapes],
        name="ssm_prep")(a_re, a_im, log_dt, bt_re, bt_im)


def _ssm_prep_bwd(a_re, a_im, log_dt, bt_re, bt_im, cots):
    def body(ar, ai, ld, br, bi, c0, c1, c2, c3, o0, o1, o2, o3, o4):
        _, vjp = jax.vjp(_ssm_prep_math, ar[...], ai[...], ld[...], br[...], bi[...])
        for o, v in zip((o0, o1, o2, o3, o4), vjp((c0[...], c1[...], c2[...], c3[...]))):
            o[...] = v

    ins = (_GS, _GS, (SSM_GROUPS, 1), _GHS, _GHS)
    return pl.pallas_call(
        body, grid=(DEPTH,), in_specs=[_layered(s) for s in ins + (_GS, _GS, _GHS, _GHS)],
        out_specs=[_layered(s) for s in ins],
        out_shape=[jax.ShapeDtypeStruct((DEPTH,) + s, F32) for s in ins],
        name="ssm_prep_bwd")(a_re, a_im, log_dt, bt_re, bt_im, *cots)


LANES_G = 512
N_LANE_GROUPS = SSM_GROUPS * SSM_STATE // LANES_G


def _ssm_embed(b_re, b_im, c_re, c_im):
    rows = SSM_GROUPS * SSM_GROUP

    def body(br, bi, cr, ci, o0, o1, o2, o3):
        state = lax.broadcasted_iota(jnp.int32, (SSM_STATE, LANES_G), 0)
        lane = lax.broadcasted_iota(jnp.int32, (SSM_STATE, LANES_G), 1)
        spread = (lane % SSM_STATE == state).astype(BF16)
        r = lax.broadcasted_iota(jnp.int32, (rows, LANES_G), 0)
        c = lax.broadcasted_iota(jnp.int32, (rows, LANES_G), 1)
        own = (r % 128) // SSM_GROUP == c // SSM_STATE
        for ref, o, sign in ((br, o0, 1.0), (bi, o1, 1.0), (cr, o2, 1.0), (ci, o3, -1.0)):
            t = (sign * ref[...]).reshape(rows, SSM_STATE).astype(BF16)
            wide = jnp.dot(t, spread, preferred_element_type=F32)
            o[...] = jnp.where(own, wide, 0.0).astype(BF16).reshape(N_LANE_GROUPS, 128, LANES_G)

    out = (N_LANE_GROUPS, 128, LANES_G)
    return pl.pallas_call(
        body, grid=(DEPTH,), in_specs=[_layered(_GHS)] * 4, out_specs=[_layered(out)] * 4,
        out_shape=[jax.ShapeDtypeStruct((DEPTH,) + out, BF16)] * 4, name="ssm_embed")(b_re, b_im, c_re, c_im)


def _scan_in_place(xr_ref, xi_ref, ar, ai, reverse):
    shape = (N_CHUNKS, xr_ref.shape[1])
    ar, ai = jnp.broadcast_to(ar, shape), jnp.broadcast_to(ai, shape)

    def rows(tau):
        t = (CHUNK - 1 - tau) if reverse else tau
        return pl.ds(pl.multiple_of(t * N_CHUNKS, N_CHUNKS), N_CHUNKS)

    def step(tau, carry):
        sr, si = carry
        return ar * sr - ai * si + xr_ref[rows(tau), :], ar * si + ai * sr + xi_ref[rows(tau), :]

    zero = jnp.zeros(shape, F32)
    er, ei = lax.fori_loop(0, CHUNK, step, (zero, zero), unroll=8)
    qr, qi = ar, ai
    for _ in range(8):
        qr, qi = qr * qr - qi * qi, 2.0 * qr * qi
    shift = _shift_up if reverse else _shift_down
    for k in (1, 2, 4):
        sr, si = shift(er, k), shift(ei, k)
        er, ei = er + qr * sr - qi * si, ei + qr * si + qi * sr
        qr, qi = qr * qr - qi * qi, 2.0 * qr * qi
    start = (shift(er, 1), shift(ei, 1))

    def write(tau, carry):
        sr, si = step(tau, carry)
        xr_ref[rows(tau), :] = sr
        xi_ref[rows(tau), :] = si
        return sr, si

    return write, start


def _ssm_specs(layer):
    col = lambda w: pl.BlockSpec((SEQ, w), lambda g: (0, g))
    diag = pl.BlockSpec((None, None, 128, LANES_G), lambda g: (layer, g, 0, 0))
    vec = pl.BlockSpec((None, 1, LANES_G), lambda g: (layer, 0, g))
    return col, diag, vec


def _to_scan_order(src_ref, dst_ref):
    for tau in range(CHUNK):
        dst_ref[pl.ds(tau * N_CHUNKS, N_CHUNKS), :] = src_ref[pl.ds(tau, N_CHUNKS, stride=CHUNK), :]


def _to_time_order(src_ref, dst_ref, dtype):
    for j in range(N_CHUNKS):
        dst_ref[pl.ds(j * CHUNK, CHUNK), :] = src_ref[pl.ds(j, CHUNK, stride=N_CHUNKS), :].astype(dtype)


def _ssm_fwd(u, mats, layer, d):
    def body(u_ref, d_ref, br_ref, bi_ref, cr_ref, ci_ref, ar_ref, ai_ref, xr_ref, xi_ref, y_ref, us_ref):
        _to_scan_order(u_ref, us_ref)
        uv = us_ref[...].astype(BF16)
        xr_ref[...] = jnp.dot(uv, br_ref[...], preferred_element_type=F32)
        xi_ref[...] = jnp.dot(uv, bi_ref[...], preferred_element_type=F32)
        write, start = _scan_in_place(xr_ref, xi_ref, ar_ref[...], ai_ref[...], False)
        lax.fori_loop(0, CHUNK, write, start, unroll=8)
        y = lax.dot_general(xr_ref[...].astype(BF16), cr_ref[...], NT, preferred_element_type=F32)
        y += lax.dot_general(xi_ref[...].astype(BF16), ci_ref[...], NT, preferred_element_type=F32)
        us_ref[...] = y + d_ref[...] * us_ref[...]
        _to_time_order(us_ref, y_ref, F32)

    col, diag, vec = _ssm_specs(layer)
    return pl.pallas_call(
        body, grid=(N_LANE_GROUPS,),
        in_specs=[col(128), pl.BlockSpec((None, 1, 128), lambda g: (layer, 0, g)),
                  diag, diag, diag, diag, vec, vec],
        out_specs=[col(LANES_G), col(LANES_G), col(128)],
        out_shape=[jax.ShapeDtypeStruct((SEQ, SSM_GROUPS * SSM_STATE), F32)] * 2
        + [jax.ShapeDtypeStruct((SEQ, WIDTH), F32)],
        scratch_shapes=[pltpu.VMEM((SEQ, 128), F32)], compiler_params=_cp(), name="ssm_fwd",
    )(u, d, mats["b_re"], mats["b_im"], mats["c_re"], mats["c_im_neg"], mats["a_re"], mats["a_im"])


def _ssm_bwd(dy, x_re, x_im, u, mats, layer, d):
    def body(dyt_ref, ut_ref, d_ref, xr_ref, xi_ref, br_ref, bi_ref, cr_ref, ci_ref, ar_ref, ai_ref,
             du_ref, dar_ref, dai_ref, dbr_ref, dbi_ref, dcr_ref, dci_ref, lr_ref, li_ref, dys_ref, u_ref):
        _to_scan_order(dyt_ref, dys_ref)
        _to_scan_order(ut_ref, u_ref)
        dy = dys_ref[...].astype(BF16)
        lr_ref[...] = jnp.dot(dy, cr_ref[...], preferred_element_type=F32)
        li_ref[...] = jnp.dot(dy, ci_ref[...], preferred_element_type=F32)
        write, start = _scan_in_place(lr_ref, li_ref, ar_ref[...], -ai_ref[...], True)

        def rows(t):
            return pl.ds(pl.multiple_of(t * N_CHUNKS, N_CHUNKS), N_CHUNKS)

        def grad(acc, lam, xpr, xpi):
            return acc[0] + xpr * lam[0] + xpi * lam[1], acc[1] + xpr * lam[1] - xpi * lam[0]

        def down(tau, carry):
            lam = write(tau, carry[0])
            t = CHUNK - 2 - tau
            return lam, grad(carry[1], lam, xr_ref[rows(t), :], xi_ref[rows(t), :])

        zero = jnp.zeros((N_CHUNKS, LANES_G), F32)
        lam, acc = lax.fori_loop(0, CHUNK - 1, down, (start, (zero, zero)), unroll=5)
        lam = write(CHUNK - 1, lam)
        last = rows(CHUNK - 1)
        acc = grad(acc, lam, _shift_down(xr_ref[last, :], 1), _shift_down(xi_ref[last, :], 1))
        dar_ref[...] = jnp.sum(acc[0], axis=0, keepdims=True)
        dai_ref[...] = jnp.sum(acc[1], axis=0, keepdims=True)

        l_re, l_im = lr_ref[...].astype(BF16), li_ref[...].astype(BF16)
        du = lax.dot_general(l_re, br_ref[...], NT, preferred_element_type=F32)
        du += lax.dot_general(l_im, bi_ref[...], NT, preferred_element_type=F32)
        dys_ref[...] = du + dys_ref[...] * d_ref[...]
        _to_time_order(dys_ref, du_ref, BF16)
        uv = u_ref[...].astype(BF16)
        dbr_ref[...] = lax.dot_general(uv, l_re, TN, preferred_element_type=F32)
        dbi_ref[...] = lax.dot_general(uv, l_im, TN, preferred_element_type=F32)
        dcr_ref[...] = lax.dot_general(dy, xr_ref[...].astype(BF16), TN, preferred_element_type=F32)
        dci_ref[...] = lax.dot_general(dy, xi_ref[...].astype(BF16), TN, preferred_element_type=F32)

    col, diag, vec = _ssm_specs(layer)
    out_vec = pl.BlockSpec((1, LANES_G), lambda g: (0, g))
    out_blk = pl.BlockSpec((None, 128, LANES_G), lambda g: (g, 0, 0))
    sds = jax.ShapeDtypeStruct
    return pl.pallas_call(
        body, grid=(N_LANE_GROUPS,),
        in_specs=[col(128), col(128), pl.BlockSpec((None, 1, 128), lambda g: (layer, 0, g)),
                  col(LANES_G), col(LANES_G), diag, diag, diag, diag, vec, vec],
        out_specs=[col(128), out_vec, out_vec, out_blk, out_blk, out_blk, out_blk],
        out_shape=[sds((SEQ, WIDTH), BF16)] + [sds((1, SSM_GROUPS * SSM_STATE), F32)] * 2
        + [sds((N_LANE_GROUPS, 128, LANES_G), F32)] * 4,
        scratch_shapes=[pltpu.VMEM((SEQ, LANES_G), F32)] * 2 + [pltpu.VMEM((SEQ, 128), F32)] * 2,
        compiler_params=_cp(), name="ssm_bwd",
    )(dy, u, d, x_re, x_im, mats["b_re"], mats["b_im"], mats["c_re"], mats["c_im_neg"],
      mats["a_re"], mats["a_im"])


_GELU_C = math.sqrt(2.0 / math.pi)


def _gelu(y):
    return 0.5 * y * (1.0 + jnp.tanh(_GELU_C * (y + 0.044715 * (y * y * y))))


def _glu_fwd(y, wglu):
    tt = 512

    def body(y_ref, w_ref, z_ref):
        ys = _gelu(y_ref[...])
        a = jnp.dot(ys.astype(BF16), w_ref[...], preferred_element_type=F32)
        z_ref[...] = (ys * jax.nn.sigmoid(a)).astype(BF16)

    blk = pl.BlockSpec((tt, WIDTH), lambda i: (i, 0))
    return pl.pallas_call(body, grid=(SEQ // tt,), in_specs=[blk, _full((WIDTH, WIDTH))], out_specs=blk,
                          out_shape=jax.ShapeDtypeStruct((SEQ, WIDTH), BF16), compiler_params=_cp(),
                          name="glu_fwd")(y, wglu)


def _glu_bwd(y, wglu, dz, u):
    tt = 512

    def body(y_ref, w_ref, dz_ref, u_ref, dy_ref, ys_ref, da_ref, dd_ref):
        @pl.when(pl.program_id(0) == 0)
        def _():
            dd_ref[...] = jnp.zeros_like(dd_ref)

        yv = y_ref[...]
        t = jnp.tanh(_GELU_C * (yv + 0.044715 * (yv * yv * yv)))
        ys = 0.5 * yv * (1.0 + t)
        ysb = ys.astype(BF16)
        sg = jax.nn.sigmoid(jnp.dot(ysb, w_ref[...], preferred_element_type=F32))
        dz = dz_ref[...].astype(F32)
        da = (dz * ys * sg * (1.0 - sg)).astype(BF16)
        dys = dz * sg + lax.dot_general(da, w_ref[...], NT, preferred_element_type=F32)
        dy = dys * (0.5 * (1.0 + t) + 0.5 * yv * (1.0 - t * t) * _GELU_C * (1.0 + 3 * 0.044715 * (yv * yv)))
        dy_ref[...] = dy
        ys_ref[...] = ysb
        da_ref[...] = da
        dd_ref[...] += jnp.sum(dy * u_ref[...], axis=0, keepdims=True)

    blk = pl.BlockSpec((tt, WIDTH), lambda i: (i, 0))
    return pl.pallas_call(
        body, grid=(SEQ // tt,), in_specs=[blk, _full((WIDTH, WIDTH)), blk, blk],
        out_specs=[blk, blk, blk, _full((1, WIDTH))],
        out_shape=[jax.ShapeDtypeStruct((SEQ, WIDTH), F32)] + [jax.ShapeDtypeStruct((SEQ, WIDTH), BF16)] * 2
        + [jax.ShapeDtypeStruct((1, WIDTH), F32)],
        compiler_params=_cp(), name="glu_bwd")(y, wglu, dz, u)


def _mix_specs(tt, layer):
    row = lambda w: pl.BlockSpec((tt, w), lambda i: (i, 0))
    gate = lambda j: pl.BlockSpec((tt, D_MODEL), lambda i: (i, j))
    wo = lambda j: pl.BlockSpec((D_MODEL, WIDTH), lambda i: (0, j))
    return [row(D_MODEL), row(WIDTH), row(WIDTH), row(WIDTH), gate(0), gate(1), gate(2),
            pl.BlockSpec((None, 1, GATE_W), lambda i: (layer, 0, 0)), wo(0), wo(1), wo(2),
            _full((D_MODEL, D_MODEL))]


def _mix_branches(o_ref, c_ref, z_ref, g_refs, b_ref, wa_ref, wc_ref, ws_ref):
    ys = [lax.dot_general(r[...], w[...], NT, preferred_element_type=F32)
          for r, w in ((o_ref, wa_ref), (c_ref, wc_ref), (z_ref, ws_ref))]
    gates = [jax.nn.sigmoid(g_refs[j][...] + b_ref[:, D_MODEL * j:D_MODEL * (j + 1)]) for j in range(3)]
    return ys, gates


def _mix_fwd(x, o, cv, z, glog, b_gate, layer, wbt, wmix, tie):
    tt = 256

    def body(x_ref, o_ref, c_ref, z_ref, g0, g1, g2, b_ref, wa_ref, wc_ref, ws_ref, wm_ref, tie_ref, x1_ref):
        ys, gates = _mix_branches(o_ref, c_ref, z_ref, (g0, g1, g2), b_ref, wa_ref, wc_ref, ws_ref)
        merged = gates[0] * ys[0] + gates[1] * ys[1] + gates[2] * ys[2]
        x1_ref[...] = x_ref[...] + jnp.dot(merged.astype(BF16), wm_ref[...], preferred_element_type=F32)

    return pl.pallas_call(
        body, grid=(SEQ // tt,), in_specs=_mix_specs(tt, layer) + [ANY],
        out_specs=pl.BlockSpec((tt, D_MODEL), lambda i: (i, 0)),
        out_shape=jax.ShapeDtypeStruct((SEQ, D_MODEL), F32), compiler_params=_cp(), name="mix_fwd",
    )(x, o, cv, z, glog, glog, glog, b_gate, wbt, wbt, wbt, wmix, tie)


def _mix_bwd(dx1, o, cv, z, glog, b_gate, layer, wbt, wmix, tie):
    tt = 256

    def body(dx_ref, o_ref, c_ref, z_ref, g0, g1, g2, b_ref, wa_ref, wc_ref, ws_ref, wm_ref, tie_ref,
             mg_ref, dya_ref, dyc_ref, dys_ref, do_ref, dc_ref, dz_ref, dgl_ref, db_ref):
        @pl.when(pl.program_id(0) == 0)
        def _():
            db_ref[...] = jnp.zeros_like(db_ref)

        ys, gates = _mix_branches(o_ref, c_ref, z_ref, (g0, g1, g2), b_ref, wa_ref, wc_ref, ws_ref)
        mg_ref[...] = (gates[0] * ys[0] + gates[1] * ys[1] + gates[2] * ys[2]).astype(BF16)
        dm = lax.dot_general(dx_ref[...].astype(BF16), wm_ref[...], NT, preferred_element_type=F32)
        for j, (dy_ref, w_ref, d_ref) in enumerate(((dya_ref, wa_ref, do_ref), (dyc_ref, wc_ref, dc_ref),
                                                    (dys_ref, ws_ref, dz_ref))):
            dy = (dm * gates[j]).astype(BF16)
            dy_ref[...] = dy
            d_ref[...] = jnp.dot(dy, w_ref[...], preferred_element_type=F32)
            dgl = dm * ys[j] * gates[j] * (1.0 - gates[j])
            dgl_ref[:, D_MODEL * j:D_MODEL * (j + 1)] = dgl.astype(BF16)
            db_ref[:, D_MODEL * j:D_MODEL * (j + 1)] += jnp.sum(dgl, axis=0, keepdims=True)

    row = lambda w: pl.BlockSpec((tt, w), lambda i: (i, 0))
    sds = jax.ShapeDtypeStruct
    return pl.pallas_call(
        body, grid=(SEQ // tt,), in_specs=_mix_specs(tt, layer) + [ANY],
        out_specs=[row(D_MODEL)] * 4 + [row(WIDTH)] * 3 + [row(GATE_W), _full((1, GATE_W))],
        out_shape=[sds((SEQ, D_MODEL), BF16)] * 4 + [sds((SEQ, WIDTH), F32)] * 3
        + [sds((SEQ, GATE_W), BF16), sds((1, GATE_W), F32)],
        compiler_params=_cp(), name="mix_bwd",
    )(dx1, o, cv, z, glog, glog, glog, b_gate, wbt, wbt, wbt, wmix, tie)


def _ffn_out_fwd(x1, act, wout, tie):
    tt = 512

    def body(x_ref, a_ref, w_ref, tie_ref, o_ref):
        o_ref[...] = x_ref[...] + jnp.dot(a_ref[...], w_ref[...], preferred_element_type=F32)

    row = lambda w: pl.BlockSpec((tt, w), lambda i: (i, 0))
    return pl.pallas_call(
        body, grid=(SEQ // tt,), in_specs=[row(D_MODEL), row(FFN_H), _full((FFN_H, D_MODEL)), ANY],
        out_specs=row(D_MODEL), out_shape=jax.ShapeDtypeStruct((SEQ, D_MODEL), F32),
        compiler_params=_cp(), name="ffn_out_fwd")(x1, act, wout, tie)


def _ffn_out_bwd(dx2, up, silu, dsilu, wout, tie):
    tt = 512

    def body(dx_ref, up_ref, silu_ref, dsilu_ref, w_ref, tie_ref, dgu_ref):
        dact = lax.dot_general(dx_ref[...].astype(BF16), w_ref[...], NT, preferred_element_type=F32).astype(BF16)
        dgu_ref[:, :FFN_H] = dact * up_ref[...] * dsilu_ref[...]
        dgu_ref[:, FFN_H:] = dact * silu_ref[...]

    row = lambda w: pl.BlockSpec((tt, w), lambda i: (i, 0))
    return pl.pallas_call(
        body, grid=(SEQ // tt,),
        in_specs=[row(D_MODEL), row(FFN_H), row(FFN_H), row(FFN_H), _resident((FFN_H, D_MODEL)), ANY],
        out_specs=row(2 * FFN_H), out_shape=jax.ShapeDtypeStruct((SEQ, 2 * FFN_H), BF16),
        compiler_params=_cp(), name="ffn_out_bwd")(dx2, up, silu, dsilu, wout, tie)


def _loss_head(x, g, target):
    tt = 256

    def body(x_ref, g_ref, t_ref, loss_ref, dx_ref, dg_ref):
        @pl.when(pl.program_id(0) == 0)
        def _():
            loss_ref[...] = jnp.zeros_like(loss_ref)
            dg_ref[...] = jnp.zeros_like(dg_ref)

        xv = x_ref[...]
        r = lax.rsqrt(jnp.mean(xv * xv, axis=-1, keepdims=True) + NORM_EPS)
        xh = xv * r
        err = xh * g_ref[...] - t_ref[...]
        loss_ref[...] += 0.5 * jnp.sum(jnp.mean(err * err, axis=-1, keepdims=True))
        dy = err * (1.0 / D_MODEL)
        gy = dy * g_ref[...]
        dx_ref[...] = r * (gy - xh * jnp.mean(gy * xh, axis=-1, keepdims=True))
        dg_ref[...] += jnp.sum(dy * xh, axis=0, keepdims=True)

    row = pl.BlockSpec((tt, D_MODEL), lambda i: (i, 0))
    return pl.pallas_call(
        body, grid=(SEQ // tt,), in_specs=[row, _full((1, D_MODEL)), row],
        out_specs=[_full((1, 128)), row, _full((1, D_MODEL))],
        out_shape=[jax.ShapeDtypeStruct((1, 128), F32), jax.ShapeDtypeStruct((SEQ, D_MODEL), F32),
                   jax.ShapeDtypeStruct((1, D_MODEL), F32)],
        compiler_params=_cp(), name="loss_head")(x, g, target)


def _adam_math(g, w, m, v):
    nm = B1 * m + (1.0 - B1) * g
    nv = B2 * v + (1.0 - B2) * (g * g)
    m_hat = nm / (1.0 - B1 ** STEP)
    v_hat = nv / (1.0 - B2 ** STEP)
    return -LR * (m_hat / (jnp.sqrt(v_hat) + ADAM_EPS) + WD * w), nm, nv


def _adamw_small(parts, w, m, v, name):
    def body(p_ref, w_ref, m_ref, v_ref, g_ref, d_ref, nm_ref, nv_ref):
        g = p_ref[0].astype(F32)
        for k in range(1, N_DEV):
            g = g + p_ref[k].astype(F32)
        g_ref[...] = g
        d_ref[...], nm_ref[...], nv_ref[...] = _adam_math(g, w_ref[...], m_ref[...], v_ref[...])

    out_shape = [jax.ShapeDtypeStruct(w.shape, F32)] * 4
    if w.ndim < 3:
        return pl.pallas_call(body, out_shape=out_shape, name=name)(parts, w, m, v)
    rest = w.shape[1:]
    zeros = (0,) * len(rest)
    blk = pl.BlockSpec((None,) + rest, lambda l: (l,) + zeros)
    return pl.pallas_call(
        body, grid=(w.shape[0],),
        in_specs=[pl.BlockSpec((N_DEV, None) + rest, lambda l: (0, l) + zeros), blk, blk, blk],
        out_specs=[blk] * 4, out_shape=out_shape, name=name)(parts, w, m, v)


def _adamw(parts, w, m, v, tr, name, groups=None, fill=None, tie=None):
    n_groups, rows, cols = w.shape
    n_parts = parts.shape[1]
    lo, hi = groups if groups is not None else (0, n_groups)

    def body(p_ref, w_ref, m_ref, v_ref, *rest):
        g_ref, d_ref, nm_ref, nv_ref = rest[-4:]
        g = p_ref[0].astype(F32)
        for k in range(1, n_parts):
            g = g + p_ref[k].astype(F32)
        nm = B1 * m_ref[...] + (1.0 - B1) * g
        nv = B2 * v_ref[...] + (1.0 - B2) * (g * g)
        m_hat = nm / (1.0 - B1 ** STEP)
        v_hat = nv / (1.0 - B2 ** STEP)
        g_ref[...] = g
        d_ref[...] = -LR * (m_hat / (jnp.sqrt(v_hat) + ADAM_EPS) + WD * w_ref[...])
        nm_ref[...] = nm
        nv_ref[...] = nv

    blk = pl.BlockSpec((None, tr, cols), lambda l, i: (l + lo, i, 0))
    p_lo = lo if parts.shape[0] == n_groups else 0
    extra = ([] if fill is None else list(fill)) + ([] if tie is None else [tie])
    return pl.pallas_call(
        body, grid=(hi - lo, rows // tr),
        in_specs=[pl.BlockSpec((None, n_parts, tr, cols), lambda l, i: (l + p_lo, 0, i, 0)), blk, blk, blk]
        + [ANY] * len(extra),
        out_specs=[blk] * 4, out_shape=[jax.ShapeDtypeStruct((n_groups, rows, cols), F32)] * 4,
        input_output_aliases={} if fill is None else {4 + j: j for j in range(4)},
        compiler_params=_cp(), name=name)(parts, w, m, v, *extra)


BRANCHES = ("w_attn_o", "w_conv_o", "w_ssm_o")


def _adamw_branches(parts, wmv, name, groups, fill=None, tie=None):
    n_parts = parts.shape[1]
    lo, hi = groups

    def body(p_ref, *refs):
        ins, outs = refs[:9], refs[-12:]
        g = p_ref[0].astype(F32)
        for k in range(1, n_parts):
            g = g + p_ref[k].astype(F32)
        for j in range(3):
            gj = g[:, j * WIDTH:(j + 1) * WIDTH].T
            w_ref, m_ref, v_ref = ins[3 * j:3 * j + 3]
            d, nm, nv = _adam_math(gj, w_ref[...], m_ref[...], v_ref[...])
            for o, val in zip(outs[4 * j:4 * j + 4], (gj, d, nm, nv)):
                o[...] = val

    shard = wmv[0].shape[1:]
    blk = pl.BlockSpec((None,) + shard, lambda l: (l + lo, 0, 0))
    p_lo = lo if parts.shape[0] == DEPTH else 0
    extra = ([] if fill is None else list(fill)) + ([] if tie is None else [tie])
    return pl.pallas_call(
        body, grid=(hi - lo,),
        in_specs=[pl.BlockSpec((None,) + parts.shape[1:], lambda l: (l + p_lo, 0, 0, 0))] + [blk] * 9
        + [ANY] * len(extra),
        out_specs=[blk] * 12, out_shape=[jax.ShapeDtypeStruct((DEPTH,) + shard, F32)] * 12,
        input_output_aliases={} if fill is None else {10 + j: j for j in range(12)},
        compiler_params=_cp(), name=name)(parts, *wmv, *extra)


def _split_start(name, arrays, n_sems, plan, after=None):
    n = len(arrays)
    order = [] if after is None else [after]
    n_in = n + len(order)

    def body(*refs):
        ins, send_sems, recv_sems, token = refs[:n], refs[n_in], refs[n_in + 1], refs[-1]
        for src, dst, k, to in plan(ins)[0]:
            pltpu.make_async_remote_copy(src_ref=src, dst_ref=dst, send_sem=send_sems.at[k], recv_sem=recv_sems.at[k],
                                         device_id=to, device_id_type=MESH_ID).start()
        token[...] = jnp.zeros_like(token)

    outs = pl.pallas_call(
        body, name=name,
        out_shape=(pltpu.SemaphoreType.DMA((n_sems,)), pltpu.SemaphoreType.DMA((n_sems,)),
                   *[pltpu.HBM(a.shape, a.dtype) for a in arrays], jax.ShapeDtypeStruct((8, 128), F32)),
        in_specs=[HBM] * n + [ANY] * len(order),
        out_specs=(SEM, SEM, *[HBM] * n, pl.BlockSpec(memory_space=pltpu.VMEM)),
        input_output_aliases={i: 2 + i for i in range(n)},
        compiler_params=pltpu.CompilerParams(has_side_effects=EFFECT),
    )(*[pltpu.with_memory_space_constraint(a, pltpu.HBM) for a in arrays], *order)
    return outs[0], outs[1], list(outs[2:2 + n]), outs[-1]


def _split_wait(name, arrays, send_sems, recv_sems, after, plan):
    n = len(arrays)
    order = list(after) if isinstance(after, (list, tuple)) else [after]

    def body(*refs):
        ins, s_sems, r_sems = refs[:n], refs[n], refs[n + 1]
        sends, arrivals = plan(ins)
        x, y, c = lax.axis_index("x"), lax.axis_index("y"), lax.axis_index("c")
        for src, dst, k, to in sends:
            pltpu.make_async_remote_copy(src_ref=src, dst_ref=dst, send_sem=s_sems.at[k], recv_sem=r_sems.at[k],
                                         device_id=to, device_id_type=MESH_ID).wait_send()
        for dst, k in arrivals:
            pltpu.make_async_remote_copy(src_ref=dst, dst_ref=dst, send_sem=s_sems.at[k], recv_sem=r_sems.at[k],
                                         device_id=(x, y, c), device_id_type=MESH_ID).wait_recv()

    return pl.pallas_call(
        body, name=name, out_shape=[pltpu.HBM(a.shape, a.dtype) for a in arrays],
        in_specs=[HBM] * n + [SEM, SEM] + [ANY] * len(order), out_specs=[HBM] * n,
        input_output_aliases={i: i for i in range(n)},
        compiler_params=pltpu.CompilerParams(has_side_effects=EFFECT),
    )(*arrays, send_sems, recv_sems, *order)


def _chips():
    x, y, c = lax.axis_index("x"), lax.axis_index("y"), lax.axis_index("c")
    return x, y, c, [(1 - x, y), (x, 1 - y), (1 - x, 1 - y)]


def _plan_gather_chips(refs):
    x, y, c, chips = _chips()
    me = 4 * x + 2 * y + c
    n = len(refs) // 2
    sends, arrivals = [], []
    for i in range(n):
        src, land = refs[i], refs[n + i]
        sends.append((src, land.at[me], 4 * i, (x, y, 1 - c)))
        arrivals.append((land.at[4 * x + 2 * y + 1 - c], 4 * i))
        for j, (px, py) in enumerate(chips):
            sends.append((src, land.at[me], 4 * i + 1 + j, (px, py, c)))
            arrivals.append((land.at[4 * px + 2 * py + c], 4 * i + 1 + j))
    return sends, arrivals


def _plan_gather_pass(refs):
    x, y, c, chips = _chips()
    sends, arrivals = [], []
    for i in range(len(refs)):
        for j, (px, py) in enumerate(chips):
            slot = refs[i].at[4 * px + 2 * py + c]
            sends.append((slot, slot, 4 * i + j, (x, y, 1 - c)))
            arrivals.append((refs[i].at[4 * px + 2 * py + 1 - c], 4 * i + j))
        back = refs[i].at[4 * x + 2 * y + 1 - c]
        sends.append((back, back, 4 * i + 3, (x, y, 1 - c)))
        arrivals.append((refs[i].at[4 * x + 2 * y + c], 4 * i + 3))
    return sends, arrivals


def _plan_scatter_pair(refs):
    x, y, c = lax.axis_index("x"), lax.axis_index("y"), lax.axis_index("c")
    n = len(refs) // 2
    sends, arrivals = [], []
    for i in range(n):
        for q in range(4):
            sends.append((refs[i].at[q, 1 - c], refs[n + i].at[q], 4 * i + q, (x, y, 1 - c)))
            arrivals.append((refs[n + i].at[q], 4 * i + q))
    return sends, arrivals


def _plan_scatter_chips(layer):
    def plan(refs):
        x, y, c, chips = _chips()
        n = len(refs) // 2
        sends, arrivals = [], []
        for i in range(n):
            for j, (px, py) in enumerate(chips):
                sends.append((refs[i].at[2 * px + py], refs[n + i].at[layer, 2 * x + y], 3 * i + j, (px, py, c)))
                arrivals.append((refs[n + i].at[layer, 2 * px + py], 3 * i + j))
        return sends, arrivals

    return plan


def _pair_sum(parts4, from_pair, landing, layer, core, tr, name):
    _, _, rows, cols = parts4.shape

    def body(c_ref, p_ref, s_ref, l_ref, sum_ref, land_ref):
        v = (p_ref[...].astype(F32) + s_ref[...].astype(F32)).astype(BF16)
        sum_ref[...] = v
        land_ref[...] = v

    blk = pl.BlockSpec((None, tr, cols), lambda q, i, c_ref: (q, i, 0))
    return pl.pallas_call(
        body,
        grid_spec=pltpu.PrefetchScalarGridSpec(
            num_scalar_prefetch=1, grid=(4, rows // tr),
            in_specs=[pl.BlockSpec((None, None, tr, cols), lambda q, i, c_ref: (q, c_ref[0], i, 0)), blk, ANY],
            out_specs=[blk, pl.BlockSpec((None, None, tr, cols), lambda q, i, c_ref: (layer, q, i, 0))]),
        out_shape=[jax.ShapeDtypeStruct((4, rows, cols), BF16), jax.ShapeDtypeStruct(landing.shape, BF16)],
        input_output_aliases={3: 1}, compiler_params=_cp(), name=name,
    )(core, parts4, from_pair, landing)


def _travel_layout(t):
    tr = lambda a: jnp.swapaxes(a, 1, 2)
    branch = jnp.concatenate([tr(t["w_attn_o"]), tr(t["w_conv_o"]), tr(t["w_ssm_o"])], axis=2)
    return [tr(t["w_in"]), tr(t["w_ffn_in"]), t["w_ffn_out"], t["w_mix_o"], branch, t["w_ssm_glu"]]


def _native_layout(a):
    tr = lambda x: jnp.swapaxes(x, 1, 2)
    return {"w_in": tr(a[0]), "w_ffn_in": tr(a[1]), "w_ffn_out": a[2], "w_mix_o": a[3], "w_ssm_glu": a[5]}


def _rope_tabs():
    pos = jnp.arange(SEQ, dtype=F32)
    inv_freq = ROPE_THETA ** (-jnp.arange(0, ROT_DIM, 2, dtype=F32) / ROT_DIM)
    ang = pos[:, None] * inv_freq[None, :]
    cos, sin = jnp.cos(ang), jnp.sin(ang)
    one, zero = jnp.ones((SEQ, HEAD_DIM - ROT_DIM), F32), jnp.zeros((SEQ, HEAD_DIM - ROT_DIM), F32)
    z8 = jnp.zeros((SEQ, 8), F32)
    head = lambda *p: jnp.tile(jnp.concatenate(p, axis=1), (1, 2))
    return head(cos, cos, one), head(-sin, z8, zero), head(z8, sin, zero)


def _ssm_mats(sp):
    lr, li, bbr, bbi = _ssm_prep(sp["a_re"], sp["a_im"], sp["log_dt"], sp["bt_re"], sp["bt_im"])
    lanes = SSM_GROUPS * SSM_STATE
    b_re, b_im, c_re, c_im_neg = _ssm_embed(bbr, bbi, sp["c_re"], sp["c_im"])
    return {
        "a_re": lr.reshape(DEPTH, 1, lanes), "a_im": li.reshape(DEPTH, 1, lanes),
        "b_re": b_re, "b_im": b_im, "c_re": c_re, "c_im_neg": c_im_neg,
    }


def _layer_fwd(x, i, w, rp, mats, tabs, tie, hooks):
    q, kv, cbx, u, glog, cv, h = _rms_mm_in(x, rp["norm_mix"][i], w["win_t"], tabs, rp["conv_w"], i, tie)
    o = _attn_fwd(q, kv, tabs, rp["attn_sinks"][i] + hooks["post_in"](h)[0, 0])
    x_re, x_im, y = _ssm_fwd(u, mats, i, rp["ssm_d"])
    hooks["pre_glu"](y)
    z = _glu_fwd(y, w["wglu"])
    x1 = _mix_fwd(x, o, cv, z, glog, rp["b_gate"], i, w["branch_t"], w["wmix"], hooks["early"](z))
    hooks["pre_ffn"](x1)
    act, up, silu, dsilu, h2 = _rms_mm_ffn(x1, rp["norm_ffn"][i], w["wffn_t"])
    x2 = _ffn_out_fwd(x1, act, w["wout"], hooks["mid"](h2))
    kept = dict(x=x, q=q, kv=kv, cbx=cbx, u=u, glog=glog, h=h, o=o, cv=cv, z=z, y=y,
                x_re=x_re, x_im=x_im, x1=x1, act=act, up=up, silu=silu, dsilu=dsilu, h2=h2)
    return x2, kept


def _layer_bwd(dx2, k, i, w, rp, mats, tabs, tie, hooks):
    dgu = _ffn_out_bwd(dx2, k["up"], k["silu"], k["dsilu"], w["wout"], tie)
    g_wout = _mm_tn(k["act"], dx2, tm=FFN_H // 2, tn=1024, name="mm_tn_ffn_out")
    g_wffn_t = _mm_tn(dgu, k["h2"], tm=FFN_H // 2, tn=1024, name="mm_tn_ffn_in")
    dx1, d_norm_ffn = _mm_rmsbwd([dgu], w["wffn_t"], k["x1"], rp["norm_ffn"][i], dx2, "mm_rmsbwd_ffn")

    mg, dya, dyc, dys, do, dcv, dz, dgl, db_gate = _mix_bwd(
        dx1, k["o"], k["cv"], k["z"], k["glog"], rp["b_gate"], i, w["branch_t"], w["wmix"],
        hooks["mid"]((g_wffn_t, g_wout, d_norm_ffn)))
    g_wmix = _mm_tn(mg, dx1, tm=1024, tn=512, name="mm_tn_mix")
    g_branch_t = _tn_branches((dya, dyc, dys), (k["o"], k["cv"], k["z"]))

    dy, ys16, da16, dd = _glu_bwd(k["y"], w["wglu"], dz, k["u"])
    g_wglu = _mm_tn(ys16, da16, tm=256, tn=512, name="mm_tn_glu")
    du, da_re, da_im, db_re, db_im, dc_re, dc_im = _ssm_bwd(dy, k["x_re"], k["x_im"], k["u"], mats, i, rp["ssm_d"])

    dcb, dcc, dcx, d_conv_w = _conv_bwd(k["cbx"], rp["conv_w"], i, dcv, hooks["late"](du))
    dq, dkv, d_sinks = _attn_bwd(k["q"], k["kv"], tabs, rp["attn_sinks"][i], do)

    pieces = [dq, dkv, dcb, dcc, dcx, du, dgl]
    g_win_t = _tn_pieces(pieces, k["h"])
    dx, d_norm_mix = _mm_rmsbwd(pieces, w["win_t"], k["x"], rp["norm_mix"][i], dx1, "mm_rmsbwd_in")

    grads = [g_win_t, g_wffn_t, g_wout, g_wmix, g_branch_t, g_wglu]
    small = dict(norm_mix=d_norm_mix, b_gate=db_gate, attn_sinks=d_sinks, ssm_d=dd, norm_ffn=d_norm_ffn,
                 conv_w=d_conv_w, da_re=da_re, da_im=da_im, db_re=db_re, db_im=db_im, dc_re=dc_re, dc_im=dc_im)
    return dx, grads, small


def _ssm_diag(layers, signs):
    n_in = len(layers) * DEPTH

    def body(*refs):
        for k, out in enumerate(refs[n_in:]):
            for l in range(DEPTH):
                src = refs[k * DEPTH + l]
                for group in range(SSM_GROUPS):
                    g, a = divmod(group, LANES_G // SSM_STATE)
                    blk = src[g, pl.ds(a * SSM_GROUP, SSM_GROUP), pl.ds(a * SSM_STATE, SSM_STATE)]
                    out[l, group] = blk if signs[k] > 0 else -blk

    return pl.pallas_call(
        body, out_shape=[jax.ShapeDtypeStruct((DEPTH,) + _GHS, F32)] * len(layers),
        compiler_params=_cp(), name="ssm_diag")(*[x for kind in layers for x in kind])


def _replicated_grads(sg, sp):
    stack = lambda name: jnp.stack([sg[i][name] for i in range(DEPTH)])
    per_layer = lambda name: [sg[i][name] for i in range(DEPTH)]
    db_re, db_im, dc_re, dc_im = _ssm_diag([per_layer(n) for n in ("db_re", "db_im", "dc_re", "dc_im")],
                                           (1, 1, 1, -1))
    cots = (stack("da_re").reshape(DEPTH, *_GS), stack("da_im").reshape(DEPTH, *_GS), db_re, db_im)
    d_a_re, d_a_im, d_log_dt, d_bt_re, d_bt_im = _ssm_prep_bwd(
        sp["a_re"], sp["a_im"], sp["log_dt"], sp["bt_re"], sp["bt_im"], cots)
    sgrads = {"norm_mix": stack("norm_mix"), "b_gate": stack("b_gate"),
              "attn_sinks": stack("attn_sinks")[:, :, :N_Q_HEADS], "ssm_a_re": d_a_re, "ssm_a_im": d_a_im,
              "ssm_b_re": jnp.swapaxes(d_bt_re, 2, 3), "ssm_b_im": jnp.swapaxes(d_bt_im, 2, 3),
              "ssm_c_re": dc_re, "ssm_c_im": dc_im,
              "ssm_d": stack("ssm_d"), "ssm_log_dt": d_log_dt, "norm_ffn": stack("norm_ffn")}
    return sgrads, stack("conv_w")[:, :3]


def kernel(x, norm_mix, w_in, b_gate, attn_sinks, w_attn_o, conv_w, w_conv_o, ssm_a_re, ssm_a_im, ssm_b_re, ssm_b_im, ssm_c_re, ssm_c_im, ssm_d, ssm_log_dt, w_ssm_glu, w_ssm_o, w_mix_o, norm_ffn, w_ffn_in, w_ffn_out, norm_final, loss_target, m_norm_mix, m_w_in, m_b_gate, m_attn_sinks, m_w_attn_o, m_conv_w, m_w_conv_o, m_ssm_a_re, m_ssm_a_im, m_ssm_b_re, m_ssm_b_im, m_ssm_c_re, m_ssm_c_im, m_ssm_d, m_ssm_log_dt, m_w_ssm_glu, m_w_ssm_o, m_w_mix_o, m_norm_ffn, m_w_ffn_in, m_w_ffn_out, m_norm_final, v_norm_mix, v_w_in, v_b_gate, v_attn_sinks, v_w_attn_o, v_conv_w, v_w_conv_o, v_ssm_a_re, v_ssm_a_im, v_ssm_b_re, v_ssm_b_im, v_ssm_c_re, v_ssm_c_im, v_ssm_d, v_ssm_log_dt, v_w_ssm_glu, v_w_ssm_o, v_w_mix_o, v_norm_ffn, v_w_ffn_in, v_w_ffn_out, v_norm_final):
    big = {"w": dict(w_in=w_in, w_attn_o=w_attn_o, w_conv_o=w_conv_o, w_ssm_glu=w_ssm_glu, w_ssm_o=w_ssm_o,
                     w_mix_o=w_mix_o, w_ffn_in=w_ffn_in, w_ffn_out=w_ffn_out),
           "m": dict(w_in=m_w_in, w_attn_o=m_w_attn_o, w_conv_o=m_w_conv_o, w_ssm_glu=m_w_ssm_glu,
                     w_ssm_o=m_w_ssm_o, w_mix_o=m_w_mix_o, w_ffn_in=m_w_ffn_in, w_ffn_out=m_w_ffn_out),
           "v": dict(w_in=v_w_in, w_attn_o=v_w_attn_o, w_conv_o=v_w_conv_o, w_ssm_glu=v_w_ssm_glu,
                     w_ssm_o=v_w_ssm_o, w_mix_o=v_w_mix_o, w_ffn_in=v_w_ffn_in, w_ffn_out=v_w_ffn_out)}
    small = {"w": dict(norm_mix=norm_mix, b_gate=b_gate, attn_sinks=attn_sinks, ssm_a_re=ssm_a_re,
                       ssm_a_im=ssm_a_im, ssm_b_re=ssm_b_re, ssm_b_im=ssm_b_im, ssm_c_re=ssm_c_re,
                       ssm_c_im=ssm_c_im, ssm_d=ssm_d, ssm_log_dt=ssm_log_dt, norm_ffn=norm_ffn),
             "m": dict(norm_mix=m_norm_mix, b_gate=m_b_gate, attn_sinks=m_attn_sinks, ssm_a_re=m_ssm_a_re,
                       ssm_a_im=m_ssm_a_im, ssm_b_re=m_ssm_b_re, ssm_b_im=m_ssm_b_im, ssm_c_re=m_ssm_c_re,
                       ssm_c_im=m_ssm_c_im, ssm_d=m_ssm_d, ssm_log_dt=m_ssm_log_dt, norm_ffn=m_norm_ffn),
             "v": dict(norm_mix=v_norm_mix, b_gate=v_b_gate, attn_sinks=v_attn_sinks, ssm_a_re=v_ssm_a_re,
                       ssm_a_im=v_ssm_a_im, ssm_b_re=v_ssm_b_re, ssm_b_im=v_ssm_b_im, ssm_c_re=v_ssm_c_re,
                       ssm_c_im=v_ssm_c_im, ssm_d=v_ssm_d, ssm_log_dt=v_ssm_log_dt, norm_ffn=v_norm_ffn)}
    finals = {"w": norm_final, "m": m_norm_final, "v": v_norm_final}
    convs = {"w": conv_w, "m": m_conv_w, "v": v_conv_w}
    small_out_shapes = {name: a.shape for name, a in small["w"].items()}
    small_out_shapes.update(norm_final=(D_MODEL,), conv_w=(DEPTH, 3, 64))
    small_shapes = dict(small_out_shapes, norm_final=(1, D_MODEL), conv_w=(DEPTH, 3, WIDTH))
    dense = ("ssm_b_re", "ssm_b_im", "ssm_c_re", "ssm_c_im")
    for name in dense:
        small_shapes[name] = (DEPTH, SSM_GROUPS, SSM_GROUP * SSM_STATE)
    small_wmv = {name: [(convs[s] if name == "conv_w" else finals[s] if name == "norm_final" else small[s][name])
                        .reshape((DEPTH, 3, 64) if name == "conv_w" else small_shapes[name]) for s in "wmv"]
                 for name in small_shapes}
    mine = 4 * lax.axis_index("x") + 2 * lax.axis_index("y") + lax.axis_index("c")

    travel = {s: _travel_layout(big[s]) for s in "wmv"}
    stacked16 = list(zip(*[[a[0] for a in _travel_layout({n: w[i:i + 1].astype(BF16) for n, w in big["w"].items()})]
                           for i in range(DEPTH)]))
    rp = {"norm_mix": norm_mix[:, None], "norm_ffn": norm_ffn[:, None], "attn_sinks": attn_sinks[:, None],
          "b_gate": b_gate[:, None], "ssm_d": ssm_d[:, None]}
    sp = {"a_re": ssm_a_re, "a_im": ssm_a_im, "log_dt": ssm_log_dt[:, :, None],
          "bt_re": jnp.swapaxes(ssm_b_re, 2, 3), "bt_im": jnp.swapaxes(ssm_b_im, 2, 3),
          "c_re": ssm_c_re, "c_im": ssm_c_im}
    rows_tile = {"win_t": 368, "wffn_t": 352, "wout": 352, "wmix": 128, "branch_t": 128, "wglu": 64}
    core = lax.axis_index("c").astype(jnp.int32).reshape(1)
    no_tie = jnp.zeros((8, 128), F32)

    def landing_zones(srcs):
        return [lax.empty((N_DEV,) + s.shape, s.dtype) for s in srcs]

    def gather_chips(tag, i, kinds, after, extra=()):
        srcs = [stacked16[j][i] for j in kinds] + list(extra)
        s_sems, r_sems, arrays, token = _split_start(
            f"gather_chips_start_{tag}", srcs + landing_zones(srcs), 4 * len(srcs), _plan_gather_chips, after)
        return (tag, s_sems, r_sems, arrays), token

    def gather_pass(state, after):
        tag, s_sems, r_sems, arrays = state
        arrays = _split_wait(f"gather_chips_wait_{tag}", arrays, s_sems, r_sems, after, _plan_gather_chips)
        n = len(arrays) // 2
        s_sems, r_sems, lands, token = _split_start(
            f"gather_pass_start_{tag}", list(arrays[n:]), 4 * n, _plan_gather_pass)
        return (tag, s_sems, r_sems, lands), token

    def gather_done(state, after, kinds):
        tag, s_sems, r_sems, lands = state
        lands = _split_wait(f"gather_pass_wait_{tag}", lands, s_sems, r_sems, after, _plan_gather_pass)
        named = {KINDS[j][0]: a.reshape(N_DEV * KINDS[j][1], KINDS[j][2]) for a, j in zip(lands, kinds)}
        return named, list(lands[len(kinds):])

    all_kinds, mixer_kinds, ffn_kinds = tuple(range(len(KINDS))), (0, 3, 4, 5), (1, 2)
    no_hooks = {name: (lambda value: no_tie) for name in ("post_in", "pre_glu", "early", "pre_ffn", "mid", "late")}
    first_kinds, rest_kinds = mixer_kinds[:1], mixer_kinds[1:]
    state, token = gather_chips("0m", 0, first_kinds, None, extra=[jnp.pad(conv_w.reshape(6, 128), ((0, 2), (0, 0)))])
    mats = _ssm_mats(dict(sp, log_dt=sp["log_dt"] + token[0, 0]))
    tabs = _rope_tabs()
    early_work = list(mats.values()) + list(tabs) + [a for name in dense for a in small_wmv[name]]
    early_work += [stacked16[j][0] for j in ffn_kinds] + [stacked16[j][1] for j in mixer_kinds]
    state, _ = gather_pass(state, early_work)
    rest_state, tie = gather_chips("0b", 0, rest_kinds, state[3][0])
    ffn_state, tie = gather_chips("0f", 0, ffn_kinds, tie)
    w_next, (conv_all,) = gather_done(state, tabs[2], first_kinds)
    conv_full = conv_all[:, :6].reshape(N_DEV, DEPTH, 3, 64).transpose(1, 2, 0, 3).reshape(DEPTH, 3, WIDTH)
    rp["conv_w"] = jnp.pad(conv_full, ((0, 0), (0, 5), (0, 0)))

    act = x[0]
    weights, kept = [], []
    for i in range(DEPTH):
        w_i, hooks, held = w_next, dict(no_hooks), {}

        def early(value, ffn_state=ffn_state, held=held):
            held["ffn"], token = gather_pass(ffn_state, value)
            return token

        def pre_ffn(value, w_i=w_i, held=held):
            w_i.update(gather_done(held["ffn"], value, ffn_kinds)[0])

        hooks.update(early=early, pre_ffn=pre_ffn)
        if i == 0:
            def post_in(value, held=held):
                held["rest"], token = gather_pass(rest_state, value)
                return token

            def pre_glu(value, w_i=w_i, held=held):
                w_i.update(gather_done(held["rest"], value, rest_kinds)[0])

            hooks.update(post_in=post_in, pre_glu=pre_glu)
        if i + 1 < DEPTH:
            state, tie = gather_chips(f"{i + 1}m", i + 1, mixer_kinds, tie if i == 0 else w_i["win_t"])

            def mid(value, i=i, state=state, held=held):
                held["next"], token = gather_pass(state, value)
                held["next_ffn"], token = gather_chips(f"{i + 1}f", i + 1, ffn_kinds, token)
                return token

            hooks.update(mid=mid)
        act, k = _layer_fwd(act, i, w_i, rp, mats, tabs, tie, hooks)
        if i + 1 < DEPTH:
            w_next, _ = gather_done(held["next"], act, mixer_kinds)
            ffn_state, tie = held["next_ffn"], no_tie
        weights.append(w_i)
        kept.append(k)
    loss_row, dx, d_norm_final = _loss_head(act, norm_final[None], loss_target[0])

    landings = [lax.empty((DEPTH, 4, r, c), BF16) for _, r, c in KINDS]
    landings0 = [lax.empty((1, 4, r, c), BF16) for _, r, c in KINDS]

    def scatter_pair(tag, kinds, grads, after):
        parts4 = [g.reshape(4, 2, KINDS[j][1], KINDS[j][2]) for g, j in zip(grads, kinds)]
        zones = [lax.empty((4, KINDS[j][1], KINDS[j][2]), BF16) for j in kinds]
        s_sems, r_sems, arrays, token = _split_start(
            f"scatter_pair_start_{tag}", parts4 + zones, 4 * len(kinds), _plan_scatter_pair, after)
        return (tag, kinds, s_sems, r_sems, arrays), token

    def scatter_chips(state, lands, slot, after):
        tag, kinds, s_sems, r_sems, arrays = state
        arrays = _split_wait(f"scatter_pair_wait_{tag}", arrays, s_sems, r_sems, after, _plan_scatter_pair)
        n = len(kinds)
        sums, mine_lands = [], []
        for k, j in enumerate(kinds):
            name = KINDS[j][0]
            chip_sum, land = _pair_sum(arrays[k], arrays[n + k], lands[j], slot, core, KINDS[j][1],
                                       f"pair_sum_{name}")
            sums.append(chip_sum)
            mine_lands.append(land)
        s_sems, r_sems, arrays, token = _split_start(
            f"scatter_chips_start_{tag}", sums + mine_lands, 3 * n, _plan_scatter_chips(slot))
        return (tag, kinds, slot, s_sems, r_sems, arrays), token

    def scatter_done(state, lands, after):
        tag, kinds, slot, s_sems, r_sems, arrays = state
        arrays = _split_wait(f"scatter_chips_wait_{tag}", arrays, s_sems, r_sems, after, _plan_scatter_chips(slot))
        lands = list(lands)
        for k, j in enumerate(kinds):
            lands[j] = arrays[len(kinds) + k]
        return lands

    sg = [None] * DEPTH
    pending, tie = None, no_tie
    for i in reversed(range(DEPTH)):
        hooks, held = dict(no_hooks), {}
        if pending is not None:
            def mid(value, i=i, pending=pending, held=held):
                held["chips"], token = scatter_chips(pending, landings, i + 1, value[2])
                if i == 0:
                    held["ffn_pair"], token = scatter_pair("0f", ffn_kinds, value[:2], token)
                return token

            hooks.update(mid=mid)
        if i == 0:
            def late(value, held=held):
                held["ffn_chips"], token = scatter_chips(held["ffn_pair"], landings0, 0, value)
                return token

            hooks.update(late=late)
        dx, grads, sg[i] = _layer_bwd(dx, kept[i], i, weights[i], rp, mats, tabs, tie, hooks)
        if pending is not None:
            landings = scatter_done(held["chips"], landings, dx)
        if i > 0:
            pending, tie = scatter_pair(str(i), all_kinds, grads, dx)
        else:
            pending, _ = scatter_pair("0m", mixer_kinds, [grads[j] for j in mixer_kinds], dx)

    sgrads, conv_grad = _replicated_grads(sg, sp)

    small_names = list(REPLICATED) + ["norm_final", "conv_w"]
    sgrads.update(norm_final=d_norm_final, conv_w=conv_grad)
    small_src = [sgrads[name].reshape(small_shapes[name]).astype(BF16) for name in small_names]
    small_src.append(jnp.broadcast_to(loss_row[:, :1], (8, 128)))
    last, tie = scatter_chips(pending, landings0, 0, small_src[0])
    s_sems, r_sems, arrays, tie = _split_start(
        "gather_small_chips_start", small_src + landing_zones(small_src), 4 * len(small_src), _plan_gather_chips, tie)
    small_state = ("small", s_sems, r_sems, arrays)

    branch_wmv = [big[s][n] for n in BRANCHES for s in "wmv"]
    jb = [name for name, _, _ in KINDS].index("branch_t")

    def adamw(j, name, parts, label, groups, **kw):
        if j == jb:
            return _adamw_branches(parts, branch_wmv, label + name, groups, **kw)
        return _adamw(parts, travel["w"][j], travel["m"][j], travel["v"][j], rows_tile[name], label + name,
                      groups=groups, **kw)

    big_out = []
    for j, (name, _, _) in enumerate(KINDS):
        big_out.append(adamw(j, name, landings[j], "adamw_late_", (1, DEPTH), tie=tie))
        tie = big_out[-1][-1]
    landings0 = scatter_done(held["ffn_chips"], landings0, tie)
    landings0 = scatter_done(last, landings0, tie)
    small_state, _ = gather_pass(small_state, landings0[0])
    big_out = [adamw(j, name, landings0[j], "adamw_first_", (0, 1), fill=big_out[j])
               for j, (name, _, _) in enumerate(KINDS)]
    big_res = []
    for kind in range(4):
        res = _native_layout([None if j == jb else big_out[j][kind] for j in range(len(KINDS))])
        res.update({n: big_out[jb][4 * b + kind] for b, n in enumerate(BRANCHES)})
        big_res.append(res)

    _, sparts = gather_done(small_state, big_out[-1][0], ())
    loss = jnp.sum(sparts[-1][:, 0, 0])
    sparts = dict(zip(small_names, sparts))
    sparts["conv_w"] = lax.dynamic_slice_in_dim(sparts["conv_w"], mine * 64, 64, axis=3)
    small_res = {}
    for name in small_names:
        res = _adamw_small(sparts[name], *small_wmv[name], "adamw_" + name)
        small_res[name] = [r.reshape(small_out_shapes[name]) for r in res]

    order = ["norm_mix", "w_in", "b_gate", "attn_sinks", "w_attn_o", "conv_w", "w_conv_o", "ssm_a_re", "ssm_a_im",
             "ssm_b_re", "ssm_b_im", "ssm_c_re", "ssm_c_im", "ssm_d", "ssm_log_dt", "w_ssm_glu", "w_ssm_o",
             "w_mix_o", "norm_ffn", "w_ffn_in", "w_ffn_out", "norm_final"]
    outs = [loss, dx[None]]
    for kind in range(4):
        for name in order:
            outs.append(big_res[kind][name] if name in big_res[kind] else small_res[name][kind])
    return tuple(outs)
```

```python
import math

import jax
import jax.numpy as jnp
from jax import lax
from jax.experimental import pallas as pl
from jax.experimental.pallas import tpu as pltpu

F32 = jnp.float32
BF16 = jnp.bfloat16

N_DEV = 8
DEPTH = 4
SEQ = 2048
D_MODEL = 1024
N_Q_HEADS = 8
HEAD_DIM = 64
ATTN_W = 512
KV_W = 128
BLOCK = 128
N_BLOCKS = SEQ // BLOCK
ROPE_THETA = 500000.0
ROT_DIM = 16
NEG_INF = -1e30
WIDTH = 512
SSM_GROUPS = 32
SSM_GROUP = 16
SSM_STATE = 64
CHUNK = 256
N_CHUNKS = SEQ // CHUNK
GATE_W = 3 * D_MODEL
IN_COLS = 5888
FFN_H = 2816
NORM_EPS = 1e-6
LR, B1, B2, ADAM_EPS, WD, STEP = 0.001, 0.9, 0.999, 1e-08, 0.01, 10

COL_Q, COL_KV, COL_CBX, COL_U, COL_G = 0, 512, 768, 2304, 2816
PIECE_W = (512, 256, 512, 512, 512, 512, 3072)
PIECE_OFF = tuple(sum(PIECE_W[:i]) for i in range(len(PIECE_W)))

KINDS = (("win_t", 736, 1024), ("wffn_t", 704, 1024), ("wout", 352, 1024), ("wmix", 128, 1024),
         ("branch_t", 128, 1536), ("wglu", 64, 512))

REPLICATED = ("norm_mix", "b_gate", "attn_sinks", "ssm_a_re", "ssm_a_im", "ssm_b_re", "ssm_b_im", "ssm_c_re",
              "ssm_c_im", "ssm_d", "ssm_log_dt", "norm_ffn")

VMEM_LIMIT = 56 * 1024 * 1024
NT = (((1,), (1,)), ((), ()))
TN = (((0,), (0,)), ((), ()))
MESH_ID = pl.DeviceIdType.MESH
ANY = pl.BlockSpec(memory_space=pl.ANY)
HBM = pl.BlockSpec(memory_space=pltpu.HBM)
SEM = pl.BlockSpec(memory_space=pltpu.SEMAPHORE)
EFFECT = pltpu.SideEffectType.DATAFLOW_SIDE_EFFECTING


def _cp(**kw):
    return pltpu.CompilerParams(vmem_limit_bytes=VMEM_LIMIT, **kw)


def _full(shape):
    return pl.BlockSpec(shape, lambda *_: (0,) * len(shape))


def _resident(shape):
    return pl.BlockSpec(shape, lambda *_: (0,) * len(shape), pipeline_mode=pl.Buffered(1))


def _mm_tn(a, b, *, tm, tn, name):
    k, m = a.shape
    n = b.shape[1]

    def body(a_ref, b_ref, o_ref):
        o_ref[...] = lax.dot_general(a_ref[...].astype(BF16), b_ref[...].astype(BF16), TN,
                                     preferred_element_type=F32).astype(BF16)

    return pl.pallas_call(
        body, grid=(m // tm, n // tn),
        in_specs=[pl.BlockSpec((k, tm), lambda i, j: (0, i)), pl.BlockSpec((k, tn), lambda i, j: (0, j))],
        out_specs=pl.BlockSpec((tm, tn), lambda i, j: (i, j)),
        out_shape=jax.ShapeDtypeStruct((m, n), BF16), compiler_params=_cp(), name=name)(a, b)


def _rms_rows(xv, g):
    r = lax.rsqrt(jnp.mean(xv * xv, axis=-1, keepdims=True) + NORM_EPS)
    return ((xv * r) * g).astype(BF16)


def _rms_mm_in(x, g, wt, tabs, cw, layer, tie):
    tt = 512
    widths = (3 * WIDTH, WIDTH, GATE_W)
    offs = (COL_CBX, COL_U, COL_G)

    def body(x_ref, g_ref, w_ref, tc_ref, ta_ref, tb_ref, cw_ref, tie_ref,
             q_ref, kv_ref, cbx_ref, u_ref, gl_ref, cv_ref, h_ref, tail_ref):
        @pl.when(pl.program_id(0) == 0)
        def _():
            tail_ref[...] = jnp.zeros_like(tail_ref)

        h = _rms_rows(x_ref[...], g_ref[...])
        h_ref[...] = h
        prod = lax.dot_general(h, w_ref[...], NT, preferred_element_type=F32)
        for ref, o, w in zip((cbx_ref, u_ref, gl_ref), offs, widths):
            ref[...] = prod[:, o:o + w]
        c, a, b = tc_ref[...], ta_ref[...], tb_ref[...]
        for j in range(ATTN_W // 128):
            q_ref[:, 128 * j:128 * (j + 1)] = _rope(prod[:, 128 * j:128 * (j + 1)], c, a, b) * (HEAD_DIM ** -0.5)
        kv_ref[:, :KV_W] = _rope(prod[:, COL_KV:COL_KV + KV_W], c, a, b)
        kv_ref[:, KV_W:] = prod[:, COL_KV + KV_W:COL_CBX]

        row = lax.broadcasted_iota(jnp.int32, (tt, 128), 0)
        for j in range(WIDTH // 128):
            cols = slice(128 * j, 128 * (j + 1))
            cb = prod[:, COL_CBX + 128 * j:COL_CBX + 128 * (j + 1)]
            z = prod[:, COL_CBX + WIDTH + 128 * j:COL_CBX + WIDTH + 128 * (j + 1)] \
                * prod[:, COL_CBX + 2 * WIDTH + 128 * j:COL_CBX + 2 * WIDTH + 128 * (j + 1)]
            before1, before2 = tail_ref[7:8, cols], tail_ref[6:7, cols]
            z1 = jnp.where(row == 0, before1, pltpu.roll(z, 1, axis=0))
            z2 = jnp.where(row == 0, before2, jnp.where(row == 1, before1, pltpu.roll(z, 2, axis=0)))
            s = cw_ref[0:1, cols] * z2 + cw_ref[1:2, cols] * z1 + cw_ref[2:3, cols] * z
            cv_ref[:, cols] = (cb * s).astype(BF16)
            tail_ref[:, cols] = z[tt - 8:, :]

    row_spec = lambda w: pl.BlockSpec((tt, w), lambda i: (i, 0))
    sds = jax.ShapeDtypeStruct
    return pl.pallas_call(
        body, grid=(SEQ // tt,),
        in_specs=[row_spec(D_MODEL), _full((1, D_MODEL)), _resident((IN_COLS, D_MODEL)),
                  row_spec(128), row_spec(128), row_spec(128),
                  pl.BlockSpec((None, 8, WIDTH), lambda i: (layer, 0, 0)), ANY],
        out_specs=[row_spec(ATTN_W), row_spec(2 * KV_W), row_spec(3 * WIDTH), row_spec(WIDTH), row_spec(GATE_W),
                   row_spec(WIDTH), row_spec(D_MODEL)],
        out_shape=[sds((SEQ, ATTN_W), F32), sds((SEQ, 2 * KV_W), F32), sds((SEQ, 3 * WIDTH), F32),
                   sds((SEQ, WIDTH), F32), sds((SEQ, GATE_W), F32), sds((SEQ, WIDTH), BF16),
                   sds((SEQ, D_MODEL), BF16)],
        scratch_shapes=[pltpu.VMEM((8, WIDTH), F32)], compiler_params=_cp(), name="rms_mm_in",
    )(x, g, wt, *tabs, cw, tie)


def _rms_mm_ffn(x, g, wt):
    tt = 256

    def body(x_ref, g_ref, w_ref, act_ref, up_ref, silu_ref, dsilu_ref, h_ref):
        h = _rms_rows(x_ref[...], g_ref[...])
        h_ref[...] = h
        prod = lax.dot_general(h, w_ref[...], NT, preferred_element_type=F32)
        gt, up = prod[:, :FFN_H], prod[:, FFN_H:]
        sg = jax.nn.sigmoid(gt)
        silu = gt * sg
        act_ref[...] = (silu * up).astype(BF16)
        up_ref[...] = up.astype(BF16)
        silu_ref[...] = silu.astype(BF16)
        dsilu_ref[...] = (sg + silu * (1.0 - sg)).astype(BF16)

    row = lambda w: pl.BlockSpec((tt, w), lambda i: (i, 0))
    return pl.pallas_call(
        body, grid=(SEQ // tt,), in_specs=[row(D_MODEL), _full((1, D_MODEL)), _resident((2 * FFN_H, D_MODEL))],
        out_specs=[row(FFN_H)] * 4 + [row(D_MODEL)],
        out_shape=[jax.ShapeDtypeStruct((SEQ, FFN_H), BF16)] * 4 + [jax.ShapeDtypeStruct((SEQ, D_MODEL), BF16)],
        compiler_params=_cp(), name="rms_mm_ffn")(x, g, wt)


def _mm_rmsbwd(pieces, wt, x, g, dres, name):
    tt = 512
    widths = [p.shape[1] for p in pieces]
    offs = [sum(widths[:i]) for i in range(len(widths))]
    n = len(pieces)

    def body(*refs):
        p_refs, (w_ref, x_ref, g_ref, r_ref, dx_ref, dg_ref) = refs[:n], refs[n:]

        @pl.when(pl.program_id(0) == 0)
        def _():
            dg_ref[...] = jnp.zeros_like(dg_ref)

        dh = jnp.zeros((tt, D_MODEL), F32)
        for p_ref, o, w in zip(p_refs, offs, widths):
            dh += jnp.dot(p_ref[...], w_ref[o:o + w, :], preferred_element_type=F32)
        xv = x_ref[...]
        r = lax.rsqrt(jnp.mean(xv * xv, axis=-1, keepdims=True) + NORM_EPS)
        xh = xv * r
        gy = dh * g_ref[...]
        dx_ref[...] = r_ref[...] + r * (gy - xh * jnp.mean(gy * xh, axis=-1, keepdims=True))
        dg_ref[...] += jnp.sum(dh * xh, axis=0, keepdims=True)

    row = lambda w: pl.BlockSpec((tt, w), lambda i: (i, 0))
    return pl.pallas_call(
        body, grid=(SEQ // tt,),
        in_specs=[row(w) for w in widths] + [_resident(wt.shape), row(D_MODEL), _full((1, D_MODEL)), row(D_MODEL)],
        out_specs=[row(D_MODEL), _full((1, D_MODEL))],
        out_shape=[jax.ShapeDtypeStruct((SEQ, D_MODEL), F32), jax.ShapeDtypeStruct((1, D_MODEL), F32)],
        compiler_params=_cp(), name=name)(*pieces, wt, x, g, dres)


def _tn_pieces(pieces, h):
    tk, tn = 512, 512
    nk = SEQ // tk
    n = len(pieces)

    def body(*refs):
        p_refs, (h_ref, o_ref, acc_ref) = refs[:n], refs[n:]
        kk = pl.program_id(1)

        @pl.when(kk == 0)
        def _():
            acc_ref[...] = jnp.zeros_like(acc_ref)

        hv = h_ref[...]
        for p_ref, o, w in zip(p_refs, PIECE_OFF, PIECE_W):
            acc_ref[o:o + w, :] += lax.dot_general(p_ref[...], hv, TN, preferred_element_type=F32)

        @pl.when(kk == nk - 1)
        def _():
            o_ref[...] = acc_ref[...].astype(BF16)

    return pl.pallas_call(
        body, grid=(D_MODEL // tn, nk),
        in_specs=[pl.BlockSpec((tk, w), lambda j, kk: (kk, 0)) for w in PIECE_W]
        + [pl.BlockSpec((tk, tn), lambda j, kk: (kk, j))],
        out_specs=pl.BlockSpec((IN_COLS, tn), lambda j, kk: (0, j)),
        out_shape=jax.ShapeDtypeStruct((IN_COLS, D_MODEL), BF16),
        scratch_shapes=[pltpu.VMEM((IN_COLS, tn), F32)], compiler_params=_cp(), name="tn_pieces")(*pieces, h)


def _tn_branches(dys, acts):
    tk = 512
    nk = SEQ // tk

    def body(d0, d1, d2, a0, a1, a2, o_ref, acc_ref):
        kk = pl.program_id(0)

        @pl.when(kk == 0)
        def _():
            acc_ref[...] = jnp.zeros_like(acc_ref)

        for j, (d, a) in enumerate(((d0, a0), (d1, a1), (d2, a2))):
            acc_ref[:, WIDTH * j:WIDTH * (j + 1)] += lax.dot_general(d[...], a[...], TN, preferred_element_type=F32)

        @pl.when(kk == nk - 1)
        def _():
            o_ref[...] = acc_ref[...].astype(BF16)

    row = lambda w: pl.BlockSpec((tk, w), lambda kk: (kk, 0))
    return pl.pallas_call(
        body, grid=(nk,), in_specs=[row(D_MODEL)] * 3 + [row(WIDTH)] * 3,
        out_specs=_full((D_MODEL, 3 * WIDTH)), out_shape=jax.ShapeDtypeStruct((D_MODEL, 3 * WIDTH), BF16),
        scratch_shapes=[pltpu.VMEM((D_MODEL, 3 * WIDTH), F32)], compiler_params=_cp(), name="tn_branches",
    )(*dys, *acts)


def _rope(t, c, a, b):
    return t * c + pltpu.roll(t, 120, axis=1) * a + pltpu.roll(t, 8, axis=1) * b


def _rope_t(d, c, a, b):
    return d * c + pltpu.roll(d * a, 8, axis=1) + pltpu.roll(d * b, 120, axis=1)


def _band_sides(band):
    left = lax.broadcasted_iota(jnp.int32, band.shape, 1) < HEAD_DIM
    h0 = jnp.where(left, band, 0.0)
    h1 = jnp.where(left, 0.0, band)
    r0 = pltpu.roll(h0, HEAD_DIM, axis=1)
    r1 = pltpu.roll(h1, HEAD_DIM, axis=1)
    return ((h0.astype(BF16), r0.astype(BF16)), (r1.astype(BF16), h1.astype(BF16)))


def _attn_mask(i):
    qi = lax.broadcasted_iota(jnp.int32, (2 * BLOCK, 2 * BLOCK), 0) % BLOCK
    kj = lax.broadcasted_iota(jnp.int32, (2 * BLOCK, 2 * BLOCK), 1)
    delta = qi + BLOCK - kj
    return (delta >= 0) & (delta < BLOCK) & ((kj >= BLOCK) | (i > 0))


def _attn_probs(s, ok, sink):
    s = jnp.where(ok, s, NEG_INF)
    m = jnp.maximum(jnp.max(s, axis=-1, keepdims=True), sink)
    p = jnp.exp(s - m)
    es = jnp.exp(sink - m)
    inv = 1.0 / (jnp.sum(p, axis=-1, keepdims=True) + es)
    return p * inv, es * inv


def _kv_group(qs, ks, vs, kh, sink_ref):
    q2 = jnp.concatenate([qs[2 * kh], qs[2 * kh + 1]], axis=0)
    kst = jnp.concatenate([ks[kh][0], ks[kh][1]], axis=0)
    vst = jnp.concatenate([vs[kh][0], vs[kh][1]], axis=0)
    top = lax.broadcasted_iota(jnp.int32, (2 * BLOCK, 1), 0) < BLOCK
    sinks = [jnp.where(top, sink_ref[0, 4 * kh + h], sink_ref[0, 4 * kh + 2 + h]) for h in range(2)]
    return q2, kst, vst, sinks


def _attn_load(q_ref, kvc_ref, kvp_ref, tc_ref, ta_ref, tb_ref, pc_ref, pa_ref, pb_ref):
    c, a, b = tc_ref[...], ta_ref[...], tb_ref[...]
    kband = jnp.concatenate([kvp_ref[:, :KV_W], kvc_ref[:, :KV_W]], axis=0)
    vband = jnp.concatenate([kvp_ref[:, KV_W:], kvc_ref[:, KV_W:]], axis=0)
    qs = [q_ref[:, 128 * j:128 * (j + 1)].astype(BF16) for j in range(4)]
    return qs, _band_sides(kband), _band_sides(vband), (c, a, b)


def _attn_specs(clamp):
    cur = lambda i: (clamp(i), 0)
    prev = lambda i: (jnp.maximum(clamp(i) - 1, 0), 0)
    return [
        pl.BlockSpec((BLOCK, ATTN_W), cur), pl.BlockSpec((BLOCK, 2 * KV_W), cur),
        pl.BlockSpec((BLOCK, 2 * KV_W), prev),
        pl.BlockSpec((BLOCK, 128), cur), pl.BlockSpec((BLOCK, 128), cur), pl.BlockSpec((BLOCK, 128), cur),
        pl.BlockSpec((BLOCK, 128), prev), pl.BlockSpec((BLOCK, 128), prev), pl.BlockSpec((BLOCK, 128), prev),
        pl.BlockSpec(memory_space=pltpu.SMEM),
    ]


def _attn_fwd(q, kv, tabs, sinks):
    tc, ta, tb = tabs

    def body(q_ref, kvc_ref, kvp_ref, tc_ref, ta_ref, tb_ref, pc_ref, pa_ref, pb_ref, sink_ref, o_ref):
        i = pl.program_id(0)
        qs, ks, vs, _ = _attn_load(q_ref, kvc_ref, kvp_ref, tc_ref, ta_ref, tb_ref, pc_ref, pa_ref, pb_ref)
        ok = _attn_mask(i)
        for kh in range(2):
            q2, kst, vst, sinks = _kv_group(qs, ks, vs, kh, sink_ref)
            s = lax.dot_general(q2, kst, NT, preferred_element_type=F32)
            pn = [_attn_probs(s[:, 2 * BLOCK * h:2 * BLOCK * (h + 1)], ok, sinks[h])[0].astype(BF16) for h in range(2)]
            o2 = jnp.dot(jnp.concatenate(pn, axis=1), vst, preferred_element_type=F32).astype(BF16)
            for r in range(2):
                j = 2 * kh + r
                o_ref[:, 128 * j:128 * (j + 1)] = o2[BLOCK * r:BLOCK * (r + 1)]

    return pl.pallas_call(
        body, grid=(N_BLOCKS,), in_specs=_attn_specs(lambda i: i),
        out_specs=pl.BlockSpec((BLOCK, ATTN_W), lambda i: (i, 0)),
        out_shape=jax.ShapeDtypeStruct((SEQ, ATTN_W), BF16), compiler_params=_cp(), name="attn_fwd",
    )(q, kv, kv, tc, ta, tb, tc, ta, tb, sinks)


def _attn_bwd(q, kv, tabs, sinks, do):
    tc, ta, tb = tabs
    last = N_BLOCKS - 1
    clamp = lambda i: jnp.minimum(i, last)

    def place(full, side, kh):
        left = lax.broadcasted_iota(jnp.int32, full.shape, 1) < HEAD_DIM
        valid = jnp.where(left, full, 0.0) if side == 0 else jnp.where(left, 0.0, full)
        return valid if side == kh else pltpu.roll(valid, HEAD_DIM, axis=1)

    def body(q_ref, kvc_ref, kvp_ref, tc_ref, ta_ref, tb_ref, pc_ref, pa_ref, pb_ref, sink_ref, do_ref,
             dq_ref, dkv_ref, ds_ref, carry_ref):
        i = pl.program_id(0)

        @pl.when(i == 0)
        def _():
            ds_ref[...] = jnp.zeros_like(ds_ref)
            carry_ref[...] = jnp.zeros_like(carry_ref)

        @pl.when(i > last)
        def _():
            dkv_ref[...] = carry_ref[...].astype(BF16)

        @pl.when(i <= last)
        def _():
            qs, ks, vs, (c, a, b) = _attn_load(q_ref, kvc_ref, kvp_ref, tc_ref, ta_ref, tb_ref,
                                               pc_ref, pa_ref, pb_ref)
            ok = _attn_mask(i)
            dk = jnp.zeros((2 * BLOCK, 128), F32)
            dv = jnp.zeros((2 * BLOCK, 128), F32)
            dsink = jnp.zeros((1, 128), F32)
            lane = lax.broadcasted_iota(jnp.int32, (1, 128), 1)
            for kh in range(2):
                q2, kst, vst, sinks = _kv_group(qs, ks, vs, kh, sink_ref)
                do2 = jnp.concatenate([do_ref[:, 128 * (2 * kh + r):128 * (2 * kh + r + 1)] for r in range(2)],
                                      axis=0).astype(BF16)
                s = lax.dot_general(q2, kst, NT, preferred_element_type=F32)
                dp = lax.dot_general(do2, vst, NT, preferred_element_type=F32)
                pns, dss = [], []
                for h in range(2):
                    cols = slice(2 * BLOCK * h, 2 * BLOCK * (h + 1))
                    pn, ps = _attn_probs(s[:, cols], ok, sinks[h])
                    dr = jnp.sum(pn * dp[:, cols], axis=-1, keepdims=True)
                    pns.append(pn.astype(BF16))
                    dss.append((pn * (dp[:, cols] - dr)).astype(BF16))
                    for r in range(2):
                        part = -jnp.sum((ps * dr)[BLOCK * r:BLOCK * (r + 1)])
                        dsink += jnp.where(lane == 4 * kh + 2 * r + h, part, 0.0)
                ds2, pn2 = jnp.concatenate(dss, axis=1), jnp.concatenate(pns, axis=1)
                dq2 = jnp.dot(ds2, kst, preferred_element_type=F32) * (HEAD_DIM ** -0.5)
                dk2 = lax.dot_general(ds2, q2, TN, preferred_element_type=F32)
                dv2 = lax.dot_general(pn2, do2, TN, preferred_element_type=F32)
                for h in range(2):
                    dk += place(dk2[2 * BLOCK * h:2 * BLOCK * (h + 1)], h, kh)
                    dv += place(dv2[2 * BLOCK * h:2 * BLOCK * (h + 1)], h, kh)
                for r in range(2):
                    j = 2 * kh + r
                    dq_ref[:, 128 * j:128 * (j + 1)] = _rope_t(dq2[BLOCK * r:BLOCK * (r + 1)], c, a, b).astype(BF16)
            ds_ref[...] += dsink
            dk_prev = _rope_t(dk[:BLOCK], pc_ref[...], pa_ref[...], pb_ref[...])
            dk_cur = _rope_t(dk[BLOCK:], c, a, b)
            prev = jnp.concatenate([dk_prev, dv[:BLOCK]], axis=1)
            dkv_ref[...] = (carry_ref[...] + prev).astype(BF16)
            carry_ref[...] = jnp.concatenate([dk_cur, dv[BLOCK:]], axis=1)

    return pl.pallas_call(
        body, grid=(N_BLOCKS + 1,),
        in_specs=_attn_specs(clamp) + [pl.BlockSpec((BLOCK, ATTN_W), lambda i: (clamp(i), 0))],
        out_specs=[pl.BlockSpec((BLOCK, ATTN_W), lambda i: (clamp(i), 0)),
                   pl.BlockSpec((BLOCK, 2 * KV_W), lambda i: (jnp.maximum(i - 1, 0), 0)),
                   pl.BlockSpec((1, 128), lambda i: (0, 0))],
        out_shape=[jax.ShapeDtypeStruct((SEQ, ATTN_W), BF16), jax.ShapeDtypeStruct((SEQ, 2 * KV_W), BF16),
                   jax.ShapeDtypeStruct((1, 128), F32)],
        scratch_shapes=[pltpu.VMEM((BLOCK, 2 * KV_W), F32)], compiler_params=_cp(), name="attn_bwd",
    )(q, kv, kv, tc, ta, tb, tc, ta, tb, sinks, do)


def _shift_down(z, k):
    row = lax.broadcasted_iota(jnp.int32, z.shape, 0)
    return jnp.where(row < k, 0.0, pltpu.roll(z, k, axis=0))


def _shift_up(z, k):
    n = z.shape[0]
    row = lax.broadcasted_iota(jnp.int32, z.shape, 0)
    return jnp.where(row >= n - k, 0.0, pltpu.roll(z, n - k, axis=0))


def _conv_specs():
    nb = WIDTH // 128
    return [pl.BlockSpec((SEQ, 128), lambda j: (0, j)), pl.BlockSpec((SEQ, 128), lambda j: (0, nb + j)),
            pl.BlockSpec((SEQ, 128), lambda j: (0, 2 * nb + j)), pl.BlockSpec((None, 8, 128), lambda j: (0, 0, j))]


def _conv_bwd(cbx, cw, layer, dout, tie):
    def body(cb_ref, cc_ref, cx_ref, w_ref, do_ref, tie_ref, dcb_ref, dcc_ref, dcx_ref, dw_ref):
        cc, cx = cc_ref[...], cx_ref[...]
        z = cc * cx
        z1, z2 = _shift_down(z, 1), _shift_down(z, 2)
        w0, w1, w2 = w_ref[0:1, :], w_ref[1:2, :], w_ref[2:3, :]
        dout = do_ref[...]
        ds = dout * cb_ref[...]
        dcb_ref[...] = (dout * (w0 * z2 + w1 * z1 + w2 * z)).astype(BF16)
        dz = w2 * ds + w1 * _shift_up(ds, 1) + w0 * _shift_up(ds, 2)
        dcc_ref[...] = (dz * cx).astype(BF16)
        dcx_ref[...] = (dz * cc).astype(BF16)
        rows = [jnp.sum(ds * zz, axis=0, keepdims=True) for zz in (z2, z1, z)]
        dw_ref[...] = jnp.concatenate(rows + [jnp.zeros((5, 128), F32)], axis=0)

    col = lambda j: (0, j)
    specs = _conv_specs()
    specs[3] = pl.BlockSpec((None, 8, 128), lambda j: (layer, 0, j))
    return pl.pallas_call(
        body, grid=(WIDTH // 128,), in_specs=specs + [pl.BlockSpec((SEQ, 128), col), ANY],
        out_specs=[pl.BlockSpec((SEQ, 128), col), pl.BlockSpec((SEQ, 128), col), pl.BlockSpec((SEQ, 128), col),
                   pl.BlockSpec((8, 128), col)],
        out_shape=[jax.ShapeDtypeStruct((SEQ, WIDTH), BF16)] * 3 + [jax.ShapeDtypeStruct((8, WIDTH), F32)],
        compiler_params=_cp(), name="conv_bwd",
    )(cbx, cbx, cbx, cw, dout, tie)


def _ssm_prep_math(a_re, a_im, log_dt, bt_re, bt_im):
    dt = jnp.exp(log_dt)
    er = jnp.exp(a_re * dt)
    lr = er * jnp.cos(a_im * dt)
    li = er * jnp.sin(a_im * dt)
    n2 = a_re * a_re + a_im * a_im
    cr = ((lr - 1.0) * a_re + li * a_im) / n2
    ci = (li * a_re - (lr - 1.0) * a_im) / n2
    cr3, ci3 = cr[:, None, :], ci[:, None, :]
    return lr, li, cr3 * bt_re - ci3 * bt_im, cr3 * bt_im + ci3 * bt_re


_GS = (SSM_GROUPS, SSM_STATE)
_GHS = (SSM_GROUPS, SSM_GROUP, SSM_STATE)


def _layered(shape):
    return pl.BlockSpec((None,) + shape, lambda l: (l,) + (0,) * len(shape))


def _ssm_prep(a_re, a_im, log_dt, bt_re, bt_im):
    def body(ar, ai, ld, br, bi, o0, o1, o2, o3):
        outs = _ssm_prep_math(ar[...], ai[...], ld[...], br[...], bi[...])
        for o, v in zip((o0, o1, o2, o3), outs):
            o[...] = v

    shapes = [_GS, _GS, _GHS, _GHS]
    return pl.pallas_call(
        body, grid=(DEPTH,), in_specs=[_layered(s) for s in (_GS, _GS, (SSM_GROUPS, 1), _GHS, _GHS)],
        out_specs=[_layered(s) for s in shapes],
        out_shape=[jax.ShapeDtypeStruct((DEPTH,) + s, F32) for s in shapes],
        name="ssm_prep")(a_re, a_im, log_dt, bt_re, bt_im)


def _ssm_prep_bwd(a_re, a_im, log_dt, bt_re, bt_im, cots):
    def body(ar, ai, ld, br, bi, c0, c1, c2, c3, o0, o1, o2, o3, o4):
        _, vjp = jax.vjp(_ssm_prep_math, ar[...], ai[...], ld[...], br[...], bi[...])
        for o, v in zip((o0, o1, o2, o3, o4), vjp((c0[...], c1[...], c2[...], c3[...]))):
            o[...] = v

    ins = (_GS, _GS, (SSM_GROUPS, 1), _GHS, _GHS)
    return pl.pallas_call(
        body, grid=(DEPTH,), in_specs=[_layered(s) for s in ins + (_GS, _GS, _GHS, _GHS)],
        out_specs=[_layered(s) for s in ins],
        out_shape=[jax.ShapeDtypeStruct((DEPTH,) + s, F32) for s in ins],
        name="ssm_prep_bwd")(a_re, a_im, log_dt, bt_re, bt_im, *cots)


LANES_G = 512
N_LANE_GROUPS = SSM_GROUPS * SSM_STATE // LANES_G


def _ssm_embed(b_re, b_im, c_re, c_im):
    rows = SSM_GROUPS * SSM_GROUP

    def body(br, bi, cr, ci, o0, o1, o2, o3):
        state = lax.broadcasted_iota(jnp.int32, (SSM_STATE, LANES_G), 0)
        lane = lax.broadcasted_iota(jnp.int32, (SSM_STATE, LANES_G), 1)
        spread = (lane % SSM_STATE == state).astype(BF16)
        r = lax.broadcasted_iota(jnp.int32, (rows, LANES_G), 0)
        c = lax.broadcasted_iota(jnp.int32, (rows, LANES_G), 1)
        own = (r % 128) // SSM_GROUP == c // SSM_STATE
        for ref, o, sign in ((br, o0, 1.0), (bi, o1, 1.0), (cr, o2, 1.0), (ci, o3, -1.0)):
            t = (sign * ref[...]).reshape(rows, SSM_STATE).astype(BF16)
            wide = jnp.dot(t, spread, preferred_element_type=F32)
            o[...] = jnp.where(own, wide, 0.0).astype(BF16).reshape(N_LANE_GROUPS, 128, LANES_G)

    out = (N_LANE_GROUPS, 128, LANES_G)
    return pl.pallas_call(
        body, grid=(DEPTH,), in_specs=[_layered(_GHS)] * 4, out_specs=[_layered(out)] * 4,
        out_shape=[jax.ShapeDtypeStruct((DEPTH,) + out, BF16)] * 4, name="ssm_embed")(b_re, b_im, c_re, c_im)


def _scan_in_place(xr_ref, xi_ref, ar, ai, reverse):
    shape = (N_CHUNKS, xr_ref.shape[1])
    ar, ai = jnp.broadcast_to(ar, shape), jnp.broadcast_to(ai, shape)

    def rows(tau):
        t = (CHUNK - 1 - tau) if reverse else tau
        return pl.ds(pl.multiple_of(t * N_CHUNKS, N_CHUNKS), N_CHUNKS)

    def step(tau, carry):
        sr, si = carry
        return ar * sr - ai * si + xr_ref[rows(tau), :], ar * si + ai * sr + xi_ref[rows(tau), :]

    zero = jnp.zeros(shape, F32)
    er, ei = lax.fori_loop(0, CHUNK, step, (zero, zero), unroll=8)
    qr, qi = ar, ai
    for _ in range(8):
        qr, qi = qr * qr - qi * qi, 2.0 * qr * qi
    shift = _shift_up if reverse else _shift_down
    for k in (1, 2, 4):
        sr, si = shift(er, k), shift(ei, k)
        er, ei = er + qr * sr - qi * si, ei + qr * si + qi * sr
        qr, qi = qr * qr - qi * qi, 2.0 * qr * qi
    start = (shift(er, 1), shift(ei, 1))

    def write(tau, carry):
        sr, si = step(tau, carry)
        xr_ref[rows(tau), :] = sr
        xi_ref[rows(tau), :] = si
        return sr, si

    return write, start


def _ssm_specs(layer):
    col = lambda w: pl.BlockSpec((SEQ, w), lambda g: (0, g))
    diag = pl.BlockSpec((None, None, 128, LANES_G), lambda g: (layer, g, 0, 0))
    vec = pl.BlockSpec((None, 1, LANES_G), lambda g: (layer, 0, g))
    return col, diag, vec


def _to_scan_order(src_ref, dst_ref):
    for tau in range(CHUNK):
        dst_ref[pl.ds(tau * N_CHUNKS, N_CHUNKS), :] = src_ref[pl.ds(tau, N_CHUNKS, stride=CHUNK), :]


def _to_time_order(src_ref, dst_ref, dtype):
    for j in range(N_CHUNKS):
        dst_ref[pl.ds(j * CHUNK, CHUNK), :] = src_ref[pl.ds(j, CHUNK, stride=N_CHUNKS), :].astype(dtype)


def _ssm_fwd(u, mats, layer, d):
    def body(u_ref, d_ref, br_ref, bi_ref, cr_ref, ci_ref, ar_ref, ai_ref, xr_ref, xi_ref, y_ref, us_ref):
        _to_scan_order(u_ref, us_ref)
        uv = us_ref[...].astype(BF16)
        xr_ref[...] = jnp.dot(uv, br_ref[...], preferred_element_type=F32)
        xi_ref[...] = jnp.dot(uv, bi_ref[...], preferred_element_type=F32)
        write, start = _scan_in_place(xr_ref, xi_ref, ar_ref[...], ai_ref[...], False)
        lax.fori_loop(0, CHUNK, write, start, unroll=8)
        y = lax.dot_general(xr_ref[...].astype(BF16), cr_ref[...], NT, preferred_element_type=F32)
        y += lax.dot_general(xi_ref[...].astype(BF16), ci_ref[...], NT, preferred_element_type=F32)
        us_ref[...] = y + d_ref[...] * us_ref[...]
        _to_time_order(us_ref, y_ref, F32)

    col, diag, vec = _ssm_specs(layer)
    return pl.pallas_call(
        body, grid=(N_LANE_GROUPS,),
        in_specs=[col(128), pl.BlockSpec((None, 1, 128), lambda g: (layer, 0, g)),
                  diag, diag, diag, diag, vec, vec],
        out_specs=[col(LANES_G), col(LANES_G), col(128)],
        out_shape=[jax.ShapeDtypeStruct((SEQ, SSM_GROUPS * SSM_STATE), F32)] * 2
        + [jax.ShapeDtypeStruct((SEQ, WIDTH), F32)],
        scratch_shapes=[pltpu.VMEM((SEQ, 128), F32)], compiler_params=_cp(), name="ssm_fwd",
    )(u, d, mats["b_re"], mats["b_im"], mats["c_re"], mats["c_im_neg"], mats["a_re"], mats["a_im"])


def _ssm_bwd(dy, x_re, x_im, u, mats, layer, d):
    def body(dyt_ref, ut_ref, d_ref, xr_ref, xi_ref, br_ref, bi_ref, cr_ref, ci_ref, ar_ref, ai_ref,
             du_ref, dar_ref, dai_ref, dbr_ref, dbi_ref, dcr_ref, dci_ref, lr_ref, li_ref, dys_ref, u_ref):
        _to_scan_order(dyt_ref, dys_ref)
        _to_scan_order(ut_ref, u_ref)
        dy = dys_ref[...].astype(BF16)
        lr_ref[...] = jnp.dot(dy, cr_ref[...], preferred_element_type=F32)
        li_ref[...] = jnp.dot(dy, ci_ref[...], preferred_element_type=F32)
        write, start = _scan_in_place(lr_ref, li_ref, ar_ref[...], -ai_ref[...], True)

        def rows(t):
            return pl.ds(pl.multiple_of(t * N_CHUNKS, N_CHUNKS), N_CHUNKS)

        def grad(acc, lam, xpr, xpi):
            return acc[0] + xpr * lam[0] + xpi * lam[1], acc[1] + xpr * lam[1] - xpi * lam[0]

        def down(tau, carry):
            lam = write(tau, carry[0])
            t = CHUNK - 2 - tau
            return lam, grad(carry[1], lam, xr_ref[rows(t), :], xi_ref[rows(t), :])

        zero = jnp.zeros((N_CHUNKS, LANES_G), F32)
        lam, acc = lax.fori_loop(0, CHUNK - 1, down, (start, (zero, zero)), unroll=5)
        lam = write(CHUNK - 1, lam)
        last = rows(CHUNK - 1)
        acc = grad(acc, lam, _shift_down(xr_ref[last, :], 1), _shift_down(xi_ref[last, :], 1))
        dar_ref[...] = jnp.sum(acc[0], axis=0, keepdims=True)
        dai_ref[...] = jnp.sum(acc[1], axis=0, keepdims=True)

        l_re, l_im = lr_ref[...].astype(BF16), li_ref[...].astype(BF16)
        du = lax.dot_general(l_re, br_ref[...], NT, preferred_element_type=F32)
        du += lax.dot_general(l_im, bi_ref[...], NT, preferred_element_type=F32)
        dys_ref[...] = du + dys_ref[...] * d_ref[...]
        _to_time_order(dys_ref, du_ref, BF16)
        uv = u_ref[...].astype(BF16)
        dbr_ref[...] = lax.dot_general(uv, l_re, TN, preferred_element_type=F32)
        dbi_ref[...] = lax.dot_general(uv, l_im, TN, preferred_element_type=F32)
        dcr_ref[...] = lax.dot_general(dy, xr_ref[...].astype(BF16), TN, preferred_element_type=F32)
        dci_ref[...] = lax.dot_general(dy, xi_ref[...].astype(BF16), TN, preferred_element_type=F32)

    col, diag, vec = _ssm_specs(layer)
    out_vec = pl.BlockSpec((1, LANES_G), lambda g: (0, g))
    out_blk = pl.BlockSpec((None, 128, LANES_G), lambda g: (g, 0, 0))
    sds = jax.ShapeDtypeStruct
    return pl.pallas_call(
        body, grid=(N_LANE_GROUPS,),
        in_specs=[col(128), col(128), pl.BlockSpec((None, 1, 128), lambda g: (layer, 0, g)),
                  col(LANES_G), col(LANES_G), diag, diag, diag, diag, vec, vec],
        out_specs=[col(128), out_vec, out_vec, out_blk, out_blk, out_blk, out_blk],
        out_shape=[sds((SEQ, WIDTH), BF16)] + [sds((1, SSM_GROUPS * SSM_STATE), F32)] * 2
        + [sds((N_LANE_GROUPS, 128, LANES_G), F32)] * 4,
        scratch_shapes=[pltpu.VMEM((SEQ, LANES_G), F32)] * 2 + [pltpu.VMEM((SEQ, 128), F32)] * 2,
        compiler_params=_cp(), name="ssm_bwd",
    )(dy, u, d, x_re, x_im, mats["b_re"], mats["b_im"], mats["c_re"], mats["c_im_neg"],
      mats["a_re"], mats["a_im"])


_GELU_C = math.sqrt(2.0 / math.pi)


def _gelu(y):
    return 0.5 * y * (1.0 + jnp.tanh(_GELU_C * (y + 0.044715 * (y * y * y))))


def _glu_fwd(y, wglu):
    tt = 512

    def body(y_ref, w_ref, z_ref):
        ys = _gelu(y_ref[...])
        a = jnp.dot(ys.astype(BF16), w_ref[...], preferred_element_type=F32)
        z_ref[...] = (ys * jax.nn.sigmoid(a)).astype(BF16)

    blk = pl.BlockSpec((tt, WIDTH), lambda i: (i, 0))
    return pl.pallas_call(body, grid=(SEQ // tt,), in_specs=[blk, _full((WIDTH, WIDTH))], out_specs=blk,
                          out_shape=jax.ShapeDtypeStruct((SEQ, WIDTH), BF16), compiler_params=_cp(),
                          name="glu_fwd")(y, wglu)


def _glu_bwd(y, wglu, dz, u):
    tt = 512

    def body(y_ref, w_ref, dz_ref, u_ref, dy_ref, ys_ref, da_ref, dd_ref):
        @pl.when(pl.program_id(0) == 0)
        def _():
            dd_ref[...] = jnp.zeros_like(dd_ref)

        yv = y_ref[...]
        t = jnp.tanh(_GELU_C * (yv + 0.044715 * (yv * yv * yv)))
        ys = 0.5 * yv * (1.0 + t)
        ysb = ys.astype(BF16)
        sg = jax.nn.sigmoid(jnp.dot(ysb, w_ref[...], preferred_element_type=F32))
        dz = dz_ref[...].astype(F32)
        da = (dz * ys * sg * (1.0 - sg)).astype(BF16)
        dys = dz * sg + lax.dot_general(da, w_ref[...], NT, preferred_element_type=F32)
        dy = dys * (0.5 * (1.0 + t) + 0.5 * yv * (1.0 - t * t) * _GELU_C * (1.0 + 3 * 0.044715 * (yv * yv)))
        dy_ref[...] = dy
        ys_ref[...] = ysb
        da_ref[...] = da
        dd_ref[...] += jnp.sum(dy * u_ref[...], axis=0, keepdims=True)

    blk = pl.BlockSpec((tt, WIDTH), lambda i: (i, 0))
    return pl.pallas_call(
        body, grid=(SEQ // tt,), in_specs=[blk, _full((WIDTH, WIDTH)), blk, blk],
        out_specs=[blk, blk, blk, _full((1, WIDTH))],
        out_shape=[jax.ShapeDtypeStruct((SEQ, WIDTH), F32)] + [jax.ShapeDtypeStruct((SEQ, WIDTH), BF16)] * 2
        + [jax.ShapeDtypeStruct((1, WIDTH), F32)],
        compiler_params=_cp(), name="glu_bwd")(y, wglu, dz, u)


def _mix_specs(tt, layer):
    row = lambda w: pl.BlockSpec((tt, w), lambda i: (i, 0))
    gate = lambda j: pl.BlockSpec((tt, D_MODEL), lambda i: (i, j))
    wo = lambda j: pl.BlockSpec((D_MODEL, WIDTH), lambda i: (0, j))
    return [row(D_MODEL), row(WIDTH), row(WIDTH), row(WIDTH), gate(0), gate(1), gate(2),
            pl.BlockSpec((None, 1, GATE_W), lambda i: (layer, 0, 0)), wo(0), wo(1), wo(2),
            _full((D_MODEL, D_MODEL))]


def _mix_branches(o_ref, c_ref, z_ref, g_refs, b_ref, wa_ref, wc_ref, ws_ref):
    ys = [lax.dot_general(r[...], w[...], NT, preferred_element_type=F32)
          for r, w in ((o_ref, wa_ref), (c_ref, wc_ref), (z_ref, ws_ref))]
    gates = [jax.nn.sigmoid(g_refs[j][...] + b_ref[:, D_MODEL * j:D_MODEL * (j + 1)]) for j in range(3)]
    return ys, gates


def _mix_fwd(x, o, cv, z, glog, b_gate, layer, wbt, wmix, tie):
    tt = 256

    def body(x_ref, o_ref, c_ref, z_ref, g0, g1, g2, b_ref, wa_ref, wc_ref, ws_ref, wm_ref, tie_ref, x1_ref):
        ys, gates = _mix_branches(o_ref, c_ref, z_ref, (g0, g1, g2), b_ref, wa_ref, wc_ref, ws_ref)
        merged = gates[0] * ys[0] + gates[1] * ys[1] + gates[2] * ys[2]
        x1_ref[...] = x_ref[...] + jnp.dot(merged.astype(BF16), wm_ref[...], preferred_element_type=F32)

    return pl.pallas_call(
        body, grid=(SEQ // tt,), in_specs=_mix_specs(tt, layer) + [ANY],
        out_specs=pl.BlockSpec((tt, D_MODEL), lambda i: (i, 0)),
        out_shape=jax.ShapeDtypeStruct((SEQ, D_MODEL), F32), compiler_params=_cp(), name="mix_fwd",
    )(x, o, cv, z, glog, glog, glog, b_gate, wbt, wbt, wbt, wmix, tie)


def _mix_bwd(dx1, o, cv, z, glog, b_gate, layer, wbt, wmix, tie):
    tt = 256

    def body(dx_ref, o_ref, c_ref, z_ref, g0, g1, g2, b_ref, wa_ref, wc_ref, ws_ref, wm_ref, tie_ref,
             mg_ref, dya_ref, dyc_ref, dys_ref, do_ref, dc_ref, dz_ref, dgl_ref, db_ref):
        @pl.when(pl.program_id(0) == 0)
        def _():
            db_ref[...] = jnp.zeros_like(db_ref)

        ys, gates = _mix_branches(o_ref, c_ref, z_ref, (g0, g1, g2), b_ref, wa_ref, wc_ref, ws_ref)
        mg_ref[...] = (gates[0] * ys[0] + gates[1] * ys[1] + gates[2] * ys[2]).astype(BF16)
        dm = lax.dot_general(dx_ref[...].astype(BF16), wm_ref[...], NT, preferred_element_type=F32)
        for j, (dy_ref, w_ref, d_ref) in enumerate(((dya_ref, wa_ref, do_ref), (dyc_ref, wc_ref, dc_ref),
                                                    (dys_ref, ws_ref, dz_ref))):
            dy = (dm * gates[j]).astype(BF16)
            dy_ref[...] = dy
            d_ref[...] = jnp.dot(dy, w_ref[...], preferred_element_type=F32)
            dgl = dm * ys[j] * gates[j] * (1.0 - gates[j])
            dgl_ref[:, D_MODEL * j:D_MODEL * (j + 1)] = dgl.astype(BF16)
            db_ref[:, D_MODEL * j:D_MODEL * (j + 1)] += jnp.sum(dgl, axis=0, keepdims=True)

    row = lambda w: pl.BlockSpec((tt, w), lambda i: (i, 0))
    sds = jax.ShapeDtypeStruct
    return pl.pallas_call(
        body, grid=(SEQ // tt,), in_specs=_mix_specs(tt, layer) + [ANY],
        out_specs=[row(D_MODEL)] * 4 + [row(WIDTH)] * 3 + [row(GATE_W), _full((1, GATE_W))],
        out_shape=[sds((SEQ, D_MODEL), BF16)] * 4 + [sds((SEQ, WIDTH), F32)] * 3
        + [sds((SEQ, GATE_W), BF16), sds((1, GATE_W), F32)],
        compiler_params=_cp(), name="mix_bwd",
    )(dx1, o, cv, z, glog, glog, glog, b_gate, wbt, wbt, wbt, wmix, tie)


def _ffn_out_fwd(x1, act, wout, tie):
    tt = 512

    def body(x_ref, a_ref, w_ref, tie_ref, o_ref):
        o_ref[...] = x_ref[...] + jnp.dot(a_ref[...], w_ref[...], preferred_element_type=F32)

    row = lambda w: pl.BlockSpec((tt, w), lambda i: (i, 0))
    return pl.pallas_call(
        body, grid=(SEQ // tt,), in_specs=[row(D_MODEL), row(FFN_H), _full((FFN_H, D_MODEL)), ANY],
        out_specs=row(D_MODEL), out_shape=jax.ShapeDtypeStruct((SEQ, D_MODEL), F32),
        compiler_params=_cp(), name="ffn_out_fwd")(x1, act, wout, tie)


def _ffn_out_bwd(dx2, up, silu, dsilu, wout, tie):
    tt = 512

    def body(dx_ref, up_ref, silu_ref, dsilu_ref, w_ref, tie_ref, dgu_ref):
        dact = lax.dot_general(dx_ref[...].astype(BF16), w_ref[...], NT, preferred_element_type=F32).astype(BF16)
        dgu_ref[:, :FFN_H] = dact * up_ref[...] * dsilu_ref[...]
        dgu_ref[:, FFN_H:] = dact * silu_ref[...]

    row = lambda w: pl.BlockSpec((tt, w), lambda i: (i, 0))
    return pl.pallas_call(
        body, grid=(SEQ // tt,),
        in_specs=[row(D_MODEL), row(FFN_H), row(FFN_H), row(FFN_H), _resident((FFN_H, D_MODEL)), ANY],
        out_specs=row(2 * FFN_H), out_shape=jax.ShapeDtypeStruct((SEQ, 2 * FFN_H), BF16),
        compiler_params=_cp(), name="ffn_out_bwd")(dx2, up, silu, dsilu, wout, tie)


def _loss_head(x, g, target):
    tt = 256

    def body(x_ref, g_ref, t_ref, loss_ref, dx_ref, dg_ref):
        @pl.when(pl.program_id(0) == 0)
        def _():
            loss_ref[...] = jnp.zeros_like(loss_ref)
            dg_ref[...] = jnp.zeros_like(dg_ref)

        xv = x_ref[...]
        r = lax.rsqrt(jnp.mean(xv * xv, axis=-1, keepdims=True) + NORM_EPS)
        xh = xv * r
        err = xh * g_ref[...] - t_ref[...]
        loss_ref[...] += 0.5 * jnp.sum(jnp.mean(err * err, axis=-1, keepdims=True))
        dy = err * (1.0 / D_MODEL)
        gy = dy * g_ref[...]
        dx_ref[...] = r * (gy - xh * jnp.mean(gy * xh, axis=-1, keepdims=True))
        dg_ref[...] += jnp.sum(dy * xh, axis=0, keepdims=True)

    row = pl.BlockSpec((tt, D_MODEL), lambda i: (i, 0))
    return pl.pallas_call(
        body, grid=(SEQ // tt,), in_specs=[row, _full((1, D_MODEL)), row],
        out_specs=[_full((1, 128)), row, _full((1, D_MODEL))],
        out_shape=[jax.ShapeDtypeStruct((1, 128), F32), jax.ShapeDtypeStruct((SEQ, D_MODEL), F32),
                   jax.ShapeDtypeStruct((1, D_MODEL), F32)],
        compiler_params=_cp(), name="loss_head")(x, g, target)


def _adam_math(g, w, m, v):
    nm = B1 * m + (1.0 - B1) * g
    nv = B2 * v + (1.0 - B2) * (g * g)
    m_hat = nm / (1.0 - B1 ** STEP)
    v_hat = nv / (1.0 - B2 ** STEP)
    return -LR * (m_hat / (jnp.sqrt(v_hat) + ADAM_EPS) + WD * w), nm, nv


def _adamw_small(parts, w, m, v, name):
    def body(p_ref, w_ref, m_ref, v_ref, g_ref, d_ref, nm_ref, nv_ref):
        g = p_ref[0].astype(F32)
        for k in range(1, N_DEV):
            g = g + p_ref[k].astype(F32)
        g_ref[...] = g
        d_ref[...], nm_ref[...], nv_ref[...] = _adam_math(g, w_ref[...], m_ref[...], v_ref[...])

    out_shape = [jax.ShapeDtypeStruct(w.shape, F32)] * 4
    if w.ndim < 3:
        return pl.pallas_call(body, out_shape=out_shape, name=name)(parts, w, m, v)
    rest = w.shape[1:]
    zeros = (0,) * len(rest)
    blk = pl.BlockSpec((None,) + rest, lambda l: (l,) + zeros)
    return pl.pallas_call(
        body, grid=(w.shape[0],),
        in_specs=[pl.BlockSpec((N_DEV, None) + rest, lambda l: (0, l) + zeros), blk, blk, blk],
        out_specs=[blk] * 4, out_shape=out_shape, name=name)(parts, w, m, v)


def _adamw(parts, w, m, v, tr, name, groups=None, fill=None, tie=None):
    n_groups, rows, cols = w.shape
    n_parts = parts.shape[1]
    lo, hi = groups if groups is not None else (0, n_groups)

    def body(p_ref, w_ref, m_ref, v_ref, *rest):
        g_ref, d_ref, nm_ref, nv_ref = rest[-4:]
        g = p_ref[0].astype(F32)
        for k in range(1, n_parts):
            g = g + p_ref[k].astype(F32)
        nm = B1 * m_ref[...] + (1.0 - B1) * g
        nv = B2 * v_ref[...] + (1.0 - B2) * (g * g)
        m_hat = nm / (1.0 - B1 ** STEP)
        v_hat = nv / (1.0 - B2 ** STEP)
        g_ref[...] = g
        d_ref[...] = -LR * (m_hat / (jnp.sqrt(v_hat) + ADAM_EPS) + WD * w_ref[...])
        nm_ref[...] = nm
        nv_ref[...] = nv

    blk = pl.BlockSpec((None, tr, cols), lambda l, i: (l + lo, i, 0))
    p_lo = lo if parts.shape[0] == n_groups else 0
    extra = ([] if fill is None else list(fill)) + ([] if tie is None else [tie])
    return pl.pallas_call(
        body, grid=(hi - lo, rows // tr),
        in_specs=[pl.BlockSpec((None, n_parts, tr, cols), lambda l, i: (l + p_lo, 0, i, 0)), blk, blk, blk]
        + [ANY] * len(extra),
        out_specs=[blk] * 4, out_shape=[jax.ShapeDtypeStruct((n_groups, rows, cols), F32)] * 4,
        input_output_aliases={} if fill is None else {4 + j: j for j in range(4)},
        compiler_params=_cp(), name=name)(parts, w, m, v, *extra)


BRANCHES = ("w_attn_o", "w_conv_o", "w_ssm_o")


def _adamw_branches(parts, wmv, name, groups, fill=None, tie=None):
    n_parts = parts.shape[1]
    lo, hi = groups

    def body(p_ref, *refs):
        ins, outs = refs[:9], refs[-12:]
        g = p_ref[0].astype(F32)
        for k in range(1, n_parts):
            g = g + p_ref[k].astype(F32)
        for j in range(3):
            gj = g[:, j * WIDTH:(j + 1) * WIDTH].T
            w_ref, m_ref, v_ref = ins[3 * j:3 * j + 3]
            d, nm, nv = _adam_math(gj, w_ref[...], m_ref[...], v_ref[...])
            for o, val in zip(outs[4 * j:4 * j + 4], (gj, d, nm, nv)):
                o[...] = val

    shard = wmv[0].shape[1:]
    blk = pl.BlockSpec((None,) + shard, lambda l: (l + lo, 0, 0))
    p_lo = lo if parts.shape[0] == DEPTH else 0
    extra = ([] if fill is None else list(fill)) + ([] if tie is None else [tie])
    return pl.pallas_call(
        body, grid=(hi - lo,),
        in_specs=[pl.BlockSpec((None,) + parts.shape[1:], lambda l: (l + p_lo, 0, 0, 0))] + [blk] * 9
        + [ANY] * len(extra),
        out_specs=[blk] * 12, out_shape=[jax.ShapeDtypeStruct((DEPTH,) + shard, F32)] * 12,
        input_output_aliases={} if fill is None else {10 + j: j for j in range(12)},
        compiler_params=_cp(), name=name)(parts, *wmv, *extra)


def _split_start(name, arrays, n_sems, plan, after=None):
    n = len(arrays)
    order = [] if after is None else [after]
    n_in = n + len(order)

    def body(*refs):
        ins, send_sems, recv_sems, token = refs[:n], refs[n_in], refs[n_in + 1], refs[-1]
        for src, dst, k, to in plan(ins)[0]:
            pltpu.make_async_remote_copy(src_ref=src, dst_ref=dst, send_sem=send_sems.at[k], recv_sem=recv_sems.at[k],
                                         device_id=to, device_id_type=MESH_ID).start()
        token[...] = jnp.zeros_like(token)

    outs = pl.pallas_call(
        body, name=name,
        out_shape=(pltpu.SemaphoreType.DMA((n_sems,)), pltpu.SemaphoreType.DMA((n_sems,)),
                   *[pltpu.HBM(a.shape, a.dtype) for a in arrays], jax.ShapeDtypeStruct((8, 128), F32)),
        in_specs=[HBM] * n + [ANY] * len(order),
        out_specs=(SEM, SEM, *[HBM] * n, pl.BlockSpec(memory_space=pltpu.VMEM)),
        input_output_aliases={i: 2 + i for i in range(n)},
        compiler_params=pltpu.CompilerParams(has_side_effects=EFFECT),
    )(*[pltpu.with_memory_space_constraint(a, pltpu.HBM) for a in arrays], *order)
    return outs[0], outs[1], list(outs[2:2 + n]), outs[-1]


def _split_wait(name, arrays, send_sems, recv_sems, after, plan):
    n = len(arrays)
    order = list(after) if isinstance(after, (list, tuple)) else [after]

    def body(*refs):
        ins, s_sems, r_sems = refs[:n], refs[n], refs[n + 1]
        sends, arrivals = plan(ins)
        x, y, c = lax.axis_index("x"), lax.axis_index("y"), lax.axis_index("c")
        for src, dst, k, to in sends:
            pltpu.make_async_remote_copy(src_ref=src, dst_ref=dst, send_sem=s_sems.at[k], recv_sem=r_sems.at[k],
                                         device_id=to, device_id_type=MESH_ID).wait_send()
        for dst, k in arrivals:
            pltpu.make_async_remote_copy(src_ref=dst, dst_ref=dst, send_sem=s_sems.at[k], recv_sem=r_sems.at[k],
                                         device_id=(x, y, c), device_id_type=MESH_ID).wait_recv()

    return pl.pallas_call(
        body, name=name, out_shape=[pltpu.HBM(a.shape, a.dtype) for a in arrays],
        in_specs=[HBM] * n + [SEM, SEM] + [ANY] * len(order), out_specs=[HBM] * n,
        input_output_aliases={i: i for i in range(n)},
        compiler_params=pltpu.CompilerParams(has_side_effects=EFFECT),
    )(*arrays, send_sems, recv_sems, *order)


def _chips():
    x, y, c = lax.axis_index("x"), lax.axis_index("y"), lax.axis_index("c")
    return x, y, c, [(1 - x, y), (x, 1 - y), (1 - x, 1 - y)]


def _plan_gather_chips(refs):
    x, y, c, chips = _chips()
    me = 4 * x + 2 * y + c
    n = len(refs) // 2
    sends, arrivals = [], []
    for i in range(n):
        src, land = refs[i], refs[n + i]
        sends.append((src, land.at[me], 4 * i, (x, y, 1 - c)))
        arrivals.append((land.at[4 * x + 2 * y + 1 - c], 4 * i))
        for j, (px, py) in enumerate(chips):
            sends.append((src, land.at[me], 4 * i + 1 + j, (px, py, c)))
            arrivals.append((land.at[4 * px + 2 * py + c], 4 * i + 1 + j))
    return sends, arrivals


def _plan_gather_pass(refs):
    x, y, c, chips = _chips()
    sends, arrivals = [], []
    for i in range(len(refs)):
        for j, (px, py) in enumerate(chips):
            slot = refs[i].at[4 * px + 2 * py + c]
            sends.append((slot, slot, 4 * i + j, (x, y, 1 - c)))
            arrivals.append((refs[i].at[4 * px + 2 * py + 1 - c], 4 * i + j))
        back = refs[i].at[4 * x + 2 * y + 1 - c]
        sends.append((back, back, 4 * i + 3, (x, y, 1 - c)))
        arrivals.append((refs[i].at[4 * x + 2 * y + c], 4 * i + 3))
    return sends, arrivals


def _plan_scatter_pair(refs):
    x, y, c = lax.axis_index("x"), lax.axis_index("y"), lax.axis_index("c")
    n = len(refs) // 2
    sends, arrivals = [], []
    for i in range(n):
        for q in range(4):
            sends.append((refs[i].at[q, 1 - c], refs[n + i].at[q], 4 * i + q, (x, y, 1 - c)))
            arrivals.append((refs[n + i].at[q], 4 * i + q))
    return sends, arrivals


def _plan_scatter_chips(layer):
    def plan(refs):
        x, y, c, chips = _chips()
        n = len(refs) // 2
        sends, arrivals = [], []
        for i in range(n):
            for j, (px, py) in enumerate(chips):
                sends.append((refs[i].at[2 * px + py], refs[n + i].at[layer, 2 * x + y], 3 * i + j, (px, py, c)))
                arrivals.append((refs[n + i].at[layer, 2 * px + py], 3 * i + j))
        return sends, arrivals

    return plan


def _pair_sum(parts4, from_pair, landing, layer, core, tr, name):
    _, _, rows, cols = parts4.shape

    def body(c_ref, p_ref, s_ref, l_ref, sum_ref, land_ref):
        v = (p_ref[...].astype(F32) + s_ref[...].astype(F32)).astype(BF16)
        sum_ref[...] = v
        land_ref[...] = v

    blk = pl.BlockSpec((None, tr, cols), lambda q, i, c_ref: (q, i, 0))
    return pl.pallas_call(
        body,
        grid_spec=pltpu.PrefetchScalarGridSpec(
            num_scalar_prefetch=1, grid=(4, rows // tr),
            in_specs=[pl.BlockSpec((None, None, tr, cols), lambda q, i, c_ref: (q, c_ref[0], i, 0)), blk, ANY],
            out_specs=[blk, pl.BlockSpec((None, None, tr, cols), lambda q, i, c_ref: (layer, q, i, 0))]),
        out_shape=[jax.ShapeDtypeStruct((4, rows, cols), BF16), jax.ShapeDtypeStruct(landing.shape, BF16)],
        input_output_aliases={3: 1}, compiler_params=_cp(), name=name,
    )(core, parts4, from_pair, landing)


def _travel_layout(t):
    tr = lambda a: jnp.swapaxes(a, 1, 2)
    branch = jnp.concatenate([tr(t["w_attn_o"]), tr(t["w_conv_o"]), tr(t["w_ssm_o"])], axis=2)
    return [tr(t["w_in"]), tr(t["w_ffn_in"]), t["w_ffn_out"], t["w_mix_o"], branch, t["w_ssm_glu"]]


def _native_layout(a):
    tr = lambda x: jnp.swapaxes(x, 1, 2)
    return {"w_in": tr(a[0]), "w_ffn_in": tr(a[1]), "w_ffn_out": a[2], "w_mix_o": a[3], "w_ssm_glu": a[5]}


def _rope_tabs():
    pos = jnp.arange(SEQ, dtype=F32)
    inv_freq = ROPE_THETA ** (-jnp.arange(0, ROT_DIM, 2, dtype=F32) / ROT_DIM)
    ang = pos[:, None] * inv_freq[None, :]
    cos, sin = jnp.cos(ang), jnp.sin(ang)
    one, zero = jnp.ones((SEQ, HEAD_DIM - ROT_DIM), F32), jnp.zeros((SEQ, HEAD_DIM - ROT_DIM), F32)
    z8 = jnp.zeros((SEQ, 8), F32)
    head = lambda *p: jnp.tile(jnp.concatenate(p, axis=1), (1, 2))
    return head(cos, cos, one), head(-sin, z8, zero), head(z8, sin, zero)


def _ssm_mats(sp):
    lr, li, bbr, bbi = _ssm_prep(sp["a_re"], sp["a_im"], sp["log_dt"], sp["bt_re"], sp["bt_im"])
    lanes = SSM_GROUPS * SSM_STATE
    b_re, b_im, c_re, c_im_neg = _ssm_embed(bbr, bbi, sp["c_re"], sp["c_im"])
    return {
        "a_re": lr.reshape(DEPTH, 1, lanes), "a_im": li.reshape(DEPTH, 1, lanes),
        "b_re": b_re, "b_im": b_im, "c_re": c_re, "c_im_neg": c_im_neg,
    }


def _layer_fwd(x, i, w, rp, mats, tabs, tie, hooks):
    q, kv, cbx, u, glog, cv, h = _rms_mm_in(x, rp["norm_mix"][i], w["win_t"], tabs, rp["conv_w"], i, tie)
    o = _attn_fwd(q, kv, tabs, rp["attn_sinks"][i])
    x_re, x_im, y = _ssm_fwd(u, mats, i, rp["ssm_d"])
    z = _glu_fwd(y, w["wglu"])
    x1 = _mix_fwd(x, o, cv, z, glog, rp["b_gate"], i, w["branch_t"], w["wmix"], hooks["early"](z))
    hooks["pre_ffn"](x1)
    act, up, silu, dsilu, h2 = _rms_mm_ffn(x1, rp["norm_ffn"][i], w["wffn_t"])
    x2 = _ffn_out_fwd(x1, act, w["wout"], hooks["mid"](h2))
    kept = dict(x=x, q=q, kv=kv, cbx=cbx, u=u, glog=glog, h=h, o=o, cv=cv, z=z, y=y,
                x_re=x_re, x_im=x_im, x1=x1, act=act, up=up, silu=silu, dsilu=dsilu, h2=h2)
    return x2, kept


def _layer_bwd(dx2, k, i, w, rp, mats, tabs, tie, hooks):
    dgu = _ffn_out_bwd(dx2, k["up"], k["silu"], k["dsilu"], w["wout"], tie)
    g_wout = _mm_tn(k["act"], dx2, tm=FFN_H // 2, tn=1024, name="mm_tn_ffn_out")
    g_wffn_t = _mm_tn(dgu, k["h2"], tm=FFN_H // 2, tn=1024, name="mm_tn_ffn_in")
    dx1, d_norm_ffn = _mm_rmsbwd([dgu], w["wffn_t"], k["x1"], rp["norm_ffn"][i], dx2, "mm_rmsbwd_ffn")

    mg, dya, dyc, dys, do, dcv, dz, dgl, db_gate = _mix_bwd(
        dx1, k["o"], k["cv"], k["z"], k["glog"], rp["b_gate"], i, w["branch_t"], w["wmix"],
        hooks["mid"]((g_wffn_t, g_wout, d_norm_ffn)))
    g_wmix = _mm_tn(mg, dx1, tm=1024, tn=512, name="mm_tn_mix")
    g_branch_t = _tn_branches((dya, dyc, dys), (k["o"], k["cv"], k["z"]))

    dy, ys16, da16, dd = _glu_bwd(k["y"], w["wglu"], dz, k["u"])
    g_wglu = _mm_tn(ys16, da16, tm=256, tn=512, name="mm_tn_glu")
    du, da_re, da_im, db_re, db_im, dc_re, dc_im = _ssm_bwd(dy, k["x_re"], k["x_im"], k["u"], mats, i, rp["ssm_d"])

    dcb, dcc, dcx, d_conv_w = _conv_bwd(k["cbx"], rp["conv_w"], i, dcv, hooks["late"](du))
    dq, dkv, d_sinks = _attn_bwd(k["q"], k["kv"], tabs, rp["attn_sinks"][i], do)

    pieces = [dq, dkv, dcb, dcc, dcx, du, dgl]
    g_win_t = _tn_pieces(pieces, k["h"])
    dx, d_norm_mix = _mm_rmsbwd(pieces, w["win_t"], k["x"], rp["norm_mix"][i], dx1, "mm_rmsbwd_in")

    grads = [g_win_t, g_wffn_t, g_wout, g_wmix, g_branch_t, g_wglu]
    small = dict(norm_mix=d_norm_mix, b_gate=db_gate, attn_sinks=d_sinks, ssm_d=dd, norm_ffn=d_norm_ffn,
                 conv_w=d_conv_w, da_re=da_re, da_im=da_im, db_re=db_re, db_im=db_im, dc_re=dc_re, dc_im=dc_im)
    return dx, grads, small


def _ssm_diag(layers, signs):
    n_in = len(layers) * DEPTH

    def body(*refs):
        for k, out in enumerate(refs[n_in:]):
            for l in range(DEPTH):
                src = refs[k * DEPTH + l]
                for group in range(SSM_GROUPS):
                    g, a = divmod(group, LANES_G // SSM_STATE)
                    blk = src[g, pl.ds(a * SSM_GROUP, SSM_GROUP), pl.ds(a * SSM_STATE, SSM_STATE)]
                    out[l, group] = blk if signs[k] > 0 else -blk

    return pl.pallas_call(
        body, out_shape=[jax.ShapeDtypeStruct((DEPTH,) + _GHS, F32)] * len(layers),
        compiler_params=_cp(), name="ssm_diag")(*[x for kind in layers for x in kind])


def _replicated_grads(sg, sp):
    stack = lambda name: jnp.stack([sg[i][name] for i in range(DEPTH)])
    per_layer = lambda name: [sg[i][name] for i in range(DEPTH)]
    db_re, db_im, dc_re, dc_im = _ssm_diag([per_layer(n) for n in ("db_re", "db_im", "dc_re", "dc_im")],
                                           (1, 1, 1, -1))
    cots = (stack("da_re").reshape(DEPTH, *_GS), stack("da_im").reshape(DEPTH, *_GS), db_re, db_im)
    d_a_re, d_a_im, d_log_dt, d_bt_re, d_bt_im = _ssm_prep_bwd(
        sp["a_re"], sp["a_im"], sp["log_dt"], sp["bt_re"], sp["bt_im"], cots)
    sgrads = {"norm_mix": stack("norm_mix"), "b_gate": stack("b_gate"),
              "attn_sinks": stack("attn_sinks")[:, :, :N_Q_HEADS], "ssm_a_re": d_a_re, "ssm_a_im": d_a_im,
              "ssm_b_re": jnp.swapaxes(d_bt_re, 2, 3), "ssm_b_im": jnp.swapaxes(d_bt_im, 2, 3),
              "ssm_c_re": dc_re, "ssm_c_im": dc_im,
              "ssm_d": stack("ssm_d"), "ssm_log_dt": d_log_dt, "norm_ffn": stack("norm_ffn")}
    return sgrads, stack("conv_w")[:, :3]


def kernel(x, norm_mix, w_in, b_gate, attn_sinks, w_attn_o, conv_w, w_conv_o, ssm_a_re, ssm_a_im, ssm_b_re, ssm_b_im, ssm_c_re, ssm_c_im, ssm_d, ssm_log_dt, w_ssm_glu, w_ssm_o, w_mix_o, norm_ffn, w_ffn_in, w_ffn_out, norm_final, loss_target, m_norm_mix, m_w_in, m_b_gate, m_attn_sinks, m_w_attn_o, m_conv_w, m_w_conv_o, m_ssm_a_re, m_ssm_a_im, m_ssm_b_re, m_ssm_b_im, m_ssm_c_re, m_ssm_c_im, m_ssm_d, m_ssm_log_dt, m_w_ssm_glu, m_w_ssm_o, m_w_mix_o, m_norm_ffn, m_w_ffn_in, m_w_ffn_out, m_norm_final, v_norm_mix, v_w_in, v_b_gate, v_attn_sinks, v_w_attn_o, v_conv_w, v_w_conv_o, v_ssm_a_re, v_ssm_a_im, v_ssm_b_re, v_ssm_b_im, v_ssm_c_re, v_ssm_c_im, v_ssm_d, v_ssm_log_dt, v_w_ssm_glu, v_w_ssm_o, v_w_mix_o, v_norm_ffn, v_w_ffn_in, v_w_ffn_out, v_norm_final):
    big = {"w": dict(w_in=w_in, w_attn_o=w_attn_o, w_conv_o=w_conv_o, w_ssm_glu=w_ssm_glu, w_ssm_o=w_ssm_o,
                     w_mix_o=w_mix_o, w_ffn_in=w_ffn_in, w_ffn_out=w_ffn_out),
           "m": dict(w_in=m_w_in, w_attn_o=m_w_attn_o, w_conv_o=m_w_conv_o, w_ssm_glu=m_w_ssm_glu,
                     w_ssm_o=m_w_ssm_o, w_mix_o=m_w_mix_o, w_ffn_in=m_w_ffn_in, w_ffn_out=m_w_ffn_out),
           "v": dict(w_in=v_w_in, w_attn_o=v_w_attn_o, w_conv_o=v_w_conv_o, w_ssm_glu=v_w_ssm_glu,
                     w_ssm_o=v_w_ssm_o, w_mix_o=v_w_mix_o, w_ffn_in=v_w_ffn_in, w_ffn_out=v_w_ffn_out)}
    small = {"w": dict(norm_mix=norm_mix, b_gate=b_gate, attn_sinks=attn_sinks, ssm_a_re=ssm_a_re,
                       ssm_a_im=ssm_a_im, ssm_b_re=ssm_b_re, ssm_b_im=ssm_b_im, ssm_c_re=ssm_c_re,
                       ssm_c_im=ssm_c_im, ssm_d=ssm_d, ssm_log_dt=ssm_log_dt, norm_ffn=norm_ffn),
             "m": dict(norm_mix=m_norm_mix, b_gate=m_b_gate, attn_sinks=m_attn_sinks, ssm_a_re=m_ssm_a_re,
                       ssm_a_im=m_ssm_a_im, ssm_b_re=m_ssm_b_re, ssm_b_im=m_ssm_b_im, ssm_c_re=m_ssm_c_re,
                       ssm_c_im=m_ssm_c_im, ssm_d=m_ssm_d, ssm_log_dt=m_ssm_log_dt, norm_ffn=m_norm_ffn),
             "v": dict(norm_mix=v_norm_mix, b_gate=v_b_gate, attn_sinks=v_attn_sinks, ssm_a_re=v_ssm_a_re,
                       ssm_a_im=v_ssm_a_im, ssm_b_re=v_ssm_b_re, ssm_b_im=v_ssm_b_im, ssm_c_re=v_ssm_c_re,
                       ssm_c_im=v_ssm_c_im, ssm_d=v_ssm_d, ssm_log_dt=v_ssm_log_dt, norm_ffn=v_norm_ffn)}
    finals = {"w": norm_final, "m": m_norm_final, "v": v_norm_final}
    convs = {"w": conv_w, "m": m_conv_w, "v": v_conv_w}
    small_out_shapes = {name: a.shape for name, a in small["w"].items()}
    small_out_shapes.update(norm_final=(D_MODEL,), conv_w=(DEPTH, 3, 64))
    small_shapes = dict(small_out_shapes, norm_final=(1, D_MODEL), conv_w=(DEPTH, 3, WIDTH))
    dense = ("ssm_b_re", "ssm_b_im", "ssm_c_re", "ssm_c_im")
    for name in dense:
        small_shapes[name] = (DEPTH, SSM_GROUPS, SSM_GROUP * SSM_STATE)
    small_wmv = {name: [(convs[s] if name == "conv_w" else finals[s] if name == "norm_final" else small[s][name])
                        .reshape((DEPTH, 3, 64) if name == "conv_w" else small_shapes[name]) for s in "wmv"]
                 for name in small_shapes}
    mine = 4 * lax.axis_index("x") + 2 * lax.axis_index("y") + lax.axis_index("c")

    travel = {s: _travel_layout(big[s]) for s in "wmv"}
    stacked16 = list(zip(*[[a[0] for a in _travel_layout({n: w[i:i + 1].astype(BF16) for n, w in big["w"].items()})]
                           for i in range(DEPTH)]))
    rp = {"norm_mix": norm_mix[:, None], "norm_ffn": norm_ffn[:, None], "attn_sinks": attn_sinks[:, None],
          "b_gate": b_gate[:, None], "ssm_d": ssm_d[:, None]}
    sp = {"a_re": ssm_a_re, "a_im": ssm_a_im, "log_dt": ssm_log_dt[:, :, None],
          "bt_re": jnp.swapaxes(ssm_b_re, 2, 3), "bt_im": jnp.swapaxes(ssm_b_im, 2, 3),
          "c_re": ssm_c_re, "c_im": ssm_c_im}
    rows_tile = {"win_t": 184, "wffn_t": 176, "wout": 176, "wmix": 128, "branch_t": 128, "wglu": 64}
    core = lax.axis_index("c").astype(jnp.int32).reshape(1)
    no_tie = jnp.zeros((8, 128), F32)

    def landing_zones(srcs):
        return [lax.empty((N_DEV,) + s.shape, s.dtype) for s in srcs]

    def gather_chips(tag, i, kinds, after, extra=()):
        srcs = [stacked16[j][i] for j in kinds] + list(extra)
        s_sems, r_sems, arrays, token = _split_start(
            f"gather_chips_start_{tag}", srcs + landing_zones(srcs), 4 * len(srcs), _plan_gather_chips, after)
        return (tag, s_sems, r_sems, arrays), token

    def gather_pass(state, after):
        tag, s_sems, r_sems, arrays = state
        arrays = _split_wait(f"gather_chips_wait_{tag}", arrays, s_sems, r_sems, after, _plan_gather_chips)
        n = len(arrays) // 2
        s_sems, r_sems, lands, token = _split_start(
            f"gather_pass_start_{tag}", list(arrays[n:]), 4 * n, _plan_gather_pass)
        return (tag, s_sems, r_sems, lands), token

    def gather_done(state, after, kinds):
        tag, s_sems, r_sems, lands = state
        lands = _split_wait(f"gather_pass_wait_{tag}", lands, s_sems, r_sems, after, _plan_gather_pass)
        named = {KINDS[j][0]: a.reshape(N_DEV * KINDS[j][1], KINDS[j][2]) for a, j in zip(lands, kinds)}
        return named, list(lands[len(kinds):])

    all_kinds, mixer_kinds, ffn_kinds = tuple(range(len(KINDS))), (0, 3, 4, 5), (1, 2)
    no_hooks = {name: (lambda value: no_tie) for name in ("early", "pre_ffn", "mid", "late")}
    state, token = gather_chips("0m", 0, mixer_kinds, None, extra=[jnp.pad(conv_w.reshape(6, 128), ((0, 2), (0, 0)))])
    mats = _ssm_mats(dict(sp, log_dt=sp["log_dt"] + token[0, 0]))
    tabs = _rope_tabs()
    early_work = list(mats.values()) + list(tabs) + [a for name in dense for a in small_wmv[name]]
    early_work += [stacked16[j][0] for j in ffn_kinds] + [stacked16[j][1] for j in mixer_kinds]
    state, _ = gather_pass(state, early_work)
    ffn_state, tie = gather_chips("0f", 0, ffn_kinds, state[3][0])
    w_next, (conv_all,) = gather_done(state, tabs[2], mixer_kinds)
    conv_full = conv_all[:, :6].reshape(N_DEV, DEPTH, 3, 64).transpose(1, 2, 0, 3).reshape(DEPTH, 3, WIDTH)
    rp["conv_w"] = jnp.pad(conv_full, ((0, 0), (0, 5), (0, 0)))

    act = x[0]
    weights, kept = [], []
    for i in range(DEPTH):
        w_i, hooks, held = w_next, dict(no_hooks), {}

        def early(value, ffn_state=ffn_state, held=held):
            held["ffn"], token = gather_pass(ffn_state, value)
            return token

        def pre_ffn(value, w_i=w_i, held=held):
            w_i.update(gather_done(held["ffn"], value, ffn_kinds)[0])

        hooks.update(early=early, pre_ffn=pre_ffn)
        if i + 1 < DEPTH:
            state, tie = gather_chips(f"{i + 1}m", i + 1, mixer_kinds, tie if i == 0 else w_i["win_t"])

            def mid(value, i=i, state=state, held=held):
                held["next"], token = gather_pass(state, value)
                held["next_ffn"], token = gather_chips(f"{i + 1}f", i + 1, ffn_kinds, token)
                return token

            hooks.update(mid=mid)
        act, k = _layer_fwd(act, i, w_i, rp, mats, tabs, tie, hooks)
        if i + 1 < DEPTH:
            w_next, _ = gather_done(held["next"], act, mixer_kinds)
            ffn_state, tie = held["next_ffn"], no_tie
        weights.append(w_i)
        kept.append(k)
    loss_row, dx, d_norm_final = _loss_head(act, norm_final[None], loss_target[0])

    landings = [lax.empty((DEPTH, 4, r, c), BF16) for _, r, c in KINDS]
    landings0 = [lax.empty((1, 4, r, c), BF16) for _, r, c in KINDS]

    def scatter_pair(tag, kinds, grads, after):
        parts4 = [g.reshape(4, 2, KINDS[j][1], KINDS[j][2]) for g, j in zip(grads, kinds)]
        zones = [lax.empty((4, KINDS[j][1], KINDS[j][2]), BF16) for j in kinds]
        s_sems, r_sems, arrays, token = _split_start(
            f"scatter_pair_start_{tag}", parts4 + zones, 4 * len(kinds), _plan_scatter_pair, after)
        return (tag, kinds, s_sems, r_sems, arrays), token

    def scatter_chips(state, lands, slot, after):
        tag, kinds, s_sems, r_sems, arrays = state
        arrays = _split_wait(f"scatter_pair_wait_{tag}", arrays, s_sems, r_sems, after, _plan_scatter_pair)
        n = len(kinds)
        sums, mine_lands = [], []
        for k, j in enumerate(kinds):
            name = KINDS[j][0]
            chip_sum, land = _pair_sum(arrays[k], arrays[n + k], lands[j], slot, core, KINDS[j][1],
                                       f"pair_sum_{name}")
            sums.append(chip_sum)
            mine_lands.append(land)
        s_sems, r_sems, arrays, token = _split_start(
            f"scatter_chips_start_{tag}", sums + mine_lands, 3 * n, _plan_scatter_chips(slot))
        return (tag, kinds, slot, s_sems, r_sems, arrays), token

    def scatter_done(state, lands, after):
        tag, kinds, slot, s_sems, r_sems, arrays = state
        arrays = _split_wait(f"scatter_chips_wait_{tag}", arrays, s_sems, r_sems, after, _plan_scatter_chips(slot))
        lands = list(lands)
        for k, j in enumerate(kinds):
            lands[j] = arrays[len(kinds) + k]
        return lands

    sg = [None] * DEPTH
    pending, tie = None, no_tie
    for i in reversed(range(DEPTH)):
        hooks, held = dict(no_hooks), {}
        if pending is not None:
            def mid(value, i=i, pending=pending, held=held):
                held["chips"], token = scatter_chips(pending, landings, i + 1, value[2])
                if i == 0:
                    held["ffn_pair"], token = scatter_pair("0f", ffn_kinds, value[:2], token)
                return token

            hooks.update(mid=mid)
        if i == 0:
            def late(value, held=held):
                held["ffn_chips"], token = scatter_chips(held["ffn_pair"], landings0, 0, value)
                return token

            hooks.update(late=late)
        dx, grads, sg[i] = _layer_bwd(dx, kept[i], i, weights[i], rp, mats, tabs, tie, hooks)
        if pending is not None:
            landings = scatter_done(held["chips"], landings, dx)
        if i > 0:
            pending, tie = scatter_pair(str(i), all_kinds, grads, dx)
        else:
            pending, _ = scatter_pair("0m", mixer_kinds, [grads[j] for j in mixer_kinds], dx)

    sgrads, conv_grad = _replicated_grads(sg, sp)

    small_names = list(REPLICATED) + ["norm_final", "conv_w"]
    sgrads.update(norm_final=d_norm_final, conv_w=conv_grad)
    small_src = [sgrads[name].reshape(small_shapes[name]).astype(BF16) for name in small_names]
    small_src.append(jnp.broadcast_to(loss_row[:, :1], (8, 128)))
    last, tie = scatter_chips(pending, landings0, 0, small_src[0])
    s_sems, r_sems, arrays, tie = _split_start(
        "gather_small_chips_start", small_src + landing_zones(small_src), 4 * len(small_src), _plan_gather_chips, tie)
    small_state = ("small", s_sems, r_sems, arrays)

    branch_wmv = [big[s][n] for n in BRANCHES for s in "wmv"]
    jb = [name for name, _, _ in KINDS].index("branch_t")

    def adamw(j, name, parts, label, groups, **kw):
        if j == jb:
            return _adamw_branches(parts, branch_wmv, label + name, groups, **kw)
        return _adamw(parts, travel["w"][j], travel["m"][j], travel["v"][j], rows_tile[name], label + name,
                      groups=groups, **kw)

    big_out = []
    for j, (name, _, _) in enumerate(KINDS):
        big_out.append(adamw(j, name, landings[j], "adamw_late_", (1, DEPTH), tie=tie))
        tie = big_out[-1][-1]
    landings0 = scatter_done(held["ffn_chips"], landings0, tie)
    landings0 = scatter_done(last, landings0, tie)
    small_state, _ = gather_pass(small_state, landings0[0])
    big_out = [adamw(j, name, landings0[j], "adamw_first_", (0, 1), fill=big_out[j])
               for j, (name, _, _) in enumerate(KINDS)]
    big_res = []
    for kind in range(4):
        res = _native_layout([None if j == jb else big_out[j][kind] for j in range(len(KINDS))])
        res.update({n: big_out[jb][4 * b + kind] for b, n in enumerate(BRANCHES)})
        big_res.append(res)

    _, sparts = gather_done(small_state, big_out[-1][0], ())
    loss = jnp.sum(sparts[-1][:, 0, 0])
    sparts = dict(zip(small_names, sparts))
    sparts["conv_w"] = lax.dynamic_slice_in_dim(sparts["conv_w"], mine * 64, 64, axis=3)
    small_res = {}
    for name in small_names:
        res = _adamw_small(sparts[name], *small_wmv[name], "adamw_" + name)
        small_res[name] = [r.reshape(small_out_shapes[name]) for r in res]

    order = ["norm_mix", "w_in", "b_gate", "attn_sinks", "w_attn_o", "conv_w", "w_conv_o", "ssm_a_re", "ssm_a_im",
             "ssm_b_re", "ssm_b_im", "ssm_c_re", "ssm_c_im", "ssm_d", "ssm_log_dt", "w_ssm_glu", "w_ssm_o",
             "w_mix_o", "norm_ffn", "w_ffn_in", "w_ffn_out", "norm_final"]
    outs = [loss, dx[None]]
    for kind in range(4):
        for name in order:
            outs.append(big_res[kind][name] if name in big_res[kind] else small_res[name][kind])
    return tuple(outs)
```
